```python
import math
import jax, jax.numpy as jnp
from jax import lax
import numpy as np

D_MODEL = 1024
BATCH = 8
SEQ = 2048
DEPTH = 1

D_CONV = D_MODEL
CONV_K = 3
HEAD_DIM = 64
N_HEADS = 16
N_KV_HEADS = 4
GROUP = N_HEADS // N_KV_HEADS
D_ATTN = N_HEADS * HEAD_DIM
D_KV = N_KV_HEADS * HEAD_DIM
WINDOW = 128
BLOCK = 128
ROT_DIM = HEAD_DIM // 4
ROPE_THETA = 500000.0
ATTN_SCALE = 1.0 / math.sqrt(HEAD_DIM)
NEG_INF = -1e30
D_FF = ((8 * D_MODEL // 3 + 255) // 256) * 256
EPS = 1e-5

IN_WIDTHS = (D_CONV, D_CONV, D_CONV, D_ATTN, D_KV, D_KV, D_MODEL, D_MODEL)
N_IN = sum(IN_WIDTHS)
SPLIT_POINTS = tuple(int(v) for v in np.cumsum(IN_WIDTHS)[:-1])

kernel_name = "hybrid_gated_conv_swa_sink_block"


def rms_norm(x, g):
    xf = x.astype(jnp.float32)
    y = xf * lax.rsqrt(jnp.mean(xf * xf, axis=-1, keepdims=True) + EPS)
    return (y * g.astype(jnp.float32)).astype(x.dtype)


def rotary_tables(seq, dtype):
    inv_freq = ROPE_THETA ** (-jnp.arange(0, ROT_DIM, 2, dtype=jnp.float32) / ROT_DIM)
    ang = jnp.arange(seq, dtype=jnp.float32)[:, None] * inv_freq[None, :]
    return jnp.cos(ang).astype(dtype), jnp.sin(ang).astype(dtype)


def partial_rotary(t, cos, sin):
    rot, rest = t[..., :ROT_DIM], t[..., ROT_DIM:]
    r1, r2 = rot[..., :ROT_DIM // 2], rot[..., ROT_DIM // 2:]
    c = cos[None, :, None, :]
    s = sin[None, :, None, :]
    rot = jnp.concatenate([r1 * c - r2 * s, r2 * c + r1 * s], axis=-1)
    return jnp.concatenate([rot, rest], axis=-1)


def causal_depthwise_conv(u, w):
    rhs = w[:, None, :].astype(u.dtype)
    return lax.conv_general_dilated(
        u, rhs, window_strides=(1,), padding=[(CONV_K - 1, 0)],
        dimension_numbers=('NWC', 'WIO', 'NWC'), feature_group_count=u.shape[-1])


def sliding_window_sink_attention(q, k, v, sinks):
    b, s = q.shape[0], q.shape[1]
    nb = s // BLOCK
    qb = q.reshape(b, nb, BLOCK, N_KV_HEADS, GROUP, HEAD_DIM)

    def band(t):
        tp = jnp.pad(t, ((0, 0), (BLOCK, 0), (0, 0), (0, 0)))
        tp = tp.reshape(b, nb + 1, BLOCK, N_KV_HEADS, HEAD_DIM)
        return jnp.concatenate([tp[:, :-1], tp[:, 1:]], axis=2)

    kb, vb = band(k), band(v)
    scores = jnp.einsum('bnqhgd,bnkhd->bnhgqk', qb, kb,
                        preferred_element_type=jnp.float32) * ATTN_SCALE
    qi = jnp.arange(BLOCK)[:, None]
    kj = jnp.arange(2 * BLOCK)[None, :]
    rel = qi + BLOCK - kj
    in_window = (rel >= 0) & (rel < WINDOW)
    key_pos = jnp.arange(nb)[:, None] * BLOCK - BLOCK + jnp.arange(2 * BLOCK)[None, :]
    mask = in_window[None] & (key_pos >= 0)[:, None, :]
    scores = jnp.where(mask[None, :, None, None], scores, NEG_INF)
    sink = jnp.broadcast_to(sinks.astype(jnp.float32).reshape(1, 1, N_KV_HEADS, GROUP, 1, 1),
                            scores.shape[:-1] + (1,))
    probs = jax.nn.softmax(jnp.concatenate([scores, sink], axis=-1), axis=-1)[..., :-1]
    out = jnp.einsum('bnhgqk,bnkhd->bnqhgd', probs.astype(v.dtype), vb)
    return out.reshape(b, s, D_ATTN)


def _fwd_setup_inputs(seed: int = 0) -> dict:
    key = jax.random.key(seed)
    ks = jax.random.split(key, 13)
    f32 = jnp.float32

    def w(k, shape, fan_in):
        return jax.random.normal(k, shape, f32) * (fan_in ** -0.5)

    def gain(k, shape):
        return 1.0 + 0.05 * jax.random.normal(k, shape, f32)

    return {
        "x": jax.random.normal(ks[0], (BATCH, SEQ, D_MODEL), f32),
        "g_mix": gain(ks[1], (DEPTH, D_MODEL)),
        "w_in": w(ks[2], (DEPTH, D_MODEL, N_IN), D_MODEL),
        "conv_w": w(ks[3], (DEPTH, CONV_K, D_CONV), CONV_K),
        "attn_sinks": 0.5 * jax.random.normal(ks[4], (DEPTH, N_HEADS), f32),
        "w_conv_out": w(ks[5], (DEPTH, D_CONV, D_MODEL), D_CONV),
        "w_attn_out": w(ks[6], (DEPTH, D_ATTN, D_MODEL), D_ATTN),
        "w_o": w(ks[7], (DEPTH, D_MODEL, D_MODEL), D_MODEL),
        "g_ffn": gain(ks[8], (DEPTH, D_MODEL)),
        "w_gate_up": w(ks[9], (DEPTH, D_MODEL, 2 * D_FF), D_MODEL),
        "w_down": w(ks[10], (DEPTH, D_FF, D_MODEL), D_FF),
        "g_final": gain(ks[11], (D_MODEL,)),
    }


def _fwd_reference(x, g_mix, w_in, conv_w, attn_sinks, w_conv_out, w_attn_out, w_o,
              g_ffn, w_gate_up, w_down, g_final):
    b, s, _ = x.shape
    cos, sin = rotary_tables(s, x.dtype)
    for l in range(DEPTH):
        h = rms_norm(x, g_mix[l])
        proj = jnp.einsum('bsd,dn->bsn', h, w_in[l])
        cb, cc, cx, q, k, v, gate_c, gate_a = jnp.split(proj, SPLIT_POINTS, axis=-1)

        conv_y = cb * causal_depthwise_conv(cc * cx, conv_w[l])
        conv_out = jnp.einsum('bsc,cd->bsd', conv_y, w_conv_out[l])

        q = partial_rotary(q.reshape(b, s, N_HEADS, HEAD_DIM), cos, sin)
        k = partial_rotary(k.reshape(b, s, N_KV_HEADS, HEAD_DIM), cos, sin)
        v = v.reshape(b, s, N_KV_HEADS, HEAD_DIM)
        attn = sliding_window_sink_attention(q, k, v, attn_sinks[l])
        attn_out = jnp.einsum('bsc,cd->bsd', attn, w_attn_out[l])

        merged = jax.nn.sigmoid(gate_c) * conv_out + jax.nn.sigmoid(gate_a) * attn_out
        x = x + jnp.einsum('bsd,de->bse', merged, w_o[l])

        h = rms_norm(x, g_ffn[l])
        gu = jnp.einsum('bsd,df->bsf', h, w_gate_up[l])
        g_act, up = gu[..., :D_FF], gu[..., D_FF:]
        x = x + jnp.einsum('bsf,fd->bsd', jax.nn.silu(g_act) * up, w_down[l])
    return rms_norm(x, g_final)


import jax as _jax
import jax.numpy as _jnp

TWIN_FORMAT = 'train_step'
FWD_PARAMS = ['x', 'g_mix', 'w_in', 'conv_w', 'attn_sinks', 'w_conv_out', 'w_attn_out', 'w_o', 'g_ffn', 'w_gate_up', 'w_down', 'g_final']
TWIN_WEIGHTS = ['g_mix', 'w_in', 'conv_w', 'attn_sinks', 'w_conv_out', 'w_attn_out', 'w_o', 'g_ffn', 'w_gate_up', 'w_down', 'g_final']
TWIN_DIFF_INPUT = 'x'
TWIN_INPUTS = ['x', 'g_mix', 'w_in', 'conv_w', 'attn_sinks', 'w_conv_out', 'w_attn_out', 'w_o', 'g_ffn', 'w_gate_up', 'w_down', 'g_final', 'loss_target', 'm_g_mix', 'm_w_in', 'm_conv_w', 'm_attn_sinks', 'm_w_conv_out', 'm_w_attn_out', 'm_w_o', 'm_g_ffn', 'm_w_gate_up', 'm_w_down', 'm_g_final', 'v_g_mix', 'v_w_in', 'v_conv_w', 'v_attn_sinks', 'v_w_conv_out', 'v_w_attn_out', 'v_w_o', 'v_g_ffn', 'v_w_gate_up', 'v_w_down', 'v_g_final']
TWIN_OUTPUTS = ['loss', 'grad_x', 'grad_g_mix', 'grad_w_in', 'grad_conv_w', 'grad_attn_sinks', 'grad_w_conv_out', 'grad_w_attn_out', 'grad_w_o', 'grad_g_ffn', 'grad_w_gate_up', 'grad_w_down', 'grad_g_final', 'delta_g_mix', 'delta_w_in', 'delta_conv_w', 'delta_attn_sinks', 'delta_w_conv_out', 'delta_w_attn_out', 'delta_w_o', 'delta_g_ffn', 'delta_w_gate_up', 'delta_w_down', 'delta_g_final', 'new_m_g_mix', 'new_m_w_in', 'new_m_conv_w', 'new_m_attn_sinks', 'new_m_w_conv_out', 'new_m_w_attn_out', 'new_m_w_o', 'new_m_g_ffn', 'new_m_w_gate_up', 'new_m_w_down', 'new_m_g_final', 'new_v_g_mix', 'new_v_w_in', 'new_v_conv_w', 'new_v_attn_sinks', 'new_v_w_conv_out', 'new_v_w_attn_out', 'new_v_w_o', 'new_v_g_ffn', 'new_v_w_gate_up', 'new_v_w_down', 'new_v_g_final']
TWIN_LEAF_KINDS = {'loss': 'loss', 'grad_x': 'grad_x', 'grad_g_mix': 'grad_w', 'grad_w_in': 'grad_w', 'grad_conv_w': 'grad_w', 'grad_attn_sinks': 'grad_w', 'grad_w_conv_out': 'grad_w', 'grad_w_attn_out': 'grad_w', 'grad_w_o': 'grad_w', 'grad_g_ffn': 'grad_w', 'grad_w_gate_up': 'grad_w', 'grad_w_down': 'grad_w', 'grad_g_final': 'grad_w', 'delta_g_mix': 'delta_w', 'delta_w_in': 'delta_w', 'delta_conv_w': 'delta_w', 'delta_attn_sinks': 'delta_w', 'delta_w_conv_out': 'delta_w', 'delta_w_attn_out': 'delta_w', 'delta_w_o': 'delta_w', 'delta_g_ffn': 'delta_w', 'delta_w_gate_up': 'delta_w', 'delta_w_down': 'delta_w', 'delta_g_final': 'delta_w', 'new_m_g_mix': 'new_m', 'new_m_w_in': 'new_m', 'new_m_conv_w': 'new_m', 'new_m_attn_sinks': 'new_m', 'new_m_w_conv_out': 'new_m', 'new_m_w_attn_out': 'new_m', 'new_m_w_o': 'new_m', 'new_m_g_ffn': 'new_m', 'new_m_w_gate_up': 'new_m', 'new_m_w_down': 'new_m', 'new_m_g_final': 'new_m', 'new_v_g_mix': 'new_v', 'new_v_w_in': 'new_v', 'new_v_conv_w': 'new_v', 'new_v_attn_sinks': 'new_v', 'new_v_w_conv_out': 'new_v', 'new_v_w_attn_out': 'new_v', 'new_v_w_o': 'new_v', 'new_v_g_ffn': 'new_v', 'new_v_w_gate_up': 'new_v', 'new_v_w_down': 'new_v', 'new_v_g_final': 'new_v'}


def _forward(args):
    return _fwd_reference(*[args[k] for k in FWD_PARAMS])


def _output_shape():
    out = _jax.eval_shape(lambda: _forward(_fwd_setup_inputs(0)))
    return out.shape, out.dtype

N_MICROBATCH = 1
ADAM_LR = 0.001
ADAM_B1 = 0.9
ADAM_B2 = 0.999
ADAM_EPS = 1e-08
ADAM_WD = 0.01
ADAM_STEP = 10
PER_EXAMPLE_BATCH_AXIS = {'x': 0, 'loss_target': 0}
SHARED_INPUTS = []
_WEIGHT_DTYPES = {'g_mix': _jnp.float32, 'w_in': _jnp.float32, 'conv_w': _jnp.float32, 'attn_sinks': _jnp.float32, 'w_conv_out': _jnp.float32, 'w_attn_out': _jnp.float32, 'w_o': _jnp.float32, 'g_ffn': _jnp.float32, 'w_gate_up': _jnp.float32, 'w_down': _jnp.float32, 'g_final': _jnp.float32}
MOMENT_SCALE = {'g_mix': 1.196628e-01, 'w_in': 4.653832e-02, 'conv_w': 6.616066e-02, 'attn_sinks': 8.726469e-03, 'w_conv_out': 6.581954e-02, 'w_attn_out': 1.364345e-02, 'w_o': 6.676451e-02, 'g_ffn': 9.021972e-02, 'w_gate_up': 3.598939e-02, 'w_down': 5.912822e-02, 'g_final': 1.604199e+01}


def _to_microbatches(a, axis):
    t = _jnp.moveaxis(a, axis, 0)
    t = t.reshape((N_MICROBATCH, t.shape[0] // N_MICROBATCH) + t.shape[1:])
    return _jnp.moveaxis(t, 1, axis + 1)


def setup_inputs(seed: int = 0) -> dict:
    inp = _fwd_setup_inputs(seed)
    key = _jax.random.fold_in(_jax.random.key(seed), 7919)
    shape, _ = _output_shape()
    out = dict(inp)
    out["loss_target"] = _jax.random.normal(_jax.random.fold_in(key, 0), shape, _jnp.float32)
    for i, name in enumerate(TWIN_WEIGHTS):
        w = inp[name].astype(_jnp.float32)
        if MOMENT_SCALE is None:
            s = _jnp.sqrt(_jnp.mean(_jnp.square(w)) + 1e-30)
        else:
            s = MOMENT_SCALE[name]
        km, kv = _jax.random.split(_jax.random.fold_in(key, i + 1))
        out[name] = w
        out["m_" + name] = s * _jax.random.normal(km, w.shape, _jnp.float32)
        out["v_" + name] = (s * s) * _jax.random.uniform(kv, w.shape, _jnp.float32, 0.5, 1.5)
    if N_MICROBATCH > 1:
        for name, axis in PER_EXAMPLE_BATCH_AXIS.items():
            out[name] = _to_microbatches(out[name], axis)
    return {'x': out['x'], 'g_mix': out['g_mix'], 'w_in': out['w_in'], 'conv_w': out['conv_w'], 'attn_sinks': out['attn_sinks'], 'w_conv_out': out['w_conv_out'], 'w_attn_out': out['w_attn_out'], 'w_o': out['w_o'], 'g_ffn': out['g_ffn'], 'w_gate_up': out['w_gate_up'], 'w_down': out['w_down'], 'g_final': out['g_final'], 'loss_target': out['loss_target'], 'm_g_mix': out['m_g_mix'], 'm_w_in': out['m_w_in'], 'm_conv_w': out['m_conv_w'], 'm_attn_sinks': out['m_attn_sinks'], 'm_w_conv_out': out['m_w_conv_out'], 'm_w_attn_out': out['m_w_attn_out'], 'm_w_o': out['m_w_o'], 'm_g_ffn': out['m_g_ffn'], 'm_w_gate_up': out['m_w_gate_up'], 'm_w_down': out['m_w_down'], 'm_g_final': out['m_g_final'], 'v_g_mix': out['v_g_mix'], 'v_w_in': out['v_w_in'], 'v_conv_w': out['v_conv_w'], 'v_attn_sinks': out['v_attn_sinks'], 'v_w_conv_out': out['v_w_conv_out'], 'v_w_attn_out': out['v_w_attn_out'], 'v_w_o': out['v_w_o'], 'v_g_ffn': out['v_g_ffn'], 'v_w_gate_up': out['v_w_gate_up'], 'v_w_down': out['v_w_down'], 'v_g_final': out['v_g_final']}


def _loss(weights, diff, rest, loss_target):
    with _jax.named_scope("forward"):
        args = {**rest, TWIN_DIFF_INPUT: diff, **{k: w.astype(_WEIGHT_DTYPES[k]) for k, w in weights.items()}}
        y = _forward(args)
    with _jax.named_scope("loss_head"):
        err = _jnp.square(y.astype(_jnp.float32) - loss_target)
        return 0.5 * _jnp.sum(_jnp.mean(err, axis=-1)) if err.ndim else 0.5 * err


def _adamw(w, g, m, v):
    m = ADAM_B1 * m + (1.0 - ADAM_B1) * g
    v = ADAM_B2 * v + (1.0 - ADAM_B2) * _jnp.square(g)
    m_hat = m / (1.0 - ADAM_B1 ** ADAM_STEP)
    v_hat = v / (1.0 - ADAM_B2 ** ADAM_STEP)
    delta = -ADAM_LR * (m_hat / (_jnp.sqrt(v_hat) + ADAM_EPS) + ADAM_WD * w)
    return delta, m, v


def reference(x, g_mix, w_in, conv_w, attn_sinks, w_conv_out, w_attn_out, w_o, g_ffn, w_gate_up, w_down, g_final, loss_target, m_g_mix, m_w_in, m_conv_w, m_attn_sinks, m_w_conv_out, m_w_attn_out, m_w_o, m_g_ffn, m_w_gate_up, m_w_down, m_g_final, v_g_mix, v_w_in, v_conv_w, v_attn_sinks, v_w_conv_out, v_w_attn_out, v_w_o, v_g_ffn, v_w_gate_up, v_w_down, v_g_final):
    given = dict(x=x, g_mix=g_mix, w_in=w_in, conv_w=conv_w, attn_sinks=attn_sinks, w_conv_out=w_conv_out, w_attn_out=w_attn_out, w_o=w_o, g_ffn=g_ffn, w_gate_up=w_gate_up, w_down=w_down, g_final=g_final, loss_target=loss_target, m_g_mix=m_g_mix, m_w_in=m_w_in, m_conv_w=m_conv_w, m_attn_sinks=m_attn_sinks, m_w_conv_out=m_w_conv_out, m_w_attn_out=m_w_attn_out, m_w_o=m_w_o, m_g_ffn=m_g_ffn, m_w_gate_up=m_w_gate_up, m_w_down=m_w_down, m_g_final=m_g_final, v_g_mix=v_g_mix, v_w_in=v_w_in, v_conv_w=v_conv_w, v_attn_sinks=v_attn_sinks, v_w_conv_out=v_w_conv_out, v_w_attn_out=v_w_attn_out, v_w_o=v_w_o, v_g_ffn=v_g_ffn, v_w_gate_up=v_w_gate_up, v_w_down=v_w_down, v_g_final=v_g_final)
    weights = {n: given[n] for n in TWIN_WEIGHTS}
    shared = {n: given[n] for n in SHARED_INPUTS}
    per_example = {n: given[n] for n in ['x']}
    grad_fn = _jax.value_and_grad(_loss, argnums=(0, 1))

    def one_microbatch(ex, loss_target):
        ex = dict(ex)
        diff = ex.pop(TWIN_DIFF_INPUT)
        return grad_fn(weights, diff, {**shared, **ex}, loss_target)

    if N_MICROBATCH == 1:
        loss, (grad_w, grad_x) = one_microbatch(per_example, given["loss_target"])
    else:
        def body(carry, xs):
            loss_sum, grad_sum = carry
            l_k, (gw_k, gx_k) = one_microbatch(xs[0], xs[1])
            with _jax.named_scope("update"):
                return (loss_sum + l_k, _jax.tree.map(_jnp.add, grad_sum, gw_k)), gx_k

        init = (_jnp.zeros((), _jnp.float32), _jax.tree.map(_jnp.zeros_like, weights))
        (loss, grad_w), grad_x = _jax.lax.scan(body, init, (per_example, given["loss_target"]))
    with _jax.named_scope("update"):
        delta_w, new_m, new_v = {}, {}, {}
        for n in TWIN_WEIGHTS:
            delta_w[n], new_m[n], new_v[n] = _adamw(weights[n], grad_w[n], given["m_" + n], given["v_" + n])
    return (loss, grad_x, *[grad_w[n] for n in TWIN_WEIGHTS], *[delta_w[n] for n in TWIN_WEIGHTS],
            *[new_m[n] for n in TWIN_WEIGHTS], *[new_v[n] for n in TWIN_WEIGHTS])
```

```python
import math

import jax
import jax.numpy as jnp
from jax import lax
from jax.experimental import pallas as pl
from jax.experimental.pallas import tpu as pltpu

F32 = jnp.float32
BF16 = jnp.bfloat16

D = 1024
HEAD_DIM = 64
N_HEADS = 16
N_KV = 4
GROUP = N_HEADS // N_KV
D_KV = N_KV * HEAD_DIM
BLOCK = 128
ROT_DIM = HEAD_DIM // 4
ROPE_THETA = 500000.0
ATTN_SCALE = 1.0 / math.sqrt(HEAD_DIM)
NEG_INF = -1e30
D_FF = 2816
N_IN = 6656
EPS = 1e-5
C_CB, C_CC, C_CX, C_Q, C_K, C_V, C_GC, C_GA = 0, 1024, 2048, 3072, 4096, 4352, 4608, 5632

LR, B1, B2, EPS_ADAM, WD, STEP = 0.001, 0.9, 0.999, 1e-08, 0.01, 10

N_DEV = 8
MESH = pl.DeviceIdType.MESH
VMEM_LIMIT = 56 * 1024 * 1024

NN = (((1,), (0,)), ((), ()))
NT = (((1,), (1,)), ((), ()))
TN = (((0,), (0,)), ((), ()))


def _call(body, **kw):
    return pl.pallas_call(body, **kw)


def _params(*sem):
    return pltpu.CompilerParams(dimension_semantics=sem, vmem_limit_bytes=VMEM_LIMIT)


def _sds(shape, dtype):
    return jax.ShapeDtypeStruct(shape, dtype)


def _matmul(a, b, *, mode, tm, tn, tk, out_dtype, name, res=None):
    if mode == "nn":
        (m, kk), (_, n), dims = a.shape, b.shape, NN
    elif mode == "nt":
        (m, kk), (n, _), dims = a.shape, b.shape, NT
    else:
        (kk, m), (_, n), dims = a.shape, b.shape, TN
    tm, tn, tk = min(tm, m), min(tn, n), min(tk, kk)
    assert m % tm == 0 and n % tn == 0 and kk % tk == 0, (name, m, n, kk, tm, tn, tk)
    if mode == "tn":
        a_spec = pl.BlockSpec((tk, tm), lambda i, j, k: (k, i))
        b_spec = pl.BlockSpec((tk, tn), lambda i, j, k: (k, j))
    elif mode == "nn":
        a_spec = pl.BlockSpec((tm, tk), lambda i, j, k: (i, k))
        b_spec = pl.BlockSpec((tk, tn), lambda i, j, k: (k, j))
    else:
        a_spec = pl.BlockSpec((tm, tk), lambda i, j, k: (i, k))
        b_spec = pl.BlockSpec((tn, tk), lambda i, j, k: (j, k))
    nk = kk // tk
    o_spec = pl.BlockSpec((tm, tn), lambda i, j, k: (i, j))
    has_res = res is not None

    def body(*refs):
        if has_res:
            a_ref, b_ref, r_ref, o_ref = refs[:4]
        else:
            a_ref, b_ref, o_ref = refs[:3]
            r_ref = None
        part = lax.dot_general(a_ref[...], b_ref[...], dims, preferred_element_type=F32)

        def finish(acc):
            if has_res:
                acc = acc + r_ref[...]
            o_ref[...] = acc.astype(o_ref.dtype)

        if nk == 1:
            finish(part)
        else:
            acc_ref = refs[-1]
            k = pl.program_id(2)

            @pl.when(k == 0)
            def _():
                acc_ref[...] = part

            @pl.when(k > 0)
            def _():
                acc_ref[...] += part

            @pl.when(k == nk - 1)
            def _():
                finish(acc_ref[...])

    ins = [a, b] + ([res] if has_res else [])
    in_specs = [a_spec, b_spec] + ([o_spec] if has_res else [])
    scratch = [] if nk == 1 else [pltpu.VMEM((tm, tn), F32)]
    return _call(
        body, name=name, grid=(m // tm, n // tn, nk), in_specs=in_specs, out_specs=o_spec,
        out_shape=_sds((m, n), out_dtype), scratch_shapes=scratch,
        compiler_params=_params("parallel", "parallel", "arbitrary"),
    )(*ins)


def _row_tile(s):
    return min(256, s)


def _rms_fwd(x, g, name):
    s = x.shape[0]
    tm = _row_tile(s)

    def body(x_ref, g_ref, h_ref):
        xv = x_ref[...]
        r = lax.rsqrt(jnp.mean(xv * xv, axis=-1, keepdims=True) + EPS)
        h_ref[...] = (xv * r * g_ref[...]).astype(BF16)

    row = pl.BlockSpec((tm, D), lambda i: (i, 0))
    return _call(
        body, name=name, grid=(s // tm,), in_specs=[row, pl.BlockSpec((1, D), lambda i: (0, 0))],
        out_specs=row, out_shape=_sds((s, D), BF16), compiler_params=_params("parallel"),
    )(x, g)


def _rms_bwd(dh, x, g, dres, name):
    s = x.shape[0]
    tm = _row_tile(s)

    def body(dh_ref, x_ref, g_ref, dres_ref, dx_ref, dxb_ref, dg_ref):
        xv = x_ref[...]
        r = lax.rsqrt(jnp.mean(xv * xv, axis=-1, keepdims=True) + EPS)
        xh = xv * r
        dhv = dh_ref[...]
        dyg = dhv * g_ref[...]
        dx = dres_ref[...] + r * (dyg - xh * jnp.mean(dyg * xh, axis=-1, keepdims=True))
        dx_ref[...] = dx
        dxb_ref[...] = dx.astype(BF16)
        part = jnp.sum(dhv * xh, axis=0, keepdims=True)

        @pl.when(pl.program_id(0) == 0)
        def _():
            dg_ref[...] = part

        @pl.when(pl.program_id(0) > 0)
        def _():
            dg_ref[...] += part

    row = pl.BlockSpec((tm, D), lambda i: (i, 0))
    vec = pl.BlockSpec((1, D), lambda i: (0, 0))
    return _call(
        body, name=name, grid=(s // tm,), in_specs=[row, row, vec, row], out_specs=[row, row, vec],
        out_shape=[_sds((s, D), F32), _sds((s, D), BF16), _sds((1, D), F32)],
        compiler_params=_params("arbitrary"),
    )(dh, x, g, dres)


def _loss_head(x2, g, tgt, name):
    s = x2.shape[0]
    tm = _row_tile(s)

    def body(x_ref, g_ref, t_ref, dx_ref, dxb_ref, dg_ref, l_ref):
        xv = x_ref[...]
        gv = g_ref[...]
        r = lax.rsqrt(jnp.mean(xv * xv, axis=-1, keepdims=True) + EPS)
        xh = xv * r
        err = xh * gv - t_ref[...]
        dy = err * (1.0 / D)
        dyg = dy * gv
        dx = r * (dyg - xh * jnp.mean(dyg * xh, axis=-1, keepdims=True))
        dx_ref[...] = dx
        dxb_ref[...] = dx.astype(BF16)
        dg_part = jnp.sum(dy * xh, axis=0, keepdims=True)
        l_part = jnp.sum(err * err, axis=0, keepdims=True)

        @pl.when(pl.program_id(0) == 0)
        def _():
            dg_ref[...] = dg_part
            l_ref[...] = l_part

        @pl.when(pl.program_id(0) > 0)
        def _():
            dg_ref[...] += dg_part
            l_ref[...] += l_part

    row = pl.BlockSpec((tm, D), lambda i: (i, 0))
    vec = pl.BlockSpec((1, D), lambda i: (0, 0))
    return _call(
        body, name=name, grid=(s // tm,), in_specs=[row, vec, row], out_specs=[row, row, vec, vec],
        out_shape=[_sds((s, D), F32), _sds((s, D), BF16), _sds((1, D), F32), _sds((1, D), F32)],
        compiler_params=_params("arbitrary"),
    )(x2, g, tgt)


CONV_TC = 256


def _shift_down(u, k, rows):
    return jnp.where(rows >= k, pltpu.roll(u, k, 0), 0.0)


def _shift_up(u, k, rows, s):
    return jnp.where(rows < s - k, pltpu.roll(u, s - k, 0), 0.0)


def _conv_specs(s):
    nb = D // CONV_TC

    def col(c0):
        return pl.BlockSpec((s, CONV_TC), lambda j, c0=c0: (0, c0 // CONV_TC + j))

    return nb, col


def _conv_fwd(proj, conv_w, name):
    s = proj.shape[0]
    nb, col = _conv_specs(s)

    def body(cb_ref, cc_ref, cx_ref, w_ref, y_ref):
        rows = lax.broadcasted_iota(jnp.int32, (s, CONV_TC), 0)
        u = cc_ref[...].astype(F32) * cx_ref[...].astype(F32)
        w = w_ref[...]
        c = w[0:1] * _shift_down(u, 2, rows) + w[1:2] * _shift_down(u, 1, rows) + w[2:3] * u
        y_ref[...] = (cb_ref[...].astype(F32) * c).astype(BF16)

    return _call(
        body, name=name, grid=(nb,),
        in_specs=[col(C_CB), col(C_CC), col(C_CX), pl.BlockSpec((3, CONV_TC), lambda j: (0, j))],
        out_specs=pl.BlockSpec((s, CONV_TC), lambda j: (0, j)), out_shape=_sds((s, D), BF16),
        compiler_params=_params("parallel"),
    )(proj, proj, proj, conv_w)


def _conv_bwd(dy, proj, conv_w, dproj, name):
    s = proj.shape[0]
    nb, col = _conv_specs(s)

    def body(dy_ref, cb_ref, cc_ref, cx_ref, w_ref, dproj_in, dproj_ref, dw_ref, buf, sems):
        del dproj_in
        j = pl.program_id(0)
        rows = lax.broadcasted_iota(jnp.int32, (s, CONV_TC), 0)
        cc = cc_ref[...].astype(F32)
        cx = cx_ref[...].astype(F32)
        u = cc * cx
        u1 = _shift_down(u, 1, rows)
        u2 = _shift_down(u, 2, rows)
        w = w_ref[...]
        c = w[0:1] * u2 + w[1:2] * u1 + w[2:3] * u
        dyv = dy_ref[...]
        dc = dyv * cb_ref[...].astype(F32)
        du = w[2:3] * dc + w[1:2] * _shift_up(dc, 1, rows, s) + w[0:1] * _shift_up(dc, 2, rows, s)
        buf[0] = (dyv * c).astype(BF16)
        buf[1] = (du * cx).astype(BF16)
        buf[2] = (du * cc).astype(BF16)
        dw_ref[...] = jnp.concatenate(
            [jnp.sum(dc * u2, axis=0, keepdims=True), jnp.sum(dc * u1, axis=0, keepdims=True),
             jnp.sum(dc * u, axis=0, keepdims=True)], axis=0)
        copies = []
        for p, c0 in enumerate((C_CB, C_CC, C_CX)):
            start = pl.multiple_of(c0 + j * CONV_TC, CONV_TC)
            copies.append(pltpu.make_async_copy(buf.at[p], dproj_ref.at[:, pl.ds(start, CONV_TC)], sems.at[p]))
        for cp in copies:
            cp.start()
        for cp in copies:
            cp.wait()

    return _call(
        body, name=name, grid=(nb,),
        in_specs=[pl.BlockSpec((s, CONV_TC), lambda j: (0, j)), col(C_CB), col(C_CC), col(C_CX),
                  pl.BlockSpec((3, CONV_TC), lambda j: (0, j)), pl.BlockSpec(memory_space=pl.ANY)],
        out_specs=[pl.BlockSpec(memory_space=pl.ANY), pl.BlockSpec((3, CONV_TC), lambda j: (0, j))],
        out_shape=[_sds((s, N_IN), BF16), _sds((3, D), F32)],
        scratch_shapes=[pltpu.VMEM((3, s, CONV_TC), BF16), pltpu.SemaphoreType.DMA((3,))],
        input_output_aliases={5: 0}, compiler_params=_params("arbitrary"),
    )(dy, proj, proj, proj, conv_w, dproj)


def _rope_tables(s):
    inv_freq = ROPE_THETA ** (-jnp.arange(0, ROT_DIM, 2, dtype=F32) / ROT_DIM)
    ang = jnp.arange(s, dtype=F32)[:, None] * inv_freq[None, :]
    cos, sin = jnp.cos(ang), jnp.sin(ang)
    half = ROT_DIM // 2
    ones = jnp.ones((s, HEAD_DIM - ROT_DIM), F32)
    zeros = jnp.zeros((s, HEAD_DIM - ROT_DIM), F32)
    zh = jnp.zeros((s, half), F32)
    c64 = jnp.concatenate([cos, cos, ones], axis=1)
    a64 = jnp.concatenate([-sin, zh, zeros], axis=1)
    b64 = jnp.concatenate([zh, sin, zeros], axis=1)
    return jnp.concatenate([c64, c64, a64, a64, b64, b64], axis=1)


def _rope(x, tab):
    c, a, b = tab[:, 0:128], tab[:, 128:256], tab[:, 256:384]
    outs = []
    for i in range(x.shape[1] // 128):
        xc = x[:, i * 128:(i + 1) * 128]
        outs.append(xc * c + pltpu.roll(xc, 120, 1) * a + pltpu.roll(xc, 8, 1) * b)
    return outs[0] if len(outs) == 1 else jnp.concatenate(outs, axis=1)


def _rope_t(dx, tab):
    c, a, b = tab[:, 0:128], tab[:, 128:256], tab[:, 256:384]
    outs = []
    for i in range(dx.shape[1] // 128):
        dc = dx[:, i * 128:(i + 1) * 128]
        outs.append(dc * c + pltpu.roll(dc * a, 8, 1) + pltpu.roll(dc * b, 120, 1))
    return outs[0] if len(outs) == 1 else jnp.concatenate(outs, axis=1)


def _attn_mask(n):
    qi = lax.broadcasted_iota(jnp.int32, (GROUP * BLOCK, 2 * BLOCK), 0) & (BLOCK - 1)
    kj = lax.broadcasted_iota(jnp.int32, (GROUP * BLOCK, 2 * BLOCK), 1)
    rel = qi + BLOCK - kj
    return (rel >= 0) & (rel < BLOCK) & ((kj >= BLOCK) | (n > 0))


def _sink_col(sink_ref, hk):
    return jnp.concatenate([jnp.full((BLOCK, 1), sink_ref[0, hk * GROUP + g], F32) for g in range(GROUP)], axis=0)


def _attn_in_specs():
    prev = lambda n: jnp.maximum(n - 1, 0)
    return [
        pl.BlockSpec((BLOCK, D), lambda n: (n, C_Q // D)),
        pl.BlockSpec((BLOCK, D_KV), lambda n: (n, C_K // D_KV)),
        pl.BlockSpec((BLOCK, D_KV), lambda n: (prev(n), C_K // D_KV)),
        pl.BlockSpec((BLOCK, D_KV), lambda n: (n, C_V // D_KV)),
        pl.BlockSpec((BLOCK, D_KV), lambda n: (prev(n), C_V // D_KV)),
        pl.BlockSpec((BLOCK, 384), lambda n: (n, 0)),
        pl.BlockSpec((BLOCK, 384), lambda n: (prev(n), 0)),
        pl.BlockSpec(memory_space=pltpu.SMEM),
    ]


def _load_qkv(q_ref, kc_ref, kp_ref, vc_ref, vp_ref, tc_ref, tp_ref):
    q = _rope(q_ref[...].astype(F32), tc_ref[...]).astype(BF16)
    kc = _rope(kc_ref[...].astype(F32), tc_ref[...]).astype(BF16)
    kp = _rope(kp_ref[...].astype(F32), tp_ref[...]).astype(BF16)
    return q, kc, kp, vc_ref[...], vp_ref[...]


def _group_rows(x, hk):
    base = hk * GROUP * HEAD_DIM
    return jnp.concatenate([x[:, base + g * HEAD_DIM: base + (g + 1) * HEAD_DIM] for g in range(GROUP)], axis=0)


def _kv_rows(prev, cur, hk):
    sl = slice(hk * HEAD_DIM, (hk + 1) * HEAD_DIM)
    return jnp.concatenate([prev[:, sl], cur[:, sl]], axis=0)


def _attn_fwd(proj, tab, sinks, name):
    s = proj.shape[0]

    def body(q_ref, kc_ref, kp_ref, vc_ref, vp_ref, tc_ref, tp_ref, sink_ref, o_ref):
        n = pl.program_id(0)
        q, kc, kp, vc, vp = _load_qkv(q_ref, kc_ref, kp_ref, vc_ref, vp_ref, tc_ref, tp_ref)
        mask = _attn_mask(n)
        for hk in range(N_KV):
            qg = _group_rows(q, hk)
            kcat = _kv_rows(kp, kc, hk)
            vcat = _kv_rows(vp, vc, hk)
            sc = lax.dot_general(qg, kcat, NT, preferred_element_type=F32) * ATTN_SCALE
            sc = jnp.where(mask, sc, NEG_INF)
            sink = _sink_col(sink_ref, hk)
            m = jnp.maximum(jnp.max(sc, axis=1, keepdims=True), sink)
            p = jnp.exp(sc - m)
            inv = 1.0 / (jnp.sum(p, axis=1, keepdims=True) + jnp.exp(sink - m))
            o = lax.dot_general((p * inv).astype(BF16), vcat, NN, preferred_element_type=F32)
            base = hk * GROUP * HEAD_DIM
            for g in range(GROUP):
                o_ref[:, base + g * HEAD_DIM: base + (g + 1) * HEAD_DIM] = o[g * BLOCK:(g + 1) * BLOCK].astype(BF16)

    return _call(
        body, name=name, grid=(s // BLOCK,), in_specs=_attn_in_specs(),
        out_specs=pl.BlockSpec((BLOCK, D), lambda n: (n, 0)), out_shape=_sds((s, D), BF16),
        compiler_params=_params("parallel"),
    )(proj, proj, proj, proj, proj, tab, tab, sinks)


def _attn_bwd(do, proj, tab, sinks, dproj, name):
    s = proj.shape[0]
    nblk = s // BLOCK

    def body(do_ref, q_ref, kc_ref, kp_ref, vc_ref, vp_ref, tc_ref, tp_ref, sink_ref, dproj_in, dproj_ref,
             dk_ref, dv_ref, ds_ref, dqbuf, dqout, dkbuf, dvbuf, sem):
        del dproj_in
        n = pl.program_id(0)

        @pl.when(n == 0)
        def _():
            dk_ref[...] = jnp.zeros_like(dk_ref)
            dv_ref[...] = jnp.zeros_like(dv_ref)
            ds_ref[...] = jnp.zeros_like(ds_ref)

        q, kc, kp, vc, vp = _load_qkv(q_ref, kc_ref, kp_ref, vc_ref, vp_ref, tc_ref, tp_ref)
        dov = do_ref[...]
        mask = _attn_mask(n)
        rows = GROUP * BLOCK
        head_off = lax.broadcasted_iota(jnp.int32, (rows, 128), 1) - (lax.broadcasted_iota(jnp.int32, (rows, 128), 0) >> 7)
        dsink_row = jnp.zeros((1, 128), F32)
        prev0 = pl.multiple_of(jnp.maximum(n - 1, 0) * BLOCK, BLOCK)
        cur0 = pl.multiple_of(n * BLOCK, BLOCK)
        for hk in range(N_KV):
            qg = _group_rows(q, hk)
            dog = _group_rows(dov, hk)
            kcat = _kv_rows(kp, kc, hk)
            vcat = _kv_rows(vp, vc, hk)
            sc = lax.dot_general(qg, kcat, NT, preferred_element_type=F32) * ATTN_SCALE
            sc = jnp.where(mask, sc, NEG_INF)
            sink = _sink_col(sink_ref, hk)
            m = jnp.maximum(jnp.max(sc, axis=1, keepdims=True), sink)
            e = jnp.exp(sc - m)
            es = jnp.exp(sink - m)
            inv = 1.0 / (jnp.sum(e, axis=1, keepdims=True) + es)
            p = e * inv
            pb = p.astype(BF16)
            dp = lax.dot_general(dog, vcat, NT, preferred_element_type=F32)
            delta = jnp.sum(p * dp, axis=1, keepdims=True)
            dsc = (p * (dp - delta) * ATTN_SCALE).astype(BF16)
            dsk = -(es * inv) * delta
            dsink_row = dsink_row + jnp.sum(jnp.where(head_off == hk * GROUP, dsk, 0.0), axis=0, keepdims=True)
            dqg = lax.dot_general(dsc, kcat, NN, preferred_element_type=F32)
            dkcat = lax.dot_general(dsc, qg, TN, preferred_element_type=F32)
            dvcat = lax.dot_general(pb, dog, TN, preferred_element_type=F32)
            base = hk * GROUP * HEAD_DIM
            for g in range(GROUP):
                dqbuf[:, base + g * HEAD_DIM: base + (g + 1) * HEAD_DIM] = dqg[g * BLOCK:(g + 1) * BLOCK]
            sl = slice(hk * HEAD_DIM, (hk + 1) * HEAD_DIM)
            dkbuf[:, sl] = dkcat
            dvbuf[:, sl] = dvcat

        @pl.when(n > 0)
        def _():
            dk_ref[pl.ds(prev0, BLOCK), :] += dkbuf[0:BLOCK, :]
            dv_ref[pl.ds(prev0, BLOCK), :] += dvbuf[0:BLOCK, :]

        dk_ref[pl.ds(cur0, BLOCK), :] += dkbuf[BLOCK:2 * BLOCK, :]
        dv_ref[pl.ds(cur0, BLOCK), :] += dvbuf[BLOCK:2 * BLOCK, :]
        ds_ref[...] += dsink_row
        dqout[...] = _rope_t(dqbuf[...], tc_ref[...]).astype(BF16)
        cp = pltpu.make_async_copy(dqout, dproj_ref.at[pl.ds(cur0, BLOCK), pl.ds(C_Q, D)], sem)
        cp.start()
        cp.wait()

    blk = lambda w: pl.BlockSpec((BLOCK, w), lambda n: (n, 0))
    whole = lambda w: pl.BlockSpec((s, w), lambda n: (0, 0))
    anyspec = pl.BlockSpec(memory_space=pl.ANY)
    n_in = 1 + len(_attn_in_specs())
    return _call(
        body, name=name, grid=(nblk,), in_specs=[blk(D)] + _attn_in_specs() + [anyspec],
        out_specs=[anyspec, whole(D_KV), whole(D_KV), pl.BlockSpec((1, 128), lambda n: (0, 0))],
        out_shape=[_sds((s, N_IN), BF16), _sds((s, D_KV), F32), _sds((s, D_KV), F32), _sds((1, 128), F32)],
        scratch_shapes=[pltpu.VMEM((BLOCK, D), F32), pltpu.VMEM((BLOCK, D), BF16), pltpu.VMEM((2 * BLOCK, D_KV), F32),
                        pltpu.VMEM((2 * BLOCK, D_KV), F32), pltpu.SemaphoreType.DMA(())],
        input_output_aliases={n_in: 0}, compiler_params=_params("arbitrary"),
    )(do, proj, proj, proj, proj, proj, tab, tab, sinks, dproj)


def _kv_bwd(dkr, dv, tab, dproj, name):
    s = dkr.shape[0]
    tm = _row_tile(s)

    def body(dk_ref, dv_ref, t_ref, dproj_in, o_ref):
        del dproj_in
        o_ref[:, 0:D_KV] = _rope_t(dk_ref[...], t_ref[...]).astype(BF16)
        o_ref[:, D_KV:2 * D_KV] = dv_ref[...].astype(BF16)

    row = lambda w: pl.BlockSpec((tm, w), lambda i: (i, 0))
    return _call(
        body, name=name, grid=(s // tm,),
        in_specs=[row(D_KV), row(D_KV), row(384), pl.BlockSpec(memory_space=pl.ANY)],
        out_specs=pl.BlockSpec((tm, 2 * D_KV), lambda i: (i, C_K // (2 * D_KV))),
        out_shape=_sds((s, N_IN), BF16), input_output_aliases={3: 0}, compiler_params=_params("parallel"),
    )(dkr, dv, tab, dproj)


EW_TC = 512


def _sigmoid(x):
    return 1.0 / (1.0 + jnp.exp(-x))


def _merge_fwd(proj, conv_out, attn_out, name):
    s = proj.shape[0]
    tm = _row_tile(s)
    tile = pl.BlockSpec((tm, EW_TC), lambda i, j: (i, j))

    def body(gc_ref, ga_ref, co_ref, ao_ref, o_ref):
        o_ref[...] = (_sigmoid(gc_ref[...].astype(F32)) * co_ref[...]
                      + _sigmoid(ga_ref[...].astype(F32)) * ao_ref[...]).astype(BF16)

    return _call(
        body, name=name, grid=(s // tm, D // EW_TC),
        in_specs=[pl.BlockSpec((tm, EW_TC), lambda i, j: (i, C_GC // EW_TC + j)),
                  pl.BlockSpec((tm, EW_TC), lambda i, j: (i, C_GA // EW_TC + j)), tile, tile],
        out_specs=tile, out_shape=_sds((s, D), BF16), compiler_params=_params("parallel", "parallel"),
    )(proj, proj, conv_out, attn_out)


def _merge_bwd(dmerged, proj, conv_out, attn_out, name):
    s = proj.shape[0]
    tm = _row_tile(s)
    tile = pl.BlockSpec((tm, EW_TC), lambda i, j: (i, j))
    anyspec = pl.BlockSpec(memory_space=pl.ANY)

    def body(dm_ref, gc_ref, ga_ref, co_ref, ao_ref, dproj_ref, dco_ref, dao_ref, buf, sems):
        i, j = pl.program_id(0), pl.program_id(1)
        dm = dm_ref[...]
        sc = _sigmoid(gc_ref[...].astype(F32))
        sa = _sigmoid(ga_ref[...].astype(F32))
        dco_ref[...] = (dm * sc).astype(BF16)
        dao_ref[...] = (dm * sa).astype(BF16)
        buf[0] = (dm * co_ref[...] * sc * (1.0 - sc)).astype(BF16)
        buf[1] = (dm * ao_ref[...] * sa * (1.0 - sa)).astype(BF16)
        r0 = pl.multiple_of(i * tm, tm)
        copies = []
        for p, c0 in enumerate((C_GC, C_GA)):
            start = pl.multiple_of(c0 + j * EW_TC, EW_TC)
            copies.append(pltpu.make_async_copy(buf.at[p], dproj_ref.at[pl.ds(r0, tm), pl.ds(start, EW_TC)], sems.at[p]))
        for cp in copies:
            cp.start()
        for cp in copies:
            cp.wait()

    return _call(
        body, name=name, grid=(s // tm, D // EW_TC),
        in_specs=[tile, pl.BlockSpec((tm, EW_TC), lambda i, j: (i, C_GC // EW_TC + j)),
                  pl.BlockSpec((tm, EW_TC), lambda i, j: (i, C_GA // EW_TC + j)), tile, tile],
        out_specs=[anyspec, tile, tile],
        out_shape=[_sds((s, N_IN), BF16), _sds((s, D), BF16), _sds((s, D), BF16)],
        scratch_shapes=[pltpu.VMEM((2, tm, EW_TC), BF16), pltpu.SemaphoreType.DMA((2,))],
        compiler_params=_params("arbitrary", "arbitrary"),
    )(dmerged, proj, proj, conv_out, attn_out)


FF_TC = 256


def _swiglu_fwd(gu, name):
    s = gu.shape[0]
    tm = _row_tile(s)
    nb = D_FF // FF_TC

    def body(g_ref, u_ref, o_ref):
        g = g_ref[...].astype(F32)
        o_ref[...] = (g * _sigmoid(g) * u_ref[...].astype(F32)).astype(BF16)

    return _call(
        body, name=name, grid=(s // tm, nb),
        in_specs=[pl.BlockSpec((tm, FF_TC), lambda i, j: (i, j)), pl.BlockSpec((tm, FF_TC), lambda i, j: (i, nb + j))],
        out_specs=pl.BlockSpec((tm, FF_TC), lambda i, j: (i, j)), out_shape=_sds((s, D_FF), BF16),
        compiler_params=_params("parallel", "parallel"),
    )(gu, gu)


def _swiglu_bwd(dact, gu, name):
    s = gu.shape[0]
    tm = _row_tile(s)
    nb = D_FF // FF_TC

    def body(da_ref, g_ref, u_ref, o_ref):
        p = pl.program_id(2)
        da = da_ref[...].astype(F32)
        g = g_ref[...].astype(F32)
        sg = _sigmoid(g)

        @pl.when(p == 0)
        def _():
            o_ref[...] = (da * u_ref[...].astype(F32) * (sg * (1.0 + g * (1.0 - sg)))).astype(BF16)

        @pl.when(p == 1)
        def _():
            o_ref[...] = (da * g * sg).astype(BF16)

    tile = pl.BlockSpec((tm, FF_TC), lambda i, j, p: (i, j))
    return _call(
        body, name=name, grid=(s // tm, nb, 2),
        in_specs=[tile, tile, pl.BlockSpec((tm, FF_TC), lambda i, j, p: (i, nb + j))],
        out_specs=pl.BlockSpec((tm, FF_TC), lambda i, j, p: (i, p * nb + j)), out_shape=_sds((s, 2 * D_FF), BF16),
        compiler_params=_params("parallel", "parallel", "arbitrary"),
    )(dact, gu, gu)


def _local_step(x, tgt, g_mix, g_ffn, g_final, sinks, conv_w, win_t, wgu_t, wd, wco, wao, wo):
    s = x.shape[0]
    tab = _rope_tables(s)
    big = dict(tm=1024, tn=512, tk=1024)
    h1 = _rms_fwd(x, g_mix, "rms1_fwd")
    proj = _matmul(h1, win_t, mode="nt", out_dtype=BF16, name="proj_fwd", **big)
    conv_y = _conv_fwd(proj, conv_w, "conv_fwd")
    conv_out = _matmul(conv_y, wco, mode="nn", out_dtype=F32, name="conv_out_fwd", **big)
    attn = _attn_fwd(proj, tab, sinks, "attn_fwd")
    attn_out = _matmul(attn, wao, mode="nn", out_dtype=F32, name="attn_out_fwd", **big)
    merged = _merge_fwd(proj, conv_out, attn_out, "merge_fwd")
    x1 = _matmul(merged, wo, mode="nn", out_dtype=F32, name="wo_fwd", res=x, **big)
    h2 = _rms_fwd(x1, g_ffn, "rms2_fwd")
    gu = _matmul(h2, wgu_t, mode="nt", out_dtype=BF16, name="gate_up_fwd", **big)
    act = _swiglu_fwd(gu, "swiglu_fwd")
    x2 = _matmul(act, wd, mode="nn", out_dtype=F32, name="down_fwd", res=x1, tm=1024, tn=512, tk=D_FF)
    dx2, dx2b, dg_final, lossvec = _loss_head(x2, g_final, tgt, "loss_head")
    dact = _matmul(dx2b, wd, mode="nt", out_dtype=BF16, name="down_bwd_x", tm=1024, tn=1408, tk=1024)
    g_wd = _matmul(act, dx2b, mode="tn", out_dtype=BF16, name="down_bwd_w", tm=1408, tn=1024, tk=1024)
    dgu = _swiglu_bwd(dact, gu, "swiglu_bwd")
    dh2 = _matmul(dgu, wgu_t, mode="nn", out_dtype=F32, name="gate_up_bwd_x", tm=1024, tn=1024, tk=512)
    g_wgu_t = _matmul(dgu, h2, mode="tn", out_dtype=BF16, name="gate_up_bwd_w", tm=512, tn=1024, tk=1024)
    dx1, dx1b, dg_ffn = _rms_bwd(dh2, x1, g_ffn, dx2, "rms2_bwd")
    dmerged = _matmul(dx1b, wo, mode="nt", out_dtype=F32, name="wo_bwd_x", **big)
    g_wo = _matmul(merged, dx1b, mode="tn", out_dtype=BF16, name="wo_bwd_w", tm=512, tn=1024, tk=1024)
    dproj, dco, dao = _merge_bwd(dmerged, proj, conv_out, attn_out, "merge_bwd")
    dconv_y = _matmul(dco, wco, mode="nt", out_dtype=F32, name="conv_out_bwd_x", **big)
    g_wco = _matmul(conv_y, dco, mode="tn", out_dtype=BF16, name="conv_out_bwd_w", tm=512, tn=1024, tk=1024)
    dattn = _matmul(dao, wao, mode="nt", out_dtype=BF16, name="attn_out_bwd_x", **big)
    g_wao = _matmul(attn, dao, mode="tn", out_dtype=BF16, name="attn_out_bwd_w", tm=512, tn=1024, tk=1024)
    dproj, dconv_w = _conv_bwd(dconv_y, proj, conv_w, dproj, "conv_bwd")
    dproj, dkr, dv, dsinks = _attn_bwd(dattn, proj, tab, sinks, dproj, "attn_bwd")
    dproj = _kv_bwd(dkr, dv, tab, dproj, "kv_bwd")
    dh1 = _matmul(dproj, win_t, mode="nn", out_dtype=F32, name="proj_bwd_x", tm=1024, tn=1024, tk=512)
    g_win_t = _matmul(dproj, h1, mode="tn", out_dtype=BF16, name="proj_bwd_w", tm=512, tn=1024, tk=1024)
    dx, _, dg_mix = _rms_bwd(dh1, x, g_mix, dx1, "rms1_bwd")
    grads = dict(win_t=g_win_t, wgu_t=g_wgu_t, wd=g_wd, wco=g_wco, wao=g_wao, wo=g_wo)
    small = dict(g_mix=dg_mix, g_ffn=dg_ffn, g_final=dg_final, conv_w=dconv_w, sinks=dsinks, lossvec=lossvec)
    return dx, grads, small


def _position():
    return lax.axis_index("x"), lax.axis_index("y"), lax.axis_index("c")


def _other_chips(x, y):
    return [(1 - x, y), (x, 1 - y), (1 - x, 1 - y)]


HBM_SPEC = pl.BlockSpec(memory_space=pl.ANY)


def _all_gather(shards, name):
    n = len(shards)
    rows = [sh.shape[0] for sh in shards]

    def body(*refs):
        ins, outs = refs[:n], refs[n:2 * n]
        send_sems, recv_sems, local_sems = refs[2 * n:]
        x, y, c = _position()
        me, sibling = (x, y, c), (x, y, 1 - c)
        chips = _other_chips(x, y)

        def blk(a, px, py, pc):
            return outs[a].at[pl.ds((4 * px + 2 * py + pc) * rows[a], rows[a]), :]

        def copy(a, k, block, to, src=None):
            return pltpu.make_async_remote_copy(
                src_ref=blk(a, *block) if src is None else src, dst_ref=blk(a, *block),
                send_sem=send_sems.at[a, k], recv_sem=recv_sems.at[a, k], device_id=to, device_id_type=MESH)

        mine = [pltpu.make_async_copy(ins[a], blk(a, *me), local_sems.at[a]) for a in range(n)]
        for cp in mine:
            cp.start()
        first = []
        for a in range(n):
            first.append(copy(a, 0, me, sibling, src=ins[a]))
            first += [copy(a, 1 + j, me, (*chip, c), src=ins[a]) for j, chip in enumerate(chips)]
        for cp in first:
            cp.start()
        passed = []
        for j, chip in enumerate(chips):
            for a in range(n):
                copy(a, 1 + j, (*chip, c), me).wait_recv()
                fwd = copy(a, 4 + j, (*chip, c), sibling)
                fwd.start()
                passed.append(fwd)
        for a in range(n):
            copy(a, 0, sibling, me).wait_recv()
            for j, chip in enumerate(chips):
                copy(a, 4 + j, (*chip, 1 - c), me).wait_recv()
        for cp in first + passed:
            cp.wait_send()
        for cp in mine:
            cp.wait()

    return _call(
        body, name=name, in_specs=[HBM_SPEC] * n, out_specs=[HBM_SPEC] * n,
        out_shape=[_sds((N_DEV * sh.shape[0],) + sh.shape[1:], sh.dtype) for sh in shards],
        scratch_shapes=[pltpu.SemaphoreType.DMA((n, 7)), pltpu.SemaphoreType.DMA((n, 7)), pltpu.SemaphoreType.DMA((n,))],
    )(*shards)


def _whole(ref, nrows):
    return ref.at[pl.ds(0, nrows), :]


def _rs_sibling(grads, name):
    n = len(grads)
    rows = [g.shape[0] // N_DEV for g in grads]

    def body(*refs):
        ins, outs = refs[:n], refs[n:2 * n]
        send_sems, recv_sems = refs[2 * n:]
        x, y, c = _position()
        sibling = (x, y, 1 - c)
        for a in range(n):
            r = rows[a]
            for q in range(4):
                src = ins[a].at[pl.ds((2 * q + (1 - c)) * r, r), :]
                dst = outs[a].at[pl.ds(q * r, r), :]
                pltpu.make_async_remote_copy(src_ref=src, dst_ref=dst, send_sem=send_sems.at[a], recv_sem=recv_sems.at[a],
                                             device_id=sibling, device_id_type=MESH).start()
        for a in range(n):
            allrows = 4 * rows[a]
            pltpu.make_async_remote_copy(
                src_ref=_whole(ins[a], allrows), dst_ref=_whole(outs[a], allrows), send_sem=send_sems.at[a],
                recv_sem=recv_sems.at[a], device_id=sibling, device_id_type=MESH).wait()

    return _call(
        body, name=name, in_specs=[HBM_SPEC] * n, out_specs=[HBM_SPEC] * n,
        out_shape=[_sds((4 * r, g.shape[1]), g.dtype) for g, r in zip(grads, rows)],
        scratch_shapes=[pltpu.SemaphoreType.DMA((n,)), pltpu.SemaphoreType.DMA((n,))],
    )(*grads)


def _rs_chips(parts, name):
    n = len(parts)
    rows = [p.shape[0] // 4 for p in parts]

    def body(*refs):
        ins, outs = refs[:n], refs[n:2 * n]
        send_sems, recv_sems = refs[2 * n:]
        x, y, c = _position()
        chips = _other_chips(x, y)
        for a in range(n):
            r = rows[a]
            for j, (px, py) in enumerate(chips):
                src = ins[a].at[pl.ds((2 * px + py) * r, r), :]
                dst = outs[a].at[pl.ds(j * r, r), :]
                pltpu.make_async_remote_copy(src_ref=src, dst_ref=dst, send_sem=send_sems.at[a], recv_sem=recv_sems.at[a],
                                             device_id=(px, py, c), device_id_type=MESH).start()
        for a in range(n):
            allrows = 3 * rows[a]
            pltpu.make_async_remote_copy(
                src_ref=_whole(ins[a], allrows), dst_ref=_whole(outs[a], allrows), send_sem=send_sems.at[a],
                recv_sem=recv_sems.at[a], device_id=(x, y, c), device_id_type=MESH).wait()

    return _call(
        body, name=name, in_specs=[HBM_SPEC] * n, out_specs=[HBM_SPEC] * n,
        out_shape=[_sds((3 * r, p.shape[1]), p.dtype) for p, r in zip(parts, rows)],
        scratch_shapes=[pltpu.SemaphoreType.DMA((n,)), pltpu.SemaphoreType.DMA((n,))],
    )(*parts)


def _chip_partial(grad, recv, c_idx, name):
    r = recv.shape[0] // 4

    def body(c_ref, g_ref, s_ref, o_ref):
        del c_ref
        o_ref[...] = (g_ref[...].astype(F32) + s_ref[...].astype(F32)).astype(BF16)

    grid_spec = pltpu.PrefetchScalarGridSpec(
        num_scalar_prefetch=1, grid=(4,),
        in_specs=[pl.BlockSpec((r, D), lambda q, c_ref: (2 * q + c_ref[0], 0)), pl.BlockSpec((r, D), lambda q, c_ref: (q, 0))],
        out_specs=pl.BlockSpec((r, D), lambda q, c_ref: (q, 0)))
    return _call(body, name=name, grid_spec=grid_spec, out_shape=_sds((4 * r, D), BF16),
                 compiler_params=_params("parallel"))(c_idx, grad, recv)


def _final_grad(part, recv, q_idx, name):
    r = part.shape[0] // 4
    tr = r // 2 if r % 32 == 0 else r

    def body(q_ref, p_ref, r0_ref, r1_ref, r2_ref, o_ref):
        del q_ref
        o_ref[...] = ((p_ref[...].astype(F32) + r0_ref[...].astype(F32)) + r1_ref[...].astype(F32)) + r2_ref[...].astype(F32)

    nb = r // tr
    grid_spec = pltpu.PrefetchScalarGridSpec(
        num_scalar_prefetch=1, grid=(nb,),
        in_specs=[pl.BlockSpec((tr, D), lambda i, q_ref: (q_ref[0] * nb + i, 0))]
        + [pl.BlockSpec((tr, D), lambda i, q_ref, j=j: (j * nb + i, 0)) for j in range(3)],
        out_specs=pl.BlockSpec((tr, D), lambda i, q_ref: (i, 0)))
    return _call(body, name=name, grid_spec=grid_spec, out_shape=_sds((r, D), F32),
                 compiler_params=_params("parallel"))(q_idx, part, recv, recv, recv)


SMALL_ROWS = 8


def _small_all_reduce(pack, name):
    def body(p_ref, tot_ref, loss_ref, gath, send_sems, recv_sems):
        x, y, c = _position()
        me_id = 4 * x + 2 * y + c
        gath[me_id] = p_ref[...]
        copies = []
        for k in range(1, N_DEV):
            peer = tuple(1 - v if (k >> b) & 1 else v for v, b in ((x, 2), (y, 1), (c, 0)))
            cp = pltpu.make_async_remote_copy(src_ref=p_ref, dst_ref=gath.at[me_id], send_sem=send_sems.at[k - 1],
                                              recv_sem=recv_sems.at[k - 1], device_id=peer, device_id_type=MESH)
            cp.start()
            copies.append(cp)
        for cp in copies:
            cp.wait_recv()
        for cp in copies:
            cp.wait_send()
        tot = gath[0]
        for d in range(1, N_DEV):
            tot = tot + gath[d]
        tot_ref[...] = tot
        loss_ref[...] = jnp.full((1, 128), (0.5 / D) * jnp.sum(tot[SMALL_ROWS - 1:SMALL_ROWS, :]), F32)

    vm = pl.BlockSpec(memory_space=pltpu.VMEM)
    return _call(
        body, name=name, in_specs=[vm], out_specs=[vm, vm],
        out_shape=[_sds((SMALL_ROWS, D), F32), _sds((1, 128), F32)],
        scratch_shapes=[pltpu.VMEM((N_DEV, SMALL_ROWS, D), F32), pltpu.SemaphoreType.DMA((N_DEV - 1,)),
                        pltpu.SemaphoreType.DMA((N_DEV - 1,))],
    )(pack)


def _adamw(w, g, m, v, name):
    r, cdim = w.shape
    tr = 256 if r % 256 == 0 else (r // 2 if r % 16 == 0 else r)

    def body(w_ref, g_ref, m_ref, v_ref, d_ref, nm_ref, nv_ref):
        gv = g_ref[...]
        m2 = B1 * m_ref[...] + (1.0 - B1) * gv
        v2 = B2 * v_ref[...] + (1.0 - B2) * jnp.square(gv)
        m_hat = m2 / (1.0 - B1 ** STEP)
        v_hat = v2 / (1.0 - B2 ** STEP)
        d_ref[...] = -LR * (m_hat / (jnp.sqrt(v_hat) + EPS_ADAM) + WD * w_ref[...])
        nm_ref[...] = m2
        nv_ref[...] = v2

    spec = pl.BlockSpec((tr, cdim), lambda i: (i, 0))
    return _call(
        body, name=name, grid=(r // tr,), in_specs=[spec] * 4, out_specs=[spec] * 3,
        out_shape=[_sds((r, cdim), F32)] * 3, compiler_params=_params("parallel"),
    )(w, g, m, v)


def _cast_bf16(w, name):
    r, cdim = w.shape
    tr = 256 if r % 256 == 0 else r

    def body(w_ref, o_ref):
        o_ref[...] = w_ref[...].astype(BF16)

    spec = pl.BlockSpec((tr, cdim), lambda i: (i, 0))
    return _call(body, name=name, grid=(r // tr,), in_specs=[spec], out_specs=spec, out_shape=_sds((r, cdim), BF16),
                 compiler_params=_params("parallel"))(w)


BIG = ("win_t", "wgu_t", "wd", "wco", "wao", "wo")


def kernel(x, g_mix, w_in, conv_w, attn_sinks, w_conv_out, w_attn_out, w_o, g_ffn, w_gate_up, w_down, g_final, loss_target, m_g_mix, m_w_in, m_conv_w, m_attn_sinks, m_w_conv_out, m_w_attn_out, m_w_o, m_g_ffn, m_w_gate_up, m_w_down, m_g_final, v_g_mix, v_w_in, v_conv_w, v_attn_sinks, v_w_conv_out, v_w_attn_out, v_w_o, v_g_ffn, v_w_gate_up, v_w_down, v_g_final):
    cx, cy, cc = _position()
    c_idx = jnp.reshape(cc, (1,)).astype(jnp.int32)
    q_idx = jnp.reshape(2 * cx + cy, (1,)).astype(jnp.int32)
    me = 4 * cx + 2 * cy + cc

    shards = [
        _cast_bf16(jnp.transpose(w_in[0]), "cast_w_in"), _cast_bf16(jnp.transpose(w_gate_up[0]), "cast_w_gate_up"),
        _cast_bf16(w_down[0], "cast_w_down"), _cast_bf16(w_conv_out[0], "cast_w_conv_out"),
        _cast_bf16(w_attn_out[0], "cast_w_attn_out"), _cast_bf16(w_o[0], "cast_w_o"),
        jnp.pad(conv_w[0], ((0, 5), (0, 0))),
    ]
    full = _all_gather(shards, "all_gather_weights")
    weights = dict(zip(BIG, full[:6]))
    conv_w_full = jnp.transpose(full[6].reshape(N_DEV, 8, 128)[:, :3, :], (1, 0, 2)).reshape(3, D)

    dx, grads, small = _local_step(x[0], loss_target[0], g_mix, g_ffn, g_final[None], attn_sinks, conv_w_full, **weights)

    glist = [grads[k] for k in BIG]
    from_sibling = _rs_sibling(glist, "reduce_scatter_sibling")
    parts = [_chip_partial(g, r, c_idx, "chip_partial_" + k) for k, g, r in zip(BIG, glist, from_sibling)]
    from_chips = _rs_chips(parts, "reduce_scatter_chips")
    gfull = {k: _final_grad(p, r, q_idx, "final_grad_" + k) for k, p, r in zip(BIG, parts, from_chips)}

    sinks_row = jnp.pad(small["sinks"], ((0, 0), (0, D - 128)))
    pack = jnp.concatenate([small["g_mix"], small["g_ffn"], small["g_final"], small["conv_w"], sinks_row, small["lossvec"]], axis=0)
    tot, loss_row = _small_all_reduce(pack, "small_all_reduce")
    loss = loss_row[0, 0]
    g_small = {
        "g_mix": tot[0:1], "g_ffn": tot[1:2], "g_final": tot[2:3],
        "conv_w": lax.dynamic_slice(tot, (3, me * 128), (3, 128)), "attn_sinks": tot[6:7, :N_HEADS],
    }

    g_big = {
        "w_in": jnp.transpose(gfull["win_t"]), "w_gate_up": jnp.transpose(gfull["wgu_t"]), "w_down": gfull["wd"],
        "w_conv_out": gfull["wco"], "w_attn_out": gfull["wao"], "w_o": gfull["wo"],
    }
    w2d = {"g_mix": g_mix, "w_in": w_in[0], "conv_w": conv_w[0], "attn_sinks": attn_sinks, "w_conv_out": w_conv_out[0],
           "w_attn_out": w_attn_out[0], "w_o": w_o[0], "g_ffn": g_ffn, "w_gate_up": w_gate_up[0], "w_down": w_down[0],
           "g_final": g_final[None]}
    m2d = {"g_mix": m_g_mix, "w_in": m_w_in[0], "conv_w": m_conv_w[0], "attn_sinks": m_attn_sinks, "w_conv_out": m_w_conv_out[0],
           "w_attn_out": m_w_attn_out[0], "w_o": m_w_o[0], "g_ffn": m_g_ffn, "w_gate_up": m_w_gate_up[0], "w_down": m_w_down[0],
           "g_final": m_g_final[None]}
    v2d = {"g_mix": v_g_mix, "w_in": v_w_in[0], "conv_w": v_conv_w[0], "attn_sinks": v_attn_sinks, "w_conv_out": v_w_conv_out[0],
           "w_attn_out": v_w_attn_out[0], "w_o": v_w_o[0], "g_ffn": v_g_ffn, "w_gate_up": v_w_gate_up[0], "w_down": v_w_down[0],
           "g_final": v_g_final[None]}
    order = ["g_mix", "w_in", "conv_w", "attn_sinks", "w_conv_out", "w_attn_out", "w_o", "g_ffn", "w_gate_up", "w_down", "g_final"]
    shapes = {"g_mix": (1, D), "conv_w": (1, 3, 128), "attn_sinks": (1, N_HEADS), "g_ffn": (1, D), "g_final": (D,)}
    out_g, out_d, out_m, out_v = [], [], [], []
    for k in order:
        g = g_big[k] if k in g_big else g_small[k]
        d, nm, nv = _adamw(w2d[k], g, m2d[k], v2d[k], "adamw_" + k)
        shape = shapes.get(k, (1,) + g.shape)
        for lst, val in ((out_g, g), (out_d, d), (out_m, nm), (out_v, nv)):
            lst.append(val.reshape(shape))
    return (loss, dx[None], *out_g, *out_d, *out_m, *out_v)
```

```python
import functools
import math

import jax
import jax.numpy as jnp
from jax import lax
from jax.experimental import pallas as pl
from jax.experimental.pallas import tpu as pltpu

F32 = jnp.float32
BF16 = jnp.bfloat16

D = 1024
HEAD_DIM = 64
N_HEADS = 16
N_KV = 4
GROUP = N_HEADS // N_KV
D_KV = N_KV * HEAD_DIM
BLOCK = 128
ROT_DIM = HEAD_DIM // 4
ROPE_THETA = 500000.0
ATTN_SCALE = 1.0 / math.sqrt(HEAD_DIM)
NEG_INF = -1e30
D_FF = 2816
N_IN = 6656
EPS = 1e-5
C_CB, C_CC, C_CX, C_Q, C_K, C_V, C_GC, C_GA = 0, 1024, 2048, 3072, 4096, 4352, 4608, 5632

LR, B1, B2, EPS_ADAM, WD, STEP = 0.001, 0.9, 0.999, 1e-08, 0.01, 10

N_DEV = 8
MESH = pl.DeviceIdType.MESH
VMEM_LIMIT = 56 * 1024 * 1024

NN = (((1,), (0,)), ((), ()))
NT = (((1,), (1,)), ((), ()))
TN = (((0,), (0,)), ((), ()))
HBM_SPEC = pl.BlockSpec(memory_space=pl.ANY)


def _call(body, **kw):
    return pl.pallas_call(body, **kw)


def _params(*sem):
    return pltpu.CompilerParams(dimension_semantics=sem, vmem_limit_bytes=VMEM_LIMIT)


def _sds(shape, dtype):
    return jax.ShapeDtypeStruct(shape, dtype)


def _matmul(a, b, *, mode, tm, tn, tk, out_dtype, name, res=None, after=()):
    parts = list(a) if isinstance(a, (list, tuple)) else [a]
    rows_a = parts[0].shape[0]
    cols_a = sum(p.shape[1] for p in parts)
    if mode == "nn":
        (m, kk), (_, n), dims = (rows_a, cols_a), b.shape, NN
    elif mode == "nt":
        (m, kk), (n, _), dims = (rows_a, cols_a), b.shape, NT
    else:
        (kk, m), (_, n), dims = (rows_a, cols_a), b.shape, TN
    tm, tn, tk = min(tm, m), min(tn, n), min(tk, kk)
    assert m % tm == 0 and n % tn == 0 and kk % tk == 0, (name, m, n, kk, tm, tn, tk)
    nk = kk // tk
    split_axis, width = (2, tk) if mode == "nn" else (0, tm)
    assert len(parts) == 1 or mode in ("nn", "tn")
    assert all(p.shape[1] % width == 0 for p in parts), (name, width)
    counts = [p.shape[1] // width for p in parts]
    starts = [sum(counts[:p]) for p in range(len(parts))]

    def a_spec(p):
        def col(t):
            return jnp.clip(t - starts[p], 0, counts[p] - 1) if len(parts) > 1 else t

        if mode == "tn":
            return pl.BlockSpec((tk, tm), lambda i, j, k: (k, col(i)))
        return pl.BlockSpec((tm, tk), lambda i, j, k: (i, col(k)))

    if mode == "nt":
        b_spec = pl.BlockSpec((tn, tk), lambda i, j, k: (j, k))
    else:
        b_spec = pl.BlockSpec((tk, tn), lambda i, j, k: (k, j))
    o_spec = pl.BlockSpec((tm, tn), lambda i, j, k: (i, j))
    has_res = res is not None
    n_parts = len(parts)

    def body(*refs):
        a_refs, b_ref = refs[:n_parts], refs[n_parts]
        r_ref = refs[n_parts + 1] if has_res else None
        o_ref = refs[n_parts + 1 + has_res + len(after)]
        k = pl.program_id(2)

        def step(a_ref):
            part = lax.dot_general(a_ref[...], b_ref[...], dims, preferred_element_type=F32)

            def finish(acc):
                if has_res:
                    acc = acc + r_ref[...]
                o_ref[...] = acc.astype(o_ref.dtype)

            if nk == 1:
                finish(part)
            else:
                acc_ref = refs[-1]

                @pl.when(k == 0)
                def _():
                    acc_ref[...] = part

                @pl.when(k > 0)
                def _():
                    acc_ref[...] += part

                @pl.when(k == nk - 1)
                def _():
                    finish(acc_ref[...])

        if n_parts == 1:
            step(a_refs[0])
        else:
            t = pl.program_id(split_axis)
            for p in range(n_parts):
                pl.when((t >= starts[p]) & (t < starts[p] + counts[p]))(functools.partial(step, a_refs[p]))

    ins = parts + [b] + ([res] if has_res else []) + list(after)
    in_specs = [a_spec(p) for p in range(n_parts)] + [b_spec] + ([o_spec] if has_res else []) + [HBM_SPEC] * len(after)
    scratch = [] if nk == 1 else [pltpu.VMEM((tm, tn), F32)]
    return _call(
        body, name=name, grid=(m // tm, n // tn, nk), in_specs=in_specs, out_specs=o_spec,
        out_shape=_sds((m, n), out_dtype), scratch_shapes=scratch,
        compiler_params=_params("parallel", "parallel", "arbitrary"),
    )(*ins)


def _row_tile(s):
    return min(256, s)


def _rms_fwd(x, g, name):
    s = x.shape[0]
    tm = _row_tile(s)

    def body(x_ref, g_ref, h_ref):
        xv = x_ref[...]
        r = lax.rsqrt(jnp.mean(xv * xv, axis=-1, keepdims=True) + EPS)
        h_ref[...] = (xv * r * g_ref[...]).astype(BF16)

    row = pl.BlockSpec((tm, D), lambda i: (i, 0))
    return _call(
        body, name=name, grid=(s // tm,), in_specs=[row, pl.BlockSpec((1, D), lambda i: (0, 0))],
        out_specs=row, out_shape=_sds((s, D), BF16), compiler_params=_params("parallel"),
    )(x, g)


def _rms_bwd(dh, x, g, dres, name):
    s = x.shape[0]
    tm = _row_tile(s)

    def body(dh_ref, x_ref, g_ref, dres_ref, dx_ref, dxb_ref, dg_ref):
        xv = x_ref[...]
        r = lax.rsqrt(jnp.mean(xv * xv, axis=-1, keepdims=True) + EPS)
        xh = xv * r
        dhv = dh_ref[...]
        dyg = dhv * g_ref[...]
        dx = dres_ref[...] + r * (dyg - xh * jnp.mean(dyg * xh, axis=-1, keepdims=True))
        dx_ref[...] = dx
        dxb_ref[...] = dx.astype(BF16)
        part = jnp.sum(dhv * xh, axis=0, keepdims=True)

        @pl.when(pl.program_id(0) == 0)
        def _():
            dg_ref[...] = part

        @pl.when(pl.program_id(0) > 0)
        def _():
            dg_ref[...] += part

    row = pl.BlockSpec((tm, D), lambda i: (i, 0))
    vec = pl.BlockSpec((1, D), lambda i: (0, 0))
    return _call(
        body, name=name, grid=(s // tm,), in_specs=[row, row, vec, row], out_specs=[row, row, vec],
        out_shape=[_sds((s, D), F32), _sds((s, D), BF16), _sds((1, D), F32)],
        compiler_params=_params("arbitrary"),
    )(dh, x, g, dres)


def _loss_head(x2, g, tgt, name):
    s = x2.shape[0]
    tm = _row_tile(s)

    def body(x_ref, g_ref, t_ref, dx_ref, dxb_ref, dg_ref, l_ref):
        xv = x_ref[...]
        gv = g_ref[...]
        r = lax.rsqrt(jnp.mean(xv * xv, axis=-1, keepdims=True) + EPS)
        xh = xv * r
        err = xh * gv - t_ref[...]
        dy = err * (1.0 / D)
        dyg = dy * gv
        dx = r * (dyg - xh * jnp.mean(dyg * xh, axis=-1, keepdims=True))
        dx_ref[...] = dx
        dxb_ref[...] = dx.astype(BF16)
        dg_part = jnp.sum(dy * xh, axis=0, keepdims=True)
        l_part = jnp.sum(err * err, axis=0, keepdims=True)

        @pl.when(pl.program_id(0) == 0)
        def _():
            dg_ref[...] = dg_part
            l_ref[...] = l_part

        @pl.when(pl.program_id(0) > 0)
        def _():
            dg_ref[...] += dg_part
            l_ref[...] += l_part

    row = pl.BlockSpec((tm, D), lambda i: (i, 0))
    vec = pl.BlockSpec((1, D), lambda i: (0, 0))
    return _call(
        body, name=name, grid=(s // tm,), in_specs=[row, vec, row], out_specs=[row, row, vec, vec],
        out_shape=[_sds((s, D), F32), _sds((s, D), BF16), _sds((1, D), F32), _sds((1, D), F32)],
        compiler_params=_params("arbitrary"),
    )(x2, g, tgt)


CONV_TC = 256


def _shift_down(u, k, rows):
    return jnp.where(rows >= k, pltpu.roll(u, k, 0), 0.0)


def _shift_up(u, k, rows, s):
    return jnp.where(rows < s - k, pltpu.roll(u, s - k, 0), 0.0)


def _conv_specs(s):
    nb = D // CONV_TC

    def col(c0):
        return pl.BlockSpec((s, CONV_TC), lambda j, c0=c0: (0, c0 // CONV_TC + j))

    return nb, col


def _conv_fwd(proj, conv_w, name):
    s = proj.shape[0]
    nb, col = _conv_specs(s)

    def body(cb_ref, cc_ref, cx_ref, w_ref, y_ref):
        rows = lax.broadcasted_iota(jnp.int32, (s, CONV_TC), 0)
        u = cc_ref[...].astype(F32) * cx_ref[...].astype(F32)
        w = w_ref[...]
        c = w[0:1] * _shift_down(u, 2, rows) + w[1:2] * _shift_down(u, 1, rows) + w[2:3] * u
        y_ref[...] = (cb_ref[...].astype(F32) * c).astype(BF16)

    return _call(
        body, name=name, grid=(nb,),
        in_specs=[col(C_CB), col(C_CC), col(C_CX), pl.BlockSpec((3, CONV_TC), lambda j: (0, j))],
        out_specs=pl.BlockSpec((s, CONV_TC), lambda j: (0, j)), out_shape=_sds((s, D), BF16),
        compiler_params=_params("parallel"),
    )(proj, proj, proj, conv_w)


def _conv_bwd(dy, proj, conv_w, dproj, name):
    s = proj.shape[0]
    nb, col = _conv_specs(s)

    def body(dy_ref, cb_ref, cc_ref, cx_ref, w_ref, dproj_in, dproj_ref, dw_ref, buf, sems):
        del dproj_in
        j = pl.program_id(0)
        rows = lax.broadcasted_iota(jnp.int32, (s, CONV_TC), 0)
        cc = cc_ref[...].astype(F32)
        cx = cx_ref[...].astype(F32)
        u = cc * cx
        u1 = _shift_down(u, 1, rows)
        u2 = _shift_down(u, 2, rows)
        w = w_ref[...]
        c = w[0:1] * u2 + w[1:2] * u1 + w[2:3] * u
        dyv = dy_ref[...]
        dc = dyv * cb_ref[...].astype(F32)
        du = w[2:3] * dc + w[1:2] * _shift_up(dc, 1, rows, s) + w[0:1] * _shift_up(dc, 2, rows, s)
        buf[0] = (dyv * c).astype(BF16)
        buf[1] = (du * cx).astype(BF16)
        buf[2] = (du * cc).astype(BF16)
        dw_ref[...] = jnp.concatenate(
            [jnp.sum(dc * u2, axis=0, keepdims=True), jnp.sum(dc * u1, axis=0, keepdims=True),
             jnp.sum(dc * u, axis=0, keepdims=True)], axis=0)
        copies = []
        for p, c0 in enumerate((C_CB, C_CC, C_CX)):
            start = pl.multiple_of(c0 + j * CONV_TC, CONV_TC)
            copies.append(pltpu.make_async_copy(buf.at[p], dproj_ref.at[:, pl.ds(start, CONV_TC)], sems.at[p]))
        for cp in copies:
            cp.start()
        for cp in copies:
            cp.wait()

    return _call(
        body, name=name, grid=(nb,),
        in_specs=[pl.BlockSpec((s, CONV_TC), lambda j: (0, j)), col(C_CB), col(C_CC), col(C_CX),
                  pl.BlockSpec((3, CONV_TC), lambda j: (0, j)), pl.BlockSpec(memory_space=pl.ANY)],
        out_specs=[pl.BlockSpec(memory_space=pl.ANY), pl.BlockSpec((3, CONV_TC), lambda j: (0, j))],
        out_shape=[_sds((s, N_IN), BF16), _sds((3, D), F32)],
        scratch_shapes=[pltpu.VMEM((3, s, CONV_TC), BF16), pltpu.SemaphoreType.DMA((3,))],
        input_output_aliases={5: 0}, compiler_params=_params("arbitrary"),
    )(dy, proj, proj, proj, conv_w, dproj)


def _rope_tables(s):
    inv_freq = ROPE_THETA ** (-jnp.arange(0, ROT_DIM, 2, dtype=F32) / ROT_DIM)
    ang = jnp.arange(s, dtype=F32)[:, None] * inv_freq[None, :]
    cos, sin = jnp.cos(ang), jnp.sin(ang)
    half = ROT_DIM // 2
    ones = jnp.ones((s, HEAD_DIM - ROT_DIM), F32)
    zeros = jnp.zeros((s, HEAD_DIM - ROT_DIM), F32)
    zh = jnp.zeros((s, half), F32)
    c64 = jnp.concatenate([cos, cos, ones], axis=1)
    a64 = jnp.concatenate([-sin, zh, zeros], axis=1)
    b64 = jnp.concatenate([zh, sin, zeros], axis=1)
    return jnp.concatenate([c64, c64, a64, a64, b64, b64], axis=1)


def _rope(x, tab):
    c, a, b = tab[:, 0:128], tab[:, 128:256], tab[:, 256:384]
    outs = []
    for i in range(x.shape[1] // 128):
        xc = x[:, i * 128:(i + 1) * 128]
        outs.append(xc * c + pltpu.roll(xc, 120, 1) * a + pltpu.roll(xc, 8, 1) * b)
    return outs[0] if len(outs) == 1 else jnp.concatenate(outs, axis=1)


def _rope_t(dx, tab):
    c, a, b = tab[:, 0:128], tab[:, 128:256], tab[:, 256:384]
    outs = []
    for i in range(dx.shape[1] // 128):
        dc = dx[:, i * 128:(i + 1) * 128]
        outs.append(dc * c + pltpu.roll(dc * a, 8, 1) + pltpu.roll(dc * b, 120, 1))
    return outs[0] if len(outs) == 1 else jnp.concatenate(outs, axis=1)


def _attn_mask(n):
    qi = lax.broadcasted_iota(jnp.int32, (GROUP * BLOCK, 2 * BLOCK), 0) & (BLOCK - 1)
    kj = lax.broadcasted_iota(jnp.int32, (GROUP * BLOCK, 2 * BLOCK), 1)
    rel = qi + BLOCK - kj
    return (rel >= 0) & (rel < BLOCK) & ((kj >= BLOCK) | (n > 0))


def _sink_col(sink_ref, hk):
    return jnp.concatenate([jnp.full((BLOCK, 1), sink_ref[0, hk * GROUP + g], F32) for g in range(GROUP)], axis=0)


def _attn_in_specs():
    prev = lambda n: jnp.maximum(n - 1, 0)
    return [
        pl.BlockSpec((BLOCK, D), lambda n: (n, C_Q // D)),
        pl.BlockSpec((BLOCK, D_KV), lambda n: (n, C_K // D_KV)),
        pl.BlockSpec((BLOCK, D_KV), lambda n: (prev(n), C_K // D_KV)),
        pl.BlockSpec((BLOCK, D_KV), lambda n: (n, C_V // D_KV)),
        pl.BlockSpec((BLOCK, D_KV), lambda n: (prev(n), C_V // D_KV)),
        pl.BlockSpec((BLOCK, 384), lambda n: (n, 0)),
        pl.BlockSpec((BLOCK, 384), lambda n: (prev(n), 0)),
        pl.BlockSpec(memory_space=pltpu.SMEM),
    ]


def _load_qkv(q_ref, kc_ref, kp_ref, vc_ref, vp_ref, tc_ref, tp_ref):
    q = _rope(q_ref[...].astype(F32), tc_ref[...]).astype(BF16)
    kc = _rope(kc_ref[...].astype(F32), tc_ref[...]).astype(BF16)
    kp = _rope(kp_ref[...].astype(F32), tp_ref[...]).astype(BF16)
    return q, kc, kp, vc_ref[...], vp_ref[...]


def _group_rows(x, hk):
    base = hk * GROUP * HEAD_DIM
    return jnp.concatenate([x[:, base + g * HEAD_DIM: base + (g + 1) * HEAD_DIM] for g in range(GROUP)], axis=0)


def _kv_rows(prev, cur, hk):
    sl = slice(hk * HEAD_DIM, (hk + 1) * HEAD_DIM)
    return jnp.concatenate([prev[:, sl], cur[:, sl]], axis=0)


def _attn_fwd(proj, tab, sinks, name):
    s = proj.shape[0]

    def body(q_ref, kc_ref, kp_ref, vc_ref, vp_ref, tc_ref, tp_ref, sink_ref, o_ref):
        n = pl.program_id(0)
        q, kc, kp, vc, vp = _load_qkv(q_ref, kc_ref, kp_ref, vc_ref, vp_ref, tc_ref, tp_ref)
        mask = _attn_mask(n)
        for hk in range(N_KV):
            qg = _group_rows(q, hk)
            kcat = _kv_rows(kp, kc, hk)
            vcat = _kv_rows(vp, vc, hk)
            sc = lax.dot_general(qg, kcat, NT, preferred_element_type=F32) * ATTN_SCALE
            sc = jnp.where(mask, sc, NEG_INF)
            sink = _sink_col(sink_ref, hk)
            m = jnp.maximum(jnp.max(sc, axis=1, keepdims=True), sink)
            p = jnp.exp(sc - m)
            inv = 1.0 / (jnp.sum(p, axis=1, keepdims=True) + jnp.exp(sink - m))
            o = lax.dot_general((p * inv).astype(BF16), vcat, NN, preferred_element_type=F32)
            base = hk * GROUP * HEAD_DIM
            for g in range(GROUP):
                o_ref[:, base + g * HEAD_DIM: base + (g + 1) * HEAD_DIM] = o[g * BLOCK:(g + 1) * BLOCK].astype(BF16)

    return _call(
        body, name=name, grid=(s // BLOCK,), in_specs=_attn_in_specs(),
        out_specs=pl.BlockSpec((BLOCK, D), lambda n: (n, 0)), out_shape=_sds((s, D), BF16),
        compiler_params=_params("parallel"),
    )(proj, proj, proj, proj, proj, tab, tab, sinks)


def _attn_bwd(do, proj, tab, sinks, dproj, name):
    s = proj.shape[0]
    nblk = s // BLOCK

    def body(do_ref, q_ref, kc_ref, kp_ref, vc_ref, vp_ref, tc_ref, tp_ref, sink_ref, dproj_in, dproj_ref,
             dk_ref, dv_ref, ds_ref, dqbuf, dqout, dkbuf, dvbuf, sem):
        del dproj_in
        n = pl.program_id(0)

        @pl.when(n == 0)
        def _():
            dk_ref[...] = jnp.zeros_like(dk_ref)
            dv_ref[...] = jnp.zeros_like(dv_ref)
            ds_ref[...] = jnp.zeros_like(ds_ref)

        q, kc, kp, vc, vp = _load_qkv(q_ref, kc_ref, kp_ref, vc_ref, vp_ref, tc_ref, tp_ref)
        dov = do_ref[...]
        mask = _attn_mask(n)
        rows = GROUP * BLOCK
        head_off = lax.broadcasted_iota(jnp.int32, (rows, 128), 1) - (lax.broadcasted_iota(jnp.int32, (rows, 128), 0) >> 7)
        dsink_row = jnp.zeros((1, 128), F32)
        prev0 = pl.multiple_of(jnp.maximum(n - 1, 0) * BLOCK, BLOCK)
        cur0 = pl.multiple_of(n * BLOCK, BLOCK)
        for hk in range(N_KV):
            qg = _group_rows(q, hk)
            dog = _group_rows(dov, hk)
            kcat = _kv_rows(kp, kc, hk)
            vcat = _kv_rows(vp, vc, hk)
            sc = lax.dot_general(qg, kcat, NT, preferred_element_type=F32) * ATTN_SCALE
            sc = jnp.where(mask, sc, NEG_INF)
            sink = _sink_col(sink_ref, hk)
            m = jnp.maximum(jnp.max(sc, axis=1, keepdims=True), sink)
            e = jnp.exp(sc - m)
            es = jnp.exp(sink - m)
            inv = 1.0 / (jnp.sum(e, axis=1, keepdims=True) + es)
            p = e * inv
            pb = p.astype(BF16)
            dp = lax.dot_general(dog, vcat, NT, preferred_element_type=F32)
            delta = jnp.sum(p * dp, axis=1, keepdims=True)
            dsc = (p * (dp - delta) * ATTN_SCALE).astype(BF16)
            dsk = -(es * inv) * delta
            dsink_row = dsink_row + jnp.sum(jnp.where(head_off == hk * GROUP, dsk, 0.0), axis=0, keepdims=True)
            dqg = lax.dot_general(dsc, kcat, NN, preferred_element_type=F32)
            dkcat = lax.dot_general(dsc, qg, TN, preferred_element_type=F32)
            dvcat = lax.dot_general(pb, dog, TN, preferred_element_type=F32)
            base = hk * GROUP * HEAD_DIM
            for g in range(GROUP):
                dqbuf[:, base + g * HEAD_DIM: base + (g + 1) * HEAD_DIM] = dqg[g * BLOCK:(g + 1) * BLOCK]
            sl = slice(hk * HEAD_DIM, (hk + 1) * HEAD_DIM)
            dkbuf[:, sl] = dkcat
            dvbuf[:, sl] = dvcat

        @pl.when(n > 0)
        def _():
            dk_ref[pl.ds(prev0, BLOCK), :] += dkbuf[0:BLOCK, :]
            dv_ref[pl.ds(prev0, BLOCK), :] += dvbuf[0:BLOCK, :]

        dk_ref[pl.ds(cur0, BLOCK), :] += dkbuf[BLOCK:2 * BLOCK, :]
        dv_ref[pl.ds(cur0, BLOCK), :] += dvbuf[BLOCK:2 * BLOCK, :]
        ds_ref[...] += dsink_row
        dqout[...] = _rope_t(dqbuf[...], tc_ref[...]).astype(BF16)
        cp = pltpu.make_async_copy(dqout, dproj_ref.at[pl.ds(cur0, BLOCK), pl.ds(C_Q, D)], sem)
        cp.start()
        cp.wait()

    blk = lambda w: pl.BlockSpec((BLOCK, w), lambda n: (n, 0))
    whole = lambda w: pl.BlockSpec((s, w), lambda n: (0, 0))
    anyspec = pl.BlockSpec(memory_space=pl.ANY)
    n_in = 1 + len(_attn_in_specs())
    return _call(
        body, name=name, grid=(nblk,), in_specs=[blk(D)] + _attn_in_specs() + [anyspec],
        out_specs=[anyspec, whole(D_KV), whole(D_KV), pl.BlockSpec((1, 128), lambda n: (0, 0))],
        out_shape=[_sds((s, N_IN), BF16), _sds((s, D_KV), F32), _sds((s, D_KV), F32), _sds((1, 128), F32)],
        scratch_shapes=[pltpu.VMEM((BLOCK, D), F32), pltpu.VMEM((BLOCK, D), BF16), pltpu.VMEM((2 * BLOCK, D_KV), F32),
                        pltpu.VMEM((2 * BLOCK, D_KV), F32), pltpu.SemaphoreType.DMA(())],
        input_output_aliases={n_in: 0}, compiler_params=_params("arbitrary"),
    )(do, proj, proj, proj, proj, proj, tab, tab, sinks, dproj)


def _kv_bwd(dkr, dv, tab, dproj, name):
    s = dkr.shape[0]
    tm = _row_tile(s)

    def body(dk_ref, dv_ref, t_ref, dproj_in, o_ref):
        del dproj_in
        o_ref[:, 0:D_KV] = _rope_t(dk_ref[...], t_ref[...]).astype(BF16)
        o_ref[:, D_KV:2 * D_KV] = dv_ref[...].astype(BF16)

    row = lambda w: pl.BlockSpec((tm, w), lambda i: (i, 0))
    return _call(
        body, name=name, grid=(s // tm,),
        in_specs=[row(D_KV), row(D_KV), row(384), pl.BlockSpec(memory_space=pl.ANY)],
        out_specs=pl.BlockSpec((tm, 2 * D_KV), lambda i: (i, C_K // (2 * D_KV))),
        out_shape=_sds((s, N_IN), BF16), input_output_aliases={3: 0}, compiler_params=_params("parallel"),
    )(dkr, dv, tab, dproj)


EW_TC = 512


def _sigmoid(x):
    return 0.5 * jnp.tanh(0.5 * x) + 0.5


def _merge_fwd(proj, conv_out, attn_out, name):
    s = proj.shape[0]
    tm = _row_tile(s)
    tile = pl.BlockSpec((tm, EW_TC), lambda i, j: (i, j))

    def body(gc_ref, ga_ref, co_ref, ao_ref, o_ref):
        o_ref[...] = (_sigmoid(gc_ref[...].astype(F32)) * co_ref[...]
                      + _sigmoid(ga_ref[...].astype(F32)) * ao_ref[...]).astype(BF16)

    return _call(
        body, name=name, grid=(s // tm, D // EW_TC),
        in_specs=[pl.BlockSpec((tm, EW_TC), lambda i, j: (i, C_GC // EW_TC + j)),
                  pl.BlockSpec((tm, EW_TC), lambda i, j: (i, C_GA // EW_TC + j)), tile, tile],
        out_specs=tile, out_shape=_sds((s, D), BF16), compiler_params=_params("parallel", "parallel"),
    )(proj, proj, conv_out, attn_out)


def _merge_bwd(dmerged, proj, conv_out, attn_out, name):
    s = proj.shape[0]
    tm = _row_tile(s)
    tile = pl.BlockSpec((tm, EW_TC), lambda i, j: (i, j))
    anyspec = pl.BlockSpec(memory_space=pl.ANY)

    def body(dm_ref, gc_ref, ga_ref, co_ref, ao_ref, dproj_ref, dco_ref, dao_ref, buf, sems):
        i, j = pl.program_id(0), pl.program_id(1)
        dm = dm_ref[...]
        sc = _sigmoid(gc_ref[...].astype(F32))
        sa = _sigmoid(ga_ref[...].astype(F32))
        dco_ref[...] = (dm * sc).astype(BF16)
        dao_ref[...] = (dm * sa).astype(BF16)
        buf[0] = (dm * co_ref[...] * sc * (1.0 - sc)).astype(BF16)
        buf[1] = (dm * ao_ref[...] * sa * (1.0 - sa)).astype(BF16)
        r0 = pl.multiple_of(i * tm, tm)
        copies = []
        for p, c0 in enumerate((C_GC, C_GA)):
            start = pl.multiple_of(c0 + j * EW_TC, EW_TC)
            copies.append(pltpu.make_async_copy(buf.at[p], dproj_ref.at[pl.ds(r0, tm), pl.ds(start, EW_TC)], sems.at[p]))
        for cp in copies:
            cp.start()
        for cp in copies:
            cp.wait()

    return _call(
        body, name=name, grid=(s // tm, D // EW_TC),
        in_specs=[tile, pl.BlockSpec((tm, EW_TC), lambda i, j: (i, C_GC // EW_TC + j)),
                  pl.BlockSpec((tm, EW_TC), lambda i, j: (i, C_GA // EW_TC + j)), tile, tile],
        out_specs=[anyspec, tile, tile],
        out_shape=[_sds((s, N_IN), BF16), _sds((s, D), BF16), _sds((s, D), BF16)],
        scratch_shapes=[pltpu.VMEM((2, tm, EW_TC), BF16), pltpu.SemaphoreType.DMA((2,))],
        compiler_params=_params("arbitrary", "arbitrary"),
    )(dmerged, proj, proj, conv_out, attn_out)


FF_TC = 256


def _gate_up_fwd(h2, wgu_t, name):
    s = h2.shape[0]
    tm = min(1024, s)
    nb = D_FF // FF_TC

    def body(h_ref, wg_ref, wu_ref, g_ref, u_ref, a_ref):
        h = h_ref[...]
        g = lax.dot_general(h, wg_ref[...], NT, preferred_element_type=F32)
        u = lax.dot_general(h, wu_ref[...], NT, preferred_element_type=F32)
        g_ref[...] = g.astype(BF16)
        u_ref[...] = u.astype(BF16)
        a_ref[...] = (g * _sigmoid(g) * u).astype(BF16)

    tile = pl.BlockSpec((tm, FF_TC), lambda i, j: (i, j))
    return _call(
        body, name=name, grid=(s // tm, nb),
        in_specs=[pl.BlockSpec((tm, D), lambda i, j: (i, 0)), pl.BlockSpec((FF_TC, D), lambda i, j: (j, 0)),
                  pl.BlockSpec((FF_TC, D), lambda i, j: (nb + j, 0))],
        out_specs=[tile, tile, tile], out_shape=[_sds((s, D_FF), BF16)] * 3,
        compiler_params=_params("parallel", "parallel"),
    )(h2, wgu_t, wgu_t)


def _down_bwd_x(dx2b, wd, gate, up, name):
    s = dx2b.shape[0]
    tm = min(1024, s)
    nb = D_FF // FF_TC

    def body(dx_ref, w_ref, g_ref, u_ref, dg_ref, du_ref):
        da = lax.dot_general(dx_ref[...], w_ref[...], NT, preferred_element_type=F32)
        g = g_ref[...].astype(F32)
        sg = _sigmoid(g)
        dg_ref[...] = (da * u_ref[...].astype(F32) * (sg * (1.0 + g * (1.0 - sg)))).astype(BF16)
        du_ref[...] = (da * (g * sg)).astype(BF16)

    tile = pl.BlockSpec((tm, FF_TC), lambda i, j: (i, j))
    return _call(
        body, name=name, grid=(s // tm, nb),
        in_specs=[pl.BlockSpec((tm, D), lambda i, j: (i, 0)), pl.BlockSpec((FF_TC, D), lambda i, j: (j, 0)), tile, tile],
        out_specs=[tile, tile], out_shape=[_sds((s, D_FF), BF16)] * 2,
        compiler_params=_params("parallel", "parallel"),
    )(dx2b, wd, gate, up)


def _local_step(x, tgt, g_mix, g_ffn, g_final, sinks, conv_w, win_t, wgu_t, wd, wco, wao, wo):
    s = x.shape[0]
    tab = _rope_tables(s)
    big = dict(tm=1024, tn=512, tk=1024)
    h1 = _rms_fwd(x, g_mix, "rms1_fwd")
    proj = _matmul(h1, win_t, mode="nt", out_dtype=BF16, name="proj_fwd", **big)
    conv_y = _conv_fwd(proj, conv_w, "conv_fwd")
    conv_out = _matmul(conv_y, wco, mode="nn", out_dtype=F32, name="conv_out_fwd", **big)
    attn = _attn_fwd(proj, tab, sinks, "attn_fwd")
    attn_out = _matmul(attn, wao, mode="nn", out_dtype=F32, name="attn_out_fwd", **big)
    merged = _merge_fwd(proj, conv_out, attn_out, "merge_fwd")
    x1 = _matmul(merged, wo, mode="nn", out_dtype=F32, name="wo_fwd", res=x, **big)
    h2 = _rms_fwd(x1, g_ffn, "rms2_fwd")
    gate, up, act = _gate_up_fwd(h2, wgu_t, "gate_up_fwd")
    x2 = _matmul(act, wd, mode="nn", out_dtype=F32, name="down_fwd", res=x1, tm=1024, tn=512, tk=D_FF)
    dx2, dx2b, dg_final, lossvec = _loss_head(x2, g_final, tgt, "loss_head")
    dgate, dup = _down_bwd_x(dx2b, wd, gate, up, "down_bwd_x")
    g_wd = _matmul(act, dx2b, mode="tn", out_dtype=BF16, name="down_bwd_w", tm=1408, tn=1024, tk=1024)
    dh2 = _matmul([dgate, dup], wgu_t, mode="nn", out_dtype=F32, name="gate_up_bwd_x", tm=1024, tn=1024, tk=1408)
    g_wgu_t = _matmul([dgate, dup], h2, mode="tn", out_dtype=BF16, name="gate_up_bwd_w", tm=1408, tn=1024, tk=1024)
    dx1, dx1b, dg_ffn = _rms_bwd(dh2, x1, g_ffn, dx2, "rms2_bwd")
    dmerged = _matmul(dx1b, wo, mode="nt", out_dtype=F32, name="wo_bwd_x", **big)
    g_wo = _matmul(merged, dx1b, mode="tn", out_dtype=BF16, name="wo_bwd_w", tm=512, tn=1024, tk=1024)
    dproj, dco, dao = _merge_bwd(dmerged, proj, conv_out, attn_out, "merge_bwd")
    dconv_y = _matmul(dco, wco, mode="nt", out_dtype=F32, name="conv_out_bwd_x", **big)
    g_wco = _matmul(conv_y, dco, mode="tn", out_dtype=BF16, name="conv_out_bwd_w", tm=512, tn=1024, tk=1024)
    dattn = _matmul(dao, wao, mode="nt", out_dtype=BF16, name="attn_out_bwd_x", **big)
    g_wao = _matmul(attn, dao, mode="tn", out_dtype=BF16, name="attn_out_bwd_w", tm=512, tn=1024, tk=1024)
    dproj, dconv_w = _conv_bwd(dconv_y, proj, conv_w, dproj, "conv_bwd")
    dproj, dkr, dv, dsinks = _attn_bwd(dattn, proj, tab, sinks, dproj, "attn_bwd")
    dproj = _kv_bwd(dkr, dv, tab, dproj, "kv_bwd")
    dh1 = _matmul(dproj, win_t, mode="nn", out_dtype=F32, name="proj_bwd_x", tm=1024, tn=1024, tk=512)
    g_win_t = _matmul(dproj, h1, mode="tn", out_dtype=BF16, name="proj_bwd_w", tm=512, tn=1024, tk=1024)
    dx, _, dg_mix = _rms_bwd(dh1, x, g_mix, dx1, "rms1_bwd")
    grads = dict(win_t=g_win_t, wgu_t=g_wgu_t, wd=g_wd, wco=g_wco, wao=g_wao, wo=g_wo)
    small = dict(g_mix=dg_mix, g_ffn=dg_ffn, g_final=dg_final, conv_w=dconv_w, sinks=dsinks, lossvec=lossvec)
    return dx, grads, small


def _position():
    return lax.axis_index("x"), lax.axis_index("y"), lax.axis_index("c")


def _other_chips(x, y):
    return [(1 - x, y), (x, 1 - y), (1 - x, 1 - y)]


def _all_gather(shards, name):
    n = len(shards)
    rows = [sh.shape[0] for sh in shards]

    def body(*refs):
        ins, outs = refs[:n], refs[n:2 * n]
        send_sems, recv_sems, local_sems = refs[2 * n:]
        x, y, c = _position()
        me, sibling = (x, y, c), (x, y, 1 - c)
        chips = _other_chips(x, y)

        def blk(a, px, py, pc):
            return outs[a].at[pl.ds((4 * px + 2 * py + pc) * rows[a], rows[a]), :]

        def copy(a, k, block, to, src=None):
            return pltpu.make_async_remote_copy(
                src_ref=blk(a, *block) if src is None else src, dst_ref=blk(a, *block),
                send_sem=send_sems.at[a, k], recv_sem=recv_sems.at[a, k], device_id=to, device_id_type=MESH)

        mine = [pltpu.make_async_copy(ins[a], blk(a, *me), local_sems.at[a]) for a in range(n)]
        for cp in mine:
            cp.start()
        first = []
        for a in range(n):
            first.append(copy(a, 0, me, sibling, src=ins[a]))
            first += [copy(a, 1 + j, me, (*chip, c), src=ins[a]) for j, chip in enumerate(chips)]
        for cp in first:
            cp.start()
        passed = []
        for j, chip in enumerate(chips):
            for a in range(n):
                copy(a, 1 + j, (*chip, c), me).wait_recv()
                fwd = copy(a, 4 + j, (*chip, c), sibling)
                fwd.start()
                passed.append(fwd)
        for a in range(n):
            copy(a, 0, sibling, me).wait_recv()
            for j, chip in enumerate(chips):
                copy(a, 4 + j, (*chip, 1 - c), me).wait_recv()
        for cp in first + passed:
            cp.wait_send()
        for cp in mine:
            cp.wait()

    return _call(
        body, name=name, in_specs=[HBM_SPEC] * n, out_specs=[HBM_SPEC] * n,
        out_shape=[_sds((N_DEV * sh.shape[0],) + sh.shape[1:], sh.dtype) for sh in shards],
        scratch_shapes=[pltpu.SemaphoreType.DMA((n, 7)), pltpu.SemaphoreType.DMA((n, 7)), pltpu.SemaphoreType.DMA((n,))],
    )(*shards)


def _whole(ref, nrows):
    return ref.at[pl.ds(0, nrows), :]


def _rs_sibling(grads, name):
    n = len(grads)
    rows = [g.shape[0] // N_DEV for g in grads]

    def body(*refs):
        ins, outs = refs[:n], refs[n:2 * n]
        send_sems, recv_sems = refs[2 * n:]
        x, y, c = _position()
        sibling = (x, y, 1 - c)
        for a in range(n):
            r = rows[a]
            for q in range(4):
                src = ins[a].at[pl.ds((2 * q + (1 - c)) * r, r), :]
                dst = outs[a].at[pl.ds(q * r, r), :]
                pltpu.make_async_remote_copy(src_ref=src, dst_ref=dst, send_sem=send_sems.at[a], recv_sem=recv_sems.at[a],
                                             device_id=sibling, device_id_type=MESH).start()
        for a in range(n):
            allrows = 4 * rows[a]
            pltpu.make_async_remote_copy(
                src_ref=_whole(ins[a], allrows), dst_ref=_whole(outs[a], allrows), send_sem=send_sems.at[a],
                recv_sem=recv_sems.at[a], device_id=sibling, device_id_type=MESH).wait()

    return _call(
        body, name=name, in_specs=[HBM_SPEC] * n, out_specs=[HBM_SPEC] * n,
        out_shape=[_sds((4 * r, g.shape[1]), g.dtype) for g, r in zip(grads, rows)],
        scratch_shapes=[pltpu.SemaphoreType.DMA((n,)), pltpu.SemaphoreType.DMA((n,))],
    )(*grads)


def _rs_chips(parts, name):
    n = len(parts)
    rows = [p.shape[0] // 4 for p in parts]

    def body(*refs):
        ins, outs = refs[:n], refs[n:2 * n]
        send_sems, recv_sems = refs[2 * n:]
        x, y, c = _position()
        chips = _other_chips(x, y)
        for a in range(n):
            r = rows[a]
            for j, (px, py) in enumerate(chips):
                src = ins[a].at[pl.ds((2 * px + py) * r, r), :]
                dst = outs[a].at[pl.ds(j * r, r), :]
                pltpu.make_async_remote_copy(src_ref=src, dst_ref=dst, send_sem=send_sems.at[a], recv_sem=recv_sems.at[a],
                                             device_id=(px, py, c), device_id_type=MESH).start()
        for a in range(n):
            allrows = 3 * rows[a]
            pltpu.make_async_remote_copy(
                src_ref=_whole(ins[a], allrows), dst_ref=_whole(outs[a], allrows), send_sem=send_sems.at[a],
                recv_sem=recv_sems.at[a], device_id=(x, y, c), device_id_type=MESH).wait()

    return _call(
        body, name=name, in_specs=[HBM_SPEC] * n, out_specs=[HBM_SPEC] * n,
        out_shape=[_sds((3 * r, p.shape[1]), p.dtype) for p, r in zip(parts, rows)],
        scratch_shapes=[pltpu.SemaphoreType.DMA((n,)), pltpu.SemaphoreType.DMA((n,))],
    )(*parts)


def _chip_partial(grad, recv, c_idx, name):
    r = recv.shape[0] // 4

    def body(c_ref, g_ref, s_ref, o_ref):
        del c_ref
        o_ref[...] = (g_ref[...].astype(F32) + s_ref[...].astype(F32)).astype(BF16)

    grid_spec = pltpu.PrefetchScalarGridSpec(
        num_scalar_prefetch=1, grid=(4,),
        in_specs=[pl.BlockSpec((r, D), lambda q, c_ref: (2 * q + c_ref[0], 0)), pl.BlockSpec((r, D), lambda q, c_ref: (q, 0))],
        out_specs=pl.BlockSpec((r, D), lambda q, c_ref: (q, 0)))
    return _call(body, name=name, grid_spec=grid_spec, out_shape=_sds((4 * r, D), BF16),
                 compiler_params=_params("parallel"))(c_idx, grad, recv)


def _final_grad(part, recv, q_idx, name):
    r = part.shape[0] // 4
    tr = r // 2 if r % 32 == 0 else r

    def body(q_ref, p_ref, r0_ref, r1_ref, r2_ref, o_ref):
        del q_ref
        o_ref[...] = ((p_ref[...].astype(F32) + r0_ref[...].astype(F32)) + r1_ref[...].astype(F32)) + r2_ref[...].astype(F32)

    nb = r // tr
    grid_spec = pltpu.PrefetchScalarGridSpec(
        num_scalar_prefetch=1, grid=(nb,),
        in_specs=[pl.BlockSpec((tr, D), lambda i, q_ref: (q_ref[0] * nb + i, 0))]
        + [pl.BlockSpec((tr, D), lambda i, q_ref, j=j: (j * nb + i, 0)) for j in range(3)],
        out_specs=pl.BlockSpec((tr, D), lambda i, q_ref: (i, 0)))
    return _call(body, name=name, grid_spec=grid_spec, out_shape=_sds((r, D), F32),
                 compiler_params=_params("parallel"))(q_idx, part, recv, recv, recv)


SMALL_ROWS = 8


def _small_all_reduce(pack, name):
    def body(p_ref, tot_ref, loss_ref, gath, send_sems, recv_sems):
        x, y, c = _position()
        me_id = 4 * x + 2 * y + c
        gath[me_id] = p_ref[...]
        copies = []
        for k in range(1, N_DEV):
            peer = tuple(1 - v if (k >> b) & 1 else v for v, b in ((x, 2), (y, 1), (c, 0)))
            cp = pltpu.make_async_remote_copy(src_ref=p_ref, dst_ref=gath.at[me_id], send_sem=send_sems.at[k - 1],
                                              recv_sem=recv_sems.at[k - 1], device_id=peer, device_id_type=MESH)
            cp.start()
            copies.append(cp)
        for cp in copies:
            cp.wait_recv()
        for cp in copies:
            cp.wait_send()
        tot = gath[0]
        for d in range(1, N_DEV):
            tot = tot + gath[d]
        tot_ref[...] = tot
        loss_ref[...] = jnp.full((1, 128), (0.5 / D) * jnp.sum(tot[SMALL_ROWS - 1:SMALL_ROWS, :]), F32)

    vm = pl.BlockSpec(memory_space=pltpu.VMEM)
    return _call(
        body, name=name, in_specs=[vm], out_specs=[vm, vm],
        out_shape=[_sds((SMALL_ROWS, D), F32), _sds((1, 128), F32)],
        scratch_shapes=[pltpu.VMEM((N_DEV, SMALL_ROWS, D), F32), pltpu.SemaphoreType.DMA((N_DEV - 1,)),
                        pltpu.SemaphoreType.DMA((N_DEV - 1,))],
    )(pack)


def _adamw(w, g, m, v, name):
    r, cdim = w.shape
    tr = 256 if r % 256 == 0 else (r // 2 if r % 16 == 0 else r)

    def body(w_ref, g_ref, m_ref, v_ref, d_ref, nm_ref, nv_ref):
        gv = g_ref[...]
        m2 = B1 * m_ref[...] + (1.0 - B1) * gv
        v2 = B2 * v_ref[...] + (1.0 - B2) * jnp.square(gv)
        m_hat = m2 / (1.0 - B1 ** STEP)
        v_hat = v2 / (1.0 - B2 ** STEP)
        d_ref[...] = -LR * (m_hat / (jnp.sqrt(v_hat) + EPS_ADAM) + WD * w_ref[...])
        nm_ref[...] = m2
        nv_ref[...] = v2

    spec = pl.BlockSpec((tr, cdim), lambda i: (i, 0))
    return _call(
        body, name=name, grid=(r // tr,), in_specs=[spec] * 4, out_specs=[spec] * 3,
        out_shape=[_sds((r, cdim), F32)] * 3, compiler_params=_params("parallel"),
    )(w, g, m, v)


def _cast_bf16(w, name):
    r, cdim = w.shape
    tr = 256 if r % 256 == 0 else r

    def body(w_ref, o_ref):
        o_ref[...] = w_ref[...].astype(BF16)

    spec = pl.BlockSpec((tr, cdim), lambda i: (i, 0))
    return _call(body, name=name, grid=(r // tr,), in_specs=[spec], out_specs=spec, out_shape=_sds((r, cdim), BF16),
                 compiler_params=_params("parallel"))(w)


BIG = ("win_t", "wgu_t", "wd", "wco", "wao", "wo")


def kernel(x, g_mix, w_in, conv_w, attn_sinks, w_conv_out, w_attn_out, w_o, g_ffn, w_gate_up, w_down, g_final, loss_target, m_g_mix, m_w_in, m_conv_w, m_attn_sinks, m_w_conv_out, m_w_attn_out, m_w_o, m_g_ffn, m_w_gate_up, m_w_down, m_g_final, v_g_mix, v_w_in, v_conv_w, v_attn_sinks, v_w_conv_out, v_w_attn_out, v_w_o, v_g_ffn, v_w_gate_up, v_w_down, v_g_final):
    cx, cy, cc = _position()
    c_idx = jnp.reshape(cc, (1,)).astype(jnp.int32)
    q_idx = jnp.reshape(2 * cx + cy, (1,)).astype(jnp.int32)
    me = 4 * cx + 2 * cy + cc

    shards = [
        _cast_bf16(jnp.transpose(w_in[0]), "cast_w_in"), _cast_bf16(jnp.transpose(w_gate_up[0]), "cast_w_gate_up"),
        _cast_bf16(w_down[0], "cast_w_down"), _cast_bf16(w_conv_out[0], "cast_w_conv_out"),
        _cast_bf16(w_attn_out[0], "cast_w_attn_out"), _cast_bf16(w_o[0], "cast_w_o"),
        jnp.pad(conv_w[0], ((0, 5), (0, 0))),
    ]
    full = _all_gather(shards, "all_gather_weights")
    weights = dict(zip(BIG, full[:6]))
    conv_w_full = jnp.transpose(full[6].reshape(N_DEV, 8, 128)[:, :3, :], (1, 0, 2)).reshape(3, D)

    dx, grads, small = _local_step(x[0], loss_target[0], g_mix, g_ffn, g_final[None], attn_sinks, conv_w_full, **weights)

    glist = [grads[k] for k in BIG]
    from_sibling = _rs_sibling(glist, "reduce_scatter_sibling")
    parts = [_chip_partial(g, r, c_idx, "chip_partial_" + k) for k, g, r in zip(BIG, glist, from_sibling)]
    from_chips = _rs_chips(parts, "reduce_scatter_chips")
    gfull = {k: _final_grad(p, r, q_idx, "final_grad_" + k) for k, p, r in zip(BIG, parts, from_chips)}

    sinks_row = jnp.pad(small["sinks"], ((0, 0), (0, D - 128)))
    pack = jnp.concatenate([small["g_mix"], small["g_ffn"], small["g_final"], small["conv_w"], sinks_row, small["lossvec"]], axis=0)
    tot, loss_row = _small_all_reduce(pack, "small_all_reduce")
    loss = loss_row[0, 0]
    g_small = {
        "g_mix": tot[0:1], "g_ffn": tot[1:2], "g_final": tot[2:3],
        "conv_w": lax.dynamic_slice(tot, (3, me * 128), (3, 128)), "attn_sinks": tot[6:7, :N_HEADS],
    }

    g_big = {"w_in": gfull["win_t"], "w_gate_up": gfull["wgu_t"], "w_down": gfull["wd"],
             "w_conv_out": gfull["wco"], "w_attn_out": gfull["wao"], "w_o": gfull["wo"]}
    transposed = ("w_in", "w_gate_up")

    def as2d(k, a):
        if k in transposed:
            return jnp.transpose(a[0])
        return a[None] if a.ndim == 1 else (a[0] if a.ndim == 3 else a)

    w_all = {"g_mix": g_mix, "w_in": w_in, "conv_w": conv_w, "attn_sinks": attn_sinks, "w_conv_out": w_conv_out,
             "w_attn_out": w_attn_out, "w_o": w_o, "g_ffn": g_ffn, "w_gate_up": w_gate_up, "w_down": w_down, "g_final": g_final}
    m_all = {"g_mix": m_g_mix, "w_in": m_w_in, "conv_w": m_conv_w, "attn_sinks": m_attn_sinks, "w_conv_out": m_w_conv_out,
             "w_attn_out": m_w_attn_out, "w_o": m_w_o, "g_ffn": m_g_ffn, "w_gate_up": m_w_gate_up, "w_down": m_w_down,
             "g_final": m_g_final}
    v_all = {"g_mix": v_g_mix, "w_in": v_w_in, "conv_w": v_conv_w, "attn_sinks": v_attn_sinks, "w_conv_out": v_w_conv_out,
             "w_attn_out": v_w_attn_out, "w_o": v_w_o, "g_ffn": v_g_ffn, "w_gate_up": v_w_gate_up, "w_down": v_w_down,
             "g_final": v_g_final}
    order = ["g_mix", "w_in", "conv_w", "attn_sinks", "w_conv_out", "w_attn_out", "w_o", "g_ffn", "w_gate_up", "w_down", "g_final"]
    out_g, out_d, out_m, out_v = [], [], [], []
    for k in order:
        g = g_big[k] if k in g_big else g_small[k]
        d, nm, nv = _adamw(as2d(k, w_all[k]), g, as2d(k, m_all[k]), as2d(k, v_all[k]), "adamw_" + k)
        for lst, val in ((out_g, g), (out_d, d), (out_m, nm), (out_v, nv)):
            lst.append((jnp.transpose(val) if k in transposed else val).reshape(w_all[k].shape))
    return (loss, dx[None], *out_g, *out_d, *out_m, *out_v)
```

```python
import functools
import math

import jax
import jax.numpy as jnp
from jax import lax
from jax.experimental import pallas as pl
from jax.experimental.pallas import tpu as pltpu

F32 = jnp.float32
BF16 = jnp.bfloat16

D = 1024
HEAD_DIM = 64
N_HEADS = 16
N_KV = 4
GROUP = N_HEADS // N_KV
D_KV = N_KV * HEAD_DIM
BLOCK = 128
ROT_DIM = HEAD_DIM // 4
ROPE_THETA = 500000.0
ATTN_SCALE = 1.0 / math.sqrt(HEAD_DIM)
NEG_INF = -1e30
D_FF = 2816
N_IN = 6656
EPS = 1e-5
C_CB, C_CC, C_CX, C_Q, C_K, C_V, C_GC, C_GA = 0, 1024, 2048, 3072, 4096, 4352, 4608, 5632

LR, B1, B2, EPS_ADAM, WD, STEP = 0.001, 0.9, 0.999, 1e-08, 0.01, 10

N_DEV = 8
MESH = pl.DeviceIdType.MESH
VMEM_LIMIT = 56 * 1024 * 1024

NN = (((1,), (0,)), ((), ()))
NT = (((1,), (1,)), ((), ()))
TN = (((0,), (0,)), ((), ()))
HBM_SPEC = pl.BlockSpec(memory_space=pl.ANY)


def _call(body, **kw):
    return pl.pallas_call(body, **kw)


def _params(*sem):
    return pltpu.CompilerParams(dimension_semantics=sem, vmem_limit_bytes=VMEM_LIMIT)


def _sds(shape, dtype):
    return jax.ShapeDtypeStruct(shape, dtype)


def _matmul(a, b, *, mode, tm, tn, tk, out_dtype, name, res=None, after=()):
    parts = list(a) if isinstance(a, (list, tuple)) else [a]
    rows_a = parts[0].shape[0]
    cols_a = sum(p.shape[1] for p in parts)
    if mode == "nn":
        (m, kk), (_, n), dims = (rows_a, cols_a), b.shape, NN
    elif mode == "nt":
        (m, kk), (n, _), dims = (rows_a, cols_a), b.shape, NT
    else:
        (kk, m), (_, n), dims = (rows_a, cols_a), b.shape, TN
    tm, tn, tk = min(tm, m), min(tn, n), min(tk, kk)
    assert m % tm == 0 and n % tn == 0 and kk % tk == 0, (name, m, n, kk, tm, tn, tk)
    nk = kk // tk
    split_axis, width = (2, tk) if mode == "nn" else (0, tm)
    assert len(parts) == 1 or mode in ("nn", "tn")
    assert all(p.shape[1] % width == 0 for p in parts), (name, width)
    counts = [p.shape[1] // width for p in parts]
    starts = [sum(counts[:p]) for p in range(len(parts))]

    def a_spec(p):
        def col(t):
            return jnp.clip(t - starts[p], 0, counts[p] - 1) if len(parts) > 1 else t

        if mode == "tn":
            return pl.BlockSpec((tk, tm), lambda i, j, k: (k, col(i)))
        return pl.BlockSpec((tm, tk), lambda i, j, k: (i, col(k)))

    if mode == "nt":
        b_spec = pl.BlockSpec((tn, tk), lambda i, j, k: (j, k))
    else:
        b_spec = pl.BlockSpec((tk, tn), lambda i, j, k: (k, j))
    o_spec = pl.BlockSpec((tm, tn), lambda i, j, k: (i, j))
    has_res = res is not None
    n_parts = len(parts)

    def body(*refs):
        a_refs, b_ref = refs[:n_parts], refs[n_parts]
        r_ref = refs[n_parts + 1] if has_res else None
        o_ref = refs[n_parts + 1 + has_res + len(after)]
        k = pl.program_id(2)

        def step(a_ref):
            part = lax.dot_general(a_ref[...], b_ref[...], dims, preferred_element_type=F32)

            def finish(acc):
                if has_res:
                    acc = acc + r_ref[...]
                o_ref[...] = acc.astype(o_ref.dtype)

            if nk == 1:
                finish(part)
            else:
                acc_ref = refs[-1]

                @pl.when(k == 0)
                def _():
                    acc_ref[...] = part

                @pl.when(k > 0)
                def _():
                    acc_ref[...] += part

                @pl.when(k == nk - 1)
                def _():
                    finish(acc_ref[...])

        if n_parts == 1:
            step(a_refs[0])
        else:
            t = pl.program_id(split_axis)
            for p in range(n_parts):
                pl.when((t >= starts[p]) & (t < starts[p] + counts[p]))(functools.partial(step, a_refs[p]))

    ins = parts + [b] + ([res] if has_res else []) + list(after)
    in_specs = [a_spec(p) for p in range(n_parts)] + [b_spec] + ([o_spec] if has_res else []) + [HBM_SPEC] * len(after)
    scratch = [] if nk == 1 else [pltpu.VMEM((tm, tn), F32)]
    return _call(
        body, name=name, grid=(m // tm, n // tn, nk), in_specs=in_specs, out_specs=o_spec,
        out_shape=_sds((m, n), out_dtype), scratch_shapes=scratch,
        compiler_params=_params("parallel", "parallel", "arbitrary"),
    )(*ins)


def _row_tile(s):
    return min(256, s)


def _rms_fwd(x, g, name):
    s = x.shape[0]
    tm = _row_tile(s)

    def body(x_ref, g_ref, h_ref):
        xv = x_ref[...]
        r = lax.rsqrt(jnp.mean(xv * xv, axis=-1, keepdims=True) + EPS)
        h_ref[...] = (xv * r * g_ref[...]).astype(BF16)

    row = pl.BlockSpec((tm, D), lambda i: (i, 0))
    return _call(
        body, name=name, grid=(s // tm,), in_specs=[row, pl.BlockSpec((1, D), lambda i: (0, 0))],
        out_specs=row, out_shape=_sds((s, D), BF16), compiler_params=_params("parallel"),
    )(x, g)


def _rms_bwd(dh, x, g, dres, name):
    s = x.shape[0]
    tm = _row_tile(s)

    def body(dh_ref, x_ref, g_ref, dres_ref, dx_ref, dxb_ref, dg_ref):
        xv = x_ref[...]
        r = lax.rsqrt(jnp.mean(xv * xv, axis=-1, keepdims=True) + EPS)
        xh = xv * r
        dhv = dh_ref[...]
        dyg = dhv * g_ref[...]
        dx = dres_ref[...] + r * (dyg - xh * jnp.mean(dyg * xh, axis=-1, keepdims=True))
        dx_ref[...] = dx
        dxb_ref[...] = dx.astype(BF16)
        part = jnp.sum(dhv * xh, axis=0, keepdims=True)

        @pl.when(pl.program_id(0) == 0)
        def _():
            dg_ref[...] = part

        @pl.when(pl.program_id(0) > 0)
        def _():
            dg_ref[...] += part

    row = pl.BlockSpec((tm, D), lambda i: (i, 0))
    vec = pl.BlockSpec((1, D), lambda i: (0, 0))
    return _call(
        body, name=name, grid=(s // tm,), in_specs=[row, row, vec, row], out_specs=[row, row, vec],
        out_shape=[_sds((s, D), F32), _sds((s, D), BF16), _sds((1, D), F32)],
        compiler_params=_params("arbitrary"),
    )(dh, x, g, dres)


def _loss_head(x2, g, tgt, name):
    s = x2.shape[0]
    tm = _row_tile(s)

    def body(x_ref, g_ref, t_ref, dx_ref, dxb_ref, dg_ref, l_ref):
        xv = x_ref[...]
        gv = g_ref[...]
        r = lax.rsqrt(jnp.mean(xv * xv, axis=-1, keepdims=True) + EPS)
        xh = xv * r
        err = xh * gv - t_ref[...]
        dy = err * (1.0 / D)
        dyg = dy * gv
        dx = r * (dyg - xh * jnp.mean(dyg * xh, axis=-1, keepdims=True))
        dx_ref[...] = dx
        dxb_ref[...] = dx.astype(BF16)
        dg_part = jnp.sum(dy * xh, axis=0, keepdims=True)
        l_part = jnp.sum(err * err, axis=0, keepdims=True)

        @pl.when(pl.program_id(0) == 0)
        def _():
            dg_ref[...] = dg_part
            l_ref[...] = l_part

        @pl.when(pl.program_id(0) > 0)
        def _():
            dg_ref[...] += dg_part
            l_ref[...] += l_part

    row = pl.BlockSpec((tm, D), lambda i: (i, 0))
    vec = pl.BlockSpec((1, D), lambda i: (0, 0))
    return _call(
        body, name=name, grid=(s // tm,), in_specs=[row, vec, row], out_specs=[row, row, vec, vec],
        out_shape=[_sds((s, D), F32), _sds((s, D), BF16), _sds((1, D), F32), _sds((1, D), F32)],
        compiler_params=_params("arbitrary"),
    )(x2, g, tgt)


CONV_TC = 256


def _shift_down(u, k, rows):
    return jnp.where(rows >= k, pltpu.roll(u, k, 0), 0.0)


def _shift_up(u, k, rows, s):
    return jnp.where(rows < s - k, pltpu.roll(u, s - k, 0), 0.0)


def _conv_specs(s):
    nb = D // CONV_TC

    def col(c0):
        return pl.BlockSpec((s, CONV_TC), lambda j, c0=c0: (0, c0 // CONV_TC + j))

    return nb, col


def _conv_fwd(proj, conv_w, name):
    s = proj.shape[0]
    nb, col = _conv_specs(s)

    def body(cb_ref, cc_ref, cx_ref, w_ref, y_ref):
        rows = lax.broadcasted_iota(jnp.int32, (s, CONV_TC), 0)
        u = cc_ref[...].astype(F32) * cx_ref[...].astype(F32)
        w = w_ref[...]
        c = w[0:1] * _shift_down(u, 2, rows) + w[1:2] * _shift_down(u, 1, rows) + w[2:3] * u
        y_ref[...] = (cb_ref[...].astype(F32) * c).astype(BF16)

    return _call(
        body, name=name, grid=(nb,),
        in_specs=[col(C_CB), col(C_CC), col(C_CX), pl.BlockSpec((3, CONV_TC), lambda j: (0, j))],
        out_specs=pl.BlockSpec((s, CONV_TC), lambda j: (0, j)), out_shape=_sds((s, D), BF16),
        compiler_params=_params("parallel"),
    )(proj, proj, proj, conv_w)


def _conv_bwd(dy, proj, conv_w, dproj, name, after=()):
    s = proj.shape[0]
    nb, col = _conv_specs(s)

    def body(dy_ref, cb_ref, cc_ref, cx_ref, w_ref, *rest):
        dproj_ref, dw_ref, buf, sems = rest[1 + len(after):]
        j = pl.program_id(0)
        rows = lax.broadcasted_iota(jnp.int32, (s, CONV_TC), 0)
        cc = cc_ref[...].astype(F32)
        cx = cx_ref[...].astype(F32)
        u = cc * cx
        u1 = _shift_down(u, 1, rows)
        u2 = _shift_down(u, 2, rows)
        w = w_ref[...]
        c = w[0:1] * u2 + w[1:2] * u1 + w[2:3] * u
        dyv = dy_ref[...]
        dc = dyv * cb_ref[...].astype(F32)
        du = w[2:3] * dc + w[1:2] * _shift_up(dc, 1, rows, s) + w[0:1] * _shift_up(dc, 2, rows, s)
        buf[0] = (dyv * c).astype(BF16)
        buf[1] = (du * cx).astype(BF16)
        buf[2] = (du * cc).astype(BF16)
        dw_ref[...] = jnp.concatenate(
            [jnp.sum(dc * u2, axis=0, keepdims=True), jnp.sum(dc * u1, axis=0, keepdims=True),
             jnp.sum(dc * u, axis=0, keepdims=True)], axis=0)
        copies = []
        for p, c0 in enumerate((C_CB, C_CC, C_CX)):
            start = pl.multiple_of(c0 + j * CONV_TC, CONV_TC)
            copies.append(pltpu.make_async_copy(buf.at[p], dproj_ref.at[:, pl.ds(start, CONV_TC)], sems.at[p]))
        for cp in copies:
            cp.start()
        for cp in copies:
            cp.wait()

    return _call(
        body, name=name, grid=(nb,),
        in_specs=[pl.BlockSpec((s, CONV_TC), lambda j: (0, j)), col(C_CB), col(C_CC), col(C_CX),
                  pl.BlockSpec((3, CONV_TC), lambda j: (0, j))] + [HBM_SPEC] * (1 + len(after)),
        out_specs=[pl.BlockSpec(memory_space=pl.ANY), pl.BlockSpec((3, CONV_TC), lambda j: (0, j))],
        out_shape=[_sds((s, N_IN), BF16), _sds((3, D), F32)],
        scratch_shapes=[pltpu.VMEM((3, s, CONV_TC), BF16), pltpu.SemaphoreType.DMA((3,))],
        input_output_aliases={5: 0}, compiler_params=_params("arbitrary"),
    )(dy, proj, proj, proj, conv_w, dproj, *after)


def _rope_tables(s):
    inv_freq = ROPE_THETA ** (-jnp.arange(0, ROT_DIM, 2, dtype=F32) / ROT_DIM)
    ang = jnp.arange(s, dtype=F32)[:, None] * inv_freq[None, :]
    cos, sin = jnp.cos(ang), jnp.sin(ang)
    half = ROT_DIM // 2
    ones = jnp.ones((s, HEAD_DIM - ROT_DIM), F32)
    zeros = jnp.zeros((s, HEAD_DIM - ROT_DIM), F32)
    zh = jnp.zeros((s, half), F32)
    c64 = jnp.concatenate([cos, cos, ones], axis=1)
    a64 = jnp.concatenate([-sin, zh, zeros], axis=1)
    b64 = jnp.concatenate([zh, sin, zeros], axis=1)
    return jnp.concatenate([c64, c64, a64, a64, b64, b64], axis=1)


def _rope(x, tab):
    c, a, b = tab[:, 0:128], tab[:, 128:256], tab[:, 256:384]
    outs = []
    for i in range(x.shape[1] // 128):
        xc = x[:, i * 128:(i + 1) * 128]
        outs.append(xc * c + pltpu.roll(xc, 120, 1) * a + pltpu.roll(xc, 8, 1) * b)
    return outs[0] if len(outs) == 1 else jnp.concatenate(outs, axis=1)


def _rope_t(dx, tab):
    c, a, b = tab[:, 0:128], tab[:, 128:256], tab[:, 256:384]
    outs = []
    for i in range(dx.shape[1] // 128):
        dc = dx[:, i * 128:(i + 1) * 128]
        outs.append(dc * c + pltpu.roll(dc * a, 8, 1) + pltpu.roll(dc * b, 120, 1))
    return outs[0] if len(outs) == 1 else jnp.concatenate(outs, axis=1)


def _attn_mask(n):
    qi = lax.broadcasted_iota(jnp.int32, (GROUP * BLOCK, 2 * BLOCK), 0) & (BLOCK - 1)
    kj = lax.broadcasted_iota(jnp.int32, (GROUP * BLOCK, 2 * BLOCK), 1)
    rel = qi + BLOCK - kj
    return (rel >= 0) & (rel < BLOCK) & ((kj >= BLOCK) | (n > 0))


def _sink_col(sink_ref, hk):
    return jnp.concatenate([jnp.full((BLOCK, 1), sink_ref[0, hk * GROUP + g], F32) for g in range(GROUP)], axis=0)


def _attn_in_specs():
    prev = lambda n: jnp.maximum(n - 1, 0)
    return [
        pl.BlockSpec((BLOCK, D), lambda n: (n, C_Q // D)),
        pl.BlockSpec((BLOCK, D_KV), lambda n: (n, C_K // D_KV)),
        pl.BlockSpec((BLOCK, D_KV), lambda n: (prev(n), C_K // D_KV)),
        pl.BlockSpec((BLOCK, D_KV), lambda n: (n, C_V // D_KV)),
        pl.BlockSpec((BLOCK, D_KV), lambda n: (prev(n), C_V // D_KV)),
        pl.BlockSpec((BLOCK, 384), lambda n: (n, 0)),
        pl.BlockSpec((BLOCK, 384), lambda n: (prev(n), 0)),
        pl.BlockSpec(memory_space=pltpu.SMEM),
    ]


def _load_qkv(q_ref, kc_ref, kp_ref, vc_ref, vp_ref, tc_ref, tp_ref):
    q = _rope(q_ref[...].astype(F32), tc_ref[...]).astype(BF16)
    kc = _rope(kc_ref[...].astype(F32), tc_ref[...]).astype(BF16)
    kp = _rope(kp_ref[...].astype(F32), tp_ref[...]).astype(BF16)
    return q, kc, kp, vc_ref[...], vp_ref[...]


def _group_rows(x, hk):
    base = hk * GROUP * HEAD_DIM
    return jnp.concatenate([x[:, base + g * HEAD_DIM: base + (g + 1) * HEAD_DIM] for g in range(GROUP)], axis=0)


def _kv_rows(prev, cur, hk):
    sl = slice(hk * HEAD_DIM, (hk + 1) * HEAD_DIM)
    return jnp.concatenate([prev[:, sl], cur[:, sl]], axis=0)


def _attn_fwd(proj, tab, sinks, name):
    s = proj.shape[0]

    def body(q_ref, kc_ref, kp_ref, vc_ref, vp_ref, tc_ref, tp_ref, sink_ref, o_ref):
        n = pl.program_id(0)
        q, kc, kp, vc, vp = _load_qkv(q_ref, kc_ref, kp_ref, vc_ref, vp_ref, tc_ref, tp_ref)
        mask = _attn_mask(n)
        for hk in range(N_KV):
            qg = _group_rows(q, hk)
            kcat = _kv_rows(kp, kc, hk)
            vcat = _kv_rows(vp, vc, hk)
            sc = lax.dot_general(qg, kcat, NT, preferred_element_type=F32) * ATTN_SCALE
            sc = jnp.where(mask, sc, NEG_INF)
            sink = _sink_col(sink_ref, hk)
            m = jnp.maximum(jnp.max(sc, axis=1, keepdims=True), sink)
            p = jnp.exp(sc - m)
            inv = 1.0 / (jnp.sum(p, axis=1, keepdims=True) + jnp.exp(sink - m))
            o = lax.dot_general((p * inv).astype(BF16), vcat, NN, preferred_element_type=F32)
            base = hk * GROUP * HEAD_DIM
            for g in range(GROUP):
                o_ref[:, base + g * HEAD_DIM: base + (g + 1) * HEAD_DIM] = o[g * BLOCK:(g + 1) * BLOCK].astype(BF16)

    return _call(
        body, name=name, grid=(s // BLOCK,), in_specs=_attn_in_specs(),
        out_specs=pl.BlockSpec((BLOCK, D), lambda n: (n, 0)), out_shape=_sds((s, D), BF16),
        compiler_params=_params("parallel"),
    )(proj, proj, proj, proj, proj, tab, tab, sinks)


def _attn_bwd(do, proj, tab, sinks, dproj, name):
    s = proj.shape[0]
    nblk = s // BLOCK

    def body(do_ref, q_ref, kc_ref, kp_ref, vc_ref, vp_ref, tc_ref, tp_ref, sink_ref, dproj_in, dproj_ref,
             dk_ref, dv_ref, ds_ref, dqbuf, dqout, dkbuf, dvbuf, sem):
        del dproj_in
        n = pl.program_id(0)

        @pl.when(n == 0)
        def _():
            dk_ref[...] = jnp.zeros_like(dk_ref)
            dv_ref[...] = jnp.zeros_like(dv_ref)
            ds_ref[...] = jnp.zeros_like(ds_ref)

        q, kc, kp, vc, vp = _load_qkv(q_ref, kc_ref, kp_ref, vc_ref, vp_ref, tc_ref, tp_ref)
        dov = do_ref[...]
        mask = _attn_mask(n)
        rows = GROUP * BLOCK
        head_off = lax.broadcasted_iota(jnp.int32, (rows, 128), 1) - (lax.broadcasted_iota(jnp.int32, (rows, 128), 0) >> 7)
        dsink_row = jnp.zeros((1, 128), F32)
        prev0 = pl.multiple_of(jnp.maximum(n - 1, 0) * BLOCK, BLOCK)
        cur0 = pl.multiple_of(n * BLOCK, BLOCK)
        for hk in range(N_KV):
            qg = _group_rows(q, hk)
            dog = _group_rows(dov, hk)
            kcat = _kv_rows(kp, kc, hk)
            vcat = _kv_rows(vp, vc, hk)
            sc = lax.dot_general(qg, kcat, NT, preferred_element_type=F32) * ATTN_SCALE
            sc = jnp.where(mask, sc, NEG_INF)
            sink = _sink_col(sink_ref, hk)
            m = jnp.maximum(jnp.max(sc, axis=1, keepdims=True), sink)
            e = jnp.exp(sc - m)
            es = jnp.exp(sink - m)
            inv = 1.0 / (jnp.sum(e, axis=1, keepdims=True) + es)
            p = e * inv
            pb = p.astype(BF16)
            dp = lax.dot_general(dog, vcat, NT, preferred_element_type=F32)
            delta = jnp.sum(p * dp, axis=1, keepdims=True)
            dsc = (p * (dp - delta) * ATTN_SCALE).astype(BF16)
            dsk = -(es * inv) * delta
            dsink_row = dsink_row + jnp.sum(jnp.where(head_off == hk * GROUP, dsk, 0.0), axis=0, keepdims=True)
            dqg = lax.dot_general(dsc, kcat, NN, preferred_element_type=F32)
            dkcat = lax.dot_general(dsc, qg, TN, preferred_element_type=F32)
            dvcat = lax.dot_general(pb, dog, TN, preferred_element_type=F32)
            base = hk * GROUP * HEAD_DIM
            for g in range(GROUP):
                dqbuf[:, base + g * HEAD_DIM: base + (g + 1) * HEAD_DIM] = dqg[g * BLOCK:(g + 1) * BLOCK]
            sl = slice(hk * HEAD_DIM, (hk + 1) * HEAD_DIM)
            dkbuf[:, sl] = dkcat
            dvbuf[:, sl] = dvcat

        @pl.when(n > 0)
        def _():
            dk_ref[pl.ds(prev0, BLOCK), :] += dkbuf[0:BLOCK, :]
            dv_ref[pl.ds(prev0, BLOCK), :] += dvbuf[0:BLOCK, :]

        dk_ref[pl.ds(cur0, BLOCK), :] += dkbuf[BLOCK:2 * BLOCK, :]
        dv_ref[pl.ds(cur0, BLOCK), :] += dvbuf[BLOCK:2 * BLOCK, :]
        ds_ref[...] += dsink_row
        dqout[...] = _rope_t(dqbuf[...], tc_ref[...]).astype(BF16)
        cp = pltpu.make_async_copy(dqout, dproj_ref.at[pl.ds(cur0, BLOCK), pl.ds(C_Q, D)], sem)
        cp.start()
        cp.wait()

    blk = lambda w: pl.BlockSpec((BLOCK, w), lambda n: (n, 0))
    whole = lambda w: pl.BlockSpec((s, w), lambda n: (0, 0))
    anyspec = pl.BlockSpec(memory_space=pl.ANY)
    n_in = 1 + len(_attn_in_specs())
    return _call(
        body, name=name, grid=(nblk,), in_specs=[blk(D)] + _attn_in_specs() + [anyspec],
        out_specs=[anyspec, whole(D_KV), whole(D_KV), pl.BlockSpec((1, 128), lambda n: (0, 0))],
        out_shape=[_sds((s, N_IN), BF16), _sds((s, D_KV), F32), _sds((s, D_KV), F32), _sds((1, 128), F32)],
        scratch_shapes=[pltpu.VMEM((BLOCK, D), F32), pltpu.VMEM((BLOCK, D), BF16), pltpu.VMEM((2 * BLOCK, D_KV), F32),
                        pltpu.VMEM((2 * BLOCK, D_KV), F32), pltpu.SemaphoreType.DMA(())],
        input_output_aliases={n_in: 0}, compiler_params=_params("arbitrary"),
    )(do, proj, proj, proj, proj, proj, tab, tab, sinks, dproj)


def _kv_bwd(dkr, dv, tab, dproj, name):
    s = dkr.shape[0]
    tm = _row_tile(s)

    def body(dk_ref, dv_ref, t_ref, dproj_in, o_ref):
        del dproj_in
        o_ref[:, 0:D_KV] = _rope_t(dk_ref[...], t_ref[...]).astype(BF16)
        o_ref[:, D_KV:2 * D_KV] = dv_ref[...].astype(BF16)

    row = lambda w: pl.BlockSpec((tm, w), lambda i: (i, 0))
    return _call(
        body, name=name, grid=(s // tm,),
        in_specs=[row(D_KV), row(D_KV), row(384), pl.BlockSpec(memory_space=pl.ANY)],
        out_specs=pl.BlockSpec((tm, 2 * D_KV), lambda i: (i, C_K // (2 * D_KV))),
        out_shape=_sds((s, N_IN), BF16), input_output_aliases={3: 0}, compiler_params=_params("parallel"),
    )(dkr, dv, tab, dproj)


EW_TC = 512


def _sigmoid(x):
    return 0.5 * jnp.tanh(0.5 * x) + 0.5


def _merge_fwd(proj, conv_out, attn_out, name):
    s = proj.shape[0]
    tm = _row_tile(s)
    tile = pl.BlockSpec((tm, EW_TC), lambda i, j: (i, j))

    def body(gc_ref, ga_ref, co_ref, ao_ref, o_ref):
        o_ref[...] = (_sigmoid(gc_ref[...].astype(F32)) * co_ref[...]
                      + _sigmoid(ga_ref[...].astype(F32)) * ao_ref[...]).astype(BF16)

    return _call(
        body, name=name, grid=(s // tm, D // EW_TC),
        in_specs=[pl.BlockSpec((tm, EW_TC), lambda i, j: (i, C_GC // EW_TC + j)),
                  pl.BlockSpec((tm, EW_TC), lambda i, j: (i, C_GA // EW_TC + j)), tile, tile],
        out_specs=tile, out_shape=_sds((s, D), BF16), compiler_params=_params("parallel", "parallel"),
    )(proj, proj, conv_out, attn_out)


def _merge_bwd(dmerged, proj, conv_out, attn_out, name):
    s = proj.shape[0]
    tm = _row_tile(s)
    tile = pl.BlockSpec((tm, EW_TC), lambda i, j: (i, j))
    anyspec = pl.BlockSpec(memory_space=pl.ANY)

    def body(dm_ref, gc_ref, ga_ref, co_ref, ao_ref, dproj_ref, dco_ref, dao_ref, buf, sems):
        i, j = pl.program_id(0), pl.program_id(1)
        dm = dm_ref[...]
        sc = _sigmoid(gc_ref[...].astype(F32))
        sa = _sigmoid(ga_ref[...].astype(F32))
        dco_ref[...] = (dm * sc).astype(BF16)
        dao_ref[...] = (dm * sa).astype(BF16)
        buf[0] = (dm * co_ref[...] * sc * (1.0 - sc)).astype(BF16)
        buf[1] = (dm * ao_ref[...] * sa * (1.0 - sa)).astype(BF16)
        r0 = pl.multiple_of(i * tm, tm)
        copies = []
        for p, c0 in enumerate((C_GC, C_GA)):
            start = pl.multiple_of(c0 + j * EW_TC, EW_TC)
            copies.append(pltpu.make_async_copy(buf.at[p], dproj_ref.at[pl.ds(r0, tm), pl.ds(start, EW_TC)], sems.at[p]))
        for cp in copies:
            cp.start()
        for cp in copies:
            cp.wait()

    return _call(
        body, name=name, grid=(s // tm, D // EW_TC),
        in_specs=[tile, pl.BlockSpec((tm, EW_TC), lambda i, j: (i, C_GC // EW_TC + j)),
                  pl.BlockSpec((tm, EW_TC), lambda i, j: (i, C_GA // EW_TC + j)), tile, tile],
        out_specs=[anyspec, tile, tile],
        out_shape=[_sds((s, N_IN), BF16), _sds((s, D), BF16), _sds((s, D), BF16)],
        scratch_shapes=[pltpu.VMEM((2, tm, EW_TC), BF16), pltpu.SemaphoreType.DMA((2,))],
        compiler_params=_params("arbitrary", "arbitrary"),
    )(dmerged, proj, proj, conv_out, attn_out)


FF_TC = 256


def _gate_up_fwd(h2, wgu_t, name):
    s = h2.shape[0]
    tm = min(1024, s)
    nb = D_FF // FF_TC

    def body(h_ref, wg_ref, wu_ref, g_ref, u_ref, a_ref):
        h = h_ref[...]
        g = lax.dot_general(h, wg_ref[...], NT, preferred_element_type=F32)
        u = lax.dot_general(h, wu_ref[...], NT, preferred_element_type=F32)
        g_ref[...] = g.astype(BF16)
        u_ref[...] = u.astype(BF16)
        a_ref[...] = (g * _sigmoid(g) * u).astype(BF16)

    tile = pl.BlockSpec((tm, FF_TC), lambda i, j: (i, j))
    return _call(
        body, name=name, grid=(s // tm, nb),
        in_specs=[pl.BlockSpec((tm, D), lambda i, j: (i, 0)), pl.BlockSpec((FF_TC, D), lambda i, j: (j, 0)),
                  pl.BlockSpec((FF_TC, D), lambda i, j: (nb + j, 0))],
        out_specs=[tile, tile, tile], out_shape=[_sds((s, D_FF), BF16)] * 3,
        compiler_params=_params("parallel", "parallel"),
    )(h2, wgu_t, wgu_t)


def _down_bwd_x(dx2b, wd, gate, up, name):
    s = dx2b.shape[0]
    tm = min(1024, s)
    nb = D_FF // FF_TC

    def body(dx_ref, w_ref, g_ref, u_ref, dg_ref, du_ref):
        da = lax.dot_general(dx_ref[...], w_ref[...], NT, preferred_element_type=F32)
        g = g_ref[...].astype(F32)
        sg = _sigmoid(g)
        dg_ref[...] = (da * u_ref[...].astype(F32) * (sg * (1.0 + g * (1.0 - sg)))).astype(BF16)
        du_ref[...] = (da * (g * sg)).astype(BF16)

    tile = pl.BlockSpec((tm, FF_TC), lambda i, j: (i, j))
    return _call(
        body, name=name, grid=(s // tm, nb),
        in_specs=[pl.BlockSpec((tm, D), lambda i, j: (i, 0)), pl.BlockSpec((FF_TC, D), lambda i, j: (j, 0)), tile, tile],
        out_specs=[tile, tile], out_shape=[_sds((s, D_FF), BF16)] * 2,
        compiler_params=_params("parallel", "parallel"),
    )(dx2b, wd, gate, up)


def _local_step(x, tgt, g_mix, g_ffn, g_final, sinks, conv_w, win_t, wgu_t, wd, wco, wao, wo, on_grads=None):
    on_grads = on_grads or (lambda group, g: ())
    s = x.shape[0]
    tab = _rope_tables(s)
    big = dict(tm=1024, tn=512, tk=1024)
    h1 = _rms_fwd(x, g_mix, "rms1_fwd")
    proj = _matmul(h1, win_t, mode="nt", out_dtype=BF16, name="proj_fwd", **big)
    conv_y = _conv_fwd(proj, conv_w, "conv_fwd")
    conv_out = _matmul(conv_y, wco, mode="nn", out_dtype=F32, name="conv_out_fwd", **big)
    attn = _attn_fwd(proj, tab, sinks, "attn_fwd")
    attn_out = _matmul(attn, wao, mode="nn", out_dtype=F32, name="attn_out_fwd", **big)
    merged = _merge_fwd(proj, conv_out, attn_out, "merge_fwd")
    x1 = _matmul(merged, wo, mode="nn", out_dtype=F32, name="wo_fwd", res=x, **big)
    h2 = _rms_fwd(x1, g_ffn, "rms2_fwd")
    gate, up, act = _gate_up_fwd(h2, wgu_t, "gate_up_fwd")
    x2 = _matmul(act, wd, mode="nn", out_dtype=F32, name="down_fwd", res=x1, tm=1024, tn=512, tk=D_FF)
    dx2, dx2b, dg_final, lossvec = _loss_head(x2, g_final, tgt, "loss_head")
    dgate, dup = _down_bwd_x(dx2b, wd, gate, up, "down_bwd_x")
    g_wd = _matmul(act, dx2b, mode="tn", out_dtype=BF16, name="down_bwd_w", tm=1408, tn=1024, tk=1024)
    dh2 = _matmul([dgate, dup], wgu_t, mode="nn", out_dtype=F32, name="gate_up_bwd_x", tm=1024, tn=1024, tk=1408)
    g_wgu_t = _matmul([dgate, dup], h2, mode="tn", out_dtype=BF16, name="gate_up_bwd_w", tm=1408, tn=1024, tk=1024)
    after_ffn = on_grads("ffn", dict(wgu_t=g_wgu_t, wd=g_wd))
    dx1, dx1b, dg_ffn = _rms_bwd(dh2, x1, g_ffn, dx2, "rms2_bwd")
    dmerged = _matmul(dx1b, wo, mode="nt", out_dtype=F32, name="wo_bwd_x", after=after_ffn, **big)
    g_wo = _matmul(merged, dx1b, mode="tn", out_dtype=BF16, name="wo_bwd_w", tm=512, tn=1024, tk=1024)
    dproj, dco, dao = _merge_bwd(dmerged, proj, conv_out, attn_out, "merge_bwd")
    dconv_y = _matmul(dco, wco, mode="nt", out_dtype=F32, name="conv_out_bwd_x", **big)
    g_wco = _matmul(conv_y, dco, mode="tn", out_dtype=BF16, name="conv_out_bwd_w", tm=512, tn=1024, tk=1024)
    dattn = _matmul(dao, wao, mode="nt", out_dtype=BF16, name="attn_out_bwd_x", **big)
    g_wao = _matmul(attn, dao, mode="tn", out_dtype=BF16, name="attn_out_bwd_w", tm=512, tn=1024, tk=1024)
    after_mix = on_grads("mix", dict(wco=g_wco, wao=g_wao, wo=g_wo))
    dproj, dconv_w = _conv_bwd(dconv_y, proj, conv_w, dproj, "conv_bwd", after=after_mix)
    dproj, dkr, dv, dsinks = _attn_bwd(dattn, proj, tab, sinks, dproj, "attn_bwd")
    dproj = _kv_bwd(dkr, dv, tab, dproj, "kv_bwd")
    g_win_t = _matmul(dproj, h1, mode="tn", out_dtype=BF16, name="proj_bwd_w", tm=512, tn=1024, tk=1024)
    after_in = on_grads("in", dict(win_t=g_win_t))
    dh1 = _matmul(dproj, win_t, mode="nn", out_dtype=F32, name="proj_bwd_x", tm=1024, tn=1024, tk=512, after=after_in)
    dx, _, dg_mix = _rms_bwd(dh1, x, g_mix, dx1, "rms1_bwd")
    grads = dict(win_t=g_win_t, wgu_t=g_wgu_t, wd=g_wd, wco=g_wco, wao=g_wao, wo=g_wo)
    small = dict(g_mix=dg_mix, g_ffn=dg_ffn, g_final=dg_final, conv_w=dconv_w, sinks=dsinks, lossvec=lossvec)
    return dx, grads, small


def _position():
    return lax.axis_index("x"), lax.axis_index("y"), lax.axis_index("c")


def _other_chips(x, y):
    return [(1 - x, y), (x, 1 - y), (1 - x, 1 - y)]


def _all_gather(shards, name):
    n = len(shards)
    rows = [sh.shape[0] for sh in shards]

    def body(*refs):
        ins, outs = refs[:n], refs[n:2 * n]
        send_sems, recv_sems, local_sems = refs[2 * n:]
        x, y, c = _position()
        me, sibling = (x, y, c), (x, y, 1 - c)
        chips = _other_chips(x, y)

        def blk(a, px, py, pc):
            return outs[a].at[pl.ds((4 * px + 2 * py + pc) * rows[a], rows[a]), :]

        def copy(a, k, block, to, src=None):
            return pltpu.make_async_remote_copy(
                src_ref=blk(a, *block) if src is None else src, dst_ref=blk(a, *block),
                send_sem=send_sems.at[a, k], recv_sem=recv_sems.at[a, k], device_id=to, device_id_type=MESH)

        mine = [pltpu.make_async_copy(ins[a], blk(a, *me), local_sems.at[a]) for a in range(n)]
        for cp in mine:
            cp.start()
        first = []
        for a in range(n):
            first.append(copy(a, 0, me, sibling, src=ins[a]))
            first += [copy(a, 1 + j, me, (*chip, c), src=ins[a]) for j, chip in enumerate(chips)]
        for cp in first:
            cp.start()
        passed = []
        for j, chip in enumerate(chips):
            for a in range(n):
                copy(a, 1 + j, (*chip, c), me).wait_recv()
                fwd = copy(a, 4 + j, (*chip, c), sibling)
                fwd.start()
                passed.append(fwd)
        for a in range(n):
            copy(a, 0, sibling, me).wait_recv()
            for j, chip in enumerate(chips):
                copy(a, 4 + j, (*chip, 1 - c), me).wait_recv()
        for cp in first + passed:
            cp.wait_send()
        for cp in mine:
            cp.wait()

    return _call(
        body, name=name, in_specs=[HBM_SPEC] * n, out_specs=[HBM_SPEC] * n,
        out_shape=[_sds((N_DEV * sh.shape[0],) + sh.shape[1:], sh.dtype) for sh in shards],
        scratch_shapes=[pltpu.SemaphoreType.DMA((n, 7)), pltpu.SemaphoreType.DMA((n, 7)), pltpu.SemaphoreType.DMA((n,))],
    )(*shards)


def _whole(ref, nrows):
    return ref.at[pl.ds(0, nrows), :]


def _rs_sibling(grads, name):
    n = len(grads)
    rows = [g.shape[0] // N_DEV for g in grads]

    def body(*refs):
        ins, outs = refs[:n], refs[n:2 * n]
        send_sems, recv_sems = refs[2 * n:]
        x, y, c = _position()
        sibling = (x, y, 1 - c)
        for a in range(n):
            r = rows[a]
            for q in range(4):
                src = ins[a].at[pl.ds((2 * q + (1 - c)) * r, r), :]
                dst = outs[a].at[pl.ds(q * r, r), :]
                pltpu.make_async_remote_copy(src_ref=src, dst_ref=dst, send_sem=send_sems.at[a], recv_sem=recv_sems.at[a],
                                             device_id=sibling, device_id_type=MESH).start()
        for a in range(n):
            allrows = 4 * rows[a]
            pltpu.make_async_remote_copy(
                src_ref=_whole(ins[a], allrows), dst_ref=_whole(outs[a], allrows), send_sem=send_sems.at[a],
                recv_sem=recv_sems.at[a], device_id=sibling, device_id_type=MESH).wait()

    return _call(
        body, name=name, in_specs=[HBM_SPEC] * n, out_specs=[HBM_SPEC] * n,
        out_shape=[_sds((4 * r, g.shape[1]), g.dtype) for g, r in zip(grads, rows)],
        scratch_shapes=[pltpu.SemaphoreType.DMA((n,)), pltpu.SemaphoreType.DMA((n,))],
    )(*grads)


SEM_SPEC = pl.BlockSpec(memory_space=pltpu.SEMAPHORE)
EFFECT = pltpu.SideEffectType.DATAFLOW_SIDE_EFFECTING


def _hbm(a):
    return pltpu.with_memory_space_constraint(a, pltpu.HBM)


def _rs_chips_start(parts, name):
    n = len(parts)
    rows = [p.shape[0] // 4 for p in parts]
    lands = [lax.empty((3 * r, p.shape[1]), p.dtype) for p, r in zip(parts, rows)]

    def body(*refs):
        ins, land_refs = refs[:n], refs[n:2 * n]
        send_sems, recv_sems = refs[2 * n], refs[2 * n + 1]
        token = refs[-1]
        x, y, c = _position()
        for a in range(n):
            r = rows[a]
            for j, (px, py) in enumerate(_other_chips(x, y)):
                src = ins[a].at[pl.ds((2 * px + py) * r, r), :]
                dst = land_refs[a].at[pl.ds(j * r, r), :]
                pltpu.make_async_remote_copy(src_ref=src, dst_ref=dst, send_sem=send_sems.at[a], recv_sem=recv_sems.at[a],
                                             device_id=(px, py, c), device_id_type=MESH).start()
        token[...] = jnp.zeros_like(token)

    outs = _call(
        body, name=name, in_specs=[HBM_SPEC] * (2 * n),
        out_specs=[SEM_SPEC, SEM_SPEC] + [HBM_SPEC] * (2 * n) + [pl.BlockSpec(memory_space=pltpu.VMEM)],
        out_shape=[pltpu.SemaphoreType.DMA((n,)), pltpu.SemaphoreType.DMA((n,))]
        + [pltpu.HBM(p.shape, p.dtype) for p in parts] + [pltpu.HBM(l.shape, l.dtype) for l in lands] + [_sds((8, 128), F32)],
        input_output_aliases={i: 2 + i for i in range(2 * n)},
        compiler_params=pltpu.CompilerParams(has_side_effects=EFFECT),
    )(*[_hbm(p) for p in parts], *[_hbm(l) for l in lands])
    return outs[0], outs[1], list(outs[2:2 + n]), list(outs[2 + n:2 + 2 * n]), outs[-1]


def _rs_chips_wait(send_sems, recv_sems, parts, lands, after, name):
    n = len(parts)
    rows = [p.shape[0] // 4 for p in parts]

    def body(*refs):
        ins, land_refs = refs[:n], refs[n:2 * n]
        send_sems_ref, recv_sems_ref = refs[2 * n], refs[2 * n + 1]
        x, y, c = _position()
        for a in range(n):
            allrows = 3 * rows[a]
            cp = pltpu.make_async_remote_copy(
                src_ref=_whole(ins[a], allrows), dst_ref=_whole(land_refs[a], allrows), send_sem=send_sems_ref.at[a],
                recv_sem=recv_sems_ref.at[a], device_id=(x, y, c), device_id_type=MESH)
            cp.wait_send()
            cp.wait_recv()

    outs = _call(
        body, name=name, in_specs=[HBM_SPEC] * (2 * n) + [SEM_SPEC, SEM_SPEC] + [HBM_SPEC] * len(after),
        out_specs=[HBM_SPEC] * (2 * n),
        out_shape=[pltpu.HBM(p.shape, p.dtype) for p in parts] + [pltpu.HBM(l.shape, l.dtype) for l in lands],
        input_output_aliases={i: i for i in range(2 * n)},
        compiler_params=pltpu.CompilerParams(has_side_effects=EFFECT),
    )(*parts, *lands, send_sems, recv_sems, *after)
    return list(outs[:n]), list(outs[n:])


def _chip_partial(grad, recv, c_idx, name):
    r = recv.shape[0] // 4

    def body(c_ref, g_ref, s_ref, o_ref):
        del c_ref
        o_ref[...] = (g_ref[...].astype(F32) + s_ref[...].astype(F32)).astype(BF16)

    grid_spec = pltpu.PrefetchScalarGridSpec(
        num_scalar_prefetch=1, grid=(4,),
        in_specs=[pl.BlockSpec((r, D), lambda q, c_ref: (2 * q + c_ref[0], 0)), pl.BlockSpec((r, D), lambda q, c_ref: (q, 0))],
        out_specs=pl.BlockSpec((r, D), lambda q, c_ref: (q, 0)))
    return _call(body, name=name, grid_spec=grid_spec, out_shape=_sds((4 * r, D), BF16),
                 compiler_params=_params("parallel"))(c_idx, grad, recv)


def _final_grad(part, recv, q_idx, name):
    r = part.shape[0] // 4
    tr = r // 2 if r % 32 == 0 else r

    def body(q_ref, p_ref, r0_ref, r1_ref, r2_ref, o_ref):
        del q_ref
        o_ref[...] = ((p_ref[...].astype(F32) + r0_ref[...].astype(F32)) + r1_ref[...].astype(F32)) + r2_ref[...].astype(F32)

    nb = r // tr
    grid_spec = pltpu.PrefetchScalarGridSpec(
        num_scalar_prefetch=1, grid=(nb,),
        in_specs=[pl.BlockSpec((tr, D), lambda i, q_ref: (q_ref[0] * nb + i, 0))]
        + [pl.BlockSpec((tr, D), lambda i, q_ref, j=j: (j * nb + i, 0)) for j in range(3)],
        out_specs=pl.BlockSpec((tr, D), lambda i, q_ref: (i, 0)))
    return _call(body, name=name, grid_spec=grid_spec, out_shape=_sds((r, D), F32),
                 compiler_params=_params("parallel"))(q_idx, part, recv, recv, recv)


SMALL_ROWS = 8


def _small_all_reduce(pack, name):
    def body(p_ref, tot_ref, loss_ref, gath, send_sems, recv_sems):
        x, y, c = _position()
        me_id = 4 * x + 2 * y + c
        gath[me_id] = p_ref[...]
        copies = []
        for k in range(1, N_DEV):
            peer = tuple(1 - v if (k >> b) & 1 else v for v, b in ((x, 2), (y, 1), (c, 0)))
            cp = pltpu.make_async_remote_copy(src_ref=p_ref, dst_ref=gath.at[me_id], send_sem=send_sems.at[k - 1],
                                              recv_sem=recv_sems.at[k - 1], device_id=peer, device_id_type=MESH)
            cp.start()
            copies.append(cp)
        for cp in copies:
            cp.wait_recv()
        for cp in copies:
            cp.wait_send()
        tot = gath[0]
        for d in range(1, N_DEV):
            tot = tot + gath[d]
        tot_ref[...] = tot
        loss_ref[...] = jnp.full((1, 128), (0.5 / D) * jnp.sum(tot[SMALL_ROWS - 1:SMALL_ROWS, :]), F32)

    vm = pl.BlockSpec(memory_space=pltpu.VMEM)
    return _call(
        body, name=name, in_specs=[vm], out_specs=[vm, vm],
        out_shape=[_sds((SMALL_ROWS, D), F32), _sds((1, 128), F32)],
        scratch_shapes=[pltpu.VMEM((N_DEV, SMALL_ROWS, D), F32), pltpu.SemaphoreType.DMA((N_DEV - 1,)),
                        pltpu.SemaphoreType.DMA((N_DEV - 1,))],
    )(pack)


def _adamw(w, g, m, v, name):
    r, cdim = w.shape
    tr = 256 if r % 256 == 0 else (r // 2 if r % 16 == 0 else r)

    def body(w_ref, g_ref, m_ref, v_ref, d_ref, nm_ref, nv_ref):
        gv = g_ref[...]
        m2 = B1 * m_ref[...] + (1.0 - B1) * gv
        v2 = B2 * v_ref[...] + (1.0 - B2) * jnp.square(gv)
        m_hat = m2 / (1.0 - B1 ** STEP)
        v_hat = v2 / (1.0 - B2 ** STEP)
        d_ref[...] = -LR * (m_hat / (jnp.sqrt(v_hat) + EPS_ADAM) + WD * w_ref[...])
        nm_ref[...] = m2
        nv_ref[...] = v2

    spec = pl.BlockSpec((tr, cdim), lambda i: (i, 0))
    return _call(
        body, name=name, grid=(r // tr,), in_specs=[spec] * 4, out_specs=[spec] * 3,
        out_shape=[_sds((r, cdim), F32)] * 3, compiler_params=_params("parallel"),
    )(w, g, m, v)


def _cast_bf16(w, name):
    r, cdim = w.shape
    tr = 256 if r % 256 == 0 else r

    def body(w_ref, o_ref):
        o_ref[...] = w_ref[...].astype(BF16)

    spec = pl.BlockSpec((tr, cdim), lambda i: (i, 0))
    return _call(body, name=name, grid=(r // tr,), in_specs=[spec], out_specs=spec, out_shape=_sds((r, cdim), BF16),
                 compiler_params=_params("parallel"))(w)


BIG = ("win_t", "wgu_t", "wd", "wco", "wao", "wo")


def kernel(x, g_mix, w_in, conv_w, attn_sinks, w_conv_out, w_attn_out, w_o, g_ffn, w_gate_up, w_down, g_final, loss_target, m_g_mix, m_w_in, m_conv_w, m_attn_sinks, m_w_conv_out, m_w_attn_out, m_w_o, m_g_ffn, m_w_gate_up, m_w_down, m_g_final, v_g_mix, v_w_in, v_conv_w, v_attn_sinks, v_w_conv_out, v_w_attn_out, v_w_o, v_g_ffn, v_w_gate_up, v_w_down, v_g_final):
    cx, cy, cc = _position()
    c_idx = jnp.reshape(cc, (1,)).astype(jnp.int32)
    q_idx = jnp.reshape(2 * cx + cy, (1,)).astype(jnp.int32)
    me = 4 * cx + 2 * cy + cc

    shards = [
        _cast_bf16(jnp.transpose(w_in[0]), "cast_w_in"), _cast_bf16(jnp.transpose(w_gate_up[0]), "cast_w_gate_up"),
        _cast_bf16(w_down[0], "cast_w_down"), _cast_bf16(w_conv_out[0], "cast_w_conv_out"),
        _cast_bf16(w_attn_out[0], "cast_w_attn_out"), _cast_bf16(w_o[0], "cast_w_o"),
        jnp.pad(conv_w[0], ((0, 5), (0, 0))),
    ]
    full = _all_gather(shards, "all_gather_weights")
    weights = dict(zip(BIG, full[:6]))
    conv_w_full = jnp.transpose(full[6].reshape(N_DEV, 8, 128)[:, :3, :], (1, 0, 2)).reshape(3, D)

    in_flight = {}

    def on_grads(group, gdict):
        keys, glist = list(gdict), list(gdict.values())
        from_sibling = _rs_sibling(glist, "rs_sibling_" + group)
        parts = [_chip_partial(g, r, c_idx, "chip_partial_" + k) for k, g, r in zip(keys, glist, from_sibling)]
        send_sems, recv_sems, parts, lands, token = _rs_chips_start(parts, "rs_chips_start_" + group)
        in_flight[group] = (keys, send_sems, recv_sems, parts, lands)
        return (token,)

    dx, _, small = _local_step(x[0], loss_target[0], g_mix, g_ffn, g_final[None], attn_sinks, conv_w_full,
                               on_grads=on_grads, **weights)

    transposed = ("w_in", "w_gate_up")

    def as2d(k, a):
        if k in transposed:
            return jnp.transpose(a[0])
        return a[None] if a.ndim == 1 else (a[0] if a.ndim == 3 else a)

    w_all = {"g_mix": g_mix, "w_in": w_in, "conv_w": conv_w, "attn_sinks": attn_sinks, "w_conv_out": w_conv_out,
             "w_attn_out": w_attn_out, "w_o": w_o, "g_ffn": g_ffn, "w_gate_up": w_gate_up, "w_down": w_down, "g_final": g_final}
    m_all = {"g_mix": m_g_mix, "w_in": m_w_in, "conv_w": m_conv_w, "attn_sinks": m_attn_sinks, "w_conv_out": m_w_conv_out,
             "w_attn_out": m_w_attn_out, "w_o": m_w_o, "g_ffn": m_g_ffn, "w_gate_up": m_w_gate_up, "w_down": m_w_down,
             "g_final": m_g_final}
    v_all = {"g_mix": v_g_mix, "w_in": v_w_in, "conv_w": v_conv_w, "attn_sinks": v_attn_sinks, "w_conv_out": v_w_conv_out,
             "w_attn_out": v_w_attn_out, "w_o": v_w_o, "g_ffn": v_g_ffn, "w_gate_up": v_w_gate_up, "w_down": v_w_down,
             "g_final": v_g_final}
    results = {}

    def update(k, g):
        d, nm, nv = _adamw(as2d(k, w_all[k]), g, as2d(k, m_all[k]), as2d(k, v_all[k]), "adamw_" + k)
        results[k] = [(jnp.transpose(val) if k in transposed else val).reshape(w_all[k].shape) for val in (g, d, nm, nv)]
        return nm

    sinks_row = jnp.pad(small["sinks"], ((0, 0), (0, D - 128)))
    pack = jnp.concatenate([small["g_mix"], small["g_ffn"], small["g_final"], small["conv_w"], sinks_row, small["lossvec"]], axis=0)
    tot, loss_row = _small_all_reduce(pack, "small_all_reduce")
    loss = loss_row[0, 0]
    g_small = {
        "g_mix": tot[0:1], "g_ffn": tot[1:2], "g_final": tot[2:3],
        "conv_w": lax.dynamic_slice(tot, (3, me * 128), (3, 128)), "attn_sinks": tot[6:7, :N_HEADS],
    }
    after = tuple(update(k, g) for k, g in g_small.items())

    kernel_name = {"win_t": "w_in", "wgu_t": "w_gate_up", "wd": "w_down", "wco": "w_conv_out", "wao": "w_attn_out", "wo": "w_o"}
    for group in ("ffn", "mix", "in"):
        keys, send_sems, recv_sems, parts, lands = in_flight[group]
        parts, lands = _rs_chips_wait(send_sems, recv_sems, parts, lands, after + (dx,), "rs_chips_wait_" + group)
        after = tuple(update(kernel_name[k], _final_grad(p, r, q_idx, "final_grad_" + k)) for k, p, r in zip(keys, parts, lands))

    order = ["g_mix", "w_in", "conv_w", "attn_sinks", "w_conv_out", "w_attn_out", "w_o", "g_ffn", "w_gate_up", "w_down", "g_final"]
    return (loss, dx[None], *[results[k][i] for i in range(4) for k in order])
```

```python
import functools
import math

import jax
import jax.numpy as jnp
from jax import lax
from jax.experimental import pallas as pl
from jax.experimental.pallas import tpu as pltpu

F32 = jnp.float32
BF16 = jnp.bfloat16

D = 1024
HEAD_DIM = 64
N_HEADS = 16
N_KV = 4
GROUP = N_HEADS // N_KV
D_KV = N_KV * HEAD_DIM
BLOCK = 128
ROT_DIM = HEAD_DIM // 4
ROPE_THETA = 500000.0
ATTN_SCALE = 1.0 / math.sqrt(HEAD_DIM)
NEG_INF = -1e30
D_FF = 2816
N_IN = 6656
EPS = 1e-5
C_CB, C_CC, C_CX, C_Q, C_K, C_V, C_GC, C_GA = 0, 1024, 2048, 3072, 4096, 4352, 4608, 5632

LR, B1, B2, EPS_ADAM, WD, STEP = 0.001, 0.9, 0.999, 1e-08, 0.01, 10

N_DEV = 8
MESH = pl.DeviceIdType.MESH
VMEM_LIMIT = 56 * 1024 * 1024

NN = (((1,), (0,)), ((), ()))
NT = (((1,), (1,)), ((), ()))
TN = (((0,), (0,)), ((), ()))
HBM_SPEC = pl.BlockSpec(memory_space=pl.ANY)


def _call(body, **kw):
    return pl.pallas_call(body, **kw)


def _params(*sem):
    return pltpu.CompilerParams(dimension_semantics=sem, vmem_limit_bytes=VMEM_LIMIT)


def _sds(shape, dtype):
    return jax.ShapeDtypeStruct(shape, dtype)


def _matmul(a, b, *, mode, tm, tn, tk, out_dtype, name, res=None, after=()):
    parts = list(a) if isinstance(a, (list, tuple)) else [a]
    rows_a = parts[0].shape[0]
    cols_a = sum(p.shape[1] for p in parts)
    if mode == "nn":
        (m, kk), (_, n), dims = (rows_a, cols_a), b.shape, NN
    elif mode == "nt":
        (m, kk), (n, _), dims = (rows_a, cols_a), b.shape, NT
    else:
        (kk, m), (_, n), dims = (rows_a, cols_a), b.shape, TN
    tm, tn, tk = min(tm, m), min(tn, n), min(tk, kk)
    assert m % tm == 0 and n % tn == 0 and kk % tk == 0, (name, m, n, kk, tm, tn, tk)
    nk = kk // tk
    split_axis, width = (2, tk) if mode == "nn" else (0, tm)
    assert len(parts) == 1 or mode in ("nn", "tn")
    assert all(p.shape[1] % width == 0 for p in parts), (name, width)
    counts = [p.shape[1] // width for p in parts]
    starts = [sum(counts[:p]) for p in range(len(parts))]

    def a_spec(p):
        def col(t):
            return jnp.clip(t - starts[p], 0, counts[p] - 1) if len(parts) > 1 else t

        if mode == "tn":
            return pl.BlockSpec((tk, tm), lambda i, j, k: (k, col(i)))
        return pl.BlockSpec((tm, tk), lambda i, j, k: (i, col(k)))

    if mode == "nt":
        b_spec = pl.BlockSpec((tn, tk), lambda i, j, k: (j, k))
    else:
        b_spec = pl.BlockSpec((tk, tn), lambda i, j, k: (k, j))
    o_spec = pl.BlockSpec((tm, tn), lambda i, j, k: (i, j))
    has_res = res is not None
    n_parts = len(parts)

    def body(*refs):
        a_refs, b_ref = refs[:n_parts], refs[n_parts]
        r_ref = refs[n_parts + 1] if has_res else None
        o_ref = refs[n_parts + 1 + has_res + len(after)]
        k = pl.program_id(2)

        def step(a_ref):
            part = lax.dot_general(a_ref[...], b_ref[...], dims, preferred_element_type=F32)

            def finish(acc):
                if has_res:
                    acc = acc + r_ref[...]
                o_ref[...] = acc.astype(o_ref.dtype)

            if nk == 1:
                finish(part)
            else:
                acc_ref = refs[-1]

                @pl.when(k == 0)
                def _():
                    acc_ref[...] = part

                @pl.when(k > 0)
                def _():
                    acc_ref[...] += part

                @pl.when(k == nk - 1)
                def _():
                    finish(acc_ref[...])

        if n_parts == 1:
            step(a_refs[0])
        else:
            t = pl.program_id(split_axis)
            for p in range(n_parts):
                pl.when((t >= starts[p]) & (t < starts[p] + counts[p]))(functools.partial(step, a_refs[p]))

    ins = parts + [b] + ([res] if has_res else []) + list(after)
    in_specs = [a_spec(p) for p in range(n_parts)] + [b_spec] + ([o_spec] if has_res else []) + [HBM_SPEC] * len(after)
    scratch = [] if nk == 1 else [pltpu.VMEM((tm, tn), F32)]
    return _call(
        body, name=name, grid=(m // tm, n // tn, nk), in_specs=in_specs, out_specs=o_spec,
        out_shape=_sds((m, n), out_dtype), scratch_shapes=scratch,
        compiler_params=_params("parallel", "parallel", "arbitrary"),
    )(*ins)


def _row_tile(s):
    return min(256, s)


def _rms_fwd(x, g, name, after=()):
    s = x.shape[0]
    tm = _row_tile(s)

    def body(x_ref, g_ref, *rest):
        h_ref = rest[-1]
        xv = x_ref[...]
        r = lax.rsqrt(jnp.mean(xv * xv, axis=-1, keepdims=True) + EPS)
        h_ref[...] = (xv * r * g_ref[...]).astype(BF16)

    row = pl.BlockSpec((tm, D), lambda i: (i, 0))
    return _call(
        body, name=name, grid=(s // tm,), in_specs=[row, pl.BlockSpec((1, D), lambda i: (0, 0))] + [HBM_SPEC] * len(after),
        out_specs=row, out_shape=_sds((s, D), BF16), compiler_params=_params("parallel"),
    )(x, g, *after)


def _rms_bwd(dh, x, g, dres, name):
    s = x.shape[0]
    tm = _row_tile(s)

    def body(dh_ref, x_ref, g_ref, dres_ref, dx_ref, dxb_ref, dg_ref):
        xv = x_ref[...]
        r = lax.rsqrt(jnp.mean(xv * xv, axis=-1, keepdims=True) + EPS)
        xh = xv * r
        dhv = dh_ref[...]
        dyg = dhv * g_ref[...]
        dx = dres_ref[...] + r * (dyg - xh * jnp.mean(dyg * xh, axis=-1, keepdims=True))
        dx_ref[...] = dx
        dxb_ref[...] = dx.astype(BF16)
        part = jnp.sum(dhv * xh, axis=0, keepdims=True)

        @pl.when(pl.program_id(0) == 0)
        def _():
            dg_ref[...] = part

        @pl.when(pl.program_id(0) > 0)
        def _():
            dg_ref[...] += part

    row = pl.BlockSpec((tm, D), lambda i: (i, 0))
    vec = pl.BlockSpec((1, D), lambda i: (0, 0))
    return _call(
        body, name=name, grid=(s // tm,), in_specs=[row, row, vec, row], out_specs=[row, row, vec],
        out_shape=[_sds((s, D), F32), _sds((s, D), BF16), _sds((1, D), F32)],
        compiler_params=_params("arbitrary"),
    )(dh, x, g, dres)


def _loss_head(x2, g, tgt, name):
    s = x2.shape[0]
    tm = _row_tile(s)

    def body(x_ref, g_ref, t_ref, dx_ref, dxb_ref, dg_ref, l_ref):
        xv = x_ref[...]
        gv = g_ref[...]
        r = lax.rsqrt(jnp.mean(xv * xv, axis=-1, keepdims=True) + EPS)
        xh = xv * r
        err = xh * gv - t_ref[...]
        dy = err * (1.0 / D)
        dyg = dy * gv
        dx = r * (dyg - xh * jnp.mean(dyg * xh, axis=-1, keepdims=True))
        dx_ref[...] = dx
        dxb_ref[...] = dx.astype(BF16)
        dg_part = jnp.sum(dy * xh, axis=0, keepdims=True)
        l_part = jnp.sum(err * err, axis=0, keepdims=True)

        @pl.when(pl.program_id(0) == 0)
        def _():
            dg_ref[...] = dg_part
            l_ref[...] = l_part

        @pl.when(pl.program_id(0) > 0)
        def _():
            dg_ref[...] += dg_part
            l_ref[...] += l_part

    row = pl.BlockSpec((tm, D), lambda i: (i, 0))
    vec = pl.BlockSpec((1, D), lambda i: (0, 0))
    return _call(
        body, name=name, grid=(s // tm,), in_specs=[row, vec, row], out_specs=[row, row, vec, vec],
        out_shape=[_sds((s, D), F32), _sds((s, D), BF16), _sds((1, D), F32), _sds((1, D), F32)],
        compiler_params=_params("arbitrary"),
    )(x2, g, tgt)


CONV_TC = 256


def _shift_down(u, k, rows):
    return jnp.where(rows >= k, pltpu.roll(u, k, 0), 0.0)


def _shift_up(u, k, rows, s):
    return jnp.where(rows < s - k, pltpu.roll(u, s - k, 0), 0.0)


def _conv_specs(s):
    nb = D // CONV_TC

    def col(c0):
        return pl.BlockSpec((s, CONV_TC), lambda j, c0=c0: (0, c0 // CONV_TC + j))

    return nb, col


def _conv_fwd(proj, conv_w, name):
    s = proj.shape[0]
    nb, col = _conv_specs(s)

    def body(cb_ref, cc_ref, cx_ref, w_ref, y_ref):
        rows = lax.broadcasted_iota(jnp.int32, (s, CONV_TC), 0)
        u = cc_ref[...].astype(F32) * cx_ref[...].astype(F32)
        w = w_ref[...]
        c = w[0:1] * _shift_down(u, 2, rows) + w[1:2] * _shift_down(u, 1, rows) + w[2:3] * u
        y_ref[...] = (cb_ref[...].astype(F32) * c).astype(BF16)

    return _call(
        body, name=name, grid=(nb,),
        in_specs=[col(C_CB), col(C_CC), col(C_CX), pl.BlockSpec((3, CONV_TC), lambda j: (0, j))],
        out_specs=pl.BlockSpec((s, CONV_TC), lambda j: (0, j)), out_shape=_sds((s, D), BF16),
        compiler_params=_params("parallel"),
    )(proj, proj, proj, conv_w)


def _conv_bwd(dy, proj, conv_w, dproj, name, after=()):
    s = proj.shape[0]
    nb, col = _conv_specs(s)

    def body(dy_ref, cb_ref, cc_ref, cx_ref, w_ref, *rest):
        dproj_ref, dw_ref, buf, sems = rest[1 + len(after):]
        j = pl.program_id(0)
        rows = lax.broadcasted_iota(jnp.int32, (s, CONV_TC), 0)
        cc = cc_ref[...].astype(F32)
        cx = cx_ref[...].astype(F32)
        u = cc * cx
        u1 = _shift_down(u, 1, rows)
        u2 = _shift_down(u, 2, rows)
        w = w_ref[...]
        c = w[0:1] * u2 + w[1:2] * u1 + w[2:3] * u
        dyv = dy_ref[...]
        dc = dyv * cb_ref[...].astype(F32)
        du = w[2:3] * dc + w[1:2] * _shift_up(dc, 1, rows, s) + w[0:1] * _shift_up(dc, 2, rows, s)
        buf[0] = (dyv * c).astype(BF16)
        buf[1] = (du * cx).astype(BF16)
        buf[2] = (du * cc).astype(BF16)
        dw_ref[...] = jnp.concatenate(
            [jnp.sum(dc * u2, axis=0, keepdims=True), jnp.sum(dc * u1, axis=0, keepdims=True),
             jnp.sum(dc * u, axis=0, keepdims=True)], axis=0)
        copies = []
        for p, c0 in enumerate((C_CB, C_CC, C_CX)):
            start = pl.multiple_of(c0 + j * CONV_TC, CONV_TC)
            copies.append(pltpu.make_async_copy(buf.at[p], dproj_ref.at[:, pl.ds(start, CONV_TC)], sems.at[p]))
        for cp in copies:
            cp.start()
        for cp in copies:
            cp.wait()

    return _call(
        body, name=name, grid=(nb,),
        in_specs=[pl.BlockSpec((s, CONV_TC), lambda j: (0, j)), col(C_CB), col(C_CC), col(C_CX),
                  pl.BlockSpec((3, CONV_TC), lambda j: (0, j))] + [HBM_SPEC] * (1 + len(after)),
        out_specs=[pl.BlockSpec(memory_space=pl.ANY), pl.BlockSpec((3, CONV_TC), lambda j: (0, j))],
        out_shape=[_sds((s, N_IN), BF16), _sds((3, D), F32)],
        scratch_shapes=[pltpu.VMEM((3, s, CONV_TC), BF16), pltpu.SemaphoreType.DMA((3,))],
        input_output_aliases={5: 0}, compiler_params=_params("arbitrary"),
    )(dy, proj, proj, proj, conv_w, dproj, *after)


def _rope_tables(s):
    inv_freq = ROPE_THETA ** (-jnp.arange(0, ROT_DIM, 2, dtype=F32) / ROT_DIM)
    ang = jnp.arange(s, dtype=F32)[:, None] * inv_freq[None, :]
    cos, sin = jnp.cos(ang), jnp.sin(ang)
    half = ROT_DIM // 2
    ones = jnp.ones((s, HEAD_DIM - ROT_DIM), F32)
    zeros = jnp.zeros((s, HEAD_DIM - ROT_DIM), F32)
    zh = jnp.zeros((s, half), F32)
    c64 = jnp.concatenate([cos, cos, ones], axis=1)
    a64 = jnp.concatenate([-sin, zh, zeros], axis=1)
    b64 = jnp.concatenate([zh, sin, zeros], axis=1)
    return jnp.concatenate([c64, c64, a64, a64, b64, b64], axis=1)


def _rope(x, tab):
    c, a, b = tab[:, 0:128], tab[:, 128:256], tab[:, 256:384]
    outs = []
    for i in range(x.shape[1] // 128):
        xc = x[:, i * 128:(i + 1) * 128]
        outs.append(xc * c + pltpu.roll(xc, 120, 1) * a + pltpu.roll(xc, 8, 1) * b)
    return outs[0] if len(outs) == 1 else jnp.concatenate(outs, axis=1)


def _rope_t(dx, tab):
    c, a, b = tab[:, 0:128], tab[:, 128:256], tab[:, 256:384]
    outs = []
    for i in range(dx.shape[1] // 128):
        dc = dx[:, i * 128:(i + 1) * 128]
        outs.append(dc * c + pltpu.roll(dc * a, 8, 1) + pltpu.roll(dc * b, 120, 1))
    return outs[0] if len(outs) == 1 else jnp.concatenate(outs, axis=1)


def _attn_mask(n):
    qi = lax.broadcasted_iota(jnp.int32, (GROUP * BLOCK, 2 * BLOCK), 0) & (BLOCK - 1)
    kj = lax.broadcasted_iota(jnp.int32, (GROUP * BLOCK, 2 * BLOCK), 1)
    rel = qi + BLOCK - kj
    return (rel >= 0) & (rel < BLOCK) & ((kj >= BLOCK) | (n > 0))


def _sink_col(sink_ref, hk):
    return jnp.concatenate([jnp.full((BLOCK, 1), sink_ref[0, hk * GROUP + g], F32) for g in range(GROUP)], axis=0)


def _attn_in_specs():
    prev = lambda n: jnp.maximum(n - 1, 0)
    return [
        pl.BlockSpec((BLOCK, D), lambda n: (n, C_Q // D)),
        pl.BlockSpec((BLOCK, D_KV), lambda n: (n, C_K // D_KV)),
        pl.BlockSpec((BLOCK, D_KV), lambda n: (prev(n), C_K // D_KV)),
        pl.BlockSpec((BLOCK, D_KV), lambda n: (n, C_V // D_KV)),
        pl.BlockSpec((BLOCK, D_KV), lambda n: (prev(n), C_V // D_KV)),
        pl.BlockSpec((BLOCK, 384), lambda n: (n, 0)),
        pl.BlockSpec((BLOCK, 384), lambda n: (prev(n), 0)),
        pl.BlockSpec(memory_space=pltpu.SMEM),
    ]


def _load_qkv(q_ref, kc_ref, kp_ref, vc_ref, vp_ref, tc_ref, tp_ref):
    q = _rope(q_ref[...].astype(F32), tc_ref[...]).astype(BF16)
    kc = _rope(kc_ref[...].astype(F32), tc_ref[...]).astype(BF16)
    kp = _rope(kp_ref[...].astype(F32), tp_ref[...]).astype(BF16)
    return q, kc, kp, vc_ref[...], vp_ref[...]


def _group_rows(x, hk):
    base = hk * GROUP * HEAD_DIM
    return jnp.concatenate([x[:, base + g * HEAD_DIM: base + (g + 1) * HEAD_DIM] for g in range(GROUP)], axis=0)


def _kv_rows(prev, cur, hk):
    sl = slice(hk * HEAD_DIM, (hk + 1) * HEAD_DIM)
    return jnp.concatenate([prev[:, sl], cur[:, sl]], axis=0)


def _attn_fwd(proj, tab, sinks, name, after=()):
    s = proj.shape[0]

    def body(q_ref, kc_ref, kp_ref, vc_ref, vp_ref, tc_ref, tp_ref, sink_ref, *rest):
        o_ref = rest[-1]
        n = pl.program_id(0)
        q, kc, kp, vc, vp = _load_qkv(q_ref, kc_ref, kp_ref, vc_ref, vp_ref, tc_ref, tp_ref)
        mask = _attn_mask(n)
        for hk in range(N_KV):
            qg = _group_rows(q, hk)
            kcat = _kv_rows(kp, kc, hk)
            vcat = _kv_rows(vp, vc, hk)
            sc = lax.dot_general(qg, kcat, NT, preferred_element_type=F32) * ATTN_SCALE
            sc = jnp.where(mask, sc, NEG_INF)
            sink = _sink_col(sink_ref, hk)
            m = jnp.maximum(jnp.max(sc, axis=1, keepdims=True), sink)
            p = jnp.exp(sc - m)
            inv = 1.0 / (jnp.sum(p, axis=1, keepdims=True) + jnp.exp(sink - m))
            o = lax.dot_general((p * inv).astype(BF16), vcat, NN, preferred_element_type=F32)
            base = hk * GROUP * HEAD_DIM
            for g in range(GROUP):
                o_ref[:, base + g * HEAD_DIM: base + (g + 1) * HEAD_DIM] = o[g * BLOCK:(g + 1) * BLOCK].astype(BF16)

    return _call(
        body, name=name, grid=(s // BLOCK,), in_specs=_attn_in_specs() + [HBM_SPEC] * len(after),
        out_specs=pl.BlockSpec((BLOCK, D), lambda n: (n, 0)), out_shape=_sds((s, D), BF16),
        compiler_params=_params("parallel"),
    )(proj, proj, proj, proj, proj, tab, tab, sinks, *after)


def _attn_bwd(do, proj, tab, sinks, dproj, name):
    s = proj.shape[0]
    nblk = s // BLOCK

    def body(do_ref, q_ref, kc_ref, kp_ref, vc_ref, vp_ref, tc_ref, tp_ref, sink_ref, dproj_in, dproj_ref,
             dk_ref, dv_ref, ds_ref, dqbuf, dqout, dkbuf, dvbuf, sem):
        del dproj_in
        n = pl.program_id(0)

        @pl.when(n == 0)
        def _():
            dk_ref[...] = jnp.zeros_like(dk_ref)
            dv_ref[...] = jnp.zeros_like(dv_ref)
            ds_ref[...] = jnp.zeros_like(ds_ref)

        q, kc, kp, vc, vp = _load_qkv(q_ref, kc_ref, kp_ref, vc_ref, vp_ref, tc_ref, tp_ref)
        dov = do_ref[...]
        mask = _attn_mask(n)
        rows = GROUP * BLOCK
        head_off = lax.broadcasted_iota(jnp.int32, (rows, 128), 1) - (lax.broadcasted_iota(jnp.int32, (rows, 128), 0) >> 7)
        dsink_row = jnp.zeros((1, 128), F32)
        prev0 = pl.multiple_of(jnp.maximum(n - 1, 0) * BLOCK, BLOCK)
        cur0 = pl.multiple_of(n * BLOCK, BLOCK)
        for hk in range(N_KV):
            qg = _group_rows(q, hk)
            dog = _group_rows(dov, hk)
            kcat = _kv_rows(kp, kc, hk)
            vcat = _kv_rows(vp, vc, hk)
            sc = lax.dot_general(qg, kcat, NT, preferred_element_type=F32) * ATTN_SCALE
            sc = jnp.where(mask, sc, NEG_INF)
            sink = _sink_col(sink_ref, hk)
            m = jnp.maximum(jnp.max(sc, axis=1, keepdims=True), sink)
            e = jnp.exp(sc - m)
            es = jnp.exp(sink - m)
            inv = 1.0 / (jnp.sum(e, axis=1, keepdims=True) + es)
            p = e * inv
            pb = p.astype(BF16)
            dp = lax.dot_general(dog, vcat, NT, preferred_element_type=F32)
            delta = jnp.sum(p * dp, axis=1, keepdims=True)
            dsc = (p * (dp - delta) * ATTN_SCALE).astype(BF16)
            dsk = -(es * inv) * delta
            dsink_row = dsink_row + jnp.sum(jnp.where(head_off == hk * GROUP, dsk, 0.0), axis=0, keepdims=True)
            dqg = lax.dot_general(dsc, kcat, NN, preferred_element_type=F32)
            dkcat = lax.dot_general(dsc, qg, TN, preferred_element_type=F32)
            dvcat = lax.dot_general(pb, dog, TN, preferred_element_type=F32)
            base = hk * GROUP * HEAD_DIM
            for g in range(GROUP):
                dqbuf[:, base + g * HEAD_DIM: base + (g + 1) * HEAD_DIM] = dqg[g * BLOCK:(g + 1) * BLOCK]
            sl = slice(hk * HEAD_DIM, (hk + 1) * HEAD_DIM)
            dkbuf[:, sl] = dkcat
            dvbuf[:, sl] = dvcat

        @pl.when(n > 0)
        def _():
            dk_ref[pl.ds(prev0, BLOCK), :] += dkbuf[0:BLOCK, :]
            dv_ref[pl.ds(prev0, BLOCK), :] += dvbuf[0:BLOCK, :]

        dk_ref[pl.ds(cur0, BLOCK), :] += dkbuf[BLOCK:2 * BLOCK, :]
        dv_ref[pl.ds(cur0, BLOCK), :] += dvbuf[BLOCK:2 * BLOCK, :]
        ds_ref[...] += dsink_row
        dqout[...] = _rope_t(dqbuf[...], tc_ref[...]).astype(BF16)
        cp = pltpu.make_async_copy(dqout, dproj_ref.at[pl.ds(cur0, BLOCK), pl.ds(C_Q, D)], sem)
        cp.start()
        cp.wait()

    blk = lambda w: pl.BlockSpec((BLOCK, w), lambda n: (n, 0))
    whole = lambda w: pl.BlockSpec((s, w), lambda n: (0, 0))
    anyspec = pl.BlockSpec(memory_space=pl.ANY)
    n_in = 1 + len(_attn_in_specs())
    return _call(
        body, name=name, grid=(nblk,), in_specs=[blk(D)] + _attn_in_specs() + [anyspec],
        out_specs=[anyspec, whole(D_KV), whole(D_KV), pl.BlockSpec((1, 128), lambda n: (0, 0))],
        out_shape=[_sds((s, N_IN), BF16), _sds((s, D_KV), F32), _sds((s, D_KV), F32), _sds((1, 128), F32)],
        scratch_shapes=[pltpu.VMEM((BLOCK, D), F32), pltpu.VMEM((BLOCK, D), BF16), pltpu.VMEM((2 * BLOCK, D_KV), F32),
                        pltpu.VMEM((2 * BLOCK, D_KV), F32), pltpu.SemaphoreType.DMA(())],
        input_output_aliases={n_in: 0}, compiler_params=_params("arbitrary"),
    )(do, proj, proj, proj, proj, proj, tab, tab, sinks, dproj)


def _kv_bwd(dkr, dv, tab, dproj, name):
    s = dkr.shape[0]
    tm = _row_tile(s)

    def body(dk_ref, dv_ref, t_ref, dproj_in, o_ref):
        del dproj_in
        o_ref[:, 0:D_KV] = _rope_t(dk_ref[...], t_ref[...]).astype(BF16)
        o_ref[:, D_KV:2 * D_KV] = dv_ref[...].astype(BF16)

    row = lambda w: pl.BlockSpec((tm, w), lambda i: (i, 0))
    return _call(
        body, name=name, grid=(s // tm,),
        in_specs=[row(D_KV), row(D_KV), row(384), pl.BlockSpec(memory_space=pl.ANY)],
        out_specs=pl.BlockSpec((tm, 2 * D_KV), lambda i: (i, C_K // (2 * D_KV))),
        out_shape=_sds((s, N_IN), BF16), input_output_aliases={3: 0}, compiler_params=_params("parallel"),
    )(dkr, dv, tab, dproj)


EW_TC = 512


def _sigmoid(x):
    return 0.5 * jnp.tanh(0.5 * x) + 0.5


def _merge_fwd(proj, conv_out, attn_out, name):
    s = proj.shape[0]
    tm = _row_tile(s)
    tile = pl.BlockSpec((tm, EW_TC), lambda i, j: (i, j))

    def body(gc_ref, ga_ref, co_ref, ao_ref, o_ref):
        o_ref[...] = (_sigmoid(gc_ref[...].astype(F32)) * co_ref[...]
                      + _sigmoid(ga_ref[...].astype(F32)) * ao_ref[...]).astype(BF16)

    return _call(
        body, name=name, grid=(s // tm, D // EW_TC),
        in_specs=[pl.BlockSpec((tm, EW_TC), lambda i, j: (i, C_GC // EW_TC + j)),
                  pl.BlockSpec((tm, EW_TC), lambda i, j: (i, C_GA // EW_TC + j)), tile, tile],
        out_specs=tile, out_shape=_sds((s, D), BF16), compiler_params=_params("parallel", "parallel"),
    )(proj, proj, conv_out, attn_out)


def _merge_bwd(dmerged, proj, conv_out, attn_out, name):
    s = proj.shape[0]
    tm = _row_tile(s)
    tile = pl.BlockSpec((tm, EW_TC), lambda i, j: (i, j))
    anyspec = pl.BlockSpec(memory_space=pl.ANY)

    def body(dm_ref, gc_ref, ga_ref, co_ref, ao_ref, dproj_ref, dco_ref, dao_ref, buf, sems):
        i, j = pl.program_id(0), pl.program_id(1)
        dm = dm_ref[...]
        sc = _sigmoid(gc_ref[...].astype(F32))
        sa = _sigmoid(ga_ref[...].astype(F32))
        dco_ref[...] = (dm * sc).astype(BF16)
        dao_ref[...] = (dm * sa).astype(BF16)
        buf[0] = (dm * co_ref[...] * sc * (1.0 - sc)).astype(BF16)
        buf[1] = (dm * ao_ref[...] * sa * (1.0 - sa)).astype(BF16)
        r0 = pl.multiple_of(i * tm, tm)
        copies = []
        for p, c0 in enumerate((C_GC, C_GA)):
            start = pl.multiple_of(c0 + j * EW_TC, EW_TC)
            copies.append(pltpu.make_async_copy(buf.at[p], dproj_ref.at[pl.ds(r0, tm), pl.ds(start, EW_TC)], sems.at[p]))
        for cp in copies:
            cp.start()
        for cp in copies:
            cp.wait()

    return _call(
        body, name=name, grid=(s // tm, D // EW_TC),
        in_specs=[tile, pl.BlockSpec((tm, EW_TC), lambda i, j: (i, C_GC // EW_TC + j)),
                  pl.BlockSpec((tm, EW_TC), lambda i, j: (i, C_GA // EW_TC + j)), tile, tile],
        out_specs=[anyspec, tile, tile],
        out_shape=[_sds((s, N_IN), BF16), _sds((s, D), BF16), _sds((s, D), BF16)],
        scratch_shapes=[pltpu.VMEM((2, tm, EW_TC), BF16), pltpu.SemaphoreType.DMA((2,))],
        compiler_params=_params("arbitrary", "arbitrary"),
    )(dmerged, proj, proj, conv_out, attn_out)


FF_TC = 256


def _gate_up_fwd(h2, wgu_t, name):
    s = h2.shape[0]
    tm = min(1024, s)
    nb = D_FF // FF_TC

    def body(h_ref, wg_ref, wu_ref, g_ref, u_ref, a_ref):
        h = h_ref[...]
        g = lax.dot_general(h, wg_ref[...], NT, preferred_element_type=F32)
        u = lax.dot_general(h, wu_ref[...], NT, preferred_element_type=F32)
        g_ref[...] = g.astype(BF16)
        u_ref[...] = u.astype(BF16)
        a_ref[...] = (g * _sigmoid(g) * u).astype(BF16)

    tile = pl.BlockSpec((tm, FF_TC), lambda i, j: (i, j))
    return _call(
        body, name=name, grid=(s // tm, nb),
        in_specs=[pl.BlockSpec((tm, D), lambda i, j: (i, 0)), pl.BlockSpec((FF_TC, D), lambda i, j: (j, 0)),
                  pl.BlockSpec((FF_TC, D), lambda i, j: (nb + j, 0))],
        out_specs=[tile, tile, tile], out_shape=[_sds((s, D_FF), BF16)] * 3,
        compiler_params=_params("parallel", "parallel"),
    )(h2, wgu_t, wgu_t)


def _down_bwd_x(dx2b, wd, gate, up, name):
    s = dx2b.shape[0]
    tm = min(1024, s)
    nb = D_FF // FF_TC

    def body(dx_ref, w_ref, g_ref, u_ref, dg_ref, du_ref):
        da = lax.dot_general(dx_ref[...], w_ref[...], NT, preferred_element_type=F32)
        g = g_ref[...].astype(F32)
        sg = _sigmoid(g)
        dg_ref[...] = (da * u_ref[...].astype(F32) * (sg * (1.0 + g * (1.0 - sg)))).astype(BF16)
        du_ref[...] = (da * (g * sg)).astype(BF16)

    tile = pl.BlockSpec((tm, FF_TC), lambda i, j: (i, j))
    return _call(
        body, name=name, grid=(s // tm, nb),
        in_specs=[pl.BlockSpec((tm, D), lambda i, j: (i, 0)), pl.BlockSpec((FF_TC, D), lambda i, j: (j, 0)), tile, tile],
        out_specs=[tile, tile], out_shape=[_sds((s, D_FF), BF16)] * 2,
        compiler_params=_params("parallel", "parallel"),
    )(dx2b, wd, gate, up)


class _Weights:
    def __init__(self, **groups):
        self.groups = groups

    def begin(self, group, after):
        return ()

    def end(self, group, after):
        return self.groups[group]


def _local_step(x, tgt, g_mix, g_ffn, g_final, sinks, weights, on_grads=None, after=()):
    on_grads = on_grads or (lambda group, g: ())
    s = x.shape[0]
    tab = _rope_tables(s)
    big = dict(tm=1024, tn=512, tk=1024)
    h1 = _rms_fwd(x, g_mix, "rms1_fwd", after=after)
    win_t, conv_w = weights.end("in", weights.begin("in", (h1,)))
    proj = _matmul(h1, win_t, mode="nt", out_dtype=BF16, name="proj_fwd", **big)
    attn = _attn_fwd(proj, tab, sinks, "attn_fwd", after=weights.begin("mix", (proj,)))
    wco, wao, wo = weights.end("mix", (attn,))
    conv_y = _conv_fwd(proj, conv_w, "conv_fwd")
    conv_out = _matmul(conv_y, wco, mode="nn", out_dtype=F32, name="conv_out_fwd", after=weights.begin("ffn", (attn,)), **big)
    attn_out = _matmul(attn, wao, mode="nn", out_dtype=F32, name="attn_out_fwd", **big)
    merged = _merge_fwd(proj, conv_out, attn_out, "merge_fwd")
    x1 = _matmul(merged, wo, mode="nn", out_dtype=F32, name="wo_fwd", res=x, **big)
    h2 = _rms_fwd(x1, g_ffn, "rms2_fwd")
    wgu_t, wd = weights.end("ffn", (h2,))
    gate, up, act = _gate_up_fwd(h2, wgu_t, "gate_up_fwd")
    x2 = _matmul(act, wd, mode="nn", out_dtype=F32, name="down_fwd", res=x1, tm=1024, tn=512, tk=D_FF)
    dx2, dx2b, dg_final, lossvec = _loss_head(x2, g_final, tgt, "loss_head")
    dgate, dup = _down_bwd_x(dx2b, wd, gate, up, "down_bwd_x")
    g_wd = _matmul(act, dx2b, mode="tn", out_dtype=BF16, name="down_bwd_w", tm=1408, tn=1024, tk=1024)
    dh2 = _matmul([dgate, dup], wgu_t, mode="nn", out_dtype=F32, name="gate_up_bwd_x", tm=1024, tn=1024, tk=1408)
    g_wgu_t = _matmul([dgate, dup], h2, mode="tn", out_dtype=BF16, name="gate_up_bwd_w", tm=1408, tn=1024, tk=1024)
    after_ffn = on_grads("ffn", dict(wgu_t=g_wgu_t, wd=g_wd))
    dx1, dx1b, dg_ffn = _rms_bwd(dh2, x1, g_ffn, dx2, "rms2_bwd")
    dmerged = _matmul(dx1b, wo, mode="nt", out_dtype=F32, name="wo_bwd_x", after=after_ffn, **big)
    g_wo = _matmul(merged, dx1b, mode="tn", out_dtype=BF16, name="wo_bwd_w", tm=512, tn=1024, tk=1024)
    dproj, dco, dao = _merge_bwd(dmerged, proj, conv_out, attn_out, "merge_bwd")
    dconv_y = _matmul(dco, wco, mode="nt", out_dtype=F32, name="conv_out_bwd_x", **big)
    g_wco = _matmul(conv_y, dco, mode="tn", out_dtype=BF16, name="conv_out_bwd_w", tm=512, tn=1024, tk=1024)
    dattn = _matmul(dao, wao, mode="nt", out_dtype=BF16, name="attn_out_bwd_x", **big)
    g_wao = _matmul(attn, dao, mode="tn", out_dtype=BF16, name="attn_out_bwd_w", tm=512, tn=1024, tk=1024)
    after_mix = on_grads("mix", dict(wco=g_wco, wao=g_wao, wo=g_wo))
    dproj, dconv_w = _conv_bwd(dconv_y, proj, conv_w, dproj, "conv_bwd", after=after_mix)
    dproj, dkr, dv, dsinks = _attn_bwd(dattn, proj, tab, sinks, dproj, "attn_bwd")
    dproj = _kv_bwd(dkr, dv, tab, dproj, "kv_bwd")
    g_win_t = _matmul(dproj, h1, mode="tn", out_dtype=BF16, name="proj_bwd_w", tm=512, tn=1024, tk=1024)
    after_in = on_grads("in", dict(win_t=g_win_t))
    dh1 = _matmul(dproj, win_t, mode="nn", out_dtype=F32, name="proj_bwd_x", tm=1024, tn=1024, tk=512, after=after_in)
    dx, _, dg_mix = _rms_bwd(dh1, x, g_mix, dx1, "rms1_bwd")
    grads = dict(win_t=g_win_t, wgu_t=g_wgu_t, wd=g_wd, wco=g_wco, wao=g_wao, wo=g_wo)
    small = dict(g_mix=dg_mix, g_ffn=dg_ffn, g_final=dg_final, conv_w=dconv_w, sinks=dsinks, lossvec=lossvec)
    return dx, grads, small


def _position():
    return lax.axis_index("x"), lax.axis_index("y"), lax.axis_index("c")


def _other_chips(x, y):
    return [(1 - x, y), (x, 1 - y), (1 - x, 1 - y)]


SEM_SPEC = pl.BlockSpec(memory_space=pltpu.SEMAPHORE)
EFFECT = pltpu.SideEffectType.DATAFLOW_SIDE_EFFECTING
TOKEN = jax.ShapeDtypeStruct((8, 128), F32)
TOKEN_SPEC = pl.BlockSpec(memory_space=pltpu.VMEM)


def _hbm(a):
    return pltpu.with_memory_space_constraint(a, pltpu.HBM)


def _place(w, me_idx, dtype, name):
    r, cdim = w.shape

    def body(i_ref, w_ref, o_ref):
        del i_ref
        o_ref[...] = w_ref[...].astype(dtype)

    grid_spec = pltpu.PrefetchScalarGridSpec(
        num_scalar_prefetch=1, grid=(1,), in_specs=[pl.BlockSpec((r, cdim), lambda i, me: (0, 0))],
        out_specs=pl.BlockSpec((r, cdim), lambda i, me: (me[0], 0)))
    return _call(body, name=name, grid_spec=grid_spec, out_shape=_sds((N_DEV * r, cdim), dtype),
                 compiler_params=_params("arbitrary"))(me_idx, w)


def _own_rows(ref, r, px, py, pc):
    return ref.at[pl.ds((4 * px + 2 * py + pc) * r, r), :]


def _gather_start(bufs, groups, name):
    n = len(bufs)
    rows = [b.shape[0] // N_DEV for b in bufs]
    ng = len(groups)

    def body(*refs):
        ins = refs[:n]
        sems = refs[n:n + 2 * ng]
        token = refs[-1]
        x, y, c = _position()
        targets = [(x, y, 1 - c)] + [(*chip, c) for chip in _other_chips(x, y)]
        for g, members in enumerate(groups):
            for slot, a in enumerate(members):
                own = _own_rows(ins[a], rows[a], x, y, c)
                for to in targets:
                    pltpu.make_async_remote_copy(src_ref=own, dst_ref=own, send_sem=sems[2 * g].at[slot],
                                                 recv_sem=sems[2 * g + 1].at[slot], device_id=to, device_id_type=MESH).start()
        token[...] = jnp.zeros_like(token)

    sem_shapes = []
    for members in groups:
        sem_shapes += [pltpu.SemaphoreType.DMA((len(members),))] * 2
    outs = _call(
        body, name=name, in_specs=[HBM_SPEC] * n, out_specs=[SEM_SPEC] * (2 * ng) + [HBM_SPEC] * n + [TOKEN_SPEC],
        out_shape=sem_shapes + [pltpu.HBM(b.shape, b.dtype) for b in bufs] + [TOKEN],
        input_output_aliases={i: 2 * ng + i for i in range(n)},
        compiler_params=pltpu.CompilerParams(has_side_effects=EFFECT),
    )(*[_hbm(b) for b in bufs])
    sem_pairs = [(outs[2 * g], outs[2 * g + 1]) for g in range(ng)]
    return sem_pairs, list(outs[2 * ng:2 * ng + n]), outs[-1]


def _gather_forward(send_sems, recv_sems, bufs, after, name):
    n = len(bufs)
    rows = [b.shape[0] // N_DEV for b in bufs]

    def body(*refs):
        ins = refs[:n]
        send1, recv1 = refs[n], refs[n + 1]
        out0 = n + 2 + len(after)
        send2, recv2 = refs[out0], refs[out0 + 1]
        token = refs[-1]
        x, y, c = _position()
        for a in range(n):
            step1 = pltpu.make_async_remote_copy(
                src_ref=_whole(ins[a], 4 * rows[a]), dst_ref=_whole(ins[a], 4 * rows[a]), send_sem=send1.at[a],
                recv_sem=recv1.at[a], device_id=(x, y, c), device_id_type=MESH)
            step1.wait_send()
            step1.wait_recv()
        for a in range(n):
            for chip in _other_chips(x, y):
                blk = _own_rows(ins[a], rows[a], *chip, c)
                pltpu.make_async_remote_copy(src_ref=blk, dst_ref=blk, send_sem=send2.at[a], recv_sem=recv2.at[a],
                                             device_id=(x, y, 1 - c), device_id_type=MESH).start()
        token[...] = jnp.zeros_like(token)

    outs = _call(
        body, name=name, in_specs=[HBM_SPEC] * n + [SEM_SPEC, SEM_SPEC] + [HBM_SPEC] * len(after),
        out_specs=[SEM_SPEC, SEM_SPEC] + [HBM_SPEC] * n + [TOKEN_SPEC],
        out_shape=[pltpu.SemaphoreType.DMA((n,)), pltpu.SemaphoreType.DMA((n,))]
        + [pltpu.HBM(b.shape, b.dtype) for b in bufs] + [TOKEN],
        input_output_aliases={i: 2 + i for i in range(n)},
        compiler_params=pltpu.CompilerParams(has_side_effects=EFFECT),
    )(*bufs, send_sems, recv_sems, *after)
    return outs[0], outs[1], list(outs[2:2 + n]), outs[-1]


def _gather_done(send_sems, recv_sems, bufs, after, name):
    n = len(bufs)
    rows = [b.shape[0] // N_DEV for b in bufs]

    def body(*refs):
        ins = refs[:n]
        send2, recv2 = refs[n], refs[n + 1]
        x, y, c = _position()
        for a in range(n):
            step2 = pltpu.make_async_remote_copy(
                src_ref=_whole(ins[a], 3 * rows[a]), dst_ref=_whole(ins[a], 3 * rows[a]), send_sem=send2.at[a],
                recv_sem=recv2.at[a], device_id=(x, y, c), device_id_type=MESH)
            step2.wait_send()
            step2.wait_recv()

    outs = _call(
        body, name=name, in_specs=[HBM_SPEC] * n + [SEM_SPEC, SEM_SPEC] + [HBM_SPEC] * len(after),
        out_specs=[HBM_SPEC] * n, out_shape=[pltpu.HBM(b.shape, b.dtype) for b in bufs],
        input_output_aliases={i: i for i in range(n)},
        compiler_params=pltpu.CompilerParams(has_side_effects=EFFECT),
    )(*bufs, send_sems, recv_sems, *after)
    return list(outs)


def _whole(ref, nrows):
    return ref.at[pl.ds(0, nrows), :]


def _rs_sibling(grads, name):
    n = len(grads)
    rows = [g.shape[0] // N_DEV for g in grads]

    def body(*refs):
        ins, outs = refs[:n], refs[n:2 * n]
        send_sems, recv_sems = refs[2 * n:]
        x, y, c = _position()
        sibling = (x, y, 1 - c)
        for a in range(n):
            r = rows[a]
            for q in range(4):
                src = ins[a].at[pl.ds((2 * q + (1 - c)) * r, r), :]
                dst = outs[a].at[pl.ds(q * r, r), :]
                pltpu.make_async_remote_copy(src_ref=src, dst_ref=dst, send_sem=send_sems.at[a], recv_sem=recv_sems.at[a],
                                             device_id=sibling, device_id_type=MESH).start()
        for a in range(n):
            allrows = 4 * rows[a]
            pltpu.make_async_remote_copy(
                src_ref=_whole(ins[a], allrows), dst_ref=_whole(outs[a], allrows), send_sem=send_sems.at[a],
                recv_sem=recv_sems.at[a], device_id=sibling, device_id_type=MESH).wait()

    return _call(
        body, name=name, in_specs=[HBM_SPEC] * n, out_specs=[HBM_SPEC] * n,
        out_shape=[_sds((4 * r, g.shape[1]), g.dtype) for g, r in zip(grads, rows)],
        scratch_shapes=[pltpu.SemaphoreType.DMA((n,)), pltpu.SemaphoreType.DMA((n,))],
    )(*grads)


def _rs_chips_start(parts, name):
    n = len(parts)
    rows = [p.shape[0] // 4 for p in parts]
    lands = [lax.empty((3 * r, p.shape[1]), p.dtype) for p, r in zip(parts, rows)]

    def body(*refs):
        ins, land_refs = refs[:n], refs[n:2 * n]
        send_sems, recv_sems = refs[2 * n], refs[2 * n + 1]
        token = refs[-1]
        x, y, c = _position()
        for a in range(n):
            r = rows[a]
            for j, (px, py) in enumerate(_other_chips(x, y)):
                src = ins[a].at[pl.ds((2 * px + py) * r, r), :]
                dst = land_refs[a].at[pl.ds(j * r, r), :]
                pltpu.make_async_remote_copy(src_ref=src, dst_ref=dst, send_sem=send_sems.at[a], recv_sem=recv_sems.at[a],
                                             device_id=(px, py, c), device_id_type=MESH).start()
        token[...] = jnp.zeros_like(token)

    outs = _call(
        body, name=name, in_specs=[HBM_SPEC] * (2 * n),
        out_specs=[SEM_SPEC, SEM_SPEC] + [HBM_SPEC] * (2 * n) + [pl.BlockSpec(memory_space=pltpu.VMEM)],
        out_shape=[pltpu.SemaphoreType.DMA((n,)), pltpu.SemaphoreType.DMA((n,))]
        + [pltpu.HBM(p.shape, p.dtype) for p in parts] + [pltpu.HBM(l.shape, l.dtype) for l in lands] + [_sds((8, 128), F32)],
        input_output_aliases={i: 2 + i for i in range(2 * n)},
        compiler_params=pltpu.CompilerParams(has_side_effects=EFFECT),
    )(*[_hbm(p) for p in parts], *[_hbm(l) for l in lands])
    return outs[0], outs[1], list(outs[2:2 + n]), list(outs[2 + n:2 + 2 * n]), outs[-1]


def _rs_chips_wait(send_sems, recv_sems, parts, lands, after, name):
    n = len(parts)
    rows = [p.shape[0] // 4 for p in parts]

    def body(*refs):
        ins, land_refs = refs[:n], refs[n:2 * n]
        send_sems_ref, recv_sems_ref = refs[2 * n], refs[2 * n + 1]
        x, y, c = _position()
        for a in range(n):
            allrows = 3 * rows[a]
            cp = pltpu.make_async_remote_copy(
                src_ref=_whole(ins[a], allrows), dst_ref=_whole(land_refs[a], allrows), send_sem=send_sems_ref.at[a],
                recv_sem=recv_sems_ref.at[a], device_id=(x, y, c), device_id_type=MESH)
            cp.wait_send()
            cp.wait_recv()

    outs = _call(
        body, name=name, in_specs=[HBM_SPEC] * (2 * n) + [SEM_SPEC, SEM_SPEC] + [HBM_SPEC] * len(after),
        out_specs=[HBM_SPEC] * (2 * n),
        out_shape=[pltpu.HBM(p.shape, p.dtype) for p in parts] + [pltpu.HBM(l.shape, l.dtype) for l in lands],
        input_output_aliases={i: i for i in range(2 * n)},
        compiler_params=pltpu.CompilerParams(has_side_effects=EFFECT),
    )(*parts, *lands, send_sems, recv_sems, *after)
    return list(outs[:n]), list(outs[n:])


def _chip_partial(grad, recv, c_idx, name):
    r = recv.shape[0] // 4

    def body(c_ref, g_ref, s_ref, o_ref):
        del c_ref
        o_ref[...] = (g_ref[...].astype(F32) + s_ref[...].astype(F32)).astype(BF16)

    grid_spec = pltpu.PrefetchScalarGridSpec(
        num_scalar_prefetch=1, grid=(4,),
        in_specs=[pl.BlockSpec((r, D), lambda q, c_ref: (2 * q + c_ref[0], 0)), pl.BlockSpec((r, D), lambda q, c_ref: (q, 0))],
        out_specs=pl.BlockSpec((r, D), lambda q, c_ref: (q, 0)))
    return _call(body, name=name, grid_spec=grid_spec, out_shape=_sds((4 * r, D), BF16),
                 compiler_params=_params("parallel"))(c_idx, grad, recv)


def _final_grad(part, recv, q_idx, name):
    r = part.shape[0] // 4
    tr = r // 2 if r % 32 == 0 else r

    def body(q_ref, p_ref, r0_ref, r1_ref, r2_ref, o_ref):
        del q_ref
        o_ref[...] = ((p_ref[...].astype(F32) + r0_ref[...].astype(F32)) + r1_ref[...].astype(F32)) + r2_ref[...].astype(F32)

    nb = r // tr
    grid_spec = pltpu.PrefetchScalarGridSpec(
        num_scalar_prefetch=1, grid=(nb,),
        in_specs=[pl.BlockSpec((tr, D), lambda i, q_ref: (q_ref[0] * nb + i, 0))]
        + [pl.BlockSpec((tr, D), lambda i, q_ref, j=j: (j * nb + i, 0)) for j in range(3)],
        out_specs=pl.BlockSpec((tr, D), lambda i, q_ref: (i, 0)))
    return _call(body, name=name, grid_spec=grid_spec, out_shape=_sds((r, D), F32),
                 compiler_params=_params("parallel"))(q_idx, part, recv, recv, recv)


SMALL_ROWS = 8


def _small_all_reduce(pack, name):
    def body(p_ref, tot_ref, loss_ref, gath, send_sems, recv_sems):
        x, y, c = _position()
        me_id = 4 * x + 2 * y + c
        gath[me_id] = p_ref[...]
        copies = []
        for k in range(1, N_DEV):
            peer = tuple(1 - v if (k >> b) & 1 else v for v, b in ((x, 2), (y, 1), (c, 0)))
            cp = pltpu.make_async_remote_copy(src_ref=p_ref, dst_ref=gath.at[me_id], send_sem=send_sems.at[k - 1],
                                              recv_sem=recv_sems.at[k - 1], device_id=peer, device_id_type=MESH)
            cp.start()
            copies.append(cp)
        for cp in copies:
            cp.wait_recv()
        for cp in copies:
            cp.wait_send()
        tot = gath[0]
        for d in range(1, N_DEV):
            tot = tot + gath[d]
        tot_ref[...] = tot
        loss_ref[...] = jnp.full((1, 128), (0.5 / D) * jnp.sum(tot[SMALL_ROWS - 1:SMALL_ROWS, :]), F32)

    vm = pl.BlockSpec(memory_space=pltpu.VMEM)
    return _call(
        body, name=name, in_specs=[vm], out_specs=[vm, vm],
        out_shape=[_sds((SMALL_ROWS, D), F32), _sds((1, 128), F32)],
        scratch_shapes=[pltpu.VMEM((N_DEV, SMALL_ROWS, D), F32), pltpu.SemaphoreType.DMA((N_DEV - 1,)),
                        pltpu.SemaphoreType.DMA((N_DEV - 1,))],
    )(pack)


def _adamw(w, g, m, v, name):
    r, cdim = w.shape
    tr = 256 if r % 256 == 0 else (r // 2 if r % 16 == 0 else r)

    def body(w_ref, g_ref, m_ref, v_ref, d_ref, nm_ref, nv_ref):
        gv = g_ref[...]
        m2 = B1 * m_ref[...] + (1.0 - B1) * gv
        v2 = B2 * v_ref[...] + (1.0 - B2) * jnp.square(gv)
        m_hat = m2 / (1.0 - B1 ** STEP)
        v_hat = v2 / (1.0 - B2 ** STEP)
        d_ref[...] = -LR * (m_hat / (jnp.sqrt(v_hat) + EPS_ADAM) + WD * w_ref[...])
        nm_ref[...] = m2
        nv_ref[...] = v2

    spec = pl.BlockSpec((tr, cdim), lambda i: (i, 0))
    return _call(
        body, name=name, grid=(r // tr,), in_specs=[spec] * 4, out_specs=[spec] * 3,
        out_shape=[_sds((r, cdim), F32)] * 3, compiler_params=_params("parallel"),
    )(w, g, m, v)


def kernel(x, g_mix, w_in, conv_w, attn_sinks, w_conv_out, w_attn_out, w_o, g_ffn, w_gate_up, w_down, g_final, loss_target, m_g_mix, m_w_in, m_conv_w, m_attn_sinks, m_w_conv_out, m_w_attn_out, m_w_o, m_g_ffn, m_w_gate_up, m_w_down, m_g_final, v_g_mix, v_w_in, v_conv_w, v_attn_sinks, v_w_conv_out, v_w_attn_out, v_w_o, v_g_ffn, v_w_gate_up, v_w_down, v_g_final):
    cx, cy, cc = _position()
    c_idx = jnp.reshape(cc, (1,)).astype(jnp.int32)
    q_idx = jnp.reshape(2 * cx + cy, (1,)).astype(jnp.int32)
    me = 4 * cx + 2 * cy + cc

    me_idx = jnp.reshape(me, (1,)).astype(jnp.int32)
    bufs = [
        _place(jnp.transpose(w_in[0]), me_idx, BF16, "place_w_in"), _place(jnp.pad(conv_w[0], ((0, 5), (0, 0))), me_idx, F32, "place_conv_w"),
        _place(w_conv_out[0], me_idx, BF16, "place_w_conv_out"), _place(w_attn_out[0], me_idx, BF16, "place_w_attn_out"),
        _place(w_o[0], me_idx, BF16, "place_w_o"),
        _place(jnp.transpose(w_gate_up[0]), me_idx, BF16, "place_w_gate_up"), _place(w_down[0], me_idx, BF16, "place_w_down"),
    ]
    members = {"in": [0, 1], "mix": [2, 3, 4], "ffn": [5, 6]}
    sem_pairs, bufs, gather_token = _gather_start(bufs, list(members.values()), "gather_start")

    class Gathered:
        def __init__(self):
            self.state = {g: (sem_pairs[i], [bufs[a] for a in members[g]]) for i, g in enumerate(members)}

        def begin(self, group, after):
            (send_sems, recv_sems), group_bufs = self.state[group]
            send2, recv2, group_bufs, token = _gather_forward(send_sems, recv_sems, group_bufs, after, "gather_forward_" + group)
            self.state[group] = ((send2, recv2), group_bufs)
            return (token,)

        def end(self, group, after):
            (send2, recv2), group_bufs = self.state[group]
            full = _gather_done(send2, recv2, group_bufs, after, "gather_done_" + group)
            if group == "in":
                return full[0], jnp.transpose(full[1].reshape(N_DEV, 8, 128)[:, :3, :], (1, 0, 2)).reshape(3, D)
            return full

    in_flight = {}

    def on_grads(group, gdict):
        keys, glist = list(gdict), list(gdict.values())
        from_sibling = _rs_sibling(glist, "rs_sibling_" + group)
        parts = [_chip_partial(g, r, c_idx, "chip_partial_" + k) for k, g, r in zip(keys, glist, from_sibling)]
        send_sems, recv_sems, parts, lands, token = _rs_chips_start(parts, "rs_chips_start_" + group)
        in_flight[group] = (keys, send_sems, recv_sems, parts, lands)
        return (token,)

    dx, _, small = _local_step(x[0], loss_target[0], g_mix, g_ffn, g_final[None], attn_sinks, Gathered(),
                               on_grads=on_grads, after=(gather_token,))

    transposed = ("w_in", "w_gate_up")

    def as2d(k, a):
        if k in transposed:
            return jnp.transpose(a[0])
        return a[None] if a.ndim == 1 else (a[0] if a.ndim == 3 else a)

    w_all = {"g_mix": g_mix, "w_in": w_in, "conv_w": conv_w, "attn_sinks": attn_sinks, "w_conv_out": w_conv_out,
             "w_attn_out": w_attn_out, "w_o": w_o, "g_ffn": g_ffn, "w_gate_up": w_gate_up, "w_down": w_down, "g_final": g_final}
    m_all = {"g_mix": m_g_mix, "w_in": m_w_in, "conv_w": m_conv_w, "attn_sinks": m_attn_sinks, "w_conv_out": m_w_conv_out,
             "w_attn_out": m_w_attn_out, "w_o": m_w_o, "g_ffn": m_g_ffn, "w_gate_up": m_w_gate_up, "w_down": m_w_down,
             "g_final": m_g_final}
    v_all = {"g_mix": v_g_mix, "w_in": v_w_in, "conv_w": v_conv_w, "attn_sinks": v_attn_sinks, "w_conv_out": v_w_conv_out,
             "w_attn_out": v_w_attn_out, "w_o": v_w_o, "g_ffn": v_g_ffn, "w_gate_up": v_w_gate_up, "w_down": v_w_down,
             "g_final": v_g_final}
    results = {}

    def update(k, g):
        d, nm, nv = _adamw(as2d(k, w_all[k]), g, as2d(k, m_all[k]), as2d(k, v_all[k]), "adamw_" + k)
        results[k] = [(jnp.transpose(val) if k in transposed else val).reshape(w_all[k].shape) for val in (g, d, nm, nv)]
        return nm

    sinks_row = jnp.pad(small["sinks"], ((0, 0), (0, D - 128)))
    pack = jnp.concatenate([small["g_mix"], small["g_ffn"], small["g_final"], small["conv_w"], sinks_row, small["lossvec"]], axis=0)
    tot, loss_row = _small_all_reduce(pack, "small_all_reduce")
    loss = loss_row[0, 0]
    g_small = {
        "g_mix": tot[0:1], "g_ffn": tot[1:2], "g_final": tot[2:3],
        "conv_w": lax.dynamic_slice(tot, (3, me * 128), (3, 128)), "attn_sinks": tot[6:7, :N_HEADS],
    }
    after = tuple(update(k, g) for k, g in g_small.items())

    kernel_name = {"win_t": "w_in", "wgu_t": "w_gate_up", "wd": "w_down", "wco": "w_conv_out", "wao": "w_attn_out", "wo": "w_o"}
    for group in ("ffn", "mix", "in"):
        keys, send_sems, recv_sems, parts, lands = in_flight[group]
        parts, lands = _rs_chips_wait(send_sems, recv_sems, parts, lands, after + (dx,), "rs_chips_wait_" + group)
        after = tuple(update(kernel_name[k], _final_grad(p, r, q_idx, "final_grad_" + k)) for k, p, r in zip(keys, parts, lands))

    order = ["g_mix", "w_in", "conv_w", "attn_sinks", "w_conv_out", "w_attn_out", "w_o", "g_ffn", "w_gate_up", "w_down", "g_final"]
    return (loss, dx[None], *[results[k][i] for i in range(4) for k in order])
```

```python
import functools
import math

import jax
import jax.numpy as jnp
from jax import lax
from jax.experimental import pallas as pl
from jax.experimental.pallas import tpu as pltpu

F32 = jnp.float32
BF16 = jnp.bfloat16

D = 1024
HEAD_DIM = 64
N_HEADS = 16
N_KV = 4
GROUP = N_HEADS // N_KV
D_KV = N_KV * HEAD_DIM
BLOCK = 128
ROT_DIM = HEAD_DIM // 4
ROPE_THETA = 500000.0
ATTN_SCALE = 1.0 / math.sqrt(HEAD_DIM)
NEG_INF = -1e30
D_FF = 2816
N_IN = 6656
EPS = 1e-5
C_CB, C_CC, C_CX, C_Q, C_K, C_V, C_GC, C_GA = 0, 1024, 2048, 3072, 4096, 4352, 4608, 5632

LR, B1, B2, EPS_ADAM, WD, STEP = 0.001, 0.9, 0.999, 1e-08, 0.01, 10

N_DEV = 8
MESH = pl.DeviceIdType.MESH
VMEM_LIMIT = 56 * 1024 * 1024

NN = (((1,), (0,)), ((), ()))
NT = (((1,), (1,)), ((), ()))
TN = (((0,), (0,)), ((), ()))
HBM_SPEC = pl.BlockSpec(memory_space=pl.ANY)


def _call(body, **kw):
    return pl.pallas_call(body, **kw)


def _params(*sem):
    return pltpu.CompilerParams(dimension_semantics=sem, vmem_limit_bytes=VMEM_LIMIT)


def _sds(shape, dtype):
    return jax.ShapeDtypeStruct(shape, dtype)


def _matmul(a, b, *, mode, tm, tn, tk, out_dtype, name, res=None, after=()):
    parts = list(a) if isinstance(a, (list, tuple)) else [a]
    rows_a = parts[0].shape[0]
    cols_a = sum(p.shape[1] for p in parts)
    if mode == "nn":
        (m, kk), (_, n), dims = (rows_a, cols_a), b.shape, NN
    elif mode == "nt":
        (m, kk), (n, _), dims = (rows_a, cols_a), b.shape, NT
    else:
        (kk, m), (_, n), dims = (rows_a, cols_a), b.shape, TN
    tm, tn, tk = min(tm, m), min(tn, n), min(tk, kk)
    assert m % tm == 0 and n % tn == 0 and kk % tk == 0, (name, m, n, kk, tm, tn, tk)
    nk = kk // tk
    split_axis, width = (2, tk) if mode == "nn" else (0, tm)
    assert len(parts) == 1 or mode in ("nn", "tn")
    assert len(parts) == 1 or all(p.shape[1] % width == 0 for p in parts), (name, width)
    counts = [p.shape[1] // width for p in parts]
    starts = [sum(counts[:p]) for p in range(len(parts))]

    def a_spec(p):
        def col(t):
            return jnp.clip(t - starts[p], 0, counts[p] - 1) if len(parts) > 1 else t

        if mode == "tn":
            return pl.BlockSpec((tk, tm), lambda i, j, k: (k, col(i)))
        return pl.BlockSpec((tm, tk), lambda i, j, k: (i, col(k)))

    if mode == "nt":
        b_spec = pl.BlockSpec((tn, tk), lambda i, j, k: (j, k))
    else:
        b_spec = pl.BlockSpec((tk, tn), lambda i, j, k: (k, j))
    o_spec = pl.BlockSpec((tm, tn), lambda i, j, k: (i, j))
    has_res = res is not None
    n_parts = len(parts)

    def body(*refs):
        a_refs, b_ref = refs[:n_parts], refs[n_parts]
        r_ref = refs[n_parts + 1] if has_res else None
        o_ref = refs[n_parts + 1 + has_res + len(after)]
        k = pl.program_id(2)

        def step(a_ref):
            part = lax.dot_general(a_ref[...], b_ref[...], dims, preferred_element_type=F32)

            def finish(acc):
                if has_res:
                    acc = acc + r_ref[...]
                o_ref[...] = acc.astype(o_ref.dtype)

            if nk == 1:
                finish(part)
            else:
                acc_ref = refs[-1]

                @pl.when(k == 0)
                def _():
                    acc_ref[...] = part

                @pl.when(k > 0)
                def _():
                    acc_ref[...] += part

                @pl.when(k == nk - 1)
                def _():
                    finish(acc_ref[...])

        if n_parts == 1:
            step(a_refs[0])
        else:
            t = pl.program_id(split_axis)
            for p in range(n_parts):
                pl.when((t >= starts[p]) & (t < starts[p] + counts[p]))(functools.partial(step, a_refs[p]))

    ins = parts + [b] + ([res] if has_res else []) + list(after)
    in_specs = [a_spec(p) for p in range(n_parts)] + [b_spec] + ([o_spec] if has_res else []) + [HBM_SPEC] * len(after)
    scratch = [] if nk == 1 else [pltpu.VMEM((tm, tn), F32)]
    return _call(
        body, name=name, grid=(m // tm, n // tn, nk), in_specs=in_specs, out_specs=o_spec,
        out_shape=_sds((m, n), out_dtype), scratch_shapes=scratch,
        compiler_params=_params("parallel", "parallel", "arbitrary"),
    )(*ins)


def _row_tile(s):
    return min(256, s)


def _rms_fwd(x, g, name, after=()):
    s = x.shape[0]
    tm = _row_tile(s)

    def body(x_ref, g_ref, *rest):
        h_ref = rest[-1]
        xv = x_ref[...]
        r = lax.rsqrt(jnp.mean(xv * xv, axis=-1, keepdims=True) + EPS)
        h_ref[...] = (xv * r * g_ref[...]).astype(BF16)

    row = pl.BlockSpec((tm, D), lambda i: (i, 0))
    return _call(
        body, name=name, grid=(s // tm,), in_specs=[row, pl.BlockSpec((1, D), lambda i: (0, 0))] + [HBM_SPEC] * len(after),
        out_specs=row, out_shape=_sds((s, D), BF16), compiler_params=_params("parallel"),
    )(x, g, *after)


def _rms_bwd(dh, x, g, dres, name):
    s = x.shape[0]
    tm = _row_tile(s)

    def body(dh_ref, x_ref, g_ref, dres_ref, dx_ref, dxb_ref, dg_ref):
        xv = x_ref[...]
        r = lax.rsqrt(jnp.mean(xv * xv, axis=-1, keepdims=True) + EPS)
        xh = xv * r
        dhv = dh_ref[...]
        dyg = dhv * g_ref[...]
        dx = dres_ref[...] + r * (dyg - xh * jnp.mean(dyg * xh, axis=-1, keepdims=True))
        dx_ref[...] = dx
        dxb_ref[...] = dx.astype(BF16)
        part = jnp.sum(dhv * xh, axis=0, keepdims=True)

        @pl.when(pl.program_id(0) == 0)
        def _():
            dg_ref[...] = part

        @pl.when(pl.program_id(0) > 0)
        def _():
            dg_ref[...] += part

    row = pl.BlockSpec((tm, D), lambda i: (i, 0))
    vec = pl.BlockSpec((1, D), lambda i: (0, 0))
    return _call(
        body, name=name, grid=(s // tm,), in_specs=[row, row, vec, row], out_specs=[row, row, vec],
        out_shape=[_sds((s, D), F32), _sds((s, D), BF16), _sds((1, D), F32)],
        compiler_params=_params("arbitrary"),
    )(dh, x, g, dres)


def _loss_head(x2, g, tgt, name):
    s = x2.shape[0]
    tm = _row_tile(s)

    def body(x_ref, g_ref, t_ref, dx_ref, dxb_ref, dg_ref, l_ref):
        xv = x_ref[...]
        gv = g_ref[...]
        r = lax.rsqrt(jnp.mean(xv * xv, axis=-1, keepdims=True) + EPS)
        xh = xv * r
        err = xh * gv - t_ref[...]
        dy = err * (1.0 / D)
        dyg = dy * gv
        dx = r * (dyg - xh * jnp.mean(dyg * xh, axis=-1, keepdims=True))
        dx_ref[...] = dx
        dxb_ref[...] = dx.astype(BF16)
        dg_part = jnp.sum(dy * xh, axis=0, keepdims=True)
        l_part = jnp.sum(err * err, axis=0, keepdims=True)

        @pl.when(pl.program_id(0) == 0)
        def _():
            dg_ref[...] = dg_part
            l_ref[...] = l_part

        @pl.when(pl.program_id(0) > 0)
        def _():
            dg_ref[...] += dg_part
            l_ref[...] += l_part

    row = pl.BlockSpec((tm, D), lambda i: (i, 0))
    vec = pl.BlockSpec((1, D), lambda i: (0, 0))
    return _call(
        body, name=name, grid=(s // tm,), in_specs=[row, vec, row], out_specs=[row, row, vec, vec],
        out_shape=[_sds((s, D), F32), _sds((s, D), BF16), _sds((1, D), F32), _sds((1, D), F32)],
        compiler_params=_params("arbitrary"),
    )(x2, g, tgt)


CONV_TC = 256


def _shift_down(u, k, rows):
    return jnp.where(rows >= k, pltpu.roll(u, k, 0), 0.0)


def _shift_up(u, k, rows, s):
    return jnp.where(rows < s - k, pltpu.roll(u, s - k, 0), 0.0)


def _conv_specs(s):
    nb = D // CONV_TC

    def col(c0):
        return pl.BlockSpec((s, CONV_TC), lambda j, c0=c0: (0, c0 // CONV_TC + j))

    return nb, col


def _conv_fwd(proj, conv_w, name):
    s = proj.shape[0]
    nb, col = _conv_specs(s)

    def body(cb_ref, cc_ref, cx_ref, w_ref, y_ref):
        rows = lax.broadcasted_iota(jnp.int32, (s, CONV_TC), 0)
        u = cc_ref[...].astype(F32) * cx_ref[...].astype(F32)
        w = w_ref[...]
        c = w[0:1] * _shift_down(u, 2, rows) + w[1:2] * _shift_down(u, 1, rows) + w[2:3] * u
        y_ref[...] = (cb_ref[...].astype(F32) * c).astype(BF16)

    return _call(
        body, name=name, grid=(nb,),
        in_specs=[col(C_CB), col(C_CC), col(C_CX), pl.BlockSpec((3, CONV_TC), lambda j: (0, j))],
        out_specs=pl.BlockSpec((s, CONV_TC), lambda j: (0, j)), out_shape=_sds((s, D), BF16),
        compiler_params=_params("parallel"),
    )(proj, proj, proj, conv_w)


def _conv_bwd(dy, proj, conv_w, dproj, name, after=()):
    s = proj.shape[0]
    nb, col = _conv_specs(s)

    def body(dy_ref, cb_ref, cc_ref, cx_ref, w_ref, *rest):
        dproj_ref, dw_ref, buf, sems = rest[1 + len(after):]
        j = pl.program_id(0)
        rows = lax.broadcasted_iota(jnp.int32, (s, CONV_TC), 0)
        cc = cc_ref[...].astype(F32)
        cx = cx_ref[...].astype(F32)
        u = cc * cx
        u1 = _shift_down(u, 1, rows)
        u2 = _shift_down(u, 2, rows)
        w = w_ref[...]
        c = w[0:1] * u2 + w[1:2] * u1 + w[2:3] * u
        dyv = dy_ref[...]
        dc = dyv * cb_ref[...].astype(F32)
        du = w[2:3] * dc + w[1:2] * _shift_up(dc, 1, rows, s) + w[0:1] * _shift_up(dc, 2, rows, s)
        buf[0] = (dyv * c).astype(BF16)
        buf[1] = (du * cx).astype(BF16)
        buf[2] = (du * cc).astype(BF16)
        dw_ref[...] = jnp.concatenate(
            [jnp.sum(dc * u2, axis=0, keepdims=True), jnp.sum(dc * u1, axis=0, keepdims=True),
             jnp.sum(dc * u, axis=0, keepdims=True)], axis=0)
        copies = []
        for p, c0 in enumerate((C_CB, C_CC, C_CX)):
            start = pl.multiple_of(c0 + j * CONV_TC, CONV_TC)
            copies.append(pltpu.make_async_copy(buf.at[p], dproj_ref.at[:, pl.ds(start, CONV_TC)], sems.at[p]))
        for cp in copies:
            cp.start()
        for cp in copies:
            cp.wait()

    return _call(
        body, name=name, grid=(nb,),
        in_specs=[pl.BlockSpec((s, CONV_TC), lambda j: (0, j)), col(C_CB), col(C_CC), col(C_CX),
                  pl.BlockSpec((3, CONV_TC), lambda j: (0, j))] + [HBM_SPEC] * (1 + len(after)),
        out_specs=[pl.BlockSpec(memory_space=pl.ANY), pl.BlockSpec((3, CONV_TC), lambda j: (0, j))],
        out_shape=[_sds((s, N_IN), BF16), _sds((3, D), F32)],
        scratch_shapes=[pltpu.VMEM((3, s, CONV_TC), BF16), pltpu.SemaphoreType.DMA((3,))],
        input_output_aliases={5: 0}, compiler_params=_params("arbitrary"),
    )(dy, proj, proj, proj, conv_w, dproj, *after)


def _rope_tables(s):
    inv_freq = ROPE_THETA ** (-jnp.arange(0, ROT_DIM, 2, dtype=F32) / ROT_DIM)
    ang = jnp.arange(s, dtype=F32)[:, None] * inv_freq[None, :]
    cos, sin = lax.optimization_barrier((jnp.cos(ang), jnp.sin(ang)))
    half = ROT_DIM // 2
    ones = jnp.ones((s, HEAD_DIM - ROT_DIM), F32)
    zeros = jnp.zeros((s, HEAD_DIM - ROT_DIM), F32)
    zh = jnp.zeros((s, half), F32)
    c64 = jnp.concatenate([cos, cos, ones], axis=1)
    a64 = jnp.concatenate([-sin, zh, zeros], axis=1)
    b64 = jnp.concatenate([zh, sin, zeros], axis=1)
    return jnp.concatenate([c64, c64, a64, a64, b64, b64], axis=1)


def _rope(x, tab):
    c, a, b = tab[:, 0:128], tab[:, 128:256], tab[:, 256:384]
    outs = []
    for i in range(x.shape[1] // 128):
        xc = x[:, i * 128:(i + 1) * 128]
        outs.append(xc * c + pltpu.roll(xc, 120, 1) * a + pltpu.roll(xc, 8, 1) * b)
    return outs[0] if len(outs) == 1 else jnp.concatenate(outs, axis=1)


def _rope_t(dx, tab):
    c, a, b = tab[:, 0:128], tab[:, 128:256], tab[:, 256:384]
    outs = []
    for i in range(dx.shape[1] // 128):
        dc = dx[:, i * 128:(i + 1) * 128]
        outs.append(dc * c + pltpu.roll(dc * a, 8, 1) + pltpu.roll(dc * b, 120, 1))
    return outs[0] if len(outs) == 1 else jnp.concatenate(outs, axis=1)


def _attn_mask(n):
    qi = lax.broadcasted_iota(jnp.int32, (GROUP * BLOCK, 2 * BLOCK), 0) & (BLOCK - 1)
    kj = lax.broadcasted_iota(jnp.int32, (GROUP * BLOCK, 2 * BLOCK), 1)
    rel = qi + BLOCK - kj
    return (rel >= 0) & (rel < BLOCK) & ((kj >= BLOCK) | (n > 0))


def _sink_col(sink_ref, hk):
    return jnp.concatenate([jnp.full((BLOCK, 1), sink_ref[0, hk * GROUP + g], F32) for g in range(GROUP)], axis=0)


def _attn_in_specs():
    prev = lambda n: jnp.maximum(n - 1, 0)
    return [
        pl.BlockSpec((BLOCK, D), lambda n: (n, C_Q // D)),
        pl.BlockSpec((BLOCK, D_KV), lambda n: (n, C_K // D_KV)),
        pl.BlockSpec((BLOCK, D_KV), lambda n: (prev(n), C_K // D_KV)),
        pl.BlockSpec((BLOCK, D_KV), lambda n: (n, C_V // D_KV)),
        pl.BlockSpec((BLOCK, D_KV), lambda n: (prev(n), C_V // D_KV)),
        pl.BlockSpec((BLOCK, 384), lambda n: (n, 0)),
        pl.BlockSpec((BLOCK, 384), lambda n: (prev(n), 0)),
        pl.BlockSpec(memory_space=pltpu.SMEM),
    ]


def _load_qkv(q_ref, kc_ref, kp_ref, vc_ref, vp_ref, tc_ref, tp_ref):
    q = _rope(q_ref[...].astype(F32), tc_ref[...]).astype(BF16)
    kc = _rope(kc_ref[...].astype(F32), tc_ref[...]).astype(BF16)
    kp = _rope(kp_ref[...].astype(F32), tp_ref[...]).astype(BF16)
    return q, kc, kp, vc_ref[...], vp_ref[...]


def _group_rows(x, hk):
    base = hk * GROUP * HEAD_DIM
    return jnp.concatenate([x[:, base + g * HEAD_DIM: base + (g + 1) * HEAD_DIM] for g in range(GROUP)], axis=0)


def _kv_rows(prev, cur, hk):
    sl = slice(hk * HEAD_DIM, (hk + 1) * HEAD_DIM)
    return jnp.concatenate([prev[:, sl], cur[:, sl]], axis=0)


def _attn_bwd(do, proj, tab, sinks, dproj, name):
    s = proj.shape[0]
    nblk = s // BLOCK

    def body(do_ref, q_ref, kc_ref, kp_ref, vc_ref, vp_ref, tc_ref, tp_ref, sink_ref, dproj_in, dproj_ref,
             dk_ref, dv_ref, ds_ref, dqbuf, dqout, dkbuf, dvbuf, sem):
        del dproj_in
        n = pl.program_id(0)

        @pl.when(n == 0)
        def _():
            dk_ref[...] = jnp.zeros_like(dk_ref)
            dv_ref[...] = jnp.zeros_like(dv_ref)
            ds_ref[...] = jnp.zeros_like(ds_ref)

        q, kc, kp, vc, vp = _load_qkv(q_ref, kc_ref, kp_ref, vc_ref, vp_ref, tc_ref, tp_ref)
        dov = do_ref[...]
        mask = _attn_mask(n)
        rows = GROUP * BLOCK
        head_off = lax.broadcasted_iota(jnp.int32, (rows, 128), 1) - (lax.broadcasted_iota(jnp.int32, (rows, 128), 0) >> 7)
        dsink_row = jnp.zeros((1, 128), F32)
        prev0 = pl.multiple_of(jnp.maximum(n - 1, 0) * BLOCK, BLOCK)
        cur0 = pl.multiple_of(n * BLOCK, BLOCK)
        for hk in range(N_KV):
            qg = _group_rows(q, hk)
            dog = _group_rows(dov, hk)
            kcat = _kv_rows(kp, kc, hk)
            vcat = _kv_rows(vp, vc, hk)
            sc = lax.dot_general(qg, kcat, NT, preferred_element_type=F32) * ATTN_SCALE
            sc = jnp.where(mask, sc, NEG_INF)
            sink = _sink_col(sink_ref, hk)
            m = jnp.maximum(jnp.max(sc, axis=1, keepdims=True), sink)
            e = jnp.exp(sc - m)
            es = jnp.exp(sink - m)
            inv = 1.0 / (jnp.sum(e, axis=1, keepdims=True) + es)
            p = e * inv
            pb = p.astype(BF16)
            dp = lax.dot_general(dog, vcat, NT, preferred_element_type=F32)
            delta = jnp.sum(p * dp, axis=1, keepdims=True)
            dsc = (p * (dp - delta) * ATTN_SCALE).astype(BF16)
            dsk = -(es * inv) * delta
            dsink_row = dsink_row + jnp.sum(jnp.where(head_off == hk * GROUP, dsk, 0.0), axis=0, keepdims=True)
            dqg = lax.dot_general(dsc, kcat, NN, preferred_element_type=F32)
            dkcat = lax.dot_general(dsc, qg, TN, preferred_element_type=F32)
            dvcat = lax.dot_general(pb, dog, TN, preferred_element_type=F32)
            base = hk * GROUP * HEAD_DIM
            for g in range(GROUP):
                dqbuf[:, base + g * HEAD_DIM: base + (g + 1) * HEAD_DIM] = dqg[g * BLOCK:(g + 1) * BLOCK]
            sl = slice(hk * HEAD_DIM, (hk + 1) * HEAD_DIM)
            dkbuf[:, sl] = dkcat
            dvbuf[:, sl] = dvcat

        @pl.when(n > 0)
        def _():
            dk_ref[pl.ds(prev0, BLOCK), :] += dkbuf[0:BLOCK, :]
            dv_ref[pl.ds(prev0, BLOCK), :] += dvbuf[0:BLOCK, :]

        dk_ref[pl.ds(cur0, BLOCK), :] += dkbuf[BLOCK:2 * BLOCK, :]
        dv_ref[pl.ds(cur0, BLOCK), :] += dvbuf[BLOCK:2 * BLOCK, :]
        ds_ref[...] += dsink_row
        dqout[...] = _rope_t(dqbuf[...], tc_ref[...]).astype(BF16)
        cp = pltpu.make_async_copy(dqout, dproj_ref.at[pl.ds(cur0, BLOCK), pl.ds(C_Q, D)], sem)
        cp.start()
        cp.wait()

    blk = lambda w: pl.BlockSpec((BLOCK, w), lambda n: (n, 0))
    whole = lambda w: pl.BlockSpec((s, w), lambda n: (0, 0))
    anyspec = pl.BlockSpec(memory_space=pl.ANY)
    n_in = 1 + len(_attn_in_specs())
    return _call(
        body, name=name, grid=(nblk,), in_specs=[blk(D)] + _attn_in_specs() + [anyspec],
        out_specs=[anyspec, whole(D_KV), whole(D_KV), pl.BlockSpec((1, 128), lambda n: (0, 0))],
        out_shape=[_sds((s, N_IN), BF16), _sds((s, D_KV), F32), _sds((s, D_KV), F32), _sds((1, 128), F32)],
        scratch_shapes=[pltpu.VMEM((BLOCK, D), F32), pltpu.VMEM((BLOCK, D), BF16), pltpu.VMEM((2 * BLOCK, D_KV), F32),
                        pltpu.VMEM((2 * BLOCK, D_KV), F32), pltpu.SemaphoreType.DMA(())],
        input_output_aliases={n_in: 0}, compiler_params=_params("arbitrary"),
    )(do, proj, proj, proj, proj, proj, tab, tab, sinks, dproj)


HALF = HEAD_DIM
N_CHUNK = D // 128


def _swa_bias(n):
    qi = lax.broadcasted_iota(jnp.int32, (BLOCK, 2 * BLOCK), 0)
    kj = lax.broadcasted_iota(jnp.int32, (BLOCK, 2 * BLOCK), 1)
    rel = qi + BLOCK - kj
    valid = (rel >= 0) & (rel < BLOCK) & ((kj >= BLOCK) | (n > 0))
    return jnp.where(valid, 0.0, NEG_INF)


def _halves(x):
    lo = lax.broadcasted_iota(jnp.int32, x.shape, 1) < HALF
    return jnp.where(lo, x, 0.0).astype(BF16), jnp.where(lo, 0.0, x).astype(BF16)


def _dup_heads(x):
    out = []
    for pair in range(N_KV // 2):
        xc = x[:, pair * 128:(pair + 1) * 128]
        xr = pltpu.roll(xc, HALF, 1)
        lo = lax.broadcasted_iota(jnp.int32, xc.shape, 1) < HALF
        out += [jnp.where(lo, xc, xr), jnp.where(lo, xr, xc)]
    return out


def _swa_load(q_ref, kc_ref, kp_ref, vc_ref, vp_ref, tc_ref, tp_ref):
    qf = _rope(q_ref[...].astype(F32), tc_ref[...]) * ATTN_SCALE
    q_halves = [_halves(qf[:, c * 128:(c + 1) * 128]) for c in range(N_CHUNK)]
    kf = jnp.concatenate([_rope(kp_ref[...].astype(F32), tp_ref[...]), _rope(kc_ref[...].astype(F32), tc_ref[...])], axis=0)
    vf = jnp.concatenate([vp_ref[...], vc_ref[...]], axis=0).astype(F32)
    return q_halves, _dup_heads(kf), _dup_heads(vf)


def _swa_probs(qh, kk, bias, sink):
    s = lax.dot_general(qh, kk, NT, preferred_element_type=F32) + bias
    m = jnp.maximum(jnp.max(jnp.maximum(s[:, :BLOCK], s[:, BLOCK:]), axis=1, keepdims=True), sink)
    return jnp.exp(s - m), m


def _swa_fwd(proj, tab, sinks, name, after=()):
    s = proj.shape[0]

    def body(q_ref, kc_ref, kp_ref, vc_ref, vp_ref, tc_ref, tp_ref, sink_ref, *rest):
        o_ref = rest[-1]
        n = pl.program_id(0)
        q_halves, kdup, vdup = _swa_load(q_ref, kc_ref, kp_ref, vc_ref, vp_ref, tc_ref, tp_ref)
        bias = _swa_bias(n)
        ones = jnp.ones((2 * BLOCK, 128), BF16)
        for c in range(N_CHUNK):
            hk = c // (GROUP // 2)
            kk = kdup[hk].astype(BF16)
            acc = None
            for half, v_half in enumerate(_halves(vdup[hk])):
                sink = sink_ref[0, 2 * c + half]
                e, m = _swa_probs(q_halves[c][half], kk, bias, sink)
                o = lax.dot_general(e.astype(BF16), jnp.concatenate([v_half, ones], axis=1), NN, preferred_element_type=F32)
                part = o[:, :128] * (1.0 / (o[:, 128:] + jnp.exp(sink - m)))
                acc = part if acc is None else acc + part
            o_ref[:, c * 128:(c + 1) * 128] = acc.astype(BF16)

    return _call(
        body, name=name, grid=(s // BLOCK,), in_specs=_attn_in_specs() + [HBM_SPEC] * len(after),
        out_specs=pl.BlockSpec((BLOCK, D), lambda n: (n, 0)), out_shape=_sds((s, D), BF16),
        compiler_params=_params("parallel"),
    )(proj, proj, proj, proj, proj, tab, tab, sinks, *after)


def _kv_bwd(dkr, dv, tab, dproj, name):
    s = dkr.shape[0]
    tm = _row_tile(s)

    def body(dk_ref, dv_ref, t_ref, dproj_in, o_ref):
        del dproj_in
        o_ref[:, 0:D_KV] = _rope_t(dk_ref[...], t_ref[...]).astype(BF16)
        o_ref[:, D_KV:2 * D_KV] = dv_ref[...].astype(BF16)

    row = lambda w: pl.BlockSpec((tm, w), lambda i: (i, 0))
    return _call(
        body, name=name, grid=(s // tm,),
        in_specs=[row(D_KV), row(D_KV), row(384), pl.BlockSpec(memory_space=pl.ANY)],
        out_specs=pl.BlockSpec((tm, 2 * D_KV), lambda i: (i, C_K // (2 * D_KV))),
        out_shape=_sds((s, N_IN), BF16), input_output_aliases={3: 0}, compiler_params=_params("parallel"),
    )(dkr, dv, tab, dproj)


EW_TC = 512


def _sigmoid(x):
    return 0.5 * jnp.tanh(0.5 * x) + 0.5


def _merge_fwd(proj, conv_out, attn_out, name):
    s = proj.shape[0]
    tm = _row_tile(s)
    tile = pl.BlockSpec((tm, EW_TC), lambda i, j: (i, j))

    def body(gc_ref, ga_ref, co_ref, ao_ref, o_ref):
        o_ref[...] = (_sigmoid(gc_ref[...].astype(F32)) * co_ref[...]
                      + _sigmoid(ga_ref[...].astype(F32)) * ao_ref[...]).astype(BF16)

    return _call(
        body, name=name, grid=(s // tm, D // EW_TC),
        in_specs=[pl.BlockSpec((tm, EW_TC), lambda i, j: (i, C_GC // EW_TC + j)),
                  pl.BlockSpec((tm, EW_TC), lambda i, j: (i, C_GA // EW_TC + j)), tile, tile],
        out_specs=tile, out_shape=_sds((s, D), BF16), compiler_params=_params("parallel", "parallel"),
    )(proj, proj, conv_out, attn_out)


def _merge_bwd(dmerged, proj, conv_out, attn_out, name):
    s = proj.shape[0]
    tm = _row_tile(s)
    tile = pl.BlockSpec((tm, EW_TC), lambda i, j: (i, j))
    anyspec = pl.BlockSpec(memory_space=pl.ANY)

    def body(dm_ref, gc_ref, ga_ref, co_ref, ao_ref, dproj_ref, dco_ref, dao_ref, buf, sems):
        i, j = pl.program_id(0), pl.program_id(1)
        dm = dm_ref[...]
        sc = _sigmoid(gc_ref[...].astype(F32))
        sa = _sigmoid(ga_ref[...].astype(F32))
        dco_ref[...] = (dm * sc).astype(BF16)
        dao_ref[...] = (dm * sa).astype(BF16)
        buf[0] = (dm * co_ref[...] * sc * (1.0 - sc)).astype(BF16)
        buf[1] = (dm * ao_ref[...] * sa * (1.0 - sa)).astype(BF16)
        r0 = pl.multiple_of(i * tm, tm)
        copies = []
        for p, c0 in enumerate((C_GC, C_GA)):
            start = pl.multiple_of(c0 + j * EW_TC, EW_TC)
            copies.append(pltpu.make_async_copy(buf.at[p], dproj_ref.at[pl.ds(r0, tm), pl.ds(start, EW_TC)], sems.at[p]))
        for cp in copies:
            cp.start()
        for cp in copies:
            cp.wait()

    return _call(
        body, name=name, grid=(s // tm, D // EW_TC),
        in_specs=[tile, pl.BlockSpec((tm, EW_TC), lambda i, j: (i, C_GC // EW_TC + j)),
                  pl.BlockSpec((tm, EW_TC), lambda i, j: (i, C_GA // EW_TC + j)), tile, tile],
        out_specs=[anyspec, tile, tile],
        out_shape=[_sds((s, N_IN), BF16), _sds((s, D), BF16), _sds((s, D), BF16)],
        scratch_shapes=[pltpu.VMEM((2, tm, EW_TC), BF16), pltpu.SemaphoreType.DMA((2,))],
        compiler_params=_params("arbitrary", "arbitrary"),
    )(dmerged, proj, proj, conv_out, attn_out)


FF_TC = 256


def _gate_up_fwd(h2, wgu_t, name):
    s = h2.shape[0]
    tm = min(2048, s)
    nb = D_FF // FF_TC

    def body(h_ref, wg_ref, wu_ref, g_ref, u_ref, a_ref):
        h = h_ref[...]
        g = lax.dot_general(h, wg_ref[...], NT, preferred_element_type=F32)
        u = lax.dot_general(h, wu_ref[...], NT, preferred_element_type=F32)
        g_ref[...] = g.astype(BF16)
        u_ref[...] = u.astype(BF16)
        a_ref[...] = (g * _sigmoid(g) * u).astype(BF16)

    tile = pl.BlockSpec((tm, FF_TC), lambda i, j: (i, j))
    return _call(
        body, name=name, grid=(s // tm, nb),
        in_specs=[pl.BlockSpec((tm, D), lambda i, j: (i, 0)), pl.BlockSpec((FF_TC, D), lambda i, j: (j, 0)),
                  pl.BlockSpec((FF_TC, D), lambda i, j: (nb + j, 0))],
        out_specs=[tile, tile, tile], out_shape=[_sds((s, D_FF), BF16)] * 3,
        compiler_params=_params("parallel", "parallel"),
    )(h2, wgu_t, wgu_t)


def _down_bwd_x(dx2b, wd, gate, up, name):
    s = dx2b.shape[0]
    tm = min(2048, s)
    nb = D_FF // FF_TC

    def body(dx_ref, w_ref, g_ref, u_ref, dg_ref, du_ref):
        da = lax.dot_general(dx_ref[...], w_ref[...], NT, preferred_element_type=F32)
        g = g_ref[...].astype(F32)
        sg = _sigmoid(g)
        dg_ref[...] = (da * u_ref[...].astype(F32) * (sg * (1.0 + g * (1.0 - sg)))).astype(BF16)
        du_ref[...] = (da * (g * sg)).astype(BF16)

    tile = pl.BlockSpec((tm, FF_TC), lambda i, j: (i, j))
    return _call(
        body, name=name, grid=(s // tm, nb),
        in_specs=[pl.BlockSpec((tm, D), lambda i, j: (i, 0)), pl.BlockSpec((FF_TC, D), lambda i, j: (j, 0)), tile, tile],
        out_specs=[tile, tile], out_shape=[_sds((s, D_FF), BF16)] * 2,
        compiler_params=_params("parallel", "parallel"),
    )(dx2b, wd, gate, up)


class _Weights:
    def __init__(self, **groups):
        self.groups = groups

    def begin(self, group, after):
        return ()

    def end(self, group, after):
        return self.groups[group]


def _local_step(x, tgt, g_mix, g_ffn, g_final, sinks, weights, on_grads=None, after=()):
    on_grads = on_grads or (lambda group, g: ())
    s = x.shape[0]
    tab = _rope_tables(s)
    big = dict(tm=1024, tn=512, tk=1024)
    h1 = _rms_fwd(x, g_mix, "rms1_fwd", after=after)
    win_t, conv_w = weights.end("in", weights.begin("in", (h1,)))
    proj = _matmul(h1, win_t, mode="nt", out_dtype=BF16, name="proj_fwd", tm=2048, tn=512, tk=1024)
    attn = _swa_fwd(proj, tab, sinks, "attn_fwd", after=weights.begin("mix", (proj,)))
    wco, wao, wo = weights.end("mix", (attn,))
    conv_y = _conv_fwd(proj, conv_w, "conv_fwd")
    conv_out = _matmul(conv_y, wco, mode="nn", out_dtype=F32, name="conv_out_fwd", after=weights.begin("ffn", (attn,)), **big)
    attn_out = _matmul(attn, wao, mode="nn", out_dtype=F32, name="attn_out_fwd", **big)
    merged = _merge_fwd(proj, conv_out, attn_out, "merge_fwd")
    x1 = _matmul(merged, wo, mode="nn", out_dtype=F32, name="wo_fwd", res=x, **big)
    h2 = _rms_fwd(x1, g_ffn, "rms2_fwd")
    wgu_t, wd = weights.end("ffn", (h2,))
    gate, up, act = _gate_up_fwd(h2, wgu_t, "gate_up_fwd")
    x2 = _matmul(act, wd, mode="nn", out_dtype=F32, name="down_fwd", res=x1, tm=1024, tn=512, tk=D_FF)
    dx2, dx2b, dg_final, lossvec = _loss_head(x2, g_final, tgt, "loss_head")
    dgate, dup = _down_bwd_x(dx2b, wd, gate, up, "down_bwd_x")
    g_wd = _matmul(act, dx2b, mode="tn", out_dtype=BF16, name="down_bwd_w", tm=1408, tn=1024, tk=2048)
    dh2 = _matmul([dgate, dup], wgu_t, mode="nn", out_dtype=F32, name="gate_up_bwd_x", tm=1024, tn=1024, tk=1408)
    g_wgu_t = _matmul([dgate, dup], h2, mode="tn", out_dtype=BF16, name="gate_up_bwd_w", tm=1408, tn=1024, tk=2048)
    after_ffn = on_grads("ffn", dict(wgu_t=g_wgu_t, wd=g_wd))
    dx1, dx1b, dg_ffn = _rms_bwd(dh2, x1, g_ffn, dx2, "rms2_bwd")
    dmerged = _matmul(dx1b, wo, mode="nt", out_dtype=F32, name="wo_bwd_x", after=after_ffn, **big)
    g_wo = _matmul(merged, dx1b, mode="tn", out_dtype=BF16, name="wo_bwd_w", tm=512, tn=1024, tk=2048)
    dproj, dco, dao = _merge_bwd(dmerged, proj, conv_out, attn_out, "merge_bwd")
    dconv_y = _matmul(dco, wco, mode="nt", out_dtype=F32, name="conv_out_bwd_x", **big)
    g_wco = _matmul(conv_y, dco, mode="tn", out_dtype=BF16, name="conv_out_bwd_w", tm=512, tn=1024, tk=2048)
    dattn = _matmul(dao, wao, mode="nt", out_dtype=BF16, name="attn_out_bwd_x", **big)
    g_wao = _matmul(attn, dao, mode="tn", out_dtype=BF16, name="attn_out_bwd_w", tm=512, tn=1024, tk=2048)
    after_mix = on_grads("mix", dict(wco=g_wco, wao=g_wao, wo=g_wo))
    dproj, dconv_w = _conv_bwd(dconv_y, proj, conv_w, dproj, "conv_bwd", after=after_mix)
    dproj, dkr, dv, dsinks = _attn_bwd(dattn, proj, tab, sinks, dproj, "attn_bwd")
    dproj = _kv_bwd(dkr, dv, tab, dproj, "kv_bwd")
    g_win_t = _matmul(dproj, h1, mode="tn", out_dtype=BF16, name="proj_bwd_w", tm=512, tn=1024, tk=2048)
    after_in = on_grads("in", dict(win_t=g_win_t))
    dh1 = _matmul(dproj, win_t, mode="nn", out_dtype=F32, name="proj_bwd_x", tm=1024, tn=1024, tk=1664, after=after_in)
    dx, _, dg_mix = _rms_bwd(dh1, x, g_mix, dx1, "rms1_bwd")
    grads = dict(win_t=g_win_t, wgu_t=g_wgu_t, wd=g_wd, wco=g_wco, wao=g_wao, wo=g_wo)
    small = dict(g_mix=dg_mix, g_ffn=dg_ffn, g_final=dg_final, conv_w=dconv_w, sinks=dsinks, lossvec=lossvec)
    return dx, grads, small


def _position():
    return lax.axis_index("x"), lax.axis_index("y"), lax.axis_index("c")


def _other_chips(x, y):
    return [(1 - x, y), (x, 1 - y), (1 - x, 1 - y)]


SEM_SPEC = pl.BlockSpec(memory_space=pltpu.SEMAPHORE)
EFFECT = pltpu.SideEffectType.DATAFLOW_SIDE_EFFECTING
TOKEN = jax.ShapeDtypeStruct((8, 128), F32)
TOKEN_SPEC = pl.BlockSpec(memory_space=pltpu.VMEM)


def _hbm(a):
    return pltpu.with_memory_space_constraint(a, pltpu.HBM)


def _place(w, me_idx, dtype, name):
    r, cdim = w.shape

    def body(i_ref, w_ref, o_ref):
        del i_ref
        o_ref[...] = w_ref[...].astype(dtype)

    grid_spec = pltpu.PrefetchScalarGridSpec(
        num_scalar_prefetch=1, grid=(1,), in_specs=[pl.BlockSpec((r, cdim), lambda i, me: (0, 0))],
        out_specs=pl.BlockSpec((r, cdim), lambda i, me: (me[0], 0)))
    return _call(body, name=name, grid_spec=grid_spec, out_shape=_sds((N_DEV * r, cdim), dtype),
                 compiler_params=_params("arbitrary"))(me_idx, w)


def _own_rows(ref, r, px, py, pc):
    return ref.at[pl.ds((4 * px + 2 * py + pc) * r, r), :]


def _gather_start(bufs, groups, name):
    n = len(bufs)
    rows = [b.shape[0] // N_DEV for b in bufs]
    ng = len(groups)

    def body(*refs):
        ins = refs[:n]
        sems = refs[n:n + 2 * ng]
        token = refs[-1]
        x, y, c = _position()
        targets = [(x, y, 1 - c)] + [(*chip, c) for chip in _other_chips(x, y)]
        for g, members in enumerate(groups):
            for slot, a in enumerate(members):
                own = _own_rows(ins[a], rows[a], x, y, c)
                for to in targets:
                    pltpu.make_async_remote_copy(src_ref=own, dst_ref=own, send_sem=sems[2 * g].at[slot],
                                                 recv_sem=sems[2 * g + 1].at[slot], device_id=to, device_id_type=MESH).start()
        token[...] = jnp.zeros_like(token)

    sem_shapes = []
    for members in groups:
        sem_shapes += [pltpu.SemaphoreType.DMA((len(members),))] * 2
    outs = _call(
        body, name=name, in_specs=[HBM_SPEC] * n, out_specs=[SEM_SPEC] * (2 * ng) + [HBM_SPEC] * n + [TOKEN_SPEC],
        out_shape=sem_shapes + [pltpu.HBM(b.shape, b.dtype) for b in bufs] + [TOKEN],
        input_output_aliases={i: 2 * ng + i for i in range(n)},
        compiler_params=pltpu.CompilerParams(has_side_effects=EFFECT),
    )(*[_hbm(b) for b in bufs])
    sem_pairs = [(outs[2 * g], outs[2 * g + 1]) for g in range(ng)]
    return sem_pairs, list(outs[2 * ng:2 * ng + n]), outs[-1]


def _gather_forward(send_sems, recv_sems, bufs, after, name):
    n = len(bufs)
    rows = [b.shape[0] // N_DEV for b in bufs]

    def body(*refs):
        ins = refs[:n]
        send1, recv1 = refs[n], refs[n + 1]
        out0 = n + 2 + len(after)
        send2, recv2 = refs[out0], refs[out0 + 1]
        token = refs[-1]
        x, y, c = _position()
        for a in range(n):
            step1 = pltpu.make_async_remote_copy(
                src_ref=_whole(ins[a], 4 * rows[a]), dst_ref=_whole(ins[a], 4 * rows[a]), send_sem=send1.at[a],
                recv_sem=recv1.at[a], device_id=(x, y, c), device_id_type=MESH)
            step1.wait_send()
            step1.wait_recv()
        for a in range(n):
            for chip in _other_chips(x, y):
                blk = _own_rows(ins[a], rows[a], *chip, c)
                pltpu.make_async_remote_copy(src_ref=blk, dst_ref=blk, send_sem=send2.at[a], recv_sem=recv2.at[a],
                                             device_id=(x, y, 1 - c), device_id_type=MESH).start()
        token[...] = jnp.zeros_like(token)

    outs = _call(
        body, name=name, in_specs=[HBM_SPEC] * n + [SEM_SPEC, SEM_SPEC] + [HBM_SPEC] * len(after),
        out_specs=[SEM_SPEC, SEM_SPEC] + [HBM_SPEC] * n + [TOKEN_SPEC],
        out_shape=[pltpu.SemaphoreType.DMA((n,)), pltpu.SemaphoreType.DMA((n,))]
        + [pltpu.HBM(b.shape, b.dtype) for b in bufs] + [TOKEN],
        input_output_aliases={i: 2 + i for i in range(n)},
        compiler_params=pltpu.CompilerParams(has_side_effects=EFFECT),
    )(*bufs, send_sems, recv_sems, *after)
    return outs[0], outs[1], list(outs[2:2 + n]), outs[-1]


def _gather_done(send_sems, recv_sems, bufs, after, name):
    n = len(bufs)
    rows = [b.shape[0] // N_DEV for b in bufs]

    def body(*refs):
        ins = refs[:n]
        send2, recv2 = refs[n], refs[n + 1]
        x, y, c = _position()
        for a in range(n):
            step2 = pltpu.make_async_remote_copy(
                src_ref=_whole(ins[a], 3 * rows[a]), dst_ref=_whole(ins[a], 3 * rows[a]), send_sem=send2.at[a],
                recv_sem=recv2.at[a], device_id=(x, y, c), device_id_type=MESH)
            step2.wait_send()
            step2.wait_recv()

    outs = _call(
        body, name=name, in_specs=[HBM_SPEC] * n + [SEM_SPEC, SEM_SPEC] + [HBM_SPEC] * len(after),
        out_specs=[HBM_SPEC] * n, out_shape=[pltpu.HBM(b.shape, b.dtype) for b in bufs],
        input_output_aliases={i: i for i in range(n)},
        compiler_params=pltpu.CompilerParams(has_side_effects=EFFECT),
    )(*bufs, send_sems, recv_sems, *after)
    return list(outs)


def _whole(ref, nrows):
    return ref.at[pl.ds(0, nrows), :]


def _rs_sibling(grads, name):
    n = len(grads)
    rows = [g.shape[0] // N_DEV for g in grads]

    def body(*refs):
        ins, outs = refs[:n], refs[n:2 * n]
        send_sems, recv_sems = refs[2 * n:]
        x, y, c = _position()
        sibling = (x, y, 1 - c)
        for a in range(n):
            r = rows[a]
            for q in range(4):
                src = ins[a].at[pl.ds((2 * q + (1 - c)) * r, r), :]
                dst = outs[a].at[pl.ds(q * r, r), :]
                pltpu.make_async_remote_copy(src_ref=src, dst_ref=dst, send_sem=send_sems.at[a], recv_sem=recv_sems.at[a],
                                             device_id=sibling, device_id_type=MESH).start()
        for a in range(n):
            allrows = 4 * rows[a]
            pltpu.make_async_remote_copy(
                src_ref=_whole(ins[a], allrows), dst_ref=_whole(outs[a], allrows), send_sem=send_sems.at[a],
                recv_sem=recv_sems.at[a], device_id=sibling, device_id_type=MESH).wait()

    return _call(
        body, name=name, in_specs=[HBM_SPEC] * n, out_specs=[HBM_SPEC] * n,
        out_shape=[_sds((4 * r, g.shape[1]), g.dtype) for g, r in zip(grads, rows)],
        scratch_shapes=[pltpu.SemaphoreType.DMA((n,)), pltpu.SemaphoreType.DMA((n,))],
    )(*grads)


def _rs_chips_start(parts, name):
    n = len(parts)
    rows = [p.shape[0] // 4 for p in parts]
    lands = [lax.empty((3 * r, p.shape[1]), p.dtype) for p, r in zip(parts, rows)]

    def body(*refs):
        ins, land_refs = refs[:n], refs[n:2 * n]
        send_sems, recv_sems = refs[2 * n], refs[2 * n + 1]
        token = refs[-1]
        x, y, c = _position()
        for a in range(n):
            r = rows[a]
            for j, (px, py) in enumerate(_other_chips(x, y)):
                src = ins[a].at[pl.ds((2 * px + py) * r, r), :]
                dst = land_refs[a].at[pl.ds(j * r, r), :]
                pltpu.make_async_remote_copy(src_ref=src, dst_ref=dst, send_sem=send_sems.at[a], recv_sem=recv_sems.at[a],
                                             device_id=(px, py, c), device_id_type=MESH).start()
        token[...] = jnp.zeros_like(token)

    outs = _call(
        body, name=name, in_specs=[HBM_SPEC] * (2 * n),
        out_specs=[SEM_SPEC, SEM_SPEC] + [HBM_SPEC] * (2 * n) + [pl.BlockSpec(memory_space=pltpu.VMEM)],
        out_shape=[pltpu.SemaphoreType.DMA((n,)), pltpu.SemaphoreType.DMA((n,))]
        + [pltpu.HBM(p.shape, p.dtype) for p in parts] + [pltpu.HBM(l.shape, l.dtype) for l in lands] + [_sds((8, 128), F32)],
        input_output_aliases={i: 2 + i for i in range(2 * n)},
        compiler_params=pltpu.CompilerParams(has_side_effects=EFFECT),
    )(*[_hbm(p) for p in parts], *[_hbm(l) for l in lands])
    return outs[0], outs[1], list(outs[2:2 + n]), list(outs[2 + n:2 + 2 * n]), outs[-1]


def _rs_chips_wait(send_sems, recv_sems, parts, lands, after, name):
    n = len(parts)
    rows = [p.shape[0] // 4 for p in parts]

    def body(*refs):
        ins, land_refs = refs[:n], refs[n:2 * n]
        send_sems_ref, recv_sems_ref = refs[2 * n], refs[2 * n + 1]
        x, y, c = _position()
        for a in range(n):
            allrows = 3 * rows[a]
            cp = pltpu.make_async_remote_copy(
                src_ref=_whole(ins[a], allrows), dst_ref=_whole(land_refs[a], allrows), send_sem=send_sems_ref.at[a],
                recv_sem=recv_sems_ref.at[a], device_id=(x, y, c), device_id_type=MESH)
            cp.wait_send()
            cp.wait_recv()

    outs = _call(
        body, name=name, in_specs=[HBM_SPEC] * (2 * n) + [SEM_SPEC, SEM_SPEC] + [HBM_SPEC] * len(after),
        out_specs=[HBM_SPEC] * (2 * n),
        out_shape=[pltpu.HBM(p.shape, p.dtype) for p in parts] + [pltpu.HBM(l.shape, l.dtype) for l in lands],
        input_output_aliases={i: i for i in range(2 * n)},
        compiler_params=pltpu.CompilerParams(has_side_effects=EFFECT),
    )(*parts, *lands, send_sems, recv_sems, *after)
    return list(outs[:n]), list(outs[n:])


def _chip_partial(grad, recv, c_idx, name):
    r = recv.shape[0] // 4

    def body(c_ref, g_ref, s_ref, o_ref):
        del c_ref
        o_ref[...] = (g_ref[...].astype(F32) + s_ref[...].astype(F32)).astype(BF16)

    grid_spec = pltpu.PrefetchScalarGridSpec(
        num_scalar_prefetch=1, grid=(4,),
        in_specs=[pl.BlockSpec((r, D), lambda q, c_ref: (2 * q + c_ref[0], 0)), pl.BlockSpec((r, D), lambda q, c_ref: (q, 0))],
        out_specs=pl.BlockSpec((r, D), lambda q, c_ref: (q, 0)))
    return _call(body, name=name, grid_spec=grid_spec, out_shape=_sds((4 * r, D), BF16),
                 compiler_params=_params("parallel"))(c_idx, grad, recv)


def _adamw_math(w, g, m, v):
    m2 = B1 * m + (1.0 - B1) * g
    v2 = B2 * v + (1.0 - B2) * jnp.square(g)
    m_hat = m2 / (1.0 - B1 ** STEP)
    v_hat = v2 / (1.0 - B2 ** STEP)
    return -LR * (m_hat / (jnp.sqrt(v_hat) + EPS_ADAM) + WD * w), m2, v2


def _reduce_adamw(w, part, recv, q_idx, m, v, name):
    r = w.shape[0]
    assert part.shape == (4 * r, D) and recv.shape == (3 * r, D) and w.shape == (r, D)
    tr = r // 2
    nb = r // tr

    def body(q_ref, w_ref, p_ref, r0_ref, r1_ref, r2_ref, m_ref, v_ref, g_ref, d_ref, nm_ref, nv_ref):
        del q_ref
        g = ((p_ref[...].astype(F32) + r0_ref[...].astype(F32)) + r1_ref[...].astype(F32)) + r2_ref[...].astype(F32)
        g_ref[...] = g
        d_ref[...], nm_ref[...], nv_ref[...] = _adamw_math(w_ref[...], g, m_ref[...], v_ref[...])

    own = pl.BlockSpec((tr, D), lambda i, q_ref: (i, 0))
    grid_spec = pltpu.PrefetchScalarGridSpec(
        num_scalar_prefetch=1, grid=(nb,),
        in_specs=[own, pl.BlockSpec((tr, D), lambda i, q_ref: (q_ref[0] * nb + i, 0))]
        + [pl.BlockSpec((tr, D), lambda i, q_ref, j=j: (j * nb + i, 0)) for j in range(3)] + [own, own],
        out_specs=[own] * 4)
    return _call(body, name=name, grid_spec=grid_spec, out_shape=[_sds((r, D), F32)] * 4,
                 compiler_params=_params("parallel"))(q_idx, w, part, recv, recv, recv, m, v)


SMALL_ROWS = 8


def _small_all_reduce(pack, name):
    def body(p_ref, tot_ref, loss_ref, gath, send_sems, recv_sems):
        x, y, c = _position()
        me_id = 4 * x + 2 * y + c
        gath[me_id] = p_ref[...]
        copies = []
        for k in range(1, N_DEV):
            peer = tuple(1 - v if (k >> b) & 1 else v for v, b in ((x, 2), (y, 1), (c, 0)))
            cp = pltpu.make_async_remote_copy(src_ref=p_ref, dst_ref=gath.at[me_id], send_sem=send_sems.at[k - 1],
                                              recv_sem=recv_sems.at[k - 1], device_id=peer, device_id_type=MESH)
            cp.start()
            copies.append(cp)
        for cp in copies:
            cp.wait_recv()
        for cp in copies:
            cp.wait_send()
        tot = gath[0]
        for d in range(1, N_DEV):
            tot = tot + gath[d]
        tot_ref[...] = tot
        loss_ref[...] = jnp.full((1, 128), (0.5 / D) * jnp.sum(tot[SMALL_ROWS - 1:SMALL_ROWS, :]), F32)

    vm = pl.BlockSpec(memory_space=pltpu.VMEM)
    return _call(
        body, name=name, in_specs=[vm], out_specs=[vm, vm],
        out_shape=[_sds((SMALL_ROWS, D), F32), _sds((1, 128), F32)],
        scratch_shapes=[pltpu.VMEM((N_DEV, SMALL_ROWS, D), F32), pltpu.SemaphoreType.DMA((N_DEV - 1,)),
                        pltpu.SemaphoreType.DMA((N_DEV - 1,))],
    )(pack)


def _adamw(w, g, m, v, name):
    r, cdim = w.shape
    tr = 256 if r % 256 == 0 else (r // 2 if r % 16 == 0 else r)

    def body(w_ref, g_ref, m_ref, v_ref, d_ref, nm_ref, nv_ref):
        d_ref[...], nm_ref[...], nv_ref[...] = _adamw_math(w_ref[...], g_ref[...], m_ref[...], v_ref[...])

    spec = pl.BlockSpec((tr, cdim), lambda i: (i, 0))
    return _call(
        body, name=name, grid=(r // tr,), in_specs=[spec] * 4, out_specs=[spec] * 3,
        out_shape=[_sds((r, cdim), F32)] * 3, compiler_params=_params("parallel"),
    )(w, g, m, v)


def kernel(x, g_mix, w_in, conv_w, attn_sinks, w_conv_out, w_attn_out, w_o, g_ffn, w_gate_up, w_down, g_final, loss_target, m_g_mix, m_w_in, m_conv_w, m_attn_sinks, m_w_conv_out, m_w_attn_out, m_w_o, m_g_ffn, m_w_gate_up, m_w_down, m_g_final, v_g_mix, v_w_in, v_conv_w, v_attn_sinks, v_w_conv_out, v_w_attn_out, v_w_o, v_g_ffn, v_w_gate_up, v_w_down, v_g_final):
    cx, cy, cc = _position()
    c_idx = jnp.reshape(cc, (1,)).astype(jnp.int32)
    q_idx = jnp.reshape(2 * cx + cy, (1,)).astype(jnp.int32)
    me = 4 * cx + 2 * cy + cc

    me_idx = jnp.reshape(me, (1,)).astype(jnp.int32)
    bufs = [
        _place(jnp.transpose(w_in[0]), me_idx, BF16, "place_w_in"), _place(jnp.pad(conv_w[0], ((0, 5), (0, 0))), me_idx, F32, "place_conv_w"),
        _place(w_conv_out[0], me_idx, BF16, "place_w_conv_out"), _place(w_attn_out[0], me_idx, BF16, "place_w_attn_out"),
        _place(w_o[0], me_idx, BF16, "place_w_o"),
        _place(jnp.transpose(w_gate_up[0]), me_idx, BF16, "place_w_gate_up"), _place(w_down[0], me_idx, BF16, "place_w_down"),
    ]
    members = {"in": [0, 1], "mix": [2, 3, 4], "ffn": [5, 6]}
    sem_pairs, bufs, gather_token = _gather_start(bufs, list(members.values()), "gather_start")

    class Gathered:
        def __init__(self):
            self.state = {g: (sem_pairs[i], [bufs[a] for a in members[g]]) for i, g in enumerate(members)}

        def begin(self, group, after):
            (send_sems, recv_sems), group_bufs = self.state[group]
            send2, recv2, group_bufs, token = _gather_forward(send_sems, recv_sems, group_bufs, after, "gather_forward_" + group)
            self.state[group] = ((send2, recv2), group_bufs)
            return (token,)

        def end(self, group, after):
            (send2, recv2), group_bufs = self.state[group]
            full = _gather_done(send2, recv2, group_bufs, after, "gather_done_" + group)
            if group == "in":
                return full[0], jnp.transpose(full[1].reshape(N_DEV, 8, 128)[:, :3, :], (1, 0, 2)).reshape(3, D)
            return full

    in_flight = {}

    def on_grads(group, gdict):
        keys, glist = list(gdict), list(gdict.values())
        from_sibling = _rs_sibling(glist, "rs_sibling_" + group)
        parts = [_chip_partial(g, r, c_idx, "chip_partial_" + k) for k, g, r in zip(keys, glist, from_sibling)]
        send_sems, recv_sems, parts, lands, token = _rs_chips_start(parts, "rs_chips_start_" + group)
        in_flight[group] = (keys, send_sems, recv_sems, parts, lands)
        return (token,)

    dx, _, small = _local_step(x[0], loss_target[0], g_mix, g_ffn, g_final[None], attn_sinks, Gathered(),
                               on_grads=on_grads, after=(gather_token,))

    transposed = ("w_in", "w_gate_up")

    def as2d(k, a):
        if k in transposed:
            return jnp.transpose(a[0])
        return a[None] if a.ndim == 1 else (a[0] if a.ndim == 3 else a)

    w_all = {"g_mix": g_mix, "w_in": w_in, "conv_w": conv_w, "attn_sinks": attn_sinks, "w_conv_out": w_conv_out,
             "w_attn_out": w_attn_out, "w_o": w_o, "g_ffn": g_ffn, "w_gate_up": w_gate_up, "w_down": w_down, "g_final": g_final}
    m_all = {"g_mix": m_g_mix, "w_in": m_w_in, "conv_w": m_conv_w, "attn_sinks": m_attn_sinks, "w_conv_out": m_w_conv_out,
             "w_attn_out": m_w_attn_out, "w_o": m_w_o, "g_ffn": m_g_ffn, "w_gate_up": m_w_gate_up, "w_down": m_w_down,
             "g_final": m_g_final}
    v_all = {"g_mix": v_g_mix, "w_in": v_w_in, "conv_w": v_conv_w, "attn_sinks": v_attn_sinks, "w_conv_out": v_w_conv_out,
             "w_attn_out": v_w_attn_out, "w_o": v_w_o, "g_ffn": v_g_ffn, "w_gate_up": v_w_gate_up, "w_down": v_w_down,
             "g_final": v_g_final}
    results = {}

    def update(k, g=None, part=None, recv=None):
        w2, m2, v2 = as2d(k, w_all[k]), as2d(k, m_all[k]), as2d(k, v_all[k])
        if g is None:
            g, d, nm, nv = _reduce_adamw(w2, part, recv, q_idx, m2, v2, "adamw_" + k)
        else:
            d, nm, nv = _adamw(w2, g, m2, v2, "adamw_" + k)
        results[k] = [(jnp.transpose(val) if k in transposed else val).reshape(w_all[k].shape) for val in (g, d, nm, nv)]
        return nm

    sinks_row = jnp.pad(small["sinks"], ((0, 0), (0, D - 128)))
    pack = jnp.concatenate([small["g_mix"], small["g_ffn"], small["g_final"], small["conv_w"], sinks_row, small["lossvec"]], axis=0)
    tot, loss_row = _small_all_reduce(pack, "small_all_reduce")
    loss = loss_row[0, 0]
    g_small = {
        "g_mix": tot[0:1], "g_ffn": tot[1:2], "g_final": tot[2:3],
        "conv_w": lax.dynamic_slice(tot, (3, me * 128), (3, 128)), "attn_sinks": tot[6:7, :N_HEADS],
    }
    after = tuple(update(k, g) for k, g in g_small.items())

    kernel_name = {"win_t": "w_in", "wgu_t": "w_gate_up", "wd": "w_down", "wco": "w_conv_out", "wao": "w_attn_out", "wo": "w_o"}
    for group in ("ffn", "mix", "in"):
        keys, send_sems, recv_sems, parts, lands = in_flight[group]
        parts, lands = _rs_chips_wait(send_sems, recv_sems, parts, lands, after + (dx,), "rs_chips_wait_" + group)
        after = tuple(update(kernel_name[k], part=p, recv=r) for k, p, r in zip(keys, parts, lands))

    order = ["g_mix", "w_in", "conv_w", "attn_sinks", "w_conv_out", "w_attn_out", "w_o", "g_ffn", "w_gate_up", "w_down", "g_final"]
    return (loss, dx[None], *[results[k][i] for i in range(4) for k in order])
```

```python
import functools
import math

import jax
import jax.numpy as jnp
from jax import lax
from jax.experimental import pallas as pl
from jax.experimental.pallas import tpu as pltpu

F32 = jnp.float32
BF16 = jnp.bfloat16

D = 1024
HEAD_DIM = 64
N_HEADS = 16
N_KV = 4
GROUP = N_HEADS // N_KV
D_KV = N_KV * HEAD_DIM
BLOCK = 128
ROT_DIM = HEAD_DIM // 4
ROPE_THETA = 500000.0
ATTN_SCALE = 1.0 / math.sqrt(HEAD_DIM)
NEG_INF = -1e30
D_FF = 2816
N_IN = 6656
EPS = 1e-5
C_CB, C_CC, C_CX, C_Q, C_K, C_V, C_GC, C_GA = 0, 1024, 2048, 3072, 4096, 4352, 4608, 5632

LR, B1, B2, EPS_ADAM, WD, STEP = 0.001, 0.9, 0.999, 1e-08, 0.01, 10

N_DEV = 8
MESH = pl.DeviceIdType.MESH
VMEM_LIMIT = 56 * 1024 * 1024

NN = (((1,), (0,)), ((), ()))
NT = (((1,), (1,)), ((), ()))
TN = (((0,), (0,)), ((), ()))
HBM_SPEC = pl.BlockSpec(memory_space=pl.ANY)


def _call(body, **kw):
    return pl.pallas_call(body, **kw)


def _params(*sem):
    return pltpu.CompilerParams(dimension_semantics=sem, vmem_limit_bytes=VMEM_LIMIT)


def _sds(shape, dtype):
    return jax.ShapeDtypeStruct(shape, dtype)


def _matmul(a, b, *, mode, tm, tn, tk, out_dtype, name, res=None, after=()):
    parts = list(a) if isinstance(a, (list, tuple)) else [a]
    rows_a = parts[0].shape[0]
    cols_a = sum(p.shape[1] for p in parts)
    if mode == "nn":
        (m, kk), (_, n), dims = (rows_a, cols_a), b.shape, NN
    elif mode == "nt":
        (m, kk), (n, _), dims = (rows_a, cols_a), b.shape, NT
    else:
        (kk, m), (_, n), dims = (rows_a, cols_a), b.shape, TN
    tm, tn, tk = min(tm, m), min(tn, n), min(tk, kk)
    assert m % tm == 0 and n % tn == 0 and kk % tk == 0, (name, m, n, kk, tm, tn, tk)
    nk = kk // tk
    split_axis, width = (2, tk) if mode == "nn" else (0, tm)
    assert len(parts) == 1 or mode in ("nn", "tn")
    assert len(parts) == 1 or all(p.shape[1] % width == 0 for p in parts), (name, width)
    counts = [p.shape[1] // width for p in parts]
    starts = [sum(counts[:p]) for p in range(len(parts))]

    def a_spec(p):
        def col(t):
            return jnp.clip(t - starts[p], 0, counts[p] - 1) if len(parts) > 1 else t

        if mode == "tn":
            return pl.BlockSpec((tk, tm), lambda i, j, k: (k, col(i)))
        return pl.BlockSpec((tm, tk), lambda i, j, k: (i, col(k)))

    if mode == "nt":
        b_spec = pl.BlockSpec((tn, tk), lambda i, j, k: (j, k))
    else:
        b_spec = pl.BlockSpec((tk, tn), lambda i, j, k: (k, j))
    o_spec = pl.BlockSpec((tm, tn), lambda i, j, k: (i, j))
    has_res = res is not None
    n_parts = len(parts)

    def body(*refs):
        a_refs, b_ref = refs[:n_parts], refs[n_parts]
        r_ref = refs[n_parts + 1] if has_res else None
        o_ref = refs[n_parts + 1 + has_res + len(after)]
        k = pl.program_id(2)

        def step(a_ref):
            part = lax.dot_general(a_ref[...], b_ref[...], dims, preferred_element_type=F32)

            def finish(acc):
                if has_res:
                    acc = acc + r_ref[...]
                o_ref[...] = acc.astype(o_ref.dtype)

            if nk == 1:
                finish(part)
            else:
                acc_ref = refs[-1]

                @pl.when(k == 0)
                def _():
                    acc_ref[...] = part

                @pl.when(k > 0)
                def _():
                    acc_ref[...] += part

                @pl.when(k == nk - 1)
                def _():
                    finish(acc_ref[...])

        if n_parts == 1:
            step(a_refs[0])
        else:
            t = pl.program_id(split_axis)
            for p in range(n_parts):
                pl.when((t >= starts[p]) & (t < starts[p] + counts[p]))(functools.partial(step, a_refs[p]))

    ins = parts + [b] + ([res] if has_res else []) + list(after)
    in_specs = [a_spec(p) for p in range(n_parts)] + [b_spec] + ([o_spec] if has_res else []) + [HBM_SPEC] * len(after)
    scratch = [] if nk == 1 else [pltpu.VMEM((tm, tn), F32)]
    return _call(
        body, name=name, grid=(m // tm, n // tn, nk), in_specs=in_specs, out_specs=o_spec,
        out_shape=_sds((m, n), out_dtype), scratch_shapes=scratch,
        compiler_params=_params("parallel", "parallel", "arbitrary"),
    )(*ins)


def _row_tile(s):
    return min(256, s)


def _rms_fwd(x, g, name, after=()):
    s = x.shape[0]
    tm = _row_tile(s)

    def body(x_ref, g_ref, *rest):
        h_ref = rest[-1]
        xv = x_ref[...]
        r = lax.rsqrt(jnp.mean(xv * xv, axis=-1, keepdims=True) + EPS)
        h_ref[...] = (xv * r * g_ref[...]).astype(BF16)

    row = pl.BlockSpec((tm, D), lambda i: (i, 0))
    return _call(
        body, name=name, grid=(s // tm,), in_specs=[row, pl.BlockSpec((1, D), lambda i: (0, 0))] + [HBM_SPEC] * len(after),
        out_specs=row, out_shape=_sds((s, D), BF16), compiler_params=_params("parallel"),
    )(x, g, *after)


def _rms_bwd(dh, x, g, dres, name, after=()):
    s = x.shape[0]
    tm = _row_tile(s)

    def body(dh_ref, x_ref, g_ref, dres_ref, *rest):
        dx_ref, dxb_ref, dg_ref = rest[len(after):]
        xv = x_ref[...]
        r = lax.rsqrt(jnp.mean(xv * xv, axis=-1, keepdims=True) + EPS)
        xh = xv * r
        dhv = dh_ref[...]
        dyg = dhv * g_ref[...]
        dx = dres_ref[...] + r * (dyg - xh * jnp.mean(dyg * xh, axis=-1, keepdims=True))
        dx_ref[...] = dx
        dxb_ref[...] = dx.astype(BF16)
        part = jnp.sum(dhv * xh, axis=0, keepdims=True)

        @pl.when(pl.program_id(0) == 0)
        def _():
            dg_ref[...] = part

        @pl.when(pl.program_id(0) > 0)
        def _():
            dg_ref[...] += part

    row = pl.BlockSpec((tm, D), lambda i: (i, 0))
    vec = pl.BlockSpec((1, D), lambda i: (0, 0))
    return _call(
        body, name=name, grid=(s // tm,), in_specs=[row, row, vec, row] + [HBM_SPEC] * len(after), out_specs=[row, row, vec],
        out_shape=[_sds((s, D), F32), _sds((s, D), BF16), _sds((1, D), F32)],
        compiler_params=_params("arbitrary"),
    )(dh, x, g, dres, *after)


def _loss_head(x2, g, tgt, name):
    s = x2.shape[0]
    tm = _row_tile(s)

    def body(x_ref, g_ref, t_ref, dx_ref, dxb_ref, dg_ref, l_ref):
        xv = x_ref[...]
        gv = g_ref[...]
        r = lax.rsqrt(jnp.mean(xv * xv, axis=-1, keepdims=True) + EPS)
        xh = xv * r
        err = xh * gv - t_ref[...]
        dy = err * (1.0 / D)
        dyg = dy * gv
        dx = r * (dyg - xh * jnp.mean(dyg * xh, axis=-1, keepdims=True))
        dx_ref[...] = dx
        dxb_ref[...] = dx.astype(BF16)
        dg_part = jnp.sum(dy * xh, axis=0, keepdims=True)
        l_part = jnp.sum(err * err, axis=0, keepdims=True)

        @pl.when(pl.program_id(0) == 0)
        def _():
            dg_ref[...] = dg_part
            l_ref[...] = l_part

        @pl.when(pl.program_id(0) > 0)
        def _():
            dg_ref[...] += dg_part
            l_ref[...] += l_part

    row = pl.BlockSpec((tm, D), lambda i: (i, 0))
    vec = pl.BlockSpec((1, D), lambda i: (0, 0))
    return _call(
        body, name=name, grid=(s // tm,), in_specs=[row, vec, row], out_specs=[row, row, vec, vec],
        out_shape=[_sds((s, D), F32), _sds((s, D), BF16), _sds((1, D), F32), _sds((1, D), F32)],
        compiler_params=_params("arbitrary"),
    )(x2, g, tgt)


CONV_TC = 256


def _shift_down(u, k, rows):
    return jnp.where(rows >= k, pltpu.roll(u, k, 0), 0.0)


def _shift_up(u, k, rows, s):
    return jnp.where(rows < s - k, pltpu.roll(u, s - k, 0), 0.0)


def _conv_specs(s):
    nb = D // CONV_TC

    def col(c0):
        return pl.BlockSpec((s, CONV_TC), lambda j, c0=c0: (0, c0 // CONV_TC + j))

    return nb, col


def _conv_fwd(proj, conv_w, name):
    s = proj.shape[0]
    nb, col = _conv_specs(s)

    def body(cb_ref, cc_ref, cx_ref, w_ref, y_ref):
        rows = lax.broadcasted_iota(jnp.int32, (s, CONV_TC), 0)
        u = cc_ref[...].astype(F32) * cx_ref[...].astype(F32)
        w = w_ref[...]
        c = w[0:1] * _shift_down(u, 2, rows) + w[1:2] * _shift_down(u, 1, rows) + w[2:3] * u
        y_ref[...] = (cb_ref[...].astype(F32) * c).astype(BF16)

    return _call(
        body, name=name, grid=(nb,),
        in_specs=[col(C_CB), col(C_CC), col(C_CX), pl.BlockSpec((3, CONV_TC), lambda j: (0, j))],
        out_specs=pl.BlockSpec((s, CONV_TC), lambda j: (0, j)), out_shape=_sds((s, D), BF16),
        compiler_params=_params("parallel"),
    )(proj, proj, proj, conv_w)


def _conv_bwd(dy, proj, conv_w, dproj, name, after=()):
    s = proj.shape[0]
    nb, col = _conv_specs(s)

    def body(dy_ref, cb_ref, cc_ref, cx_ref, w_ref, *rest):
        dproj_ref, dw_ref, buf, sems = rest[1 + len(after):]
        j = pl.program_id(0)
        rows = lax.broadcasted_iota(jnp.int32, (s, CONV_TC), 0)
        cc = cc_ref[...].astype(F32)
        cx = cx_ref[...].astype(F32)
        u = cc * cx
        u1 = _shift_down(u, 1, rows)
        u2 = _shift_down(u, 2, rows)
        w = w_ref[...]
        c = w[0:1] * u2 + w[1:2] * u1 + w[2:3] * u
        dyv = dy_ref[...].astype(F32)
        dc = dyv * cb_ref[...].astype(F32)
        du = w[2:3] * dc + w[1:2] * _shift_up(dc, 1, rows, s) + w[0:1] * _shift_up(dc, 2, rows, s)
        buf[0] = (dyv * c).astype(BF16)
        buf[1] = (du * cx).astype(BF16)
        buf[2] = (du * cc).astype(BF16)
        dw_ref[...] = jnp.concatenate(
            [jnp.sum(dc * u2, axis=0, keepdims=True), jnp.sum(dc * u1, axis=0, keepdims=True),
             jnp.sum(dc * u, axis=0, keepdims=True)], axis=0)
        copies = []
        for p, c0 in enumerate((C_CB, C_CC, C_CX)):
            start = pl.multiple_of(c0 + j * CONV_TC, CONV_TC)
            copies.append(pltpu.make_async_copy(buf.at[p], dproj_ref.at[:, pl.ds(start, CONV_TC)], sems.at[p]))
        for cp in copies:
            cp.start()
        for cp in copies:
            cp.wait()

    return _call(
        body, name=name, grid=(nb,),
        in_specs=[pl.BlockSpec((s, CONV_TC), lambda j: (0, j)), col(C_CB), col(C_CC), col(C_CX),
                  pl.BlockSpec((3, CONV_TC), lambda j: (0, j))] + [HBM_SPEC] * (1 + len(after)),
        out_specs=[pl.BlockSpec(memory_space=pl.ANY), pl.BlockSpec((3, CONV_TC), lambda j: (0, j))],
        out_shape=[_sds((s, N_IN), BF16), _sds((3, D), F32)],
        scratch_shapes=[pltpu.VMEM((3, s, CONV_TC), BF16), pltpu.SemaphoreType.DMA((3,))],
        input_output_aliases={5: 0}, compiler_params=_params("arbitrary"),
    )(dy, proj, proj, proj, conv_w, dproj, *after)


def _rope_tables(s):
    half = ROT_DIM // 2
    inv_freq = ROPE_THETA ** (-jnp.arange(0, ROT_DIM, 2, dtype=F32) / ROT_DIM)
    inv64 = jnp.concatenate([inv_freq, inv_freq, jnp.zeros((HEAD_DIM - ROT_DIM,), F32)])
    ang = jnp.arange(s, dtype=F32)[:, None] * jnp.concatenate([inv64, inv64])[None, :]
    d = lax.broadcasted_iota(jnp.int32, (s, 128), 1) % HEAD_DIM
    cos, sin = jnp.cos(ang), jnp.sin(ang)
    c = jnp.where(d < ROT_DIM, cos, 1.0)
    a = jnp.where(d < half, -sin, 0.0)
    b = jnp.where((d >= half) & (d < ROT_DIM), sin, 0.0)
    return jnp.concatenate([c, a, b], axis=1)


def _rope(x, tab):
    c, a, b = tab[:, 0:128], tab[:, 128:256], tab[:, 256:384]
    outs = []
    for i in range(x.shape[1] // 128):
        xc = x[:, i * 128:(i + 1) * 128]
        outs.append(xc * c + pltpu.roll(xc, 120, 1) * a + pltpu.roll(xc, 8, 1) * b)
    return outs[0] if len(outs) == 1 else jnp.concatenate(outs, axis=1)


def _rope_t(dx, tab):
    c, a, b = tab[:, 0:128], tab[:, 128:256], tab[:, 256:384]
    outs = []
    for i in range(dx.shape[1] // 128):
        dc = dx[:, i * 128:(i + 1) * 128]
        outs.append(dc * c + pltpu.roll(dc * a, 8, 1) + pltpu.roll(dc * b, 120, 1))
    return outs[0] if len(outs) == 1 else jnp.concatenate(outs, axis=1)


def _attn_mask(n):
    qi = lax.broadcasted_iota(jnp.int32, (GROUP * BLOCK, 2 * BLOCK), 0) & (BLOCK - 1)
    kj = lax.broadcasted_iota(jnp.int32, (GROUP * BLOCK, 2 * BLOCK), 1)
    rel = qi + BLOCK - kj
    return (rel >= 0) & (rel < BLOCK) & ((kj >= BLOCK) | (n > 0))


def _sink_col(sink_ref, hk):
    return jnp.concatenate([jnp.full((BLOCK, 1), sink_ref[0, hk * GROUP + g], F32) for g in range(GROUP)], axis=0)


def _attn_in_specs():
    prev = lambda n: jnp.maximum(n - 1, 0)
    return [
        pl.BlockSpec((BLOCK, D), lambda n: (n, C_Q // D)),
        pl.BlockSpec((BLOCK, D_KV), lambda n: (n, C_K // D_KV)),
        pl.BlockSpec((BLOCK, D_KV), lambda n: (prev(n), C_K // D_KV)),
        pl.BlockSpec((BLOCK, D_KV), lambda n: (n, C_V // D_KV)),
        pl.BlockSpec((BLOCK, D_KV), lambda n: (prev(n), C_V // D_KV)),
        pl.BlockSpec((BLOCK, 384), lambda n: (n, 0)),
        pl.BlockSpec((BLOCK, 384), lambda n: (prev(n), 0)),
        pl.BlockSpec(memory_space=pltpu.SMEM),
    ]


def _load_qkv(q_ref, kc_ref, kp_ref, vc_ref, vp_ref, tc_ref, tp_ref):
    q = _rope(q_ref[...].astype(F32), tc_ref[...]).astype(BF16)
    kc = _rope(kc_ref[...].astype(F32), tc_ref[...]).astype(BF16)
    kp = _rope(kp_ref[...].astype(F32), tp_ref[...]).astype(BF16)
    return q, kc, kp, vc_ref[...], vp_ref[...]


def _group_rows(x, hk):
    base = hk * GROUP * HEAD_DIM
    return jnp.concatenate([x[:, base + g * HEAD_DIM: base + (g + 1) * HEAD_DIM] for g in range(GROUP)], axis=0)


def _kv_rows(prev, cur, hk):
    sl = slice(hk * HEAD_DIM, (hk + 1) * HEAD_DIM)
    return jnp.concatenate([prev[:, sl], cur[:, sl]], axis=0)


def _attn_bwd(do, proj, tab, sinks, dproj, name, after=()):
    s = proj.shape[0]
    nblk = s // BLOCK

    def body(do_ref, q_ref, kc_ref, kp_ref, vc_ref, vp_ref, tc_ref, tp_ref, sink_ref, *rest):
        dproj_ref, dk_ref, dv_ref, ds_ref, dqbuf, dqout, dkbuf, dvbuf, sem = rest[1 + len(after):]
        n = pl.program_id(0)

        @pl.when(n == 0)
        def _():
            dk_ref[...] = jnp.zeros_like(dk_ref)
            dv_ref[...] = jnp.zeros_like(dv_ref)
            ds_ref[...] = jnp.zeros_like(ds_ref)

        q, kc, kp, vc, vp = _load_qkv(q_ref, kc_ref, kp_ref, vc_ref, vp_ref, tc_ref, tp_ref)
        dov = do_ref[...]
        mask = _attn_mask(n)
        rows = GROUP * BLOCK
        head_off = lax.broadcasted_iota(jnp.int32, (rows, 128), 1) - (lax.broadcasted_iota(jnp.int32, (rows, 128), 0) >> 7)
        dsink_row = jnp.zeros((1, 128), F32)
        prev0 = pl.multiple_of(jnp.maximum(n - 1, 0) * BLOCK, BLOCK)
        cur0 = pl.multiple_of(n * BLOCK, BLOCK)
        for hk in range(N_KV):
            qg = _group_rows(q, hk)
            dog = _group_rows(dov, hk)
            kcat = _kv_rows(kp, kc, hk)
            vcat = _kv_rows(vp, vc, hk)
            sc = lax.dot_general(qg, kcat, NT, preferred_element_type=F32) * ATTN_SCALE
            sc = jnp.where(mask, sc, NEG_INF)
            sink = _sink_col(sink_ref, hk)
            m = jnp.maximum(jnp.max(sc, axis=1, keepdims=True), sink)
            e = jnp.exp(sc - m)
            es = jnp.exp(sink - m)
            inv = 1.0 / (jnp.sum(e, axis=1, keepdims=True) + es)
            p = e * inv
            pb = p.astype(BF16)
            dp = lax.dot_general(dog, vcat, NT, preferred_element_type=F32)
            delta = jnp.sum(p * dp, axis=1, keepdims=True)
            dsc = (p * (dp - delta) * ATTN_SCALE).astype(BF16)
            dsk = -(es * inv) * delta
            dsink_row = dsink_row + jnp.sum(jnp.where(head_off == hk * GROUP, dsk, 0.0), axis=0, keepdims=True)
            dqg = lax.dot_general(dsc, kcat, NN, preferred_element_type=F32)
            dkcat = lax.dot_general(dsc, qg, TN, preferred_element_type=F32)
            dvcat = lax.dot_general(pb, dog, TN, preferred_element_type=F32)
            base = hk * GROUP * HEAD_DIM
            for g in range(GROUP):
                dqbuf[:, base + g * HEAD_DIM: base + (g + 1) * HEAD_DIM] = dqg[g * BLOCK:(g + 1) * BLOCK]
            sl = slice(hk * HEAD_DIM, (hk + 1) * HEAD_DIM)
            dkbuf[:, sl] = dkcat
            dvbuf[:, sl] = dvcat

        @pl.when(n > 0)
        def _():
            dk_ref[pl.ds(prev0, BLOCK), :] += dkbuf[0:BLOCK, :]
            dv_ref[pl.ds(prev0, BLOCK), :] += dvbuf[0:BLOCK, :]

        dk_ref[pl.ds(cur0, BLOCK), :] += dkbuf[BLOCK:2 * BLOCK, :]
        dv_ref[pl.ds(cur0, BLOCK), :] += dvbuf[BLOCK:2 * BLOCK, :]
        ds_ref[...] += dsink_row
        dqout[...] = _rope_t(dqbuf[...], tc_ref[...]).astype(BF16)
        cp = pltpu.make_async_copy(dqout, dproj_ref.at[pl.ds(cur0, BLOCK), pl.ds(C_Q, D)], sem)
        cp.start()
        cp.wait()

    blk = lambda w: pl.BlockSpec((BLOCK, w), lambda n: (n, 0))
    whole = lambda w: pl.BlockSpec((s, w), lambda n: (0, 0))
    anyspec = pl.BlockSpec(memory_space=pl.ANY)
    n_in = 1 + len(_attn_in_specs())
    return _call(
        body, name=name, grid=(nblk,), in_specs=[blk(D)] + _attn_in_specs() + [anyspec] * (1 + len(after)),
        out_specs=[anyspec, whole(D_KV), whole(D_KV), pl.BlockSpec((1, 128), lambda n: (0, 0))],
        out_shape=[_sds((s, N_IN), BF16), _sds((s, D_KV), F32), _sds((s, D_KV), F32), _sds((1, 128), F32)],
        scratch_shapes=[pltpu.VMEM((BLOCK, D), F32), pltpu.VMEM((BLOCK, D), BF16), pltpu.VMEM((2 * BLOCK, D_KV), F32),
                        pltpu.VMEM((2 * BLOCK, D_KV), F32), pltpu.SemaphoreType.DMA(())],
        input_output_aliases={n_in: 0}, compiler_params=_params("arbitrary"),
    )(do, proj, proj, proj, proj, proj, tab, tab, sinks, dproj, *after)


HALF = HEAD_DIM
N_CHUNK = D // 128


def _swa_bias(n):
    qi = lax.broadcasted_iota(jnp.int32, (BLOCK, 2 * BLOCK), 0)
    kj = lax.broadcasted_iota(jnp.int32, (BLOCK, 2 * BLOCK), 1)
    rel = qi + BLOCK - kj
    valid = (rel >= 0) & (rel < BLOCK) & ((kj >= BLOCK) | (n > 0))
    return jnp.where(valid, 0.0, NEG_INF)


def _halves(x):
    lo = lax.broadcasted_iota(jnp.int32, x.shape, 1) < HALF
    return jnp.where(lo, x, 0.0).astype(BF16), jnp.where(lo, 0.0, x).astype(BF16)


def _dup_heads(x):
    out = []
    for pair in range(N_KV // 2):
        xc = x[:, pair * 128:(pair + 1) * 128]
        xr = pltpu.roll(xc, HALF, 1)
        lo = lax.broadcasted_iota(jnp.int32, xc.shape, 1) < HALF
        out += [jnp.where(lo, xc, xr), jnp.where(lo, xr, xc)]
    return out


def _swa_load(q_ref, kc_ref, kp_ref, vc_ref, vp_ref, tc_ref, tp_ref):
    qf = _rope(q_ref[...].astype(F32), tc_ref[...]) * ATTN_SCALE
    q_halves = [_halves(qf[:, c * 128:(c + 1) * 128]) for c in range(N_CHUNK)]
    kf = jnp.concatenate([_rope(kp_ref[...].astype(F32), tp_ref[...]), _rope(kc_ref[...].astype(F32), tc_ref[...])], axis=0)
    vf = jnp.concatenate([vp_ref[...], vc_ref[...]], axis=0).astype(F32)
    return q_halves, _dup_heads(kf), _dup_heads(vf)


def _swa_probs(qh, kk, bias, sink):
    s = lax.dot_general(qh, kk, NT, preferred_element_type=F32) + bias
    m = jnp.maximum(jnp.max(jnp.maximum(s[:, :BLOCK], s[:, BLOCK:]), axis=1, keepdims=True), sink)
    return jnp.exp(s - m), m


def _swa_fwd(proj, tab, sinks, name, after=()):
    s = proj.shape[0]

    def body(q_ref, kc_ref, kp_ref, vc_ref, vp_ref, tc_ref, tp_ref, sink_ref, *rest):
        o_ref = rest[-1]
        n = pl.program_id(0)
        q_halves, kdup, vdup = _swa_load(q_ref, kc_ref, kp_ref, vc_ref, vp_ref, tc_ref, tp_ref)
        bias = _swa_bias(n)
        ones = jnp.ones((2 * BLOCK, 128), BF16)
        for c in range(N_CHUNK):
            hk = c // (GROUP // 2)
            kk = kdup[hk].astype(BF16)
            acc = None
            for half, v_half in enumerate(_halves(vdup[hk])):
                sink = sink_ref[0, 2 * c + half]
                e, m = _swa_probs(q_halves[c][half], kk, bias, sink)
                o = lax.dot_general(e.astype(BF16), jnp.concatenate([v_half, ones], axis=1), NN, preferred_element_type=F32)
                part = o[:, :128] * (1.0 / (o[:, 128:] + jnp.exp(sink - m)))
                acc = part if acc is None else acc + part
            o_ref[:, c * 128:(c + 1) * 128] = acc.astype(BF16)

    return _call(
        body, name=name, grid=(s // BLOCK,), in_specs=_attn_in_specs() + [HBM_SPEC] * len(after),
        out_specs=pl.BlockSpec((BLOCK, D), lambda n: (n, 0)), out_shape=_sds((s, D), BF16),
        compiler_params=_params("parallel"),
    )(proj, proj, proj, proj, proj, tab, tab, sinks, *after)


def _kv_bwd(dkr, dv, tab, dproj, name):
    s = dkr.shape[0]
    tm = _row_tile(s)

    def body(dk_ref, dv_ref, t_ref, dproj_in, o_ref):
        del dproj_in
        o_ref[:, 0:D_KV] = _rope_t(dk_ref[...], t_ref[...]).astype(BF16)
        o_ref[:, D_KV:2 * D_KV] = dv_ref[...].astype(BF16)

    row = lambda w: pl.BlockSpec((tm, w), lambda i: (i, 0))
    return _call(
        body, name=name, grid=(s // tm,),
        in_specs=[row(D_KV), row(D_KV), row(384), pl.BlockSpec(memory_space=pl.ANY)],
        out_specs=pl.BlockSpec((tm, 2 * D_KV), lambda i: (i, C_K // (2 * D_KV))),
        out_shape=_sds((s, N_IN), BF16), input_output_aliases={3: 0}, compiler_params=_params("parallel"),
    )(dkr, dv, tab, dproj)


EW_TC = 512


def _sigmoid(x):
    return 0.5 * jnp.tanh(0.5 * x) + 0.5


def _merge_fwd(proj, conv_out, attn_out, name):
    s = proj.shape[0]
    tm = _row_tile(s)
    tile = pl.BlockSpec((tm, EW_TC), lambda i, j: (i, j))

    def body(gc_ref, ga_ref, co_ref, ao_ref, o_ref):
        o_ref[...] = (_sigmoid(gc_ref[...].astype(F32)) * co_ref[...].astype(F32)
                      + _sigmoid(ga_ref[...].astype(F32)) * ao_ref[...].astype(F32)).astype(BF16)

    return _call(
        body, name=name, grid=(s // tm, D // EW_TC),
        in_specs=[pl.BlockSpec((tm, EW_TC), lambda i, j: (i, C_GC // EW_TC + j)),
                  pl.BlockSpec((tm, EW_TC), lambda i, j: (i, C_GA // EW_TC + j)), tile, tile],
        out_specs=tile, out_shape=_sds((s, D), BF16), compiler_params=_params("parallel", "parallel"),
    )(proj, proj, conv_out, attn_out)


def _merge_bwd(dmerged, proj, conv_out, attn_out, name):
    s = proj.shape[0]
    tm = _row_tile(s)
    tile = pl.BlockSpec((tm, EW_TC), lambda i, j: (i, j))
    anyspec = pl.BlockSpec(memory_space=pl.ANY)

    def body(dm_ref, gc_ref, ga_ref, co_ref, ao_ref, dproj_ref, dco_ref, dao_ref, buf, sems):
        i, j = pl.program_id(0), pl.program_id(1)
        dm = dm_ref[...].astype(F32)
        sc = _sigmoid(gc_ref[...].astype(F32))
        sa = _sigmoid(ga_ref[...].astype(F32))
        dco_ref[...] = (dm * sc).astype(BF16)
        dao_ref[...] = (dm * sa).astype(BF16)
        buf[0] = (dm * co_ref[...].astype(F32) * sc * (1.0 - sc)).astype(BF16)
        buf[1] = (dm * ao_ref[...].astype(F32) * sa * (1.0 - sa)).astype(BF16)
        r0 = pl.multiple_of(i * tm, tm)
        copies = []
        for p, c0 in enumerate((C_GC, C_GA)):
            start = pl.multiple_of(c0 + j * EW_TC, EW_TC)
            copies.append(pltpu.make_async_copy(buf.at[p], dproj_ref.at[pl.ds(r0, tm), pl.ds(start, EW_TC)], sems.at[p]))
        for cp in copies:
            cp.start()
        for cp in copies:
            cp.wait()

    return _call(
        body, name=name, grid=(s // tm, D // EW_TC),
        in_specs=[tile, pl.BlockSpec((tm, EW_TC), lambda i, j: (i, C_GC // EW_TC + j)),
                  pl.BlockSpec((tm, EW_TC), lambda i, j: (i, C_GA // EW_TC + j)), tile, tile],
        out_specs=[anyspec, tile, tile],
        out_shape=[_sds((s, N_IN), BF16), _sds((s, D), BF16), _sds((s, D), BF16)],
        scratch_shapes=[pltpu.VMEM((2, tm, EW_TC), BF16), pltpu.SemaphoreType.DMA((2,))],
        compiler_params=_params("arbitrary", "arbitrary"),
    )(dmerged, proj, proj, conv_out, attn_out)


FF_TC = 256


def _gate_up_fwd(h2, wgu_t, name):
    s = h2.shape[0]
    tm = min(2048, s)
    nb = D_FF // FF_TC

    def body(h_ref, wg_ref, wu_ref, g_ref, u_ref, a_ref):
        h = h_ref[...]
        g = lax.dot_general(h, wg_ref[...], NT, preferred_element_type=F32)
        u = lax.dot_general(h, wu_ref[...], NT, preferred_element_type=F32)
        g_ref[...] = g.astype(BF16)
        u_ref[...] = u.astype(BF16)
        a_ref[...] = (g * _sigmoid(g) * u).astype(BF16)

    tile = pl.BlockSpec((tm, FF_TC), lambda i, j: (i, j))
    return _call(
        body, name=name, grid=(s // tm, nb),
        in_specs=[pl.BlockSpec((tm, D), lambda i, j: (i, 0)), pl.BlockSpec((FF_TC, D), lambda i, j: (j, 0)),
                  pl.BlockSpec((FF_TC, D), lambda i, j: (nb + j, 0))],
        out_specs=[tile, tile, tile], out_shape=[_sds((s, D_FF), BF16)] * 3,
        compiler_params=_params("parallel", "parallel"),
    )(h2, wgu_t, wgu_t)


def _down_bwd_x(dx2b, wd, gate, up, name):
    s = dx2b.shape[0]
    tm = min(2048, s)
    nb = D_FF // FF_TC

    def body(dx_ref, w_ref, g_ref, u_ref, dg_ref, du_ref):
        da = lax.dot_general(dx_ref[...], w_ref[...], NT, preferred_element_type=F32)
        g = g_ref[...].astype(F32)
        sg = _sigmoid(g)
        dg_ref[...] = (da * u_ref[...].astype(F32) * (sg * (1.0 + g * (1.0 - sg)))).astype(BF16)
        du_ref[...] = (da * (g * sg)).astype(BF16)

    tile = pl.BlockSpec((tm, FF_TC), lambda i, j: (i, j))
    return _call(
        body, name=name, grid=(s // tm, nb),
        in_specs=[pl.BlockSpec((tm, D), lambda i, j: (i, 0)), pl.BlockSpec((FF_TC, D), lambda i, j: (j, 0)), tile, tile],
        out_specs=[tile, tile], out_shape=[_sds((s, D_FF), BF16)] * 2,
        compiler_params=_params("parallel", "parallel"),
    )(dx2b, wd, gate, up)


class _Weights:
    def __init__(self, **groups):
        self.groups = groups

    def begin(self, group, after):
        return ()

    def end(self, group, after):
        return self.groups[group]


class _NoReduce:
    def start(self, group, grads):
        return ()

    def middle(self, group, after):
        return ()


def _local_step(x, tgt, g_mix, g_ffn, g_final, sinks, weights, reducer=None, after=()):
    reducer = reducer or _NoReduce()
    s = x.shape[0]
    tab = _rope_tables(s)
    big = dict(tm=1024, tn=512, tk=1024)
    h1 = _rms_fwd(x, g_mix, "rms1_fwd", after=after)
    win_t, conv_w = weights.end("in", weights.begin("in", (h1,)))
    proj = _matmul(h1, win_t, mode="nt", out_dtype=BF16, name="proj_fwd", tm=2048, tn=512, tk=1024)
    attn = _swa_fwd(proj, tab, sinks, "attn_fwd", after=weights.begin("mix", (proj,)))
    wco, wao, wo = weights.end("mix", (attn,))
    conv_y = _conv_fwd(proj, conv_w, "conv_fwd")
    conv_out = _matmul(conv_y, wco, mode="nn", out_dtype=BF16, name="conv_out_fwd", after=weights.begin("ffn", (attn,)), **big)
    attn_out = _matmul(attn, wao, mode="nn", out_dtype=BF16, name="attn_out_fwd", **big)
    merged = _merge_fwd(proj, conv_out, attn_out, "merge_fwd")
    x1 = _matmul(merged, wo, mode="nn", out_dtype=F32, name="wo_fwd", res=x, **big)
    h2 = _rms_fwd(x1, g_ffn, "rms2_fwd")
    wgu_t, wd = weights.end("ffn", (h2,))
    gate, up, act = _gate_up_fwd(h2, wgu_t, "gate_up_fwd")
    x2 = _matmul(act, wd, mode="nn", out_dtype=F32, name="down_fwd", res=x1, tm=1024, tn=512, tk=D_FF)
    dx2, dx2b, dg_final, lossvec = _loss_head(x2, g_final, tgt, "loss_head")
    dgate, dup = _down_bwd_x(dx2b, wd, gate, up, "down_bwd_x")
    g_wd = _matmul(act, dx2b, mode="tn", out_dtype=BF16, name="down_bwd_w", tm=1408, tn=1024, tk=2048)
    dh2 = _matmul([dgate, dup], wgu_t, mode="nn", out_dtype=F32, name="gate_up_bwd_x", tm=1024, tn=1024, tk=1408)
    g_wgu_t = _matmul([dgate, dup], h2, mode="tn", out_dtype=BF16, name="gate_up_bwd_w", tm=1408, tn=1024, tk=2048)
    after_ffn = reducer.start("ffn", dict(wgu_t=g_wgu_t, wd=g_wd))
    dx1, dx1b, dg_ffn = _rms_bwd(dh2, x1, g_ffn, dx2, "rms2_bwd")
    dmerged = _matmul(dx1b, wo, mode="nt", out_dtype=BF16, name="wo_bwd_x", after=after_ffn, **big)
    after_ffn = reducer.middle("ffn", (dmerged,))
    g_wo = _matmul(merged, dx1b, mode="tn", out_dtype=BF16, name="wo_bwd_w", tm=512, tn=1024, tk=2048, after=after_ffn)
    dproj, dco, dao = _merge_bwd(dmerged, proj, conv_out, attn_out, "merge_bwd")
    dconv_y = _matmul(dco, wco, mode="nt", out_dtype=BF16, name="conv_out_bwd_x", **big)
    g_wco = _matmul(conv_y, dco, mode="tn", out_dtype=BF16, name="conv_out_bwd_w", tm=512, tn=1024, tk=2048)
    dattn = _matmul(dao, wao, mode="nt", out_dtype=BF16, name="attn_out_bwd_x", **big)
    g_wao = _matmul(attn, dao, mode="tn", out_dtype=BF16, name="attn_out_bwd_w", tm=512, tn=1024, tk=2048)
    after_mix = reducer.start("mix", dict(wco=g_wco, wao=g_wao, wo=g_wo))
    dproj, dconv_w = _conv_bwd(dconv_y, proj, conv_w, dproj, "conv_bwd", after=after_mix)
    after_mix = reducer.middle("mix", (dconv_w,))
    dproj, dkr, dv, dsinks = _attn_bwd(dattn, proj, tab, sinks, dproj, "attn_bwd", after=after_mix)
    dproj = _kv_bwd(dkr, dv, tab, dproj, "kv_bwd")
    g_win_t = _matmul(dproj, h1, mode="tn", out_dtype=BF16, name="proj_bwd_w", tm=512, tn=1024, tk=2048)
    after_in = reducer.start("in", dict(win_t=g_win_t))
    dh1 = _matmul(dproj, win_t, mode="nn", out_dtype=F32, name="proj_bwd_x", tm=1024, tn=1024, tk=1664, after=after_in)
    dx, _, dg_mix = _rms_bwd(dh1, x, g_mix, dx1, "rms1_bwd", after=reducer.middle("in", (dh1,)))
    grads = dict(win_t=g_win_t, wgu_t=g_wgu_t, wd=g_wd, wco=g_wco, wao=g_wao, wo=g_wo)
    small = dict(g_mix=dg_mix, g_ffn=dg_ffn, g_final=dg_final, conv_w=dconv_w, sinks=dsinks, lossvec=lossvec)
    return dx, grads, small


def _position():
    return lax.axis_index("x"), lax.axis_index("y"), lax.axis_index("c")


def _other_chips(x, y):
    return [(1 - x, y), (x, 1 - y), (1 - x, 1 - y)]


SEM_SPEC = pl.BlockSpec(memory_space=pltpu.SEMAPHORE)
EFFECT = pltpu.SideEffectType.DATAFLOW_SIDE_EFFECTING
TOKEN = jax.ShapeDtypeStruct((8, 128), F32)
TOKEN_SPEC = pl.BlockSpec(memory_space=pltpu.VMEM)


def _hbm(a):
    return pltpu.with_memory_space_constraint(a, pltpu.HBM)


def _place(w, me_idx, dtype, name):
    r, cdim = w.shape

    def body(i_ref, w_ref, o_ref):
        del i_ref
        o_ref[...] = w_ref[...].astype(dtype)

    grid_spec = pltpu.PrefetchScalarGridSpec(
        num_scalar_prefetch=1, grid=(1,), in_specs=[pl.BlockSpec((r, cdim), lambda i, me: (0, 0))],
        out_specs=pl.BlockSpec((r, cdim), lambda i, me: (me[0], 0)))
    return _call(body, name=name, grid_spec=grid_spec, out_shape=_sds((N_DEV * r, cdim), dtype),
                 compiler_params=_params("arbitrary"))(me_idx, w)


def _own_rows(ref, r, px, py, pc):
    return ref.at[pl.ds((4 * px + 2 * py + pc) * r, r), :]


def _gather_start(bufs, groups, name):
    n = len(bufs)
    rows = [b.shape[0] // N_DEV for b in bufs]
    ng = len(groups)

    def body(*refs):
        ins = refs[:n]
        sems = refs[n:n + 2 * ng]
        token = refs[-1]
        x, y, c = _position()
        targets = [(x, y, 1 - c)] + [(*chip, c) for chip in _other_chips(x, y)]
        for g, members in enumerate(groups):
            for slot, a in enumerate(members):
                own = _own_rows(ins[a], rows[a], x, y, c)
                for to in targets:
                    pltpu.make_async_remote_copy(src_ref=own, dst_ref=own, send_sem=sems[2 * g].at[slot],
                                                 recv_sem=sems[2 * g + 1].at[slot], device_id=to, device_id_type=MESH).start()
        token[...] = jnp.zeros_like(token)

    sem_shapes = []
    for members in groups:
        sem_shapes += [pltpu.SemaphoreType.DMA((len(members),))] * 2
    outs = _call(
        body, name=name, in_specs=[HBM_SPEC] * n, out_specs=[SEM_SPEC] * (2 * ng) + [HBM_SPEC] * n + [TOKEN_SPEC],
        out_shape=sem_shapes + [pltpu.HBM(b.shape, b.dtype) for b in bufs] + [TOKEN],
        input_output_aliases={i: 2 * ng + i for i in range(n)},
        compiler_params=pltpu.CompilerParams(has_side_effects=EFFECT),
    )(*[_hbm(b) for b in bufs])
    sem_pairs = [(outs[2 * g], outs[2 * g + 1]) for g in range(ng)]
    return sem_pairs, list(outs[2 * ng:2 * ng + n]), outs[-1]


def _gather_forward(send_sems, recv_sems, bufs, after, name):
    n = len(bufs)
    rows = [b.shape[0] // N_DEV for b in bufs]

    def body(*refs):
        ins = refs[:n]
        send1, recv1 = refs[n], refs[n + 1]
        out0 = n + 2 + len(after)
        send2, recv2 = refs[out0], refs[out0 + 1]
        token = refs[-1]
        x, y, c = _position()
        for a in range(n):
            step1 = pltpu.make_async_remote_copy(
                src_ref=_whole(ins[a], 4 * rows[a]), dst_ref=_whole(ins[a], 4 * rows[a]), send_sem=send1.at[a],
                recv_sem=recv1.at[a], device_id=(x, y, c), device_id_type=MESH)
            step1.wait_send()
            step1.wait_recv()
        for a in range(n):
            for chip in _other_chips(x, y):
                blk = _own_rows(ins[a], rows[a], *chip, c)
                pltpu.make_async_remote_copy(src_ref=blk, dst_ref=blk, send_sem=send2.at[a], recv_sem=recv2.at[a],
                                             device_id=(x, y, 1 - c), device_id_type=MESH).start()
        token[...] = jnp.zeros_like(token)

    outs = _call(
        body, name=name, in_specs=[HBM_SPEC] * n + [SEM_SPEC, SEM_SPEC] + [HBM_SPEC] * len(after),
        out_specs=[SEM_SPEC, SEM_SPEC] + [HBM_SPEC] * n + [TOKEN_SPEC],
        out_shape=[pltpu.SemaphoreType.DMA((n,)), pltpu.SemaphoreType.DMA((n,))]
        + [pltpu.HBM(b.shape, b.dtype) for b in bufs] + [TOKEN],
        input_output_aliases={i: 2 + i for i in range(n)},
        compiler_params=pltpu.CompilerParams(has_side_effects=EFFECT),
    )(*bufs, send_sems, recv_sems, *after)
    return outs[0], outs[1], list(outs[2:2 + n]), outs[-1]


def _gather_done(send_sems, recv_sems, bufs, after, name):
    n = len(bufs)
    rows = [b.shape[0] // N_DEV for b in bufs]

    def body(*refs):
        ins = refs[:n]
        send2, recv2 = refs[n], refs[n + 1]
        x, y, c = _position()
        for a in range(n):
            step2 = pltpu.make_async_remote_copy(
                src_ref=_whole(ins[a], 3 * rows[a]), dst_ref=_whole(ins[a], 3 * rows[a]), send_sem=send2.at[a],
                recv_sem=recv2.at[a], device_id=(x, y, c), device_id_type=MESH)
            step2.wait_send()
            step2.wait_recv()

    outs = _call(
        body, name=name, in_specs=[HBM_SPEC] * n + [SEM_SPEC, SEM_SPEC] + [HBM_SPEC] * len(after),
        out_specs=[HBM_SPEC] * n, out_shape=[pltpu.HBM(b.shape, b.dtype) for b in bufs],
        input_output_aliases={i: i for i in range(n)},
        compiler_params=pltpu.CompilerParams(has_side_effects=EFFECT),
    )(*bufs, send_sems, recv_sems, *after)
    return list(outs)


def _whole(ref, nrows):
    return ref.at[pl.ds(0, nrows), :]


def _to_sibling(x, y, c):
    return [(2 * q + (1 - c), q, (x, y, 1 - c)) for q in range(4)]


def _to_chips(x, y, c):
    return [(2 * px + py, j, (px, py, c)) for j, (px, py) in enumerate(_other_chips(x, y))]


def _exchange_start(srcs, src_slots, plan, name):
    n = len(srcs)
    rows = [a.shape[0] // src_slots for a in srcs]
    n_copies = len(plan(0, 0, 0))
    lands = [lax.empty((n_copies * r, a.shape[1]), a.dtype) for a, r in zip(srcs, rows)]

    def body(*refs):
        ins, land_refs = refs[:n], refs[n:2 * n]
        send_sems, recv_sems = refs[2 * n], refs[2 * n + 1]
        token = refs[-1]
        for a in range(n):
            r = rows[a]
            for src_slot, dst_slot, target in plan(*_position()):
                pltpu.make_async_remote_copy(
                    src_ref=ins[a].at[pl.ds(src_slot * r, r), :], dst_ref=land_refs[a].at[pl.ds(dst_slot * r, r), :],
                    send_sem=send_sems.at[a], recv_sem=recv_sems.at[a], device_id=target, device_id_type=MESH).start()
        token[...] = jnp.zeros_like(token)

    outs = _call(
        body, name=name, in_specs=[HBM_SPEC] * (2 * n),
        out_specs=[SEM_SPEC, SEM_SPEC] + [HBM_SPEC] * (2 * n) + [TOKEN_SPEC],
        out_shape=[pltpu.SemaphoreType.DMA((n,)), pltpu.SemaphoreType.DMA((n,))]
        + [pltpu.HBM(a.shape, a.dtype) for a in srcs] + [pltpu.HBM(l.shape, l.dtype) for l in lands] + [TOKEN],
        input_output_aliases={i: 2 + i for i in range(2 * n)},
        compiler_params=pltpu.CompilerParams(has_side_effects=EFFECT),
    )(*[_hbm(a) for a in srcs], *[_hbm(l) for l in lands])
    return outs[0], outs[1], list(outs[2:2 + n]), list(outs[2 + n:2 + 2 * n]), outs[-1]


def _exchange_wait(send_sems, recv_sems, srcs, lands, after, name):
    n = len(srcs)

    def body(*refs):
        ins, land_refs = refs[:n], refs[n:2 * n]
        send_sems_ref, recv_sems_ref = refs[2 * n], refs[2 * n + 1]
        for a in range(n):
            allrows = lands[a].shape[0]
            cp = pltpu.make_async_remote_copy(
                src_ref=_whole(ins[a], allrows), dst_ref=_whole(land_refs[a], allrows), send_sem=send_sems_ref.at[a],
                recv_sem=recv_sems_ref.at[a], device_id=_position(), device_id_type=MESH)
            cp.wait_send()
            cp.wait_recv()

    outs = _call(
        body, name=name, in_specs=[HBM_SPEC] * (2 * n) + [SEM_SPEC, SEM_SPEC] + [HBM_SPEC] * len(after),
        out_specs=[HBM_SPEC] * (2 * n),
        out_shape=[pltpu.HBM(a.shape, a.dtype) for a in srcs] + [pltpu.HBM(l.shape, l.dtype) for l in lands],
        input_output_aliases={i: i for i in range(2 * n)},
        compiler_params=pltpu.CompilerParams(has_side_effects=EFFECT),
    )(*srcs, *lands, send_sems, recv_sems, *after)
    return list(outs[:n]), list(outs[n:])


def _chip_partial(grad, recv, c_idx, name):
    r = recv.shape[0] // 4

    def body(c_ref, g_ref, s_ref, o_ref):
        del c_ref
        o_ref[...] = (g_ref[...].astype(F32) + s_ref[...].astype(F32)).astype(BF16)

    grid_spec = pltpu.PrefetchScalarGridSpec(
        num_scalar_prefetch=1, grid=(4,),
        in_specs=[pl.BlockSpec((r, D), lambda q, c_ref: (2 * q + c_ref[0], 0)), pl.BlockSpec((r, D), lambda q, c_ref: (q, 0))],
        out_specs=pl.BlockSpec((r, D), lambda q, c_ref: (q, 0)))
    return _call(body, name=name, grid_spec=grid_spec, out_shape=_sds((4 * r, D), BF16),
                 compiler_params=_params("parallel"))(c_idx, grad, recv)


def _adamw_math(w, g, m, v):
    m2 = B1 * m + (1.0 - B1) * g
    v2 = B2 * v + (1.0 - B2) * jnp.square(g)
    m_hat = m2 / (1.0 - B1 ** STEP)
    v_hat = v2 / (1.0 - B2 ** STEP)
    return -LR * (m_hat / (jnp.sqrt(v_hat) + EPS_ADAM) + WD * w), m2, v2


def _reduce_adamw(w, part, recv, q_idx, m, v, name):
    r = w.shape[0]
    assert part.shape == (4 * r, D) and recv.shape == (3 * r, D) and w.shape == (r, D)
    tr = r // 2
    nb = r // tr

    def body(q_ref, w_ref, p_ref, r0_ref, r1_ref, r2_ref, m_ref, v_ref, g_ref, d_ref, nm_ref, nv_ref):
        del q_ref
        g = ((p_ref[...].astype(F32) + r0_ref[...].astype(F32)) + r1_ref[...].astype(F32)) + r2_ref[...].astype(F32)
        g_ref[...] = g
        d_ref[...], nm_ref[...], nv_ref[...] = _adamw_math(w_ref[...], g, m_ref[...], v_ref[...])

    own = pl.BlockSpec((tr, D), lambda i, q_ref: (i, 0))
    grid_spec = pltpu.PrefetchScalarGridSpec(
        num_scalar_prefetch=1, grid=(nb,),
        in_specs=[own, pl.BlockSpec((tr, D), lambda i, q_ref: (q_ref[0] * nb + i, 0))]
        + [pl.BlockSpec((tr, D), lambda i, q_ref, j=j: (j * nb + i, 0)) for j in range(3)] + [own, own],
        out_specs=[own] * 4)
    return _call(body, name=name, grid_spec=grid_spec, out_shape=[_sds((r, D), F32)] * 4,
                 compiler_params=_params("parallel"))(q_idx, w, part, recv, recv, recv, m, v)


SMALL_ROWS = 8


def _small_all_reduce(pack, name):
    def body(p_ref, tot_ref, loss_ref, gath, send_sems, recv_sems):
        x, y, c = _position()
        me_id = 4 * x + 2 * y + c
        gath[me_id] = p_ref[...]
        copies = []
        for k in range(1, N_DEV):
            peer = tuple(1 - v if (k >> b) & 1 else v for v, b in ((x, 2), (y, 1), (c, 0)))
            cp = pltpu.make_async_remote_copy(src_ref=p_ref, dst_ref=gath.at[me_id], send_sem=send_sems.at[k - 1],
                                              recv_sem=recv_sems.at[k - 1], device_id=peer, device_id_type=MESH)
            cp.start()
            copies.append(cp)
        for cp in copies:
            cp.wait_recv()
        for cp in copies:
            cp.wait_send()
        tot = gath[0]
        for d in range(1, N_DEV):
            tot = tot + gath[d]
        tot_ref[...] = tot
        loss_ref[...] = jnp.full((1, 128), (0.5 / D) * jnp.sum(tot[SMALL_ROWS - 1:SMALL_ROWS, :]), F32)

    vm = pl.BlockSpec(memory_space=pltpu.VMEM)
    return _call(
        body, name=name, in_specs=[vm], out_specs=[vm, vm],
        out_shape=[_sds((SMALL_ROWS, D), F32), _sds((1, 128), F32)],
        scratch_shapes=[pltpu.VMEM((N_DEV, SMALL_ROWS, D), F32), pltpu.SemaphoreType.DMA((N_DEV - 1,)),
                        pltpu.SemaphoreType.DMA((N_DEV - 1,))],
    )(pack)


def _adamw(w, g, m, v, name):
    r, cdim = w.shape
    tr = 256 if r % 256 == 0 else (r // 2 if r % 16 == 0 else r)

    def body(w_ref, g_ref, m_ref, v_ref, d_ref, nm_ref, nv_ref):
        d_ref[...], nm_ref[...], nv_ref[...] = _adamw_math(w_ref[...], g_ref[...], m_ref[...], v_ref[...])

    spec = pl.BlockSpec((tr, cdim), lambda i: (i, 0))
    return _call(
        body, name=name, grid=(r // tr,), in_specs=[spec] * 4, out_specs=[spec] * 3,
        out_shape=[_sds((r, cdim), F32)] * 3, compiler_params=_params("parallel"),
    )(w, g, m, v)


def kernel(x, g_mix, w_in, conv_w, attn_sinks, w_conv_out, w_attn_out, w_o, g_ffn, w_gate_up, w_down, g_final, loss_target, m_g_mix, m_w_in, m_conv_w, m_attn_sinks, m_w_conv_out, m_w_attn_out, m_w_o, m_g_ffn, m_w_gate_up, m_w_down, m_g_final, v_g_mix, v_w_in, v_conv_w, v_attn_sinks, v_w_conv_out, v_w_attn_out, v_w_o, v_g_ffn, v_w_gate_up, v_w_down, v_g_final):
    cx, cy, cc = _position()
    c_idx = jnp.reshape(cc, (1,)).astype(jnp.int32)
    q_idx = jnp.reshape(2 * cx + cy, (1,)).astype(jnp.int32)
    me = 4 * cx + 2 * cy + cc

    me_idx = jnp.reshape(me, (1,)).astype(jnp.int32)
    bufs = [
        _place(jnp.transpose(w_in[0]), me_idx, BF16, "place_w_in"), _place(jnp.pad(conv_w[0], ((0, 5), (0, 0))), me_idx, F32, "place_conv_w"),
        _place(w_conv_out[0], me_idx, BF16, "place_w_conv_out"), _place(w_attn_out[0], me_idx, BF16, "place_w_attn_out"),
        _place(w_o[0], me_idx, BF16, "place_w_o"),
        _place(jnp.transpose(w_gate_up[0]), me_idx, BF16, "place_w_gate_up"), _place(w_down[0], me_idx, BF16, "place_w_down"),
    ]
    members = {"in": [0, 1], "mix": [2, 3, 4], "ffn": [5, 6]}
    sem_pairs, bufs, gather_token = _gather_start(bufs, list(members.values()), "gather_start")

    class Gathered:
        def __init__(self):
            self.state = {g: (sem_pairs[i], [bufs[a] for a in members[g]]) for i, g in enumerate(members)}

        def begin(self, group, after):
            (send_sems, recv_sems), group_bufs = self.state[group]
            send2, recv2, group_bufs, token = _gather_forward(send_sems, recv_sems, group_bufs, after, "gather_forward_" + group)
            self.state[group] = ((send2, recv2), group_bufs)
            return (token,)

        def end(self, group, after):
            (send2, recv2), group_bufs = self.state[group]
            full = _gather_done(send2, recv2, group_bufs, after, "gather_done_" + group)
            if group == "in":
                return full[0], jnp.transpose(full[1].reshape(N_DEV, 8, 128)[:, :3, :], (1, 0, 2)).reshape(3, D)
            return full

    in_flight = {}

    class Reducer:
        def start(self, group, gdict):
            keys, glist = list(gdict), list(gdict.values())
            send_sems, recv_sems, glist, lands, token = _exchange_start(glist, N_DEV, _to_sibling, "rs_sibling_start_" + group)
            in_flight[group] = (keys, send_sems, recv_sems, glist, lands)
            return (token,)

        def middle(self, group, after):
            keys, send_sems, recv_sems, glist, lands = in_flight[group]
            glist, lands = _exchange_wait(send_sems, recv_sems, glist, lands, after, "rs_sibling_wait_" + group)
            parts = [_chip_partial(g, r, c_idx, "chip_partial_" + k) for k, g, r in zip(keys, glist, lands)]
            send_sems, recv_sems, parts, lands, token = _exchange_start(parts, 4, _to_chips, "rs_chips_start_" + group)
            in_flight[group] = (keys, send_sems, recv_sems, parts, lands)
            return (token,)

    dx, _, small = _local_step(x[0], loss_target[0], g_mix, g_ffn, g_final[None], attn_sinks, Gathered(),
                               reducer=Reducer(), after=(gather_token,))

    transposed = ("w_in", "w_gate_up")

    def as2d(k, a):
        if k in transposed:
            return jnp.transpose(a[0])
        return a[None] if a.ndim == 1 else (a[0] if a.ndim == 3 else a)

    w_all = {"g_mix": g_mix, "w_in": w_in, "conv_w": conv_w, "attn_sinks": attn_sinks, "w_conv_out": w_conv_out,
             "w_attn_out": w_attn_out, "w_o": w_o, "g_ffn": g_ffn, "w_gate_up": w_gate_up, "w_down": w_down, "g_final": g_final}
    m_all = {"g_mix": m_g_mix, "w_in": m_w_in, "conv_w": m_conv_w, "attn_sinks": m_attn_sinks, "w_conv_out": m_w_conv_out,
             "w_attn_out": m_w_attn_out, "w_o": m_w_o, "g_ffn": m_g_ffn, "w_gate_up": m_w_gate_up, "w_down": m_w_down,
             "g_final": m_g_final}
    v_all = {"g_mix": v_g_mix, "w_in": v_w_in, "conv_w": v_conv_w, "attn_sinks": v_attn_sinks, "w_conv_out": v_w_conv_out,
             "w_attn_out": v_w_attn_out, "w_o": v_w_o, "g_ffn": v_g_ffn, "w_gate_up": v_w_gate_up, "w_down": v_w_down,
             "g_final": v_g_final}
    results = {}

    def update(k, g=None, part=None, recv=None):
        w2, m2, v2 = as2d(k, w_all[k]), as2d(k, m_all[k]), as2d(k, v_all[k])
        if g is None:
            g, d, nm, nv = _reduce_adamw(w2, part, recv, q_idx, m2, v2, "adamw_" + k)
        else:
            d, nm, nv = _adamw(w2, g, m2, v2, "adamw_" + k)
        results[k] = [(jnp.transpose(val) if k in transposed else val).reshape(w_all[k].shape) for val in (g, d, nm, nv)]
        return nm

    sinks_row = jnp.pad(small["sinks"], ((0, 0), (0, D - 128)))
    pack = jnp.concatenate([small["g_mix"], small["g_ffn"], small["g_final"], small["conv_w"], sinks_row, small["lossvec"]], axis=0)
    tot, loss_row = _small_all_reduce(pack, "small_all_reduce")
    loss = loss_row[0, 0]
    g_small = {
        "g_mix": tot[0:1], "g_ffn": tot[1:2], "g_final": tot[2:3],
        "conv_w": lax.dynamic_slice(tot, (3, me * 128), (3, 128)), "attn_sinks": tot[6:7, :N_HEADS],
    }
    after = tuple(update(k, g) for k, g in g_small.items())

    kernel_name = {"win_t": "w_in", "wgu_t": "w_gate_up", "wd": "w_down", "wco": "w_conv_out", "wao": "w_attn_out", "wo": "w_o"}
    for group in ("ffn", "mix", "in"):
        keys, send_sems, recv_sems, parts, lands = in_flight[group]
        parts, lands = _exchange_wait(send_sems, recv_sems, parts, lands, after + (dx,), "rs_chips_wait_" + group)
        after = tuple(update(kernel_name[k], part=p, recv=r) for k, p, r in zip(keys, parts, lands))

    order = ["g_mix", "w_in", "conv_w", "attn_sinks", "w_conv_out", "w_attn_out", "w_o", "g_ffn", "w_gate_up", "w_down", "g_final"]
    return (loss, dx[None], *[results[k][i] for i in range(4) for k in order])
```

```python
import functools
import math

import jax
import jax.numpy as jnp
from jax import lax
from jax.experimental import pallas as pl
from jax.experimental.pallas import tpu as pltpu

F32 = jnp.float32
BF16 = jnp.bfloat16

D = 1024
HEAD_DIM = 64
N_HEADS = 16
N_KV = 4
GROUP = N_HEADS // N_KV
D_KV = N_KV * HEAD_DIM
BLOCK = 128
ROT_DIM = HEAD_DIM // 4
ROPE_THETA = 500000.0
ATTN_SCALE = 1.0 / math.sqrt(HEAD_DIM)
NEG_INF = -1e30
D_FF = 2816
N_IN = 6656
EPS = 1e-5
C_CB, C_CC, C_CX, C_Q, C_K, C_V, C_GC, C_GA = 0, 1024, 2048, 3072, 4096, 4352, 4608, 5632

LR, B1, B2, EPS_ADAM, WD, STEP = 0.001, 0.9, 0.999, 1e-08, 0.01, 10

N_DEV = 8
MESH = pl.DeviceIdType.MESH
VMEM_LIMIT = 56 * 1024 * 1024

NN = (((1,), (0,)), ((), ()))
NT = (((1,), (1,)), ((), ()))
TN = (((0,), (0,)), ((), ()))
HBM_SPEC = pl.BlockSpec(memory_space=pl.ANY)


def _call(body, **kw):
    return pl.pallas_call(body, **kw)


def _params(*sem):
    return pltpu.CompilerParams(dimension_semantics=sem, vmem_limit_bytes=VMEM_LIMIT)


def _sds(shape, dtype):
    return jax.ShapeDtypeStruct(shape, dtype)


def _matmul(a, b, *, mode, tm, tn, tk, out_dtype, name, res=None, after=()):
    parts = list(a) if isinstance(a, (list, tuple)) else [a]
    rows_a = parts[0].shape[0]
    cols_a = sum(p.shape[1] for p in parts)
    if mode == "nn":
        (m, kk), (_, n), dims = (rows_a, cols_a), b.shape, NN
    elif mode == "nt":
        (m, kk), (n, _), dims = (rows_a, cols_a), b.shape, NT
    else:
        (kk, m), (_, n), dims = (rows_a, cols_a), b.shape, TN
    tm, tn, tk = min(tm, m), min(tn, n), min(tk, kk)
    assert m % tm == 0 and n % tn == 0 and kk % tk == 0, (name, m, n, kk, tm, tn, tk)
    nk = kk // tk
    split_axis, width = (2, tk) if mode == "nn" else (0, tm)
    assert len(parts) == 1 or mode in ("nn", "tn")
    assert len(parts) == 1 or all(p.shape[1] % width == 0 for p in parts), (name, width)
    counts = [p.shape[1] // width for p in parts]
    starts = [sum(counts[:p]) for p in range(len(parts))]

    def a_spec(p):
        def col(t):
            return jnp.clip(t - starts[p], 0, counts[p] - 1) if len(parts) > 1 else t

        if mode == "tn":
            return pl.BlockSpec((tk, tm), lambda i, j, k: (k, col(i)))
        return pl.BlockSpec((tm, tk), lambda i, j, k: (i, col(k)))

    if mode == "nt":
        b_spec = pl.BlockSpec((tn, tk), lambda i, j, k: (j, k))
    else:
        b_spec = pl.BlockSpec((tk, tn), lambda i, j, k: (k, j))
    o_spec = pl.BlockSpec((tm, tn), lambda i, j, k: (i, j))
    has_res = res is not None
    n_parts = len(parts)

    def body(*refs):
        a_refs, b_ref = refs[:n_parts], refs[n_parts]
        r_ref = refs[n_parts + 1] if has_res else None
        o_ref = refs[n_parts + 1 + has_res + len(after)]
        k = pl.program_id(2)

        def step(a_ref):
            part = lax.dot_general(a_ref[...], b_ref[...], dims, preferred_element_type=F32)

            def finish(acc):
                if has_res:
                    acc = acc + r_ref[...]
                o_ref[...] = acc.astype(o_ref.dtype)

            if nk == 1:
                finish(part)
            else:
                acc_ref = refs[-1]

                @pl.when(k == 0)
                def _():
                    acc_ref[...] = part

                @pl.when(k > 0)
                def _():
                    acc_ref[...] += part

                @pl.when(k == nk - 1)
                def _():
                    finish(acc_ref[...])

        if n_parts == 1:
            step(a_refs[0])
        else:
            t = pl.program_id(split_axis)
            for p in range(n_parts):
                pl.when((t >= starts[p]) & (t < starts[p] + counts[p]))(functools.partial(step, a_refs[p]))

    ins = parts + [b] + ([res] if has_res else []) + list(after)
    in_specs = [a_spec(p) for p in range(n_parts)] + [b_spec] + ([o_spec] if has_res else []) + [HBM_SPEC] * len(after)
    scratch = [] if nk == 1 else [pltpu.VMEM((tm, tn), F32)]
    return _call(
        body, name=name, grid=(m // tm, n // tn, nk), in_specs=in_specs, out_specs=o_spec,
        out_shape=_sds((m, n), out_dtype), scratch_shapes=scratch,
        compiler_params=_params("parallel", "parallel", "arbitrary"),
    )(*ins)


def _row_tile(s):
    return min(256, s)


def _rms_fwd(x, g, name, after=()):
    s = x.shape[0]
    tm = _row_tile(s)

    def body(x_ref, g_ref, *rest):
        h_ref = rest[-1]
        xv = x_ref[...]
        r = lax.rsqrt(jnp.mean(xv * xv, axis=-1, keepdims=True) + EPS)
        h_ref[...] = (xv * r * g_ref[...]).astype(BF16)

    row = pl.BlockSpec((tm, D), lambda i: (i, 0))
    return _call(
        body, name=name, grid=(s // tm,), in_specs=[row, pl.BlockSpec((1, D), lambda i: (0, 0))] + [HBM_SPEC] * len(after),
        out_specs=row, out_shape=_sds((s, D), BF16), compiler_params=_params("parallel"),
    )(x, g, *after)


def _rms_bwd(dh, x, g, dres, name, after=()):
    s = x.shape[0]
    tm = _row_tile(s)

    def body(dh_ref, x_ref, g_ref, dres_ref, *rest):
        dx_ref, dxb_ref, dg_ref = rest[len(after):]
        xv = x_ref[...]
        r = lax.rsqrt(jnp.mean(xv * xv, axis=-1, keepdims=True) + EPS)
        xh = xv * r
        dhv = dh_ref[...]
        dyg = dhv * g_ref[...]
        dx = dres_ref[...] + r * (dyg - xh * jnp.mean(dyg * xh, axis=-1, keepdims=True))
        dx_ref[...] = dx
        dxb_ref[...] = dx.astype(BF16)
        part = jnp.sum(dhv * xh, axis=0, keepdims=True)

        @pl.when(pl.program_id(0) == 0)
        def _():
            dg_ref[...] = part

        @pl.when(pl.program_id(0) > 0)
        def _():
            dg_ref[...] += part

    row = pl.BlockSpec((tm, D), lambda i: (i, 0))
    vec = pl.BlockSpec((1, D), lambda i: (0, 0))
    return _call(
        body, name=name, grid=(s // tm,), in_specs=[row, row, vec, row] + [HBM_SPEC] * len(after), out_specs=[row, row, vec],
        out_shape=[_sds((s, D), F32), _sds((s, D), BF16), _sds((1, D), F32)],
        compiler_params=_params("arbitrary"),
    )(dh, x, g, dres, *after)


def _loss_head(x2, g, tgt, name):
    s = x2.shape[0]
    tm = _row_tile(s)

    def body(x_ref, g_ref, t_ref, dx_ref, dxb_ref, dg_ref, l_ref):
        xv = x_ref[...]
        gv = g_ref[...]
        r = lax.rsqrt(jnp.mean(xv * xv, axis=-1, keepdims=True) + EPS)
        xh = xv * r
        err = xh * gv - t_ref[...]
        dy = err * (1.0 / D)
        dyg = dy * gv
        dx = r * (dyg - xh * jnp.mean(dyg * xh, axis=-1, keepdims=True))
        dx_ref[...] = dx
        dxb_ref[...] = dx.astype(BF16)
        dg_part = jnp.sum(dy * xh, axis=0, keepdims=True)
        l_part = jnp.sum(err * err, axis=0, keepdims=True)

        @pl.when(pl.program_id(0) == 0)
        def _():
            dg_ref[...] = dg_part
            l_ref[...] = l_part

        @pl.when(pl.program_id(0) > 0)
        def _():
            dg_ref[...] += dg_part
            l_ref[...] += l_part

    row = pl.BlockSpec((tm, D), lambda i: (i, 0))
    vec = pl.BlockSpec((1, D), lambda i: (0, 0))
    return _call(
        body, name=name, grid=(s // tm,), in_specs=[row, vec, row], out_specs=[row, row, vec, vec],
        out_shape=[_sds((s, D), F32), _sds((s, D), BF16), _sds((1, D), F32), _sds((1, D), F32)],
        compiler_params=_params("arbitrary"),
    )(x2, g, tgt)


CONV_TC = 256


def _shift_down(u, k, rows):
    return jnp.where(rows >= k, pltpu.roll(u, k, 0), 0.0)


def _shift_up(u, k, rows, s):
    return jnp.where(rows < s - k, pltpu.roll(u, s - k, 0), 0.0)


def _conv_specs(s):
    nb = D // CONV_TC

    def col(c0):
        return pl.BlockSpec((s, CONV_TC), lambda j, c0=c0: (0, c0 // CONV_TC + j))

    return nb, col


def _conv_fwd(proj, conv_w, name):
    s = proj.shape[0]
    nb, col = _conv_specs(s)

    def body(cb_ref, cc_ref, cx_ref, w_ref, y_ref):
        rows = lax.broadcasted_iota(jnp.int32, (s, CONV_TC), 0)
        u = cc_ref[...].astype(F32) * cx_ref[...].astype(F32)
        w = w_ref[...]
        c = w[0:1] * _shift_down(u, 2, rows) + w[1:2] * _shift_down(u, 1, rows) + w[2:3] * u
        y_ref[...] = (cb_ref[...].astype(F32) * c).astype(BF16)

    return _call(
        body, name=name, grid=(nb,),
        in_specs=[col(C_CB), col(C_CC), col(C_CX), pl.BlockSpec((3, CONV_TC), lambda j: (0, j))],
        out_specs=pl.BlockSpec((s, CONV_TC), lambda j: (0, j)), out_shape=_sds((s, D), BF16),
        compiler_params=_params("parallel"),
    )(proj, proj, proj, conv_w)


def _conv_bwd(dy, proj, conv_w, dproj, name, after=()):
    s = proj.shape[0]
    nb, col = _conv_specs(s)

    def body(dy_ref, cb_ref, cc_ref, cx_ref, w_ref, *rest):
        dproj_ref, dw_ref, buf, sems = rest[1 + len(after):]
        j = pl.program_id(0)
        rows = lax.broadcasted_iota(jnp.int32, (s, CONV_TC), 0)
        cc = cc_ref[...].astype(F32)
        cx = cx_ref[...].astype(F32)
        u = cc * cx
        u1 = _shift_down(u, 1, rows)
        u2 = _shift_down(u, 2, rows)
        w = w_ref[...]
        c = w[0:1] * u2 + w[1:2] * u1 + w[2:3] * u
        dyv = dy_ref[...].astype(F32)
        dc = dyv * cb_ref[...].astype(F32)
        du = w[2:3] * dc + w[1:2] * _shift_up(dc, 1, rows, s) + w[0:1] * _shift_up(dc, 2, rows, s)
        buf[0] = (dyv * c).astype(BF16)
        buf[1] = (du * cx).astype(BF16)
        buf[2] = (du * cc).astype(BF16)
        dw_ref[...] = jnp.concatenate(
            [jnp.sum(dc * u2, axis=0, keepdims=True), jnp.sum(dc * u1, axis=0, keepdims=True),
             jnp.sum(dc * u, axis=0, keepdims=True)], axis=0)
        copies = []
        for p, c0 in enumerate((C_CB, C_CC, C_CX)):
            start = pl.multiple_of(c0 + j * CONV_TC, CONV_TC)
            copies.append(pltpu.make_async_copy(buf.at[p], dproj_ref.at[:, pl.ds(start, CONV_TC)], sems.at[p]))
        for cp in copies:
            cp.start()
        for cp in copies:
            cp.wait()

    return _call(
        body, name=name, grid=(nb,),
        in_specs=[pl.BlockSpec((s, CONV_TC), lambda j: (0, j)), col(C_CB), col(C_CC), col(C_CX),
                  pl.BlockSpec((3, CONV_TC), lambda j: (0, j))] + [HBM_SPEC] * (1 + len(after)),
        out_specs=[pl.BlockSpec(memory_space=pl.ANY), pl.BlockSpec((3, CONV_TC), lambda j: (0, j))],
        out_shape=[_sds((s, N_IN), BF16), _sds((3, D), F32)],
        scratch_shapes=[pltpu.VMEM((3, s, CONV_TC), BF16), pltpu.SemaphoreType.DMA((3,))],
        input_output_aliases={5: 0}, compiler_params=_params("arbitrary"),
    )(dy, proj, proj, proj, conv_w, dproj, *after)


def _rope_tables(s):
    half = ROT_DIM // 2
    inv_freq = ROPE_THETA ** (-jnp.arange(0, ROT_DIM, 2, dtype=F32) / ROT_DIM)
    inv64 = jnp.concatenate([inv_freq, inv_freq, jnp.zeros((HEAD_DIM - ROT_DIM,), F32)])
    ang = jnp.arange(s, dtype=F32)[:, None] * jnp.concatenate([inv64, inv64])[None, :]
    d = lax.broadcasted_iota(jnp.int32, (s, 128), 1) % HEAD_DIM
    cos, sin = jnp.cos(ang), jnp.sin(ang)
    c = jnp.where(d < ROT_DIM, cos, 1.0)
    a = jnp.where(d < half, -sin, 0.0)
    b = jnp.where((d >= half) & (d < ROT_DIM), sin, 0.0)
    return jnp.concatenate([c, a, b], axis=1)


def _rope(x, tab):
    c, a, b = tab[:, 0:128], tab[:, 128:256], tab[:, 256:384]
    outs = []
    for i in range(x.shape[1] // 128):
        xc = x[:, i * 128:(i + 1) * 128]
        outs.append(xc * c + pltpu.roll(xc, 120, 1) * a + pltpu.roll(xc, 8, 1) * b)
    return outs[0] if len(outs) == 1 else jnp.concatenate(outs, axis=1)


def _rope_t(dx, tab):
    c, a, b = tab[:, 0:128], tab[:, 128:256], tab[:, 256:384]
    outs = []
    for i in range(dx.shape[1] // 128):
        dc = dx[:, i * 128:(i + 1) * 128]
        outs.append(dc * c + pltpu.roll(dc * a, 8, 1) + pltpu.roll(dc * b, 120, 1))
    return outs[0] if len(outs) == 1 else jnp.concatenate(outs, axis=1)


def _attn_mask(n):
    qi = lax.broadcasted_iota(jnp.int32, (GROUP * BLOCK, 2 * BLOCK), 0) & (BLOCK - 1)
    kj = lax.broadcasted_iota(jnp.int32, (GROUP * BLOCK, 2 * BLOCK), 1)
    rel = qi + BLOCK - kj
    return (rel >= 0) & (rel < BLOCK) & ((kj >= BLOCK) | (n > 0))


def _sink_col(sink_ref, hk):
    return jnp.concatenate([jnp.full((BLOCK, 1), sink_ref[0, hk * GROUP + g], F32) for g in range(GROUP)], axis=0)


def _attn_in_specs():
    prev = lambda n: jnp.maximum(n - 1, 0)
    return [
        pl.BlockSpec((BLOCK, D), lambda n: (n, C_Q // D)),
        pl.BlockSpec((BLOCK, D_KV), lambda n: (n, C_K // D_KV)),
        pl.BlockSpec((BLOCK, D_KV), lambda n: (prev(n), C_K // D_KV)),
        pl.BlockSpec((BLOCK, D_KV), lambda n: (n, C_V // D_KV)),
        pl.BlockSpec((BLOCK, D_KV), lambda n: (prev(n), C_V // D_KV)),
        pl.BlockSpec((BLOCK, 384), lambda n: (n, 0)),
        pl.BlockSpec((BLOCK, 384), lambda n: (prev(n), 0)),
        pl.BlockSpec(memory_space=pltpu.SMEM),
    ]


def _load_qkv(q_ref, kc_ref, kp_ref, vc_ref, vp_ref, tc_ref, tp_ref):
    q = _rope(q_ref[...].astype(F32), tc_ref[...]).astype(BF16)
    kc = _rope(kc_ref[...].astype(F32), tc_ref[...]).astype(BF16)
    kp = _rope(kp_ref[...].astype(F32), tp_ref[...]).astype(BF16)
    return q, kc, kp, vc_ref[...], vp_ref[...]


def _group_rows(x, hk):
    base = hk * GROUP * HEAD_DIM
    return jnp.concatenate([x[:, base + g * HEAD_DIM: base + (g + 1) * HEAD_DIM] for g in range(GROUP)], axis=0)


def _kv_rows(prev, cur, hk):
    sl = slice(hk * HEAD_DIM, (hk + 1) * HEAD_DIM)
    return jnp.concatenate([prev[:, sl], cur[:, sl]], axis=0)


def _attn_bwd(do, proj, tab, sinks, dproj, name, after=()):
    s = proj.shape[0]
    nblk = s // BLOCK

    def body(do_ref, q_ref, kc_ref, kp_ref, vc_ref, vp_ref, tc_ref, tp_ref, sink_ref, *rest):
        dproj_ref, dk_ref, dv_ref, ds_ref, dqbuf, dqout, dkbuf, dvbuf, sem = rest[1 + len(after):]
        n = pl.program_id(0)

        @pl.when(n == 0)
        def _():
            dk_ref[...] = jnp.zeros_like(dk_ref)
            dv_ref[...] = jnp.zeros_like(dv_ref)
            ds_ref[...] = jnp.zeros_like(ds_ref)

        q, kc, kp, vc, vp = _load_qkv(q_ref, kc_ref, kp_ref, vc_ref, vp_ref, tc_ref, tp_ref)
        dov = do_ref[...]
        mask = _attn_mask(n)
        rows = GROUP * BLOCK
        head_off = lax.broadcasted_iota(jnp.int32, (rows, 128), 1) - (lax.broadcasted_iota(jnp.int32, (rows, 128), 0) >> 7)
        dsink_row = jnp.zeros((1, 128), F32)
        prev0 = pl.multiple_of(jnp.maximum(n - 1, 0) * BLOCK, BLOCK)
        cur0 = pl.multiple_of(n * BLOCK, BLOCK)
        for hk in range(N_KV):
            qg = _group_rows(q, hk)
            dog = _group_rows(dov, hk)
            kcat = _kv_rows(kp, kc, hk)
            vcat = _kv_rows(vp, vc, hk)
            sc = lax.dot_general(qg, kcat, NT, preferred_element_type=F32) * ATTN_SCALE
            sc = jnp.where(mask, sc, NEG_INF)
            sink = _sink_col(sink_ref, hk)
            m = jnp.maximum(jnp.max(sc, axis=1, keepdims=True), sink)
            e = jnp.exp(sc - m)
            es = jnp.exp(sink - m)
            inv = 1.0 / (jnp.sum(e, axis=1, keepdims=True) + es)
            p = e * inv
            pb = p.astype(BF16)
            dp = lax.dot_general(dog, vcat, NT, preferred_element_type=F32)
            delta = jnp.sum(p * dp, axis=1, keepdims=True)
            dsc = (p * (dp - delta) * ATTN_SCALE).astype(BF16)
            dsk = -(es * inv) * delta
            dsink_row = dsink_row + jnp.sum(jnp.where(head_off == hk * GROUP, dsk, 0.0), axis=0, keepdims=True)
            dqg = lax.dot_general(dsc, kcat, NN, preferred_element_type=F32)
            dkcat = lax.dot_general(dsc, qg, TN, preferred_element_type=F32)
            dvcat = lax.dot_general(pb, dog, TN, preferred_element_type=F32)
            base = hk * GROUP * HEAD_DIM
            for g in range(GROUP):
                dqbuf[:, base + g * HEAD_DIM: base + (g + 1) * HEAD_DIM] = dqg[g * BLOCK:(g + 1) * BLOCK]
            sl = slice(hk * HEAD_DIM, (hk + 1) * HEAD_DIM)
            dkbuf[:, sl] = dkcat
            dvbuf[:, sl] = dvcat

        @pl.when(n > 0)
        def _():
            dk_ref[pl.ds(prev0, BLOCK), :] += dkbuf[0:BLOCK, :]
            dv_ref[pl.ds(prev0, BLOCK), :] += dvbuf[0:BLOCK, :]

        dk_ref[pl.ds(cur0, BLOCK), :] += dkbuf[BLOCK:2 * BLOCK, :]
        dv_ref[pl.ds(cur0, BLOCK), :] += dvbuf[BLOCK:2 * BLOCK, :]
        ds_ref[...] += dsink_row
        dqout[...] = _rope_t(dqbuf[...], tc_ref[...]).astype(BF16)
        cp = pltpu.make_async_copy(dqout, dproj_ref.at[pl.ds(cur0, BLOCK), pl.ds(C_Q, D)], sem)
        cp.start()
        cp.wait()

    blk = lambda w: pl.BlockSpec((BLOCK, w), lambda n: (n, 0))
    whole = lambda w: pl.BlockSpec((s, w), lambda n: (0, 0))
    anyspec = pl.BlockSpec(memory_space=pl.ANY)
    n_in = 1 + len(_attn_in_specs())
    return _call(
        body, name=name, grid=(nblk,), in_specs=[blk(D)] + _attn_in_specs() + [anyspec] * (1 + len(after)),
        out_specs=[anyspec, whole(D_KV), whole(D_KV), pl.BlockSpec((1, 128), lambda n: (0, 0))],
        out_shape=[_sds((s, N_IN), BF16), _sds((s, D_KV), F32), _sds((s, D_KV), F32), _sds((1, 128), F32)],
        scratch_shapes=[pltpu.VMEM((BLOCK, D), F32), pltpu.VMEM((BLOCK, D), BF16), pltpu.VMEM((2 * BLOCK, D_KV), F32),
                        pltpu.VMEM((2 * BLOCK, D_KV), F32), pltpu.SemaphoreType.DMA(())],
        input_output_aliases={n_in: 0}, compiler_params=_params("arbitrary"),
    )(do, proj, proj, proj, proj, proj, tab, tab, sinks, dproj, *after)


HALF = HEAD_DIM
N_CHUNK = D // 128


def _swa_bias(n):
    qi = lax.broadcasted_iota(jnp.int32, (BLOCK, 2 * BLOCK), 0)
    kj = lax.broadcasted_iota(jnp.int32, (BLOCK, 2 * BLOCK), 1)
    rel = qi + BLOCK - kj
    valid = (rel >= 0) & (rel < BLOCK) & ((kj >= BLOCK) | (n > 0))
    return jnp.where(valid, 0.0, NEG_INF)


def _halves(x):
    lo = lax.broadcasted_iota(jnp.int32, x.shape, 1) < HALF
    return jnp.where(lo, x, 0.0).astype(BF16), jnp.where(lo, 0.0, x).astype(BF16)


def _dup_heads(x):
    out = []
    for pair in range(N_KV // 2):
        xc = x[:, pair * 128:(pair + 1) * 128]
        xr = pltpu.roll(xc, HALF, 1)
        lo = lax.broadcasted_iota(jnp.int32, xc.shape, 1) < HALF
        out += [jnp.where(lo, xc, xr), jnp.where(lo, xr, xc)]
    return out


def _swa_load(q_ref, kc_ref, kp_ref, vc_ref, vp_ref, tc_ref, tp_ref):
    qf = _rope(q_ref[...].astype(F32), tc_ref[...]) * ATTN_SCALE
    q_halves = [_halves(qf[:, c * 128:(c + 1) * 128]) for c in range(N_CHUNK)]
    kf = jnp.concatenate([_rope(kp_ref[...].astype(F32), tp_ref[...]), _rope(kc_ref[...].astype(F32), tc_ref[...])], axis=0)
    vf = jnp.concatenate([vp_ref[...], vc_ref[...]], axis=0).astype(F32)
    return q_halves, _dup_heads(kf), _dup_heads(vf)


def _swa_probs(qh, kk, bias, sink):
    s = lax.dot_general(qh, kk, NT, preferred_element_type=F32) + bias
    m = jnp.maximum(jnp.max(jnp.maximum(s[:, :BLOCK], s[:, BLOCK:]), axis=1, keepdims=True), sink)
    return jnp.exp(s - m), m


def _swa_fwd(proj, tab, sinks, name, after=()):
    s = proj.shape[0]

    def body(q_ref, kc_ref, kp_ref, vc_ref, vp_ref, tc_ref, tp_ref, sink_ref, *rest):
        o_ref = rest[-1]
        n = pl.program_id(0)
        q_halves, kdup, vdup = _swa_load(q_ref, kc_ref, kp_ref, vc_ref, vp_ref, tc_ref, tp_ref)
        bias = _swa_bias(n)
        ones = jnp.ones((2 * BLOCK, 128), BF16)
        for c in range(N_CHUNK):
            hk = c // (GROUP // 2)
            kk = kdup[hk].astype(BF16)
            acc = None
            for half, v_half in enumerate(_halves(vdup[hk])):
                sink = sink_ref[0, 2 * c + half]
                e, m = _swa_probs(q_halves[c][half], kk, bias, sink)
                o = lax.dot_general(e.astype(BF16), jnp.concatenate([v_half, ones], axis=1), NN, preferred_element_type=F32)
                part = o[:, :128] * (1.0 / (o[:, 128:] + jnp.exp(sink - m)))
                acc = part if acc is None else acc + part
            o_ref[:, c * 128:(c + 1) * 128] = acc.astype(BF16)

    return _call(
        body, name=name, grid=(s // BLOCK,), in_specs=_attn_in_specs() + [HBM_SPEC] * len(after),
        out_specs=pl.BlockSpec((BLOCK, D), lambda n: (n, 0)), out_shape=_sds((s, D), BF16),
        compiler_params=_params("parallel"),
    )(proj, proj, proj, proj, proj, tab, tab, sinks, *after)


def _kv_bwd(dkr, dv, tab, dproj, name):
    s = dkr.shape[0]
    tm = _row_tile(s)

    def body(dk_ref, dv_ref, t_ref, dproj_in, o_ref):
        del dproj_in
        o_ref[:, 0:D_KV] = _rope_t(dk_ref[...], t_ref[...]).astype(BF16)
        o_ref[:, D_KV:2 * D_KV] = dv_ref[...].astype(BF16)

    row = lambda w: pl.BlockSpec((tm, w), lambda i: (i, 0))
    return _call(
        body, name=name, grid=(s // tm,),
        in_specs=[row(D_KV), row(D_KV), row(384), pl.BlockSpec(memory_space=pl.ANY)],
        out_specs=pl.BlockSpec((tm, 2 * D_KV), lambda i: (i, C_K // (2 * D_KV))),
        out_shape=_sds((s, N_IN), BF16), input_output_aliases={3: 0}, compiler_params=_params("parallel"),
    )(dkr, dv, tab, dproj)


EW_TC = 512


def _sigmoid(x):
    return 0.5 * jnp.tanh(0.5 * x) + 0.5


def _merge_fwd(proj, conv_out, attn_out, name):
    s = proj.shape[0]
    tm = _row_tile(s)
    tile = pl.BlockSpec((tm, EW_TC), lambda i, j: (i, j))

    def body(gc_ref, ga_ref, co_ref, ao_ref, o_ref):
        o_ref[...] = (_sigmoid(gc_ref[...].astype(F32)) * co_ref[...].astype(F32)
                      + _sigmoid(ga_ref[...].astype(F32)) * ao_ref[...].astype(F32)).astype(BF16)

    return _call(
        body, name=name, grid=(s // tm, D // EW_TC),
        in_specs=[pl.BlockSpec((tm, EW_TC), lambda i, j: (i, C_GC // EW_TC + j)),
                  pl.BlockSpec((tm, EW_TC), lambda i, j: (i, C_GA // EW_TC + j)), tile, tile],
        out_specs=tile, out_shape=_sds((s, D), BF16), compiler_params=_params("parallel", "parallel"),
    )(proj, proj, conv_out, attn_out)


def _merge_bwd(dmerged, proj, conv_out, attn_out, name):
    s = proj.shape[0]
    tm = _row_tile(s)
    tile = pl.BlockSpec((tm, EW_TC), lambda i, j: (i, j))
    anyspec = pl.BlockSpec(memory_space=pl.ANY)

    def body(dm_ref, gc_ref, ga_ref, co_ref, ao_ref, dproj_ref, dco_ref, dao_ref, buf, sems):
        i, j = pl.program_id(0), pl.program_id(1)
        dm = dm_ref[...].astype(F32)
        sc = _sigmoid(gc_ref[...].astype(F32))
        sa = _sigmoid(ga_ref[...].astype(F32))
        dco_ref[...] = (dm * sc).astype(BF16)
        dao_ref[...] = (dm * sa).astype(BF16)
        buf[0] = (dm * co_ref[...].astype(F32) * sc * (1.0 - sc)).astype(BF16)
        buf[1] = (dm * ao_ref[...].astype(F32) * sa * (1.0 - sa)).astype(BF16)
        r0 = pl.multiple_of(i * tm, tm)
        copies = []
        for p, c0 in enumerate((C_GC, C_GA)):
            start = pl.multiple_of(c0 + j * EW_TC, EW_TC)
            copies.append(pltpu.make_async_copy(buf.at[p], dproj_ref.at[pl.ds(r0, tm), pl.ds(start, EW_TC)], sems.at[p]))
        for cp in copies:
            cp.start()
        for cp in copies:
            cp.wait()

    return _call(
        body, name=name, grid=(s // tm, D // EW_TC),
        in_specs=[tile, pl.BlockSpec((tm, EW_TC), lambda i, j: (i, C_GC // EW_TC + j)),
                  pl.BlockSpec((tm, EW_TC), lambda i, j: (i, C_GA // EW_TC + j)), tile, tile],
        out_specs=[anyspec, tile, tile],
        out_shape=[_sds((s, N_IN), BF16), _sds((s, D), BF16), _sds((s, D), BF16)],
        scratch_shapes=[pltpu.VMEM((2, tm, EW_TC), BF16), pltpu.SemaphoreType.DMA((2,))],
        compiler_params=_params("arbitrary", "arbitrary"),
    )(dmerged, proj, proj, conv_out, attn_out)


FF_TC = 256


def _gate_up_fwd(h2, wgu_t, name):
    s = h2.shape[0]
    tm = min(2048, s)
    nb = D_FF // FF_TC

    def body(h_ref, wg_ref, wu_ref, g_ref, u_ref, a_ref):
        h = h_ref[...]
        g = lax.dot_general(h, wg_ref[...], NT, preferred_element_type=F32)
        u = lax.dot_general(h, wu_ref[...], NT, preferred_element_type=F32)
        g_ref[...] = g.astype(BF16)
        u_ref[...] = u.astype(BF16)
        a_ref[...] = (g * _sigmoid(g) * u).astype(BF16)

    tile = pl.BlockSpec((tm, FF_TC), lambda i, j: (i, j))
    return _call(
        body, name=name, grid=(s // tm, nb),
        in_specs=[pl.BlockSpec((tm, D), lambda i, j: (i, 0)), pl.BlockSpec((FF_TC, D), lambda i, j: (j, 0)),
                  pl.BlockSpec((FF_TC, D), lambda i, j: (nb + j, 0))],
        out_specs=[tile, tile, tile], out_shape=[_sds((s, D_FF), BF16)] * 3,
        compiler_params=_params("parallel", "parallel"),
    )(h2, wgu_t, wgu_t)


def _down_bwd_x(dx2b, wd, gate, up, name):
    s = dx2b.shape[0]
    tm = min(2048, s)
    nb = D_FF // FF_TC

    def body(dx_ref, w_ref, g_ref, u_ref, dg_ref, du_ref):
        da = lax.dot_general(dx_ref[...], w_ref[...], NT, preferred_element_type=F32)
        g = g_ref[...].astype(F32)
        sg = _sigmoid(g)
        dg_ref[...] = (da * u_ref[...].astype(F32) * (sg * (1.0 + g * (1.0 - sg)))).astype(BF16)
        du_ref[...] = (da * (g * sg)).astype(BF16)

    tile = pl.BlockSpec((tm, FF_TC), lambda i, j: (i, j))
    return _call(
        body, name=name, grid=(s // tm, nb),
        in_specs=[pl.BlockSpec((tm, D), lambda i, j: (i, 0)), pl.BlockSpec((FF_TC, D), lambda i, j: (j, 0)), tile, tile],
        out_specs=[tile, tile], out_shape=[_sds((s, D_FF), BF16)] * 2,
        compiler_params=_params("parallel", "parallel"),
    )(dx2b, wd, gate, up)


class _Weights:
    def __init__(self, **groups):
        self.groups = groups

    def begin(self, group, after):
        return ()

    def end(self, group, after):
        return self.groups[group]


class _NoReduce:
    def start(self, group, grads):
        return ()

    def middle(self, group, after):
        return ()


def _local_step(x, tgt, g_mix, g_ffn, g_final, sinks, weights, reducer=None, after=()):
    reducer = reducer or _NoReduce()
    s = x.shape[0]
    tab = _rope_tables(s)
    big = dict(tm=1024, tn=512, tk=1024)
    h1 = _rms_fwd(x, g_mix, "rms1_fwd", after=after)
    win_t, conv_w = weights.end("in", weights.begin("in", (h1,)))
    proj = _matmul(h1, win_t, mode="nt", out_dtype=BF16, name="proj_fwd", tm=2048, tn=512, tk=1024)
    attn = _swa_fwd(proj, tab, sinks, "attn_fwd", after=weights.begin("mix", (proj,)))
    wco, wao, wo = weights.end("mix", (attn,))
    conv_y = _conv_fwd(proj, conv_w, "conv_fwd")
    conv_out = _matmul(conv_y, wco, mode="nn", out_dtype=BF16, name="conv_out_fwd", **big)
    attn_out = _matmul(attn, wao, mode="nn", out_dtype=BF16, name="attn_out_fwd", **big)
    merged = _merge_fwd(proj, conv_out, attn_out, "merge_fwd")
    x1 = _matmul(merged, wo, mode="nn", out_dtype=F32, name="wo_fwd", res=x, after=weights.begin("ffn", (merged,)), **big)
    h2 = _rms_fwd(x1, g_ffn, "rms2_fwd")
    wgu_t, wd = weights.end("ffn", (h2,))
    gate, up, act = _gate_up_fwd(h2, wgu_t, "gate_up_fwd")
    x2 = _matmul(act, wd, mode="nn", out_dtype=F32, name="down_fwd", res=x1, tm=1024, tn=512, tk=D_FF)
    dx2, dx2b, dg_final, lossvec = _loss_head(x2, g_final, tgt, "loss_head")
    dgate, dup = _down_bwd_x(dx2b, wd, gate, up, "down_bwd_x")
    g_wd = _matmul(act, dx2b, mode="tn", out_dtype=BF16, name="down_bwd_w", tm=1408, tn=1024, tk=2048)
    dh2 = _matmul([dgate, dup], wgu_t, mode="nn", out_dtype=F32, name="gate_up_bwd_x", tm=1024, tn=1024, tk=1408)
    g_wgu_t = _matmul([dgate, dup], h2, mode="tn", out_dtype=BF16, name="gate_up_bwd_w", tm=1408, tn=1024, tk=2048)
    after_ffn = reducer.start("ffn", dict(wgu_t=g_wgu_t, wd=g_wd))
    dx1, dx1b, dg_ffn = _rms_bwd(dh2, x1, g_ffn, dx2, "rms2_bwd")
    dmerged = _matmul(dx1b, wo, mode="nt", out_dtype=BF16, name="wo_bwd_x", after=after_ffn, **big)
    after_ffn = reducer.middle("ffn", (dmerged,))
    g_wo = _matmul(merged, dx1b, mode="tn", out_dtype=BF16, name="wo_bwd_w", tm=512, tn=1024, tk=2048, after=after_ffn)
    dproj, dco, dao = _merge_bwd(dmerged, proj, conv_out, attn_out, "merge_bwd")
    dconv_y = _matmul(dco, wco, mode="nt", out_dtype=BF16, name="conv_out_bwd_x", **big)
    g_wco = _matmul(conv_y, dco, mode="tn", out_dtype=BF16, name="conv_out_bwd_w", tm=512, tn=1024, tk=2048)
    dattn = _matmul(dao, wao, mode="nt", out_dtype=BF16, name="attn_out_bwd_x", **big)
    g_wao = _matmul(attn, dao, mode="tn", out_dtype=BF16, name="attn_out_bwd_w", tm=512, tn=1024, tk=2048)
    after_mix = reducer.start("mix", dict(wco=g_wco, wao=g_wao, wo=g_wo))
    dproj, dconv_w = _conv_bwd(dconv_y, proj, conv_w, dproj, "conv_bwd", after=after_mix)
    after_mix = reducer.middle("mix", (dconv_w,))
    dproj, dkr, dv, dsinks = _attn_bwd(dattn, proj, tab, sinks, dproj, "attn_bwd", after=after_mix)
    dproj = _kv_bwd(dkr, dv, tab, dproj, "kv_bwd")
    g_win_t = _matmul(dproj, h1, mode="tn", out_dtype=BF16, name="proj_bwd_w", tm=512, tn=1024, tk=2048)
    after_in = reducer.middle("in", reducer.start("in", dict(win_t=g_win_t)))
    dh1 = _matmul(dproj, win_t, mode="nn", out_dtype=F32, name="proj_bwd_x", tm=1024, tn=1024, tk=1664, after=after_in)
    dx, _, dg_mix = _rms_bwd(dh1, x, g_mix, dx1, "rms1_bwd")
    grads = dict(win_t=g_win_t, wgu_t=g_wgu_t, wd=g_wd, wco=g_wco, wao=g_wao, wo=g_wo)
    small = dict(g_mix=dg_mix, g_ffn=dg_ffn, g_final=dg_final, conv_w=dconv_w, sinks=dsinks, lossvec=lossvec)
    return dx, grads, small


def _position():
    return lax.axis_index("x"), lax.axis_index("y"), lax.axis_index("c")


def _other_chips(x, y):
    return [(1 - x, y), (x, 1 - y), (1 - x, 1 - y)]


SEM_SPEC = pl.BlockSpec(memory_space=pltpu.SEMAPHORE)
EFFECT = pltpu.SideEffectType.DATAFLOW_SIDE_EFFECTING
TOKEN = jax.ShapeDtypeStruct((8, 128), F32)
TOKEN_SPEC = pl.BlockSpec(memory_space=pltpu.VMEM)


def _hbm(a):
    return pltpu.with_memory_space_constraint(a, pltpu.HBM)


def _place(w, me_idx, dtype, name):
    r, cdim = w.shape

    def body(i_ref, w_ref, o_ref):
        del i_ref
        o_ref[...] = w_ref[...].astype(dtype)

    grid_spec = pltpu.PrefetchScalarGridSpec(
        num_scalar_prefetch=1, grid=(1,), in_specs=[pl.BlockSpec((r, cdim), lambda i, me: (0, 0))],
        out_specs=pl.BlockSpec((r, cdim), lambda i, me: (me[0], 0)))
    return _call(body, name=name, grid_spec=grid_spec, out_shape=_sds((N_DEV * r, cdim), dtype),
                 compiler_params=_params("arbitrary"))(me_idx, w)


def _own_rows(ref, r, px, py, pc):
    return ref.at[pl.ds((4 * px + 2 * py + pc) * r, r), :]


def _gather_start(bufs, groups, name):
    n = len(bufs)
    rows = [b.shape[0] // N_DEV for b in bufs]
    ng = len(groups)

    def body(*refs):
        ins = refs[:n]
        sems = refs[n:n + 2 * ng]
        token = refs[-1]
        x, y, c = _position()
        targets = [(x, y, 1 - c)] + [(*chip, c) for chip in _other_chips(x, y)]
        for g, members in enumerate(groups):
            for slot, a in enumerate(members):
                own = _own_rows(ins[a], rows[a], x, y, c)
                for to in targets:
                    pltpu.make_async_remote_copy(src_ref=own, dst_ref=own, send_sem=sems[2 * g].at[slot],
                                                 recv_sem=sems[2 * g + 1].at[slot], device_id=to, device_id_type=MESH).start()
        token[...] = jnp.zeros_like(token)

    sem_shapes = []
    for members in groups:
        sem_shapes += [pltpu.SemaphoreType.DMA((len(members),))] * 2
    outs = _call(
        body, name=name, in_specs=[HBM_SPEC] * n, out_specs=[SEM_SPEC] * (2 * ng) + [HBM_SPEC] * n + [TOKEN_SPEC],
        out_shape=sem_shapes + [pltpu.HBM(b.shape, b.dtype) for b in bufs] + [TOKEN],
        input_output_aliases={i: 2 * ng + i for i in range(n)},
        compiler_params=pltpu.CompilerParams(has_side_effects=EFFECT),
    )(*[_hbm(b) for b in bufs])
    sem_pairs = [(outs[2 * g], outs[2 * g + 1]) for g in range(ng)]
    return sem_pairs, list(outs[2 * ng:2 * ng + n]), outs[-1]


def _gather_forward(send_sems, recv_sems, bufs, after, name):
    n = len(bufs)
    rows = [b.shape[0] // N_DEV for b in bufs]

    def body(*refs):
        ins = refs[:n]
        send1, recv1 = refs[n], refs[n + 1]
        out0 = n + 2 + len(after)
        send2, recv2 = refs[out0], refs[out0 + 1]
        token = refs[-1]
        x, y, c = _position()
        for a in range(n):
            step1 = pltpu.make_async_remote_copy(
                src_ref=_whole(ins[a], 4 * rows[a]), dst_ref=_whole(ins[a], 4 * rows[a]), send_sem=send1.at[a],
                recv_sem=recv1.at[a], device_id=(x, y, c), device_id_type=MESH)
            step1.wait_send()
            step1.wait_recv()
        for a in range(n):
            for chip in _other_chips(x, y):
                blk = _own_rows(ins[a], rows[a], *chip, c)
                pltpu.make_async_remote_copy(src_ref=blk, dst_ref=blk, send_sem=send2.at[a], recv_sem=recv2.at[a],
                                             device_id=(x, y, 1 - c), device_id_type=MESH).start()
        token[...] = jnp.zeros_like(token)

    outs = _call(
        body, name=name, in_specs=[HBM_SPEC] * n + [SEM_SPEC, SEM_SPEC] + [HBM_SPEC] * len(after),
        out_specs=[SEM_SPEC, SEM_SPEC] + [HBM_SPEC] * n + [TOKEN_SPEC],
        out_shape=[pltpu.SemaphoreType.DMA((n,)), pltpu.SemaphoreType.DMA((n,))]
        + [pltpu.HBM(b.shape, b.dtype) for b in bufs] + [TOKEN],
        input_output_aliases={i: 2 + i for i in range(n)},
        compiler_params=pltpu.CompilerParams(has_side_effects=EFFECT),
    )(*bufs, send_sems, recv_sems, *after)
    return outs[0], outs[1], list(outs[2:2 + n]), outs[-1]


def _gather_done(send_sems, recv_sems, bufs, after, name):
    n = len(bufs)
    rows = [b.shape[0] // N_DEV for b in bufs]

    def body(*refs):
        ins = refs[:n]
        send2, recv2 = refs[n], refs[n + 1]
        x, y, c = _position()
        for a in range(n):
            step2 = pltpu.make_async_remote_copy(
                src_ref=_whole(ins[a], 3 * rows[a]), dst_ref=_whole(ins[a], 3 * rows[a]), send_sem=send2.at[a],
                recv_sem=recv2.at[a], device_id=(x, y, c), device_id_type=MESH)
            step2.wait_send()
            step2.wait_recv()

    outs = _call(
        body, name=name, in_specs=[HBM_SPEC] * n + [SEM_SPEC, SEM_SPEC] + [HBM_SPEC] * len(after),
        out_specs=[HBM_SPEC] * n, out_shape=[pltpu.HBM(b.shape, b.dtype) for b in bufs],
        input_output_aliases={i: i for i in range(n)},
        compiler_params=pltpu.CompilerParams(has_side_effects=EFFECT),
    )(*bufs, send_sems, recv_sems, *after)
    return list(outs)


def _whole(ref, nrows):
    return ref.at[pl.ds(0, nrows), :]


def _to_sibling(x, y, c):
    return [(2 * q + (1 - c), q, (x, y, 1 - c)) for q in range(4)]


def _to_chips(x, y, c):
    return [(2 * px + py, j, (px, py, c)) for j, (px, py) in enumerate(_other_chips(x, y))]


def _exchange_start(srcs, src_slots, plan, name):
    n = len(srcs)
    rows = [a.shape[0] // src_slots for a in srcs]
    n_copies = len(plan(0, 0, 0))
    lands = [lax.empty((n_copies * r, a.shape[1]), a.dtype) for a, r in zip(srcs, rows)]

    def body(*refs):
        ins, land_refs = refs[:n], refs[n:2 * n]
        send_sems, recv_sems = refs[2 * n], refs[2 * n + 1]
        token = refs[-1]
        for a in range(n):
            r = rows[a]
            for src_slot, dst_slot, target in plan(*_position()):
                pltpu.make_async_remote_copy(
                    src_ref=ins[a].at[pl.ds(src_slot * r, r), :], dst_ref=land_refs[a].at[pl.ds(dst_slot * r, r), :],
                    send_sem=send_sems.at[a], recv_sem=recv_sems.at[a], device_id=target, device_id_type=MESH).start()
        token[...] = jnp.zeros_like(token)

    outs = _call(
        body, name=name, in_specs=[HBM_SPEC] * (2 * n),
        out_specs=[SEM_SPEC, SEM_SPEC] + [HBM_SPEC] * (2 * n) + [TOKEN_SPEC],
        out_shape=[pltpu.SemaphoreType.DMA((n,)), pltpu.SemaphoreType.DMA((n,))]
        + [pltpu.HBM(a.shape, a.dtype) for a in srcs] + [pltpu.HBM(l.shape, l.dtype) for l in lands] + [TOKEN],
        input_output_aliases={i: 2 + i for i in range(2 * n)},
        compiler_params=pltpu.CompilerParams(has_side_effects=EFFECT),
    )(*[_hbm(a) for a in srcs], *[_hbm(l) for l in lands])
    return outs[0], outs[1], list(outs[2:2 + n]), list(outs[2 + n:2 + 2 * n]), outs[-1]


def _exchange_wait(send_sems, recv_sems, srcs, lands, after, name):
    n = len(srcs)

    def body(*refs):
        ins, land_refs = refs[:n], refs[n:2 * n]
        send_sems_ref, recv_sems_ref = refs[2 * n], refs[2 * n + 1]
        for a in range(n):
            allrows = lands[a].shape[0]
            cp = pltpu.make_async_remote_copy(
                src_ref=_whole(ins[a], allrows), dst_ref=_whole(land_refs[a], allrows), send_sem=send_sems_ref.at[a],
                recv_sem=recv_sems_ref.at[a], device_id=_position(), device_id_type=MESH)
            cp.wait_send()
            cp.wait_recv()

    outs = _call(
        body, name=name, in_specs=[HBM_SPEC] * (2 * n) + [SEM_SPEC, SEM_SPEC] + [HBM_SPEC] * len(after),
        out_specs=[HBM_SPEC] * (2 * n),
        out_shape=[pltpu.HBM(a.shape, a.dtype) for a in srcs] + [pltpu.HBM(l.shape, l.dtype) for l in lands],
        input_output_aliases={i: i for i in range(2 * n)},
        compiler_params=pltpu.CompilerParams(has_side_effects=EFFECT),
    )(*srcs, *lands, send_sems, recv_sems, *after)
    return list(outs[:n]), list(outs[n:])


def _chip_partial(grad, recv, c_idx, name):
    r = recv.shape[0] // 4

    def body(c_ref, g_ref, s_ref, o_ref):
        del c_ref
        o_ref[...] = (g_ref[...].astype(F32) + s_ref[...].astype(F32)).astype(BF16)

    grid_spec = pltpu.PrefetchScalarGridSpec(
        num_scalar_prefetch=1, grid=(4,),
        in_specs=[pl.BlockSpec((r, D), lambda q, c_ref: (2 * q + c_ref[0], 0)), pl.BlockSpec((r, D), lambda q, c_ref: (q, 0))],
        out_specs=pl.BlockSpec((r, D), lambda q, c_ref: (q, 0)))
    return _call(body, name=name, grid_spec=grid_spec, out_shape=_sds((4 * r, D), BF16),
                 compiler_params=_params("parallel"))(c_idx, grad, recv)


def _adamw_math(w, g, m, v):
    m2 = B1 * m + (1.0 - B1) * g
    v2 = B2 * v + (1.0 - B2) * jnp.square(g)
    m_hat = m2 / (1.0 - B1 ** STEP)
    v_hat = v2 / (1.0 - B2 ** STEP)
    return -LR * (m_hat / (jnp.sqrt(v_hat) + EPS_ADAM) + WD * w), m2, v2


def _reduce_adamw(w, part, recv, q_idx, m, v, name):
    r = w.shape[0]
    assert part.shape == (4 * r, D) and recv.shape == (3 * r, D) and w.shape == (r, D)
    tr = r // 2
    nb = r // tr

    def body(q_ref, w_ref, p_ref, r0_ref, r1_ref, r2_ref, m_ref, v_ref, g_ref, d_ref, nm_ref, nv_ref):
        del q_ref
        g = ((p_ref[...].astype(F32) + r0_ref[...].astype(F32)) + r1_ref[...].astype(F32)) + r2_ref[...].astype(F32)
        g_ref[...] = g
        d_ref[...], nm_ref[...], nv_ref[...] = _adamw_math(w_ref[...], g, m_ref[...], v_ref[...])

    own = pl.BlockSpec((tr, D), lambda i, q_ref: (i, 0))
    grid_spec = pltpu.PrefetchScalarGridSpec(
        num_scalar_prefetch=1, grid=(nb,),
        in_specs=[own, pl.BlockSpec((tr, D), lambda i, q_ref: (q_ref[0] * nb + i, 0))]
        + [pl.BlockSpec((tr, D), lambda i, q_ref, j=j: (j * nb + i, 0)) for j in range(3)] + [own, own],
        out_specs=[own] * 4)
    return _call(body, name=name, grid_spec=grid_spec, out_shape=[_sds((r, D), F32)] * 4,
                 compiler_params=_params("parallel"))(q_idx, w, part, recv, recv, recv, m, v)


SMALL_ROWS = 8


def _small_all_reduce(pack, name, after=()):
    def body(p_ref, *rest):
        tot_ref, loss_ref, gath, send_sems, recv_sems = rest[len(after):]
        x, y, c = _position()
        me_id = 4 * x + 2 * y + c
        gath[me_id] = p_ref[...]
        copies = []
        for k in range(1, N_DEV):
            peer = tuple(1 - v if (k >> b) & 1 else v for v, b in ((x, 2), (y, 1), (c, 0)))
            cp = pltpu.make_async_remote_copy(src_ref=p_ref, dst_ref=gath.at[me_id], send_sem=send_sems.at[k - 1],
                                              recv_sem=recv_sems.at[k - 1], device_id=peer, device_id_type=MESH)
            cp.start()
            copies.append(cp)
        for cp in copies:
            cp.wait_recv()
        for cp in copies:
            cp.wait_send()
        tot = gath[0]
        for d in range(1, N_DEV):
            tot = tot + gath[d]
        tot_ref[...] = tot
        loss_ref[...] = jnp.full((1, 128), (0.5 / D) * jnp.sum(tot[SMALL_ROWS - 1:SMALL_ROWS, :]), F32)

    vm = pl.BlockSpec(memory_space=pltpu.VMEM)
    return _call(
        body, name=name, in_specs=[vm] + [HBM_SPEC] * len(after), out_specs=[vm, vm],
        out_shape=[_sds((SMALL_ROWS, D), F32), _sds((1, 128), F32)],
        scratch_shapes=[pltpu.VMEM((N_DEV, SMALL_ROWS, D), F32), pltpu.SemaphoreType.DMA((N_DEV - 1,)),
                        pltpu.SemaphoreType.DMA((N_DEV - 1,))],
    )(pack, *after)


def _adamw(w, g, m, v, name):
    r, cdim = w.shape
    tr = 256 if r % 256 == 0 else (r // 2 if r % 16 == 0 else r)

    def body(w_ref, g_ref, m_ref, v_ref, d_ref, nm_ref, nv_ref):
        d_ref[...], nm_ref[...], nv_ref[...] = _adamw_math(w_ref[...], g_ref[...], m_ref[...], v_ref[...])

    spec = pl.BlockSpec((tr, cdim), lambda i: (i, 0))
    return _call(
        body, name=name, grid=(r // tr,), in_specs=[spec] * 4, out_specs=[spec] * 3,
        out_shape=[_sds((r, cdim), F32)] * 3, compiler_params=_params("parallel"),
    )(w, g, m, v)


def kernel(x, g_mix, w_in, conv_w, attn_sinks, w_conv_out, w_attn_out, w_o, g_ffn, w_gate_up, w_down, g_final, loss_target, m_g_mix, m_w_in, m_conv_w, m_attn_sinks, m_w_conv_out, m_w_attn_out, m_w_o, m_g_ffn, m_w_gate_up, m_w_down, m_g_final, v_g_mix, v_w_in, v_conv_w, v_attn_sinks, v_w_conv_out, v_w_attn_out, v_w_o, v_g_ffn, v_w_gate_up, v_w_down, v_g_final):
    cx, cy, cc = _position()
    c_idx = jnp.reshape(cc, (1,)).astype(jnp.int32)
    q_idx = jnp.reshape(2 * cx + cy, (1,)).astype(jnp.int32)
    me = 4 * cx + 2 * cy + cc

    me_idx = jnp.reshape(me, (1,)).astype(jnp.int32)
    bufs = [
        _place(jnp.transpose(w_in[0]), me_idx, BF16, "place_w_in"), _place(jnp.pad(conv_w[0], ((0, 5), (0, 0))), me_idx, F32, "place_conv_w"),
        _place(w_conv_out[0], me_idx, BF16, "place_w_conv_out"), _place(w_attn_out[0], me_idx, BF16, "place_w_attn_out"),
        _place(w_o[0], me_idx, BF16, "place_w_o"),
        _place(jnp.transpose(w_gate_up[0]), me_idx, BF16, "place_w_gate_up"), _place(w_down[0], me_idx, BF16, "place_w_down"),
    ]
    members = {"in": [0, 1], "mix": [2, 3, 4], "ffn": [5, 6]}
    sem_pairs, bufs, gather_token = _gather_start(bufs, list(members.values()), "gather_start")

    class Gathered:
        def __init__(self):
            self.state = {g: (sem_pairs[i], [bufs[a] for a in members[g]]) for i, g in enumerate(members)}

        def begin(self, group, after):
            (send_sems, recv_sems), group_bufs = self.state[group]
            send2, recv2, group_bufs, token = _gather_forward(send_sems, recv_sems, group_bufs, after, "gather_forward_" + group)
            self.state[group] = ((send2, recv2), group_bufs)
            return (token,)

        def end(self, group, after):
            (send2, recv2), group_bufs = self.state[group]
            full = _gather_done(send2, recv2, group_bufs, after, "gather_done_" + group)
            if group == "in":
                return full[0], jnp.transpose(full[1].reshape(N_DEV, 8, 128)[:, :3, :], (1, 0, 2)).reshape(3, D)
            return full

    in_flight = {}

    class Reducer:
        def start(self, group, gdict):
            keys, glist = list(gdict), list(gdict.values())
            send_sems, recv_sems, glist, lands, token = _exchange_start(glist, N_DEV, _to_sibling, "rs_sibling_start_" + group)
            in_flight[group] = (keys, send_sems, recv_sems, glist, lands)
            return (token,)

        def middle(self, group, after):
            keys, send_sems, recv_sems, glist, lands = in_flight[group]
            glist, lands = _exchange_wait(send_sems, recv_sems, glist, lands, after, "rs_sibling_wait_" + group)
            parts = [_chip_partial(g, r, c_idx, "chip_partial_" + k) for k, g, r in zip(keys, glist, lands)]
            send_sems, recv_sems, parts, lands, token = _exchange_start(parts, 4, _to_chips, "rs_chips_start_" + group)
            in_flight[group] = (keys, send_sems, recv_sems, parts, lands)
            return (token,)

    dx, _, small = _local_step(x[0], loss_target[0], g_mix, g_ffn, g_final[None], attn_sinks, Gathered(),
                               reducer=Reducer(), after=(gather_token,))

    transposed = ("w_in", "w_gate_up")

    def as2d(k, a):
        if k in transposed:
            return jnp.transpose(a[0])
        return a[None] if a.ndim == 1 else (a[0] if a.ndim == 3 else a)

    w_all = {"g_mix": g_mix, "w_in": w_in, "conv_w": conv_w, "attn_sinks": attn_sinks, "w_conv_out": w_conv_out,
             "w_attn_out": w_attn_out, "w_o": w_o, "g_ffn": g_ffn, "w_gate_up": w_gate_up, "w_down": w_down, "g_final": g_final}
    m_all = {"g_mix": m_g_mix, "w_in": m_w_in, "conv_w": m_conv_w, "attn_sinks": m_attn_sinks, "w_conv_out": m_w_conv_out,
             "w_attn_out": m_w_attn_out, "w_o": m_w_o, "g_ffn": m_g_ffn, "w_gate_up": m_w_gate_up, "w_down": m_w_down,
             "g_final": m_g_final}
    v_all = {"g_mix": v_g_mix, "w_in": v_w_in, "conv_w": v_conv_w, "attn_sinks": v_attn_sinks, "w_conv_out": v_w_conv_out,
             "w_attn_out": v_w_attn_out, "w_o": v_w_o, "g_ffn": v_g_ffn, "w_gate_up": v_w_gate_up, "w_down": v_w_down,
             "g_final": v_g_final}
    results = {}

    def update(k, g=None, part=None, recv=None):
        w2, m2, v2 = as2d(k, w_all[k]), as2d(k, m_all[k]), as2d(k, v_all[k])
        if g is None:
            g, d, nm, nv = _reduce_adamw(w2, part, recv, q_idx, m2, v2, "adamw_" + k)
        else:
            d, nm, nv = _adamw(w2, g, m2, v2, "adamw_" + k)
        results[k] = [(jnp.transpose(val) if k in transposed else val).reshape(w_all[k].shape) for val in (g, d, nm, nv)]
        return nm

    kernel_name = {"win_t": "w_in", "wgu_t": "w_gate_up", "wd": "w_down", "wco": "w_conv_out", "wao": "w_attn_out", "wo": "w_o"}

    def finish(group, after):
        keys, send_sems, recv_sems, parts, lands = in_flight[group]
        parts, lands = _exchange_wait(send_sems, recv_sems, parts, lands, after, "rs_chips_wait_" + group)
        return tuple(update(kernel_name[k], part=p, recv=r) for k, p, r in zip(keys, parts, lands))

    after = finish("mix", finish("ffn", (dx,)))

    sinks_row = jnp.pad(small["sinks"], ((0, 0), (0, D - 128)))
    pack = jnp.concatenate([small["g_mix"], small["g_ffn"], small["g_final"], small["conv_w"], sinks_row, small["lossvec"]], axis=0)
    tot, loss_row = _small_all_reduce(pack, "small_all_reduce", after=after)
    loss = loss_row[0, 0]
    g_small = {
        "g_mix": tot[0:1], "g_ffn": tot[1:2], "g_final": tot[2:3],
        "conv_w": lax.dynamic_slice(tot, (3, me * 128), (3, 128)), "attn_sinks": tot[6:7, :N_HEADS],
    }
    finish("in", tuple(update(k, g) for k, g in g_small.items()))

    order = ["g_mix", "w_in", "conv_w", "attn_sinks", "w_conv_out", "w_attn_out", "w_o", "g_ffn", "w_gate_up", "w_down", "g_final"]
    return (loss, dx[None], *[results[k][i] for i in range(4) for k in order])
```

```python
import functools
import math

import jax
import jax.numpy as jnp
from jax import lax
from jax.experimental import pallas as pl
from jax.experimental.pallas import tpu as pltpu

F32 = jnp.float32
BF16 = jnp.bfloat16

D = 1024
HEAD_DIM = 64
N_HEADS = 16
N_KV = 4
GROUP = N_HEADS // N_KV
D_KV = N_KV * HEAD_DIM
BLOCK = 128
ROT_DIM = HEAD_DIM // 4
ROPE_THETA = 500000.0
ATTN_SCALE = 1.0 / math.sqrt(HEAD_DIM)
NEG_INF = -1e30
D_FF = 2816
N_IN = 6656
EPS = 1e-5
C_CB, C_CC, C_CX, C_Q, C_K, C_V, C_GC, C_GA = 0, 1024, 2048, 3072, 4096, 4352, 4608, 5632

LR, B1, B2, EPS_ADAM, WD, STEP = 0.001, 0.9, 0.999, 1e-08, 0.01, 10

N_DEV = 8
MESH = pl.DeviceIdType.MESH
VMEM_LIMIT = 56 * 1024 * 1024

NN = (((1,), (0,)), ((), ()))
NT = (((1,), (1,)), ((), ()))
TN = (((0,), (0,)), ((), ()))
HBM_SPEC = pl.BlockSpec(memory_space=pl.ANY)


def _call(body, **kw):
    return pl.pallas_call(body, **kw)


def _params(*sem):
    return pltpu.CompilerParams(dimension_semantics=sem, vmem_limit_bytes=VMEM_LIMIT)


def _sds(shape, dtype):
    return jax.ShapeDtypeStruct(shape, dtype)


def _matmul(a, b, *, mode, tm, tn, tk, out_dtype, name, res=None, after=()):
    parts = list(a) if isinstance(a, (list, tuple)) else [a]
    rows_a = parts[0].shape[0]
    cols_a = sum(p.shape[1] for p in parts)
    if mode == "nn":
        (m, kk), (_, n), dims = (rows_a, cols_a), b.shape, NN
    elif mode == "nt":
        (m, kk), (n, _), dims = (rows_a, cols_a), b.shape, NT
    else:
        (kk, m), (_, n), dims = (rows_a, cols_a), b.shape, TN
    tm, tn, tk = min(tm, m), min(tn, n), min(tk, kk)
    assert m % tm == 0 and n % tn == 0 and kk % tk == 0, (name, m, n, kk, tm, tn, tk)
    nk = kk // tk
    split_axis, width = (2, tk) if mode == "nn" else (0, tm)
    assert len(parts) == 1 or mode in ("nn", "tn")
    assert len(parts) == 1 or all(p.shape[1] % width == 0 for p in parts), (name, width)
    counts = [p.shape[1] // width for p in parts]
    starts = [sum(counts[:p]) for p in range(len(parts))]

    def a_spec(p):
        def col(t):
            return jnp.clip(t - starts[p], 0, counts[p] - 1) if len(parts) > 1 else t

        if mode == "tn":
            return pl.BlockSpec((tk, tm), lambda i, j, k: (k, col(i)))
        return pl.BlockSpec((tm, tk), lambda i, j, k: (i, col(k)))

    if mode == "nt":
        b_spec = pl.BlockSpec((tn, tk), lambda i, j, k: (j, k))
    else:
        b_spec = pl.BlockSpec((tk, tn), lambda i, j, k: (k, j))
    o_spec = pl.BlockSpec((tm, tn), lambda i, j, k: (i, j))
    has_res = res is not None
    n_parts = len(parts)

    def body(*refs):
        a_refs, b_ref = refs[:n_parts], refs[n_parts]
        r_ref = refs[n_parts + 1] if has_res else None
        o_ref = refs[n_parts + 1 + has_res + len(after)]
        k = pl.program_id(2)

        def step(a_ref):
            part = lax.dot_general(a_ref[...], b_ref[...], dims, preferred_element_type=F32)

            def finish(acc):
                if has_res:
                    acc = acc + r_ref[...]
                o_ref[...] = acc.astype(o_ref.dtype)

            if nk == 1:
                finish(part)
            else:
                acc_ref = refs[-1]

                @pl.when(k == 0)
                def _():
                    acc_ref[...] = part

                @pl.when(k > 0)
                def _():
                    acc_ref[...] += part

                @pl.when(k == nk - 1)
                def _():
                    finish(acc_ref[...])

        if n_parts == 1:
            step(a_refs[0])
        else:
            t = pl.program_id(split_axis)
            for p in range(n_parts):
                pl.when((t >= starts[p]) & (t < starts[p] + counts[p]))(functools.partial(step, a_refs[p]))

    ins = parts + [b] + ([res] if has_res else []) + list(after)
    in_specs = [a_spec(p) for p in range(n_parts)] + [b_spec] + ([o_spec] if has_res else []) + [HBM_SPEC] * len(after)
    scratch = [] if nk == 1 else [pltpu.VMEM((tm, tn), F32)]
    return _call(
        body, name=name, grid=(m // tm, n // tn, nk), in_specs=in_specs, out_specs=o_spec,
        out_shape=_sds((m, n), out_dtype), scratch_shapes=scratch,
        compiler_params=_params("parallel", "parallel", "arbitrary"),
    )(*ins)


def _row_tile(s):
    return min(512, s)


def _rms_fwd(x, g, name, after=()):
    s = x.shape[0]
    tm = _row_tile(s)

    def body(x_ref, g_ref, *rest):
        h_ref = rest[-1]
        xv = x_ref[...]
        r = lax.rsqrt(jnp.mean(xv * xv, axis=-1, keepdims=True) + EPS)
        h_ref[...] = (xv * r * g_ref[...]).astype(BF16)

    row = pl.BlockSpec((tm, D), lambda i: (i, 0))
    return _call(
        body, name=name, grid=(s // tm,), in_specs=[row, pl.BlockSpec((1, D), lambda i: (0, 0))] + [HBM_SPEC] * len(after),
        out_specs=row, out_shape=_sds((s, D), BF16), compiler_params=_params("parallel"),
    )(x, g, *after)


def _rms_bwd(dh, x, g, dres, name, after=()):
    s = x.shape[0]
    tm = _row_tile(s)

    def body(dh_ref, x_ref, g_ref, dres_ref, *rest):
        dx_ref, dxb_ref, dg_ref = rest[len(after):]
        xv = x_ref[...]
        r = lax.rsqrt(jnp.mean(xv * xv, axis=-1, keepdims=True) + EPS)
        xh = xv * r
        dhv = dh_ref[...]
        dyg = dhv * g_ref[...]
        dx = dres_ref[...] + r * (dyg - xh * jnp.mean(dyg * xh, axis=-1, keepdims=True))
        dx_ref[...] = dx
        dxb_ref[...] = dx.astype(BF16)
        part = jnp.sum(dhv * xh, axis=0, keepdims=True)

        @pl.when(pl.program_id(0) == 0)
        def _():
            dg_ref[...] = part

        @pl.when(pl.program_id(0) > 0)
        def _():
            dg_ref[...] += part

    row = pl.BlockSpec((tm, D), lambda i: (i, 0))
    vec = pl.BlockSpec((1, D), lambda i: (0, 0))
    return _call(
        body, name=name, grid=(s // tm,), in_specs=[row, row, vec, row] + [HBM_SPEC] * len(after), out_specs=[row, row, vec],
        out_shape=[_sds((s, D), F32), _sds((s, D), BF16), _sds((1, D), F32)],
        compiler_params=_params("arbitrary"),
    )(dh, x, g, dres, *after)


def _loss_head(x2, g, tgt, name):
    s = x2.shape[0]
    tm = _row_tile(s)

    def body(x_ref, g_ref, t_ref, dx_ref, dxb_ref, dg_ref, l_ref):
        xv = x_ref[...]
        gv = g_ref[...]
        r = lax.rsqrt(jnp.mean(xv * xv, axis=-1, keepdims=True) + EPS)
        xh = xv * r
        err = xh * gv - t_ref[...]
        dy = err * (1.0 / D)
        dyg = dy * gv
        dx = r * (dyg - xh * jnp.mean(dyg * xh, axis=-1, keepdims=True))
        dx_ref[...] = dx
        dxb_ref[...] = dx.astype(BF16)
        dg_part = jnp.sum(dy * xh, axis=0, keepdims=True)
        l_part = jnp.sum(err * err, axis=0, keepdims=True)

        @pl.when(pl.program_id(0) == 0)
        def _():
            dg_ref[...] = dg_part
            l_ref[...] = l_part

        @pl.when(pl.program_id(0) > 0)
        def _():
            dg_ref[...] += dg_part
            l_ref[...] += l_part

    row = pl.BlockSpec((tm, D), lambda i: (i, 0))
    vec = pl.BlockSpec((1, D), lambda i: (0, 0))
    return _call(
        body, name=name, grid=(s // tm,), in_specs=[row, vec, row], out_specs=[row, row, vec, vec],
        out_shape=[_sds((s, D), F32), _sds((s, D), BF16), _sds((1, D), F32), _sds((1, D), F32)],
        compiler_params=_params("arbitrary"),
    )(x2, g, tgt)


CONV_TC = 256


def _shift_down(u, k, rows):
    return jnp.where(rows >= k, pltpu.roll(u, k, 0), 0.0)


def _shift_up(u, k, rows, s):
    return jnp.where(rows < s - k, pltpu.roll(u, s - k, 0), 0.0)


def _conv_specs(s):
    nb = D // CONV_TC

    def col(c0):
        return pl.BlockSpec((s, CONV_TC), lambda j, c0=c0: (0, c0 // CONV_TC + j))

    return nb, col


def _conv_fwd(proj, conv_w, name):
    s = proj.shape[0]
    nb, col = _conv_specs(s)

    def body(cb_ref, cc_ref, cx_ref, w_ref, y_ref):
        rows = lax.broadcasted_iota(jnp.int32, (s, CONV_TC), 0)
        u = cc_ref[...].astype(F32) * cx_ref[...].astype(F32)
        w = w_ref[...]
        c = w[0:1] * _shift_down(u, 2, rows) + w[1:2] * _shift_down(u, 1, rows) + w[2:3] * u
        y_ref[...] = (cb_ref[...].astype(F32) * c).astype(BF16)

    return _call(
        body, name=name, grid=(nb,),
        in_specs=[col(C_CB), col(C_CC), col(C_CX), pl.BlockSpec((3, CONV_TC), lambda j: (0, j))],
        out_specs=pl.BlockSpec((s, CONV_TC), lambda j: (0, j)), out_shape=_sds((s, D), BF16),
        compiler_params=_params("parallel"),
    )(proj, proj, proj, conv_w)


def _write_behind(t, nt, buf, sems, tiles, window, where):
    slot = t % 2

    def copies(sl, at):
        return [pltpu.make_async_copy(buf.at[sl, p], window(p, at), sems.at[sl, p]) for p in range(len(tiles))]

    @pl.when(t >= 2)
    def _():
        for cp in copies(slot, where):
            cp.wait()

    for p, tile in enumerate(tiles):
        buf[slot, p] = tile
    started = copies(slot, where)
    for cp in started:
        cp.start()

    @pl.when(t == nt - 1)
    def _():
        for cp in started:
            cp.wait()
        if nt > 1:
            for cp in copies(1 - slot, where):
                cp.wait()


def _conv_bwd(dy, proj, conv_w, dproj, name, after=()):
    s = proj.shape[0]
    nb, col = _conv_specs(s)

    def body(dy_ref, cb_ref, cc_ref, cx_ref, w_ref, *rest):
        dproj_ref, dw_ref, buf, sems = rest[1 + len(after):]
        j = pl.program_id(0)
        rows = lax.broadcasted_iota(jnp.int32, (s, CONV_TC), 0)
        cc = cc_ref[...].astype(F32)
        cx = cx_ref[...].astype(F32)
        u = cc * cx
        u1 = _shift_down(u, 1, rows)
        u2 = _shift_down(u, 2, rows)
        w = w_ref[...]
        c = w[0:1] * u2 + w[1:2] * u1 + w[2:3] * u
        dyv = dy_ref[...].astype(F32)
        dc = dyv * cb_ref[...].astype(F32)
        du = w[2:3] * dc + w[1:2] * _shift_up(dc, 1, rows, s) + w[0:1] * _shift_up(dc, 2, rows, s)
        def window(p, jj):
            start = pl.multiple_of((C_CB, C_CC, C_CX)[p] + jj * CONV_TC, CONV_TC)
            return dproj_ref.at[:, pl.ds(start, CONV_TC)]

        tiles = ((dyv * c).astype(BF16), (du * cx).astype(BF16), (du * cc).astype(BF16))
        _write_behind(j, nb, buf, sems, tiles, window, j)
        dw_ref[...] = jnp.concatenate(
            [jnp.sum(dc * u2, axis=0, keepdims=True), jnp.sum(dc * u1, axis=0, keepdims=True),
             jnp.sum(dc * u, axis=0, keepdims=True)], axis=0)

    return _call(
        body, name=name, grid=(nb,),
        in_specs=[pl.BlockSpec((s, CONV_TC), lambda j: (0, j)), col(C_CB), col(C_CC), col(C_CX),
                  pl.BlockSpec((3, CONV_TC), lambda j: (0, j))] + [HBM_SPEC] * (1 + len(after)),
        out_specs=[pl.BlockSpec(memory_space=pl.ANY), pl.BlockSpec((3, CONV_TC), lambda j: (0, j))],
        out_shape=[_sds((s, N_IN), BF16), _sds((3, D), F32)],
        scratch_shapes=[pltpu.VMEM((2, 3, s, CONV_TC), BF16), pltpu.SemaphoreType.DMA((2, 3))],
        input_output_aliases={5: 0}, compiler_params=_params("arbitrary"),
    )(dy, proj, proj, proj, conv_w, dproj, *after)


def _rope_tables(s):
    half = ROT_DIM // 2
    inv_freq = ROPE_THETA ** (-jnp.arange(0, ROT_DIM, 2, dtype=F32) / ROT_DIM)
    inv64 = jnp.concatenate([inv_freq, inv_freq, jnp.zeros((HEAD_DIM - ROT_DIM,), F32)])
    ang = jnp.arange(s, dtype=F32)[:, None] * jnp.concatenate([inv64, inv64])[None, :]
    d = lax.broadcasted_iota(jnp.int32, (s, 128), 1) % HEAD_DIM
    cos, sin = jnp.cos(ang), jnp.sin(ang)
    c = jnp.where(d < ROT_DIM, cos, 1.0)
    a = jnp.where(d < half, -sin, 0.0)
    b = jnp.where((d >= half) & (d < ROT_DIM), sin, 0.0)
    return jnp.concatenate([c, a, b], axis=1)


def _rope(x, tab):
    c, a, b = tab[:, 0:128], tab[:, 128:256], tab[:, 256:384]
    outs = []
    for i in range(x.shape[1] // 128):
        xc = x[:, i * 128:(i + 1) * 128]
        outs.append(xc * c + pltpu.roll(xc, 120, 1) * a + pltpu.roll(xc, 8, 1) * b)
    return outs[0] if len(outs) == 1 else jnp.concatenate(outs, axis=1)


def _rope_t(dx, tab):
    c, a, b = tab[:, 0:128], tab[:, 128:256], tab[:, 256:384]
    outs = []
    for i in range(dx.shape[1] // 128):
        dc = dx[:, i * 128:(i + 1) * 128]
        outs.append(dc * c + pltpu.roll(dc * a, 8, 1) + pltpu.roll(dc * b, 120, 1))
    return outs[0] if len(outs) == 1 else jnp.concatenate(outs, axis=1)


def _attn_mask(n):
    qi = lax.broadcasted_iota(jnp.int32, (GROUP * BLOCK, 2 * BLOCK), 0) & (BLOCK - 1)
    kj = lax.broadcasted_iota(jnp.int32, (GROUP * BLOCK, 2 * BLOCK), 1)
    rel = qi + BLOCK - kj
    return (rel >= 0) & (rel < BLOCK) & ((kj >= BLOCK) | (n > 0))


def _sink_col(sink_ref, hk):
    return jnp.concatenate([jnp.full((BLOCK, 1), sink_ref[0, hk * GROUP + g], F32) for g in range(GROUP)], axis=0)


def _attn_in_specs():
    prev = lambda n: jnp.maximum(n - 1, 0)
    return [
        pl.BlockSpec((BLOCK, D), lambda n: (n, C_Q // D)),
        pl.BlockSpec((BLOCK, D_KV), lambda n: (n, C_K // D_KV)),
        pl.BlockSpec((BLOCK, D_KV), lambda n: (prev(n), C_K // D_KV)),
        pl.BlockSpec((BLOCK, D_KV), lambda n: (n, C_V // D_KV)),
        pl.BlockSpec((BLOCK, D_KV), lambda n: (prev(n), C_V // D_KV)),
        pl.BlockSpec((BLOCK, 384), lambda n: (n, 0)),
        pl.BlockSpec((BLOCK, 384), lambda n: (prev(n), 0)),
        pl.BlockSpec(memory_space=pltpu.SMEM),
    ]


def _load_qkv(q_ref, kc_ref, kp_ref, vc_ref, vp_ref, tc_ref, tp_ref):
    q = _rope(q_ref[...].astype(F32), tc_ref[...]).astype(BF16)
    kc = _rope(kc_ref[...].astype(F32), tc_ref[...]).astype(BF16)
    kp = _rope(kp_ref[...].astype(F32), tp_ref[...]).astype(BF16)
    return q, kc, kp, vc_ref[...], vp_ref[...]


def _group_rows(x, hk):
    base = hk * GROUP * HEAD_DIM
    return jnp.concatenate([x[:, base + g * HEAD_DIM: base + (g + 1) * HEAD_DIM] for g in range(GROUP)], axis=0)


def _kv_rows(prev, cur, hk):
    sl = slice(hk * HEAD_DIM, (hk + 1) * HEAD_DIM)
    return jnp.concatenate([prev[:, sl], cur[:, sl]], axis=0)


def _attn_bwd(do, proj, tab, sinks, dproj, name, after=()):
    s = proj.shape[0]
    nblk = s // BLOCK

    def body(do_ref, q_ref, kc_ref, kp_ref, vc_ref, vp_ref, tc_ref, tp_ref, sink_ref, *rest):
        dproj_ref, dk_ref, dv_ref, ds_ref, dqbuf, dqout, dkbuf, dvbuf, sem = rest[1 + len(after):]
        n = pl.program_id(0)

        @pl.when(n == 0)
        def _():
            dk_ref[...] = jnp.zeros_like(dk_ref)
            dv_ref[...] = jnp.zeros_like(dv_ref)
            ds_ref[...] = jnp.zeros_like(ds_ref)

        q, kc, kp, vc, vp = _load_qkv(q_ref, kc_ref, kp_ref, vc_ref, vp_ref, tc_ref, tp_ref)
        dov = do_ref[...]
        mask = _attn_mask(n)
        rows = GROUP * BLOCK
        head_off = lax.broadcasted_iota(jnp.int32, (rows, 128), 1) - (lax.broadcasted_iota(jnp.int32, (rows, 128), 0) >> 7)
        dsink_row = jnp.zeros((1, 128), F32)
        prev0 = pl.multiple_of(jnp.maximum(n - 1, 0) * BLOCK, BLOCK)
        cur0 = pl.multiple_of(n * BLOCK, BLOCK)
        for hk in range(N_KV):
            qg = _group_rows(q, hk)
            dog = _group_rows(dov, hk)
            kcat = _kv_rows(kp, kc, hk)
            vcat = _kv_rows(vp, vc, hk)
            sc = lax.dot_general(qg, kcat, NT, preferred_element_type=F32) * ATTN_SCALE
            sc = jnp.where(mask, sc, NEG_INF)
            sink = _sink_col(sink_ref, hk)
            m = jnp.maximum(jnp.max(sc, axis=1, keepdims=True), sink)
            e = jnp.exp(sc - m)
            es = jnp.exp(sink - m)
            inv = 1.0 / (jnp.sum(e, axis=1, keepdims=True) + es)
            p = e * inv
            pb = p.astype(BF16)
            dp = lax.dot_general(dog, vcat, NT, preferred_element_type=F32)
            delta = jnp.sum(p * dp, axis=1, keepdims=True)
            dsc = (p * (dp - delta) * ATTN_SCALE).astype(BF16)
            dsk = -(es * inv) * delta
            dsink_row = dsink_row + jnp.sum(jnp.where(head_off == hk * GROUP, dsk, 0.0), axis=0, keepdims=True)
            dqg = lax.dot_general(dsc, kcat, NN, preferred_element_type=F32)
            dkcat = lax.dot_general(dsc, qg, TN, preferred_element_type=F32)
            dvcat = lax.dot_general(pb, dog, TN, preferred_element_type=F32)
            base = hk * GROUP * HEAD_DIM
            for g in range(GROUP):
                dqbuf[:, base + g * HEAD_DIM: base + (g + 1) * HEAD_DIM] = dqg[g * BLOCK:(g + 1) * BLOCK]
            sl = slice(hk * HEAD_DIM, (hk + 1) * HEAD_DIM)
            dkbuf[:, sl] = dkcat
            dvbuf[:, sl] = dvcat

        @pl.when(n > 0)
        def _():
            dk_ref[pl.ds(prev0, BLOCK), :] += dkbuf[0:BLOCK, :]
            dv_ref[pl.ds(prev0, BLOCK), :] += dvbuf[0:BLOCK, :]

        dk_ref[pl.ds(cur0, BLOCK), :] += dkbuf[BLOCK:2 * BLOCK, :]
        dv_ref[pl.ds(cur0, BLOCK), :] += dvbuf[BLOCK:2 * BLOCK, :]
        ds_ref[...] += dsink_row
        dqout[...] = _rope_t(dqbuf[...], tc_ref[...]).astype(BF16)
        cp = pltpu.make_async_copy(dqout, dproj_ref.at[pl.ds(cur0, BLOCK), pl.ds(C_Q, D)], sem)
        cp.start()
        cp.wait()

    blk = lambda w: pl.BlockSpec((BLOCK, w), lambda n: (n, 0))
    whole = lambda w: pl.BlockSpec((s, w), lambda n: (0, 0))
    anyspec = pl.BlockSpec(memory_space=pl.ANY)
    n_in = 1 + len(_attn_in_specs())
    return _call(
        body, name=name, grid=(nblk,), in_specs=[blk(D)] + _attn_in_specs() + [anyspec] * (1 + len(after)),
        out_specs=[anyspec, whole(D_KV), whole(D_KV), pl.BlockSpec((1, 128), lambda n: (0, 0))],
        out_shape=[_sds((s, N_IN), BF16), _sds((s, D_KV), F32), _sds((s, D_KV), F32), _sds((1, 128), F32)],
        scratch_shapes=[pltpu.VMEM((BLOCK, D), F32), pltpu.VMEM((BLOCK, D), BF16), pltpu.VMEM((2 * BLOCK, D_KV), F32),
                        pltpu.VMEM((2 * BLOCK, D_KV), F32), pltpu.SemaphoreType.DMA(())],
        input_output_aliases={n_in: 0}, compiler_params=_params("arbitrary"),
    )(do, proj, proj, proj, proj, proj, tab, tab, sinks, dproj, *after)


HALF = HEAD_DIM
N_CHUNK = D // 128


def _swa_bias(n):
    qi = lax.broadcasted_iota(jnp.int32, (BLOCK, 2 * BLOCK), 0)
    kj = lax.broadcasted_iota(jnp.int32, (BLOCK, 2 * BLOCK), 1)
    rel = qi + BLOCK - kj
    valid = (rel >= 0) & (rel < BLOCK) & ((kj >= BLOCK) | (n > 0))
    return jnp.where(valid, 0.0, NEG_INF)


def _halves(x):
    lo = lax.broadcasted_iota(jnp.int32, x.shape, 1) < HALF
    return jnp.where(lo, x, 0.0).astype(BF16), jnp.where(lo, 0.0, x).astype(BF16)


def _dup_heads(x):
    out = []
    for pair in range(N_KV // 2):
        xc = x[:, pair * 128:(pair + 1) * 128]
        xr = pltpu.roll(xc, HALF, 1)
        lo = lax.broadcasted_iota(jnp.int32, xc.shape, 1) < HALF
        out += [jnp.where(lo, xc, xr), jnp.where(lo, xr, xc)]
    return out


def _swa_load(q_ref, kc_ref, kp_ref, vc_ref, vp_ref, tc_ref, tp_ref):
    qf = _rope(q_ref[...].astype(F32), tc_ref[...]) * ATTN_SCALE
    q_halves = [_halves(qf[:, c * 128:(c + 1) * 128]) for c in range(N_CHUNK)]
    kf = jnp.concatenate([_rope(kp_ref[...].astype(F32), tp_ref[...]), _rope(kc_ref[...].astype(F32), tc_ref[...])], axis=0)
    vf = jnp.concatenate([vp_ref[...], vc_ref[...]], axis=0).astype(F32)
    return q_halves, _dup_heads(kf), _dup_heads(vf)


def _swa_probs(qh, kk, bias, sink):
    s = lax.dot_general(qh, kk, NT, preferred_element_type=F32) + bias
    m = jnp.maximum(jnp.max(jnp.maximum(s[:, :BLOCK], s[:, BLOCK:]), axis=1, keepdims=True), sink)
    return jnp.exp(s - m), m


def _swa_fwd(proj, tab, sinks, name, after=()):
    s = proj.shape[0]

    def body(q_ref, kc_ref, kp_ref, vc_ref, vp_ref, tc_ref, tp_ref, sink_ref, *rest):
        o_ref = rest[-1]
        n = pl.program_id(0)
        q_halves, kdup, vdup = _swa_load(q_ref, kc_ref, kp_ref, vc_ref, vp_ref, tc_ref, tp_ref)
        bias = _swa_bias(n)
        ones = jnp.ones((2 * BLOCK, 128), BF16)
        for c in range(N_CHUNK):
            hk = c // (GROUP // 2)
            kk = kdup[hk].astype(BF16)
            acc = None
            for half, v_half in enumerate(_halves(vdup[hk])):
                sink = sink_ref[0, 2 * c + half]
                e, m = _swa_probs(q_halves[c][half], kk, bias, sink)
                o = lax.dot_general(e.astype(BF16), jnp.concatenate([v_half, ones], axis=1), NN, preferred_element_type=F32)
                part = o[:, :128] * (1.0 / (o[:, 128:] + jnp.exp(sink - m)))
                acc = part if acc is None else acc + part
            o_ref[:, c * 128:(c + 1) * 128] = acc.astype(BF16)

    return _call(
        body, name=name, grid=(s // BLOCK,), in_specs=_attn_in_specs() + [HBM_SPEC] * len(after),
        out_specs=pl.BlockSpec((BLOCK, D), lambda n: (n, 0)), out_shape=_sds((s, D), BF16),
        compiler_params=_params("parallel"),
    )(proj, proj, proj, proj, proj, tab, tab, sinks, *after)


def _kv_bwd(dkr, dv, tab, dproj, name):
    s = dkr.shape[0]
    tm = _row_tile(s)

    def body(dk_ref, dv_ref, t_ref, dproj_in, o_ref):
        del dproj_in
        o_ref[:, 0:D_KV] = _rope_t(dk_ref[...], t_ref[...]).astype(BF16)
        o_ref[:, D_KV:2 * D_KV] = dv_ref[...].astype(BF16)

    row = lambda w: pl.BlockSpec((tm, w), lambda i: (i, 0))
    return _call(
        body, name=name, grid=(s // tm,),
        in_specs=[row(D_KV), row(D_KV), row(384), pl.BlockSpec(memory_space=pl.ANY)],
        out_specs=pl.BlockSpec((tm, 2 * D_KV), lambda i: (i, C_K // (2 * D_KV))),
        out_shape=_sds((s, N_IN), BF16), input_output_aliases={3: 0}, compiler_params=_params("parallel"),
    )(dkr, dv, tab, dproj)


EW_TC = 512


def _sigmoid(x):
    return 0.5 * jnp.tanh(0.5 * x) + 0.5


def _merge_fwd(proj, conv_out, attn_out, name):
    s = proj.shape[0]
    tm = _row_tile(s)
    tile = pl.BlockSpec((tm, EW_TC), lambda i, j: (i, j))

    def body(gc_ref, ga_ref, co_ref, ao_ref, o_ref):
        o_ref[...] = (_sigmoid(gc_ref[...].astype(F32)) * co_ref[...].astype(F32)
                      + _sigmoid(ga_ref[...].astype(F32)) * ao_ref[...].astype(F32)).astype(BF16)

    return _call(
        body, name=name, grid=(s // tm, D // EW_TC),
        in_specs=[pl.BlockSpec((tm, EW_TC), lambda i, j: (i, C_GC // EW_TC + j)),
                  pl.BlockSpec((tm, EW_TC), lambda i, j: (i, C_GA // EW_TC + j)), tile, tile],
        out_specs=tile, out_shape=_sds((s, D), BF16), compiler_params=_params("parallel", "parallel"),
    )(proj, proj, conv_out, attn_out)


def _merge_bwd(dmerged, proj, conv_out, attn_out, name):
    s = proj.shape[0]
    tm = _row_tile(s)
    tile = pl.BlockSpec((tm, EW_TC), lambda i, j: (i, j))
    anyspec = pl.BlockSpec(memory_space=pl.ANY)

    def body(dm_ref, gc_ref, ga_ref, co_ref, ao_ref, dproj_ref, dco_ref, dao_ref, buf, sems):
        i, j = pl.program_id(0), pl.program_id(1)
        dm = dm_ref[...].astype(F32)
        sc = _sigmoid(gc_ref[...].astype(F32))
        sa = _sigmoid(ga_ref[...].astype(F32))
        dco_ref[...] = (dm * sc).astype(BF16)
        dao_ref[...] = (dm * sa).astype(BF16)
        tiles = ((dm * co_ref[...].astype(F32) * sc * (1.0 - sc)).astype(BF16),
                 (dm * ao_ref[...].astype(F32) * sa * (1.0 - sa)).astype(BF16))

        def window(p, at):
            start = pl.multiple_of((C_GC, C_GA)[p] + at[1] * EW_TC, EW_TC)
            return dproj_ref.at[pl.ds(pl.multiple_of(at[0] * tm, tm), tm), pl.ds(start, EW_TC)]

        _write_behind(i * nj + j, (s // tm) * nj, buf, sems, tiles, window, (i, j))

    nj = D // EW_TC
    return _call(
        body, name=name, grid=(s // tm, nj),
        in_specs=[tile, pl.BlockSpec((tm, EW_TC), lambda i, j: (i, C_GC // EW_TC + j)),
                  pl.BlockSpec((tm, EW_TC), lambda i, j: (i, C_GA // EW_TC + j)), tile, tile],
        out_specs=[anyspec, tile, tile],
        out_shape=[_sds((s, N_IN), BF16), _sds((s, D), BF16), _sds((s, D), BF16)],
        scratch_shapes=[pltpu.VMEM((2, 2, tm, EW_TC), BF16), pltpu.SemaphoreType.DMA((2, 2))],
        compiler_params=_params("arbitrary", "arbitrary"),
    )(dmerged, proj, proj, conv_out, attn_out)


FF_TC = 256


def _gate_up_fwd(h2, wgu_t, name):
    s = h2.shape[0]
    tm = min(2048, s)
    nb = D_FF // FF_TC

    def body(h_ref, wg_ref, wu_ref, g_ref, u_ref, a_ref):
        h = h_ref[...]
        g = lax.dot_general(h, wg_ref[...], NT, preferred_element_type=F32)
        u = lax.dot_general(h, wu_ref[...], NT, preferred_element_type=F32)
        g_ref[...] = g.astype(BF16)
        u_ref[...] = u.astype(BF16)
        a_ref[...] = (g * _sigmoid(g) * u).astype(BF16)

    tile = pl.BlockSpec((tm, FF_TC), lambda i, j: (i, j))
    return _call(
        body, name=name, grid=(s // tm, nb),
        in_specs=[pl.BlockSpec((tm, D), lambda i, j: (i, 0)), pl.BlockSpec((FF_TC, D), lambda i, j: (j, 0)),
                  pl.BlockSpec((FF_TC, D), lambda i, j: (nb + j, 0))],
        out_specs=[tile, tile, tile], out_shape=[_sds((s, D_FF), BF16)] * 3,
        compiler_params=_params("parallel", "parallel"),
    )(h2, wgu_t, wgu_t)


def _down_bwd_x(dx2b, wd, gate, up, name):
    s = dx2b.shape[0]
    tm = min(2048, s)
    nb = D_FF // FF_TC

    def body(dx_ref, w_ref, g_ref, u_ref, dg_ref, du_ref):
        da = lax.dot_general(dx_ref[...], w_ref[...], NT, preferred_element_type=F32)
        g = g_ref[...].astype(F32)
        sg = _sigmoid(g)
        dg_ref[...] = (da * u_ref[...].astype(F32) * (sg * (1.0 + g * (1.0 - sg)))).astype(BF16)
        du_ref[...] = (da * (g * sg)).astype(BF16)

    tile = pl.BlockSpec((tm, FF_TC), lambda i, j: (i, j))
    return _call(
        body, name=name, grid=(s // tm, nb),
        in_specs=[pl.BlockSpec((tm, D), lambda i, j: (i, 0)), pl.BlockSpec((FF_TC, D), lambda i, j: (j, 0)), tile, tile],
        out_specs=[tile, tile], out_shape=[_sds((s, D_FF), BF16)] * 2,
        compiler_params=_params("parallel", "parallel"),
    )(dx2b, wd, gate, up)


class _Weights:
    def __init__(self, **groups):
        self.groups = groups

    def begin(self, group, after):
        return ()

    def end(self, group, after):
        return self.groups[group]


class _NoReduce:
    def start(self, group, grads):
        return ()

    def middle(self, group, after):
        return ()


def _local_step(x, tgt, g_mix, g_ffn, g_final, sinks, weights, reducer=None, after=()):
    reducer = reducer or _NoReduce()
    s = x.shape[0]
    tab = _rope_tables(s)
    big = dict(tm=1024, tn=512, tk=1024)
    h1 = _rms_fwd(x, g_mix, "rms1_fwd", after=after)
    win_t, conv_w = weights.end("in", weights.begin("in", (h1,)))
    proj = _matmul(h1, win_t, mode="nt", out_dtype=BF16, name="proj_fwd", tm=2048, tn=512, tk=1024)
    attn = _swa_fwd(proj, tab, sinks, "attn_fwd", after=weights.begin("mix", (proj,)))
    wco, wao, wo = weights.end("mix", (attn,))
    conv_y = _conv_fwd(proj, conv_w, "conv_fwd")
    conv_out = _matmul(conv_y, wco, mode="nn", out_dtype=BF16, name="conv_out_fwd", **big)
    attn_out = _matmul(attn, wao, mode="nn", out_dtype=BF16, name="attn_out_fwd", **big)
    merged = _merge_fwd(proj, conv_out, attn_out, "merge_fwd")
    x1 = _matmul(merged, wo, mode="nn", out_dtype=F32, name="wo_fwd", res=x, after=weights.begin("ffn", (merged,)), **big)
    h2 = _rms_fwd(x1, g_ffn, "rms2_fwd")
    wgu_t, wd = weights.end("ffn", (h2,))
    gate, up, act = _gate_up_fwd(h2, wgu_t, "gate_up_fwd")
    x2 = _matmul(act, wd, mode="nn", out_dtype=F32, name="down_fwd", res=x1, tm=1024, tn=512, tk=D_FF)
    dx2, dx2b, dg_final, lossvec = _loss_head(x2, g_final, tgt, "loss_head")
    dgate, dup = _down_bwd_x(dx2b, wd, gate, up, "down_bwd_x")
    g_wd = _matmul(act, dx2b, mode="tn", out_dtype=BF16, name="down_bwd_w", tm=1408, tn=1024, tk=2048)
    dh2 = _matmul([dgate, dup], wgu_t, mode="nn", out_dtype=F32, name="gate_up_bwd_x", tm=1024, tn=1024, tk=1408)
    g_wgu_t = _matmul([dgate, dup], h2, mode="tn", out_dtype=BF16, name="gate_up_bwd_w", tm=1408, tn=1024, tk=2048)
    after_ffn = reducer.start("ffn", dict(wgu_t=g_wgu_t, wd=g_wd))
    dx1, dx1b, dg_ffn = _rms_bwd(dh2, x1, g_ffn, dx2, "rms2_bwd")
    dmerged = _matmul(dx1b, wo, mode="nt", out_dtype=BF16, name="wo_bwd_x", after=after_ffn, **big)
    after_ffn = reducer.middle("ffn", (dmerged,))
    g_wo = _matmul(merged, dx1b, mode="tn", out_dtype=BF16, name="wo_bwd_w", tm=512, tn=1024, tk=2048, after=after_ffn)
    dproj, dco, dao = _merge_bwd(dmerged, proj, conv_out, attn_out, "merge_bwd")
    dconv_y = _matmul(dco, wco, mode="nt", out_dtype=BF16, name="conv_out_bwd_x", **big)
    g_wco = _matmul(conv_y, dco, mode="tn", out_dtype=BF16, name="conv_out_bwd_w", tm=512, tn=1024, tk=2048)
    dattn = _matmul(dao, wao, mode="nt", out_dtype=BF16, name="attn_out_bwd_x", **big)
    g_wao = _matmul(attn, dao, mode="tn", out_dtype=BF16, name="attn_out_bwd_w", tm=512, tn=1024, tk=2048)
    after_mix = reducer.start("mix", dict(wco=g_wco, wao=g_wao, wo=g_wo))
    dproj, dconv_w = _conv_bwd(dconv_y, proj, conv_w, dproj, "conv_bwd", after=after_mix)
    after_mix = reducer.middle("mix", (dconv_w,))
    dproj, dkr, dv, dsinks = _attn_bwd(dattn, proj, tab, sinks, dproj, "attn_bwd", after=after_mix)
    dproj = _kv_bwd(dkr, dv, tab, dproj, "kv_bwd")
    g_win_t = _matmul(dproj, h1, mode="tn", out_dtype=BF16, name="proj_bwd_w", tm=512, tn=1024, tk=2048)
    after_in = reducer.middle("in", reducer.start("in", dict(win_t=g_win_t)))
    dh1 = _matmul(dproj, win_t, mode="nn", out_dtype=F32, name="proj_bwd_x", tm=1024, tn=1024, tk=1664, after=after_in)
    dx, _, dg_mix = _rms_bwd(dh1, x, g_mix, dx1, "rms1_bwd")
    grads = dict(win_t=g_win_t, wgu_t=g_wgu_t, wd=g_wd, wco=g_wco, wao=g_wao, wo=g_wo)
    small = dict(g_mix=dg_mix, g_ffn=dg_ffn, g_final=dg_final, conv_w=dconv_w, sinks=dsinks, lossvec=lossvec)
    return dx, grads, small


def _position():
    return lax.axis_index("x"), lax.axis_index("y"), lax.axis_index("c")


def _other_chips(x, y):
    return [(1 - x, y), (x, 1 - y), (1 - x, 1 - y)]


SEM_SPEC = pl.BlockSpec(memory_space=pltpu.SEMAPHORE)
EFFECT = pltpu.SideEffectType.DATAFLOW_SIDE_EFFECTING
TOKEN = jax.ShapeDtypeStruct((8, 128), F32)
TOKEN_SPEC = pl.BlockSpec(memory_space=pltpu.VMEM)


def _hbm(a):
    return pltpu.with_memory_space_constraint(a, pltpu.HBM)


def _place(w, me_idx, dtype, name, after=()):
    r, cdim = w.shape

    def body(i_ref, w_ref, *rest):
        rest[-1][...] = w_ref[...].astype(dtype)

    grid_spec = pltpu.PrefetchScalarGridSpec(
        num_scalar_prefetch=1, grid=(1,), in_specs=[pl.BlockSpec((r, cdim), lambda i, me: (0, 0))] + [HBM_SPEC] * len(after),
        out_specs=pl.BlockSpec((r, cdim), lambda i, me: (me[0], 0)))
    return _call(body, name=name, grid_spec=grid_spec, out_shape=_sds((N_DEV * r, cdim), dtype),
                 compiler_params=_params("arbitrary"))(me_idx, w, *after)


def _own_rows(ref, r, px, py, pc):
    return ref.at[pl.ds((4 * px + 2 * py + pc) * r, r), :]


def _gather_start(bufs, groups, name):
    n = len(bufs)
    rows = [b.shape[0] // N_DEV for b in bufs]
    ng = len(groups)

    def body(*refs):
        ins = refs[:n]
        sems = refs[n:n + 2 * ng]
        token = refs[-1]
        x, y, c = _position()
        targets = [(x, y, 1 - c)] + [(*chip, c) for chip in _other_chips(x, y)]
        for g, members in enumerate(groups):
            for slot, a in enumerate(members):
                own = _own_rows(ins[a], rows[a], x, y, c)
                for to in targets:
                    pltpu.make_async_remote_copy(src_ref=own, dst_ref=own, send_sem=sems[2 * g].at[slot],
                                                 recv_sem=sems[2 * g + 1].at[slot], device_id=to, device_id_type=MESH).start()
        token[...] = jnp.zeros_like(token)

    sem_shapes = []
    for members in groups:
        sem_shapes += [pltpu.SemaphoreType.DMA((len(members),))] * 2
    outs = _call(
        body, name=name, in_specs=[HBM_SPEC] * n, out_specs=[SEM_SPEC] * (2 * ng) + [HBM_SPEC] * n + [TOKEN_SPEC],
        out_shape=sem_shapes + [pltpu.HBM(b.shape, b.dtype) for b in bufs] + [TOKEN],
        input_output_aliases={i: 2 * ng + i for i in range(n)},
        compiler_params=pltpu.CompilerParams(has_side_effects=EFFECT),
    )(*[_hbm(b) for b in bufs])
    sem_pairs = [(outs[2 * g], outs[2 * g + 1]) for g in range(ng)]
    return sem_pairs, list(outs[2 * ng:2 * ng + n]), outs[-1]


def _gather_forward(send_sems, recv_sems, bufs, after, name):
    n = len(bufs)
    rows = [b.shape[0] // N_DEV for b in bufs]

    def body(*refs):
        ins = refs[:n]
        send1, recv1 = refs[n], refs[n + 1]
        out0 = n + 2 + len(after)
        send2, recv2 = refs[out0], refs[out0 + 1]
        token = refs[-1]
        x, y, c = _position()
        for a in range(n):
            step1 = pltpu.make_async_remote_copy(
                src_ref=_whole(ins[a], 4 * rows[a]), dst_ref=_whole(ins[a], 4 * rows[a]), send_sem=send1.at[a],
                recv_sem=recv1.at[a], device_id=(x, y, c), device_id_type=MESH)
            step1.wait_send()
            step1.wait_recv()
        for a in range(n):
            for chip in _other_chips(x, y):
                blk = _own_rows(ins[a], rows[a], *chip, c)
                pltpu.make_async_remote_copy(src_ref=blk, dst_ref=blk, send_sem=send2.at[a], recv_sem=recv2.at[a],
                                             device_id=(x, y, 1 - c), device_id_type=MESH).start()
        token[...] = jnp.zeros_like(token)

    outs = _call(
        body, name=name, in_specs=[HBM_SPEC] * n + [SEM_SPEC, SEM_SPEC] + [HBM_SPEC] * len(after),
        out_specs=[SEM_SPEC, SEM_SPEC] + [HBM_SPEC] * n + [TOKEN_SPEC],
        out_shape=[pltpu.SemaphoreType.DMA((n,)), pltpu.SemaphoreType.DMA((n,))]
        + [pltpu.HBM(b.shape, b.dtype) for b in bufs] + [TOKEN],
        input_output_aliases={i: 2 + i for i in range(n)},
        compiler_params=pltpu.CompilerParams(has_side_effects=EFFECT),
    )(*bufs, send_sems, recv_sems, *after)
    return outs[0], outs[1], list(outs[2:2 + n]), outs[-1]


def _gather_done(send_sems, recv_sems, bufs, after, name):
    n = len(bufs)
    rows = [b.shape[0] // N_DEV for b in bufs]

    def body(*refs):
        ins = refs[:n]
        send2, recv2 = refs[n], refs[n + 1]
        x, y, c = _position()
        for a in range(n):
            step2 = pltpu.make_async_remote_copy(
                src_ref=_whole(ins[a], 3 * rows[a]), dst_ref=_whole(ins[a], 3 * rows[a]), send_sem=send2.at[a],
                recv_sem=recv2.at[a], device_id=(x, y, c), device_id_type=MESH)
            step2.wait_send()
            step2.wait_recv()

    outs = _call(
        body, name=name, in_specs=[HBM_SPEC] * n + [SEM_SPEC, SEM_SPEC] + [HBM_SPEC] * len(after),
        out_specs=[HBM_SPEC] * n, out_shape=[pltpu.HBM(b.shape, b.dtype) for b in bufs],
        input_output_aliases={i: i for i in range(n)},
        compiler_params=pltpu.CompilerParams(has_side_effects=EFFECT),
    )(*bufs, send_sems, recv_sems, *after)
    return list(outs)


def _whole(ref, nrows):
    return ref.at[pl.ds(0, nrows), :]


def _to_sibling(x, y, c):
    return [(2 * q + (1 - c), q, (x, y, 1 - c)) for q in range(4)]


def _to_chips(x, y, c):
    return [(2 * px + py, j, (px, py, c)) for j, (px, py) in enumerate(_other_chips(x, y))]


def _exchange_start(srcs, src_slots, plan, name):
    n = len(srcs)
    rows = [a.shape[0] // src_slots for a in srcs]
    n_copies = len(plan(0, 0, 0))
    lands = [lax.empty((n_copies * r, a.shape[1]), a.dtype) for a, r in zip(srcs, rows)]

    def body(*refs):
        ins, land_refs = refs[:n], refs[n:2 * n]
        send_sems, recv_sems = refs[2 * n], refs[2 * n + 1]
        token = refs[-1]
        for a in range(n):
            r = rows[a]
            for src_slot, dst_slot, target in plan(*_position()):
                pltpu.make_async_remote_copy(
                    src_ref=ins[a].at[pl.ds(src_slot * r, r), :], dst_ref=land_refs[a].at[pl.ds(dst_slot * r, r), :],
                    send_sem=send_sems.at[a], recv_sem=recv_sems.at[a], device_id=target, device_id_type=MESH).start()
        token[...] = jnp.zeros_like(token)

    outs = _call(
        body, name=name, in_specs=[HBM_SPEC] * (2 * n),
        out_specs=[SEM_SPEC, SEM_SPEC] + [HBM_SPEC] * (2 * n) + [TOKEN_SPEC],
        out_shape=[pltpu.SemaphoreType.DMA((n,)), pltpu.SemaphoreType.DMA((n,))]
        + [pltpu.HBM(a.shape, a.dtype) for a in srcs] + [pltpu.HBM(l.shape, l.dtype) for l in lands] + [TOKEN],
        input_output_aliases={i: 2 + i for i in range(2 * n)},
        compiler_params=pltpu.CompilerParams(has_side_effects=EFFECT),
    )(*[_hbm(a) for a in srcs], *[_hbm(l) for l in lands])
    return outs[0], outs[1], list(outs[2:2 + n]), list(outs[2 + n:2 + 2 * n]), outs[-1]


def _exchange_wait(send_sems, recv_sems, srcs, lands, after, name):
    n = len(srcs)

    def body(*refs):
        ins, land_refs = refs[:n], refs[n:2 * n]
        send_sems_ref, recv_sems_ref = refs[2 * n], refs[2 * n + 1]
        for a in range(n):
            allrows = lands[a].shape[0]
            cp = pltpu.make_async_remote_copy(
                src_ref=_whole(ins[a], allrows), dst_ref=_whole(land_refs[a], allrows), send_sem=send_sems_ref.at[a],
                recv_sem=recv_sems_ref.at[a], device_id=_position(), device_id_type=MESH)
            cp.wait_send()
            cp.wait_recv()

    outs = _call(
        body, name=name, in_specs=[HBM_SPEC] * (2 * n) + [SEM_SPEC, SEM_SPEC] + [HBM_SPEC] * len(after),
        out_specs=[HBM_SPEC] * (2 * n),
        out_shape=[pltpu.HBM(a.shape, a.dtype) for a in srcs] + [pltpu.HBM(l.shape, l.dtype) for l in lands],
        input_output_aliases={i: i for i in range(2 * n)},
        compiler_params=pltpu.CompilerParams(has_side_effects=EFFECT),
    )(*srcs, *lands, send_sems, recv_sems, *after)
    return list(outs[:n]), list(outs[n:])


def _chip_partial(grad, recv, c_idx, name):
    r = recv.shape[0] // 4

    def body(c_ref, g_ref, s_ref, o_ref):
        del c_ref
        o_ref[...] = (g_ref[...].astype(F32) + s_ref[...].astype(F32)).astype(BF16)

    grid_spec = pltpu.PrefetchScalarGridSpec(
        num_scalar_prefetch=1, grid=(4,),
        in_specs=[pl.BlockSpec((r, D), lambda q, c_ref: (2 * q + c_ref[0], 0)), pl.BlockSpec((r, D), lambda q, c_ref: (q, 0))],
        out_specs=pl.BlockSpec((r, D), lambda q, c_ref: (q, 0)))
    return _call(body, name=name, grid_spec=grid_spec, out_shape=_sds((4 * r, D), BF16),
                 compiler_params=_params("parallel"))(c_idx, grad, recv)


def _adamw_math(w, g, m, v):
    m2 = B1 * m + (1.0 - B1) * g
    v2 = B2 * v + (1.0 - B2) * jnp.square(g)
    m_hat = m2 / (1.0 - B1 ** STEP)
    v_hat = v2 / (1.0 - B2 ** STEP)
    return -LR * (m_hat / (jnp.sqrt(v_hat) + EPS_ADAM) + WD * w), m2, v2


def _reduce_adamw(w, part, recv, q_idx, m, v, name):
    r = w.shape[0]
    assert part.shape == (4 * r, D) and recv.shape == (3 * r, D) and w.shape == (r, D)
    tr = r // 2
    nb = r // tr

    def body(q_ref, w_ref, p_ref, r0_ref, r1_ref, r2_ref, m_ref, v_ref, g_ref, d_ref, nm_ref, nv_ref):
        del q_ref
        g = ((p_ref[...].astype(F32) + r0_ref[...].astype(F32)) + r1_ref[...].astype(F32)) + r2_ref[...].astype(F32)
        g_ref[...] = g
        d_ref[...], nm_ref[...], nv_ref[...] = _adamw_math(w_ref[...], g, m_ref[...], v_ref[...])

    own = pl.BlockSpec((tr, D), lambda i, q_ref: (i, 0))
    grid_spec = pltpu.PrefetchScalarGridSpec(
        num_scalar_prefetch=1, grid=(nb,),
        in_specs=[own, pl.BlockSpec((tr, D), lambda i, q_ref: (q_ref[0] * nb + i, 0))]
        + [pl.BlockSpec((tr, D), lambda i, q_ref, j=j: (j * nb + i, 0)) for j in range(3)] + [own, own],
        out_specs=[own] * 4)
    return _call(body, name=name, grid_spec=grid_spec, out_shape=[_sds((r, D), F32)] * 4,
                 compiler_params=_params("parallel"))(q_idx, w, part, recv, recv, recv, m, v)


SMALL_ROWS = 8


def _small_all_reduce(pack, name, after=()):
    def body(p_ref, *rest):
        tot_ref, loss_ref, gath, send_sems, recv_sems = rest[len(after):]
        x, y, c = _position()
        me_id = 4 * x + 2 * y + c
        gath[me_id] = p_ref[...]
        copies = []
        for k in range(1, N_DEV):
            peer = tuple(1 - v if (k >> b) & 1 else v for v, b in ((x, 2), (y, 1), (c, 0)))
            cp = pltpu.make_async_remote_copy(src_ref=p_ref, dst_ref=gath.at[me_id], send_sem=send_sems.at[k - 1],
                                              recv_sem=recv_sems.at[k - 1], device_id=peer, device_id_type=MESH)
            cp.start()
            copies.append(cp)
        for cp in copies:
            cp.wait_recv()
        for cp in copies:
            cp.wait_send()
        tot = gath[0]
        for d in range(1, N_DEV):
            tot = tot + gath[d]
        tot_ref[...] = tot
        loss_ref[...] = jnp.full((1, 128), (0.5 / D) * jnp.sum(tot[SMALL_ROWS - 1:SMALL_ROWS, :]), F32)

    vm = pl.BlockSpec(memory_space=pltpu.VMEM)
    return _call(
        body, name=name, in_specs=[vm] + [HBM_SPEC] * len(after), out_specs=[vm, vm],
        out_shape=[_sds((SMALL_ROWS, D), F32), _sds((1, 128), F32)],
        scratch_shapes=[pltpu.VMEM((N_DEV, SMALL_ROWS, D), F32), pltpu.SemaphoreType.DMA((N_DEV - 1,)),
                        pltpu.SemaphoreType.DMA((N_DEV - 1,))],
    )(pack, *after)


def _adamw(w, g, m, v, name):
    r, cdim = w.shape
    tr = 256 if r % 256 == 0 else (r // 2 if r % 16 == 0 else r)

    def body(w_ref, g_ref, m_ref, v_ref, d_ref, nm_ref, nv_ref):
        d_ref[...], nm_ref[...], nv_ref[...] = _adamw_math(w_ref[...], g_ref[...], m_ref[...], v_ref[...])

    spec = pl.BlockSpec((tr, cdim), lambda i: (i, 0))
    return _call(
        body, name=name, grid=(r // tr,), in_specs=[spec] * 4, out_specs=[spec] * 3,
        out_shape=[_sds((r, cdim), F32)] * 3, compiler_params=_params("parallel"),
    )(w, g, m, v)


def kernel(x, g_mix, w_in, conv_w, attn_sinks, w_conv_out, w_attn_out, w_o, g_ffn, w_gate_up, w_down, g_final, loss_target, m_g_mix, m_w_in, m_conv_w, m_attn_sinks, m_w_conv_out, m_w_attn_out, m_w_o, m_g_ffn, m_w_gate_up, m_w_down, m_g_final, v_g_mix, v_w_in, v_conv_w, v_attn_sinks, v_w_conv_out, v_w_attn_out, v_w_o, v_g_ffn, v_w_gate_up, v_w_down, v_g_final):
    cx, cy, cc = _position()
    c_idx = jnp.reshape(cc, (1,)).astype(jnp.int32)
    q_idx = jnp.reshape(2 * cx + cy, (1,)).astype(jnp.int32)
    me = 4 * cx + 2 * cy + cc

    me_idx = jnp.reshape(me, (1,)).astype(jnp.int32)
    first = [_place(jnp.transpose(w_in[0]), me_idx, BF16, "place_w_in"),
             _place(jnp.pad(conv_w[0], ((0, 5), (0, 0))), me_idx, F32, "place_conv_w")]
    (sems_in,), first, token_in = _gather_start(first, [[0, 1]], "gather_start_in")
    later = [_place(w, me_idx, BF16, "place_" + k, after=(token_in,)) for k, w in (
        ("w_conv_out", w_conv_out[0]), ("w_attn_out", w_attn_out[0]), ("w_o", w_o[0]),
        ("w_gate_up", jnp.transpose(w_gate_up[0])), ("w_down", w_down[0]))]
    (sems_mix, sems_ffn), later, token_later = _gather_start(later, [[0, 1, 2], [3, 4]], "gather_start_later")
    gather_tokens = (token_in, token_later)

    class Gathered:
        def __init__(self):
            self.state = {"in": (sems_in, first), "mix": (sems_mix, later[:3]), "ffn": (sems_ffn, later[3:])}

        def begin(self, group, after):
            (send_sems, recv_sems), group_bufs = self.state[group]
            send2, recv2, group_bufs, token = _gather_forward(send_sems, recv_sems, group_bufs, after, "gather_forward_" + group)
            self.state[group] = ((send2, recv2), group_bufs)
            return (token,)

        def end(self, group, after):
            (send2, recv2), group_bufs = self.state[group]
            full = _gather_done(send2, recv2, group_bufs, after, "gather_done_" + group)
            if group == "in":
                return full[0], jnp.transpose(full[1].reshape(N_DEV, 8, 128)[:, :3, :], (1, 0, 2)).reshape(3, D)
            return full

    in_flight = {}

    class Reducer:
        def start(self, group, gdict):
            keys, glist = list(gdict), list(gdict.values())
            send_sems, recv_sems, glist, lands, token = _exchange_start(glist, N_DEV, _to_sibling, "rs_sibling_start_" + group)
            in_flight[group] = (keys, send_sems, recv_sems, glist, lands)
            return (token,)

        def middle(self, group, after):
            keys, send_sems, recv_sems, glist, lands = in_flight[group]
            glist, lands = _exchange_wait(send_sems, recv_sems, glist, lands, after, "rs_sibling_wait_" + group)
            parts = [_chip_partial(g, r, c_idx, "chip_partial_" + k) for k, g, r in zip(keys, glist, lands)]
            send_sems, recv_sems, parts, lands, token = _exchange_start(parts, 4, _to_chips, "rs_chips_start_" + group)
            in_flight[group] = (keys, send_sems, recv_sems, parts, lands)
            return (token,)

    dx, _, small = _local_step(x[0], loss_target[0], g_mix, g_ffn, g_final[None], attn_sinks, Gathered(),
                               reducer=Reducer(), after=gather_tokens)

    transposed = ("w_in", "w_gate_up")

    def as2d(k, a):
        if k in transposed:
            return jnp.transpose(a[0])
        return a[None] if a.ndim == 1 else (a[0] if a.ndim == 3 else a)

    w_all = {"g_mix": g_mix, "w_in": w_in, "conv_w": conv_w, "attn_sinks": attn_sinks, "w_conv_out": w_conv_out,
             "w_attn_out": w_attn_out, "w_o": w_o, "g_ffn": g_ffn, "w_gate_up": w_gate_up, "w_down": w_down, "g_final": g_final}
    m_all = {"g_mix": m_g_mix, "w_in": m_w_in, "conv_w": m_conv_w, "attn_sinks": m_attn_sinks, "w_conv_out": m_w_conv_out,
             "w_attn_out": m_w_attn_out, "w_o": m_w_o, "g_ffn": m_g_ffn, "w_gate_up": m_w_gate_up, "w_down": m_w_down,
             "g_final": m_g_final}
    v_all = {"g_mix": v_g_mix, "w_in": v_w_in, "conv_w": v_conv_w, "attn_sinks": v_attn_sinks, "w_conv_out": v_w_conv_out,
             "w_attn_out": v_w_attn_out, "w_o": v_w_o, "g_ffn": v_g_ffn, "w_gate_up": v_w_gate_up, "w_down": v_w_down,
             "g_final": v_g_final}
    results = {}

    def update(k, g=None, part=None, recv=None):
        w2, m2, v2 = as2d(k, w_all[k]), as2d(k, m_all[k]), as2d(k, v_all[k])
        if g is None:
            g, d, nm, nv = _reduce_adamw(w2, part, recv, q_idx, m2, v2, "adamw_" + k)
        else:
            d, nm, nv = _adamw(w2, g, m2, v2, "adamw_" + k)
        results[k] = [(jnp.transpose(val) if k in transposed else val).reshape(w_all[k].shape) for val in (g, d, nm, nv)]
        return nm

    kernel_name = {"win_t": "w_in", "wgu_t": "w_gate_up", "wd": "w_down", "wco": "w_conv_out", "wao": "w_attn_out", "wo": "w_o"}

    def finish(group, after):
        keys, send_sems, recv_sems, parts, lands = in_flight[group]
        parts, lands = _exchange_wait(send_sems, recv_sems, parts, lands, after, "rs_chips_wait_" + group)
        return tuple(update(kernel_name[k], part=p, recv=r) for k, p, r in zip(keys, parts, lands))

    after = finish("mix", finish("ffn", (dx,)))

    sinks_row = jnp.pad(small["sinks"], ((0, 0), (0, D - 128)))
    pack = jnp.concatenate([small["g_mix"], small["g_ffn"], small["g_final"], small["conv_w"], sinks_row, small["lossvec"]], axis=0)
    tot, loss_row = _small_all_reduce(pack, "small_all_reduce", after=after)
    loss = loss_row[0, 0]
    g_small = {
        "g_mix": tot[0:1], "g_ffn": tot[1:2], "g_final": tot[2:3],
        "conv_w": lax.dynamic_slice(tot, (3, me * 128), (3, 128)), "attn_sinks": tot[6:7, :N_HEADS],
    }
    finish("in", tuple(update(k, g) for k, g in g_small.items()))

    order = ["g_mix", "w_in", "conv_w", "attn_sinks", "w_conv_out", "w_attn_out", "w_o", "g_ffn", "w_gate_up", "w_down", "g_final"]
    return (loss, dx[None], *[results[k][i] for i in range(4) for k in order])
```

```python
import functools
import math

import jax
import jax.numpy as jnp
from jax import lax
from jax.experimental import pallas as pl
from jax.experimental.pallas import tpu as pltpu

F32 = jnp.float32
BF16 = jnp.bfloat16

D = 1024
HEAD_DIM = 64
N_HEADS = 16
N_KV = 4
GROUP = N_HEADS // N_KV
D_KV = N_KV * HEAD_DIM
BLOCK = 128
ROT_DIM = HEAD_DIM // 4
ROPE_THETA = 500000.0
ATTN_SCALE = 1.0 / math.sqrt(HEAD_DIM)
NEG_INF = -1e30
D_FF = 2816
N_IN = 6656
EPS = 1e-5
C_CB, C_CC, C_CX, C_Q, C_K, C_V, C_GC, C_GA = 0, 1024, 2048, 3072, 4096, 4352, 4608, 5632

LR, B1, B2, EPS_ADAM, WD, STEP = 0.001, 0.9, 0.999, 1e-08, 0.01, 10

N_DEV = 8
MESH = pl.DeviceIdType.MESH
VMEM_LIMIT = 56 * 1024 * 1024

NN = (((1,), (0,)), ((), ()))
NT = (((1,), (1,)), ((), ()))
TN = (((0,), (0,)), ((), ()))
HBM_SPEC = pl.BlockSpec(memory_space=pl.ANY)


def _call(body, **kw):
    return pl.pallas_call(body, **kw)


def _params(*sem):
    return pltpu.CompilerParams(dimension_semantics=sem, vmem_limit_bytes=VMEM_LIMIT)


def _sds(shape, dtype):
    return jax.ShapeDtypeStruct(shape, dtype)


def _matmul(a, b, *, mode, tm, tn, tk, out_dtype, name, res=None, after=()):
    parts = list(a) if isinstance(a, (list, tuple)) else [a]
    rows_a = parts[0].shape[0]
    cols_a = sum(p.shape[1] for p in parts)
    if mode == "nn":
        (m, kk), (_, n), dims = (rows_a, cols_a), b.shape, NN
    elif mode == "nt":
        (m, kk), (n, _), dims = (rows_a, cols_a), b.shape, NT
    else:
        (kk, m), (_, n), dims = (rows_a, cols_a), b.shape, TN
    tm, tn, tk = min(tm, m), min(tn, n), min(tk, kk)
    assert m % tm == 0 and n % tn == 0 and kk % tk == 0, (name, m, n, kk, tm, tn, tk)
    nk = kk // tk
    split_axis, width = (2, tk) if mode == "nn" else (0, tm)
    assert len(parts) == 1 or mode in ("nn", "tn")
    assert len(parts) == 1 or all(p.shape[1] % width == 0 for p in parts), (name, width)
    counts = [p.shape[1] // width for p in parts]
    starts = [sum(counts[:p]) for p in range(len(parts))]

    def a_spec(p):
        def col(t):
            return jnp.clip(t - starts[p], 0, counts[p] - 1) if len(parts) > 1 else t

        if mode == "tn":
            return pl.BlockSpec((tk, tm), lambda i, j, k: (k, col(i)))
        return pl.BlockSpec((tm, tk), lambda i, j, k: (i, col(k)))

    if mode == "nt":
        b_spec = pl.BlockSpec((tn, tk), lambda i, j, k: (j, k))
    else:
        b_spec = pl.BlockSpec((tk, tn), lambda i, j, k: (k, j))
    o_spec = pl.BlockSpec((tm, tn), lambda i, j, k: (i, j))
    has_res = res is not None
    n_parts = len(parts)

    def body(*refs):
        a_refs, b_ref = refs[:n_parts], refs[n_parts]
        r_ref = refs[n_parts + 1] if has_res else None
        o_ref = refs[n_parts + 1 + has_res + len(after)]
        k = pl.program_id(2)

        def step(a_ref):
            part = lax.dot_general(a_ref[...], b_ref[...], dims, preferred_element_type=F32)

            def finish(acc):
                if has_res:
                    acc = acc + r_ref[...]
                o_ref[...] = acc.astype(o_ref.dtype)

            if nk == 1:
                finish(part)
            else:
                acc_ref = refs[-1]

                @pl.when(k == 0)
                def _():
                    acc_ref[...] = part

                @pl.when(k > 0)
                def _():
                    acc_ref[...] += part

                @pl.when(k == nk - 1)
                def _():
                    finish(acc_ref[...])

        if n_parts == 1:
            step(a_refs[0])
        else:
            t = pl.program_id(split_axis)
            for p in range(n_parts):
                pl.when((t >= starts[p]) & (t < starts[p] + counts[p]))(functools.partial(step, a_refs[p]))

    ins = parts + [b] + ([res] if has_res else []) + list(after)
    in_specs = [a_spec(p) for p in range(n_parts)] + [b_spec] + ([o_spec] if has_res else []) + [HBM_SPEC] * len(after)
    scratch = [] if nk == 1 else [pltpu.VMEM((tm, tn), F32)]
    return _call(
        body, name=name, grid=(m // tm, n // tn, nk), in_specs=in_specs, out_specs=o_spec,
        out_shape=_sds((m, n), out_dtype), scratch_shapes=scratch,
        compiler_params=_params("parallel", "parallel", "arbitrary"),
    )(*ins)


def _row_tile(s):
    return min(512, s)


def _rms_fwd(x, g, name, after=()):
    s = x.shape[0]
    tm = _row_tile(s)

    def body(x_ref, g_ref, *rest):
        h_ref = rest[-1]
        xv = x_ref[...]
        r = lax.rsqrt(jnp.mean(xv * xv, axis=-1, keepdims=True) + EPS)
        h_ref[...] = (xv * r * g_ref[...]).astype(BF16)

    row = pl.BlockSpec((tm, D), lambda i: (i, 0))
    return _call(
        body, name=name, grid=(s // tm,), in_specs=[row, pl.BlockSpec((1, D), lambda i: (0, 0))] + [HBM_SPEC] * len(after),
        out_specs=row, out_shape=_sds((s, D), BF16), compiler_params=_params("parallel"),
    )(x, g, *after)


def _rms_bwd(dh, x, g, dres, name, after=()):
    s = x.shape[0]
    tm = _row_tile(s)

    def body(dh_ref, x_ref, g_ref, dres_ref, *rest):
        dx_ref, dxb_ref, dg_ref = rest[len(after):]
        xv = x_ref[...]
        r = lax.rsqrt(jnp.mean(xv * xv, axis=-1, keepdims=True) + EPS)
        xh = xv * r
        dhv = dh_ref[...]
        dyg = dhv * g_ref[...]
        dx = dres_ref[...] + r * (dyg - xh * jnp.mean(dyg * xh, axis=-1, keepdims=True))
        dx_ref[...] = dx
        dxb_ref[...] = dx.astype(BF16)
        part = jnp.sum(dhv * xh, axis=0, keepdims=True)

        @pl.when(pl.program_id(0) == 0)
        def _():
            dg_ref[...] = part

        @pl.when(pl.program_id(0) > 0)
        def _():
            dg_ref[...] += part

    row = pl.BlockSpec((tm, D), lambda i: (i, 0))
    vec = pl.BlockSpec((1, D), lambda i: (0, 0))
    return _call(
        body, name=name, grid=(s // tm,), in_specs=[row, row, vec, row] + [HBM_SPEC] * len(after), out_specs=[row, row, vec],
        out_shape=[_sds((s, D), F32), _sds((s, D), BF16), _sds((1, D), F32)],
        compiler_params=_params("arbitrary"),
    )(dh, x, g, dres, *after)


def _loss_head(x2, g, tgt, name):
    s = x2.shape[0]
    tm = _row_tile(s)

    def body(x_ref, g_ref, t_ref, dx_ref, dxb_ref, dg_ref, l_ref):
        xv = x_ref[...]
        gv = g_ref[...]
        r = lax.rsqrt(jnp.mean(xv * xv, axis=-1, keepdims=True) + EPS)
        xh = xv * r
        err = xh * gv - t_ref[...]
        dy = err * (1.0 / D)
        dyg = dy * gv
        dx = r * (dyg - xh * jnp.mean(dyg * xh, axis=-1, keepdims=True))
        dx_ref[...] = dx
        dxb_ref[...] = dx.astype(BF16)
        dg_part = jnp.sum(dy * xh, axis=0, keepdims=True)
        l_part = jnp.sum(err * err, axis=0, keepdims=True)

        @pl.when(pl.program_id(0) == 0)
        def _():
            dg_ref[...] = dg_part
            l_ref[...] = l_part

        @pl.when(pl.program_id(0) > 0)
        def _():
            dg_ref[...] += dg_part
            l_ref[...] += l_part

    row = pl.BlockSpec((tm, D), lambda i: (i, 0))
    vec = pl.BlockSpec((1, D), lambda i: (0, 0))
    return _call(
        body, name=name, grid=(s // tm,), in_specs=[row, vec, row], out_specs=[row, row, vec, vec],
        out_shape=[_sds((s, D), F32), _sds((s, D), BF16), _sds((1, D), F32), _sds((1, D), F32)],
        compiler_params=_params("arbitrary"),
    )(x2, g, tgt)


CONV_TC = 128


def _shift_down(u, k, rows):
    return jnp.where(rows >= k, pltpu.roll(u, k, 0), 0.0)


def _shift_up(u, k, rows, s):
    return jnp.where(rows < s - k, pltpu.roll(u, s - k, 0), 0.0)


def _conv_specs(s):
    nb = D // CONV_TC

    def col(c0):
        return pl.BlockSpec((s, CONV_TC), lambda j, c0=c0: (0, c0 // CONV_TC + j))

    return nb, col


def _conv_fwd(proj, conv_w, name):
    s = proj.shape[0]
    nb, col = _conv_specs(s)

    def body(cb_ref, cc_ref, cx_ref, w_ref, y_ref):
        rows = lax.broadcasted_iota(jnp.int32, (s, CONV_TC), 0)
        u = cc_ref[...].astype(F32) * cx_ref[...].astype(F32)
        w = w_ref[...]
        c = w[0:1] * _shift_down(u, 2, rows) + w[1:2] * _shift_down(u, 1, rows) + w[2:3] * u
        y_ref[...] = (cb_ref[...].astype(F32) * c).astype(BF16)

    return _call(
        body, name=name, grid=(nb,),
        in_specs=[col(C_CB), col(C_CC), col(C_CX), pl.BlockSpec((3, CONV_TC), lambda j: (0, j))],
        out_specs=pl.BlockSpec((s, CONV_TC), lambda j: (0, j)), out_shape=_sds((s, D), BF16),
        compiler_params=_params("parallel"),
    )(proj, proj, proj, conv_w)


def _write_behind(t, nt, buf, sems, tiles, window, where):
    slot = t % 2

    def copies(sl, at):
        return [pltpu.make_async_copy(buf.at[sl, p], window(p, at), sems.at[sl, p]) for p in range(len(tiles))]

    @pl.when(t >= 2)
    def _():
        for cp in copies(slot, where):
            cp.wait()

    for p, tile in enumerate(tiles):
        buf[slot, p] = tile
    started = copies(slot, where)
    for cp in started:
        cp.start()

    @pl.when(t == nt - 1)
    def _():
        for cp in started:
            cp.wait()
        if nt > 1:
            for cp in copies(1 - slot, where):
                cp.wait()


def _conv_bwd(dy, proj, conv_w, dproj, name, after=()):
    s = proj.shape[0]
    nb, col = _conv_specs(s)

    def body(dy_ref, cb_ref, cc_ref, cx_ref, w_ref, *rest):
        dproj_ref, dw_ref, buf, sems = rest[1 + len(after):]
        j = pl.program_id(0)
        rows = lax.broadcasted_iota(jnp.int32, (s, CONV_TC), 0)
        cc = cc_ref[...].astype(F32)
        cx = cx_ref[...].astype(F32)
        u = cc * cx
        u1 = _shift_down(u, 1, rows)
        u2 = _shift_down(u, 2, rows)
        w = w_ref[...]
        c = w[0:1] * u2 + w[1:2] * u1 + w[2:3] * u
        dyv = dy_ref[...].astype(F32)
        dc = dyv * cb_ref[...].astype(F32)
        du = w[2:3] * dc + w[1:2] * _shift_up(dc, 1, rows, s) + w[0:1] * _shift_up(dc, 2, rows, s)
        def window(p, jj):
            start = pl.multiple_of((C_CB, C_CC, C_CX)[p] + jj * CONV_TC, CONV_TC)
            return dproj_ref.at[:, pl.ds(start, CONV_TC)]

        tiles = ((dyv * c).astype(BF16), (du * cx).astype(BF16), (du * cc).astype(BF16))
        _write_behind(j, nb, buf, sems, tiles, window, j)
        dw_ref[...] = jnp.concatenate(
            [jnp.sum(dc * u2, axis=0, keepdims=True), jnp.sum(dc * u1, axis=0, keepdims=True),
             jnp.sum(dc * u, axis=0, keepdims=True)], axis=0)

    return _call(
        body, name=name, grid=(nb,),
        in_specs=[pl.BlockSpec((s, CONV_TC), lambda j: (0, j)), col(C_CB), col(C_CC), col(C_CX),
                  pl.BlockSpec((3, CONV_TC), lambda j: (0, j))] + [HBM_SPEC] * (1 + len(after)),
        out_specs=[pl.BlockSpec(memory_space=pl.ANY), pl.BlockSpec((3, CONV_TC), lambda j: (0, j))],
        out_shape=[_sds((s, N_IN), BF16), _sds((3, D), F32)],
        scratch_shapes=[pltpu.VMEM((2, 3, s, CONV_TC), BF16), pltpu.SemaphoreType.DMA((2, 3))],
        input_output_aliases={5: 0}, compiler_params=_params("arbitrary"),
    )(dy, proj, proj, proj, conv_w, dproj, *after)


def _rope_tables(s):
    half = ROT_DIM // 2
    inv_freq = ROPE_THETA ** (-jnp.arange(0, ROT_DIM, 2, dtype=F32) / ROT_DIM)
    inv64 = jnp.concatenate([inv_freq, inv_freq, jnp.zeros((HEAD_DIM - ROT_DIM,), F32)])
    ang = jnp.arange(s, dtype=F32)[:, None] * jnp.concatenate([inv64, inv64])[None, :]
    d = lax.broadcasted_iota(jnp.int32, (s, 128), 1) % HEAD_DIM
    cos, sin = jnp.cos(ang), jnp.sin(ang)
    c = jnp.where(d < ROT_DIM, cos, 1.0)
    a = jnp.where(d < half, -sin, 0.0)
    b = jnp.where((d >= half) & (d < ROT_DIM), sin, 0.0)
    return jnp.concatenate([c, a, b], axis=1)


def _rope(x, tab):
    c, a, b = tab[:, 0:128], tab[:, 128:256], tab[:, 256:384]
    outs = []
    for i in range(x.shape[1] // 128):
        xc = x[:, i * 128:(i + 1) * 128]
        outs.append(xc * c + pltpu.roll(xc, 120, 1) * a + pltpu.roll(xc, 8, 1) * b)
    return outs[0] if len(outs) == 1 else jnp.concatenate(outs, axis=1)


def _rope_t(dx, tab):
    c, a, b = tab[:, 0:128], tab[:, 128:256], tab[:, 256:384]
    outs = []
    for i in range(dx.shape[1] // 128):
        dc = dx[:, i * 128:(i + 1) * 128]
        outs.append(dc * c + pltpu.roll(dc * a, 8, 1) + pltpu.roll(dc * b, 120, 1))
    return outs[0] if len(outs) == 1 else jnp.concatenate(outs, axis=1)


def _attn_mask(n):
    qi = lax.broadcasted_iota(jnp.int32, (GROUP * BLOCK, 2 * BLOCK), 0) & (BLOCK - 1)
    kj = lax.broadcasted_iota(jnp.int32, (GROUP * BLOCK, 2 * BLOCK), 1)
    rel = qi + BLOCK - kj
    return (rel >= 0) & (rel < BLOCK) & ((kj >= BLOCK) | (n > 0))


def _sink_col(sink_ref, hk):
    return jnp.concatenate([jnp.full((BLOCK, 1), sink_ref[0, hk * GROUP + g], F32) for g in range(GROUP)], axis=0)


def _attn_in_specs():
    prev = lambda n: jnp.maximum(n - 1, 0)
    return [
        pl.BlockSpec((BLOCK, D), lambda n: (n, C_Q // D)),
        pl.BlockSpec((BLOCK, D_KV), lambda n: (n, C_K // D_KV)),
        pl.BlockSpec((BLOCK, D_KV), lambda n: (prev(n), C_K // D_KV)),
        pl.BlockSpec((BLOCK, D_KV), lambda n: (n, C_V // D_KV)),
        pl.BlockSpec((BLOCK, D_KV), lambda n: (prev(n), C_V // D_KV)),
        pl.BlockSpec((BLOCK, 384), lambda n: (n, 0)),
        pl.BlockSpec((BLOCK, 384), lambda n: (prev(n), 0)),
        pl.BlockSpec(memory_space=pltpu.SMEM),
    ]


def _load_qkv(q_ref, kc_ref, kp_ref, vc_ref, vp_ref, tc_ref, tp_ref):
    q = _rope(q_ref[...].astype(F32), tc_ref[...]).astype(BF16)
    kc = _rope(kc_ref[...].astype(F32), tc_ref[...]).astype(BF16)
    kp = _rope(kp_ref[...].astype(F32), tp_ref[...]).astype(BF16)
    return q, kc, kp, vc_ref[...], vp_ref[...]


def _group_rows(x, hk):
    base = hk * GROUP * HEAD_DIM
    return jnp.concatenate([x[:, base + g * HEAD_DIM: base + (g + 1) * HEAD_DIM] for g in range(GROUP)], axis=0)


def _kv_rows(prev, cur, hk):
    sl = slice(hk * HEAD_DIM, (hk + 1) * HEAD_DIM)
    return jnp.concatenate([prev[:, sl], cur[:, sl]], axis=0)


def _attn_bwd(do, proj, tab, sinks, dproj, name, after=()):
    s = proj.shape[0]
    nblk = s // BLOCK

    def body(do_ref, q_ref, kc_ref, kp_ref, vc_ref, vp_ref, tc_ref, tp_ref, sink_ref, *rest):
        dproj_ref, dk_ref, dv_ref, ds_ref, dqbuf, dqout, dkbuf, dvbuf, sem = rest[1 + len(after):]
        n = pl.program_id(0)

        @pl.when(n == 0)
        def _():
            dk_ref[...] = jnp.zeros_like(dk_ref)
            dv_ref[...] = jnp.zeros_like(dv_ref)
            ds_ref[...] = jnp.zeros_like(ds_ref)

        q, kc, kp, vc, vp = _load_qkv(q_ref, kc_ref, kp_ref, vc_ref, vp_ref, tc_ref, tp_ref)
        dov = do_ref[...]
        mask = _attn_mask(n)
        rows = GROUP * BLOCK
        head_off = lax.broadcasted_iota(jnp.int32, (rows, 128), 1) - (lax.broadcasted_iota(jnp.int32, (rows, 128), 0) >> 7)
        dsink_row = jnp.zeros((1, 128), F32)
        prev0 = pl.multiple_of(jnp.maximum(n - 1, 0) * BLOCK, BLOCK)
        cur0 = pl.multiple_of(n * BLOCK, BLOCK)
        for hk in range(N_KV):
            qg = _group_rows(q, hk)
            dog = _group_rows(dov, hk)
            kcat = _kv_rows(kp, kc, hk)
            vcat = _kv_rows(vp, vc, hk)
            sc = lax.dot_general(qg, kcat, NT, preferred_element_type=F32) * ATTN_SCALE
            sc = jnp.where(mask, sc, NEG_INF)
            sink = _sink_col(sink_ref, hk)
            m = jnp.maximum(jnp.max(sc, axis=1, keepdims=True), sink)
            e = jnp.exp(sc - m)
            es = jnp.exp(sink - m)
            inv = 1.0 / (jnp.sum(e, axis=1, keepdims=True) + es)
            p = e * inv
            pb = p.astype(BF16)
            dp = lax.dot_general(dog, vcat, NT, preferred_element_type=F32)
            delta = jnp.sum(p * dp, axis=1, keepdims=True)
            dsc = (p * (dp - delta) * ATTN_SCALE).astype(BF16)
            dsk = -(es * inv) * delta
            dsink_row = dsink_row + jnp.sum(jnp.where(head_off == hk * GROUP, dsk, 0.0), axis=0, keepdims=True)
            dqg = lax.dot_general(dsc, kcat, NN, preferred_element_type=F32)
            dkcat = lax.dot_general(dsc, qg, TN, preferred_element_type=F32)
            dvcat = lax.dot_general(pb, dog, TN, preferred_element_type=F32)
            base = hk * GROUP * HEAD_DIM
            for g in range(GROUP):
                dqbuf[:, base + g * HEAD_DIM: base + (g + 1) * HEAD_DIM] = dqg[g * BLOCK:(g + 1) * BLOCK]
            sl = slice(hk * HEAD_DIM, (hk + 1) * HEAD_DIM)
            dkbuf[:, sl] = dkcat
            dvbuf[:, sl] = dvcat

        @pl.when(n > 0)
        def _():
            dk_ref[pl.ds(prev0, BLOCK), :] += dkbuf[0:BLOCK, :]
            dv_ref[pl.ds(prev0, BLOCK), :] += dvbuf[0:BLOCK, :]

        dk_ref[pl.ds(cur0, BLOCK), :] += dkbuf[BLOCK:2 * BLOCK, :]
        dv_ref[pl.ds(cur0, BLOCK), :] += dvbuf[BLOCK:2 * BLOCK, :]
        ds_ref[...] += dsink_row
        def window(p, at):
            return dproj_ref.at[pl.ds(pl.multiple_of(at * BLOCK, BLOCK), BLOCK), pl.ds(C_Q, D)]

        _write_behind(n, nblk, dqout, sem, (_rope_t(dqbuf[...], tc_ref[...]).astype(BF16),), window, n)

    blk = lambda w: pl.BlockSpec((BLOCK, w), lambda n: (n, 0))
    whole = lambda w: pl.BlockSpec((s, w), lambda n: (0, 0))
    anyspec = pl.BlockSpec(memory_space=pl.ANY)
    n_in = 1 + len(_attn_in_specs())
    return _call(
        body, name=name, grid=(nblk,), in_specs=[blk(D)] + _attn_in_specs() + [anyspec] * (1 + len(after)),
        out_specs=[anyspec, whole(D_KV), whole(D_KV), pl.BlockSpec((1, 128), lambda n: (0, 0))],
        out_shape=[_sds((s, N_IN), BF16), _sds((s, D_KV), F32), _sds((s, D_KV), F32), _sds((1, 128), F32)],
        scratch_shapes=[pltpu.VMEM((BLOCK, D), F32), pltpu.VMEM((2, 1, BLOCK, D), BF16), pltpu.VMEM((2 * BLOCK, D_KV), F32),
                        pltpu.VMEM((2 * BLOCK, D_KV), F32), pltpu.SemaphoreType.DMA((2, 1))],
        input_output_aliases={n_in: 0}, compiler_params=_params("arbitrary"),
    )(do, proj, proj, proj, proj, proj, tab, tab, sinks, dproj, *after)


HALF = HEAD_DIM
N_CHUNK = D // 128


def _swa_bias(n):
    qi = lax.broadcasted_iota(jnp.int32, (BLOCK, 2 * BLOCK), 0)
    kj = lax.broadcasted_iota(jnp.int32, (BLOCK, 2 * BLOCK), 1)
    rel = qi + BLOCK - kj
    valid = (rel >= 0) & (rel < BLOCK) & ((kj >= BLOCK) | (n > 0))
    return jnp.where(valid, 0.0, NEG_INF)


def _halves(x):
    lo = lax.broadcasted_iota(jnp.int32, x.shape, 1) < HALF
    return jnp.where(lo, x, 0.0).astype(BF16), jnp.where(lo, 0.0, x).astype(BF16)


def _dup_heads(x):
    out = []
    for pair in range(N_KV // 2):
        xc = x[:, pair * 128:(pair + 1) * 128]
        xr = pltpu.roll(xc, HALF, 1)
        lo = lax.broadcasted_iota(jnp.int32, xc.shape, 1) < HALF
        out += [jnp.where(lo, xc, xr), jnp.where(lo, xr, xc)]
    return out


def _swa_load(q_ref, kc_ref, kp_ref, vc_ref, vp_ref, tc_ref, tp_ref):
    qf = _rope(q_ref[...].astype(F32), tc_ref[...]) * ATTN_SCALE
    q_halves = [_halves(qf[:, c * 128:(c + 1) * 128]) for c in range(N_CHUNK)]
    kf = jnp.concatenate([_rope(kp_ref[...].astype(F32), tp_ref[...]), _rope(kc_ref[...].astype(F32), tc_ref[...])], axis=0)
    vf = jnp.concatenate([vp_ref[...], vc_ref[...]], axis=0).astype(F32)
    return q_halves, _dup_heads(kf), _dup_heads(vf)


def _swa_probs(qh, kk, bias, sink):
    s = lax.dot_general(qh, kk, NT, preferred_element_type=F32) + bias
    m = jnp.maximum(jnp.max(jnp.maximum(s[:, :BLOCK], s[:, BLOCK:]), axis=1, keepdims=True), sink)
    return jnp.exp(s - m), m


def _swa_fwd(proj, tab, sinks, name, after=()):
    s = proj.shape[0]

    def body(q_ref, kc_ref, kp_ref, vc_ref, vp_ref, tc_ref, tp_ref, sink_ref, *rest):
        o_ref = rest[-1]
        n = pl.program_id(0)
        q_halves, kdup, vdup = _swa_load(q_ref, kc_ref, kp_ref, vc_ref, vp_ref, tc_ref, tp_ref)
        bias = _swa_bias(n)
        ones = jnp.ones((2 * BLOCK, 128), BF16)
        for c in range(N_CHUNK):
            hk = c // (GROUP // 2)
            kk = kdup[hk].astype(BF16)
            acc = None
            for half, v_half in enumerate(_halves(vdup[hk])):
                sink = sink_ref[0, 2 * c + half]
                e, m = _swa_probs(q_halves[c][half], kk, bias, sink)
                o = lax.dot_general(e.astype(BF16), jnp.concatenate([v_half, ones], axis=1), NN, preferred_element_type=F32)
                part = o[:, :128] * (1.0 / (o[:, 128:] + jnp.exp(sink - m)))
                acc = part if acc is None else acc + part
            o_ref[:, c * 128:(c + 1) * 128] = acc.astype(BF16)

    return _call(
        body, name=name, grid=(s // BLOCK,), in_specs=_attn_in_specs() + [HBM_SPEC] * len(after),
        out_specs=pl.BlockSpec((BLOCK, D), lambda n: (n, 0)), out_shape=_sds((s, D), BF16),
        compiler_params=_params("parallel"),
    )(proj, proj, proj, proj, proj, tab, tab, sinks, *after)


def _kv_bwd(dkr, dv, tab, dproj, name):
    s = dkr.shape[0]
    tm = _row_tile(s)

    def body(dk_ref, dv_ref, t_ref, dproj_in, o_ref):
        del dproj_in
        o_ref[:, 0:D_KV] = _rope_t(dk_ref[...], t_ref[...]).astype(BF16)
        o_ref[:, D_KV:2 * D_KV] = dv_ref[...].astype(BF16)

    row = lambda w: pl.BlockSpec((tm, w), lambda i: (i, 0))
    return _call(
        body, name=name, grid=(s // tm,),
        in_specs=[row(D_KV), row(D_KV), row(384), pl.BlockSpec(memory_space=pl.ANY)],
        out_specs=pl.BlockSpec((tm, 2 * D_KV), lambda i: (i, C_K // (2 * D_KV))),
        out_shape=_sds((s, N_IN), BF16), input_output_aliases={3: 0}, compiler_params=_params("parallel"),
    )(dkr, dv, tab, dproj)


EW_TC = 512


def _sigmoid(x):
    return 0.5 * jnp.tanh(0.5 * x) + 0.5


def _merge_fwd(proj, conv_out, attn_out, name):
    s = proj.shape[0]
    tm = _row_tile(s)
    tile = pl.BlockSpec((tm, EW_TC), lambda i, j: (i, j))

    def body(gc_ref, ga_ref, co_ref, ao_ref, o_ref):
        o_ref[...] = (_sigmoid(gc_ref[...].astype(F32)) * co_ref[...].astype(F32)
                      + _sigmoid(ga_ref[...].astype(F32)) * ao_ref[...].astype(F32)).astype(BF16)

    return _call(
        body, name=name, grid=(s // tm, D // EW_TC),
        in_specs=[pl.BlockSpec((tm, EW_TC), lambda i, j: (i, C_GC // EW_TC + j)),
                  pl.BlockSpec((tm, EW_TC), lambda i, j: (i, C_GA // EW_TC + j)), tile, tile],
        out_specs=tile, out_shape=_sds((s, D), BF16), compiler_params=_params("parallel", "parallel"),
    )(proj, proj, conv_out, attn_out)


def _merge_bwd(dmerged, proj, conv_out, attn_out, name):
    s = proj.shape[0]
    tm = _row_tile(s)
    tile = pl.BlockSpec((tm, EW_TC), lambda i, j: (i, j))
    anyspec = pl.BlockSpec(memory_space=pl.ANY)

    def body(dm_ref, gc_ref, ga_ref, co_ref, ao_ref, dproj_ref, dco_ref, dao_ref, buf, sems):
        i, j = pl.program_id(0), pl.program_id(1)
        dm = dm_ref[...].astype(F32)
        sc = _sigmoid(gc_ref[...].astype(F32))
        sa = _sigmoid(ga_ref[...].astype(F32))
        dco_ref[...] = (dm * sc).astype(BF16)
        dao_ref[...] = (dm * sa).astype(BF16)
        tiles = ((dm * co_ref[...].astype(F32) * sc * (1.0 - sc)).astype(BF16),
                 (dm * ao_ref[...].astype(F32) * sa * (1.0 - sa)).astype(BF16))

        def window(p, at):
            start = pl.multiple_of((C_GC, C_GA)[p] + at[1] * EW_TC, EW_TC)
            return dproj_ref.at[pl.ds(pl.multiple_of(at[0] * tm, tm), tm), pl.ds(start, EW_TC)]

        _write_behind(i * nj + j, (s // tm) * nj, buf, sems, tiles, window, (i, j))

    nj = D // EW_TC
    return _call(
        body, name=name, grid=(s // tm, nj),
        in_specs=[tile, pl.BlockSpec((tm, EW_TC), lambda i, j: (i, C_GC // EW_TC + j)),
                  pl.BlockSpec((tm, EW_TC), lambda i, j: (i, C_GA // EW_TC + j)), tile, tile],
        out_specs=[anyspec, tile, tile],
        out_shape=[_sds((s, N_IN), BF16), _sds((s, D), BF16), _sds((s, D), BF16)],
        scratch_shapes=[pltpu.VMEM((2, 2, tm, EW_TC), BF16), pltpu.SemaphoreType.DMA((2, 2))],
        compiler_params=_params("arbitrary", "arbitrary"),
    )(dmerged, proj, proj, conv_out, attn_out)


FF_TC = 256


def _gate_up_fwd(h2, wgu_t, name):
    s = h2.shape[0]
    tm = min(2048, s)
    nb = D_FF // FF_TC

    def body(h_ref, wg_ref, wu_ref, g_ref, u_ref, a_ref):
        h = h_ref[...]
        g = lax.dot_general(h, wg_ref[...], NT, preferred_element_type=F32)
        u = lax.dot_general(h, wu_ref[...], NT, preferred_element_type=F32)
        g_ref[...] = g.astype(BF16)
        u_ref[...] = u.astype(BF16)
        a_ref[...] = (g * _sigmoid(g) * u).astype(BF16)

    tile = pl.BlockSpec((tm, FF_TC), lambda i, j: (i, j))
    return _call(
        body, name=name, grid=(s // tm, nb),
        in_specs=[pl.BlockSpec((tm, D), lambda i, j: (i, 0)), pl.BlockSpec((FF_TC, D), lambda i, j: (j, 0)),
                  pl.BlockSpec((FF_TC, D), lambda i, j: (nb + j, 0))],
        out_specs=[tile, tile, tile], out_shape=[_sds((s, D_FF), BF16)] * 3,
        compiler_params=_params("parallel", "parallel"),
    )(h2, wgu_t, wgu_t)


def _down_bwd_x(dx2b, wd, gate, up, name):
    s = dx2b.shape[0]
    tm = min(2048, s)
    nb = D_FF // FF_TC

    def body(dx_ref, w_ref, g_ref, u_ref, dg_ref, du_ref):
        da = lax.dot_general(dx_ref[...], w_ref[...], NT, preferred_element_type=F32)
        g = g_ref[...].astype(F32)
        sg = _sigmoid(g)
        dg_ref[...] = (da * u_ref[...].astype(F32) * (sg * (1.0 + g * (1.0 - sg)))).astype(BF16)
        du_ref[...] = (da * (g * sg)).astype(BF16)

    tile = pl.BlockSpec((tm, FF_TC), lambda i, j: (i, j))
    return _call(
        body, name=name, grid=(s // tm, nb),
        in_specs=[pl.BlockSpec((tm, D), lambda i, j: (i, 0)), pl.BlockSpec((FF_TC, D), lambda i, j: (j, 0)), tile, tile],
        out_specs=[tile, tile], out_shape=[_sds((s, D_FF), BF16)] * 2,
        compiler_params=_params("parallel", "parallel"),
    )(dx2b, wd, gate, up)


class _Weights:
    def __init__(self, **groups):
        self.groups = groups

    def begin(self, group, after):
        return ()

    def end(self, group, after):
        return self.groups[group]


class _NoReduce:
    def start(self, group, grads):
        return ()

    def middle(self, group, after):
        return ()


def _local_step(x, tgt, g_mix, g_ffn, g_final, sinks, weights, reducer=None, after=()):
    reducer = reducer or _NoReduce()
    s = x.shape[0]
    tab = _rope_tables(s)
    big = dict(tm=1024, tn=512, tk=1024)
    h1 = _rms_fwd(x, g_mix, "rms1_fwd", after=after)
    win_t, conv_w = weights.end("in", weights.begin("in", (h1,)))
    proj = _matmul(h1, win_t, mode="nt", out_dtype=BF16, name="proj_fwd", tm=2048, tn=512, tk=1024)
    attn = _swa_fwd(proj, tab, sinks, "attn_fwd", after=weights.begin("mix", (proj,)))
    wco, wao, wo = weights.end("mix", (attn,))
    conv_y = _conv_fwd(proj, conv_w, "conv_fwd")
    conv_out = _matmul(conv_y, wco, mode="nn", out_dtype=BF16, name="conv_out_fwd", **big)
    attn_out = _matmul(attn, wao, mode="nn", out_dtype=BF16, name="attn_out_fwd", **big)
    merged = _merge_fwd(proj, conv_out, attn_out, "merge_fwd")
    x1 = _matmul(merged, wo, mode="nn", out_dtype=F32, name="wo_fwd", res=x, after=weights.begin("ffn", (merged,)), **big)
    h2 = _rms_fwd(x1, g_ffn, "rms2_fwd")
    wgu_t, wd = weights.end("ffn", (h2,))
    gate, up, act = _gate_up_fwd(h2, wgu_t, "gate_up_fwd")
    x2 = _matmul(act, wd, mode="nn", out_dtype=F32, name="down_fwd", res=x1, tm=1024, tn=512, tk=D_FF)
    dx2, dx2b, dg_final, lossvec = _loss_head(x2, g_final, tgt, "loss_head")
    dgate, dup = _down_bwd_x(dx2b, wd, gate, up, "down_bwd_x")
    g_wd = _matmul(act, dx2b, mode="tn", out_dtype=BF16, name="down_bwd_w", tm=1408, tn=1024, tk=2048)
    dh2 = _matmul([dgate, dup], wgu_t, mode="nn", out_dtype=F32, name="gate_up_bwd_x", tm=1024, tn=1024, tk=1408)
    g_wgu_t = _matmul([dgate, dup], h2, mode="tn", out_dtype=BF16, name="gate_up_bwd_w", tm=1408, tn=1024, tk=2048)
    after_ffn = reducer.start("ffn", dict(wgu_t=g_wgu_t, wd=g_wd))
    dx1, dx1b, dg_ffn = _rms_bwd(dh2, x1, g_ffn, dx2, "rms2_bwd")
    dmerged = _matmul(dx1b, wo, mode="nt", out_dtype=BF16, name="wo_bwd_x", after=after_ffn, **big)
    after_ffn = reducer.middle("ffn", (dmerged,))
    g_wo = _matmul(merged, dx1b, mode="tn", out_dtype=BF16, name="wo_bwd_w", tm=512, tn=1024, tk=2048, after=after_ffn)
    dproj, dco, dao = _merge_bwd(dmerged, proj, conv_out, attn_out, "merge_bwd")
    dconv_y = _matmul(dco, wco, mode="nt", out_dtype=BF16, name="conv_out_bwd_x", **big)
    g_wco = _matmul(conv_y, dco, mode="tn", out_dtype=BF16, name="conv_out_bwd_w", tm=512, tn=1024, tk=2048)
    dattn = _matmul(dao, wao, mode="nt", out_dtype=BF16, name="attn_out_bwd_x", **big)
    g_wao = _matmul(attn, dao, mode="tn", out_dtype=BF16, name="attn_out_bwd_w", tm=512, tn=1024, tk=2048)
    after_mix = reducer.start("mix", dict(wco=g_wco, wao=g_wao, wo=g_wo))
    dproj, dconv_w = _conv_bwd(dconv_y, proj, conv_w, dproj, "conv_bwd", after=after_mix)
    after_mix = reducer.middle("mix", (dconv_w,))
    dproj, dkr, dv, dsinks = _attn_bwd(dattn, proj, tab, sinks, dproj, "attn_bwd", after=after_mix)
    dproj = _kv_bwd(dkr, dv, tab, dproj, "kv_bwd")
    g_win_t = _matmul(dproj, h1, mode="tn", out_dtype=BF16, name="proj_bwd_w", tm=512, tn=1024, tk=2048)
    after_in = reducer.middle("in", reducer.start("in", dict(win_t=g_win_t)))
    dh1 = _matmul(dproj, win_t, mode="nn", out_dtype=F32, name="proj_bwd_x", tm=1024, tn=1024, tk=1664, after=after_in)
    dx, _, dg_mix = _rms_bwd(dh1, x, g_mix, dx1, "rms1_bwd")
    grads = dict(win_t=g_win_t, wgu_t=g_wgu_t, wd=g_wd, wco=g_wco, wao=g_wao, wo=g_wo)
    small = dict(g_mix=dg_mix, g_ffn=dg_ffn, g_final=dg_final, conv_w=dconv_w, sinks=dsinks, lossvec=lossvec)
    return dx, grads, small


def _position():
    return lax.axis_index("x"), lax.axis_index("y"), lax.axis_index("c")


def _other_chips(x, y):
    return [(1 - x, y), (x, 1 - y), (1 - x, 1 - y)]


SEM_SPEC = pl.BlockSpec(memory_space=pltpu.SEMAPHORE)
EFFECT = pltpu.SideEffectType.DATAFLOW_SIDE_EFFECTING
TOKEN = jax.ShapeDtypeStruct((8, 128), F32)
TOKEN_SPEC = pl.BlockSpec(memory_space=pltpu.VMEM)


def _hbm(a):
    return pltpu.with_memory_space_constraint(a, pltpu.HBM)


def _place(w, me_idx, dtype, name, after=()):
    r, cdim = w.shape

    def body(i_ref, w_ref, *rest):
        rest[-1][...] = w_ref[...].astype(dtype)

    grid_spec = pltpu.PrefetchScalarGridSpec(
        num_scalar_prefetch=1, grid=(1,), in_specs=[pl.BlockSpec((r, cdim), lambda i, me: (0, 0))] + [HBM_SPEC] * len(after),
        out_specs=pl.BlockSpec((r, cdim), lambda i, me: (me[0], 0)))
    return _call(body, name=name, grid_spec=grid_spec, out_shape=_sds((N_DEV * r, cdim), dtype),
                 compiler_params=_params("arbitrary"))(me_idx, w, *after)


def _own_rows(ref, r, px, py, pc):
    return ref.at[pl.ds((4 * px + 2 * py + pc) * r, r), :]


def _gather_start(bufs, groups, name):
    n = len(bufs)
    rows = [b.shape[0] // N_DEV for b in bufs]
    ng = len(groups)

    def body(*refs):
        ins = refs[:n]
        sems = refs[n:n + 2 * ng]
        token = refs[-1]
        x, y, c = _position()
        targets = [(x, y, 1 - c)] + [(*chip, c) for chip in _other_chips(x, y)]
        for g, members in enumerate(groups):
            for slot, a in enumerate(members):
                own = _own_rows(ins[a], rows[a], x, y, c)
                for to in targets:
                    pltpu.make_async_remote_copy(src_ref=own, dst_ref=own, send_sem=sems[2 * g].at[slot],
                                                 recv_sem=sems[2 * g + 1].at[slot], device_id=to, device_id_type=MESH).start()
        token[...] = jnp.zeros_like(token)

    sem_shapes = []
    for members in groups:
        sem_shapes += [pltpu.SemaphoreType.DMA((len(members),))] * 2
    outs = _call(
        body, name=name, in_specs=[HBM_SPEC] * n, out_specs=[SEM_SPEC] * (2 * ng) + [HBM_SPEC] * n + [TOKEN_SPEC],
        out_shape=sem_shapes + [pltpu.HBM(b.shape, b.dtype) for b in bufs] + [TOKEN],
        input_output_aliases={i: 2 * ng + i for i in range(n)},
        compiler_params=pltpu.CompilerParams(has_side_effects=EFFECT),
    )(*[_hbm(b) for b in bufs])
    sem_pairs = [(outs[2 * g], outs[2 * g + 1]) for g in range(ng)]
    return sem_pairs, list(outs[2 * ng:2 * ng + n]), outs[-1]


def _gather_forward(send_sems, recv_sems, bufs, after, name):
    n = len(bufs)
    rows = [b.shape[0] // N_DEV for b in bufs]

    def body(*refs):
        ins = refs[:n]
        send1, recv1 = refs[n], refs[n + 1]
        out0 = n + 2 + len(after)
        send2, recv2 = refs[out0], refs[out0 + 1]
        token = refs[-1]
        x, y, c = _position()
        for a in range(n):
            step1 = pltpu.make_async_remote_copy(
                src_ref=_whole(ins[a], 4 * rows[a]), dst_ref=_whole(ins[a], 4 * rows[a]), send_sem=send1.at[a],
                recv_sem=recv1.at[a], device_id=(x, y, c), device_id_type=MESH)
            step1.wait_send()
            step1.wait_recv()
        for a in range(n):
            for chip in _other_chips(x, y):
                blk = _own_rows(ins[a], rows[a], *chip, c)
                pltpu.make_async_remote_copy(src_ref=blk, dst_ref=blk, send_sem=send2.at[a], recv_sem=recv2.at[a],
                                             device_id=(x, y, 1 - c), device_id_type=MESH).start()
        token[...] = jnp.zeros_like(token)

    outs = _call(
        body, name=name, in_specs=[HBM_SPEC] * n + [SEM_SPEC, SEM_SPEC] + [HBM_SPEC] * len(after),
        out_specs=[SEM_SPEC, SEM_SPEC] + [HBM_SPEC] * n + [TOKEN_SPEC],
        out_shape=[pltpu.SemaphoreType.DMA((n,)), pltpu.SemaphoreType.DMA((n,))]
        + [pltpu.HBM(b.shape, b.dtype) for b in bufs] + [TOKEN],
        input_output_aliases={i: 2 + i for i in range(n)},
        compiler_params=pltpu.CompilerParams(has_side_effects=EFFECT),
    )(*bufs, send_sems, recv_sems, *after)
    return outs[0], outs[1], list(outs[2:2 + n]), outs[-1]


def _gather_done(send_sems, recv_sems, bufs, after, name):
    n = len(bufs)
    rows = [b.shape[0] // N_DEV for b in bufs]

    def body(*refs):
        ins = refs[:n]
        send2, recv2 = refs[n], refs[n + 1]
        x, y, c = _position()
        for a in range(n):
            step2 = pltpu.make_async_remote_copy(
                src_ref=_whole(ins[a], 3 * rows[a]), dst_ref=_whole(ins[a], 3 * rows[a]), send_sem=send2.at[a],
                recv_sem=recv2.at[a], device_id=(x, y, c), device_id_type=MESH)
            step2.wait_send()
            step2.wait_recv()

    outs = _call(
        body, name=name, in_specs=[HBM_SPEC] * n + [SEM_SPEC, SEM_SPEC] + [HBM_SPEC] * len(after),
        out_specs=[HBM_SPEC] * n, out_shape=[pltpu.HBM(b.shape, b.dtype) for b in bufs],
        input_output_aliases={i: i for i in range(n)},
        compiler_params=pltpu.CompilerParams(has_side_effects=EFFECT),
    )(*bufs, send_sems, recv_sems, *after)
    return list(outs)


def _whole(ref, nrows):
    return ref.at[pl.ds(0, nrows), :]


def _to_sibling(x, y, c):
    return [(2 * q + (1 - c), q, (x, y, 1 - c)) for q in range(4)]


def _to_chips(x, y, c):
    return [(2 * px + py, j, (px, py, c)) for j, (px, py) in enumerate(_other_chips(x, y))]


def _exchange_start(srcs, src_slots, plan, name):
    n = len(srcs)
    rows = [a.shape[0] // src_slots for a in srcs]
    n_copies = len(plan(0, 0, 0))
    lands = [lax.empty((n_copies * r, a.shape[1]), a.dtype) for a, r in zip(srcs, rows)]

    def body(*refs):
        ins, land_refs = refs[:n], refs[n:2 * n]
        send_sems, recv_sems = refs[2 * n], refs[2 * n + 1]
        token = refs[-1]
        for a in range(n):
            r = rows[a]
            for src_slot, dst_slot, target in plan(*_position()):
                pltpu.make_async_remote_copy(
                    src_ref=ins[a].at[pl.ds(src_slot * r, r), :], dst_ref=land_refs[a].at[pl.ds(dst_slot * r, r), :],
                    send_sem=send_sems.at[a], recv_sem=recv_sems.at[a], device_id=target, device_id_type=MESH).start()
        token[...] = jnp.zeros_like(token)

    outs = _call(
        body, name=name, in_specs=[HBM_SPEC] * (2 * n),
        out_specs=[SEM_SPEC, SEM_SPEC] + [HBM_SPEC] * (2 * n) + [TOKEN_SPEC],
        out_shape=[pltpu.SemaphoreType.DMA((n,)), pltpu.SemaphoreType.DMA((n,))]
        + [pltpu.HBM(a.shape, a.dtype) for a in srcs] + [pltpu.HBM(l.shape, l.dtype) for l in lands] + [TOKEN],
        input_output_aliases={i: 2 + i for i in range(2 * n)},
        compiler_params=pltpu.CompilerParams(has_side_effects=EFFECT),
    )(*[_hbm(a) for a in srcs], *[_hbm(l) for l in lands])
    return outs[0], outs[1], list(outs[2:2 + n]), list(outs[2 + n:2 + 2 * n]), outs[-1]


def _exchange_wait(send_sems, recv_sems, srcs, lands, after, name):
    n = len(srcs)

    def body(*refs):
        ins, land_refs = refs[:n], refs[n:2 * n]
        send_sems_ref, recv_sems_ref = refs[2 * n], refs[2 * n + 1]
        for a in range(n):
            allrows = lands[a].shape[0]
            cp = pltpu.make_async_remote_copy(
                src_ref=_whole(ins[a], allrows), dst_ref=_whole(land_refs[a], allrows), send_sem=send_sems_ref.at[a],
                recv_sem=recv_sems_ref.at[a], device_id=_position(), device_id_type=MESH)
            cp.wait_send()
            cp.wait_recv()

    outs = _call(
        body, name=name, in_specs=[HBM_SPEC] * (2 * n) + [SEM_SPEC, SEM_SPEC] + [HBM_SPEC] * len(after),
        out_specs=[HBM_SPEC] * (2 * n),
        out_shape=[pltpu.HBM(a.shape, a.dtype) for a in srcs] + [pltpu.HBM(l.shape, l.dtype) for l in lands],
        input_output_aliases={i: i for i in range(2 * n)},
        compiler_params=pltpu.CompilerParams(has_side_effects=EFFECT),
    )(*srcs, *lands, send_sems, recv_sems, *after)
    return list(outs[:n]), list(outs[n:])


def _chip_partial(grad, recv, idx, name):
    r = recv.shape[0] // 4

    def body(i_ref, g_ref, s_ref, o_ref):
        del i_ref
        o_ref[...] = (g_ref[...].astype(F32) + s_ref[...].astype(F32)).astype(BF16)

    grid_spec = pltpu.PrefetchScalarGridSpec(
        num_scalar_prefetch=1, grid=(3,),
        in_specs=[pl.BlockSpec((r, D), lambda t, i_ref: (2 * i_ref[1 + t] + i_ref[0], 0)),
                  pl.BlockSpec((r, D), lambda t, i_ref: (i_ref[1 + t], 0))],
        out_specs=pl.BlockSpec((r, D), lambda t, i_ref: (i_ref[1 + t], 0)))
    return _call(body, name=name, grid_spec=grid_spec, out_shape=_sds((4 * r, D), BF16),
                 compiler_params=_params("arbitrary"))(idx, grad, recv)


def _adamw_math(w, g, m, v):
    m2 = B1 * m + (1.0 - B1) * g
    v2 = B2 * v + (1.0 - B2) * jnp.square(g)
    m_hat = m2 / (1.0 - B1 ** STEP)
    v_hat = v2 / (1.0 - B2 ** STEP)
    return -LR * (m_hat / (jnp.sqrt(v_hat) + EPS_ADAM) + WD * w), m2, v2


def _reduce_adamw(w, grad, from_sibling, from_chips, idx, m, v, name):
    r = w.shape[0]
    assert grad.shape == (N_DEV * r, D) and from_sibling.shape == (4 * r, D) and from_chips.shape == (3 * r, D)
    tr = r // 2
    nb = r // tr

    def body(i_ref, w_ref, p_ref, s_ref, r0_ref, r1_ref, r2_ref, m_ref, v_ref, g_ref, d_ref, nm_ref, nv_ref):
        del i_ref
        g = p_ref[...].astype(F32) + s_ref[...].astype(F32)
        g = ((g + r0_ref[...].astype(F32)) + r1_ref[...].astype(F32)) + r2_ref[...].astype(F32)
        g_ref[...] = g
        d_ref[...], nm_ref[...], nv_ref[...] = _adamw_math(w_ref[...], g, m_ref[...], v_ref[...])

    own = pl.BlockSpec((tr, D), lambda i, i_ref: (i, 0))
    grid_spec = pltpu.PrefetchScalarGridSpec(
        num_scalar_prefetch=1, grid=(nb,),
        in_specs=[own, pl.BlockSpec((tr, D), lambda i, i_ref: (i_ref[0] * nb + i, 0)),
                  pl.BlockSpec((tr, D), lambda i, i_ref: (i_ref[1] * nb + i, 0))]
        + [pl.BlockSpec((tr, D), lambda i, i_ref, j=j: (j * nb + i, 0)) for j in range(3)] + [own, own],
        out_specs=[own] * 4)
    return _call(body, name=name, grid_spec=grid_spec, out_shape=[_sds((r, D), F32)] * 4,
                 compiler_params=_params("parallel"))(idx, w, grad, from_sibling, from_chips, from_chips, from_chips, m, v)


SMALL_ROWS = 8


def _small_all_reduce(pack, name, after=()):
    def body(p_ref, *rest):
        tot_ref, loss_ref, gath, send_sems, recv_sems = rest[len(after):]
        x, y, c = _position()
        me_id = 4 * x + 2 * y + c
        gath[me_id] = p_ref[...]
        copies = []
        for k in range(1, N_DEV):
            peer = tuple(1 - v if (k >> b) & 1 else v for v, b in ((x, 2), (y, 1), (c, 0)))
            cp = pltpu.make_async_remote_copy(src_ref=p_ref, dst_ref=gath.at[me_id], send_sem=send_sems.at[k - 1],
                                              recv_sem=recv_sems.at[k - 1], device_id=peer, device_id_type=MESH)
            cp.start()
            copies.append(cp)
        for cp in copies:
            cp.wait_recv()
        for cp in copies:
            cp.wait_send()
        tot = gath[0]
        for d in range(1, N_DEV):
            tot = tot + gath[d]
        tot_ref[...] = tot
        loss_ref[...] = jnp.full((1, 128), (0.5 / D) * jnp.sum(tot[SMALL_ROWS - 1:SMALL_ROWS, :]), F32)

    vm = pl.BlockSpec(memory_space=pltpu.VMEM)
    return _call(
        body, name=name, in_specs=[vm] + [HBM_SPEC] * len(after), out_specs=[vm, vm],
        out_shape=[_sds((SMALL_ROWS, D), F32), _sds((1, 128), F32)],
        scratch_shapes=[pltpu.VMEM((N_DEV, SMALL_ROWS, D), F32), pltpu.SemaphoreType.DMA((N_DEV - 1,)),
                        pltpu.SemaphoreType.DMA((N_DEV - 1,))],
    )(pack, *after)


def _adamw(w, g, m, v, name):
    r, cdim = w.shape
    tr = 256 if r % 256 == 0 else (r // 2 if r % 16 == 0 else r)

    def body(w_ref, g_ref, m_ref, v_ref, d_ref, nm_ref, nv_ref):
        d_ref[...], nm_ref[...], nv_ref[...] = _adamw_math(w_ref[...], g_ref[...], m_ref[...], v_ref[...])

    spec = pl.BlockSpec((tr, cdim), lambda i: (i, 0))
    return _call(
        body, name=name, grid=(r // tr,), in_specs=[spec] * 4, out_specs=[spec] * 3,
        out_shape=[_sds((r, cdim), F32)] * 3, compiler_params=_params("parallel"),
    )(w, g, m, v)


def kernel(x, g_mix, w_in, conv_w, attn_sinks, w_conv_out, w_attn_out, w_o, g_ffn, w_gate_up, w_down, g_final, loss_target, m_g_mix, m_w_in, m_conv_w, m_attn_sinks, m_w_conv_out, m_w_attn_out, m_w_o, m_g_ffn, m_w_gate_up, m_w_down, m_g_final, v_g_mix, v_w_in, v_conv_w, v_attn_sinks, v_w_conv_out, v_w_attn_out, v_w_o, v_g_ffn, v_w_gate_up, v_w_down, v_g_final):
    cx, cy, cc = _position()
    chip = 2 * cx + cy
    partial_idx = jnp.stack([cc, 2 * (1 - cx) + cy, 2 * cx + (1 - cy), 2 * (1 - cx) + (1 - cy)]).astype(jnp.int32)
    own_idx = jnp.stack([2 * chip + cc, chip]).astype(jnp.int32)
    me = 4 * cx + 2 * cy + cc

    me_idx = jnp.reshape(me, (1,)).astype(jnp.int32)
    first = [_place(jnp.transpose(w_in[0]), me_idx, BF16, "place_w_in"),
             _place(jnp.pad(conv_w[0], ((0, 5), (0, 0))), me_idx, F32, "place_conv_w")]
    (sems_in,), first, token_in = _gather_start(first, [[0, 1]], "gather_start_in")
    later = [_place(w, me_idx, BF16, "place_" + k, after=(token_in,)) for k, w in (
        ("w_conv_out", w_conv_out[0]), ("w_attn_out", w_attn_out[0]), ("w_o", w_o[0]),
        ("w_gate_up", jnp.transpose(w_gate_up[0])), ("w_down", w_down[0]))]
    (sems_mix, sems_ffn), later, token_later = _gather_start(later, [[0, 1, 2], [3, 4]], "gather_start_later")
    gather_tokens = (token_in, token_later)

    class Gathered:
        def __init__(self):
            self.state = {"in": (sems_in, first), "mix": (sems_mix, later[:3]), "ffn": (sems_ffn, later[3:])}

        def begin(self, group, after):
            (send_sems, recv_sems), group_bufs = self.state[group]
            send2, recv2, group_bufs, token = _gather_forward(send_sems, recv_sems, group_bufs, after, "gather_forward_" + group)
            self.state[group] = ((send2, recv2), group_bufs)
            return (token,)

        def end(self, group, after):
            (send2, recv2), group_bufs = self.state[group]
            full = _gather_done(send2, recv2, group_bufs, after, "gather_done_" + group)
            if group == "in":
                return full[0], jnp.transpose(full[1].reshape(N_DEV, 8, 128)[:, :3, :], (1, 0, 2)).reshape(3, D)
            return full

    in_flight, own_pieces = {}, {}

    class Reducer:
        def start(self, group, gdict):
            keys, glist = list(gdict), list(gdict.values())
            send_sems, recv_sems, glist, lands, token = _exchange_start(glist, N_DEV, _to_sibling, "rs_sibling_start_" + group)
            in_flight[group] = (keys, send_sems, recv_sems, glist, lands)
            return (token,)

        def middle(self, group, after):
            keys, send_sems, recv_sems, glist, lands = in_flight[group]
            glist, lands = _exchange_wait(send_sems, recv_sems, glist, lands, after, "rs_sibling_wait_" + group)
            parts = [_chip_partial(g, r, partial_idx, "chip_partial_" + k) for k, g, r in zip(keys, glist, lands)]
            send_sems, recv_sems, parts, from_chips, token = _exchange_start(parts, 4, _to_chips, "rs_chips_start_" + group)
            in_flight[group] = (keys, send_sems, recv_sems, parts, from_chips)
            own_pieces[group] = (glist, lands)
            return (token,)

    dx, _, small = _local_step(x[0], loss_target[0], g_mix, g_ffn, g_final[None], attn_sinks, Gathered(),
                               reducer=Reducer(), after=gather_tokens)

    transposed = ("w_in", "w_gate_up")

    def as2d(k, a):
        if k in transposed:
            return jnp.transpose(a[0])
        return a[None] if a.ndim == 1 else (a[0] if a.ndim == 3 else a)

    w_all = {"g_mix": g_mix, "w_in": w_in, "conv_w": conv_w, "attn_sinks": attn_sinks, "w_conv_out": w_conv_out,
             "w_attn_out": w_attn_out, "w_o": w_o, "g_ffn": g_ffn, "w_gate_up": w_gate_up, "w_down": w_down, "g_final": g_final}
    m_all = {"g_mix": m_g_mix, "w_in": m_w_in, "conv_w": m_conv_w, "attn_sinks": m_attn_sinks, "w_conv_out": m_w_conv_out,
             "w_attn_out": m_w_attn_out, "w_o": m_w_o, "g_ffn": m_g_ffn, "w_gate_up": m_w_gate_up, "w_down": m_w_down,
             "g_final": m_g_final}
    v_all = {"g_mix": v_g_mix, "w_in": v_w_in, "conv_w": v_conv_w, "attn_sinks": v_attn_sinks, "w_conv_out": v_w_conv_out,
             "w_attn_out": v_w_attn_out, "w_o": v_w_o, "g_ffn": v_g_ffn, "w_gate_up": v_w_gate_up, "w_down": v_w_down,
             "g_final": v_g_final}
    results = {}

    def update(k, g=None, pieces=None):
        w2, m2, v2 = as2d(k, w_all[k]), as2d(k, m_all[k]), as2d(k, v_all[k])
        if g is None:
            g, d, nm, nv = _reduce_adamw(w2, *pieces, own_idx, m2, v2, "adamw_" + k)
        else:
            d, nm, nv = _adamw(w2, g, m2, v2, "adamw_" + k)
        results[k] = [(jnp.transpose(val) if k in transposed else val).reshape(w_all[k].shape) for val in (g, d, nm, nv)]
        return nm

    kernel_name = {"win_t": "w_in", "wgu_t": "w_gate_up", "wd": "w_down", "wco": "w_conv_out", "wao": "w_attn_out", "wo": "w_o"}

    def finish(group, after):
        keys, send_sems, recv_sems, parts, from_chips = in_flight[group]
        _, from_chips = _exchange_wait(send_sems, recv_sems, parts, from_chips, after, "rs_chips_wait_" + group)
        grads, from_sibling = own_pieces[group]
        return tuple(update(kernel_name[k], pieces=p) for k, *p in zip(keys, grads, from_sibling, from_chips))

    after = finish("mix", finish("ffn", (dx,)))

    sinks_row = jnp.pad(small["sinks"], ((0, 0), (0, D - 128)))
    pack = jnp.concatenate([small["g_mix"], small["g_ffn"], small["g_final"], small["conv_w"], sinks_row, small["lossvec"]], axis=0)
    tot, loss_row = _small_all_reduce(pack, "small_all_reduce", after=after)
    loss = loss_row[0, 0]
    g_small = {
        "g_mix": tot[0:1], "g_ffn": tot[1:2], "g_final": tot[2:3],
        "conv_w": lax.dynamic_slice(tot, (3, me * 128), (3, 128)), "attn_sinks": tot[6:7, :N_HEADS],
    }
    finish("in", tuple(update(k, g) for k, g in g_small.items()))

    order = ["g_mix", "w_in", "conv_w", "attn_sinks", "w_conv_out", "w_attn_out", "w_o", "g_ffn", "w_gate_up", "w_down", "g_final"]
    return (loss, dx[None], *[results[k][i] for i in range(4) for k in order])
```

```python
import functools
import math

import jax
import jax.numpy as jnp
from jax import lax
from jax.experimental import pallas as pl
from jax.experimental.pallas import tpu as pltpu

F32 = jnp.float32
BF16 = jnp.bfloat16

D = 1024
HEAD_DIM = 64
N_HEADS = 16
N_KV = 4
GROUP = N_HEADS // N_KV
D_KV = N_KV * HEAD_DIM
BLOCK = 128
ROT_DIM = HEAD_DIM // 4
ROPE_THETA = 500000.0
ATTN_SCALE = 1.0 / math.sqrt(HEAD_DIM)
NEG_INF = -1e30
D_FF = 2816
N_IN = 6656
EPS = 1e-5
C_CB, C_CC, C_CX, C_Q, C_K, C_V, C_GC, C_GA = 0, 1024, 2048, 3072, 4096, 4352, 4608, 5632

LR, B1, B2, EPS_ADAM, WD, STEP = 0.001, 0.9, 0.999, 1e-08, 0.01, 10

N_DEV = 8
MESH = pl.DeviceIdType.MESH
VMEM_LIMIT = 56 * 1024 * 1024

NN = (((1,), (0,)), ((), ()))
NT = (((1,), (1,)), ((), ()))
TN = (((0,), (0,)), ((), ()))
HBM_SPEC = pl.BlockSpec(memory_space=pl.ANY)


def _call(body, **kw):
    return pl.pallas_call(body, **kw)


def _params(*sem):
    return pltpu.CompilerParams(dimension_semantics=sem, vmem_limit_bytes=VMEM_LIMIT)


def _sds(shape, dtype):
    return jax.ShapeDtypeStruct(shape, dtype)


def _matmul(a, b, *, mode, tm, tn, tk, out_dtype, name, res=None, after=()):
    parts = list(a) if isinstance(a, (list, tuple)) else [a]
    rows_a = parts[0].shape[0]
    cols_a = sum(p.shape[1] for p in parts)
    if mode == "nn":
        (m, kk), (_, n), dims = (rows_a, cols_a), b.shape, NN
    elif mode == "nt":
        (m, kk), (n, _), dims = (rows_a, cols_a), b.shape, NT
    else:
        (kk, m), (_, n), dims = (rows_a, cols_a), b.shape, TN
    tm, tn, tk = min(tm, m), min(tn, n), min(tk, kk)
    assert m % tm == 0 and n % tn == 0 and kk % tk == 0, (name, m, n, kk, tm, tn, tk)
    nk = kk // tk
    split_axis, width = (2, tk) if mode == "nn" else (0, tm)
    assert len(parts) == 1 or mode in ("nn", "tn")
    assert len(parts) == 1 or all(p.shape[1] % width == 0 for p in parts), (name, width)
    counts = [p.shape[1] // width for p in parts]
    starts = [sum(counts[:p]) for p in range(len(parts))]

    def a_spec(p):
        def col(t):
            return jnp.clip(t - starts[p], 0, counts[p] - 1) if len(parts) > 1 else t

        if mode == "tn":
            return pl.BlockSpec((tk, tm), lambda i, j, k: (k, col(i)))
        return pl.BlockSpec((tm, tk), lambda i, j, k: (i, col(k)))

    if mode == "nt":
        b_spec = pl.BlockSpec((tn, tk), lambda i, j, k: (j, k))
    else:
        b_spec = pl.BlockSpec((tk, tn), lambda i, j, k: (k, j))
    o_spec = pl.BlockSpec((tm, tn), lambda i, j, k: (i, j))
    has_res = res is not None
    n_parts = len(parts)

    def body(*refs):
        a_refs, b_ref = refs[:n_parts], refs[n_parts]
        r_ref = refs[n_parts + 1] if has_res else None
        o_ref = refs[n_parts + 1 + has_res + len(after)]
        k = pl.program_id(2)

        def step(a_ref):
            part = lax.dot_general(a_ref[...], b_ref[...], dims, preferred_element_type=F32)

            def finish(acc):
                if has_res:
                    acc = acc + r_ref[...]
                o_ref[...] = acc.astype(o_ref.dtype)

            if nk == 1:
                finish(part)
            else:
                acc_ref = refs[-1]

                @pl.when(k == 0)
                def _():
                    acc_ref[...] = part

                @pl.when(k > 0)
                def _():
                    acc_ref[...] += part

                @pl.when(k == nk - 1)
                def _():
                    finish(acc_ref[...])

        if n_parts == 1:
            step(a_refs[0])
        else:
            t = pl.program_id(split_axis)
            for p in range(n_parts):
                pl.when((t >= starts[p]) & (t < starts[p] + counts[p]))(functools.partial(step, a_refs[p]))

    ins = parts + [b] + ([res] if has_res else []) + list(after)
    in_specs = [a_spec(p) for p in range(n_parts)] + [b_spec] + ([o_spec] if has_res else []) + [HBM_SPEC] * len(after)
    scratch = [] if nk == 1 else [pltpu.VMEM((tm, tn), F32)]
    return _call(
        body, name=name, grid=(m // tm, n // tn, nk), in_specs=in_specs, out_specs=o_spec,
        out_shape=_sds((m, n), out_dtype), scratch_shapes=scratch,
        compiler_params=_params("parallel", "parallel", "arbitrary"),
    )(*ins)


def _row_tile(s):
    return min(512, s)


def _rms_fwd(x, g, name, after=()):
    s = x.shape[0]
    tm = _row_tile(s)

    def body(x_ref, g_ref, *rest):
        h_ref = rest[-1]
        xv = x_ref[...]
        r = lax.rsqrt(jnp.mean(xv * xv, axis=-1, keepdims=True) + EPS)
        h_ref[...] = (xv * r * g_ref[...]).astype(BF16)

    row = pl.BlockSpec((tm, D), lambda i: (i, 0))
    return _call(
        body, name=name, grid=(s // tm,), in_specs=[row, pl.BlockSpec((1, D), lambda i: (0, 0))] + [HBM_SPEC] * len(after),
        out_specs=row, out_shape=_sds((s, D), BF16), compiler_params=_params("parallel"),
    )(x, g, *after)


def _rms_bwd(dh, x, g, dres, name, after=()):
    s = x.shape[0]
    tm = _row_tile(s)

    def body(dh_ref, x_ref, g_ref, dres_ref, *rest):
        dx_ref, dxb_ref, dg_ref = rest[len(after):]
        xv = x_ref[...]
        r = lax.rsqrt(jnp.mean(xv * xv, axis=-1, keepdims=True) + EPS)
        xh = xv * r
        dhv = dh_ref[...]
        dyg = dhv * g_ref[...]
        dx = dres_ref[...] + r * (dyg - xh * jnp.mean(dyg * xh, axis=-1, keepdims=True))
        dx_ref[...] = dx
        dxb_ref[...] = dx.astype(BF16)
        part = jnp.sum(dhv * xh, axis=0, keepdims=True)

        @pl.when(pl.program_id(0) == 0)
        def _():
            dg_ref[...] = part

        @pl.when(pl.program_id(0) > 0)
        def _():
            dg_ref[...] += part

    row = pl.BlockSpec((tm, D), lambda i: (i, 0))
    vec = pl.BlockSpec((1, D), lambda i: (0, 0))
    return _call(
        body, name=name, grid=(s // tm,), in_specs=[row, row, vec, row] + [HBM_SPEC] * len(after), out_specs=[row, row, vec],
        out_shape=[_sds((s, D), F32), _sds((s, D), BF16), _sds((1, D), F32)],
        compiler_params=_params("arbitrary"),
    )(dh, x, g, dres, *after)


def _loss_head(x2, g, tgt, name):
    s = x2.shape[0]
    tm = _row_tile(s)

    def body(x_ref, g_ref, t_ref, dx_ref, dxb_ref, dg_ref, l_ref):
        xv = x_ref[...]
        gv = g_ref[...]
        r = lax.rsqrt(jnp.mean(xv * xv, axis=-1, keepdims=True) + EPS)
        xh = xv * r
        err = xh * gv - t_ref[...]
        dy = err * (1.0 / D)
        dyg = dy * gv
        dx = r * (dyg - xh * jnp.mean(dyg * xh, axis=-1, keepdims=True))
        dx_ref[...] = dx
        dxb_ref[...] = dx.astype(BF16)
        dg_part = jnp.sum(dy * xh, axis=0, keepdims=True)
        l_part = jnp.sum(err * err, axis=0, keepdims=True)

        @pl.when(pl.program_id(0) == 0)
        def _():
            dg_ref[...] = dg_part
            l_ref[...] = l_part

        @pl.when(pl.program_id(0) > 0)
        def _():
            dg_ref[...] += dg_part
            l_ref[...] += l_part

    row = pl.BlockSpec((tm, D), lambda i: (i, 0))
    vec = pl.BlockSpec((1, D), lambda i: (0, 0))
    return _call(
        body, name=name, grid=(s // tm,), in_specs=[row, vec, row], out_specs=[row, row, vec, vec],
        out_shape=[_sds((s, D), F32), _sds((s, D), BF16), _sds((1, D), F32), _sds((1, D), F32)],
        compiler_params=_params("arbitrary"),
    )(x2, g, tgt)


CONV_TC = 128


def _shift_down(u, k, rows):
    return jnp.where(rows >= k, pltpu.roll(u, k, 0), 0.0)


def _shift_up(u, k, rows, s):
    return jnp.where(rows < s - k, pltpu.roll(u, s - k, 0), 0.0)


def _conv_specs(s):
    nb = D // CONV_TC

    def col(c0):
        return pl.BlockSpec((s, CONV_TC), lambda j, c0=c0: (0, c0 // CONV_TC + j))

    return nb, col


def _conv_fwd(proj, conv_w, name):
    s = proj.shape[0]
    nb, col = _conv_specs(s)

    def body(cb_ref, cc_ref, cx_ref, w_ref, y_ref):
        rows = lax.broadcasted_iota(jnp.int32, (s, CONV_TC), 0)
        u = cc_ref[...].astype(F32) * cx_ref[...].astype(F32)
        w = w_ref[...]
        c = w[0:1] * _shift_down(u, 2, rows) + w[1:2] * _shift_down(u, 1, rows) + w[2:3] * u
        y_ref[...] = (cb_ref[...].astype(F32) * c).astype(BF16)

    return _call(
        body, name=name, grid=(nb,),
        in_specs=[col(C_CB), col(C_CC), col(C_CX), pl.BlockSpec((3, CONV_TC), lambda j: (0, j))],
        out_specs=pl.BlockSpec((s, CONV_TC), lambda j: (0, j)), out_shape=_sds((s, D), BF16),
        compiler_params=_params("parallel"),
    )(proj, proj, proj, conv_w)


def _write_behind(t, nt, buf, sems, tiles, window, where):
    slot = t % 2

    def copies(sl, at):
        return [pltpu.make_async_copy(buf.at[sl, p], window(p, at), sems.at[sl, p]) for p in range(len(tiles))]

    @pl.when(t >= 2)
    def _():
        for cp in copies(slot, where):
            cp.wait()

    for p, tile in enumerate(tiles):
        buf[slot, p] = tile
    started = copies(slot, where)
    for cp in started:
        cp.start()

    @pl.when(t == nt - 1)
    def _():
        for cp in started:
            cp.wait()
        if nt > 1:
            for cp in copies(1 - slot, where):
                cp.wait()


def _conv_bwd(dy, proj, conv_w, dproj, name, after=()):
    s = proj.shape[0]
    nb, col = _conv_specs(s)

    def body(dy_ref, cb_ref, cc_ref, cx_ref, w_ref, *rest):
        dproj_ref, dw_ref, buf, sems = rest[1 + len(after):]
        j = pl.program_id(0)
        rows = lax.broadcasted_iota(jnp.int32, (s, CONV_TC), 0)
        cc = cc_ref[...].astype(F32)
        cx = cx_ref[...].astype(F32)
        u = cc * cx
        u1 = _shift_down(u, 1, rows)
        u2 = _shift_down(u, 2, rows)
        w = w_ref[...]
        c = w[0:1] * u2 + w[1:2] * u1 + w[2:3] * u
        dyv = dy_ref[...].astype(F32)
        dc = dyv * cb_ref[...].astype(F32)
        du = w[2:3] * dc + w[1:2] * _shift_up(dc, 1, rows, s) + w[0:1] * _shift_up(dc, 2, rows, s)
        def window(p, jj):
            start = pl.multiple_of((C_CB, C_CC, C_CX)[p] + jj * CONV_TC, CONV_TC)
            return dproj_ref.at[:, pl.ds(start, CONV_TC)]

        tiles = ((dyv * c).astype(BF16), (du * cx).astype(BF16), (du * cc).astype(BF16))
        _write_behind(j, nb, buf, sems, tiles, window, j)
        dw_ref[...] = jnp.concatenate(
            [jnp.sum(dc * u2, axis=0, keepdims=True), jnp.sum(dc * u1, axis=0, keepdims=True),
             jnp.sum(dc * u, axis=0, keepdims=True)], axis=0)

    return _call(
        body, name=name, grid=(nb,),
        in_specs=[pl.BlockSpec((s, CONV_TC), lambda j: (0, j)), col(C_CB), col(C_CC), col(C_CX),
                  pl.BlockSpec((3, CONV_TC), lambda j: (0, j))] + [HBM_SPEC] * (1 + len(after)),
        out_specs=[pl.BlockSpec(memory_space=pl.ANY), pl.BlockSpec((3, CONV_TC), lambda j: (0, j))],
        out_shape=[_sds((s, N_IN), BF16), _sds((3, D), F32)],
        scratch_shapes=[pltpu.VMEM((2, 3, s, CONV_TC), BF16), pltpu.SemaphoreType.DMA((2, 3))],
        input_output_aliases={5: 0}, compiler_params=_params("arbitrary"),
    )(dy, proj, proj, proj, conv_w, dproj, *after)


def _rope_tables(s):
    half = ROT_DIM // 2
    inv_freq = ROPE_THETA ** (-jnp.arange(0, ROT_DIM, 2, dtype=F32) / ROT_DIM)
    inv64 = jnp.concatenate([inv_freq, inv_freq, jnp.zeros((HEAD_DIM - ROT_DIM,), F32)])
    ang = jnp.arange(s, dtype=F32)[:, None] * jnp.concatenate([inv64, inv64])[None, :]
    d = lax.broadcasted_iota(jnp.int32, (s, 128), 1) % HEAD_DIM
    cos, sin = jnp.cos(ang), jnp.sin(ang)
    c = jnp.where(d < ROT_DIM, cos, 1.0)
    a = jnp.where(d < half, -sin, 0.0)
    b = jnp.where((d >= half) & (d < ROT_DIM), sin, 0.0)
    return jnp.concatenate([c, a, b], axis=1)


def _rope(x, tab):
    c, a, b = tab[:, 0:128], tab[:, 128:256], tab[:, 256:384]
    outs = []
    for i in range(x.shape[1] // 128):
        xc = x[:, i * 128:(i + 1) * 128]
        outs.append(xc * c + pltpu.roll(xc, 120, 1) * a + pltpu.roll(xc, 8, 1) * b)
    return outs[0] if len(outs) == 1 else jnp.concatenate(outs, axis=1)


def _rope_t(dx, tab):
    c, a, b = tab[:, 0:128], tab[:, 128:256], tab[:, 256:384]
    outs = []
    for i in range(dx.shape[1] // 128):
        dc = dx[:, i * 128:(i + 1) * 128]
        outs.append(dc * c + pltpu.roll(dc * a, 8, 1) + pltpu.roll(dc * b, 120, 1))
    return outs[0] if len(outs) == 1 else jnp.concatenate(outs, axis=1)


def _attn_in_specs():
    prev = lambda n: jnp.maximum(n - 1, 0)
    return [
        pl.BlockSpec((BLOCK, D), lambda n: (n, C_Q // D)),
        pl.BlockSpec((BLOCK, D_KV), lambda n: (n, C_K // D_KV)),
        pl.BlockSpec((BLOCK, D_KV), lambda n: (prev(n), C_K // D_KV)),
        pl.BlockSpec((BLOCK, D_KV), lambda n: (n, C_V // D_KV)),
        pl.BlockSpec((BLOCK, D_KV), lambda n: (prev(n), C_V // D_KV)),
        pl.BlockSpec((BLOCK, 384), lambda n: (n, 0)),
        pl.BlockSpec((BLOCK, 384), lambda n: (prev(n), 0)),
        pl.BlockSpec(memory_space=pltpu.SMEM),
    ]


HALF = HEAD_DIM
N_CHUNK = D // 128


def _swa_bias(n):
    qi = lax.broadcasted_iota(jnp.int32, (BLOCK, 2 * BLOCK), 0)
    kj = lax.broadcasted_iota(jnp.int32, (BLOCK, 2 * BLOCK), 1)
    rel = qi + BLOCK - kj
    valid = (rel >= 0) & (rel < BLOCK) & ((kj >= BLOCK) | (n > 0))
    return jnp.where(valid, 0.0, NEG_INF)


def _halves(x):
    lo = lax.broadcasted_iota(jnp.int32, x.shape, 1) < HALF
    return jnp.where(lo, x, 0.0).astype(BF16), jnp.where(lo, 0.0, x).astype(BF16)


def _dup_heads(x):
    out = []
    for pair in range(N_KV // 2):
        xc = x[:, pair * 128:(pair + 1) * 128]
        xr = pltpu.roll(xc, HALF, 1)
        lo = lax.broadcasted_iota(jnp.int32, xc.shape, 1) < HALF
        out += [jnp.where(lo, xc, xr), jnp.where(lo, xr, xc)]
    return out


def _swa_load(q_ref, kc_ref, kp_ref, vc_ref, vp_ref, tc_ref, tp_ref):
    qf = _rope(q_ref[...].astype(F32), tc_ref[...]) * ATTN_SCALE
    q_halves = [_halves(qf[:, c * 128:(c + 1) * 128]) for c in range(N_CHUNK)]
    kf = jnp.concatenate([_rope(kp_ref[...].astype(F32), tp_ref[...]), _rope(kc_ref[...].astype(F32), tc_ref[...])], axis=0)
    vf = jnp.concatenate([vp_ref[...], vc_ref[...]], axis=0).astype(F32)
    return q_halves, _dup_heads(kf), _dup_heads(vf)


def _swa_probs(qh, kk, bias, sink):
    s = lax.dot_general(qh, kk, NT, preferred_element_type=F32) + bias
    m = jnp.maximum(jnp.max(jnp.maximum(s[:, :BLOCK], s[:, BLOCK:]), axis=1, keepdims=True), sink)
    return jnp.exp(s - m), m


def _swa_fwd(proj, tab, sinks, name, after=()):
    s = proj.shape[0]

    def body(q_ref, kc_ref, kp_ref, vc_ref, vp_ref, tc_ref, tp_ref, sink_ref, *rest):
        o_ref = rest[-1]
        n = pl.program_id(0)
        q_halves, kdup, vdup = _swa_load(q_ref, kc_ref, kp_ref, vc_ref, vp_ref, tc_ref, tp_ref)
        bias = _swa_bias(n)
        ones = jnp.ones((2 * BLOCK, 128), BF16)
        kk = [k.astype(BF16) for k in kdup]
        vv = [[jnp.concatenate([v_half, ones], axis=1) for v_half in _halves(v)] for v in vdup]
        heads = [(c, half) for c in range(N_CHUNK) for half in range(2)]
        scores = [lax.dot_general(q_halves[c][half], kk[c // (GROUP // 2)], NT, preferred_element_type=F32)
                  for c, half in heads]
        probs = []
        for (c, half), sc in zip(heads, scores):
            sc = sc + bias
            m = jnp.maximum(jnp.max(jnp.maximum(sc[:, :BLOCK], sc[:, BLOCK:]), axis=1, keepdims=True), sink_ref[0, 2 * c + half])
            probs.append((jnp.exp(sc - m).astype(BF16), jnp.exp(sink_ref[0, 2 * c + half] - m)))
        outs = [lax.dot_general(e, vv[c // (GROUP // 2)][half], NN, preferred_element_type=F32)
                for (c, half), (e, _) in zip(heads, probs)]
        for c in range(N_CHUNK):
            parts = [outs[2 * c + half][:, :128] * (1.0 / (outs[2 * c + half][:, 128:] + probs[2 * c + half][1]))
                     for half in range(2)]
            o_ref[:, c * 128:(c + 1) * 128] = (parts[0] + parts[1]).astype(BF16)

    return _call(
        body, name=name, grid=(s // BLOCK,), in_specs=_attn_in_specs() + [HBM_SPEC] * len(after),
        out_specs=pl.BlockSpec((BLOCK, D), lambda n: (n, 0)), out_shape=_sds((s, D), BF16),
        compiler_params=_params("parallel"),
    )(proj, proj, proj, proj, proj, tab, tab, sinks, *after)


def _swa_bwd(do, proj, tab, sinks, dproj, name, after=()):
    s = proj.shape[0]
    nblk = s // BLOCK
    kv_of = lambda c: c // (GROUP // 2)

    def body(do_ref, q_ref, kc_ref, kp_ref, vc_ref, vp_ref, tc_ref, tp_ref, sink_ref, *rest):
        dproj_ref, dk_ref, dv_ref, ds_ref, dqout, dkbuf, dvbuf, sems = rest[1 + len(after):]
        n = pl.program_id(0)

        @pl.when(n == 0)
        def _():
            dk_ref[...] = jnp.zeros_like(dk_ref)
            dv_ref[...] = jnp.zeros_like(dv_ref)
            ds_ref[...] = jnp.zeros_like(ds_ref)

        q_halves, kdup, vdup = _swa_load(q_ref, kc_ref, kp_ref, vc_ref, vp_ref, tc_ref, tp_ref)
        dof = do_ref[...].astype(F32)
        do_halves = [_halves(dof[:, c * 128:(c + 1) * 128]) for c in range(N_CHUNK)]
        bias = _swa_bias(n)
        ones = jnp.ones((2 * BLOCK, 128), BF16)
        kk = [k.astype(BF16) for k in kdup]
        vv = [v.astype(BF16) for v in vdup]
        k_halves = [_halves(k) for k in kdup]
        heads = [(c, half) for c in range(N_CHUNK) for half in range(2)]
        lane_row = lax.broadcasted_iota(jnp.int32, (1, 128), 1)
        lo_kv = lax.broadcasted_iota(jnp.int32, (2 * BLOCK, 128), 1) < HALF
        scores = [lax.dot_general(q_halves[c][half], kk[kv_of(c)], NT, preferred_element_type=F32) for c, half in heads]
        dps = [lax.dot_general(do_halves[c][half], vv[kv_of(c)], NT, preferred_element_type=F32) for c, half in heads]
        exps = []
        for (c, half), sc in zip(heads, scores):
            sink = sink_ref[0, 2 * c + half]
            sc = sc + bias
            m = jnp.maximum(jnp.max(jnp.maximum(sc[:, :BLOCK], sc[:, BLOCK:]), axis=1, keepdims=True), sink)
            exps.append((jnp.exp(sc - m), jnp.exp(sink - m)))
        sums = [lax.dot_general(e.astype(BF16), ones, NN, preferred_element_type=F32) for e, _ in exps]
        dsink_row = jnp.zeros((1, 128), F32)
        dsb, pb = [], []
        for h, ((e, es), row_sum, dp) in enumerate(zip(exps, sums, dps)):
            inv = 1.0 / (row_sum + es)
            p = e * jnp.concatenate([inv, inv], axis=1)
            t = p * dp
            delta = jnp.sum(t, axis=1, keepdims=True)
            dsb.append((t - p * delta).astype(BF16))
            pb.append(p.astype(BF16))
            dsink = -jnp.sum(es * inv * delta, axis=0, keepdims=True)
            dsink_row = dsink_row + jnp.where(lane_row == h, dsink, 0.0)
        dq_parts = [lax.dot_general(d, k_halves[kv_of(c)][half], NN, preferred_element_type=F32) for (c, half), d in zip(heads, dsb)]
        dk_parts = [lax.dot_general(d, q_halves[c][half], TN, preferred_element_type=F32) for (c, half), d in zip(heads, dsb)]
        dv_parts = [lax.dot_general(p, do_halves[c][half], TN, preferred_element_type=F32) for (c, half), p in zip(heads, pb)]
        dq = jnp.concatenate([(dq_parts[2 * c] + dq_parts[2 * c + 1]) * ATTN_SCALE for c in range(N_CHUNK)], axis=1)

        def kv_sum(parts, hk):
            acc = (parts[GROUP * hk] + parts[GROUP * hk + 1]) + (parts[GROUP * hk + 2] + parts[GROUP * hk + 3])
            return acc + pltpu.roll(acc, HALF, 1)

        for pair in range(N_KV // 2):
            dkbuf[:, pair * 128:(pair + 1) * 128] = jnp.where(lo_kv, kv_sum(dk_parts, 2 * pair), kv_sum(dk_parts, 2 * pair + 1))
            dvbuf[:, pair * 128:(pair + 1) * 128] = jnp.where(lo_kv, kv_sum(dv_parts, 2 * pair), kv_sum(dv_parts, 2 * pair + 1))
        prev0 = pl.multiple_of(jnp.maximum(n - 1, 0) * BLOCK, BLOCK)
        cur0 = pl.multiple_of(n * BLOCK, BLOCK)

        @pl.when(n > 0)
        def _():
            dk_ref[pl.ds(prev0, BLOCK), :] += dkbuf[0:BLOCK, :]
            dv_ref[pl.ds(prev0, BLOCK), :] += dvbuf[0:BLOCK, :]

        dk_ref[pl.ds(cur0, BLOCK), :] += dkbuf[BLOCK:2 * BLOCK, :]
        dv_ref[pl.ds(cur0, BLOCK), :] += dvbuf[BLOCK:2 * BLOCK, :]
        ds_ref[...] += dsink_row

        def window(p, at):
            return dproj_ref.at[pl.ds(pl.multiple_of(at * BLOCK, BLOCK), BLOCK), pl.ds(C_Q, D)]

        _write_behind(n, nblk, dqout, sems, (_rope_t(dq, tc_ref[...]).astype(BF16),), window, n)

    blk = lambda w: pl.BlockSpec((BLOCK, w), lambda n: (n, 0))
    whole = lambda w: pl.BlockSpec((s, w), lambda n: (0, 0))
    n_in = 1 + len(_attn_in_specs())
    return _call(
        body, name=name, grid=(nblk,), in_specs=[blk(D)] + _attn_in_specs() + [HBM_SPEC] * (1 + len(after)),
        out_specs=[HBM_SPEC, whole(D_KV), whole(D_KV), pl.BlockSpec((1, 128), lambda n: (0, 0))],
        out_shape=[_sds((s, N_IN), BF16), _sds((s, D_KV), F32), _sds((s, D_KV), F32), _sds((1, 128), F32)],
        scratch_shapes=[pltpu.VMEM((2, 1, BLOCK, D), BF16), pltpu.VMEM((2 * BLOCK, D_KV), F32),
                        pltpu.VMEM((2 * BLOCK, D_KV), F32), pltpu.SemaphoreType.DMA((2, 1))],
        input_output_aliases={n_in: 0}, compiler_params=_params("arbitrary"),
    )(do, proj, proj, proj, proj, proj, tab, tab, sinks, dproj, *after)


def _kv_bwd(dkr, dv, tab, dproj, name):
    s = dkr.shape[0]
    tm = _row_tile(s)

    def body(dk_ref, dv_ref, t_ref, dproj_in, o_ref):
        del dproj_in
        o_ref[:, 0:D_KV] = _rope_t(dk_ref[...], t_ref[...]).astype(BF16)
        o_ref[:, D_KV:2 * D_KV] = dv_ref[...].astype(BF16)

    row = lambda w: pl.BlockSpec((tm, w), lambda i: (i, 0))
    return _call(
        body, name=name, grid=(s // tm,),
        in_specs=[row(D_KV), row(D_KV), row(384), pl.BlockSpec(memory_space=pl.ANY)],
        out_specs=pl.BlockSpec((tm, 2 * D_KV), lambda i: (i, C_K // (2 * D_KV))),
        out_shape=_sds((s, N_IN), BF16), input_output_aliases={3: 0}, compiler_params=_params("parallel"),
    )(dkr, dv, tab, dproj)


EW_TC = 512


def _sigmoid(x):
    return 0.5 * jnp.tanh(0.5 * x) + 0.5


def _merge_fwd(proj, conv_out, attn_out, name):
    s = proj.shape[0]
    tm = _row_tile(s)
    tile = pl.BlockSpec((tm, EW_TC), lambda i, j: (i, j))

    def body(gc_ref, ga_ref, co_ref, ao_ref, o_ref):
        o_ref[...] = (_sigmoid(gc_ref[...].astype(F32)) * co_ref[...].astype(F32)
                      + _sigmoid(ga_ref[...].astype(F32)) * ao_ref[...].astype(F32)).astype(BF16)

    return _call(
        body, name=name, grid=(s // tm, D // EW_TC),
        in_specs=[pl.BlockSpec((tm, EW_TC), lambda i, j: (i, C_GC // EW_TC + j)),
                  pl.BlockSpec((tm, EW_TC), lambda i, j: (i, C_GA // EW_TC + j)), tile, tile],
        out_specs=tile, out_shape=_sds((s, D), BF16), compiler_params=_params("parallel", "parallel"),
    )(proj, proj, conv_out, attn_out)


def _merge_bwd(dmerged, proj, conv_out, attn_out, name):
    s = proj.shape[0]
    tm = _row_tile(s)
    tile = pl.BlockSpec((tm, EW_TC), lambda i, j: (i, j))
    anyspec = pl.BlockSpec(memory_space=pl.ANY)

    def body(dm_ref, gc_ref, ga_ref, co_ref, ao_ref, dproj_ref, dco_ref, dao_ref, buf, sems):
        i, j = pl.program_id(0), pl.program_id(1)
        dm = dm_ref[...].astype(F32)
        sc = _sigmoid(gc_ref[...].astype(F32))
        sa = _sigmoid(ga_ref[...].astype(F32))
        dco_ref[...] = (dm * sc).astype(BF16)
        dao_ref[...] = (dm * sa).astype(BF16)
        tiles = ((dm * co_ref[...].astype(F32) * sc * (1.0 - sc)).astype(BF16),
                 (dm * ao_ref[...].astype(F32) * sa * (1.0 - sa)).astype(BF16))

        def window(p, at):
            start = pl.multiple_of((C_GC, C_GA)[p] + at[1] * EW_TC, EW_TC)
            return dproj_ref.at[pl.ds(pl.multiple_of(at[0] * tm, tm), tm), pl.ds(start, EW_TC)]

        _write_behind(i * nj + j, (s // tm) * nj, buf, sems, tiles, window, (i, j))

    nj = D // EW_TC
    return _call(
        body, name=name, grid=(s // tm, nj),
        in_specs=[tile, pl.BlockSpec((tm, EW_TC), lambda i, j: (i, C_GC // EW_TC + j)),
                  pl.BlockSpec((tm, EW_TC), lambda i, j: (i, C_GA // EW_TC + j)), tile, tile],
        out_specs=[anyspec, tile, tile],
        out_shape=[_sds((s, N_IN), BF16), _sds((s, D), BF16), _sds((s, D), BF16)],
        scratch_shapes=[pltpu.VMEM((2, 2, tm, EW_TC), BF16), pltpu.SemaphoreType.DMA((2, 2))],
        compiler_params=_params("arbitrary", "arbitrary"),
    )(dmerged, proj, proj, conv_out, attn_out)


FF_TC = 256


def _gate_up_fwd(h2, wgu_t, name):
    s = h2.shape[0]
    tm = min(2048, s)
    nb = D_FF // FF_TC

    def body(h_ref, wg_ref, wu_ref, g_ref, u_ref, a_ref):
        h = h_ref[...]
        g = lax.dot_general(h, wg_ref[...], NT, preferred_element_type=F32)
        u = lax.dot_general(h, wu_ref[...], NT, preferred_element_type=F32)
        g_ref[...] = g.astype(BF16)
        u_ref[...] = u.astype(BF16)
        a_ref[...] = (g * _sigmoid(g) * u).astype(BF16)

    tile = pl.BlockSpec((tm, FF_TC), lambda i, j: (i, j))
    return _call(
        body, name=name, grid=(s // tm, nb),
        in_specs=[pl.BlockSpec((tm, D), lambda i, j: (i, 0)), pl.BlockSpec((FF_TC, D), lambda i, j: (j, 0)),
                  pl.BlockSpec((FF_TC, D), lambda i, j: (nb + j, 0))],
        out_specs=[tile, tile, tile], out_shape=[_sds((s, D_FF), BF16)] * 3,
        compiler_params=_params("parallel", "parallel"),
    )(h2, wgu_t, wgu_t)


def _down_bwd_x(dx2b, wd, gate, up, name):
    s = dx2b.shape[0]
    tm = min(2048, s)
    nb = D_FF // FF_TC

    def body(dx_ref, w_ref, g_ref, u_ref, dg_ref, du_ref):
        da = lax.dot_general(dx_ref[...], w_ref[...], NT, preferred_element_type=F32)
        g = g_ref[...].astype(F32)
        sg = _sigmoid(g)
        dg_ref[...] = (da * u_ref[...].astype(F32) * (sg * (1.0 + g * (1.0 - sg)))).astype(BF16)
        du_ref[...] = (da * (g * sg)).astype(BF16)

    tile = pl.BlockSpec((tm, FF_TC), lambda i, j: (i, j))
    return _call(
        body, name=name, grid=(s // tm, nb),
        in_specs=[pl.BlockSpec((tm, D), lambda i, j: (i, 0)), pl.BlockSpec((FF_TC, D), lambda i, j: (j, 0)), tile, tile],
        out_specs=[tile, tile], out_shape=[_sds((s, D_FF), BF16)] * 2,
        compiler_params=_params("parallel", "parallel"),
    )(dx2b, wd, gate, up)


class _Weights:
    def __init__(self, **groups):
        self.groups = groups

    def begin(self, group, after):
        return ()

    def end(self, group, after):
        return self.groups[group]


class _NoReduce:
    def start(self, group, grads):
        return ()

    def middle(self, group, after):
        return ()


def _local_step(x, tgt, g_mix, g_ffn, g_final, sinks, weights, reducer=None, after=()):
    reducer = reducer or _NoReduce()
    s = x.shape[0]
    tab = _rope_tables(s)
    big = dict(tm=1024, tn=512, tk=1024)
    h1 = _rms_fwd(x, g_mix, "rms1_fwd", after=after)
    win_t, conv_w = weights.end("in", weights.begin("in", (h1,)))
    proj = _matmul(h1, win_t, mode="nt", out_dtype=BF16, name="proj_fwd", tm=2048, tn=512, tk=1024)
    attn = _swa_fwd(proj, tab, sinks, "attn_fwd", after=weights.begin("mix", (proj,)))
    wco, wao, wo = weights.end("mix", (attn,))
    conv_y = _conv_fwd(proj, conv_w, "conv_fwd")
    conv_out = _matmul(conv_y, wco, mode="nn", out_dtype=BF16, name="conv_out_fwd", **big)
    attn_out = _matmul(attn, wao, mode="nn", out_dtype=BF16, name="attn_out_fwd", **big)
    merged = _merge_fwd(proj, conv_out, attn_out, "merge_fwd")
    x1 = _matmul(merged, wo, mode="nn", out_dtype=F32, name="wo_fwd", res=x, after=weights.begin("ffn", (merged,)), **big)
    h2 = _rms_fwd(x1, g_ffn, "rms2_fwd")
    wgu_t, wd = weights.end("ffn", (h2,))
    gate, up, act = _gate_up_fwd(h2, wgu_t, "gate_up_fwd")
    x2 = _matmul(act, wd, mode="nn", out_dtype=F32, name="down_fwd", res=x1, tm=1024, tn=512, tk=D_FF)
    dx2, dx2b, dg_final, lossvec = _loss_head(x2, g_final, tgt, "loss_head")
    dgate, dup = _down_bwd_x(dx2b, wd, gate, up, "down_bwd_x")
    g_wd = _matmul(act, dx2b, mode="tn", out_dtype=BF16, name="down_bwd_w", tm=1408, tn=1024, tk=2048)
    dh2 = _matmul([dgate, dup], wgu_t, mode="nn", out_dtype=F32, name="gate_up_bwd_x", tm=1024, tn=1024, tk=1408)
    g_wgu_t = _matmul([dgate, dup], h2, mode="tn", out_dtype=BF16, name="gate_up_bwd_w", tm=1408, tn=1024, tk=2048)
    after_ffn = reducer.start("ffn", dict(wgu_t=g_wgu_t, wd=g_wd))
    dx1, dx1b, dg_ffn = _rms_bwd(dh2, x1, g_ffn, dx2, "rms2_bwd")
    dmerged = _matmul(dx1b, wo, mode="nt", out_dtype=BF16, name="wo_bwd_x", after=after_ffn, **big)
    after_ffn = reducer.middle("ffn", (dmerged,))
    g_wo = _matmul(merged, dx1b, mode="tn", out_dtype=BF16, name="wo_bwd_w", tm=512, tn=1024, tk=2048, after=after_ffn)
    dproj, dco, dao = _merge_bwd(dmerged, proj, conv_out, attn_out, "merge_bwd")
    dconv_y = _matmul(dco, wco, mode="nt", out_dtype=BF16, name="conv_out_bwd_x", **big)
    g_wco = _matmul(conv_y, dco, mode="tn", out_dtype=BF16, name="conv_out_bwd_w", tm=512, tn=1024, tk=2048)
    dattn = _matmul(dao, wao, mode="nt", out_dtype=BF16, name="attn_out_bwd_x", **big)
    g_wao = _matmul(attn, dao, mode="tn", out_dtype=BF16, name="attn_out_bwd_w", tm=512, tn=1024, tk=2048)
    after_mix = reducer.start("mix", dict(wco=g_wco, wao=g_wao, wo=g_wo))
    dproj, dconv_w = _conv_bwd(dconv_y, proj, conv_w, dproj, "conv_bwd", after=after_mix)
    after_mix = reducer.middle("mix", (dconv_w,))
    dproj, dkr, dv, dsinks = _swa_bwd(dattn, proj, tab, sinks, dproj, "attn_bwd", after=after_mix)
    dproj = _kv_bwd(dkr, dv, tab, dproj, "kv_bwd")
    g_win_t = _matmul(dproj, h1, mode="tn", out_dtype=BF16, name="proj_bwd_w", tm=512, tn=1024, tk=2048)
    after_in = reducer.middle("in", reducer.start("in", dict(win_t=g_win_t)))
    dh1 = _matmul(dproj, win_t, mode="nn", out_dtype=F32, name="proj_bwd_x", tm=1024, tn=1024, tk=1664, after=after_in)
    dx, _, dg_mix = _rms_bwd(dh1, x, g_mix, dx1, "rms1_bwd")
    grads = dict(win_t=g_win_t, wgu_t=g_wgu_t, wd=g_wd, wco=g_wco, wao=g_wao, wo=g_wo)
    small = dict(g_mix=dg_mix, g_ffn=dg_ffn, g_final=dg_final, conv_w=dconv_w, sinks=dsinks, lossvec=lossvec)
    return dx, grads, small


def _position():
    return lax.axis_index("x"), lax.axis_index("y"), lax.axis_index("c")


def _other_chips(x, y):
    return [(1 - x, y), (x, 1 - y), (1 - x, 1 - y)]


SEM_SPEC = pl.BlockSpec(memory_space=pltpu.SEMAPHORE)
EFFECT = pltpu.SideEffectType.DATAFLOW_SIDE_EFFECTING
TOKEN = jax.ShapeDtypeStruct((8, 128), F32)
TOKEN_SPEC = pl.BlockSpec(memory_space=pltpu.VMEM)


def _hbm(a):
    return pltpu.with_memory_space_constraint(a, pltpu.HBM)


def _place(w, me_idx, dtype, name, after=()):
    r, cdim = w.shape

    def body(i_ref, w_ref, *rest):
        rest[-1][...] = w_ref[...].astype(dtype)

    grid_spec = pltpu.PrefetchScalarGridSpec(
        num_scalar_prefetch=1, grid=(1,), in_specs=[pl.BlockSpec((r, cdim), lambda i, me: (0, 0))] + [HBM_SPEC] * len(after),
        out_specs=pl.BlockSpec((r, cdim), lambda i, me: (me[0], 0)))
    return _call(body, name=name, grid_spec=grid_spec, out_shape=_sds((N_DEV * r, cdim), dtype),
                 compiler_params=_params("arbitrary"))(me_idx, w, *after)


def _own_rows(ref, r, px, py, pc):
    return ref.at[pl.ds((4 * px + 2 * py + pc) * r, r), :]


def _gather_start(bufs, groups, name):
    n = len(bufs)
    rows = [b.shape[0] // N_DEV for b in bufs]
    ng = len(groups)

    def body(*refs):
        ins = refs[:n]
        sems = refs[n:n + 2 * ng]
        token = refs[-1]
        x, y, c = _position()
        targets = [(x, y, 1 - c)] + [(*chip, c) for chip in _other_chips(x, y)]
        for g, members in enumerate(groups):
            for slot, a in enumerate(members):
                own = _own_rows(ins[a], rows[a], x, y, c)
                for to in targets:
                    pltpu.make_async_remote_copy(src_ref=own, dst_ref=own, send_sem=sems[2 * g].at[slot],
                                                 recv_sem=sems[2 * g + 1].at[slot], device_id=to, device_id_type=MESH).start()
        token[...] = jnp.zeros_like(token)

    sem_shapes = []
    for members in groups:
        sem_shapes += [pltpu.SemaphoreType.DMA((len(members),))] * 2
    outs = _call(
        body, name=name, in_specs=[HBM_SPEC] * n, out_specs=[SEM_SPEC] * (2 * ng) + [HBM_SPEC] * n + [TOKEN_SPEC],
        out_shape=sem_shapes + [pltpu.HBM(b.shape, b.dtype) for b in bufs] + [TOKEN],
        input_output_aliases={i: 2 * ng + i for i in range(n)},
        compiler_params=pltpu.CompilerParams(has_side_effects=EFFECT),
    )(*[_hbm(b) for b in bufs])
    sem_pairs = [(outs[2 * g], outs[2 * g + 1]) for g in range(ng)]
    return sem_pairs, list(outs[2 * ng:2 * ng + n]), outs[-1]


def _gather_forward(send_sems, recv_sems, bufs, after, name):
    n = len(bufs)
    rows = [b.shape[0] // N_DEV for b in bufs]

    def body(*refs):
        ins = refs[:n]
        send1, recv1 = refs[n], refs[n + 1]
        out0 = n + 2 + len(after)
        send2, recv2 = refs[out0], refs[out0 + 1]
        token = refs[-1]
        x, y, c = _position()
        for a in range(n):
            step1 = pltpu.make_async_remote_copy(
                src_ref=_whole(ins[a], 4 * rows[a]), dst_ref=_whole(ins[a], 4 * rows[a]), send_sem=send1.at[a],
                recv_sem=recv1.at[a], device_id=(x, y, c), device_id_type=MESH)
            step1.wait_send()
            step1.wait_recv()
        for a in range(n):
            for chip in _other_chips(x, y):
                blk = _own_rows(ins[a], rows[a], *chip, c)
                pltpu.make_async_remote_copy(src_ref=blk, dst_ref=blk, send_sem=send2.at[a], recv_sem=recv2.at[a],
                                             device_id=(x, y, 1 - c), device_id_type=MESH).start()
        token[...] = jnp.zeros_like(token)

    outs = _call(
        body, name=name, in_specs=[HBM_SPEC] * n + [SEM_SPEC, SEM_SPEC] + [HBM_SPEC] * len(after),
        out_specs=[SEM_SPEC, SEM_SPEC] + [HBM_SPEC] * n + [TOKEN_SPEC],
        out_shape=[pltpu.SemaphoreType.DMA((n,)), pltpu.SemaphoreType.DMA((n,))]
        + [pltpu.HBM(b.shape, b.dtype) for b in bufs] + [TOKEN],
        input_output_aliases={i: 2 + i for i in range(n)},
        compiler_params=pltpu.CompilerParams(has_side_effects=EFFECT),
    )(*bufs, send_sems, recv_sems, *after)
    return outs[0], outs[1], list(outs[2:2 + n]), outs[-1]


def _gather_done(send_sems, recv_sems, bufs, after, name):
    n = len(bufs)
    rows = [b.shape[0] // N_DEV for b in bufs]

    def body(*refs):
        ins = refs[:n]
        send2, recv2 = refs[n], refs[n + 1]
        x, y, c = _position()
        for a in range(n):
            step2 = pltpu.make_async_remote_copy(
                src_ref=_whole(ins[a], 3 * rows[a]), dst_ref=_whole(ins[a], 3 * rows[a]), send_sem=send2.at[a],
                recv_sem=recv2.at[a], device_id=(x, y, c), device_id_type=MESH)
            step2.wait_send()
            step2.wait_recv()

    outs = _call(
        body, name=name, in_specs=[HBM_SPEC] * n + [SEM_SPEC, SEM_SPEC] + [HBM_SPEC] * len(after),
        out_specs=[HBM_SPEC] * n, out_shape=[pltpu.HBM(b.shape, b.dtype) for b in bufs],
        input_output_aliases={i: i for i in range(n)},
        compiler_params=pltpu.CompilerParams(has_side_effects=EFFECT),
    )(*bufs, send_sems, recv_sems, *after)
    return list(outs)


def _whole(ref, nrows):
    return ref.at[pl.ds(0, nrows), :]


def _to_sibling(x, y, c):
    return [(2 * q + (1 - c), q, (x, y, 1 - c)) for q in range(4)]


def _to_chips(x, y, c):
    return [(2 * px + py, j, (px, py, c)) for j, (px, py) in enumerate(_other_chips(x, y))]


def _exchange_start(srcs, src_slots, plan, name):
    n = len(srcs)
    rows = [a.shape[0] // src_slots for a in srcs]
    n_copies = len(plan(0, 0, 0))
    lands = [lax.empty((n_copies * r, a.shape[1]), a.dtype) for a, r in zip(srcs, rows)]

    def body(*refs):
        ins, land_refs = refs[:n], refs[n:2 * n]
        send_sems, recv_sems = refs[2 * n], refs[2 * n + 1]
        token = refs[-1]
        for a in range(n):
            r = rows[a]
            for src_slot, dst_slot, target in plan(*_position()):
                pltpu.make_async_remote_copy(
                    src_ref=ins[a].at[pl.ds(src_slot * r, r), :], dst_ref=land_refs[a].at[pl.ds(dst_slot * r, r), :],
                    send_sem=send_sems.at[a], recv_sem=recv_sems.at[a], device_id=target, device_id_type=MESH).start()
        token[...] = jnp.zeros_like(token)

    outs = _call(
        body, name=name, in_specs=[HBM_SPEC] * (2 * n),
        out_specs=[SEM_SPEC, SEM_SPEC] + [HBM_SPEC] * (2 * n) + [TOKEN_SPEC],
        out_shape=[pltpu.SemaphoreType.DMA((n,)), pltpu.SemaphoreType.DMA((n,))]
        + [pltpu.HBM(a.shape, a.dtype) for a in srcs] + [pltpu.HBM(l.shape, l.dtype) for l in lands] + [TOKEN],
        input_output_aliases={i: 2 + i for i in range(2 * n)},
        compiler_params=pltpu.CompilerParams(has_side_effects=EFFECT),
    )(*[_hbm(a) for a in srcs], *[_hbm(l) for l in lands])
    return outs[0], outs[1], list(outs[2:2 + n]), list(outs[2 + n:2 + 2 * n]), outs[-1]


def _exchange_wait(send_sems, recv_sems, srcs, lands, after, name):
    n = len(srcs)

    def body(*refs):
        ins, land_refs = refs[:n], refs[n:2 * n]
        send_sems_ref, recv_sems_ref = refs[2 * n], refs[2 * n + 1]
        for a in range(n):
            allrows = lands[a].shape[0]
            cp = pltpu.make_async_remote_copy(
                src_ref=_whole(ins[a], allrows), dst_ref=_whole(land_refs[a], allrows), send_sem=send_sems_ref.at[a],
                recv_sem=recv_sems_ref.at[a], device_id=_position(), device_id_type=MESH)
            cp.wait_send()
            cp.wait_recv()

    outs = _call(
        body, name=name, in_specs=[HBM_SPEC] * (2 * n) + [SEM_SPEC, SEM_SPEC] + [HBM_SPEC] * len(after),
        out_specs=[HBM_SPEC] * (2 * n),
        out_shape=[pltpu.HBM(a.shape, a.dtype) for a in srcs] + [pltpu.HBM(l.shape, l.dtype) for l in lands],
        input_output_aliases={i: i for i in range(2 * n)},
        compiler_params=pltpu.CompilerParams(has_side_effects=EFFECT),
    )(*srcs, *lands, send_sems, recv_sems, *after)
    return list(outs[:n]), list(outs[n:])


def _chip_partial(grad, recv, idx, name):
    r = recv.shape[0] // 4

    def body(i_ref, g_ref, s_ref, o_ref):
        del i_ref
        o_ref[...] = (g_ref[...].astype(F32) + s_ref[...].astype(F32)).astype(BF16)

    nb = 2
    tr = r // nb
    grid_spec = pltpu.PrefetchScalarGridSpec(
        num_scalar_prefetch=1, grid=(3, nb),
        in_specs=[pl.BlockSpec((tr, D), lambda t, i, i_ref: ((2 * i_ref[1 + t] + i_ref[0]) * nb + i, 0)),
                  pl.BlockSpec((tr, D), lambda t, i, i_ref: (i_ref[1 + t] * nb + i, 0))],
        out_specs=pl.BlockSpec((tr, D), lambda t, i, i_ref: (i_ref[1 + t] * nb + i, 0)))
    return _call(body, name=name, grid_spec=grid_spec, out_shape=_sds((4 * r, D), BF16),
                 compiler_params=_params("arbitrary", "arbitrary"))(idx, grad, recv)


def _adamw_math(w, g, m, v):
    m2 = B1 * m + (1.0 - B1) * g
    v2 = B2 * v + (1.0 - B2) * jnp.square(g)
    m_hat = m2 / (1.0 - B1 ** STEP)
    v_hat = v2 / (1.0 - B2 ** STEP)
    return -LR * (m_hat / (jnp.sqrt(v_hat) + EPS_ADAM) + WD * w), m2, v2


def _reduce_adamw(w, grad, from_sibling, from_chips, idx, m, v, name):
    r = w.shape[0]
    assert grad.shape == (N_DEV * r, D) and from_sibling.shape == (4 * r, D) and from_chips.shape == (3 * r, D)
    tr = r // 2
    nb = r // tr

    def body(i_ref, w_ref, p_ref, s_ref, r0_ref, r1_ref, r2_ref, m_ref, v_ref, g_ref, d_ref, nm_ref, nv_ref):
        del i_ref
        g = p_ref[...].astype(F32) + s_ref[...].astype(F32)
        g = ((g + r0_ref[...].astype(F32)) + r1_ref[...].astype(F32)) + r2_ref[...].astype(F32)
        g_ref[...] = g
        d_ref[...], nm_ref[...], nv_ref[...] = _adamw_math(w_ref[...], g, m_ref[...], v_ref[...])

    own = pl.BlockSpec((tr, D), lambda i, i_ref: (i, 0))
    grid_spec = pltpu.PrefetchScalarGridSpec(
        num_scalar_prefetch=1, grid=(nb,),
        in_specs=[own, pl.BlockSpec((tr, D), lambda i, i_ref: (i_ref[0] * nb + i, 0)),
                  pl.BlockSpec((tr, D), lambda i, i_ref: (i_ref[1] * nb + i, 0))]
        + [pl.BlockSpec((tr, D), lambda i, i_ref, j=j: (j * nb + i, 0)) for j in range(3)] + [own, own],
        out_specs=[own] * 4)
    return _call(body, name=name, grid_spec=grid_spec, out_shape=[_sds((r, D), F32)] * 4,
                 compiler_params=_params("parallel"))(idx, w, grad, from_sibling, from_chips, from_chips, from_chips, m, v)


SMALL_ROWS = 8


def _small_all_reduce(pack, name, after=()):
    def body(p_ref, *rest):
        tot_ref, loss_ref, gath, send_sems, recv_sems = rest[len(after):]
        x, y, c = _position()
        me_id = 4 * x + 2 * y + c
        gath[me_id] = p_ref[...]
        copies = []
        for k in range(1, N_DEV):
            peer = tuple(1 - v if (k >> b) & 1 else v for v, b in ((x, 2), (y, 1), (c, 0)))
            cp = pltpu.make_async_remote_copy(src_ref=p_ref, dst_ref=gath.at[me_id], send_sem=send_sems.at[k - 1],
                                              recv_sem=recv_sems.at[k - 1], device_id=peer, device_id_type=MESH)
            cp.start()
            copies.append(cp)
        for cp in copies:
            cp.wait_recv()
        for cp in copies:
            cp.wait_send()
        tot = gath[0]
        for d in range(1, N_DEV):
            tot = tot + gath[d]
        tot_ref[...] = tot
        loss_ref[...] = jnp.full((1, 128), (0.5 / D) * jnp.sum(tot[SMALL_ROWS - 1:SMALL_ROWS, :]), F32)

    vm = pl.BlockSpec(memory_space=pltpu.VMEM)
    return _call(
        body, name=name, in_specs=[vm] + [HBM_SPEC] * len(after), out_specs=[vm, vm],
        out_shape=[_sds((SMALL_ROWS, D), F32), _sds((1, 128), F32)],
        scratch_shapes=[pltpu.VMEM((N_DEV, SMALL_ROWS, D), F32), pltpu.SemaphoreType.DMA((N_DEV - 1,)),
                        pltpu.SemaphoreType.DMA((N_DEV - 1,))],
    )(pack, *after)


def _adamw(w, g, m, v, name):
    r, cdim = w.shape
    tr = 256 if r % 256 == 0 else (r // 2 if r % 16 == 0 else r)

    def body(w_ref, g_ref, m_ref, v_ref, d_ref, nm_ref, nv_ref):
        d_ref[...], nm_ref[...], nv_ref[...] = _adamw_math(w_ref[...], g_ref[...], m_ref[...], v_ref[...])

    spec = pl.BlockSpec((tr, cdim), lambda i: (i, 0))
    return _call(
        body, name=name, grid=(r // tr,), in_specs=[spec] * 4, out_specs=[spec] * 3,
        out_shape=[_sds((r, cdim), F32)] * 3, compiler_params=_params("parallel"),
    )(w, g, m, v)


def kernel(x, g_mix, w_in, conv_w, attn_sinks, w_conv_out, w_attn_out, w_o, g_ffn, w_gate_up, w_down, g_final, loss_target, m_g_mix, m_w_in, m_conv_w, m_attn_sinks, m_w_conv_out, m_w_attn_out, m_w_o, m_g_ffn, m_w_gate_up, m_w_down, m_g_final, v_g_mix, v_w_in, v_conv_w, v_attn_sinks, v_w_conv_out, v_w_attn_out, v_w_o, v_g_ffn, v_w_gate_up, v_w_down, v_g_final):
    cx, cy, cc = _position()
    chip = 2 * cx + cy
    partial_idx = jnp.stack([cc, 2 * (1 - cx) + cy, 2 * cx + (1 - cy), 2 * (1 - cx) + (1 - cy)]).astype(jnp.int32)
    own_idx = jnp.stack([2 * chip + cc, chip]).astype(jnp.int32)
    me = 4 * cx + 2 * cy + cc

    me_idx = jnp.reshape(me, (1,)).astype(jnp.int32)
    first = [_place(jnp.transpose(w_in[0]), me_idx, BF16, "place_w_in"),
             _place(jnp.pad(conv_w[0], ((0, 5), (0, 0))), me_idx, F32, "place_conv_w")]
    (sems_in,), first, token_in = _gather_start(first, [[0, 1]], "gather_start_in")
    later = [_place(w, me_idx, BF16, "place_" + k, after=(token_in,)) for k, w in (
        ("w_conv_out", w_conv_out[0]), ("w_attn_out", w_attn_out[0]), ("w_o", w_o[0]),
        ("w_gate_up", jnp.transpose(w_gate_up[0])), ("w_down", w_down[0]))]
    (sems_mix, sems_ffn), later, token_later = _gather_start(later, [[0, 1, 2], [3, 4]], "gather_start_later")
    gather_tokens = (token_in, token_later)

    class Gathered:
        def __init__(self):
            self.state = {"in": (sems_in, first), "mix": (sems_mix, later[:3]), "ffn": (sems_ffn, later[3:])}

        def begin(self, group, after):
            (send_sems, recv_sems), group_bufs = self.state[group]
            send2, recv2, group_bufs, token = _gather_forward(send_sems, recv_sems, group_bufs, after, "gather_forward_" + group)
            self.state[group] = ((send2, recv2), group_bufs)
            return (token,)

        def end(self, group, after):
            (send2, recv2), group_bufs = self.state[group]
            full = _gather_done(send2, recv2, group_bufs, after, "gather_done_" + group)
            if group == "in":
                return full[0], jnp.transpose(full[1].reshape(N_DEV, 8, 128)[:, :3, :], (1, 0, 2)).reshape(3, D)
            return full

    in_flight, own_pieces = {}, {}

    class Reducer:
        def start(self, group, gdict):
            keys, glist = list(gdict), list(gdict.values())
            send_sems, recv_sems, glist, lands, token = _exchange_start(glist, N_DEV, _to_sibling, "rs_sibling_start_" + group)
            in_flight[group] = (keys, send_sems, recv_sems, glist, lands)
            return (token,)

        def middle(self, group, after):
            keys, send_sems, recv_sems, glist, lands = in_flight[group]
            glist, lands = _exchange_wait(send_sems, recv_sems, glist, lands, after, "rs_sibling_wait_" + group)
            parts = [_chip_partial(g, r, partial_idx, "chip_partial_" + k) for k, g, r in zip(keys, glist, lands)]
            send_sems, recv_sems, parts, from_chips, token = _exchange_start(parts, 4, _to_chips, "rs_chips_start_" + group)
            in_flight[group] = (keys, send_sems, recv_sems, parts, from_chips)
            own_pieces[group] = (glist, lands)
            return (token,)

    dx, _, small = _local_step(x[0], loss_target[0], g_mix, g_ffn, g_final[None], attn_sinks, Gathered(),
                               reducer=Reducer(), after=gather_tokens)

    transposed = ("w_in", "w_gate_up")

    def as2d(k, a):
        if k in transposed:
            return jnp.transpose(a[0])
        return a[None] if a.ndim == 1 else (a[0] if a.ndim == 3 else a)

    w_all = {"g_mix": g_mix, "w_in": w_in, "conv_w": conv_w, "attn_sinks": attn_sinks, "w_conv_out": w_conv_out,
             "w_attn_out": w_attn_out, "w_o": w_o, "g_ffn": g_ffn, "w_gate_up": w_gate_up, "w_down": w_down, "g_final": g_final}
    m_all = {"g_mix": m_g_mix, "w_in": m_w_in, "conv_w": m_conv_w, "attn_sinks": m_attn_sinks, "w_conv_out": m_w_conv_out,
             "w_attn_out": m_w_attn_out, "w_o": m_w_o, "g_ffn": m_g_ffn, "w_gate_up": m_w_gate_up, "w_down": m_w_down,
             "g_final": m_g_final}
    v_all = {"g_mix": v_g_mix, "w_in": v_w_in, "conv_w": v_conv_w, "attn_sinks": v_attn_sinks, "w_conv_out": v_w_conv_out,
             "w_attn_out": v_w_attn_out, "w_o": v_w_o, "g_ffn": v_g_ffn, "w_gate_up": v_w_gate_up, "w_down": v_w_down,
             "g_final": v_g_final}
    results = {}

    def update(k, g=None, pieces=None):
        w2, m2, v2 = as2d(k, w_all[k]), as2d(k, m_all[k]), as2d(k, v_all[k])
        if g is None:
            g, d, nm, nv = _reduce_adamw(w2, *pieces, own_idx, m2, v2, "adamw_" + k)
        else:
            d, nm, nv = _adamw(w2, g, m2, v2, "adamw_" + k)
        results[k] = [(jnp.transpose(val) if k in transposed else val).reshape(w_all[k].shape) for val in (g, d, nm, nv)]
        return nm

    kernel_name = {"win_t": "w_in", "wgu_t": "w_gate_up", "wd": "w_down", "wco": "w_conv_out", "wao": "w_attn_out", "wo": "w_o"}

    def finish(group, after):
        keys, send_sems, recv_sems, parts, from_chips = in_flight[group]
        _, from_chips = _exchange_wait(send_sems, recv_sems, parts, from_chips, after, "rs_chips_wait_" + group)
        grads, from_sibling = own_pieces[group]
        return tuple(update(kernel_name[k], pieces=p) for k, *p in zip(keys, grads, from_sibling, from_chips))

    after = finish("mix", finish("ffn", (dx,)))

    sinks_row = jnp.pad(small["sinks"], ((0, 0), (0, D - 128)))
    pack = jnp.concatenate([small["g_mix"], small["g_ffn"], small["g_final"], small["conv_w"], sinks_row, small["lossvec"]], axis=0)
    tot, loss_row = _small_all_reduce(pack, "small_all_reduce", after=after)
    loss = loss_row[0, 0]
    g_small = {
        "g_mix": tot[0:1], "g_ffn": tot[1:2], "g_final": tot[2:3],
        "conv_w": lax.dynamic_slice(tot, (3, me * 128), (3, 128)), "attn_sinks": tot[6:7, :N_HEADS],
    }
    finish("in", tuple(update(k, g) for k, g in g_small.items()))

    order = ["g_mix", "w_in", "conv_w", "attn_sinks", "w_conv_out", "w_attn_out", "w_o", "g_ffn", "w_gate_up", "w_down", "g_final"]
    return (loss, dx[None], *[results[k][i] for i in range(4) for k in order])
```

```python
import functools
import math

import jax
import jax.numpy as jnp
from jax import lax
from jax.experimental import pallas as pl
from jax.experimental.pallas import tpu as pltpu

F32 = jnp.float32
BF16 = jnp.bfloat16

D = 1024
HEAD_DIM = 64
N_HEADS = 16
N_KV = 4
GROUP = N_HEADS // N_KV
D_KV = N_KV * HEAD_DIM
BLOCK = 128
ROT_DIM = HEAD_DIM // 4
ROPE_THETA = 500000.0
ATTN_SCALE = 1.0 / math.sqrt(HEAD_DIM)
NEG_INF = -1e30
D_FF = 2816
N_IN = 6656
EPS = 1e-5
C_CB, C_CC, C_CX, C_Q, C_K, C_V, C_GC, C_GA = 0, 1024, 2048, 3072, 4096, 4352, 4608, 5632

LR, B1, B2, EPS_ADAM, WD, STEP = 0.001, 0.9, 0.999, 1e-08, 0.01, 10

N_DEV = 8
MESH = pl.DeviceIdType.MESH
VMEM_LIMIT = 56 * 1024 * 1024

NN = (((1,), (0,)), ((), ()))
NT = (((1,), (1,)), ((), ()))
TN = (((0,), (0,)), ((), ()))
HBM_SPEC = pl.BlockSpec(memory_space=pl.ANY)


def _call(body, **kw):
    return pl.pallas_call(body, **kw)


def _params(*sem):
    return pltpu.CompilerParams(dimension_semantics=sem, vmem_limit_bytes=VMEM_LIMIT)


def _sds(shape, dtype):
    return jax.ShapeDtypeStruct(shape, dtype)


def _matmul(a, b, *, mode, tm, tn, tk, out_dtype, name, res=None, after=()):
    parts = list(a) if isinstance(a, (list, tuple)) else [a]
    rows_a = parts[0].shape[0]
    cols_a = sum(p.shape[1] for p in parts)
    if mode == "nn":
        (m, kk), (_, n), dims = (rows_a, cols_a), b.shape, NN
    elif mode == "nt":
        (m, kk), (n, _), dims = (rows_a, cols_a), b.shape, NT
    else:
        (kk, m), (_, n), dims = (rows_a, cols_a), b.shape, TN
    tm, tn, tk = min(tm, m), min(tn, n), min(tk, kk)
    assert m % tm == 0 and n % tn == 0 and kk % tk == 0, (name, m, n, kk, tm, tn, tk)
    nk = kk // tk
    split_axis, width = (2, tk) if mode == "nn" else (0, tm)
    assert len(parts) == 1 or mode in ("nn", "tn")
    assert len(parts) == 1 or all(p.shape[1] % width == 0 for p in parts), (name, width)
    counts = [p.shape[1] // width for p in parts]
    starts = [sum(counts[:p]) for p in range(len(parts))]

    def a_spec(p):
        def col(t):
            return jnp.clip(t - starts[p], 0, counts[p] - 1) if len(parts) > 1 else t

        if mode == "tn":
            return pl.BlockSpec((tk, tm), lambda i, j, k: (k, col(i)))
        return pl.BlockSpec((tm, tk), lambda i, j, k: (i, col(k)))

    if mode == "nt":
        b_spec = pl.BlockSpec((tn, tk), lambda i, j, k: (j, k))
    else:
        b_spec = pl.BlockSpec((tk, tn), lambda i, j, k: (k, j))
    o_spec = pl.BlockSpec((tm, tn), lambda i, j, k: (i, j))
    has_res = res is not None
    n_parts = len(parts)

    def body(*refs):
        a_refs, b_ref = refs[:n_parts], refs[n_parts]
        r_ref = refs[n_parts + 1] if has_res else None
        o_ref = refs[n_parts + 1 + has_res + len(after)]
        k = pl.program_id(2)

        def step(a_ref):
            part = lax.dot_general(a_ref[...], b_ref[...], dims, preferred_element_type=F32)

            def finish(acc):
                if has_res:
                    acc = acc + r_ref[...]
                o_ref[...] = acc.astype(o_ref.dtype)

            if nk == 1:
                finish(part)
            else:
                acc_ref = refs[-1]

                @pl.when(k == 0)
                def _():
                    acc_ref[...] = part

                @pl.when(k > 0)
                def _():
                    acc_ref[...] += part

                @pl.when(k == nk - 1)
                def _():
                    finish(acc_ref[...])

        if n_parts == 1:
            step(a_refs[0])
        else:
            t = pl.program_id(split_axis)
            for p in range(n_parts):
                pl.when((t >= starts[p]) & (t < starts[p] + counts[p]))(functools.partial(step, a_refs[p]))

    ins = parts + [b] + ([res] if has_res else []) + list(after)
    in_specs = [a_spec(p) for p in range(n_parts)] + [b_spec] + ([o_spec] if has_res else []) + [HBM_SPEC] * len(after)
    scratch = [] if nk == 1 else [pltpu.VMEM((tm, tn), F32)]
    return _call(
        body, name=name, grid=(m // tm, n // tn, nk), in_specs=in_specs, out_specs=o_spec,
        out_shape=_sds((m, n), out_dtype), scratch_shapes=scratch,
        compiler_params=_params("parallel", "parallel", "arbitrary"),
    )(*ins)


def _row_tile(s):
    return min(512, s)


def _rms_fwd(x, g, name, after=()):
    s = x.shape[0]
    tm = _row_tile(s)

    def body(x_ref, g_ref, *rest):
        h_ref = rest[-1]
        xv = x_ref[...]
        r = lax.rsqrt(jnp.mean(xv * xv, axis=-1, keepdims=True) + EPS)
        h_ref[...] = (xv * r * g_ref[...]).astype(BF16)

    row = pl.BlockSpec((tm, D), lambda i: (i, 0))
    return _call(
        body, name=name, grid=(s // tm,), in_specs=[row, pl.BlockSpec((1, D), lambda i: (0, 0))] + [HBM_SPEC] * len(after),
        out_specs=row, out_shape=_sds((s, D), BF16), compiler_params=_params("parallel"),
    )(x, g, *after)


def _rms_bwd(dh, x, g, dres, name, after=()):
    s = x.shape[0]
    tm = _row_tile(s)

    def body(dh_ref, x_ref, g_ref, dres_ref, *rest):
        dx_ref, dxb_ref, dg_ref = rest[len(after):]
        xv = x_ref[...]
        r = lax.rsqrt(jnp.mean(xv * xv, axis=-1, keepdims=True) + EPS)
        xh = xv * r
        dhv = dh_ref[...]
        dyg = dhv * g_ref[...]
        dx = dres_ref[...] + r * (dyg - xh * jnp.mean(dyg * xh, axis=-1, keepdims=True))
        dx_ref[...] = dx
        dxb_ref[...] = dx.astype(BF16)
        part = jnp.sum(dhv * xh, axis=0, keepdims=True)

        @pl.when(pl.program_id(0) == 0)
        def _():
            dg_ref[...] = part

        @pl.when(pl.program_id(0) > 0)
        def _():
            dg_ref[...] += part

    row = pl.BlockSpec((tm, D), lambda i: (i, 0))
    vec = pl.BlockSpec((1, D), lambda i: (0, 0))
    return _call(
        body, name=name, grid=(s // tm,), in_specs=[row, row, vec, row] + [HBM_SPEC] * len(after), out_specs=[row, row, vec],
        out_shape=[_sds((s, D), F32), _sds((s, D), BF16), _sds((1, D), F32)],
        compiler_params=_params("arbitrary"),
    )(dh, x, g, dres, *after)


def _loss_head(x2, g, tgt, name):
    s = x2.shape[0]
    tm = _row_tile(s)

    def body(x_ref, g_ref, t_ref, dx_ref, dxb_ref, dg_ref, l_ref):
        xv = x_ref[...]
        gv = g_ref[...]
        r = lax.rsqrt(jnp.mean(xv * xv, axis=-1, keepdims=True) + EPS)
        xh = xv * r
        err = xh * gv - t_ref[...]
        dy = err * (1.0 / D)
        dyg = dy * gv
        dx = r * (dyg - xh * jnp.mean(dyg * xh, axis=-1, keepdims=True))
        dx_ref[...] = dx
        dxb_ref[...] = dx.astype(BF16)
        dg_part = jnp.sum(dy * xh, axis=0, keepdims=True)
        l_part = jnp.sum(err * err, axis=0, keepdims=True)

        @pl.when(pl.program_id(0) == 0)
        def _():
            dg_ref[...] = dg_part
            l_ref[...] = l_part

        @pl.when(pl.program_id(0) > 0)
        def _():
            dg_ref[...] += dg_part
            l_ref[...] += l_part

    row = pl.BlockSpec((tm, D), lambda i: (i, 0))
    vec = pl.BlockSpec((1, D), lambda i: (0, 0))
    return _call(
        body, name=name, grid=(s // tm,), in_specs=[row, vec, row], out_specs=[row, row, vec, vec],
        out_shape=[_sds((s, D), F32), _sds((s, D), BF16), _sds((1, D), F32), _sds((1, D), F32)],
        compiler_params=_params("arbitrary"),
    )(x2, g, tgt)


CONV_TC = 128


def _shift_down(u, k, rows):
    return jnp.where(rows >= k, pltpu.roll(u, k, 0), 0.0)


def _shift_up(u, k, rows, s):
    return jnp.where(rows < s - k, pltpu.roll(u, s - k, 0), 0.0)


def _conv_specs(s):
    nb = D // CONV_TC

    def col(c0):
        return pl.BlockSpec((s, CONV_TC), lambda j, c0=c0: (0, c0 // CONV_TC + j))

    return nb, col


def _conv_fwd(proj, conv_w, name):
    s = proj.shape[0]
    nb, col = _conv_specs(s)

    def body(cb_ref, cc_ref, cx_ref, w_ref, y_ref):
        rows = lax.broadcasted_iota(jnp.int32, (s, CONV_TC), 0)
        u = cc_ref[...].astype(F32) * cx_ref[...].astype(F32)
        w = w_ref[...]
        c = w[0:1] * _shift_down(u, 2, rows) + w[1:2] * _shift_down(u, 1, rows) + w[2:3] * u
        y_ref[...] = (cb_ref[...].astype(F32) * c).astype(BF16)

    return _call(
        body, name=name, grid=(nb,),
        in_specs=[col(C_CB), col(C_CC), col(C_CX), pl.BlockSpec((3, CONV_TC), lambda j: (0, j))],
        out_specs=pl.BlockSpec((s, CONV_TC), lambda j: (0, j)), out_shape=_sds((s, D), BF16),
        compiler_params=_params("parallel"),
    )(proj, proj, proj, conv_w)


def _write_behind(t, nt, buf, sems, tiles, window, where):
    slot = t % 2

    def copies(sl, at):
        return [pltpu.make_async_copy(buf.at[sl, p], window(p, at), sems.at[sl, p]) for p in range(len(tiles))]

    @pl.when(t >= 2)
    def _():
        for cp in copies(slot, where):
            cp.wait()

    for p, tile in enumerate(tiles):
        buf[slot, p] = tile
    started = copies(slot, where)
    for cp in started:
        cp.start()

    @pl.when(t == nt - 1)
    def _():
        for cp in started:
            cp.wait()
        if nt > 1:
            for cp in copies(1 - slot, where):
                cp.wait()


def _conv_bwd(dy, proj, conv_w, dproj, name, after=()):
    s = proj.shape[0]
    nb, col = _conv_specs(s)

    def body(dy_ref, cb_ref, cc_ref, cx_ref, w_ref, *rest):
        dproj_ref, dw_ref, buf, sems = rest[1 + len(after):]
        j = pl.program_id(0)
        rows = lax.broadcasted_iota(jnp.int32, (s, CONV_TC), 0)
        cc = cc_ref[...].astype(F32)
        cx = cx_ref[...].astype(F32)
        u = cc * cx
        u1 = _shift_down(u, 1, rows)
        u2 = _shift_down(u, 2, rows)
        w = w_ref[...]
        c = w[0:1] * u2 + w[1:2] * u1 + w[2:3] * u
        dyv = dy_ref[...].astype(F32)
        dc = dyv * cb_ref[...].astype(F32)
        du = w[2:3] * dc + w[1:2] * _shift_up(dc, 1, rows, s) + w[0:1] * _shift_up(dc, 2, rows, s)
        def window(p, jj):
            start = pl.multiple_of((C_CB, C_CC, C_CX)[p] + jj * CONV_TC, CONV_TC)
            return dproj_ref.at[:, pl.ds(start, CONV_TC)]

        tiles = ((dyv * c).astype(BF16), (du * cx).astype(BF16), (du * cc).astype(BF16))
        _write_behind(j, nb, buf, sems, tiles, window, j)
        dw_ref[...] = jnp.concatenate(
            [jnp.sum(dc * u2, axis=0, keepdims=True), jnp.sum(dc * u1, axis=0, keepdims=True),
             jnp.sum(dc * u, axis=0, keepdims=True)], axis=0)

    return _call(
        body, name=name, grid=(nb,),
        in_specs=[pl.BlockSpec((s, CONV_TC), lambda j: (0, j)), col(C_CB), col(C_CC), col(C_CX),
                  pl.BlockSpec((3, CONV_TC), lambda j: (0, j))] + [HBM_SPEC] * (1 + len(after)),
        out_specs=[pl.BlockSpec(memory_space=pl.ANY), pl.BlockSpec((3, CONV_TC), lambda j: (0, j))],
        out_shape=[_sds((s, N_IN), BF16), _sds((3, D), F32)],
        scratch_shapes=[pltpu.VMEM((2, 3, s, CONV_TC), BF16), pltpu.SemaphoreType.DMA((2, 3))],
        input_output_aliases={5: 0}, compiler_params=_params("arbitrary"),
    )(dy, proj, proj, proj, conv_w, dproj, *after)


def _rope_tables(s):
    half = ROT_DIM // 2
    inv_freq = ROPE_THETA ** (-jnp.arange(0, ROT_DIM, 2, dtype=F32) / ROT_DIM)
    inv64 = jnp.concatenate([inv_freq, inv_freq, jnp.zeros((HEAD_DIM - ROT_DIM,), F32)])
    ang = jnp.arange(s, dtype=F32)[:, None] * jnp.concatenate([inv64, inv64])[None, :]
    d = lax.broadcasted_iota(jnp.int32, (s, 128), 1) % HEAD_DIM
    cos, sin = jnp.cos(ang), jnp.sin(ang)
    c = jnp.where(d < ROT_DIM, cos, 1.0)
    a = jnp.where(d < half, -sin, 0.0)
    b = jnp.where((d >= half) & (d < ROT_DIM), sin, 0.0)
    return jnp.concatenate([c, a, b], axis=1)


def _rope(x, tab):
    c, a, b = tab[:, 0:128], tab[:, 128:256], tab[:, 256:384]
    outs = []
    for i in range(x.shape[1] // 128):
        xc = x[:, i * 128:(i + 1) * 128]
        outs.append(xc * c + pltpu.roll(xc, 120, 1) * a + pltpu.roll(xc, 8, 1) * b)
    return outs[0] if len(outs) == 1 else jnp.concatenate(outs, axis=1)


def _rope_t(dx, tab):
    c, a, b = tab[:, 0:128], tab[:, 128:256], tab[:, 256:384]
    outs = []
    for i in range(dx.shape[1] // 128):
        dc = dx[:, i * 128:(i + 1) * 128]
        outs.append(dc * c + pltpu.roll(dc * a, 8, 1) + pltpu.roll(dc * b, 120, 1))
    return outs[0] if len(outs) == 1 else jnp.concatenate(outs, axis=1)


def _attn_in_specs():
    prev = lambda n: jnp.maximum(n - 1, 0)
    return [
        pl.BlockSpec((BLOCK, D), lambda n: (n, C_Q // D)),
        pl.BlockSpec((BLOCK, D_KV), lambda n: (n, C_K // D_KV)),
        pl.BlockSpec((BLOCK, D_KV), lambda n: (prev(n), C_K // D_KV)),
        pl.BlockSpec((BLOCK, D_KV), lambda n: (n, C_V // D_KV)),
        pl.BlockSpec((BLOCK, D_KV), lambda n: (prev(n), C_V // D_KV)),
        pl.BlockSpec((BLOCK, 384), lambda n: (n, 0)),
        pl.BlockSpec((BLOCK, 384), lambda n: (prev(n), 0)),
        pl.BlockSpec(memory_space=pltpu.SMEM),
    ]


HALF = HEAD_DIM
N_CHUNK = D // 128


def _swa_bias(n):
    qi = lax.broadcasted_iota(jnp.int32, (BLOCK, 2 * BLOCK), 0)
    kj = lax.broadcasted_iota(jnp.int32, (BLOCK, 2 * BLOCK), 1)
    rel = qi + BLOCK - kj
    valid = (rel >= 0) & (rel < BLOCK) & ((kj >= BLOCK) | (n > 0))
    return jnp.where(valid, 0.0, NEG_INF)


def _halves(x):
    lo = lax.broadcasted_iota(jnp.int32, x.shape, 1) < HALF
    return jnp.where(lo, x, 0.0).astype(BF16), jnp.where(lo, 0.0, x).astype(BF16)


def _dup_heads(x):
    out = []
    for pair in range(N_KV // 2):
        xc = x[:, pair * 128:(pair + 1) * 128]
        xr = pltpu.roll(xc, HALF, 1)
        lo = lax.broadcasted_iota(jnp.int32, xc.shape, 1) < HALF
        out += [jnp.where(lo, xc, xr), jnp.where(lo, xr, xc)]
    return out


def _swa_load(q_ref, kc_ref, kp_ref, vc_ref, vp_ref, tc_ref, tp_ref):
    qf = _rope(q_ref[...].astype(F32), tc_ref[...]) * ATTN_SCALE
    q_halves = [_halves(qf[:, c * 128:(c + 1) * 128]) for c in range(N_CHUNK)]
    kf = jnp.concatenate([_rope(kp_ref[...].astype(F32), tp_ref[...]), _rope(kc_ref[...].astype(F32), tc_ref[...])], axis=0)
    vf = jnp.concatenate([vp_ref[...], vc_ref[...]], axis=0).astype(F32)
    return q_halves, _dup_heads(kf), _dup_heads(vf)


def _swa_probs(qh, kk, bias, sink):
    s = lax.dot_general(qh, kk, NT, preferred_element_type=F32) + bias
    m = jnp.maximum(jnp.max(jnp.maximum(s[:, :BLOCK], s[:, BLOCK:]), axis=1, keepdims=True), sink)
    return jnp.exp(s - m), m


def _swa_fwd(proj, tab, sinks, name, after=()):
    s = proj.shape[0]

    def body(q_ref, kc_ref, kp_ref, vc_ref, vp_ref, tc_ref, tp_ref, sink_ref, *rest):
        o_ref = rest[-1]
        n = pl.program_id(0)
        q_halves, kdup, vdup = _swa_load(q_ref, kc_ref, kp_ref, vc_ref, vp_ref, tc_ref, tp_ref)
        bias = _swa_bias(n)
        ones = jnp.ones((2 * BLOCK, 128), BF16)
        kk = [k.astype(BF16) for k in kdup]
        vv = [[jnp.concatenate([v_half, ones], axis=1) for v_half in _halves(v)] for v in vdup]
        heads = [(c, half) for c in range(N_CHUNK) for half in range(2)]
        scores = [lax.dot_general(q_halves[c][half], kk[c // (GROUP // 2)], NT, preferred_element_type=F32)
                  for c, half in heads]
        probs = []
        for (c, half), sc in zip(heads, scores):
            sc = sc + bias
            m = jnp.maximum(jnp.max(jnp.maximum(sc[:, :BLOCK], sc[:, BLOCK:]), axis=1, keepdims=True), sink_ref[0, 2 * c + half])
            probs.append((jnp.exp(sc - m).astype(BF16), jnp.exp(sink_ref[0, 2 * c + half] - m)))
        outs = [lax.dot_general(e, vv[c // (GROUP // 2)][half], NN, preferred_element_type=F32)
                for (c, half), (e, _) in zip(heads, probs)]
        for c in range(N_CHUNK):
            parts = [outs[2 * c + half][:, :128] * (1.0 / (outs[2 * c + half][:, 128:] + probs[2 * c + half][1]))
                     for half in range(2)]
            o_ref[:, c * 128:(c + 1) * 128] = (parts[0] + parts[1]).astype(BF16)

    return _call(
        body, name=name, grid=(s // BLOCK,), in_specs=_attn_in_specs() + [HBM_SPEC] * len(after),
        out_specs=pl.BlockSpec((BLOCK, D), lambda n: (n, 0)), out_shape=_sds((s, D), BF16),
        compiler_params=_params("parallel"),
    )(proj, proj, proj, proj, proj, tab, tab, sinks, *after)


def _swa_bwd(do, proj, tab, sinks, dproj, name, after=()):
    s = proj.shape[0]
    nblk = s // BLOCK
    kv_of = lambda c: c // (GROUP // 2)

    def body(do_ref, q_ref, kc_ref, kp_ref, vc_ref, vp_ref, tc_ref, tp_ref, sink_ref, *rest):
        dproj_ref, dk_ref, dv_ref, ds_ref, dqout, dkbuf, dvbuf, sems = rest[1 + len(after):]
        n = pl.program_id(0)

        @pl.when(n == 0)
        def _():
            dk_ref[...] = jnp.zeros_like(dk_ref)
            dv_ref[...] = jnp.zeros_like(dv_ref)
            ds_ref[...] = jnp.zeros_like(ds_ref)

        q_halves, kdup, vdup = _swa_load(q_ref, kc_ref, kp_ref, vc_ref, vp_ref, tc_ref, tp_ref)
        dof = do_ref[...].astype(F32)
        do_halves = [_halves(dof[:, c * 128:(c + 1) * 128]) for c in range(N_CHUNK)]
        bias = _swa_bias(n)
        ones = jnp.ones((2 * BLOCK, 128), BF16)
        kk = [k.astype(BF16) for k in kdup]
        vv = [v.astype(BF16) for v in vdup]
        k_halves = [_halves(k) for k in kdup]
        heads = [(c, half) for c in range(N_CHUNK) for half in range(2)]
        lane_row = lax.broadcasted_iota(jnp.int32, (1, 128), 1)
        lo_kv = lax.broadcasted_iota(jnp.int32, (2 * BLOCK, 128), 1) < HALF
        scores = [lax.dot_general(q_halves[c][half], kk[kv_of(c)], NT, preferred_element_type=F32) for c, half in heads]
        dps = [lax.dot_general(do_halves[c][half], vv[kv_of(c)], NT, preferred_element_type=F32) for c, half in heads]
        exps = []
        for (c, half), sc in zip(heads, scores):
            sink = sink_ref[0, 2 * c + half]
            sc = sc + bias
            m = jnp.maximum(jnp.max(jnp.maximum(sc[:, :BLOCK], sc[:, BLOCK:]), axis=1, keepdims=True), sink)
            exps.append((jnp.exp(sc - m), jnp.exp(sink - m)))
        sums = [lax.dot_general(e.astype(BF16), ones, NN, preferred_element_type=F32) for e, _ in exps]
        dsink_row = jnp.zeros((1, 128), F32)
        dsb, pb = [], []
        for h, ((e, es), row_sum, dp) in enumerate(zip(exps, sums, dps)):
            inv = 1.0 / (row_sum + es)
            p = e * jnp.concatenate([inv, inv], axis=1)
            t = p * dp
            delta = jnp.sum(t, axis=1, keepdims=True)
            dsb.append((t - p * delta).astype(BF16))
            pb.append(p.astype(BF16))
            dsink = -jnp.sum(es * inv * delta, axis=0, keepdims=True)
            dsink_row = dsink_row + jnp.where(lane_row == h, dsink, 0.0)
        dq_parts = [lax.dot_general(d, k_halves[kv_of(c)][half], NN, preferred_element_type=F32) for (c, half), d in zip(heads, dsb)]
        dk_parts = [lax.dot_general(d, q_halves[c][half], TN, preferred_element_type=F32) for (c, half), d in zip(heads, dsb)]
        dv_parts = [lax.dot_general(p, do_halves[c][half], TN, preferred_element_type=F32) for (c, half), p in zip(heads, pb)]
        dq = jnp.concatenate([(dq_parts[2 * c] + dq_parts[2 * c + 1]) * ATTN_SCALE for c in range(N_CHUNK)], axis=1)

        def kv_sum(parts, hk):
            acc = (parts[GROUP * hk] + parts[GROUP * hk + 1]) + (parts[GROUP * hk + 2] + parts[GROUP * hk + 3])
            return acc + pltpu.roll(acc, HALF, 1)

        for pair in range(N_KV // 2):
            dkbuf[:, pair * 128:(pair + 1) * 128] = jnp.where(lo_kv, kv_sum(dk_parts, 2 * pair), kv_sum(dk_parts, 2 * pair + 1))
            dvbuf[:, pair * 128:(pair + 1) * 128] = jnp.where(lo_kv, kv_sum(dv_parts, 2 * pair), kv_sum(dv_parts, 2 * pair + 1))
        prev0 = pl.multiple_of(jnp.maximum(n - 1, 0) * BLOCK, BLOCK)
        cur0 = pl.multiple_of(n * BLOCK, BLOCK)

        @pl.when(n > 0)
        def _():
            dk_ref[pl.ds(prev0, BLOCK), :] += dkbuf[0:BLOCK, :]
            dv_ref[pl.ds(prev0, BLOCK), :] += dvbuf[0:BLOCK, :]

        dk_ref[pl.ds(cur0, BLOCK), :] += dkbuf[BLOCK:2 * BLOCK, :]
        dv_ref[pl.ds(cur0, BLOCK), :] += dvbuf[BLOCK:2 * BLOCK, :]
        ds_ref[...] += dsink_row

        def window(p, at):
            return dproj_ref.at[pl.ds(pl.multiple_of(at * BLOCK, BLOCK), BLOCK), pl.ds(C_Q, D)]

        _write_behind(n, nblk, dqout, sems, (_rope_t(dq, tc_ref[...]).astype(BF16),), window, n)

    blk = lambda w: pl.BlockSpec((BLOCK, w), lambda n: (n, 0))
    whole = lambda w: pl.BlockSpec((s, w), lambda n: (0, 0))
    n_in = 1 + len(_attn_in_specs())
    return _call(
        body, name=name, grid=(nblk,), in_specs=[blk(D)] + _attn_in_specs() + [HBM_SPEC] * (1 + len(after)),
        out_specs=[HBM_SPEC, whole(D_KV), whole(D_KV), pl.BlockSpec((1, 128), lambda n: (0, 0))],
        out_shape=[_sds((s, N_IN), BF16), _sds((s, D_KV), F32), _sds((s, D_KV), F32), _sds((1, 128), F32)],
        scratch_shapes=[pltpu.VMEM((2, 1, BLOCK, D), BF16), pltpu.VMEM((2 * BLOCK, D_KV), F32),
                        pltpu.VMEM((2 * BLOCK, D_KV), F32), pltpu.SemaphoreType.DMA((2, 1))],
        input_output_aliases={n_in: 0}, compiler_params=_params("arbitrary"),
    )(do, proj, proj, proj, proj, proj, tab, tab, sinks, dproj, *after)


def _kv_bwd(dkr, dv, tab, dproj, name):
    s = dkr.shape[0]
    tm = _row_tile(s)

    def body(dk_ref, dv_ref, t_ref, dproj_in, o_ref):
        del dproj_in
        o_ref[:, 0:D_KV] = _rope_t(dk_ref[...], t_ref[...]).astype(BF16)
        o_ref[:, D_KV:2 * D_KV] = dv_ref[...].astype(BF16)

    row = lambda w: pl.BlockSpec((tm, w), lambda i: (i, 0))
    return _call(
        body, name=name, grid=(s // tm,),
        in_specs=[row(D_KV), row(D_KV), row(384), pl.BlockSpec(memory_space=pl.ANY)],
        out_specs=pl.BlockSpec((tm, 2 * D_KV), lambda i: (i, C_K // (2 * D_KV))),
        out_shape=_sds((s, N_IN), BF16), input_output_aliases={3: 0}, compiler_params=_params("parallel"),
    )(dkr, dv, tab, dproj)


EW_TC = 512


def _sigmoid(x):
    return 0.5 * jnp.tanh(0.5 * x) + 0.5


def _merge_fwd(proj, conv_out, attn_out, name):
    s = proj.shape[0]
    tm = _row_tile(s)
    tile = pl.BlockSpec((tm, EW_TC), lambda i, j: (i, j))

    def body(gc_ref, ga_ref, co_ref, ao_ref, o_ref):
        o_ref[...] = (_sigmoid(gc_ref[...].astype(F32)) * co_ref[...].astype(F32)
                      + _sigmoid(ga_ref[...].astype(F32)) * ao_ref[...].astype(F32)).astype(BF16)

    return _call(
        body, name=name, grid=(s // tm, D // EW_TC),
        in_specs=[pl.BlockSpec((tm, EW_TC), lambda i, j: (i, C_GC // EW_TC + j)),
                  pl.BlockSpec((tm, EW_TC), lambda i, j: (i, C_GA // EW_TC + j)), tile, tile],
        out_specs=tile, out_shape=_sds((s, D), BF16), compiler_params=_params("parallel", "parallel"),
    )(proj, proj, conv_out, attn_out)


def _merge_bwd(dmerged, proj, conv_out, attn_out, name):
    s = proj.shape[0]
    tm = _row_tile(s)
    tile = pl.BlockSpec((tm, EW_TC), lambda i, j: (i, j))
    anyspec = pl.BlockSpec(memory_space=pl.ANY)

    def body(dm_ref, gc_ref, ga_ref, co_ref, ao_ref, dproj_ref, dco_ref, dao_ref, buf, sems):
        i, j = pl.program_id(0), pl.program_id(1)
        dm = dm_ref[...].astype(F32)
        sc = _sigmoid(gc_ref[...].astype(F32))
        sa = _sigmoid(ga_ref[...].astype(F32))
        dco_ref[...] = (dm * sc).astype(BF16)
        dao_ref[...] = (dm * sa).astype(BF16)
        tiles = ((dm * co_ref[...].astype(F32) * sc * (1.0 - sc)).astype(BF16),
                 (dm * ao_ref[...].astype(F32) * sa * (1.0 - sa)).astype(BF16))

        def window(p, at):
            start = pl.multiple_of((C_GC, C_GA)[p] + at[1] * EW_TC, EW_TC)
            return dproj_ref.at[pl.ds(pl.multiple_of(at[0] * tm, tm), tm), pl.ds(start, EW_TC)]

        _write_behind(i * nj + j, (s // tm) * nj, buf, sems, tiles, window, (i, j))

    nj = D // EW_TC
    return _call(
        body, name=name, grid=(s // tm, nj),
        in_specs=[tile, pl.BlockSpec((tm, EW_TC), lambda i, j: (i, C_GC // EW_TC + j)),
                  pl.BlockSpec((tm, EW_TC), lambda i, j: (i, C_GA // EW_TC + j)), tile, tile],
        out_specs=[anyspec, tile, tile],
        out_shape=[_sds((s, N_IN), BF16), _sds((s, D), BF16), _sds((s, D), BF16)],
        scratch_shapes=[pltpu.VMEM((2, 2, tm, EW_TC), BF16), pltpu.SemaphoreType.DMA((2, 2))],
        compiler_params=_params("arbitrary", "arbitrary"),
    )(dmerged, proj, proj, conv_out, attn_out)


FF_TC = 256


ROW_SPLIT = 4
TAIL_ROWS = 64


def _row_pipeline(tm, matmul, finish):
    step = tm // ROW_SPLIT
    tail = min(TAIL_ROWS, step)

    def finish_block(r, result):
        parts = result if isinstance(result, tuple) else (result,)
        for t in range(step // tail):
            piece = tuple(p[t * tail:(t + 1) * tail] for p in parts)
            finish(pl.ds(r * step + t * tail, tail), piece if isinstance(result, tuple) else piece[0])

    pending = None
    for r in range(ROW_SPLIT):
        result = matmul(pl.ds(r * step, step))
        if pending is not None:
            finish_block(*pending)
        pending = (r, result)
    finish_block(*pending)


def _gate_up_fwd(h2, wgu_t, name):
    s = h2.shape[0]
    tm = min(2048, s)
    nb = D_FF // FF_TC

    def body(h_ref, wg_ref, wu_ref, a_ref, dadu_ref, dadg_ref):
        def matmuls(rows):
            h = h_ref[rows, :]
            return (lax.dot_general(h, wg_ref[...], NT, preferred_element_type=F32),
                    lax.dot_general(h, wu_ref[...], NT, preferred_element_type=F32))

        def finish(rows, gu):
            g, u = gu
            sg = _sigmoid(g)
            silu = g * sg
            a_ref[rows, :] = (silu * u).astype(BF16)
            dadu_ref[rows, :] = silu.astype(BF16)
            dadg_ref[rows, :] = (u * (sg * (1.0 + g * (1.0 - sg)))).astype(BF16)

        _row_pipeline(tm, matmuls, finish)

    tile = pl.BlockSpec((tm, FF_TC), lambda i, j: (i, j))
    return _call(
        body, name=name, grid=(s // tm, nb),
        in_specs=[pl.BlockSpec((tm, D), lambda i, j: (i, 0)), pl.BlockSpec((FF_TC, D), lambda i, j: (j, 0)),
                  pl.BlockSpec((FF_TC, D), lambda i, j: (nb + j, 0))],
        out_specs=[tile, tile, tile], out_shape=[_sds((s, D_FF), BF16)] * 3,
        compiler_params=_params("parallel", "parallel"),
    )(h2, wgu_t, wgu_t)


def _down_bwd_x(dx2b, wd, dadg, dadu, name):
    s = dx2b.shape[0]
    tm = min(2048, s)
    nb = D_FF // FF_TC

    def body(dx_ref, w_ref, dadg_ref, dadu_ref, dg_ref, du_ref):
        def matmul(rows):
            return lax.dot_general(dx_ref[rows, :], w_ref[...], NT, preferred_element_type=F32)

        def finish(rows, da):
            dg_ref[rows, :] = (da * dadg_ref[rows, :].astype(F32)).astype(BF16)
            du_ref[rows, :] = (da * dadu_ref[rows, :].astype(F32)).astype(BF16)

        _row_pipeline(tm, matmul, finish)

    tile = pl.BlockSpec((tm, FF_TC), lambda i, j: (i, j))
    return _call(
        body, name=name, grid=(s // tm, nb),
        in_specs=[pl.BlockSpec((tm, D), lambda i, j: (i, 0)), pl.BlockSpec((FF_TC, D), lambda i, j: (j, 0)), tile, tile],
        out_specs=[tile, tile], out_shape=[_sds((s, D_FF), BF16)] * 2,
        compiler_params=_params("parallel", "parallel"),
    )(dx2b, wd, dadg, dadu)


class _Weights:
    def __init__(self, **groups):
        self.groups = groups

    def begin(self, group, after):
        return ()

    def end(self, group, after):
        return self.groups[group]


class _NoReduce:
    def start(self, group, grads):
        return ()

    def middle(self, group, after):
        return ()


def _local_step(x, tgt, g_mix, g_ffn, g_final, sinks, weights, reducer=None, after=()):
    reducer = reducer or _NoReduce()
    s = x.shape[0]
    tab = _rope_tables(s)
    big = dict(tm=1024, tn=512, tk=1024)
    h1 = _rms_fwd(x, g_mix, "rms1_fwd", after=after)
    win_t, conv_w = weights.end("in", weights.begin("in", (h1,)))
    proj = _matmul(h1, win_t, mode="nt", out_dtype=BF16, name="proj_fwd", tm=2048, tn=512, tk=1024)
    attn = _swa_fwd(proj, tab, sinks, "attn_fwd", after=weights.begin("mix", (proj,)))
    wco, wao, wo = weights.end("mix", (attn,))
    conv_y = _conv_fwd(proj, conv_w, "conv_fwd")
    conv_out = _matmul(conv_y, wco, mode="nn", out_dtype=BF16, name="conv_out_fwd", **big)
    attn_out = _matmul(attn, wao, mode="nn", out_dtype=BF16, name="attn_out_fwd", **big)
    merged = _merge_fwd(proj, conv_out, attn_out, "merge_fwd")
    x1 = _matmul(merged, wo, mode="nn", out_dtype=F32, name="wo_fwd", res=x, after=weights.begin("ffn", (merged,)), **big)
    h2 = _rms_fwd(x1, g_ffn, "rms2_fwd")
    wgu_t, wd = weights.end("ffn", (h2,))
    act, dadu, dadg = _gate_up_fwd(h2, wgu_t, "gate_up_fwd")
    x2 = _matmul(act, wd, mode="nn", out_dtype=F32, name="down_fwd", res=x1, tm=1024, tn=512, tk=D_FF)
    dx2, dx2b, dg_final, lossvec = _loss_head(x2, g_final, tgt, "loss_head")
    dgate, dup = _down_bwd_x(dx2b, wd, dadg, dadu, "down_bwd_x")
    g_wd = _matmul(act, dx2b, mode="tn", out_dtype=BF16, name="down_bwd_w", tm=1408, tn=1024, tk=2048)
    dh2 = _matmul([dgate, dup], wgu_t, mode="nn", out_dtype=F32, name="gate_up_bwd_x", tm=1024, tn=1024, tk=1408)
    g_wgu_t = _matmul([dgate, dup], h2, mode="tn", out_dtype=BF16, name="gate_up_bwd_w", tm=1408, tn=1024, tk=2048)
    after_ffn = reducer.start("ffn", dict(wgu_t=g_wgu_t, wd=g_wd))
    dx1, dx1b, dg_ffn = _rms_bwd(dh2, x1, g_ffn, dx2, "rms2_bwd")
    dmerged = _matmul(dx1b, wo, mode="nt", out_dtype=BF16, name="wo_bwd_x", after=after_ffn, **big)
    after_ffn = reducer.middle("ffn", (dmerged,))
    g_wo = _matmul(merged, dx1b, mode="tn", out_dtype=BF16, name="wo_bwd_w", tm=512, tn=1024, tk=2048, after=after_ffn)
    dproj, dco, dao = _merge_bwd(dmerged, proj, conv_out, attn_out, "merge_bwd")
    dconv_y = _matmul(dco, wco, mode="nt", out_dtype=BF16, name="conv_out_bwd_x", **big)
    g_wco = _matmul(conv_y, dco, mode="tn", out_dtype=BF16, name="conv_out_bwd_w", tm=512, tn=1024, tk=2048)
    dattn = _matmul(dao, wao, mode="nt", out_dtype=BF16, name="attn_out_bwd_x", **big)
    g_wao = _matmul(attn, dao, mode="tn", out_dtype=BF16, name="attn_out_bwd_w", tm=512, tn=1024, tk=2048)
    after_mix = reducer.start("mix", dict(wco=g_wco, wao=g_wao, wo=g_wo))
    dproj, dconv_w = _conv_bwd(dconv_y, proj, conv_w, dproj, "conv_bwd", after=after_mix)
    after_mix = reducer.middle("mix", (dconv_w,))
    dproj, dkr, dv, dsinks = _swa_bwd(dattn, proj, tab, sinks, dproj, "attn_bwd", after=after_mix)
    dproj = _kv_bwd(dkr, dv, tab, dproj, "kv_bwd")
    g_win_t = _matmul(dproj, h1, mode="tn", out_dtype=BF16, name="proj_bwd_w", tm=512, tn=1024, tk=2048)
    after_in = reducer.middle("in", reducer.start("in", dict(win_t=g_win_t)))
    dh1 = _matmul(dproj, win_t, mode="nn", out_dtype=F32, name="proj_bwd_x", tm=1024, tn=1024, tk=1664, after=after_in)
    dx, _, dg_mix = _rms_bwd(dh1, x, g_mix, dx1, "rms1_bwd")
    grads = dict(win_t=g_win_t, wgu_t=g_wgu_t, wd=g_wd, wco=g_wco, wao=g_wao, wo=g_wo)
    small = dict(g_mix=dg_mix, g_ffn=dg_ffn, g_final=dg_final, conv_w=dconv_w, sinks=dsinks, lossvec=lossvec)
    return dx, grads, small


def _position():
    return lax.axis_index("x"), lax.axis_index("y"), lax.axis_index("c")


def _other_chips(x, y):
    return [(1 - x, y), (x, 1 - y), (1 - x, 1 - y)]


SEM_SPEC = pl.BlockSpec(memory_space=pltpu.SEMAPHORE)
EFFECT = pltpu.SideEffectType.DATAFLOW_SIDE_EFFECTING
TOKEN = jax.ShapeDtypeStruct((8, 128), F32)
TOKEN_SPEC = pl.BlockSpec(memory_space=pltpu.VMEM)


def _hbm(a):
    return pltpu.with_memory_space_constraint(a, pltpu.HBM)


def _place(w, me_idx, dtype, name, after=()):
    r, cdim = w.shape

    def body(i_ref, w_ref, *rest):
        rest[-1][...] = w_ref[...].astype(dtype)

    grid_spec = pltpu.PrefetchScalarGridSpec(
        num_scalar_prefetch=1, grid=(1,), in_specs=[pl.BlockSpec((r, cdim), lambda i, me: (0, 0))] + [HBM_SPEC] * len(after),
        out_specs=pl.BlockSpec((r, cdim), lambda i, me: (me[0], 0)))
    return _call(body, name=name, grid_spec=grid_spec, out_shape=_sds((N_DEV * r, cdim), dtype),
                 compiler_params=_params("arbitrary"))(me_idx, w, *after)


def _own_rows(ref, r, px, py, pc):
    return ref.at[pl.ds((4 * px + 2 * py + pc) * r, r), :]


def _gather_start(bufs, groups, name):
    n = len(bufs)
    rows = [b.shape[0] // N_DEV for b in bufs]
    ng = len(groups)

    def body(*refs):
        ins = refs[:n]
        sems = refs[n:n + 2 * ng]
        token = refs[-1]
        x, y, c = _position()
        targets = [(x, y, 1 - c)] + [(*chip, c) for chip in _other_chips(x, y)]
        for g, members in enumerate(groups):
            for slot, a in enumerate(members):
                own = _own_rows(ins[a], rows[a], x, y, c)
                for to in targets:
                    pltpu.make_async_remote_copy(src_ref=own, dst_ref=own, send_sem=sems[2 * g].at[slot],
                                                 recv_sem=sems[2 * g + 1].at[slot], device_id=to, device_id_type=MESH).start()
        token[...] = jnp.zeros_like(token)

    sem_shapes = []
    for members in groups:
        sem_shapes += [pltpu.SemaphoreType.DMA((len(members),))] * 2
    outs = _call(
        body, name=name, in_specs=[HBM_SPEC] * n, out_specs=[SEM_SPEC] * (2 * ng) + [HBM_SPEC] * n + [TOKEN_SPEC],
        out_shape=sem_shapes + [pltpu.HBM(b.shape, b.dtype) for b in bufs] + [TOKEN],
        input_output_aliases={i: 2 * ng + i for i in range(n)},
        compiler_params=pltpu.CompilerParams(has_side_effects=EFFECT),
    )(*[_hbm(b) for b in bufs])
    sem_pairs = [(outs[2 * g], outs[2 * g + 1]) for g in range(ng)]
    return sem_pairs, list(outs[2 * ng:2 * ng + n]), outs[-1]


def _gather_forward(send_sems, recv_sems, bufs, after, name):
    n = len(bufs)
    rows = [b.shape[0] // N_DEV for b in bufs]

    def body(*refs):
        ins = refs[:n]
        send1, recv1 = refs[n], refs[n + 1]
        out0 = n + 2 + len(after)
        send2, recv2 = refs[out0], refs[out0 + 1]
        token = refs[-1]
        x, y, c = _position()
        for a in range(n):
            step1 = pltpu.make_async_remote_copy(
                src_ref=_whole(ins[a], 4 * rows[a]), dst_ref=_whole(ins[a], 4 * rows[a]), send_sem=send1.at[a],
                recv_sem=recv1.at[a], device_id=(x, y, c), device_id_type=MESH)
            step1.wait_send()
            step1.wait_recv()
        for a in range(n):
            for chip in _other_chips(x, y):
                blk = _own_rows(ins[a], rows[a], *chip, c)
                pltpu.make_async_remote_copy(src_ref=blk, dst_ref=blk, send_sem=send2.at[a], recv_sem=recv2.at[a],
                                             device_id=(x, y, 1 - c), device_id_type=MESH).start()
        token[...] = jnp.zeros_like(token)

    outs = _call(
        body, name=name, in_specs=[HBM_SPEC] * n + [SEM_SPEC, SEM_SPEC] + [HBM_SPEC] * len(after),
        out_specs=[SEM_SPEC, SEM_SPEC] + [HBM_SPEC] * n + [TOKEN_SPEC],
        out_shape=[pltpu.SemaphoreType.DMA((n,)), pltpu.SemaphoreType.DMA((n,))]
        + [pltpu.HBM(b.shape, b.dtype) for b in bufs] + [TOKEN],
        input_output_aliases={i: 2 + i for i in range(n)},
        compiler_params=pltpu.CompilerParams(has_side_effects=EFFECT),
    )(*bufs, send_sems, recv_sems, *after)
    return outs[0], outs[1], list(outs[2:2 + n]), outs[-1]


def _gather_done(send_sems, recv_sems, bufs, after, name):
    n = len(bufs)
    rows = [b.shape[0] // N_DEV for b in bufs]

    def body(*refs):
        ins = refs[:n]
        send2, recv2 = refs[n], refs[n + 1]
        x, y, c = _position()
        for a in range(n):
            step2 = pltpu.make_async_remote_copy(
                src_ref=_whole(ins[a], 3 * rows[a]), dst_ref=_whole(ins[a], 3 * rows[a]), send_sem=send2.at[a],
                recv_sem=recv2.at[a], device_id=(x, y, c), device_id_type=MESH)
            step2.wait_send()
            step2.wait_recv()

    outs = _call(
        body, name=name, in_specs=[HBM_SPEC] * n + [SEM_SPEC, SEM_SPEC] + [HBM_SPEC] * len(after),
        out_specs=[HBM_SPEC] * n, out_shape=[pltpu.HBM(b.shape, b.dtype) for b in bufs],
        input_output_aliases={i: i for i in range(n)},
        compiler_params=pltpu.CompilerParams(has_side_effects=EFFECT),
    )(*bufs, send_sems, recv_sems, *after)
    return list(outs)


def _whole(ref, nrows):
    return ref.at[pl.ds(0, nrows), :]


def _to_sibling(x, y, c):
    return [(2 * q + (1 - c), q, (x, y, 1 - c)) for q in range(4)]


def _to_chips(x, y, c):
    return [(2 * px + py, j, (px, py, c)) for j, (px, py) in enumerate(_other_chips(x, y))]


def _exchange_start(srcs, src_slots, plan, name):
    n = len(srcs)
    rows = [a.shape[0] // src_slots for a in srcs]
    n_copies = len(plan(0, 0, 0))
    lands = [lax.empty((n_copies * r, a.shape[1]), a.dtype) for a, r in zip(srcs, rows)]

    def body(*refs):
        ins, land_refs = refs[:n], refs[n:2 * n]
        send_sems, recv_sems = refs[2 * n], refs[2 * n + 1]
        token = refs[-1]
        for a in range(n):
            r = rows[a]
            for src_slot, dst_slot, target in plan(*_position()):
                pltpu.make_async_remote_copy(
                    src_ref=ins[a].at[pl.ds(src_slot * r, r), :], dst_ref=land_refs[a].at[pl.ds(dst_slot * r, r), :],
                    send_sem=send_sems.at[a], recv_sem=recv_sems.at[a], device_id=target, device_id_type=MESH).start()
        token[...] = jnp.zeros_like(token)

    outs = _call(
        body, name=name, in_specs=[HBM_SPEC] * (2 * n),
        out_specs=[SEM_SPEC, SEM_SPEC] + [HBM_SPEC] * (2 * n) + [TOKEN_SPEC],
        out_shape=[pltpu.SemaphoreType.DMA((n,)), pltpu.SemaphoreType.DMA((n,))]
        + [pltpu.HBM(a.shape, a.dtype) for a in srcs] + [pltpu.HBM(l.shape, l.dtype) for l in lands] + [TOKEN],
        input_output_aliases={i: 2 + i for i in range(2 * n)},
        compiler_params=pltpu.CompilerParams(has_side_effects=EFFECT),
    )(*[_hbm(a) for a in srcs], *[_hbm(l) for l in lands])
    return outs[0], outs[1], list(outs[2:2 + n]), list(outs[2 + n:2 + 2 * n]), outs[-1]


def _exchange_wait(send_sems, recv_sems, srcs, lands, after, name):
    n = len(srcs)

    def body(*refs):
        ins, land_refs = refs[:n], refs[n:2 * n]
        send_sems_ref, recv_sems_ref = refs[2 * n], refs[2 * n + 1]
        for a in range(n):
            allrows = lands[a].shape[0]
            cp = pltpu.make_async_remote_copy(
                src_ref=_whole(ins[a], allrows), dst_ref=_whole(land_refs[a], allrows), send_sem=send_sems_ref.at[a],
                recv_sem=recv_sems_ref.at[a], device_id=_position(), device_id_type=MESH)
            cp.wait_send()
            cp.wait_recv()

    outs = _call(
        body, name=name, in_specs=[HBM_SPEC] * (2 * n) + [SEM_SPEC, SEM_SPEC] + [HBM_SPEC] * len(after),
        out_specs=[HBM_SPEC] * (2 * n),
        out_shape=[pltpu.HBM(a.shape, a.dtype) for a in srcs] + [pltpu.HBM(l.shape, l.dtype) for l in lands],
        input_output_aliases={i: i for i in range(2 * n)},
        compiler_params=pltpu.CompilerParams(has_side_effects=EFFECT),
    )(*srcs, *lands, send_sems, recv_sems, *after)
    return list(outs[:n]), list(outs[n:])


def _chip_partial(grad, recv, idx, name):
    r = recv.shape[0] // 4

    def body(i_ref, g_ref, s_ref, o_ref):
        del i_ref
        o_ref[...] = (g_ref[...].astype(F32) + s_ref[...].astype(F32)).astype(BF16)

    nb = 1
    tr = r // nb
    grid_spec = pltpu.PrefetchScalarGridSpec(
        num_scalar_prefetch=1, grid=(3, nb),
        in_specs=[pl.BlockSpec((tr, D), lambda t, i, i_ref: ((2 * i_ref[1 + t] + i_ref[0]) * nb + i, 0)),
                  pl.BlockSpec((tr, D), lambda t, i, i_ref: (i_ref[1 + t] * nb + i, 0))],
        out_specs=pl.BlockSpec((tr, D), lambda t, i, i_ref: (i_ref[1 + t] * nb + i, 0)))
    return _call(body, name=name, grid_spec=grid_spec, out_shape=_sds((4 * r, D), BF16),
                 compiler_params=_params("arbitrary", "arbitrary"))(idx, grad, recv)


def _adamw_math(w, g, m, v):
    m2 = B1 * m + (1.0 - B1) * g
    v2 = B2 * v + (1.0 - B2) * jnp.square(g)
    m_hat = m2 / (1.0 - B1 ** STEP)
    v_hat = v2 / (1.0 - B2 ** STEP)
    return -LR * (m_hat / (jnp.sqrt(v_hat) + EPS_ADAM) + WD * w), m2, v2


def _reduce_adamw(w, grad, from_sibling, from_chips, idx, m, v, name):
    r = w.shape[0]
    assert grad.shape == (N_DEV * r, D) and from_sibling.shape == (4 * r, D) and from_chips.shape == (3 * r, D)
    tr = r // 2
    nb = r // tr

    def body(i_ref, w_ref, p_ref, s_ref, r0_ref, r1_ref, r2_ref, m_ref, v_ref, g_ref, d_ref, nm_ref, nv_ref):
        del i_ref
        g = p_ref[...].astype(F32) + s_ref[...].astype(F32)
        g = ((g + r0_ref[...].astype(F32)) + r1_ref[...].astype(F32)) + r2_ref[...].astype(F32)
        g_ref[...] = g
        d_ref[...], nm_ref[...], nv_ref[...] = _adamw_math(w_ref[...], g, m_ref[...], v_ref[...])

    own = pl.BlockSpec((tr, D), lambda i, i_ref: (i, 0))
    grid_spec = pltpu.PrefetchScalarGridSpec(
        num_scalar_prefetch=1, grid=(nb,),
        in_specs=[own, pl.BlockSpec((tr, D), lambda i, i_ref: (i_ref[0] * nb + i, 0)),
                  pl.BlockSpec((tr, D), lambda i, i_ref: (i_ref[1] * nb + i, 0))]
        + [pl.BlockSpec((tr, D), lambda i, i_ref, j=j: (j * nb + i, 0)) for j in range(3)] + [own, own],
        out_specs=[own] * 4)
    return _call(body, name=name, grid_spec=grid_spec, out_shape=[_sds((r, D), F32)] * 4,
                 compiler_params=_params("parallel"))(idx, w, grad, from_sibling, from_chips, from_chips, from_chips, m, v)


SMALL_ROWS = 8


def _small_all_reduce(pack, name, after=()):
    def body(p_ref, *rest):
        tot_ref, loss_ref, gath, send_sems, recv_sems = rest[len(after):]
        x, y, c = _position()
        me_id = 4 * x + 2 * y + c
        gath[me_id] = p_ref[...]
        copies = []
        for k in range(1, N_DEV):
            peer = tuple(1 - v if (k >> b) & 1 else v for v, b in ((x, 2), (y, 1), (c, 0)))
            cp = pltpu.make_async_remote_copy(src_ref=p_ref, dst_ref=gath.at[me_id], send_sem=send_sems.at[k - 1],
                                              recv_sem=recv_sems.at[k - 1], device_id=peer, device_id_type=MESH)
            cp.start()
            copies.append(cp)
        for cp in copies:
            cp.wait_recv()
        for cp in copies:
            cp.wait_send()
        tot = gath[0]
        for d in range(1, N_DEV):
            tot = tot + gath[d]
        tot_ref[...] = tot
        loss_ref[...] = jnp.full((1, 128), (0.5 / D) * jnp.sum(tot[SMALL_ROWS - 1:SMALL_ROWS, :]), F32)

    vm = pl.BlockSpec(memory_space=pltpu.VMEM)
    return _call(
        body, name=name, in_specs=[vm] + [HBM_SPEC] * len(after), out_specs=[vm, vm],
        out_shape=[_sds((SMALL_ROWS, D), F32), _sds((1, 128), F32)],
        scratch_shapes=[pltpu.VMEM((N_DEV, SMALL_ROWS, D), F32), pltpu.SemaphoreType.DMA((N_DEV - 1,)),
                        pltpu.SemaphoreType.DMA((N_DEV - 1,))],
    )(pack, *after)


def _adamw(w, g, m, v, name):
    r, cdim = w.shape
    tr = 256 if r % 256 == 0 else (r // 2 if r % 16 == 0 else r)

    def body(w_ref, g_ref, m_ref, v_ref, d_ref, nm_ref, nv_ref):
        d_ref[...], nm_ref[...], nv_ref[...] = _adamw_math(w_ref[...], g_ref[...], m_ref[...], v_ref[...])

    spec = pl.BlockSpec((tr, cdim), lambda i: (i, 0))
    return _call(
        body, name=name, grid=(r // tr,), in_specs=[spec] * 4, out_specs=[spec] * 3,
        out_shape=[_sds((r, cdim), F32)] * 3, compiler_params=_params("parallel"),
    )(w, g, m, v)


def kernel(x, g_mix, w_in, conv_w, attn_sinks, w_conv_out, w_attn_out, w_o, g_ffn, w_gate_up, w_down, g_final, loss_target, m_g_mix, m_w_in, m_conv_w, m_attn_sinks, m_w_conv_out, m_w_attn_out, m_w_o, m_g_ffn, m_w_gate_up, m_w_down, m_g_final, v_g_mix, v_w_in, v_conv_w, v_attn_sinks, v_w_conv_out, v_w_attn_out, v_w_o, v_g_ffn, v_w_gate_up, v_w_down, v_g_final):
    cx, cy, cc = _position()
    chip = 2 * cx + cy
    partial_idx = jnp.stack([cc, 2 * (1 - cx) + cy, 2 * cx + (1 - cy), 2 * (1 - cx) + (1 - cy)]).astype(jnp.int32)
    own_idx = jnp.stack([2 * chip + cc, chip]).astype(jnp.int32)
    me = 4 * cx + 2 * cy + cc

    me_idx = jnp.reshape(me, (1,)).astype(jnp.int32)
    first = [_place(jnp.transpose(w_in[0]), me_idx, BF16, "place_w_in"),
             _place(jnp.pad(conv_w[0], ((0, 5), (0, 0))), me_idx, F32, "place_conv_w")]
    (sems_in,), first, token_in = _gather_start(first, [[0, 1]], "gather_start_in")
    later = [_place(w, me_idx, BF16, "place_" + k, after=(token_in,)) for k, w in (
        ("w_conv_out", w_conv_out[0]), ("w_attn_out", w_attn_out[0]), ("w_o", w_o[0]),
        ("w_gate_up", jnp.transpose(w_gate_up[0])), ("w_down", w_down[0]))]
    (sems_mix, sems_ffn), later, token_later = _gather_start(later, [[0, 1, 2], [3, 4]], "gather_start_later")
    gather_tokens = (token_in, token_later)

    class Gathered:
        def __init__(self):
            self.state = {"in": (sems_in, first), "mix": (sems_mix, later[:3]), "ffn": (sems_ffn, later[3:])}

        def begin(self, group, after):
            (send_sems, recv_sems), group_bufs = self.state[group]
            send2, recv2, group_bufs, token = _gather_forward(send_sems, recv_sems, group_bufs, after, "gather_forward_" + group)
            self.state[group] = ((send2, recv2), group_bufs)
            return (token,)

        def end(self, group, after):
            (send2, recv2), group_bufs = self.state[group]
            full = _gather_done(send2, recv2, group_bufs, after, "gather_done_" + group)
            if group == "in":
                return full[0], jnp.transpose(full[1].reshape(N_DEV, 8, 128)[:, :3, :], (1, 0, 2)).reshape(3, D)
            return full

    in_flight, own_pieces = {}, {}

    class Reducer:
        def start(self, group, gdict):
            keys, glist = list(gdict), list(gdict.values())
            send_sems, recv_sems, glist, lands, token = _exchange_start(glist, N_DEV, _to_sibling, "rs_sibling_start_" + group)
            in_flight[group] = (keys, send_sems, recv_sems, glist, lands)
            return (token,)

        def middle(self, group, after):
            keys, send_sems, recv_sems, glist, lands = in_flight[group]
            glist, lands = _exchange_wait(send_sems, recv_sems, glist, lands, after, "rs_sibling_wait_" + group)
            parts = [_chip_partial(g, r, partial_idx, "chip_partial_" + k) for k, g, r in zip(keys, glist, lands)]
            send_sems, recv_sems, parts, from_chips, token = _exchange_start(parts, 4, _to_chips, "rs_chips_start_" + group)
            in_flight[group] = (keys, send_sems, recv_sems, parts, from_chips)
            own_pieces[group] = (glist, lands)
            return (token,)

    dx, _, small = _local_step(x[0], loss_target[0], g_mix, g_ffn, g_final[None], attn_sinks, Gathered(),
                               reducer=Reducer(), after=gather_tokens)

    transposed = ("w_in", "w_gate_up")

    def as2d(k, a):
        if k in transposed:
            return jnp.transpose(a[0])
        return a[None] if a.ndim == 1 else (a[0] if a.ndim == 3 else a)

    w_all = {"g_mix": g_mix, "w_in": w_in, "conv_w": conv_w, "attn_sinks": attn_sinks, "w_conv_out": w_conv_out,
             "w_attn_out": w_attn_out, "w_o": w_o, "g_ffn": g_ffn, "w_gate_up": w_gate_up, "w_down": w_down, "g_final": g_final}
    m_all = {"g_mix": m_g_mix, "w_in": m_w_in, "conv_w": m_conv_w, "attn_sinks": m_attn_sinks, "w_conv_out": m_w_conv_out,
             "w_attn_out": m_w_attn_out, "w_o": m_w_o, "g_ffn": m_g_ffn, "w_gate_up": m_w_gate_up, "w_down": m_w_down,
             "g_final": m_g_final}
    v_all = {"g_mix": v_g_mix, "w_in": v_w_in, "conv_w": v_conv_w, "attn_sinks": v_attn_sinks, "w_conv_out": v_w_conv_out,
             "w_attn_out": v_w_attn_out, "w_o": v_w_o, "g_ffn": v_g_ffn, "w_gate_up": v_w_gate_up, "w_down": v_w_down,
             "g_final": v_g_final}
    results = {}

    def update(k, g=None, pieces=None):
        w2, m2, v2 = as2d(k, w_all[k]), as2d(k, m_all[k]), as2d(k, v_all[k])
        if g is None:
            g, d, nm, nv = _reduce_adamw(w2, *pieces, own_idx, m2, v2, "adamw_" + k)
        else:
            d, nm, nv = _adamw(w2, g, m2, v2, "adamw_" + k)
        results[k] = [(jnp.transpose(val) if k in transposed else val).reshape(w_all[k].shape) for val in (g, d, nm, nv)]
        return nm

    kernel_name = {"win_t": "w_in", "wgu_t": "w_gate_up", "wd": "w_down", "wco": "w_conv_out", "wao": "w_attn_out", "wo": "w_o"}

    def finish(group, after):
        keys, send_sems, recv_sems, parts, from_chips = in_flight[group]
        _, from_chips = _exchange_wait(send_sems, recv_sems, parts, from_chips, after, "rs_chips_wait_" + group)
        grads, from_sibling = own_pieces[group]
        return tuple(update(kernel_name[k], pieces=p) for k, *p in zip(keys, grads, from_sibling, from_chips))

    after = finish("mix", finish("ffn", (dx,)))

    sinks_row = jnp.pad(small["sinks"], ((0, 0), (0, D - 128)))
    pack = jnp.concatenate([small["g_mix"], small["g_ffn"], small["g_final"], small["conv_w"], sinks_row, small["lossvec"]], axis=0)
    tot, loss_row = _small_all_reduce(pack, "small_all_reduce", after=after)
    loss = loss_row[0, 0]
    g_small = {
        "g_mix": tot[0:1], "g_ffn": tot[1:2], "g_final": tot[2:3],
        "conv_w": lax.dynamic_slice(tot, (3, me * 128), (3, 128)), "attn_sinks": tot[6:7, :N_HEADS],
    }
    finish("in", tuple(update(k, g) for k, g in g_small.items()))

    order = ["g_mix", "w_in", "conv_w", "attn_sinks", "w_conv_out", "w_attn_out", "w_o", "g_ffn", "w_gate_up", "w_down", "g_final"]
    return (loss, dx[None], *[results[k][i] for i in range(4) for k in order])
```

```python
import functools
import math

import jax
import jax.numpy as jnp
from jax import lax
from jax.experimental import pallas as pl
from jax.experimental.pallas import tpu as pltpu

F32 = jnp.float32
BF16 = jnp.bfloat16

D = 1024
HEAD_DIM = 64
N_HEADS = 16
N_KV = 4
GROUP = N_HEADS // N_KV
D_KV = N_KV * HEAD_DIM
BLOCK = 128
ROT_DIM = HEAD_DIM // 4
ROPE_THETA = 500000.0
ATTN_SCALE = 1.0 / math.sqrt(HEAD_DIM)
NEG_INF = -1e30
D_FF = 2816
N_IN = 6656
EPS = 1e-5
C_CB, C_CC, C_CX, C_Q, C_K, C_V, C_GC, C_GA = 0, 1024, 2048, 3072, 4096, 4352, 4608, 5632

LR, B1, B2, EPS_ADAM, WD, STEP = 0.001, 0.9, 0.999, 1e-08, 0.01, 10

N_DEV = 8
MESH = pl.DeviceIdType.MESH
VMEM_LIMIT = 56 * 1024 * 1024

NN = (((1,), (0,)), ((), ()))
NT = (((1,), (1,)), ((), ()))
TN = (((0,), (0,)), ((), ()))
HBM_SPEC = pl.BlockSpec(memory_space=pl.ANY)
ROW_SPLIT = 4


def _call(body, **kw):
    return pl.pallas_call(body, **kw)


def _params(*sem):
    return pltpu.CompilerParams(dimension_semantics=sem, vmem_limit_bytes=VMEM_LIMIT)


def _sds(shape, dtype):
    return jax.ShapeDtypeStruct(shape, dtype)


def _matmul(a, b, *, mode, tm, tn, tk, out_dtype, name, res=None, after=()):
    parts = list(a) if isinstance(a, (list, tuple)) else [a]
    rows_a = parts[0].shape[0]
    cols_a = sum(p.shape[1] for p in parts)
    if mode == "nn":
        (m, kk), (_, n), dims = (rows_a, cols_a), b.shape, NN
    elif mode == "nt":
        (m, kk), (n, _), dims = (rows_a, cols_a), b.shape, NT
    else:
        (kk, m), (_, n), dims = (rows_a, cols_a), b.shape, TN
    tm, tn, tk = min(tm, m), min(tn, n), min(tk, kk)
    assert m % tm == 0 and n % tn == 0 and kk % tk == 0, (name, m, n, kk, tm, tn, tk)
    nk = kk // tk
    split_axis, width = (2, tk) if mode == "nn" else (0, tm)
    assert len(parts) == 1 or mode in ("nn", "tn")
    assert len(parts) == 1 or all(p.shape[1] % width == 0 for p in parts), (name, width)
    counts = [p.shape[1] // width for p in parts]
    starts = [sum(counts[:p]) for p in range(len(parts))]

    def a_spec(p):
        def col(t):
            return jnp.clip(t - starts[p], 0, counts[p] - 1) if len(parts) > 1 else t

        if mode == "tn":
            return pl.BlockSpec((tk, tm), lambda i, j, k: (k, col(i)))
        return pl.BlockSpec((tm, tk), lambda i, j, k: (i, col(k)))

    if mode == "nt":
        b_spec = pl.BlockSpec((tn, tk), lambda i, j, k: (j, k))
    else:
        b_spec = pl.BlockSpec((tk, tn), lambda i, j, k: (k, j))
    o_spec = pl.BlockSpec((tm, tn), lambda i, j, k: (i, j))
    has_res = res is not None
    n_parts = len(parts)
    unit = 128 if mode == "tn" else 16
    split = ROW_SPLIT if tm % (ROW_SPLIT * unit) == 0 else 1

    def body(*refs):
        a_refs, b_ref = refs[:n_parts], refs[n_parts]
        r_ref = refs[n_parts + 1] if has_res else None
        o_ref = refs[n_parts + 1 + has_res + len(after)]
        k = pl.program_id(2)

        acc_ref = refs[-1] if nk > 1 else None

        def step(a_ref):
            def matmul(rows):
                a_blk = a_ref[:, rows] if mode == "tn" else a_ref[rows, :]
                return lax.dot_general(a_blk, b_ref[...], dims, preferred_element_type=F32)

            def finish(rows, part):
                if nk > 1:
                    acc_ref[rows, :] += part
                else:
                    o_ref[rows, :] = (part + r_ref[rows, :] if has_res else part).astype(o_ref.dtype)

            _row_pipeline(tm, matmul, finish, split)

        if nk > 1:
            @pl.when(k == 0)
            def _():
                acc_ref[...] = jnp.zeros_like(acc_ref)

        if n_parts == 1:
            step(a_refs[0])
        else:
            t = pl.program_id(split_axis)
            for p in range(n_parts):
                pl.when((t >= starts[p]) & (t < starts[p] + counts[p]))(functools.partial(step, a_refs[p]))

        if nk > 1:
            @pl.when(k == nk - 1)
            def _():
                o_ref[...] = (acc_ref[...] + r_ref[...] if has_res else acc_ref[...]).astype(o_ref.dtype)

    ins = parts + [b] + ([res] if has_res else []) + list(after)
    in_specs = [a_spec(p) for p in range(n_parts)] + [b_spec] + ([o_spec] if has_res else []) + [HBM_SPEC] * len(after)
    scratch = [] if nk == 1 else [pltpu.VMEM((tm, tn), F32)]
    return _call(
        body, name=name, grid=(m // tm, n // tn, nk), in_specs=in_specs, out_specs=o_spec,
        out_shape=_sds((m, n), out_dtype), scratch_shapes=scratch,
        compiler_params=_params("parallel", "parallel", "arbitrary"),
    )(*ins)


def _row_tile(s):
    return min(512, s)


def _rms_fwd(x, g, name, after=()):
    s = x.shape[0]
    tm = _row_tile(s)

    def body(x_ref, g_ref, *rest):
        h_ref = rest[-1]
        xv = x_ref[...]
        r = lax.rsqrt(jnp.mean(xv * xv, axis=-1, keepdims=True) + EPS)
        h_ref[...] = (xv * r * g_ref[...]).astype(BF16)

    row = pl.BlockSpec((tm, D), lambda i: (i, 0))
    return _call(
        body, name=name, grid=(s // tm,), in_specs=[row, pl.BlockSpec((1, D), lambda i: (0, 0))] + [HBM_SPEC] * len(after),
        out_specs=row, out_shape=_sds((s, D), BF16), compiler_params=_params("parallel"),
    )(x, g, *after)


def _rms_bwd(dh, x, g, dres, name, after=()):
    s = x.shape[0]
    tm = _row_tile(s)

    def body(dh_ref, x_ref, g_ref, dres_ref, *rest):
        dx_ref, dxb_ref, dg_ref = rest[len(after):]
        xv = x_ref[...]
        r = lax.rsqrt(jnp.mean(xv * xv, axis=-1, keepdims=True) + EPS)
        xh = xv * r
        dhv = dh_ref[...]
        dyg = dhv * g_ref[...]
        dx = dres_ref[...] + r * (dyg - xh * jnp.mean(dyg * xh, axis=-1, keepdims=True))
        dx_ref[...] = dx
        dxb_ref[...] = dx.astype(BF16)
        part = jnp.sum(dhv * xh, axis=0, keepdims=True)

        @pl.when(pl.program_id(0) == 0)
        def _():
            dg_ref[...] = part

        @pl.when(pl.program_id(0) > 0)
        def _():
            dg_ref[...] += part

    row = pl.BlockSpec((tm, D), lambda i: (i, 0))
    vec = pl.BlockSpec((1, D), lambda i: (0, 0))
    return _call(
        body, name=name, grid=(s // tm,), in_specs=[row, row, vec, row] + [HBM_SPEC] * len(after), out_specs=[row, row, vec],
        out_shape=[_sds((s, D), F32), _sds((s, D), BF16), _sds((1, D), F32)],
        compiler_params=_params("arbitrary"),
    )(dh, x, g, dres, *after)


def _loss_head(x2, g, tgt, name):
    s = x2.shape[0]
    tm = _row_tile(s)

    def body(x_ref, g_ref, t_ref, dx_ref, dxb_ref, dg_ref, l_ref):
        xv = x_ref[...]
        gv = g_ref[...]
        r = lax.rsqrt(jnp.mean(xv * xv, axis=-1, keepdims=True) + EPS)
        xh = xv * r
        err = xh * gv - t_ref[...]
        dy = err * (1.0 / D)
        dyg = dy * gv
        dx = r * (dyg - xh * jnp.mean(dyg * xh, axis=-1, keepdims=True))
        dx_ref[...] = dx
        dxb_ref[...] = dx.astype(BF16)
        dg_part = jnp.sum(dy * xh, axis=0, keepdims=True)
        l_part = jnp.sum(err * err, axis=0, keepdims=True)

        @pl.when(pl.program_id(0) == 0)
        def _():
            dg_ref[...] = dg_part
            l_ref[...] = l_part

        @pl.when(pl.program_id(0) > 0)
        def _():
            dg_ref[...] += dg_part
            l_ref[...] += l_part

    row = pl.BlockSpec((tm, D), lambda i: (i, 0))
    vec = pl.BlockSpec((1, D), lambda i: (0, 0))
    return _call(
        body, name=name, grid=(s // tm,), in_specs=[row, vec, row], out_specs=[row, row, vec, vec],
        out_shape=[_sds((s, D), F32), _sds((s, D), BF16), _sds((1, D), F32), _sds((1, D), F32)],
        compiler_params=_params("arbitrary"),
    )(x2, g, tgt)


CONV_TC = 512
CONV_ROWS = 256
HALO = 16


def _shifted_down(u, before):
    ext = jnp.concatenate([before, u], axis=0)
    return pltpu.roll(ext, 1, 0)[HALO:], pltpu.roll(ext, 2, 0)[HALO:]


def _shifted_up(u, behind):
    n = u.shape[0]
    ext = jnp.concatenate([u, behind], axis=0)
    return pltpu.roll(ext, n + HALO - 1, 0)[:n], pltpu.roll(ext, n + HALO - 2, 0)[:n]


def _conv_specs(s):
    tr = min(CONV_ROWS, s)
    ni, nj, per = s // tr, D // CONV_TC, tr // HALO

    def main(c0):
        return pl.BlockSpec((tr, CONV_TC), lambda j, i: (i, c0 // CONV_TC + j))

    def before(c0):
        return pl.BlockSpec((HALO, CONV_TC), lambda j, i: (jnp.maximum(i * per - 1, 0), c0 // CONV_TC + j))

    def behind(c0):
        return pl.BlockSpec((HALO, CONV_TC), lambda j, i: (jnp.minimum((i + 1) * per, s // HALO - 1), c0 // CONV_TC + j))

    return tr, ni, nj, main, before, behind


def _conv_fwd(proj, conv_w, name):
    s = proj.shape[0]
    tr, ni, nj, main, before, behind = _conv_specs(s)

    def body(cb_ref, cc_ref, cx_ref, ccp_ref, cxp_ref, w_ref, y_ref):
        i = pl.program_id(1)
        u = cc_ref[...].astype(F32) * cx_ref[...].astype(F32)
        u1, u2 = _shifted_down(u, jnp.where(i > 0, ccp_ref[...].astype(F32) * cxp_ref[...].astype(F32), 0.0))
        w = w_ref[...]
        y_ref[...] = (cb_ref[...].astype(F32) * (w[0:1] * u2 + w[1:2] * u1 + w[2:3] * u)).astype(BF16)

    return _call(
        body, name=name, grid=(nj, ni),
        in_specs=[main(C_CB), main(C_CC), main(C_CX), before(C_CC), before(C_CX),
                  pl.BlockSpec((3, CONV_TC), lambda j, i: (0, j))],
        out_specs=pl.BlockSpec((tr, CONV_TC), lambda j, i: (i, j)), out_shape=_sds((s, D), BF16),
        compiler_params=_params("parallel", "parallel"),
    )(proj, proj, proj, proj, proj, conv_w)


def _write_behind(t, nt, buf, sems, tiles, window, where):
    slot = t % 2

    def copies(sl, at):
        return [pltpu.make_async_copy(buf.at[sl, p], window(p, at), sems.at[sl, p]) for p in range(len(tiles))]

    @pl.when(t >= 2)
    def _():
        for cp in copies(slot, where):
            cp.wait()

    for p, tile in enumerate(tiles):
        buf[slot, p] = tile
    started = copies(slot, where)
    for cp in started:
        cp.start()

    @pl.when(t == nt - 1)
    def _():
        for cp in started:
            cp.wait()
        if nt > 1:
            for cp in copies(1 - slot, where):
                cp.wait()


def _conv_bwd(dy, proj, conv_w, dproj, name, after=()):
    s = proj.shape[0]
    tr, ni, nj, main, before, behind = _conv_specs(s)

    def body(dy_ref, cb_ref, cc_ref, cx_ref, ccp_ref, cxp_ref, dyn_ref, cbn_ref, w_ref, *rest):
        dproj_ref, dw_ref, buf, sems = rest[1 + len(after):]
        j, i = pl.program_id(0), pl.program_id(1)
        cc = cc_ref[...].astype(F32)
        cx = cx_ref[...].astype(F32)
        u = cc * cx
        u1, u2 = _shifted_down(u, jnp.where(i > 0, ccp_ref[...].astype(F32) * cxp_ref[...].astype(F32), 0.0))
        w = w_ref[...]
        c = w[0:1] * u2 + w[1:2] * u1 + w[2:3] * u
        dyv = dy_ref[...].astype(F32)
        dc = dyv * cb_ref[...].astype(F32)
        dc1, dc2 = _shifted_up(dc, jnp.where(i < ni - 1, dyn_ref[...].astype(F32) * cbn_ref[...].astype(F32), 0.0))
        du = w[2:3] * dc + w[1:2] * dc1 + w[0:1] * dc2

        def window(p, at):
            start = pl.multiple_of((C_CB, C_CC, C_CX)[p] + at[0] * CONV_TC, CONV_TC)
            return dproj_ref.at[pl.ds(pl.multiple_of(at[1] * tr, tr), tr), pl.ds(start, CONV_TC)]

        tiles = ((dyv * c).astype(BF16), (du * cx).astype(BF16), (du * cc).astype(BF16))
        _write_behind(j * ni + i, nj * ni, buf, sems, tiles, window, (j, i))
        part = jnp.concatenate(
            [jnp.sum(dc * u2, axis=0, keepdims=True), jnp.sum(dc * u1, axis=0, keepdims=True),
             jnp.sum(dc * u, axis=0, keepdims=True)], axis=0)

        @pl.when(i == 0)
        def _():
            dw_ref[...] = part

        @pl.when(i > 0)
        def _():
            dw_ref[...] += part

    wspec = pl.BlockSpec((3, CONV_TC), lambda j, i: (0, j))
    return _call(
        body, name=name, grid=(nj, ni),
        in_specs=[main(0), main(C_CB), main(C_CC), main(C_CX), before(C_CC), before(C_CX), behind(0), behind(C_CB), wspec]
        + [HBM_SPEC] * (1 + len(after)),
        out_specs=[pl.BlockSpec(memory_space=pl.ANY), wspec],
        out_shape=[_sds((s, N_IN), BF16), _sds((3, D), F32)],
        scratch_shapes=[pltpu.VMEM((2, 3, tr, CONV_TC), BF16), pltpu.SemaphoreType.DMA((2, 3))],
        input_output_aliases={9: 0}, compiler_params=_params("arbitrary", "arbitrary"),
    )(dy, proj, proj, proj, proj, proj, dy, proj, conv_w, dproj, *after)


def _rope_tables(s):
    half = ROT_DIM // 2
    inv_freq = ROPE_THETA ** (-jnp.arange(0, ROT_DIM, 2, dtype=F32) / ROT_DIM)
    inv64 = jnp.concatenate([inv_freq, inv_freq, jnp.zeros((HEAD_DIM - ROT_DIM,), F32)])
    ang = jnp.arange(s, dtype=F32)[:, None] * jnp.concatenate([inv64, inv64])[None, :]
    d = lax.broadcasted_iota(jnp.int32, (s, 128), 1) % HEAD_DIM
    cos, sin = jnp.cos(ang), jnp.sin(ang)
    c = jnp.where(d < ROT_DIM, cos, 1.0)
    a = jnp.where(d < half, -sin, 0.0)
    b = jnp.where((d >= half) & (d < ROT_DIM), sin, 0.0)
    return jnp.concatenate([c, a, b], axis=1)


def _rope(x, tab):
    c, a, b = tab[:, 0:128], tab[:, 128:256], tab[:, 256:384]
    outs = []
    for i in range(x.shape[1] // 128):
        xc = x[:, i * 128:(i + 1) * 128]
        outs.append(xc * c + pltpu.roll(xc, 120, 1) * a + pltpu.roll(xc, 8, 1) * b)
    return outs[0] if len(outs) == 1 else jnp.concatenate(outs, axis=1)


def _rope_t(dx, tab):
    c, a, b = tab[:, 0:128], tab[:, 128:256], tab[:, 256:384]
    outs = []
    for i in range(dx.shape[1] // 128):
        dc = dx[:, i * 128:(i + 1) * 128]
        outs.append(dc * c + pltpu.roll(dc * a, 8, 1) + pltpu.roll(dc * b, 120, 1))
    return outs[0] if len(outs) == 1 else jnp.concatenate(outs, axis=1)


def _attn_in_specs():
    prev = lambda n: jnp.maximum(n - 1, 0)
    return [
        pl.BlockSpec((BLOCK, D), lambda n: (n, C_Q // D)),
        pl.BlockSpec((BLOCK, D_KV), lambda n: (n, C_K // D_KV)),
        pl.BlockSpec((BLOCK, D_KV), lambda n: (prev(n), C_K // D_KV)),
        pl.BlockSpec((BLOCK, D_KV), lambda n: (n, C_V // D_KV)),
        pl.BlockSpec((BLOCK, D_KV), lambda n: (prev(n), C_V // D_KV)),
        pl.BlockSpec((BLOCK, 384), lambda n: (n, 0)),
        pl.BlockSpec((BLOCK, 384), lambda n: (prev(n), 0)),
        pl.BlockSpec(memory_space=pltpu.SMEM),
    ]


HALF = HEAD_DIM
N_CHUNK = D // 128


def _swa_bias(n):
    qi = lax.broadcasted_iota(jnp.int32, (BLOCK, 2 * BLOCK), 0)
    kj = lax.broadcasted_iota(jnp.int32, (BLOCK, 2 * BLOCK), 1)
    rel = qi + BLOCK - kj
    valid = (rel >= 0) & (rel < BLOCK) & ((kj >= BLOCK) | (n > 0))
    return jnp.where(valid, 0.0, NEG_INF)


def _halves(x):
    lo = lax.broadcasted_iota(jnp.int32, x.shape, 1) < HALF
    return jnp.where(lo, x, 0.0).astype(BF16), jnp.where(lo, 0.0, x).astype(BF16)


def _dup_heads(x):
    out = []
    for pair in range(N_KV // 2):
        xc = x[:, pair * 128:(pair + 1) * 128]
        xr = pltpu.roll(xc, HALF, 1)
        lo = lax.broadcasted_iota(jnp.int32, xc.shape, 1) < HALF
        out += [jnp.where(lo, xc, xr), jnp.where(lo, xr, xc)]
    return out


def _swa_load(q_ref, kc_ref, kp_ref, vc_ref, vp_ref, tc_ref, tp_ref):
    qf = _rope(q_ref[...].astype(F32), tc_ref[...]) * ATTN_SCALE
    q_halves = [_halves(qf[:, c * 128:(c + 1) * 128]) for c in range(N_CHUNK)]
    kf = jnp.concatenate([_rope(kp_ref[...].astype(F32), tp_ref[...]), _rope(kc_ref[...].astype(F32), tc_ref[...])], axis=0)
    vf = jnp.concatenate([vp_ref[...], vc_ref[...]], axis=0).astype(F32)
    return q_halves, _dup_heads(kf), _dup_heads(vf)


def _swa_probs(qh, kk, bias, sink):
    s = lax.dot_general(qh, kk, NT, preferred_element_type=F32) + bias
    m = jnp.maximum(jnp.max(jnp.maximum(s[:, :BLOCK], s[:, BLOCK:]), axis=1, keepdims=True), sink)
    return jnp.exp(s - m), m


def _swa_fwd(proj, tab, sinks, name, after=()):
    s = proj.shape[0]

    def body(q_ref, kc_ref, kp_ref, vc_ref, vp_ref, tc_ref, tp_ref, sink_ref, *rest):
        o_ref = rest[-1]
        n = pl.program_id(0)
        q_halves, kdup, vdup = _swa_load(q_ref, kc_ref, kp_ref, vc_ref, vp_ref, tc_ref, tp_ref)
        bias = _swa_bias(n)
        ones = jnp.ones((2 * BLOCK, 128), BF16)
        kk = [k.astype(BF16) for k in kdup]
        vv = [[jnp.concatenate([v_half, ones], axis=1) for v_half in _halves(v)] for v in vdup]
        heads = [(c, half) for c in range(N_CHUNK) for half in range(2)]
        scores = [lax.dot_general(q_halves[c][half], kk[c // (GROUP // 2)], NT, preferred_element_type=F32)
                  for c, half in heads]
        probs = []
        for (c, half), sc in zip(heads, scores):
            sc = sc + bias
            m = jnp.maximum(jnp.max(jnp.maximum(sc[:, :BLOCK], sc[:, BLOCK:]), axis=1, keepdims=True), sink_ref[0, 2 * c + half])
            probs.append((jnp.exp(sc - m).astype(BF16), jnp.exp(sink_ref[0, 2 * c + half] - m)))
        outs = [lax.dot_general(e, vv[c // (GROUP // 2)][half], NN, preferred_element_type=F32)
                for (c, half), (e, _) in zip(heads, probs)]
        for c in range(N_CHUNK):
            parts = [outs[2 * c + half][:, :128] * (1.0 / (outs[2 * c + half][:, 128:] + probs[2 * c + half][1]))
                     for half in range(2)]
            o_ref[:, c * 128:(c + 1) * 128] = (parts[0] + parts[1]).astype(BF16)

    return _call(
        body, name=name, grid=(s // BLOCK,), in_specs=_attn_in_specs() + [HBM_SPEC] * len(after),
        out_specs=pl.BlockSpec((BLOCK, D), lambda n: (n, 0)), out_shape=_sds((s, D), BF16),
        compiler_params=_params("parallel"),
    )(proj, proj, proj, proj, proj, tab, tab, sinks, *after)


def _swa_bwd(do, proj, tab, sinks, dproj, name, after=()):
    s = proj.shape[0]
    nblk = s // BLOCK
    kv_of = lambda c: c // (GROUP // 2)

    def body(do_ref, q_ref, kc_ref, kp_ref, vc_ref, vp_ref, tc_ref, tp_ref, sink_ref, *rest):
        dproj_ref, dk_ref, dv_ref, ds_ref, dqout, dkbuf, dvbuf, sems = rest[1 + len(after):]
        n = pl.program_id(0)

        @pl.when(n == 0)
        def _():
            dk_ref[...] = jnp.zeros_like(dk_ref)
            dv_ref[...] = jnp.zeros_like(dv_ref)
            ds_ref[...] = jnp.zeros_like(ds_ref)

        q_halves, kdup, vdup = _swa_load(q_ref, kc_ref, kp_ref, vc_ref, vp_ref, tc_ref, tp_ref)
        dof = do_ref[...].astype(F32)
        do_halves = [_halves(dof[:, c * 128:(c + 1) * 128]) for c in range(N_CHUNK)]
        bias = _swa_bias(n)
        ones = jnp.ones((2 * BLOCK, 128), BF16)
        kk = [k.astype(BF16) for k in kdup]
        vv = [v.astype(BF16) for v in vdup]
        k_halves = [_halves(k) for k in kdup]
        heads = [(c, half) for c in range(N_CHUNK) for half in range(2)]
        lane_row = lax.broadcasted_iota(jnp.int32, (1, 128), 1)
        lo_kv = lax.broadcasted_iota(jnp.int32, (2 * BLOCK, 128), 1) < HALF
        scores = [lax.dot_general(q_halves[c][half], kk[kv_of(c)], NT, preferred_element_type=F32) for c, half in heads]
        dps = [lax.dot_general(do_halves[c][half], vv[kv_of(c)], NT, preferred_element_type=F32) for c, half in heads]
        exps = []
        for (c, half), sc in zip(heads, scores):
            sink = sink_ref[0, 2 * c + half]
            sc = sc + bias
            m = jnp.maximum(jnp.max(jnp.maximum(sc[:, :BLOCK], sc[:, BLOCK:]), axis=1, keepdims=True), sink)
            exps.append((jnp.exp(sc - m), jnp.exp(sink - m)))
        sums = [lax.dot_general(e.astype(BF16), ones, NN, preferred_element_type=F32) for e, _ in exps]
        dsink_row = jnp.zeros((1, 128), F32)
        dsb, pb = [], []
        for h, ((e, es), row_sum, dp) in enumerate(zip(exps, sums, dps)):
            inv = 1.0 / (row_sum + es)
            p = e * jnp.concatenate([inv, inv], axis=1)
            t = p * dp
            delta = jnp.sum(t, axis=1, keepdims=True)
            dsb.append((t - p * delta).astype(BF16))
            pb.append(p.astype(BF16))
            dsink = -jnp.sum(es * inv * delta, axis=0, keepdims=True)
            dsink_row = dsink_row + jnp.where(lane_row == h, dsink, 0.0)
        dq_parts = [lax.dot_general(d, k_halves[kv_of(c)][half], NN, preferred_element_type=F32) for (c, half), d in zip(heads, dsb)]
        dk_parts = [lax.dot_general(d, q_halves[c][half], TN, preferred_element_type=F32) for (c, half), d in zip(heads, dsb)]
        dv_parts = [lax.dot_general(p, do_halves[c][half], TN, preferred_element_type=F32) for (c, half), p in zip(heads, pb)]
        dq = jnp.concatenate([(dq_parts[2 * c] + dq_parts[2 * c + 1]) * ATTN_SCALE for c in range(N_CHUNK)], axis=1)

        def kv_sum(parts, hk):
            acc = (parts[GROUP * hk] + parts[GROUP * hk + 1]) + (parts[GROUP * hk + 2] + parts[GROUP * hk + 3])
            return acc + pltpu.roll(acc, HALF, 1)

        for pair in range(N_KV // 2):
            dkbuf[:, pair * 128:(pair + 1) * 128] = jnp.where(lo_kv, kv_sum(dk_parts, 2 * pair), kv_sum(dk_parts, 2 * pair + 1))
            dvbuf[:, pair * 128:(pair + 1) * 128] = jnp.where(lo_kv, kv_sum(dv_parts, 2 * pair), kv_sum(dv_parts, 2 * pair + 1))
        prev0 = pl.multiple_of(jnp.maximum(n - 1, 0) * BLOCK, BLOCK)
        cur0 = pl.multiple_of(n * BLOCK, BLOCK)

        @pl.when(n > 0)
        def _():
            dk_ref[pl.ds(prev0, BLOCK), :] += dkbuf[0:BLOCK, :]
            dv_ref[pl.ds(prev0, BLOCK), :] += dvbuf[0:BLOCK, :]

        dk_ref[pl.ds(cur0, BLOCK), :] += dkbuf[BLOCK:2 * BLOCK, :]
        dv_ref[pl.ds(cur0, BLOCK), :] += dvbuf[BLOCK:2 * BLOCK, :]
        ds_ref[...] += dsink_row

        def window(p, at):
            return dproj_ref.at[pl.ds(pl.multiple_of(at * BLOCK, BLOCK), BLOCK), pl.ds(C_Q, D)]

        _write_behind(n, nblk, dqout, sems, (_rope_t(dq, tc_ref[...]).astype(BF16),), window, n)

    blk = lambda w: pl.BlockSpec((BLOCK, w), lambda n: (n, 0))
    whole = lambda w: pl.BlockSpec((s, w), lambda n: (0, 0))
    n_in = 1 + len(_attn_in_specs())
    return _call(
        body, name=name, grid=(nblk,), in_specs=[blk(D)] + _attn_in_specs() + [HBM_SPEC] * (1 + len(after)),
        out_specs=[HBM_SPEC, whole(D_KV), whole(D_KV), pl.BlockSpec((1, 128), lambda n: (0, 0))],
        out_shape=[_sds((s, N_IN), BF16), _sds((s, D_KV), F32), _sds((s, D_KV), F32), _sds((1, 128), F32)],
        scratch_shapes=[pltpu.VMEM((2, 1, BLOCK, D), BF16), pltpu.VMEM((2 * BLOCK, D_KV), F32),
                        pltpu.VMEM((2 * BLOCK, D_KV), F32), pltpu.SemaphoreType.DMA((2, 1))],
        input_output_aliases={n_in: 0}, compiler_params=_params("arbitrary"),
    )(do, proj, proj, proj, proj, proj, tab, tab, sinks, dproj, *after)


def _kv_bwd(dkr, dv, tab, dproj, name):
    s = dkr.shape[0]
    tm = _row_tile(s)

    def body(dk_ref, dv_ref, t_ref, dproj_in, o_ref):
        del dproj_in
        o_ref[:, 0:D_KV] = _rope_t(dk_ref[...], t_ref[...]).astype(BF16)
        o_ref[:, D_KV:2 * D_KV] = dv_ref[...].astype(BF16)

    row = lambda w: pl.BlockSpec((tm, w), lambda i: (i, 0))
    return _call(
        body, name=name, grid=(s // tm,),
        in_specs=[row(D_KV), row(D_KV), row(384), pl.BlockSpec(memory_space=pl.ANY)],
        out_specs=pl.BlockSpec((tm, 2 * D_KV), lambda i: (i, C_K // (2 * D_KV))),
        out_shape=_sds((s, N_IN), BF16), input_output_aliases={3: 0}, compiler_params=_params("parallel"),
    )(dkr, dv, tab, dproj)


EW_TC = 512


def _sigmoid(x):
    return 0.5 * jnp.tanh(0.5 * x) + 0.5


def _merge_fwd(proj, conv_out, attn_out, name):
    s = proj.shape[0]
    tm = _row_tile(s)
    tile = pl.BlockSpec((tm, EW_TC), lambda i, j: (i, j))

    def body(gc_ref, ga_ref, co_ref, ao_ref, o_ref):
        o_ref[...] = (_sigmoid(gc_ref[...].astype(F32)) * co_ref[...].astype(F32)
                      + _sigmoid(ga_ref[...].astype(F32)) * ao_ref[...].astype(F32)).astype(BF16)

    return _call(
        body, name=name, grid=(s // tm, D // EW_TC),
        in_specs=[pl.BlockSpec((tm, EW_TC), lambda i, j: (i, C_GC // EW_TC + j)),
                  pl.BlockSpec((tm, EW_TC), lambda i, j: (i, C_GA // EW_TC + j)), tile, tile],
        out_specs=tile, out_shape=_sds((s, D), BF16), compiler_params=_params("parallel", "parallel"),
    )(proj, proj, conv_out, attn_out)


def _merge_bwd(dmerged, proj, conv_out, attn_out, name):
    s = proj.shape[0]
    tm = _row_tile(s)
    tile = pl.BlockSpec((tm, EW_TC), lambda i, j: (i, j))
    anyspec = pl.BlockSpec(memory_space=pl.ANY)

    def body(dm_ref, gc_ref, ga_ref, co_ref, ao_ref, dproj_ref, dco_ref, dao_ref, buf, sems):
        i, j = pl.program_id(0), pl.program_id(1)
        dm = dm_ref[...].astype(F32)
        sc = _sigmoid(gc_ref[...].astype(F32))
        sa = _sigmoid(ga_ref[...].astype(F32))
        dco_ref[...] = (dm * sc).astype(BF16)
        dao_ref[...] = (dm * sa).astype(BF16)
        tiles = ((dm * co_ref[...].astype(F32) * sc * (1.0 - sc)).astype(BF16),
                 (dm * ao_ref[...].astype(F32) * sa * (1.0 - sa)).astype(BF16))

        def window(p, at):
            start = pl.multiple_of((C_GC, C_GA)[p] + at[1] * EW_TC, EW_TC)
            return dproj_ref.at[pl.ds(pl.multiple_of(at[0] * tm, tm), tm), pl.ds(start, EW_TC)]

        _write_behind(i * nj + j, (s // tm) * nj, buf, sems, tiles, window, (i, j))

    nj = D // EW_TC
    return _call(
        body, name=name, grid=(s // tm, nj),
        in_specs=[tile, pl.BlockSpec((tm, EW_TC), lambda i, j: (i, C_GC // EW_TC + j)),
                  pl.BlockSpec((tm, EW_TC), lambda i, j: (i, C_GA // EW_TC + j)), tile, tile],
        out_specs=[anyspec, tile, tile],
        out_shape=[_sds((s, N_IN), BF16), _sds((s, D), BF16), _sds((s, D), BF16)],
        scratch_shapes=[pltpu.VMEM((2, 2, tm, EW_TC), BF16), pltpu.SemaphoreType.DMA((2, 2))],
        compiler_params=_params("arbitrary", "arbitrary"),
    )(dmerged, proj, proj, conv_out, attn_out)


FF_TC = 256


def _row_pipeline(tm, matmul, finish, split=ROW_SPLIT):
    step = tm // split
    pending = None
    for r in range(split):
        rows = pl.ds(r * step, step)
        result = matmul(rows)
        if pending is not None:
            finish(*pending)
        pending = (rows, result)
    finish(*pending)


def _gate_up_fwd(h2, wgu_t, name):
    s = h2.shape[0]
    tm = min(2048, s)
    nb = D_FF // FF_TC

    def body(h_ref, wg_ref, wu_ref, a_ref, dadu_ref, dadg_ref):
        def matmuls(rows):
            h = h_ref[rows, :]
            return (lax.dot_general(h, wg_ref[...], NT, preferred_element_type=F32),
                    lax.dot_general(h, wu_ref[...], NT, preferred_element_type=F32))

        def finish(rows, gu):
            g, u = gu
            sg = _sigmoid(g)
            silu = g * sg
            a_ref[rows, :] = (silu * u).astype(BF16)
            dadu_ref[rows, :] = silu.astype(BF16)
            dadg_ref[rows, :] = (u * (sg * (1.0 + g * (1.0 - sg)))).astype(BF16)

        _row_pipeline(tm, matmuls, finish)

    tile = pl.BlockSpec((tm, FF_TC), lambda i, j: (i, j))
    return _call(
        body, name=name, grid=(s // tm, nb),
        in_specs=[pl.BlockSpec((tm, D), lambda i, j: (i, 0)), pl.BlockSpec((FF_TC, D), lambda i, j: (j, 0)),
                  pl.BlockSpec((FF_TC, D), lambda i, j: (nb + j, 0))],
        out_specs=[tile, tile, tile], out_shape=[_sds((s, D_FF), BF16)] * 3,
        compiler_params=_params("parallel", "parallel"),
    )(h2, wgu_t, wgu_t)


def _down_bwd_x(dx2b, wd, dadg, dadu, name):
    s = dx2b.shape[0]
    tm = min(2048, s)
    nb = D_FF // FF_TC

    def body(dx_ref, w_ref, dadg_ref, dadu_ref, dg_ref, du_ref):
        def matmul(rows):
            return lax.dot_general(dx_ref[rows, :], w_ref[...], NT, preferred_element_type=F32)

        def finish(rows, da):
            dg_ref[rows, :] = (da * dadg_ref[rows, :].astype(F32)).astype(BF16)
            du_ref[rows, :] = (da * dadu_ref[rows, :].astype(F32)).astype(BF16)

        _row_pipeline(tm, matmul, finish)

    tile = pl.BlockSpec((tm, FF_TC), lambda i, j: (i, j))
    return _call(
        body, name=name, grid=(s // tm, nb),
        in_specs=[pl.BlockSpec((tm, D), lambda i, j: (i, 0)), pl.BlockSpec((FF_TC, D), lambda i, j: (j, 0)), tile, tile],
        out_specs=[tile, tile], out_shape=[_sds((s, D_FF), BF16)] * 2,
        compiler_params=_params("parallel", "parallel"),
    )(dx2b, wd, dadg, dadu)


class _Weights:
    def __init__(self, **groups):
        self.groups = groups

    def begin(self, group, after):
        return ()

    def end(self, group, after):
        return self.groups[group]


class _NoReduce:
    def start(self, group, grads):
        return ()

    def middle(self, group, after):
        return ()


def _local_step(x, tgt, g_mix, g_ffn, g_final, sinks, weights, reducer=None, after=()):
    reducer = reducer or _NoReduce()
    s = x.shape[0]
    tab = _rope_tables(s)
    big = dict(tm=1024, tn=512, tk=1024)
    h1 = _rms_fwd(x, g_mix, "rms1_fwd", after=after)
    win_t, conv_w = weights.end("in", weights.begin("in", (h1,)))
    proj = _matmul(h1, win_t, mode="nt", out_dtype=BF16, name="proj_fwd", tm=2048, tn=512, tk=1024)
    attn = _swa_fwd(proj, tab, sinks, "attn_fwd", after=weights.begin("mix", (proj,)))
    wco, wao, wo = weights.end("mix", (attn,))
    conv_y = _conv_fwd(proj, conv_w, "conv_fwd")
    conv_out = _matmul(conv_y, wco, mode="nn", out_dtype=BF16, name="conv_out_fwd", **big)
    attn_out = _matmul(attn, wao, mode="nn", out_dtype=BF16, name="attn_out_fwd", **big)
    merged = _merge_fwd(proj, conv_out, attn_out, "merge_fwd")
    x1 = _matmul(merged, wo, mode="nn", out_dtype=F32, name="wo_fwd", res=x, after=weights.begin("ffn", (merged,)), **big)
    h2 = _rms_fwd(x1, g_ffn, "rms2_fwd")
    wgu_t, wd = weights.end("ffn", (h2,))
    act, dadu, dadg = _gate_up_fwd(h2, wgu_t, "gate_up_fwd")
    x2 = _matmul(act, wd, mode="nn", out_dtype=F32, name="down_fwd", res=x1, tm=1024, tn=512, tk=D_FF)
    dx2, dx2b, dg_final, lossvec = _loss_head(x2, g_final, tgt, "loss_head")
    dgate, dup = _down_bwd_x(dx2b, wd, dadg, dadu, "down_bwd_x")
    g_wd = _matmul(act, dx2b, mode="tn", out_dtype=BF16, name="down_bwd_w", tm=1408, tn=1024, tk=2048)
    dh2 = _matmul([dgate, dup], wgu_t, mode="nn", out_dtype=F32, name="gate_up_bwd_x", tm=1024, tn=1024, tk=1408)
    g_wgu_t = _matmul([dgate, dup], h2, mode="tn", out_dtype=BF16, name="gate_up_bwd_w", tm=1408, tn=1024, tk=2048)
    after_ffn = reducer.start("ffn", dict(wgu_t=g_wgu_t, wd=g_wd))
    dx1, dx1b, dg_ffn = _rms_bwd(dh2, x1, g_ffn, dx2, "rms2_bwd")
    dmerged = _matmul(dx1b, wo, mode="nt", out_dtype=BF16, name="wo_bwd_x", after=after_ffn, **big)
    after_ffn = reducer.middle("ffn", (dmerged,))
    g_wo = _matmul(merged, dx1b, mode="tn", out_dtype=BF16, name="wo_bwd_w", tm=512, tn=1024, tk=2048, after=after_ffn)
    dproj, dco, dao = _merge_bwd(dmerged, proj, conv_out, attn_out, "merge_bwd")
    dconv_y = _matmul(dco, wco, mode="nt", out_dtype=BF16, name="conv_out_bwd_x", **big)
    g_wco = _matmul(conv_y, dco, mode="tn", out_dtype=BF16, name="conv_out_bwd_w", tm=512, tn=1024, tk=2048)
    dattn = _matmul(dao, wao, mode="nt", out_dtype=BF16, name="attn_out_bwd_x", **big)
    g_wao = _matmul(attn, dao, mode="tn", out_dtype=BF16, name="attn_out_bwd_w", tm=512, tn=1024, tk=2048)
    after_mix = reducer.start("mix", dict(wco=g_wco, wao=g_wao, wo=g_wo))
    dproj, dconv_w = _conv_bwd(dconv_y, proj, conv_w, dproj, "conv_bwd", after=after_mix)
    after_mix = reducer.middle("mix", (dconv_w,))
    dproj, dkr, dv, dsinks = _swa_bwd(dattn, proj, tab, sinks, dproj, "attn_bwd", after=after_mix)
    dproj = _kv_bwd(dkr, dv, tab, dproj, "kv_bwd")
    g_win_t = _matmul(dproj, h1, mode="tn", out_dtype=BF16, name="proj_bwd_w", tm=512, tn=1024, tk=2048)
    after_in = reducer.middle("in", reducer.start("in", dict(win_t=g_win_t)))
    dh1 = _matmul(dproj, win_t, mode="nn", out_dtype=F32, name="proj_bwd_x", tm=1024, tn=1024, tk=1664, after=after_in)
    dx, _, dg_mix = _rms_bwd(dh1, x, g_mix, dx1, "rms1_bwd")
    grads = dict(win_t=g_win_t, wgu_t=g_wgu_t, wd=g_wd, wco=g_wco, wao=g_wao, wo=g_wo)
    small = dict(g_mix=dg_mix, g_ffn=dg_ffn, g_final=dg_final, conv_w=dconv_w, sinks=dsinks, lossvec=lossvec)
    return dx, grads, small


def _position():
    return lax.axis_index("x"), lax.axis_index("y"), lax.axis_index("c")


def _other_chips(x, y):
    return [(1 - x, y), (x, 1 - y), (1 - x, 1 - y)]


SEM_SPEC = pl.BlockSpec(memory_space=pltpu.SEMAPHORE)
EFFECT = pltpu.SideEffectType.DATAFLOW_SIDE_EFFECTING
TOKEN = jax.ShapeDtypeStruct((8, 128), F32)
TOKEN_SPEC = pl.BlockSpec(memory_space=pltpu.VMEM)


def _hbm(a):
    return pltpu.with_memory_space_constraint(a, pltpu.HBM)


def _place(w, me_idx, dtype, name, after=()):
    r, cdim = w.shape

    def body(i_ref, w_ref, *rest):
        rest[-1][...] = w_ref[...].astype(dtype)

    grid_spec = pltpu.PrefetchScalarGridSpec(
        num_scalar_prefetch=1, grid=(1,), in_specs=[pl.BlockSpec((r, cdim), lambda i, me: (0, 0))] + [HBM_SPEC] * len(after),
        out_specs=pl.BlockSpec((r, cdim), lambda i, me: (me[0], 0)))
    return _call(body, name=name, grid_spec=grid_spec, out_shape=_sds((N_DEV * r, cdim), dtype),
                 compiler_params=_params("arbitrary"))(me_idx, w, *after)


def _own_rows(ref, r, px, py, pc):
    return ref.at[pl.ds((4 * px + 2 * py + pc) * r, r), :]


def _gather_start(bufs, groups, name):
    n = len(bufs)
    rows = [b.shape[0] // N_DEV for b in bufs]
    ng = len(groups)

    def body(*refs):
        ins = refs[:n]
        sems = refs[n:n + 2 * ng]
        token = refs[-1]
        x, y, c = _position()
        targets = [(x, y, 1 - c)] + [(*chip, c) for chip in _other_chips(x, y)]
        for g, members in enumerate(groups):
            for slot, a in enumerate(members):
                own = _own_rows(ins[a], rows[a], x, y, c)
                for to in targets:
                    pltpu.make_async_remote_copy(src_ref=own, dst_ref=own, send_sem=sems[2 * g].at[slot],
                                                 recv_sem=sems[2 * g + 1].at[slot], device_id=to, device_id_type=MESH).start()
        token[...] = jnp.zeros_like(token)

    sem_shapes = []
    for members in groups:
        sem_shapes += [pltpu.SemaphoreType.DMA((len(members),))] * 2
    outs = _call(
        body, name=name, in_specs=[HBM_SPEC] * n, out_specs=[SEM_SPEC] * (2 * ng) + [HBM_SPEC] * n + [TOKEN_SPEC],
        out_shape=sem_shapes + [pltpu.HBM(b.shape, b.dtype) for b in bufs] + [TOKEN],
        input_output_aliases={i: 2 * ng + i for i in range(n)},
        compiler_params=pltpu.CompilerParams(has_side_effects=EFFECT),
    )(*[_hbm(b) for b in bufs])
    sem_pairs = [(outs[2 * g], outs[2 * g + 1]) for g in range(ng)]
    return sem_pairs, list(outs[2 * ng:2 * ng + n]), outs[-1]


def _gather_forward(send_sems, recv_sems, bufs, after, name):
    n = len(bufs)
    rows = [b.shape[0] // N_DEV for b in bufs]

    def body(*refs):
        ins = refs[:n]
        send1, recv1 = refs[n], refs[n + 1]
        out0 = n + 2 + len(after)
        send2, recv2 = refs[out0], refs[out0 + 1]
        token = refs[-1]
        x, y, c = _position()
        for a in range(n):
            step1 = pltpu.make_async_remote_copy(
                src_ref=_whole(ins[a], 4 * rows[a]), dst_ref=_whole(ins[a], 4 * rows[a]), send_sem=send1.at[a],
                recv_sem=recv1.at[a], device_id=(x, y, c), device_id_type=MESH)
            step1.wait_send()
            step1.wait_recv()
        for a in range(n):
            for chip in _other_chips(x, y):
                blk = _own_rows(ins[a], rows[a], *chip, c)
                pltpu.make_async_remote_copy(src_ref=blk, dst_ref=blk, send_sem=send2.at[a], recv_sem=recv2.at[a],
                                             device_id=(x, y, 1 - c), device_id_type=MESH).start()
        token[...] = jnp.zeros_like(token)

    outs = _call(
        body, name=name, in_specs=[HBM_SPEC] * n + [SEM_SPEC, SEM_SPEC] + [HBM_SPEC] * len(after),
        out_specs=[SEM_SPEC, SEM_SPEC] + [HBM_SPEC] * n + [TOKEN_SPEC],
        out_shape=[pltpu.SemaphoreType.DMA((n,)), pltpu.SemaphoreType.DMA((n,))]
        + [pltpu.HBM(b.shape, b.dtype) for b in bufs] + [TOKEN],
        input_output_aliases={i: 2 + i for i in range(n)},
        compiler_params=pltpu.CompilerParams(has_side_effects=EFFECT),
    )(*bufs, send_sems, recv_sems, *after)
    return outs[0], outs[1], list(outs[2:2 + n]), outs[-1]


def _gather_done(send_sems, recv_sems, bufs, after, name):
    n = len(bufs)
    rows = [b.shape[0] // N_DEV for b in bufs]

    def body(*refs):
        ins = refs[:n]
        send2, recv2 = refs[n], refs[n + 1]
        x, y, c = _position()
        for a in range(n):
            step2 = pltpu.make_async_remote_copy(
                src_ref=_whole(ins[a], 3 * rows[a]), dst_ref=_whole(ins[a], 3 * rows[a]), send_sem=send2.at[a],
                recv_sem=recv2.at[a], device_id=(x, y, c), device_id_type=MESH)
            step2.wait_send()
            step2.wait_recv()

    outs = _call(
        body, name=name, in_specs=[HBM_SPEC] * n + [SEM_SPEC, SEM_SPEC] + [HBM_SPEC] * len(after),
        out_specs=[HBM_SPEC] * n, out_shape=[pltpu.HBM(b.shape, b.dtype) for b in bufs],
        input_output_aliases={i: i for i in range(n)},
        compiler_params=pltpu.CompilerParams(has_side_effects=EFFECT),
    )(*bufs, send_sems, recv_sems, *after)
    return list(outs)


def _whole(ref, nrows):
    return ref.at[pl.ds(0, nrows), :]


def _to_sibling(x, y, c):
    return [(2 * q + (1 - c), q, (x, y, 1 - c)) for q in range(4)]


def _to_chips(x, y, c):
    return [(2 * px + py, j, (px, py, c)) for j, (px, py) in enumerate(_other_chips(x, y))]


def _exchange_start(srcs, src_slots, plan, name):
    n = len(srcs)
    rows = [a.shape[0] // src_slots for a in srcs]
    n_copies = len(plan(0, 0, 0))
    lands = [lax.empty((n_copies * r, a.shape[1]), a.dtype) for a, r in zip(srcs, rows)]

    def body(*refs):
        ins, land_refs = refs[:n], refs[n:2 * n]
        send_sems, recv_sems = refs[2 * n], refs[2 * n + 1]
        token = refs[-1]
        for a in range(n):
            r = rows[a]
            for src_slot, dst_slot, target in plan(*_position()):
                pltpu.make_async_remote_copy(
                    src_ref=ins[a].at[pl.ds(src_slot * r, r), :], dst_ref=land_refs[a].at[pl.ds(dst_slot * r, r), :],
                    send_sem=send_sems.at[a], recv_sem=recv_sems.at[a], device_id=target, device_id_type=MESH).start()
        token[...] = jnp.zeros_like(token)

    outs = _call(
        body, name=name, in_specs=[HBM_SPEC] * (2 * n),
        out_specs=[SEM_SPEC, SEM_SPEC] + [HBM_SPEC] * (2 * n) + [TOKEN_SPEC],
        out_shape=[pltpu.SemaphoreType.DMA((n,)), pltpu.SemaphoreType.DMA((n,))]
        + [pltpu.HBM(a.shape, a.dtype) for a in srcs] + [pltpu.HBM(l.shape, l.dtype) for l in lands] + [TOKEN],
        input_output_aliases={i: 2 + i for i in range(2 * n)},
        compiler_params=pltpu.CompilerParams(has_side_effects=EFFECT),
    )(*[_hbm(a) for a in srcs], *[_hbm(l) for l in lands])
    return outs[0], outs[1], list(outs[2:2 + n]), list(outs[2 + n:2 + 2 * n]), outs[-1]


def _exchange_wait(send_sems, recv_sems, srcs, lands, after, name):
    n = len(srcs)

    def body(*refs):
        ins, land_refs = refs[:n], refs[n:2 * n]
        send_sems_ref, recv_sems_ref = refs[2 * n], refs[2 * n + 1]
        for a in range(n):
            allrows = lands[a].shape[0]
            cp = pltpu.make_async_remote_copy(
                src_ref=_whole(ins[a], allrows), dst_ref=_whole(land_refs[a], allrows), send_sem=send_sems_ref.at[a],
                recv_sem=recv_sems_ref.at[a], device_id=_position(), device_id_type=MESH)
            cp.wait_send()
            cp.wait_recv()

    outs = _call(
        body, name=name, in_specs=[HBM_SPEC] * (2 * n) + [SEM_SPEC, SEM_SPEC] + [HBM_SPEC] * len(after),
        out_specs=[HBM_SPEC] * (2 * n),
        out_shape=[pltpu.HBM(a.shape, a.dtype) for a in srcs] + [pltpu.HBM(l.shape, l.dtype) for l in lands],
        input_output_aliases={i: i for i in range(2 * n)},
        compiler_params=pltpu.CompilerParams(has_side_effects=EFFECT),
    )(*srcs, *lands, send_sems, recv_sems, *after)
    return list(outs[:n]), list(outs[n:])


def _chip_partial(grad, recv, idx, name):
    r = recv.shape[0] // 4

    def body(i_ref, g_ref, s_ref, o_ref):
        del i_ref
        o_ref[...] = (g_ref[...].astype(F32) + s_ref[...].astype(F32)).astype(BF16)

    nb = 1
    tr = r // nb
    grid_spec = pltpu.PrefetchScalarGridSpec(
        num_scalar_prefetch=1, grid=(3, nb),
        in_specs=[pl.BlockSpec((tr, D), lambda t, i, i_ref: ((2 * i_ref[1 + t] + i_ref[0]) * nb + i, 0)),
                  pl.BlockSpec((tr, D), lambda t, i, i_ref: (i_ref[1 + t] * nb + i, 0))],
        out_specs=pl.BlockSpec((tr, D), lambda t, i, i_ref: (i_ref[1 + t] * nb + i, 0)))
    return _call(body, name=name, grid_spec=grid_spec, out_shape=_sds((4 * r, D), BF16),
                 compiler_params=_params("arbitrary", "arbitrary"))(idx, grad, recv)


def _adamw_math(w, g, m, v):
    m2 = B1 * m + (1.0 - B1) * g
    v2 = B2 * v + (1.0 - B2) * jnp.square(g)
    m_hat = m2 / (1.0 - B1 ** STEP)
    v_hat = v2 / (1.0 - B2 ** STEP)
    return -LR * (m_hat / (jnp.sqrt(v_hat) + EPS_ADAM) + WD * w), m2, v2


def _reduce_adamw(w, grad, from_sibling, from_chips, idx, m, v, name):
    r = w.shape[0]
    assert grad.shape == (N_DEV * r, D) and from_sibling.shape == (4 * r, D) and from_chips.shape == (3 * r, D)
    tr = r // 2
    nb = r // tr

    def body(i_ref, w_ref, p_ref, s_ref, r0_ref, r1_ref, r2_ref, m_ref, v_ref, g_ref, d_ref, nm_ref, nv_ref):
        del i_ref
        g = p_ref[...].astype(F32) + s_ref[...].astype(F32)
        g = ((g + r0_ref[...].astype(F32)) + r1_ref[...].astype(F32)) + r2_ref[...].astype(F32)
        g_ref[...] = g
        d_ref[...], nm_ref[...], nv_ref[...] = _adamw_math(w_ref[...], g, m_ref[...], v_ref[...])

    own = pl.BlockSpec((tr, D), lambda i, i_ref: (i, 0))
    grid_spec = pltpu.PrefetchScalarGridSpec(
        num_scalar_prefetch=1, grid=(nb,),
        in_specs=[own, pl.BlockSpec((tr, D), lambda i, i_ref: (i_ref[0] * nb + i, 0)),
                  pl.BlockSpec((tr, D), lambda i, i_ref: (i_ref[1] * nb + i, 0))]
        + [pl.BlockSpec((tr, D), lambda i, i_ref, j=j: (j * nb + i, 0)) for j in range(3)] + [own, own],
        out_specs=[own] * 4)
    return _call(body, name=name, grid_spec=grid_spec, out_shape=[_sds((r, D), F32)] * 4,
                 compiler_params=_params("parallel"))(idx, w, grad, from_sibling, from_chips, from_chips, from_chips, m, v)


SMALL_ROWS = 8


def _small_all_reduce(pack, name, after=()):
    def body(p_ref, *rest):
        tot_ref, loss_ref, gath, send_sems, recv_sems = rest[len(after):]
        x, y, c = _position()
        me_id = 4 * x + 2 * y + c
        gath[me_id] = p_ref[...]
        copies = []
        for k in range(1, N_DEV):
            peer = tuple(1 - v if (k >> b) & 1 else v for v, b in ((x, 2), (y, 1), (c, 0)))
            cp = pltpu.make_async_remote_copy(src_ref=p_ref, dst_ref=gath.at[me_id], send_sem=send_sems.at[k - 1],
                                              recv_sem=recv_sems.at[k - 1], device_id=peer, device_id_type=MESH)
            cp.start()
            copies.append(cp)
        for cp in copies:
            cp.wait_recv()
        for cp in copies:
            cp.wait_send()
        tot = gath[0]
        for d in range(1, N_DEV):
            tot = tot + gath[d]
        tot_ref[...] = tot
        loss_ref[...] = jnp.full((1, 128), (0.5 / D) * jnp.sum(tot[SMALL_ROWS - 1:SMALL_ROWS, :]), F32)

    vm = pl.BlockSpec(memory_space=pltpu.VMEM)
    return _call(
        body, name=name, in_specs=[vm] + [HBM_SPEC] * len(after), out_specs=[vm, vm],
        out_shape=[_sds((SMALL_ROWS, D), F32), _sds((1, 128), F32)],
        scratch_shapes=[pltpu.VMEM((N_DEV, SMALL_ROWS, D), F32), pltpu.SemaphoreType.DMA((N_DEV - 1,)),
                        pltpu.SemaphoreType.DMA((N_DEV - 1,))],
    )(pack, *after)


def _adamw(w, g, m, v, name):
    r, cdim = w.shape
    tr = 256 if r % 256 == 0 else (r // 2 if r % 16 == 0 else r)

    def body(w_ref, g_ref, m_ref, v_ref, d_ref, nm_ref, nv_ref):
        d_ref[...], nm_ref[...], nv_ref[...] = _adamw_math(w_ref[...], g_ref[...], m_ref[...], v_ref[...])

    spec = pl.BlockSpec((tr, cdim), lambda i: (i, 0))
    return _call(
        body, name=name, grid=(r // tr,), in_specs=[spec] * 4, out_specs=[spec] * 3,
        out_shape=[_sds((r, cdim), F32)] * 3, compiler_params=_params("parallel"),
    )(w, g, m, v)


def kernel(x, g_mix, w_in, conv_w, attn_sinks, w_conv_out, w_attn_out, w_o, g_ffn, w_gate_up, w_down, g_final, loss_target, m_g_mix, m_w_in, m_conv_w, m_attn_sinks, m_w_conv_out, m_w_attn_out, m_w_o, m_g_ffn, m_w_gate_up, m_w_down, m_g_final, v_g_mix, v_w_in, v_conv_w, v_attn_sinks, v_w_conv_out, v_w_attn_out, v_w_o, v_g_ffn, v_w_gate_up, v_w_down, v_g_final):
    cx, cy, cc = _position()
    chip = 2 * cx + cy
    partial_idx = jnp.stack([cc, 2 * (1 - cx) + cy, 2 * cx + (1 - cy), 2 * (1 - cx) + (1 - cy)]).astype(jnp.int32)
    own_idx = jnp.stack([2 * chip + cc, chip]).astype(jnp.int32)
    me = 4 * cx + 2 * cy + cc

    me_idx = jnp.reshape(me, (1,)).astype(jnp.int32)
    first = [_place(jnp.transpose(w_in[0]), me_idx, BF16, "place_w_in"),
             _place(jnp.pad(conv_w[0], ((0, 5), (0, 0))), me_idx, F32, "place_conv_w")]
    (sems_in,), first, token_in = _gather_start(first, [[0, 1]], "gather_start_in")
    later = [_place(w, me_idx, BF16, "place_" + k, after=(token_in,)) for k, w in (
        ("w_conv_out", w_conv_out[0]), ("w_attn_out", w_attn_out[0]), ("w_o", w_o[0]),
        ("w_gate_up", jnp.transpose(w_gate_up[0])), ("w_down", w_down[0]))]
    (sems_mix, sems_ffn), later, token_later = _gather_start(later, [[0, 1, 2], [3, 4]], "gather_start_later")
    gather_tokens = (token_in, token_later)

    class Gathered:
        def __init__(self):
            self.state = {"in": (sems_in, first), "mix": (sems_mix, later[:3]), "ffn": (sems_ffn, later[3:])}

        def begin(self, group, after):
            (send_sems, recv_sems), group_bufs = self.state[group]
            send2, recv2, group_bufs, token = _gather_forward(send_sems, recv_sems, group_bufs, after, "gather_forward_" + group)
            self.state[group] = ((send2, recv2), group_bufs)
            return (token,)

        def end(self, group, after):
            (send2, recv2), group_bufs = self.state[group]
            full = _gather_done(send2, recv2, group_bufs, after, "gather_done_" + group)
            if group == "in":
                return full[0], jnp.transpose(full[1].reshape(N_DEV, 8, 128)[:, :3, :], (1, 0, 2)).reshape(3, D)
            return full

    in_flight, own_pieces = {}, {}

    class Reducer:
        def start(self, group, gdict):
            keys, glist = list(gdict), list(gdict.values())
            send_sems, recv_sems, glist, lands, token = _exchange_start(glist, N_DEV, _to_sibling, "rs_sibling_start_" + group)
            in_flight[group] = (keys, send_sems, recv_sems, glist, lands)
            return (token,)

        def middle(self, group, after):
            keys, send_sems, recv_sems, glist, lands = in_flight[group]
            glist, lands = _exchange_wait(send_sems, recv_sems, glist, lands, after, "rs_sibling_wait_" + group)
            parts = [_chip_partial(g, r, partial_idx, "chip_partial_" + k) for k, g, r in zip(keys, glist, lands)]
            send_sems, recv_sems, parts, from_chips, token = _exchange_start(parts, 4, _to_chips, "rs_chips_start_" + group)
            in_flight[group] = (keys, send_sems, recv_sems, parts, from_chips)
            own_pieces[group] = (glist, lands)
            return (token,)

    dx, _, small = _local_step(x[0], loss_target[0], g_mix, g_ffn, g_final[None], attn_sinks, Gathered(),
                               reducer=Reducer(), after=gather_tokens)

    transposed = ("w_in", "w_gate_up")

    def as2d(k, a):
        if k in transposed:
            return jnp.transpose(a[0])
        return a[None] if a.ndim == 1 else (a[0] if a.ndim == 3 else a)

    w_all = {"g_mix": g_mix, "w_in": w_in, "conv_w": conv_w, "attn_sinks": attn_sinks, "w_conv_out": w_conv_out,
             "w_attn_out": w_attn_out, "w_o": w_o, "g_ffn": g_ffn, "w_gate_up": w_gate_up, "w_down": w_down, "g_final": g_final}
    m_all = {"g_mix": m_g_mix, "w_in": m_w_in, "conv_w": m_conv_w, "attn_sinks": m_attn_sinks, "w_conv_out": m_w_conv_out,
             "w_attn_out": m_w_attn_out, "w_o": m_w_o, "g_ffn": m_g_ffn, "w_gate_up": m_w_gate_up, "w_down": m_w_down,
             "g_final": m_g_final}
    v_all = {"g_mix": v_g_mix, "w_in": v_w_in, "conv_w": v_conv_w, "attn_sinks": v_attn_sinks, "w_conv_out": v_w_conv_out,
             "w_attn_out": v_w_attn_out, "w_o": v_w_o, "g_ffn": v_g_ffn, "w_gate_up": v_w_gate_up, "w_down": v_w_down,
             "g_final": v_g_final}
    results = {}

    def update(k, g=None, pieces=None):
        w2, m2, v2 = as2d(k, w_all[k]), as2d(k, m_all[k]), as2d(k, v_all[k])
        if g is None:
            g, d, nm, nv = _reduce_adamw(w2, *pieces, own_idx, m2, v2, "adamw_" + k)
        else:
            d, nm, nv = _adamw(w2, g, m2, v2, "adamw_" + k)
        results[k] = [(jnp.transpose(val) if k in transposed else val).reshape(w_all[k].shape) for val in (g, d, nm, nv)]
        return nm

    kernel_name = {"win_t": "w_in", "wgu_t": "w_gate_up", "wd": "w_down", "wco": "w_conv_out", "wao": "w_attn_out", "wo": "w_o"}

    def finish(group, after):
        keys, send_sems, recv_sems, parts, from_chips = in_flight[group]
        _, from_chips = _exchange_wait(send_sems, recv_sems, parts, from_chips, after, "rs_chips_wait_" + group)
        grads, from_sibling = own_pieces[group]
        return tuple(update(kernel_name[k], pieces=p) for k, *p in zip(keys, grads, from_sibling, from_chips))

    after = finish("mix", finish("ffn", (dx,)))

    sinks_row = jnp.pad(small["sinks"], ((0, 0), (0, D - 128)))
    pack = jnp.concatenate([small["g_mix"], small["g_ffn"], small["g_final"], small["conv_w"], sinks_row, small["lossvec"]], axis=0)
    tot, loss_row = _small_all_reduce(pack, "small_all_reduce", after=after)
    loss = loss_row[0, 0]
    g_small = {
        "g_mix": tot[0:1], "g_ffn": tot[1:2], "g_final": tot[2:3],
        "conv_w": lax.dynamic_slice(tot, (3, me * 128), (3, 128)), "attn_sinks": tot[6:7, :N_HEADS],
    }
    finish("in", tuple(update(k, g) for k, g in g_small.items()))

    order = ["g_mix", "w_in", "conv_w", "attn_sinks", "w_conv_out", "w_attn_out", "w_o", "g_ffn", "w_gate_up", "w_down", "g_final"]
    return (loss, dx[None], *[results[k][i] for i in range(4) for k in order])
```

```python
import functools
import math

import jax
import jax.numpy as jnp
from jax import lax
from jax.experimental import pallas as pl
from jax.experimental.pallas import tpu as pltpu

F32 = jnp.float32
BF16 = jnp.bfloat16

D = 1024
HEAD_DIM = 64
N_HEADS = 16
N_KV = 4
GROUP = N_HEADS // N_KV
D_KV = N_KV * HEAD_DIM
BLOCK = 128
ROT_DIM = HEAD_DIM // 4
ROPE_THETA = 500000.0
ATTN_SCALE = 1.0 / math.sqrt(HEAD_DIM)
NEG_INF = -1e30
D_FF = 2816
N_IN = 6656
EPS = 1e-5
C_CB, C_CC, C_CX, C_Q, C_K, C_V, C_GC, C_GA = 0, 1024, 2048, 3072, 4096, 4352, 4608, 5632

LR, B1, B2, EPS_ADAM, WD, STEP = 0.001, 0.9, 0.999, 1e-08, 0.01, 10

N_DEV = 8
MESH = pl.DeviceIdType.MESH
VMEM_LIMIT = 56 * 1024 * 1024

NN = (((1,), (0,)), ((), ()))
NT = (((1,), (1,)), ((), ()))
TN = (((0,), (0,)), ((), ()))
HBM_SPEC = pl.BlockSpec(memory_space=pl.ANY)
ROW_SPLIT = 4


def _call(body, **kw):
    return pl.pallas_call(body, **kw)


def _params(*sem):
    return pltpu.CompilerParams(dimension_semantics=sem, vmem_limit_bytes=VMEM_LIMIT)


def _sds(shape, dtype):
    return jax.ShapeDtypeStruct(shape, dtype)


def _matmul(a, b, *, mode, tm, tn, tk, out_dtype, name, res=None, after=()):
    parts = list(a) if isinstance(a, (list, tuple)) else [a]
    rows_a = parts[0].shape[0]
    cols_a = sum(p.shape[1] for p in parts)
    if mode == "nn":
        (m, kk), (_, n), dims = (rows_a, cols_a), b.shape, NN
    elif mode == "nt":
        (m, kk), (n, _), dims = (rows_a, cols_a), b.shape, NT
    else:
        (kk, m), (_, n), dims = (rows_a, cols_a), b.shape, TN
    tm, tn, tk = min(tm, m), min(tn, n), min(tk, kk)
    assert m % tm == 0 and n % tn == 0 and kk % tk == 0, (name, m, n, kk, tm, tn, tk)
    nk = kk // tk
    split_axis, width = (2, tk) if mode == "nn" else (0, tm)
    assert len(parts) == 1 or mode in ("nn", "tn")
    assert len(parts) == 1 or all(p.shape[1] % width == 0 for p in parts), (name, width)
    counts = [p.shape[1] // width for p in parts]
    starts = [sum(counts[:p]) for p in range(len(parts))]

    def a_spec(p):
        def col(t):
            return jnp.clip(t - starts[p], 0, counts[p] - 1) if len(parts) > 1 else t

        if mode == "tn":
            return pl.BlockSpec((tk, tm), lambda i, j, k: (k, col(i)))
        return pl.BlockSpec((tm, tk), lambda i, j, k: (i, col(k)))

    if mode == "nt":
        b_spec = pl.BlockSpec((tn, tk), lambda i, j, k: (j, k))
    else:
        b_spec = pl.BlockSpec((tk, tn), lambda i, j, k: (k, j))
    o_spec = pl.BlockSpec((tm, tn), lambda i, j, k: (i, j))
    has_res = res is not None
    n_parts = len(parts)
    unit = 128 if mode == "tn" else 16
    split = ROW_SPLIT if tm % (ROW_SPLIT * unit) == 0 else 1

    def body(*refs):
        a_refs, b_ref = refs[:n_parts], refs[n_parts]
        r_ref = refs[n_parts + 1] if has_res else None
        o_ref = refs[n_parts + 1 + has_res + len(after)]
        k = pl.program_id(2)

        acc_ref = refs[-1] if nk > 1 else None

        def step(a_ref):
            def matmul(rows):
                a_blk = a_ref[:, rows] if mode == "tn" else a_ref[rows, :]
                return lax.dot_general(a_blk, b_ref[...], dims, preferred_element_type=F32)

            def finish(rows, part):
                if nk > 1:
                    acc_ref[rows, :] += part
                else:
                    o_ref[rows, :] = (part + r_ref[rows, :] if has_res else part).astype(o_ref.dtype)

            _row_pipeline(tm, matmul, finish, split)

        if nk > 1:
            @pl.when(k == 0)
            def _():
                acc_ref[...] = jnp.zeros_like(acc_ref)

        if n_parts == 1:
            step(a_refs[0])
        else:
            t = pl.program_id(split_axis)
            for p in range(n_parts):
                pl.when((t >= starts[p]) & (t < starts[p] + counts[p]))(functools.partial(step, a_refs[p]))

        if nk > 1:
            @pl.when(k == nk - 1)
            def _():
                o_ref[...] = (acc_ref[...] + r_ref[...] if has_res else acc_ref[...]).astype(o_ref.dtype)

    ins = parts + [b] + ([res] if has_res else []) + list(after)
    in_specs = [a_spec(p) for p in range(n_parts)] + [b_spec] + ([o_spec] if has_res else []) + [HBM_SPEC] * len(after)
    scratch = [] if nk == 1 else [pltpu.VMEM((tm, tn), F32)]
    return _call(
        body, name=name, grid=(m // tm, n // tn, nk), in_specs=in_specs, out_specs=o_spec,
        out_shape=_sds((m, n), out_dtype), scratch_shapes=scratch,
        compiler_params=_params("parallel", "parallel", "arbitrary"),
    )(*ins)


def _row_tile(s):
    return min(512, s)


def _rms_fwd(x, g, name, after=()):
    s = x.shape[0]
    tm = _row_tile(s)

    def body(x_ref, g_ref, *rest):
        h_ref = rest[-1]
        xv = x_ref[...]
        r = lax.rsqrt(jnp.mean(xv * xv, axis=-1, keepdims=True) + EPS)
        h_ref[...] = (xv * r * g_ref[...]).astype(BF16)

    row = pl.BlockSpec((tm, D), lambda i: (i, 0))
    return _call(
        body, name=name, grid=(s // tm,), in_specs=[row, pl.BlockSpec((1, D), lambda i: (0, 0))] + [HBM_SPEC] * len(after),
        out_specs=row, out_shape=_sds((s, D), BF16), compiler_params=_params("parallel"),
    )(x, g, *after)


def _rms_bwd(dh, x, g, dres, name, after=()):
    s = x.shape[0]
    tm = _row_tile(s)

    def body(dh_ref, x_ref, g_ref, dres_ref, *rest):
        dx_ref, dxb_ref, dg_ref = rest[len(after):]
        xv = x_ref[...]
        r = lax.rsqrt(jnp.mean(xv * xv, axis=-1, keepdims=True) + EPS)
        xh = xv * r
        dhv = dh_ref[...]
        dyg = dhv * g_ref[...]
        dx = dres_ref[...] + r * (dyg - xh * jnp.mean(dyg * xh, axis=-1, keepdims=True))
        dx_ref[...] = dx
        dxb_ref[...] = dx.astype(BF16)
        part = jnp.sum(dhv * xh, axis=0, keepdims=True)

        @pl.when(pl.program_id(0) == 0)
        def _():
            dg_ref[...] = part

        @pl.when(pl.program_id(0) > 0)
        def _():
            dg_ref[...] += part

    row = pl.BlockSpec((tm, D), lambda i: (i, 0))
    vec = pl.BlockSpec((1, D), lambda i: (0, 0))
    return _call(
        body, name=name, grid=(s // tm,), in_specs=[row, row, vec, row] + [HBM_SPEC] * len(after), out_specs=[row, row, vec],
        out_shape=[_sds((s, D), F32), _sds((s, D), BF16), _sds((1, D), F32)],
        compiler_params=_params("arbitrary"),
    )(dh, x, g, dres, *after)


def _loss_head(x2, g, tgt, name):
    s = x2.shape[0]
    tm = _row_tile(s)

    def body(x_ref, g_ref, t_ref, dx_ref, dxb_ref, dg_ref, l_ref):
        xv = x_ref[...]
        gv = g_ref[...]
        r = lax.rsqrt(jnp.mean(xv * xv, axis=-1, keepdims=True) + EPS)
        xh = xv * r
        err = xh * gv - t_ref[...]
        dy = err * (1.0 / D)
        dyg = dy * gv
        dx = r * (dyg - xh * jnp.mean(dyg * xh, axis=-1, keepdims=True))
        dx_ref[...] = dx
        dxb_ref[...] = dx.astype(BF16)
        dg_part = jnp.sum(dy * xh, axis=0, keepdims=True)
        l_part = jnp.sum(err * err, axis=0, keepdims=True)

        @pl.when(pl.program_id(0) == 0)
        def _():
            dg_ref[...] = dg_part
            l_ref[...] = l_part

        @pl.when(pl.program_id(0) > 0)
        def _():
            dg_ref[...] += dg_part
            l_ref[...] += l_part

    row = pl.BlockSpec((tm, D), lambda i: (i, 0))
    vec = pl.BlockSpec((1, D), lambda i: (0, 0))
    return _call(
        body, name=name, grid=(s // tm,), in_specs=[row, vec, row], out_specs=[row, row, vec, vec],
        out_shape=[_sds((s, D), F32), _sds((s, D), BF16), _sds((1, D), F32), _sds((1, D), F32)],
        compiler_params=_params("arbitrary"),
    )(x2, g, tgt)


CONV_TC = 256


def _shift_down(u, k, rows):
    return jnp.where(rows >= k, pltpu.roll(u, k, 0), 0.0)


def _shift_up(u, k, rows, s):
    return jnp.where(rows < s - k, pltpu.roll(u, s - k, 0), 0.0)


def _conv_specs(s):
    nb = D // CONV_TC

    def col(c0):
        return pl.BlockSpec((s, CONV_TC), lambda j, c0=c0: (0, c0 // CONV_TC + j))

    return nb, col


def _conv_fwd(proj, conv_w, name):
    s = proj.shape[0]
    nb, col = _conv_specs(s)

    def body(cb_ref, cc_ref, cx_ref, w_ref, y_ref):
        rows = lax.broadcasted_iota(jnp.int32, (s, CONV_TC), 0)
        u = cc_ref[...].astype(F32) * cx_ref[...].astype(F32)
        w = w_ref[...]
        c = w[0:1] * _shift_down(u, 2, rows) + w[1:2] * _shift_down(u, 1, rows) + w[2:3] * u
        y_ref[...] = (cb_ref[...].astype(F32) * c).astype(BF16)

    return _call(
        body, name=name, grid=(nb,),
        in_specs=[col(C_CB), col(C_CC), col(C_CX), pl.BlockSpec((3, CONV_TC), lambda j: (0, j))],
        out_specs=pl.BlockSpec((s, CONV_TC), lambda j: (0, j)), out_shape=_sds((s, D), BF16),
        compiler_params=_params("parallel"),
    )(proj, proj, proj, conv_w)


def _write_behind(t, nt, buf, sems, tiles, window, where):
    slot = t % 2

    def copies(sl, at):
        return [pltpu.make_async_copy(buf.at[sl, p], window(p, at), sems.at[sl, p]) for p in range(len(tiles))]

    @pl.when(t >= 2)
    def _():
        for cp in copies(slot, where):
            cp.wait()

    for p, tile in enumerate(tiles):
        buf[slot, p] = tile
    started = copies(slot, where)
    for cp in started:
        cp.start()

    @pl.when(t == nt - 1)
    def _():
        for cp in started:
            cp.wait()
        if nt > 1:
            for cp in copies(1 - slot, where):
                cp.wait()


def _conv_bwd(dy, proj, conv_w, dproj, name, after=()):
    s = proj.shape[0]
    nb, col = _conv_specs(s)

    def body(dy_ref, cb_ref, cc_ref, cx_ref, w_ref, *rest):
        dproj_ref, dw_ref, buf, sems = rest[1 + len(after):]
        j = pl.program_id(0)
        rows = lax.broadcasted_iota(jnp.int32, (s, CONV_TC), 0)
        cc = cc_ref[...].astype(F32)
        cx = cx_ref[...].astype(F32)
        u = cc * cx
        u1 = _shift_down(u, 1, rows)
        u2 = _shift_down(u, 2, rows)
        w = w_ref[...]
        c = w[0:1] * u2 + w[1:2] * u1 + w[2:3] * u
        dyv = dy_ref[...].astype(F32)
        dc = dyv * cb_ref[...].astype(F32)
        du = w[2:3] * dc + w[1:2] * _shift_up(dc, 1, rows, s) + w[0:1] * _shift_up(dc, 2, rows, s)

        def window(p, jj):
            start = pl.multiple_of((C_CB, C_CC, C_CX)[p] + jj * CONV_TC, CONV_TC)
            return dproj_ref.at[:, pl.ds(start, CONV_TC)]

        tiles = ((dyv * c).astype(BF16), (du * cx).astype(BF16), (du * cc).astype(BF16))
        _write_behind(j * 0, 1, buf, sems, tiles, window, j)
        dw_ref[...] = jnp.concatenate(
            [jnp.sum(dc * u2, axis=0, keepdims=True), jnp.sum(dc * u1, axis=0, keepdims=True),
             jnp.sum(dc * u, axis=0, keepdims=True)], axis=0)

    return _call(
        body, name=name, grid=(nb,),
        in_specs=[pl.BlockSpec((s, CONV_TC), lambda j: (0, j)), col(C_CB), col(C_CC), col(C_CX),
                  pl.BlockSpec((3, CONV_TC), lambda j: (0, j))] + [HBM_SPEC] * (1 + len(after)),
        out_specs=[pl.BlockSpec(memory_space=pl.ANY), pl.BlockSpec((3, CONV_TC), lambda j: (0, j))],
        out_shape=[_sds((s, N_IN), BF16), _sds((3, D), F32)],
        scratch_shapes=[pltpu.VMEM((1, 3, s, CONV_TC), BF16), pltpu.SemaphoreType.DMA((1, 3))],
        input_output_aliases={5: 0}, compiler_params=_params("arbitrary"),
    )(dy, proj, proj, proj, conv_w, dproj, *after)


def _rope_tables(s):
    half = ROT_DIM // 2
    inv_freq = ROPE_THETA ** (-jnp.arange(0, ROT_DIM, 2, dtype=F32) / ROT_DIM)
    inv64 = jnp.concatenate([inv_freq, inv_freq, jnp.zeros((HEAD_DIM - ROT_DIM,), F32)])
    ang = jnp.arange(s, dtype=F32)[:, None] * jnp.concatenate([inv64, inv64])[None, :]
    d = lax.broadcasted_iota(jnp.int32, (s, 128), 1) % HEAD_DIM
    cos, sin = jnp.cos(ang), jnp.sin(ang)
    c = jnp.where(d < ROT_DIM, cos, 1.0)
    a = jnp.where(d < half, -sin, 0.0)
    b = jnp.where((d >= half) & (d < ROT_DIM), sin, 0.0)
    return jnp.concatenate([c, a, b], axis=1)


def _rope(x, tab):
    c, a, b = tab[:, 0:128], tab[:, 128:256], tab[:, 256:384]
    outs = []
    for i in range(x.shape[1] // 128):
        xc = x[:, i * 128:(i + 1) * 128]
        outs.append(xc * c + pltpu.roll(xc, 120, 1) * a + pltpu.roll(xc, 8, 1) * b)
    return outs[0] if len(outs) == 1 else jnp.concatenate(outs, axis=1)


def _rope_t(dx, tab):
    c, a, b = tab[:, 0:128], tab[:, 128:256], tab[:, 256:384]
    outs = []
    for i in range(dx.shape[1] // 128):
        dc = dx[:, i * 128:(i + 1) * 128]
        outs.append(dc * c + pltpu.roll(dc * a, 8, 1) + pltpu.roll(dc * b, 120, 1))
    return outs[0] if len(outs) == 1 else jnp.concatenate(outs, axis=1)


def _attn_in_specs():
    prev = lambda n: jnp.maximum(n - 1, 0)
    return [
        pl.BlockSpec((BLOCK, D), lambda n: (n, C_Q // D)),
        pl.BlockSpec((BLOCK, D_KV), lambda n: (n, C_K // D_KV)),
        pl.BlockSpec((BLOCK, D_KV), lambda n: (prev(n), C_K // D_KV)),
        pl.BlockSpec((BLOCK, D_KV), lambda n: (n, C_V // D_KV)),
        pl.BlockSpec((BLOCK, D_KV), lambda n: (prev(n), C_V // D_KV)),
        pl.BlockSpec((BLOCK, 384), lambda n: (n, 0)),
        pl.BlockSpec((BLOCK, 384), lambda n: (prev(n), 0)),
        pl.BlockSpec(memory_space=pltpu.SMEM),
    ]


HALF = HEAD_DIM
N_CHUNK = D // 128


def _swa_bias(n):
    qi = lax.broadcasted_iota(jnp.int32, (BLOCK, 2 * BLOCK), 0)
    kj = lax.broadcasted_iota(jnp.int32, (BLOCK, 2 * BLOCK), 1)
    rel = qi + BLOCK - kj
    valid = (rel >= 0) & (rel < BLOCK) & ((kj >= BLOCK) | (n > 0))
    return jnp.where(valid, 0.0, NEG_INF)


def _halves(x):
    lo = lax.broadcasted_iota(jnp.int32, x.shape, 1) < HALF
    return jnp.where(lo, x, 0.0).astype(BF16), jnp.where(lo, 0.0, x).astype(BF16)


def _dup_heads(x):
    out = []
    for pair in range(N_KV // 2):
        xc = x[:, pair * 128:(pair + 1) * 128]
        xr = pltpu.roll(xc, HALF, 1)
        lo = lax.broadcasted_iota(jnp.int32, xc.shape, 1) < HALF
        out += [jnp.where(lo, xc, xr), jnp.where(lo, xr, xc)]
    return out


def _swa_load(q_ref, kc_ref, kp_ref, vc_ref, vp_ref, tc_ref, tp_ref):
    qf = _rope(q_ref[...].astype(F32), tc_ref[...]) * ATTN_SCALE
    q_halves = [_halves(qf[:, c * 128:(c + 1) * 128]) for c in range(N_CHUNK)]
    kf = jnp.concatenate([_rope(kp_ref[...].astype(F32), tp_ref[...]), _rope(kc_ref[...].astype(F32), tc_ref[...])], axis=0)
    vf = jnp.concatenate([vp_ref[...], vc_ref[...]], axis=0).astype(F32)
    return q_halves, _dup_heads(kf), _dup_heads(vf)


def _swa_probs(qh, kk, bias, sink):
    s = lax.dot_general(qh, kk, NT, preferred_element_type=F32) + bias
    m = jnp.maximum(jnp.max(jnp.maximum(s[:, :BLOCK], s[:, BLOCK:]), axis=1, keepdims=True), sink)
    return jnp.exp(s - m), m


def _swa_fwd(proj, tab, sinks, name, after=()):
    s = proj.shape[0]

    def body(q_ref, kc_ref, kp_ref, vc_ref, vp_ref, tc_ref, tp_ref, sink_ref, *rest):
        o_ref = rest[-1]
        n = pl.program_id(0)
        q_halves, kdup, vdup = _swa_load(q_ref, kc_ref, kp_ref, vc_ref, vp_ref, tc_ref, tp_ref)
        bias = _swa_bias(n)
        ones = jnp.ones((2 * BLOCK, 128), BF16)
        kk = [k.astype(BF16) for k in kdup]
        vv = [[jnp.concatenate([v_half, ones], axis=1) for v_half in _halves(v)] for v in vdup]
        heads = [(c, half) for c in range(N_CHUNK) for half in range(2)]
        scores = [lax.dot_general(q_halves[c][half], kk[c // (GROUP // 2)], NT, preferred_element_type=F32)
                  for c, half in heads]
        probs = []
        for (c, half), sc in zip(heads, scores):
            sc = sc + bias
            m = jnp.maximum(jnp.max(jnp.maximum(sc[:, :BLOCK], sc[:, BLOCK:]), axis=1, keepdims=True), sink_ref[0, 2 * c + half])
            probs.append((jnp.exp(sc - m).astype(BF16), jnp.exp(sink_ref[0, 2 * c + half] - m)))
        outs = [lax.dot_general(e, vv[c // (GROUP // 2)][half], NN, preferred_element_type=F32)
                for (c, half), (e, _) in zip(heads, probs)]
        for c in range(N_CHUNK):
            parts = [outs[2 * c + half][:, :128] * (1.0 / (outs[2 * c + half][:, 128:] + probs[2 * c + half][1]))
                     for half in range(2)]
            o_ref[:, c * 128:(c + 1) * 128] = (parts[0] + parts[1]).astype(BF16)

    return _call(
        body, name=name, grid=(s // BLOCK,), in_specs=_attn_in_specs() + [HBM_SPEC] * len(after),
        out_specs=pl.BlockSpec((BLOCK, D), lambda n: (n, 0)), out_shape=_sds((s, D), BF16),
        compiler_params=_params("parallel"),
    )(proj, proj, proj, proj, proj, tab, tab, sinks, *after)


def _swa_bwd(do, proj, tab, sinks, dproj, name, after=()):
    s = proj.shape[0]
    nblk = s // BLOCK
    kv_of = lambda c: c // (GROUP // 2)

    def body(do_ref, q_ref, kc_ref, kp_ref, vc_ref, vp_ref, tc_ref, tp_ref, sink_ref, *rest):
        dproj_ref, dk_ref, dv_ref, ds_ref, dqout, dkbuf, dvbuf, sems = rest[1 + len(after):]
        n = pl.program_id(0)

        @pl.when(n == 0)
        def _():
            dk_ref[...] = jnp.zeros_like(dk_ref)
            dv_ref[...] = jnp.zeros_like(dv_ref)
            ds_ref[...] = jnp.zeros_like(ds_ref)

        q_halves, kdup, vdup = _swa_load(q_ref, kc_ref, kp_ref, vc_ref, vp_ref, tc_ref, tp_ref)
        dof = do_ref[...].astype(F32)
        do_halves = [_halves(dof[:, c * 128:(c + 1) * 128]) for c in range(N_CHUNK)]
        bias = _swa_bias(n)
        ones = jnp.ones((2 * BLOCK, 128), BF16)
        kk = [k.astype(BF16) for k in kdup]
        vv = [v.astype(BF16) for v in vdup]
        k_halves = [_halves(k) for k in kdup]
        heads = [(c, half) for c in range(N_CHUNK) for half in range(2)]
        lane_row = lax.broadcasted_iota(jnp.int32, (1, 128), 1)
        lo_kv = lax.broadcasted_iota(jnp.int32, (2 * BLOCK, 128), 1) < HALF
        scores = [lax.dot_general(q_halves[c][half], kk[kv_of(c)], NT, preferred_element_type=F32) for c, half in heads]
        dps = [lax.dot_general(do_halves[c][half], vv[kv_of(c)], NT, preferred_element_type=F32) for c, half in heads]
        exps = []
        for (c, half), sc in zip(heads, scores):
            sink = sink_ref[0, 2 * c + half]
            sc = sc + bias
            m = jnp.maximum(jnp.max(jnp.maximum(sc[:, :BLOCK], sc[:, BLOCK:]), axis=1, keepdims=True), sink)
            exps.append((jnp.exp(sc - m), jnp.exp(sink - m)))
        sums = [lax.dot_general(e.astype(BF16), ones, NN, preferred_element_type=F32) for e, _ in exps]
        dsink_row = jnp.zeros((1, 128), F32)
        dsb, pb = [], []
        for h, ((e, es), row_sum, dp) in enumerate(zip(exps, sums, dps)):
            inv = 1.0 / (row_sum + es)
            p = e * jnp.concatenate([inv, inv], axis=1)
            t = p * dp
            delta = jnp.sum(t, axis=1, keepdims=True)
            dsb.append((t - p * delta).astype(BF16))
            pb.append(p.astype(BF16))
            dsink = -jnp.sum(es * inv * delta, axis=0, keepdims=True)
            dsink_row = dsink_row + jnp.where(lane_row == h, dsink, 0.0)
        dq_parts = [lax.dot_general(d, k_halves[kv_of(c)][half], NN, preferred_element_type=F32) for (c, half), d in zip(heads, dsb)]
        dk_parts = [lax.dot_general(d, q_halves[c][half], TN, preferred_element_type=F32) for (c, half), d in zip(heads, dsb)]
        dv_parts = [lax.dot_general(p, do_halves[c][half], TN, preferred_element_type=F32) for (c, half), p in zip(heads, pb)]
        dq = jnp.concatenate([(dq_parts[2 * c] + dq_parts[2 * c + 1]) * ATTN_SCALE for c in range(N_CHUNK)], axis=1)

        def kv_sum(parts, hk):
            acc = (parts[GROUP * hk] + parts[GROUP * hk + 1]) + (parts[GROUP * hk + 2] + parts[GROUP * hk + 3])
            return acc + pltpu.roll(acc, HALF, 1)

        for pair in range(N_KV // 2):
            dkbuf[:, pair * 128:(pair + 1) * 128] = jnp.where(lo_kv, kv_sum(dk_parts, 2 * pair), kv_sum(dk_parts, 2 * pair + 1))
            dvbuf[:, pair * 128:(pair + 1) * 128] = jnp.where(lo_kv, kv_sum(dv_parts, 2 * pair), kv_sum(dv_parts, 2 * pair + 1))
        prev0 = pl.multiple_of(jnp.maximum(n - 1, 0) * BLOCK, BLOCK)
        cur0 = pl.multiple_of(n * BLOCK, BLOCK)

        @pl.when(n > 0)
        def _():
            dk_ref[pl.ds(prev0, BLOCK), :] += dkbuf[0:BLOCK, :]
            dv_ref[pl.ds(prev0, BLOCK), :] += dvbuf[0:BLOCK, :]

        dk_ref[pl.ds(cur0, BLOCK), :] += dkbuf[BLOCK:2 * BLOCK, :]
        dv_ref[pl.ds(cur0, BLOCK), :] += dvbuf[BLOCK:2 * BLOCK, :]
        ds_ref[...] += dsink_row

        def window(p, at):
            return dproj_ref.at[pl.ds(pl.multiple_of(at * BLOCK, BLOCK), BLOCK), pl.ds(C_Q, D)]

        _write_behind(n, nblk, dqout, sems, (_rope_t(dq, tc_ref[...]).astype(BF16),), window, n)

    blk = lambda w: pl.BlockSpec((BLOCK, w), lambda n: (n, 0))
    whole = lambda w: pl.BlockSpec((s, w), lambda n: (0, 0))
    n_in = 1 + len(_attn_in_specs())
    return _call(
        body, name=name, grid=(nblk,), in_specs=[blk(D)] + _attn_in_specs() + [HBM_SPEC] * (1 + len(after)),
        out_specs=[HBM_SPEC, whole(D_KV), whole(D_KV), pl.BlockSpec((1, 128), lambda n: (0, 0))],
        out_shape=[_sds((s, N_IN), BF16), _sds((s, D_KV), F32), _sds((s, D_KV), F32), _sds((1, 128), F32)],
        scratch_shapes=[pltpu.VMEM((2, 1, BLOCK, D), BF16), pltpu.VMEM((2 * BLOCK, D_KV), F32),
                        pltpu.VMEM((2 * BLOCK, D_KV), F32), pltpu.SemaphoreType.DMA((2, 1))],
        input_output_aliases={n_in: 0}, compiler_params=_params("arbitrary"),
    )(do, proj, proj, proj, proj, proj, tab, tab, sinks, dproj, *after)


def _kv_bwd(dkr, dv, tab, dproj, name):
    s = dkr.shape[0]
    tm = _row_tile(s)

    def body(dk_ref, dv_ref, t_ref, dproj_in, o_ref):
        del dproj_in
        o_ref[:, 0:D_KV] = _rope_t(dk_ref[...], t_ref[...]).astype(BF16)
        o_ref[:, D_KV:2 * D_KV] = dv_ref[...].astype(BF16)

    row = lambda w: pl.BlockSpec((tm, w), lambda i: (i, 0))
    return _call(
        body, name=name, grid=(s // tm,),
        in_specs=[row(D_KV), row(D_KV), row(384), pl.BlockSpec(memory_space=pl.ANY)],
        out_specs=pl.BlockSpec((tm, 2 * D_KV), lambda i: (i, C_K // (2 * D_KV))),
        out_shape=_sds((s, N_IN), BF16), input_output_aliases={3: 0}, compiler_params=_params("parallel"),
    )(dkr, dv, tab, dproj)


EW_TC = 512


def _sigmoid(x):
    return 0.5 * jnp.tanh(0.5 * x) + 0.5


def _merge_fwd(proj, conv_out, attn_out, name):
    s = proj.shape[0]
    tm = _row_tile(s)
    tile = pl.BlockSpec((tm, EW_TC), lambda i, j: (i, j))

    def body(gc_ref, ga_ref, co_ref, ao_ref, o_ref):
        o_ref[...] = (_sigmoid(gc_ref[...].astype(F32)) * co_ref[...].astype(F32)
                      + _sigmoid(ga_ref[...].astype(F32)) * ao_ref[...].astype(F32)).astype(BF16)

    return _call(
        body, name=name, grid=(s // tm, D // EW_TC),
        in_specs=[pl.BlockSpec((tm, EW_TC), lambda i, j: (i, C_GC // EW_TC + j)),
                  pl.BlockSpec((tm, EW_TC), lambda i, j: (i, C_GA // EW_TC + j)), tile, tile],
        out_specs=tile, out_shape=_sds((s, D), BF16), compiler_params=_params("parallel", "parallel"),
    )(proj, proj, conv_out, attn_out)


def _merge_bwd(dmerged, proj, conv_out, attn_out, name):
    s = proj.shape[0]
    tm = _row_tile(s)
    tile = pl.BlockSpec((tm, EW_TC), lambda i, j: (i, j))
    anyspec = pl.BlockSpec(memory_space=pl.ANY)

    def body(dm_ref, gc_ref, ga_ref, co_ref, ao_ref, dproj_ref, dco_ref, dao_ref, buf, sems):
        i, j = pl.program_id(0), pl.program_id(1)
        dm = dm_ref[...].astype(F32)
        sc = _sigmoid(gc_ref[...].astype(F32))
        sa = _sigmoid(ga_ref[...].astype(F32))
        dco_ref[...] = (dm * sc).astype(BF16)
        dao_ref[...] = (dm * sa).astype(BF16)
        tiles = ((dm * co_ref[...].astype(F32) * sc * (1.0 - sc)).astype(BF16),
                 (dm * ao_ref[...].astype(F32) * sa * (1.0 - sa)).astype(BF16))

        def window(p, at):
            start = pl.multiple_of((C_GC, C_GA)[p] + at[1] * EW_TC, EW_TC)
            return dproj_ref.at[pl.ds(pl.multiple_of(at[0] * tm, tm), tm), pl.ds(start, EW_TC)]

        _write_behind(i * nj + j, (s // tm) * nj, buf, sems, tiles, window, (i, j))

    nj = D // EW_TC
    return _call(
        body, name=name, grid=(s // tm, nj),
        in_specs=[tile, pl.BlockSpec((tm, EW_TC), lambda i, j: (i, C_GC // EW_TC + j)),
                  pl.BlockSpec((tm, EW_TC), lambda i, j: (i, C_GA // EW_TC + j)), tile, tile],
        out_specs=[anyspec, tile, tile],
        out_shape=[_sds((s, N_IN), BF16), _sds((s, D), BF16), _sds((s, D), BF16)],
        scratch_shapes=[pltpu.VMEM((2, 2, tm, EW_TC), BF16), pltpu.SemaphoreType.DMA((2, 2))],
        compiler_params=_params("arbitrary", "arbitrary"),
    )(dmerged, proj, proj, conv_out, attn_out)


FF_TC = 256


def _row_pipeline(tm, matmul, finish, split=ROW_SPLIT):
    step = tm // split
    pending = None
    for r in range(split):
        rows = pl.ds(r * step, step)
        result = matmul(rows)
        if pending is not None:
            finish(*pending)
        pending = (rows, result)
    finish(*pending)


def _gate_up_fwd(h2, wgu_t, name):
    s = h2.shape[0]
    tm = min(2048, s)
    nb = D_FF // FF_TC

    def body(h_ref, wg_ref, wu_ref, a_ref, dadu_ref, dadg_ref):
        def matmuls(rows):
            h = h_ref[rows, :]
            return (lax.dot_general(h, wg_ref[...], NT, preferred_element_type=F32),
                    lax.dot_general(h, wu_ref[...], NT, preferred_element_type=F32))

        def finish(rows, gu):
            g, u = gu
            sg = _sigmoid(g)
            silu = g * sg
            a_ref[rows, :] = (silu * u).astype(BF16)
            dadu_ref[rows, :] = silu.astype(BF16)
            dadg_ref[rows, :] = (u * (sg * (1.0 + g * (1.0 - sg)))).astype(BF16)

        _row_pipeline(tm, matmuls, finish)

    tile = pl.BlockSpec((tm, FF_TC), lambda i, j: (i, j))
    return _call(
        body, name=name, grid=(s // tm, nb),
        in_specs=[pl.BlockSpec((tm, D), lambda i, j: (i, 0)), pl.BlockSpec((FF_TC, D), lambda i, j: (j, 0)),
                  pl.BlockSpec((FF_TC, D), lambda i, j: (nb + j, 0))],
        out_specs=[tile, tile, tile], out_shape=[_sds((s, D_FF), BF16)] * 3,
        compiler_params=_params("parallel", "parallel"),
    )(h2, wgu_t, wgu_t)


def _down_bwd_x(dx2b, wd, dadg, dadu, name):
    s = dx2b.shape[0]
    tm = min(2048, s)
    nb = D_FF // FF_TC

    def body(dx_ref, w_ref, dadg_ref, dadu_ref, dg_ref, du_ref):
        def matmul(rows):
            return lax.dot_general(dx_ref[rows, :], w_ref[...], NT, preferred_element_type=F32)

        def finish(rows, da):
            dg_ref[rows, :] = (da * dadg_ref[rows, :].astype(F32)).astype(BF16)
            du_ref[rows, :] = (da * dadu_ref[rows, :].astype(F32)).astype(BF16)

        _row_pipeline(tm, matmul, finish)

    tile = pl.BlockSpec((tm, FF_TC), lambda i, j: (i, j))
    return _call(
        body, name=name, grid=(s // tm, nb),
        in_specs=[pl.BlockSpec((tm, D), lambda i, j: (i, 0)), pl.BlockSpec((FF_TC, D), lambda i, j: (j, 0)), tile, tile],
        out_specs=[tile, tile], out_shape=[_sds((s, D_FF), BF16)] * 2,
        compiler_params=_params("parallel", "parallel"),
    )(dx2b, wd, dadg, dadu)


class _Weights:
    def __init__(self, **groups):
        self.groups = groups

    def begin(self, group, after):
        return ()

    def end(self, group, after):
        return self.groups[group]


class _NoReduce:
    def start(self, group, grads):
        return ()

    def middle(self, group, after):
        return ()


def _local_step(x, tgt, g_mix, g_ffn, g_final, sinks, weights, reducer=None, after=()):
    reducer = reducer or _NoReduce()
    s = x.shape[0]
    tab = _rope_tables(s)
    big = dict(tm=1024, tn=512, tk=1024)
    h1 = _rms_fwd(x, g_mix, "rms1_fwd", after=after)
    win_t, conv_w = weights.end("in", weights.begin("in", (h1,)))
    proj = _matmul(h1, win_t, mode="nt", out_dtype=BF16, name="proj_fwd", tm=2048, tn=512, tk=1024)
    attn = _swa_fwd(proj, tab, sinks, "attn_fwd", after=weights.begin("mix", (proj,)))
    wco, wao, wo = weights.end("mix", (attn,))
    conv_y = _conv_fwd(proj, conv_w, "conv_fwd")
    conv_out = _matmul(conv_y, wco, mode="nn", out_dtype=BF16, name="conv_out_fwd", **big)
    attn_out = _matmul(attn, wao, mode="nn", out_dtype=BF16, name="attn_out_fwd", **big)
    merged = _merge_fwd(proj, conv_out, attn_out, "merge_fwd")
    x1 = _matmul(merged, wo, mode="nn", out_dtype=F32, name="wo_fwd", res=x, after=weights.begin("ffn", (merged,)), **big)
    h2 = _rms_fwd(x1, g_ffn, "rms2_fwd")
    wgu_t, wd = weights.end("ffn", (h2,))
    act, dadu, dadg = _gate_up_fwd(h2, wgu_t, "gate_up_fwd")
    x2 = _matmul(act, wd, mode="nn", out_dtype=F32, name="down_fwd", res=x1, tm=1024, tn=512, tk=D_FF)
    dx2, dx2b, dg_final, lossvec = _loss_head(x2, g_final, tgt, "loss_head")
    dgate, dup = _down_bwd_x(dx2b, wd, dadg, dadu, "down_bwd_x")
    g_wd = _matmul(act, dx2b, mode="tn", out_dtype=BF16, name="down_bwd_w", tm=1408, tn=1024, tk=2048)
    dh2 = _matmul([dgate, dup], wgu_t, mode="nn", out_dtype=F32, name="gate_up_bwd_x", tm=1024, tn=1024, tk=1408)
    g_wgu_t = _matmul([dgate, dup], h2, mode="tn", out_dtype=BF16, name="gate_up_bwd_w", tm=1408, tn=1024, tk=2048)
    after_ffn = reducer.start("ffn", dict(wgu_t=g_wgu_t, wd=g_wd))
    dx1, dx1b, dg_ffn = _rms_bwd(dh2, x1, g_ffn, dx2, "rms2_bwd")
    dmerged = _matmul(dx1b, wo, mode="nt", out_dtype=BF16, name="wo_bwd_x", after=after_ffn, **big)
    after_ffn = reducer.middle("ffn", (dmerged,))
    g_wo = _matmul(merged, dx1b, mode="tn", out_dtype=BF16, name="wo_bwd_w", tm=512, tn=1024, tk=2048, after=after_ffn)
    dproj, dco, dao = _merge_bwd(dmerged, proj, conv_out, attn_out, "merge_bwd")
    dconv_y = _matmul(dco, wco, mode="nt", out_dtype=BF16, name="conv_out_bwd_x", **big)
    g_wco = _matmul(conv_y, dco, mode="tn", out_dtype=BF16, name="conv_out_bwd_w", tm=512, tn=1024, tk=2048)
    dattn = _matmul(dao, wao, mode="nt", out_dtype=BF16, name="attn_out_bwd_x", **big)
    g_wao = _matmul(attn, dao, mode="tn", out_dtype=BF16, name="attn_out_bwd_w", tm=512, tn=1024, tk=2048)
    after_mix = reducer.start("mix", dict(wco=g_wco, wao=g_wao, wo=g_wo))
    dproj, dconv_w = _conv_bwd(dconv_y, proj, conv_w, dproj, "conv_bwd", after=after_mix)
    after_mix = reducer.middle("mix", (dconv_w,))
    dproj, dkr, dv, dsinks = _swa_bwd(dattn, proj, tab, sinks, dproj, "attn_bwd", after=after_mix)
    dproj = _kv_bwd(dkr, dv, tab, dproj, "kv_bwd")
    g_win_t = _matmul(dproj, h1, mode="tn", out_dtype=BF16, name="proj_bwd_w", tm=512, tn=1024, tk=2048)
    after_in = reducer.middle("in", reducer.start("in", dict(win_t=g_win_t)))
    dh1 = _matmul(dproj, win_t, mode="nn", out_dtype=F32, name="proj_bwd_x", tm=1024, tn=1024, tk=1664, after=after_in)
    dx, _, dg_mix = _rms_bwd(dh1, x, g_mix, dx1, "rms1_bwd")
    grads = dict(win_t=g_win_t, wgu_t=g_wgu_t, wd=g_wd, wco=g_wco, wao=g_wao, wo=g_wo)
    small = dict(g_mix=dg_mix, g_ffn=dg_ffn, g_final=dg_final, conv_w=dconv_w, sinks=dsinks, lossvec=lossvec)
    return dx, grads, small


def _position():
    return lax.axis_index("x"), lax.axis_index("y"), lax.axis_index("c")


def _other_chips(x, y):
    return [(1 - x, y), (x, 1 - y), (1 - x, 1 - y)]


SEM_SPEC = pl.BlockSpec(memory_space=pltpu.SEMAPHORE)
EFFECT = pltpu.SideEffectType.DATAFLOW_SIDE_EFFECTING
TOKEN = jax.ShapeDtypeStruct((8, 128), F32)
TOKEN_SPEC = pl.BlockSpec(memory_space=pltpu.VMEM)


def _hbm(a):
    return pltpu.with_memory_space_constraint(a, pltpu.HBM)


def _place(w, me_idx, dtype, name, after=()):
    r, cdim = w.shape

    def body(i_ref, w_ref, *rest):
        rest[-1][...] = w_ref[...].astype(dtype)

    grid_spec = pltpu.PrefetchScalarGridSpec(
        num_scalar_prefetch=1, grid=(1,), in_specs=[pl.BlockSpec((r, cdim), lambda i, me: (0, 0))] + [HBM_SPEC] * len(after),
        out_specs=pl.BlockSpec((r, cdim), lambda i, me: (me[0], 0)))
    return _call(body, name=name, grid_spec=grid_spec, out_shape=_sds((N_DEV * r, cdim), dtype),
                 compiler_params=_params("arbitrary"))(me_idx, w, *after)


def _own_rows(ref, r, px, py, pc):
    return ref.at[pl.ds((4 * px + 2 * py + pc) * r, r), :]


def _gather_start(bufs, groups, name):
    n = len(bufs)
    rows = [b.shape[0] // N_DEV for b in bufs]
    ng = len(groups)

    def body(*refs):
        ins = refs[:n]
        sems = refs[n:n + 2 * ng]
        token = refs[-1]
        x, y, c = _position()
        targets = [(x, y, 1 - c)] + [(*chip, c) for chip in _other_chips(x, y)]
        for g, members in enumerate(groups):
            for slot, a in enumerate(members):
                own = _own_rows(ins[a], rows[a], x, y, c)
                for to in targets:
                    pltpu.make_async_remote_copy(src_ref=own, dst_ref=own, send_sem=sems[2 * g].at[slot],
                                                 recv_sem=sems[2 * g + 1].at[slot], device_id=to, device_id_type=MESH).start()
        token[...] = jnp.zeros_like(token)

    sem_shapes = []
    for members in groups:
        sem_shapes += [pltpu.SemaphoreType.DMA((len(members),))] * 2
    outs = _call(
        body, name=name, in_specs=[HBM_SPEC] * n, out_specs=[SEM_SPEC] * (2 * ng) + [HBM_SPEC] * n + [TOKEN_SPEC],
        out_shape=sem_shapes + [pltpu.HBM(b.shape, b.dtype) for b in bufs] + [TOKEN],
        input_output_aliases={i: 2 * ng + i for i in range(n)},
        compiler_params=pltpu.CompilerParams(has_side_effects=EFFECT),
    )(*[_hbm(b) for b in bufs])
    sem_pairs = [(outs[2 * g], outs[2 * g + 1]) for g in range(ng)]
    return sem_pairs, list(outs[2 * ng:2 * ng + n]), outs[-1]


def _gather_forward(send_sems, recv_sems, bufs, after, name):
    n = len(bufs)
    rows = [b.shape[0] // N_DEV for b in bufs]

    def body(*refs):
        ins = refs[:n]
        send1, recv1 = refs[n], refs[n + 1]
        out0 = n + 2 + len(after)
        send2, recv2 = refs[out0], refs[out0 + 1]
        token = refs[-1]
        x, y, c = _position()
        for a in range(n):
            step1 = pltpu.make_async_remote_copy(
                src_ref=_whole(ins[a], 4 * rows[a]), dst_ref=_whole(ins[a], 4 * rows[a]), send_sem=send1.at[a],
                recv_sem=recv1.at[a], device_id=(x, y, c), device_id_type=MESH)
            step1.wait_send()
            step1.wait_recv()
        for a in range(n):
            for chip in _other_chips(x, y):
                blk = _own_rows(ins[a], rows[a], *chip, c)
                pltpu.make_async_remote_copy(src_ref=blk, dst_ref=blk, send_sem=send2.at[a], recv_sem=recv2.at[a],
                                             device_id=(x, y, 1 - c), device_id_type=MESH).start()
        token[...] = jnp.zeros_like(token)

    outs = _call(
        body, name=name, in_specs=[HBM_SPEC] * n + [SEM_SPEC, SEM_SPEC] + [HBM_SPEC] * len(after),
        out_specs=[SEM_SPEC, SEM_SPEC] + [HBM_SPEC] * n + [TOKEN_SPEC],
        out_shape=[pltpu.SemaphoreType.DMA((n,)), pltpu.SemaphoreType.DMA((n,))]
        + [pltpu.HBM(b.shape, b.dtype) for b in bufs] + [TOKEN],
        input_output_aliases={i: 2 + i for i in range(n)},
        compiler_params=pltpu.CompilerParams(has_side_effects=EFFECT),
    )(*bufs, send_sems, recv_sems, *after)
    return outs[0], outs[1], list(outs[2:2 + n]), outs[-1]


def _gather_done(send_sems, recv_sems, bufs, after, name):
    n = len(bufs)
    rows = [b.shape[0] // N_DEV for b in bufs]

    def body(*refs):
        ins = refs[:n]
        send2, recv2 = refs[n], refs[n + 1]
        x, y, c = _position()
        for a in range(n):
            step2 = pltpu.make_async_remote_copy(
                src_ref=_whole(ins[a], 3 * rows[a]), dst_ref=_whole(ins[a], 3 * rows[a]), send_sem=send2.at[a],
                recv_sem=recv2.at[a], device_id=(x, y, c), device_id_type=MESH)
            step2.wait_send()
            step2.wait_recv()

    outs = _call(
        body, name=name, in_specs=[HBM_SPEC] * n + [SEM_SPEC, SEM_SPEC] + [HBM_SPEC] * len(after),
        out_specs=[HBM_SPEC] * n, out_shape=[pltpu.HBM(b.shape, b.dtype) for b in bufs],
        input_output_aliases={i: i for i in range(n)},
        compiler_params=pltpu.CompilerParams(has_side_effects=EFFECT),
    )(*bufs, send_sems, recv_sems, *after)
    return list(outs)


def _whole(ref, nrows):
    return ref.at[pl.ds(0, nrows), :]


def _to_sibling(x, y, c):
    return [(2 * q + (1 - c), q, (x, y, 1 - c)) for q in range(4)]


def _to_chips(x, y, c):
    return [(2 * px + py, j, (px, py, c)) for j, (px, py) in enumerate(_other_chips(x, y))]


def _exchange_start(srcs, src_slots, plan, name):
    n = len(srcs)
    rows = [a.shape[0] // src_slots for a in srcs]
    n_copies = len(plan(0, 0, 0))
    lands = [lax.empty((n_copies * r, a.shape[1]), a.dtype) for a, r in zip(srcs, rows)]

    def body(*refs):
        ins, land_refs = refs[:n], refs[n:2 * n]
        send_sems, recv_sems = refs[2 * n], refs[2 * n + 1]
        token = refs[-1]
        for a in range(n):
            r = rows[a]
            for src_slot, dst_slot, target in plan(*_position()):
                pltpu.make_async_remote_copy(
                    src_ref=ins[a].at[pl.ds(src_slot * r, r), :], dst_ref=land_refs[a].at[pl.ds(dst_slot * r, r), :],
                    send_sem=send_sems.at[a], recv_sem=recv_sems.at[a], device_id=target, device_id_type=MESH).start()
        token[...] = jnp.zeros_like(token)

    outs = _call(
        body, name=name, in_specs=[HBM_SPEC] * (2 * n),
        out_specs=[SEM_SPEC, SEM_SPEC] + [HBM_SPEC] * (2 * n) + [TOKEN_SPEC],
        out_shape=[pltpu.SemaphoreType.DMA((n,)), pltpu.SemaphoreType.DMA((n,))]
        + [pltpu.HBM(a.shape, a.dtype) for a in srcs] + [pltpu.HBM(l.shape, l.dtype) for l in lands] + [TOKEN],
        input_output_aliases={i: 2 + i for i in range(2 * n)},
        compiler_params=pltpu.CompilerParams(has_side_effects=EFFECT),
    )(*[_hbm(a) for a in srcs], *[_hbm(l) for l in lands])
    return outs[0], outs[1], list(outs[2:2 + n]), list(outs[2 + n:2 + 2 * n]), outs[-1]


def _exchange_wait(send_sems, recv_sems, srcs, lands, after, name):
    n = len(srcs)

    def body(*refs):
        ins, land_refs = refs[:n], refs[n:2 * n]
        send_sems_ref, recv_sems_ref = refs[2 * n], refs[2 * n + 1]
        for a in range(n):
            allrows = lands[a].shape[0]
            cp = pltpu.make_async_remote_copy(
                src_ref=_whole(ins[a], allrows), dst_ref=_whole(land_refs[a], allrows), send_sem=send_sems_ref.at[a],
                recv_sem=recv_sems_ref.at[a], device_id=_position(), device_id_type=MESH)
            cp.wait_send()
            cp.wait_recv()

    outs = _call(
        body, name=name, in_specs=[HBM_SPEC] * (2 * n) + [SEM_SPEC, SEM_SPEC] + [HBM_SPEC] * len(after),
        out_specs=[HBM_SPEC] * (2 * n),
        out_shape=[pltpu.HBM(a.shape, a.dtype) for a in srcs] + [pltpu.HBM(l.shape, l.dtype) for l in lands],
        input_output_aliases={i: i for i in range(2 * n)},
        compiler_params=pltpu.CompilerParams(has_side_effects=EFFECT),
    )(*srcs, *lands, send_sems, recv_sems, *after)
    return list(outs[:n]), list(outs[n:])


def _chip_partial(grad, recv, idx, name):
    r = recv.shape[0] // 4

    def body(i_ref, g_ref, s_ref, o_ref):
        del i_ref
        o_ref[...] = (g_ref[...].astype(F32) + s_ref[...].astype(F32)).astype(BF16)

    nb = 1
    tr = r // nb
    grid_spec = pltpu.PrefetchScalarGridSpec(
        num_scalar_prefetch=1, grid=(3, nb),
        in_specs=[pl.BlockSpec((tr, D), lambda t, i, i_ref: ((2 * i_ref[1 + t] + i_ref[0]) * nb + i, 0)),
                  pl.BlockSpec((tr, D), lambda t, i, i_ref: (i_ref[1 + t] * nb + i, 0))],
        out_specs=pl.BlockSpec((tr, D), lambda t, i, i_ref: (i_ref[1 + t] * nb + i, 0)))
    return _call(body, name=name, grid_spec=grid_spec, out_shape=_sds((4 * r, D), BF16),
                 compiler_params=_params("arbitrary", "arbitrary"))(idx, grad, recv)


def _adamw_math(w, g, m, v):
    m2 = B1 * m + (1.0 - B1) * g
    v2 = B2 * v + (1.0 - B2) * jnp.square(g)
    m_hat = m2 / (1.0 - B1 ** STEP)
    v_hat = v2 / (1.0 - B2 ** STEP)
    return -LR * (m_hat / (jnp.sqrt(v_hat) + EPS_ADAM) + WD * w), m2, v2


def _reduce_adamw(w, grad, from_sibling, from_chips, idx, m, v, name):
    r = w.shape[0]
    assert grad.shape == (N_DEV * r, D) and from_sibling.shape == (4 * r, D) and from_chips.shape == (3 * r, D)
    tr = r // 2
    nb = r // tr

    def body(i_ref, w_ref, p_ref, s_ref, r0_ref, r1_ref, r2_ref, m_ref, v_ref, g_ref, d_ref, nm_ref, nv_ref):
        del i_ref
        g = p_ref[...].astype(F32) + s_ref[...].astype(F32)
        g = ((g + r0_ref[...].astype(F32)) + r1_ref[...].astype(F32)) + r2_ref[...].astype(F32)
        g_ref[...] = g
        d_ref[...], nm_ref[...], nv_ref[...] = _adamw_math(w_ref[...], g, m_ref[...], v_ref[...])

    own = pl.BlockSpec((tr, D), lambda i, i_ref: (i, 0))
    grid_spec = pltpu.PrefetchScalarGridSpec(
        num_scalar_prefetch=1, grid=(nb,),
        in_specs=[own, pl.BlockSpec((tr, D), lambda i, i_ref: (i_ref[0] * nb + i, 0)),
                  pl.BlockSpec((tr, D), lambda i, i_ref: (i_ref[1] * nb + i, 0))]
        + [pl.BlockSpec((tr, D), lambda i, i_ref, j=j: (j * nb + i, 0)) for j in range(3)] + [own, own],
        out_specs=[own] * 4)
    return _call(body, name=name, grid_spec=grid_spec, out_shape=[_sds((r, D), F32)] * 4,
                 compiler_params=_params("parallel"))(idx, w, grad, from_sibling, from_chips, from_chips, from_chips, m, v)


SMALL_ROWS = 8


def _small_all_reduce(pack, name, after=()):
    def body(p_ref, *rest):
        tot_ref, loss_ref, gath, send_sems, recv_sems = rest[len(after):]
        x, y, c = _position()
        me_id = 4 * x + 2 * y + c
        gath[me_id] = p_ref[...]
        copies = []
        for k in range(1, N_DEV):
            peer = tuple(1 - v if (k >> b) & 1 else v for v, b in ((x, 2), (y, 1), (c, 0)))
            cp = pltpu.make_async_remote_copy(src_ref=p_ref, dst_ref=gath.at[me_id], send_sem=send_sems.at[k - 1],
                                              recv_sem=recv_sems.at[k - 1], device_id=peer, device_id_type=MESH)
            cp.start()
            copies.append(cp)
        for cp in copies:
            cp.wait_recv()
        for cp in copies:
            cp.wait_send()
        tot = gath[0]
        for d in range(1, N_DEV):
            tot = tot + gath[d]
        tot_ref[...] = tot
        loss_ref[...] = jnp.full((1, 128), (0.5 / D) * jnp.sum(tot[SMALL_ROWS - 1:SMALL_ROWS, :]), F32)

    vm = pl.BlockSpec(memory_space=pltpu.VMEM)
    return _call(
        body, name=name, in_specs=[vm] + [HBM_SPEC] * len(after), out_specs=[vm, vm],
        out_shape=[_sds((SMALL_ROWS, D), F32), _sds((1, 128), F32)],
        scratch_shapes=[pltpu.VMEM((N_DEV, SMALL_ROWS, D), F32), pltpu.SemaphoreType.DMA((N_DEV - 1,)),
                        pltpu.SemaphoreType.DMA((N_DEV - 1,))],
    )(pack, *after)


def _adamw(w, g, m, v, name):
    r, cdim = w.shape
    tr = 256 if r % 256 == 0 else (r // 2 if r % 16 == 0 else r)

    def body(w_ref, g_ref, m_ref, v_ref, d_ref, nm_ref, nv_ref):
        d_ref[...], nm_ref[...], nv_ref[...] = _adamw_math(w_ref[...], g_ref[...], m_ref[...], v_ref[...])

    spec = pl.BlockSpec((tr, cdim), lambda i: (i, 0))
    return _call(
        body, name=name, grid=(r // tr,), in_specs=[spec] * 4, out_specs=[spec] * 3,
        out_shape=[_sds((r, cdim), F32)] * 3, compiler_params=_params("parallel"),
    )(w, g, m, v)


def kernel(x, g_mix, w_in, conv_w, attn_sinks, w_conv_out, w_attn_out, w_o, g_ffn, w_gate_up, w_down, g_final, loss_target, m_g_mix, m_w_in, m_conv_w, m_attn_sinks, m_w_conv_out, m_w_attn_out, m_w_o, m_g_ffn, m_w_gate_up, m_w_down, m_g_final, v_g_mix, v_w_in, v_conv_w, v_attn_sinks, v_w_conv_out, v_w_attn_out, v_w_o, v_g_ffn, v_w_gate_up, v_w_down, v_g_final):
    cx, cy, cc = _position()
    chip = 2 * cx + cy
    partial_idx = jnp.stack([cc, 2 * (1 - cx) + cy, 2 * cx + (1 - cy), 2 * (1 - cx) + (1 - cy)]).astype(jnp.int32)
    own_idx = jnp.stack([2 * chip + cc, chip]).astype(jnp.int32)
    me = 4 * cx + 2 * cy + cc

    me_idx = jnp.reshape(me, (1,)).astype(jnp.int32)
    first = [_place(jnp.transpose(w_in[0]), me_idx, BF16, "place_w_in"),
             _place(jnp.pad(conv_w[0], ((0, 5), (0, 0))), me_idx, F32, "place_conv_w")]
    (sems_in,), first, token_in = _gather_start(first, [[0, 1]], "gather_start_in")
    later = [_place(w, me_idx, BF16, "place_" + k, after=(token_in,)) for k, w in (
        ("w_conv_out", w_conv_out[0]), ("w_attn_out", w_attn_out[0]), ("w_o", w_o[0]),
        ("w_gate_up", jnp.transpose(w_gate_up[0])), ("w_down", w_down[0]))]
    (sems_mix, sems_ffn), later, token_later = _gather_start(later, [[0, 1, 2], [3, 4]], "gather_start_later")
    gather_tokens = (token_in, token_later)

    class Gathered:
        def __init__(self):
            self.state = {"in": (sems_in, first), "mix": (sems_mix, later[:3]), "ffn": (sems_ffn, later[3:])}

        def begin(self, group, after):
            (send_sems, recv_sems), group_bufs = self.state[group]
            send2, recv2, group_bufs, token = _gather_forward(send_sems, recv_sems, group_bufs, after, "gather_forward_" + group)
            self.state[group] = ((send2, recv2), group_bufs)
            return (token,)

        def end(self, group, after):
            (send2, recv2), group_bufs = self.state[group]
            full = _gather_done(send2, recv2, group_bufs, after, "gather_done_" + group)
            if group == "in":
                return full[0], jnp.transpose(full[1].reshape(N_DEV, 8, 128)[:, :3, :], (1, 0, 2)).reshape(3, D)
            return full

    in_flight, own_pieces = {}, {}

    class Reducer:
        def start(self, group, gdict):
            keys, glist = list(gdict), list(gdict.values())
            send_sems, recv_sems, glist, lands, token = _exchange_start(glist, N_DEV, _to_sibling, "rs_sibling_start_" + group)
            in_flight[group] = (keys, send_sems, recv_sems, glist, lands)
            return (token,)

        def middle(self, group, after):
            keys, send_sems, recv_sems, glist, lands = in_flight[group]
            glist, lands = _exchange_wait(send_sems, recv_sems, glist, lands, after, "rs_sibling_wait_" + group)
            parts = [_chip_partial(g, r, partial_idx, "chip_partial_" + k) for k, g, r in zip(keys, glist, lands)]
            send_sems, recv_sems, parts, from_chips, token = _exchange_start(parts, 4, _to_chips, "rs_chips_start_" + group)
            in_flight[group] = (keys, send_sems, recv_sems, parts, from_chips)
            own_pieces[group] = (glist, lands)
            return (token,)

    dx, _, small = _local_step(x[0], loss_target[0], g_mix, g_ffn, g_final[None], attn_sinks, Gathered(),
                               reducer=Reducer(), after=gather_tokens)

    transposed = ("w_in", "w_gate_up")

    def as2d(k, a):
        if k in transposed:
            return jnp.transpose(a[0])
        return a[None] if a.ndim == 1 else (a[0] if a.ndim == 3 else a)

    w_all = {"g_mix": g_mix, "w_in": w_in, "conv_w": conv_w, "attn_sinks": attn_sinks, "w_conv_out": w_conv_out,
             "w_attn_out": w_attn_out, "w_o": w_o, "g_ffn": g_ffn, "w_gate_up": w_gate_up, "w_down": w_down, "g_final": g_final}
    m_all = {"g_mix": m_g_mix, "w_in": m_w_in, "conv_w": m_conv_w, "attn_sinks": m_attn_sinks, "w_conv_out": m_w_conv_out,
             "w_attn_out": m_w_attn_out, "w_o": m_w_o, "g_ffn": m_g_ffn, "w_gate_up": m_w_gate_up, "w_down": m_w_down,
             "g_final": m_g_final}
    v_all = {"g_mix": v_g_mix, "w_in": v_w_in, "conv_w": v_conv_w, "attn_sinks": v_attn_sinks, "w_conv_out": v_w_conv_out,
             "w_attn_out": v_w_attn_out, "w_o": v_w_o, "g_ffn": v_g_ffn, "w_gate_up": v_w_gate_up, "w_down": v_w_down,
             "g_final": v_g_final}
    results = {}

    def update(k, g=None, pieces=None):
        w2, m2, v2 = as2d(k, w_all[k]), as2d(k, m_all[k]), as2d(k, v_all[k])
        if g is None:
            g, d, nm, nv = _reduce_adamw(w2, *pieces, own_idx, m2, v2, "adamw_" + k)
        else:
            d, nm, nv = _adamw(w2, g, m2, v2, "adamw_" + k)
        results[k] = [(jnp.transpose(val) if k in transposed else val).reshape(w_all[k].shape) for val in (g, d, nm, nv)]
        return nm

    kernel_name = {"win_t": "w_in", "wgu_t": "w_gate_up", "wd": "w_down", "wco": "w_conv_out", "wao": "w_attn_out", "wo": "w_o"}

    def finish(group, after):
        keys, send_sems, recv_sems, parts, from_chips = in_flight[group]
        _, from_chips = _exchange_wait(send_sems, recv_sems, parts, from_chips, after, "rs_chips_wait_" + group)
        grads, from_sibling = own_pieces[group]
        return tuple(update(kernel_name[k], pieces=p) for k, *p in zip(keys, grads, from_sibling, from_chips))

    after = finish("mix", finish("ffn", (dx,)))

    sinks_row = jnp.pad(small["sinks"], ((0, 0), (0, D - 128)))
    pack = jnp.concatenate([small["g_mix"], small["g_ffn"], small["g_final"], small["conv_w"], sinks_row, small["lossvec"]], axis=0)
    tot, loss_row = _small_all_reduce(pack, "small_all_reduce", after=after)
    loss = loss_row[0, 0]
    g_small = {
        "g_mix": tot[0:1], "g_ffn": tot[1:2], "g_final": tot[2:3],
        "conv_w": lax.dynamic_slice(tot, (3, me * 128), (3, 128)), "attn_sinks": tot[6:7, :N_HEADS],
    }
    finish("in", tuple(update(k, g) for k, g in g_small.items()))

    order = ["g_mix", "w_in", "conv_w", "attn_sinks", "w_conv_out", "w_attn_out", "w_o", "g_ffn", "w_gate_up", "w_down", "g_final"]
    return (loss, dx[None], *[results[k][i] for i in range(4) for k in order])
```

```python
import functools
import math

import jax
import jax.numpy as jnp
from jax import lax
from jax.experimental import pallas as pl
from jax.experimental.pallas import tpu as pltpu

F32 = jnp.float32
BF16 = jnp.bfloat16

D = 1024
HEAD_DIM = 64
N_HEADS = 16
N_KV = 4
GROUP = N_HEADS // N_KV
D_KV = N_KV * HEAD_DIM
BLOCK = 128
ROT_DIM = HEAD_DIM // 4
ROPE_THETA = 500000.0
ATTN_SCALE = 1.0 / math.sqrt(HEAD_DIM)
NEG_INF = -1e30
D_FF = 2816
N_IN = 6656
EPS = 1e-5
C_CB, C_CC, C_CX, C_Q, C_K, C_V, C_GC, C_GA = 0, 1024, 2048, 3072, 4096, 4352, 4608, 5632

LR, B1, B2, EPS_ADAM, WD, STEP = 0.001, 0.9, 0.999, 1e-08, 0.01, 10

N_DEV = 8
MESH = pl.DeviceIdType.MESH
VMEM_LIMIT = 56 * 1024 * 1024

NN = (((1,), (0,)), ((), ()))
NT = (((1,), (1,)), ((), ()))
TN = (((0,), (0,)), ((), ()))
HBM_SPEC = pl.BlockSpec(memory_space=pl.ANY)
ROW_SPLIT = 4


def _call(body, **kw):
    return pl.pallas_call(body, **kw)


def _params(*sem):
    return pltpu.CompilerParams(dimension_semantics=sem, vmem_limit_bytes=VMEM_LIMIT)


def _sds(shape, dtype):
    return jax.ShapeDtypeStruct(shape, dtype)


def _matmul(a, b, *, mode, tm, tn, tk, out_dtype, name, res=None, after=()):
    parts = list(a) if isinstance(a, (list, tuple)) else [a]
    rows_a = parts[0].shape[0]
    cols_a = sum(p.shape[1] for p in parts)
    if mode == "nn":
        (m, kk), (_, n), dims = (rows_a, cols_a), b.shape, NN
    elif mode == "nt":
        (m, kk), (n, _), dims = (rows_a, cols_a), b.shape, NT
    else:
        (kk, m), (_, n), dims = (rows_a, cols_a), b.shape, TN
    tm, tn, tk = min(tm, m), min(tn, n), min(tk, kk)
    assert m % tm == 0 and n % tn == 0 and kk % tk == 0, (name, m, n, kk, tm, tn, tk)
    nk = kk // tk
    split_axis, width = (2, tk) if mode == "nn" else (0, tm)
    assert len(parts) == 1 or mode in ("nn", "tn")
    assert len(parts) == 1 or all(p.shape[1] % width == 0 for p in parts), (name, width)
    counts = [p.shape[1] // width for p in parts]
    starts = [sum(counts[:p]) for p in range(len(parts))]

    def a_spec(p):
        def col(t):
            return jnp.clip(t - starts[p], 0, counts[p] - 1) if len(parts) > 1 else t

        if mode == "tn":
            return pl.BlockSpec((tk, tm), lambda i, j, k: (k, col(i)))
        return pl.BlockSpec((tm, tk), lambda i, j, k: (i, col(k)))

    if mode == "nt":
        b_spec = pl.BlockSpec((tn, tk), lambda i, j, k: (j, k))
    else:
        b_spec = pl.BlockSpec((tk, tn), lambda i, j, k: (k, j))
    o_spec = pl.BlockSpec((tm, tn), lambda i, j, k: (i, j))
    has_res = res is not None
    n_parts = len(parts)
    unit = 128 if mode == "tn" else 16
    split = ROW_SPLIT if tm % (ROW_SPLIT * unit) == 0 else 1

    def body(*refs):
        a_refs, b_ref = refs[:n_parts], refs[n_parts]
        r_ref = refs[n_parts + 1] if has_res else None
        o_ref = refs[n_parts + 1 + has_res + len(after)]
        k = pl.program_id(2)

        acc_ref = refs[-1] if nk > 1 else None

        def step(a_ref):
            def matmul(rows):
                a_blk = a_ref[:, rows] if mode == "tn" else a_ref[rows, :]
                return lax.dot_general(a_blk, b_ref[...], dims, preferred_element_type=F32)

            def finish(rows, part):
                if nk > 1:
                    acc_ref[rows, :] += part
                else:
                    o_ref[rows, :] = (part + r_ref[rows, :] if has_res else part).astype(o_ref.dtype)

            _row_pipeline(tm, matmul, finish, split)

        if nk > 1:
            @pl.when(k == 0)
            def _():
                acc_ref[...] = jnp.zeros_like(acc_ref)

        if n_parts == 1:
            step(a_refs[0])
        else:
            t = pl.program_id(split_axis)
            for p in range(n_parts):
                pl.when((t >= starts[p]) & (t < starts[p] + counts[p]))(functools.partial(step, a_refs[p]))

        if nk > 1:
            @pl.when(k == nk - 1)
            def _():
                o_ref[...] = (acc_ref[...] + r_ref[...] if has_res else acc_ref[...]).astype(o_ref.dtype)

    ins = parts + [b] + ([res] if has_res else []) + list(after)
    in_specs = [a_spec(p) for p in range(n_parts)] + [b_spec] + ([o_spec] if has_res else []) + [HBM_SPEC] * len(after)
    scratch = [] if nk == 1 else [pltpu.VMEM((tm, tn), F32)]
    return _call(
        body, name=name, grid=(m // tm, n // tn, nk), in_specs=in_specs, out_specs=o_spec,
        out_shape=_sds((m, n), out_dtype), scratch_shapes=scratch,
        compiler_params=_params("parallel", "parallel", "arbitrary"),
    )(*ins)


def _row_tile(s):
    return min(512, s)


def _rms_fwd(x, g, name, after=()):
    s = x.shape[0]
    tm = _row_tile(s)

    def body(x_ref, g_ref, *rest):
        h_ref = rest[-1]
        xv = x_ref[...]
        r = lax.rsqrt(jnp.mean(xv * xv, axis=-1, keepdims=True) + EPS)
        h_ref[...] = (xv * r * g_ref[...]).astype(BF16)

    row = pl.BlockSpec((tm, D), lambda i: (i, 0))
    return _call(
        body, name=name, grid=(s // tm,), in_specs=[row, pl.BlockSpec((1, D), lambda i: (0, 0))] + [HBM_SPEC] * len(after),
        out_specs=row, out_shape=_sds((s, D), BF16), compiler_params=_params("parallel"),
    )(x, g, *after)


def _rms_bwd(dh, x, g, dres, name, after=()):
    s = x.shape[0]
    tm = _row_tile(s)

    def body(dh_ref, x_ref, g_ref, dres_ref, *rest):
        dx_ref, dxb_ref, dg_ref = rest[len(after):]
        xv = x_ref[...]
        r = lax.rsqrt(jnp.mean(xv * xv, axis=-1, keepdims=True) + EPS)
        xh = xv * r
        dhv = dh_ref[...]
        dyg = dhv * g_ref[...]
        dx = dres_ref[...] + r * (dyg - xh * jnp.mean(dyg * xh, axis=-1, keepdims=True))
        dx_ref[...] = dx
        dxb_ref[...] = dx.astype(BF16)
        part = jnp.sum(dhv * xh, axis=0, keepdims=True)

        @pl.when(pl.program_id(0) == 0)
        def _():
            dg_ref[...] = part

        @pl.when(pl.program_id(0) > 0)
        def _():
            dg_ref[...] += part

    row = pl.BlockSpec((tm, D), lambda i: (i, 0))
    vec = pl.BlockSpec((1, D), lambda i: (0, 0))
    return _call(
        body, name=name, grid=(s // tm,), in_specs=[row, row, vec, row] + [HBM_SPEC] * len(after), out_specs=[row, row, vec],
        out_shape=[_sds((s, D), F32), _sds((s, D), BF16), _sds((1, D), F32)],
        compiler_params=_params("arbitrary"),
    )(dh, x, g, dres, *after)


def _loss_head(x2, g, tgt, name):
    s = x2.shape[0]
    tm = _row_tile(s)

    def body(x_ref, g_ref, t_ref, dx_ref, dxb_ref, dg_ref, l_ref):
        xv = x_ref[...]
        gv = g_ref[...]
        r = lax.rsqrt(jnp.mean(xv * xv, axis=-1, keepdims=True) + EPS)
        xh = xv * r
        err = xh * gv - t_ref[...]
        dy = err * (1.0 / D)
        dyg = dy * gv
        dx = r * (dyg - xh * jnp.mean(dyg * xh, axis=-1, keepdims=True))
        dx_ref[...] = dx
        dxb_ref[...] = dx.astype(BF16)
        dg_part = jnp.sum(dy * xh, axis=0, keepdims=True)
        l_part = jnp.sum(err * err, axis=0, keepdims=True)

        @pl.when(pl.program_id(0) == 0)
        def _():
            dg_ref[...] = dg_part
            l_ref[...] = l_part

        @pl.when(pl.program_id(0) > 0)
        def _():
            dg_ref[...] += dg_part
            l_ref[...] += l_part

    row = pl.BlockSpec((tm, D), lambda i: (i, 0))
    vec = pl.BlockSpec((1, D), lambda i: (0, 0))
    return _call(
        body, name=name, grid=(s // tm,), in_specs=[row, vec, row], out_specs=[row, row, vec, vec],
        out_shape=[_sds((s, D), F32), _sds((s, D), BF16), _sds((1, D), F32), _sds((1, D), F32)],
        compiler_params=_params("arbitrary"),
    )(x2, g, tgt)


CONV_TC = 256


def _shift_down(u, k, rows):
    return jnp.where(rows >= k, pltpu.roll(u, k, 0), 0.0)


def _shift_up(u, k, rows, s):
    return jnp.where(rows < s - k, pltpu.roll(u, s - k, 0), 0.0)


def _conv_specs(s):
    nb = D // CONV_TC

    def col(c0):
        return pl.BlockSpec((s, CONV_TC), lambda j, c0=c0: (0, c0 // CONV_TC + j))

    return nb, col


def _conv_fwd(proj, conv_w, name):
    s = proj.shape[0]
    nb, col = _conv_specs(s)

    def body(cb_ref, cc_ref, cx_ref, w_ref, y_ref):
        rows = lax.broadcasted_iota(jnp.int32, (s, CONV_TC), 0)
        u = cc_ref[...].astype(F32) * cx_ref[...].astype(F32)
        w = w_ref[...]
        c = w[0:1] * _shift_down(u, 2, rows) + w[1:2] * _shift_down(u, 1, rows) + w[2:3] * u
        y_ref[...] = (cb_ref[...].astype(F32) * c).astype(BF16)

    return _call(
        body, name=name, grid=(nb,),
        in_specs=[col(C_CB), col(C_CC), col(C_CX), pl.BlockSpec((3, CONV_TC), lambda j: (0, j))],
        out_specs=pl.BlockSpec((s, CONV_TC), lambda j: (0, j)), out_shape=_sds((s, D), BF16),
        compiler_params=_params("parallel"),
    )(proj, proj, proj, conv_w)


def _write_behind(t, nt, buf, sems, tiles, window, where):
    slot = t % 2

    def copies(sl, at):
        return [pltpu.make_async_copy(buf.at[sl, p], window(p, at), sems.at[sl, p]) for p in range(len(tiles))]

    @pl.when(t >= 2)
    def _():
        for cp in copies(slot, where):
            cp.wait()

    for p, tile in enumerate(tiles):
        buf[slot, p] = tile
    started = copies(slot, where)
    for cp in started:
        cp.start()

    @pl.when(t == nt - 1)
    def _():
        for cp in started:
            cp.wait()
        if nt > 1:
            for cp in copies(1 - slot, where):
                cp.wait()


def _conv_bwd(dy, proj, conv_w, dproj, name, after=()):
    s = proj.shape[0]
    nb, col = _conv_specs(s)

    def body(dy_ref, cb_ref, cc_ref, cx_ref, w_ref, *rest):
        dproj_ref, dw_ref, buf, sems = rest[1 + len(after):]
        j = pl.program_id(0)
        rows = lax.broadcasted_iota(jnp.int32, (s, CONV_TC), 0)
        cc = cc_ref[...].astype(F32)
        cx = cx_ref[...].astype(F32)
        u = cc * cx
        u1 = _shift_down(u, 1, rows)
        u2 = _shift_down(u, 2, rows)
        w = w_ref[...]
        c = w[0:1] * u2 + w[1:2] * u1 + w[2:3] * u
        dyv = dy_ref[...].astype(F32)
        dc = dyv * cb_ref[...].astype(F32)
        du = w[2:3] * dc + w[1:2] * _shift_up(dc, 1, rows, s) + w[0:1] * _shift_up(dc, 2, rows, s)

        def window(p, jj):
            start = pl.multiple_of((C_CB, C_CC, C_CX)[p] + jj * CONV_TC, CONV_TC)
            return dproj_ref.at[:, pl.ds(start, CONV_TC)]

        tiles = ((dyv * c).astype(BF16), (du * cx).astype(BF16), (du * cc).astype(BF16))
        _write_behind(j * 0, 1, buf, sems, tiles, window, j)
        dw_ref[...] = jnp.concatenate(
            [jnp.sum(dc * u2, axis=0, keepdims=True), jnp.sum(dc * u1, axis=0, keepdims=True),
             jnp.sum(dc * u, axis=0, keepdims=True)], axis=0)

    return _call(
        body, name=name, grid=(nb,),
        in_specs=[pl.BlockSpec((s, CONV_TC), lambda j: (0, j)), col(C_CB), col(C_CC), col(C_CX),
                  pl.BlockSpec((3, CONV_TC), lambda j: (0, j))] + [HBM_SPEC] * (1 + len(after)),
        out_specs=[pl.BlockSpec(memory_space=pl.ANY), pl.BlockSpec((3, CONV_TC), lambda j: (0, j))],
        out_shape=[_sds((s, N_IN), BF16), _sds((3, D), F32)],
        scratch_shapes=[pltpu.VMEM((1, 3, s, CONV_TC), BF16), pltpu.SemaphoreType.DMA((1, 3))],
        input_output_aliases={5: 0}, compiler_params=_params("arbitrary"),
    )(dy, proj, proj, proj, conv_w, dproj, *after)


def _rope_tables(s):
    half = ROT_DIM // 2
    inv_freq = ROPE_THETA ** (-jnp.arange(0, ROT_DIM, 2, dtype=F32) / ROT_DIM)
    inv64 = jnp.concatenate([inv_freq, inv_freq, jnp.zeros((HEAD_DIM - ROT_DIM,), F32)])
    ang = jnp.arange(s, dtype=F32)[:, None] * jnp.concatenate([inv64, inv64])[None, :]
    d = lax.broadcasted_iota(jnp.int32, (s, 128), 1) % HEAD_DIM
    cos, sin = jnp.cos(ang), jnp.sin(ang)
    c = jnp.where(d < ROT_DIM, cos, 1.0)
    a = jnp.where(d < half, -sin, 0.0)
    b = jnp.where((d >= half) & (d < ROT_DIM), sin, 0.0)
    return jnp.concatenate([c, a, b], axis=1)


def _rope(x, tab):
    c, a, b = tab[:, 0:128], tab[:, 128:256], tab[:, 256:384]
    outs = []
    for i in range(x.shape[1] // 128):
        xc = x[:, i * 128:(i + 1) * 128]
        outs.append(xc * c + pltpu.roll(xc, 120, 1) * a + pltpu.roll(xc, 8, 1) * b)
    return outs[0] if len(outs) == 1 else jnp.concatenate(outs, axis=1)


def _rope_t(dx, tab):
    c, a, b = tab[:, 0:128], tab[:, 128:256], tab[:, 256:384]
    outs = []
    for i in range(dx.shape[1] // 128):
        dc = dx[:, i * 128:(i + 1) * 128]
        outs.append(dc * c + pltpu.roll(dc * a, 8, 1) + pltpu.roll(dc * b, 120, 1))
    return outs[0] if len(outs) == 1 else jnp.concatenate(outs, axis=1)


def _attn_in_specs():
    prev = lambda n: jnp.maximum(n - 1, 0)
    return [
        pl.BlockSpec((BLOCK, D), lambda n: (n, C_Q // D)),
        pl.BlockSpec((BLOCK, D_KV), lambda n: (n, C_K // D_KV)),
        pl.BlockSpec((BLOCK, D_KV), lambda n: (prev(n), C_K // D_KV)),
        pl.BlockSpec((BLOCK, D_KV), lambda n: (n, C_V // D_KV)),
        pl.BlockSpec((BLOCK, D_KV), lambda n: (prev(n), C_V // D_KV)),
        pl.BlockSpec((BLOCK, 384), lambda n: (n, 0)),
        pl.BlockSpec((BLOCK, 384), lambda n: (prev(n), 0)),
        pl.BlockSpec(memory_space=pltpu.SMEM),
    ]


HALF = HEAD_DIM
N_CHUNK = D // 128


def _swa_bias(n):
    qi = lax.broadcasted_iota(jnp.int32, (BLOCK, 2 * BLOCK), 0)
    kj = lax.broadcasted_iota(jnp.int32, (BLOCK, 2 * BLOCK), 1)
    rel = qi + BLOCK - kj
    valid = (rel >= 0) & (rel < BLOCK) & ((kj >= BLOCK) | (n > 0))
    return jnp.where(valid, 0.0, NEG_INF)


def _halves(x):
    lo = lax.broadcasted_iota(jnp.int32, x.shape, 1) < HALF
    return jnp.where(lo, x, 0.0).astype(BF16), jnp.where(lo, 0.0, x).astype(BF16)


def _dup_heads(x):
    out = []
    for pair in range(N_KV // 2):
        xc = x[:, pair * 128:(pair + 1) * 128]
        xr = pltpu.roll(xc, HALF, 1)
        lo = lax.broadcasted_iota(jnp.int32, xc.shape, 1) < HALF
        out += [jnp.where(lo, xc, xr), jnp.where(lo, xr, xc)]
    return out


def _swa_load(q_ref, kc_ref, kp_ref, vc_ref, vp_ref, tc_ref, tp_ref):
    qf = _rope(q_ref[...].astype(F32), tc_ref[...]) * ATTN_SCALE
    q_halves = [_halves(qf[:, c * 128:(c + 1) * 128]) for c in range(N_CHUNK)]
    kf = jnp.concatenate([_rope(kp_ref[...].astype(F32), tp_ref[...]), _rope(kc_ref[...].astype(F32), tc_ref[...])], axis=0)
    vf = jnp.concatenate([vp_ref[...], vc_ref[...]], axis=0).astype(F32)
    return q_halves, _dup_heads(kf), _dup_heads(vf)


def _swa_probs(qh, kk, bias, sink):
    s = lax.dot_general(qh, kk, NT, preferred_element_type=F32) + bias
    m = jnp.maximum(jnp.max(jnp.maximum(s[:, :BLOCK], s[:, BLOCK:]), axis=1, keepdims=True), sink)
    return jnp.exp(s - m), m


def _swa_fwd(proj, tab, sinks, name, after=()):
    s = proj.shape[0]

    def body(q_ref, kc_ref, kp_ref, vc_ref, vp_ref, tc_ref, tp_ref, sink_ref, *rest):
        o_ref = rest[-1]
        n = pl.program_id(0)
        q_halves, kdup, vdup = _swa_load(q_ref, kc_ref, kp_ref, vc_ref, vp_ref, tc_ref, tp_ref)
        bias = _swa_bias(n)
        ones = jnp.ones((2 * BLOCK, 128), BF16)
        kk = [k.astype(BF16) for k in kdup]
        vv = [[jnp.concatenate([v_half, ones], axis=1) for v_half in _halves(v)] for v in vdup]
        heads = [(c, half) for c in range(N_CHUNK) for half in range(2)]
        scores = [lax.dot_general(q_halves[c][half], kk[c // (GROUP // 2)], NT, preferred_element_type=F32)
                  for c, half in heads]
        probs = []
        for (c, half), sc in zip(heads, scores):
            sc = sc + bias
            m = jnp.maximum(jnp.max(jnp.maximum(sc[:, :BLOCK], sc[:, BLOCK:]), axis=1, keepdims=True), sink_ref[0, 2 * c + half])
            probs.append((jnp.exp(sc - m).astype(BF16), jnp.exp(sink_ref[0, 2 * c + half] - m)))
        outs = [lax.dot_general(e, vv[c // (GROUP // 2)][half], NN, preferred_element_type=F32)
                for (c, half), (e, _) in zip(heads, probs)]
        for c in range(N_CHUNK):
            parts = [outs[2 * c + half][:, :128] * (1.0 / (outs[2 * c + half][:, 128:] + probs[2 * c + half][1]))
                     for half in range(2)]
            o_ref[:, c * 128:(c + 1) * 128] = (parts[0] + parts[1]).astype(BF16)

    return _call(
        body, name=name, grid=(s // BLOCK,), in_specs=_attn_in_specs() + [HBM_SPEC] * len(after),
        out_specs=pl.BlockSpec((BLOCK, D), lambda n: (n, 0)), out_shape=_sds((s, D), BF16),
        compiler_params=_params("parallel"),
    )(proj, proj, proj, proj, proj, tab, tab, sinks, *after)


def _swa_bwd(do, proj, tab, sinks, dproj, name, after=()):
    s = proj.shape[0]
    nblk = s // BLOCK
    kv_of = lambda c: c // (GROUP // 2)

    def body(do_ref, q_ref, kc_ref, kp_ref, vc_ref, vp_ref, tc_ref, tp_ref, sink_ref, *rest):
        dproj_ref, dk_ref, dv_ref, ds_ref, dqout, dkbuf, dvbuf, sems = rest[1 + len(after):]
        n = pl.program_id(0)

        @pl.when(n == 0)
        def _():
            dk_ref[...] = jnp.zeros_like(dk_ref)
            dv_ref[...] = jnp.zeros_like(dv_ref)
            ds_ref[...] = jnp.zeros_like(ds_ref)

        q_halves, kdup, vdup = _swa_load(q_ref, kc_ref, kp_ref, vc_ref, vp_ref, tc_ref, tp_ref)
        dof = do_ref[...].astype(F32)
        do_halves = [_halves(dof[:, c * 128:(c + 1) * 128]) for c in range(N_CHUNK)]
        bias = _swa_bias(n)
        ones = jnp.ones((2 * BLOCK, 128), BF16)
        kk = [k.astype(BF16) for k in kdup]
        vv = [v.astype(BF16) for v in vdup]
        k_halves = [_halves(k) for k in kdup]
        heads = [(c, half) for c in range(N_CHUNK) for half in range(2)]
        lane_row = lax.broadcasted_iota(jnp.int32, (1, 128), 1)
        lo_kv = lax.broadcasted_iota(jnp.int32, (2 * BLOCK, 128), 1) < HALF
        scores = [lax.dot_general(q_halves[c][half], kk[kv_of(c)], NT, preferred_element_type=F32) for c, half in heads]
        dps = [lax.dot_general(do_halves[c][half], vv[kv_of(c)], NT, preferred_element_type=F32) for c, half in heads]
        exps = []
        for (c, half), sc in zip(heads, scores):
            sink = sink_ref[0, 2 * c + half]
            sc = sc + bias
            m = jnp.maximum(jnp.max(jnp.maximum(sc[:, :BLOCK], sc[:, BLOCK:]), axis=1, keepdims=True), sink)
            exps.append((jnp.exp(sc - m), jnp.exp(sink - m)))
        sums = [lax.dot_general(e.astype(BF16), ones, NN, preferred_element_type=F32) for e, _ in exps]
        dsink_row = jnp.zeros((1, 128), F32)
        dsb, pb = [], []
        for h, ((e, es), row_sum, dp) in enumerate(zip(exps, sums, dps)):
            inv = 1.0 / (row_sum + es)
            p = e * jnp.concatenate([inv, inv], axis=1)
            t = p * dp
            delta = jnp.sum(t, axis=1, keepdims=True)
            dsb.append((t - p * delta).astype(BF16))
            pb.append(p.astype(BF16))
            dsink = -jnp.sum(es * inv * delta, axis=0, keepdims=True)
            dsink_row = dsink_row + jnp.where(lane_row == h, dsink, 0.0)
        dq_parts = [lax.dot_general(d, k_halves[kv_of(c)][half], NN, preferred_element_type=F32) for (c, half), d in zip(heads, dsb)]
        dk_parts = [lax.dot_general(d, q_halves[c][half], TN, preferred_element_type=F32) for (c, half), d in zip(heads, dsb)]
        dv_parts = [lax.dot_general(p, do_halves[c][half], TN, preferred_element_type=F32) for (c, half), p in zip(heads, pb)]
        dq = jnp.concatenate([(dq_parts[2 * c] + dq_parts[2 * c + 1]) * ATTN_SCALE for c in range(N_CHUNK)], axis=1)

        def kv_sum(parts, hk):
            acc = (parts[GROUP * hk] + parts[GROUP * hk + 1]) + (parts[GROUP * hk + 2] + parts[GROUP * hk + 3])
            return acc + pltpu.roll(acc, HALF, 1)

        for pair in range(N_KV // 2):
            dkbuf[:, pair * 128:(pair + 1) * 128] = jnp.where(lo_kv, kv_sum(dk_parts, 2 * pair), kv_sum(dk_parts, 2 * pair + 1))
            dvbuf[:, pair * 128:(pair + 1) * 128] = jnp.where(lo_kv, kv_sum(dv_parts, 2 * pair), kv_sum(dv_parts, 2 * pair + 1))
        prev0 = pl.multiple_of(jnp.maximum(n - 1, 0) * BLOCK, BLOCK)
        cur0 = pl.multiple_of(n * BLOCK, BLOCK)

        @pl.when(n > 0)
        def _():
            dk_ref[pl.ds(prev0, BLOCK), :] += dkbuf[0:BLOCK, :]
            dv_ref[pl.ds(prev0, BLOCK), :] += dvbuf[0:BLOCK, :]

        dk_ref[pl.ds(cur0, BLOCK), :] += dkbuf[BLOCK:2 * BLOCK, :]
        dv_ref[pl.ds(cur0, BLOCK), :] += dvbuf[BLOCK:2 * BLOCK, :]
        ds_ref[...] += dsink_row

        def window(p, at):
            return dproj_ref.at[pl.ds(pl.multiple_of(at * BLOCK, BLOCK), BLOCK), pl.ds(C_Q, D)]

        _write_behind(n, nblk, dqout, sems, (_rope_t(dq, tc_ref[...]).astype(BF16),), window, n)

    blk = lambda w: pl.BlockSpec((BLOCK, w), lambda n: (n, 0))
    whole = lambda w: pl.BlockSpec((s, w), lambda n: (0, 0))
    n_in = 1 + len(_attn_in_specs())
    return _call(
        body, name=name, grid=(nblk,), in_specs=[blk(D)] + _attn_in_specs() + [HBM_SPEC] * (1 + len(after)),
        out_specs=[HBM_SPEC, whole(D_KV), whole(D_KV), pl.BlockSpec((1, 128), lambda n: (0, 0))],
        out_shape=[_sds((s, N_IN), BF16), _sds((s, D_KV), F32), _sds((s, D_KV), F32), _sds((1, 128), F32)],
        scratch_shapes=[pltpu.VMEM((2, 1, BLOCK, D), BF16), pltpu.VMEM((2 * BLOCK, D_KV), F32),
                        pltpu.VMEM((2 * BLOCK, D_KV), F32), pltpu.SemaphoreType.DMA((2, 1))],
        input_output_aliases={n_in: 0}, compiler_params=_params("arbitrary"),
    )(do, proj, proj, proj, proj, proj, tab, tab, sinks, dproj, *after)


def _kv_bwd(dkr, dv, tab, dproj, name):
    s = dkr.shape[0]
    tm = _row_tile(s)

    def body(dk_ref, dv_ref, t_ref, dproj_in, o_ref):
        del dproj_in
        o_ref[:, 0:D_KV] = _rope_t(dk_ref[...], t_ref[...]).astype(BF16)
        o_ref[:, D_KV:2 * D_KV] = dv_ref[...].astype(BF16)

    row = lambda w: pl.BlockSpec((tm, w), lambda i: (i, 0))
    return _call(
        body, name=name, grid=(s // tm,),
        in_specs=[row(D_KV), row(D_KV), row(384), pl.BlockSpec(memory_space=pl.ANY)],
        out_specs=pl.BlockSpec((tm, 2 * D_KV), lambda i: (i, C_K // (2 * D_KV))),
        out_shape=_sds((s, N_IN), BF16), input_output_aliases={3: 0}, compiler_params=_params("parallel"),
    )(dkr, dv, tab, dproj)


EW_TC = 512


def _sigmoid(x):
    return 0.5 * jnp.tanh(0.5 * x) + 0.5


def _merge_fwd(proj, conv_out, attn_out, name):
    s = proj.shape[0]
    tm = _row_tile(s)
    tile = pl.BlockSpec((tm, EW_TC), lambda i, j: (i, j))

    def body(gc_ref, ga_ref, co_ref, ao_ref, o_ref):
        o_ref[...] = (_sigmoid(gc_ref[...].astype(F32)) * co_ref[...].astype(F32)
                      + _sigmoid(ga_ref[...].astype(F32)) * ao_ref[...].astype(F32)).astype(BF16)

    return _call(
        body, name=name, grid=(s // tm, D // EW_TC),
        in_specs=[pl.BlockSpec((tm, EW_TC), lambda i, j: (i, C_GC // EW_TC + j)),
                  pl.BlockSpec((tm, EW_TC), lambda i, j: (i, C_GA // EW_TC + j)), tile, tile],
        out_specs=tile, out_shape=_sds((s, D), BF16), compiler_params=_params("parallel", "parallel"),
    )(proj, proj, conv_out, attn_out)


def _merge_bwd(dmerged, proj, conv_out, attn_out, name):
    s = proj.shape[0]
    tm = _row_tile(s)
    tile = pl.BlockSpec((tm, EW_TC), lambda i, j: (i, j))
    anyspec = pl.BlockSpec(memory_space=pl.ANY)

    def body(dm_ref, gc_ref, ga_ref, co_ref, ao_ref, dproj_ref, dco_ref, dao_ref, buf, sems):
        i, j = pl.program_id(0), pl.program_id(1)
        dm = dm_ref[...].astype(F32)
        sc = _sigmoid(gc_ref[...].astype(F32))
        sa = _sigmoid(ga_ref[...].astype(F32))
        dco_ref[...] = (dm * sc).astype(BF16)
        dao_ref[...] = (dm * sa).astype(BF16)
        tiles = ((dm * co_ref[...].astype(F32) * sc * (1.0 - sc)).astype(BF16),
                 (dm * ao_ref[...].astype(F32) * sa * (1.0 - sa)).astype(BF16))

        def window(p, at):
            start = pl.multiple_of((C_GC, C_GA)[p] + at[1] * EW_TC, EW_TC)
            return dproj_ref.at[pl.ds(pl.multiple_of(at[0] * tm, tm), tm), pl.ds(start, EW_TC)]

        _write_behind(i * nj + j, (s // tm) * nj, buf, sems, tiles, window, (i, j))

    nj = D // EW_TC
    return _call(
        body, name=name, grid=(s // tm, nj),
        in_specs=[tile, pl.BlockSpec((tm, EW_TC), lambda i, j: (i, C_GC // EW_TC + j)),
                  pl.BlockSpec((tm, EW_TC), lambda i, j: (i, C_GA // EW_TC + j)), tile, tile],
        out_specs=[anyspec, tile, tile],
        out_shape=[_sds((s, N_IN), BF16), _sds((s, D), BF16), _sds((s, D), BF16)],
        scratch_shapes=[pltpu.VMEM((2, 2, tm, EW_TC), BF16), pltpu.SemaphoreType.DMA((2, 2))],
        compiler_params=_params("arbitrary", "arbitrary"),
    )(dmerged, proj, proj, conv_out, attn_out)


FF_TC = 256


def _row_pipeline(tm, matmul, finish, split=ROW_SPLIT):
    step = tm // split
    pending = None
    for r in range(split):
        rows = pl.ds(r * step, step)
        result = matmul(rows)
        if pending is not None:
            finish(*pending)
        pending = (rows, result)
    finish(*pending)


def _gate_up_fwd(h2, wgu_t, name):
    s = h2.shape[0]
    tm = min(2048, s)
    nb = D_FF // FF_TC

    def body(h_ref, wg_ref, wu_ref, a_ref, dadu_ref, dadg_ref):
        def matmuls(rows):
            h = h_ref[rows, :]
            return (lax.dot_general(h, wg_ref[...], NT, preferred_element_type=F32),
                    lax.dot_general(h, wu_ref[...], NT, preferred_element_type=F32))

        def finish(rows, gu):
            g, u = gu
            sg = _sigmoid(g)
            silu = g * sg
            a_ref[rows, :] = (silu * u).astype(BF16)
            dadu_ref[rows, :] = silu.astype(BF16)
            dadg_ref[rows, :] = (u * (sg * (1.0 + g * (1.0 - sg)))).astype(BF16)

        _row_pipeline(tm, matmuls, finish)

    tile = pl.BlockSpec((tm, FF_TC), lambda i, j: (i, j))
    return _call(
        body, name=name, grid=(s // tm, nb),
        in_specs=[pl.BlockSpec((tm, D), lambda i, j: (i, 0)), pl.BlockSpec((FF_TC, D), lambda i, j: (j, 0)),
                  pl.BlockSpec((FF_TC, D), lambda i, j: (nb + j, 0))],
        out_specs=[tile, tile, tile], out_shape=[_sds((s, D_FF), BF16)] * 3,
        compiler_params=_params("parallel", "parallel"),
    )(h2, wgu_t, wgu_t)


def _down_bwd_x(dx2b, wd, dadg, dadu, name):
    s = dx2b.shape[0]
    tm = min(2048, s)
    nb = D_FF // FF_TC

    def body(dx_ref, w_ref, dadg_ref, dadu_ref, dg_ref, du_ref):
        def matmul(rows):
            return lax.dot_general(dx_ref[rows, :], w_ref[...], NT, preferred_element_type=F32)

        def finish(rows, da):
            dg_ref[rows, :] = (da * dadg_ref[rows, :].astype(F32)).astype(BF16)
            du_ref[rows, :] = (da * dadu_ref[rows, :].astype(F32)).astype(BF16)

        _row_pipeline(tm, matmul, finish)

    tile = pl.BlockSpec((tm, FF_TC), lambda i, j: (i, j))
    return _call(
        body, name=name, grid=(s // tm, nb),
        in_specs=[pl.BlockSpec((tm, D), lambda i, j: (i, 0)), pl.BlockSpec((FF_TC, D), lambda i, j: (j, 0)), tile, tile],
        out_specs=[tile, tile], out_shape=[_sds((s, D_FF), BF16)] * 2,
        compiler_params=_params("parallel", "parallel"),
    )(dx2b, wd, dadg, dadu)


class _Weights:
    def __init__(self, **groups):
        self.groups = groups

    def begin(self, group, after):
        return ()

    def end(self, group, after):
        return self.groups[group]


class _NoReduce:
    def start(self, group, grads):
        return ()

    def middle(self, group, after):
        return ()


def _local_step(x, tgt, g_mix, g_ffn, g_final, sinks, weights, reducer=None, after=()):
    reducer = reducer or _NoReduce()
    s = x.shape[0]
    tab = _rope_tables(s)
    big = dict(tm=1024, tn=512, tk=1024)
    h1 = _rms_fwd(x, g_mix, "rms1_fwd", after=after)
    win_t, conv_w = weights.end("in", weights.begin("in", (h1,)))
    proj = _matmul(h1, win_t, mode="nt", out_dtype=BF16, name="proj_fwd", tm=2048, tn=512, tk=1024)
    attn = _swa_fwd(proj, tab, sinks, "attn_fwd", after=weights.begin("mix", (proj,)))
    wco, wao, wo = weights.end("mix", (attn,))
    conv_y = _conv_fwd(proj, conv_w, "conv_fwd")
    conv_out = _matmul(conv_y, wco, mode="nn", out_dtype=BF16, name="conv_out_fwd", **big)
    attn_out = _matmul(attn, wao, mode="nn", out_dtype=BF16, name="attn_out_fwd", **big)
    merged = _merge_fwd(proj, conv_out, attn_out, "merge_fwd")
    x1 = _matmul(merged, wo, mode="nn", out_dtype=F32, name="wo_fwd", res=x, after=weights.begin("ffn", (merged,)), **big)
    h2 = _rms_fwd(x1, g_ffn, "rms2_fwd")
    wgu_t, wd = weights.end("ffn", (h2,))
    act, dadu, dadg = _gate_up_fwd(h2, wgu_t, "gate_up_fwd")
    x2 = _matmul(act, wd, mode="nn", out_dtype=F32, name="down_fwd", res=x1, tm=1024, tn=512, tk=D_FF)
    dx2, dx2b, dg_final, lossvec = _loss_head(x2, g_final, tgt, "loss_head")
    dgate, dup = _down_bwd_x(dx2b, wd, dadg, dadu, "down_bwd_x")
    g_wd = _matmul(act, dx2b, mode="tn", out_dtype=BF16, name="down_bwd_w", tm=1408, tn=1024, tk=2048)
    dh2 = _matmul([dgate, dup], wgu_t, mode="nn", out_dtype=F32, name="gate_up_bwd_x", tm=1024, tn=1024, tk=1408)
    g_wgu_t = _matmul([dgate, dup], h2, mode="tn", out_dtype=BF16, name="gate_up_bwd_w", tm=1408, tn=1024, tk=2048)
    after_ffn = reducer.start("ffn", dict(wgu_t=g_wgu_t, wd=g_wd))
    dx1, dx1b, dg_ffn = _rms_bwd(dh2, x1, g_ffn, dx2, "rms2_bwd")
    dmerged = _matmul(dx1b, wo, mode="nt", out_dtype=BF16, name="wo_bwd_x", after=after_ffn, **big)
    after_ffn = reducer.middle("ffn", (dmerged,))
    g_wo = _matmul(merged, dx1b, mode="tn", out_dtype=BF16, name="wo_bwd_w", tm=512, tn=1024, tk=2048, after=after_ffn)
    dproj, dco, dao = _merge_bwd(dmerged, proj, conv_out, attn_out, "merge_bwd")
    dconv_y = _matmul(dco, wco, mode="nt", out_dtype=BF16, name="conv_out_bwd_x", **big)
    g_wco = _matmul(conv_y, dco, mode="tn", out_dtype=BF16, name="conv_out_bwd_w", tm=512, tn=1024, tk=2048)
    dattn = _matmul(dao, wao, mode="nt", out_dtype=BF16, name="attn_out_bwd_x", **big)
    g_wao = _matmul(attn, dao, mode="tn", out_dtype=BF16, name="attn_out_bwd_w", tm=512, tn=1024, tk=2048)
    after_mix = reducer.start("mix", dict(wco=g_wco, wao=g_wao, wo=g_wo))
    dproj, dconv_w = _conv_bwd(dconv_y, proj, conv_w, dproj, "conv_bwd", after=after_mix)
    after_mix = reducer.middle("mix", (dconv_w,))
    dproj, dkr, dv, dsinks = _swa_bwd(dattn, proj, tab, sinks, dproj, "attn_bwd", after=after_mix)
    dproj = _kv_bwd(dkr, dv, tab, dproj, "kv_bwd")
    g_win_t = _matmul(dproj, h1, mode="tn", out_dtype=BF16, name="proj_bwd_w", tm=512, tn=1024, tk=2048)
    after_in = reducer.middle("in", reducer.start("in", dict(win_t=g_win_t)))
    dh1 = _matmul(dproj, win_t, mode="nn", out_dtype=F32, name="proj_bwd_x", tm=1024, tn=1024, tk=1664, after=after_in)
    dx, _, dg_mix = _rms_bwd(dh1, x, g_mix, dx1, "rms1_bwd")
    grads = dict(win_t=g_win_t, wgu_t=g_wgu_t, wd=g_wd, wco=g_wco, wao=g_wao, wo=g_wo)
    small = dict(g_mix=dg_mix, g_ffn=dg_ffn, g_final=dg_final, conv_w=dconv_w, sinks=dsinks, lossvec=lossvec)
    return dx, grads, small


def _position():
    return lax.axis_index("x"), lax.axis_index("y"), lax.axis_index("c")


def _other_chips(x, y):
    return [(1 - x, y), (x, 1 - y), (1 - x, 1 - y)]


SEM_SPEC = pl.BlockSpec(memory_space=pltpu.SEMAPHORE)
EFFECT = pltpu.SideEffectType.DATAFLOW_SIDE_EFFECTING
TOKEN = jax.ShapeDtypeStruct((8, 128), F32)
TOKEN_SPEC = pl.BlockSpec(memory_space=pltpu.VMEM)


def _hbm(a):
    return pltpu.with_memory_space_constraint(a, pltpu.HBM)


def _place(w, me_idx, dtype, name, after=()):
    r, cdim = w.shape

    def body(i_ref, w_ref, *rest):
        rest[-1][...] = w_ref[...].astype(dtype)

    grid_spec = pltpu.PrefetchScalarGridSpec(
        num_scalar_prefetch=1, grid=(1,), in_specs=[pl.BlockSpec((r, cdim), lambda i, me: (0, 0))] + [HBM_SPEC] * len(after),
        out_specs=pl.BlockSpec((r, cdim), lambda i, me: (me[0], 0)))
    return _call(body, name=name, grid_spec=grid_spec, out_shape=_sds((N_DEV * r, cdim), dtype),
                 compiler_params=_params("arbitrary"))(me_idx, w, *after)


def _own_rows(ref, r, px, py, pc):
    return ref.at[pl.ds((4 * px + 2 * py + pc) * r, r), :]


def _gather_phase(bufs, waits, plans, after, name):
    n = len(bufs)
    rows = [b.shape[0] // N_DEV for b in bufs]
    nw, npl = len(waits), len(plans)

    def body(*refs):
        ins = refs[:n]
        wait_sems = refs[n:n + 2 * nw]
        out0 = n + 2 * nw + len(after)
        new_sems = refs[out0:out0 + 2 * npl]
        token = refs[-1]
        x, y, c = _position()
        for w, (_, _, sent, received) in enumerate(waits):
            for a in range(n):
                for count, wait in ((sent, "wait_send"), (received, "wait_recv")):
                    span = _whole(ins[a], count * rows[a])
                    getattr(pltpu.make_async_remote_copy(
                        src_ref=span, dst_ref=span, send_sem=wait_sems[2 * w].at[a], recv_sem=wait_sems[2 * w + 1].at[a],
                        device_id=(x, y, c), device_id_type=MESH), wait)()
        for k, plan in enumerate(plans):
            for a in range(n):
                for block, target in plan(x, y, c):
                    span = _own_rows(ins[a], rows[a], *block)
                    pltpu.make_async_remote_copy(src_ref=span, dst_ref=span, send_sem=new_sems[2 * k].at[a],
                                                 recv_sem=new_sems[2 * k + 1].at[a], device_id=target, device_id_type=MESH).start()
        token[...] = jnp.zeros_like(token)

    sem_ops = [s for send, recv, _, _ in waits for s in (send, recv)]
    outs = _call(
        body, name=name, in_specs=[HBM_SPEC] * n + [SEM_SPEC] * (2 * nw) + [HBM_SPEC] * len(after),
        out_specs=[SEM_SPEC] * (2 * npl) + [HBM_SPEC] * n + [TOKEN_SPEC],
        out_shape=[pltpu.SemaphoreType.DMA((n,))] * (2 * npl) + [pltpu.HBM(b.shape, b.dtype) for b in bufs] + [TOKEN],
        input_output_aliases={i: 2 * npl + i for i in range(n)},
        compiler_params=pltpu.CompilerParams(has_side_effects=EFFECT),
    )(*[_hbm(b) for b in bufs], *sem_ops, *after)
    pairs = [(outs[2 * k], outs[2 * k + 1]) for k in range(npl)]
    return pairs, list(outs[2 * npl:2 * npl + n]), outs[-1]


def _own_to_near(x, y, c):
    return [((x, y, c), (x, y, 1 - c)), ((x, y, c), (1 - x, y, c)), ((x, y, c), (x, 1 - y, c))]


def _near_to_sibling(x, y, c):
    return [((1 - x, y, c), (x, y, 1 - c)), ((x, 1 - y, c), (x, y, 1 - c))]


def _relay_diagonal(x, y, c):
    north = c
    source = (x * north + (1 - x) * (1 - north), (1 - y) * north + y * (1 - north), c)
    target = ((1 - x) * north + x * (1 - north), y * north + (1 - y) * (1 - north), c)
    return [(source, target)]


def _diagonal_to_sibling(x, y, c):
    return [((1 - x, 1 - y, c), (x, y, 1 - c))]


def _gather_start(bufs, groups, name):
    n = len(bufs)
    rows = [b.shape[0] // N_DEV for b in bufs]
    ng = len(groups)

    def body(*refs):
        ins = refs[:n]
        sems = refs[n:n + 2 * ng]
        token = refs[-1]
        x, y, c = _position()
        targets = [(x, y, 1 - c)] + [(*chip, c) for chip in _other_chips(x, y)]
        for g, members in enumerate(groups):
            for slot, a in enumerate(members):
                own = _own_rows(ins[a], rows[a], x, y, c)
                for to in targets:
                    pltpu.make_async_remote_copy(src_ref=own, dst_ref=own, send_sem=sems[2 * g].at[slot],
                                                 recv_sem=sems[2 * g + 1].at[slot], device_id=to, device_id_type=MESH).start()
        token[...] = jnp.zeros_like(token)

    sem_shapes = []
    for members in groups:
        sem_shapes += [pltpu.SemaphoreType.DMA((len(members),))] * 2
    outs = _call(
        body, name=name, in_specs=[HBM_SPEC] * n, out_specs=[SEM_SPEC] * (2 * ng) + [HBM_SPEC] * n + [TOKEN_SPEC],
        out_shape=sem_shapes + [pltpu.HBM(b.shape, b.dtype) for b in bufs] + [TOKEN],
        input_output_aliases={i: 2 * ng + i for i in range(n)},
        compiler_params=pltpu.CompilerParams(has_side_effects=EFFECT),
    )(*[_hbm(b) for b in bufs])
    sem_pairs = [(outs[2 * g], outs[2 * g + 1]) for g in range(ng)]
    return sem_pairs, list(outs[2 * ng:2 * ng + n]), outs[-1]


def _gather_forward(send_sems, recv_sems, bufs, after, name):
    n = len(bufs)
    rows = [b.shape[0] // N_DEV for b in bufs]

    def body(*refs):
        ins = refs[:n]
        send1, recv1 = refs[n], refs[n + 1]
        out0 = n + 2 + len(after)
        send2, recv2 = refs[out0], refs[out0 + 1]
        token = refs[-1]
        x, y, c = _position()
        for a in range(n):
            step1 = pltpu.make_async_remote_copy(
                src_ref=_whole(ins[a], 4 * rows[a]), dst_ref=_whole(ins[a], 4 * rows[a]), send_sem=send1.at[a],
                recv_sem=recv1.at[a], device_id=(x, y, c), device_id_type=MESH)
            step1.wait_send()
            step1.wait_recv()
        for a in range(n):
            for chip in _other_chips(x, y):
                blk = _own_rows(ins[a], rows[a], *chip, c)
                pltpu.make_async_remote_copy(src_ref=blk, dst_ref=blk, send_sem=send2.at[a], recv_sem=recv2.at[a],
                                             device_id=(x, y, 1 - c), device_id_type=MESH).start()
        token[...] = jnp.zeros_like(token)

    outs = _call(
        body, name=name, in_specs=[HBM_SPEC] * n + [SEM_SPEC, SEM_SPEC] + [HBM_SPEC] * len(after),
        out_specs=[SEM_SPEC, SEM_SPEC] + [HBM_SPEC] * n + [TOKEN_SPEC],
        out_shape=[pltpu.SemaphoreType.DMA((n,)), pltpu.SemaphoreType.DMA((n,))]
        + [pltpu.HBM(b.shape, b.dtype) for b in bufs] + [TOKEN],
        input_output_aliases={i: 2 + i for i in range(n)},
        compiler_params=pltpu.CompilerParams(has_side_effects=EFFECT),
    )(*bufs, send_sems, recv_sems, *after)
    return outs[0], outs[1], list(outs[2:2 + n]), outs[-1]


def _gather_done(send_sems, recv_sems, bufs, after, name):
    n = len(bufs)
    rows = [b.shape[0] // N_DEV for b in bufs]

    def body(*refs):
        ins = refs[:n]
        send2, recv2 = refs[n], refs[n + 1]
        x, y, c = _position()
        for a in range(n):
            step2 = pltpu.make_async_remote_copy(
                src_ref=_whole(ins[a], 3 * rows[a]), dst_ref=_whole(ins[a], 3 * rows[a]), send_sem=send2.at[a],
                recv_sem=recv2.at[a], device_id=(x, y, c), device_id_type=MESH)
            step2.wait_send()
            step2.wait_recv()

    outs = _call(
        body, name=name, in_specs=[HBM_SPEC] * n + [SEM_SPEC, SEM_SPEC] + [HBM_SPEC] * len(after),
        out_specs=[HBM_SPEC] * n, out_shape=[pltpu.HBM(b.shape, b.dtype) for b in bufs],
        input_output_aliases={i: i for i in range(n)},
        compiler_params=pltpu.CompilerParams(has_side_effects=EFFECT),
    )(*bufs, send_sems, recv_sems, *after)
    return list(outs)


def _whole(ref, nrows):
    return ref.at[pl.ds(0, nrows), :]


def _to_sibling(x, y, c):
    return [(2 * q + (1 - c), q, (x, y, 1 - c)) for q in range(4)]


def _to_chips(x, y, c):
    return [(2 * px + py, j, (px, py, c)) for j, (px, py) in enumerate(_other_chips(x, y))]


def _exchange_start(srcs, src_slots, plan, name):
    n = len(srcs)
    rows = [a.shape[0] // src_slots for a in srcs]
    n_copies = len(plan(0, 0, 0))
    lands = [lax.empty((n_copies * r, a.shape[1]), a.dtype) for a, r in zip(srcs, rows)]

    def body(*refs):
        ins, land_refs = refs[:n], refs[n:2 * n]
        send_sems, recv_sems = refs[2 * n], refs[2 * n + 1]
        token = refs[-1]
        for a in range(n):
            r = rows[a]
            for src_slot, dst_slot, target in plan(*_position()):
                pltpu.make_async_remote_copy(
                    src_ref=ins[a].at[pl.ds(src_slot * r, r), :], dst_ref=land_refs[a].at[pl.ds(dst_slot * r, r), :],
                    send_sem=send_sems.at[a], recv_sem=recv_sems.at[a], device_id=target, device_id_type=MESH).start()
        token[...] = jnp.zeros_like(token)

    outs = _call(
        body, name=name, in_specs=[HBM_SPEC] * (2 * n),
        out_specs=[SEM_SPEC, SEM_SPEC] + [HBM_SPEC] * (2 * n) + [TOKEN_SPEC],
        out_shape=[pltpu.SemaphoreType.DMA((n,)), pltpu.SemaphoreType.DMA((n,))]
        + [pltpu.HBM(a.shape, a.dtype) for a in srcs] + [pltpu.HBM(l.shape, l.dtype) for l in lands] + [TOKEN],
        input_output_aliases={i: 2 + i for i in range(2 * n)},
        compiler_params=pltpu.CompilerParams(has_side_effects=EFFECT),
    )(*[_hbm(a) for a in srcs], *[_hbm(l) for l in lands])
    return outs[0], outs[1], list(outs[2:2 + n]), list(outs[2 + n:2 + 2 * n]), outs[-1]


def _exchange_wait(send_sems, recv_sems, srcs, lands, after, name):
    n = len(srcs)

    def body(*refs):
        ins, land_refs = refs[:n], refs[n:2 * n]
        send_sems_ref, recv_sems_ref = refs[2 * n], refs[2 * n + 1]
        for a in range(n):
            allrows = lands[a].shape[0]
            cp = pltpu.make_async_remote_copy(
                src_ref=_whole(ins[a], allrows), dst_ref=_whole(land_refs[a], allrows), send_sem=send_sems_ref.at[a],
                recv_sem=recv_sems_ref.at[a], device_id=_position(), device_id_type=MESH)
            cp.wait_send()
            cp.wait_recv()

    outs = _call(
        body, name=name, in_specs=[HBM_SPEC] * (2 * n) + [SEM_SPEC, SEM_SPEC] + [HBM_SPEC] * len(after),
        out_specs=[HBM_SPEC] * (2 * n),
        out_shape=[pltpu.HBM(a.shape, a.dtype) for a in srcs] + [pltpu.HBM(l.shape, l.dtype) for l in lands],
        input_output_aliases={i: i for i in range(2 * n)},
        compiler_params=pltpu.CompilerParams(has_side_effects=EFFECT),
    )(*srcs, *lands, send_sems, recv_sems, *after)
    return list(outs[:n]), list(outs[n:])


def _chip_partial(grad, recv, idx, name):
    r = recv.shape[0] // 4

    def body(i_ref, g_ref, s_ref, o_ref):
        del i_ref
        o_ref[...] = (g_ref[...].astype(F32) + s_ref[...].astype(F32)).astype(BF16)

    nb = 1
    tr = r // nb
    grid_spec = pltpu.PrefetchScalarGridSpec(
        num_scalar_prefetch=1, grid=(3, nb),
        in_specs=[pl.BlockSpec((tr, D), lambda t, i, i_ref: ((2 * i_ref[1 + t] + i_ref[0]) * nb + i, 0)),
                  pl.BlockSpec((tr, D), lambda t, i, i_ref: (i_ref[1 + t] * nb + i, 0))],
        out_specs=pl.BlockSpec((tr, D), lambda t, i, i_ref: (i_ref[1 + t] * nb + i, 0)))
    return _call(body, name=name, grid_spec=grid_spec, out_shape=_sds((4 * r, D), BF16),
                 compiler_params=_params("arbitrary", "arbitrary"))(idx, grad, recv)


def _adamw_math(w, g, m, v):
    m2 = B1 * m + (1.0 - B1) * g
    v2 = B2 * v + (1.0 - B2) * jnp.square(g)
    m_hat = m2 / (1.0 - B1 ** STEP)
    v_hat = v2 / (1.0 - B2 ** STEP)
    return -LR * (m_hat / (jnp.sqrt(v_hat) + EPS_ADAM) + WD * w), m2, v2


def _reduce_adamw(w, grad, from_sibling, from_chips, idx, m, v, name):
    r = w.shape[0]
    assert grad.shape == (N_DEV * r, D) and from_sibling.shape == (4 * r, D) and from_chips.shape == (3 * r, D)
    tr = r // 2
    nb = r // tr

    def body(i_ref, w_ref, p_ref, s_ref, r0_ref, r1_ref, r2_ref, m_ref, v_ref, g_ref, d_ref, nm_ref, nv_ref):
        del i_ref
        g = p_ref[...].astype(F32) + s_ref[...].astype(F32)
        g = ((g + r0_ref[...].astype(F32)) + r1_ref[...].astype(F32)) + r2_ref[...].astype(F32)
        g_ref[...] = g
        d_ref[...], nm_ref[...], nv_ref[...] = _adamw_math(w_ref[...], g, m_ref[...], v_ref[...])

    own = pl.BlockSpec((tr, D), lambda i, i_ref: (i, 0))
    grid_spec = pltpu.PrefetchScalarGridSpec(
        num_scalar_prefetch=1, grid=(nb,),
        in_specs=[own, pl.BlockSpec((tr, D), lambda i, i_ref: (i_ref[0] * nb + i, 0)),
                  pl.BlockSpec((tr, D), lambda i, i_ref: (i_ref[1] * nb + i, 0))]
        + [pl.BlockSpec((tr, D), lambda i, i_ref, j=j: (j * nb + i, 0)) for j in range(3)] + [own, own],
        out_specs=[own] * 4)
    return _call(body, name=name, grid_spec=grid_spec, out_shape=[_sds((r, D), F32)] * 4,
                 compiler_params=_params("parallel"))(idx, w, grad, from_sibling, from_chips, from_chips, from_chips, m, v)


SMALL_ROWS = 8


def _small_all_reduce(pack, name, after=()):
    def body(p_ref, *rest):
        tot_ref, loss_ref, gath, send_sems, recv_sems = rest[len(after):]
        x, y, c = _position()
        me_id = 4 * x + 2 * y + c
        gath[me_id] = p_ref[...]
        copies = []
        for k in range(1, N_DEV):
            peer = tuple(1 - v if (k >> b) & 1 else v for v, b in ((x, 2), (y, 1), (c, 0)))
            cp = pltpu.make_async_remote_copy(src_ref=p_ref, dst_ref=gath.at[me_id], send_sem=send_sems.at[k - 1],
                                              recv_sem=recv_sems.at[k - 1], device_id=peer, device_id_type=MESH)
            cp.start()
            copies.append(cp)
        for cp in copies:
            cp.wait_recv()
        for cp in copies:
            cp.wait_send()
        tot = gath[0]
        for d in range(1, N_DEV):
            tot = tot + gath[d]
        tot_ref[...] = tot
        loss_ref[...] = jnp.full((1, 128), (0.5 / D) * jnp.sum(tot[SMALL_ROWS - 1:SMALL_ROWS, :]), F32)

    vm = pl.BlockSpec(memory_space=pltpu.VMEM)
    return _call(
        body, name=name, in_specs=[vm] + [HBM_SPEC] * len(after), out_specs=[vm, vm],
        out_shape=[_sds((SMALL_ROWS, D), F32), _sds((1, 128), F32)],
        scratch_shapes=[pltpu.VMEM((N_DEV, SMALL_ROWS, D), F32), pltpu.SemaphoreType.DMA((N_DEV - 1,)),
                        pltpu.SemaphoreType.DMA((N_DEV - 1,))],
    )(pack, *after)


def _adamw(w, g, m, v, name):
    r, cdim = w.shape
    tr = 256 if r % 256 == 0 else (r // 2 if r % 16 == 0 else r)

    def body(w_ref, g_ref, m_ref, v_ref, d_ref, nm_ref, nv_ref):
        d_ref[...], nm_ref[...], nv_ref[...] = _adamw_math(w_ref[...], g_ref[...], m_ref[...], v_ref[...])

    spec = pl.BlockSpec((tr, cdim), lambda i: (i, 0))
    return _call(
        body, name=name, grid=(r // tr,), in_specs=[spec] * 4, out_specs=[spec] * 3,
        out_shape=[_sds((r, cdim), F32)] * 3, compiler_params=_params("parallel"),
    )(w, g, m, v)


def kernel(x, g_mix, w_in, conv_w, attn_sinks, w_conv_out, w_attn_out, w_o, g_ffn, w_gate_up, w_down, g_final, loss_target, m_g_mix, m_w_in, m_conv_w, m_attn_sinks, m_w_conv_out, m_w_attn_out, m_w_o, m_g_ffn, m_w_gate_up, m_w_down, m_g_final, v_g_mix, v_w_in, v_conv_w, v_attn_sinks, v_w_conv_out, v_w_attn_out, v_w_o, v_g_ffn, v_w_gate_up, v_w_down, v_g_final):
    cx, cy, cc = _position()
    chip = 2 * cx + cy
    partial_idx = jnp.stack([cc, 2 * (1 - cx) + cy, 2 * cx + (1 - cy), 2 * (1 - cx) + (1 - cy)]).astype(jnp.int32)
    own_idx = jnp.stack([2 * chip + cc, chip]).astype(jnp.int32)
    me = 4 * cx + 2 * cy + cc

    me_idx = jnp.reshape(me, (1,)).astype(jnp.int32)
    first = [_place(jnp.transpose(w_in[0]), me_idx, BF16, "place_w_in"),
             _place(jnp.pad(conv_w[0], ((0, 5), (0, 0))), me_idx, F32, "place_conv_w")]
    (to_near,), first, token_in = _gather_phase(first, [], [_own_to_near], (), "gather_in_start")
    gather_tokens = (token_in,)

    class Gathered:
        def __init__(self):
            self.state = {}

        def begin(self, group, after):
            if group == "in":
                (near, relay), bufs, token = _gather_phase(
                    first, [(*to_near, 3, 3)], [_near_to_sibling, _relay_diagonal], after, "gather_in_relay")
                later = [_place(w, me_idx, BF16, "place_" + k, after=(token,)) for k, w in (
                    ("w_conv_out", w_conv_out[0]), ("w_attn_out", w_attn_out[0]), ("w_o", w_o[0]),
                    ("w_gate_up", jnp.transpose(w_gate_up[0])), ("w_down", w_down[0]))]
                (sems_mix, sems_ffn), later, token_later = _gather_start(later, [[0, 1, 2], [3, 4]], "gather_start_later")
                self.state.update({"in": (near, relay, bufs), "mix": (sems_mix, later[:3]), "ffn": (sems_ffn, later[3:])})
                return (token_later,)
            (send_sems, recv_sems), group_bufs = self.state[group]
            send2, recv2, group_bufs, token = _gather_forward(send_sems, recv_sems, group_bufs, after, "gather_forward_" + group)
            self.state[group] = ((send2, recv2), group_bufs)
            return (token,)

        def end(self, group, after):
            if group == "in":
                near, relay, bufs = self.state[group]
                (last,), bufs, token = _gather_phase(bufs, [(*relay, 1, 1)], [_diagonal_to_sibling], after, "gather_in_last")
                _, full, _ = _gather_phase(bufs, [(*near, 2, 2), (*last, 1, 1)], [], (token,), "gather_in_done")
                return full[0], jnp.transpose(full[1].reshape(N_DEV, 8, 128)[:, :3, :], (1, 0, 2)).reshape(3, D)
            (send2, recv2), group_bufs = self.state[group]
            return _gather_done(send2, recv2, group_bufs, after, "gather_done_" + group)

    in_flight, own_pieces = {}, {}

    class Reducer:
        def start(self, group, gdict):
            keys, glist = list(gdict), list(gdict.values())
            send_sems, recv_sems, glist, lands, token = _exchange_start(glist, N_DEV, _to_sibling, "rs_sibling_start_" + group)
            in_flight[group] = (keys, send_sems, recv_sems, glist, lands)
            return (token,)

        def middle(self, group, after):
            keys, send_sems, recv_sems, glist, lands = in_flight[group]
            glist, lands = _exchange_wait(send_sems, recv_sems, glist, lands, after, "rs_sibling_wait_" + group)
            parts = [_chip_partial(g, r, partial_idx, "chip_partial_" + k) for k, g, r in zip(keys, glist, lands)]
            send_sems, recv_sems, parts, from_chips, token = _exchange_start(parts, 4, _to_chips, "rs_chips_start_" + group)
            in_flight[group] = (keys, send_sems, recv_sems, parts, from_chips)
            own_pieces[group] = (glist, lands)
            return (token,)

    dx, _, small = _local_step(x[0], loss_target[0], g_mix, g_ffn, g_final[None], attn_sinks, Gathered(),
                               reducer=Reducer(), after=gather_tokens)

    transposed = ("w_in", "w_gate_up")

    def as2d(k, a):
        if k in transposed:
            return jnp.transpose(a[0])
        return a[None] if a.ndim == 1 else (a[0] if a.ndim == 3 else a)

    w_all = {"g_mix": g_mix, "w_in": w_in, "conv_w": conv_w, "attn_sinks": attn_sinks, "w_conv_out": w_conv_out,
             "w_attn_out": w_attn_out, "w_o": w_o, "g_ffn": g_ffn, "w_gate_up": w_gate_up, "w_down": w_down, "g_final": g_final}
    m_all = {"g_mix": m_g_mix, "w_in": m_w_in, "conv_w": m_conv_w, "attn_sinks": m_attn_sinks, "w_conv_out": m_w_conv_out,
             "w_attn_out": m_w_attn_out, "w_o": m_w_o, "g_ffn": m_g_ffn, "w_gate_up": m_w_gate_up, "w_down": m_w_down,
             "g_final": m_g_final}
    v_all = {"g_mix": v_g_mix, "w_in": v_w_in, "conv_w": v_conv_w, "attn_sinks": v_attn_sinks, "w_conv_out": v_w_conv_out,
             "w_attn_out": v_w_attn_out, "w_o": v_w_o, "g_ffn": v_g_ffn, "w_gate_up": v_w_gate_up, "w_down": v_w_down,
             "g_final": v_g_final}
    results = {}

    def update(k, g=None, pieces=None):
        w2, m2, v2 = as2d(k, w_all[k]), as2d(k, m_all[k]), as2d(k, v_all[k])
        if g is None:
            g, d, nm, nv = _reduce_adamw(w2, *pieces, own_idx, m2, v2, "adamw_" + k)
        else:
            d, nm, nv = _adamw(w2, g, m2, v2, "adamw_" + k)
        results[k] = [(jnp.transpose(val) if k in transposed else val).reshape(w_all[k].shape) for val in (g, d, nm, nv)]
        return nm

    kernel_name = {"win_t": "w_in", "wgu_t": "w_gate_up", "wd": "w_down", "wco": "w_conv_out", "wao": "w_attn_out", "wo": "w_o"}

    def finish(group, after):
        keys, send_sems, recv_sems, parts, from_chips = in_flight[group]
        _, from_chips = _exchange_wait(send_sems, recv_sems, parts, from_chips, after, "rs_chips_wait_" + group)
        grads, from_sibling = own_pieces[group]
        return tuple(update(kernel_name[k], pieces=p) for k, *p in zip(keys, grads, from_sibling, from_chips))

    after = finish("mix", finish("ffn", (dx,)))

    sinks_row = jnp.pad(small["sinks"], ((0, 0), (0, D - 128)))
    pack = jnp.concatenate([small["g_mix"], small["g_ffn"], small["g_final"], small["conv_w"], sinks_row, small["lossvec"]], axis=0)
    tot, loss_row = _small_all_reduce(pack, "small_all_reduce", after=after)
    loss = loss_row[0, 0]
    g_small = {
        "g_mix": tot[0:1], "g_ffn": tot[1:2], "g_final": tot[2:3],
        "conv_w": lax.dynamic_slice(tot, (3, me * 128), (3, 128)), "attn_sinks": tot[6:7, :N_HEADS],
    }
    finish("in", tuple(update(k, g) for k, g in g_small.items()))

    order = ["g_mix", "w_in", "conv_w", "attn_sinks", "w_conv_out", "w_attn_out", "w_o", "g_ffn", "w_gate_up", "w_down", "g_final"]
    return (loss, dx[None], *[results[k][i] for i in range(4) for k in order])
```

```python
import functools
import math

import jax
import jax.numpy as jnp
from jax import lax
from jax.experimental import pallas as pl
from jax.experimental.pallas import tpu as pltpu

F32 = jnp.float32
BF16 = jnp.bfloat16

D = 1024
HEAD_DIM = 64
N_HEADS = 16
N_KV = 4
GROUP = N_HEADS // N_KV
D_KV = N_KV * HEAD_DIM
BLOCK = 128
ROT_DIM = HEAD_DIM // 4
ROPE_THETA = 500000.0
ATTN_SCALE = 1.0 / math.sqrt(HEAD_DIM)
NEG_INF = -1e30
D_FF = 2816
N_IN = 6656
EPS = 1e-5
C_CB, C_CC, C_CX, C_Q, C_K, C_V, C_GC, C_GA = 0, 1024, 2048, 3072, 4096, 4352, 4608, 5632

LR, B1, B2, EPS_ADAM, WD, STEP = 0.001, 0.9, 0.999, 1e-08, 0.01, 10

N_DEV = 8
MESH = pl.DeviceIdType.MESH
VMEM_LIMIT = 56 * 1024 * 1024

NN = (((1,), (0,)), ((), ()))
NT = (((1,), (1,)), ((), ()))
TN = (((0,), (0,)), ((), ()))
HBM_SPEC = pl.BlockSpec(memory_space=pl.ANY)
ROW_SPLIT = 4


def _call(body, **kw):
    return pl.pallas_call(body, **kw)


def _params(*sem):
    return pltpu.CompilerParams(dimension_semantics=sem, vmem_limit_bytes=VMEM_LIMIT)


def _sds(shape, dtype):
    return jax.ShapeDtypeStruct(shape, dtype)


def _matmul(a, b, *, mode, tm, tn, tk, out_dtype, name, res=None, after=()):
    parts = list(a) if isinstance(a, (list, tuple)) else [a]
    rows_a = parts[0].shape[0]
    cols_a = sum(p.shape[1] for p in parts)
    if mode == "nn":
        (m, kk), (_, n), dims = (rows_a, cols_a), b.shape, NN
    elif mode == "nt":
        (m, kk), (n, _), dims = (rows_a, cols_a), b.shape, NT
    else:
        (kk, m), (_, n), dims = (rows_a, cols_a), b.shape, TN
    tm, tn, tk = min(tm, m), min(tn, n), min(tk, kk)
    assert m % tm == 0 and n % tn == 0 and kk % tk == 0, (name, m, n, kk, tm, tn, tk)
    nk = kk // tk
    split_axis, width = (2, tk) if mode == "nn" else (0, tm)
    assert len(parts) == 1 or mode in ("nn", "tn")
    assert len(parts) == 1 or all(p.shape[1] % width == 0 for p in parts), (name, width)
    counts = [p.shape[1] // width for p in parts]
    starts = [sum(counts[:p]) for p in range(len(parts))]

    def a_spec(p):
        def col(t):
            return jnp.clip(t - starts[p], 0, counts[p] - 1) if len(parts) > 1 else t

        if mode == "tn":
            return pl.BlockSpec((tk, tm), lambda i, j, k: (k, col(i)))
        return pl.BlockSpec((tm, tk), lambda i, j, k: (i, col(k)))

    if mode == "nt":
        b_spec = pl.BlockSpec((tn, tk), lambda i, j, k: (j, k))
    else:
        b_spec = pl.BlockSpec((tk, tn), lambda i, j, k: (k, j))
    o_spec = pl.BlockSpec((tm, tn), lambda i, j, k: (i, j))
    has_res = res is not None
    n_parts = len(parts)
    unit = 128 if mode == "tn" else 16
    split = ROW_SPLIT if tm % (ROW_SPLIT * unit) == 0 else 1

    def body(*refs):
        a_refs, b_ref = refs[:n_parts], refs[n_parts]
        r_ref = refs[n_parts + 1] if has_res else None
        o_ref = refs[n_parts + 1 + has_res + len(after)]
        k = pl.program_id(2)

        acc_ref = refs[-1] if nk > 1 else None

        def step(a_ref):
            def matmul(rows):
                a_blk = a_ref[:, rows] if mode == "tn" else a_ref[rows, :]
                return lax.dot_general(a_blk, b_ref[...], dims, preferred_element_type=F32)

            def finish(rows, part):
                if nk > 1:
                    acc_ref[rows, :] += part
                else:
                    o_ref[rows, :] = (part + r_ref[rows, :] if has_res else part).astype(o_ref.dtype)

            _row_pipeline(tm, matmul, finish, split)

        if nk > 1:
            @pl.when(k == 0)
            def _():
                acc_ref[...] = jnp.zeros_like(acc_ref)

        if n_parts == 1:
            step(a_refs[0])
        else:
            t = pl.program_id(split_axis)
            for p in range(n_parts):
                pl.when((t >= starts[p]) & (t < starts[p] + counts[p]))(functools.partial(step, a_refs[p]))

        if nk > 1:
            @pl.when(k == nk - 1)
            def _():
                o_ref[...] = (acc_ref[...] + r_ref[...] if has_res else acc_ref[...]).astype(o_ref.dtype)

    ins = parts + [b] + ([res] if has_res else []) + list(after)
    in_specs = [a_spec(p) for p in range(n_parts)] + [b_spec] + ([o_spec] if has_res else []) + [HBM_SPEC] * len(after)
    scratch = [] if nk == 1 else [pltpu.VMEM((tm, tn), F32)]
    return _call(
        body, name=name, grid=(m // tm, n // tn, nk), in_specs=in_specs, out_specs=o_spec,
        out_shape=_sds((m, n), out_dtype), scratch_shapes=scratch,
        compiler_params=_params("parallel", "parallel", "arbitrary"),
    )(*ins)


def _row_tile(s):
    return min(512, s)


def _rms_fwd(x, g, name, after=()):
    s = x.shape[0]
    tm = _row_tile(s)

    def body(x_ref, g_ref, *rest):
        h_ref = rest[-1]
        xv = x_ref[...]
        r = lax.rsqrt(jnp.mean(xv * xv, axis=-1, keepdims=True) + EPS)
        h_ref[...] = (xv * r * g_ref[...]).astype(BF16)

    row = pl.BlockSpec((tm, D), lambda i: (i, 0))
    return _call(
        body, name=name, grid=(s // tm,), in_specs=[row, pl.BlockSpec((1, D), lambda i: (0, 0))] + [HBM_SPEC] * len(after),
        out_specs=row, out_shape=_sds((s, D), BF16), compiler_params=_params("parallel"),
    )(x, g, *after)


def _rms_bwd(dh, x, g, dres, name, after=()):
    s = x.shape[0]
    tm = _row_tile(s)

    def body(dh_ref, x_ref, g_ref, dres_ref, *rest):
        dx_ref, dxb_ref, dg_ref = rest[len(after):]
        xv = x_ref[...]
        r = lax.rsqrt(jnp.mean(xv * xv, axis=-1, keepdims=True) + EPS)
        xh = xv * r
        dhv = dh_ref[...]
        dyg = dhv * g_ref[...]
        dx = dres_ref[...] + r * (dyg - xh * jnp.mean(dyg * xh, axis=-1, keepdims=True))
        dx_ref[...] = dx
        dxb_ref[...] = dx.astype(BF16)
        part = jnp.sum(dhv * xh, axis=0, keepdims=True)

        @pl.when(pl.program_id(0) == 0)
        def _():
            dg_ref[...] = part

        @pl.when(pl.program_id(0) > 0)
        def _():
            dg_ref[...] += part

    row = pl.BlockSpec((tm, D), lambda i: (i, 0))
    vec = pl.BlockSpec((1, D), lambda i: (0, 0))
    return _call(
        body, name=name, grid=(s // tm,), in_specs=[row, row, vec, row] + [HBM_SPEC] * len(after), out_specs=[row, row, vec],
        out_shape=[_sds((s, D), F32), _sds((s, D), BF16), _sds((1, D), F32)],
        compiler_params=_params("arbitrary"),
    )(dh, x, g, dres, *after)


def _loss_head(x2, g, tgt, name):
    s = x2.shape[0]
    tm = _row_tile(s)

    def body(x_ref, g_ref, t_ref, dx_ref, dxb_ref, dg_ref, l_ref):
        xv = x_ref[...]
        gv = g_ref[...]
        r = lax.rsqrt(jnp.mean(xv * xv, axis=-1, keepdims=True) + EPS)
        xh = xv * r
        err = xh * gv - t_ref[...]
        dy = err * (1.0 / D)
        dyg = dy * gv
        dx = r * (dyg - xh * jnp.mean(dyg * xh, axis=-1, keepdims=True))
        dx_ref[...] = dx
        dxb_ref[...] = dx.astype(BF16)
        dg_part = jnp.sum(dy * xh, axis=0, keepdims=True)
        l_part = jnp.sum(err * err, axis=0, keepdims=True)

        @pl.when(pl.program_id(0) == 0)
        def _():
            dg_ref[...] = dg_part
            l_ref[...] = l_part

        @pl.when(pl.program_id(0) > 0)
        def _():
            dg_ref[...] += dg_part
            l_ref[...] += l_part

    row = pl.BlockSpec((tm, D), lambda i: (i, 0))
    vec = pl.BlockSpec((1, D), lambda i: (0, 0))
    return _call(
        body, name=name, grid=(s // tm,), in_specs=[row, vec, row], out_specs=[row, row, vec, vec],
        out_shape=[_sds((s, D), F32), _sds((s, D), BF16), _sds((1, D), F32), _sds((1, D), F32)],
        compiler_params=_params("arbitrary"),
    )(x2, g, tgt)


CONV_TC = 256


def _shift_down(u, k, rows):
    return jnp.where(rows >= k, pltpu.roll(u, k, 0), 0.0)


def _shift_up(u, k, rows, s):
    return jnp.where(rows < s - k, pltpu.roll(u, s - k, 0), 0.0)


def _conv_specs(s):
    nb = D // CONV_TC

    def col(c0):
        return pl.BlockSpec((s, CONV_TC), lambda j, c0=c0: (0, c0 // CONV_TC + j))

    return nb, col


def _conv_fwd(proj, conv_w, name):
    s = proj.shape[0]
    nb, col = _conv_specs(s)

    def body(cb_ref, cc_ref, cx_ref, w_ref, y_ref):
        rows = lax.broadcasted_iota(jnp.int32, (s, CONV_TC), 0)
        u = cc_ref[...].astype(F32) * cx_ref[...].astype(F32)
        w = w_ref[...]
        c = w[0:1] * _shift_down(u, 2, rows) + w[1:2] * _shift_down(u, 1, rows) + w[2:3] * u
        y_ref[...] = (cb_ref[...].astype(F32) * c).astype(BF16)

    return _call(
        body, name=name, grid=(nb,),
        in_specs=[col(C_CB), col(C_CC), col(C_CX), pl.BlockSpec((3, CONV_TC), lambda j: (0, j))],
        out_specs=pl.BlockSpec((s, CONV_TC), lambda j: (0, j)), out_shape=_sds((s, D), BF16),
        compiler_params=_params("parallel"),
    )(proj, proj, proj, conv_w)


def _write_behind(t, nt, buf, sems, tiles, window, where):
    slot = t % 2

    def copies(sl, at):
        return [pltpu.make_async_copy(buf.at[sl, p], window(p, at), sems.at[sl, p]) for p in range(len(tiles))]

    @pl.when(t >= 2)
    def _():
        for cp in copies(slot, where):
            cp.wait()

    for p, tile in enumerate(tiles):
        buf[slot, p] = tile
    started = copies(slot, where)
    for cp in started:
        cp.start()

    @pl.when(t == nt - 1)
    def _():
        for cp in started:
            cp.wait()
        if nt > 1:
            for cp in copies(1 - slot, where):
                cp.wait()


def _conv_bwd(dy, proj, conv_w, dproj, name, after=()):
    s = proj.shape[0]
    nb, col = _conv_specs(s)

    def body(dy_ref, cb_ref, cc_ref, cx_ref, w_ref, *rest):
        dproj_ref, dw_ref, buf, sems = rest[1 + len(after):]
        j = pl.program_id(0)
        rows = lax.broadcasted_iota(jnp.int32, (s, CONV_TC), 0)
        cc = cc_ref[...].astype(F32)
        cx = cx_ref[...].astype(F32)
        u = cc * cx
        u1 = _shift_down(u, 1, rows)
        u2 = _shift_down(u, 2, rows)
        w = w_ref[...]
        c = w[0:1] * u2 + w[1:2] * u1 + w[2:3] * u
        dyv = dy_ref[...].astype(F32)
        dc = dyv * cb_ref[...].astype(F32)
        du = w[2:3] * dc + w[1:2] * _shift_up(dc, 1, rows, s) + w[0:1] * _shift_up(dc, 2, rows, s)

        def window(p, jj):
            start = pl.multiple_of((C_CB, C_CC, C_CX)[p] + jj * CONV_TC, CONV_TC)
            return dproj_ref.at[:, pl.ds(start, CONV_TC)]

        tiles = ((dyv * c).astype(BF16), (du * cx).astype(BF16), (du * cc).astype(BF16))
        _write_behind(j * 0, 1, buf, sems, tiles, window, j)
        dw_ref[...] = jnp.concatenate(
            [jnp.sum(dc * u2, axis=0, keepdims=True), jnp.sum(dc * u1, axis=0, keepdims=True),
             jnp.sum(dc * u, axis=0, keepdims=True)], axis=0)

    return _call(
        body, name=name, grid=(nb,),
        in_specs=[pl.BlockSpec((s, CONV_TC), lambda j: (0, j)), col(C_CB), col(C_CC), col(C_CX),
                  pl.BlockSpec((3, CONV_TC), lambda j: (0, j))] + [HBM_SPEC] * (1 + len(after)),
        out_specs=[pl.BlockSpec(memory_space=pl.ANY), pl.BlockSpec((3, CONV_TC), lambda j: (0, j))],
        out_shape=[_sds((s, N_IN), BF16), _sds((3, D), F32)],
        scratch_shapes=[pltpu.VMEM((1, 3, s, CONV_TC), BF16), pltpu.SemaphoreType.DMA((1, 3))],
        input_output_aliases={5: 0}, compiler_params=_params("arbitrary"),
    )(dy, proj, proj, proj, conv_w, dproj, *after)


def _rope_tables(s):
    half = ROT_DIM // 2
    inv_freq = ROPE_THETA ** (-jnp.arange(0, ROT_DIM, 2, dtype=F32) / ROT_DIM)
    inv64 = jnp.concatenate([inv_freq, inv_freq, jnp.zeros((HEAD_DIM - ROT_DIM,), F32)])
    ang = jnp.arange(s, dtype=F32)[:, None] * jnp.concatenate([inv64, inv64])[None, :]
    d = lax.broadcasted_iota(jnp.int32, (s, 128), 1) % HEAD_DIM
    cos, sin = jnp.cos(ang), jnp.sin(ang)
    c = jnp.where(d < ROT_DIM, cos, 1.0)
    a = jnp.where(d < half, -sin, 0.0)
    b = jnp.where((d >= half) & (d < ROT_DIM), sin, 0.0)
    return jnp.concatenate([c, a, b], axis=1)


def _rope(x, tab):
    c, a, b = tab[:, 0:128], tab[:, 128:256], tab[:, 256:384]
    outs = []
    for i in range(x.shape[1] // 128):
        xc = x[:, i * 128:(i + 1) * 128]
        outs.append(xc * c + pltpu.roll(xc, 120, 1) * a + pltpu.roll(xc, 8, 1) * b)
    return outs[0] if len(outs) == 1 else jnp.concatenate(outs, axis=1)


def _rope_t(dx, tab):
    c, a, b = tab[:, 0:128], tab[:, 128:256], tab[:, 256:384]
    outs = []
    for i in range(dx.shape[1] // 128):
        dc = dx[:, i * 128:(i + 1) * 128]
        outs.append(dc * c + pltpu.roll(dc * a, 8, 1) + pltpu.roll(dc * b, 120, 1))
    return outs[0] if len(outs) == 1 else jnp.concatenate(outs, axis=1)


def _attn_in_specs():
    prev = lambda n: jnp.maximum(n - 1, 0)
    return [
        pl.BlockSpec((BLOCK, D), lambda n: (n, C_Q // D)),
        pl.BlockSpec((BLOCK, D_KV), lambda n: (n, C_K // D_KV)),
        pl.BlockSpec((BLOCK, D_KV), lambda n: (prev(n), C_K // D_KV)),
        pl.BlockSpec((BLOCK, D_KV), lambda n: (n, C_V // D_KV)),
        pl.BlockSpec((BLOCK, D_KV), lambda n: (prev(n), C_V // D_KV)),
        pl.BlockSpec((BLOCK, 384), lambda n: (n, 0)),
        pl.BlockSpec((BLOCK, 384), lambda n: (prev(n), 0)),
        pl.BlockSpec(memory_space=pltpu.SMEM),
    ]


HALF = HEAD_DIM
N_CHUNK = D // 128


def _swa_bias(n):
    qi = lax.broadcasted_iota(jnp.int32, (BLOCK, 2 * BLOCK), 0)
    kj = lax.broadcasted_iota(jnp.int32, (BLOCK, 2 * BLOCK), 1)
    rel = qi + BLOCK - kj
    valid = (rel >= 0) & (rel < BLOCK) & ((kj >= BLOCK) | (n > 0))
    return jnp.where(valid, 0.0, NEG_INF)


def _halves(x):
    lo = lax.broadcasted_iota(jnp.int32, x.shape, 1) < HALF
    return jnp.where(lo, x, 0.0).astype(BF16), jnp.where(lo, 0.0, x).astype(BF16)


def _dup_heads(x):
    out = []
    for pair in range(N_KV // 2):
        xc = x[:, pair * 128:(pair + 1) * 128]
        xr = pltpu.roll(xc, HALF, 1)
        lo = lax.broadcasted_iota(jnp.int32, xc.shape, 1) < HALF
        out += [jnp.where(lo, xc, xr), jnp.where(lo, xr, xc)]
    return out


def _swa_load(q_ref, kc_ref, kp_ref, vc_ref, vp_ref, tc_ref, tp_ref):
    qf = _rope(q_ref[...].astype(F32), tc_ref[...]) * ATTN_SCALE
    q_halves = [_halves(qf[:, c * 128:(c + 1) * 128]) for c in range(N_CHUNK)]
    kf = jnp.concatenate([_rope(kp_ref[...].astype(F32), tp_ref[...]), _rope(kc_ref[...].astype(F32), tc_ref[...])], axis=0)
    vf = jnp.concatenate([vp_ref[...], vc_ref[...]], axis=0).astype(F32)
    return q_halves, _dup_heads(kf), _dup_heads(vf)


def _swa_probs(qh, kk, bias, sink):
    s = lax.dot_general(qh, kk, NT, preferred_element_type=F32) + bias
    m = jnp.maximum(jnp.max(jnp.maximum(s[:, :BLOCK], s[:, BLOCK:]), axis=1, keepdims=True), sink)
    return jnp.exp(s - m), m


def _swa_fwd(proj, tab, sinks, name, after=()):
    s = proj.shape[0]

    def body(q_ref, kc_ref, kp_ref, vc_ref, vp_ref, tc_ref, tp_ref, sink_ref, *rest):
        o_ref = rest[-1]
        n = pl.program_id(0)
        q_halves, kdup, vdup = _swa_load(q_ref, kc_ref, kp_ref, vc_ref, vp_ref, tc_ref, tp_ref)
        bias = _swa_bias(n)
        ones = jnp.ones((2 * BLOCK, 128), BF16)
        kk = [k.astype(BF16) for k in kdup]
        vv = [[jnp.concatenate([v_half, ones], axis=1) for v_half in _halves(v)] for v in vdup]
        heads = [(c, half) for c in range(N_CHUNK) for half in range(2)]
        scores = [lax.dot_general(q_halves[c][half], kk[c // (GROUP // 2)], NT, preferred_element_type=F32)
                  for c, half in heads]
        probs = []
        for (c, half), sc in zip(heads, scores):
            sc = sc + bias
            m = jnp.maximum(jnp.max(jnp.maximum(sc[:, :BLOCK], sc[:, BLOCK:]), axis=1, keepdims=True), sink_ref[0, 2 * c + half])
            probs.append((jnp.exp(sc - m).astype(BF16), jnp.exp(sink_ref[0, 2 * c + half] - m)))
        outs = [lax.dot_general(e, vv[c // (GROUP // 2)][half], NN, preferred_element_type=F32)
                for (c, half), (e, _) in zip(heads, probs)]
        for c in range(N_CHUNK):
            parts = [outs[2 * c + half][:, :128] * (1.0 / (outs[2 * c + half][:, 128:] + probs[2 * c + half][1]))
                     for half in range(2)]
            o_ref[:, c * 128:(c + 1) * 128] = (parts[0] + parts[1]).astype(BF16)

    return _call(
        body, name=name, grid=(s // BLOCK,), in_specs=_attn_in_specs() + [HBM_SPEC] * len(after),
        out_specs=pl.BlockSpec((BLOCK, D), lambda n: (n, 0)), out_shape=_sds((s, D), BF16),
        compiler_params=_params("parallel"),
    )(proj, proj, proj, proj, proj, tab, tab, sinks, *after)


def _swa_bwd(do, proj, tab, sinks, dproj, name, after=()):
    s = proj.shape[0]
    nblk = s // BLOCK
    kv_of = lambda c: c // (GROUP // 2)

    def body(do_ref, q_ref, kc_ref, kp_ref, vc_ref, vp_ref, tc_ref, tp_ref, sink_ref, *rest):
        dproj_ref, dk_ref, dv_ref, ds_ref, dqout, dkbuf, dvbuf, sems = rest[1 + len(after):]
        n = pl.program_id(0)

        @pl.when(n == 0)
        def _():
            dk_ref[...] = jnp.zeros_like(dk_ref)
            dv_ref[...] = jnp.zeros_like(dv_ref)
            ds_ref[...] = jnp.zeros_like(ds_ref)

        q_halves, kdup, vdup = _swa_load(q_ref, kc_ref, kp_ref, vc_ref, vp_ref, tc_ref, tp_ref)
        dof = do_ref[...].astype(F32)
        do_halves = [_halves(dof[:, c * 128:(c + 1) * 128]) for c in range(N_CHUNK)]
        bias = _swa_bias(n)
        ones = jnp.ones((2 * BLOCK, 128), BF16)
        kk = [k.astype(BF16) for k in kdup]
        vv = [v.astype(BF16) for v in vdup]
        k_halves = [_halves(k) for k in kdup]
        heads = [(c, half) for c in range(N_CHUNK) for half in range(2)]
        lane_row = lax.broadcasted_iota(jnp.int32, (1, 128), 1)
        lo_kv = lax.broadcasted_iota(jnp.int32, (2 * BLOCK, 128), 1) < HALF
        scores = [lax.dot_general(q_halves[c][half], kk[kv_of(c)], NT, preferred_element_type=F32) for c, half in heads]
        dps = [lax.dot_general(do_halves[c][half], vv[kv_of(c)], NT, preferred_element_type=F32) for c, half in heads]
        exps = []
        for (c, half), sc in zip(heads, scores):
            sink = sink_ref[0, 2 * c + half]
            sc = sc + bias
            m = jnp.maximum(jnp.max(jnp.maximum(sc[:, :BLOCK], sc[:, BLOCK:]), axis=1, keepdims=True), sink)
            exps.append((jnp.exp(sc - m), jnp.exp(sink - m)))
        sums = [lax.dot_general(e.astype(BF16), ones, NN, preferred_element_type=F32) for e, _ in exps]
        dsink_row = jnp.zeros((1, 128), F32)
        dsb, pb = [], []
        for h, ((e, es), row_sum, dp) in enumerate(zip(exps, sums, dps)):
            inv = 1.0 / (row_sum + es)
            p = e * jnp.concatenate([inv, inv], axis=1)
            t = p * dp
            delta = jnp.sum(t, axis=1, keepdims=True)
            dsb.append((t - p * delta).astype(BF16))
            pb.append(p.astype(BF16))
            dsink = -jnp.sum(es * inv * delta, axis=0, keepdims=True)
            dsink_row = dsink_row + jnp.where(lane_row == h, dsink, 0.0)
        dq_parts = [lax.dot_general(d, k_halves[kv_of(c)][half], NN, preferred_element_type=F32) for (c, half), d in zip(heads, dsb)]
        dk_parts = [lax.dot_general(d, q_halves[c][half], TN, preferred_element_type=F32) for (c, half), d in zip(heads, dsb)]
        dv_parts = [lax.dot_general(p, do_halves[c][half], TN, preferred_element_type=F32) for (c, half), p in zip(heads, pb)]
        dq = jnp.concatenate([(dq_parts[2 * c] + dq_parts[2 * c + 1]) * ATTN_SCALE for c in range(N_CHUNK)], axis=1)

        def kv_sum(parts, hk):
            acc = (parts[GROUP * hk] + parts[GROUP * hk + 1]) + (parts[GROUP * hk + 2] + parts[GROUP * hk + 3])
            return acc + pltpu.roll(acc, HALF, 1)

        for pair in range(N_KV // 2):
            dkbuf[:, pair * 128:(pair + 1) * 128] = jnp.where(lo_kv, kv_sum(dk_parts, 2 * pair), kv_sum(dk_parts, 2 * pair + 1))
            dvbuf[:, pair * 128:(pair + 1) * 128] = jnp.where(lo_kv, kv_sum(dv_parts, 2 * pair), kv_sum(dv_parts, 2 * pair + 1))
        prev0 = pl.multiple_of(jnp.maximum(n - 1, 0) * BLOCK, BLOCK)
        cur0 = pl.multiple_of(n * BLOCK, BLOCK)

        @pl.when(n > 0)
        def _():
            dk_ref[pl.ds(prev0, BLOCK), :] += dkbuf[0:BLOCK, :]
            dv_ref[pl.ds(prev0, BLOCK), :] += dvbuf[0:BLOCK, :]

        dk_ref[pl.ds(cur0, BLOCK), :] += dkbuf[BLOCK:2 * BLOCK, :]
        dv_ref[pl.ds(cur0, BLOCK), :] += dvbuf[BLOCK:2 * BLOCK, :]
        ds_ref[...] += dsink_row

        def window(p, at):
            return dproj_ref.at[pl.ds(pl.multiple_of(at * BLOCK, BLOCK), BLOCK), pl.ds(C_Q, D)]

        _write_behind(n, nblk, dqout, sems, (_rope_t(dq, tc_ref[...]).astype(BF16),), window, n)

    blk = lambda w: pl.BlockSpec((BLOCK, w), lambda n: (n, 0))
    whole = lambda w: pl.BlockSpec((s, w), lambda n: (0, 0))
    n_in = 1 + len(_attn_in_specs())
    return _call(
        body, name=name, grid=(nblk,), in_specs=[blk(D)] + _attn_in_specs() + [HBM_SPEC] * (1 + len(after)),
        out_specs=[HBM_SPEC, whole(D_KV), whole(D_KV), pl.BlockSpec((1, 128), lambda n: (0, 0))],
        out_shape=[_sds((s, N_IN), BF16), _sds((s, D_KV), F32), _sds((s, D_KV), F32), _sds((1, 128), F32)],
        scratch_shapes=[pltpu.VMEM((2, 1, BLOCK, D), BF16), pltpu.VMEM((2 * BLOCK, D_KV), F32),
                        pltpu.VMEM((2 * BLOCK, D_KV), F32), pltpu.SemaphoreType.DMA((2, 1))],
        input_output_aliases={n_in: 0}, compiler_params=_params("arbitrary"),
    )(do, proj, proj, proj, proj, proj, tab, tab, sinks, dproj, *after)


def _kv_bwd(dkr, dv, tab, dproj, name):
    s = dkr.shape[0]
    tm = _row_tile(s)

    def body(dk_ref, dv_ref, t_ref, dproj_in, o_ref):
        del dproj_in
        o_ref[:, 0:D_KV] = _rope_t(dk_ref[...], t_ref[...]).astype(BF16)
        o_ref[:, D_KV:2 * D_KV] = dv_ref[...].astype(BF16)

    row = lambda w: pl.BlockSpec((tm, w), lambda i: (i, 0))
    return _call(
        body, name=name, grid=(s // tm,),
        in_specs=[row(D_KV), row(D_KV), row(384), pl.BlockSpec(memory_space=pl.ANY)],
        out_specs=pl.BlockSpec((tm, 2 * D_KV), lambda i: (i, C_K // (2 * D_KV))),
        out_shape=_sds((s, N_IN), BF16), input_output_aliases={3: 0}, compiler_params=_params("parallel"),
    )(dkr, dv, tab, dproj)


EW_TC = 512


def _sigmoid(x):
    return 0.5 * jnp.tanh(0.5 * x) + 0.5


def _merge_fwd(proj, conv_out, attn_out, name):
    s = proj.shape[0]
    tm = _row_tile(s)
    tile = pl.BlockSpec((tm, EW_TC), lambda i, j: (i, j))

    def body(gc_ref, ga_ref, co_ref, ao_ref, o_ref):
        o_ref[...] = (_sigmoid(gc_ref[...].astype(F32)) * co_ref[...].astype(F32)
                      + _sigmoid(ga_ref[...].astype(F32)) * ao_ref[...].astype(F32)).astype(BF16)

    return _call(
        body, name=name, grid=(s // tm, D // EW_TC),
        in_specs=[pl.BlockSpec((tm, EW_TC), lambda i, j: (i, C_GC // EW_TC + j)),
                  pl.BlockSpec((tm, EW_TC), lambda i, j: (i, C_GA // EW_TC + j)), tile, tile],
        out_specs=tile, out_shape=_sds((s, D), BF16), compiler_params=_params("parallel", "parallel"),
    )(proj, proj, conv_out, attn_out)


def _merge_bwd(dmerged, proj, conv_out, attn_out, name):
    s = proj.shape[0]
    tm = _row_tile(s)
    tile = pl.BlockSpec((tm, EW_TC), lambda i, j: (i, j))
    anyspec = pl.BlockSpec(memory_space=pl.ANY)

    def body(dm_ref, gc_ref, ga_ref, co_ref, ao_ref, dproj_ref, dco_ref, dao_ref, buf, sems):
        i, j = pl.program_id(0), pl.program_id(1)
        dm = dm_ref[...].astype(F32)
        sc = _sigmoid(gc_ref[...].astype(F32))
        sa = _sigmoid(ga_ref[...].astype(F32))
        dco_ref[...] = (dm * sc).astype(BF16)
        dao_ref[...] = (dm * sa).astype(BF16)
        tiles = ((dm * co_ref[...].astype(F32) * sc * (1.0 - sc)).astype(BF16),
                 (dm * ao_ref[...].astype(F32) * sa * (1.0 - sa)).astype(BF16))

        def window(p, at):
            start = pl.multiple_of((C_GC, C_GA)[p] + at[1] * EW_TC, EW_TC)
            return dproj_ref.at[pl.ds(pl.multiple_of(at[0] * tm, tm), tm), pl.ds(start, EW_TC)]

        _write_behind(i * nj + j, (s // tm) * nj, buf, sems, tiles, window, (i, j))

    nj = D // EW_TC
    return _call(
        body, name=name, grid=(s // tm, nj),
        in_specs=[tile, pl.BlockSpec((tm, EW_TC), lambda i, j: (i, C_GC // EW_TC + j)),
                  pl.BlockSpec((tm, EW_TC), lambda i, j: (i, C_GA // EW_TC + j)), tile, tile],
        out_specs=[anyspec, tile, tile],
        out_shape=[_sds((s, N_IN), BF16), _sds((s, D), BF16), _sds((s, D), BF16)],
        scratch_shapes=[pltpu.VMEM((2, 2, tm, EW_TC), BF16), pltpu.SemaphoreType.DMA((2, 2))],
        compiler_params=_params("arbitrary", "arbitrary"),
    )(dmerged, proj, proj, conv_out, attn_out)


FF_TC = 256


def _row_pipeline(tm, matmul, finish, split=ROW_SPLIT):
    step = tm // split
    pending = None
    for r in range(split):
        rows = pl.ds(r * step, step)
        result = matmul(rows)
        if pending is not None:
            finish(*pending)
        pending = (rows, result)
    finish(*pending)


def _gate_up_fwd(h2, wgu_t, name):
    s = h2.shape[0]
    tm = min(2048, s)
    nb = D_FF // FF_TC

    def body(h_ref, wg_ref, wu_ref, a_ref, dadu_ref, dadg_ref):
        def matmuls(rows):
            h = h_ref[rows, :]
            return (lax.dot_general(h, wg_ref[...], NT, preferred_element_type=F32),
                    lax.dot_general(h, wu_ref[...], NT, preferred_element_type=F32))

        def finish(rows, gu):
            g, u = gu
            sg = _sigmoid(g)
            silu = g * sg
            a_ref[rows, :] = (silu * u).astype(BF16)
            dadu_ref[rows, :] = silu.astype(BF16)
            dadg_ref[rows, :] = (u * (sg * (1.0 + g * (1.0 - sg)))).astype(BF16)

        _row_pipeline(tm, matmuls, finish)

    tile = pl.BlockSpec((tm, FF_TC), lambda i, j: (i, j))
    return _call(
        body, name=name, grid=(s // tm, nb),
        in_specs=[pl.BlockSpec((tm, D), lambda i, j: (i, 0)), pl.BlockSpec((FF_TC, D), lambda i, j: (j, 0)),
                  pl.BlockSpec((FF_TC, D), lambda i, j: (nb + j, 0))],
        out_specs=[tile, tile, tile], out_shape=[_sds((s, D_FF), BF16)] * 3,
        compiler_params=_params("parallel", "parallel"),
    )(h2, wgu_t, wgu_t)


def _down_bwd_x(dx2b, wd, dadg, dadu, name):
    s = dx2b.shape[0]
    tm = min(2048, s)
    nb = D_FF // FF_TC

    def body(dx_ref, w_ref, dadg_ref, dadu_ref, dg_ref, du_ref):
        def matmul(rows):
            return lax.dot_general(dx_ref[rows, :], w_ref[...], NT, preferred_element_type=F32)

        def finish(rows, da):
            dg_ref[rows, :] = (da * dadg_ref[rows, :].astype(F32)).astype(BF16)
            du_ref[rows, :] = (da * dadu_ref[rows, :].astype(F32)).astype(BF16)

        _row_pipeline(tm, matmul, finish)

    tile = pl.BlockSpec((tm, FF_TC), lambda i, j: (i, j))
    return _call(
        body, name=name, grid=(s // tm, nb),
        in_specs=[pl.BlockSpec((tm, D), lambda i, j: (i, 0)), pl.BlockSpec((FF_TC, D), lambda i, j: (j, 0)), tile, tile],
        out_specs=[tile, tile], out_shape=[_sds((s, D_FF), BF16)] * 2,
        compiler_params=_params("parallel", "parallel"),
    )(dx2b, wd, dadg, dadu)


class _Weights:
    def __init__(self, **groups):
        self.groups = groups

    def begin(self, group, after):
        return ()

    def end(self, group, after):
        return self.groups[group]


class _NoReduce:
    def start(self, group, grads):
        return ()

    def middle(self, group, after):
        return ()


def _local_step(x, tgt, g_mix, g_ffn, g_final, sinks, weights, reducer=None, after=()):
    reducer = reducer or _NoReduce()
    s = x.shape[0]
    tab = _rope_tables(s)
    big = dict(tm=2048, tn=512, tk=1024)
    h1 = _rms_fwd(x, g_mix, "rms1_fwd", after=after)
    win_t, conv_w = weights.end("in", weights.begin("in", (h1,)))
    proj = _matmul(h1, win_t, mode="nt", out_dtype=BF16, name="proj_fwd", tm=2048, tn=512, tk=1024)
    attn = _swa_fwd(proj, tab, sinks, "attn_fwd", after=weights.begin("mix", (proj,)))
    wco, wao, wo = weights.end("mix", (attn,))
    conv_y = _conv_fwd(proj, conv_w, "conv_fwd")
    conv_out = _matmul(conv_y, wco, mode="nn", out_dtype=BF16, name="conv_out_fwd", **big)
    attn_out = _matmul(attn, wao, mode="nn", out_dtype=BF16, name="attn_out_fwd", **big)
    merged = _merge_fwd(proj, conv_out, attn_out, "merge_fwd")
    x1 = _matmul(merged, wo, mode="nn", out_dtype=F32, name="wo_fwd", res=x, after=weights.begin("ffn", (merged,)), **big)
    h2 = _rms_fwd(x1, g_ffn, "rms2_fwd")
    wgu_t, wd = weights.end("ffn", (h2,))
    act, dadu, dadg = _gate_up_fwd(h2, wgu_t, "gate_up_fwd")
    x2 = _matmul(act, wd, mode="nn", out_dtype=F32, name="down_fwd", res=x1, tm=1024, tn=512, tk=D_FF)
    dx2, dx2b, dg_final, lossvec = _loss_head(x2, g_final, tgt, "loss_head")
    dgate, dup = _down_bwd_x(dx2b, wd, dadg, dadu, "down_bwd_x")
    g_wd = _matmul(act, dx2b, mode="tn", out_dtype=BF16, name="down_bwd_w", tm=1408, tn=1024, tk=2048)
    dh2 = _matmul([dgate, dup], wgu_t, mode="nn", out_dtype=F32, name="gate_up_bwd_x", tm=1024, tn=1024, tk=1408)
    g_wgu_t = _matmul([dgate, dup], h2, mode="tn", out_dtype=BF16, name="gate_up_bwd_w", tm=1408, tn=1024, tk=2048)
    after_ffn = reducer.start("ffn", dict(wgu_t=g_wgu_t, wd=g_wd))
    dx1, dx1b, dg_ffn = _rms_bwd(dh2, x1, g_ffn, dx2, "rms2_bwd")
    dmerged = _matmul(dx1b, wo, mode="nt", out_dtype=BF16, name="wo_bwd_x", after=after_ffn, **big)
    after_ffn = reducer.middle("ffn", (dmerged,))
    g_wo = _matmul(merged, dx1b, mode="tn", out_dtype=BF16, name="wo_bwd_w", tm=512, tn=1024, tk=2048, after=after_ffn)
    dproj, dco, dao = _merge_bwd(dmerged, proj, conv_out, attn_out, "merge_bwd")
    dconv_y = _matmul(dco, wco, mode="nt", out_dtype=BF16, name="conv_out_bwd_x", **big)
    g_wco = _matmul(conv_y, dco, mode="tn", out_dtype=BF16, name="conv_out_bwd_w", tm=512, tn=1024, tk=2048)
    dattn = _matmul(dao, wao, mode="nt", out_dtype=BF16, name="attn_out_bwd_x", **big)
    g_wao = _matmul(attn, dao, mode="tn", out_dtype=BF16, name="attn_out_bwd_w", tm=512, tn=1024, tk=2048)
    after_mix = reducer.start("mix", dict(wco=g_wco, wao=g_wao, wo=g_wo))
    dproj, dconv_w = _conv_bwd(dconv_y, proj, conv_w, dproj, "conv_bwd", after=after_mix)
    after_mix = reducer.middle("mix", (dconv_w,))
    dproj, dkr, dv, dsinks = _swa_bwd(dattn, proj, tab, sinks, dproj, "attn_bwd", after=after_mix)
    dproj = _kv_bwd(dkr, dv, tab, dproj, "kv_bwd")
    g_win_t = _matmul(dproj, h1, mode="tn", out_dtype=BF16, name="proj_bwd_w", tm=512, tn=1024, tk=2048)
    after_in = reducer.middle("in", reducer.start("in", dict(win_t=g_win_t)))
    dh1 = _matmul(dproj, win_t, mode="nn", out_dtype=F32, name="proj_bwd_x", tm=1024, tn=1024, tk=1664, after=after_in)
    dx, _, dg_mix = _rms_bwd(dh1, x, g_mix, dx1, "rms1_bwd")
    grads = dict(win_t=g_win_t, wgu_t=g_wgu_t, wd=g_wd, wco=g_wco, wao=g_wao, wo=g_wo)
    small = dict(g_mix=dg_mix, g_ffn=dg_ffn, g_final=dg_final, conv_w=dconv_w, sinks=dsinks, lossvec=lossvec)
    return dx, grads, small


def _position():
    return lax.axis_index("x"), lax.axis_index("y"), lax.axis_index("c")


def _other_chips(x, y):
    return [(1 - x, y), (x, 1 - y), (1 - x, 1 - y)]


SEM_SPEC = pl.BlockSpec(memory_space=pltpu.SEMAPHORE)
EFFECT = pltpu.SideEffectType.DATAFLOW_SIDE_EFFECTING
TOKEN = jax.ShapeDtypeStruct((8, 128), F32)
TOKEN_SPEC = pl.BlockSpec(memory_space=pltpu.VMEM)


def _hbm(a):
    return pltpu.with_memory_space_constraint(a, pltpu.HBM)


def _place(w, me_idx, dtype, name, after=()):
    r, cdim = w.shape

    def body(i_ref, w_ref, *rest):
        rest[-1][...] = w_ref[...].astype(dtype)

    grid_spec = pltpu.PrefetchScalarGridSpec(
        num_scalar_prefetch=1, grid=(1,), in_specs=[pl.BlockSpec((r, cdim), lambda i, me: (0, 0))] + [HBM_SPEC] * len(after),
        out_specs=pl.BlockSpec((r, cdim), lambda i, me: (me[0], 0)))
    return _call(body, name=name, grid_spec=grid_spec, out_shape=_sds((N_DEV * r, cdim), dtype),
                 compiler_params=_params("arbitrary"))(me_idx, w, *after)


def _own_rows(ref, r, px, py, pc):
    return ref.at[pl.ds((4 * px + 2 * py + pc) * r, r), :]


def _gather_phase(bufs, waits, plans, after, name):
    n = len(bufs)
    rows = [b.shape[0] // N_DEV for b in bufs]
    nw, npl = len(waits), len(plans)

    def body(*refs):
        ins = refs[:n]
        wait_sems = refs[n:n + 2 * nw]
        out0 = n + 2 * nw + len(after)
        new_sems = refs[out0:out0 + 2 * npl]
        token = refs[-1]
        x, y, c = _position()
        for w, (_, _, sent, received) in enumerate(waits):
            for a in range(n):
                for count, wait in ((sent, "wait_send"), (received, "wait_recv")):
                    span = _whole(ins[a], count * rows[a])
                    getattr(pltpu.make_async_remote_copy(
                        src_ref=span, dst_ref=span, send_sem=wait_sems[2 * w].at[a], recv_sem=wait_sems[2 * w + 1].at[a],
                        device_id=(x, y, c), device_id_type=MESH), wait)()
        for k, plan in enumerate(plans):
            for a in range(n):
                for block, target in plan(x, y, c):
                    span = _own_rows(ins[a], rows[a], *block)
                    pltpu.make_async_remote_copy(src_ref=span, dst_ref=span, send_sem=new_sems[2 * k].at[a],
                                                 recv_sem=new_sems[2 * k + 1].at[a], device_id=target, device_id_type=MESH).start()
        token[...] = jnp.zeros_like(token)

    sem_ops = [s for send, recv, _, _ in waits for s in (send, recv)]
    outs = _call(
        body, name=name, in_specs=[HBM_SPEC] * n + [SEM_SPEC] * (2 * nw) + [HBM_SPEC] * len(after),
        out_specs=[SEM_SPEC] * (2 * npl) + [HBM_SPEC] * n + [TOKEN_SPEC],
        out_shape=[pltpu.SemaphoreType.DMA((n,))] * (2 * npl) + [pltpu.HBM(b.shape, b.dtype) for b in bufs] + [TOKEN],
        input_output_aliases={i: 2 * npl + i for i in range(n)},
        compiler_params=pltpu.CompilerParams(has_side_effects=EFFECT),
    )(*[_hbm(b) for b in bufs], *sem_ops, *after)
    pairs = [(outs[2 * k], outs[2 * k + 1]) for k in range(npl)]
    return pairs, list(outs[2 * npl:2 * npl + n]), outs[-1]


def _own_to_near(x, y, c):
    return [((x, y, c), (x, y, 1 - c)), ((x, y, c), (1 - x, y, c)), ((x, y, c), (x, 1 - y, c))]


def _near_to_sibling(x, y, c):
    return [((1 - x, y, c), (x, y, 1 - c)), ((x, 1 - y, c), (x, y, 1 - c))]


def _relay_diagonal(x, y, c):
    north = c
    source = (x * north + (1 - x) * (1 - north), (1 - y) * north + y * (1 - north), c)
    target = ((1 - x) * north + x * (1 - north), y * north + (1 - y) * (1 - north), c)
    return [(source, target)]


def _diagonal_to_sibling(x, y, c):
    return [((1 - x, 1 - y, c), (x, y, 1 - c))]


def _gather_start(bufs, groups, name):
    n = len(bufs)
    rows = [b.shape[0] // N_DEV for b in bufs]
    ng = len(groups)

    def body(*refs):
        ins = refs[:n]
        sems = refs[n:n + 2 * ng]
        token = refs[-1]
        x, y, c = _position()
        targets = [(x, y, 1 - c)] + [(*chip, c) for chip in _other_chips(x, y)]
        for g, members in enumerate(groups):
            for slot, a in enumerate(members):
                own = _own_rows(ins[a], rows[a], x, y, c)
                for to in targets:
                    pltpu.make_async_remote_copy(src_ref=own, dst_ref=own, send_sem=sems[2 * g].at[slot],
                                                 recv_sem=sems[2 * g + 1].at[slot], device_id=to, device_id_type=MESH).start()
        token[...] = jnp.zeros_like(token)

    sem_shapes = []
    for members in groups:
        sem_shapes += [pltpu.SemaphoreType.DMA((len(members),))] * 2
    outs = _call(
        body, name=name, in_specs=[HBM_SPEC] * n, out_specs=[SEM_SPEC] * (2 * ng) + [HBM_SPEC] * n + [TOKEN_SPEC],
        out_shape=sem_shapes + [pltpu.HBM(b.shape, b.dtype) for b in bufs] + [TOKEN],
        input_output_aliases={i: 2 * ng + i for i in range(n)},
        compiler_params=pltpu.CompilerParams(has_side_effects=EFFECT),
    )(*[_hbm(b) for b in bufs])
    sem_pairs = [(outs[2 * g], outs[2 * g + 1]) for g in range(ng)]
    return sem_pairs, list(outs[2 * ng:2 * ng + n]), outs[-1]


def _gather_forward(send_sems, recv_sems, bufs, after, name):
    n = len(bufs)
    rows = [b.shape[0] // N_DEV for b in bufs]

    def body(*refs):
        ins = refs[:n]
        send1, recv1 = refs[n], refs[n + 1]
        out0 = n + 2 + len(after)
        send2, recv2 = refs[out0], refs[out0 + 1]
        token = refs[-1]
        x, y, c = _position()
        for a in range(n):
            step1 = pltpu.make_async_remote_copy(
                src_ref=_whole(ins[a], 4 * rows[a]), dst_ref=_whole(ins[a], 4 * rows[a]), send_sem=send1.at[a],
                recv_sem=recv1.at[a], device_id=(x, y, c), device_id_type=MESH)
            step1.wait_send()
            step1.wait_recv()
        for a in range(n):
            for chip in _other_chips(x, y):
                blk = _own_rows(ins[a], rows[a], *chip, c)
                pltpu.make_async_remote_copy(src_ref=blk, dst_ref=blk, send_sem=send2.at[a], recv_sem=recv2.at[a],
                                             device_id=(x, y, 1 - c), device_id_type=MESH).start()
        token[...] = jnp.zeros_like(token)

    outs = _call(
        body, name=name, in_specs=[HBM_SPEC] * n + [SEM_SPEC, SEM_SPEC] + [HBM_SPEC] * len(after),
        out_specs=[SEM_SPEC, SEM_SPEC] + [HBM_SPEC] * n + [TOKEN_SPEC],
        out_shape=[pltpu.SemaphoreType.DMA((n,)), pltpu.SemaphoreType.DMA((n,))]
        + [pltpu.HBM(b.shape, b.dtype) for b in bufs] + [TOKEN],
        input_output_aliases={i: 2 + i for i in range(n)},
        compiler_params=pltpu.CompilerParams(has_side_effects=EFFECT),
    )(*bufs, send_sems, recv_sems, *after)
    return outs[0], outs[1], list(outs[2:2 + n]), outs[-1]


def _gather_done(send_sems, recv_sems, bufs, after, name):
    n = len(bufs)
    rows = [b.shape[0] // N_DEV for b in bufs]

    def body(*refs):
        ins = refs[:n]
        send2, recv2 = refs[n], refs[n + 1]
        x, y, c = _position()
        for a in range(n):
            step2 = pltpu.make_async_remote_copy(
                src_ref=_whole(ins[a], 3 * rows[a]), dst_ref=_whole(ins[a], 3 * rows[a]), send_sem=send2.at[a],
                recv_sem=recv2.at[a], device_id=(x, y, c), device_id_type=MESH)
            step2.wait_send()
            step2.wait_recv()

    outs = _call(
        body, name=name, in_specs=[HBM_SPEC] * n + [SEM_SPEC, SEM_SPEC] + [HBM_SPEC] * len(after),
        out_specs=[HBM_SPEC] * n, out_shape=[pltpu.HBM(b.shape, b.dtype) for b in bufs],
        input_output_aliases={i: i for i in range(n)},
        compiler_params=pltpu.CompilerParams(has_side_effects=EFFECT),
    )(*bufs, send_sems, recv_sems, *after)
    return list(outs)


def _whole(ref, nrows):
    return ref.at[pl.ds(0, nrows), :]


def _to_sibling(x, y, c):
    return [(2 * q + (1 - c), q, (x, y, 1 - c)) for q in range(4)]


def _to_chips(x, y, c):
    return [(2 * px + py, j, (px, py, c)) for j, (px, py) in enumerate(_other_chips(x, y))]


def _exchange_start(srcs, src_slots, plan, name):
    n = len(srcs)
    rows = [a.shape[0] // src_slots for a in srcs]
    n_copies = len(plan(0, 0, 0))
    lands = [lax.empty((n_copies * r, a.shape[1]), a.dtype) for a, r in zip(srcs, rows)]

    def body(*refs):
        ins, land_refs = refs[:n], refs[n:2 * n]
        send_sems, recv_sems = refs[2 * n], refs[2 * n + 1]
        token = refs[-1]
        for a in range(n):
            r = rows[a]
            for src_slot, dst_slot, target in plan(*_position()):
                pltpu.make_async_remote_copy(
                    src_ref=ins[a].at[pl.ds(src_slot * r, r), :], dst_ref=land_refs[a].at[pl.ds(dst_slot * r, r), :],
                    send_sem=send_sems.at[a], recv_sem=recv_sems.at[a], device_id=target, device_id_type=MESH).start()
        token[...] = jnp.zeros_like(token)

    outs = _call(
        body, name=name, in_specs=[HBM_SPEC] * (2 * n),
        out_specs=[SEM_SPEC, SEM_SPEC] + [HBM_SPEC] * (2 * n) + [TOKEN_SPEC],
        out_shape=[pltpu.SemaphoreType.DMA((n,)), pltpu.SemaphoreType.DMA((n,))]
        + [pltpu.HBM(a.shape, a.dtype) for a in srcs] + [pltpu.HBM(l.shape, l.dtype) for l in lands] + [TOKEN],
        input_output_aliases={i: 2 + i for i in range(2 * n)},
        compiler_params=pltpu.CompilerParams(has_side_effects=EFFECT),
    )(*[_hbm(a) for a in srcs], *[_hbm(l) for l in lands])
    return outs[0], outs[1], list(outs[2:2 + n]), list(outs[2 + n:2 + 2 * n]), outs[-1]


def _exchange_wait(send_sems, recv_sems, srcs, lands, after, name):
    n = len(srcs)

    def body(*refs):
        ins, land_refs = refs[:n], refs[n:2 * n]
        send_sems_ref, recv_sems_ref = refs[2 * n], refs[2 * n + 1]
        for a in range(n):
            allrows = lands[a].shape[0]
            cp = pltpu.make_async_remote_copy(
                src_ref=_whole(ins[a], allrows), dst_ref=_whole(land_refs[a], allrows), send_sem=send_sems_ref.at[a],
                recv_sem=recv_sems_ref.at[a], device_id=_position(), device_id_type=MESH)
            cp.wait_send()
            cp.wait_recv()

    outs = _call(
        body, name=name, in_specs=[HBM_SPEC] * (2 * n) + [SEM_SPEC, SEM_SPEC] + [HBM_SPEC] * len(after),
        out_specs=[HBM_SPEC] * (2 * n),
        out_shape=[pltpu.HBM(a.shape, a.dtype) for a in srcs] + [pltpu.HBM(l.shape, l.dtype) for l in lands],
        input_output_aliases={i: i for i in range(2 * n)},
        compiler_params=pltpu.CompilerParams(has_side_effects=EFFECT),
    )(*srcs, *lands, send_sems, recv_sems, *after)
    return list(outs[:n]), list(outs[n:])


def _chip_partial(grad, recv, idx, name):
    r = recv.shape[0] // 4

    def body(i_ref, g_ref, s_ref, o_ref):
        del i_ref
        o_ref[...] = (g_ref[...].astype(F32) + s_ref[...].astype(F32)).astype(BF16)

    nb = 1
    tr = r // nb
    grid_spec = pltpu.PrefetchScalarGridSpec(
        num_scalar_prefetch=1, grid=(3, nb),
        in_specs=[pl.BlockSpec((tr, D), lambda t, i, i_ref: ((2 * i_ref[1 + t] + i_ref[0]) * nb + i, 0)),
                  pl.BlockSpec((tr, D), lambda t, i, i_ref: (i_ref[1 + t] * nb + i, 0))],
        out_specs=pl.BlockSpec((tr, D), lambda t, i, i_ref: (i_ref[1 + t] * nb + i, 0)))
    return _call(body, name=name, grid_spec=grid_spec, out_shape=_sds((4 * r, D), BF16),
                 compiler_params=_params("arbitrary", "arbitrary"))(idx, grad, recv)


def _adamw_math(w, g, m, v):
    m2 = B1 * m + (1.0 - B1) * g
    v2 = B2 * v + (1.0 - B2) * jnp.square(g)
    m_hat = m2 / (1.0 - B1 ** STEP)
    v_hat = v2 / (1.0 - B2 ** STEP)
    return -LR * (m_hat / (jnp.sqrt(v_hat) + EPS_ADAM) + WD * w), m2, v2


def _reduce_adamw(w, grad, from_sibling, from_chips, idx, m, v, name):
    r = w.shape[0]
    assert grad.shape == (N_DEV * r, D) and from_sibling.shape == (4 * r, D) and from_chips.shape == (3 * r, D)
    tr = r // 2
    nb = r // tr

    def body(i_ref, w_ref, p_ref, s_ref, r0_ref, r1_ref, r2_ref, m_ref, v_ref, g_ref, d_ref, nm_ref, nv_ref):
        del i_ref
        g = p_ref[...].astype(F32) + s_ref[...].astype(F32)
        g = ((g + r0_ref[...].astype(F32)) + r1_ref[...].astype(F32)) + r2_ref[...].astype(F32)
        g_ref[...] = g
        d_ref[...], nm_ref[...], nv_ref[...] = _adamw_math(w_ref[...], g, m_ref[...], v_ref[...])

    own = pl.BlockSpec((tr, D), lambda i, i_ref: (i, 0))
    grid_spec = pltpu.PrefetchScalarGridSpec(
        num_scalar_prefetch=1, grid=(nb,),
        in_specs=[own, pl.BlockSpec((tr, D), lambda i, i_ref: (i_ref[0] * nb + i, 0)),
                  pl.BlockSpec((tr, D), lambda i, i_ref: (i_ref[1] * nb + i, 0))]
        + [pl.BlockSpec((tr, D), lambda i, i_ref, j=j: (j * nb + i, 0)) for j in range(3)] + [own, own],
        out_specs=[own] * 4)
    return _call(body, name=name, grid_spec=grid_spec, out_shape=[_sds((r, D), F32)] * 4,
                 compiler_params=_params("parallel"))(idx, w, grad, from_sibling, from_chips, from_chips, from_chips, m, v)


SMALL_ROWS = 8


def _small_all_reduce(pack, name, after=()):
    def body(p_ref, *rest):
        tot_ref, loss_ref, gath, send_sems, recv_sems = rest[len(after):]
        x, y, c = _position()
        me_id = 4 * x + 2 * y + c
        gath[me_id] = p_ref[...]
        copies = []
        for k in range(1, N_DEV):
            peer = tuple(1 - v if (k >> b) & 1 else v for v, b in ((x, 2), (y, 1), (c, 0)))
            cp = pltpu.make_async_remote_copy(src_ref=p_ref, dst_ref=gath.at[me_id], send_sem=send_sems.at[k - 1],
                                              recv_sem=recv_sems.at[k - 1], device_id=peer, device_id_type=MESH)
            cp.start()
            copies.append(cp)
        for cp in copies:
            cp.wait_recv()
        for cp in copies:
            cp.wait_send()
        tot = gath[0]
        for d in range(1, N_DEV):
            tot = tot + gath[d]
        tot_ref[...] = tot
        loss_ref[...] = jnp.full((1, 128), (0.5 / D) * jnp.sum(tot[SMALL_ROWS - 1:SMALL_ROWS, :]), F32)

    vm = pl.BlockSpec(memory_space=pltpu.VMEM)
    return _call(
        body, name=name, in_specs=[vm] + [HBM_SPEC] * len(after), out_specs=[vm, vm],
        out_shape=[_sds((SMALL_ROWS, D), F32), _sds((1, 128), F32)],
        scratch_shapes=[pltpu.VMEM((N_DEV, SMALL_ROWS, D), F32), pltpu.SemaphoreType.DMA((N_DEV - 1,)),
                        pltpu.SemaphoreType.DMA((N_DEV - 1,))],
    )(pack, *after)


def _adamw_small(ws, gs, ms, vs, name):
    n = len(ws)

    def body(*refs):
        for a in range(n):
            w_ref, g_ref, m_ref, v_ref = (refs[k * n + a] for k in range(4))
            d_ref, nm_ref, nv_ref = (refs[(4 + k) * n + a] for k in range(3))
            d_ref[...], nm_ref[...], nv_ref[...] = _adamw_math(w_ref[...], g_ref[...], m_ref[...], v_ref[...])

    vm = pl.BlockSpec(memory_space=pltpu.VMEM)
    outs = _call(body, name=name, in_specs=[vm] * (4 * n), out_specs=[vm] * (3 * n),
                 out_shape=[_sds(w.shape, F32) for w in ws] * 3)(*ws, *gs, *ms, *vs)
    return [(outs[a], outs[n + a], outs[2 * n + a]) for a in range(n)]


def kernel(x, g_mix, w_in, conv_w, attn_sinks, w_conv_out, w_attn_out, w_o, g_ffn, w_gate_up, w_down, g_final, loss_target, m_g_mix, m_w_in, m_conv_w, m_attn_sinks, m_w_conv_out, m_w_attn_out, m_w_o, m_g_ffn, m_w_gate_up, m_w_down, m_g_final, v_g_mix, v_w_in, v_conv_w, v_attn_sinks, v_w_conv_out, v_w_attn_out, v_w_o, v_g_ffn, v_w_gate_up, v_w_down, v_g_final):
    cx, cy, cc = _position()
    chip = 2 * cx + cy
    partial_idx = jnp.stack([cc, 2 * (1 - cx) + cy, 2 * cx + (1 - cy), 2 * (1 - cx) + (1 - cy)]).astype(jnp.int32)
    own_idx = jnp.stack([2 * chip + cc, chip]).astype(jnp.int32)
    me = 4 * cx + 2 * cy + cc

    me_idx = jnp.reshape(me, (1,)).astype(jnp.int32)
    first = [_place(jnp.transpose(w_in[0]), me_idx, BF16, "place_w_in"),
             _place(jnp.pad(conv_w[0], ((0, 5), (0, 0))), me_idx, F32, "place_conv_w")]
    (to_near,), first, token_in = _gather_phase(first, [], [_own_to_near], (), "gather_in_start")
    gather_tokens = (token_in,)

    class Gathered:
        def __init__(self):
            self.state = {}

        def begin(self, group, after):
            if group == "in":
                (near, relay), bufs, token = _gather_phase(
                    first, [(*to_near, 3, 3)], [_near_to_sibling, _relay_diagonal], after, "gather_in_relay")
                later = [_place(w, me_idx, BF16, "place_" + k, after=(token,)) for k, w in (
                    ("w_conv_out", w_conv_out[0]), ("w_attn_out", w_attn_out[0]), ("w_o", w_o[0]),
                    ("w_gate_up", jnp.transpose(w_gate_up[0])), ("w_down", w_down[0]))]
                (sems_mix, sems_ffn), later, token_later = _gather_start(later, [[0, 1, 2], [3, 4]], "gather_start_later")
                self.state.update({"in": (near, relay, bufs), "mix": (sems_mix, later[:3]), "ffn": (sems_ffn, later[3:])})
                return (token_later,)
            (send_sems, recv_sems), group_bufs = self.state[group]
            send2, recv2, group_bufs, token = _gather_forward(send_sems, recv_sems, group_bufs, after, "gather_forward_" + group)
            self.state[group] = ((send2, recv2), group_bufs)
            return (token,)

        def end(self, group, after):
            if group == "in":
                near, relay, bufs = self.state[group]
                (last,), bufs, token = _gather_phase(bufs, [(*relay, 1, 1)], [_diagonal_to_sibling], after, "gather_in_last")
                _, full, _ = _gather_phase(bufs, [(*near, 2, 2), (*last, 1, 1)], [], (token,), "gather_in_done")
                return full[0], jnp.transpose(full[1].reshape(N_DEV, 8, 128)[:, :3, :], (1, 0, 2)).reshape(3, D)
            (send2, recv2), group_bufs = self.state[group]
            return _gather_done(send2, recv2, group_bufs, after, "gather_done_" + group)

    in_flight, own_pieces = {}, {}

    transposed = ("w_in", "w_gate_up")

    def as2d(k, a):
        if k in transposed:
            return jnp.transpose(a[0])
        return a[None] if a.ndim == 1 else (a[0] if a.ndim == 3 else a)

    w_all = {"g_mix": g_mix, "w_in": w_in, "conv_w": conv_w, "attn_sinks": attn_sinks, "w_conv_out": w_conv_out,
             "w_attn_out": w_attn_out, "w_o": w_o, "g_ffn": g_ffn, "w_gate_up": w_gate_up, "w_down": w_down, "g_final": g_final}
    m_all = {"g_mix": m_g_mix, "w_in": m_w_in, "conv_w": m_conv_w, "attn_sinks": m_attn_sinks, "w_conv_out": m_w_conv_out,
             "w_attn_out": m_w_attn_out, "w_o": m_w_o, "g_ffn": m_g_ffn, "w_gate_up": m_w_gate_up, "w_down": m_w_down,
             "g_final": m_g_final}
    v_all = {"g_mix": v_g_mix, "w_in": v_w_in, "conv_w": v_conv_w, "attn_sinks": v_attn_sinks, "w_conv_out": v_w_conv_out,
             "w_attn_out": v_w_attn_out, "w_o": v_w_o, "g_ffn": v_g_ffn, "w_gate_up": v_w_gate_up, "w_down": v_w_down,
             "g_final": v_g_final}
    results = {}

    def record(k, *vals):
        results[k] = [(jnp.transpose(val) if k in transposed else val).reshape(w_all[k].shape) for val in vals]

    def update(k, pieces):
        g, d, nm, nv = _reduce_adamw(as2d(k, w_all[k]), *pieces, own_idx, as2d(k, m_all[k]), as2d(k, v_all[k]), "adamw_" + k)
        record(k, g, d, nm, nv)
        return nm

    def update_small(grads):
        keys = list(grads)
        outs = _adamw_small([as2d(k, w_all[k]) for k in keys], [grads[k] for k in keys], [as2d(k, m_all[k]) for k in keys],
                            [as2d(k, v_all[k]) for k in keys], "adamw_small")
        for k, (d, nm, nv) in zip(keys, outs):
            record(k, grads[k], d, nm, nv)
        return tuple(nm for _, nm, _ in outs)

    kernel_name = {"win_t": "w_in", "wgu_t": "w_gate_up", "wd": "w_down", "wco": "w_conv_out", "wao": "w_attn_out", "wo": "w_o"}

    def finish(group, after):
        keys, send_sems, recv_sems, parts, from_chips = in_flight[group]
        _, from_chips = _exchange_wait(send_sems, recv_sems, parts, from_chips, after, "rs_chips_wait_" + group)
        grads, from_sibling = own_pieces[group]
        return tuple(update(kernel_name[k], p) for k, *p in zip(keys, grads, from_sibling, from_chips))

    class Reducer:
        def start(self, group, gdict):
            keys, glist = list(gdict), list(gdict.values())
            send_sems, recv_sems, glist, lands, token = _exchange_start(glist, N_DEV, _to_sibling, "rs_sibling_start_" + group)
            in_flight[group] = (keys, send_sems, recv_sems, glist, lands)
            return (token,)

        def middle(self, group, after):
            keys, send_sems, recv_sems, glist, lands = in_flight[group]
            if group == "in":
                after = finish("ffn", after)
            glist, lands = _exchange_wait(send_sems, recv_sems, glist, lands, after, "rs_sibling_wait_" + group)
            parts = [_chip_partial(g, r, partial_idx, "chip_partial_" + k) for k, g, r in zip(keys, glist, lands)]
            send_sems, recv_sems, parts, from_chips, token = _exchange_start(parts, 4, _to_chips, "rs_chips_start_" + group)
            in_flight[group] = (keys, send_sems, recv_sems, parts, from_chips)
            own_pieces[group] = (glist, lands)
            return (token,)

    dx, _, small = _local_step(x[0], loss_target[0], g_mix, g_ffn, g_final[None], attn_sinks, Gathered(),
                               reducer=Reducer(), after=gather_tokens)
    after = finish("mix", (dx,))

    sinks_row = jnp.pad(small["sinks"], ((0, 0), (0, D - 128)))
    pack = jnp.concatenate([small["g_mix"], small["g_ffn"], small["g_final"], small["conv_w"], sinks_row, small["lossvec"]], axis=0)
    tot, loss_row = _small_all_reduce(pack, "small_all_reduce", after=after)
    loss = loss_row[0, 0]
    g_small = {
        "g_mix": tot[0:1], "g_ffn": tot[1:2], "g_final": tot[2:3],
        "conv_w": lax.dynamic_slice(tot, (3, me * 128), (3, 128)), "attn_sinks": tot[6:7, :N_HEADS],
    }
    finish("in", update_small(g_small))

    order = ["g_mix", "w_in", "conv_w", "attn_sinks", "w_conv_out", "w_attn_out", "w_o", "g_ffn", "w_gate_up", "w_down", "g_final"]
    return (loss, dx[None], *[results[k][i] for i in range(4) for k in order])
```

```python
import functools
import math

import jax
import jax.numpy as jnp
from jax import lax
from jax.experimental import pallas as pl
from jax.experimental.pallas import tpu as pltpu

F32 = jnp.float32
BF16 = jnp.bfloat16

D = 1024
HEAD_DIM = 64
N_HEADS = 16
N_KV = 4
GROUP = N_HEADS // N_KV
D_KV = N_KV * HEAD_DIM
BLOCK = 128
ROT_DIM = HEAD_DIM // 4
ROPE_THETA = 500000.0
ATTN_SCALE = 1.0 / math.sqrt(HEAD_DIM)
NEG_INF = -1e30
D_FF = 2816
N_IN = 6656
EPS = 1e-5
C_CB, C_CC, C_CX, C_Q, C_K, C_V, C_GC, C_GA = 0, 1024, 2048, 3072, 4096, 4352, 4608, 5632

LR, B1, B2, EPS_ADAM, WD, STEP = 0.001, 0.9, 0.999, 1e-08, 0.01, 10

N_DEV = 8
MESH = pl.DeviceIdType.MESH
VMEM_LIMIT = 56 * 1024 * 1024

NN = (((1,), (0,)), ((), ()))
NT = (((1,), (1,)), ((), ()))
TN = (((0,), (0,)), ((), ()))
HBM_SPEC = pl.BlockSpec(memory_space=pl.ANY)
ROW_SPLIT = 4


def _call(body, **kw):
    return pl.pallas_call(body, **kw)


def _params(*sem):
    return pltpu.CompilerParams(dimension_semantics=sem, vmem_limit_bytes=VMEM_LIMIT)


def _sds(shape, dtype):
    return jax.ShapeDtypeStruct(shape, dtype)


def _matmul(a, b, *, mode, tm, tn, tk, out_dtype, name, res=None, after=()):
    parts = list(a) if isinstance(a, (list, tuple)) else [a]
    rows_a = parts[0].shape[0]
    cols_a = sum(p.shape[1] for p in parts)
    if mode == "nn":
        (m, kk), (_, n), dims = (rows_a, cols_a), b.shape, NN
    elif mode == "nt":
        (m, kk), (n, _), dims = (rows_a, cols_a), b.shape, NT
    else:
        (kk, m), (_, n), dims = (rows_a, cols_a), b.shape, TN
    tm, tn, tk = min(tm, m), min(tn, n), min(tk, kk)
    assert m % tm == 0 and n % tn == 0 and kk % tk == 0, (name, m, n, kk, tm, tn, tk)
    nk = kk // tk
    split_axis, width = (2, tk) if mode == "nn" else (0, tm)
    assert len(parts) == 1 or mode in ("nn", "tn")
    assert len(parts) == 1 or all(p.shape[1] % width == 0 for p in parts), (name, width)
    counts = [p.shape[1] // width for p in parts]
    starts = [sum(counts[:p]) for p in range(len(parts))]

    def a_spec(p):
        def col(t):
            return jnp.clip(t - starts[p], 0, counts[p] - 1) if len(parts) > 1 else t

        if mode == "tn":
            return pl.BlockSpec((tk, tm), lambda i, j, k: (k, col(i)))
        return pl.BlockSpec((tm, tk), lambda i, j, k: (i, col(k)))

    if mode == "nt":
        b_spec = pl.BlockSpec((tn, tk), lambda i, j, k: (j, k))
    else:
        b_spec = pl.BlockSpec((tk, tn), lambda i, j, k: (k, j))
    o_spec = pl.BlockSpec((tm, tn), lambda i, j, k: (i, j))
    has_res = res is not None
    n_parts = len(parts)
    unit = 128 if mode == "tn" else 16
    split = ROW_SPLIT if tm % (ROW_SPLIT * unit) == 0 else 1

    def body(*refs):
        a_refs, b_ref = refs[:n_parts], refs[n_parts]
        r_ref = refs[n_parts + 1] if has_res else None
        o_ref = refs[n_parts + 1 + has_res + len(after)]
        k = pl.program_id(2)

        acc_ref = refs[-1] if nk > 1 else None

        def step(a_ref):
            def matmul(rows):
                a_blk = a_ref[:, rows] if mode == "tn" else a_ref[rows, :]
                return lax.dot_general(a_blk, b_ref[...], dims, preferred_element_type=F32)

            def finish(rows, part):
                if nk > 1:
                    acc_ref[rows, :] += part
                else:
                    o_ref[rows, :] = (part + r_ref[rows, :] if has_res else part).astype(o_ref.dtype)

            _row_pipeline(tm, matmul, finish, split)

        if nk > 1:
            @pl.when(k == 0)
            def _():
                acc_ref[...] = jnp.zeros_like(acc_ref)

        if n_parts == 1:
            step(a_refs[0])
        else:
            t = pl.program_id(split_axis)
            for p in range(n_parts):
                pl.when((t >= starts[p]) & (t < starts[p] + counts[p]))(functools.partial(step, a_refs[p]))

        if nk > 1:
            @pl.when(k == nk - 1)
            def _():
                o_ref[...] = (acc_ref[...] + r_ref[...] if has_res else acc_ref[...]).astype(o_ref.dtype)

    ins = parts + [b] + ([res] if has_res else []) + list(after)
    in_specs = [a_spec(p) for p in range(n_parts)] + [b_spec] + ([o_spec] if has_res else []) + [HBM_SPEC] * len(after)
    scratch = [] if nk == 1 else [pltpu.VMEM((tm, tn), F32)]
    return _call(
        body, name=name, grid=(m // tm, n // tn, nk), in_specs=in_specs, out_specs=o_spec,
        out_shape=_sds((m, n), out_dtype), scratch_shapes=scratch,
        compiler_params=_params("parallel", "parallel", "arbitrary"),
    )(*ins)


def _row_tile(s):
    return min(512, s)


def _rms_fwd(x, g, name, after=()):
    s = x.shape[0]
    tm = _row_tile(s)

    def body(x_ref, g_ref, *rest):
        h_ref = rest[-1]
        xv = x_ref[...]
        r = lax.rsqrt(jnp.mean(xv * xv, axis=-1, keepdims=True) + EPS)
        h_ref[...] = (xv * r * g_ref[...]).astype(BF16)

    row = pl.BlockSpec((tm, D), lambda i: (i, 0))
    return _call(
        body, name=name, grid=(s // tm,), in_specs=[row, pl.BlockSpec((1, D), lambda i: (0, 0))] + [HBM_SPEC] * len(after),
        out_specs=row, out_shape=_sds((s, D), BF16), compiler_params=_params("parallel"),
    )(x, g, *after)


def _rms_bwd(dh, x, g, dres, name, after=()):
    s = x.shape[0]
    tm = _row_tile(s)

    def body(dh_ref, x_ref, g_ref, dres_ref, *rest):
        dx_ref, dxb_ref, dg_ref = rest[len(after):]
        xv = x_ref[...]
        r = lax.rsqrt(jnp.mean(xv * xv, axis=-1, keepdims=True) + EPS)
        xh = xv * r
        dhv = dh_ref[...]
        dyg = dhv * g_ref[...]
        dx = dres_ref[...] + r * (dyg - xh * jnp.mean(dyg * xh, axis=-1, keepdims=True))
        dx_ref[...] = dx
        dxb_ref[...] = dx.astype(BF16)
        part = jnp.sum(dhv * xh, axis=0, keepdims=True)

        @pl.when(pl.program_id(0) == 0)
        def _():
            dg_ref[...] = part

        @pl.when(pl.program_id(0) > 0)
        def _():
            dg_ref[...] += part

    row = pl.BlockSpec((tm, D), lambda i: (i, 0))
    vec = pl.BlockSpec((1, D), lambda i: (0, 0))
    return _call(
        body, name=name, grid=(s // tm,), in_specs=[row, row, vec, row] + [HBM_SPEC] * len(after), out_specs=[row, row, vec],
        out_shape=[_sds((s, D), F32), _sds((s, D), BF16), _sds((1, D), F32)],
        compiler_params=_params("arbitrary"),
    )(dh, x, g, dres, *after)


def _loss_head(x2, g, tgt, name):
    s = x2.shape[0]
    tm = _row_tile(s)

    def body(x_ref, g_ref, t_ref, dx_ref, dxb_ref, dg_ref, l_ref):
        xv = x_ref[...]
        gv = g_ref[...]
        r = lax.rsqrt(jnp.mean(xv * xv, axis=-1, keepdims=True) + EPS)
        xh = xv * r
        err = xh * gv - t_ref[...]
        dy = err * (1.0 / D)
        dyg = dy * gv
        dx = r * (dyg - xh * jnp.mean(dyg * xh, axis=-1, keepdims=True))
        dx_ref[...] = dx
        dxb_ref[...] = dx.astype(BF16)
        dg_part = jnp.sum(dy * xh, axis=0, keepdims=True)
        l_part = jnp.sum(err * err, axis=0, keepdims=True)

        @pl.when(pl.program_id(0) == 0)
        def _():
            dg_ref[...] = dg_part
            l_ref[...] = l_part

        @pl.when(pl.program_id(0) > 0)
        def _():
            dg_ref[...] += dg_part
            l_ref[...] += l_part

    row = pl.BlockSpec((tm, D), lambda i: (i, 0))
    vec = pl.BlockSpec((1, D), lambda i: (0, 0))
    return _call(
        body, name=name, grid=(s // tm,), in_specs=[row, vec, row], out_specs=[row, row, vec, vec],
        out_shape=[_sds((s, D), F32), _sds((s, D), BF16), _sds((1, D), F32), _sds((1, D), F32)],
        compiler_params=_params("arbitrary"),
    )(x2, g, tgt)


CONV_TC = 256


def _shift_down(u, k, rows):
    return jnp.where(rows >= k, pltpu.roll(u, k, 0), 0.0)


def _shift_up(u, k, rows, s):
    return jnp.where(rows < s - k, pltpu.roll(u, s - k, 0), 0.0)


def _conv_specs(s):
    nb = D // CONV_TC

    def col(c0):
        return pl.BlockSpec((s, CONV_TC), lambda j, c0=c0: (0, c0 // CONV_TC + j))

    return nb, col


def _conv_fwd(proj, conv_w, name):
    s = proj.shape[0]
    nb, col = _conv_specs(s)

    def body(cb_ref, cc_ref, cx_ref, w_ref, y_ref):
        rows = lax.broadcasted_iota(jnp.int32, (s, CONV_TC), 0)
        u = cc_ref[...].astype(F32) * cx_ref[...].astype(F32)
        w = w_ref[...]
        c = w[0:1] * _shift_down(u, 2, rows) + w[1:2] * _shift_down(u, 1, rows) + w[2:3] * u
        y_ref[...] = (cb_ref[...].astype(F32) * c).astype(BF16)

    return _call(
        body, name=name, grid=(nb,),
        in_specs=[col(C_CB), col(C_CC), col(C_CX), pl.BlockSpec((3, CONV_TC), lambda j: (0, j))],
        out_specs=pl.BlockSpec((s, CONV_TC), lambda j: (0, j)), out_shape=_sds((s, D), BF16),
        compiler_params=_params("parallel"),
    )(proj, proj, proj, conv_w)


def _write_behind(t, nt, buf, sems, tiles, window, where):
    slot = t % 2

    def copies(sl, at):
        return [pltpu.make_async_copy(buf.at[sl, p], window(p, at), sems.at[sl, p]) for p in range(len(tiles))]

    @pl.when(t >= 2)
    def _():
        for cp in copies(slot, where):
            cp.wait()

    for p, tile in enumerate(tiles):
        buf[slot, p] = tile
    started = copies(slot, where)
    for cp in started:
        cp.start()

    @pl.when(t == nt - 1)
    def _():
        for cp in started:
            cp.wait()
        if nt > 1:
            for cp in copies(1 - slot, where):
                cp.wait()


def _conv_bwd(dy, proj, conv_w, dproj, name, after=()):
    s = proj.shape[0]
    nb, col = _conv_specs(s)

    def body(dy_ref, cb_ref, cc_ref, cx_ref, w_ref, *rest):
        dproj_ref, dw_ref, buf, sems = rest[1 + len(after):]
        j = pl.program_id(0)
        rows = lax.broadcasted_iota(jnp.int32, (s, CONV_TC), 0)
        cc = cc_ref[...].astype(F32)
        cx = cx_ref[...].astype(F32)
        u = cc * cx
        u1 = _shift_down(u, 1, rows)
        u2 = _shift_down(u, 2, rows)
        w = w_ref[...]
        c = w[0:1] * u2 + w[1:2] * u1 + w[2:3] * u
        dyv = dy_ref[...].astype(F32)
        dc = dyv * cb_ref[...].astype(F32)
        du = w[2:3] * dc + w[1:2] * _shift_up(dc, 1, rows, s) + w[0:1] * _shift_up(dc, 2, rows, s)

        def window(p, jj):
            start = pl.multiple_of((C_CB, C_CC, C_CX)[p] + jj * CONV_TC, CONV_TC)
            return dproj_ref.at[:, pl.ds(start, CONV_TC)]

        tiles = ((dyv * c).astype(BF16), (du * cx).astype(BF16), (du * cc).astype(BF16))
        _write_behind(j * 0, 1, buf, sems, tiles, window, j)
        dw_ref[...] = jnp.concatenate(
            [jnp.sum(dc * u2, axis=0, keepdims=True), jnp.sum(dc * u1, axis=0, keepdims=True),
             jnp.sum(dc * u, axis=0, keepdims=True)], axis=0)

    return _call(
        body, name=name, grid=(nb,),
        in_specs=[pl.BlockSpec((s, CONV_TC), lambda j: (0, j)), col(C_CB), col(C_CC), col(C_CX),
                  pl.BlockSpec((3, CONV_TC), lambda j: (0, j))] + [HBM_SPEC] * (1 + len(after)),
        out_specs=[pl.BlockSpec(memory_space=pl.ANY), pl.BlockSpec((3, CONV_TC), lambda j: (0, j))],
        out_shape=[_sds((s, N_IN), BF16), _sds((3, D), F32)],
        scratch_shapes=[pltpu.VMEM((1, 3, s, CONV_TC), BF16), pltpu.SemaphoreType.DMA((1, 3))],
        input_output_aliases={5: 0}, compiler_params=_params("arbitrary"),
    )(dy, proj, proj, proj, conv_w, dproj, *after)


def _rope_tables(s):
    half = ROT_DIM // 2
    inv_freq = ROPE_THETA ** (-jnp.arange(0, ROT_DIM, 2, dtype=F32) / ROT_DIM)
    inv64 = jnp.concatenate([inv_freq, inv_freq, jnp.zeros((HEAD_DIM - ROT_DIM,), F32)])
    ang = jnp.arange(s, dtype=F32)[:, None] * jnp.concatenate([inv64, inv64])[None, :]
    d = lax.broadcasted_iota(jnp.int32, (s, 128), 1) % HEAD_DIM
    cos, sin = jnp.cos(ang), jnp.sin(ang)
    c = jnp.where(d < ROT_DIM, cos, 1.0)
    a = jnp.where(d < half, -sin, 0.0)
    b = jnp.where((d >= half) & (d < ROT_DIM), sin, 0.0)
    return jnp.concatenate([c, a, b], axis=1)


def _rope(x, tab):
    c, a, b = tab[:, 0:128], tab[:, 128:256], tab[:, 256:384]
    outs = []
    for i in range(x.shape[1] // 128):
        xc = x[:, i * 128:(i + 1) * 128]
        outs.append(xc * c + pltpu.roll(xc, 120, 1) * a + pltpu.roll(xc, 8, 1) * b)
    return outs[0] if len(outs) == 1 else jnp.concatenate(outs, axis=1)


def _rope_t(dx, tab):
    c, a, b = tab[:, 0:128], tab[:, 128:256], tab[:, 256:384]
    outs = []
    for i in range(dx.shape[1] // 128):
        dc = dx[:, i * 128:(i + 1) * 128]
        outs.append(dc * c + pltpu.roll(dc * a, 8, 1) + pltpu.roll(dc * b, 120, 1))
    return outs[0] if len(outs) == 1 else jnp.concatenate(outs, axis=1)


def _attn_in_specs():
    prev = lambda n: jnp.maximum(n - 1, 0)
    return [
        pl.BlockSpec((BLOCK, D), lambda n: (n, C_Q // D)),
        pl.BlockSpec((BLOCK, D_KV), lambda n: (n, C_K // D_KV)),
        pl.BlockSpec((BLOCK, D_KV), lambda n: (prev(n), C_K // D_KV)),
        pl.BlockSpec((BLOCK, D_KV), lambda n: (n, C_V // D_KV)),
        pl.BlockSpec((BLOCK, D_KV), lambda n: (prev(n), C_V // D_KV)),
        pl.BlockSpec((BLOCK, 384), lambda n: (n, 0)),
        pl.BlockSpec((BLOCK, 384), lambda n: (prev(n), 0)),
        pl.BlockSpec(memory_space=pltpu.SMEM),
    ]


HALF = HEAD_DIM
N_CHUNK = D // 128


def _swa_bias(n):
    qi = lax.broadcasted_iota(jnp.int32, (BLOCK, 2 * BLOCK), 0)
    kj = lax.broadcasted_iota(jnp.int32, (BLOCK, 2 * BLOCK), 1)
    rel = qi + BLOCK - kj
    valid = (rel >= 0) & (rel < BLOCK) & ((kj >= BLOCK) | (n > 0))
    return jnp.where(valid, 0.0, NEG_INF)


def _halves(x):
    lo = lax.broadcasted_iota(jnp.int32, x.shape, 1) < HALF
    return jnp.where(lo, x, 0.0).astype(BF16), jnp.where(lo, 0.0, x).astype(BF16)


def _dup_heads(x):
    out = []
    for pair in range(N_KV // 2):
        xc = x[:, pair * 128:(pair + 1) * 128]
        xr = pltpu.roll(xc, HALF, 1)
        lo = lax.broadcasted_iota(jnp.int32, xc.shape, 1) < HALF
        out += [jnp.where(lo, xc, xr), jnp.where(lo, xr, xc)]
    return out


def _swa_load(q_ref, kc_ref, kp_ref, vc_ref, vp_ref, tc_ref, tp_ref):
    qf = _rope(q_ref[...].astype(F32), tc_ref[...]) * ATTN_SCALE
    q_halves = [_halves(qf[:, c * 128:(c + 1) * 128]) for c in range(N_CHUNK)]
    kf = jnp.concatenate([_rope(kp_ref[...].astype(F32), tp_ref[...]), _rope(kc_ref[...].astype(F32), tc_ref[...])], axis=0)
    vf = jnp.concatenate([vp_ref[...], vc_ref[...]], axis=0).astype(F32)
    return q_halves, _dup_heads(kf), _dup_heads(vf)


def _swa_probs(qh, kk, bias, sink):
    s = lax.dot_general(qh, kk, NT, preferred_element_type=F32) + bias
    m = jnp.maximum(jnp.max(jnp.maximum(s[:, :BLOCK], s[:, BLOCK:]), axis=1, keepdims=True), sink)
    return jnp.exp(s - m), m


def _swa_fwd(proj, tab, sinks, name, after=()):
    s = proj.shape[0]

    def body(q_ref, kc_ref, kp_ref, vc_ref, vp_ref, tc_ref, tp_ref, sink_ref, *rest):
        o_ref = rest[-1]
        n = pl.program_id(0)
        q_halves, kdup, vdup = _swa_load(q_ref, kc_ref, kp_ref, vc_ref, vp_ref, tc_ref, tp_ref)
        bias = _swa_bias(n)
        ones = jnp.ones((2 * BLOCK, 128), BF16)
        kk = [k.astype(BF16) for k in kdup]
        vv = [[jnp.concatenate([v_half, ones], axis=1) for v_half in _halves(v)] for v in vdup]
        heads = [(c, half) for c in range(N_CHUNK) for half in range(2)]
        scores = [lax.dot_general(q_halves[c][half], kk[c // (GROUP // 2)], NT, preferred_element_type=F32)
                  for c, half in heads]
        probs = []
        for (c, half), sc in zip(heads, scores):
            sc = sc + bias
            m = jnp.maximum(jnp.max(jnp.maximum(sc[:, :BLOCK], sc[:, BLOCK:]), axis=1, keepdims=True), sink_ref[0, 2 * c + half])
            probs.append((jnp.exp(sc - m).astype(BF16), jnp.exp(sink_ref[0, 2 * c + half] - m)))
        outs = [lax.dot_general(e, vv[c // (GROUP // 2)][half], NN, preferred_element_type=F32)
                for (c, half), (e, _) in zip(heads, probs)]
        for c in range(N_CHUNK):
            parts = [outs[2 * c + half][:, :128] * (1.0 / (outs[2 * c + half][:, 128:] + probs[2 * c + half][1]))
                     for half in range(2)]
            o_ref[:, c * 128:(c + 1) * 128] = (parts[0] + parts[1]).astype(BF16)

    return _call(
        body, name=name, grid=(s // BLOCK,), in_specs=_attn_in_specs() + [HBM_SPEC] * len(after),
        out_specs=pl.BlockSpec((BLOCK, D), lambda n: (n, 0)), out_shape=_sds((s, D), BF16),
        compiler_params=_params("parallel"),
    )(proj, proj, proj, proj, proj, tab, tab, sinks, *after)


def _swa_bwd(do, proj, tab, sinks, dproj, name, after=()):
    s = proj.shape[0]
    nblk = s // BLOCK
    kv_of = lambda c: c // (GROUP // 2)

    def body(do_ref, q_ref, kc_ref, kp_ref, vc_ref, vp_ref, tc_ref, tp_ref, sink_ref, *rest):
        dproj_ref, dk_ref, dv_ref, ds_ref, dqout, dkbuf, dvbuf, sems = rest[1 + len(after):]
        n = pl.program_id(0)

        @pl.when(n == 0)
        def _():
            dk_ref[...] = jnp.zeros_like(dk_ref)
            dv_ref[...] = jnp.zeros_like(dv_ref)
            ds_ref[...] = jnp.zeros_like(ds_ref)

        q_halves, kdup, vdup = _swa_load(q_ref, kc_ref, kp_ref, vc_ref, vp_ref, tc_ref, tp_ref)
        dof = do_ref[...].astype(F32)
        do_halves = [_halves(dof[:, c * 128:(c + 1) * 128]) for c in range(N_CHUNK)]
        bias = _swa_bias(n)
        ones = jnp.ones((2 * BLOCK, 128), BF16)
        kk = [k.astype(BF16) for k in kdup]
        vv = [v.astype(BF16) for v in vdup]
        k_halves = [_halves(k) for k in kdup]
        heads = [(c, half) for c in range(N_CHUNK) for half in range(2)]
        lane_row = lax.broadcasted_iota(jnp.int32, (1, 128), 1)
        lo_kv = lax.broadcasted_iota(jnp.int32, (2 * BLOCK, 128), 1) < HALF
        scores = [lax.dot_general(q_halves[c][half], kk[kv_of(c)], NT, preferred_element_type=F32) for c, half in heads]
        dps = [lax.dot_general(do_halves[c][half], vv[kv_of(c)], NT, preferred_element_type=F32) for c, half in heads]
        exps = []
        for (c, half), sc in zip(heads, scores):
            sink = sink_ref[0, 2 * c + half]
            sc = sc + bias
            m = jnp.maximum(jnp.max(jnp.maximum(sc[:, :BLOCK], sc[:, BLOCK:]), axis=1, keepdims=True), sink)
            exps.append((jnp.exp(sc - m), jnp.exp(sink - m)))
        sums = [lax.dot_general(e.astype(BF16), ones, NN, preferred_element_type=F32) for e, _ in exps]
        dsink_row = jnp.zeros((1, 128), F32)
        dsb, pb = [], []
        for h, ((e, es), row_sum, dp) in enumerate(zip(exps, sums, dps)):
            inv = 1.0 / (row_sum + es)
            p = e * jnp.concatenate([inv, inv], axis=1)
            t = p * dp
            delta = jnp.sum(t, axis=1, keepdims=True)
            dsb.append((t - p * delta).astype(BF16))
            pb.append(p.astype(BF16))
            dsink = -jnp.sum(es * inv * delta, axis=0, keepdims=True)
            dsink_row = dsink_row + jnp.where(lane_row == h, dsink, 0.0)
        dq_parts = [lax.dot_general(d, k_halves[kv_of(c)][half], NN, preferred_element_type=F32) for (c, half), d in zip(heads, dsb)]
        dk_parts = [lax.dot_general(d, q_halves[c][half], TN, preferred_element_type=F32) for (c, half), d in zip(heads, dsb)]
        dv_parts = [lax.dot_general(p, do_halves[c][half], TN, preferred_element_type=F32) for (c, half), p in zip(heads, pb)]
        dq = jnp.concatenate([(dq_parts[2 * c] + dq_parts[2 * c + 1]) * ATTN_SCALE for c in range(N_CHUNK)], axis=1)

        def kv_sum(parts, hk):
            acc = (parts[GROUP * hk] + parts[GROUP * hk + 1]) + (parts[GROUP * hk + 2] + parts[GROUP * hk + 3])
            return acc + pltpu.roll(acc, HALF, 1)

        for pair in range(N_KV // 2):
            dkbuf[:, pair * 128:(pair + 1) * 128] = jnp.where(lo_kv, kv_sum(dk_parts, 2 * pair), kv_sum(dk_parts, 2 * pair + 1))
            dvbuf[:, pair * 128:(pair + 1) * 128] = jnp.where(lo_kv, kv_sum(dv_parts, 2 * pair), kv_sum(dv_parts, 2 * pair + 1))
        prev0 = pl.multiple_of(jnp.maximum(n - 1, 0) * BLOCK, BLOCK)
        cur0 = pl.multiple_of(n * BLOCK, BLOCK)

        @pl.when(n > 0)
        def _():
            dk_ref[pl.ds(prev0, BLOCK), :] += dkbuf[0:BLOCK, :]
            dv_ref[pl.ds(prev0, BLOCK), :] += dvbuf[0:BLOCK, :]

        dk_ref[pl.ds(cur0, BLOCK), :] += dkbuf[BLOCK:2 * BLOCK, :]
        dv_ref[pl.ds(cur0, BLOCK), :] += dvbuf[BLOCK:2 * BLOCK, :]
        ds_ref[...] += dsink_row

        def window(p, at):
            return dproj_ref.at[pl.ds(pl.multiple_of(at * BLOCK, BLOCK), BLOCK), pl.ds(C_Q, D)]

        _write_behind(n, nblk, dqout, sems, (_rope_t(dq, tc_ref[...]).astype(BF16),), window, n)

    blk = lambda w: pl.BlockSpec((BLOCK, w), lambda n: (n, 0))
    whole = lambda w: pl.BlockSpec((s, w), lambda n: (0, 0))
    n_in = 1 + len(_attn_in_specs())
    return _call(
        body, name=name, grid=(nblk,), in_specs=[blk(D)] + _attn_in_specs() + [HBM_SPEC] * (1 + len(after)),
        out_specs=[HBM_SPEC, whole(D_KV), whole(D_KV), pl.BlockSpec((1, 128), lambda n: (0, 0))],
        out_shape=[_sds((s, N_IN), BF16), _sds((s, D_KV), F32), _sds((s, D_KV), F32), _sds((1, 128), F32)],
        scratch_shapes=[pltpu.VMEM((2, 1, BLOCK, D), BF16), pltpu.VMEM((2 * BLOCK, D_KV), F32),
                        pltpu.VMEM((2 * BLOCK, D_KV), F32), pltpu.SemaphoreType.DMA((2, 1))],
        input_output_aliases={n_in: 0}, compiler_params=_params("arbitrary"),
    )(do, proj, proj, proj, proj, proj, tab, tab, sinks, dproj, *after)


def _kv_bwd(dkr, dv, tab, dproj, name):
    s = dkr.shape[0]
    tm = _row_tile(s)

    def body(dk_ref, dv_ref, t_ref, dproj_in, o_ref):
        del dproj_in
        o_ref[:, 0:D_KV] = _rope_t(dk_ref[...], t_ref[...]).astype(BF16)
        o_ref[:, D_KV:2 * D_KV] = dv_ref[...].astype(BF16)

    row = lambda w: pl.BlockSpec((tm, w), lambda i: (i, 0))
    return _call(
        body, name=name, grid=(s // tm,),
        in_specs=[row(D_KV), row(D_KV), row(384), pl.BlockSpec(memory_space=pl.ANY)],
        out_specs=pl.BlockSpec((tm, 2 * D_KV), lambda i: (i, C_K // (2 * D_KV))),
        out_shape=_sds((s, N_IN), BF16), input_output_aliases={3: 0}, compiler_params=_params("parallel"),
    )(dkr, dv, tab, dproj)


EW_TC = 512


def _sigmoid(x):
    return 0.5 * jnp.tanh(0.5 * x) + 0.5


def _merge_fwd(proj, conv_out, attn_out, name):
    s = proj.shape[0]
    tm = _row_tile(s)
    tile = pl.BlockSpec((tm, EW_TC), lambda i, j: (i, j))

    def body(gc_ref, ga_ref, co_ref, ao_ref, o_ref):
        o_ref[...] = (_sigmoid(gc_ref[...].astype(F32)) * co_ref[...].astype(F32)
                      + _sigmoid(ga_ref[...].astype(F32)) * ao_ref[...].astype(F32)).astype(BF16)

    return _call(
        body, name=name, grid=(s // tm, D // EW_TC),
        in_specs=[pl.BlockSpec((tm, EW_TC), lambda i, j: (i, C_GC // EW_TC + j)),
                  pl.BlockSpec((tm, EW_TC), lambda i, j: (i, C_GA // EW_TC + j)), tile, tile],
        out_specs=tile, out_shape=_sds((s, D), BF16), compiler_params=_params("parallel", "parallel"),
    )(proj, proj, conv_out, attn_out)


def _merge_bwd(dmerged, proj, conv_out, attn_out, name):
    s = proj.shape[0]
    tm = _row_tile(s)
    tile = pl.BlockSpec((tm, EW_TC), lambda i, j: (i, j))
    anyspec = pl.BlockSpec(memory_space=pl.ANY)

    def body(dm_ref, gc_ref, ga_ref, co_ref, ao_ref, dproj_ref, dco_ref, dao_ref, buf, sems):
        i, j = pl.program_id(0), pl.program_id(1)
        dm = dm_ref[...].astype(F32)
        sc = _sigmoid(gc_ref[...].astype(F32))
        sa = _sigmoid(ga_ref[...].astype(F32))
        dco_ref[...] = (dm * sc).astype(BF16)
        dao_ref[...] = (dm * sa).astype(BF16)
        tiles = ((dm * co_ref[...].astype(F32) * sc * (1.0 - sc)).astype(BF16),
                 (dm * ao_ref[...].astype(F32) * sa * (1.0 - sa)).astype(BF16))

        def window(p, at):
            start = pl.multiple_of((C_GC, C_GA)[p] + at[1] * EW_TC, EW_TC)
            return dproj_ref.at[pl.ds(pl.multiple_of(at[0] * tm, tm), tm), pl.ds(start, EW_TC)]

        _write_behind(i * nj + j, (s // tm) * nj, buf, sems, tiles, window, (i, j))

    nj = D // EW_TC
    return _call(
        body, name=name, grid=(s // tm, nj),
        in_specs=[tile, pl.BlockSpec((tm, EW_TC), lambda i, j: (i, C_GC // EW_TC + j)),
                  pl.BlockSpec((tm, EW_TC), lambda i, j: (i, C_GA // EW_TC + j)), tile, tile],
        out_specs=[anyspec, tile, tile],
        out_shape=[_sds((s, N_IN), BF16), _sds((s, D), BF16), _sds((s, D), BF16)],
        scratch_shapes=[pltpu.VMEM((2, 2, tm, EW_TC), BF16), pltpu.SemaphoreType.DMA((2, 2))],
        compiler_params=_params("arbitrary", "arbitrary"),
    )(dmerged, proj, proj, conv_out, attn_out)


FF_TC = 256
FF_TM = 2048


def _row_pipeline(tm, matmul, finish, split=ROW_SPLIT):
    step = tm // split
    pending = None
    for r in range(split):
        rows = pl.ds(r * step, step)
        result = matmul(rows)
        if pending is not None:
            finish(*pending)
        pending = (rows, result)
    finish(*pending)


def _gate_up_fwd(h2, wgu_t, name):
    s = h2.shape[0]
    tm = min(FF_TM, s)
    nb = D_FF // FF_TC

    def body(h_ref, wg_ref, wu_ref, a_ref, g_ref, u_ref):
        def matmuls(rows):
            h = h_ref[rows, :]
            return (lax.dot_general(h, wg_ref[...], NT, preferred_element_type=F32),
                    lax.dot_general(h, wu_ref[...], NT, preferred_element_type=F32))

        def finish(rows, gu):
            g, u = gu
            a_ref[rows, :] = (g * _sigmoid(g) * u).astype(BF16)
            g_ref[rows, :] = g.astype(BF16)
            u_ref[rows, :] = u.astype(BF16)

        _row_pipeline(tm, matmuls, finish)

    tile = pl.BlockSpec((tm, FF_TC), lambda j, i: (i, j))
    return _call(
        body, name=name, grid=(nb, s // tm),
        in_specs=[pl.BlockSpec((tm, D), lambda j, i: (i, 0)), pl.BlockSpec((FF_TC, D), lambda j, i: (j, 0)),
                  pl.BlockSpec((FF_TC, D), lambda j, i: (nb + j, 0))],
        out_specs=[tile, tile, tile], out_shape=[_sds((s, D_FF), BF16)] * 3,
        compiler_params=_params("parallel", "parallel"),
    )(h2, wgu_t, wgu_t)


def _down_bwd_x(dx2b, wd, gate, up, name):
    s = dx2b.shape[0]
    tm = min(FF_TM, s)
    nb = D_FF // FF_TC

    def body(dx_ref, w_ref, g_ref, u_ref, dg_ref, du_ref):
        def matmul(rows):
            return lax.dot_general(dx_ref[rows, :], w_ref[...], NT, preferred_element_type=F32)

        def finish(rows, da):
            g = g_ref[rows, :].astype(F32)
            sg = _sigmoid(g)
            dg_ref[rows, :] = (da * u_ref[rows, :].astype(F32) * (sg * (1.0 + g * (1.0 - sg)))).astype(BF16)
            du_ref[rows, :] = (da * (g * sg)).astype(BF16)

        _row_pipeline(tm, matmul, finish)

    tile = pl.BlockSpec((tm, FF_TC), lambda j, i: (i, j))
    return _call(
        body, name=name, grid=(nb, s // tm),
        in_specs=[pl.BlockSpec((tm, D), lambda j, i: (i, 0)), pl.BlockSpec((FF_TC, D), lambda j, i: (j, 0)), tile, tile],
        out_specs=[tile, tile], out_shape=[_sds((s, D_FF), BF16)] * 2,
        compiler_params=_params("parallel", "parallel"),
    )(dx2b, wd, gate, up)


class _Weights:
    def __init__(self, **groups):
        self.groups = groups

    def begin(self, group, after):
        return ()

    def end(self, group, after):
        return self.groups[group]


class _NoReduce:
    def start(self, group, grads):
        return ()

    def middle(self, group, after):
        return ()


def _local_step(x, tgt, g_mix, g_ffn, g_final, sinks, weights, reducer=None, after=()):
    reducer = reducer or _NoReduce()
    s = x.shape[0]
    tab = _rope_tables(s)
    big = dict(tm=2048, tn=512, tk=1024)
    h1 = _rms_fwd(x, g_mix, "rms1_fwd", after=after)
    win_t, conv_w = weights.end("in", weights.begin("in", (h1,)))
    proj = _matmul(h1, win_t, mode="nt", out_dtype=BF16, name="proj_fwd", tm=2048, tn=512, tk=1024)
    attn = _swa_fwd(proj, tab, sinks, "attn_fwd", after=weights.begin("mix", (proj,)))
    wco, wao, wo = weights.end("mix", (attn,))
    conv_y = _conv_fwd(proj, conv_w, "conv_fwd")
    conv_out = _matmul(conv_y, wco, mode="nn", out_dtype=BF16, name="conv_out_fwd", **big)
    attn_out = _matmul(attn, wao, mode="nn", out_dtype=BF16, name="attn_out_fwd", **big)
    merged = _merge_fwd(proj, conv_out, attn_out, "merge_fwd")
    x1 = _matmul(merged, wo, mode="nn", out_dtype=F32, name="wo_fwd", res=x, after=weights.begin("ffn", (merged,)), **big)
    h2 = _rms_fwd(x1, g_ffn, "rms2_fwd")
    wgu_t, wd = weights.end("ffn", (h2,))
    act, gate, up = _gate_up_fwd(h2, wgu_t, "gate_up_fwd")
    x2 = _matmul(act, wd, mode="nn", out_dtype=F32, name="down_fwd", res=x1, tm=1024, tn=512, tk=D_FF)
    dx2, dx2b, dg_final, lossvec = _loss_head(x2, g_final, tgt, "loss_head")
    dgate, dup = _down_bwd_x(dx2b, wd, gate, up, "down_bwd_x")
    g_wd = _matmul(act, dx2b, mode="tn", out_dtype=BF16, name="down_bwd_w", tm=1408, tn=1024, tk=2048)
    dh2 = _matmul([dgate, dup], wgu_t, mode="nn", out_dtype=F32, name="gate_up_bwd_x", tm=1024, tn=1024, tk=1408)
    g_wgu_t = _matmul([dgate, dup], h2, mode="tn", out_dtype=BF16, name="gate_up_bwd_w", tm=1408, tn=1024, tk=2048)
    after_ffn = reducer.start("ffn", dict(wgu_t=g_wgu_t, wd=g_wd))
    dx1, dx1b, dg_ffn = _rms_bwd(dh2, x1, g_ffn, dx2, "rms2_bwd")
    dmerged = _matmul(dx1b, wo, mode="nt", out_dtype=BF16, name="wo_bwd_x", after=after_ffn, **big)
    after_ffn = reducer.middle("ffn", (dmerged,))
    g_wo = _matmul(merged, dx1b, mode="tn", out_dtype=BF16, name="wo_bwd_w", tm=512, tn=1024, tk=2048, after=after_ffn)
    dproj, dco, dao = _merge_bwd(dmerged, proj, conv_out, attn_out, "merge_bwd")
    dconv_y = _matmul(dco, wco, mode="nt", out_dtype=BF16, name="conv_out_bwd_x", **big)
    g_wco = _matmul(conv_y, dco, mode="tn", out_dtype=BF16, name="conv_out_bwd_w", tm=512, tn=1024, tk=2048)
    dattn = _matmul(dao, wao, mode="nt", out_dtype=BF16, name="attn_out_bwd_x", **big)
    g_wao = _matmul(attn, dao, mode="tn", out_dtype=BF16, name="attn_out_bwd_w", tm=512, tn=1024, tk=2048)
    after_mix = reducer.start("mix", dict(wco=g_wco, wao=g_wao, wo=g_wo))
    dproj, dconv_w = _conv_bwd(dconv_y, proj, conv_w, dproj, "conv_bwd", after=after_mix)
    after_mix = reducer.middle("mix", (dconv_w,))
    dproj, dkr, dv, dsinks = _swa_bwd(dattn, proj, tab, sinks, dproj, "attn_bwd", after=after_mix)
    dproj = _kv_bwd(dkr, dv, tab, dproj, "kv_bwd")
    g_win_t = _matmul(dproj, h1, mode="tn", out_dtype=BF16, name="proj_bwd_w", tm=512, tn=1024, tk=2048)
    after_in = reducer.middle("in", reducer.start("in", dict(win_t=g_win_t)))
    dh1 = _matmul(dproj, win_t, mode="nn", out_dtype=F32, name="proj_bwd_x", tm=1024, tn=1024, tk=1664, after=after_in)
    dx, _, dg_mix = _rms_bwd(dh1, x, g_mix, dx1, "rms1_bwd")
    grads = dict(win_t=g_win_t, wgu_t=g_wgu_t, wd=g_wd, wco=g_wco, wao=g_wao, wo=g_wo)
    small = dict(g_mix=dg_mix, g_ffn=dg_ffn, g_final=dg_final, conv_w=dconv_w, sinks=dsinks, lossvec=lossvec)
    return dx, grads, small


def _position():
    return lax.axis_index("x"), lax.axis_index("y"), lax.axis_index("c")


def _other_chips(x, y):
    return [(1 - x, y), (x, 1 - y), (1 - x, 1 - y)]


SEM_SPEC = pl.BlockSpec(memory_space=pltpu.SEMAPHORE)
EFFECT = pltpu.SideEffectType.DATAFLOW_SIDE_EFFECTING
TOKEN = jax.ShapeDtypeStruct((8, 128), F32)
TOKEN_SPEC = pl.BlockSpec(memory_space=pltpu.VMEM)


def _hbm(a):
    return pltpu.with_memory_space_constraint(a, pltpu.HBM)


def _place(w, me_idx, dtype, name, after=()):
    r, cdim = w.shape

    def body(i_ref, w_ref, *rest):
        rest[-1][...] = w_ref[...].astype(dtype)

    grid_spec = pltpu.PrefetchScalarGridSpec(
        num_scalar_prefetch=1, grid=(1,), in_specs=[pl.BlockSpec((r, cdim), lambda i, me: (0, 0))] + [HBM_SPEC] * len(after),
        out_specs=pl.BlockSpec((r, cdim), lambda i, me: (me[0], 0)))
    return _call(body, name=name, grid_spec=grid_spec, out_shape=_sds((N_DEV * r, cdim), dtype),
                 compiler_params=_params("arbitrary"))(me_idx, w, *after)


def _own_rows(ref, r, px, py, pc):
    return ref.at[pl.ds((4 * px + 2 * py + pc) * r, r), :]


def _gather_phase(bufs, waits, plans, after, name):
    n = len(bufs)
    rows = [b.shape[0] // N_DEV for b in bufs]
    nw, npl = len(waits), len(plans)

    def body(*refs):
        ins = refs[:n]
        wait_sems = refs[n:n + 2 * nw]
        out0 = n + 2 * nw + len(after)
        new_sems = refs[out0:out0 + 2 * npl]
        token = refs[-1]
        x, y, c = _position()
        for w, (_, _, sent, received) in enumerate(waits):
            for a in range(n):
                for count, wait in ((sent, "wait_send"), (received, "wait_recv")):
                    span = _whole(ins[a], count * rows[a])
                    getattr(pltpu.make_async_remote_copy(
                        src_ref=span, dst_ref=span, send_sem=wait_sems[2 * w].at[a], recv_sem=wait_sems[2 * w + 1].at[a],
                        device_id=(x, y, c), device_id_type=MESH), wait)()
        for k, plan in enumerate(plans):
            for a in range(n):
                for block, target in plan(x, y, c):
                    span = _own_rows(ins[a], rows[a], *block)
                    pltpu.make_async_remote_copy(src_ref=span, dst_ref=span, send_sem=new_sems[2 * k].at[a],
                                                 recv_sem=new_sems[2 * k + 1].at[a], device_id=target, device_id_type=MESH).start()
        token[...] = jnp.zeros_like(token)

    sem_ops = [s for send, recv, _, _ in waits for s in (send, recv)]
    outs = _call(
        body, name=name, in_specs=[HBM_SPEC] * n + [SEM_SPEC] * (2 * nw) + [HBM_SPEC] * len(after),
        out_specs=[SEM_SPEC] * (2 * npl) + [HBM_SPEC] * n + [TOKEN_SPEC],
        out_shape=[pltpu.SemaphoreType.DMA((n,))] * (2 * npl) + [pltpu.HBM(b.shape, b.dtype) for b in bufs] + [TOKEN],
        input_output_aliases={i: 2 * npl + i for i in range(n)},
        compiler_params=pltpu.CompilerParams(has_side_effects=EFFECT),
    )(*[_hbm(b) for b in bufs], *sem_ops, *after)
    pairs = [(outs[2 * k], outs[2 * k + 1]) for k in range(npl)]
    return pairs, list(outs[2 * npl:2 * npl + n]), outs[-1]


def _own_to_near(x, y, c):
    return [((x, y, c), (x, y, 1 - c)), ((x, y, c), (1 - x, y, c)), ((x, y, c), (x, 1 - y, c))]


def _near_to_sibling(x, y, c):
    return [((1 - x, y, c), (x, y, 1 - c)), ((x, 1 - y, c), (x, y, 1 - c))]


def _relay_diagonal(x, y, c):
    north = c
    source = (x * north + (1 - x) * (1 - north), (1 - y) * north + y * (1 - north), c)
    target = ((1 - x) * north + x * (1 - north), y * north + (1 - y) * (1 - north), c)
    return [(source, target)]


def _diagonal_to_sibling(x, y, c):
    return [((1 - x, 1 - y, c), (x, y, 1 - c))]


def _gather_start(bufs, groups, name):
    n = len(bufs)
    rows = [b.shape[0] // N_DEV for b in bufs]
    ng = len(groups)

    def body(*refs):
        ins = refs[:n]
        sems = refs[n:n + 2 * ng]
        token = refs[-1]
        x, y, c = _position()
        targets = [(x, y, 1 - c)] + [(*chip, c) for chip in _other_chips(x, y)]
        for g, members in enumerate(groups):
            for slot, a in enumerate(members):
                own = _own_rows(ins[a], rows[a], x, y, c)
                for to in targets:
                    pltpu.make_async_remote_copy(src_ref=own, dst_ref=own, send_sem=sems[2 * g].at[slot],
                                                 recv_sem=sems[2 * g + 1].at[slot], device_id=to, device_id_type=MESH).start()
        token[...] = jnp.zeros_like(token)

    sem_shapes = []
    for members in groups:
        sem_shapes += [pltpu.SemaphoreType.DMA((len(members),))] * 2
    outs = _call(
        body, name=name, in_specs=[HBM_SPEC] * n, out_specs=[SEM_SPEC] * (2 * ng) + [HBM_SPEC] * n + [TOKEN_SPEC],
        out_shape=sem_shapes + [pltpu.HBM(b.shape, b.dtype) for b in bufs] + [TOKEN],
        input_output_aliases={i: 2 * ng + i for i in range(n)},
        compiler_params=pltpu.CompilerParams(has_side_effects=EFFECT),
    )(*[_hbm(b) for b in bufs])
    sem_pairs = [(outs[2 * g], outs[2 * g + 1]) for g in range(ng)]
    return sem_pairs, list(outs[2 * ng:2 * ng + n]), outs[-1]


def _gather_forward(send_sems, recv_sems, bufs, after, name):
    n = len(bufs)
    rows = [b.shape[0] // N_DEV for b in bufs]

    def body(*refs):
        ins = refs[:n]
        send1, recv1 = refs[n], refs[n + 1]
        out0 = n + 2 + len(after)
        send2, recv2 = refs[out0], refs[out0 + 1]
        token = refs[-1]
        x, y, c = _position()
        for a in range(n):
            step1 = pltpu.make_async_remote_copy(
                src_ref=_whole(ins[a], 4 * rows[a]), dst_ref=_whole(ins[a], 4 * rows[a]), send_sem=send1.at[a],
                recv_sem=recv1.at[a], device_id=(x, y, c), device_id_type=MESH)
            step1.wait_send()
            step1.wait_recv()
        for a in range(n):
            for chip in _other_chips(x, y):
                blk = _own_rows(ins[a], rows[a], *chip, c)
                pltpu.make_async_remote_copy(src_ref=blk, dst_ref=blk, send_sem=send2.at[a], recv_sem=recv2.at[a],
                                             device_id=(x, y, 1 - c), device_id_type=MESH).start()
        token[...] = jnp.zeros_like(token)

    outs = _call(
        body, name=name, in_specs=[HBM_SPEC] * n + [SEM_SPEC, SEM_SPEC] + [HBM_SPEC] * len(after),
        out_specs=[SEM_SPEC, SEM_SPEC] + [HBM_SPEC] * n + [TOKEN_SPEC],
        out_shape=[pltpu.SemaphoreType.DMA((n,)), pltpu.SemaphoreType.DMA((n,))]
        + [pltpu.HBM(b.shape, b.dtype) for b in bufs] + [TOKEN],
        input_output_aliases={i: 2 + i for i in range(n)},
        compiler_params=pltpu.CompilerParams(has_side_effects=EFFECT),
    )(*bufs, send_sems, recv_sems, *after)
    return outs[0], outs[1], list(outs[2:2 + n]), outs[-1]


def _gather_done(send_sems, recv_sems, bufs, after, name):
    n = len(bufs)
    rows = [b.shape[0] // N_DEV for b in bufs]

    def body(*refs):
        ins = refs[:n]
        send2, recv2 = refs[n], refs[n + 1]
        x, y, c = _position()
        for a in range(n):
            step2 = pltpu.make_async_remote_copy(
                src_ref=_whole(ins[a], 3 * rows[a]), dst_ref=_whole(ins[a], 3 * rows[a]), send_sem=send2.at[a],
                recv_sem=recv2.at[a], device_id=(x, y, c), device_id_type=MESH)
            step2.wait_send()
            step2.wait_recv()

    outs = _call(
        body, name=name, in_specs=[HBM_SPEC] * n + [SEM_SPEC, SEM_SPEC] + [HBM_SPEC] * len(after),
        out_specs=[HBM_SPEC] * n, out_shape=[pltpu.HBM(b.shape, b.dtype) for b in bufs],
        input_output_aliases={i: i for i in range(n)},
        compiler_params=pltpu.CompilerParams(has_side_effects=EFFECT),
    )(*bufs, send_sems, recv_sems, *after)
    return list(outs)


def _whole(ref, nrows):
    return ref.at[pl.ds(0, nrows), :]


def _to_sibling(x, y, c):
    return [(2 * q + (1 - c), q, (x, y, 1 - c)) for q in range(4)]


def _to_chips(x, y, c):
    return [(2 * px + py, j, (px, py, c)) for j, (px, py) in enumerate(_other_chips(x, y))]


def _exchange_start(srcs, src_slots, plan, name):
    n = len(srcs)
    rows = [a.shape[0] // src_slots for a in srcs]
    n_copies = len(plan(0, 0, 0))
    lands = [lax.empty((n_copies * r, a.shape[1]), a.dtype) for a, r in zip(srcs, rows)]

    def body(*refs):
        ins, land_refs = refs[:n], refs[n:2 * n]
        send_sems, recv_sems = refs[2 * n], refs[2 * n + 1]
        token = refs[-1]
        for a in range(n):
            r = rows[a]
            for src_slot, dst_slot, target in plan(*_position()):
                pltpu.make_async_remote_copy(
                    src_ref=ins[a].at[pl.ds(src_slot * r, r), :], dst_ref=land_refs[a].at[pl.ds(dst_slot * r, r), :],
                    send_sem=send_sems.at[a], recv_sem=recv_sems.at[a], device_id=target, device_id_type=MESH).start()
        token[...] = jnp.zeros_like(token)

    outs = _call(
        body, name=name, in_specs=[HBM_SPEC] * (2 * n),
        out_specs=[SEM_SPEC, SEM_SPEC] + [HBM_SPEC] * (2 * n) + [TOKEN_SPEC],
        out_shape=[pltpu.SemaphoreType.DMA((n,)), pltpu.SemaphoreType.DMA((n,))]
        + [pltpu.HBM(a.shape, a.dtype) for a in srcs] + [pltpu.HBM(l.shape, l.dtype) for l in lands] + [TOKEN],
        input_output_aliases={i: 2 + i for i in range(2 * n)},
        compiler_params=pltpu.CompilerParams(has_side_effects=EFFECT),
    )(*[_hbm(a) for a in srcs], *[_hbm(l) for l in lands])
    return outs[0], outs[1], list(outs[2:2 + n]), list(outs[2 + n:2 + 2 * n]), outs[-1]


def _exchange_wait(send_sems, recv_sems, srcs, lands, after, name):
    n = len(srcs)

    def body(*refs):
        ins, land_refs = refs[:n], refs[n:2 * n]
        send_sems_ref, recv_sems_ref = refs[2 * n], refs[2 * n + 1]
        for a in range(n):
            allrows = lands[a].shape[0]
            cp = pltpu.make_async_remote_copy(
                src_ref=_whole(ins[a], allrows), dst_ref=_whole(land_refs[a], allrows), send_sem=send_sems_ref.at[a],
                recv_sem=recv_sems_ref.at[a], device_id=_position(), device_id_type=MESH)
            cp.wait_send()
            cp.wait_recv()

    outs = _call(
        body, name=name, in_specs=[HBM_SPEC] * (2 * n) + [SEM_SPEC, SEM_SPEC] + [HBM_SPEC] * len(after),
        out_specs=[HBM_SPEC] * (2 * n),
        out_shape=[pltpu.HBM(a.shape, a.dtype) for a in srcs] + [pltpu.HBM(l.shape, l.dtype) for l in lands],
        input_output_aliases={i: i for i in range(2 * n)},
        compiler_params=pltpu.CompilerParams(has_side_effects=EFFECT),
    )(*srcs, *lands, send_sems, recv_sems, *after)
    return list(outs[:n]), list(outs[n:])


def _chip_partial(grad, recv, idx, name):
    r = recv.shape[0] // 4

    def body(i_ref, g_ref, s_ref, o_ref):
        del i_ref
        o_ref[...] = (g_ref[...].astype(F32) + s_ref[...].astype(F32)).astype(BF16)

    nb = 1
    tr = r // nb
    grid_spec = pltpu.PrefetchScalarGridSpec(
        num_scalar_prefetch=1, grid=(3, nb),
        in_specs=[pl.BlockSpec((tr, D), lambda t, i, i_ref: ((2 * i_ref[1 + t] + i_ref[0]) * nb + i, 0)),
                  pl.BlockSpec((tr, D), lambda t, i, i_ref: (i_ref[1 + t] * nb + i, 0))],
        out_specs=pl.BlockSpec((tr, D), lambda t, i, i_ref: (i_ref[1 + t] * nb + i, 0)))
    return _call(body, name=name, grid_spec=grid_spec, out_shape=_sds((4 * r, D), BF16),
                 compiler_params=_params("arbitrary", "arbitrary"))(idx, grad, recv)


def _adamw_math(w, g, m, v):
    m2 = B1 * m + (1.0 - B1) * g
    v2 = B2 * v + (1.0 - B2) * jnp.square(g)
    m_hat = m2 / (1.0 - B1 ** STEP)
    v_hat = v2 / (1.0 - B2 ** STEP)
    return -LR * (m_hat / (jnp.sqrt(v_hat) + EPS_ADAM) + WD * w), m2, v2


def _reduce_adamw(w, grad, from_sibling, from_chips, idx, m, v, name):
    r = w.shape[0]
    assert grad.shape == (N_DEV * r, D) and from_sibling.shape == (4 * r, D) and from_chips.shape == (3 * r, D)
    tr = r // 2
    nb = r // tr

    def body(i_ref, w_ref, p_ref, s_ref, r0_ref, r1_ref, r2_ref, m_ref, v_ref, g_ref, d_ref, nm_ref, nv_ref):
        del i_ref
        g = p_ref[...].astype(F32) + s_ref[...].astype(F32)
        g = ((g + r0_ref[...].astype(F32)) + r1_ref[...].astype(F32)) + r2_ref[...].astype(F32)
        g_ref[...] = g
        d_ref[...], nm_ref[...], nv_ref[...] = _adamw_math(w_ref[...], g, m_ref[...], v_ref[...])

    own = pl.BlockSpec((tr, D), lambda i, i_ref: (i, 0))
    grid_spec = pltpu.PrefetchScalarGridSpec(
        num_scalar_prefetch=1, grid=(nb,),
        in_specs=[own, pl.BlockSpec((tr, D), lambda i, i_ref: (i_ref[0] * nb + i, 0)),
                  pl.BlockSpec((tr, D), lambda i, i_ref: (i_ref[1] * nb + i, 0))]
        + [pl.BlockSpec((tr, D), lambda i, i_ref, j=j: (j * nb + i, 0)) for j in range(3)] + [own, own],
        out_specs=[own] * 4)
    return _call(body, name=name, grid_spec=grid_spec, out_shape=[_sds((r, D), F32)] * 4,
                 compiler_params=_params("parallel"))(idx, w, grad, from_sibling, from_chips, from_chips, from_chips, m, v)


SMALL_ROWS = 8


def _small_all_reduce(pack, name, after=()):
    def body(p_ref, *rest):
        tot_ref, loss_ref, gath, send_sems, recv_sems = rest[len(after):]
        x, y, c = _position()
        me_id = 4 * x + 2 * y + c
        gath[me_id] = p_ref[...]
        copies = []
        for k in range(1, N_DEV):
            peer = tuple(1 - v if (k >> b) & 1 else v for v, b in ((x, 2), (y, 1), (c, 0)))
            cp = pltpu.make_async_remote_copy(src_ref=p_ref, dst_ref=gath.at[me_id], send_sem=send_sems.at[k - 1],
                                              recv_sem=recv_sems.at[k - 1], device_id=peer, device_id_type=MESH)
            cp.start()
            copies.append(cp)
        for cp in copies:
            cp.wait_recv()
        for cp in copies:
            cp.wait_send()
        tot = gath[0]
        for d in range(1, N_DEV):
            tot = tot + gath[d]
        tot_ref[...] = tot
        loss_ref[...] = jnp.full((1, 128), (0.5 / D) * jnp.sum(tot[SMALL_ROWS - 1:SMALL_ROWS, :]), F32)

    vm = pl.BlockSpec(memory_space=pltpu.VMEM)
    return _call(
        body, name=name, in_specs=[vm] + [HBM_SPEC] * len(after), out_specs=[vm, vm],
        out_shape=[_sds((SMALL_ROWS, D), F32), _sds((1, 128), F32)],
        scratch_shapes=[pltpu.VMEM((N_DEV, SMALL_ROWS, D), F32), pltpu.SemaphoreType.DMA((N_DEV - 1,)),
                        pltpu.SemaphoreType.DMA((N_DEV - 1,))],
    )(pack, *after)


def _adamw_small(ws, gs, ms, vs, name):
    n = len(ws)

    def body(*refs):
        for a in range(n):
            w_ref, g_ref, m_ref, v_ref = (refs[k * n + a] for k in range(4))
            d_ref, nm_ref, nv_ref = (refs[(4 + k) * n + a] for k in range(3))
            d_ref[...], nm_ref[...], nv_ref[...] = _adamw_math(w_ref[...], g_ref[...], m_ref[...], v_ref[...])

    vm = pl.BlockSpec(memory_space=pltpu.VMEM)
    outs = _call(body, name=name, in_specs=[vm] * (4 * n), out_specs=[vm] * (3 * n),
                 out_shape=[_sds(w.shape, F32) for w in ws] * 3)(*ws, *gs, *ms, *vs)
    return [(outs[a], outs[n + a], outs[2 * n + a]) for a in range(n)]


def kernel(x, g_mix, w_in, conv_w, attn_sinks, w_conv_out, w_attn_out, w_o, g_ffn, w_gate_up, w_down, g_final, loss_target, m_g_mix, m_w_in, m_conv_w, m_attn_sinks, m_w_conv_out, m_w_attn_out, m_w_o, m_g_ffn, m_w_gate_up, m_w_down, m_g_final, v_g_mix, v_w_in, v_conv_w, v_attn_sinks, v_w_conv_out, v_w_attn_out, v_w_o, v_g_ffn, v_w_gate_up, v_w_down, v_g_final):
    cx, cy, cc = _position()
    chip = 2 * cx + cy
    partial_idx = jnp.stack([cc, 2 * (1 - cx) + cy, 2 * cx + (1 - cy), 2 * (1 - cx) + (1 - cy)]).astype(jnp.int32)
    own_idx = jnp.stack([2 * chip + cc, chip]).astype(jnp.int32)
    me = 4 * cx + 2 * cy + cc

    me_idx = jnp.reshape(me, (1,)).astype(jnp.int32)
    first = [_place(jnp.transpose(w_in[0]), me_idx, BF16, "place_w_in"),
             _place(jnp.pad(conv_w[0], ((0, 5), (0, 0))), me_idx, F32, "place_conv_w")]
    (to_near,), first, token_in = _gather_phase(first, [], [_own_to_near], (), "gather_in_start")
    gather_tokens = (token_in,)

    class Gathered:
        def __init__(self):
            self.state = {}

        def begin(self, group, after):
            if group == "in":
                (near, relay), bufs, token = _gather_phase(
                    first, [(*to_near, 3, 3)], [_near_to_sibling, _relay_diagonal], after, "gather_in_relay")
                later = [_place(w, me_idx, BF16, "place_" + k, after=(token,)) for k, w in (
                    ("w_conv_out", w_conv_out[0]), ("w_attn_out", w_attn_out[0]), ("w_o", w_o[0]),
                    ("w_gate_up", jnp.transpose(w_gate_up[0])), ("w_down", w_down[0]))]
                (sems_mix, sems_ffn), later, token_later = _gather_start(later, [[0, 1, 2], [3, 4]], "gather_start_later")
                self.state.update({"in": (near, relay, bufs), "mix": (sems_mix, later[:3]), "ffn": (sems_ffn, later[3:])})
                return (token_later,)
            (send_sems, recv_sems), group_bufs = self.state[group]
            send2, recv2, group_bufs, token = _gather_forward(send_sems, recv_sems, group_bufs, after, "gather_forward_" + group)
            self.state[group] = ((send2, recv2), group_bufs)
            return (token,)

        def end(self, group, after):
            if group == "in":
                near, relay, bufs = self.state[group]
                (last,), bufs, token = _gather_phase(bufs, [(*relay, 1, 1)], [_diagonal_to_sibling], after, "gather_in_last")
                _, full, _ = _gather_phase(bufs, [(*near, 2, 2), (*last, 1, 1)], [], (token,), "gather_in_done")
                return full[0], jnp.transpose(full[1].reshape(N_DEV, 8, 128)[:, :3, :], (1, 0, 2)).reshape(3, D)
            (send2, recv2), group_bufs = self.state[group]
            return _gather_done(send2, recv2, group_bufs, after, "gather_done_" + group)

    in_flight, own_pieces = {}, {}

    transposed = ("w_in", "w_gate_up")

    def as2d(k, a):
        if k in transposed:
            return jnp.transpose(a[0])
        return a[None] if a.ndim == 1 else (a[0] if a.ndim == 3 else a)

    w_all = {"g_mix": g_mix, "w_in": w_in, "conv_w": conv_w, "attn_sinks": attn_sinks, "w_conv_out": w_conv_out,
             "w_attn_out": w_attn_out, "w_o": w_o, "g_ffn": g_ffn, "w_gate_up": w_gate_up, "w_down": w_down, "g_final": g_final}
    m_all = {"g_mix": m_g_mix, "w_in": m_w_in, "conv_w": m_conv_w, "attn_sinks": m_attn_sinks, "w_conv_out": m_w_conv_out,
             "w_attn_out": m_w_attn_out, "w_o": m_w_o, "g_ffn": m_g_ffn, "w_gate_up": m_w_gate_up, "w_down": m_w_down,
             "g_final": m_g_final}
    v_all = {"g_mix": v_g_mix, "w_in": v_w_in, "conv_w": v_conv_w, "attn_sinks": v_attn_sinks, "w_conv_out": v_w_conv_out,
             "w_attn_out": v_w_attn_out, "w_o": v_w_o, "g_ffn": v_g_ffn, "w_gate_up": v_w_gate_up, "w_down": v_w_down,
             "g_final": v_g_final}
    results = {}

    def record(k, *vals):
        results[k] = [(jnp.transpose(val) if k in transposed else val).reshape(w_all[k].shape) for val in vals]

    def update(k, pieces):
        g, d, nm, nv = _reduce_adamw(as2d(k, w_all[k]), *pieces, own_idx, as2d(k, m_all[k]), as2d(k, v_all[k]), "adamw_" + k)
        record(k, g, d, nm, nv)
        return nm

    def update_small(grads):
        keys = list(grads)
        outs = _adamw_small([as2d(k, w_all[k]) for k in keys], [grads[k] for k in keys], [as2d(k, m_all[k]) for k in keys],
                            [as2d(k, v_all[k]) for k in keys], "adamw_small")
        for k, (d, nm, nv) in zip(keys, outs):
            record(k, grads[k], d, nm, nv)
        return tuple(nm for _, nm, _ in outs)

    kernel_name = {"win_t": "w_in", "wgu_t": "w_gate_up", "wd": "w_down", "wco": "w_conv_out", "wao": "w_attn_out", "wo": "w_o"}

    def finish(group, after):
        keys, send_sems, recv_sems, parts, from_chips = in_flight[group]
        _, from_chips = _exchange_wait(send_sems, recv_sems, parts, from_chips, after, "rs_chips_wait_" + group)
        grads, from_sibling = own_pieces[group]
        return tuple(update(kernel_name[k], p) for k, *p in zip(keys, grads, from_sibling, from_chips))

    class Reducer:
        def start(self, group, gdict):
            keys, glist = list(gdict), list(gdict.values())
            send_sems, recv_sems, glist, lands, token = _exchange_start(glist, N_DEV, _to_sibling, "rs_sibling_start_" + group)
            in_flight[group] = (keys, send_sems, recv_sems, glist, lands)
            return (token,)

        def middle(self, group, after):
            keys, send_sems, recv_sems, glist, lands = in_flight[group]
            if group == "in":
                after = finish("ffn", after)
            glist, lands = _exchange_wait(send_sems, recv_sems, glist, lands, after, "rs_sibling_wait_" + group)
            parts = [_chip_partial(g, r, partial_idx, "chip_partial_" + k) for k, g, r in zip(keys, glist, lands)]
            send_sems, recv_sems, parts, from_chips, token = _exchange_start(parts, 4, _to_chips, "rs_chips_start_" + group)
            in_flight[group] = (keys, send_sems, recv_sems, parts, from_chips)
            own_pieces[group] = (glist, lands)
            return (token,)

    dx, _, small = _local_step(x[0], loss_target[0], g_mix, g_ffn, g_final[None], attn_sinks, Gathered(),
                               reducer=Reducer(), after=gather_tokens)
    after = finish("mix", (dx,))

    sinks_row = jnp.pad(small["sinks"], ((0, 0), (0, D - 128)))
    pack = jnp.concatenate([small["g_mix"], small["g_ffn"], small["g_final"], small["conv_w"], sinks_row, small["lossvec"]], axis=0)
    tot, loss_row = _small_all_reduce(pack, "small_all_reduce", after=after)
    loss = loss_row[0, 0]
    g_small = {
        "g_mix": tot[0:1], "g_ffn": tot[1:2], "g_final": tot[2:3],
        "conv_w": lax.dynamic_slice(tot, (3, me * 128), (3, 128)), "attn_sinks": tot[6:7, :N_HEADS],
    }
    finish("in", update_small(g_small))

    order = ["g_mix", "w_in", "conv_w", "attn_sinks", "w_conv_out", "w_attn_out", "w_o", "g_ffn", "w_gate_up", "w_down", "g_final"]
    return (loss, dx[None], *[results[k][i] for i in range(4) for k in order])
```

```python
import functools
import math

import jax
import jax.numpy as jnp
from jax import lax
from jax.experimental import pallas as pl
from jax.experimental.pallas import tpu as pltpu
from jax.experimental.pallas import tpu_sc as plsc

F32 = jnp.float32
BF16 = jnp.bfloat16

D = 1024
HEAD_DIM = 64
N_HEADS = 16
N_KV = 4
GROUP = N_HEADS // N_KV
D_KV = N_KV * HEAD_DIM
BLOCK = 128
ROT_DIM = HEAD_DIM // 4
ROPE_THETA = 500000.0
ATTN_SCALE = 1.0 / math.sqrt(HEAD_DIM)
NEG_INF = -1e30
D_FF = 2816
N_IN = 6656
EPS = 1e-5
C_CB, C_CC, C_CX, C_Q, C_K, C_V, C_GC, C_GA = 0, 1024, 2048, 3072, 4096, 4352, 4608, 5632

LR, B1, B2, EPS_ADAM, WD, STEP = 0.001, 0.9, 0.999, 1e-08, 0.01, 10

N_DEV = 8
MESH = pl.DeviceIdType.MESH
VMEM_LIMIT = 56 * 1024 * 1024

NN = (((1,), (0,)), ((), ()))
NT = (((1,), (1,)), ((), ()))
TN = (((0,), (0,)), ((), ()))
HBM_SPEC = pl.BlockSpec(memory_space=pl.ANY)
ROW_SPLIT = 4


def _call(body, **kw):
    return pl.pallas_call(body, **kw)


def _params(*sem):
    return pltpu.CompilerParams(dimension_semantics=sem, vmem_limit_bytes=VMEM_LIMIT)


def _sds(shape, dtype):
    return jax.ShapeDtypeStruct(shape, dtype)


def _matmul(a, b, *, mode, tm, tn, tk, out_dtype, name, res=None, after=()):
    parts = list(a) if isinstance(a, (list, tuple)) else [a]
    rows_a = parts[0].shape[0]
    cols_a = sum(p.shape[1] for p in parts)
    if mode == "nn":
        (m, kk), (_, n), dims = (rows_a, cols_a), b.shape, NN
    elif mode == "nt":
        (m, kk), (n, _), dims = (rows_a, cols_a), b.shape, NT
    else:
        (kk, m), (_, n), dims = (rows_a, cols_a), b.shape, TN
    tm, tn, tk = min(tm, m), min(tn, n), min(tk, kk)
    assert m % tm == 0 and n % tn == 0 and kk % tk == 0, (name, m, n, kk, tm, tn, tk)
    nk = kk // tk
    split_axis, width = (2, tk) if mode == "nn" else (0, tm)
    assert len(parts) == 1 or mode in ("nn", "tn")
    assert len(parts) == 1 or all(p.shape[1] % width == 0 for p in parts), (name, width)
    counts = [p.shape[1] // width for p in parts]
    starts = [sum(counts[:p]) for p in range(len(parts))]

    def a_spec(p):
        def col(t):
            return jnp.clip(t - starts[p], 0, counts[p] - 1) if len(parts) > 1 else t

        if mode == "tn":
            return pl.BlockSpec((tk, tm), lambda i, j, k: (k, col(i)))
        return pl.BlockSpec((tm, tk), lambda i, j, k: (i, col(k)))

    if mode == "nt":
        b_spec = pl.BlockSpec((tn, tk), lambda i, j, k: (j, k))
    else:
        b_spec = pl.BlockSpec((tk, tn), lambda i, j, k: (k, j))
    o_spec = pl.BlockSpec((tm, tn), lambda i, j, k: (i, j))
    has_res = res is not None
    n_parts = len(parts)
    unit = 128 if mode == "tn" else 16
    split = ROW_SPLIT if tm % (ROW_SPLIT * unit) == 0 else 1

    def body(*refs):
        a_refs, b_ref = refs[:n_parts], refs[n_parts]
        r_ref = refs[n_parts + 1] if has_res else None
        o_ref = refs[n_parts + 1 + has_res + len(after)]
        k = pl.program_id(2)

        acc_ref = refs[-1] if nk > 1 else None

        def step(a_ref):
            def matmul(rows):
                a_blk = a_ref[:, rows] if mode == "tn" else a_ref[rows, :]
                return lax.dot_general(a_blk, b_ref[...], dims, preferred_element_type=F32)

            def finish(rows, part):
                if nk > 1:
                    acc_ref[rows, :] += part
                else:
                    o_ref[rows, :] = (part + r_ref[rows, :] if has_res else part).astype(o_ref.dtype)

            _row_pipeline(tm, matmul, finish, split)

        if nk > 1:
            @pl.when(k == 0)
            def _():
                acc_ref[...] = jnp.zeros_like(acc_ref)

        if n_parts == 1:
            step(a_refs[0])
        else:
            t = pl.program_id(split_axis)
            for p in range(n_parts):
                pl.when((t >= starts[p]) & (t < starts[p] + counts[p]))(functools.partial(step, a_refs[p]))

        if nk > 1:
            @pl.when(k == nk - 1)
            def _():
                o_ref[...] = (acc_ref[...] + r_ref[...] if has_res else acc_ref[...]).astype(o_ref.dtype)

    ins = parts + [b] + ([res] if has_res else []) + list(after)
    in_specs = [a_spec(p) for p in range(n_parts)] + [b_spec] + ([o_spec] if has_res else []) + [HBM_SPEC] * len(after)
    scratch = [] if nk == 1 else [pltpu.VMEM((tm, tn), F32)]
    return _call(
        body, name=name, grid=(m // tm, n // tn, nk), in_specs=in_specs, out_specs=o_spec,
        out_shape=_sds((m, n), out_dtype), scratch_shapes=scratch,
        compiler_params=_params("parallel", "parallel", "arbitrary"),
    )(*ins)


def _row_tile(s):
    return min(512, s)


def _rms_fwd(x, g, name, after=()):
    s = x.shape[0]
    tm = _row_tile(s)

    def body(x_ref, g_ref, *rest):
        h_ref = rest[-1]
        xv = x_ref[...]
        r = lax.rsqrt(jnp.mean(xv * xv, axis=-1, keepdims=True) + EPS)
        h_ref[...] = (xv * r * g_ref[...]).astype(BF16)

    row = pl.BlockSpec((tm, D), lambda i: (i, 0))
    return _call(
        body, name=name, grid=(s // tm,), in_specs=[row, pl.BlockSpec((1, D), lambda i: (0, 0))] + [HBM_SPEC] * len(after),
        out_specs=row, out_shape=_sds((s, D), BF16), compiler_params=_params("parallel"),
    )(x, g, *after)


def _rms_bwd(dh, x, g, dres, name, after=()):
    s = x.shape[0]
    tm = _row_tile(s)

    def body(dh_ref, x_ref, g_ref, dres_ref, *rest):
        dx_ref, dxb_ref, dg_ref = rest[len(after):]
        xv = x_ref[...]
        r = lax.rsqrt(jnp.mean(xv * xv, axis=-1, keepdims=True) + EPS)
        xh = xv * r
        dhv = dh_ref[...]
        dyg = dhv * g_ref[...]
        dx = dres_ref[...] + r * (dyg - xh * jnp.mean(dyg * xh, axis=-1, keepdims=True))
        dx_ref[...] = dx
        dxb_ref[...] = dx.astype(BF16)
        part = jnp.sum(dhv * xh, axis=0, keepdims=True)

        @pl.when(pl.program_id(0) == 0)
        def _():
            dg_ref[...] = part

        @pl.when(pl.program_id(0) > 0)
        def _():
            dg_ref[...] += part

    row = pl.BlockSpec((tm, D), lambda i: (i, 0))
    vec = pl.BlockSpec((1, D), lambda i: (0, 0))
    return _call(
        body, name=name, grid=(s // tm,), in_specs=[row, row, vec, row] + [HBM_SPEC] * len(after), out_specs=[row, row, vec],
        out_shape=[_sds((s, D), F32), _sds((s, D), BF16), _sds((1, D), F32)],
        compiler_params=_params("arbitrary"),
    )(dh, x, g, dres, *after)


def _loss_head(x2, g, tgt, name):
    s = x2.shape[0]
    tm = _row_tile(s)

    def body(x_ref, g_ref, t_ref, dx_ref, dxb_ref, dg_ref, l_ref):
        xv = x_ref[...]
        gv = g_ref[...]
        r = lax.rsqrt(jnp.mean(xv * xv, axis=-1, keepdims=True) + EPS)
        xh = xv * r
        err = xh * gv - t_ref[...]
        dy = err * (1.0 / D)
        dyg = dy * gv
        dx = r * (dyg - xh * jnp.mean(dyg * xh, axis=-1, keepdims=True))
        dx_ref[...] = dx
        dxb_ref[...] = dx.astype(BF16)
        dg_part = jnp.sum(dy * xh, axis=0, keepdims=True)
        l_part = jnp.sum(err * err, axis=0, keepdims=True)

        @pl.when(pl.program_id(0) == 0)
        def _():
            dg_ref[...] = dg_part
            l_ref[...] = l_part

        @pl.when(pl.program_id(0) > 0)
        def _():
            dg_ref[...] += dg_part
            l_ref[...] += l_part

    row = pl.BlockSpec((tm, D), lambda i: (i, 0))
    vec = pl.BlockSpec((1, D), lambda i: (0, 0))
    return _call(
        body, name=name, grid=(s // tm,), in_specs=[row, vec, row], out_specs=[row, row, vec, vec],
        out_shape=[_sds((s, D), F32), _sds((s, D), BF16), _sds((1, D), F32), _sds((1, D), F32)],
        compiler_params=_params("arbitrary"),
    )(x2, g, tgt)


CONV_TC = 256


def _shift_down(u, k, rows):
    return jnp.where(rows >= k, pltpu.roll(u, k, 0), 0.0)


def _shift_up(u, k, rows, s):
    return jnp.where(rows < s - k, pltpu.roll(u, s - k, 0), 0.0)


def _conv_specs(s):
    nb = D // CONV_TC

    def col(c0):
        return pl.BlockSpec((s, CONV_TC), lambda j, c0=c0: (0, c0 // CONV_TC + j))

    return nb, col


def _conv_fwd(proj, conv_w, name):
    s = proj.shape[0]
    nb, col = _conv_specs(s)

    def body(cb_ref, cc_ref, cx_ref, w_ref, y_ref):
        rows = lax.broadcasted_iota(jnp.int32, (s, CONV_TC), 0)
        u = cc_ref[...].astype(F32) * cx_ref[...].astype(F32)
        w = w_ref[...]
        c = w[0:1] * _shift_down(u, 2, rows) + w[1:2] * _shift_down(u, 1, rows) + w[2:3] * u
        y_ref[...] = (cb_ref[...].astype(F32) * c).astype(BF16)

    return _call(
        body, name=name, grid=(nb,),
        in_specs=[col(C_CB), col(C_CC), col(C_CX), pl.BlockSpec((3, CONV_TC), lambda j: (0, j))],
        out_specs=pl.BlockSpec((s, CONV_TC), lambda j: (0, j)), out_shape=_sds((s, D), BF16),
        compiler_params=_params("parallel"),
    )(proj, proj, proj, conv_w)


def _write_behind(t, nt, buf, sems, tiles, window, where):
    slot = t % 2

    def copies(sl, at):
        return [pltpu.make_async_copy(buf.at[sl, p], window(p, at), sems.at[sl, p]) for p in range(len(tiles))]

    @pl.when(t >= 2)
    def _():
        for cp in copies(slot, where):
            cp.wait()

    for p, tile in enumerate(tiles):
        buf[slot, p] = tile
    started = copies(slot, where)
    for cp in started:
        cp.start()

    @pl.when(t == nt - 1)
    def _():
        for cp in started:
            cp.wait()
        if nt > 1:
            for cp in copies(1 - slot, where):
                cp.wait()


def _conv_bwd(dy, proj, conv_w, dproj, name, after=()):
    s = proj.shape[0]
    nb, col = _conv_specs(s)

    def body(dy_ref, cb_ref, cc_ref, cx_ref, w_ref, *rest):
        dproj_ref, dw_ref, buf, sems = rest[1 + len(after):]
        j = pl.program_id(0)
        rows = lax.broadcasted_iota(jnp.int32, (s, CONV_TC), 0)
        cc = cc_ref[...].astype(F32)
        cx = cx_ref[...].astype(F32)
        u = cc * cx
        u1 = _shift_down(u, 1, rows)
        u2 = _shift_down(u, 2, rows)
        w = w_ref[...]
        c = w[0:1] * u2 + w[1:2] * u1 + w[2:3] * u
        dyv = dy_ref[...].astype(F32)
        dc = dyv * cb_ref[...].astype(F32)
        du = w[2:3] * dc + w[1:2] * _shift_up(dc, 1, rows, s) + w[0:1] * _shift_up(dc, 2, rows, s)

        def window(p, jj):
            start = pl.multiple_of((C_CB, C_CC, C_CX)[p] + jj * CONV_TC, CONV_TC)
            return dproj_ref.at[:, pl.ds(start, CONV_TC)]

        tiles = ((dyv * c).astype(BF16), (du * cx).astype(BF16), (du * cc).astype(BF16))
        _write_behind(j * 0, 1, buf, sems, tiles, window, j)
        dw_ref[...] = jnp.concatenate(
            [jnp.sum(dc * u2, axis=0, keepdims=True), jnp.sum(dc * u1, axis=0, keepdims=True),
             jnp.sum(dc * u, axis=0, keepdims=True)], axis=0)

    return _call(
        body, name=name, grid=(nb,),
        in_specs=[pl.BlockSpec((s, CONV_TC), lambda j: (0, j)), col(C_CB), col(C_CC), col(C_CX),
                  pl.BlockSpec((3, CONV_TC), lambda j: (0, j))] + [HBM_SPEC] * (1 + len(after)),
        out_specs=[pl.BlockSpec(memory_space=pl.ANY), pl.BlockSpec((3, CONV_TC), lambda j: (0, j))],
        out_shape=[_sds((s, N_IN), BF16), _sds((3, D), F32)],
        scratch_shapes=[pltpu.VMEM((1, 3, s, CONV_TC), BF16), pltpu.SemaphoreType.DMA((1, 3))],
        input_output_aliases={5: 0}, compiler_params=_params("arbitrary"),
    )(dy, proj, proj, proj, conv_w, dproj, *after)


def _rope_tables(s):
    half = ROT_DIM // 2
    inv_freq = ROPE_THETA ** (-jnp.arange(0, ROT_DIM, 2, dtype=F32) / ROT_DIM)
    inv64 = jnp.concatenate([inv_freq, inv_freq, jnp.zeros((HEAD_DIM - ROT_DIM,), F32)])
    ang = jnp.arange(s, dtype=F32)[:, None] * jnp.concatenate([inv64, inv64])[None, :]
    d = lax.broadcasted_iota(jnp.int32, (s, 128), 1) % HEAD_DIM
    cos, sin = jnp.cos(ang), jnp.sin(ang)
    c = jnp.where(d < ROT_DIM, cos, 1.0)
    a = jnp.where(d < half, -sin, 0.0)
    b = jnp.where((d >= half) & (d < ROT_DIM), sin, 0.0)
    return jnp.concatenate([c, a, b], axis=1)


def _rope(x, tab):
    c, a, b = tab[:, 0:128], tab[:, 128:256], tab[:, 256:384]
    outs = []
    for i in range(x.shape[1] // 128):
        xc = x[:, i * 128:(i + 1) * 128]
        outs.append(xc * c + pltpu.roll(xc, 120, 1) * a + pltpu.roll(xc, 8, 1) * b)
    return outs[0] if len(outs) == 1 else jnp.concatenate(outs, axis=1)


def _rope_t(dx, tab):
    c, a, b = tab[:, 0:128], tab[:, 128:256], tab[:, 256:384]
    outs = []
    for i in range(dx.shape[1] // 128):
        dc = dx[:, i * 128:(i + 1) * 128]
        outs.append(dc * c + pltpu.roll(dc * a, 8, 1) + pltpu.roll(dc * b, 120, 1))
    return outs[0] if len(outs) == 1 else jnp.concatenate(outs, axis=1)


def _attn_in_specs():
    prev = lambda n: jnp.maximum(n - 1, 0)
    return [
        pl.BlockSpec((BLOCK, D), lambda n: (n, C_Q // D)),
        pl.BlockSpec((BLOCK, D_KV), lambda n: (n, C_K // D_KV)),
        pl.BlockSpec((BLOCK, D_KV), lambda n: (prev(n), C_K // D_KV)),
        pl.BlockSpec((BLOCK, D_KV), lambda n: (n, C_V // D_KV)),
        pl.BlockSpec((BLOCK, D_KV), lambda n: (prev(n), C_V // D_KV)),
        pl.BlockSpec((BLOCK, 384), lambda n: (n, 0)),
        pl.BlockSpec((BLOCK, 384), lambda n: (prev(n), 0)),
        pl.BlockSpec(memory_space=pltpu.SMEM),
    ]


HALF = HEAD_DIM
N_CHUNK = D // 128


def _swa_bias(n):
    qi = lax.broadcasted_iota(jnp.int32, (BLOCK, 2 * BLOCK), 0)
    kj = lax.broadcasted_iota(jnp.int32, (BLOCK, 2 * BLOCK), 1)
    rel = qi + BLOCK - kj
    valid = (rel >= 0) & (rel < BLOCK) & ((kj >= BLOCK) | (n > 0))
    return jnp.where(valid, 0.0, NEG_INF)


def _halves(x):
    lo = lax.broadcasted_iota(jnp.int32, x.shape, 1) < HALF
    return jnp.where(lo, x, 0.0).astype(BF16), jnp.where(lo, 0.0, x).astype(BF16)


def _dup_heads(x):
    out = []
    for pair in range(N_KV // 2):
        xc = x[:, pair * 128:(pair + 1) * 128]
        xr = pltpu.roll(xc, HALF, 1)
        lo = lax.broadcasted_iota(jnp.int32, xc.shape, 1) < HALF
        out += [jnp.where(lo, xc, xr), jnp.where(lo, xr, xc)]
    return out


def _swa_load(q_ref, kc_ref, kp_ref, vc_ref, vp_ref, tc_ref, tp_ref):
    qf = _rope(q_ref[...].astype(F32), tc_ref[...]) * ATTN_SCALE
    q_halves = [_halves(qf[:, c * 128:(c + 1) * 128]) for c in range(N_CHUNK)]
    kf = jnp.concatenate([_rope(kp_ref[...].astype(F32), tp_ref[...]), _rope(kc_ref[...].astype(F32), tc_ref[...])], axis=0)
    vf = jnp.concatenate([vp_ref[...], vc_ref[...]], axis=0).astype(F32)
    return q_halves, _dup_heads(kf), _dup_heads(vf)


def _swa_probs(qh, kk, bias, sink):
    s = lax.dot_general(qh, kk, NT, preferred_element_type=F32) + bias
    m = jnp.maximum(jnp.max(jnp.maximum(s[:, :BLOCK], s[:, BLOCK:]), axis=1, keepdims=True), sink)
    return jnp.exp(s - m), m


def _swa_fwd(proj, tab, sinks, name, after=()):
    s = proj.shape[0]

    def body(q_ref, kc_ref, kp_ref, vc_ref, vp_ref, tc_ref, tp_ref, sink_ref, *rest):
        o_ref = rest[-1]
        n = pl.program_id(0)
        q_halves, kdup, vdup = _swa_load(q_ref, kc_ref, kp_ref, vc_ref, vp_ref, tc_ref, tp_ref)
        bias = _swa_bias(n)
        ones = jnp.ones((2 * BLOCK, 128), BF16)
        kk = [k.astype(BF16) for k in kdup]
        vv = [[jnp.concatenate([v_half, ones], axis=1) for v_half in _halves(v)] for v in vdup]
        heads = [(c, half) for c in range(N_CHUNK) for half in range(2)]
        scores = [lax.dot_general(q_halves[c][half], kk[c // (GROUP // 2)], NT, preferred_element_type=F32)
                  for c, half in heads]
        probs = []
        for (c, half), sc in zip(heads, scores):
            sc = sc + bias
            m = jnp.maximum(jnp.max(jnp.maximum(sc[:, :BLOCK], sc[:, BLOCK:]), axis=1, keepdims=True), sink_ref[0, 2 * c + half])
            probs.append((jnp.exp(sc - m).astype(BF16), jnp.exp(sink_ref[0, 2 * c + half] - m)))
        outs = [lax.dot_general(e, vv[c // (GROUP // 2)][half], NN, preferred_element_type=F32)
                for (c, half), (e, _) in zip(heads, probs)]
        for c in range(N_CHUNK):
            parts = [outs[2 * c + half][:, :128] * (1.0 / (outs[2 * c + half][:, 128:] + probs[2 * c + half][1]))
                     for half in range(2)]
            o_ref[:, c * 128:(c + 1) * 128] = (parts[0] + parts[1]).astype(BF16)

    return _call(
        body, name=name, grid=(s // BLOCK,), in_specs=_attn_in_specs() + [HBM_SPEC] * len(after),
        out_specs=pl.BlockSpec((BLOCK, D), lambda n: (n, 0)), out_shape=_sds((s, D), BF16),
        compiler_params=_params("parallel"),
    )(proj, proj, proj, proj, proj, tab, tab, sinks, *after)


def _swa_bwd(do, proj, tab, sinks, dproj, name, after=()):
    s = proj.shape[0]
    nblk = s // BLOCK
    kv_of = lambda c: c // (GROUP // 2)

    def body(do_ref, q_ref, kc_ref, kp_ref, vc_ref, vp_ref, tc_ref, tp_ref, sink_ref, *rest):
        dproj_ref, dk_ref, dv_ref, ds_ref, dqout, dkbuf, dvbuf, sems = rest[1 + len(after):]
        n = pl.program_id(0)

        @pl.when(n == 0)
        def _():
            dk_ref[...] = jnp.zeros_like(dk_ref)
            dv_ref[...] = jnp.zeros_like(dv_ref)
            ds_ref[...] = jnp.zeros_like(ds_ref)

        q_halves, kdup, vdup = _swa_load(q_ref, kc_ref, kp_ref, vc_ref, vp_ref, tc_ref, tp_ref)
        dof = do_ref[...].astype(F32)
        do_halves = [_halves(dof[:, c * 128:(c + 1) * 128]) for c in range(N_CHUNK)]
        bias = _swa_bias(n)
        ones = jnp.ones((2 * BLOCK, 128), BF16)
        kk = [k.astype(BF16) for k in kdup]
        vv = [v.astype(BF16) for v in vdup]
        k_halves = [_halves(k) for k in kdup]
        heads = [(c, half) for c in range(N_CHUNK) for half in range(2)]
        lane_row = lax.broadcasted_iota(jnp.int32, (1, 128), 1)
        lo_kv = lax.broadcasted_iota(jnp.int32, (2 * BLOCK, 128), 1) < HALF
        scores = [lax.dot_general(q_halves[c][half], kk[kv_of(c)], NT, preferred_element_type=F32) for c, half in heads]
        dps = [lax.dot_general(do_halves[c][half], vv[kv_of(c)], NT, preferred_element_type=F32) for c, half in heads]
        exps = []
        for (c, half), sc in zip(heads, scores):
            sink = sink_ref[0, 2 * c + half]
            sc = sc + bias
            m = jnp.maximum(jnp.max(jnp.maximum(sc[:, :BLOCK], sc[:, BLOCK:]), axis=1, keepdims=True), sink)
            exps.append((jnp.exp(sc - m), jnp.exp(sink - m)))
        sums = [lax.dot_general(e.astype(BF16), ones, NN, preferred_element_type=F32) for e, _ in exps]
        dsink_row = jnp.zeros((1, 128), F32)
        dsb, pb = [], []
        for h, ((e, es), row_sum, dp) in enumerate(zip(exps, sums, dps)):
            inv = 1.0 / (row_sum + es)
            p = e * jnp.concatenate([inv, inv], axis=1)
            t = p * dp
            delta = jnp.sum(t, axis=1, keepdims=True)
            dsb.append((t - p * delta).astype(BF16))
            pb.append(p.astype(BF16))
            dsink = -jnp.sum(es * inv * delta, axis=0, keepdims=True)
            dsink_row = dsink_row + jnp.where(lane_row == h, dsink, 0.0)
        dq_parts = [lax.dot_general(d, k_halves[kv_of(c)][half], NN, preferred_element_type=F32) for (c, half), d in zip(heads, dsb)]
        dk_parts = [lax.dot_general(d, q_halves[c][half], TN, preferred_element_type=F32) for (c, half), d in zip(heads, dsb)]
        dv_parts = [lax.dot_general(p, do_halves[c][half], TN, preferred_element_type=F32) for (c, half), p in zip(heads, pb)]
        dq = jnp.concatenate([(dq_parts[2 * c] + dq_parts[2 * c + 1]) * ATTN_SCALE for c in range(N_CHUNK)], axis=1)

        def kv_sum(parts, hk):
            acc = (parts[GROUP * hk] + parts[GROUP * hk + 1]) + (parts[GROUP * hk + 2] + parts[GROUP * hk + 3])
            return acc + pltpu.roll(acc, HALF, 1)

        for pair in range(N_KV // 2):
            dkbuf[:, pair * 128:(pair + 1) * 128] = jnp.where(lo_kv, kv_sum(dk_parts, 2 * pair), kv_sum(dk_parts, 2 * pair + 1))
            dvbuf[:, pair * 128:(pair + 1) * 128] = jnp.where(lo_kv, kv_sum(dv_parts, 2 * pair), kv_sum(dv_parts, 2 * pair + 1))
        prev0 = pl.multiple_of(jnp.maximum(n - 1, 0) * BLOCK, BLOCK)
        cur0 = pl.multiple_of(n * BLOCK, BLOCK)

        @pl.when(n > 0)
        def _():
            dk_ref[pl.ds(prev0, BLOCK), :] += dkbuf[0:BLOCK, :]
            dv_ref[pl.ds(prev0, BLOCK), :] += dvbuf[0:BLOCK, :]

        dk_ref[pl.ds(cur0, BLOCK), :] += dkbuf[BLOCK:2 * BLOCK, :]
        dv_ref[pl.ds(cur0, BLOCK), :] += dvbuf[BLOCK:2 * BLOCK, :]
        ds_ref[...] += dsink_row

        def window(p, at):
            return dproj_ref.at[pl.ds(pl.multiple_of(at * BLOCK, BLOCK), BLOCK), pl.ds(C_Q, D)]

        _write_behind(n, nblk, dqout, sems, (_rope_t(dq, tc_ref[...]).astype(BF16),), window, n)

    blk = lambda w: pl.BlockSpec((BLOCK, w), lambda n: (n, 0))
    whole = lambda w: pl.BlockSpec((s, w), lambda n: (0, 0))
    n_in = 1 + len(_attn_in_specs())
    return _call(
        body, name=name, grid=(nblk,), in_specs=[blk(D)] + _attn_in_specs() + [HBM_SPEC] * (1 + len(after)),
        out_specs=[HBM_SPEC, whole(D_KV), whole(D_KV), pl.BlockSpec((1, 128), lambda n: (0, 0))],
        out_shape=[_sds((s, N_IN), BF16), _sds((s, D_KV), F32), _sds((s, D_KV), F32), _sds((1, 128), F32)],
        scratch_shapes=[pltpu.VMEM((2, 1, BLOCK, D), BF16), pltpu.VMEM((2 * BLOCK, D_KV), F32),
                        pltpu.VMEM((2 * BLOCK, D_KV), F32), pltpu.SemaphoreType.DMA((2, 1))],
        input_output_aliases={n_in: 0}, compiler_params=_params("arbitrary"),
    )(do, proj, proj, proj, proj, proj, tab, tab, sinks, dproj, *after)


def _kv_bwd(dkr, dv, tab, dproj, name):
    s = dkr.shape[0]
    tm = _row_tile(s)

    def body(dk_ref, dv_ref, t_ref, dproj_in, o_ref):
        del dproj_in
        o_ref[:, 0:D_KV] = _rope_t(dk_ref[...], t_ref[...]).astype(BF16)
        o_ref[:, D_KV:2 * D_KV] = dv_ref[...].astype(BF16)

    row = lambda w: pl.BlockSpec((tm, w), lambda i: (i, 0))
    return _call(
        body, name=name, grid=(s // tm,),
        in_specs=[row(D_KV), row(D_KV), row(384), pl.BlockSpec(memory_space=pl.ANY)],
        out_specs=pl.BlockSpec((tm, 2 * D_KV), lambda i: (i, C_K // (2 * D_KV))),
        out_shape=_sds((s, N_IN), BF16), input_output_aliases={3: 0}, compiler_params=_params("parallel"),
    )(dkr, dv, tab, dproj)


EW_TC = 512


def _sigmoid(x):
    return 0.5 * jnp.tanh(0.5 * x) + 0.5


def _merge_fwd(proj, conv_out, attn_out, name):
    s = proj.shape[0]
    tm = _row_tile(s)
    tile = pl.BlockSpec((tm, EW_TC), lambda i, j: (i, j))

    def body(gc_ref, ga_ref, co_ref, ao_ref, o_ref):
        o_ref[...] = (_sigmoid(gc_ref[...].astype(F32)) * co_ref[...].astype(F32)
                      + _sigmoid(ga_ref[...].astype(F32)) * ao_ref[...].astype(F32)).astype(BF16)

    return _call(
        body, name=name, grid=(s // tm, D // EW_TC),
        in_specs=[pl.BlockSpec((tm, EW_TC), lambda i, j: (i, C_GC // EW_TC + j)),
                  pl.BlockSpec((tm, EW_TC), lambda i, j: (i, C_GA // EW_TC + j)), tile, tile],
        out_specs=tile, out_shape=_sds((s, D), BF16), compiler_params=_params("parallel", "parallel"),
    )(proj, proj, conv_out, attn_out)


def _merge_bwd(dmerged, proj, conv_out, attn_out, name):
    s = proj.shape[0]
    tm = _row_tile(s)
    tile = pl.BlockSpec((tm, EW_TC), lambda i, j: (i, j))
    anyspec = pl.BlockSpec(memory_space=pl.ANY)

    def body(dm_ref, gc_ref, ga_ref, co_ref, ao_ref, dproj_ref, dco_ref, dao_ref, buf, sems):
        i, j = pl.program_id(0), pl.program_id(1)
        dm = dm_ref[...].astype(F32)
        sc = _sigmoid(gc_ref[...].astype(F32))
        sa = _sigmoid(ga_ref[...].astype(F32))
        dco_ref[...] = (dm * sc).astype(BF16)
        dao_ref[...] = (dm * sa).astype(BF16)
        tiles = ((dm * co_ref[...].astype(F32) * sc * (1.0 - sc)).astype(BF16),
                 (dm * ao_ref[...].astype(F32) * sa * (1.0 - sa)).astype(BF16))

        def window(p, at):
            start = pl.multiple_of((C_GC, C_GA)[p] + at[1] * EW_TC, EW_TC)
            return dproj_ref.at[pl.ds(pl.multiple_of(at[0] * tm, tm), tm), pl.ds(start, EW_TC)]

        _write_behind(i * nj + j, (s // tm) * nj, buf, sems, tiles, window, (i, j))

    nj = D // EW_TC
    return _call(
        body, name=name, grid=(s // tm, nj),
        in_specs=[tile, pl.BlockSpec((tm, EW_TC), lambda i, j: (i, C_GC // EW_TC + j)),
                  pl.BlockSpec((tm, EW_TC), lambda i, j: (i, C_GA // EW_TC + j)), tile, tile],
        out_specs=[anyspec, tile, tile],
        out_shape=[_sds((s, N_IN), BF16), _sds((s, D), BF16), _sds((s, D), BF16)],
        scratch_shapes=[pltpu.VMEM((2, 2, tm, EW_TC), BF16), pltpu.SemaphoreType.DMA((2, 2))],
        compiler_params=_params("arbitrary", "arbitrary"),
    )(dmerged, proj, proj, conv_out, attn_out)


FF_TC = 256
FF_TM = 2048


def _row_pipeline(tm, matmul, finish, split=ROW_SPLIT):
    step = tm // split
    pending = None
    for r in range(split):
        rows = pl.ds(r * step, step)
        result = matmul(rows)
        if pending is not None:
            finish(*pending)
        pending = (rows, result)
    finish(*pending)


def _gate_up_fwd(h2, wgu_t, name):
    s = h2.shape[0]
    tm = min(FF_TM, s)
    nb = D_FF // FF_TC

    def body(h_ref, wg_ref, wu_ref, a_ref, g_ref, u_ref):
        def matmuls(rows):
            h = h_ref[rows, :]
            return (lax.dot_general(h, wg_ref[...], NT, preferred_element_type=F32),
                    lax.dot_general(h, wu_ref[...], NT, preferred_element_type=F32))

        def finish(rows, gu):
            g, u = gu
            a_ref[rows, :] = (g * _sigmoid(g) * u).astype(BF16)
            g_ref[rows, :] = g.astype(BF16)
            u_ref[rows, :] = u.astype(BF16)

        _row_pipeline(tm, matmuls, finish)

    tile = pl.BlockSpec((tm, FF_TC), lambda j, i: (i, j))
    return _call(
        body, name=name, grid=(nb, s // tm),
        in_specs=[pl.BlockSpec((tm, D), lambda j, i: (i, 0)), pl.BlockSpec((FF_TC, D), lambda j, i: (j, 0)),
                  pl.BlockSpec((FF_TC, D), lambda j, i: (nb + j, 0))],
        out_specs=[tile, tile, tile], out_shape=[_sds((s, D_FF), BF16)] * 3,
        compiler_params=_params("parallel", "parallel"),
    )(h2, wgu_t, wgu_t)


def _down_bwd_x(dx2b, wd, gate, up, name):
    s = dx2b.shape[0]
    tm = min(FF_TM, s)
    nb = D_FF // FF_TC

    def body(dx_ref, w_ref, g_ref, u_ref, dg_ref, du_ref):
        def matmul(rows):
            return lax.dot_general(dx_ref[rows, :], w_ref[...], NT, preferred_element_type=F32)

        def finish(rows, da):
            g = g_ref[rows, :].astype(F32)
            sg = _sigmoid(g)
            dg_ref[rows, :] = (da * u_ref[rows, :].astype(F32) * (sg * (1.0 + g * (1.0 - sg)))).astype(BF16)
            du_ref[rows, :] = (da * (g * sg)).astype(BF16)

        _row_pipeline(tm, matmul, finish)

    tile = pl.BlockSpec((tm, FF_TC), lambda j, i: (i, j))
    return _call(
        body, name=name, grid=(nb, s // tm),
        in_specs=[pl.BlockSpec((tm, D), lambda j, i: (i, 0)), pl.BlockSpec((FF_TC, D), lambda j, i: (j, 0)), tile, tile],
        out_specs=[tile, tile], out_shape=[_sds((s, D_FF), BF16)] * 2,
        compiler_params=_params("parallel", "parallel"),
    )(dx2b, wd, gate, up)


class _Weights:
    def __init__(self, **groups):
        self.groups = groups

    def begin(self, group, after):
        return ()

    def end(self, group, after):
        return self.groups[group]


class _NoReduce:
    def start(self, group, grads):
        return ()

    def middle(self, group, after):
        return ()


def _local_step(x, tgt, g_mix, g_ffn, g_final, sinks, weights, reducer=None, after=()):
    reducer = reducer or _NoReduce()
    s = x.shape[0]
    tab = _rope_tables(s)
    big = dict(tm=2048, tn=512, tk=1024)
    h1 = _rms_fwd(x, g_mix, "rms1_fwd", after=after)
    win_t, conv_w = weights.end("in", weights.begin("in", (h1,)))
    proj = _matmul(h1, win_t, mode="nt", out_dtype=BF16, name="proj_fwd", tm=2048, tn=512, tk=1024)
    attn = _swa_fwd(proj, tab, sinks, "attn_fwd", after=weights.begin("mix", (proj,)))
    wco, wao, wo = weights.end("mix", (attn,))
    conv_y = _conv_fwd(proj, conv_w, "conv_fwd")
    conv_out = _matmul(conv_y, wco, mode="nn", out_dtype=BF16, name="conv_out_fwd", **big)
    attn_out = _matmul(attn, wao, mode="nn", out_dtype=BF16, name="attn_out_fwd", **big)
    merged = _merge_fwd(proj, conv_out, attn_out, "merge_fwd")
    x1 = _matmul(merged, wo, mode="nn", out_dtype=F32, name="wo_fwd", res=x, after=weights.begin("ffn", (merged,)), **big)
    h2 = _rms_fwd(x1, g_ffn, "rms2_fwd")
    wgu_t, wd = weights.end("ffn", (h2,))
    act, gate, up = _gate_up_fwd(h2, wgu_t, "gate_up_fwd")
    x2 = _matmul(act, wd, mode="nn", out_dtype=F32, name="down_fwd", res=x1, tm=1024, tn=512, tk=D_FF)
    dx2, dx2b, dg_final, lossvec = _loss_head(x2, g_final, tgt, "loss_head")
    dgate, dup = _down_bwd_x(dx2b, wd, gate, up, "down_bwd_x")
    g_wd = _matmul(act, dx2b, mode="tn", out_dtype=BF16, name="down_bwd_w", tm=1408, tn=1024, tk=2048)
    dh2 = _matmul([dgate, dup], wgu_t, mode="nn", out_dtype=F32, name="gate_up_bwd_x", tm=1024, tn=1024, tk=1408)
    g_wgu_t = _matmul([dgate, dup], h2, mode="tn", out_dtype=BF16, name="gate_up_bwd_w", tm=1408, tn=1024, tk=2048)
    after_ffn = reducer.start("ffn", dict(wgu_t=g_wgu_t, wd=g_wd))
    dx1, dx1b, dg_ffn = _rms_bwd(dh2, x1, g_ffn, dx2, "rms2_bwd")
    dmerged = _matmul(dx1b, wo, mode="nt", out_dtype=BF16, name="wo_bwd_x", after=after_ffn, **big)
    after_ffn = reducer.middle("ffn", (dmerged,))
    g_wo = _matmul(merged, dx1b, mode="tn", out_dtype=BF16, name="wo_bwd_w", tm=512, tn=1024, tk=2048, after=after_ffn)
    dproj, dco, dao = _merge_bwd(dmerged, proj, conv_out, attn_out, "merge_bwd")
    dconv_y = _matmul(dco, wco, mode="nt", out_dtype=BF16, name="conv_out_bwd_x", **big)
    g_wco = _matmul(conv_y, dco, mode="tn", out_dtype=BF16, name="conv_out_bwd_w", tm=512, tn=1024, tk=2048)
    dattn = _matmul(dao, wao, mode="nt", out_dtype=BF16, name="attn_out_bwd_x", **big)
    g_wao = _matmul(attn, dao, mode="tn", out_dtype=BF16, name="attn_out_bwd_w", tm=512, tn=1024, tk=2048)
    after_mix = reducer.start("mix", dict(wco=g_wco, wao=g_wao, wo=g_wo))
    dproj, dconv_w = _conv_bwd(dconv_y, proj, conv_w, dproj, "conv_bwd", after=after_mix)
    after_mix = reducer.middle("mix", (dconv_w,))
    dproj, dkr, dv, dsinks = _swa_bwd(dattn, proj, tab, sinks, dproj, "attn_bwd", after=after_mix)
    dproj = _kv_bwd(dkr, dv, tab, dproj, "kv_bwd")
    g_win_t = _matmul(dproj, h1, mode="tn", out_dtype=BF16, name="proj_bwd_w", tm=512, tn=1024, tk=2048)
    after_in = reducer.middle("in", reducer.start("in", dict(win_t=g_win_t)))
    dh1 = _matmul(dproj, win_t, mode="nn", out_dtype=F32, name="proj_bwd_x", tm=1024, tn=1024, tk=1664, after=after_in)
    dx, _, dg_mix = _rms_bwd(dh1, x, g_mix, dx1, "rms1_bwd")
    grads = dict(win_t=g_win_t, wgu_t=g_wgu_t, wd=g_wd, wco=g_wco, wao=g_wao, wo=g_wo)
    small = dict(g_mix=dg_mix, g_ffn=dg_ffn, g_final=dg_final, conv_w=dconv_w, sinks=dsinks, lossvec=lossvec)
    return dx, grads, small


def _position():
    return lax.axis_index("x"), lax.axis_index("y"), lax.axis_index("c")


def _other_chips(x, y):
    return [(1 - x, y), (x, 1 - y), (1 - x, 1 - y)]


SEM_SPEC = pl.BlockSpec(memory_space=pltpu.SEMAPHORE)
EFFECT = pltpu.SideEffectType.DATAFLOW_SIDE_EFFECTING
TOKEN = jax.ShapeDtypeStruct((8, 128), F32)
TOKEN_SPEC = pl.BlockSpec(memory_space=pltpu.VMEM)


def _hbm(a):
    return pltpu.with_memory_space_constraint(a, pltpu.HBM)


def _place(w, me_idx, dtype, name, after=()):
    r, cdim = w.shape

    def body(i_ref, w_ref, *rest):
        rest[-1][...] = w_ref[...].astype(dtype)

    grid_spec = pltpu.PrefetchScalarGridSpec(
        num_scalar_prefetch=1, grid=(1,), in_specs=[pl.BlockSpec((r, cdim), lambda i, me: (0, 0))] + [HBM_SPEC] * len(after),
        out_specs=pl.BlockSpec((r, cdim), lambda i, me: (me[0], 0)))
    return _call(body, name=name, grid_spec=grid_spec, out_shape=_sds((N_DEV * r, cdim), dtype),
                 compiler_params=_params("arbitrary"))(me_idx, w, *after)


def _own_rows(ref, r, px, py, pc):
    return ref.at[pl.ds((4 * px + 2 * py + pc) * r, r), :]


def _gather_phase(bufs, waits, plans, after, name):
    n = len(bufs)
    rows = [b.shape[0] // N_DEV for b in bufs]
    nw, npl = len(waits), len(plans)

    def body(*refs):
        ins = refs[:n]
        wait_sems = refs[n:n + 2 * nw]
        out0 = n + 2 * nw + len(after)
        new_sems = refs[out0:out0 + 2 * npl]
        token = refs[-1]
        x, y, c = _position()
        for w, (_, _, sent, received) in enumerate(waits):
            for a in range(n):
                for count, wait in ((sent, "wait_send"), (received, "wait_recv")):
                    span = _whole(ins[a], count * rows[a])
                    getattr(pltpu.make_async_remote_copy(
                        src_ref=span, dst_ref=span, send_sem=wait_sems[2 * w].at[a], recv_sem=wait_sems[2 * w + 1].at[a],
                        device_id=(x, y, c), device_id_type=MESH), wait)()
        for k, plan in enumerate(plans):
            for a in range(n):
                for block, target in plan(x, y, c):
                    span = _own_rows(ins[a], rows[a], *block)
                    pltpu.make_async_remote_copy(src_ref=span, dst_ref=span, send_sem=new_sems[2 * k].at[a],
                                                 recv_sem=new_sems[2 * k + 1].at[a], device_id=target, device_id_type=MESH).start()
        token[...] = jnp.zeros_like(token)

    sem_ops = [s for send, recv, _, _ in waits for s in (send, recv)]
    outs = _call(
        body, name=name, in_specs=[HBM_SPEC] * n + [SEM_SPEC] * (2 * nw) + [HBM_SPEC] * len(after),
        out_specs=[SEM_SPEC] * (2 * npl) + [HBM_SPEC] * n + [TOKEN_SPEC],
        out_shape=[pltpu.SemaphoreType.DMA((n,))] * (2 * npl) + [pltpu.HBM(b.shape, b.dtype) for b in bufs] + [TOKEN],
        input_output_aliases={i: 2 * npl + i for i in range(n)},
        compiler_params=pltpu.CompilerParams(has_side_effects=EFFECT),
    )(*[_hbm(b) for b in bufs], *sem_ops, *after)
    pairs = [(outs[2 * k], outs[2 * k + 1]) for k in range(npl)]
    return pairs, list(outs[2 * npl:2 * npl + n]), outs[-1]


def _own_to_near(x, y, c):
    return [((x, y, c), (x, y, 1 - c)), ((x, y, c), (1 - x, y, c)), ((x, y, c), (x, 1 - y, c))]


def _near_to_sibling(x, y, c):
    return [((1 - x, y, c), (x, y, 1 - c)), ((x, 1 - y, c), (x, y, 1 - c))]


def _relay_diagonal(x, y, c):
    north = c
    source = (x * north + (1 - x) * (1 - north), (1 - y) * north + y * (1 - north), c)
    target = ((1 - x) * north + x * (1 - north), y * north + (1 - y) * (1 - north), c)
    return [(source, target)]


def _diagonal_to_sibling(x, y, c):
    return [((1 - x, 1 - y, c), (x, y, 1 - c))]


def _gather_start(bufs, groups, name):
    n = len(bufs)
    rows = [b.shape[0] // N_DEV for b in bufs]
    ng = len(groups)

    def body(*refs):
        ins = refs[:n]
        sems = refs[n:n + 2 * ng]
        token = refs[-1]
        x, y, c = _position()
        targets = [(x, y, 1 - c)] + [(*chip, c) for chip in _other_chips(x, y)]
        for g, members in enumerate(groups):
            for slot, a in enumerate(members):
                own = _own_rows(ins[a], rows[a], x, y, c)
                for to in targets:
                    pltpu.make_async_remote_copy(src_ref=own, dst_ref=own, send_sem=sems[2 * g].at[slot],
                                                 recv_sem=sems[2 * g + 1].at[slot], device_id=to, device_id_type=MESH).start()
        token[...] = jnp.zeros_like(token)

    sem_shapes = []
    for members in groups:
        sem_shapes += [pltpu.SemaphoreType.DMA((len(members),))] * 2
    outs = _call(
        body, name=name, in_specs=[HBM_SPEC] * n, out_specs=[SEM_SPEC] * (2 * ng) + [HBM_SPEC] * n + [TOKEN_SPEC],
        out_shape=sem_shapes + [pltpu.HBM(b.shape, b.dtype) for b in bufs] + [TOKEN],
        input_output_aliases={i: 2 * ng + i for i in range(n)},
        compiler_params=pltpu.CompilerParams(has_side_effects=EFFECT),
    )(*[_hbm(b) for b in bufs])
    sem_pairs = [(outs[2 * g], outs[2 * g + 1]) for g in range(ng)]
    return sem_pairs, list(outs[2 * ng:2 * ng + n]), outs[-1]


def _gather_forward(send_sems, recv_sems, bufs, after, name):
    n = len(bufs)
    rows = [b.shape[0] // N_DEV for b in bufs]

    def body(*refs):
        ins = refs[:n]
        send1, recv1 = refs[n], refs[n + 1]
        out0 = n + 2 + len(after)
        send2, recv2 = refs[out0], refs[out0 + 1]
        token = refs[-1]
        x, y, c = _position()
        for a in range(n):
            step1 = pltpu.make_async_remote_copy(
                src_ref=_whole(ins[a], 4 * rows[a]), dst_ref=_whole(ins[a], 4 * rows[a]), send_sem=send1.at[a],
                recv_sem=recv1.at[a], device_id=(x, y, c), device_id_type=MESH)
            step1.wait_send()
            step1.wait_recv()
        for a in range(n):
            for chip in _other_chips(x, y):
                blk = _own_rows(ins[a], rows[a], *chip, c)
                pltpu.make_async_remote_copy(src_ref=blk, dst_ref=blk, send_sem=send2.at[a], recv_sem=recv2.at[a],
                                             device_id=(x, y, 1 - c), device_id_type=MESH).start()
        token[...] = jnp.zeros_like(token)

    outs = _call(
        body, name=name, in_specs=[HBM_SPEC] * n + [SEM_SPEC, SEM_SPEC] + [HBM_SPEC] * len(after),
        out_specs=[SEM_SPEC, SEM_SPEC] + [HBM_SPEC] * n + [TOKEN_SPEC],
        out_shape=[pltpu.SemaphoreType.DMA((n,)), pltpu.SemaphoreType.DMA((n,))]
        + [pltpu.HBM(b.shape, b.dtype) for b in bufs] + [TOKEN],
        input_output_aliases={i: 2 + i for i in range(n)},
        compiler_params=pltpu.CompilerParams(has_side_effects=EFFECT),
    )(*bufs, send_sems, recv_sems, *after)
    return outs[0], outs[1], list(outs[2:2 + n]), outs[-1]


def _gather_done(send_sems, recv_sems, bufs, after, name):
    n = len(bufs)
    rows = [b.shape[0] // N_DEV for b in bufs]

    def body(*refs):
        ins = refs[:n]
        send2, recv2 = refs[n], refs[n + 1]
        x, y, c = _position()
        for a in range(n):
            step2 = pltpu.make_async_remote_copy(
                src_ref=_whole(ins[a], 3 * rows[a]), dst_ref=_whole(ins[a], 3 * rows[a]), send_sem=send2.at[a],
                recv_sem=recv2.at[a], device_id=(x, y, c), device_id_type=MESH)
            step2.wait_send()
            step2.wait_recv()

    outs = _call(
        body, name=name, in_specs=[HBM_SPEC] * n + [SEM_SPEC, SEM_SPEC] + [HBM_SPEC] * len(after),
        out_specs=[HBM_SPEC] * n, out_shape=[pltpu.HBM(b.shape, b.dtype) for b in bufs],
        input_output_aliases={i: i for i in range(n)},
        compiler_params=pltpu.CompilerParams(has_side_effects=EFFECT),
    )(*bufs, send_sems, recv_sems, *after)
    return list(outs)


def _whole(ref, nrows):
    return ref.at[pl.ds(0, nrows), :]


def _to_sibling(x, y, c):
    return [(2 * q + (1 - c), q, (x, y, 1 - c)) for q in range(4)]


def _to_chips(x, y, c):
    return [(2 * px + py, j, (px, py, c)) for j, (px, py) in enumerate(_other_chips(x, y))]


def _exchange_start(srcs, src_slots, plan, name):
    n = len(srcs)
    rows = [a.shape[0] // src_slots for a in srcs]
    n_copies = len(plan(0, 0, 0))
    lands = [lax.empty((n_copies * r, a.shape[1]), a.dtype) for a, r in zip(srcs, rows)]

    def body(*refs):
        ins, land_refs = refs[:n], refs[n:2 * n]
        send_sems, recv_sems = refs[2 * n], refs[2 * n + 1]
        token = refs[-1]
        for a in range(n):
            r = rows[a]
            for src_slot, dst_slot, target in plan(*_position()):
                pltpu.make_async_remote_copy(
                    src_ref=ins[a].at[pl.ds(src_slot * r, r), :], dst_ref=land_refs[a].at[pl.ds(dst_slot * r, r), :],
                    send_sem=send_sems.at[a], recv_sem=recv_sems.at[a], device_id=target, device_id_type=MESH).start()
        token[...] = jnp.zeros_like(token)

    outs = _call(
        body, name=name, in_specs=[HBM_SPEC] * (2 * n),
        out_specs=[SEM_SPEC, SEM_SPEC] + [HBM_SPEC] * (2 * n) + [TOKEN_SPEC],
        out_shape=[pltpu.SemaphoreType.DMA((n,)), pltpu.SemaphoreType.DMA((n,))]
        + [pltpu.HBM(a.shape, a.dtype) for a in srcs] + [pltpu.HBM(l.shape, l.dtype) for l in lands] + [TOKEN],
        input_output_aliases={i: 2 + i for i in range(2 * n)},
        compiler_params=pltpu.CompilerParams(has_side_effects=EFFECT),
    )(*[_hbm(a) for a in srcs], *[_hbm(l) for l in lands])
    return outs[0], outs[1], list(outs[2:2 + n]), list(outs[2 + n:2 + 2 * n]), outs[-1]


def _exchange_wait(send_sems, recv_sems, srcs, lands, after, name):
    n = len(srcs)

    def body(*refs):
        ins, land_refs = refs[:n], refs[n:2 * n]
        send_sems_ref, recv_sems_ref = refs[2 * n], refs[2 * n + 1]
        for a in range(n):
            allrows = lands[a].shape[0]
            cp = pltpu.make_async_remote_copy(
                src_ref=_whole(ins[a], allrows), dst_ref=_whole(land_refs[a], allrows), send_sem=send_sems_ref.at[a],
                recv_sem=recv_sems_ref.at[a], device_id=_position(), device_id_type=MESH)
            cp.wait_send()
            cp.wait_recv()

    outs = _call(
        body, name=name, in_specs=[HBM_SPEC] * (2 * n) + [SEM_SPEC, SEM_SPEC] + [HBM_SPEC] * len(after),
        out_specs=[HBM_SPEC] * (2 * n),
        out_shape=[pltpu.HBM(a.shape, a.dtype) for a in srcs] + [pltpu.HBM(l.shape, l.dtype) for l in lands],
        input_output_aliases={i: i for i in range(2 * n)},
        compiler_params=pltpu.CompilerParams(has_side_effects=EFFECT),
    )(*srcs, *lands, send_sems, recv_sems, *after)
    return list(outs[:n]), list(outs[n:])


def _chip_partial(grad, recv, idx, name):
    r = recv.shape[0] // 4

    def body(i_ref, g_ref, s_ref, o_ref):
        del i_ref
        o_ref[...] = (g_ref[...].astype(F32) + s_ref[...].astype(F32)).astype(BF16)

    nb = 1
    tr = r // nb
    grid_spec = pltpu.PrefetchScalarGridSpec(
        num_scalar_prefetch=1, grid=(3, nb),
        in_specs=[pl.BlockSpec((tr, D), lambda t, i, i_ref: ((2 * i_ref[1 + t] + i_ref[0]) * nb + i, 0)),
                  pl.BlockSpec((tr, D), lambda t, i, i_ref: (i_ref[1 + t] * nb + i, 0))],
        out_specs=pl.BlockSpec((tr, D), lambda t, i, i_ref: (i_ref[1 + t] * nb + i, 0)))
    return _call(body, name=name, grid_spec=grid_spec, out_shape=_sds((4 * r, D), BF16),
                 compiler_params=_params("arbitrary", "arbitrary"))(idx, grad, recv)


def _adamw_math(w, g, m, v):
    m2 = B1 * m + (1.0 - B1) * g
    v2 = B2 * v + (1.0 - B2) * (g * g)
    m_hat = m2 / (1.0 - B1 ** STEP)
    v_hat = v2 / (1.0 - B2 ** STEP)
    return -LR * (m_hat / (jnp.sqrt(v_hat) + EPS_ADAM) + WD * w), m2, v2


def _reduce_adamw(w, grad, from_sibling, from_chips, idx, m, v, name, update=True):
    r = grad.shape[0] // N_DEV
    assert from_sibling.shape == (4 * r, D) and from_chips.shape == (3 * r, D)
    tr = r // 2
    nb = r // tr

    def body(i_ref, p_ref, s_ref, r0_ref, r1_ref, r2_ref, *rest):
        g = p_ref[...].astype(F32) + s_ref[...].astype(F32)
        g = ((g + r0_ref[...].astype(F32)) + r1_ref[...].astype(F32)) + r2_ref[...].astype(F32)
        if update:
            w_ref, m_ref, v_ref, g_ref, d_ref, nm_ref, nv_ref = rest
            d_ref[...], nm_ref[...], nv_ref[...] = _adamw_math(w_ref[...], g, m_ref[...], v_ref[...])
        else:
            g_ref, = rest
        g_ref[...] = g

    own = pl.BlockSpec((tr, D), lambda i, i_ref: (i, 0))
    grid_spec = pltpu.PrefetchScalarGridSpec(
        num_scalar_prefetch=1, grid=(nb,),
        in_specs=[pl.BlockSpec((tr, D), lambda i, i_ref: (i_ref[0] * nb + i, 0)),
                  pl.BlockSpec((tr, D), lambda i, i_ref: (i_ref[1] * nb + i, 0))]
        + [pl.BlockSpec((tr, D), lambda i, i_ref, j=j: (j * nb + i, 0)) for j in range(3)] + ([own] * 3 if update else []),
        out_specs=[own] * (4 if update else 1))
    return _call(body, name=name, grid_spec=grid_spec, out_shape=[_sds((r, D), F32)] * (4 if update else 1),
                 compiler_params=_params("parallel"))(
        idx, grad, from_sibling, from_chips, from_chips, from_chips, *((w, m, v) if update else ()))


SMALL_ROWS = 8


def _small_all_reduce(pack, name, after=()):
    def body(p_ref, *rest):
        tot_ref, loss_ref, gath, send_sems, recv_sems = rest[len(after):]
        x, y, c = _position()
        me_id = 4 * x + 2 * y + c
        gath[me_id] = p_ref[...]
        copies = []
        for k in range(1, N_DEV):
            peer = tuple(1 - v if (k >> b) & 1 else v for v, b in ((x, 2), (y, 1), (c, 0)))
            cp = pltpu.make_async_remote_copy(src_ref=p_ref, dst_ref=gath.at[me_id], send_sem=send_sems.at[k - 1],
                                              recv_sem=recv_sems.at[k - 1], device_id=peer, device_id_type=MESH)
            cp.start()
            copies.append(cp)
        for cp in copies:
            cp.wait_recv()
        for cp in copies:
            cp.wait_send()
        tot = gath[0]
        for d in range(1, N_DEV):
            tot = tot + gath[d]
        tot_ref[...] = tot
        loss_ref[...] = jnp.full((1, 128), (0.5 / D) * jnp.sum(tot[SMALL_ROWS - 1:SMALL_ROWS, :]), F32)

    vm = pl.BlockSpec(memory_space=pltpu.VMEM)
    return _call(
        body, name=name, in_specs=[vm] + [HBM_SPEC] * len(after), out_specs=[vm, vm],
        out_shape=[_sds((SMALL_ROWS, D), F32), _sds((1, 128), F32)],
        scratch_shapes=[pltpu.VMEM((N_DEV, SMALL_ROWS, D), F32), pltpu.SemaphoreType.DMA((N_DEV - 1,)),
                        pltpu.SemaphoreType.DMA((N_DEV - 1,))],
    )(pack, *after)


SC_TILES = 32
SC_LANES = 16
SC_CHUNK_MAX = 8192


def _sc_adamw(w, g, m, v, name):
    shape = w.shape
    n = w.size
    per_tile = n // SC_TILES
    assert n % (SC_TILES * SC_LANES) == 0
    chunk = max(c for c in range(SC_LANES, min(per_tile, SC_CHUNK_MAX) + 1, SC_LANES) if per_tile % c == 0)

    def body(w_hbm, g_hbm, m_hbm, v_hbm, d_hbm, nm_hbm, nv_hbm, wb, gb, mb, vb, db):
        tile = lax.axis_index("subcore") * 2 + lax.axis_index("core")

        @pl.loop(0, per_tile, step=chunk)
        def _(c0):
            span = pl.ds(tile * per_tile + c0, chunk)
            for hbm, buf in ((w_hbm, wb), (g_hbm, gb), (m_hbm, mb), (v_hbm, vb)):
                pltpu.sync_copy(hbm.at[span], buf)

            @pl.loop(0, chunk, step=SC_LANES)
            def _(i):
                s = pl.ds(i, SC_LANES)
                db[s], mb[s], vb[s] = _adamw_math(wb[s], gb[s], mb[s], vb[s])

            for buf, hbm in ((db, d_hbm), (mb, nm_hbm), (vb, nv_hbm)):
                pltpu.sync_copy(buf, hbm.at[span])

    flat = _sds((n,), F32)
    outs = pl.kernel(
        body, name=name, out_type=[flat, flat, flat],
        mesh=plsc.VectorSubcoreMesh(core_axis_name="core", subcore_axis_name="subcore"),
        scratch_types=[pltpu.VMEM((chunk,), F32)] * 5,
    )(w.reshape(n), g.reshape(n), m.reshape(n), v.reshape(n))
    return [o.reshape(shape) for o in outs]


def _adamw_small(ws, gs, ms, vs, name):
    n = len(ws)

    def body(*refs):
        for a in range(n):
            w_ref, g_ref, m_ref, v_ref = (refs[k * n + a] for k in range(4))
            d_ref, nm_ref, nv_ref = (refs[(4 + k) * n + a] for k in range(3))
            d_ref[...], nm_ref[...], nv_ref[...] = _adamw_math(w_ref[...], g_ref[...], m_ref[...], v_ref[...])

    vm = pl.BlockSpec(memory_space=pltpu.VMEM)
    outs = _call(body, name=name, in_specs=[vm] * (4 * n), out_specs=[vm] * (3 * n),
                 out_shape=[_sds(w.shape, F32) for w in ws] * 3)(*ws, *gs, *ms, *vs)
    return [(outs[a], outs[n + a], outs[2 * n + a]) for a in range(n)]


def kernel(x, g_mix, w_in, conv_w, attn_sinks, w_conv_out, w_attn_out, w_o, g_ffn, w_gate_up, w_down, g_final, loss_target, m_g_mix, m_w_in, m_conv_w, m_attn_sinks, m_w_conv_out, m_w_attn_out, m_w_o, m_g_ffn, m_w_gate_up, m_w_down, m_g_final, v_g_mix, v_w_in, v_conv_w, v_attn_sinks, v_w_conv_out, v_w_attn_out, v_w_o, v_g_ffn, v_w_gate_up, v_w_down, v_g_final):
    cx, cy, cc = _position()
    chip = 2 * cx + cy
    partial_idx = jnp.stack([cc, 2 * (1 - cx) + cy, 2 * cx + (1 - cy), 2 * (1 - cx) + (1 - cy)]).astype(jnp.int32)
    own_idx = jnp.stack([2 * chip + cc, chip]).astype(jnp.int32)
    me = 4 * cx + 2 * cy + cc

    me_idx = jnp.reshape(me, (1,)).astype(jnp.int32)
    first = [_place(jnp.transpose(w_in[0]), me_idx, BF16, "place_w_in"),
             _place(jnp.pad(conv_w[0], ((0, 5), (0, 0))), me_idx, F32, "place_conv_w")]
    (to_near,), first, token_in = _gather_phase(first, [], [_own_to_near], (), "gather_in_start")
    gather_tokens = (token_in,)

    class Gathered:
        def __init__(self):
            self.state = {}

        def begin(self, group, after):
            if group == "in":
                (near, relay), bufs, token = _gather_phase(
                    first, [(*to_near, 3, 3)], [_near_to_sibling, _relay_diagonal], after, "gather_in_relay")
                later = [_place(w, me_idx, BF16, "place_" + k, after=(token,)) for k, w in (
                    ("w_conv_out", w_conv_out[0]), ("w_attn_out", w_attn_out[0]), ("w_o", w_o[0]),
                    ("w_gate_up", jnp.transpose(w_gate_up[0])), ("w_down", w_down[0]))]
                (sems_mix, sems_ffn), later, token_later = _gather_start(later, [[0, 1, 2], [3, 4]], "gather_start_later")
                self.state.update({"in": (near, relay, bufs), "mix": (sems_mix, later[:3]), "ffn": (sems_ffn, later[3:])})
                return (token_later,)
            (send_sems, recv_sems), group_bufs = self.state[group]
            send2, recv2, group_bufs, token = _gather_forward(send_sems, recv_sems, group_bufs, after, "gather_forward_" + group)
            self.state[group] = ((send2, recv2), group_bufs)
            return (token,)

        def end(self, group, after):
            if group == "in":
                near, relay, bufs = self.state[group]
                (last,), bufs, token = _gather_phase(bufs, [(*relay, 1, 1)], [_diagonal_to_sibling], after, "gather_in_last")
                _, full, _ = _gather_phase(bufs, [(*near, 2, 2), (*last, 1, 1)], [], (token,), "gather_in_done")
                return full[0], jnp.transpose(full[1].reshape(N_DEV, 8, 128)[:, :3, :], (1, 0, 2)).reshape(3, D)
            (send2, recv2), group_bufs = self.state[group]
            return _gather_done(send2, recv2, group_bufs, after, "gather_done_" + group)

    in_flight, own_pieces = {}, {}

    transposed = ("w_in", "w_gate_up")

    def as2d(k, a):
        if k in transposed:
            return jnp.transpose(a[0])
        return a[None] if a.ndim == 1 else (a[0] if a.ndim == 3 else a)

    w_all = {"g_mix": g_mix, "w_in": w_in, "conv_w": conv_w, "attn_sinks": attn_sinks, "w_conv_out": w_conv_out,
             "w_attn_out": w_attn_out, "w_o": w_o, "g_ffn": g_ffn, "w_gate_up": w_gate_up, "w_down": w_down, "g_final": g_final}
    m_all = {"g_mix": m_g_mix, "w_in": m_w_in, "conv_w": m_conv_w, "attn_sinks": m_attn_sinks, "w_conv_out": m_w_conv_out,
             "w_attn_out": m_w_attn_out, "w_o": m_w_o, "g_ffn": m_g_ffn, "w_gate_up": m_w_gate_up, "w_down": m_w_down,
             "g_final": m_g_final}
    v_all = {"g_mix": v_g_mix, "w_in": v_w_in, "conv_w": v_conv_w, "attn_sinks": v_attn_sinks, "w_conv_out": v_w_conv_out,
             "w_attn_out": v_w_attn_out, "w_o": v_w_o, "g_ffn": v_g_ffn, "w_gate_up": v_w_gate_up, "w_down": v_w_down,
             "g_final": v_g_final}
    results = {}

    def record(k, *vals):
        results[k] = [(jnp.transpose(val) if k in transposed else val).reshape(w_all[k].shape) for val in vals]

    def update(k, pieces, sparse):
        w2, m2, v2 = as2d(k, w_all[k]), as2d(k, m_all[k]), as2d(k, v_all[k])
        if sparse:
            g, = _reduce_adamw(w2, *pieces, own_idx, m2, v2, "grad_" + k, update=False)
            record(k, g, *_sc_adamw(w2, g, m2, v2, "adamw_" + k))
            return g
        g, d, nm, nv = _reduce_adamw(w2, *pieces, own_idx, m2, v2, "adamw_" + k)
        record(k, g, d, nm, nv)
        return nm

    def update_small(grads):
        keys = list(grads)
        outs = _adamw_small([as2d(k, w_all[k]) for k in keys], [grads[k] for k in keys], [as2d(k, m_all[k]) for k in keys],
                            [as2d(k, v_all[k]) for k in keys], "adamw_small")
        for k, (d, nm, nv) in zip(keys, outs):
            record(k, grads[k], d, nm, nv)
        return tuple(nm for _, nm, _ in outs)

    kernel_name = {"win_t": "w_in", "wgu_t": "w_gate_up", "wd": "w_down", "wco": "w_conv_out", "wao": "w_attn_out", "wo": "w_o"}

    def finish(group, after):
        keys, send_sems, recv_sems, parts, from_chips = in_flight[group]
        _, from_chips = _exchange_wait(send_sems, recv_sems, parts, from_chips, after, "rs_chips_wait_" + group)
        grads, from_sibling = own_pieces[group]
        return tuple(update(kernel_name[k], p, sparse=group != "in") for k, *p in zip(keys, grads, from_sibling, from_chips))

    class Reducer:
        def start(self, group, gdict):
            keys, glist = list(gdict), list(gdict.values())
            send_sems, recv_sems, glist, lands, token = _exchange_start(glist, N_DEV, _to_sibling, "rs_sibling_start_" + group)
            in_flight[group] = (keys, send_sems, recv_sems, glist, lands)
            return (token,)

        def middle(self, group, after):
            keys, send_sems, recv_sems, glist, lands = in_flight[group]
            if group == "in":
                after = finish("ffn", after)
            glist, lands = _exchange_wait(send_sems, recv_sems, glist, lands, after, "rs_sibling_wait_" + group)
            parts = [_chip_partial(g, r, partial_idx, "chip_partial_" + k) for k, g, r in zip(keys, glist, lands)]
            send_sems, recv_sems, parts, from_chips, token = _exchange_start(parts, 4, _to_chips, "rs_chips_start_" + group)
            in_flight[group] = (keys, send_sems, recv_sems, parts, from_chips)
            own_pieces[group] = (glist, lands)
            return (token,)

    dx, _, small = _local_step(x[0], loss_target[0], g_mix, g_ffn, g_final[None], attn_sinks, Gathered(),
                               reducer=Reducer(), after=gather_tokens)
    after = finish("mix", (dx,))

    sinks_row = jnp.pad(small["sinks"], ((0, 0), (0, D - 128)))
    pack = jnp.concatenate([small["g_mix"], small["g_ffn"], small["g_final"], small["conv_w"], sinks_row, small["lossvec"]], axis=0)
    tot, loss_row = _small_all_reduce(pack, "small_all_reduce", after=after)
    loss = loss_row[0, 0]
    g_small = {
        "g_mix": tot[0:1], "g_ffn": tot[1:2], "g_final": tot[2:3],
        "conv_w": lax.dynamic_slice(tot, (3, me * 128), (3, 128)), "attn_sinks": tot[6:7, :N_HEADS],
    }
    finish("in", update_small(g_small))

    order = ["g_mix", "w_in", "conv_w", "attn_sinks", "w_conv_out", "w_attn_out", "w_o", "g_ffn", "w_gate_up", "w_down", "g_final"]
    return (loss, dx[None], *[results[k][i] for i in range(4) for k in order])
```

```python
import functools
import math

import jax
import jax.numpy as jnp
from jax import lax
from jax.experimental import pallas as pl
from jax.experimental.pallas import tpu as pltpu

F32 = jnp.float32
BF16 = jnp.bfloat16

D = 1024
HEAD_DIM = 64
N_HEADS = 16
N_KV = 4
GROUP = N_HEADS // N_KV
D_KV = N_KV * HEAD_DIM
BLOCK = 128
ROT_DIM = HEAD_DIM // 4
ROPE_THETA = 500000.0
ATTN_SCALE = 1.0 / math.sqrt(HEAD_DIM)
NEG_INF = -1e30
D_FF = 2816
N_IN = 6656
EPS = 1e-5
C_CB, C_CC, C_CX, C_Q, C_K, C_V, C_GC, C_GA = 0, 1024, 2048, 3072, 4096, 4352, 4608, 5632

LR, B1, B2, EPS_ADAM, WD, STEP = 0.001, 0.9, 0.999, 1e-08, 0.01, 10

N_DEV = 8
MESH = pl.DeviceIdType.MESH
VMEM_LIMIT = 56 * 1024 * 1024

NN = (((1,), (0,)), ((), ()))
NT = (((1,), (1,)), ((), ()))
TN = (((0,), (0,)), ((), ()))
HBM_SPEC = pl.BlockSpec(memory_space=pl.ANY)
ROW_SPLIT = 4


def _call(body, **kw):
    return pl.pallas_call(body, **kw)


def _params(*sem):
    return pltpu.CompilerParams(dimension_semantics=sem, vmem_limit_bytes=VMEM_LIMIT)


def _sds(shape, dtype):
    return jax.ShapeDtypeStruct(shape, dtype)


def _matmul(a, b, *, mode, tm, tn, tk, out_dtype, name, res=None, after=()):
    parts = list(a) if isinstance(a, (list, tuple)) else [a]
    rows_a = parts[0].shape[0]
    cols_a = sum(p.shape[1] for p in parts)
    if mode == "nn":
        (m, kk), (_, n), dims = (rows_a, cols_a), b.shape, NN
    elif mode == "nt":
        (m, kk), (n, _), dims = (rows_a, cols_a), b.shape, NT
    else:
        (kk, m), (_, n), dims = (rows_a, cols_a), b.shape, TN
    tm, tn, tk = min(tm, m), min(tn, n), min(tk, kk)
    assert m % tm == 0 and n % tn == 0 and kk % tk == 0, (name, m, n, kk, tm, tn, tk)
    nk = kk // tk
    split_axis, width = (2, tk) if mode == "nn" else (0, tm)
    assert len(parts) == 1 or mode in ("nn", "tn")
    assert len(parts) == 1 or all(p.shape[1] % width == 0 for p in parts), (name, width)
    counts = [p.shape[1] // width for p in parts]
    starts = [sum(counts[:p]) for p in range(len(parts))]

    def a_spec(p):
        def col(t):
            return jnp.clip(t - starts[p], 0, counts[p] - 1) if len(parts) > 1 else t

        if mode == "tn":
            return pl.BlockSpec((tk, tm), lambda i, j, k: (k, col(i)))
        return pl.BlockSpec((tm, tk), lambda i, j, k: (i, col(k)))

    if mode == "nt":
        b_spec = pl.BlockSpec((tn, tk), lambda i, j, k: (j, k))
    else:
        b_spec = pl.BlockSpec((tk, tn), lambda i, j, k: (k, j))
    o_spec = pl.BlockSpec((tm, tn), lambda i, j, k: (i, j))
    has_res = res is not None
    n_parts = len(parts)
    unit = 128 if mode == "tn" else 16
    split = ROW_SPLIT if tm % (ROW_SPLIT * unit) == 0 else 1

    def body(*refs):
        a_refs, b_ref = refs[:n_parts], refs[n_parts]
        r_ref = refs[n_parts + 1] if has_res else None
        o_ref = refs[n_parts + 1 + has_res + len(after)]
        k = pl.program_id(2)

        acc_ref = refs[-1] if nk > 1 else None

        def step(a_ref):
            def matmul(rows):
                a_blk = a_ref[:, rows] if mode == "tn" else a_ref[rows, :]
                return lax.dot_general(a_blk, b_ref[...], dims, preferred_element_type=F32)

            def finish(rows, part):
                if nk > 1:
                    acc_ref[rows, :] += part
                else:
                    o_ref[rows, :] = (part + r_ref[rows, :] if has_res else part).astype(o_ref.dtype)

            _row_pipeline(tm, matmul, finish, split)

        if nk > 1:
            @pl.when(k == 0)
            def _():
                acc_ref[...] = jnp.zeros_like(acc_ref)

        if n_parts == 1:
            step(a_refs[0])
        else:
            t = pl.program_id(split_axis)
            for p in range(n_parts):
                pl.when((t >= starts[p]) & (t < starts[p] + counts[p]))(functools.partial(step, a_refs[p]))

        if nk > 1:
            @pl.when(k == nk - 1)
            def _():
                o_ref[...] = (acc_ref[...] + r_ref[...] if has_res else acc_ref[...]).astype(o_ref.dtype)

    ins = parts + [b] + ([res] if has_res else []) + list(after)
    in_specs = [a_spec(p) for p in range(n_parts)] + [b_spec] + ([o_spec] if has_res else []) + [HBM_SPEC] * len(after)
    scratch = [] if nk == 1 else [pltpu.VMEM((tm, tn), F32)]
    return _call(
        body, name=name, grid=(m // tm, n // tn, nk), in_specs=in_specs, out_specs=o_spec,
        out_shape=_sds((m, n), out_dtype), scratch_shapes=scratch,
        compiler_params=_params("parallel", "parallel", "arbitrary"),
    )(*ins)


def _row_tile(s):
    return min(512, s)


def _rms_fwd(x, g, name, after=()):
    s = x.shape[0]
    tm = _row_tile(s)

    def body(x_ref, g_ref, *rest):
        h_ref = rest[-1]
        xv = x_ref[...]
        r = lax.rsqrt(jnp.mean(xv * xv, axis=-1, keepdims=True) + EPS)
        h_ref[...] = (xv * r * g_ref[...]).astype(BF16)

    row = pl.BlockSpec((tm, D), lambda i: (i, 0))
    return _call(
        body, name=name, grid=(s // tm,), in_specs=[row, pl.BlockSpec((1, D), lambda i: (0, 0))] + [HBM_SPEC] * len(after),
        out_specs=row, out_shape=_sds((s, D), BF16), compiler_params=_params("parallel"),
    )(x, g, *after)


def _rms_bwd(dh, x, g, dres, name, after=()):
    s = x.shape[0]
    tm = _row_tile(s)

    def body(dh_ref, x_ref, g_ref, dres_ref, *rest):
        dx_ref, dxb_ref, dg_ref = rest[len(after):]
        xv = x_ref[...]
        r = lax.rsqrt(jnp.mean(xv * xv, axis=-1, keepdims=True) + EPS)
        xh = xv * r
        dhv = dh_ref[...].astype(F32)
        dyg = dhv * g_ref[...]
        dx = dres_ref[...] + r * (dyg - xh * jnp.mean(dyg * xh, axis=-1, keepdims=True))
        dx_ref[...] = dx
        dxb_ref[...] = dx.astype(BF16)
        part = jnp.sum(dhv * xh, axis=0, keepdims=True)

        @pl.when(pl.program_id(0) == 0)
        def _():
            dg_ref[...] = part

        @pl.when(pl.program_id(0) > 0)
        def _():
            dg_ref[...] += part

    row = pl.BlockSpec((tm, D), lambda i: (i, 0))
    vec = pl.BlockSpec((1, D), lambda i: (0, 0))
    return _call(
        body, name=name, grid=(s // tm,), in_specs=[row, row, vec, row] + [HBM_SPEC] * len(after), out_specs=[row, row, vec],
        out_shape=[_sds((s, D), F32), _sds((s, D), BF16), _sds((1, D), F32)],
        compiler_params=_params("arbitrary"),
    )(dh, x, g, dres, *after)


def _loss_head(x2, g, tgt, name):
    s = x2.shape[0]
    tm = _row_tile(s)

    def body(x_ref, g_ref, t_ref, dx_ref, dxb_ref, dg_ref, l_ref):
        xv = x_ref[...]
        gv = g_ref[...]
        r = lax.rsqrt(jnp.mean(xv * xv, axis=-1, keepdims=True) + EPS)
        xh = xv * r
        err = xh * gv - t_ref[...]
        dy = err * (1.0 / D)
        dyg = dy * gv
        dx = r * (dyg - xh * jnp.mean(dyg * xh, axis=-1, keepdims=True))
        dx_ref[...] = dx
        dxb_ref[...] = dx.astype(BF16)
        dg_part = jnp.sum(dy * xh, axis=0, keepdims=True)
        l_part = jnp.sum(err * err, axis=0, keepdims=True)

        @pl.when(pl.program_id(0) == 0)
        def _():
            dg_ref[...] = dg_part
            l_ref[...] = l_part

        @pl.when(pl.program_id(0) > 0)
        def _():
            dg_ref[...] += dg_part
            l_ref[...] += l_part

    row = pl.BlockSpec((tm, D), lambda i: (i, 0))
    vec = pl.BlockSpec((1, D), lambda i: (0, 0))
    return _call(
        body, name=name, grid=(s // tm,), in_specs=[row, vec, row], out_specs=[row, row, vec, vec],
        out_shape=[_sds((s, D), F32), _sds((s, D), BF16), _sds((1, D), F32), _sds((1, D), F32)],
        compiler_params=_params("arbitrary"),
    )(x2, g, tgt)


CONV_TC = 256


def _shift_down(u, k, rows):
    return jnp.where(rows >= k, pltpu.roll(u, k, 0), 0.0)


def _shift_up(u, k, rows, s):
    return jnp.where(rows < s - k, pltpu.roll(u, s - k, 0), 0.0)


def _conv_specs(s):
    nb = D // CONV_TC

    def col(c0):
        return pl.BlockSpec((s, CONV_TC), lambda j, c0=c0: (0, c0 // CONV_TC + j))

    return nb, col


def _conv_fwd(proj, conv_w, name):
    s = proj.shape[0]
    nb, col = _conv_specs(s)

    def body(cb_ref, cc_ref, cx_ref, w_ref, y_ref):
        rows = lax.broadcasted_iota(jnp.int32, (s, CONV_TC), 0)
        u = cc_ref[...].astype(F32) * cx_ref[...].astype(F32)
        w = w_ref[...]
        c = w[0:1] * _shift_down(u, 2, rows) + w[1:2] * _shift_down(u, 1, rows) + w[2:3] * u
        y_ref[...] = (cb_ref[...].astype(F32) * c).astype(BF16)

    return _call(
        body, name=name, grid=(nb,),
        in_specs=[col(C_CB), col(C_CC), col(C_CX), pl.BlockSpec((3, CONV_TC), lambda j: (0, j))],
        out_specs=pl.BlockSpec((s, CONV_TC), lambda j: (0, j)), out_shape=_sds((s, D), BF16),
        compiler_params=_params("parallel"),
    )(proj, proj, proj, conv_w)


def _write_behind(t, nt, buf, sems, tiles, window, where):
    slot = t % 2

    def copies(sl, at):
        return [pltpu.make_async_copy(buf.at[sl, p], window(p, at), sems.at[sl, p]) for p in range(len(tiles))]

    @pl.when(t >= 2)
    def _():
        for cp in copies(slot, where):
            cp.wait()

    for p, tile in enumerate(tiles):
        buf[slot, p] = tile
    started = copies(slot, where)
    for cp in started:
        cp.start()

    @pl.when(t == nt - 1)
    def _():
        for cp in started:
            cp.wait()
        if nt > 1:
            for cp in copies(1 - slot, where):
                cp.wait()


def _conv_bwd(dy, proj, conv_w, dproj, name, after=()):
    s = proj.shape[0]
    nb, col = _conv_specs(s)

    def body(dy_ref, cb_ref, cc_ref, cx_ref, w_ref, *rest):
        dproj_ref, dw_ref, buf, sems = rest[1 + len(after):]
        j = pl.program_id(0)
        rows = lax.broadcasted_iota(jnp.int32, (s, CONV_TC), 0)
        cc = cc_ref[...].astype(F32)
        cx = cx_ref[...].astype(F32)
        u = cc * cx
        u1 = _shift_down(u, 1, rows)
        u2 = _shift_down(u, 2, rows)
        w = w_ref[...]
        c = w[0:1] * u2 + w[1:2] * u1 + w[2:3] * u
        dyv = dy_ref[...].astype(F32)
        dc = dyv * cb_ref[...].astype(F32)
        du = w[2:3] * dc + w[1:2] * _shift_up(dc, 1, rows, s) + w[0:1] * _shift_up(dc, 2, rows, s)

        def window(p, jj):
            start = pl.multiple_of((C_CB, C_CC, C_CX)[p] + jj * CONV_TC, CONV_TC)
            return dproj_ref.at[:, pl.ds(start, CONV_TC)]

        tiles = ((dyv * c).astype(BF16), (du * cx).astype(BF16), (du * cc).astype(BF16))
        _write_behind(j * 0, 1, buf, sems, tiles, window, j)
        dw_ref[...] = jnp.concatenate(
            [jnp.sum(dc * u2, axis=0, keepdims=True), jnp.sum(dc * u1, axis=0, keepdims=True),
             jnp.sum(dc * u, axis=0, keepdims=True)], axis=0)

    return _call(
        body, name=name, grid=(nb,),
        in_specs=[pl.BlockSpec((s, CONV_TC), lambda j: (0, j)), col(C_CB), col(C_CC), col(C_CX),
                  pl.BlockSpec((3, CONV_TC), lambda j: (0, j))] + [HBM_SPEC] * (1 + len(after)),
        out_specs=[pl.BlockSpec(memory_space=pl.ANY), pl.BlockSpec((3, CONV_TC), lambda j: (0, j))],
        out_shape=[_sds((s, N_IN), BF16), _sds((3, D), F32)],
        scratch_shapes=[pltpu.VMEM((1, 3, s, CONV_TC), BF16), pltpu.SemaphoreType.DMA((1, 3))],
        input_output_aliases={5: 0}, compiler_params=_params("arbitrary"),
    )(dy, proj, proj, proj, conv_w, dproj, *after)


def _rope_tables(s):
    half = ROT_DIM // 2
    inv_freq = ROPE_THETA ** (-jnp.arange(0, ROT_DIM, 2, dtype=F32) / ROT_DIM)
    inv64 = jnp.concatenate([inv_freq, inv_freq, jnp.zeros((HEAD_DIM - ROT_DIM,), F32)])
    ang = jnp.arange(s, dtype=F32)[:, None] * jnp.concatenate([inv64, inv64])[None, :]
    d = lax.broadcasted_iota(jnp.int32, (s, 128), 1) % HEAD_DIM
    cos, sin = jnp.cos(ang), jnp.sin(ang)
    c = jnp.where(d < ROT_DIM, cos, 1.0)
    a = jnp.where(d < half, -sin, 0.0)
    b = jnp.where((d >= half) & (d < ROT_DIM), sin, 0.0)
    return jnp.concatenate([c, a, b], axis=1)


def _rope(x, tab):
    c, a, b = tab[:, 0:128], tab[:, 128:256], tab[:, 256:384]
    outs = []
    for i in range(x.shape[1] // 128):
        xc = x[:, i * 128:(i + 1) * 128]
        outs.append(xc * c + pltpu.roll(xc, 120, 1) * a + pltpu.roll(xc, 8, 1) * b)
    return outs[0] if len(outs) == 1 else jnp.concatenate(outs, axis=1)


def _rope_t(dx, tab):
    c, a, b = tab[:, 0:128], tab[:, 128:256], tab[:, 256:384]
    outs = []
    for i in range(dx.shape[1] // 128):
        dc = dx[:, i * 128:(i + 1) * 128]
        outs.append(dc * c + pltpu.roll(dc * a, 8, 1) + pltpu.roll(dc * b, 120, 1))
    return outs[0] if len(outs) == 1 else jnp.concatenate(outs, axis=1)


def _attn_in_specs():
    prev = lambda n: jnp.maximum(n - 1, 0)
    return [
        pl.BlockSpec((BLOCK, D), lambda n: (n, C_Q // D)),
        pl.BlockSpec((BLOCK, D_KV), lambda n: (n, C_K // D_KV)),
        pl.BlockSpec((BLOCK, D_KV), lambda n: (prev(n), C_K // D_KV)),
        pl.BlockSpec((BLOCK, D_KV), lambda n: (n, C_V // D_KV)),
        pl.BlockSpec((BLOCK, D_KV), lambda n: (prev(n), C_V // D_KV)),
        pl.BlockSpec((BLOCK, 384), lambda n: (n, 0)),
        pl.BlockSpec((BLOCK, 384), lambda n: (prev(n), 0)),
        pl.BlockSpec(memory_space=pltpu.SMEM),
    ]


HALF = HEAD_DIM
N_CHUNK = D // 128


def _swa_bias(n):
    qi = lax.broadcasted_iota(jnp.int32, (BLOCK, 2 * BLOCK), 0)
    kj = lax.broadcasted_iota(jnp.int32, (BLOCK, 2 * BLOCK), 1)
    rel = qi + BLOCK - kj
    valid = (rel >= 0) & (rel < BLOCK) & ((kj >= BLOCK) | (n > 0))
    return jnp.where(valid, 0.0, NEG_INF)


def _halves(x):
    lo = lax.broadcasted_iota(jnp.int32, x.shape, 1) < HALF
    return jnp.where(lo, x, 0.0).astype(BF16), jnp.where(lo, 0.0, x).astype(BF16)


def _dup_heads(x):
    out = []
    for pair in range(N_KV // 2):
        xc = x[:, pair * 128:(pair + 1) * 128]
        xr = pltpu.roll(xc, HALF, 1)
        lo = lax.broadcasted_iota(jnp.int32, xc.shape, 1) < HALF
        out += [jnp.where(lo, xc, xr), jnp.where(lo, xr, xc)]
    return out


def _swa_load(q_ref, kc_ref, kp_ref, vc_ref, vp_ref, tc_ref, tp_ref):
    qf = _rope(q_ref[...].astype(F32), tc_ref[...]) * ATTN_SCALE
    q_halves = [_halves(qf[:, c * 128:(c + 1) * 128]) for c in range(N_CHUNK)]
    kf = jnp.concatenate([_rope(kp_ref[...].astype(F32), tp_ref[...]), _rope(kc_ref[...].astype(F32), tc_ref[...])], axis=0)
    vf = jnp.concatenate([vp_ref[...], vc_ref[...]], axis=0).astype(F32)
    return q_halves, _dup_heads(kf), _dup_heads(vf)


def _swa_probs(qh, kk, bias, sink):
    s = lax.dot_general(qh, kk, NT, preferred_element_type=F32) + bias
    m = jnp.maximum(jnp.max(jnp.maximum(s[:, :BLOCK], s[:, BLOCK:]), axis=1, keepdims=True), sink)
    return jnp.exp(s - m), m


def _swa_fwd(proj, tab, sinks, name, after=()):
    s = proj.shape[0]

    def body(q_ref, kc_ref, kp_ref, vc_ref, vp_ref, tc_ref, tp_ref, sink_ref, *rest):
        o_ref = rest[-1]
        n = pl.program_id(0)
        q_halves, kdup, vdup = _swa_load(q_ref, kc_ref, kp_ref, vc_ref, vp_ref, tc_ref, tp_ref)
        bias = _swa_bias(n)
        ones = jnp.ones((2 * BLOCK, 128), BF16)
        kk = [k.astype(BF16) for k in kdup]
        vv = [[jnp.concatenate([v_half, ones], axis=1) for v_half in _halves(v)] for v in vdup]
        heads = [(c, half) for c in range(N_CHUNK) for half in range(2)]
        scores = [lax.dot_general(q_halves[c][half], kk[c // (GROUP // 2)], NT, preferred_element_type=F32)
                  for c, half in heads]
        probs = []
        for (c, half), sc in zip(heads, scores):
            sc = sc + bias
            m = jnp.maximum(jnp.max(jnp.maximum(sc[:, :BLOCK], sc[:, BLOCK:]), axis=1, keepdims=True), sink_ref[0, 2 * c + half])
            probs.append((jnp.exp(sc - m).astype(BF16), jnp.exp(sink_ref[0, 2 * c + half] - m)))
        outs = [lax.dot_general(e, vv[c // (GROUP // 2)][half], NN, preferred_element_type=F32)
                for (c, half), (e, _) in zip(heads, probs)]
        for c in range(N_CHUNK):
            parts = [outs[2 * c + half][:, :128] * (1.0 / (outs[2 * c + half][:, 128:] + probs[2 * c + half][1]))
                     for half in range(2)]
            o_ref[:, c * 128:(c + 1) * 128] = (parts[0] + parts[1]).astype(BF16)

    return _call(
        body, name=name, grid=(s // BLOCK,), in_specs=_attn_in_specs() + [HBM_SPEC] * len(after),
        out_specs=pl.BlockSpec((BLOCK, D), lambda n: (n, 0)), out_shape=_sds((s, D), BF16),
        compiler_params=_params("parallel"),
    )(proj, proj, proj, proj, proj, tab, tab, sinks, *after)


def _swa_bwd(do, proj, tab, sinks, dproj, name, after=()):
    s = proj.shape[0]
    nblk = s // BLOCK
    kv_of = lambda c: c // (GROUP // 2)

    def body(do_ref, q_ref, kc_ref, kp_ref, vc_ref, vp_ref, tc_ref, tp_ref, sink_ref, *rest):
        dproj_ref, dk_ref, dv_ref, ds_ref, dqout, dkbuf, dvbuf, sems = rest[1 + len(after):]
        n = pl.program_id(0)

        @pl.when(n == 0)
        def _():
            dk_ref[...] = jnp.zeros_like(dk_ref)
            dv_ref[...] = jnp.zeros_like(dv_ref)
            ds_ref[...] = jnp.zeros_like(ds_ref)

        q_halves, kdup, vdup = _swa_load(q_ref, kc_ref, kp_ref, vc_ref, vp_ref, tc_ref, tp_ref)
        dof = do_ref[...].astype(F32)
        do_halves = [_halves(dof[:, c * 128:(c + 1) * 128]) for c in range(N_CHUNK)]
        bias = _swa_bias(n)
        ones = jnp.ones((2 * BLOCK, 128), BF16)
        kk = [k.astype(BF16) for k in kdup]
        vv = [v.astype(BF16) for v in vdup]
        k_halves = [_halves(k) for k in kdup]
        heads = [(c, half) for c in range(N_CHUNK) for half in range(2)]
        lane_row = lax.broadcasted_iota(jnp.int32, (1, 128), 1)
        lo_kv = lax.broadcasted_iota(jnp.int32, (2 * BLOCK, 128), 1) < HALF
        scores = [lax.dot_general(q_halves[c][half], kk[kv_of(c)], NT, preferred_element_type=F32) for c, half in heads]
        dps = [lax.dot_general(do_halves[c][half], vv[kv_of(c)], NT, preferred_element_type=F32) for c, half in heads]
        exps = []
        for (c, half), sc in zip(heads, scores):
            sink = sink_ref[0, 2 * c + half]
            sc = sc + bias
            m = jnp.maximum(jnp.max(jnp.maximum(sc[:, :BLOCK], sc[:, BLOCK:]), axis=1, keepdims=True), sink)
            exps.append((jnp.exp(sc - m), jnp.exp(sink - m)))
        sums = [lax.dot_general(e.astype(BF16), ones, NN, preferred_element_type=F32) for e, _ in exps]
        dsink_row = jnp.zeros((1, 128), F32)
        dsb, pb = [], []
        for h, ((e, es), row_sum, dp) in enumerate(zip(exps, sums, dps)):
            inv = 1.0 / (row_sum + es)
            p = e * jnp.concatenate([inv, inv], axis=1)
            t = p * dp
            delta = jnp.sum(t, axis=1, keepdims=True)
            dsb.append((t - p * delta).astype(BF16))
            pb.append(p.astype(BF16))
            dsink = -jnp.sum(es * inv * delta, axis=0, keepdims=True)
            dsink_row = dsink_row + jnp.where(lane_row == h, dsink, 0.0)
        dq_parts = [lax.dot_general(d, k_halves[kv_of(c)][half], NN, preferred_element_type=F32) for (c, half), d in zip(heads, dsb)]
        dk_parts = [lax.dot_general(d, q_halves[c][half], TN, preferred_element_type=F32) for (c, half), d in zip(heads, dsb)]
        dv_parts = [lax.dot_general(p, do_halves[c][half], TN, preferred_element_type=F32) for (c, half), p in zip(heads, pb)]
        dq = jnp.concatenate([(dq_parts[2 * c] + dq_parts[2 * c + 1]) * ATTN_SCALE for c in range(N_CHUNK)], axis=1)

        def kv_sum(parts, hk):
            acc = (parts[GROUP * hk] + parts[GROUP * hk + 1]) + (parts[GROUP * hk + 2] + parts[GROUP * hk + 3])
            return acc + pltpu.roll(acc, HALF, 1)

        for pair in range(N_KV // 2):
            dkbuf[:, pair * 128:(pair + 1) * 128] = jnp.where(lo_kv, kv_sum(dk_parts, 2 * pair), kv_sum(dk_parts, 2 * pair + 1))
            dvbuf[:, pair * 128:(pair + 1) * 128] = jnp.where(lo_kv, kv_sum(dv_parts, 2 * pair), kv_sum(dv_parts, 2 * pair + 1))
        prev0 = pl.multiple_of(jnp.maximum(n - 1, 0) * BLOCK, BLOCK)
        cur0 = pl.multiple_of(n * BLOCK, BLOCK)

        @pl.when(n > 0)
        def _():
            dk_ref[pl.ds(prev0, BLOCK), :] += dkbuf[0:BLOCK, :]
            dv_ref[pl.ds(prev0, BLOCK), :] += dvbuf[0:BLOCK, :]

        dk_ref[pl.ds(cur0, BLOCK), :] += dkbuf[BLOCK:2 * BLOCK, :]
        dv_ref[pl.ds(cur0, BLOCK), :] += dvbuf[BLOCK:2 * BLOCK, :]
        ds_ref[...] += dsink_row

        def window(p, at):
            return dproj_ref.at[pl.ds(pl.multiple_of(at * BLOCK, BLOCK), BLOCK), pl.ds(C_Q, D)]

        _write_behind(n, nblk, dqout, sems, (_rope_t(dq, tc_ref[...]).astype(BF16),), window, n)

    blk = lambda w: pl.BlockSpec((BLOCK, w), lambda n: (n, 0))
    whole = lambda w: pl.BlockSpec((s, w), lambda n: (0, 0))
    n_in = 1 + len(_attn_in_specs())
    return _call(
        body, name=name, grid=(nblk,), in_specs=[blk(D)] + _attn_in_specs() + [HBM_SPEC] * (1 + len(after)),
        out_specs=[HBM_SPEC, whole(D_KV), whole(D_KV), pl.BlockSpec((1, 128), lambda n: (0, 0))],
        out_shape=[_sds((s, N_IN), BF16), _sds((s, D_KV), F32), _sds((s, D_KV), F32), _sds((1, 128), F32)],
        scratch_shapes=[pltpu.VMEM((2, 1, BLOCK, D), BF16), pltpu.VMEM((2 * BLOCK, D_KV), F32),
                        pltpu.VMEM((2 * BLOCK, D_KV), F32), pltpu.SemaphoreType.DMA((2, 1))],
        input_output_aliases={n_in: 0}, compiler_params=_params("arbitrary"),
    )(do, proj, proj, proj, proj, proj, tab, tab, sinks, dproj, *after)


def _kv_bwd(dkr, dv, tab, dproj, name):
    s = dkr.shape[0]
    tm = _row_tile(s)

    def body(dk_ref, dv_ref, t_ref, dproj_in, o_ref):
        del dproj_in
        o_ref[:, 0:D_KV] = _rope_t(dk_ref[...], t_ref[...]).astype(BF16)
        o_ref[:, D_KV:2 * D_KV] = dv_ref[...].astype(BF16)

    row = lambda w: pl.BlockSpec((tm, w), lambda i: (i, 0))
    return _call(
        body, name=name, grid=(s // tm,),
        in_specs=[row(D_KV), row(D_KV), row(384), pl.BlockSpec(memory_space=pl.ANY)],
        out_specs=pl.BlockSpec((tm, 2 * D_KV), lambda i: (i, C_K // (2 * D_KV))),
        out_shape=_sds((s, N_IN), BF16), input_output_aliases={3: 0}, compiler_params=_params("parallel"),
    )(dkr, dv, tab, dproj)


EW_TC = 512


def _sigmoid(x):
    return 0.5 * jnp.tanh(0.5 * x) + 0.5


def _merge_fwd(proj, conv_out, attn_out, name):
    s = proj.shape[0]
    tm = _row_tile(s)
    tile = pl.BlockSpec((tm, EW_TC), lambda i, j: (i, j))

    def body(gc_ref, ga_ref, co_ref, ao_ref, o_ref):
        o_ref[...] = (_sigmoid(gc_ref[...].astype(F32)) * co_ref[...].astype(F32)
                      + _sigmoid(ga_ref[...].astype(F32)) * ao_ref[...].astype(F32)).astype(BF16)

    return _call(
        body, name=name, grid=(s // tm, D // EW_TC),
        in_specs=[pl.BlockSpec((tm, EW_TC), lambda i, j: (i, C_GC // EW_TC + j)),
                  pl.BlockSpec((tm, EW_TC), lambda i, j: (i, C_GA // EW_TC + j)), tile, tile],
        out_specs=tile, out_shape=_sds((s, D), BF16), compiler_params=_params("parallel", "parallel"),
    )(proj, proj, conv_out, attn_out)


def _merge_bwd(dmerged, proj, conv_out, attn_out, name):
    s = proj.shape[0]
    tm = _row_tile(s)
    tile = pl.BlockSpec((tm, EW_TC), lambda i, j: (i, j))
    anyspec = pl.BlockSpec(memory_space=pl.ANY)

    def body(dm_ref, gc_ref, ga_ref, co_ref, ao_ref, dproj_ref, dco_ref, dao_ref, buf, sems):
        i, j = pl.program_id(0), pl.program_id(1)
        dm = dm_ref[...].astype(F32)
        sc = _sigmoid(gc_ref[...].astype(F32))
        sa = _sigmoid(ga_ref[...].astype(F32))
        dco_ref[...] = (dm * sc).astype(BF16)
        dao_ref[...] = (dm * sa).astype(BF16)
        tiles = ((dm * co_ref[...].astype(F32) * sc * (1.0 - sc)).astype(BF16),
                 (dm * ao_ref[...].astype(F32) * sa * (1.0 - sa)).astype(BF16))

        def window(p, at):
            start = pl.multiple_of((C_GC, C_GA)[p] + at[1] * EW_TC, EW_TC)
            return dproj_ref.at[pl.ds(pl.multiple_of(at[0] * tm, tm), tm), pl.ds(start, EW_TC)]

        _write_behind(i * nj + j, (s // tm) * nj, buf, sems, tiles, window, (i, j))

    nj = D // EW_TC
    return _call(
        body, name=name, grid=(s // tm, nj),
        in_specs=[tile, pl.BlockSpec((tm, EW_TC), lambda i, j: (i, C_GC // EW_TC + j)),
                  pl.BlockSpec((tm, EW_TC), lambda i, j: (i, C_GA // EW_TC + j)), tile, tile],
        out_specs=[anyspec, tile, tile],
        out_shape=[_sds((s, N_IN), BF16), _sds((s, D), BF16), _sds((s, D), BF16)],
        scratch_shapes=[pltpu.VMEM((2, 2, tm, EW_TC), BF16), pltpu.SemaphoreType.DMA((2, 2))],
        compiler_params=_params("arbitrary", "arbitrary"),
    )(dmerged, proj, proj, conv_out, attn_out)


FF_TC = 256
FF_TM = 2048


def _row_pipeline(tm, matmul, finish, split=ROW_SPLIT):
    step = tm // split
    pending = None
    for r in range(split):
        rows = pl.ds(r * step, step)
        result = matmul(rows)
        if pending is not None:
            finish(*pending)
        pending = (rows, result)
    finish(*pending)


def _gate_up_fwd(h2, wgu_t, name):
    s = h2.shape[0]
    tm = min(FF_TM, s)
    nb = D_FF // FF_TC

    def body(h_ref, wg_ref, wu_ref, a_ref, g_ref, u_ref):
        def matmuls(rows):
            h = h_ref[rows, :]
            return (lax.dot_general(h, wg_ref[...], NT, preferred_element_type=F32),
                    lax.dot_general(h, wu_ref[...], NT, preferred_element_type=F32))

        def finish(rows, gu):
            g, u = gu
            a_ref[rows, :] = (g * _sigmoid(g) * u).astype(BF16)
            g_ref[rows, :] = g.astype(BF16)
            u_ref[rows, :] = u.astype(BF16)

        _row_pipeline(tm, matmuls, finish)

    tile = pl.BlockSpec((tm, FF_TC), lambda j, i: (i, j))
    return _call(
        body, name=name, grid=(nb, s // tm),
        in_specs=[pl.BlockSpec((tm, D), lambda j, i: (i, 0)), pl.BlockSpec((FF_TC, D), lambda j, i: (j, 0)),
                  pl.BlockSpec((FF_TC, D), lambda j, i: (nb + j, 0))],
        out_specs=[tile, tile, tile], out_shape=[_sds((s, D_FF), BF16)] * 3,
        compiler_params=_params("parallel", "parallel"),
    )(h2, wgu_t, wgu_t)


def _down_bwd_x(dx2b, wd, gate, up, name):
    s = dx2b.shape[0]
    tm = min(FF_TM, s)
    nb = D_FF // FF_TC

    def body(dx_ref, w_ref, g_ref, u_ref, dg_ref, du_ref):
        def matmul(rows):
            return lax.dot_general(dx_ref[rows, :], w_ref[...], NT, preferred_element_type=F32)

        def finish(rows, da):
            g = g_ref[rows, :].astype(F32)
            sg = _sigmoid(g)
            dg_ref[rows, :] = (da * u_ref[rows, :].astype(F32) * (sg * (1.0 + g * (1.0 - sg)))).astype(BF16)
            du_ref[rows, :] = (da * (g * sg)).astype(BF16)

        _row_pipeline(tm, matmul, finish)

    tile = pl.BlockSpec((tm, FF_TC), lambda j, i: (i, j))
    return _call(
        body, name=name, grid=(nb, s // tm),
        in_specs=[pl.BlockSpec((tm, D), lambda j, i: (i, 0)), pl.BlockSpec((FF_TC, D), lambda j, i: (j, 0)), tile, tile],
        out_specs=[tile, tile], out_shape=[_sds((s, D_FF), BF16)] * 2,
        compiler_params=_params("parallel", "parallel"),
    )(dx2b, wd, gate, up)


class _Weights:
    def __init__(self, **groups):
        self.groups = groups

    def begin(self, group, after):
        return ()

    def end(self, group, after):
        return self.groups[group]


class _NoReduce:
    def start(self, group, grads):
        return ()

    def middle(self, group, after):
        return ()


def _local_step(x, tgt, g_mix, g_ffn, g_final, sinks, weights, reducer=None, after=()):
    reducer = reducer or _NoReduce()
    s = x.shape[0]
    tab = _rope_tables(s)
    big = dict(tm=2048, tn=512, tk=1024)
    h1 = _rms_fwd(x, g_mix, "rms1_fwd", after=after)
    win_t, conv_w = weights.end("in", weights.begin("in", (h1,)))
    proj = _matmul(h1, win_t, mode="nt", out_dtype=BF16, name="proj_fwd", tm=2048, tn=512, tk=1024)
    attn = _swa_fwd(proj, tab, sinks, "attn_fwd", after=weights.begin("mix", (proj,)))
    wco, wao, wo = weights.end("mix", (attn,))
    conv_y = _conv_fwd(proj, conv_w, "conv_fwd")
    conv_out = _matmul(conv_y, wco, mode="nn", out_dtype=BF16, name="conv_out_fwd", **big)
    attn_out = _matmul(attn, wao, mode="nn", out_dtype=BF16, name="attn_out_fwd", **big)
    merged = _merge_fwd(proj, conv_out, attn_out, "merge_fwd")
    x1 = _matmul(merged, wo, mode="nn", out_dtype=F32, name="wo_fwd", res=x, after=weights.begin("ffn", (merged,)), **big)
    h2 = _rms_fwd(x1, g_ffn, "rms2_fwd")
    wgu_t, wd = weights.end("ffn", (h2,))
    act, gate, up = _gate_up_fwd(h2, wgu_t, "gate_up_fwd")
    x2 = _matmul(act, wd, mode="nn", out_dtype=F32, name="down_fwd", res=x1, tm=1024, tn=512, tk=D_FF)
    dx2, dx2b, dg_final, lossvec = _loss_head(x2, g_final, tgt, "loss_head")
    dgate, dup = _down_bwd_x(dx2b, wd, gate, up, "down_bwd_x")
    g_wd = _matmul(act, dx2b, mode="tn", out_dtype=BF16, name="down_bwd_w", tm=1408, tn=1024, tk=2048)
    dh2 = _matmul([dgate, dup], wgu_t, mode="nn", out_dtype=BF16, name="gate_up_bwd_x", tm=1024, tn=1024, tk=1408)
    g_wgu_t = _matmul([dgate, dup], h2, mode="tn", out_dtype=BF16, name="gate_up_bwd_w", tm=1408, tn=1024, tk=2048)
    after_ffn = reducer.start("ffn", dict(wgu_t=g_wgu_t, wd=g_wd))
    dx1, dx1b, dg_ffn = _rms_bwd(dh2, x1, g_ffn, dx2, "rms2_bwd")
    dmerged = _matmul(dx1b, wo, mode="nt", out_dtype=BF16, name="wo_bwd_x", after=after_ffn, **big)
    after_ffn = reducer.middle("ffn", (dmerged,))
    g_wo = _matmul(merged, dx1b, mode="tn", out_dtype=BF16, name="wo_bwd_w", tm=512, tn=1024, tk=2048, after=after_ffn)
    dproj, dco, dao = _merge_bwd(dmerged, proj, conv_out, attn_out, "merge_bwd")
    dconv_y = _matmul(dco, wco, mode="nt", out_dtype=BF16, name="conv_out_bwd_x", **big)
    g_wco = _matmul(conv_y, dco, mode="tn", out_dtype=BF16, name="conv_out_bwd_w", tm=512, tn=1024, tk=2048)
    dattn = _matmul(dao, wao, mode="nt", out_dtype=BF16, name="attn_out_bwd_x", **big)
    g_wao = _matmul(attn, dao, mode="tn", out_dtype=BF16, name="attn_out_bwd_w", tm=512, tn=1024, tk=2048)
    after_mix = reducer.start("mix", dict(wco=g_wco, wao=g_wao, wo=g_wo))
    dproj, dconv_w = _conv_bwd(dconv_y, proj, conv_w, dproj, "conv_bwd", after=after_mix)
    after_mix = reducer.middle("mix", (dconv_w,))
    dproj, dkr, dv, dsinks = _swa_bwd(dattn, proj, tab, sinks, dproj, "attn_bwd", after=after_mix)
    dproj = _kv_bwd(dkr, dv, tab, dproj, "kv_bwd")
    g_win_t = _matmul(dproj, h1, mode="tn", out_dtype=BF16, name="proj_bwd_w", tm=512, tn=1024, tk=2048)
    after_in = reducer.middle("in", reducer.start("in", dict(win_t=g_win_t)))
    dh1 = _matmul(dproj, win_t, mode="nn", out_dtype=BF16, name="proj_bwd_x", tm=1024, tn=1024, tk=1664, after=after_in)
    dx, _, dg_mix = _rms_bwd(dh1, x, g_mix, dx1, "rms1_bwd")
    grads = dict(win_t=g_win_t, wgu_t=g_wgu_t, wd=g_wd, wco=g_wco, wao=g_wao, wo=g_wo)
    small = dict(g_mix=dg_mix, g_ffn=dg_ffn, g_final=dg_final, conv_w=dconv_w, sinks=dsinks, lossvec=lossvec)
    return dx, grads, small


def _position():
    return lax.axis_index("x"), lax.axis_index("y"), lax.axis_index("c")


def _other_chips(x, y):
    return [(1 - x, y), (x, 1 - y), (1 - x, 1 - y)]


SEM_SPEC = pl.BlockSpec(memory_space=pltpu.SEMAPHORE)
EFFECT = pltpu.SideEffectType.DATAFLOW_SIDE_EFFECTING
TOKEN = jax.ShapeDtypeStruct((8, 128), F32)
TOKEN_SPEC = pl.BlockSpec(memory_space=pltpu.VMEM)


def _hbm(a):
    return pltpu.with_memory_space_constraint(a, pltpu.HBM)


def _place(w, me_idx, dtype, name, after=()):
    r, cdim = w.shape

    def body(i_ref, w_ref, *rest):
        rest[-1][...] = w_ref[...].astype(dtype)

    grid_spec = pltpu.PrefetchScalarGridSpec(
        num_scalar_prefetch=1, grid=(1,), in_specs=[pl.BlockSpec((r, cdim), lambda i, me: (0, 0))] + [HBM_SPEC] * len(after),
        out_specs=pl.BlockSpec((r, cdim), lambda i, me: (me[0], 0)))
    return _call(body, name=name, grid_spec=grid_spec, out_shape=_sds((N_DEV * r, cdim), dtype),
                 compiler_params=_params("arbitrary"))(me_idx, w, *after)


def _own_rows(ref, r, px, py, pc):
    return ref.at[pl.ds((4 * px + 2 * py + pc) * r, r), :]


def _gather_phase(bufs, waits, plans, after, name):
    n = len(bufs)
    rows = [b.shape[0] // N_DEV for b in bufs]
    nw, npl = len(waits), len(plans)

    def body(*refs):
        ins = refs[:n]
        wait_sems = refs[n:n + 2 * nw]
        out0 = n + 2 * nw + len(after)
        new_sems = refs[out0:out0 + 2 * npl]
        token = refs[-1]
        x, y, c = _position()
        for w, (_, _, sent, received) in enumerate(waits):
            for a in range(n):
                for count, wait in ((sent, "wait_send"), (received, "wait_recv")):
                    span = _whole(ins[a], count * rows[a])
                    getattr(pltpu.make_async_remote_copy(
                        src_ref=span, dst_ref=span, send_sem=wait_sems[2 * w].at[a], recv_sem=wait_sems[2 * w + 1].at[a],
                        device_id=(x, y, c), device_id_type=MESH), wait)()
        for k, plan in enumerate(plans):
            for a in range(n):
                for block, target in plan(x, y, c):
                    span = _own_rows(ins[a], rows[a], *block)
                    pltpu.make_async_remote_copy(src_ref=span, dst_ref=span, send_sem=new_sems[2 * k].at[a],
                                                 recv_sem=new_sems[2 * k + 1].at[a], device_id=target, device_id_type=MESH).start()
        token[...] = jnp.zeros_like(token)

    sem_ops = [s for send, recv, _, _ in waits for s in (send, recv)]
    outs = _call(
        body, name=name, in_specs=[HBM_SPEC] * n + [SEM_SPEC] * (2 * nw) + [HBM_SPEC] * len(after),
        out_specs=[SEM_SPEC] * (2 * npl) + [HBM_SPEC] * n + [TOKEN_SPEC],
        out_shape=[pltpu.SemaphoreType.DMA((n,))] * (2 * npl) + [pltpu.HBM(b.shape, b.dtype) for b in bufs] + [TOKEN],
        input_output_aliases={i: 2 * npl + i for i in range(n)},
        compiler_params=pltpu.CompilerParams(has_side_effects=EFFECT),
    )(*[_hbm(b) for b in bufs], *sem_ops, *after)
    pairs = [(outs[2 * k], outs[2 * k + 1]) for k in range(npl)]
    return pairs, list(outs[2 * npl:2 * npl + n]), outs[-1]


def _own_to_near(x, y, c):
    return [((x, y, c), (x, y, 1 - c)), ((x, y, c), (1 - x, y, c)), ((x, y, c), (x, 1 - y, c))]


def _near_to_sibling(x, y, c):
    return [((1 - x, y, c), (x, y, 1 - c)), ((x, 1 - y, c), (x, y, 1 - c))]


def _relay_diagonal(x, y, c):
    north = c
    source = (x * north + (1 - x) * (1 - north), (1 - y) * north + y * (1 - north), c)
    target = ((1 - x) * north + x * (1 - north), y * north + (1 - y) * (1 - north), c)
    return [(source, target)]


def _diagonal_to_sibling(x, y, c):
    return [((1 - x, 1 - y, c), (x, y, 1 - c))]


def _gather_start(bufs, groups, name):
    n = len(bufs)
    rows = [b.shape[0] // N_DEV for b in bufs]
    ng = len(groups)

    def body(*refs):
        ins = refs[:n]
        sems = refs[n:n + 2 * ng]
        token = refs[-1]
        x, y, c = _position()
        targets = [(x, y, 1 - c)] + [(*chip, c) for chip in _other_chips(x, y)]
        for g, members in enumerate(groups):
            for slot, a in enumerate(members):
                own = _own_rows(ins[a], rows[a], x, y, c)
                for to in targets:
                    pltpu.make_async_remote_copy(src_ref=own, dst_ref=own, send_sem=sems[2 * g].at[slot],
                                                 recv_sem=sems[2 * g + 1].at[slot], device_id=to, device_id_type=MESH).start()
        token[...] = jnp.zeros_like(token)

    sem_shapes = []
    for members in groups:
        sem_shapes += [pltpu.SemaphoreType.DMA((len(members),))] * 2
    outs = _call(
        body, name=name, in_specs=[HBM_SPEC] * n, out_specs=[SEM_SPEC] * (2 * ng) + [HBM_SPEC] * n + [TOKEN_SPEC],
        out_shape=sem_shapes + [pltpu.HBM(b.shape, b.dtype) for b in bufs] + [TOKEN],
        input_output_aliases={i: 2 * ng + i for i in range(n)},
        compiler_params=pltpu.CompilerParams(has_side_effects=EFFECT),
    )(*[_hbm(b) for b in bufs])
    sem_pairs = [(outs[2 * g], outs[2 * g + 1]) for g in range(ng)]
    return sem_pairs, list(outs[2 * ng:2 * ng + n]), outs[-1]


def _gather_forward(send_sems, recv_sems, bufs, after, name):
    n = len(bufs)
    rows = [b.shape[0] // N_DEV for b in bufs]

    def body(*refs):
        ins = refs[:n]
        send1, recv1 = refs[n], refs[n + 1]
        out0 = n + 2 + len(after)
        send2, recv2 = refs[out0], refs[out0 + 1]
        token = refs[-1]
        x, y, c = _position()
        for a in range(n):
            step1 = pltpu.make_async_remote_copy(
                src_ref=_whole(ins[a], 4 * rows[a]), dst_ref=_whole(ins[a], 4 * rows[a]), send_sem=send1.at[a],
                recv_sem=recv1.at[a], device_id=(x, y, c), device_id_type=MESH)
            step1.wait_send()
            step1.wait_recv()
        for a in range(n):
            for chip in _other_chips(x, y):
                blk = _own_rows(ins[a], rows[a], *chip, c)
                pltpu.make_async_remote_copy(src_ref=blk, dst_ref=blk, send_sem=send2.at[a], recv_sem=recv2.at[a],
                                             device_id=(x, y, 1 - c), device_id_type=MESH).start()
        token[...] = jnp.zeros_like(token)

    outs = _call(
        body, name=name, in_specs=[HBM_SPEC] * n + [SEM_SPEC, SEM_SPEC] + [HBM_SPEC] * len(after),
        out_specs=[SEM_SPEC, SEM_SPEC] + [HBM_SPEC] * n + [TOKEN_SPEC],
        out_shape=[pltpu.SemaphoreType.DMA((n,)), pltpu.SemaphoreType.DMA((n,))]
        + [pltpu.HBM(b.shape, b.dtype) for b in bufs] + [TOKEN],
        input_output_aliases={i: 2 + i for i in range(n)},
        compiler_params=pltpu.CompilerParams(has_side_effects=EFFECT),
    )(*bufs, send_sems, recv_sems, *after)
    return outs[0], outs[1], list(outs[2:2 + n]), outs[-1]


def _gather_done(send_sems, recv_sems, bufs, after, name):
    n = len(bufs)
    rows = [b.shape[0] // N_DEV for b in bufs]

    def body(*refs):
        ins = refs[:n]
        send2, recv2 = refs[n], refs[n + 1]
        x, y, c = _position()
        for a in range(n):
            step2 = pltpu.make_async_remote_copy(
                src_ref=_whole(ins[a], 3 * rows[a]), dst_ref=_whole(ins[a], 3 * rows[a]), send_sem=send2.at[a],
                recv_sem=recv2.at[a], device_id=(x, y, c), device_id_type=MESH)
            step2.wait_send()
            step2.wait_recv()

    outs = _call(
        body, name=name, in_specs=[HBM_SPEC] * n + [SEM_SPEC, SEM_SPEC] + [HBM_SPEC] * len(after),
        out_specs=[HBM_SPEC] * n, out_shape=[pltpu.HBM(b.shape, b.dtype) for b in bufs],
        input_output_aliases={i: i for i in range(n)},
        compiler_params=pltpu.CompilerParams(has_side_effects=EFFECT),
    )(*bufs, send_sems, recv_sems, *after)
    return list(outs)


def _whole(ref, nrows):
    return ref.at[pl.ds(0, nrows), :]


def _to_sibling(x, y, c):
    return [(2 * q + (1 - c), q, (x, y, 1 - c)) for q in range(4)]


def _to_chips(x, y, c):
    return [(2 * px + py, j, (px, py, c)) for j, (px, py) in enumerate(_other_chips(x, y))]


def _exchange_start(srcs, src_slots, plan, name):
    n = len(srcs)
    rows = [a.shape[0] // src_slots for a in srcs]
    n_copies = len(plan(0, 0, 0))
    lands = [lax.empty((n_copies * r, a.shape[1]), a.dtype) for a, r in zip(srcs, rows)]

    def body(*refs):
        ins, land_refs = refs[:n], refs[n:2 * n]
        send_sems, recv_sems = refs[2 * n], refs[2 * n + 1]
        token = refs[-1]
        for a in range(n):
            r = rows[a]
            for src_slot, dst_slot, target in plan(*_position()):
                pltpu.make_async_remote_copy(
                    src_ref=ins[a].at[pl.ds(src_slot * r, r), :], dst_ref=land_refs[a].at[pl.ds(dst_slot * r, r), :],
                    send_sem=send_sems.at[a], recv_sem=recv_sems.at[a], device_id=target, device_id_type=MESH).start()
        token[...] = jnp.zeros_like(token)

    outs = _call(
        body, name=name, in_specs=[HBM_SPEC] * (2 * n),
        out_specs=[SEM_SPEC, SEM_SPEC] + [HBM_SPEC] * (2 * n) + [TOKEN_SPEC],
        out_shape=[pltpu.SemaphoreType.DMA((n,)), pltpu.SemaphoreType.DMA((n,))]
        + [pltpu.HBM(a.shape, a.dtype) for a in srcs] + [pltpu.HBM(l.shape, l.dtype) for l in lands] + [TOKEN],
        input_output_aliases={i: 2 + i for i in range(2 * n)},
        compiler_params=pltpu.CompilerParams(has_side_effects=EFFECT),
    )(*[_hbm(a) for a in srcs], *[_hbm(l) for l in lands])
    return outs[0], outs[1], list(outs[2:2 + n]), list(outs[2 + n:2 + 2 * n]), outs[-1]


def _exchange_wait(send_sems, recv_sems, srcs, lands, after, name):
    n = len(srcs)

    def body(*refs):
        ins, land_refs = refs[:n], refs[n:2 * n]
        send_sems_ref, recv_sems_ref = refs[2 * n], refs[2 * n + 1]
        for a in range(n):
            span = _whole(land_refs[a], lands[a].shape[0])
            cp = pltpu.make_async_remote_copy(
                src_ref=span, dst_ref=span, send_sem=send_sems_ref.at[a],
                recv_sem=recv_sems_ref.at[a], device_id=_position(), device_id_type=MESH)
            cp.wait_send()
            cp.wait_recv()

    outs = _call(
        body, name=name, in_specs=[HBM_SPEC] * (2 * n) + [SEM_SPEC, SEM_SPEC] + [HBM_SPEC] * len(after),
        out_specs=[HBM_SPEC] * (2 * n),
        out_shape=[pltpu.HBM(a.shape, a.dtype) for a in srcs] + [pltpu.HBM(l.shape, l.dtype) for l in lands],
        input_output_aliases={i: i for i in range(2 * n)},
        compiler_params=pltpu.CompilerParams(has_side_effects=EFFECT),
    )(*srcs, *lands, send_sems, recv_sems, *after)
    return list(outs[:n]), list(outs[n:])


def _chip_partial(grad, recv, idx, name):
    r = recv.shape[0] // 4

    def body(i_ref, g_ref, s_ref, o_ref):
        del i_ref
        o_ref[...] = (g_ref[...].astype(F32) + s_ref[...].astype(F32)).astype(BF16)

    nb = 1
    tr = r // nb
    grid_spec = pltpu.PrefetchScalarGridSpec(
        num_scalar_prefetch=1, grid=(3, nb),
        in_specs=[pl.BlockSpec((tr, D), lambda t, i, i_ref: ((2 * i_ref[1 + t] + i_ref[0]) * nb + i, 0)),
                  pl.BlockSpec((tr, D), lambda t, i, i_ref: (i_ref[1 + t] * nb + i, 0))],
        out_specs=pl.BlockSpec((tr, D), lambda t, i, i_ref: (i_ref[1 + t] * nb + i, 0)))
    return _call(body, name=name, grid_spec=grid_spec, out_shape=_sds((4 * r, D), BF16),
                 compiler_params=_params("arbitrary", "arbitrary"))(idx, grad, recv)


def _adamw_math(w, g, m, v):
    m2 = B1 * m + (1.0 - B1) * g
    v2 = B2 * v + (1.0 - B2) * jnp.square(g)
    m_hat = m2 / (1.0 - B1 ** STEP)
    v_hat = v2 / (1.0 - B2 ** STEP)
    return -LR * (m_hat / (jnp.sqrt(v_hat) + EPS_ADAM) + WD * w), m2, v2


def _reduce_adamw(w, grad, from_sibling, from_chips, idx, m, v, name):
    r = w.shape[0]
    assert grad.shape == (N_DEV * r, D) and from_sibling.shape == (4 * r, D) and from_chips.shape == (3 * r, D)
    tr = r // 2
    nb = r // tr

    def body(i_ref, w_ref, p_ref, s_ref, r0_ref, r1_ref, r2_ref, m_ref, v_ref, g_ref, d_ref, nm_ref, nv_ref):
        del i_ref
        g = p_ref[...].astype(F32) + s_ref[...].astype(F32)
        g = ((g + r0_ref[...].astype(F32)) + r1_ref[...].astype(F32)) + r2_ref[...].astype(F32)
        g_ref[...] = g
        d_ref[...], nm_ref[...], nv_ref[...] = _adamw_math(w_ref[...], g, m_ref[...], v_ref[...])

    own = pl.BlockSpec((tr, D), lambda i, i_ref: (i, 0))
    grid_spec = pltpu.PrefetchScalarGridSpec(
        num_scalar_prefetch=1, grid=(nb,),
        in_specs=[own, pl.BlockSpec((tr, D), lambda i, i_ref: (i_ref[0] * nb + i, 0)),
                  pl.BlockSpec((tr, D), lambda i, i_ref: (i_ref[1] * nb + i, 0))]
        + [pl.BlockSpec((tr, D), lambda i, i_ref, j=j: (j * nb + i, 0)) for j in range(3)] + [own, own],
        out_specs=[own] * 4)
    return _call(body, name=name, grid_spec=grid_spec, out_shape=[_sds((r, D), F32)] * 4,
                 compiler_params=_params("parallel"))(idx, w, grad, from_sibling, from_chips, from_chips, from_chips, m, v)


SMALL_ROWS = 8


def _to_everyone(x, y, c):
    return [(0, k - 1, tuple(1 - v if (k >> b) & 1 else v for v, b in ((x, 2), (y, 1), (c, 0)))) for k in range(1, N_DEV)]


def _small_sum(pack, others, order, name):
    def body(order_ref, p_ref, o_ref, tot_ref, loss_ref):
        tot = None
        for d in range(N_DEV):
            k = order_ref[d]
            theirs = o_ref[pl.ds(pl.multiple_of((jnp.maximum(k, 1) - 1) * SMALL_ROWS, SMALL_ROWS), SMALL_ROWS), :]
            term = jnp.where(k == 0, p_ref[...], theirs)
            tot = term if tot is None else tot + term
        tot_ref[...] = tot
        loss_ref[...] = jnp.full((1, 128), (0.5 / D) * jnp.sum(tot[SMALL_ROWS - 1:SMALL_ROWS, :]), F32)

    full = lambda shape: pl.BlockSpec(shape, lambda i, order_ref: (0, 0))
    grid_spec = pltpu.PrefetchScalarGridSpec(
        num_scalar_prefetch=1, grid=(1,), in_specs=[full(pack.shape), full(others.shape)],
        out_specs=[full((SMALL_ROWS, D)), full((1, 128))])
    return _call(body, name=name, grid_spec=grid_spec, out_shape=[_sds((SMALL_ROWS, D), F32), _sds((1, 128), F32)],
                 compiler_params=_params("arbitrary"))(order, pack, others)


def _adamw_small(ws, gs, ms, vs, name):
    n = len(ws)

    def body(*refs):
        for a in range(n):
            w_ref, g_ref, m_ref, v_ref = (refs[k * n + a] for k in range(4))
            d_ref, nm_ref, nv_ref = (refs[(4 + k) * n + a] for k in range(3))
            d_ref[...], nm_ref[...], nv_ref[...] = _adamw_math(w_ref[...], g_ref[...], m_ref[...], v_ref[...])

    vm = pl.BlockSpec(memory_space=pltpu.VMEM)
    outs = _call(body, name=name, in_specs=[vm] * (4 * n), out_specs=[vm] * (3 * n),
                 out_shape=[_sds(w.shape, F32) for w in ws] * 3)(*ws, *gs, *ms, *vs)
    return [(outs[a], outs[n + a], outs[2 * n + a]) for a in range(n)]


def kernel(x, g_mix, w_in, conv_w, attn_sinks, w_conv_out, w_attn_out, w_o, g_ffn, w_gate_up, w_down, g_final, loss_target, m_g_mix, m_w_in, m_conv_w, m_attn_sinks, m_w_conv_out, m_w_attn_out, m_w_o, m_g_ffn, m_w_gate_up, m_w_down, m_g_final, v_g_mix, v_w_in, v_conv_w, v_attn_sinks, v_w_conv_out, v_w_attn_out, v_w_o, v_g_ffn, v_w_gate_up, v_w_down, v_g_final):
    cx, cy, cc = _position()
    chip = 2 * cx + cy
    partial_idx = jnp.stack([cc, 2 * (1 - cx) + cy, 2 * cx + (1 - cy), 2 * (1 - cx) + (1 - cy)]).astype(jnp.int32)
    own_idx = jnp.stack([2 * chip + cc, chip]).astype(jnp.int32)
    me = 4 * cx + 2 * cy + cc

    me_idx = jnp.reshape(me, (1,)).astype(jnp.int32)
    first = [_place(jnp.transpose(w_in[0]), me_idx, BF16, "place_w_in"),
             _place(jnp.pad(conv_w[0], ((0, 5), (0, 0))), me_idx, F32, "place_conv_w")]
    (to_near,), first, token_in = _gather_phase(first, [], [_own_to_near], (), "gather_in_start")
    gather_tokens = (token_in,)

    class Gathered:
        def __init__(self):
            self.state = {}

        def begin(self, group, after):
            if group == "in":
                (near, relay), bufs, token = _gather_phase(
                    first, [(*to_near, 3, 3)], [_near_to_sibling, _relay_diagonal], after, "gather_in_relay")
                later = [_place(w, me_idx, BF16, "place_" + k, after=(token,)) for k, w in (
                    ("w_conv_out", w_conv_out[0]), ("w_attn_out", w_attn_out[0]), ("w_o", w_o[0]),
                    ("w_gate_up", jnp.transpose(w_gate_up[0])), ("w_down", w_down[0]))]
                (sems_mix, sems_ffn), later, token_later = _gather_start(later, [[0, 1, 2], [3, 4]], "gather_start_later")
                self.state.update({"in": (near, relay, bufs), "mix": (sems_mix, later[:3]), "ffn": (sems_ffn, later[3:])})
                return (token_later,)
            (send_sems, recv_sems), group_bufs = self.state[group]
            send2, recv2, group_bufs, token = _gather_forward(send_sems, recv_sems, group_bufs, after, "gather_forward_" + group)
            self.state[group] = ((send2, recv2), group_bufs)
            return (token,)

        def end(self, group, after):
            if group == "in":
                near, relay, bufs = self.state[group]
                (last,), bufs, token = _gather_phase(bufs, [(*relay, 1, 1)], [_diagonal_to_sibling], after, "gather_in_last")
                _, full, _ = _gather_phase(bufs, [(*near, 2, 2), (*last, 1, 1)], [], (token,), "gather_in_done")
                return full[0], jnp.transpose(full[1].reshape(N_DEV, 8, 128)[:, :3, :], (1, 0, 2)).reshape(3, D)
            (send2, recv2), group_bufs = self.state[group]
            return _gather_done(send2, recv2, group_bufs, after, "gather_done_" + group)

    in_flight, own_pieces = {}, {}

    transposed = ("w_in", "w_gate_up")

    def as2d(k, a):
        if k in transposed:
            return jnp.transpose(a[0])
        return a[None] if a.ndim == 1 else (a[0] if a.ndim == 3 else a)

    w_all = {"g_mix": g_mix, "w_in": w_in, "conv_w": conv_w, "attn_sinks": attn_sinks, "w_conv_out": w_conv_out,
             "w_attn_out": w_attn_out, "w_o": w_o, "g_ffn": g_ffn, "w_gate_up": w_gate_up, "w_down": w_down, "g_final": g_final}
    m_all = {"g_mix": m_g_mix, "w_in": m_w_in, "conv_w": m_conv_w, "attn_sinks": m_attn_sinks, "w_conv_out": m_w_conv_out,
             "w_attn_out": m_w_attn_out, "w_o": m_w_o, "g_ffn": m_g_ffn, "w_gate_up": m_w_gate_up, "w_down": m_w_down,
             "g_final": m_g_final}
    v_all = {"g_mix": v_g_mix, "w_in": v_w_in, "conv_w": v_conv_w, "attn_sinks": v_attn_sinks, "w_conv_out": v_w_conv_out,
             "w_attn_out": v_w_attn_out, "w_o": v_w_o, "g_ffn": v_g_ffn, "w_gate_up": v_w_gate_up, "w_down": v_w_down,
             "g_final": v_g_final}
    results = {}

    def record(k, *vals):
        results[k] = [(jnp.transpose(val) if k in transposed else val).reshape(w_all[k].shape) for val in vals]

    def update(k, pieces):
        g, d, nm, nv = _reduce_adamw(as2d(k, w_all[k]), *pieces, own_idx, as2d(k, m_all[k]), as2d(k, v_all[k]), "adamw_" + k)
        record(k, g, d, nm, nv)
        return nm

    def update_small(grads):
        keys = list(grads)
        outs = _adamw_small([as2d(k, w_all[k]) for k in keys], [grads[k] for k in keys], [as2d(k, m_all[k]) for k in keys],
                            [as2d(k, v_all[k]) for k in keys], "adamw_small")
        for k, (d, nm, nv) in zip(keys, outs):
            record(k, grads[k], d, nm, nv)
        return tuple(nm for _, nm, _ in outs)

    kernel_name = {"win_t": "w_in", "wgu_t": "w_gate_up", "wd": "w_down", "wco": "w_conv_out", "wao": "w_attn_out", "wo": "w_o"}

    def finish(group, after):
        keys, send_sems, recv_sems, parts, from_chips = in_flight[group]
        _, from_chips = _exchange_wait(send_sems, recv_sems, parts, from_chips, after, "rs_chips_wait_" + group)
        grads, from_sibling = own_pieces[group]
        return tuple(update(kernel_name[k], p) for k, *p in zip(keys, grads, from_sibling, from_chips))

    class Reducer:
        def start(self, group, gdict):
            keys, glist = list(gdict), list(gdict.values())
            send_sems, recv_sems, glist, lands, token = _exchange_start(glist, N_DEV, _to_sibling, "rs_sibling_start_" + group)
            in_flight[group] = (keys, send_sems, recv_sems, glist, lands)
            return (token,)

        def middle(self, group, after):
            keys, send_sems, recv_sems, glist, lands = in_flight[group]
            if group == "in":
                after = finish("ffn", after)
            glist, lands = _exchange_wait(send_sems, recv_sems, glist, lands, after, "rs_sibling_wait_" + group)
            parts = [_chip_partial(g, r, partial_idx, "chip_partial_" + k) for k, g, r in zip(keys, glist, lands)]
            send_sems, recv_sems, parts, from_chips, token = _exchange_start(parts, 4, _to_chips, "rs_chips_start_" + group)
            in_flight[group] = (keys, send_sems, recv_sems, parts, from_chips)
            own_pieces[group] = (glist, lands)
            return (token,)

    dx, _, small = _local_step(x[0], loss_target[0], g_mix, g_ffn, g_final[None], attn_sinks, Gathered(),
                               reducer=Reducer(), after=gather_tokens)
    sinks_row = jnp.pad(small["sinks"], ((0, 0), (0, D - 128)))
    pack = jnp.concatenate([small["g_mix"], small["g_ffn"], small["g_final"], small["conv_w"], sinks_row, small["lossvec"]], axis=0)
    send_sems, recv_sems, (pack,), (others,), token = _exchange_start([pack], 1, _to_everyone, "small_start")
    after = finish("mix", (dx, token))
    (pack,), (others,) = _exchange_wait(send_sems, recv_sems, [pack], [others], after, "small_wait")
    relation = jnp.stack([jnp.bitwise_xor(me, d) for d in range(N_DEV)]).astype(jnp.int32)
    tot, loss_row = _small_sum(pack, others, relation, "small_sum")
    loss = loss_row[0, 0]
    g_small = {
        "g_mix": tot[0:1], "g_ffn": tot[1:2], "g_final": tot[2:3],
        "conv_w": lax.dynamic_slice(tot, (3, me * 128), (3, 128)), "attn_sinks": tot[6:7, :N_HEADS],
    }
    finish("in", update_small(g_small))

    order = ["g_mix", "w_in", "conv_w", "attn_sinks", "w_conv_out", "w_attn_out", "w_o", "g_ffn", "w_gate_up", "w_down", "g_final"]
    return (loss, dx[None], *[results[k][i] for i in range(4) for k in order])
```

```python
import functools
import math

import jax
import jax.numpy as jnp
from jax import lax
from jax.experimental import pallas as pl
from jax.experimental.pallas import tpu as pltpu

F32 = jnp.float32
BF16 = jnp.bfloat16

D = 1024
HEAD_DIM = 64
N_HEADS = 16
N_KV = 4
GROUP = N_HEADS // N_KV
D_KV = N_KV * HEAD_DIM
BLOCK = 128
ROT_DIM = HEAD_DIM // 4
ROPE_THETA = 500000.0
ATTN_SCALE = 1.0 / math.sqrt(HEAD_DIM)
NEG_INF = -1e30
D_FF = 2816
N_IN = 6656
EPS = 1e-5
C_CB, C_CC, C_CX, C_Q, C_K, C_V, C_GC, C_GA = 0, 1024, 2048, 3072, 4096, 4352, 4608, 5632

LR, B1, B2, EPS_ADAM, WD, STEP = 0.001, 0.9, 0.999, 1e-08, 0.01, 10

N_DEV = 8
MESH = pl.DeviceIdType.MESH
VMEM_LIMIT = 56 * 1024 * 1024

NN = (((1,), (0,)), ((), ()))
NT = (((1,), (1,)), ((), ()))
TN = (((0,), (0,)), ((), ()))
HBM_SPEC = pl.BlockSpec(memory_space=pl.ANY)
ROW_SPLIT = 4


def _call(body, **kw):
    return pl.pallas_call(body, **kw)


def _params(*sem):
    return pltpu.CompilerParams(dimension_semantics=sem, vmem_limit_bytes=VMEM_LIMIT)


def _sds(shape, dtype):
    return jax.ShapeDtypeStruct(shape, dtype)


def _matmul(a, b, *, mode, tm, tn, tk, out_dtype, name, res=None, after=()):
    parts = list(a) if isinstance(a, (list, tuple)) else [a]
    rows_a = parts[0].shape[0]
    cols_a = sum(p.shape[1] for p in parts)
    if mode == "nn":
        (m, kk), (_, n), dims = (rows_a, cols_a), b.shape, NN
    elif mode == "nt":
        (m, kk), (n, _), dims = (rows_a, cols_a), b.shape, NT
    else:
        (kk, m), (_, n), dims = (rows_a, cols_a), b.shape, TN
    tm, tn, tk = min(tm, m), min(tn, n), min(tk, kk)
    assert m % tm == 0 and n % tn == 0 and kk % tk == 0, (name, m, n, kk, tm, tn, tk)
    nk = kk // tk
    split_axis, width = (2, tk) if mode == "nn" else (0, tm)
    assert len(parts) == 1 or mode in ("nn", "tn")
    assert len(parts) == 1 or all(p.shape[1] % width == 0 for p in parts), (name, width)
    counts = [p.shape[1] // width for p in parts]
    starts = [sum(counts[:p]) for p in range(len(parts))]

    def a_spec(p):
        def col(t):
            return jnp.clip(t - starts[p], 0, counts[p] - 1) if len(parts) > 1 else t

        if mode == "tn":
            return pl.BlockSpec((tk, tm), lambda i, j, k: (k, col(i)))
        return pl.BlockSpec((tm, tk), lambda i, j, k: (i, col(k)))

    if mode == "nt":
        b_spec = pl.BlockSpec((tn, tk), lambda i, j, k: (j, k))
    else:
        b_spec = pl.BlockSpec((tk, tn), lambda i, j, k: (k, j))
    o_spec = pl.BlockSpec((tm, tn), lambda i, j, k: (i, j))
    has_res = res is not None
    n_parts = len(parts)
    unit = 128 if mode == "tn" else 16
    split = ROW_SPLIT if tm % (ROW_SPLIT * unit) == 0 else 1

    def body(*refs):
        a_refs, b_ref = refs[:n_parts], refs[n_parts]
        r_ref = refs[n_parts + 1] if has_res else None
        o_ref = refs[n_parts + 1 + has_res + len(after)]
        k = pl.program_id(2)

        acc_ref = refs[-1] if nk > 1 else None

        def step(a_ref):
            def matmul(rows):
                a_blk = a_ref[:, rows] if mode == "tn" else a_ref[rows, :]
                return lax.dot_general(a_blk, b_ref[...], dims, preferred_element_type=F32)

            def finish(rows, part):
                if nk > 1:
                    acc_ref[rows, :] += part
                else:
                    o_ref[rows, :] = (part + r_ref[rows, :] if has_res else part).astype(o_ref.dtype)

            _row_pipeline(tm, matmul, finish, split)

        if nk > 1:
            @pl.when(k == 0)
            def _():
                acc_ref[...] = jnp.zeros_like(acc_ref)

        if n_parts == 1:
            step(a_refs[0])
        else:
            t = pl.program_id(split_axis)
            for p in range(n_parts):
                pl.when((t >= starts[p]) & (t < starts[p] + counts[p]))(functools.partial(step, a_refs[p]))

        if nk > 1:
            @pl.when(k == nk - 1)
            def _():
                o_ref[...] = (acc_ref[...] + r_ref[...] if has_res else acc_ref[...]).astype(o_ref.dtype)

    ins = parts + [b] + ([res] if has_res else []) + list(after)
    in_specs = [a_spec(p) for p in range(n_parts)] + [b_spec] + ([o_spec] if has_res else []) + [HBM_SPEC] * len(after)
    scratch = [] if nk == 1 else [pltpu.VMEM((tm, tn), F32)]
    return _call(
        body, name=name, grid=(m // tm, n // tn, nk), in_specs=in_specs, out_specs=o_spec,
        out_shape=_sds((m, n), out_dtype), scratch_shapes=scratch,
        compiler_params=_params("parallel", "parallel", "arbitrary"),
    )(*ins)


def _matmul_group(a_group, b_group, *, mode, tm, tn, out_dtype, name):
    a0, b0 = a_group[0], b_group[0]
    a_pair, b_pair, count = a_group, b_group, len(a_group)
    if mode == "nn":
        (m, kk), (_, n), dims = a0.shape, b0.shape, NN
    elif mode == "nt":
        (m, kk), (n, _), dims = a0.shape, b0.shape, NT
    else:
        (kk, m), (_, n), dims = a0.shape, b0.shape, TN
    assert all(a.shape == a0.shape for a in a_pair) and all(b.shape == b0.shape for b in b_pair)
    tm, tn = min(tm, m), min(tn, n)
    assert m % tm == 0 and n % tn == 0, (name, m, n, tm, tn)
    a_spec = pl.BlockSpec((kk, tm), lambda i, j: (0, i)) if mode == "tn" else pl.BlockSpec((tm, kk), lambda i, j: (i, 0))
    b_spec = pl.BlockSpec((tn, kk), lambda i, j: (j, 0)) if mode == "nt" else pl.BlockSpec((kk, tn), lambda i, j: (0, j))
    o_spec = pl.BlockSpec((tm, tn), lambda i, j: (i, j))
    unit = 128 if mode == "tn" else 16
    split = ROW_SPLIT if tm % (ROW_SPLIT * unit) == 0 else 1

    def body(*refs):
        a_refs, b_refs, o_refs = refs[:count], refs[count:2 * count], refs[2 * count:]

        def matmul(rows):
            return tuple(lax.dot_general(a_ref[:, rows] if mode == "tn" else a_ref[rows, :], b_ref[...], dims,
                                         preferred_element_type=F32) for a_ref, b_ref in zip(a_refs, b_refs))

        def finish(rows, parts):
            for o_ref, part in zip(o_refs, parts):
                o_ref[rows, :] = part.astype(out_dtype)

        _row_pipeline(tm, matmul, finish, split)

    return _call(
        body, name=name, grid=(m // tm, n // tn), in_specs=[a_spec] * count + [b_spec] * count, out_specs=[o_spec] * count,
        out_shape=[_sds((m, n), out_dtype)] * count, compiler_params=_params("parallel", "parallel"),
    )(*a_pair, *b_pair)


def _row_tile(s):
    return min(512, s)


def _rms_fwd(x, g, name, after=()):
    s = x.shape[0]
    tm = _row_tile(s)

    def body(x_ref, g_ref, *rest):
        h_ref = rest[-1]
        xv = x_ref[...]
        r = lax.rsqrt(jnp.mean(xv * xv, axis=-1, keepdims=True) + EPS)
        h_ref[...] = (xv * r * g_ref[...]).astype(BF16)

    row = pl.BlockSpec((tm, D), lambda i: (i, 0))
    return _call(
        body, name=name, grid=(s // tm,), in_specs=[row, pl.BlockSpec((1, D), lambda i: (0, 0))] + [HBM_SPEC] * len(after),
        out_specs=row, out_shape=_sds((s, D), BF16), compiler_params=_params("parallel"),
    )(x, g, *after)


def _rms_bwd(dh, x, g, dres, name, after=()):
    s = x.shape[0]
    tm = _row_tile(s)

    def body(dh_ref, x_ref, g_ref, dres_ref, *rest):
        dx_ref, dxb_ref, dg_ref = rest[len(after):]
        xv = x_ref[...]
        r = lax.rsqrt(jnp.mean(xv * xv, axis=-1, keepdims=True) + EPS)
        xh = xv * r
        dhv = dh_ref[...].astype(F32)
        dyg = dhv * g_ref[...]
        dx = dres_ref[...] + r * (dyg - xh * jnp.mean(dyg * xh, axis=-1, keepdims=True))
        dx_ref[...] = dx
        dxb_ref[...] = dx.astype(BF16)
        part = jnp.sum(dhv * xh, axis=0, keepdims=True)

        @pl.when(pl.program_id(0) == 0)
        def _():
            dg_ref[...] = part

        @pl.when(pl.program_id(0) > 0)
        def _():
            dg_ref[...] += part

    row = pl.BlockSpec((tm, D), lambda i: (i, 0))
    vec = pl.BlockSpec((1, D), lambda i: (0, 0))
    return _call(
        body, name=name, grid=(s // tm,), in_specs=[row, row, vec, row] + [HBM_SPEC] * len(after), out_specs=[row, row, vec],
        out_shape=[_sds((s, D), F32), _sds((s, D), BF16), _sds((1, D), F32)],
        compiler_params=_params("arbitrary"),
    )(dh, x, g, dres, *after)


def _loss_head(x2, g, tgt, name):
    s = x2.shape[0]
    tm = _row_tile(s)

    def body(x_ref, g_ref, t_ref, dx_ref, dxb_ref, dg_ref, l_ref):
        xv = x_ref[...]
        gv = g_ref[...]
        r = lax.rsqrt(jnp.mean(xv * xv, axis=-1, keepdims=True) + EPS)
        xh = xv * r
        err = xh * gv - t_ref[...]
        dy = err * (1.0 / D)
        dyg = dy * gv
        dx = r * (dyg - xh * jnp.mean(dyg * xh, axis=-1, keepdims=True))
        dx_ref[...] = dx
        dxb_ref[...] = dx.astype(BF16)
        dg_part = jnp.sum(dy * xh, axis=0, keepdims=True)
        l_part = jnp.sum(err * err, axis=0, keepdims=True)

        @pl.when(pl.program_id(0) == 0)
        def _():
            dg_ref[...] = dg_part
            l_ref[...] = l_part

        @pl.when(pl.program_id(0) > 0)
        def _():
            dg_ref[...] += dg_part
            l_ref[...] += l_part

    row = pl.BlockSpec((tm, D), lambda i: (i, 0))
    vec = pl.BlockSpec((1, D), lambda i: (0, 0))
    return _call(
        body, name=name, grid=(s // tm,), in_specs=[row, vec, row], out_specs=[row, row, vec, vec],
        out_shape=[_sds((s, D), F32), _sds((s, D), BF16), _sds((1, D), F32), _sds((1, D), F32)],
        compiler_params=_params("arbitrary"),
    )(x2, g, tgt)


CONV_TC = 256


def _shift_down(u, k, rows):
    return jnp.where(rows >= k, pltpu.roll(u, k, 0), 0.0)


def _shift_up(u, k, rows, s):
    return jnp.where(rows < s - k, pltpu.roll(u, s - k, 0), 0.0)


def _conv_specs(s):
    nb = D // CONV_TC

    def col(c0):
        return pl.BlockSpec((s, CONV_TC), lambda j, c0=c0: (0, c0 // CONV_TC + j))

    return nb, col


def _conv_fwd(proj, conv_w, name):
    s = proj.shape[0]
    nb, col = _conv_specs(s)

    def body(cb_ref, cc_ref, cx_ref, w_ref, y_ref):
        rows = lax.broadcasted_iota(jnp.int32, (s, CONV_TC), 0)
        u = cc_ref[...].astype(F32) * cx_ref[...].astype(F32)
        w = w_ref[...]
        c = w[0:1] * _shift_down(u, 2, rows) + w[1:2] * _shift_down(u, 1, rows) + w[2:3] * u
        y_ref[...] = (cb_ref[...].astype(F32) * c).astype(BF16)

    return _call(
        body, name=name, grid=(nb,),
        in_specs=[col(C_CB), col(C_CC), col(C_CX), pl.BlockSpec((3, CONV_TC), lambda j: (0, j))],
        out_specs=pl.BlockSpec((s, CONV_TC), lambda j: (0, j)), out_shape=_sds((s, D), BF16),
        compiler_params=_params("parallel"),
    )(proj, proj, proj, conv_w)


def _write_behind(t, nt, buf, sems, tiles, window, where):
    slot = t % 2

    def copies(sl, at):
        return [pltpu.make_async_copy(buf.at[sl, p], window(p, at), sems.at[sl, p]) for p in range(len(tiles))]

    @pl.when(t >= 2)
    def _():
        for cp in copies(slot, where):
            cp.wait()

    for p, tile in enumerate(tiles):
        buf[slot, p] = tile
    started = copies(slot, where)
    for cp in started:
        cp.start()

    @pl.when(t == nt - 1)
    def _():
        for cp in started:
            cp.wait()
        if nt > 1:
            for cp in copies(1 - slot, where):
                cp.wait()


def _conv_bwd(dy, proj, conv_w, dproj, name, after=()):
    s = proj.shape[0]
    nb, col = _conv_specs(s)

    def body(dy_ref, cb_ref, cc_ref, cx_ref, w_ref, *rest):
        dproj_ref, dw_ref, buf, sems = rest[1 + len(after):]
        j = pl.program_id(0)
        rows = lax.broadcasted_iota(jnp.int32, (s, CONV_TC), 0)
        cc = cc_ref[...].astype(F32)
        cx = cx_ref[...].astype(F32)
        u = cc * cx
        u1 = _shift_down(u, 1, rows)
        u2 = _shift_down(u, 2, rows)
        w = w_ref[...]
        c = w[0:1] * u2 + w[1:2] * u1 + w[2:3] * u
        dyv = dy_ref[...].astype(F32)
        dc = dyv * cb_ref[...].astype(F32)
        du = w[2:3] * dc + w[1:2] * _shift_up(dc, 1, rows, s) + w[0:1] * _shift_up(dc, 2, rows, s)

        def window(p, jj):
            start = pl.multiple_of((C_CB, C_CC, C_CX)[p] + jj * CONV_TC, CONV_TC)
            return dproj_ref.at[:, pl.ds(start, CONV_TC)]

        tiles = ((dyv * c).astype(BF16), (du * cx).astype(BF16), (du * cc).astype(BF16))
        _write_behind(j * 0, 1, buf, sems, tiles, window, j)
        dw_ref[...] = jnp.concatenate(
            [jnp.sum(dc * u2, axis=0, keepdims=True), jnp.sum(dc * u1, axis=0, keepdims=True),
             jnp.sum(dc * u, axis=0, keepdims=True)], axis=0)

    return _call(
        body, name=name, grid=(nb,),
        in_specs=[pl.BlockSpec((s, CONV_TC), lambda j: (0, j)), col(C_CB), col(C_CC), col(C_CX),
                  pl.BlockSpec((3, CONV_TC), lambda j: (0, j))] + [HBM_SPEC] * (1 + len(after)),
        out_specs=[pl.BlockSpec(memory_space=pl.ANY), pl.BlockSpec((3, CONV_TC), lambda j: (0, j))],
        out_shape=[_sds((s, N_IN), BF16), _sds((3, D), F32)],
        scratch_shapes=[pltpu.VMEM((1, 3, s, CONV_TC), BF16), pltpu.SemaphoreType.DMA((1, 3))],
        input_output_aliases={5: 0}, compiler_params=_params("arbitrary"),
    )(dy, proj, proj, proj, conv_w, dproj, *after)


def _rope_tables(s):
    half = ROT_DIM // 2
    inv_freq = ROPE_THETA ** (-jnp.arange(0, ROT_DIM, 2, dtype=F32) / ROT_DIM)
    inv64 = jnp.concatenate([inv_freq, inv_freq, jnp.zeros((HEAD_DIM - ROT_DIM,), F32)])
    ang = jnp.arange(s, dtype=F32)[:, None] * jnp.concatenate([inv64, inv64])[None, :]
    d = lax.broadcasted_iota(jnp.int32, (s, 128), 1) % HEAD_DIM
    cos, sin = jnp.cos(ang), jnp.sin(ang)
    c = jnp.where(d < ROT_DIM, cos, 1.0)
    a = jnp.where(d < half, -sin, 0.0)
    b = jnp.where((d >= half) & (d < ROT_DIM), sin, 0.0)
    return jnp.concatenate([c, a, b], axis=1)


def _rope(x, tab):
    c, a, b = tab[:, 0:128], tab[:, 128:256], tab[:, 256:384]
    outs = []
    for i in range(x.shape[1] // 128):
        xc = x[:, i * 128:(i + 1) * 128]
        outs.append(xc * c + pltpu.roll(xc, 120, 1) * a + pltpu.roll(xc, 8, 1) * b)
    return outs[0] if len(outs) == 1 else jnp.concatenate(outs, axis=1)


def _rope_t(dx, tab):
    c, a, b = tab[:, 0:128], tab[:, 128:256], tab[:, 256:384]
    outs = []
    for i in range(dx.shape[1] // 128):
        dc = dx[:, i * 128:(i + 1) * 128]
        outs.append(dc * c + pltpu.roll(dc * a, 8, 1) + pltpu.roll(dc * b, 120, 1))
    return outs[0] if len(outs) == 1 else jnp.concatenate(outs, axis=1)


def _attn_in_specs():
    prev = lambda n: jnp.maximum(n - 1, 0)
    return [
        pl.BlockSpec((BLOCK, D), lambda n: (n, C_Q // D)),
        pl.BlockSpec((BLOCK, D_KV), lambda n: (n, C_K // D_KV)),
        pl.BlockSpec((BLOCK, D_KV), lambda n: (prev(n), C_K // D_KV)),
        pl.BlockSpec((BLOCK, D_KV), lambda n: (n, C_V // D_KV)),
        pl.BlockSpec((BLOCK, D_KV), lambda n: (prev(n), C_V // D_KV)),
        pl.BlockSpec((BLOCK, 384), lambda n: (n, 0)),
        pl.BlockSpec((BLOCK, 384), lambda n: (prev(n), 0)),
        pl.BlockSpec(memory_space=pltpu.SMEM),
    ]


HALF = HEAD_DIM
N_CHUNK = D // 128


def _swa_bias(n):
    qi = lax.broadcasted_iota(jnp.int32, (BLOCK, 2 * BLOCK), 0)
    kj = lax.broadcasted_iota(jnp.int32, (BLOCK, 2 * BLOCK), 1)
    rel = qi + BLOCK - kj
    valid = (rel >= 0) & (rel < BLOCK) & ((kj >= BLOCK) | (n > 0))
    return jnp.where(valid, 0.0, NEG_INF)


def _halves(x):
    lo = lax.broadcasted_iota(jnp.int32, x.shape, 1) < HALF
    return jnp.where(lo, x, 0.0).astype(BF16), jnp.where(lo, 0.0, x).astype(BF16)


def _dup_heads(x):
    out = []
    for pair in range(N_KV // 2):
        xc = x[:, pair * 128:(pair + 1) * 128]
        xr = pltpu.roll(xc, HALF, 1)
        lo = lax.broadcasted_iota(jnp.int32, xc.shape, 1) < HALF
        out += [jnp.where(lo, xc, xr), jnp.where(lo, xr, xc)]
    return out


def _swa_load(q_ref, kc_ref, kp_ref, vc_ref, vp_ref, tc_ref, tp_ref):
    qf = _rope(q_ref[...].astype(F32), tc_ref[...]) * ATTN_SCALE
    q_halves = [_halves(qf[:, c * 128:(c + 1) * 128]) for c in range(N_CHUNK)]
    kf = jnp.concatenate([_rope(kp_ref[...].astype(F32), tp_ref[...]), _rope(kc_ref[...].astype(F32), tc_ref[...])], axis=0)
    vf = jnp.concatenate([vp_ref[...], vc_ref[...]], axis=0).astype(F32)
    return q_halves, _dup_heads(kf), _dup_heads(vf)


def _swa_probs(qh, kk, bias, sink):
    s = lax.dot_general(qh, kk, NT, preferred_element_type=F32) + bias
    m = jnp.maximum(jnp.max(jnp.maximum(s[:, :BLOCK], s[:, BLOCK:]), axis=1, keepdims=True), sink)
    return jnp.exp(s - m), m


def _swa_fwd(proj, tab, sinks, name, after=()):
    s = proj.shape[0]

    def body(q_ref, kc_ref, kp_ref, vc_ref, vp_ref, tc_ref, tp_ref, sink_ref, *rest):
        o_ref = rest[-1]
        n = pl.program_id(0)
        q_halves, kdup, vdup = _swa_load(q_ref, kc_ref, kp_ref, vc_ref, vp_ref, tc_ref, tp_ref)
        bias = _swa_bias(n)
        ones = jnp.ones((2 * BLOCK, 128), BF16)
        kk = [k.astype(BF16) for k in kdup]
        vv = [[jnp.concatenate([v_half, ones], axis=1) for v_half in _halves(v)] for v in vdup]
        heads = [(c, half) for c in range(N_CHUNK) for half in range(2)]
        scores = [lax.dot_general(q_halves[c][half], kk[c // (GROUP // 2)], NT, preferred_element_type=F32)
                  for c, half in heads]
        probs = []
        for (c, half), sc in zip(heads, scores):
            sc = sc + bias
            m = jnp.maximum(jnp.max(jnp.maximum(sc[:, :BLOCK], sc[:, BLOCK:]), axis=1, keepdims=True), sink_ref[0, 2 * c + half])
            probs.append((jnp.exp(sc - m).astype(BF16), jnp.exp(sink_ref[0, 2 * c + half] - m)))
        outs = [lax.dot_general(e, vv[c // (GROUP // 2)][half], NN, preferred_element_type=F32)
                for (c, half), (e, _) in zip(heads, probs)]
        for c in range(N_CHUNK):
            parts = [outs[2 * c + half][:, :128] * (1.0 / (outs[2 * c + half][:, 128:] + probs[2 * c + half][1]))
                     for half in range(2)]
            o_ref[:, c * 128:(c + 1) * 128] = (parts[0] + parts[1]).astype(BF16)

    return _call(
        body, name=name, grid=(s // BLOCK,), in_specs=_attn_in_specs() + [HBM_SPEC] * len(after),
        out_specs=pl.BlockSpec((BLOCK, D), lambda n: (n, 0)), out_shape=_sds((s, D), BF16),
        compiler_params=_params("parallel"),
    )(proj, proj, proj, proj, proj, tab, tab, sinks, *after)


def _swa_bwd(do, proj, tab, sinks, dproj, name, after=()):
    s = proj.shape[0]
    nblk = s // BLOCK
    kv_of = lambda c: c // (GROUP // 2)

    def body(do_ref, q_ref, kc_ref, kp_ref, vc_ref, vp_ref, tc_ref, tp_ref, sink_ref, *rest):
        dproj_ref, dk_ref, dv_ref, ds_ref, dqout, dkbuf, dvbuf, sems = rest[1 + len(after):]
        n = pl.program_id(0)

        @pl.when(n == 0)
        def _():
            dk_ref[...] = jnp.zeros_like(dk_ref)
            dv_ref[...] = jnp.zeros_like(dv_ref)
            ds_ref[...] = jnp.zeros_like(ds_ref)

        q_halves, kdup, vdup = _swa_load(q_ref, kc_ref, kp_ref, vc_ref, vp_ref, tc_ref, tp_ref)
        dof = do_ref[...].astype(F32)
        do_halves = [_halves(dof[:, c * 128:(c + 1) * 128]) for c in range(N_CHUNK)]
        bias = _swa_bias(n)
        ones = jnp.ones((2 * BLOCK, 128), BF16)
        kk = [k.astype(BF16) for k in kdup]
        vv = [v.astype(BF16) for v in vdup]
        k_halves = [_halves(k) for k in kdup]
        heads = [(c, half) for c in range(N_CHUNK) for half in range(2)]
        lane_row = lax.broadcasted_iota(jnp.int32, (1, 128), 1)
        lo_kv = lax.broadcasted_iota(jnp.int32, (2 * BLOCK, 128), 1) < HALF
        scores = [lax.dot_general(q_halves[c][half], kk[kv_of(c)], NT, preferred_element_type=F32) for c, half in heads]
        dps = [lax.dot_general(do_halves[c][half], vv[kv_of(c)], NT, preferred_element_type=F32) for c, half in heads]
        exps = []
        for (c, half), sc in zip(heads, scores):
            sink = sink_ref[0, 2 * c + half]
            sc = sc + bias
            m = jnp.maximum(jnp.max(jnp.maximum(sc[:, :BLOCK], sc[:, BLOCK:]), axis=1, keepdims=True), sink)
            exps.append((jnp.exp(sc - m), jnp.exp(sink - m)))
        sums = [lax.dot_general(e.astype(BF16), ones, NN, preferred_element_type=F32) for e, _ in exps]
        dsink_row = jnp.zeros((1, 128), F32)
        dsb, pb = [], []
        for h, ((e, es), row_sum, dp) in enumerate(zip(exps, sums, dps)):
            inv = 1.0 / (row_sum + es)
            p = e * jnp.concatenate([inv, inv], axis=1)
            t = p * dp
            delta = jnp.sum(t, axis=1, keepdims=True)
            dsb.append((t - p * delta).astype(BF16))
            pb.append(p.astype(BF16))
            dsink = -jnp.sum(es * inv * delta, axis=0, keepdims=True)
            dsink_row = dsink_row + jnp.where(lane_row == h, dsink, 0.0)
        dq_parts = [lax.dot_general(d, k_halves[kv_of(c)][half], NN, preferred_element_type=F32) for (c, half), d in zip(heads, dsb)]
        dk_parts = [lax.dot_general(d, q_halves[c][half], TN, preferred_element_type=F32) for (c, half), d in zip(heads, dsb)]
        dv_parts = [lax.dot_general(p, do_halves[c][half], TN, preferred_element_type=F32) for (c, half), p in zip(heads, pb)]
        dq = jnp.concatenate([(dq_parts[2 * c] + dq_parts[2 * c + 1]) * ATTN_SCALE for c in range(N_CHUNK)], axis=1)

        def kv_sum(parts, hk):
            acc = (parts[GROUP * hk] + parts[GROUP * hk + 1]) + (parts[GROUP * hk + 2] + parts[GROUP * hk + 3])
            return acc + pltpu.roll(acc, HALF, 1)

        for pair in range(N_KV // 2):
            dkbuf[:, pair * 128:(pair + 1) * 128] = jnp.where(lo_kv, kv_sum(dk_parts, 2 * pair), kv_sum(dk_parts, 2 * pair + 1))
            dvbuf[:, pair * 128:(pair + 1) * 128] = jnp.where(lo_kv, kv_sum(dv_parts, 2 * pair), kv_sum(dv_parts, 2 * pair + 1))
        prev0 = pl.multiple_of(jnp.maximum(n - 1, 0) * BLOCK, BLOCK)
        cur0 = pl.multiple_of(n * BLOCK, BLOCK)

        @pl.when(n > 0)
        def _():
            dk_ref[pl.ds(prev0, BLOCK), :] += dkbuf[0:BLOCK, :]
            dv_ref[pl.ds(prev0, BLOCK), :] += dvbuf[0:BLOCK, :]

        dk_ref[pl.ds(cur0, BLOCK), :] += dkbuf[BLOCK:2 * BLOCK, :]
        dv_ref[pl.ds(cur0, BLOCK), :] += dvbuf[BLOCK:2 * BLOCK, :]
        ds_ref[...] += dsink_row

        def window(p, at):
            return dproj_ref.at[pl.ds(pl.multiple_of(at * BLOCK, BLOCK), BLOCK), pl.ds(C_Q, D)]

        _write_behind(n, nblk, dqout, sems, (_rope_t(dq, tc_ref[...]).astype(BF16),), window, n)

    blk = lambda w: pl.BlockSpec((BLOCK, w), lambda n: (n, 0))
    whole = lambda w: pl.BlockSpec((s, w), lambda n: (0, 0))
    n_in = 1 + len(_attn_in_specs())
    return _call(
        body, name=name, grid=(nblk,), in_specs=[blk(D)] + _attn_in_specs() + [HBM_SPEC] * (1 + len(after)),
        out_specs=[HBM_SPEC, whole(D_KV), whole(D_KV), pl.BlockSpec((1, 128), lambda n: (0, 0))],
        out_shape=[_sds((s, N_IN), BF16), _sds((s, D_KV), F32), _sds((s, D_KV), F32), _sds((1, 128), F32)],
        scratch_shapes=[pltpu.VMEM((2, 1, BLOCK, D), BF16), pltpu.VMEM((2 * BLOCK, D_KV), F32),
                        pltpu.VMEM((2 * BLOCK, D_KV), F32), pltpu.SemaphoreType.DMA((2, 1))],
        input_output_aliases={n_in: 0}, compiler_params=_params("arbitrary"),
    )(do, proj, proj, proj, proj, proj, tab, tab, sinks, dproj, *after)


def _kv_bwd(dkr, dv, tab, dproj, name):
    s = dkr.shape[0]
    tm = _row_tile(s)

    def body(dk_ref, dv_ref, t_ref, dproj_in, o_ref):
        del dproj_in
        o_ref[:, 0:D_KV] = _rope_t(dk_ref[...], t_ref[...]).astype(BF16)
        o_ref[:, D_KV:2 * D_KV] = dv_ref[...].astype(BF16)

    row = lambda w: pl.BlockSpec((tm, w), lambda i: (i, 0))
    return _call(
        body, name=name, grid=(s // tm,),
        in_specs=[row(D_KV), row(D_KV), row(384), pl.BlockSpec(memory_space=pl.ANY)],
        out_specs=pl.BlockSpec((tm, 2 * D_KV), lambda i: (i, C_K // (2 * D_KV))),
        out_shape=_sds((s, N_IN), BF16), input_output_aliases={3: 0}, compiler_params=_params("parallel"),
    )(dkr, dv, tab, dproj)


EW_TC = 512


def _sigmoid(x):
    return 0.5 * jnp.tanh(0.5 * x) + 0.5


def _merge_fwd(proj, conv_out, attn_out, name):
    s = proj.shape[0]
    tm = _row_tile(s)
    tile = pl.BlockSpec((tm, EW_TC), lambda i, j: (i, j))

    def body(gc_ref, ga_ref, co_ref, ao_ref, o_ref):
        o_ref[...] = (_sigmoid(gc_ref[...].astype(F32)) * co_ref[...].astype(F32)
                      + _sigmoid(ga_ref[...].astype(F32)) * ao_ref[...].astype(F32)).astype(BF16)

    return _call(
        body, name=name, grid=(s // tm, D // EW_TC),
        in_specs=[pl.BlockSpec((tm, EW_TC), lambda i, j: (i, C_GC // EW_TC + j)),
                  pl.BlockSpec((tm, EW_TC), lambda i, j: (i, C_GA // EW_TC + j)), tile, tile],
        out_specs=tile, out_shape=_sds((s, D), BF16), compiler_params=_params("parallel", "parallel"),
    )(proj, proj, conv_out, attn_out)


def _merge_bwd(dmerged, proj, conv_out, attn_out, name):
    s = proj.shape[0]
    tm = _row_tile(s)
    tile = pl.BlockSpec((tm, EW_TC), lambda i, j: (i, j))
    anyspec = pl.BlockSpec(memory_space=pl.ANY)

    def body(dm_ref, gc_ref, ga_ref, co_ref, ao_ref, dproj_ref, dco_ref, dao_ref, buf, sems):
        i, j = pl.program_id(0), pl.program_id(1)
        dm = dm_ref[...].astype(F32)
        sc = _sigmoid(gc_ref[...].astype(F32))
        sa = _sigmoid(ga_ref[...].astype(F32))
        dco_ref[...] = (dm * sc).astype(BF16)
        dao_ref[...] = (dm * sa).astype(BF16)
        tiles = ((dm * co_ref[...].astype(F32) * sc * (1.0 - sc)).astype(BF16),
                 (dm * ao_ref[...].astype(F32) * sa * (1.0 - sa)).astype(BF16))

        def window(p, at):
            start = pl.multiple_of((C_GC, C_GA)[p] + at[1] * EW_TC, EW_TC)
            return dproj_ref.at[pl.ds(pl.multiple_of(at[0] * tm, tm), tm), pl.ds(start, EW_TC)]

        _write_behind(i * nj + j, (s // tm) * nj, buf, sems, tiles, window, (i, j))

    nj = D // EW_TC
    return _call(
        body, name=name, grid=(s // tm, nj),
        in_specs=[tile, pl.BlockSpec((tm, EW_TC), lambda i, j: (i, C_GC // EW_TC + j)),
                  pl.BlockSpec((tm, EW_TC), lambda i, j: (i, C_GA // EW_TC + j)), tile, tile],
        out_specs=[anyspec, tile, tile],
        out_shape=[_sds((s, N_IN), BF16), _sds((s, D), BF16), _sds((s, D), BF16)],
        scratch_shapes=[pltpu.VMEM((2, 2, tm, EW_TC), BF16), pltpu.SemaphoreType.DMA((2, 2))],
        compiler_params=_params("arbitrary", "arbitrary"),
    )(dmerged, proj, proj, conv_out, attn_out)


FF_TC = 256
FF_TM = 2048


def _row_pipeline(tm, matmul, finish, split=ROW_SPLIT):
    step = tm // split
    pending = None
    for r in range(split):
        rows = pl.ds(r * step, step)
        result = matmul(rows)
        if pending is not None:
            finish(*pending)
        pending = (rows, result)
    finish(*pending)


def _gate_up_fwd(h2, wgu_t, name):
    s = h2.shape[0]
    tm = min(FF_TM, s)
    nb = D_FF // FF_TC

    def body(h_ref, wg_ref, wu_ref, a_ref, g_ref, u_ref):
        def matmuls(rows):
            h = h_ref[rows, :]
            return (lax.dot_general(h, wg_ref[...], NT, preferred_element_type=F32),
                    lax.dot_general(h, wu_ref[...], NT, preferred_element_type=F32))

        def finish(rows, gu):
            g, u = gu
            a_ref[rows, :] = (g * _sigmoid(g) * u).astype(BF16)
            g_ref[rows, :] = g.astype(BF16)
            u_ref[rows, :] = u.astype(BF16)

        _row_pipeline(tm, matmuls, finish)

    tile = pl.BlockSpec((tm, FF_TC), lambda j, i: (i, j))
    return _call(
        body, name=name, grid=(nb, s // tm),
        in_specs=[pl.BlockSpec((tm, D), lambda j, i: (i, 0)), pl.BlockSpec((FF_TC, D), lambda j, i: (j, 0)),
                  pl.BlockSpec((FF_TC, D), lambda j, i: (nb + j, 0))],
        out_specs=[tile, tile, tile], out_shape=[_sds((s, D_FF), BF16)] * 3,
        compiler_params=_params("parallel", "parallel"),
    )(h2, wgu_t, wgu_t)


def _down_bwd_x(dx2b, wd, gate, up, name):
    s = dx2b.shape[0]
    tm = min(FF_TM, s)
    nb = D_FF // FF_TC

    def body(dx_ref, w_ref, g_ref, u_ref, dg_ref, du_ref):
        def matmul(rows):
            return lax.dot_general(dx_ref[rows, :], w_ref[...], NT, preferred_element_type=F32)

        def finish(rows, da):
            g = g_ref[rows, :].astype(F32)
            sg = _sigmoid(g)
            dg_ref[rows, :] = (da * u_ref[rows, :].astype(F32) * (sg * (1.0 + g * (1.0 - sg)))).astype(BF16)
            du_ref[rows, :] = (da * (g * sg)).astype(BF16)

        _row_pipeline(tm, matmul, finish)

    tile = pl.BlockSpec((tm, FF_TC), lambda j, i: (i, j))
    return _call(
        body, name=name, grid=(nb, s // tm),
        in_specs=[pl.BlockSpec((tm, D), lambda j, i: (i, 0)), pl.BlockSpec((FF_TC, D), lambda j, i: (j, 0)), tile, tile],
        out_specs=[tile, tile], out_shape=[_sds((s, D_FF), BF16)] * 2,
        compiler_params=_params("parallel", "parallel"),
    )(dx2b, wd, gate, up)


class _Weights:
    def __init__(self, **groups):
        self.groups = groups

    def begin(self, group, after):
        return ()

    def end(self, group, after):
        return self.groups[group]


class _NoReduce:
    def start(self, group, grads):
        return ()

    def middle(self, group, after):
        return ()


def _local_step(x, tgt, g_mix, g_ffn, g_final, sinks, weights, reducer=None, after=()):
    reducer = reducer or _NoReduce()
    s = x.shape[0]
    tab = _rope_tables(s)
    big = dict(tm=2048, tn=512, tk=1024)
    h1 = _rms_fwd(x, g_mix, "rms1_fwd", after=after)
    win_t, conv_w = weights.end("in", weights.begin("in", (h1,)))
    proj = _matmul(h1, win_t, mode="nt", out_dtype=BF16, name="proj_fwd", tm=2048, tn=512, tk=1024)
    attn = _swa_fwd(proj, tab, sinks, "attn_fwd", after=weights.begin("mix", (proj,)))
    wco, wao, wo = weights.end("mix", (attn,))
    conv_y = _conv_fwd(proj, conv_w, "conv_fwd")
    conv_out, attn_out = _matmul_group((conv_y, attn), (wco, wao), mode="nn", tm=2048, tn=512, out_dtype=BF16, name="branch_out_fwd")
    merged = _merge_fwd(proj, conv_out, attn_out, "merge_fwd")
    x1 = _matmul(merged, wo, mode="nn", out_dtype=F32, name="wo_fwd", res=x, after=weights.begin("ffn", (merged,)), **big)
    h2 = _rms_fwd(x1, g_ffn, "rms2_fwd")
    wgu_t, wd = weights.end("ffn", (h2,))
    act, gate, up = _gate_up_fwd(h2, wgu_t, "gate_up_fwd")
    x2 = _matmul(act, wd, mode="nn", out_dtype=F32, name="down_fwd", res=x1, tm=1024, tn=512, tk=D_FF)
    dx2, dx2b, dg_final, lossvec = _loss_head(x2, g_final, tgt, "loss_head")
    dgate, dup = _down_bwd_x(dx2b, wd, gate, up, "down_bwd_x")
    g_wd = _matmul(act, dx2b, mode="tn", out_dtype=BF16, name="down_bwd_w", tm=1408, tn=1024, tk=2048)
    dh2 = _matmul([dgate, dup], wgu_t, mode="nn", out_dtype=BF16, name="gate_up_bwd_x", tm=1024, tn=1024, tk=1408)
    g_wgu_t = _matmul([dgate, dup], h2, mode="tn", out_dtype=BF16, name="gate_up_bwd_w", tm=1408, tn=1024, tk=2048)
    after_ffn = reducer.start("ffn", dict(wgu_t=g_wgu_t, wd=g_wd))
    dx1, dx1b, dg_ffn = _rms_bwd(dh2, x1, g_ffn, dx2, "rms2_bwd")
    dmerged = _matmul(dx1b, wo, mode="nt", out_dtype=BF16, name="wo_bwd_x", after=after_ffn, **big)
    after_ffn = reducer.middle("ffn", (dmerged,))
    g_wo = _matmul(merged, dx1b, mode="tn", out_dtype=BF16, name="wo_bwd_w", tm=512, tn=1024, tk=2048, after=after_ffn)
    dproj, dco, dao = _merge_bwd(dmerged, proj, conv_out, attn_out, "merge_bwd")
    dconv_y, dattn = _matmul_group((dco, dao), (wco, wao), mode="nt", tm=2048, tn=512, out_dtype=BF16, name="branch_out_bwd_x")
    g_wco, g_wao = _matmul_group((conv_y, attn), (dco, dao), mode="tn", tm=512, tn=1024, out_dtype=BF16, name="branch_out_bwd_w")
    after_mix = reducer.start("mix", dict(wco=g_wco, wao=g_wao, wo=g_wo))
    dproj, dconv_w = _conv_bwd(dconv_y, proj, conv_w, dproj, "conv_bwd", after=after_mix)
    after_mix = reducer.middle("mix", (dconv_w,))
    dproj, dkr, dv, dsinks = _swa_bwd(dattn, proj, tab, sinks, dproj, "attn_bwd", after=after_mix)
    dproj = _kv_bwd(dkr, dv, tab, dproj, "kv_bwd")
    g_win_t = _matmul(dproj, h1, mode="tn", out_dtype=BF16, name="proj_bwd_w", tm=512, tn=1024, tk=2048)
    after_in = reducer.middle("in", reducer.start("in", dict(win_t=g_win_t)))
    dh1 = _matmul(dproj, win_t, mode="nn", out_dtype=BF16, name="proj_bwd_x", tm=1024, tn=1024, tk=1664, after=after_in)
    dx, _, dg_mix = _rms_bwd(dh1, x, g_mix, dx1, "rms1_bwd")
    grads = dict(win_t=g_win_t, wgu_t=g_wgu_t, wd=g_wd, wco=g_wco, wao=g_wao, wo=g_wo)
    small = dict(g_mix=dg_mix, g_ffn=dg_ffn, g_final=dg_final, conv_w=dconv_w, sinks=dsinks, lossvec=lossvec)
    return dx, grads, small


def _position():
    return lax.axis_index("x"), lax.axis_index("y"), lax.axis_index("c")


def _other_chips(x, y):
    return [(1 - x, y), (x, 1 - y), (1 - x, 1 - y)]


SEM_SPEC = pl.BlockSpec(memory_space=pltpu.SEMAPHORE)
EFFECT = pltpu.SideEffectType.DATAFLOW_SIDE_EFFECTING
TOKEN = jax.ShapeDtypeStruct((8, 128), F32)
TOKEN_SPEC = pl.BlockSpec(memory_space=pltpu.VMEM)


def _hbm(a):
    return pltpu.with_memory_space_constraint(a, pltpu.HBM)


def _place(w, me_idx, dtype, name, after=()):
    r, cdim = w.shape

    def body(i_ref, w_ref, *rest):
        rest[-1][...] = w_ref[...].astype(dtype)

    grid_spec = pltpu.PrefetchScalarGridSpec(
        num_scalar_prefetch=1, grid=(1,), in_specs=[pl.BlockSpec((r, cdim), lambda i, me: (0, 0))] + [HBM_SPEC] * len(after),
        out_specs=pl.BlockSpec((r, cdim), lambda i, me: (me[0], 0)))
    return _call(body, name=name, grid_spec=grid_spec, out_shape=_sds((N_DEV * r, cdim), dtype),
                 compiler_params=_params("arbitrary"))(me_idx, w, *after)


def _own_rows(ref, r, px, py, pc):
    return ref.at[pl.ds((4 * px + 2 * py + pc) * r, r), :]


def _gather_phase(bufs, waits, plans, after, name):
    n = len(bufs)
    rows = [b.shape[0] // N_DEV for b in bufs]
    nw, npl = len(waits), len(plans)

    def body(*refs):
        ins = refs[:n]
        wait_sems = refs[n:n + 2 * nw]
        out0 = n + 2 * nw + len(after)
        new_sems = refs[out0:out0 + 2 * npl]
        token = refs[-1]
        x, y, c = _position()
        for w, (_, _, sent, received) in enumerate(waits):
            for a in range(n):
                for count, wait in ((sent, "wait_send"), (received, "wait_recv")):
                    span = _whole(ins[a], count * rows[a])
                    getattr(pltpu.make_async_remote_copy(
                        src_ref=span, dst_ref=span, send_sem=wait_sems[2 * w].at[a], recv_sem=wait_sems[2 * w + 1].at[a],
                        device_id=(x, y, c), device_id_type=MESH), wait)()
        for k, plan in enumerate(plans):
            for a in range(n):
                for block, target in plan(x, y, c):
                    span = _own_rows(ins[a], rows[a], *block)
                    pltpu.make_async_remote_copy(src_ref=span, dst_ref=span, send_sem=new_sems[2 * k].at[a],
                                                 recv_sem=new_sems[2 * k + 1].at[a], device_id=target, device_id_type=MESH).start()
        token[...] = jnp.zeros_like(token)

    sem_ops = [s for send, recv, _, _ in waits for s in (send, recv)]
    outs = _call(
        body, name=name, in_specs=[HBM_SPEC] * n + [SEM_SPEC] * (2 * nw) + [HBM_SPEC] * len(after),
        out_specs=[SEM_SPEC] * (2 * npl) + [HBM_SPEC] * n + [TOKEN_SPEC],
        out_shape=[pltpu.SemaphoreType.DMA((n,))] * (2 * npl) + [pltpu.HBM(b.shape, b.dtype) for b in bufs] + [TOKEN],
        input_output_aliases={i: 2 * npl + i for i in range(n)},
        compiler_params=pltpu.CompilerParams(has_side_effects=EFFECT),
    )(*[_hbm(b) for b in bufs], *sem_ops, *after)
    pairs = [(outs[2 * k], outs[2 * k + 1]) for k in range(npl)]
    return pairs, list(outs[2 * npl:2 * npl + n]), outs[-1]


def _own_to_near(x, y, c):
    return [((x, y, c), (x, y, 1 - c)), ((x, y, c), (1 - x, y, c)), ((x, y, c), (x, 1 - y, c))]


def _near_to_sibling(x, y, c):
    return [((1 - x, y, c), (x, y, 1 - c)), ((x, 1 - y, c), (x, y, 1 - c))]


def _relay_diagonal(x, y, c):
    north = c
    source = (x * north + (1 - x) * (1 - north), (1 - y) * north + y * (1 - north), c)
    target = ((1 - x) * north + x * (1 - north), y * north + (1 - y) * (1 - north), c)
    return [(source, target)]


def _diagonal_to_sibling(x, y, c):
    return [((1 - x, 1 - y, c), (x, y, 1 - c))]


def _gather_start(bufs, groups, name):
    n = len(bufs)
    rows = [b.shape[0] // N_DEV for b in bufs]
    ng = len(groups)

    def body(*refs):
        ins = refs[:n]
        sems = refs[n:n + 2 * ng]
        token = refs[-1]
        x, y, c = _position()
        targets = [(x, y, 1 - c)] + [(*chip, c) for chip in _other_chips(x, y)]
        for g, members in enumerate(groups):
            for slot, a in enumerate(members):
                own = _own_rows(ins[a], rows[a], x, y, c)
                for to in targets:
                    pltpu.make_async_remote_copy(src_ref=own, dst_ref=own, send_sem=sems[2 * g].at[slot],
                                                 recv_sem=sems[2 * g + 1].at[slot], device_id=to, device_id_type=MESH).start()
        token[...] = jnp.zeros_like(token)

    sem_shapes = []
    for members in groups:
        sem_shapes += [pltpu.SemaphoreType.DMA((len(members),))] * 2
    outs = _call(
        body, name=name, in_specs=[HBM_SPEC] * n, out_specs=[SEM_SPEC] * (2 * ng) + [HBM_SPEC] * n + [TOKEN_SPEC],
        out_shape=sem_shapes + [pltpu.HBM(b.shape, b.dtype) for b in bufs] + [TOKEN],
        input_output_aliases={i: 2 * ng + i for i in range(n)},
        compiler_params=pltpu.CompilerParams(has_side_effects=EFFECT),
    )(*[_hbm(b) for b in bufs])
    sem_pairs = [(outs[2 * g], outs[2 * g + 1]) for g in range(ng)]
    return sem_pairs, list(outs[2 * ng:2 * ng + n]), outs[-1]


def _gather_forward(send_sems, recv_sems, bufs, after, name):
    n = len(bufs)
    rows = [b.shape[0] // N_DEV for b in bufs]

    def body(*refs):
        ins = refs[:n]
        send1, recv1 = refs[n], refs[n + 1]
        out0 = n + 2 + len(after)
        send2, recv2 = refs[out0], refs[out0 + 1]
        token = refs[-1]
        x, y, c = _position()
        for a in range(n):
            step1 = pltpu.make_async_remote_copy(
                src_ref=_whole(ins[a], 4 * rows[a]), dst_ref=_whole(ins[a], 4 * rows[a]), send_sem=send1.at[a],
                recv_sem=recv1.at[a], device_id=(x, y, c), device_id_type=MESH)
            step1.wait_send()
            step1.wait_recv()
        for a in range(n):
            for chip in _other_chips(x, y):
                blk = _own_rows(ins[a], rows[a], *chip, c)
                pltpu.make_async_remote_copy(src_ref=blk, dst_ref=blk, send_sem=send2.at[a], recv_sem=recv2.at[a],
                                             device_id=(x, y, 1 - c), device_id_type=MESH).start()
        token[...] = jnp.zeros_like(token)

    outs = _call(
        body, name=name, in_specs=[HBM_SPEC] * n + [SEM_SPEC, SEM_SPEC] + [HBM_SPEC] * len(after),
        out_specs=[SEM_SPEC, SEM_SPEC] + [HBM_SPEC] * n + [TOKEN_SPEC],
        out_shape=[pltpu.SemaphoreType.DMA((n,)), pltpu.SemaphoreType.DMA((n,))]
        + [pltpu.HBM(b.shape, b.dtype) for b in bufs] + [TOKEN],
        input_output_aliases={i: 2 + i for i in range(n)},
        compiler_params=pltpu.CompilerParams(has_side_effects=EFFECT),
    )(*bufs, send_sems, recv_sems, *after)
    return outs[0], outs[1], list(outs[2:2 + n]), outs[-1]


def _gather_done(send_sems, recv_sems, bufs, after, name):
    n = len(bufs)
    rows = [b.shape[0] // N_DEV for b in bufs]

    def body(*refs):
        ins = refs[:n]
        send2, recv2 = refs[n], refs[n + 1]
        x, y, c = _position()
        for a in range(n):
            step2 = pltpu.make_async_remote_copy(
                src_ref=_whole(ins[a], 3 * rows[a]), dst_ref=_whole(ins[a], 3 * rows[a]), send_sem=send2.at[a],
                recv_sem=recv2.at[a], device_id=(x, y, c), device_id_type=MESH)
            step2.wait_send()
            step2.wait_recv()

    outs = _call(
        body, name=name, in_specs=[HBM_SPEC] * n + [SEM_SPEC, SEM_SPEC] + [HBM_SPEC] * len(after),
        out_specs=[HBM_SPEC] * n, out_shape=[pltpu.HBM(b.shape, b.dtype) for b in bufs],
        input_output_aliases={i: i for i in range(n)},
        compiler_params=pltpu.CompilerParams(has_side_effects=EFFECT),
    )(*bufs, send_sems, recv_sems, *after)
    return list(outs)


def _whole(ref, nrows):
    return ref.at[pl.ds(0, nrows), :]


def _to_sibling(x, y, c):
    return [(2 * q + (1 - c), q, (x, y, 1 - c)) for q in range(4)]


def _to_chips(x, y, c):
    return [(2 * px + py, j, (px, py, c)) for j, (px, py) in enumerate(_other_chips(x, y))]


def _exchange_start(srcs, src_slots, plan, name):
    n = len(srcs)
    rows = [a.shape[0] // src_slots for a in srcs]
    n_copies = len(plan(0, 0, 0))
    lands = [lax.empty((n_copies * r, a.shape[1]), a.dtype) for a, r in zip(srcs, rows)]

    def body(*refs):
        ins, land_refs = refs[:n], refs[n:2 * n]
        send_sems, recv_sems = refs[2 * n], refs[2 * n + 1]
        token = refs[-1]
        for a in range(n):
            r = rows[a]
            for src_slot, dst_slot, target in plan(*_position()):
                pltpu.make_async_remote_copy(
                    src_ref=ins[a].at[pl.ds(src_slot * r, r), :], dst_ref=land_refs[a].at[pl.ds(dst_slot * r, r), :],
                    send_sem=send_sems.at[a], recv_sem=recv_sems.at[a], device_id=target, device_id_type=MESH).start()
        token[...] = jnp.zeros_like(token)

    outs = _call(
        body, name=name, in_specs=[HBM_SPEC] * (2 * n),
        out_specs=[SEM_SPEC, SEM_SPEC] + [HBM_SPEC] * (2 * n) + [TOKEN_SPEC],
        out_shape=[pltpu.SemaphoreType.DMA((n,)), pltpu.SemaphoreType.DMA((n,))]
        + [pltpu.HBM(a.shape, a.dtype) for a in srcs] + [pltpu.HBM(l.shape, l.dtype) for l in lands] + [TOKEN],
        input_output_aliases={i: 2 + i for i in range(2 * n)},
        compiler_params=pltpu.CompilerParams(has_side_effects=EFFECT),
    )(*[_hbm(a) for a in srcs], *[_hbm(l) for l in lands])
    return outs[0], outs[1], list(outs[2:2 + n]), list(outs[2 + n:2 + 2 * n]), outs[-1]


def _exchange_wait(send_sems, recv_sems, srcs, lands, after, name):
    n = len(srcs)

    def body(*refs):
        ins, land_refs = refs[:n], refs[n:2 * n]
        send_sems_ref, recv_sems_ref = refs[2 * n], refs[2 * n + 1]
        for a in range(n):
            span = _whole(land_refs[a], lands[a].shape[0])
            cp = pltpu.make_async_remote_copy(
                src_ref=span, dst_ref=span, send_sem=send_sems_ref.at[a],
                recv_sem=recv_sems_ref.at[a], device_id=_position(), device_id_type=MESH)
            cp.wait_send()
            cp.wait_recv()

    outs = _call(
        body, name=name, in_specs=[HBM_SPEC] * (2 * n) + [SEM_SPEC, SEM_SPEC] + [HBM_SPEC] * len(after),
        out_specs=[HBM_SPEC] * (2 * n),
        out_shape=[pltpu.HBM(a.shape, a.dtype) for a in srcs] + [pltpu.HBM(l.shape, l.dtype) for l in lands],
        input_output_aliases={i: i for i in range(2 * n)},
        compiler_params=pltpu.CompilerParams(has_side_effects=EFFECT),
    )(*srcs, *lands, send_sems, recv_sems, *after)
    return list(outs[:n]), list(outs[n:])


def _chip_partial(grad, recv, idx, name):
    r = recv.shape[0] // 4

    def body(i_ref, g_ref, s_ref, o_ref):
        del i_ref
        o_ref[...] = (g_ref[...].astype(F32) + s_ref[...].astype(F32)).astype(BF16)

    nb = 1
    tr = r // nb
    grid_spec = pltpu.PrefetchScalarGridSpec(
        num_scalar_prefetch=1, grid=(3, nb),
        in_specs=[pl.BlockSpec((tr, D), lambda t, i, i_ref: ((2 * i_ref[1 + t] + i_ref[0]) * nb + i, 0)),
                  pl.BlockSpec((tr, D), lambda t, i, i_ref: (i_ref[1 + t] * nb + i, 0))],
        out_specs=pl.BlockSpec((tr, D), lambda t, i, i_ref: (i_ref[1 + t] * nb + i, 0)))
    return _call(body, name=name, grid_spec=grid_spec, out_shape=_sds((4 * r, D), BF16),
                 compiler_params=_params("arbitrary", "arbitrary"))(idx, grad, recv)


def _adamw_math(w, g, m, v):
    m2 = B1 * m + (1.0 - B1) * g
    v2 = B2 * v + (1.0 - B2) * jnp.square(g)
    m_hat = m2 / (1.0 - B1 ** STEP)
    v_hat = v2 / (1.0 - B2 ** STEP)
    return -LR * (m_hat / (jnp.sqrt(v_hat) + EPS_ADAM) + WD * w), m2, v2


def _reduce_adamw(w, grad, from_sibling, from_chips, idx, m, v, name):
    r = w.shape[0]
    assert grad.shape == (N_DEV * r, D) and from_sibling.shape == (4 * r, D) and from_chips.shape == (3 * r, D)
    tr = r // 2
    nb = r // tr

    def body(i_ref, w_ref, p_ref, s_ref, r0_ref, r1_ref, r2_ref, m_ref, v_ref, g_ref, d_ref, nm_ref, nv_ref):
        del i_ref
        g = p_ref[...].astype(F32) + s_ref[...].astype(F32)
        g = ((g + r0_ref[...].astype(F32)) + r1_ref[...].astype(F32)) + r2_ref[...].astype(F32)
        g_ref[...] = g
        d_ref[...], nm_ref[...], nv_ref[...] = _adamw_math(w_ref[...], g, m_ref[...], v_ref[...])

    own = pl.BlockSpec((tr, D), lambda i, i_ref: (i, 0))
    grid_spec = pltpu.PrefetchScalarGridSpec(
        num_scalar_prefetch=1, grid=(nb,),
        in_specs=[own, pl.BlockSpec((tr, D), lambda i, i_ref: (i_ref[0] * nb + i, 0)),
                  pl.BlockSpec((tr, D), lambda i, i_ref: (i_ref[1] * nb + i, 0))]
        + [pl.BlockSpec((tr, D), lambda i, i_ref, j=j: (j * nb + i, 0)) for j in range(3)] + [own, own],
        out_specs=[own] * 4)
    return _call(body, name=name, grid_spec=grid_spec, out_shape=[_sds((r, D), F32)] * 4,
                 compiler_params=_params("parallel"))(idx, w, grad, from_sibling, from_chips, from_chips, from_chips, m, v)


SMALL_ROWS = 8


def _small_all_reduce(pack, name, after=()):
    def body(p_ref, *rest):
        tot_ref, loss_ref, gath, send_sems, recv_sems = rest[len(after):]
        x, y, c = _position()
        me_id = 4 * x + 2 * y + c
        gath[me_id] = p_ref[...]
        copies = []
        for k in range(1, N_DEV):
            peer = tuple(1 - v if (k >> b) & 1 else v for v, b in ((x, 2), (y, 1), (c, 0)))
            cp = pltpu.make_async_remote_copy(src_ref=p_ref, dst_ref=gath.at[me_id], send_sem=send_sems.at[k - 1],
                                              recv_sem=recv_sems.at[k - 1], device_id=peer, device_id_type=MESH)
            cp.start()
            copies.append(cp)
        for cp in copies:
            cp.wait_recv()
        for cp in copies:
            cp.wait_send()
        tot = gath[0]
        for d in range(1, N_DEV):
            tot = tot + gath[d]
        tot_ref[...] = tot
        loss_ref[...] = jnp.full((1, 128), (0.5 / D) * jnp.sum(tot[SMALL_ROWS - 1:SMALL_ROWS, :]), F32)

    vm = pl.BlockSpec(memory_space=pltpu.VMEM)
    return _call(
        body, name=name, in_specs=[vm] + [HBM_SPEC] * len(after), out_specs=[vm, vm],
        out_shape=[_sds((SMALL_ROWS, D), F32), _sds((1, 128), F32)],
        scratch_shapes=[pltpu.VMEM((N_DEV, SMALL_ROWS, D), F32), pltpu.SemaphoreType.DMA((N_DEV - 1,)),
                        pltpu.SemaphoreType.DMA((N_DEV - 1,))],
    )(pack, *after)


def _adamw_small(ws, gs, ms, vs, name):
    n = len(ws)

    def body(*refs):
        for a in range(n):
            w_ref, g_ref, m_ref, v_ref = (refs[k * n + a] for k in range(4))
            d_ref, nm_ref, nv_ref = (refs[(4 + k) * n + a] for k in range(3))
            d_ref[...], nm_ref[...], nv_ref[...] = _adamw_math(w_ref[...], g_ref[...], m_ref[...], v_ref[...])

    vm = pl.BlockSpec(memory_space=pltpu.VMEM)
    outs = _call(body, name=name, in_specs=[vm] * (4 * n), out_specs=[vm] * (3 * n),
                 out_shape=[_sds(w.shape, F32) for w in ws] * 3)(*ws, *gs, *ms, *vs)
    return [(outs[a], outs[n + a], outs[2 * n + a]) for a in range(n)]


def kernel(x, g_mix, w_in, conv_w, attn_sinks, w_conv_out, w_attn_out, w_o, g_ffn, w_gate_up, w_down, g_final, loss_target, m_g_mix, m_w_in, m_conv_w, m_attn_sinks, m_w_conv_out, m_w_attn_out, m_w_o, m_g_ffn, m_w_gate_up, m_w_down, m_g_final, v_g_mix, v_w_in, v_conv_w, v_attn_sinks, v_w_conv_out, v_w_attn_out, v_w_o, v_g_ffn, v_w_gate_up, v_w_down, v_g_final):
    cx, cy, cc = _position()
    chip = 2 * cx + cy
    partial_idx = jnp.stack([cc, 2 * (1 - cx) + cy, 2 * cx + (1 - cy), 2 * (1 - cx) + (1 - cy)]).astype(jnp.int32)
    own_idx = jnp.stack([2 * chip + cc, chip]).astype(jnp.int32)
    me = 4 * cx + 2 * cy + cc

    me_idx = jnp.reshape(me, (1,)).astype(jnp.int32)
    first = [_place(jnp.transpose(w_in[0]), me_idx, BF16, "place_w_in"),
             _place(jnp.pad(conv_w[0], ((0, 5), (0, 0))), me_idx, F32, "place_conv_w")]
    (to_near,), first, token_in = _gather_phase(first, [], [_own_to_near], (), "gather_in_start")
    gather_tokens = (token_in,)

    class Gathered:
        def __init__(self):
            self.state = {}

        def begin(self, group, after):
            if group == "in":
                (near, relay), bufs, token = _gather_phase(
                    first, [(*to_near, 3, 3)], [_near_to_sibling, _relay_diagonal], after, "gather_in_relay")
                later = [_place(w, me_idx, BF16, "place_" + k, after=(token,)) for k, w in (
                    ("w_conv_out", w_conv_out[0]), ("w_attn_out", w_attn_out[0]), ("w_o", w_o[0]),
                    ("w_gate_up", jnp.transpose(w_gate_up[0])), ("w_down", w_down[0]))]
                (sems_mix, sems_ffn), later, token_later = _gather_start(later, [[0, 1, 2], [3, 4]], "gather_start_later")
                self.state.update({"in": (near, relay, bufs), "mix": (sems_mix, later[:3]), "ffn": (sems_ffn, later[3:])})
                return (token_later,)
            (send_sems, recv_sems), group_bufs = self.state[group]
            send2, recv2, group_bufs, token = _gather_forward(send_sems, recv_sems, group_bufs, after, "gather_forward_" + group)
            self.state[group] = ((send2, recv2), group_bufs)
            return (token,)

        def end(self, group, after):
            if group == "in":
                near, relay, bufs = self.state[group]
                (last,), bufs, token = _gather_phase(bufs, [(*relay, 1, 1)], [_diagonal_to_sibling], after, "gather_in_last")
                _, full, _ = _gather_phase(bufs, [(*near, 2, 2), (*last, 1, 1)], [], (token,), "gather_in_done")
                return full[0], jnp.transpose(full[1].reshape(N_DEV, 8, 128)[:, :3, :], (1, 0, 2)).reshape(3, D)
            (send2, recv2), group_bufs = self.state[group]
            return _gather_done(send2, recv2, group_bufs, after, "gather_done_" + group)

    in_flight, own_pieces = {}, {}

    transposed = ("w_in", "w_gate_up")

    def as2d(k, a):
        if k in transposed:
            return jnp.transpose(a[0])
        return a[None] if a.ndim == 1 else (a[0] if a.ndim == 3 else a)

    w_all = {"g_mix": g_mix, "w_in": w_in, "conv_w": conv_w, "attn_sinks": attn_sinks, "w_conv_out": w_conv_out,
             "w_attn_out": w_attn_out, "w_o": w_o, "g_ffn": g_ffn, "w_gate_up": w_gate_up, "w_down": w_down, "g_final": g_final}
    m_all = {"g_mix": m_g_mix, "w_in": m_w_in, "conv_w": m_conv_w, "attn_sinks": m_attn_sinks, "w_conv_out": m_w_conv_out,
             "w_attn_out": m_w_attn_out, "w_o": m_w_o, "g_ffn": m_g_ffn, "w_gate_up": m_w_gate_up, "w_down": m_w_down,
             "g_final": m_g_final}
    v_all = {"g_mix": v_g_mix, "w_in": v_w_in, "conv_w": v_conv_w, "attn_sinks": v_attn_sinks, "w_conv_out": v_w_conv_out,
             "w_attn_out": v_w_attn_out, "w_o": v_w_o, "g_ffn": v_g_ffn, "w_gate_up": v_w_gate_up, "w_down": v_w_down,
             "g_final": v_g_final}
    results = {}

    def record(k, *vals):
        results[k] = [(jnp.transpose(val) if k in transposed else val).reshape(w_all[k].shape) for val in vals]

    def update(k, pieces):
        g, d, nm, nv = _reduce_adamw(as2d(k, w_all[k]), *pieces, own_idx, as2d(k, m_all[k]), as2d(k, v_all[k]), "adamw_" + k)
        record(k, g, d, nm, nv)
        return nm

    def update_small(grads):
        keys = list(grads)
        outs = _adamw_small([as2d(k, w_all[k]) for k in keys], [grads[k] for k in keys], [as2d(k, m_all[k]) for k in keys],
                            [as2d(k, v_all[k]) for k in keys], "adamw_small")
        for k, (d, nm, nv) in zip(keys, outs):
            record(k, grads[k], d, nm, nv)
        return tuple(nm for _, nm, _ in outs)

    kernel_name = {"win_t": "w_in", "wgu_t": "w_gate_up", "wd": "w_down", "wco": "w_conv_out", "wao": "w_attn_out", "wo": "w_o"}

    def finish(group, after):
        keys, send_sems, recv_sems, parts, from_chips = in_flight[group]
        _, from_chips = _exchange_wait(send_sems, recv_sems, parts, from_chips, after, "rs_chips_wait_" + group)
        grads, from_sibling = own_pieces[group]
        return tuple(update(kernel_name[k], p) for k, *p in zip(keys, grads, from_sibling, from_chips))

    class Reducer:
        def start(self, group, gdict):
            keys, glist = list(gdict), list(gdict.values())
            send_sems, recv_sems, glist, lands, token = _exchange_start(glist, N_DEV, _to_sibling, "rs_sibling_start_" + group)
            in_flight[group] = (keys, send_sems, recv_sems, glist, lands)
            return (token,)

        def middle(self, group, after):
            keys, send_sems, recv_sems, glist, lands = in_flight[group]
            if group == "in":
                after = finish("ffn", after)
            glist, lands = _exchange_wait(send_sems, recv_sems, glist, lands, after, "rs_sibling_wait_" + group)
            parts = [_chip_partial(g, r, partial_idx, "chip_partial_" + k) for k, g, r in zip(keys, glist, lands)]
            send_sems, recv_sems, parts, from_chips, token = _exchange_start(parts, 4, _to_chips, "rs_chips_start_" + group)
            in_flight[group] = (keys, send_sems, recv_sems, parts, from_chips)
            own_pieces[group] = (glist, lands)
            return (token,)

    dx, _, small = _local_step(x[0], loss_target[0], g_mix, g_ffn, g_final[None], attn_sinks, Gathered(),
                               reducer=Reducer(), after=gather_tokens)
    after = finish("mix", (dx,))

    sinks_row = jnp.pad(small["sinks"], ((0, 0), (0, D - 128)))
    pack = jnp.concatenate([small["g_mix"], small["g_ffn"], small["g_final"], small["conv_w"], sinks_row, small["lossvec"]], axis=0)
    tot, loss_row = _small_all_reduce(pack, "small_all_reduce", after=after)
    loss = loss_row[0, 0]
    g_small = {
        "g_mix": tot[0:1], "g_ffn": tot[1:2], "g_final": tot[2:3],
        "conv_w": lax.dynamic_slice(tot, (3, me * 128), (3, 128)), "attn_sinks": tot[6:7, :N_HEADS],
    }
    finish("in", update_small(g_small))

    order = ["g_mix", "w_in", "conv_w", "attn_sinks", "w_conv_out", "w_attn_out", "w_o", "g_ffn", "w_gate_up", "w_down", "g_final"]
    return (loss, dx[None], *[results[k][i] for i in range(4) for k in order])
```

```python
import functools
import math

import jax
import jax.numpy as jnp
from jax import lax
from jax.experimental import pallas as pl
from jax.experimental.pallas import tpu as pltpu

F32 = jnp.float32
BF16 = jnp.bfloat16

D = 1024
HEAD_DIM = 64
N_HEADS = 16
N_KV = 4
GROUP = N_HEADS // N_KV
D_KV = N_KV * HEAD_DIM
BLOCK = 128
ROT_DIM = HEAD_DIM // 4
ROPE_THETA = 500000.0
ATTN_SCALE = 1.0 / math.sqrt(HEAD_DIM)
NEG_INF = -1e30
D_FF = 2816
N_IN = 6656
EPS = 1e-5
C_CB, C_CC, C_CX, C_Q, C_K, C_V, C_GC, C_GA = 0, 1024, 2048, 3072, 4096, 4352, 4608, 5632

LR, B1, B2, EPS_ADAM, WD, STEP = 0.001, 0.9, 0.999, 1e-08, 0.01, 10

N_DEV = 8
MESH = pl.DeviceIdType.MESH
VMEM_LIMIT = 56 * 1024 * 1024

NN = (((1,), (0,)), ((), ()))
NT = (((1,), (1,)), ((), ()))
TN = (((0,), (0,)), ((), ()))
HBM_SPEC = pl.BlockSpec(memory_space=pl.ANY)
ROW_SPLIT = 4


def _call(body, **kw):
    return pl.pallas_call(body, **kw)


def _params(*sem):
    return pltpu.CompilerParams(dimension_semantics=sem, vmem_limit_bytes=VMEM_LIMIT)


def _sds(shape, dtype):
    return jax.ShapeDtypeStruct(shape, dtype)


def _matmul(a, b, *, mode, tm, tn, tk, out_dtype, name, res=None, after=()):
    parts = list(a) if isinstance(a, (list, tuple)) else [a]
    rows_a = parts[0].shape[0]
    cols_a = sum(p.shape[1] for p in parts)
    if mode == "nn":
        (m, kk), (_, n), dims = (rows_a, cols_a), b.shape, NN
    elif mode == "nt":
        (m, kk), (n, _), dims = (rows_a, cols_a), b.shape, NT
    else:
        (kk, m), (_, n), dims = (rows_a, cols_a), b.shape, TN
    tm, tn, tk = min(tm, m), min(tn, n), min(tk, kk)
    assert m % tm == 0 and n % tn == 0 and kk % tk == 0, (name, m, n, kk, tm, tn, tk)
    nk = kk // tk
    split_axis, width = (2, tk) if mode == "nn" else (0, tm)
    assert len(parts) == 1 or mode in ("nn", "tn")
    assert len(parts) == 1 or all(p.shape[1] % width == 0 for p in parts), (name, width)
    counts = [p.shape[1] // width for p in parts]
    starts = [sum(counts[:p]) for p in range(len(parts))]

    def a_spec(p):
        def col(t):
            return jnp.clip(t - starts[p], 0, counts[p] - 1) if len(parts) > 1 else t

        if mode == "tn":
            return pl.BlockSpec((tk, tm), lambda i, j, k: (k, col(i)))
        return pl.BlockSpec((tm, tk), lambda i, j, k: (i, col(k)))

    if mode == "nt":
        b_spec = pl.BlockSpec((tn, tk), lambda i, j, k: (j, k))
    else:
        b_spec = pl.BlockSpec((tk, tn), lambda i, j, k: (k, j))
    o_spec = pl.BlockSpec((tm, tn), lambda i, j, k: (i, j))
    has_res = res is not None
    n_parts = len(parts)
    unit = 128 if mode == "tn" else 16
    split = ROW_SPLIT if tm % (ROW_SPLIT * unit) == 0 else 1

    def body(*refs):
        a_refs, b_ref = refs[:n_parts], refs[n_parts]
        r_ref = refs[n_parts + 1] if has_res else None
        o_ref = refs[n_parts + 1 + has_res + len(after)]
        k = pl.program_id(2)

        acc_ref = refs[-1] if nk > 1 else None

        def step(a_ref):
            def matmul(rows):
                a_blk = a_ref[:, rows] if mode == "tn" else a_ref[rows, :]
                return lax.dot_general(a_blk, b_ref[...], dims, preferred_element_type=F32)

            def finish(rows, part):
                if nk > 1:
                    acc_ref[rows, :] += part
                else:
                    o_ref[rows, :] = (part + r_ref[rows, :] if has_res else part).astype(o_ref.dtype)

            _row_pipeline(tm, matmul, finish, split)

        if nk > 1:
            @pl.when(k == 0)
            def _():
                acc_ref[...] = jnp.zeros_like(acc_ref)

        if n_parts == 1:
            step(a_refs[0])
        else:
            t = pl.program_id(split_axis)
            for p in range(n_parts):
                pl.when((t >= starts[p]) & (t < starts[p] + counts[p]))(functools.partial(step, a_refs[p]))

        if nk > 1:
            @pl.when(k == nk - 1)
            def _():
                o_ref[...] = (acc_ref[...] + r_ref[...] if has_res else acc_ref[...]).astype(o_ref.dtype)

    ins = parts + [b] + ([res] if has_res else []) + list(after)
    in_specs = [a_spec(p) for p in range(n_parts)] + [b_spec] + ([o_spec] if has_res else []) + [HBM_SPEC] * len(after)
    scratch = [] if nk == 1 else [pltpu.VMEM((tm, tn), F32)]
    return _call(
        body, name=name, grid=(m // tm, n // tn, nk), in_specs=in_specs, out_specs=o_spec,
        out_shape=_sds((m, n), out_dtype), scratch_shapes=scratch,
        compiler_params=_params("parallel", "parallel", "arbitrary"),
    )(*ins)


def _matmul_group(a_group, b_group, *, mode, tm, tn, out_dtype, name):
    a0, b0 = a_group[0], b_group[0]
    a_pair, b_pair, count = a_group, b_group, len(a_group)
    if mode == "nn":
        (m, kk), (_, n), dims = a0.shape, b0.shape, NN
    elif mode == "nt":
        (m, kk), (n, _), dims = a0.shape, b0.shape, NT
    else:
        (kk, m), (_, n), dims = a0.shape, b0.shape, TN
    assert all(a.shape == a0.shape for a in a_pair) and all(b.shape == b0.shape for b in b_pair)
    tm, tn = min(tm, m), min(tn, n)
    assert m % tm == 0 and n % tn == 0, (name, m, n, tm, tn)
    a_spec = pl.BlockSpec((kk, tm), lambda i, j: (0, i)) if mode == "tn" else pl.BlockSpec((tm, kk), lambda i, j: (i, 0))
    b_spec = pl.BlockSpec((tn, kk), lambda i, j: (j, 0)) if mode == "nt" else pl.BlockSpec((kk, tn), lambda i, j: (0, j))
    o_spec = pl.BlockSpec((tm, tn), lambda i, j: (i, j))
    unit = 128 if mode == "tn" else 16
    split = ROW_SPLIT if tm % (ROW_SPLIT * unit) == 0 else 1

    def body(*refs):
        a_refs, b_refs, o_refs = refs[:count], refs[count:2 * count], refs[2 * count:]

        def matmul(rows):
            return tuple(lax.dot_general(a_ref[:, rows] if mode == "tn" else a_ref[rows, :], b_ref[...], dims,
                                         preferred_element_type=F32) for a_ref, b_ref in zip(a_refs, b_refs))

        def finish(rows, parts):
            for o_ref, part in zip(o_refs, parts):
                o_ref[rows, :] = part.astype(out_dtype)

        _row_pipeline(tm, matmul, finish, split)

    return _call(
        body, name=name, grid=(m // tm, n // tn), in_specs=[a_spec] * count + [b_spec] * count, out_specs=[o_spec] * count,
        out_shape=[_sds((m, n), out_dtype)] * count, compiler_params=_params("parallel", "parallel"),
    )(*a_pair, *b_pair)


def _row_tile(s):
    return min(512, s)


def _rms_fwd(x, g, name, after=()):
    s = x.shape[0]
    tm = _row_tile(s)

    def body(x_ref, g_ref, *rest):
        h_ref = rest[-1]
        xv = x_ref[...]
        r = lax.rsqrt(jnp.mean(xv * xv, axis=-1, keepdims=True) + EPS)
        h_ref[...] = (xv * r * g_ref[...]).astype(BF16)

    row = pl.BlockSpec((tm, D), lambda i: (i, 0))
    return _call(
        body, name=name, grid=(s // tm,), in_specs=[row, pl.BlockSpec((1, D), lambda i: (0, 0))] + [HBM_SPEC] * len(after),
        out_specs=row, out_shape=_sds((s, D), BF16), compiler_params=_params("parallel"),
    )(x, g, *after)


def _rms_bwd(dh, x, g, dres, name, after=()):
    s = x.shape[0]
    tm = _row_tile(s)

    def body(dh_ref, x_ref, g_ref, dres_ref, *rest):
        dx_ref, dxb_ref, dg_ref = rest[len(after):]
        xv = x_ref[...]
        r = lax.rsqrt(jnp.mean(xv * xv, axis=-1, keepdims=True) + EPS)
        xh = xv * r
        dhv = dh_ref[...].astype(F32)
        dyg = dhv * g_ref[...]
        dx = dres_ref[...] + r * (dyg - xh * jnp.mean(dyg * xh, axis=-1, keepdims=True))
        dx_ref[...] = dx
        dxb_ref[...] = dx.astype(BF16)
        part = jnp.sum(dhv * xh, axis=0, keepdims=True)

        @pl.when(pl.program_id(0) == 0)
        def _():
            dg_ref[...] = part

        @pl.when(pl.program_id(0) > 0)
        def _():
            dg_ref[...] += part

    row = pl.BlockSpec((tm, D), lambda i: (i, 0))
    vec = pl.BlockSpec((1, D), lambda i: (0, 0))
    return _call(
        body, name=name, grid=(s // tm,), in_specs=[row, row, vec, row] + [HBM_SPEC] * len(after), out_specs=[row, row, vec],
        out_shape=[_sds((s, D), F32), _sds((s, D), BF16), _sds((1, D), F32)],
        compiler_params=_params("arbitrary"),
    )(dh, x, g, dres, *after)


def _loss_head(x2, g, tgt, name):
    s = x2.shape[0]
    tm = _row_tile(s)

    def body(x_ref, g_ref, t_ref, dx_ref, dxb_ref, dg_ref, l_ref):
        xv = x_ref[...]
        gv = g_ref[...]
        r = lax.rsqrt(jnp.mean(xv * xv, axis=-1, keepdims=True) + EPS)
        xh = xv * r
        err = xh * gv - t_ref[...]
        dy = err * (1.0 / D)
        dyg = dy * gv
        dx = r * (dyg - xh * jnp.mean(dyg * xh, axis=-1, keepdims=True))
        dx_ref[...] = dx
        dxb_ref[...] = dx.astype(BF16)
        dg_part = jnp.sum(dy * xh, axis=0, keepdims=True)
        l_part = jnp.sum(err * err, axis=0, keepdims=True)

        @pl.when(pl.program_id(0) == 0)
        def _():
            dg_ref[...] = dg_part
            l_ref[...] = l_part

        @pl.when(pl.program_id(0) > 0)
        def _():
            dg_ref[...] += dg_part
            l_ref[...] += l_part

    row = pl.BlockSpec((tm, D), lambda i: (i, 0))
    vec = pl.BlockSpec((1, D), lambda i: (0, 0))
    return _call(
        body, name=name, grid=(s // tm,), in_specs=[row, vec, row], out_specs=[row, row, vec, vec],
        out_shape=[_sds((s, D), F32), _sds((s, D), BF16), _sds((1, D), F32), _sds((1, D), F32)],
        compiler_params=_params("arbitrary"),
    )(x2, g, tgt)


CONV_TC = 256


def _shift_down(u, k, rows):
    return jnp.where(rows >= k, pltpu.roll(u, k, 0), 0.0)


def _shift_up(u, k, rows, s):
    return jnp.where(rows < s - k, pltpu.roll(u, s - k, 0), 0.0)


def _conv_specs(s):
    nb = D // CONV_TC

    def col(c0):
        return pl.BlockSpec((s, CONV_TC), lambda j, c0=c0: (0, c0 // CONV_TC + j))

    return nb, col


def _conv_fwd(proj, conv_w, name):
    s = proj.shape[0]
    nb, col = _conv_specs(s)

    def body(cb_ref, cc_ref, cx_ref, w_ref, y_ref):
        rows = lax.broadcasted_iota(jnp.int32, (s, CONV_TC), 0)
        u = cc_ref[...].astype(F32) * cx_ref[...].astype(F32)
        w = w_ref[...]
        c = w[0:1] * _shift_down(u, 2, rows) + w[1:2] * _shift_down(u, 1, rows) + w[2:3] * u
        y_ref[...] = (cb_ref[...].astype(F32) * c).astype(BF16)

    return _call(
        body, name=name, grid=(nb,),
        in_specs=[col(C_CB), col(C_CC), col(C_CX), pl.BlockSpec((3, CONV_TC), lambda j: (0, j))],
        out_specs=pl.BlockSpec((s, CONV_TC), lambda j: (0, j)), out_shape=_sds((s, D), BF16),
        compiler_params=_params("parallel"),
    )(proj, proj, proj, conv_w)


def _write_behind(t, nt, buf, sems, tiles, window, where):
    slot = t % 2

    def copies(sl, at):
        return [pltpu.make_async_copy(buf.at[sl, p], window(p, at), sems.at[sl, p]) for p in range(len(tiles))]

    @pl.when(t >= 2)
    def _():
        for cp in copies(slot, where):
            cp.wait()

    for p, tile in enumerate(tiles):
        buf[slot, p] = tile
    started = copies(slot, where)
    for cp in started:
        cp.start()

    @pl.when(t == nt - 1)
    def _():
        for cp in started:
            cp.wait()
        if nt > 1:
            for cp in copies(1 - slot, where):
                cp.wait()


def _conv_bwd(dy, proj, conv_w, dproj, name, after=()):
    s = proj.shape[0]
    nb, col = _conv_specs(s)

    def body(dy_ref, cb_ref, cc_ref, cx_ref, w_ref, *rest):
        dproj_ref, dw_ref, buf, sems = rest[1 + len(after):]
        j = pl.program_id(0)
        rows = lax.broadcasted_iota(jnp.int32, (s, CONV_TC), 0)
        cc = cc_ref[...].astype(F32)
        cx = cx_ref[...].astype(F32)
        u = cc * cx
        u1 = _shift_down(u, 1, rows)
        u2 = _shift_down(u, 2, rows)
        w = w_ref[...]
        c = w[0:1] * u2 + w[1:2] * u1 + w[2:3] * u
        dyv = dy_ref[...].astype(F32)
        dc = dyv * cb_ref[...].astype(F32)
        du = w[2:3] * dc + w[1:2] * _shift_up(dc, 1, rows, s) + w[0:1] * _shift_up(dc, 2, rows, s)

        def window(p, jj):
            start = pl.multiple_of((C_CB, C_CC, C_CX)[p] + jj * CONV_TC, CONV_TC)
            return dproj_ref.at[:, pl.ds(start, CONV_TC)]

        tiles = ((dyv * c).astype(BF16), (du * cx).astype(BF16), (du * cc).astype(BF16))
        _write_behind(j * 0, 1, buf, sems, tiles, window, j)
        dw_ref[...] = jnp.concatenate(
            [jnp.sum(dc * u2, axis=0, keepdims=True), jnp.sum(dc * u1, axis=0, keepdims=True),
             jnp.sum(dc * u, axis=0, keepdims=True)], axis=0)

    return _call(
        body, name=name, grid=(nb,),
        in_specs=[pl.BlockSpec((s, CONV_TC), lambda j: (0, j)), col(C_CB), col(C_CC), col(C_CX),
                  pl.BlockSpec((3, CONV_TC), lambda j: (0, j))] + [HBM_SPEC] * (1 + len(after)),
        out_specs=[pl.BlockSpec(memory_space=pl.ANY), pl.BlockSpec((3, CONV_TC), lambda j: (0, j))],
        out_shape=[_sds((s, N_IN), BF16), _sds((3, D), F32)],
        scratch_shapes=[pltpu.VMEM((1, 3, s, CONV_TC), BF16), pltpu.SemaphoreType.DMA((1, 3))],
        input_output_aliases={5: 0}, compiler_params=_params("arbitrary"),
    )(dy, proj, proj, proj, conv_w, dproj, *after)


def _rope_tables(s):
    half = ROT_DIM // 2
    inv_freq = ROPE_THETA ** (-jnp.arange(0, ROT_DIM, 2, dtype=F32) / ROT_DIM)
    inv64 = jnp.concatenate([inv_freq, inv_freq, jnp.zeros((HEAD_DIM - ROT_DIM,), F32)])
    ang = jnp.arange(s, dtype=F32)[:, None] * jnp.concatenate([inv64, inv64])[None, :]
    d = lax.broadcasted_iota(jnp.int32, (s, 128), 1) % HEAD_DIM
    cos, sin = jnp.cos(ang), jnp.sin(ang)
    c = jnp.where(d < ROT_DIM, cos, 1.0)
    a = jnp.where(d < half, -sin, 0.0)
    b = jnp.where((d >= half) & (d < ROT_DIM), sin, 0.0)
    return jnp.concatenate([c, a, b], axis=1)


def _rope(x, tab):
    c, a, b = tab[:, 0:128], tab[:, 128:256], tab[:, 256:384]
    outs = []
    for i in range(x.shape[1] // 128):
        xc = x[:, i * 128:(i + 1) * 128]
        outs.append(xc * c + pltpu.roll(xc, 120, 1) * a + pltpu.roll(xc, 8, 1) * b)
    return outs[0] if len(outs) == 1 else jnp.concatenate(outs, axis=1)


def _rope_t(dx, tab):
    c, a, b = tab[:, 0:128], tab[:, 128:256], tab[:, 256:384]
    outs = []
    for i in range(dx.shape[1] // 128):
        dc = dx[:, i * 128:(i + 1) * 128]
        outs.append(dc * c + pltpu.roll(dc * a, 8, 1) + pltpu.roll(dc * b, 120, 1))
    return outs[0] if len(outs) == 1 else jnp.concatenate(outs, axis=1)


def _attn_in_specs():
    prev = lambda n: jnp.maximum(n - 1, 0)
    return [
        pl.BlockSpec((BLOCK, D), lambda n: (n, C_Q // D)),
        pl.BlockSpec((BLOCK, D_KV), lambda n: (n, C_K // D_KV)),
        pl.BlockSpec((BLOCK, D_KV), lambda n: (prev(n), C_K // D_KV)),
        pl.BlockSpec((BLOCK, D_KV), lambda n: (n, C_V // D_KV)),
        pl.BlockSpec((BLOCK, D_KV), lambda n: (prev(n), C_V // D_KV)),
        pl.BlockSpec((BLOCK, 384), lambda n: (n, 0)),
        pl.BlockSpec((BLOCK, 384), lambda n: (prev(n), 0)),
        pl.BlockSpec(memory_space=pltpu.SMEM),
    ]


HALF = HEAD_DIM
N_CHUNK = D // 128


def _swa_bias(n):
    qi = lax.broadcasted_iota(jnp.int32, (BLOCK, 2 * BLOCK), 0)
    kj = lax.broadcasted_iota(jnp.int32, (BLOCK, 2 * BLOCK), 1)
    rel = qi + BLOCK - kj
    valid = (rel >= 0) & (rel < BLOCK) & ((kj >= BLOCK) | (n > 0))
    return jnp.where(valid, 0.0, NEG_INF)


def _halves(x):
    lo = lax.broadcasted_iota(jnp.int32, x.shape, 1) < HALF
    return jnp.where(lo, x, 0.0).astype(BF16), jnp.where(lo, 0.0, x).astype(BF16)


def _dup_heads(x):
    out = []
    for pair in range(N_KV // 2):
        xc = x[:, pair * 128:(pair + 1) * 128]
        xr = pltpu.roll(xc, HALF, 1)
        lo = lax.broadcasted_iota(jnp.int32, xc.shape, 1) < HALF
        out += [jnp.where(lo, xc, xr), jnp.where(lo, xr, xc)]
    return out


def _swa_load(q_ref, kc_ref, kp_ref, vc_ref, vp_ref, tc_ref, tp_ref):
    qf = _rope(q_ref[...].astype(F32), tc_ref[...]) * ATTN_SCALE
    q_halves = [_halves(qf[:, c * 128:(c + 1) * 128]) for c in range(N_CHUNK)]
    kf = jnp.concatenate([_rope(kp_ref[...].astype(F32), tp_ref[...]), _rope(kc_ref[...].astype(F32), tc_ref[...])], axis=0)
    vf = jnp.concatenate([vp_ref[...], vc_ref[...]], axis=0).astype(F32)
    return q_halves, _dup_heads(kf), _dup_heads(vf)


def _swa_probs(qh, kk, bias, sink):
    s = lax.dot_general(qh, kk, NT, preferred_element_type=F32) + bias
    m = jnp.maximum(jnp.max(jnp.maximum(s[:, :BLOCK], s[:, BLOCK:]), axis=1, keepdims=True), sink)
    return jnp.exp(s - m), m


def _swa_fwd(proj, tab, sinks, name, after=()):
    s = proj.shape[0]

    def body(q_ref, kc_ref, kp_ref, vc_ref, vp_ref, tc_ref, tp_ref, sink_ref, *rest):
        o_ref = rest[-1]
        n = pl.program_id(0)
        q_halves, kdup, vdup = _swa_load(q_ref, kc_ref, kp_ref, vc_ref, vp_ref, tc_ref, tp_ref)
        bias = _swa_bias(n)
        ones = jnp.ones((2 * BLOCK, 128), BF16)
        kk = [k.astype(BF16) for k in kdup]
        vv = [[jnp.concatenate([v_half, ones], axis=1) for v_half in _halves(v)] for v in vdup]
        heads = [(c, half) for c in range(N_CHUNK) for half in range(2)]
        scores = [lax.dot_general(q_halves[c][half], kk[c // (GROUP // 2)], NT, preferred_element_type=F32)
                  for c, half in heads]
        probs = []
        for (c, half), sc in zip(heads, scores):
            sc = sc + bias
            m = jnp.maximum(jnp.max(jnp.maximum(sc[:, :BLOCK], sc[:, BLOCK:]), axis=1, keepdims=True), sink_ref[0, 2 * c + half])
            probs.append((jnp.exp(sc - m).astype(BF16), jnp.exp(sink_ref[0, 2 * c + half] - m)))
        outs = [lax.dot_general(e, vv[c // (GROUP // 2)][half], NN, preferred_element_type=F32)
                for (c, half), (e, _) in zip(heads, probs)]
        for c in range(N_CHUNK):
            parts = [outs[2 * c + half][:, :128] * (1.0 / (outs[2 * c + half][:, 128:] + probs[2 * c + half][1]))
                     for half in range(2)]
            o_ref[:, c * 128:(c + 1) * 128] = (parts[0] + parts[1]).astype(BF16)

    return _call(
        body, name=name, grid=(s // BLOCK,), in_specs=_attn_in_specs() + [HBM_SPEC] * len(after),
        out_specs=pl.BlockSpec((BLOCK, D), lambda n: (n, 0)), out_shape=_sds((s, D), BF16),
        compiler_params=_params("parallel"),
    )(proj, proj, proj, proj, proj, tab, tab, sinks, *after)


def _swa_bwd(do, proj, tab, sinks, dproj, name, after=()):
    s = proj.shape[0]
    nblk = s // BLOCK
    kv_of = lambda c: c // (GROUP // 2)

    def body(do_ref, q_ref, kc_ref, kp_ref, vc_ref, vp_ref, tc_ref, tp_ref, sink_ref, *rest):
        dproj_ref, dk_ref, dv_ref, ds_ref, dqout, dkbuf, dvbuf, sems = rest[1 + len(after):]
        n = pl.program_id(0)

        @pl.when(n == 0)
        def _():
            dk_ref[...] = jnp.zeros_like(dk_ref)
            dv_ref[...] = jnp.zeros_like(dv_ref)
            ds_ref[...] = jnp.zeros_like(ds_ref)

        q_halves, kdup, vdup = _swa_load(q_ref, kc_ref, kp_ref, vc_ref, vp_ref, tc_ref, tp_ref)
        dof = do_ref[...].astype(F32)
        do_halves = [_halves(dof[:, c * 128:(c + 1) * 128]) for c in range(N_CHUNK)]
        bias = _swa_bias(n)
        ones = jnp.ones((2 * BLOCK, 128), BF16)
        kk = [k.astype(BF16) for k in kdup]
        vv = [v.astype(BF16) for v in vdup]
        k_halves = [_halves(k) for k in kdup]
        heads = [(c, half) for c in range(N_CHUNK) for half in range(2)]
        lane_row = lax.broadcasted_iota(jnp.int32, (1, 128), 1)
        lo_kv = lax.broadcasted_iota(jnp.int32, (2 * BLOCK, 128), 1) < HALF
        scores = [lax.dot_general(q_halves[c][half], kk[kv_of(c)], NT, preferred_element_type=F32) for c, half in heads]
        dps = [lax.dot_general(do_halves[c][half], vv[kv_of(c)], NT, preferred_element_type=F32) for c, half in heads]
        exps = []
        for (c, half), sc in zip(heads, scores):
            sink = sink_ref[0, 2 * c + half]
            sc = sc + bias
            m = jnp.maximum(jnp.max(jnp.maximum(sc[:, :BLOCK], sc[:, BLOCK:]), axis=1, keepdims=True), sink)
            exps.append((jnp.exp(sc - m), jnp.exp(sink - m)))
        sums = [lax.dot_general(e.astype(BF16), ones, NN, preferred_element_type=F32) for e, _ in exps]
        dsink_row = jnp.zeros((1, 128), F32)
        dsb, pb = [], []
        for h, ((e, es), row_sum, dp) in enumerate(zip(exps, sums, dps)):
            inv = 1.0 / (row_sum + es)
            p = e * jnp.concatenate([inv, inv], axis=1)
            t = p * dp
            delta = jnp.sum(t, axis=1, keepdims=True)
            dsb.append((t - p * delta).astype(BF16))
            pb.append(p.astype(BF16))
            dsink = -jnp.sum(es * inv * delta, axis=0, keepdims=True)
            dsink_row = dsink_row + jnp.where(lane_row == h, dsink, 0.0)
        dq_parts = [lax.dot_general(d, k_halves[kv_of(c)][half], NN, preferred_element_type=F32) for (c, half), d in zip(heads, dsb)]
        dk_parts = [lax.dot_general(d, q_halves[c][half], TN, preferred_element_type=F32) for (c, half), d in zip(heads, dsb)]
        dv_parts = [lax.dot_general(p, do_halves[c][half], TN, preferred_element_type=F32) for (c, half), p in zip(heads, pb)]
        dq = jnp.concatenate([(dq_parts[2 * c] + dq_parts[2 * c + 1]) * ATTN_SCALE for c in range(N_CHUNK)], axis=1)

        def kv_sum(parts, hk):
            acc = (parts[GROUP * hk] + parts[GROUP * hk + 1]) + (parts[GROUP * hk + 2] + parts[GROUP * hk + 3])
            return acc + pltpu.roll(acc, HALF, 1)

        for pair in range(N_KV // 2):
            dkbuf[:, pair * 128:(pair + 1) * 128] = jnp.where(lo_kv, kv_sum(dk_parts, 2 * pair), kv_sum(dk_parts, 2 * pair + 1))
            dvbuf[:, pair * 128:(pair + 1) * 128] = jnp.where(lo_kv, kv_sum(dv_parts, 2 * pair), kv_sum(dv_parts, 2 * pair + 1))
        prev0 = pl.multiple_of(jnp.maximum(n - 1, 0) * BLOCK, BLOCK)
        cur0 = pl.multiple_of(n * BLOCK, BLOCK)

        @pl.when(n > 0)
        def _():
            dk_ref[pl.ds(prev0, BLOCK), :] += dkbuf[0:BLOCK, :]
            dv_ref[pl.ds(prev0, BLOCK), :] += dvbuf[0:BLOCK, :]

        dk_ref[pl.ds(cur0, BLOCK), :] += dkbuf[BLOCK:2 * BLOCK, :]
        dv_ref[pl.ds(cur0, BLOCK), :] += dvbuf[BLOCK:2 * BLOCK, :]
        ds_ref[...] += dsink_row

        def window(p, at):
            return dproj_ref.at[pl.ds(pl.multiple_of(at * BLOCK, BLOCK), BLOCK), pl.ds(C_Q, D)]

        _write_behind(n, nblk, dqout, sems, (_rope_t(dq, tc_ref[...]).astype(BF16),), window, n)

    blk = lambda w: pl.BlockSpec((BLOCK, w), lambda n: (n, 0))
    whole = lambda w: pl.BlockSpec((s, w), lambda n: (0, 0))
    n_in = 1 + len(_attn_in_specs())
    return _call(
        body, name=name, grid=(nblk,), in_specs=[blk(D)] + _attn_in_specs() + [HBM_SPEC] * (1 + len(after)),
        out_specs=[HBM_SPEC, whole(D_KV), whole(D_KV), pl.BlockSpec((1, 128), lambda n: (0, 0))],
        out_shape=[_sds((s, N_IN), BF16), _sds((s, D_KV), F32), _sds((s, D_KV), F32), _sds((1, 128), F32)],
        scratch_shapes=[pltpu.VMEM((2, 1, BLOCK, D), BF16), pltpu.VMEM((2 * BLOCK, D_KV), F32),
                        pltpu.VMEM((2 * BLOCK, D_KV), F32), pltpu.SemaphoreType.DMA((2, 1))],
        input_output_aliases={n_in: 0}, compiler_params=_params("arbitrary"),
    )(do, proj, proj, proj, proj, proj, tab, tab, sinks, dproj, *after)


def _kv_bwd(dkr, dv, tab, dproj, name):
    s = dkr.shape[0]
    tm = _row_tile(s)

    def body(dk_ref, dv_ref, t_ref, dproj_in, o_ref):
        del dproj_in
        o_ref[:, 0:D_KV] = _rope_t(dk_ref[...], t_ref[...]).astype(BF16)
        o_ref[:, D_KV:2 * D_KV] = dv_ref[...].astype(BF16)

    row = lambda w: pl.BlockSpec((tm, w), lambda i: (i, 0))
    return _call(
        body, name=name, grid=(s // tm,),
        in_specs=[row(D_KV), row(D_KV), row(384), pl.BlockSpec(memory_space=pl.ANY)],
        out_specs=pl.BlockSpec((tm, 2 * D_KV), lambda i: (i, C_K // (2 * D_KV))),
        out_shape=_sds((s, N_IN), BF16), input_output_aliases={3: 0}, compiler_params=_params("parallel"),
    )(dkr, dv, tab, dproj)


EW_TC = 512


def _sigmoid(x):
    return 0.5 * jnp.tanh(0.5 * x) + 0.5


def _merge_fwd(proj, conv_out, attn_out, name):
    s = proj.shape[0]
    tm = _row_tile(s)
    tile = pl.BlockSpec((tm, EW_TC), lambda i, j: (i, j))

    def body(gc_ref, ga_ref, co_ref, ao_ref, o_ref):
        o_ref[...] = (_sigmoid(gc_ref[...].astype(F32)) * co_ref[...].astype(F32)
                      + _sigmoid(ga_ref[...].astype(F32)) * ao_ref[...].astype(F32)).astype(BF16)

    return _call(
        body, name=name, grid=(s // tm, D // EW_TC),
        in_specs=[pl.BlockSpec((tm, EW_TC), lambda i, j: (i, C_GC // EW_TC + j)),
                  pl.BlockSpec((tm, EW_TC), lambda i, j: (i, C_GA // EW_TC + j)), tile, tile],
        out_specs=tile, out_shape=_sds((s, D), BF16), compiler_params=_params("parallel", "parallel"),
    )(proj, proj, conv_out, attn_out)


def _merge_bwd(dmerged, proj, conv_out, attn_out, name):
    s = proj.shape[0]
    tm = _row_tile(s)
    tile = pl.BlockSpec((tm, EW_TC), lambda i, j: (i, j))
    anyspec = pl.BlockSpec(memory_space=pl.ANY)

    def body(dm_ref, gc_ref, ga_ref, co_ref, ao_ref, dproj_ref, dco_ref, dao_ref, buf, sems):
        i, j = pl.program_id(0), pl.program_id(1)
        dm = dm_ref[...].astype(F32)
        sc = _sigmoid(gc_ref[...].astype(F32))
        sa = _sigmoid(ga_ref[...].astype(F32))
        dco_ref[...] = (dm * sc).astype(BF16)
        dao_ref[...] = (dm * sa).astype(BF16)
        tiles = ((dm * co_ref[...].astype(F32) * sc * (1.0 - sc)).astype(BF16),
                 (dm * ao_ref[...].astype(F32) * sa * (1.0 - sa)).astype(BF16))

        def window(p, at):
            start = pl.multiple_of((C_GC, C_GA)[p] + at[1] * EW_TC, EW_TC)
            return dproj_ref.at[pl.ds(pl.multiple_of(at[0] * tm, tm), tm), pl.ds(start, EW_TC)]

        _write_behind(i * nj + j, (s // tm) * nj, buf, sems, tiles, window, (i, j))

    nj = D // EW_TC
    return _call(
        body, name=name, grid=(s // tm, nj),
        in_specs=[tile, pl.BlockSpec((tm, EW_TC), lambda i, j: (i, C_GC // EW_TC + j)),
                  pl.BlockSpec((tm, EW_TC), lambda i, j: (i, C_GA // EW_TC + j)), tile, tile],
        out_specs=[anyspec, tile, tile],
        out_shape=[_sds((s, N_IN), BF16), _sds((s, D), BF16), _sds((s, D), BF16)],
        scratch_shapes=[pltpu.VMEM((2, 2, tm, EW_TC), BF16), pltpu.SemaphoreType.DMA((2, 2))],
        compiler_params=_params("arbitrary", "arbitrary"),
    )(dmerged, proj, proj, conv_out, attn_out)


FF_TC = 256
FF_TM = 2048


def _row_pipeline(tm, matmul, finish, split=ROW_SPLIT):
    step = tm // split
    pending = None
    for r in range(split):
        rows = pl.ds(r * step, step)
        result = matmul(rows)
        if pending is not None:
            finish(*pending)
        pending = (rows, result)
    finish(*pending)


def _gate_up_fwd(h2, wgu_t, name):
    s = h2.shape[0]
    tm = min(FF_TM, s)
    nb = D_FF // FF_TC

    def body(h_ref, wg_ref, wu_ref, a_ref, g_ref, u_ref):
        def matmuls(rows):
            h = h_ref[rows, :]
            return (lax.dot_general(h, wg_ref[...], NT, preferred_element_type=F32),
                    lax.dot_general(h, wu_ref[...], NT, preferred_element_type=F32))

        def finish(rows, gu):
            g, u = gu
            a_ref[rows, :] = (g * _sigmoid(g) * u).astype(BF16)
            g_ref[rows, :] = g.astype(BF16)
            u_ref[rows, :] = u.astype(BF16)

        _row_pipeline(tm, matmuls, finish)

    tile = pl.BlockSpec((tm, FF_TC), lambda j, i: (i, j))
    return _call(
        body, name=name, grid=(nb, s // tm),
        in_specs=[pl.BlockSpec((tm, D), lambda j, i: (i, 0)), pl.BlockSpec((FF_TC, D), lambda j, i: (j, 0)),
                  pl.BlockSpec((FF_TC, D), lambda j, i: (nb + j, 0))],
        out_specs=[tile, tile, tile], out_shape=[_sds((s, D_FF), BF16)] * 3,
        compiler_params=_params("parallel", "parallel"),
    )(h2, wgu_t, wgu_t)


def _down_bwd_x(dx2b, wd, gate, up, name):
    s = dx2b.shape[0]
    tm = min(FF_TM, s)
    nb = D_FF // FF_TC

    def body(dx_ref, w_ref, g_ref, u_ref, dg_ref, du_ref):
        def matmul(rows):
            return lax.dot_general(dx_ref[rows, :], w_ref[...], NT, preferred_element_type=F32)

        def finish(rows, da):
            g = g_ref[rows, :].astype(F32)
            sg = _sigmoid(g)
            dg_ref[rows, :] = (da * u_ref[rows, :].astype(F32) * (sg * (1.0 + g * (1.0 - sg)))).astype(BF16)
            du_ref[rows, :] = (da * (g * sg)).astype(BF16)

        _row_pipeline(tm, matmul, finish)

    tile = pl.BlockSpec((tm, FF_TC), lambda j, i: (i, j))
    return _call(
        body, name=name, grid=(nb, s // tm),
        in_specs=[pl.BlockSpec((tm, D), lambda j, i: (i, 0)), pl.BlockSpec((FF_TC, D), lambda j, i: (j, 0)), tile, tile],
        out_specs=[tile, tile], out_shape=[_sds((s, D_FF), BF16)] * 2,
        compiler_params=_params("parallel", "parallel"),
    )(dx2b, wd, gate, up)


class _Weights:
    def __init__(self, **groups):
        self.groups = groups

    def begin(self, group, after):
        return ()

    def end(self, group, after):
        return self.groups[group]


class _NoReduce:
    def start(self, group, grads):
        return ()

    def middle(self, group, after):
        return ()


def _local_step(x, tgt, g_mix, g_ffn, g_final, sinks, weights, reducer=None, after=()):
    reducer = reducer or _NoReduce()
    s = x.shape[0]
    tab = _rope_tables(s)
    big = dict(tm=2048, tn=512, tk=1024)
    h1 = _rms_fwd(x, g_mix, "rms1_fwd", after=after)
    win_t, conv_w = weights.end("in", weights.begin("in", (h1,)))
    proj = _matmul(h1, win_t, mode="nt", out_dtype=BF16, name="proj_fwd", tm=2048, tn=512, tk=1024)
    attn = _swa_fwd(proj, tab, sinks, "attn_fwd", after=weights.begin("mix", (proj,)))
    wco, wao, wo = weights.end("mix", (attn,))
    conv_y = _conv_fwd(proj, conv_w, "conv_fwd")
    conv_out, attn_out = _matmul_group((conv_y, attn), (wco, wao), mode="nn", tm=2048, tn=512, out_dtype=BF16, name="branch_out_fwd")
    merged = _merge_fwd(proj, conv_out, attn_out, "merge_fwd")
    x1 = _matmul(merged, wo, mode="nn", out_dtype=F32, name="wo_fwd", res=x, after=weights.begin("ffn", (merged,)), **big)
    h2 = _rms_fwd(x1, g_ffn, "rms2_fwd")
    wgu_t, wd = weights.end("ffn", (h2,))
    act, gate, up = _gate_up_fwd(h2, wgu_t, "gate_up_fwd")
    x2 = _matmul(act, wd, mode="nn", out_dtype=F32, name="down_fwd", res=x1, tm=1024, tn=512, tk=D_FF)
    dx2, dx2b, dg_final, lossvec = _loss_head(x2, g_final, tgt, "loss_head")
    dgate, dup = _down_bwd_x(dx2b, wd, gate, up, "down_bwd_x")
    g_wd = _matmul(act, dx2b, mode="tn", out_dtype=BF16, name="down_bwd_w", tm=1408, tn=1024, tk=2048)
    dh2 = _matmul([dgate, dup], wgu_t, mode="nn", out_dtype=BF16, name="gate_up_bwd_x", tm=1024, tn=1024, tk=1408)
    g_wgu_t = _matmul([dgate, dup], h2, mode="tn", out_dtype=BF16, name="gate_up_bwd_w", tm=1408, tn=1024, tk=2048)
    after_ffn = reducer.start("ffn", dict(wgu_t=g_wgu_t, wd=g_wd))
    dx1, dx1b, dg_ffn = _rms_bwd(dh2, x1, g_ffn, dx2, "rms2_bwd")
    dmerged = _matmul(dx1b, wo, mode="nt", out_dtype=BF16, name="wo_bwd_x", after=after_ffn, **big)
    after_ffn = reducer.middle("ffn", (dmerged,))
    g_wo = _matmul(merged, dx1b, mode="tn", out_dtype=BF16, name="wo_bwd_w", tm=512, tn=1024, tk=2048, after=after_ffn)
    dproj, dco, dao = _merge_bwd(dmerged, proj, conv_out, attn_out, "merge_bwd")
    dconv_y, dattn = _matmul_group((dco, dao), (wco, wao), mode="nt", tm=2048, tn=512, out_dtype=BF16, name="branch_out_bwd_x")
    g_wco, g_wao = _matmul_group((conv_y, attn), (dco, dao), mode="tn", tm=512, tn=1024, out_dtype=BF16, name="branch_out_bwd_w")
    after_mix = reducer.start("mix", dict(wco=g_wco, wao=g_wao, wo=g_wo))
    dproj, dconv_w = _conv_bwd(dconv_y, proj, conv_w, dproj, "conv_bwd", after=after_mix)
    after_mix = reducer.middle("mix", (dconv_w,))
    dproj, dkr, dv, dsinks = _swa_bwd(dattn, proj, tab, sinks, dproj, "attn_bwd", after=after_mix)
    dproj = _kv_bwd(dkr, dv, tab, dproj, "kv_bwd")
    g_win_t = _matmul(dproj, h1, mode="tn", out_dtype=BF16, name="proj_bwd_w", tm=512, tn=1024, tk=2048)
    after_in = reducer.middle("in", reducer.start("in", dict(win_t=g_win_t)))
    dh1 = _matmul(dproj, win_t, mode="nn", out_dtype=BF16, name="proj_bwd_x", tm=1024, tn=1024, tk=1664, after=after_in)
    dx, _, dg_mix = _rms_bwd(dh1, x, g_mix, dx1, "rms1_bwd")
    grads = dict(win_t=g_win_t, wgu_t=g_wgu_t, wd=g_wd, wco=g_wco, wao=g_wao, wo=g_wo)
    small = dict(g_mix=dg_mix, g_ffn=dg_ffn, g_final=dg_final, conv_w=dconv_w, sinks=dsinks, lossvec=lossvec)
    return dx, grads, small


def _position():
    return lax.axis_index("x"), lax.axis_index("y"), lax.axis_index("c")


def _other_chips(x, y):
    return [(1 - x, y), (x, 1 - y), (1 - x, 1 - y)]


SEM_SPEC = pl.BlockSpec(memory_space=pltpu.SEMAPHORE)
EFFECT = pltpu.SideEffectType.DATAFLOW_SIDE_EFFECTING
TOKEN = jax.ShapeDtypeStruct((8, 128), F32)
TOKEN_SPEC = pl.BlockSpec(memory_space=pltpu.VMEM)


def _hbm(a):
    return pltpu.with_memory_space_constraint(a, pltpu.HBM)


def _place(w, me_idx, dtype, name, after=()):
    r, cdim = w.shape

    def body(i_ref, w_ref, *rest):
        rest[-1][...] = w_ref[...].astype(dtype)

    grid_spec = pltpu.PrefetchScalarGridSpec(
        num_scalar_prefetch=1, grid=(1,), in_specs=[pl.BlockSpec((r, cdim), lambda i, me: (0, 0))] + [HBM_SPEC] * len(after),
        out_specs=pl.BlockSpec((r, cdim), lambda i, me: (me[0], 0)))
    return _call(body, name=name, grid_spec=grid_spec, out_shape=_sds((N_DEV * r, cdim), dtype),
                 compiler_params=_params("arbitrary"))(me_idx, w, *after)


def _own_rows(ref, r, px, py, pc):
    return ref.at[pl.ds((4 * px + 2 * py + pc) * r, r), :]


def _gather_phase(bufs, waits, plans, after, name):
    n = len(bufs)
    rows = [b.shape[0] // N_DEV for b in bufs]
    nw, npl = len(waits), len(plans)

    def body(*refs):
        ins = refs[:n]
        wait_sems = refs[n:n + 2 * nw]
        out0 = n + 2 * nw + len(after)
        new_sems = refs[out0:out0 + 2 * npl]
        token = refs[-1]
        x, y, c = _position()
        for w, (_, _, sent, received) in enumerate(waits):
            for a in range(n):
                for count, wait in ((sent, "wait_send"), (received, "wait_recv")):
                    span = _whole(ins[a], count * rows[a])
                    getattr(pltpu.make_async_remote_copy(
                        src_ref=span, dst_ref=span, send_sem=wait_sems[2 * w].at[a], recv_sem=wait_sems[2 * w + 1].at[a],
                        device_id=(x, y, c), device_id_type=MESH), wait)()
        for k, plan in enumerate(plans):
            for a in range(n):
                for block, target in plan(x, y, c):
                    span = _own_rows(ins[a], rows[a], *block)
                    pltpu.make_async_remote_copy(src_ref=span, dst_ref=span, send_sem=new_sems[2 * k].at[a],
                                                 recv_sem=new_sems[2 * k + 1].at[a], device_id=target, device_id_type=MESH).start()
        token[...] = jnp.zeros_like(token)

    sem_ops = [s for send, recv, _, _ in waits for s in (send, recv)]
    outs = _call(
        body, name=name, in_specs=[HBM_SPEC] * n + [SEM_SPEC] * (2 * nw) + [HBM_SPEC] * len(after),
        out_specs=[SEM_SPEC] * (2 * npl) + [HBM_SPEC] * n + [TOKEN_SPEC],
        out_shape=[pltpu.SemaphoreType.DMA((n,))] * (2 * npl) + [pltpu.HBM(b.shape, b.dtype) for b in bufs] + [TOKEN],
        input_output_aliases={i: 2 * npl + i for i in range(n)},
        compiler_params=pltpu.CompilerParams(has_side_effects=EFFECT),
    )(*[_hbm(b) for b in bufs], *sem_ops, *after)
    pairs = [(outs[2 * k], outs[2 * k + 1]) for k in range(npl)]
    return pairs, list(outs[2 * npl:2 * npl + n]), outs[-1]


def _own_to_near(x, y, c):
    return [((x, y, c), (x, y, 1 - c)), ((x, y, c), (1 - x, y, c)), ((x, y, c), (x, 1 - y, c))]


def _near_to_sibling(x, y, c):
    return [((1 - x, y, c), (x, y, 1 - c)), ((x, 1 - y, c), (x, y, 1 - c))]


def _relay_diagonal(x, y, c):
    north = c
    source = (x * north + (1 - x) * (1 - north), (1 - y) * north + y * (1 - north), c)
    target = ((1 - x) * north + x * (1 - north), y * north + (1 - y) * (1 - north), c)
    return [(source, target)]


def _diagonal_to_sibling(x, y, c):
    return [((1 - x, 1 - y, c), (x, y, 1 - c))]


def _gather_start(bufs, groups, name):
    n = len(bufs)
    rows = [b.shape[0] // N_DEV for b in bufs]
    ng = len(groups)

    def body(*refs):
        ins = refs[:n]
        sems = refs[n:n + 2 * ng]
        token = refs[-1]
        x, y, c = _position()
        targets = [(x, y, 1 - c)] + [(*chip, c) for chip in _other_chips(x, y)]
        for g, members in enumerate(groups):
            for slot, a in enumerate(members):
                own = _own_rows(ins[a], rows[a], x, y, c)
                for to in targets:
                    pltpu.make_async_remote_copy(src_ref=own, dst_ref=own, send_sem=sems[2 * g].at[slot],
                                                 recv_sem=sems[2 * g + 1].at[slot], device_id=to, device_id_type=MESH).start()
        token[...] = jnp.zeros_like(token)

    sem_shapes = []
    for members in groups:
        sem_shapes += [pltpu.SemaphoreType.DMA((len(members),))] * 2
    outs = _call(
        body, name=name, in_specs=[HBM_SPEC] * n, out_specs=[SEM_SPEC] * (2 * ng) + [HBM_SPEC] * n + [TOKEN_SPEC],
        out_shape=sem_shapes + [pltpu.HBM(b.shape, b.dtype) for b in bufs] + [TOKEN],
        input_output_aliases={i: 2 * ng + i for i in range(n)},
        compiler_params=pltpu.CompilerParams(has_side_effects=EFFECT),
    )(*[_hbm(b) for b in bufs])
    sem_pairs = [(outs[2 * g], outs[2 * g + 1]) for g in range(ng)]
    return sem_pairs, list(outs[2 * ng:2 * ng + n]), outs[-1]


def _gather_forward(send_sems, recv_sems, bufs, after, name):
    n = len(bufs)
    rows = [b.shape[0] // N_DEV for b in bufs]

    def body(*refs):
        ins = refs[:n]
        send1, recv1 = refs[n], refs[n + 1]
        out0 = n + 2 + len(after)
        send2, recv2 = refs[out0], refs[out0 + 1]
        token = refs[-1]
        x, y, c = _position()
        for a in range(n):
            step1 = pltpu.make_async_remote_copy(
                src_ref=_whole(ins[a], 4 * rows[a]), dst_ref=_whole(ins[a], 4 * rows[a]), send_sem=send1.at[a],
                recv_sem=recv1.at[a], device_id=(x, y, c), device_id_type=MESH)
            step1.wait_send()
            step1.wait_recv()
        for a in range(n):
            for chip in _other_chips(x, y):
                blk = _own_rows(ins[a], rows[a], *chip, c)
                pltpu.make_async_remote_copy(src_ref=blk, dst_ref=blk, send_sem=send2.at[a], recv_sem=recv2.at[a],
                                             device_id=(x, y, 1 - c), device_id_type=MESH).start()
        token[...] = jnp.zeros_like(token)

    outs = _call(
        body, name=name, in_specs=[HBM_SPEC] * n + [SEM_SPEC, SEM_SPEC] + [HBM_SPEC] * len(after),
        out_specs=[SEM_SPEC, SEM_SPEC] + [HBM_SPEC] * n + [TOKEN_SPEC],
        out_shape=[pltpu.SemaphoreType.DMA((n,)), pltpu.SemaphoreType.DMA((n,))]
        + [pltpu.HBM(b.shape, b.dtype) for b in bufs] + [TOKEN],
        input_output_aliases={i: 2 + i for i in range(n)},
        compiler_params=pltpu.CompilerParams(has_side_effects=EFFECT),
    )(*bufs, send_sems, recv_sems, *after)
    return outs[0], outs[1], list(outs[2:2 + n]), outs[-1]


def _gather_done(send_sems, recv_sems, bufs, after, name):
    n = len(bufs)
    rows = [b.shape[0] // N_DEV for b in bufs]

    def body(*refs):
        ins = refs[:n]
        send2, recv2 = refs[n], refs[n + 1]
        x, y, c = _position()
        for a in range(n):
            step2 = pltpu.make_async_remote_copy(
                src_ref=_whole(ins[a], 3 * rows[a]), dst_ref=_whole(ins[a], 3 * rows[a]), send_sem=send2.at[a],
                recv_sem=recv2.at[a], device_id=(x, y, c), device_id_type=MESH)
            step2.wait_send()
            step2.wait_recv()

    outs = _call(
        body, name=name, in_specs=[HBM_SPEC] * n + [SEM_SPEC, SEM_SPEC] + [HBM_SPEC] * len(after),
        out_specs=[HBM_SPEC] * n, out_shape=[pltpu.HBM(b.shape, b.dtype) for b in bufs],
        input_output_aliases={i: i for i in range(n)},
        compiler_params=pltpu.CompilerParams(has_side_effects=EFFECT),
    )(*bufs, send_sems, recv_sems, *after)
    return list(outs)


def _whole(ref, nrows):
    return ref.at[pl.ds(0, nrows), :]


def _to_sibling(x, y, c):
    return [(2 * q + (1 - c), q, (x, y, 1 - c)) for q in range(4)]


def _to_chips(x, y, c):
    return [(2 * px + py, j, (px, py, c)) for j, (px, py) in enumerate(_other_chips(x, y))]


def _exchange_start(srcs, src_slots, plan, name):
    n = len(srcs)
    rows = [a.shape[0] // src_slots for a in srcs]
    n_copies = len(plan(0, 0, 0))
    lands = [lax.empty((n_copies * r, a.shape[1]), a.dtype) for a, r in zip(srcs, rows)]

    def body(*refs):
        ins, land_refs = refs[:n], refs[n:2 * n]
        send_sems, recv_sems = refs[2 * n], refs[2 * n + 1]
        token = refs[-1]
        for a in range(n):
            r = rows[a]
            for src_slot, dst_slot, target in plan(*_position()):
                pltpu.make_async_remote_copy(
                    src_ref=ins[a].at[pl.ds(src_slot * r, r), :], dst_ref=land_refs[a].at[pl.ds(dst_slot * r, r), :],
                    send_sem=send_sems.at[a], recv_sem=recv_sems.at[a], device_id=target, device_id_type=MESH).start()
        token[...] = jnp.zeros_like(token)

    outs = _call(
        body, name=name, in_specs=[HBM_SPEC] * (2 * n),
        out_specs=[SEM_SPEC, SEM_SPEC] + [HBM_SPEC] * (2 * n) + [TOKEN_SPEC],
        out_shape=[pltpu.SemaphoreType.DMA((n,)), pltpu.SemaphoreType.DMA((n,))]
        + [pltpu.HBM(a.shape, a.dtype) for a in srcs] + [pltpu.HBM(l.shape, l.dtype) for l in lands] + [TOKEN],
        input_output_aliases={i: 2 + i for i in range(2 * n)},
        compiler_params=pltpu.CompilerParams(has_side_effects=EFFECT),
    )(*[_hbm(a) for a in srcs], *[_hbm(l) for l in lands])
    return outs[0], outs[1], list(outs[2:2 + n]), list(outs[2 + n:2 + 2 * n]), outs[-1]


def _exchange_wait(send_sems, recv_sems, srcs, lands, after, name):
    n = len(srcs)

    def body(*refs):
        ins, land_refs = refs[:n], refs[n:2 * n]
        send_sems_ref, recv_sems_ref = refs[2 * n], refs[2 * n + 1]
        for a in range(n):
            span = _whole(land_refs[a], lands[a].shape[0])
            cp = pltpu.make_async_remote_copy(
                src_ref=span, dst_ref=span, send_sem=send_sems_ref.at[a],
                recv_sem=recv_sems_ref.at[a], device_id=_position(), device_id_type=MESH)
            cp.wait_send()
            cp.wait_recv()

    outs = _call(
        body, name=name, in_specs=[HBM_SPEC] * (2 * n) + [SEM_SPEC, SEM_SPEC] + [HBM_SPEC] * len(after),
        out_specs=[HBM_SPEC] * (2 * n),
        out_shape=[pltpu.HBM(a.shape, a.dtype) for a in srcs] + [pltpu.HBM(l.shape, l.dtype) for l in lands],
        input_output_aliases={i: i for i in range(2 * n)},
        compiler_params=pltpu.CompilerParams(has_side_effects=EFFECT),
    )(*srcs, *lands, send_sems, recv_sems, *after)
    return list(outs[:n]), list(outs[n:])


def _chip_partial(grads, recvs, idx, name):
    n = len(grads)
    rows = [recv.shape[0] // 4 for recv in recvs]

    def body(i_ref, *refs):
        del i_ref
        for g_ref, s_ref, o_ref in zip(refs[:n], refs[n:2 * n], refs[2 * n:]):
            o_ref[...] = (g_ref[...].astype(F32) + s_ref[...].astype(F32)).astype(BF16)

    grid_spec = pltpu.PrefetchScalarGridSpec(
        num_scalar_prefetch=1, grid=(3,),
        in_specs=[pl.BlockSpec((r, D), lambda t, i_ref: (2 * i_ref[1 + t] + i_ref[0], 0)) for r in rows]
        + [pl.BlockSpec((r, D), lambda t, i_ref: (i_ref[1 + t], 0)) for r in rows],
        out_specs=[pl.BlockSpec((r, D), lambda t, i_ref: (i_ref[1 + t], 0)) for r in rows])
    return _call(body, name=name, grid_spec=grid_spec, out_shape=[_sds((4 * r, D), BF16) for r in rows],
                 compiler_params=_params("arbitrary"))(idx, *grads, *recvs)


def _adamw_math(w, g, m, v):
    m2 = B1 * m + (1.0 - B1) * g
    v2 = B2 * v + (1.0 - B2) * jnp.square(g)
    m_hat = m2 / (1.0 - B1 ** STEP)
    v_hat = v2 / (1.0 - B2 ** STEP)
    return -LR * (m_hat / (jnp.sqrt(v_hat) + EPS_ADAM) + WD * w), m2, v2


def _reduce_adamw(ws, grads, from_sibling, from_chips, idx, ms, vs, name):
    n = len(ws)
    nb = 2
    tiles = [w.shape[0] // nb for w in ws]
    for w, g, s, c in zip(ws, grads, from_sibling, from_chips):
        r = w.shape[0]
        assert g.shape == (N_DEV * r, D) and s.shape == (4 * r, D) and c.shape == (3 * r, D)

    def body(i_ref, *refs):
        del i_ref
        ins, outs = refs[:8 * n], refs[8 * n:]
        for a in range(n):
            w_ref, p_ref, s_ref, r0_ref, r1_ref, r2_ref, m_ref, v_ref = ins[8 * a:8 * a + 8]
            g_ref, d_ref, nm_ref, nv_ref = outs[4 * a:4 * a + 4]
            g = p_ref[...].astype(F32) + s_ref[...].astype(F32)
            g = ((g + r0_ref[...].astype(F32)) + r1_ref[...].astype(F32)) + r2_ref[...].astype(F32)
            g_ref[...] = g
            d_ref[...], nm_ref[...], nv_ref[...] = _adamw_math(w_ref[...], g, m_ref[...], v_ref[...])

    in_specs, out_specs, operands, out_shape = [], [], [], []
    for a, tr in enumerate(tiles):
        own = pl.BlockSpec((tr, D), lambda i, i_ref: (i, 0))
        in_specs += [own, pl.BlockSpec((tr, D), lambda i, i_ref: (i_ref[0] * nb + i, 0)),
                     pl.BlockSpec((tr, D), lambda i, i_ref: (i_ref[1] * nb + i, 0))]
        in_specs += [pl.BlockSpec((tr, D), lambda i, i_ref, j=j: (j * nb + i, 0)) for j in range(3)] + [own, own]
        operands += [ws[a], grads[a], from_sibling[a], from_chips[a], from_chips[a], from_chips[a], ms[a], vs[a]]
        out_specs += [own] * 4
        out_shape += [_sds(ws[a].shape, F32)] * 4
    grid_spec = pltpu.PrefetchScalarGridSpec(num_scalar_prefetch=1, grid=(nb,), in_specs=in_specs, out_specs=out_specs)
    outs = _call(body, name=name, grid_spec=grid_spec, out_shape=out_shape, compiler_params=_params("parallel"))(idx, *operands)
    return [tuple(outs[4 * a:4 * a + 4]) for a in range(n)]


SMALL_ROWS = 8


def _small_all_reduce(pack, name, after=()):
    def body(p_ref, *rest):
        tot_ref, loss_ref, gath, send_sems, recv_sems = rest[len(after):]
        x, y, c = _position()
        me_id = 4 * x + 2 * y + c
        gath[me_id] = p_ref[...]
        copies = []
        for k in range(1, N_DEV):
            peer = tuple(1 - v if (k >> b) & 1 else v for v, b in ((x, 2), (y, 1), (c, 0)))
            cp = pltpu.make_async_remote_copy(src_ref=p_ref, dst_ref=gath.at[me_id], send_sem=send_sems.at[k - 1],
                                              recv_sem=recv_sems.at[k - 1], device_id=peer, device_id_type=MESH)
            cp.start()
            copies.append(cp)
        for cp in copies:
            cp.wait_recv()
        for cp in copies:
            cp.wait_send()
        tot = gath[0]
        for d in range(1, N_DEV):
            tot = tot + gath[d]
        tot_ref[...] = tot
        loss_ref[...] = jnp.full((1, 128), (0.5 / D) * jnp.sum(tot[SMALL_ROWS - 1:SMALL_ROWS, :]), F32)

    vm = pl.BlockSpec(memory_space=pltpu.VMEM)
    return _call(
        body, name=name, in_specs=[vm] + [HBM_SPEC] * len(after), out_specs=[vm, vm],
        out_shape=[_sds((SMALL_ROWS, D), F32), _sds((1, 128), F32)],
        scratch_shapes=[pltpu.VMEM((N_DEV, SMALL_ROWS, D), F32), pltpu.SemaphoreType.DMA((N_DEV - 1,)),
                        pltpu.SemaphoreType.DMA((N_DEV - 1,))],
    )(pack, *after)


def _adamw_small(ws, gs, ms, vs, name):
    n = len(ws)

    def body(*refs):
        for a in range(n):
            w_ref, g_ref, m_ref, v_ref = (refs[k * n + a] for k in range(4))
            d_ref, nm_ref, nv_ref = (refs[(4 + k) * n + a] for k in range(3))
            d_ref[...], nm_ref[...], nv_ref[...] = _adamw_math(w_ref[...], g_ref[...], m_ref[...], v_ref[...])

    vm = pl.BlockSpec(memory_space=pltpu.VMEM)
    outs = _call(body, name=name, in_specs=[vm] * (4 * n), out_specs=[vm] * (3 * n),
                 out_shape=[_sds(w.shape, F32) for w in ws] * 3)(*ws, *gs, *ms, *vs)
    return [(outs[a], outs[n + a], outs[2 * n + a]) for a in range(n)]


def kernel(x, g_mix, w_in, conv_w, attn_sinks, w_conv_out, w_attn_out, w_o, g_ffn, w_gate_up, w_down, g_final, loss_target, m_g_mix, m_w_in, m_conv_w, m_attn_sinks, m_w_conv_out, m_w_attn_out, m_w_o, m_g_ffn, m_w_gate_up, m_w_down, m_g_final, v_g_mix, v_w_in, v_conv_w, v_attn_sinks, v_w_conv_out, v_w_attn_out, v_w_o, v_g_ffn, v_w_gate_up, v_w_down, v_g_final):
    cx, cy, cc = _position()
    chip = 2 * cx + cy
    partial_idx = jnp.stack([cc, 2 * (1 - cx) + cy, 2 * cx + (1 - cy), 2 * (1 - cx) + (1 - cy)]).astype(jnp.int32)
    own_idx = jnp.stack([2 * chip + cc, chip]).astype(jnp.int32)
    me = 4 * cx + 2 * cy + cc

    me_idx = jnp.reshape(me, (1,)).astype(jnp.int32)
    first = [_place(jnp.transpose(w_in[0]), me_idx, BF16, "place_w_in"),
             _place(jnp.pad(conv_w[0], ((0, 5), (0, 0))), me_idx, F32, "place_conv_w")]
    (to_near,), first, token_in = _gather_phase(first, [], [_own_to_near], (), "gather_in_start")
    gather_tokens = (token_in,)

    class Gathered:
        def __init__(self):
            self.state = {}

        def begin(self, group, after):
            if group == "in":
                (near, relay), bufs, token = _gather_phase(
                    first, [(*to_near, 3, 3)], [_near_to_sibling, _relay_diagonal], after, "gather_in_relay")
                later = [_place(w, me_idx, BF16, "place_" + k, after=(token,)) for k, w in (
                    ("w_conv_out", w_conv_out[0]), ("w_attn_out", w_attn_out[0]), ("w_o", w_o[0]),
                    ("w_gate_up", jnp.transpose(w_gate_up[0])), ("w_down", w_down[0]))]
                (sems_mix, sems_ffn), later, token_later = _gather_start(later, [[0, 1, 2], [3, 4]], "gather_start_later")
                self.state.update({"in": (near, relay, bufs), "mix": (sems_mix, later[:3]), "ffn": (sems_ffn, later[3:])})
                return (token_later,)
            (send_sems, recv_sems), group_bufs = self.state[group]
            send2, recv2, group_bufs, token = _gather_forward(send_sems, recv_sems, group_bufs, after, "gather_forward_" + group)
            self.state[group] = ((send2, recv2), group_bufs)
            return (token,)

        def end(self, group, after):
            if group == "in":
                near, relay, bufs = self.state[group]
                (last,), bufs, token = _gather_phase(bufs, [(*relay, 1, 1)], [_diagonal_to_sibling], after, "gather_in_last")
                _, full, _ = _gather_phase(bufs, [(*near, 2, 2), (*last, 1, 1)], [], (token,), "gather_in_done")
                return full[0], jnp.transpose(full[1].reshape(N_DEV, 8, 128)[:, :3, :], (1, 0, 2)).reshape(3, D)
            (send2, recv2), group_bufs = self.state[group]
            return _gather_done(send2, recv2, group_bufs, after, "gather_done_" + group)

    in_flight, own_pieces = {}, {}

    transposed = ("w_in", "w_gate_up")

    def as2d(k, a):
        if k in transposed:
            return jnp.transpose(a[0])
        return a[None] if a.ndim == 1 else (a[0] if a.ndim == 3 else a)

    w_all = {"g_mix": g_mix, "w_in": w_in, "conv_w": conv_w, "attn_sinks": attn_sinks, "w_conv_out": w_conv_out,
             "w_attn_out": w_attn_out, "w_o": w_o, "g_ffn": g_ffn, "w_gate_up": w_gate_up, "w_down": w_down, "g_final": g_final}
    m_all = {"g_mix": m_g_mix, "w_in": m_w_in, "conv_w": m_conv_w, "attn_sinks": m_attn_sinks, "w_conv_out": m_w_conv_out,
             "w_attn_out": m_w_attn_out, "w_o": m_w_o, "g_ffn": m_g_ffn, "w_gate_up": m_w_gate_up, "w_down": m_w_down,
             "g_final": m_g_final}
    v_all = {"g_mix": v_g_mix, "w_in": v_w_in, "conv_w": v_conv_w, "attn_sinks": v_attn_sinks, "w_conv_out": v_w_conv_out,
             "w_attn_out": v_w_attn_out, "w_o": v_w_o, "g_ffn": v_g_ffn, "w_gate_up": v_w_gate_up, "w_down": v_w_down,
             "g_final": v_g_final}
    results = {}

    def record(k, *vals):
        results[k] = [(jnp.transpose(val) if k in transposed else val).reshape(w_all[k].shape) for val in vals]

    def update(group, names, grads, from_sibling, from_chips):
        outs = _reduce_adamw([as2d(k, w_all[k]) for k in names], grads, from_sibling, from_chips, own_idx,
                             [as2d(k, m_all[k]) for k in names], [as2d(k, v_all[k]) for k in names], "adamw_" + group)
        for k, vals in zip(names, outs):
            record(k, *vals)
        return tuple(vals[2] for vals in outs)

    def update_small(grads):
        keys = list(grads)
        outs = _adamw_small([as2d(k, w_all[k]) for k in keys], [grads[k] for k in keys], [as2d(k, m_all[k]) for k in keys],
                            [as2d(k, v_all[k]) for k in keys], "adamw_small")
        for k, (d, nm, nv) in zip(keys, outs):
            record(k, grads[k], d, nm, nv)
        return tuple(nm for _, nm, _ in outs)

    kernel_name = {"win_t": "w_in", "wgu_t": "w_gate_up", "wd": "w_down", "wco": "w_conv_out", "wao": "w_attn_out", "wo": "w_o"}

    def finish(group, after):
        keys, send_sems, recv_sems, parts, from_chips = in_flight[group]
        _, from_chips = _exchange_wait(send_sems, recv_sems, parts, from_chips, after, "rs_chips_wait_" + group)
        grads, from_sibling = own_pieces[group]
        return update(group, [kernel_name[k] for k in keys], grads, from_sibling, from_chips)

    class Reducer:
        def start(self, group, gdict):
            keys, glist = list(gdict), list(gdict.values())
            send_sems, recv_sems, glist, lands, token = _exchange_start(glist, N_DEV, _to_sibling, "rs_sibling_start_" + group)
            in_flight[group] = (keys, send_sems, recv_sems, glist, lands)
            return (token,)

        def middle(self, group, after):
            keys, send_sems, recv_sems, glist, lands = in_flight[group]
            if group == "in":
                after = finish("ffn", after)
            glist, lands = _exchange_wait(send_sems, recv_sems, glist, lands, after, "rs_sibling_wait_" + group)
            parts = _chip_partial(glist, lands, partial_idx, "chip_partial_" + group)
            send_sems, recv_sems, parts, from_chips, token = _exchange_start(parts, 4, _to_chips, "rs_chips_start_" + group)
            in_flight[group] = (keys, send_sems, recv_sems, parts, from_chips)
            own_pieces[group] = (glist, lands)
            return (token,)

    dx, _, small = _local_step(x[0], loss_target[0], g_mix, g_ffn, g_final[None], attn_sinks, Gathered(),
                               reducer=Reducer(), after=gather_tokens)
    after = finish("mix", (dx,))

    sinks_row = jnp.pad(small["sinks"], ((0, 0), (0, D - 128)))
    pack = jnp.concatenate([small["g_mix"], small["g_ffn"], small["g_final"], small["conv_w"], sinks_row, small["lossvec"]], axis=0)
    tot, loss_row = _small_all_reduce(pack, "small_all_reduce", after=after)
    loss = loss_row[0, 0]
    g_small = {
        "g_mix": tot[0:1], "g_ffn": tot[1:2], "g_final": tot[2:3],
        "conv_w": lax.dynamic_slice(tot, (3, me * 128), (3, 128)), "attn_sinks": tot[6:7, :N_HEADS],
    }
    finish("in", update_small(g_small))

    order = ["g_mix", "w_in", "conv_w", "attn_sinks", "w_conv_out", "w_attn_out", "w_o", "g_ffn", "w_gate_up", "w_down", "g_final"]
    return (loss, dx[None], *[results[k][i] for i in range(4) for k in order])
```

```python
import functools
import math

import jax
import jax.numpy as jnp
from jax import lax
from jax.experimental import pallas as pl
from jax.experimental.pallas import tpu as pltpu

F32 = jnp.float32
BF16 = jnp.bfloat16

D = 1024
HEAD_DIM = 64
N_HEADS = 16
N_KV = 4
GROUP = N_HEADS // N_KV
D_KV = N_KV * HEAD_DIM
BLOCK = 128
ROT_DIM = HEAD_DIM // 4
ROPE_THETA = 500000.0
ATTN_SCALE = 1.0 / math.sqrt(HEAD_DIM)
NEG_INF = -1e30
D_FF = 2816
N_IN = 6656
EPS = 1e-5
C_CB, C_CC, C_CX, C_Q, C_K, C_V, C_GC, C_GA = 0, 1024, 2048, 3072, 4096, 4352, 4608, 5632

LR, B1, B2, EPS_ADAM, WD, STEP = 0.001, 0.9, 0.999, 1e-08, 0.01, 10

N_DEV = 8
MESH = pl.DeviceIdType.MESH
VMEM_LIMIT = 56 * 1024 * 1024

NN = (((1,), (0,)), ((), ()))
NT = (((1,), (1,)), ((), ()))
TN = (((0,), (0,)), ((), ()))
HBM_SPEC = pl.BlockSpec(memory_space=pl.ANY)
ROW_SPLIT = 4


def _call(body, **kw):
    return pl.pallas_call(body, **kw)


def _params(*sem):
    return pltpu.CompilerParams(dimension_semantics=sem, vmem_limit_bytes=VMEM_LIMIT)


def _sds(shape, dtype):
    return jax.ShapeDtypeStruct(shape, dtype)


def _matmul(a, b, *, mode, tm, tn, tk, out_dtype, name, res=None, after=()):
    parts = list(a) if isinstance(a, (list, tuple)) else [a]
    rows_a = parts[0].shape[0]
    cols_a = sum(p.shape[1] for p in parts)
    if mode == "nn":
        (m, kk), (_, n), dims = (rows_a, cols_a), b.shape, NN
    elif mode == "nt":
        (m, kk), (n, _), dims = (rows_a, cols_a), b.shape, NT
    else:
        (kk, m), (_, n), dims = (rows_a, cols_a), b.shape, TN
    tm, tn, tk = min(tm, m), min(tn, n), min(tk, kk)
    assert m % tm == 0 and n % tn == 0 and kk % tk == 0, (name, m, n, kk, tm, tn, tk)
    nk = kk // tk
    split_axis, width = (2, tk) if mode == "nn" else (0, tm)
    assert len(parts) == 1 or mode in ("nn", "tn")
    assert len(parts) == 1 or all(p.shape[1] % width == 0 for p in parts), (name, width)
    counts = [p.shape[1] // width for p in parts]
    starts = [sum(counts[:p]) for p in range(len(parts))]

    def a_spec(p):
        def col(t):
            return jnp.clip(t - starts[p], 0, counts[p] - 1) if len(parts) > 1 else t

        if mode == "tn":
            return pl.BlockSpec((tk, tm), lambda i, j, k: (k, col(i)))
        return pl.BlockSpec((tm, tk), lambda i, j, k: (i, col(k)))

    if mode == "nt":
        b_spec = pl.BlockSpec((tn, tk), lambda i, j, k: (j, k))
    else:
        b_spec = pl.BlockSpec((tk, tn), lambda i, j, k: (k, j))
    o_spec = pl.BlockSpec((tm, tn), lambda i, j, k: (i, j))
    has_res = res is not None
    n_parts = len(parts)
    unit = 128 if mode == "tn" else 16
    split = ROW_SPLIT if tm % (ROW_SPLIT * unit) == 0 else 1

    def body(*refs):
        a_refs, b_ref = refs[:n_parts], refs[n_parts]
        r_ref = refs[n_parts + 1] if has_res else None
        o_ref = refs[n_parts + 1 + has_res + len(after)]
        k = pl.program_id(2)

        acc_ref = refs[-1] if nk > 1 else None

        def step(a_ref):
            def matmul(rows):
                a_blk = a_ref[:, rows] if mode == "tn" else a_ref[rows, :]
                return lax.dot_general(a_blk, b_ref[...], dims, preferred_element_type=F32)

            def finish(rows, part):
                if nk > 1:
                    acc_ref[rows, :] += part
                else:
                    o_ref[rows, :] = (part + r_ref[rows, :] if has_res else part).astype(o_ref.dtype)

            _row_pipeline(tm, matmul, finish, split)

        if nk > 1:
            @pl.when(k == 0)
            def _():
                acc_ref[...] = jnp.zeros_like(acc_ref)

        if n_parts == 1:
            step(a_refs[0])
        else:
            t = pl.program_id(split_axis)
            for p in range(n_parts):
                pl.when((t >= starts[p]) & (t < starts[p] + counts[p]))(functools.partial(step, a_refs[p]))

        if nk > 1:
            @pl.when(k == nk - 1)
            def _():
                o_ref[...] = (acc_ref[...] + r_ref[...] if has_res else acc_ref[...]).astype(o_ref.dtype)

    ins = parts + [b] + ([res] if has_res else []) + list(after)
    in_specs = [a_spec(p) for p in range(n_parts)] + [b_spec] + ([o_spec] if has_res else []) + [HBM_SPEC] * len(after)
    scratch = [] if nk == 1 else [pltpu.VMEM((tm, tn), F32)]
    return _call(
        body, name=name, grid=(m // tm, n // tn, nk), in_specs=in_specs, out_specs=o_spec,
        out_shape=_sds((m, n), out_dtype), scratch_shapes=scratch,
        compiler_params=_params("parallel", "parallel", "arbitrary"),
    )(*ins)


def _matmul_group(a_group, b_group, *, mode, tm, tn, out_dtype, name, after=()):
    a0, b0 = a_group[0], b_group[0]
    a_pair, b_pair, count = a_group, b_group, len(a_group)
    if mode == "nn":
        (m, kk), (_, n), dims = a0.shape, b0.shape, NN
    elif mode == "nt":
        (m, kk), (n, _), dims = a0.shape, b0.shape, NT
    else:
        (kk, m), (_, n), dims = a0.shape, b0.shape, TN
    assert all(a.shape == a0.shape for a in a_pair) and all(b.shape == b0.shape for b in b_pair)
    tm, tn = min(tm, m), min(tn, n)
    assert m % tm == 0 and n % tn == 0, (name, m, n, tm, tn)
    a_spec = pl.BlockSpec((kk, tm), lambda i, j: (0, i)) if mode == "tn" else pl.BlockSpec((tm, kk), lambda i, j: (i, 0))
    b_spec = pl.BlockSpec((tn, kk), lambda i, j: (j, 0)) if mode == "nt" else pl.BlockSpec((kk, tn), lambda i, j: (0, j))
    o_spec = pl.BlockSpec((tm, tn), lambda i, j: (i, j))
    unit = 128 if mode == "tn" else 16
    split = ROW_SPLIT if tm % (ROW_SPLIT * unit) == 0 else 1

    def body(*refs):
        a_refs, b_refs, o_refs = refs[:count], refs[count:2 * count], refs[2 * count + len(after):]

        def matmul(rows):
            return tuple(lax.dot_general(a_ref[:, rows] if mode == "tn" else a_ref[rows, :], b_ref[...], dims,
                                         preferred_element_type=F32) for a_ref, b_ref in zip(a_refs, b_refs))

        def finish(rows, parts):
            for o_ref, part in zip(o_refs, parts):
                o_ref[rows, :] = part.astype(out_dtype)

        _row_pipeline(tm, matmul, finish, split)

    return _call(
        body, name=name, grid=(m // tm, n // tn), in_specs=[a_spec] * count + [b_spec] * count + [HBM_SPEC] * len(after),
        out_specs=[o_spec] * count, out_shape=[_sds((m, n), out_dtype)] * count,
        compiler_params=_params("parallel", "parallel"),
    )(*a_pair, *b_pair, *after)


def _row_tile(s):
    return min(512, s)


def _rms_fwd(x, g, name, after=()):
    s = x.shape[0]
    tm = _row_tile(s)

    def body(x_ref, g_ref, *rest):
        h_ref = rest[-1]
        xv = x_ref[...]
        r = lax.rsqrt(jnp.mean(xv * xv, axis=-1, keepdims=True) + EPS)
        h_ref[...] = (xv * r * g_ref[...]).astype(BF16)

    row = pl.BlockSpec((tm, D), lambda i: (i, 0))
    return _call(
        body, name=name, grid=(s // tm,), in_specs=[row, pl.BlockSpec((1, D), lambda i: (0, 0))] + [HBM_SPEC] * len(after),
        out_specs=row, out_shape=_sds((s, D), BF16), compiler_params=_params("parallel"),
    )(x, g, *after)


def _rms_bwd(dh, x, g, dres, name, after=()):
    s = x.shape[0]
    tm = _row_tile(s)

    def body(dh_ref, x_ref, g_ref, dres_ref, *rest):
        dx_ref, dxb_ref, dg_ref = rest[len(after):]
        xv = x_ref[...]
        r = lax.rsqrt(jnp.mean(xv * xv, axis=-1, keepdims=True) + EPS)
        xh = xv * r
        dhv = dh_ref[...].astype(F32)
        dyg = dhv * g_ref[...]
        dx = dres_ref[...] + r * (dyg - xh * jnp.mean(dyg * xh, axis=-1, keepdims=True))
        dx_ref[...] = dx
        dxb_ref[...] = dx.astype(BF16)
        part = jnp.sum(dhv * xh, axis=0, keepdims=True)

        @pl.when(pl.program_id(0) == 0)
        def _():
            dg_ref[...] = part

        @pl.when(pl.program_id(0) > 0)
        def _():
            dg_ref[...] += part

    row = pl.BlockSpec((tm, D), lambda i: (i, 0))
    vec = pl.BlockSpec((1, D), lambda i: (0, 0))
    return _call(
        body, name=name, grid=(s // tm,), in_specs=[row, row, vec, row] + [HBM_SPEC] * len(after), out_specs=[row, row, vec],
        out_shape=[_sds((s, D), F32), _sds((s, D), BF16), _sds((1, D), F32)],
        compiler_params=_params("arbitrary"),
    )(dh, x, g, dres, *after)


def _loss_head(x2, g, tgt, name):
    s = x2.shape[0]
    tm = _row_tile(s)

    def body(x_ref, g_ref, t_ref, dx_ref, dxb_ref, dg_ref, l_ref):
        xv = x_ref[...]
        gv = g_ref[...]
        r = lax.rsqrt(jnp.mean(xv * xv, axis=-1, keepdims=True) + EPS)
        xh = xv * r
        err = xh * gv - t_ref[...]
        dy = err * (1.0 / D)
        dyg = dy * gv
        dx = r * (dyg - xh * jnp.mean(dyg * xh, axis=-1, keepdims=True))
        dx_ref[...] = dx
        dxb_ref[...] = dx.astype(BF16)
        dg_part = jnp.sum(dy * xh, axis=0, keepdims=True)
        l_part = jnp.sum(err * err, axis=0, keepdims=True)

        @pl.when(pl.program_id(0) == 0)
        def _():
            dg_ref[...] = dg_part
            l_ref[...] = l_part

        @pl.when(pl.program_id(0) > 0)
        def _():
            dg_ref[...] += dg_part
            l_ref[...] += l_part

    row = pl.BlockSpec((tm, D), lambda i: (i, 0))
    vec = pl.BlockSpec((1, D), lambda i: (0, 0))
    return _call(
        body, name=name, grid=(s // tm,), in_specs=[row, vec, row], out_specs=[row, row, vec, vec],
        out_shape=[_sds((s, D), F32), _sds((s, D), BF16), _sds((1, D), F32), _sds((1, D), F32)],
        compiler_params=_params("arbitrary"),
    )(x2, g, tgt)


CONV_TC = 256


def _shift_down(u, k, rows):
    return jnp.where(rows >= k, pltpu.roll(u, k, 0), 0.0)


def _shift_up(u, k, rows, s):
    return jnp.where(rows < s - k, pltpu.roll(u, s - k, 0), 0.0)


def _conv_specs(s):
    nb = D // CONV_TC

    def col(c0):
        return pl.BlockSpec((s, CONV_TC), lambda j, c0=c0: (0, c0 // CONV_TC + j))

    return nb, col


def _conv_fwd(proj, conv_w, name):
    s = proj.shape[0]
    nb, col = _conv_specs(s)

    def body(cb_ref, cc_ref, cx_ref, w_ref, y_ref):
        rows = lax.broadcasted_iota(jnp.int32, (s, CONV_TC), 0)
        u = cc_ref[...].astype(F32) * cx_ref[...].astype(F32)
        w = w_ref[...]
        c = w[0:1] * _shift_down(u, 2, rows) + w[1:2] * _shift_down(u, 1, rows) + w[2:3] * u
        y_ref[...] = (cb_ref[...].astype(F32) * c).astype(BF16)

    return _call(
        body, name=name, grid=(nb,),
        in_specs=[col(C_CB), col(C_CC), col(C_CX), pl.BlockSpec((3, CONV_TC), lambda j: (0, j))],
        out_specs=pl.BlockSpec((s, CONV_TC), lambda j: (0, j)), out_shape=_sds((s, D), BF16),
        compiler_params=_params("parallel"),
    )(proj, proj, proj, conv_w)


def _write_behind(t, nt, buf, sems, tiles, window, where):
    slot = t % 2

    def copies(sl, at):
        return [pltpu.make_async_copy(buf.at[sl, p], window(p, at), sems.at[sl, p]) for p in range(len(tiles))]

    @pl.when(t >= 2)
    def _():
        for cp in copies(slot, where):
            cp.wait()

    for p, tile in enumerate(tiles):
        buf[slot, p] = tile
    started = copies(slot, where)
    for cp in started:
        cp.start()

    @pl.when(t == nt - 1)
    def _():
        for cp in started:
            cp.wait()
        if nt > 1:
            for cp in copies(1 - slot, where):
                cp.wait()


def _conv_bwd(dy, proj, conv_w, dproj, name, after=()):
    s = proj.shape[0]
    nb, col = _conv_specs(s)

    def body(dy_ref, cb_ref, cc_ref, cx_ref, w_ref, *rest):
        dproj_ref, dw_ref, buf, sems = rest[1 + len(after):]
        j = pl.program_id(0)
        rows = lax.broadcasted_iota(jnp.int32, (s, CONV_TC), 0)
        cc = cc_ref[...].astype(F32)
        cx = cx_ref[...].astype(F32)
        u = cc * cx
        u1 = _shift_down(u, 1, rows)
        u2 = _shift_down(u, 2, rows)
        w = w_ref[...]
        c = w[0:1] * u2 + w[1:2] * u1 + w[2:3] * u
        dyv = dy_ref[...].astype(F32)
        dc = dyv * cb_ref[...].astype(F32)
        du = w[2:3] * dc + w[1:2] * _shift_up(dc, 1, rows, s) + w[0:1] * _shift_up(dc, 2, rows, s)

        def window(p, jj):
            start = pl.multiple_of((C_CB, C_CC, C_CX)[p] + jj * CONV_TC, CONV_TC)
            return dproj_ref.at[:, pl.ds(start, CONV_TC)]

        tiles = ((dyv * c).astype(BF16), (du * cx).astype(BF16), (du * cc).astype(BF16))
        _write_behind(j * 0, 1, buf, sems, tiles, window, j)
        dw_ref[...] = jnp.concatenate(
            [jnp.sum(dc * u2, axis=0, keepdims=True), jnp.sum(dc * u1, axis=0, keepdims=True),
             jnp.sum(dc * u, axis=0, keepdims=True)], axis=0)

    return _call(
        body, name=name, grid=(nb,),
        in_specs=[pl.BlockSpec((s, CONV_TC), lambda j: (0, j)), col(C_CB), col(C_CC), col(C_CX),
                  pl.BlockSpec((3, CONV_TC), lambda j: (0, j))] + [HBM_SPEC] * (1 + len(after)),
        out_specs=[pl.BlockSpec(memory_space=pl.ANY), pl.BlockSpec((3, CONV_TC), lambda j: (0, j))],
        out_shape=[_sds((s, N_IN), BF16), _sds((3, D), F32)],
        scratch_shapes=[pltpu.VMEM((1, 3, s, CONV_TC), BF16), pltpu.SemaphoreType.DMA((1, 3))],
        input_output_aliases={5: 0}, compiler_params=_params("arbitrary"),
    )(dy, proj, proj, proj, conv_w, dproj, *after)


def _rope_tables(s):
    half = ROT_DIM // 2
    inv_freq = ROPE_THETA ** (-jnp.arange(0, ROT_DIM, 2, dtype=F32) / ROT_DIM)
    inv64 = jnp.concatenate([inv_freq, inv_freq, jnp.zeros((HEAD_DIM - ROT_DIM,), F32)])
    ang = jnp.arange(s, dtype=F32)[:, None] * jnp.concatenate([inv64, inv64])[None, :]
    d = lax.broadcasted_iota(jnp.int32, (s, 128), 1) % HEAD_DIM
    cos, sin = jnp.cos(ang), jnp.sin(ang)
    c = jnp.where(d < ROT_DIM, cos, 1.0)
    a = jnp.where(d < half, -sin, 0.0)
    b = jnp.where((d >= half) & (d < ROT_DIM), sin, 0.0)
    return jnp.concatenate([c, a, b], axis=1)


def _rope(x, tab):
    c, a, b = tab[:, 0:128], tab[:, 128:256], tab[:, 256:384]
    outs = []
    for i in range(x.shape[1] // 128):
        xc = x[:, i * 128:(i + 1) * 128]
        outs.append(xc * c + pltpu.roll(xc, 120, 1) * a + pltpu.roll(xc, 8, 1) * b)
    return outs[0] if len(outs) == 1 else jnp.concatenate(outs, axis=1)


def _rope_t(dx, tab):
    c, a, b = tab[:, 0:128], tab[:, 128:256], tab[:, 256:384]
    outs = []
    for i in range(dx.shape[1] // 128):
        dc = dx[:, i * 128:(i + 1) * 128]
        outs.append(dc * c + pltpu.roll(dc * a, 8, 1) + pltpu.roll(dc * b, 120, 1))
    return outs[0] if len(outs) == 1 else jnp.concatenate(outs, axis=1)


def _attn_in_specs():
    prev = lambda n: jnp.maximum(n - 1, 0)
    return [
        pl.BlockSpec((BLOCK, D), lambda n: (n, C_Q // D)),
        pl.BlockSpec((BLOCK, D_KV), lambda n: (n, C_K // D_KV)),
        pl.BlockSpec((BLOCK, D_KV), lambda n: (prev(n), C_K // D_KV)),
        pl.BlockSpec((BLOCK, D_KV), lambda n: (n, C_V // D_KV)),
        pl.BlockSpec((BLOCK, D_KV), lambda n: (prev(n), C_V // D_KV)),
        pl.BlockSpec((BLOCK, 384), lambda n: (n, 0)),
        pl.BlockSpec((BLOCK, 384), lambda n: (prev(n), 0)),
        pl.BlockSpec(memory_space=pltpu.SMEM),
    ]


HALF = HEAD_DIM
N_CHUNK = D // 128


def _swa_bias(n):
    qi = lax.broadcasted_iota(jnp.int32, (BLOCK, 2 * BLOCK), 0)
    kj = lax.broadcasted_iota(jnp.int32, (BLOCK, 2 * BLOCK), 1)
    rel = qi + BLOCK - kj
    valid = (rel >= 0) & (rel < BLOCK) & ((kj >= BLOCK) | (n > 0))
    return jnp.where(valid, 0.0, NEG_INF)


def _halves(x):
    lo = lax.broadcasted_iota(jnp.int32, x.shape, 1) < HALF
    return jnp.where(lo, x, 0.0).astype(BF16), jnp.where(lo, 0.0, x).astype(BF16)


def _dup_heads(x):
    out = []
    for pair in range(N_KV // 2):
        xc = x[:, pair * 128:(pair + 1) * 128]
        xr = pltpu.roll(xc, HALF, 1)
        lo = lax.broadcasted_iota(jnp.int32, xc.shape, 1) < HALF
        out += [jnp.where(lo, xc, xr), jnp.where(lo, xr, xc)]
    return out


def _swa_load(q_ref, kc_ref, kp_ref, vc_ref, vp_ref, tc_ref, tp_ref):
    qf = _rope(q_ref[...].astype(F32), tc_ref[...]) * ATTN_SCALE
    q_halves = [_halves(qf[:, c * 128:(c + 1) * 128]) for c in range(N_CHUNK)]
    kf = jnp.concatenate([_rope(kp_ref[...].astype(F32), tp_ref[...]), _rope(kc_ref[...].astype(F32), tc_ref[...])], axis=0)
    vf = jnp.concatenate([vp_ref[...], vc_ref[...]], axis=0).astype(F32)
    return q_halves, _dup_heads(kf), _dup_heads(vf)


def _swa_probs(qh, kk, bias, sink):
    s = lax.dot_general(qh, kk, NT, preferred_element_type=F32) + bias
    m = jnp.maximum(jnp.max(jnp.maximum(s[:, :BLOCK], s[:, BLOCK:]), axis=1, keepdims=True), sink)
    return jnp.exp(s - m), m


def _swa_fwd(proj, tab, sinks, name, after=()):
    s = proj.shape[0]

    def body(q_ref, kc_ref, kp_ref, vc_ref, vp_ref, tc_ref, tp_ref, sink_ref, *rest):
        o_ref = rest[-1]
        n = pl.program_id(0)
        q_halves, kdup, vdup = _swa_load(q_ref, kc_ref, kp_ref, vc_ref, vp_ref, tc_ref, tp_ref)
        bias = _swa_bias(n)
        ones = jnp.ones((2 * BLOCK, 128), BF16)
        kk = [k.astype(BF16) for k in kdup]
        vv = [[jnp.concatenate([v_half, ones], axis=1) for v_half in _halves(v)] for v in vdup]
        heads = [(c, half) for c in range(N_CHUNK) for half in range(2)]
        scores = [lax.dot_general(q_halves[c][half], kk[c // (GROUP // 2)], NT, preferred_element_type=F32)
                  for c, half in heads]
        probs = []
        for (c, half), sc in zip(heads, scores):
            sc = sc + bias
            m = jnp.maximum(jnp.max(jnp.maximum(sc[:, :BLOCK], sc[:, BLOCK:]), axis=1, keepdims=True), sink_ref[0, 2 * c + half])
            probs.append((jnp.exp(sc - m).astype(BF16), jnp.exp(sink_ref[0, 2 * c + half] - m)))
        outs = [lax.dot_general(e, vv[c // (GROUP // 2)][half], NN, preferred_element_type=F32)
                for (c, half), (e, _) in zip(heads, probs)]
        for c in range(N_CHUNK):
            parts = [outs[2 * c + half][:, :128] * (1.0 / (outs[2 * c + half][:, 128:] + probs[2 * c + half][1]))
                     for half in range(2)]
            o_ref[:, c * 128:(c + 1) * 128] = (parts[0] + parts[1]).astype(BF16)

    return _call(
        body, name=name, grid=(s // BLOCK,), in_specs=_attn_in_specs() + [HBM_SPEC] * len(after),
        out_specs=pl.BlockSpec((BLOCK, D), lambda n: (n, 0)), out_shape=_sds((s, D), BF16),
        compiler_params=_params("parallel"),
    )(proj, proj, proj, proj, proj, tab, tab, sinks, *after)


def _swa_bwd(do, proj, tab, sinks, dproj, name, after=()):
    s = proj.shape[0]
    nblk = s // BLOCK
    kv_of = lambda c: c // (GROUP // 2)

    def body(do_ref, q_ref, kc_ref, kp_ref, vc_ref, vp_ref, tc_ref, tp_ref, sink_ref, *rest):
        dproj_ref, dk_ref, dv_ref, ds_ref, dqout, dkbuf, dvbuf, sems = rest[1 + len(after):]
        n = pl.program_id(0)

        @pl.when(n == 0)
        def _():
            dk_ref[...] = jnp.zeros_like(dk_ref)
            dv_ref[...] = jnp.zeros_like(dv_ref)
            ds_ref[...] = jnp.zeros_like(ds_ref)

        q_halves, kdup, vdup = _swa_load(q_ref, kc_ref, kp_ref, vc_ref, vp_ref, tc_ref, tp_ref)
        dof = do_ref[...].astype(F32)
        do_halves = [_halves(dof[:, c * 128:(c + 1) * 128]) for c in range(N_CHUNK)]
        bias = _swa_bias(n)
        ones = jnp.ones((2 * BLOCK, 128), BF16)
        kk = [k.astype(BF16) for k in kdup]
        vv = [v.astype(BF16) for v in vdup]
        k_halves = [_halves(k) for k in kdup]
        heads = [(c, half) for c in range(N_CHUNK) for half in range(2)]
        lane_row = lax.broadcasted_iota(jnp.int32, (1, 128), 1)
        lo_kv = lax.broadcasted_iota(jnp.int32, (2 * BLOCK, 128), 1) < HALF
        scores = [lax.dot_general(q_halves[c][half], kk[kv_of(c)], NT, preferred_element_type=F32) for c, half in heads]
        dps = [lax.dot_general(do_halves[c][half], vv[kv_of(c)], NT, preferred_element_type=F32) for c, half in heads]
        exps = []
        for (c, half), sc in zip(heads, scores):
            sink = sink_ref[0, 2 * c + half]
            sc = sc + bias
            m = jnp.maximum(jnp.max(jnp.maximum(sc[:, :BLOCK], sc[:, BLOCK:]), axis=1, keepdims=True), sink)
            exps.append((jnp.exp(sc - m), jnp.exp(sink - m)))
        sums = [lax.dot_general(e.astype(BF16), ones, NN, preferred_element_type=F32) for e, _ in exps]
        dsink_row = jnp.zeros((1, 128), F32)
        dsb, pb = [], []
        for h, ((e, es), row_sum, dp) in enumerate(zip(exps, sums, dps)):
            inv = 1.0 / (row_sum + es)
            p = e * jnp.concatenate([inv, inv], axis=1)
            t = p * dp
            delta = jnp.sum(t, axis=1, keepdims=True)
            dsb.append((t - p * delta).astype(BF16))
            pb.append(p.astype(BF16))
            dsink = -jnp.sum(es * inv * delta, axis=0, keepdims=True)
            dsink_row = dsink_row + jnp.where(lane_row == h, dsink, 0.0)
        dq_parts = [lax.dot_general(d, k_halves[kv_of(c)][half], NN, preferred_element_type=F32) for (c, half), d in zip(heads, dsb)]
        dk_parts = [lax.dot_general(d, q_halves[c][half], TN, preferred_element_type=F32) for (c, half), d in zip(heads, dsb)]
        dv_parts = [lax.dot_general(p, do_halves[c][half], TN, preferred_element_type=F32) for (c, half), p in zip(heads, pb)]
        dq = jnp.concatenate([(dq_parts[2 * c] + dq_parts[2 * c + 1]) * ATTN_SCALE for c in range(N_CHUNK)], axis=1)

        def kv_sum(parts, hk):
            acc = (parts[GROUP * hk] + parts[GROUP * hk + 1]) + (parts[GROUP * hk + 2] + parts[GROUP * hk + 3])
            return acc + pltpu.roll(acc, HALF, 1)

        for pair in range(N_KV // 2):
            dkbuf[:, pair * 128:(pair + 1) * 128] = jnp.where(lo_kv, kv_sum(dk_parts, 2 * pair), kv_sum(dk_parts, 2 * pair + 1))
            dvbuf[:, pair * 128:(pair + 1) * 128] = jnp.where(lo_kv, kv_sum(dv_parts, 2 * pair), kv_sum(dv_parts, 2 * pair + 1))
        prev0 = pl.multiple_of(jnp.maximum(n - 1, 0) * BLOCK, BLOCK)
        cur0 = pl.multiple_of(n * BLOCK, BLOCK)

        @pl.when(n > 0)
        def _():
            dk_ref[pl.ds(prev0, BLOCK), :] += dkbuf[0:BLOCK, :]
            dv_ref[pl.ds(prev0, BLOCK), :] += dvbuf[0:BLOCK, :]

        dk_ref[pl.ds(cur0, BLOCK), :] += dkbuf[BLOCK:2 * BLOCK, :]
        dv_ref[pl.ds(cur0, BLOCK), :] += dvbuf[BLOCK:2 * BLOCK, :]
        ds_ref[...] += dsink_row

        def window(p, at):
            return dproj_ref.at[pl.ds(pl.multiple_of(at * BLOCK, BLOCK), BLOCK), pl.ds(C_Q, D)]

        _write_behind(n, nblk, dqout, sems, (_rope_t(dq, tc_ref[...]).astype(BF16),), window, n)

    blk = lambda w: pl.BlockSpec((BLOCK, w), lambda n: (n, 0))
    whole = lambda w: pl.BlockSpec((s, w), lambda n: (0, 0))
    n_in = 1 + len(_attn_in_specs())
    return _call(
        body, name=name, grid=(nblk,), in_specs=[blk(D)] + _attn_in_specs() + [HBM_SPEC] * (1 + len(after)),
        out_specs=[HBM_SPEC, whole(D_KV), whole(D_KV), pl.BlockSpec((1, 128), lambda n: (0, 0))],
        out_shape=[_sds((s, N_IN), BF16), _sds((s, D_KV), F32), _sds((s, D_KV), F32), _sds((1, 128), F32)],
        scratch_shapes=[pltpu.VMEM((2, 1, BLOCK, D), BF16), pltpu.VMEM((2 * BLOCK, D_KV), F32),
                        pltpu.VMEM((2 * BLOCK, D_KV), F32), pltpu.SemaphoreType.DMA((2, 1))],
        input_output_aliases={n_in: 0}, compiler_params=_params("arbitrary"),
    )(do, proj, proj, proj, proj, proj, tab, tab, sinks, dproj, *after)


def _kv_bwd(dkr, dv, tab, dproj, name):
    s = dkr.shape[0]
    tm = _row_tile(s)

    def body(dk_ref, dv_ref, t_ref, dproj_in, o_ref):
        del dproj_in
        o_ref[:, 0:D_KV] = _rope_t(dk_ref[...], t_ref[...]).astype(BF16)
        o_ref[:, D_KV:2 * D_KV] = dv_ref[...].astype(BF16)

    row = lambda w: pl.BlockSpec((tm, w), lambda i: (i, 0))
    return _call(
        body, name=name, grid=(s // tm,),
        in_specs=[row(D_KV), row(D_KV), row(384), pl.BlockSpec(memory_space=pl.ANY)],
        out_specs=pl.BlockSpec((tm, 2 * D_KV), lambda i: (i, C_K // (2 * D_KV))),
        out_shape=_sds((s, N_IN), BF16), input_output_aliases={3: 0}, compiler_params=_params("parallel"),
    )(dkr, dv, tab, dproj)


EW_TC = 512


def _sigmoid(x):
    return 0.5 * jnp.tanh(0.5 * x) + 0.5


def _merge_fwd(proj, conv_out, attn_out, name):
    s = proj.shape[0]
    tm = _row_tile(s)
    tile = pl.BlockSpec((tm, EW_TC), lambda i, j: (i, j))

    def body(gc_ref, ga_ref, co_ref, ao_ref, o_ref):
        o_ref[...] = (_sigmoid(gc_ref[...].astype(F32)) * co_ref[...].astype(F32)
                      + _sigmoid(ga_ref[...].astype(F32)) * ao_ref[...].astype(F32)).astype(BF16)

    return _call(
        body, name=name, grid=(s // tm, D // EW_TC),
        in_specs=[pl.BlockSpec((tm, EW_TC), lambda i, j: (i, C_GC // EW_TC + j)),
                  pl.BlockSpec((tm, EW_TC), lambda i, j: (i, C_GA // EW_TC + j)), tile, tile],
        out_specs=tile, out_shape=_sds((s, D), BF16), compiler_params=_params("parallel", "parallel"),
    )(proj, proj, conv_out, attn_out)


def _merge_bwd(dmerged, proj, conv_out, attn_out, name):
    s = proj.shape[0]
    tm = _row_tile(s)
    tile = pl.BlockSpec((tm, EW_TC), lambda i, j: (i, j))
    anyspec = pl.BlockSpec(memory_space=pl.ANY)

    def body(dm_ref, gc_ref, ga_ref, co_ref, ao_ref, dproj_ref, dco_ref, dao_ref, buf, sems):
        i, j = pl.program_id(0), pl.program_id(1)
        dm = dm_ref[...].astype(F32)
        sc = _sigmoid(gc_ref[...].astype(F32))
        sa = _sigmoid(ga_ref[...].astype(F32))
        dco_ref[...] = (dm * sc).astype(BF16)
        dao_ref[...] = (dm * sa).astype(BF16)
        tiles = ((dm * co_ref[...].astype(F32) * sc * (1.0 - sc)).astype(BF16),
                 (dm * ao_ref[...].astype(F32) * sa * (1.0 - sa)).astype(BF16))

        def window(p, at):
            start = pl.multiple_of((C_GC, C_GA)[p] + at[1] * EW_TC, EW_TC)
            return dproj_ref.at[pl.ds(pl.multiple_of(at[0] * tm, tm), tm), pl.ds(start, EW_TC)]

        _write_behind(i * nj + j, (s // tm) * nj, buf, sems, tiles, window, (i, j))

    nj = D // EW_TC
    return _call(
        body, name=name, grid=(s // tm, nj),
        in_specs=[tile, pl.BlockSpec((tm, EW_TC), lambda i, j: (i, C_GC // EW_TC + j)),
                  pl.BlockSpec((tm, EW_TC), lambda i, j: (i, C_GA // EW_TC + j)), tile, tile],
        out_specs=[anyspec, tile, tile],
        out_shape=[_sds((s, N_IN), BF16), _sds((s, D), BF16), _sds((s, D), BF16)],
        scratch_shapes=[pltpu.VMEM((2, 2, tm, EW_TC), BF16), pltpu.SemaphoreType.DMA((2, 2))],
        compiler_params=_params("arbitrary", "arbitrary"),
    )(dmerged, proj, proj, conv_out, attn_out)


FF_TC = 256
FF_TM = 2048


def _row_pipeline(tm, matmul, finish, split=ROW_SPLIT):
    step = tm // split
    pending = None
    for r in range(split):
        rows = pl.ds(r * step, step)
        result = matmul(rows)
        if pending is not None:
            finish(*pending)
        pending = (rows, result)
    finish(*pending)


def _gate_up_fwd(h2, wgu_t, name):
    s = h2.shape[0]
    tm = min(FF_TM, s)
    nb = D_FF // FF_TC

    def body(h_ref, wg_ref, wu_ref, a_ref, g_ref, u_ref):
        def matmuls(rows):
            h = h_ref[rows, :]
            return (lax.dot_general(h, wg_ref[...], NT, preferred_element_type=F32),
                    lax.dot_general(h, wu_ref[...], NT, preferred_element_type=F32))

        def finish(rows, gu):
            g, u = gu
            a_ref[rows, :] = (g * _sigmoid(g) * u).astype(BF16)
            g_ref[rows, :] = g.astype(BF16)
            u_ref[rows, :] = u.astype(BF16)

        _row_pipeline(tm, matmuls, finish)

    tile = pl.BlockSpec((tm, FF_TC), lambda j, i: (i, j))
    return _call(
        body, name=name, grid=(nb, s // tm),
        in_specs=[pl.BlockSpec((tm, D), lambda j, i: (i, 0)), pl.BlockSpec((FF_TC, D), lambda j, i: (j, 0)),
                  pl.BlockSpec((FF_TC, D), lambda j, i: (nb + j, 0))],
        out_specs=[tile, tile, tile], out_shape=[_sds((s, D_FF), BF16)] * 3,
        compiler_params=_params("parallel", "parallel"),
    )(h2, wgu_t, wgu_t)


def _down_bwd_x(dx2b, wd, gate, up, name):
    s = dx2b.shape[0]
    tm = min(FF_TM, s)
    nb = D_FF // FF_TC

    def body(dx_ref, w_ref, g_ref, u_ref, dg_ref, du_ref):
        def matmul(rows):
            return lax.dot_general(dx_ref[rows, :], w_ref[...], NT, preferred_element_type=F32)

        def finish(rows, da):
            g = g_ref[rows, :].astype(F32)
            sg = _sigmoid(g)
            dg_ref[rows, :] = (da * u_ref[rows, :].astype(F32) * (sg * (1.0 + g * (1.0 - sg)))).astype(BF16)
            du_ref[rows, :] = (da * (g * sg)).astype(BF16)

        _row_pipeline(tm, matmul, finish)

    tile = pl.BlockSpec((tm, FF_TC), lambda j, i: (i, j))
    return _call(
        body, name=name, grid=(nb, s // tm),
        in_specs=[pl.BlockSpec((tm, D), lambda j, i: (i, 0)), pl.BlockSpec((FF_TC, D), lambda j, i: (j, 0)), tile, tile],
        out_specs=[tile, tile], out_shape=[_sds((s, D_FF), BF16)] * 2,
        compiler_params=_params("parallel", "parallel"),
    )(dx2b, wd, gate, up)


class _Weights:
    def __init__(self, **groups):
        self.groups = groups

    def begin(self, group, after):
        return ()

    def end(self, group, after):
        return self.groups[group]


class _NoReduce:
    def start(self, group, grads):
        return ()

    def middle(self, group, after):
        return ()


def _local_step(x, tgt, g_mix, g_ffn, g_final, sinks, weights, reducer=None, after=()):
    reducer = reducer or _NoReduce()
    s = x.shape[0]
    tab = _rope_tables(s)
    big = dict(tm=2048, tn=512, tk=1024)
    h1 = _rms_fwd(x, g_mix, "rms1_fwd", after=after)
    win_t, conv_w = weights.end("in", weights.begin("in", (h1,)))
    proj = _matmul(h1, win_t, mode="nt", out_dtype=BF16, name="proj_fwd", tm=2048, tn=512, tk=1024)
    attn = _swa_fwd(proj, tab, sinks, "attn_fwd", after=weights.begin("mix", (proj,)))
    wco, wao, wo = weights.end("mix", (attn,))
    conv_y = _conv_fwd(proj, conv_w, "conv_fwd")
    conv_out, attn_out = _matmul_group((conv_y, attn), (wco, wao), mode="nn", tm=2048, tn=512, out_dtype=BF16, name="branch_out_fwd")
    merged = _merge_fwd(proj, conv_out, attn_out, "merge_fwd")
    x1 = _matmul(merged, wo, mode="nn", out_dtype=F32, name="wo_fwd", res=x, after=weights.begin("ffn", (merged,)), **big)
    h2 = _rms_fwd(x1, g_ffn, "rms2_fwd")
    wgu_t, wd = weights.end("ffn", (h2,))
    act, gate, up = _gate_up_fwd(h2, wgu_t, "gate_up_fwd")
    x2 = _matmul(act, wd, mode="nn", out_dtype=F32, name="down_fwd", res=x1, tm=1024, tn=512, tk=D_FF)
    dx2, dx2b, dg_final, lossvec = _loss_head(x2, g_final, tgt, "loss_head")
    dgate, dup = _down_bwd_x(dx2b, wd, gate, up, "down_bwd_x")
    g_wd = _matmul(act, dx2b, mode="tn", out_dtype=BF16, name="down_bwd_w", tm=1408, tn=1024, tk=2048)
    dh2 = _matmul([dgate, dup], wgu_t, mode="nn", out_dtype=BF16, name="gate_up_bwd_x", tm=1024, tn=1024, tk=1408)
    g_wgu_t = _matmul([dgate, dup], h2, mode="tn", out_dtype=BF16, name="gate_up_bwd_w", tm=1408, tn=1024, tk=2048)
    after_ffn = reducer.start("ffn", dict(wgu_t=g_wgu_t, wd=g_wd))
    dx1, dx1b, dg_ffn = _rms_bwd(dh2, x1, g_ffn, dx2, "rms2_bwd")
    dmerged = _matmul(dx1b, wo, mode="nt", out_dtype=BF16, name="wo_bwd_x", after=after_ffn, **big)
    after_ffn = reducer.middle("ffn", (dmerged,))
    dproj, dco, dao = _merge_bwd(dmerged, proj, conv_out, attn_out, "merge_bwd")
    dconv_y, dattn = _matmul_group((dco, dao), (wco, wao), mode="nt", tm=2048, tn=512, out_dtype=BF16, name="branch_out_bwd_x",
                                   after=after_ffn)
    g_wco, g_wao, g_wo = _matmul_group((conv_y, attn, merged), (dco, dao, dx1b), mode="tn", tm=512, tn=1024, out_dtype=BF16,
                                       name="mix_bwd_w")
    after_mix = reducer.start("mix", dict(wco=g_wco, wao=g_wao, wo=g_wo))
    dproj, dconv_w = _conv_bwd(dconv_y, proj, conv_w, dproj, "conv_bwd", after=after_mix)
    after_mix = reducer.middle("mix", (dconv_w,))
    dproj, dkr, dv, dsinks = _swa_bwd(dattn, proj, tab, sinks, dproj, "attn_bwd", after=after_mix)
    dproj = _kv_bwd(dkr, dv, tab, dproj, "kv_bwd")
    g_win_t = _matmul(dproj, h1, mode="tn", out_dtype=BF16, name="proj_bwd_w", tm=512, tn=1024, tk=2048)
    after_in = reducer.middle("in", reducer.start("in", dict(win_t=g_win_t)))
    dh1 = _matmul(dproj, win_t, mode="nn", out_dtype=BF16, name="proj_bwd_x", tm=1024, tn=1024, tk=1664, after=after_in)
    dx, _, dg_mix = _rms_bwd(dh1, x, g_mix, dx1, "rms1_bwd")
    grads = dict(win_t=g_win_t, wgu_t=g_wgu_t, wd=g_wd, wco=g_wco, wao=g_wao, wo=g_wo)
    small = dict(g_mix=dg_mix, g_ffn=dg_ffn, g_final=dg_final, conv_w=dconv_w, sinks=dsinks, lossvec=lossvec)
    return dx, grads, small


def _position():
    return lax.axis_index("x"), lax.axis_index("y"), lax.axis_index("c")


def _other_chips(x, y):
    return [(1 - x, y), (x, 1 - y), (1 - x, 1 - y)]


SEM_SPEC = pl.BlockSpec(memory_space=pltpu.SEMAPHORE)
EFFECT = pltpu.SideEffectType.DATAFLOW_SIDE_EFFECTING
TOKEN = jax.ShapeDtypeStruct((8, 128), F32)
TOKEN_SPEC = pl.BlockSpec(memory_space=pltpu.VMEM)


def _hbm(a):
    return pltpu.with_memory_space_constraint(a, pltpu.HBM)


def _place(ws, me_idx, dtypes, name, after=()):
    n = len(ws)

    def body(i_ref, *refs):
        for w_ref, o_ref, dtype in zip(refs[:n], refs[n + len(after):], dtypes):
            o_ref[...] = w_ref[...].astype(dtype)

    grid_spec = pltpu.PrefetchScalarGridSpec(
        num_scalar_prefetch=1, grid=(1,),
        in_specs=[pl.BlockSpec(w.shape, lambda i, me: (0, 0)) for w in ws] + [HBM_SPEC] * len(after),
        out_specs=[pl.BlockSpec(w.shape, lambda i, me: (me[0], 0)) for w in ws])
    return _call(body, name=name, grid_spec=grid_spec,
                 out_shape=[_sds((N_DEV * w.shape[0], w.shape[1]), dtype) for w, dtype in zip(ws, dtypes)],
                 compiler_params=_params("arbitrary"))(me_idx, *ws, *after)


def _own_rows(ref, r, px, py, pc):
    return ref.at[pl.ds((4 * px + 2 * py + pc) * r, r), :]


def _gather_phase(bufs, waits, plans, after, name):
    n = len(bufs)
    rows = [b.shape[0] // N_DEV for b in bufs]
    nw, npl = len(waits), len(plans)

    def body(*refs):
        ins = refs[:n]
        wait_sems = refs[n:n + 2 * nw]
        out0 = n + 2 * nw + len(after)
        new_sems = refs[out0:out0 + 2 * npl]
        token = refs[-1]
        x, y, c = _position()
        for w, (_, _, sent, received) in enumerate(waits):
            for a in range(n):
                for count, wait in ((sent, "wait_send"), (received, "wait_recv")):
                    span = _whole(ins[a], count * rows[a])
                    getattr(pltpu.make_async_remote_copy(
                        src_ref=span, dst_ref=span, send_sem=wait_sems[2 * w].at[a], recv_sem=wait_sems[2 * w + 1].at[a],
                        device_id=(x, y, c), device_id_type=MESH), wait)()
        for k, plan in enumerate(plans):
            for a in range(n):
                for block, target in plan(x, y, c):
                    span = _own_rows(ins[a], rows[a], *block)
                    pltpu.make_async_remote_copy(src_ref=span, dst_ref=span, send_sem=new_sems[2 * k].at[a],
                                                 recv_sem=new_sems[2 * k + 1].at[a], device_id=target, device_id_type=MESH).start()
        token[...] = jnp.zeros_like(token)

    sem_ops = [s for send, recv, _, _ in waits for s in (send, recv)]
    outs = _call(
        body, name=name, in_specs=[HBM_SPEC] * n + [SEM_SPEC] * (2 * nw) + [HBM_SPEC] * len(after),
        out_specs=[SEM_SPEC] * (2 * npl) + [HBM_SPEC] * n + [TOKEN_SPEC],
        out_shape=[pltpu.SemaphoreType.DMA((n,))] * (2 * npl) + [pltpu.HBM(b.shape, b.dtype) for b in bufs] + [TOKEN],
        input_output_aliases={i: 2 * npl + i for i in range(n)},
        compiler_params=pltpu.CompilerParams(has_side_effects=EFFECT),
    )(*[_hbm(b) for b in bufs], *sem_ops, *after)
    pairs = [(outs[2 * k], outs[2 * k + 1]) for k in range(npl)]
    return pairs, list(outs[2 * npl:2 * npl + n]), outs[-1]


def _own_to_near(x, y, c):
    return [((x, y, c), (x, y, 1 - c)), ((x, y, c), (1 - x, y, c)), ((x, y, c), (x, 1 - y, c))]


def _near_to_sibling(x, y, c):
    return [((1 - x, y, c), (x, y, 1 - c)), ((x, 1 - y, c), (x, y, 1 - c))]


def _relay_diagonal(x, y, c):
    north = c
    source = (x * north + (1 - x) * (1 - north), (1 - y) * north + y * (1 - north), c)
    target = ((1 - x) * north + x * (1 - north), y * north + (1 - y) * (1 - north), c)
    return [(source, target)]


def _diagonal_to_sibling(x, y, c):
    return [((1 - x, 1 - y, c), (x, y, 1 - c))]


def _gather_start(bufs, groups, name, after=()):
    n = len(bufs)
    rows = [b.shape[0] // N_DEV for b in bufs]
    ng = len(groups)

    def body(*refs):
        ins = refs[:n]
        sems = refs[n + len(after):n + len(after) + 2 * ng]
        token = refs[-1]
        x, y, c = _position()
        targets = [(x, y, 1 - c)] + [(*chip, c) for chip in _other_chips(x, y)]
        for g, members in enumerate(groups):
            for slot, a in enumerate(members):
                own = _own_rows(ins[a], rows[a], x, y, c)
                for to in targets:
                    pltpu.make_async_remote_copy(src_ref=own, dst_ref=own, send_sem=sems[2 * g].at[slot],
                                                 recv_sem=sems[2 * g + 1].at[slot], device_id=to, device_id_type=MESH).start()
        token[...] = jnp.zeros_like(token)

    sem_shapes = []
    for members in groups:
        sem_shapes += [pltpu.SemaphoreType.DMA((len(members),))] * 2
    outs = _call(
        body, name=name, in_specs=[HBM_SPEC] * (n + len(after)),
        out_specs=[SEM_SPEC] * (2 * ng) + [HBM_SPEC] * n + [TOKEN_SPEC],
        out_shape=sem_shapes + [pltpu.HBM(b.shape, b.dtype) for b in bufs] + [TOKEN],
        input_output_aliases={i: 2 * ng + i for i in range(n)},
        compiler_params=pltpu.CompilerParams(has_side_effects=EFFECT),
    )(*[_hbm(b) for b in bufs], *after)
    sem_pairs = [(outs[2 * g], outs[2 * g + 1]) for g in range(ng)]
    return sem_pairs, list(outs[2 * ng:2 * ng + n]), outs[-1]


def _gather_forward(send_sems, recv_sems, bufs, after, name):
    n = len(bufs)
    rows = [b.shape[0] // N_DEV for b in bufs]

    def body(*refs):
        ins = refs[:n]
        send1, recv1 = refs[n], refs[n + 1]
        out0 = n + 2 + len(after)
        send2, recv2 = refs[out0], refs[out0 + 1]
        token = refs[-1]
        x, y, c = _position()
        for a in range(n):
            step1 = pltpu.make_async_remote_copy(
                src_ref=_whole(ins[a], 4 * rows[a]), dst_ref=_whole(ins[a], 4 * rows[a]), send_sem=send1.at[a],
                recv_sem=recv1.at[a], device_id=(x, y, c), device_id_type=MESH)
            step1.wait_send()
            step1.wait_recv()
        for a in range(n):
            for chip in _other_chips(x, y):
                blk = _own_rows(ins[a], rows[a], *chip, c)
                pltpu.make_async_remote_copy(src_ref=blk, dst_ref=blk, send_sem=send2.at[a], recv_sem=recv2.at[a],
                                             device_id=(x, y, 1 - c), device_id_type=MESH).start()
        token[...] = jnp.zeros_like(token)

    outs = _call(
        body, name=name, in_specs=[HBM_SPEC] * n + [SEM_SPEC, SEM_SPEC] + [HBM_SPEC] * len(after),
        out_specs=[SEM_SPEC, SEM_SPEC] + [HBM_SPEC] * n + [TOKEN_SPEC],
        out_shape=[pltpu.SemaphoreType.DMA((n,)), pltpu.SemaphoreType.DMA((n,))]
        + [pltpu.HBM(b.shape, b.dtype) for b in bufs] + [TOKEN],
        input_output_aliases={i: 2 + i for i in range(n)},
        compiler_params=pltpu.CompilerParams(has_side_effects=EFFECT),
    )(*bufs, send_sems, recv_sems, *after)
    return outs[0], outs[1], list(outs[2:2 + n]), outs[-1]


def _gather_done(send_sems, recv_sems, bufs, after, name):
    n = len(bufs)
    rows = [b.shape[0] // N_DEV for b in bufs]

    def body(*refs):
        ins = refs[:n]
        send2, recv2 = refs[n], refs[n + 1]
        x, y, c = _position()
        for a in range(n):
            step2 = pltpu.make_async_remote_copy(
                src_ref=_whole(ins[a], 3 * rows[a]), dst_ref=_whole(ins[a], 3 * rows[a]), send_sem=send2.at[a],
                recv_sem=recv2.at[a], device_id=(x, y, c), device_id_type=MESH)
            step2.wait_send()
            step2.wait_recv()

    outs = _call(
        body, name=name, in_specs=[HBM_SPEC] * n + [SEM_SPEC, SEM_SPEC] + [HBM_SPEC] * len(after),
        out_specs=[HBM_SPEC] * n, out_shape=[pltpu.HBM(b.shape, b.dtype) for b in bufs],
        input_output_aliases={i: i for i in range(n)},
        compiler_params=pltpu.CompilerParams(has_side_effects=EFFECT),
    )(*bufs, send_sems, recv_sems, *after)
    return list(outs)


def _whole(ref, nrows):
    return ref.at[pl.ds(0, nrows), :]


def _to_sibling(x, y, c):
    return [(2 * q + (1 - c), q, (x, y, 1 - c)) for q in range(4)]


def _to_chips(x, y, c):
    return [(2 * px + py, j, (px, py, c)) for j, (px, py) in enumerate(_other_chips(x, y))]


def _exchange_start(srcs, src_slots, plan, name):
    n = len(srcs)
    rows = [a.shape[0] // src_slots for a in srcs]
    n_copies = len(plan(0, 0, 0))
    lands = [lax.empty((n_copies * r, a.shape[1]), a.dtype) for a, r in zip(srcs, rows)]

    def body(*refs):
        ins, land_refs = refs[:n], refs[n:2 * n]
        send_sems, recv_sems = refs[2 * n], refs[2 * n + 1]
        token = refs[-1]
        for a in range(n):
            r = rows[a]
            for src_slot, dst_slot, target in plan(*_position()):
                pltpu.make_async_remote_copy(
                    src_ref=ins[a].at[pl.ds(src_slot * r, r), :], dst_ref=land_refs[a].at[pl.ds(dst_slot * r, r), :],
                    send_sem=send_sems.at[a], recv_sem=recv_sems.at[a], device_id=target, device_id_type=MESH).start()
        token[...] = jnp.zeros_like(token)

    outs = _call(
        body, name=name, in_specs=[HBM_SPEC] * (2 * n),
        out_specs=[SEM_SPEC, SEM_SPEC] + [HBM_SPEC] * (2 * n) + [TOKEN_SPEC],
        out_shape=[pltpu.SemaphoreType.DMA((n,)), pltpu.SemaphoreType.DMA((n,))]
        + [pltpu.HBM(a.shape, a.dtype) for a in srcs] + [pltpu.HBM(l.shape, l.dtype) for l in lands] + [TOKEN],
        input_output_aliases={i: 2 + i for i in range(2 * n)},
        compiler_params=pltpu.CompilerParams(has_side_effects=EFFECT),
    )(*[_hbm(a) for a in srcs], *[_hbm(l) for l in lands])
    return outs[0], outs[1], list(outs[2:2 + n]), list(outs[2 + n:2 + 2 * n]), outs[-1]


def _exchange_wait(send_sems, recv_sems, srcs, lands, after, name):
    n = len(srcs)

    def body(*refs):
        ins, land_refs = refs[:n], refs[n:2 * n]
        send_sems_ref, recv_sems_ref = refs[2 * n], refs[2 * n + 1]
        for a in range(n):
            span = _whole(land_refs[a], lands[a].shape[0])
            cp = pltpu.make_async_remote_copy(
                src_ref=span, dst_ref=span, send_sem=send_sems_ref.at[a],
                recv_sem=recv_sems_ref.at[a], device_id=_position(), device_id_type=MESH)
            cp.wait_send()
            cp.wait_recv()

    outs = _call(
        body, name=name, in_specs=[HBM_SPEC] * (2 * n) + [SEM_SPEC, SEM_SPEC] + [HBM_SPEC] * len(after),
        out_specs=[HBM_SPEC] * (2 * n),
        out_shape=[pltpu.HBM(a.shape, a.dtype) for a in srcs] + [pltpu.HBM(l.shape, l.dtype) for l in lands],
        input_output_aliases={i: i for i in range(2 * n)},
        compiler_params=pltpu.CompilerParams(has_side_effects=EFFECT),
    )(*srcs, *lands, send_sems, recv_sems, *after)
    return list(outs[:n]), list(outs[n:])


def _chip_partial(grads, recvs, idx, name):
    n = len(grads)
    rows = [recv.shape[0] // 4 for recv in recvs]

    def body(i_ref, *refs):
        del i_ref
        for g_ref, s_ref, o_ref in zip(refs[:n], refs[n:2 * n], refs[2 * n:]):
            o_ref[...] = (g_ref[...].astype(F32) + s_ref[...].astype(F32)).astype(BF16)

    grid_spec = pltpu.PrefetchScalarGridSpec(
        num_scalar_prefetch=1, grid=(3,),
        in_specs=[pl.BlockSpec((r, D), lambda t, i_ref: (2 * i_ref[1 + t] + i_ref[0], 0)) for r in rows]
        + [pl.BlockSpec((r, D), lambda t, i_ref: (i_ref[1 + t], 0)) for r in rows],
        out_specs=[pl.BlockSpec((r, D), lambda t, i_ref: (i_ref[1 + t], 0)) for r in rows])
    return _call(body, name=name, grid_spec=grid_spec, out_shape=[_sds((4 * r, D), BF16) for r in rows],
                 compiler_params=_params("arbitrary"))(idx, *grads, *recvs)


def _adamw_math(w, g, m, v):
    m2 = B1 * m + (1.0 - B1) * g
    v2 = B2 * v + (1.0 - B2) * jnp.square(g)
    m_hat = m2 / (1.0 - B1 ** STEP)
    v_hat = v2 / (1.0 - B2 ** STEP)
    return -LR * (m_hat / (jnp.sqrt(v_hat) + EPS_ADAM) + WD * w), m2, v2


def _reduce_adamw(ws, grads, from_sibling, from_chips, idx, ms, vs, name):
    n = len(ws)
    nb = 2
    tiles = [w.shape[0] // nb for w in ws]
    for w, g, s, c in zip(ws, grads, from_sibling, from_chips):
        r = w.shape[0]
        assert g.shape == (N_DEV * r, D) and s.shape == (4 * r, D) and c.shape == (3 * r, D)

    def body(i_ref, *refs):
        del i_ref
        ins, outs = refs[:8 * n], refs[8 * n:]
        for a in range(n):
            w_ref, p_ref, s_ref, r0_ref, r1_ref, r2_ref, m_ref, v_ref = ins[8 * a:8 * a + 8]
            g_ref, d_ref, nm_ref, nv_ref = outs[4 * a:4 * a + 4]
            g = p_ref[...].astype(F32) + s_ref[...].astype(F32)
            g = ((g + r0_ref[...].astype(F32)) + r1_ref[...].astype(F32)) + r2_ref[...].astype(F32)
            g_ref[...] = g
            d_ref[...], nm_ref[...], nv_ref[...] = _adamw_math(w_ref[...], g, m_ref[...], v_ref[...])

    in_specs, out_specs, operands, out_shape = [], [], [], []
    for a, tr in enumerate(tiles):
        own = pl.BlockSpec((tr, D), lambda i, i_ref: (i, 0))
        in_specs += [own, pl.BlockSpec((tr, D), lambda i, i_ref: (i_ref[0] * nb + i, 0)),
                     pl.BlockSpec((tr, D), lambda i, i_ref: (i_ref[1] * nb + i, 0))]
        in_specs += [pl.BlockSpec((tr, D), lambda i, i_ref, j=j: (j * nb + i, 0)) for j in range(3)] + [own, own]
        operands += [ws[a], grads[a], from_sibling[a], from_chips[a], from_chips[a], from_chips[a], ms[a], vs[a]]
        out_specs += [own] * 4
        out_shape += [_sds(ws[a].shape, F32)] * 4
    grid_spec = pltpu.PrefetchScalarGridSpec(num_scalar_prefetch=1, grid=(nb,), in_specs=in_specs, out_specs=out_specs)
    outs = _call(body, name=name, grid_spec=grid_spec, out_shape=out_shape, compiler_params=_params("parallel"))(idx, *operands)
    return [tuple(outs[4 * a:4 * a + 4]) for a in range(n)]


SMALL_ROWS = 8


def _small_all_reduce(pack, name, after=()):
    def body(p_ref, *rest):
        tot_ref, loss_ref, gath, send_sems, recv_sems = rest[len(after):]
        x, y, c = _position()
        me_id = 4 * x + 2 * y + c
        gath[me_id] = p_ref[...]
        copies = []
        for k in range(1, N_DEV):
            peer = tuple(1 - v if (k >> b) & 1 else v for v, b in ((x, 2), (y, 1), (c, 0)))
            cp = pltpu.make_async_remote_copy(src_ref=p_ref, dst_ref=gath.at[me_id], send_sem=send_sems.at[k - 1],
                                              recv_sem=recv_sems.at[k - 1], device_id=peer, device_id_type=MESH)
            cp.start()
            copies.append(cp)
        for cp in copies:
            cp.wait_recv()
        for cp in copies:
            cp.wait_send()
        tot = gath[0]
        for d in range(1, N_DEV):
            tot = tot + gath[d]
        tot_ref[...] = tot
        loss_ref[...] = jnp.full((1, 128), (0.5 / D) * jnp.sum(tot[SMALL_ROWS - 1:SMALL_ROWS, :]), F32)

    vm = pl.BlockSpec(memory_space=pltpu.VMEM)
    return _call(
        body, name=name, in_specs=[vm] + [HBM_SPEC] * len(after), out_specs=[vm, vm],
        out_shape=[_sds((SMALL_ROWS, D), F32), _sds((1, 128), F32)],
        scratch_shapes=[pltpu.VMEM((N_DEV, SMALL_ROWS, D), F32), pltpu.SemaphoreType.DMA((N_DEV - 1,)),
                        pltpu.SemaphoreType.DMA((N_DEV - 1,))],
    )(pack, *after)


def _adamw_small(ws, gs, ms, vs, name):
    n = len(ws)

    def body(*refs):
        for a in range(n):
            w_ref, g_ref, m_ref, v_ref = (refs[k * n + a] for k in range(4))
            d_ref, nm_ref, nv_ref = (refs[(4 + k) * n + a] for k in range(3))
            d_ref[...], nm_ref[...], nv_ref[...] = _adamw_math(w_ref[...], g_ref[...], m_ref[...], v_ref[...])

    vm = pl.BlockSpec(memory_space=pltpu.VMEM)
    outs = _call(body, name=name, in_specs=[vm] * (4 * n), out_specs=[vm] * (3 * n),
                 out_shape=[_sds(w.shape, F32) for w in ws] * 3)(*ws, *gs, *ms, *vs)
    return [(outs[a], outs[n + a], outs[2 * n + a]) for a in range(n)]


def kernel(x, g_mix, w_in, conv_w, attn_sinks, w_conv_out, w_attn_out, w_o, g_ffn, w_gate_up, w_down, g_final, loss_target, m_g_mix, m_w_in, m_conv_w, m_attn_sinks, m_w_conv_out, m_w_attn_out, m_w_o, m_g_ffn, m_w_gate_up, m_w_down, m_g_final, v_g_mix, v_w_in, v_conv_w, v_attn_sinks, v_w_conv_out, v_w_attn_out, v_w_o, v_g_ffn, v_w_gate_up, v_w_down, v_g_final):
    cx, cy, cc = _position()
    chip = 2 * cx + cy
    partial_idx = jnp.stack([cc, 2 * (1 - cx) + cy, 2 * cx + (1 - cy), 2 * (1 - cx) + (1 - cy)]).astype(jnp.int32)
    own_idx = jnp.stack([2 * chip + cc, chip]).astype(jnp.int32)
    me = 4 * cx + 2 * cy + cc

    me_idx = jnp.reshape(me, (1,)).astype(jnp.int32)
    first = _place([jnp.transpose(w_in[0]), jnp.pad(conv_w[0], ((0, 5), (0, 0)))], me_idx, (BF16, F32), "place_in")
    (to_near,), first, token_in = _gather_phase(first, [], [_own_to_near], (), "gather_in_start")
    gather_tokens = (token_in,)

    class Gathered:
        def __init__(self):
            self.state = {}

        def begin(self, group, after):
            if group == "in":
                (near, relay), bufs, token = _gather_phase(
                    first, [(*to_near, 3, 3)], [_near_to_sibling, _relay_diagonal], after, "gather_in_relay")
                later = [_place([w], me_idx, (BF16,), "place_" + k, after=(token,))[0] for k, w in (
                    ("w_conv_out", w_conv_out[0]), ("w_attn_out", w_attn_out[0]), ("w_o", w_o[0]),
                    ("w_gate_up", jnp.transpose(w_gate_up[0])), ("w_down", w_down[0]))]
                (sems_mix, sems_ffn), later, token_later = _gather_start(later, [[0, 1, 2], [3, 4]], "gather_start_later")
                self.state.update({"in": (near, relay, bufs), "mix": (sems_mix, later[:3]), "ffn": (sems_ffn, later[3:])})
                return (token_later,)
            (send_sems, recv_sems), group_bufs = self.state[group]
            send2, recv2, group_bufs, token = _gather_forward(send_sems, recv_sems, group_bufs, after, "gather_forward_" + group)
            self.state[group] = ((send2, recv2), group_bufs)
            return (token,)

        def end(self, group, after):
            if group == "in":
                near, relay, bufs = self.state[group]
                (last,), bufs, token = _gather_phase(bufs, [(*relay, 1, 1)], [_diagonal_to_sibling], after, "gather_in_last")
                _, full, _ = _gather_phase(bufs, [(*near, 2, 2), (*last, 1, 1)], [], (token,), "gather_in_done")
                return full[0], jnp.transpose(full[1].reshape(N_DEV, 8, 128)[:, :3, :], (1, 0, 2)).reshape(3, D)
            (send2, recv2), group_bufs = self.state[group]
            return _gather_done(send2, recv2, group_bufs, after, "gather_done_" + group)

    in_flight, own_pieces = {}, {}

    transposed = ("w_in", "w_gate_up")

    def as2d(k, a):
        if k in transposed:
            return jnp.transpose(a[0])
        return a[None] if a.ndim == 1 else (a[0] if a.ndim == 3 else a)

    w_all = {"g_mix": g_mix, "w_in": w_in, "conv_w": conv_w, "attn_sinks": attn_sinks, "w_conv_out": w_conv_out,
             "w_attn_out": w_attn_out, "w_o": w_o, "g_ffn": g_ffn, "w_gate_up": w_gate_up, "w_down": w_down, "g_final": g_final}
    m_all = {"g_mix": m_g_mix, "w_in": m_w_in, "conv_w": m_conv_w, "attn_sinks": m_attn_sinks, "w_conv_out": m_w_conv_out,
             "w_attn_out": m_w_attn_out, "w_o": m_w_o, "g_ffn": m_g_ffn, "w_gate_up": m_w_gate_up, "w_down": m_w_down,
             "g_final": m_g_final}
    v_all = {"g_mix": v_g_mix, "w_in": v_w_in, "conv_w": v_conv_w, "attn_sinks": v_attn_sinks, "w_conv_out": v_w_conv_out,
             "w_attn_out": v_w_attn_out, "w_o": v_w_o, "g_ffn": v_g_ffn, "w_gate_up": v_w_gate_up, "w_down": v_w_down,
             "g_final": v_g_final}
    results = {}

    def record(k, *vals):
        results[k] = [(jnp.transpose(val) if k in transposed else val).reshape(w_all[k].shape) for val in vals]

    def update(group, names, grads, from_sibling, from_chips):
        outs = _reduce_adamw([as2d(k, w_all[k]) for k in names], grads, from_sibling, from_chips, own_idx,
                             [as2d(k, m_all[k]) for k in names], [as2d(k, v_all[k]) for k in names], "adamw_" + group)
        for k, vals in zip(names, outs):
            record(k, *vals)
        return tuple(vals[2] for vals in outs)

    def update_small(grads):
        keys = list(grads)
        outs = _adamw_small([as2d(k, w_all[k]) for k in keys], [grads[k] for k in keys], [as2d(k, m_all[k]) for k in keys],
                            [as2d(k, v_all[k]) for k in keys], "adamw_small")
        for k, (d, nm, nv) in zip(keys, outs):
            record(k, grads[k], d, nm, nv)
        return tuple(nm for _, nm, _ in outs)

    kernel_name = {"win_t": "w_in", "wgu_t": "w_gate_up", "wd": "w_down", "wco": "w_conv_out", "wao": "w_attn_out", "wo": "w_o"}

    def finish(group, after):
        keys, send_sems, recv_sems, parts, from_chips = in_flight[group]
        _, from_chips = _exchange_wait(send_sems, recv_sems, parts, from_chips, after, "rs_chips_wait_" + group)
        grads, from_sibling = own_pieces[group]
        return update(group, [kernel_name[k] for k in keys], grads, from_sibling, from_chips)

    class Reducer:
        def start(self, group, gdict):
            keys, glist = list(gdict), list(gdict.values())
            send_sems, recv_sems, glist, lands, token = _exchange_start(glist, N_DEV, _to_sibling, "rs_sibling_start_" + group)
            in_flight[group] = (keys, send_sems, recv_sems, glist, lands)
            return (token,)

        def middle(self, group, after):
            keys, send_sems, recv_sems, glist, lands = in_flight[group]
            if group == "in":
                after = finish("ffn", after)
            glist, lands = _exchange_wait(send_sems, recv_sems, glist, lands, after, "rs_sibling_wait_" + group)
            parts = _chip_partial(glist, lands, partial_idx, "chip_partial_" + group)
            send_sems, recv_sems, parts, from_chips, token = _exchange_start(parts, 4, _to_chips, "rs_chips_start_" + group)
            in_flight[group] = (keys, send_sems, recv_sems, parts, from_chips)
            own_pieces[group] = (glist, lands)
            return (token,)

    dx, _, small = _local_step(x[0], loss_target[0], g_mix, g_ffn, g_final[None], attn_sinks, Gathered(),
                               reducer=Reducer(), after=gather_tokens)
    after = finish("mix", (dx,))

    sinks_row = jnp.pad(small["sinks"], ((0, 0), (0, D - 128)))
    pack = jnp.concatenate([small["g_mix"], small["g_ffn"], small["g_final"], small["conv_w"], sinks_row, small["lossvec"]], axis=0)
    tot, loss_row = _small_all_reduce(pack, "small_all_reduce", after=after)
    loss = loss_row[0, 0]
    g_small = {
        "g_mix": tot[0:1], "g_ffn": tot[1:2], "g_final": tot[2:3],
        "conv_w": lax.dynamic_slice(tot, (3, me * 128), (3, 128)), "attn_sinks": tot[6:7, :N_HEADS],
    }
    finish("in", update_small(g_small))

    order = ["g_mix", "w_in", "conv_w", "attn_sinks", "w_conv_out", "w_attn_out", "w_o", "g_ffn", "w_gate_up", "w_down", "g_final"]
    return (loss, dx[None], *[results[k][i] for i in range(4) for k in order])
```

```python
import functools
import math

import jax
import jax.numpy as jnp
from jax import lax
from jax.experimental import pallas as pl
from jax.experimental.pallas import tpu as pltpu

F32 = jnp.float32
BF16 = jnp.bfloat16

D = 1024
HEAD_DIM = 64
N_HEADS = 16
N_KV = 4
GROUP = N_HEADS // N_KV
D_KV = N_KV * HEAD_DIM
BLOCK = 128
ROT_DIM = HEAD_DIM // 4
ROPE_THETA = 500000.0
ATTN_SCALE = 1.0 / math.sqrt(HEAD_DIM)
NEG_INF = -1e30
D_FF = 2816
N_IN = 6656
EPS = 1e-5
C_CB, C_CC, C_CX, C_Q, C_K, C_V, C_GC, C_GA = 0, 1024, 2048, 3072, 4096, 4352, 4608, 5632

LR, B1, B2, EPS_ADAM, WD, STEP = 0.001, 0.9, 0.999, 1e-08, 0.01, 10

N_DEV = 8
MESH = pl.DeviceIdType.MESH
VMEM_LIMIT = 56 * 1024 * 1024

NN = (((1,), (0,)), ((), ()))
NT = (((1,), (1,)), ((), ()))
TN = (((0,), (0,)), ((), ()))
HBM_SPEC = pl.BlockSpec(memory_space=pl.ANY)
ROW_SPLIT = 4


def _call(body, **kw):
    return pl.pallas_call(body, **kw)


def _params(*sem):
    return pltpu.CompilerParams(dimension_semantics=sem, vmem_limit_bytes=VMEM_LIMIT)


def _sds(shape, dtype):
    return jax.ShapeDtypeStruct(shape, dtype)


def _matmul(a, b, *, mode, tm, tn, tk, out_dtype, name, res=None, after=()):
    parts = list(a) if isinstance(a, (list, tuple)) else [a]
    rows_a = parts[0].shape[0]
    cols_a = sum(p.shape[1] for p in parts)
    if mode == "nn":
        (m, kk), (_, n), dims = (rows_a, cols_a), b.shape, NN
    elif mode == "nt":
        (m, kk), (n, _), dims = (rows_a, cols_a), b.shape, NT
    else:
        (kk, m), (_, n), dims = (rows_a, cols_a), b.shape, TN
    tm, tn, tk = min(tm, m), min(tn, n), min(tk, kk)
    assert m % tm == 0 and n % tn == 0 and kk % tk == 0, (name, m, n, kk, tm, tn, tk)
    nk = kk // tk
    split_axis, width = (2, tk) if mode == "nn" else (0, tm)
    assert len(parts) == 1 or mode in ("nn", "tn")
    assert len(parts) == 1 or all(p.shape[1] % width == 0 for p in parts), (name, width)
    counts = [p.shape[1] // width for p in parts]
    starts = [sum(counts[:p]) for p in range(len(parts))]

    def a_spec(p):
        def col(t):
            return jnp.clip(t - starts[p], 0, counts[p] - 1) if len(parts) > 1 else t

        if mode == "tn":
            return pl.BlockSpec((tk, tm), lambda i, j, k: (k, col(i)))
        return pl.BlockSpec((tm, tk), lambda i, j, k: (i, col(k)))

    if mode == "nt":
        b_spec = pl.BlockSpec((tn, tk), lambda i, j, k: (j, k))
    else:
        b_spec = pl.BlockSpec((tk, tn), lambda i, j, k: (k, j))
    o_spec = pl.BlockSpec((tm, tn), lambda i, j, k: (i, j))
    has_res = res is not None
    n_parts = len(parts)
    unit = 128 if mode == "tn" else 16
    split = ROW_SPLIT if tm % (ROW_SPLIT * unit) == 0 else 1

    def body(*refs):
        a_refs, b_ref = refs[:n_parts], refs[n_parts]
        r_ref = refs[n_parts + 1] if has_res else None
        o_ref = refs[n_parts + 1 + has_res + len(after)]
        k = pl.program_id(2)

        acc_ref = refs[-1] if nk > 1 else None

        def step(a_ref):
            def matmul(rows):
                a_blk = a_ref[:, rows] if mode == "tn" else a_ref[rows, :]
                return lax.dot_general(a_blk, b_ref[...], dims, preferred_element_type=F32)

            def finish(rows, part):
                if nk > 1:
                    acc_ref[rows, :] += part
                else:
                    o_ref[rows, :] = (part + r_ref[rows, :] if has_res else part).astype(o_ref.dtype)

            _row_pipeline(tm, matmul, finish, split)

        if nk > 1:
            @pl.when(k == 0)
            def _():
                acc_ref[...] = jnp.zeros_like(acc_ref)

        if n_parts == 1:
            step(a_refs[0])
        else:
            t = pl.program_id(split_axis)
            for p in range(n_parts):
                pl.when((t >= starts[p]) & (t < starts[p] + counts[p]))(functools.partial(step, a_refs[p]))

        if nk > 1:
            @pl.when(k == nk - 1)
            def _():
                o_ref[...] = (acc_ref[...] + r_ref[...] if has_res else acc_ref[...]).astype(o_ref.dtype)

    ins = parts + [b] + ([res] if has_res else []) + list(after)
    in_specs = [a_spec(p) for p in range(n_parts)] + [b_spec] + ([o_spec] if has_res else []) + [HBM_SPEC] * len(after)
    scratch = [] if nk == 1 else [pltpu.VMEM((tm, tn), F32)]
    return _call(
        body, name=name, grid=(m // tm, n // tn, nk), in_specs=in_specs, out_specs=o_spec,
        out_shape=_sds((m, n), out_dtype), scratch_shapes=scratch,
        compiler_params=_params("parallel", "parallel", "arbitrary"),
    )(*ins)


def _matmul_group(a_group, b_group, *, mode, tm, tn, out_dtype, name, after=()):
    a0, b0 = a_group[0], b_group[0]
    a_pair, b_pair, count = a_group, b_group, len(a_group)
    if mode == "nn":
        (m, kk), (_, n), dims = a0.shape, b0.shape, NN
    elif mode == "nt":
        (m, kk), (n, _), dims = a0.shape, b0.shape, NT
    else:
        (kk, m), (_, n), dims = a0.shape, b0.shape, TN
    assert all(a.shape == a0.shape for a in a_pair) and all(b.shape == b0.shape for b in b_pair)
    tm, tn = min(tm, m), min(tn, n)
    assert m % tm == 0 and n % tn == 0, (name, m, n, tm, tn)
    a_spec = pl.BlockSpec((kk, tm), lambda i, j: (0, i)) if mode == "tn" else pl.BlockSpec((tm, kk), lambda i, j: (i, 0))
    b_spec = pl.BlockSpec((tn, kk), lambda i, j: (j, 0)) if mode == "nt" else pl.BlockSpec((kk, tn), lambda i, j: (0, j))
    o_spec = pl.BlockSpec((tm, tn), lambda i, j: (i, j))
    unit = 128 if mode == "tn" else 16
    split = ROW_SPLIT if tm % (ROW_SPLIT * unit) == 0 else 1

    def body(*refs):
        a_refs, b_refs, o_refs = refs[:count], refs[count:2 * count], refs[2 * count + len(after):]

        def matmul(rows):
            return tuple(lax.dot_general(a_ref[:, rows] if mode == "tn" else a_ref[rows, :], b_ref[...], dims,
                                         preferred_element_type=F32) for a_ref, b_ref in zip(a_refs, b_refs))

        def finish(rows, parts):
            for o_ref, part in zip(o_refs, parts):
                o_ref[rows, :] = part.astype(out_dtype)

        _row_pipeline(tm, matmul, finish, split)

    return _call(
        body, name=name, grid=(m // tm, n // tn), in_specs=[a_spec] * count + [b_spec] * count + [HBM_SPEC] * len(after),
        out_specs=[o_spec] * count, out_shape=[_sds((m, n), out_dtype)] * count,
        compiler_params=_params("parallel", "parallel"),
    )(*a_pair, *b_pair, *after)


def _row_tile(s):
    return min(512, s)


def _rms_fwd(x, g, name, after=()):
    s = x.shape[0]
    tm = _row_tile(s)

    def body(x_ref, g_ref, *rest):
        h_ref = rest[-1]
        xv = x_ref[...]
        r = lax.rsqrt(jnp.mean(xv * xv, axis=-1, keepdims=True) + EPS)
        h_ref[...] = (xv * r * g_ref[...]).astype(BF16)

    row = pl.BlockSpec((tm, D), lambda i: (i, 0))
    return _call(
        body, name=name, grid=(s // tm,), in_specs=[row, pl.BlockSpec((1, D), lambda i: (0, 0))] + [HBM_SPEC] * len(after),
        out_specs=row, out_shape=_sds((s, D), BF16), compiler_params=_params("parallel"),
    )(x, g, *after)


def _rms_bwd(dh, x, g, dres, name, after=()):
    s = x.shape[0]
    tm = _row_tile(s)

    def body(dh_ref, x_ref, g_ref, dres_ref, *rest):
        dx_ref, dxb_ref, dg_ref = rest[len(after):]
        xv = x_ref[...]
        r = lax.rsqrt(jnp.mean(xv * xv, axis=-1, keepdims=True) + EPS)
        xh = xv * r
        dhv = dh_ref[...].astype(F32)
        dyg = dhv * g_ref[...]
        dx = dres_ref[...] + r * (dyg - xh * jnp.mean(dyg * xh, axis=-1, keepdims=True))
        dx_ref[...] = dx
        dxb_ref[...] = dx.astype(BF16)
        part = jnp.sum(dhv * xh, axis=0, keepdims=True)

        @pl.when(pl.program_id(0) == 0)
        def _():
            dg_ref[...] = part

        @pl.when(pl.program_id(0) > 0)
        def _():
            dg_ref[...] += part

    row = pl.BlockSpec((tm, D), lambda i: (i, 0))
    vec = pl.BlockSpec((1, D), lambda i: (0, 0))
    return _call(
        body, name=name, grid=(s // tm,), in_specs=[row, row, vec, row] + [HBM_SPEC] * len(after), out_specs=[row, row, vec],
        out_shape=[_sds((s, D), F32), _sds((s, D), BF16), _sds((1, D), F32)],
        compiler_params=_params("arbitrary"),
    )(dh, x, g, dres, *after)


def _loss_head(x2, g, tgt, name):
    s = x2.shape[0]
    tm = _row_tile(s)

    def body(x_ref, g_ref, t_ref, dx_ref, dxb_ref, dg_ref, l_ref):
        xv = x_ref[...]
        gv = g_ref[...]
        r = lax.rsqrt(jnp.mean(xv * xv, axis=-1, keepdims=True) + EPS)
        xh = xv * r
        err = xh * gv - t_ref[...]
        dy = err * (1.0 / D)
        dyg = dy * gv
        dx = r * (dyg - xh * jnp.mean(dyg * xh, axis=-1, keepdims=True))
        dx_ref[...] = dx
        dxb_ref[...] = dx.astype(BF16)
        dg_part = jnp.sum(dy * xh, axis=0, keepdims=True)
        l_part = jnp.sum(err * err, axis=0, keepdims=True)

        @pl.when(pl.program_id(0) == 0)
        def _():
            dg_ref[...] = dg_part
            l_ref[...] = l_part

        @pl.when(pl.program_id(0) > 0)
        def _():
            dg_ref[...] += dg_part
            l_ref[...] += l_part

    row = pl.BlockSpec((tm, D), lambda i: (i, 0))
    vec = pl.BlockSpec((1, D), lambda i: (0, 0))
    return _call(
        body, name=name, grid=(s // tm,), in_specs=[row, vec, row], out_specs=[row, row, vec, vec],
        out_shape=[_sds((s, D), F32), _sds((s, D), BF16), _sds((1, D), F32), _sds((1, D), F32)],
        compiler_params=_params("arbitrary"),
    )(x2, g, tgt)


CONV_TC = 256


def _shift_down(u, k, rows):
    return jnp.where(rows >= k, pltpu.roll(u, k, 0), 0.0)


def _shift_up(u, k, rows, s):
    return jnp.where(rows < s - k, pltpu.roll(u, s - k, 0), 0.0)


def _conv_specs(s):
    nb = D // CONV_TC

    def col(c0):
        return pl.BlockSpec((s, CONV_TC), lambda j, c0=c0: (0, c0 // CONV_TC + j))

    return nb, col


def _conv_fwd(proj, conv_w, name):
    s = proj.shape[0]
    nb, col = _conv_specs(s)

    def body(cb_ref, cc_ref, cx_ref, w_ref, y_ref):
        rows = lax.broadcasted_iota(jnp.int32, (s, CONV_TC), 0)
        u = cc_ref[...].astype(F32) * cx_ref[...].astype(F32)
        w = w_ref[...]
        c = w[0:1] * _shift_down(u, 2, rows) + w[1:2] * _shift_down(u, 1, rows) + w[2:3] * u
        y_ref[...] = (cb_ref[...].astype(F32) * c).astype(BF16)

    return _call(
        body, name=name, grid=(nb,),
        in_specs=[col(C_CB), col(C_CC), col(C_CX), pl.BlockSpec((3, CONV_TC), lambda j: (0, j))],
        out_specs=pl.BlockSpec((s, CONV_TC), lambda j: (0, j)), out_shape=_sds((s, D), BF16),
        compiler_params=_params("parallel"),
    )(proj, proj, proj, conv_w)


def _write_behind(t, nt, buf, sems, tiles, window, where):
    slot = t % 2

    def copies(sl, at):
        return [pltpu.make_async_copy(buf.at[sl, p], window(p, at), sems.at[sl, p]) for p in range(len(tiles))]

    @pl.when(t >= 2)
    def _():
        for cp in copies(slot, where):
            cp.wait()

    for p, tile in enumerate(tiles):
        buf[slot, p] = tile
    started = copies(slot, where)
    for cp in started:
        cp.start()

    @pl.when(t == nt - 1)
    def _():
        for cp in started:
            cp.wait()
        if nt > 1:
            for cp in copies(1 - slot, where):
                cp.wait()


def _conv_bwd(dy, proj, conv_w, dproj, name, after=()):
    s = proj.shape[0]
    nb, col = _conv_specs(s)

    def body(dy_ref, cb_ref, cc_ref, cx_ref, w_ref, *rest):
        dproj_ref, dw_ref, buf, sems = rest[1 + len(after):]
        j = pl.program_id(0)
        rows = lax.broadcasted_iota(jnp.int32, (s, CONV_TC), 0)
        cc = cc_ref[...].astype(F32)
        cx = cx_ref[...].astype(F32)
        u = cc * cx
        u1 = _shift_down(u, 1, rows)
        u2 = _shift_down(u, 2, rows)
        w = w_ref[...]
        c = w[0:1] * u2 + w[1:2] * u1 + w[2:3] * u
        dyv = dy_ref[...].astype(F32)
        dc = dyv * cb_ref[...].astype(F32)
        du = w[2:3] * dc + w[1:2] * _shift_up(dc, 1, rows, s) + w[0:1] * _shift_up(dc, 2, rows, s)

        def window(p, jj):
            start = pl.multiple_of((C_CB, C_CC, C_CX)[p] + jj * CONV_TC, CONV_TC)
            return dproj_ref.at[:, pl.ds(start, CONV_TC)]

        tiles = ((dyv * c).astype(BF16), (du * cx).astype(BF16), (du * cc).astype(BF16))
        _write_behind(j * 0, 1, buf, sems, tiles, window, j)
        dw_ref[...] = jnp.concatenate(
            [jnp.sum(dc * u2, axis=0, keepdims=True), jnp.sum(dc * u1, axis=0, keepdims=True),
             jnp.sum(dc * u, axis=0, keepdims=True)], axis=0)

    return _call(
        body, name=name, grid=(nb,),
        in_specs=[pl.BlockSpec((s, CONV_TC), lambda j: (0, j)), col(C_CB), col(C_CC), col(C_CX),
                  pl.BlockSpec((3, CONV_TC), lambda j: (0, j))] + [HBM_SPEC] * (1 + len(after)),
        out_specs=[pl.BlockSpec(memory_space=pl.ANY), pl.BlockSpec((3, CONV_TC), lambda j: (0, j))],
        out_shape=[_sds((s, N_IN), BF16), _sds((3, D), F32)],
        scratch_shapes=[pltpu.VMEM((1, 3, s, CONV_TC), BF16), pltpu.SemaphoreType.DMA((1, 3))],
        input_output_aliases={5: 0}, compiler_params=_params("arbitrary"),
    )(dy, proj, proj, proj, conv_w, dproj, *after)


def _rope_tables(s):
    half = ROT_DIM // 2
    inv_freq = ROPE_THETA ** (-jnp.arange(0, ROT_DIM, 2, dtype=F32) / ROT_DIM)
    inv64 = jnp.concatenate([inv_freq, inv_freq, jnp.zeros((HEAD_DIM - ROT_DIM,), F32)])
    ang = jnp.arange(s, dtype=F32)[:, None] * jnp.concatenate([inv64, inv64])[None, :]
    d = lax.broadcasted_iota(jnp.int32, (s, 128), 1) % HEAD_DIM
    cos, sin = jnp.cos(ang), jnp.sin(ang)
    c = jnp.where(d < ROT_DIM, cos, 1.0)
    a = jnp.where(d < half, -sin, 0.0)
    b = jnp.where((d >= half) & (d < ROT_DIM), sin, 0.0)
    return jnp.concatenate([c, a, b], axis=1)


def _rope(x, tab):
    c, a, b = tab[:, 0:128], tab[:, 128:256], tab[:, 256:384]
    outs = []
    for i in range(x.shape[1] // 128):
        xc = x[:, i * 128:(i + 1) * 128]
        outs.append(xc * c + pltpu.roll(xc, 120, 1) * a + pltpu.roll(xc, 8, 1) * b)
    return outs[0] if len(outs) == 1 else jnp.concatenate(outs, axis=1)


def _rope_t(dx, tab):
    c, a, b = tab[:, 0:128], tab[:, 128:256], tab[:, 256:384]
    outs = []
    for i in range(dx.shape[1] // 128):
        dc = dx[:, i * 128:(i + 1) * 128]
        outs.append(dc * c + pltpu.roll(dc * a, 8, 1) + pltpu.roll(dc * b, 120, 1))
    return outs[0] if len(outs) == 1 else jnp.concatenate(outs, axis=1)


def _attn_in_specs():
    prev = lambda n: jnp.maximum(n - 1, 0)
    return [
        pl.BlockSpec((BLOCK, D), lambda n: (n, C_Q // D)),
        pl.BlockSpec((BLOCK, D_KV), lambda n: (n, C_K // D_KV)),
        pl.BlockSpec((BLOCK, D_KV), lambda n: (prev(n), C_K // D_KV)),
        pl.BlockSpec((BLOCK, D_KV), lambda n: (n, C_V // D_KV)),
        pl.BlockSpec((BLOCK, D_KV), lambda n: (prev(n), C_V // D_KV)),
        pl.BlockSpec((BLOCK, 384), lambda n: (n, 0)),
        pl.BlockSpec((BLOCK, 384), lambda n: (prev(n), 0)),
        pl.BlockSpec(memory_space=pltpu.SMEM),
    ]


HALF = HEAD_DIM
N_CHUNK = D // 128


def _swa_bias(n):
    qi = lax.broadcasted_iota(jnp.int32, (BLOCK, 2 * BLOCK), 0)
    kj = lax.broadcasted_iota(jnp.int32, (BLOCK, 2 * BLOCK), 1)
    rel = qi + BLOCK - kj
    valid = (rel >= 0) & (rel < BLOCK) & ((kj >= BLOCK) | (n > 0))
    return jnp.where(valid, 0.0, NEG_INF)


def _halves(x):
    lo = lax.broadcasted_iota(jnp.int32, x.shape, 1) < HALF
    return jnp.where(lo, x, 0.0).astype(BF16), jnp.where(lo, 0.0, x).astype(BF16)


def _dup_heads(x):
    out = []
    for pair in range(N_KV // 2):
        xc = x[:, pair * 128:(pair + 1) * 128]
        xr = pltpu.roll(xc, HALF, 1)
        lo = lax.broadcasted_iota(jnp.int32, xc.shape, 1) < HALF
        out += [jnp.where(lo, xc, xr), jnp.where(lo, xr, xc)]
    return out


def _swa_load(q_ref, kc_ref, kp_ref, vc_ref, vp_ref, tc_ref, tp_ref):
    qf = _rope(q_ref[...].astype(F32), tc_ref[...]) * ATTN_SCALE
    q_halves = [_halves(qf[:, c * 128:(c + 1) * 128]) for c in range(N_CHUNK)]
    kf = jnp.concatenate([_rope(kp_ref[...].astype(F32), tp_ref[...]), _rope(kc_ref[...].astype(F32), tc_ref[...])], axis=0)
    vf = jnp.concatenate([vp_ref[...], vc_ref[...]], axis=0).astype(F32)
    return q_halves, _dup_heads(kf), _dup_heads(vf)


def _swa_probs(qh, kk, bias, sink):
    s = lax.dot_general(qh, kk, NT, preferred_element_type=F32) + bias
    m = jnp.maximum(jnp.max(jnp.maximum(s[:, :BLOCK], s[:, BLOCK:]), axis=1, keepdims=True), sink)
    return jnp.exp(s - m), m


def _swa_fwd(proj, tab, sinks, name, after=()):
    s = proj.shape[0]

    def body(q_ref, kc_ref, kp_ref, vc_ref, vp_ref, tc_ref, tp_ref, sink_ref, *rest):
        o_ref = rest[-1]
        n = pl.program_id(0)
        q_halves, kdup, vdup = _swa_load(q_ref, kc_ref, kp_ref, vc_ref, vp_ref, tc_ref, tp_ref)
        bias = _swa_bias(n)
        ones = jnp.ones((2 * BLOCK, 128), BF16)
        kk = [k.astype(BF16) for k in kdup]
        vv = [[jnp.concatenate([v_half, ones], axis=1) for v_half in _halves(v)] for v in vdup]
        heads = [(c, half) for c in range(N_CHUNK) for half in range(2)]
        scores = [lax.dot_general(q_halves[c][half], kk[c // (GROUP // 2)], NT, preferred_element_type=F32)
                  for c, half in heads]
        probs = []
        for (c, half), sc in zip(heads, scores):
            sc = sc + bias
            m = jnp.maximum(jnp.max(jnp.maximum(sc[:, :BLOCK], sc[:, BLOCK:]), axis=1, keepdims=True), sink_ref[0, 2 * c + half])
            probs.append((jnp.exp(sc - m).astype(BF16), jnp.exp(sink_ref[0, 2 * c + half] - m)))
        outs = [lax.dot_general(e, vv[c // (GROUP // 2)][half], NN, preferred_element_type=F32)
                for (c, half), (e, _) in zip(heads, probs)]
        for c in range(N_CHUNK):
            parts = [outs[2 * c + half][:, :128] * (1.0 / (outs[2 * c + half][:, 128:] + probs[2 * c + half][1]))
                     for half in range(2)]
            o_ref[:, c * 128:(c + 1) * 128] = (parts[0] + parts[1]).astype(BF16)

    return _call(
        body, name=name, grid=(s // BLOCK,), in_specs=_attn_in_specs() + [HBM_SPEC] * len(after),
        out_specs=pl.BlockSpec((BLOCK, D), lambda n: (n, 0)), out_shape=_sds((s, D), BF16),
        compiler_params=_params("parallel"),
    )(proj, proj, proj, proj, proj, tab, tab, sinks, *after)


def _swa_bwd(do, proj, tab, sinks, dproj, name, after=()):
    s = proj.shape[0]
    nblk = s // BLOCK
    kv_of = lambda c: c // (GROUP // 2)

    def body(do_ref, q_ref, kc_ref, kp_ref, vc_ref, vp_ref, tc_ref, tp_ref, sink_ref, *rest):
        dproj_ref, dk_ref, dv_ref, ds_ref, dqout, dkbuf, dvbuf, sems = rest[1 + len(after):]
        n = pl.program_id(0)

        @pl.when(n == 0)
        def _():
            dk_ref[...] = jnp.zeros_like(dk_ref)
            dv_ref[...] = jnp.zeros_like(dv_ref)
            ds_ref[...] = jnp.zeros_like(ds_ref)

        q_halves, kdup, vdup = _swa_load(q_ref, kc_ref, kp_ref, vc_ref, vp_ref, tc_ref, tp_ref)
        dof = do_ref[...].astype(F32)
        do_halves = [_halves(dof[:, c * 128:(c + 1) * 128]) for c in range(N_CHUNK)]
        bias = _swa_bias(n)
        ones = jnp.ones((2 * BLOCK, 128), BF16)
        kk = [k.astype(BF16) for k in kdup]
        vv = [v.astype(BF16) for v in vdup]
        k_halves = [_halves(k) for k in kdup]
        heads = [(c, half) for c in range(N_CHUNK) for half in range(2)]
        lane_row = lax.broadcasted_iota(jnp.int32, (1, 128), 1)
        lo_kv = lax.broadcasted_iota(jnp.int32, (2 * BLOCK, 128), 1) < HALF
        scores = [lax.dot_general(q_halves[c][half], kk[kv_of(c)], NT, preferred_element_type=F32) for c, half in heads]
        dps = [lax.dot_general(do_halves[c][half], vv[kv_of(c)], NT, preferred_element_type=F32) for c, half in heads]
        exps = []
        for (c, half), sc in zip(heads, scores):
            sink = sink_ref[0, 2 * c + half]
            sc = sc + bias
            m = jnp.maximum(jnp.max(jnp.maximum(sc[:, :BLOCK], sc[:, BLOCK:]), axis=1, keepdims=True), sink)
            exps.append((jnp.exp(sc - m), jnp.exp(sink - m)))
        sums = [lax.dot_general(e.astype(BF16), ones, NN, preferred_element_type=F32) for e, _ in exps]
        dsink_row = jnp.zeros((1, 128), F32)
        dsb, pb = [], []
        for h, ((e, es), row_sum, dp) in enumerate(zip(exps, sums, dps)):
            inv = 1.0 / (row_sum + es)
            p = e * jnp.concatenate([inv, inv], axis=1)
            t = p * dp
            delta = jnp.sum(t, axis=1, keepdims=True)
            dsb.append((t - p * delta).astype(BF16))
            pb.append(p.astype(BF16))
            dsink = -jnp.sum(es * inv * delta, axis=0, keepdims=True)
            dsink_row = dsink_row + jnp.where(lane_row == h, dsink, 0.0)
        dq_parts = [lax.dot_general(d, k_halves[kv_of(c)][half], NN, preferred_element_type=F32) for (c, half), d in zip(heads, dsb)]
        dk_parts = [lax.dot_general(d, q_halves[c][half], TN, preferred_element_type=F32) for (c, half), d in zip(heads, dsb)]
        dv_parts = [lax.dot_general(p, do_halves[c][half], TN, preferred_element_type=F32) for (c, half), p in zip(heads, pb)]
        dq = jnp.concatenate([(dq_parts[2 * c] + dq_parts[2 * c + 1]) * ATTN_SCALE for c in range(N_CHUNK)], axis=1)

        def kv_sum(parts, hk):
            acc = (parts[GROUP * hk] + parts[GROUP * hk + 1]) + (parts[GROUP * hk + 2] + parts[GROUP * hk + 3])
            return acc + pltpu.roll(acc, HALF, 1)

        for pair in range(N_KV // 2):
            dkbuf[:, pair * 128:(pair + 1) * 128] = jnp.where(lo_kv, kv_sum(dk_parts, 2 * pair), kv_sum(dk_parts, 2 * pair + 1))
            dvbuf[:, pair * 128:(pair + 1) * 128] = jnp.where(lo_kv, kv_sum(dv_parts, 2 * pair), kv_sum(dv_parts, 2 * pair + 1))
        prev0 = pl.multiple_of(jnp.maximum(n - 1, 0) * BLOCK, BLOCK)
        cur0 = pl.multiple_of(n * BLOCK, BLOCK)

        @pl.when(n > 0)
        def _():
            dk_ref[pl.ds(prev0, BLOCK), :] += dkbuf[0:BLOCK, :]
            dv_ref[pl.ds(prev0, BLOCK), :] += dvbuf[0:BLOCK, :]

        dk_ref[pl.ds(cur0, BLOCK), :] += dkbuf[BLOCK:2 * BLOCK, :]
        dv_ref[pl.ds(cur0, BLOCK), :] += dvbuf[BLOCK:2 * BLOCK, :]
        ds_ref[...] += dsink_row

        def window(p, at):
            return dproj_ref.at[pl.ds(pl.multiple_of(at * BLOCK, BLOCK), BLOCK), pl.ds(C_Q, D)]

        _write_behind(n, nblk, dqout, sems, (_rope_t(dq, tc_ref[...]).astype(BF16),), window, n)

    blk = lambda w: pl.BlockSpec((BLOCK, w), lambda n: (n, 0))
    whole = lambda w: pl.BlockSpec((s, w), lambda n: (0, 0))
    n_in = 1 + len(_attn_in_specs())
    return _call(
        body, name=name, grid=(nblk,), in_specs=[blk(D)] + _attn_in_specs() + [HBM_SPEC] * (1 + len(after)),
        out_specs=[HBM_SPEC, whole(D_KV), whole(D_KV), pl.BlockSpec((1, 128), lambda n: (0, 0))],
        out_shape=[_sds((s, N_IN), BF16), _sds((s, D_KV), F32), _sds((s, D_KV), F32), _sds((1, 128), F32)],
        scratch_shapes=[pltpu.VMEM((2, 1, BLOCK, D), BF16), pltpu.VMEM((2 * BLOCK, D_KV), F32),
                        pltpu.VMEM((2 * BLOCK, D_KV), F32), pltpu.SemaphoreType.DMA((2, 1))],
        input_output_aliases={n_in: 0}, compiler_params=_params("arbitrary"),
    )(do, proj, proj, proj, proj, proj, tab, tab, sinks, dproj, *after)


def _kv_bwd(dkr, dv, tab, dproj, name):
    s = dkr.shape[0]
    tm = _row_tile(s)

    def body(dk_ref, dv_ref, t_ref, dproj_in, o_ref):
        del dproj_in
        o_ref[:, 0:D_KV] = _rope_t(dk_ref[...], t_ref[...]).astype(BF16)
        o_ref[:, D_KV:2 * D_KV] = dv_ref[...].astype(BF16)

    row = lambda w: pl.BlockSpec((tm, w), lambda i: (i, 0))
    return _call(
        body, name=name, grid=(s // tm,),
        in_specs=[row(D_KV), row(D_KV), row(384), pl.BlockSpec(memory_space=pl.ANY)],
        out_specs=pl.BlockSpec((tm, 2 * D_KV), lambda i: (i, C_K // (2 * D_KV))),
        out_shape=_sds((s, N_IN), BF16), input_output_aliases={3: 0}, compiler_params=_params("parallel"),
    )(dkr, dv, tab, dproj)


EW_TC = 512


def _sigmoid(x):
    return 0.5 * jnp.tanh(0.5 * x) + 0.5


def _branches_merge_fwd(conv_y, attn, wco, wao, proj, name):
    s = proj.shape[0]
    tm = min(2048, s)

    def body(y_ref, a_ref, wc_ref, wa_ref, gc_ref, ga_ref, co_ref, ao_ref, m_ref):
        def matmuls(rows):
            return (lax.dot_general(y_ref[rows, :], wc_ref[...], NN, preferred_element_type=F32),
                    lax.dot_general(a_ref[rows, :], wa_ref[...], NN, preferred_element_type=F32))

        def finish(rows, parts):
            co, ao = parts
            co_ref[rows, :] = co.astype(BF16)
            ao_ref[rows, :] = ao.astype(BF16)
            m_ref[rows, :] = (_sigmoid(gc_ref[rows, :].astype(F32)) * co + _sigmoid(ga_ref[rows, :].astype(F32)) * ao).astype(BF16)

        _row_pipeline(tm, matmuls, finish)

    act = pl.BlockSpec((tm, D), lambda i, j: (i, 0))
    wgt = pl.BlockSpec((D, EW_TC), lambda i, j: (0, j))
    tile = pl.BlockSpec((tm, EW_TC), lambda i, j: (i, j))
    return _call(
        body, name=name, grid=(s // tm, D // EW_TC),
        in_specs=[act, act, wgt, wgt, pl.BlockSpec((tm, EW_TC), lambda i, j: (i, C_GC // EW_TC + j)),
                  pl.BlockSpec((tm, EW_TC), lambda i, j: (i, C_GA // EW_TC + j))],
        out_specs=[tile, tile, tile], out_shape=[_sds((s, D), BF16)] * 3, compiler_params=_params("parallel", "parallel"),
    )(conv_y, attn, wco, wao, proj, proj)


def _wo_merge_bwd(dx1b, wo, proj, conv_out, attn_out, name, after=()):
    s = proj.shape[0]
    tm = min(1024, s)
    nj = D // EW_TC

    def body(dx_ref, w_ref, gc_ref, ga_ref, co_ref, ao_ref, *rest):
        dproj_ref, dco_ref, dao_ref, buf, sems = rest[len(after):]
        i, j = pl.program_id(0), pl.program_id(1)
        gate_c, gate_a = [], []

        def matmul(rows):
            return lax.dot_general(dx_ref[rows, :], w_ref[...], NT, preferred_element_type=F32)

        def finish(rows, dm):
            sc = _sigmoid(gc_ref[rows, :].astype(F32))
            sa = _sigmoid(ga_ref[rows, :].astype(F32))
            dco_ref[rows, :] = (dm * sc).astype(BF16)
            dao_ref[rows, :] = (dm * sa).astype(BF16)
            gate_c.append((dm * co_ref[rows, :].astype(F32) * sc * (1.0 - sc)).astype(BF16))
            gate_a.append((dm * ao_ref[rows, :].astype(F32) * sa * (1.0 - sa)).astype(BF16))

        _row_pipeline(tm, matmul, finish)

        def window(p, at):
            start = pl.multiple_of((C_GC, C_GA)[p] + at[1] * EW_TC, EW_TC)
            return dproj_ref.at[pl.ds(pl.multiple_of(at[0] * tm, tm), tm), pl.ds(start, EW_TC)]

        tiles = (jnp.concatenate(gate_c, axis=0), jnp.concatenate(gate_a, axis=0))
        _write_behind(i * nj + j, (s // tm) * nj, buf, sems, tiles, window, (i, j))

    tile = pl.BlockSpec((tm, EW_TC), lambda i, j: (i, j))
    return _call(
        body, name=name, grid=(s // tm, nj),
        in_specs=[pl.BlockSpec((tm, D), lambda i, j: (i, 0)), pl.BlockSpec((EW_TC, D), lambda i, j: (j, 0)),
                  pl.BlockSpec((tm, EW_TC), lambda i, j: (i, C_GC // EW_TC + j)),
                  pl.BlockSpec((tm, EW_TC), lambda i, j: (i, C_GA // EW_TC + j)), tile, tile] + [HBM_SPEC] * len(after),
        out_specs=[HBM_SPEC, tile, tile],
        out_shape=[_sds((s, N_IN), BF16), _sds((s, D), BF16), _sds((s, D), BF16)],
        scratch_shapes=[pltpu.VMEM((2, 2, tm, EW_TC), BF16), pltpu.SemaphoreType.DMA((2, 2))],
        compiler_params=_params("arbitrary", "arbitrary"),
    )(dx1b, wo, proj, proj, conv_out, attn_out, *after)


FF_TC = 256
FF_TM = 2048


def _row_pipeline(tm, matmul, finish, split=ROW_SPLIT):
    step = tm // split
    pending = None
    for r in range(split):
        rows = pl.ds(r * step, step)
        result = matmul(rows)
        if pending is not None:
            finish(*pending)
        pending = (rows, result)
    finish(*pending)


def _gate_up_fwd(h2, wgu_t, name):
    s = h2.shape[0]
    tm = min(FF_TM, s)
    nb = D_FF // FF_TC

    def body(h_ref, wg_ref, wu_ref, a_ref, g_ref, u_ref):
        def matmuls(rows):
            h = h_ref[rows, :]
            return (lax.dot_general(h, wg_ref[...], NT, preferred_element_type=F32),
                    lax.dot_general(h, wu_ref[...], NT, preferred_element_type=F32))

        def finish(rows, gu):
            g, u = gu
            a_ref[rows, :] = (g * _sigmoid(g) * u).astype(BF16)
            g_ref[rows, :] = g.astype(BF16)
            u_ref[rows, :] = u.astype(BF16)

        _row_pipeline(tm, matmuls, finish)

    tile = pl.BlockSpec((tm, FF_TC), lambda j, i: (i, j))
    return _call(
        body, name=name, grid=(nb, s // tm),
        in_specs=[pl.BlockSpec((tm, D), lambda j, i: (i, 0)), pl.BlockSpec((FF_TC, D), lambda j, i: (j, 0)),
                  pl.BlockSpec((FF_TC, D), lambda j, i: (nb + j, 0))],
        out_specs=[tile, tile, tile], out_shape=[_sds((s, D_FF), BF16)] * 3,
        compiler_params=_params("parallel", "parallel"),
    )(h2, wgu_t, wgu_t)


def _down_bwd_x(dx2b, wd, gate, up, name):
    s = dx2b.shape[0]
    tm = min(FF_TM, s)
    nb = D_FF // FF_TC

    def body(dx_ref, w_ref, g_ref, u_ref, dg_ref, du_ref):
        def matmul(rows):
            return lax.dot_general(dx_ref[rows, :], w_ref[...], NT, preferred_element_type=F32)

        def finish(rows, da):
            g = g_ref[rows, :].astype(F32)
            sg = _sigmoid(g)
            dg_ref[rows, :] = (da * u_ref[rows, :].astype(F32) * (sg * (1.0 + g * (1.0 - sg)))).astype(BF16)
            du_ref[rows, :] = (da * (g * sg)).astype(BF16)

        _row_pipeline(tm, matmul, finish)

    tile = pl.BlockSpec((tm, FF_TC), lambda j, i: (i, j))
    return _call(
        body, name=name, grid=(nb, s // tm),
        in_specs=[pl.BlockSpec((tm, D), lambda j, i: (i, 0)), pl.BlockSpec((FF_TC, D), lambda j, i: (j, 0)), tile, tile],
        out_specs=[tile, tile], out_shape=[_sds((s, D_FF), BF16)] * 2,
        compiler_params=_params("parallel", "parallel"),
    )(dx2b, wd, gate, up)


class _Weights:
    def __init__(self, **groups):
        self.groups = groups

    def begin(self, group, after):
        return ()

    def end(self, group, after):
        return self.groups[group]


class _NoReduce:
    def start(self, group, grads):
        return ()

    def middle(self, group, after):
        return ()


def _local_step(x, tgt, g_mix, g_ffn, g_final, sinks, weights, reducer=None, after=()):
    reducer = reducer or _NoReduce()
    s = x.shape[0]
    tab = _rope_tables(s)
    big = dict(tm=2048, tn=512, tk=1024)
    h1 = _rms_fwd(x, g_mix, "rms1_fwd", after=after)
    win_t, conv_w = weights.end("in", weights.begin("in", (h1,)))
    proj = _matmul(h1, win_t, mode="nt", out_dtype=BF16, name="proj_fwd", tm=2048, tn=512, tk=1024)
    attn = _swa_fwd(proj, tab, sinks, "attn_fwd", after=weights.begin("mix", (proj,)))
    wco, wao, wo = weights.end("mix", (attn,))
    conv_y = _conv_fwd(proj, conv_w, "conv_fwd")
    conv_out, attn_out, merged = _branches_merge_fwd(conv_y, attn, wco, wao, proj, "branch_out_fwd")
    x1 = _matmul(merged, wo, mode="nn", out_dtype=F32, name="wo_fwd", res=x, after=weights.begin("ffn", (merged,)), **big)
    h2 = _rms_fwd(x1, g_ffn, "rms2_fwd")
    wgu_t, wd = weights.end("ffn", (h2,))
    act, gate, up = _gate_up_fwd(h2, wgu_t, "gate_up_fwd")
    x2 = _matmul(act, wd, mode="nn", out_dtype=F32, name="down_fwd", res=x1, tm=1024, tn=512, tk=D_FF)
    dx2, dx2b, dg_final, lossvec = _loss_head(x2, g_final, tgt, "loss_head")
    dgate, dup = _down_bwd_x(dx2b, wd, gate, up, "down_bwd_x")
    g_wd = _matmul(act, dx2b, mode="tn", out_dtype=BF16, name="down_bwd_w", tm=1408, tn=1024, tk=2048)
    dh2 = _matmul([dgate, dup], wgu_t, mode="nn", out_dtype=BF16, name="gate_up_bwd_x", tm=1024, tn=1024, tk=1408)
    g_wgu_t = _matmul([dgate, dup], h2, mode="tn", out_dtype=BF16, name="gate_up_bwd_w", tm=1408, tn=1024, tk=2048)
    after_ffn = reducer.start("ffn", dict(wgu_t=g_wgu_t, wd=g_wd))
    dx1, dx1b, dg_ffn = _rms_bwd(dh2, x1, g_ffn, dx2, "rms2_bwd")
    dproj, dco, dao = _wo_merge_bwd(dx1b, wo, proj, conv_out, attn_out, "wo_bwd_x", after=after_ffn)
    after_ffn = reducer.middle("ffn", (dco,))
    dconv_y, dattn = _matmul_group((dco, dao), (wco, wao), mode="nt", tm=2048, tn=512, out_dtype=BF16, name="branch_out_bwd_x",
                                   after=after_ffn)
    g_wco, g_wao, g_wo = _matmul_group((conv_y, attn, merged), (dco, dao, dx1b), mode="tn", tm=512, tn=1024, out_dtype=BF16,
                                       name="mix_bwd_w")
    after_mix = reducer.start("mix", dict(wco=g_wco, wao=g_wao, wo=g_wo))
    dproj, dconv_w = _conv_bwd(dconv_y, proj, conv_w, dproj, "conv_bwd", after=after_mix)
    after_mix = reducer.middle("mix", (dconv_w,))
    dproj, dkr, dv, dsinks = _swa_bwd(dattn, proj, tab, sinks, dproj, "attn_bwd", after=after_mix)
    dproj = _kv_bwd(dkr, dv, tab, dproj, "kv_bwd")
    g_win_t = _matmul(dproj, h1, mode="tn", out_dtype=BF16, name="proj_bwd_w", tm=512, tn=1024, tk=2048)
    after_in = reducer.middle("in", reducer.start("in", dict(win_t=g_win_t)))
    dh1 = _matmul(dproj, win_t, mode="nn", out_dtype=BF16, name="proj_bwd_x", tm=1024, tn=1024, tk=1664, after=after_in)
    dx, _, dg_mix = _rms_bwd(dh1, x, g_mix, dx1, "rms1_bwd")
    grads = dict(win_t=g_win_t, wgu_t=g_wgu_t, wd=g_wd, wco=g_wco, wao=g_wao, wo=g_wo)
    small = dict(g_mix=dg_mix, g_ffn=dg_ffn, g_final=dg_final, conv_w=dconv_w, sinks=dsinks, lossvec=lossvec)
    return dx, grads, small


def _position():
    return lax.axis_index("x"), lax.axis_index("y"), lax.axis_index("c")


def _other_chips(x, y):
    return [(1 - x, y), (x, 1 - y), (1 - x, 1 - y)]


SEM_SPEC = pl.BlockSpec(memory_space=pltpu.SEMAPHORE)
EFFECT = pltpu.SideEffectType.DATAFLOW_SIDE_EFFECTING
TOKEN = jax.ShapeDtypeStruct((8, 128), F32)
TOKEN_SPEC = pl.BlockSpec(memory_space=pltpu.VMEM)


def _hbm(a):
    return pltpu.with_memory_space_constraint(a, pltpu.HBM)


def _place(ws, me_idx, dtypes, name, after=()):
    n = len(ws)

    def body(i_ref, *refs):
        for w_ref, o_ref, dtype in zip(refs[:n], refs[n + len(after):], dtypes):
            o_ref[...] = w_ref[...].astype(dtype)

    grid_spec = pltpu.PrefetchScalarGridSpec(
        num_scalar_prefetch=1, grid=(1,),
        in_specs=[pl.BlockSpec(w.shape, lambda i, me: (0, 0)) for w in ws] + [HBM_SPEC] * len(after),
        out_specs=[pl.BlockSpec(w.shape, lambda i, me: (me[0], 0)) for w in ws])
    return _call(body, name=name, grid_spec=grid_spec,
                 out_shape=[_sds((N_DEV * w.shape[0], w.shape[1]), dtype) for w, dtype in zip(ws, dtypes)],
                 compiler_params=_params("arbitrary"))(me_idx, *ws, *after)


def _own_rows(ref, r, px, py, pc):
    return ref.at[pl.ds((4 * px + 2 * py + pc) * r, r), :]


def _gather_phase(bufs, waits, plans, after, name):
    n = len(bufs)
    rows = [b.shape[0] // N_DEV for b in bufs]
    nw, npl = len(waits), len(plans)

    def body(*refs):
        ins = refs[:n]
        wait_sems = refs[n:n + 2 * nw]
        out0 = n + 2 * nw + len(after)
        new_sems = refs[out0:out0 + 2 * npl]
        token = refs[-1]
        x, y, c = _position()
        for w, (_, _, sent, received) in enumerate(waits):
            for a in range(n):
                for count, wait in ((sent, "wait_send"), (received, "wait_recv")):
                    span = _whole(ins[a], count * rows[a])
                    getattr(pltpu.make_async_remote_copy(
                        src_ref=span, dst_ref=span, send_sem=wait_sems[2 * w].at[a], recv_sem=wait_sems[2 * w + 1].at[a],
                        device_id=(x, y, c), device_id_type=MESH), wait)()
        for k, plan in enumerate(plans):
            for a in range(n):
                for block, target in plan(x, y, c):
                    span = _own_rows(ins[a], rows[a], *block)
                    pltpu.make_async_remote_copy(src_ref=span, dst_ref=span, send_sem=new_sems[2 * k].at[a],
                                                 recv_sem=new_sems[2 * k + 1].at[a], device_id=target, device_id_type=MESH).start()
        token[...] = jnp.zeros_like(token)

    sem_ops = [s for send, recv, _, _ in waits for s in (send, recv)]
    outs = _call(
        body, name=name, in_specs=[HBM_SPEC] * n + [SEM_SPEC] * (2 * nw) + [HBM_SPEC] * len(after),
        out_specs=[SEM_SPEC] * (2 * npl) + [HBM_SPEC] * n + [TOKEN_SPEC],
        out_shape=[pltpu.SemaphoreType.DMA((n,))] * (2 * npl) + [pltpu.HBM(b.shape, b.dtype) for b in bufs] + [TOKEN],
        input_output_aliases={i: 2 * npl + i for i in range(n)},
        compiler_params=pltpu.CompilerParams(has_side_effects=EFFECT),
    )(*[_hbm(b) for b in bufs], *sem_ops, *after)
    pairs = [(outs[2 * k], outs[2 * k + 1]) for k in range(npl)]
    return pairs, list(outs[2 * npl:2 * npl + n]), outs[-1]


def _own_to_near(x, y, c):
    return [((x, y, c), (x, y, 1 - c)), ((x, y, c), (1 - x, y, c)), ((x, y, c), (x, 1 - y, c))]


def _near_to_sibling(x, y, c):
    return [((1 - x, y, c), (x, y, 1 - c)), ((x, 1 - y, c), (x, y, 1 - c))]


def _relay_diagonal(x, y, c):
    north = c
    source = (x * north + (1 - x) * (1 - north), (1 - y) * north + y * (1 - north), c)
    target = ((1 - x) * north + x * (1 - north), y * north + (1 - y) * (1 - north), c)
    return [(source, target)]


def _diagonal_to_sibling(x, y, c):
    return [((1 - x, 1 - y, c), (x, y, 1 - c))]


def _gather_start(bufs, groups, name, after=()):
    n = len(bufs)
    rows = [b.shape[0] // N_DEV for b in bufs]
    ng = len(groups)

    def body(*refs):
        ins = refs[:n]
        sems = refs[n + len(after):n + len(after) + 2 * ng]
        token = refs[-1]
        x, y, c = _position()
        targets = [(x, y, 1 - c)] + [(*chip, c) for chip in _other_chips(x, y)]
        for g, members in enumerate(groups):
            for slot, a in enumerate(members):
                own = _own_rows(ins[a], rows[a], x, y, c)
                for to in targets:
                    pltpu.make_async_remote_copy(src_ref=own, dst_ref=own, send_sem=sems[2 * g].at[slot],
                                                 recv_sem=sems[2 * g + 1].at[slot], device_id=to, device_id_type=MESH).start()
        token[...] = jnp.zeros_like(token)

    sem_shapes = []
    for members in groups:
        sem_shapes += [pltpu.SemaphoreType.DMA((len(members),))] * 2
    outs = _call(
        body, name=name, in_specs=[HBM_SPEC] * (n + len(after)),
        out_specs=[SEM_SPEC] * (2 * ng) + [HBM_SPEC] * n + [TOKEN_SPEC],
        out_shape=sem_shapes + [pltpu.HBM(b.shape, b.dtype) for b in bufs] + [TOKEN],
        input_output_aliases={i: 2 * ng + i for i in range(n)},
        compiler_params=pltpu.CompilerParams(has_side_effects=EFFECT),
    )(*[_hbm(b) for b in bufs], *after)
    sem_pairs = [(outs[2 * g], outs[2 * g + 1]) for g in range(ng)]
    return sem_pairs, list(outs[2 * ng:2 * ng + n]), outs[-1]


def _gather_forward(send_sems, recv_sems, bufs, after, name):
    n = len(bufs)
    rows = [b.shape[0] // N_DEV for b in bufs]

    def body(*refs):
        ins = refs[:n]
        send1, recv1 = refs[n], refs[n + 1]
        out0 = n + 2 + len(after)
        send2, recv2 = refs[out0], refs[out0 + 1]
        token = refs[-1]
        x, y, c = _position()
        for a in range(n):
            step1 = pltpu.make_async_remote_copy(
                src_ref=_whole(ins[a], 4 * rows[a]), dst_ref=_whole(ins[a], 4 * rows[a]), send_sem=send1.at[a],
                recv_sem=recv1.at[a], device_id=(x, y, c), device_id_type=MESH)
            step1.wait_send()
            step1.wait_recv()
        for a in range(n):
            for chip in _other_chips(x, y):
                blk = _own_rows(ins[a], rows[a], *chip, c)
                pltpu.make_async_remote_copy(src_ref=blk, dst_ref=blk, send_sem=send2.at[a], recv_sem=recv2.at[a],
                                             device_id=(x, y, 1 - c), device_id_type=MESH).start()
        token[...] = jnp.zeros_like(token)

    outs = _call(
        body, name=name, in_specs=[HBM_SPEC] * n + [SEM_SPEC, SEM_SPEC] + [HBM_SPEC] * len(after),
        out_specs=[SEM_SPEC, SEM_SPEC] + [HBM_SPEC] * n + [TOKEN_SPEC],
        out_shape=[pltpu.SemaphoreType.DMA((n,)), pltpu.SemaphoreType.DMA((n,))]
        + [pltpu.HBM(b.shape, b.dtype) for b in bufs] + [TOKEN],
        input_output_aliases={i: 2 + i for i in range(n)},
        compiler_params=pltpu.CompilerParams(has_side_effects=EFFECT),
    )(*bufs, send_sems, recv_sems, *after)
    return outs[0], outs[1], list(outs[2:2 + n]), outs[-1]


def _gather_done(send_sems, recv_sems, bufs, after, name):
    n = len(bufs)
    rows = [b.shape[0] // N_DEV for b in bufs]

    def body(*refs):
        ins = refs[:n]
        send2, recv2 = refs[n], refs[n + 1]
        x, y, c = _position()
        for a in range(n):
            step2 = pltpu.make_async_remote_copy(
                src_ref=_whole(ins[a], 3 * rows[a]), dst_ref=_whole(ins[a], 3 * rows[a]), send_sem=send2.at[a],
                recv_sem=recv2.at[a], device_id=(x, y, c), device_id_type=MESH)
            step2.wait_send()
            step2.wait_recv()

    outs = _call(
        body, name=name, in_specs=[HBM_SPEC] * n + [SEM_SPEC, SEM_SPEC] + [HBM_SPEC] * len(after),
        out_specs=[HBM_SPEC] * n, out_shape=[pltpu.HBM(b.shape, b.dtype) for b in bufs],
        input_output_aliases={i: i for i in range(n)},
        compiler_params=pltpu.CompilerParams(has_side_effects=EFFECT),
    )(*bufs, send_sems, recv_sems, *after)
    return list(outs)


def _whole(ref, nrows):
    return ref.at[pl.ds(0, nrows), :]


def _to_sibling(x, y, c):
    return [(2 * q + (1 - c), q, (x, y, 1 - c)) for q in range(4)]


def _to_chips(x, y, c):
    return [(2 * px + py, j, (px, py, c)) for j, (px, py) in enumerate(_other_chips(x, y))]


def _exchange_start(srcs, src_slots, plan, name):
    n = len(srcs)
    rows = [a.shape[0] // src_slots for a in srcs]
    n_copies = len(plan(0, 0, 0))
    lands = [lax.empty((n_copies * r, a.shape[1]), a.dtype) for a, r in zip(srcs, rows)]

    def body(*refs):
        ins, land_refs = refs[:n], refs[n:2 * n]
        send_sems, recv_sems = refs[2 * n], refs[2 * n + 1]
        token = refs[-1]
        for a in range(n):
            r = rows[a]
            for src_slot, dst_slot, target in plan(*_position()):
                pltpu.make_async_remote_copy(
                    src_ref=ins[a].at[pl.ds(src_slot * r, r), :], dst_ref=land_refs[a].at[pl.ds(dst_slot * r, r), :],
                    send_sem=send_sems.at[a], recv_sem=recv_sems.at[a], device_id=target, device_id_type=MESH).start()
        token[...] = jnp.zeros_like(token)

    outs = _call(
        body, name=name, in_specs=[HBM_SPEC] * (2 * n),
        out_specs=[SEM_SPEC, SEM_SPEC] + [HBM_SPEC] * (2 * n) + [TOKEN_SPEC],
        out_shape=[pltpu.SemaphoreType.DMA((n,)), pltpu.SemaphoreType.DMA((n,))]
        + [pltpu.HBM(a.shape, a.dtype) for a in srcs] + [pltpu.HBM(l.shape, l.dtype) for l in lands] + [TOKEN],
        input_output_aliases={i: 2 + i for i in range(2 * n)},
        compiler_params=pltpu.CompilerParams(has_side_effects=EFFECT),
    )(*[_hbm(a) for a in srcs], *[_hbm(l) for l in lands])
    return outs[0], outs[1], list(outs[2:2 + n]), list(outs[2 + n:2 + 2 * n]), outs[-1]


def _exchange_wait(send_sems, recv_sems, srcs, lands, after, name):
    n = len(srcs)

    def body(*refs):
        ins, land_refs = refs[:n], refs[n:2 * n]
        send_sems_ref, recv_sems_ref = refs[2 * n], refs[2 * n + 1]
        for a in range(n):
            span = _whole(land_refs[a], lands[a].shape[0])
            cp = pltpu.make_async_remote_copy(
                src_ref=span, dst_ref=span, send_sem=send_sems_ref.at[a],
                recv_sem=recv_sems_ref.at[a], device_id=_position(), device_id_type=MESH)
            cp.wait_send()
            cp.wait_recv()

    outs = _call(
        body, name=name, in_specs=[HBM_SPEC] * (2 * n) + [SEM_SPEC, SEM_SPEC] + [HBM_SPEC] * len(after),
        out_specs=[HBM_SPEC] * (2 * n),
        out_shape=[pltpu.HBM(a.shape, a.dtype) for a in srcs] + [pltpu.HBM(l.shape, l.dtype) for l in lands],
        input_output_aliases={i: i for i in range(2 * n)},
        compiler_params=pltpu.CompilerParams(has_side_effects=EFFECT),
    )(*srcs, *lands, send_sems, recv_sems, *after)
    return list(outs[:n]), list(outs[n:])


def _chip_partial(grads, recvs, idx, name):
    n = len(grads)
    rows = [recv.shape[0] // 4 for recv in recvs]

    def body(i_ref, *refs):
        del i_ref
        for g_ref, s_ref, o_ref in zip(refs[:n], refs[n:2 * n], refs[2 * n:]):
            o_ref[...] = (g_ref[...].astype(F32) + s_ref[...].astype(F32)).astype(BF16)

    grid_spec = pltpu.PrefetchScalarGridSpec(
        num_scalar_prefetch=1, grid=(3,),
        in_specs=[pl.BlockSpec((r, D), lambda t, i_ref: (2 * i_ref[1 + t] + i_ref[0], 0)) for r in rows]
        + [pl.BlockSpec((r, D), lambda t, i_ref: (i_ref[1 + t], 0)) for r in rows],
        out_specs=[pl.BlockSpec((r, D), lambda t, i_ref: (i_ref[1 + t], 0)) for r in rows])
    return _call(body, name=name, grid_spec=grid_spec, out_shape=[_sds((4 * r, D), BF16) for r in rows],
                 compiler_params=_params("arbitrary"))(idx, *grads, *recvs)


def _adamw_math(w, g, m, v):
    m2 = B1 * m + (1.0 - B1) * g
    v2 = B2 * v + (1.0 - B2) * jnp.square(g)
    m_hat = m2 / (1.0 - B1 ** STEP)
    v_hat = v2 / (1.0 - B2 ** STEP)
    return -LR * (m_hat / (jnp.sqrt(v_hat) + EPS_ADAM) + WD * w), m2, v2


def _reduce_adamw(ws, grads, from_sibling, from_chips, idx, ms, vs, name):
    n = len(ws)
    nb = 2
    tiles = [w.shape[0] // nb for w in ws]
    for w, g, s, c in zip(ws, grads, from_sibling, from_chips):
        r = w.shape[0]
        assert g.shape == (N_DEV * r, D) and s.shape == (4 * r, D) and c.shape == (3 * r, D)

    def body(i_ref, *refs):
        del i_ref
        ins, outs = refs[:8 * n], refs[8 * n:]
        for a in range(n):
            w_ref, p_ref, s_ref, r0_ref, r1_ref, r2_ref, m_ref, v_ref = ins[8 * a:8 * a + 8]
            g_ref, d_ref, nm_ref, nv_ref = outs[4 * a:4 * a + 4]
            g = p_ref[...].astype(F32) + s_ref[...].astype(F32)
            g = ((g + r0_ref[...].astype(F32)) + r1_ref[...].astype(F32)) + r2_ref[...].astype(F32)
            g_ref[...] = g
            d_ref[...], nm_ref[...], nv_ref[...] = _adamw_math(w_ref[...], g, m_ref[...], v_ref[...])

    in_specs, out_specs, operands, out_shape = [], [], [], []
    for a, tr in enumerate(tiles):
        own = pl.BlockSpec((tr, D), lambda i, i_ref: (i, 0))
        in_specs += [own, pl.BlockSpec((tr, D), lambda i, i_ref: (i_ref[0] * nb + i, 0)),
                     pl.BlockSpec((tr, D), lambda i, i_ref: (i_ref[1] * nb + i, 0))]
        in_specs += [pl.BlockSpec((tr, D), lambda i, i_ref, j=j: (j * nb + i, 0)) for j in range(3)] + [own, own]
        operands += [ws[a], grads[a], from_sibling[a], from_chips[a], from_chips[a], from_chips[a], ms[a], vs[a]]
        out_specs += [own] * 4
        out_shape += [_sds(ws[a].shape, F32)] * 4
    grid_spec = pltpu.PrefetchScalarGridSpec(num_scalar_prefetch=1, grid=(nb,), in_specs=in_specs, out_specs=out_specs)
    outs = _call(body, name=name, grid_spec=grid_spec, out_shape=out_shape, compiler_params=_params("parallel"))(idx, *operands)
    return [tuple(outs[4 * a:4 * a + 4]) for a in range(n)]


SMALL_ROWS = 8


def _small_all_reduce(pack, name, after=()):
    def body(p_ref, *rest):
        tot_ref, loss_ref, gath, send_sems, recv_sems = rest[len(after):]
        x, y, c = _position()
        me_id = 4 * x + 2 * y + c
        gath[me_id] = p_ref[...]
        copies = []
        for k in range(1, N_DEV):
            peer = tuple(1 - v if (k >> b) & 1 else v for v, b in ((x, 2), (y, 1), (c, 0)))
            cp = pltpu.make_async_remote_copy(src_ref=p_ref, dst_ref=gath.at[me_id], send_sem=send_sems.at[k - 1],
                                              recv_sem=recv_sems.at[k - 1], device_id=peer, device_id_type=MESH)
            cp.start()
            copies.append(cp)
        for cp in copies:
            cp.wait_recv()
        for cp in copies:
            cp.wait_send()
        tot = gath[0]
        for d in range(1, N_DEV):
            tot = tot + gath[d]
        tot_ref[...] = tot
        loss_ref[...] = jnp.full((1, 128), (0.5 / D) * jnp.sum(tot[SMALL_ROWS - 1:SMALL_ROWS, :]), F32)

    vm = pl.BlockSpec(memory_space=pltpu.VMEM)
    return _call(
        body, name=name, in_specs=[vm] + [HBM_SPEC] * len(after), out_specs=[vm, vm],
        out_shape=[_sds((SMALL_ROWS, D), F32), _sds((1, 128), F32)],
        scratch_shapes=[pltpu.VMEM((N_DEV, SMALL_ROWS, D), F32), pltpu.SemaphoreType.DMA((N_DEV - 1,)),
                        pltpu.SemaphoreType.DMA((N_DEV - 1,))],
    )(pack, *after)


def _adamw_small(ws, gs, ms, vs, name):
    n = len(ws)

    def body(*refs):
        for a in range(n):
            w_ref, g_ref, m_ref, v_ref = (refs[k * n + a] for k in range(4))
            d_ref, nm_ref, nv_ref = (refs[(4 + k) * n + a] for k in range(3))
            d_ref[...], nm_ref[...], nv_ref[...] = _adamw_math(w_ref[...], g_ref[...], m_ref[...], v_ref[...])

    vm = pl.BlockSpec(memory_space=pltpu.VMEM)
    outs = _call(body, name=name, in_specs=[vm] * (4 * n), out_specs=[vm] * (3 * n),
                 out_shape=[_sds(w.shape, F32) for w in ws] * 3)(*ws, *gs, *ms, *vs)
    return [(outs[a], outs[n + a], outs[2 * n + a]) for a in range(n)]


def kernel(x, g_mix, w_in, conv_w, attn_sinks, w_conv_out, w_attn_out, w_o, g_ffn, w_gate_up, w_down, g_final, loss_target, m_g_mix, m_w_in, m_conv_w, m_attn_sinks, m_w_conv_out, m_w_attn_out, m_w_o, m_g_ffn, m_w_gate_up, m_w_down, m_g_final, v_g_mix, v_w_in, v_conv_w, v_attn_sinks, v_w_conv_out, v_w_attn_out, v_w_o, v_g_ffn, v_w_gate_up, v_w_down, v_g_final):
    cx, cy, cc = _position()
    chip = 2 * cx + cy
    partial_idx = jnp.stack([cc, 2 * (1 - cx) + cy, 2 * cx + (1 - cy), 2 * (1 - cx) + (1 - cy)]).astype(jnp.int32)
    own_idx = jnp.stack([2 * chip + cc, chip]).astype(jnp.int32)
    me = 4 * cx + 2 * cy + cc

    me_idx = jnp.reshape(me, (1,)).astype(jnp.int32)
    first = _place([jnp.transpose(w_in[0]), jnp.pad(conv_w[0], ((0, 5), (0, 0)))], me_idx, (BF16, F32), "place_in")
    (to_near,), first, token_in = _gather_phase(first, [], [_own_to_near], (), "gather_in_start")
    gather_tokens = (token_in,)

    class Gathered:
        def __init__(self):
            self.state = {}

        def begin(self, group, after):
            if group == "in":
                (near, relay), bufs, token = _gather_phase(
                    first, [(*to_near, 3, 3)], [_near_to_sibling, _relay_diagonal], after, "gather_in_relay")
                later = [_place([w], me_idx, (BF16,), "place_" + k, after=(token,))[0] for k, w in (
                    ("w_conv_out", w_conv_out[0]), ("w_attn_out", w_attn_out[0]), ("w_o", w_o[0]),
                    ("w_gate_up", jnp.transpose(w_gate_up[0])), ("w_down", w_down[0]))]
                (sems_mix, sems_ffn), later, token_later = _gather_start(later, [[0, 1, 2], [3, 4]], "gather_start_later")
                self.state.update({"in": (near, relay, bufs), "mix": (sems_mix, later[:3]), "ffn": (sems_ffn, later[3:])})
                return (token_later,)
            (send_sems, recv_sems), group_bufs = self.state[group]
            send2, recv2, group_bufs, token = _gather_forward(send_sems, recv_sems, group_bufs, after, "gather_forward_" + group)
            self.state[group] = ((send2, recv2), group_bufs)
            return (token,)

        def end(self, group, after):
            if group == "in":
                near, relay, bufs = self.state[group]
                (last,), bufs, token = _gather_phase(bufs, [(*relay, 1, 1)], [_diagonal_to_sibling], after, "gather_in_last")
                _, full, _ = _gather_phase(bufs, [(*near, 2, 2), (*last, 1, 1)], [], (token,), "gather_in_done")
                return full[0], jnp.transpose(full[1].reshape(N_DEV, 8, 128)[:, :3, :], (1, 0, 2)).reshape(3, D)
            (send2, recv2), group_bufs = self.state[group]
            return _gather_done(send2, recv2, group_bufs, after, "gather_done_" + group)

    in_flight, own_pieces = {}, {}

    transposed = ("w_in", "w_gate_up")

    def as2d(k, a):
        if k in transposed:
            return jnp.transpose(a[0])
        return a[None] if a.ndim == 1 else (a[0] if a.ndim == 3 else a)

    w_all = {"g_mix": g_mix, "w_in": w_in, "conv_w": conv_w, "attn_sinks": attn_sinks, "w_conv_out": w_conv_out,
             "w_attn_out": w_attn_out, "w_o": w_o, "g_ffn": g_ffn, "w_gate_up": w_gate_up, "w_down": w_down, "g_final": g_final}
    m_all = {"g_mix": m_g_mix, "w_in": m_w_in, "conv_w": m_conv_w, "attn_sinks": m_attn_sinks, "w_conv_out": m_w_conv_out,
             "w_attn_out": m_w_attn_out, "w_o": m_w_o, "g_ffn": m_g_ffn, "w_gate_up": m_w_gate_up, "w_down": m_w_down,
             "g_final": m_g_final}
    v_all = {"g_mix": v_g_mix, "w_in": v_w_in, "conv_w": v_conv_w, "attn_sinks": v_attn_sinks, "w_conv_out": v_w_conv_out,
             "w_attn_out": v_w_attn_out, "w_o": v_w_o, "g_ffn": v_g_ffn, "w_gate_up": v_w_gate_up, "w_down": v_w_down,
             "g_final": v_g_final}
    results = {}

    def record(k, *vals):
        results[k] = [(jnp.transpose(val) if k in transposed else val).reshape(w_all[k].shape) for val in vals]

    def update(group, names, grads, from_sibling, from_chips):
        outs = _reduce_adamw([as2d(k, w_all[k]) for k in names], grads, from_sibling, from_chips, own_idx,
                             [as2d(k, m_all[k]) for k in names], [as2d(k, v_all[k]) for k in names], "adamw_" + group)
        for k, vals in zip(names, outs):
            record(k, *vals)
        return tuple(vals[2] for vals in outs)

    def update_small(grads):
        keys = list(grads)
        outs = _adamw_small([as2d(k, w_all[k]) for k in keys], [grads[k] for k in keys], [as2d(k, m_all[k]) for k in keys],
                            [as2d(k, v_all[k]) for k in keys], "adamw_small")
        for k, (d, nm, nv) in zip(keys, outs):
            record(k, grads[k], d, nm, nv)
        return tuple(nm for _, nm, _ in outs)

    kernel_name = {"win_t": "w_in", "wgu_t": "w_gate_up", "wd": "w_down", "wco": "w_conv_out", "wao": "w_attn_out", "wo": "w_o"}

    def finish(group, after):
        keys, send_sems, recv_sems, parts, from_chips = in_flight[group]
        _, from_chips = _exchange_wait(send_sems, recv_sems, parts, from_chips, after, "rs_chips_wait_" + group)
        grads, from_sibling = own_pieces[group]
        return update(group, [kernel_name[k] for k in keys], grads, from_sibling, from_chips)

    class Reducer:
        def start(self, group, gdict):
            keys, glist = list(gdict), list(gdict.values())
            send_sems, recv_sems, glist, lands, token = _exchange_start(glist, N_DEV, _to_sibling, "rs_sibling_start_" + group)
            in_flight[group] = (keys, send_sems, recv_sems, glist, lands)
            return (token,)

        def middle(self, group, after):
            keys, send_sems, recv_sems, glist, lands = in_flight[group]
            if group == "in":
                after = finish("ffn", after)
            glist, lands = _exchange_wait(send_sems, recv_sems, glist, lands, after, "rs_sibling_wait_" + group)
            parts = _chip_partial(glist, lands, partial_idx, "chip_partial_" + group)
            send_sems, recv_sems, parts, from_chips, token = _exchange_start(parts, 4, _to_chips, "rs_chips_start_" + group)
            in_flight[group] = (keys, send_sems, recv_sems, parts, from_chips)
            own_pieces[group] = (glist, lands)
            return (token,)

    dx, _, small = _local_step(x[0], loss_target[0], g_mix, g_ffn, g_final[None], attn_sinks, Gathered(),
                               reducer=Reducer(), after=gather_tokens)
    after = finish("mix", (dx,))

    sinks_row = jnp.pad(small["sinks"], ((0, 0), (0, D - 128)))
    pack = jnp.concatenate([small["g_mix"], small["g_ffn"], small["g_final"], small["conv_w"], sinks_row, small["lossvec"]], axis=0)
    tot, loss_row = _small_all_reduce(pack, "small_all_reduce", after=after)
    loss = loss_row[0, 0]
    g_small = {
        "g_mix": tot[0:1], "g_ffn": tot[1:2], "g_final": tot[2:3],
        "conv_w": lax.dynamic_slice(tot, (3, me * 128), (3, 128)), "attn_sinks": tot[6:7, :N_HEADS],
    }
    finish("in", update_small(g_small))

    order = ["g_mix", "w_in", "conv_w", "attn_sinks", "w_conv_out", "w_attn_out", "w_o", "g_ffn", "w_gate_up", "w_down", "g_final"]
    return (loss, dx[None], *[results[k][i] for i in range(4) for k in order])
```

```python
import functools
import math

import jax
import jax.numpy as jnp
from jax import lax
from jax.experimental import pallas as pl
from jax.experimental.pallas import tpu as pltpu

F32 = jnp.float32
BF16 = jnp.bfloat16

D = 1024
HEAD_DIM = 64
N_HEADS = 16
N_KV = 4
GROUP = N_HEADS // N_KV
D_KV = N_KV * HEAD_DIM
BLOCK = 128
ROT_DIM = HEAD_DIM // 4
ROPE_THETA = 500000.0
ATTN_SCALE = 1.0 / math.sqrt(HEAD_DIM)
NEG_INF = -1e30
D_FF = 2816
N_IN = 6656
EPS = 1e-5
C_CB, C_CC, C_CX, C_Q, C_K, C_V, C_GC, C_GA = 0, 1024, 2048, 3072, 4096, 4352, 4608, 5632

LR, B1, B2, EPS_ADAM, WD, STEP = 0.001, 0.9, 0.999, 1e-08, 0.01, 10

N_DEV = 8
MESH = pl.DeviceIdType.MESH
VMEM_LIMIT = 56 * 1024 * 1024

NN = (((1,), (0,)), ((), ()))
NT = (((1,), (1,)), ((), ()))
TN = (((0,), (0,)), ((), ()))
HBM_SPEC = pl.BlockSpec(memory_space=pl.ANY)
ROW_SPLIT = 4


def _call(body, **kw):
    return pl.pallas_call(body, **kw)


def _params(*sem):
    return pltpu.CompilerParams(dimension_semantics=sem, vmem_limit_bytes=VMEM_LIMIT)


def _sds(shape, dtype):
    return jax.ShapeDtypeStruct(shape, dtype)


def _matmul(a, b, *, mode, tm, tn, tk, out_dtype, name, res=None, after=()):
    parts = list(a) if isinstance(a, (list, tuple)) else [a]
    rows_a = parts[0].shape[0]
    cols_a = sum(p.shape[1] for p in parts)
    if mode == "nn":
        (m, kk), (_, n), dims = (rows_a, cols_a), b.shape, NN
    elif mode == "nt":
        (m, kk), (n, _), dims = (rows_a, cols_a), b.shape, NT
    else:
        (kk, m), (_, n), dims = (rows_a, cols_a), b.shape, TN
    tm, tn, tk = min(tm, m), min(tn, n), min(tk, kk)
    assert m % tm == 0 and n % tn == 0 and kk % tk == 0, (name, m, n, kk, tm, tn, tk)
    nk = kk // tk
    split_axis, width = (2, tk) if mode == "nn" else (0, tm)
    assert len(parts) == 1 or mode in ("nn", "tn")
    assert len(parts) == 1 or all(p.shape[1] % width == 0 for p in parts), (name, width)
    counts = [p.shape[1] // width for p in parts]
    starts = [sum(counts[:p]) for p in range(len(parts))]

    def a_spec(p):
        def col(t):
            return jnp.clip(t - starts[p], 0, counts[p] - 1) if len(parts) > 1 else t

        if mode == "tn":
            return pl.BlockSpec((tk, tm), lambda i, j, k: (k, col(i)))
        return pl.BlockSpec((tm, tk), lambda i, j, k: (i, col(k)))

    if mode == "nt":
        b_spec = pl.BlockSpec((tn, tk), lambda i, j, k: (j, k))
    else:
        b_spec = pl.BlockSpec((tk, tn), lambda i, j, k: (k, j))
    o_spec = pl.BlockSpec((tm, tn), lambda i, j, k: (i, j))
    has_res = res is not None
    n_parts = len(parts)
    unit = 128 if mode == "tn" else 16
    split = ROW_SPLIT if tm % (ROW_SPLIT * unit) == 0 else 1

    def body(*refs):
        a_refs, b_ref = refs[:n_parts], refs[n_parts]
        r_ref = refs[n_parts + 1] if has_res else None
        o_ref = refs[n_parts + 1 + has_res + len(after)]
        k = pl.program_id(2)

        acc_ref = refs[-1] if nk > 1 else None

        def step(a_ref):
            def matmul(rows):
                a_blk = a_ref[:, rows] if mode == "tn" else a_ref[rows, :]
                return lax.dot_general(a_blk, b_ref[...], dims, preferred_element_type=F32)

            def finish(rows, part):
                if nk > 1:
                    acc_ref[rows, :] += part
                else:
                    o_ref[rows, :] = (part + r_ref[rows, :] if has_res else part).astype(o_ref.dtype)

            _row_pipeline(tm, matmul, finish, split)

        if nk > 1:
            @pl.when(k == 0)
            def _():
                acc_ref[...] = jnp.zeros_like(acc_ref)

        if n_parts == 1:
            step(a_refs[0])
        else:
            t = pl.program_id(split_axis)
            for p in range(n_parts):
                pl.when((t >= starts[p]) & (t < starts[p] + counts[p]))(functools.partial(step, a_refs[p]))

        if nk > 1:
            @pl.when(k == nk - 1)
            def _():
                o_ref[...] = (acc_ref[...] + r_ref[...] if has_res else acc_ref[...]).astype(o_ref.dtype)

    ins = parts + [b] + ([res] if has_res else []) + list(after)
    in_specs = [a_spec(p) for p in range(n_parts)] + [b_spec] + ([o_spec] if has_res else []) + [HBM_SPEC] * len(after)
    scratch = [] if nk == 1 else [pltpu.VMEM((tm, tn), F32)]
    return _call(
        body, name=name, grid=(m // tm, n // tn, nk), in_specs=in_specs, out_specs=o_spec,
        out_shape=_sds((m, n), out_dtype), scratch_shapes=scratch,
        compiler_params=_params("parallel", "parallel", "arbitrary"),
    )(*ins)


def _matmul_group(a_group, b_group, *, mode, tm, tn, out_dtype, name, after=()):
    a0, b0 = a_group[0], b_group[0]
    a_pair, b_pair, count = a_group, b_group, len(a_group)
    if mode == "nn":
        (m, kk), (_, n), dims = a0.shape, b0.shape, NN
    elif mode == "nt":
        (m, kk), (n, _), dims = a0.shape, b0.shape, NT
    else:
        (kk, m), (_, n), dims = a0.shape, b0.shape, TN
    assert all(a.shape == a0.shape for a in a_pair) and all(b.shape == b0.shape for b in b_pair)
    tm, tn = min(tm, m), min(tn, n)
    assert m % tm == 0 and n % tn == 0, (name, m, n, tm, tn)
    a_spec = pl.BlockSpec((kk, tm), lambda i, j: (0, i)) if mode == "tn" else pl.BlockSpec((tm, kk), lambda i, j: (i, 0))
    b_spec = pl.BlockSpec((tn, kk), lambda i, j: (j, 0)) if mode == "nt" else pl.BlockSpec((kk, tn), lambda i, j: (0, j))
    o_spec = pl.BlockSpec((tm, tn), lambda i, j: (i, j))
    unit = 128 if mode == "tn" else 16
    split = ROW_SPLIT if tm % (ROW_SPLIT * unit) == 0 else 1

    def body(*refs):
        a_refs, b_refs, o_refs = refs[:count], refs[count:2 * count], refs[2 * count + len(after):]

        def matmul(rows):
            return tuple(lax.dot_general(a_ref[:, rows] if mode == "tn" else a_ref[rows, :], b_ref[...], dims,
                                         preferred_element_type=F32) for a_ref, b_ref in zip(a_refs, b_refs))

        def finish(rows, parts):
            for o_ref, part in zip(o_refs, parts):
                o_ref[rows, :] = part.astype(out_dtype)

        _row_pipeline(tm, matmul, finish, split)

    return _call(
        body, name=name, grid=(m // tm, n // tn), in_specs=[a_spec] * count + [b_spec] * count + [HBM_SPEC] * len(after),
        out_specs=[o_spec] * count, out_shape=[_sds((m, n), out_dtype)] * count,
        compiler_params=_params("parallel", "parallel"),
    )(*a_pair, *b_pair, *after)


def _row_tile(s):
    return min(512, s)


def _rms_fwd(x, g, name, after=()):
    s = x.shape[0]
    tm = _row_tile(s)

    def body(x_ref, g_ref, *rest):
        h_ref = rest[-1]
        xv = x_ref[...]
        r = lax.rsqrt(jnp.mean(xv * xv, axis=-1, keepdims=True) + EPS)
        h_ref[...] = (xv * r * g_ref[...]).astype(BF16)

    row = pl.BlockSpec((tm, D), lambda i: (i, 0))
    return _call(
        body, name=name, grid=(s // tm,), in_specs=[row, pl.BlockSpec((1, D), lambda i: (0, 0))] + [HBM_SPEC] * len(after),
        out_specs=row, out_shape=_sds((s, D), BF16), compiler_params=_params("parallel"),
    )(x, g, *after)


def _rms_bwd(dh, x, g, dres, name, after=()):
    s = x.shape[0]
    tm = _row_tile(s)

    def body(dh_ref, x_ref, g_ref, dres_ref, *rest):
        dx_ref, dxb_ref, dg_ref = rest[len(after):]
        xv = x_ref[...]
        r = lax.rsqrt(jnp.mean(xv * xv, axis=-1, keepdims=True) + EPS)
        xh = xv * r
        dhv = dh_ref[...].astype(F32)
        dyg = dhv * g_ref[...]
        dx = dres_ref[...] + r * (dyg - xh * jnp.mean(dyg * xh, axis=-1, keepdims=True))
        dx_ref[...] = dx
        dxb_ref[...] = dx.astype(BF16)
        part = jnp.sum(dhv * xh, axis=0, keepdims=True)

        @pl.when(pl.program_id(0) == 0)
        def _():
            dg_ref[...] = part

        @pl.when(pl.program_id(0) > 0)
        def _():
            dg_ref[...] += part

    row = pl.BlockSpec((tm, D), lambda i: (i, 0))
    vec = pl.BlockSpec((1, D), lambda i: (0, 0))
    return _call(
        body, name=name, grid=(s // tm,), in_specs=[row, row, vec, row] + [HBM_SPEC] * len(after), out_specs=[row, row, vec],
        out_shape=[_sds((s, D), F32), _sds((s, D), BF16), _sds((1, D), F32)],
        compiler_params=_params("arbitrary"),
    )(dh, x, g, dres, *after)


def _loss_head(x2, g, tgt, name):
    s = x2.shape[0]
    tm = _row_tile(s)

    def body(x_ref, g_ref, t_ref, dx_ref, dxb_ref, dg_ref, l_ref):
        xv = x_ref[...]
        gv = g_ref[...]
        r = lax.rsqrt(jnp.mean(xv * xv, axis=-1, keepdims=True) + EPS)
        xh = xv * r
        err = xh * gv - t_ref[...]
        dy = err * (1.0 / D)
        dyg = dy * gv
        dx = r * (dyg - xh * jnp.mean(dyg * xh, axis=-1, keepdims=True))
        dx_ref[...] = dx
        dxb_ref[...] = dx.astype(BF16)
        dg_part = jnp.sum(dy * xh, axis=0, keepdims=True)
        l_part = jnp.sum(err * err, axis=0, keepdims=True)

        @pl.when(pl.program_id(0) == 0)
        def _():
            dg_ref[...] = dg_part
            l_ref[...] = l_part

        @pl.when(pl.program_id(0) > 0)
        def _():
            dg_ref[...] += dg_part
            l_ref[...] += l_part

    row = pl.BlockSpec((tm, D), lambda i: (i, 0))
    vec = pl.BlockSpec((1, D), lambda i: (0, 0))
    return _call(
        body, name=name, grid=(s // tm,), in_specs=[row, vec, row], out_specs=[row, row, vec, vec],
        out_shape=[_sds((s, D), F32), _sds((s, D), BF16), _sds((1, D), F32), _sds((1, D), F32)],
        compiler_params=_params("arbitrary"),
    )(x2, g, tgt)


CONV_TC = 256


def _shift_down(u, k, rows):
    return jnp.where(rows >= k, pltpu.roll(u, k, 0), 0.0)


def _shift_up(u, k, rows, s):
    return jnp.where(rows < s - k, pltpu.roll(u, s - k, 0), 0.0)


def _conv_specs(s):
    nb = D // CONV_TC

    def col(c0):
        return pl.BlockSpec((s, CONV_TC), lambda j, c0=c0: (0, c0 // CONV_TC + j))

    return nb, col


def _conv_fwd(proj, conv_w, name):
    s = proj.shape[0]
    nb, col = _conv_specs(s)

    def body(cb_ref, cc_ref, cx_ref, w_ref, y_ref):
        rows = lax.broadcasted_iota(jnp.int32, (s, CONV_TC), 0)
        u = cc_ref[...].astype(F32) * cx_ref[...].astype(F32)
        w = w_ref[...]
        c = w[0:1] * _shift_down(u, 2, rows) + w[1:2] * _shift_down(u, 1, rows) + w[2:3] * u
        y_ref[...] = (cb_ref[...].astype(F32) * c).astype(BF16)

    return _call(
        body, name=name, grid=(nb,),
        in_specs=[col(C_CB), col(C_CC), col(C_CX), pl.BlockSpec((3, CONV_TC), lambda j: (0, j))],
        out_specs=pl.BlockSpec((s, CONV_TC), lambda j: (0, j)), out_shape=_sds((s, D), BF16),
        compiler_params=_params("parallel"),
    )(proj, proj, proj, conv_w)


def _write_behind(t, nt, buf, sems, tiles, window, where):
    slot = t % 2

    def copies(sl, at):
        return [pltpu.make_async_copy(buf.at[sl, p], window(p, at), sems.at[sl, p]) for p in range(len(tiles))]

    @pl.when(t >= 2)
    def _():
        for cp in copies(slot, where):
            cp.wait()

    for p, tile in enumerate(tiles):
        buf[slot, p] = tile
    started = copies(slot, where)
    for cp in started:
        cp.start()

    @pl.when(t == nt - 1)
    def _():
        for cp in started:
            cp.wait()
        if nt > 1:
            for cp in copies(1 - slot, where):
                cp.wait()


def _conv_bwd(dy, proj, conv_w, dproj, name, after=()):
    s = proj.shape[0]
    nb, col = _conv_specs(s)

    def body(dy_ref, cb_ref, cc_ref, cx_ref, w_ref, *rest):
        dproj_ref, dw_ref, buf, sems = rest[1 + len(after):]
        j = pl.program_id(0)
        rows = lax.broadcasted_iota(jnp.int32, (s, CONV_TC), 0)
        cc = cc_ref[...].astype(F32)
        cx = cx_ref[...].astype(F32)
        u = cc * cx
        u1 = _shift_down(u, 1, rows)
        u2 = _shift_down(u, 2, rows)
        w = w_ref[...]
        c = w[0:1] * u2 + w[1:2] * u1 + w[2:3] * u
        dyv = dy_ref[...].astype(F32)
        dc = dyv * cb_ref[...].astype(F32)
        du = w[2:3] * dc + w[1:2] * _shift_up(dc, 1, rows, s) + w[0:1] * _shift_up(dc, 2, rows, s)

        def window(p, jj):
            start = pl.multiple_of((C_CB, C_CC, C_CX)[p] + jj * CONV_TC, CONV_TC)
            return dproj_ref.at[:, pl.ds(start, CONV_TC)]

        tiles = ((dyv * c).astype(BF16), (du * cx).astype(BF16), (du * cc).astype(BF16))
        _write_behind(j * 0, 1, buf, sems, tiles, window, j)
        dw_ref[...] = jnp.concatenate(
            [jnp.sum(dc * u2, axis=0, keepdims=True), jnp.sum(dc * u1, axis=0, keepdims=True),
             jnp.sum(dc * u, axis=0, keepdims=True)], axis=0)

    return _call(
        body, name=name, grid=(nb,),
        in_specs=[pl.BlockSpec((s, CONV_TC), lambda j: (0, j)), col(C_CB), col(C_CC), col(C_CX),
                  pl.BlockSpec((3, CONV_TC), lambda j: (0, j))] + [HBM_SPEC] * (1 + len(after)),
        out_specs=[pl.BlockSpec(memory_space=pl.ANY), pl.BlockSpec((3, CONV_TC), lambda j: (0, j))],
        out_shape=[_sds((s, N_IN), BF16), _sds((3, D), F32)],
        scratch_shapes=[pltpu.VMEM((1, 3, s, CONV_TC), BF16), pltpu.SemaphoreType.DMA((1, 3))],
        input_output_aliases={5: 0}, compiler_params=_params("arbitrary"),
    )(dy, proj, proj, proj, conv_w, dproj, *after)


def _rope_tables(s):
    half = ROT_DIM // 2
    inv_freq = ROPE_THETA ** (-jnp.arange(0, ROT_DIM, 2, dtype=F32) / ROT_DIM)
    inv64 = jnp.concatenate([inv_freq, inv_freq, jnp.zeros((HEAD_DIM - ROT_DIM,), F32)])
    ang = jnp.arange(s, dtype=F32)[:, None] * jnp.concatenate([inv64, inv64])[None, :]
    d = lax.broadcasted_iota(jnp.int32, (s, 128), 1) % HEAD_DIM
    cos, sin = jnp.cos(ang), jnp.sin(ang)
    c = jnp.where(d < ROT_DIM, cos, 1.0)
    a = jnp.where(d < half, -sin, 0.0)
    b = jnp.where((d >= half) & (d < ROT_DIM), sin, 0.0)
    return jnp.concatenate([c, a, b], axis=1)


def _rope(x, tab):
    c, a, b = tab[:, 0:128], tab[:, 128:256], tab[:, 256:384]
    outs = []
    for i in range(x.shape[1] // 128):
        xc = x[:, i * 128:(i + 1) * 128]
        outs.append(xc * c + pltpu.roll(xc, 120, 1) * a + pltpu.roll(xc, 8, 1) * b)
    return outs[0] if len(outs) == 1 else jnp.concatenate(outs, axis=1)


def _rope_t(dx, tab):
    c, a, b = tab[:, 0:128], tab[:, 128:256], tab[:, 256:384]
    outs = []
    for i in range(dx.shape[1] // 128):
        dc = dx[:, i * 128:(i + 1) * 128]
        outs.append(dc * c + pltpu.roll(dc * a, 8, 1) + pltpu.roll(dc * b, 120, 1))
    return outs[0] if len(outs) == 1 else jnp.concatenate(outs, axis=1)


def _attn_in_specs():
    prev = lambda n: jnp.maximum(n - 1, 0)
    return [
        pl.BlockSpec((BLOCK, D), lambda n: (n, C_Q // D)),
        pl.BlockSpec((BLOCK, D_KV), lambda n: (n, C_K // D_KV)),
        pl.BlockSpec((BLOCK, D_KV), lambda n: (prev(n), C_K // D_KV)),
        pl.BlockSpec((BLOCK, D_KV), lambda n: (n, C_V // D_KV)),
        pl.BlockSpec((BLOCK, D_KV), lambda n: (prev(n), C_V // D_KV)),
        pl.BlockSpec((BLOCK, 384), lambda n: (n, 0)),
        pl.BlockSpec((BLOCK, 384), lambda n: (prev(n), 0)),
        pl.BlockSpec(memory_space=pltpu.SMEM),
    ]


HALF = HEAD_DIM
N_CHUNK = D // 128


def _swa_bias(n):
    qi = lax.broadcasted_iota(jnp.int32, (BLOCK, 2 * BLOCK), 0)
    kj = lax.broadcasted_iota(jnp.int32, (BLOCK, 2 * BLOCK), 1)
    rel = qi + BLOCK - kj
    valid = (rel >= 0) & (rel < BLOCK) & ((kj >= BLOCK) | (n > 0))
    return jnp.where(valid, 0.0, NEG_INF)


def _halves(x):
    lo = lax.broadcasted_iota(jnp.int32, x.shape, 1) < HALF
    return jnp.where(lo, x, 0.0).astype(BF16), jnp.where(lo, 0.0, x).astype(BF16)


def _dup_heads(x):
    out = []
    for pair in range(N_KV // 2):
        xc = x[:, pair * 128:(pair + 1) * 128]
        xr = pltpu.roll(xc, HALF, 1)
        lo = lax.broadcasted_iota(jnp.int32, xc.shape, 1) < HALF
        out += [jnp.where(lo, xc, xr), jnp.where(lo, xr, xc)]
    return out


def _swa_load(q_ref, kc_ref, kp_ref, vc_ref, vp_ref, tc_ref, tp_ref):
    qf = _rope(q_ref[...].astype(F32), tc_ref[...]) * ATTN_SCALE
    q_halves = [_halves(qf[:, c * 128:(c + 1) * 128]) for c in range(N_CHUNK)]
    kf = jnp.concatenate([_rope(kp_ref[...].astype(F32), tp_ref[...]), _rope(kc_ref[...].astype(F32), tc_ref[...])], axis=0)
    vf = jnp.concatenate([vp_ref[...], vc_ref[...]], axis=0).astype(F32)
    return q_halves, _dup_heads(kf), _dup_heads(vf)


def _swa_probs(qh, kk, bias, sink):
    s = lax.dot_general(qh, kk, NT, preferred_element_type=F32) + bias
    m = jnp.maximum(jnp.max(jnp.maximum(s[:, :BLOCK], s[:, BLOCK:]), axis=1, keepdims=True), sink)
    return jnp.exp(s - m), m


def _swa_fwd(proj, tab, sinks, name, after=()):
    s = proj.shape[0]

    def body(q_ref, kc_ref, kp_ref, vc_ref, vp_ref, tc_ref, tp_ref, sink_ref, *rest):
        o_ref = rest[-1]
        n = pl.program_id(0)
        q_halves, kdup, vdup = _swa_load(q_ref, kc_ref, kp_ref, vc_ref, vp_ref, tc_ref, tp_ref)
        bias = _swa_bias(n)
        ones = jnp.ones((2 * BLOCK, 128), BF16)
        kk = [k.astype(BF16) for k in kdup]
        vv = [[jnp.concatenate([v_half, ones], axis=1) for v_half in _halves(v)] for v in vdup]
        heads = [(c, half) for c in range(N_CHUNK) for half in range(2)]
        scores = [lax.dot_general(q_halves[c][half], kk[c // (GROUP // 2)], NT, preferred_element_type=F32)
                  for c, half in heads]
        probs = []
        for (c, half), sc in zip(heads, scores):
            sc = sc + bias
            m = jnp.maximum(jnp.max(jnp.maximum(sc[:, :BLOCK], sc[:, BLOCK:]), axis=1, keepdims=True), sink_ref[0, 2 * c + half])
            probs.append((jnp.exp(sc - m).astype(BF16), jnp.exp(sink_ref[0, 2 * c + half] - m)))
        outs = [lax.dot_general(e, vv[c // (GROUP // 2)][half], NN, preferred_element_type=F32)
                for (c, half), (e, _) in zip(heads, probs)]
        for c in range(N_CHUNK):
            parts = [outs[2 * c + half][:, :128] * (1.0 / (outs[2 * c + half][:, 128:] + probs[2 * c + half][1]))
                     for half in range(2)]
            o_ref[:, c * 128:(c + 1) * 128] = (parts[0] + parts[1]).astype(BF16)

    return _call(
        body, name=name, grid=(s // BLOCK,), in_specs=_attn_in_specs() + [HBM_SPEC] * len(after),
        out_specs=pl.BlockSpec((BLOCK, D), lambda n: (n, 0)), out_shape=_sds((s, D), BF16),
        compiler_params=_params("parallel"),
    )(proj, proj, proj, proj, proj, tab, tab, sinks, *after)


def _swa_bwd(do, proj, tab, sinks, dproj, name, after=()):
    s = proj.shape[0]
    nblk = s // BLOCK
    kv_of = lambda c: c // (GROUP // 2)

    def body(do_ref, q_ref, kc_ref, kp_ref, vc_ref, vp_ref, tc_ref, tp_ref, sink_ref, *rest):
        dproj_ref, dk_ref, dv_ref, ds_ref, dqout, dkbuf, dvbuf, sems = rest[1 + len(after):]
        n = pl.program_id(0)

        @pl.when(n == 0)
        def _():
            dk_ref[...] = jnp.zeros_like(dk_ref)
            dv_ref[...] = jnp.zeros_like(dv_ref)
            ds_ref[...] = jnp.zeros_like(ds_ref)

        q_halves, kdup, vdup = _swa_load(q_ref, kc_ref, kp_ref, vc_ref, vp_ref, tc_ref, tp_ref)
        dof = do_ref[...].astype(F32)
        do_halves = [_halves(dof[:, c * 128:(c + 1) * 128]) for c in range(N_CHUNK)]
        bias = _swa_bias(n)
        ones = jnp.ones((2 * BLOCK, 128), BF16)
        kk = [k.astype(BF16) for k in kdup]
        vv = [v.astype(BF16) for v in vdup]
        k_halves = [_halves(k) for k in kdup]
        heads = [(c, half) for c in range(N_CHUNK) for half in range(2)]
        lane_row = lax.broadcasted_iota(jnp.int32, (1, 128), 1)
        lo_kv = lax.broadcasted_iota(jnp.int32, (2 * BLOCK, 128), 1) < HALF
        scores = [lax.dot_general(q_halves[c][half], kk[kv_of(c)], NT, preferred_element_type=F32) for c, half in heads]
        dps = [lax.dot_general(do_halves[c][half], vv[kv_of(c)], NT, preferred_element_type=F32) for c, half in heads]
        exps = []
        for (c, half), sc in zip(heads, scores):
            sink = sink_ref[0, 2 * c + half]
            sc = sc + bias
            m = jnp.maximum(jnp.max(jnp.maximum(sc[:, :BLOCK], sc[:, BLOCK:]), axis=1, keepdims=True), sink)
            exps.append((jnp.exp(sc - m), jnp.exp(sink - m)))
        sums = [lax.dot_general(e.astype(BF16), ones, NN, preferred_element_type=F32) for e, _ in exps]
        dsink_row = jnp.zeros((1, 128), F32)
        dsb, pb = [], []
        for h, ((e, es), row_sum, dp) in enumerate(zip(exps, sums, dps)):
            inv = 1.0 / (row_sum + es)
            p = e * jnp.concatenate([inv, inv], axis=1)
            t = p * dp
            delta = jnp.sum(t, axis=1, keepdims=True)
            dsb.append((t - p * delta).astype(BF16))
            pb.append(p.astype(BF16))
            dsink = -jnp.sum(es * inv * delta, axis=0, keepdims=True)
            dsink_row = dsink_row + jnp.where(lane_row == h, dsink, 0.0)
        dq_parts = [lax.dot_general(d, k_halves[kv_of(c)][half], NN, preferred_element_type=F32) for (c, half), d in zip(heads, dsb)]
        dk_parts = [lax.dot_general(d, q_halves[c][half], TN, preferred_element_type=F32) for (c, half), d in zip(heads, dsb)]
        dv_parts = [lax.dot_general(p, do_halves[c][half], TN, preferred_element_type=F32) for (c, half), p in zip(heads, pb)]
        dq = jnp.concatenate([(dq_parts[2 * c] + dq_parts[2 * c + 1]) * ATTN_SCALE for c in range(N_CHUNK)], axis=1)

        def kv_sum(parts, hk):
            acc = (parts[GROUP * hk] + parts[GROUP * hk + 1]) + (parts[GROUP * hk + 2] + parts[GROUP * hk + 3])
            return acc + pltpu.roll(acc, HALF, 1)

        for pair in range(N_KV // 2):
            dkbuf[:, pair * 128:(pair + 1) * 128] = jnp.where(lo_kv, kv_sum(dk_parts, 2 * pair), kv_sum(dk_parts, 2 * pair + 1))
            dvbuf[:, pair * 128:(pair + 1) * 128] = jnp.where(lo_kv, kv_sum(dv_parts, 2 * pair), kv_sum(dv_parts, 2 * pair + 1))
        prev0 = pl.multiple_of(jnp.maximum(n - 1, 0) * BLOCK, BLOCK)
        cur0 = pl.multiple_of(n * BLOCK, BLOCK)

        @pl.when(n > 0)
        def _():
            dk_ref[pl.ds(prev0, BLOCK), :] += dkbuf[0:BLOCK, :]
            dv_ref[pl.ds(prev0, BLOCK), :] += dvbuf[0:BLOCK, :]

        dk_ref[pl.ds(cur0, BLOCK), :] += dkbuf[BLOCK:2 * BLOCK, :]
        dv_ref[pl.ds(cur0, BLOCK), :] += dvbuf[BLOCK:2 * BLOCK, :]
        ds_ref[...] += dsink_row

        def window(p, at):
            return dproj_ref.at[pl.ds(pl.multiple_of(at * BLOCK, BLOCK), BLOCK), pl.ds(C_Q, D)]

        _write_behind(n, nblk, dqout, sems, (_rope_t(dq, tc_ref[...]).astype(BF16),), window, n)

    blk = lambda w: pl.BlockSpec((BLOCK, w), lambda n: (n, 0))
    whole = lambda w: pl.BlockSpec((s, w), lambda n: (0, 0))
    n_in = 1 + len(_attn_in_specs())
    return _call(
        body, name=name, grid=(nblk,), in_specs=[blk(D)] + _attn_in_specs() + [HBM_SPEC] * (1 + len(after)),
        out_specs=[HBM_SPEC, whole(D_KV), whole(D_KV), pl.BlockSpec((1, 128), lambda n: (0, 0))],
        out_shape=[_sds((s, N_IN), BF16), _sds((s, D_KV), F32), _sds((s, D_KV), F32), _sds((1, 128), F32)],
        scratch_shapes=[pltpu.VMEM((2, 1, BLOCK, D), BF16), pltpu.VMEM((2 * BLOCK, D_KV), F32),
                        pltpu.VMEM((2 * BLOCK, D_KV), F32), pltpu.SemaphoreType.DMA((2, 1))],
        input_output_aliases={n_in: 0}, compiler_params=_params("arbitrary"),
    )(do, proj, proj, proj, proj, proj, tab, tab, sinks, dproj, *after)


def _kv_bwd(dkr, dv, tab, dproj, name):
    s = dkr.shape[0]
    tm = _row_tile(s)

    def body(dk_ref, dv_ref, t_ref, dproj_in, o_ref):
        del dproj_in
        o_ref[:, 0:D_KV] = _rope_t(dk_ref[...], t_ref[...]).astype(BF16)
        o_ref[:, D_KV:2 * D_KV] = dv_ref[...].astype(BF16)

    row = lambda w: pl.BlockSpec((tm, w), lambda i: (i, 0))
    return _call(
        body, name=name, grid=(s // tm,),
        in_specs=[row(D_KV), row(D_KV), row(384), pl.BlockSpec(memory_space=pl.ANY)],
        out_specs=pl.BlockSpec((tm, 2 * D_KV), lambda i: (i, C_K // (2 * D_KV))),
        out_shape=_sds((s, N_IN), BF16), input_output_aliases={3: 0}, compiler_params=_params("parallel"),
    )(dkr, dv, tab, dproj)


EW_TC = 512


def _sigmoid(x):
    return 0.5 * jnp.tanh(0.5 * x) + 0.5


def _branches_merge_fwd(conv_y, attn, wco, wao, proj, name):
    s = proj.shape[0]
    tm = min(2048, s)

    def body(y_ref, a_ref, wc_ref, wa_ref, gc_ref, ga_ref, co_ref, ao_ref, m_ref):
        def matmuls(rows):
            return (lax.dot_general(y_ref[rows, :], wc_ref[...], NN, preferred_element_type=F32),
                    lax.dot_general(a_ref[rows, :], wa_ref[...], NN, preferred_element_type=F32))

        def finish(rows, parts):
            co, ao = parts
            co_ref[rows, :] = co.astype(BF16)
            ao_ref[rows, :] = ao.astype(BF16)
            m_ref[rows, :] = (_sigmoid(gc_ref[rows, :].astype(F32)) * co + _sigmoid(ga_ref[rows, :].astype(F32)) * ao).astype(BF16)

        _row_pipeline(tm, matmuls, finish)

    act = pl.BlockSpec((tm, D), lambda i, j: (i, 0))
    wgt = pl.BlockSpec((D, EW_TC), lambda i, j: (0, j))
    tile = pl.BlockSpec((tm, EW_TC), lambda i, j: (i, j))
    return _call(
        body, name=name, grid=(s // tm, D // EW_TC),
        in_specs=[act, act, wgt, wgt, pl.BlockSpec((tm, EW_TC), lambda i, j: (i, C_GC // EW_TC + j)),
                  pl.BlockSpec((tm, EW_TC), lambda i, j: (i, C_GA // EW_TC + j))],
        out_specs=[tile, tile, tile], out_shape=[_sds((s, D), BF16)] * 3, compiler_params=_params("parallel", "parallel"),
    )(conv_y, attn, wco, wao, proj, proj)


def _wo_merge_bwd(dx1b, wo, proj, conv_out, attn_out, name, after=()):
    s = proj.shape[0]
    tm = min(1024, s)
    nj = D // EW_TC

    def body(dx_ref, w_ref, gc_ref, ga_ref, co_ref, ao_ref, *rest):
        dproj_ref, dco_ref, dao_ref, buf, sems = rest[len(after):]
        i, j = pl.program_id(0), pl.program_id(1)
        gate_c, gate_a = [], []

        def matmul(rows):
            return lax.dot_general(dx_ref[rows, :], w_ref[...], NT, preferred_element_type=F32)

        def finish(rows, dm):
            sc = _sigmoid(gc_ref[rows, :].astype(F32))
            sa = _sigmoid(ga_ref[rows, :].astype(F32))
            dco_ref[rows, :] = (dm * sc).astype(BF16)
            dao_ref[rows, :] = (dm * sa).astype(BF16)
            gate_c.append((dm * co_ref[rows, :].astype(F32) * sc * (1.0 - sc)).astype(BF16))
            gate_a.append((dm * ao_ref[rows, :].astype(F32) * sa * (1.0 - sa)).astype(BF16))

        _row_pipeline(tm, matmul, finish)

        def window(p, at):
            start = pl.multiple_of((C_GC, C_GA)[p] + at[1] * EW_TC, EW_TC)
            return dproj_ref.at[pl.ds(pl.multiple_of(at[0] * tm, tm), tm), pl.ds(start, EW_TC)]

        tiles = (jnp.concatenate(gate_c, axis=0), jnp.concatenate(gate_a, axis=0))
        _write_behind(i * nj + j, (s // tm) * nj, buf, sems, tiles, window, (i, j))

    tile = pl.BlockSpec((tm, EW_TC), lambda i, j: (i, j))
    return _call(
        body, name=name, grid=(s // tm, nj),
        in_specs=[pl.BlockSpec((tm, D), lambda i, j: (i, 0)), pl.BlockSpec((EW_TC, D), lambda i, j: (j, 0)),
                  pl.BlockSpec((tm, EW_TC), lambda i, j: (i, C_GC // EW_TC + j)),
                  pl.BlockSpec((tm, EW_TC), lambda i, j: (i, C_GA // EW_TC + j)), tile, tile] + [HBM_SPEC] * len(after),
        out_specs=[HBM_SPEC, tile, tile],
        out_shape=[_sds((s, N_IN), BF16), _sds((s, D), BF16), _sds((s, D), BF16)],
        scratch_shapes=[pltpu.VMEM((2, 2, tm, EW_TC), BF16), pltpu.SemaphoreType.DMA((2, 2))],
        compiler_params=_params("arbitrary", "arbitrary"),
    )(dx1b, wo, proj, proj, conv_out, attn_out, *after)


FF_TC = 256
FF_TM = 2048


def _row_pipeline(tm, matmul, finish, split=ROW_SPLIT):
    step = tm // split
    pending = None
    for r in range(split):
        rows = pl.ds(r * step, step)
        result = matmul(rows)
        if pending is not None:
            finish(*pending)
        pending = (rows, result)
    finish(*pending)


def _gate_up_fwd(h2, wgu_t, name):
    s = h2.shape[0]
    tm = min(FF_TM, s)
    nb = D_FF // FF_TC

    def body(h_ref, wg_ref, wu_ref, a_ref, g_ref, u_ref):
        def matmuls(rows):
            h = h_ref[rows, :]
            return (lax.dot_general(h, wg_ref[...], NT, preferred_element_type=F32),
                    lax.dot_general(h, wu_ref[...], NT, preferred_element_type=F32))

        def finish(rows, gu):
            g, u = gu
            a_ref[rows, :] = (g * _sigmoid(g) * u).astype(BF16)
            g_ref[rows, :] = g.astype(BF16)
            u_ref[rows, :] = u.astype(BF16)

        _row_pipeline(tm, matmuls, finish)

    tile = pl.BlockSpec((tm, FF_TC), lambda j, i: (i, j))
    return _call(
        body, name=name, grid=(nb, s // tm),
        in_specs=[pl.BlockSpec((tm, D), lambda j, i: (i, 0)), pl.BlockSpec((FF_TC, D), lambda j, i: (j, 0)),
                  pl.BlockSpec((FF_TC, D), lambda j, i: (nb + j, 0))],
        out_specs=[tile, tile, tile], out_shape=[_sds((s, D_FF), BF16)] * 3,
        compiler_params=_params("parallel", "parallel"),
    )(h2, wgu_t, wgu_t)


def _down_bwd_x(dx2b, wd, gate, up, name):
    s = dx2b.shape[0]
    tm = min(FF_TM, s)
    nb = D_FF // FF_TC

    def body(dx_ref, w_ref, g_ref, u_ref, dg_ref, du_ref):
        def matmul(rows):
            return lax.dot_general(dx_ref[rows, :], w_ref[...], NT, preferred_element_type=F32)

        def finish(rows, da):
            g = g_ref[rows, :].astype(F32)
            sg = _sigmoid(g)
            dg_ref[rows, :] = (da * u_ref[rows, :].astype(F32) * (sg * (1.0 + g * (1.0 - sg)))).astype(BF16)
            du_ref[rows, :] = (da * (g * sg)).astype(BF16)

        _row_pipeline(tm, matmul, finish)

    tile = pl.BlockSpec((tm, FF_TC), lambda j, i: (i, j))
    return _call(
        body, name=name, grid=(nb, s // tm),
        in_specs=[pl.BlockSpec((tm, D), lambda j, i: (i, 0)), pl.BlockSpec((FF_TC, D), lambda j, i: (j, 0)), tile, tile],
        out_specs=[tile, tile], out_shape=[_sds((s, D_FF), BF16)] * 2,
        compiler_params=_params("parallel", "parallel"),
    )(dx2b, wd, gate, up)


class _Weights:
    def __init__(self, **groups):
        self.groups = groups

    def begin(self, group, after):
        return ()

    def end(self, group, after):
        return self.groups[group]


class _NoReduce:
    def start(self, group, grads):
        return ()

    def middle(self, group, after):
        return ()


def _local_step(x, tgt, g_mix, g_ffn, g_final, sinks, weights, reducer=None, after=()):
    reducer = reducer or _NoReduce()
    s = x.shape[0]
    tab = _rope_tables(s)
    big = dict(tm=2048, tn=512, tk=1024)
    h1 = _rms_fwd(x, g_mix, "rms1_fwd", after=after)
    win_t, conv_w = weights.end("in", weights.begin("in", (h1,)))
    proj = _matmul(h1, win_t, mode="nt", out_dtype=BF16, name="proj_fwd", tm=2048, tn=512, tk=1024)
    attn = _swa_fwd(proj, tab, sinks, "attn_fwd", after=weights.begin("mix", (proj,)))
    wco, wao, wo = weights.end("mix", (attn,))
    conv_y = _conv_fwd(proj, conv_w, "conv_fwd")
    conv_out, attn_out, merged = _branches_merge_fwd(conv_y, attn, wco, wao, proj, "branch_out_fwd")
    x1 = _matmul(merged, wo, mode="nn", out_dtype=F32, name="wo_fwd", res=x, after=weights.begin("ffn", (merged,)), **big)
    h2 = _rms_fwd(x1, g_ffn, "rms2_fwd")
    wgu_t, wd = weights.end("ffn", (h2,))
    act, gate, up = _gate_up_fwd(h2, wgu_t, "gate_up_fwd")
    x2 = _matmul(act, wd, mode="nn", out_dtype=F32, name="down_fwd", res=x1, tm=1024, tn=512, tk=D_FF)
    dx2, dx2b, dg_final, lossvec = _loss_head(x2, g_final, tgt, "loss_head")
    dgate, dup = _down_bwd_x(dx2b, wd, gate, up, "down_bwd_x")
    g_wd = _matmul(act, dx2b, mode="tn", out_dtype=BF16, name="down_bwd_w", tm=1408, tn=1024, tk=2048)
    dh2 = _matmul([dgate, dup], wgu_t, mode="nn", out_dtype=BF16, name="gate_up_bwd_x", tm=1024, tn=1024, tk=1408)
    g_wgu_t = _matmul([dgate, dup], h2, mode="tn", out_dtype=BF16, name="gate_up_bwd_w", tm=1408, tn=1024, tk=2048)
    after_ffn = reducer.start("ffn", dict(wgu_t=g_wgu_t, wd=g_wd))
    dx1, dx1b, dg_ffn = _rms_bwd(dh2, x1, g_ffn, dx2, "rms2_bwd")
    dproj, dco, dao = _wo_merge_bwd(dx1b, wo, proj, conv_out, attn_out, "wo_bwd_x", after=after_ffn)
    after_ffn = reducer.middle("ffn", (dco,))
    dconv_y, dattn = _matmul_group((dco, dao), (wco, wao), mode="nt", tm=2048, tn=512, out_dtype=BF16, name="branch_out_bwd_x",
                                   after=after_ffn)
    g_wco, g_wao, g_wo = _matmul_group((conv_y, attn, merged), (dco, dao, dx1b), mode="tn", tm=512, tn=1024, out_dtype=BF16,
                                       name="mix_bwd_w")
    after_mix = reducer.start("mix", dict(wco=g_wco, wao=g_wao, wo=g_wo))
    dproj, dconv_w = _conv_bwd(dconv_y, proj, conv_w, dproj, "conv_bwd", after=after_mix)
    after_mix = reducer.middle("mix", (dconv_w,))
    dproj, dkr, dv, dsinks = _swa_bwd(dattn, proj, tab, sinks, dproj, "attn_bwd", after=after_mix)
    dproj = _kv_bwd(dkr, dv, tab, dproj, "kv_bwd")
    g_win_t = _matmul(dproj, h1, mode="tn", out_dtype=BF16, name="proj_bwd_w", tm=512, tn=1024, tk=2048)
    after_in = reducer.middle("in", reducer.start("in", dict(win_t=g_win_t)))
    dh1 = _matmul(dproj, win_t, mode="nn", out_dtype=BF16, name="proj_bwd_x", tm=1024, tn=1024, tk=1664, after=after_in)
    dx, _, dg_mix = _rms_bwd(dh1, x, g_mix, dx1, "rms1_bwd")
    grads = dict(win_t=g_win_t, wgu_t=g_wgu_t, wd=g_wd, wco=g_wco, wao=g_wao, wo=g_wo)
    small = dict(g_mix=dg_mix, g_ffn=dg_ffn, g_final=dg_final, conv_w=dconv_w, sinks=dsinks, lossvec=lossvec)
    return dx, grads, small


def _position():
    return lax.axis_index("x"), lax.axis_index("y"), lax.axis_index("c")


def _other_chips(x, y):
    return [(1 - x, y), (x, 1 - y), (1 - x, 1 - y)]


SEM_SPEC = pl.BlockSpec(memory_space=pltpu.SEMAPHORE)
EFFECT = pltpu.SideEffectType.DATAFLOW_SIDE_EFFECTING
TOKEN = jax.ShapeDtypeStruct((8, 128), F32)
TOKEN_SPEC = pl.BlockSpec(memory_space=pltpu.VMEM)


def _hbm(a):
    return pltpu.with_memory_space_constraint(a, pltpu.HBM)


def _place(ws, me_idx, dtypes, name, after=()):
    n = len(ws)

    def body(i_ref, *refs):
        for w_ref, o_ref, dtype in zip(refs[:n], refs[n + len(after):], dtypes):
            o_ref[...] = w_ref[...].astype(dtype)

    grid_spec = pltpu.PrefetchScalarGridSpec(
        num_scalar_prefetch=1, grid=(1,),
        in_specs=[pl.BlockSpec(w.shape, lambda i, me: (0, 0)) for w in ws] + [HBM_SPEC] * len(after),
        out_specs=[pl.BlockSpec(w.shape, lambda i, me: (me[0], 0)) for w in ws])
    return _call(body, name=name, grid_spec=grid_spec,
                 out_shape=[_sds((N_DEV * w.shape[0], w.shape[1]), dtype) for w, dtype in zip(ws, dtypes)],
                 compiler_params=_params("arbitrary"))(me_idx, *ws, *after)


def _own_rows(ref, r, px, py, pc):
    return ref.at[pl.ds((4 * px + 2 * py + pc) * r, r), :]


def _gather_phase(bufs, waits, plans, after, name):
    n = len(bufs)
    rows = [b.shape[0] // N_DEV for b in bufs]
    nw, npl = len(waits), len(plans)

    def body(*refs):
        ins = refs[:n]
        wait_sems = refs[n:n + 2 * nw]
        out0 = n + 2 * nw + len(after)
        new_sems = refs[out0:out0 + 2 * npl]
        token = refs[-1]
        x, y, c = _position()
        for w, (_, _, sent, received) in enumerate(waits):
            for a in range(n):
                for count, wait in ((sent, "wait_send"), (received, "wait_recv")):
                    span = _whole(ins[a], count * rows[a])
                    getattr(pltpu.make_async_remote_copy(
                        src_ref=span, dst_ref=span, send_sem=wait_sems[2 * w].at[a], recv_sem=wait_sems[2 * w + 1].at[a],
                        device_id=(x, y, c), device_id_type=MESH), wait)()
        for k, plan in enumerate(plans):
            for a in range(n):
                for block, target in plan(x, y, c):
                    span = _own_rows(ins[a], rows[a], *block)
                    pltpu.make_async_remote_copy(src_ref=span, dst_ref=span, send_sem=new_sems[2 * k].at[a],
                                                 recv_sem=new_sems[2 * k + 1].at[a], device_id=target, device_id_type=MESH).start()
        token[...] = jnp.zeros_like(token)

    sem_ops = [s for send, recv, _, _ in waits for s in (send, recv)]
    outs = _call(
        body, name=name, in_specs=[HBM_SPEC] * n + [SEM_SPEC] * (2 * nw) + [HBM_SPEC] * len(after),
        out_specs=[SEM_SPEC] * (2 * npl) + [HBM_SPEC] * n + [TOKEN_SPEC],
        out_shape=[pltpu.SemaphoreType.DMA((n,))] * (2 * npl) + [pltpu.HBM(b.shape, b.dtype) for b in bufs] + [TOKEN],
        input_output_aliases={i: 2 * npl + i for i in range(n)},
        compiler_params=pltpu.CompilerParams(has_side_effects=EFFECT),
    )(*[_hbm(b) for b in bufs], *sem_ops, *after)
    pairs = [(outs[2 * k], outs[2 * k + 1]) for k in range(npl)]
    return pairs, list(outs[2 * npl:2 * npl + n]), outs[-1]


def _own_to_near(x, y, c):
    return [((x, y, c), (x, y, 1 - c)), ((x, y, c), (1 - x, y, c)), ((x, y, c), (x, 1 - y, c))]


def _near_to_sibling(x, y, c):
    return [((1 - x, y, c), (x, y, 1 - c)), ((x, 1 - y, c), (x, y, 1 - c))]


def _relay_diagonal(x, y, c):
    north = c
    source = (x * north + (1 - x) * (1 - north), (1 - y) * north + y * (1 - north), c)
    target = ((1 - x) * north + x * (1 - north), y * north + (1 - y) * (1 - north), c)
    return [(source, target)]


def _diagonal_to_sibling(x, y, c):
    return [((1 - x, 1 - y, c), (x, y, 1 - c))]


def _gather_start(bufs, groups, name, after=()):
    n = len(bufs)
    rows = [b.shape[0] // N_DEV for b in bufs]
    ng = len(groups)

    def body(*refs):
        ins = refs[:n]
        sems = refs[n + len(after):n + len(after) + 2 * ng]
        token = refs[-1]
        x, y, c = _position()
        targets = [(x, y, 1 - c)] + [(*chip, c) for chip in _other_chips(x, y)]
        for g, members in enumerate(groups):
            for slot, a in enumerate(members):
                own = _own_rows(ins[a], rows[a], x, y, c)
                for to in targets:
                    pltpu.make_async_remote_copy(src_ref=own, dst_ref=own, send_sem=sems[2 * g].at[slot],
                                                 recv_sem=sems[2 * g + 1].at[slot], device_id=to, device_id_type=MESH).start()
        token[...] = jnp.zeros_like(token)

    sem_shapes = []
    for members in groups:
        sem_shapes += [pltpu.SemaphoreType.DMA((len(members),))] * 2
    outs = _call(
        body, name=name, in_specs=[HBM_SPEC] * (n + len(after)),
        out_specs=[SEM_SPEC] * (2 * ng) + [HBM_SPEC] * n + [TOKEN_SPEC],
        out_shape=sem_shapes + [pltpu.HBM(b.shape, b.dtype) for b in bufs] + [TOKEN],
        input_output_aliases={i: 2 * ng + i for i in range(n)},
        compiler_params=pltpu.CompilerParams(has_side_effects=EFFECT),
    )(*[_hbm(b) for b in bufs], *after)
    sem_pairs = [(outs[2 * g], outs[2 * g + 1]) for g in range(ng)]
    return sem_pairs, list(outs[2 * ng:2 * ng + n]), outs[-1]


def _gather_forward(send_sems, recv_sems, bufs, after, name):
    n = len(bufs)
    rows = [b.shape[0] // N_DEV for b in bufs]

    def body(*refs):
        ins = refs[:n]
        send1, recv1 = refs[n], refs[n + 1]
        out0 = n + 2 + len(after)
        send2, recv2 = refs[out0], refs[out0 + 1]
        token = refs[-1]
        x, y, c = _position()
        for a in range(n):
            step1 = pltpu.make_async_remote_copy(
                src_ref=_whole(ins[a], 4 * rows[a]), dst_ref=_whole(ins[a], 4 * rows[a]), send_sem=send1.at[a],
                recv_sem=recv1.at[a], device_id=(x, y, c), device_id_type=MESH)
            step1.wait_send()
            step1.wait_recv()
        for a in range(n):
            for chip in _other_chips(x, y):
                blk = _own_rows(ins[a], rows[a], *chip, c)
                pltpu.make_async_remote_copy(src_ref=blk, dst_ref=blk, send_sem=send2.at[a], recv_sem=recv2.at[a],
                                             device_id=(x, y, 1 - c), device_id_type=MESH).start()
        token[...] = jnp.zeros_like(token)

    outs = _call(
        body, name=name, in_specs=[HBM_SPEC] * n + [SEM_SPEC, SEM_SPEC] + [HBM_SPEC] * len(after),
        out_specs=[SEM_SPEC, SEM_SPEC] + [HBM_SPEC] * n + [TOKEN_SPEC],
        out_shape=[pltpu.SemaphoreType.DMA((n,)), pltpu.SemaphoreType.DMA((n,))]
        + [pltpu.HBM(b.shape, b.dtype) for b in bufs] + [TOKEN],
        input_output_aliases={i: 2 + i for i in range(n)},
        compiler_params=pltpu.CompilerParams(has_side_effects=EFFECT),
    )(*bufs, send_sems, recv_sems, *after)
    return outs[0], outs[1], list(outs[2:2 + n]), outs[-1]


def _gather_done(send_sems, recv_sems, bufs, after, name):
    n = len(bufs)
    rows = [b.shape[0] // N_DEV for b in bufs]

    def body(*refs):
        ins = refs[:n]
        send2, recv2 = refs[n], refs[n + 1]
        x, y, c = _position()
        for a in range(n):
            step2 = pltpu.make_async_remote_copy(
                src_ref=_whole(ins[a], 3 * rows[a]), dst_ref=_whole(ins[a], 3 * rows[a]), send_sem=send2.at[a],
                recv_sem=recv2.at[a], device_id=(x, y, c), device_id_type=MESH)
            step2.wait_send()
            step2.wait_recv()

    outs = _call(
        body, name=name, in_specs=[HBM_SPEC] * n + [SEM_SPEC, SEM_SPEC] + [HBM_SPEC] * len(after),
        out_specs=[HBM_SPEC] * n, out_shape=[pltpu.HBM(b.shape, b.dtype) for b in bufs],
        input_output_aliases={i: i for i in range(n)},
        compiler_params=pltpu.CompilerParams(has_side_effects=EFFECT),
    )(*bufs, send_sems, recv_sems, *after)
    return list(outs)


def _whole(ref, nrows):
    return ref.at[pl.ds(0, nrows), :]


def _to_sibling(x, y, c):
    return [(2 * q + (1 - c), q, (x, y, 1 - c)) for q in range(4)]


def _to_chips(x, y, c):
    return [(2 * px + py, j, (px, py, c)) for j, (px, py) in enumerate(_other_chips(x, y))]


def _exchange_start(exchanges, name):
    members = [(e, a, src, src.shape[0] // slots, plan)
               for e, (srcs, slots, plan) in enumerate(exchanges) for a, src in enumerate(srcs)]
    n, n_sems = len(members), 2 * len(exchanges)
    lands = [lax.empty((len(plan(0, 0, 0)) * r, src.shape[1]), src.dtype) for _, _, src, r, plan in members]

    def body(*refs):
        ins, land_refs, sems = refs[:n], refs[n:2 * n], refs[2 * n:2 * n + n_sems]
        token = refs[-1]
        for i, (e, a, _, r, plan) in enumerate(members):
            for src_slot, dst_slot, target in plan(*_position()):
                pltpu.make_async_remote_copy(
                    src_ref=ins[i].at[pl.ds(src_slot * r, r), :], dst_ref=land_refs[i].at[pl.ds(dst_slot * r, r), :],
                    send_sem=sems[2 * e].at[a], recv_sem=sems[2 * e + 1].at[a], device_id=target, device_id_type=MESH).start()
        token[...] = jnp.zeros_like(token)

    sem_shapes = [pltpu.SemaphoreType.DMA((len(srcs),)) for srcs, _, _ in exchanges for _ in range(2)]
    outs = _call(
        body, name=name, in_specs=[HBM_SPEC] * (2 * n),
        out_specs=[SEM_SPEC] * n_sems + [HBM_SPEC] * (2 * n) + [TOKEN_SPEC],
        out_shape=sem_shapes + [pltpu.HBM(m[2].shape, m[2].dtype) for m in members]
        + [pltpu.HBM(l.shape, l.dtype) for l in lands] + [TOKEN],
        input_output_aliases={i: n_sems + i for i in range(2 * n)},
        compiler_params=pltpu.CompilerParams(has_side_effects=EFFECT),
    )(*[_hbm(m[2]) for m in members], *[_hbm(l) for l in lands])
    started, at = [], 0
    for e, (srcs, _, _) in enumerate(exchanges):
        k = len(srcs)
        started.append((outs[2 * e], outs[2 * e + 1], list(outs[n_sems + at:n_sems + at + k]),
                        list(outs[n_sems + n + at:n_sems + n + at + k])))
        at += k
    return started, outs[-1]


def _exchange_wait(send_sems, recv_sems, srcs, lands, after, name):
    n = len(srcs)

    def body(*refs):
        ins, land_refs = refs[:n], refs[n:2 * n]
        send_sems_ref, recv_sems_ref = refs[2 * n], refs[2 * n + 1]
        for a in range(n):
            span = _whole(land_refs[a], lands[a].shape[0])
            cp = pltpu.make_async_remote_copy(
                src_ref=span, dst_ref=span, send_sem=send_sems_ref.at[a],
                recv_sem=recv_sems_ref.at[a], device_id=_position(), device_id_type=MESH)
            cp.wait_send()
            cp.wait_recv()

    outs = _call(
        body, name=name, in_specs=[HBM_SPEC] * (2 * n) + [SEM_SPEC, SEM_SPEC] + [HBM_SPEC] * len(after),
        out_specs=[HBM_SPEC] * (2 * n),
        out_shape=[pltpu.HBM(a.shape, a.dtype) for a in srcs] + [pltpu.HBM(l.shape, l.dtype) for l in lands],
        input_output_aliases={i: i for i in range(2 * n)},
        compiler_params=pltpu.CompilerParams(has_side_effects=EFFECT),
    )(*srcs, *lands, send_sems, recv_sems, *after)
    return list(outs[:n]), list(outs[n:])


def _chip_partial(grads, recvs, idx, name):
    n = len(grads)
    rows = [recv.shape[0] // 4 for recv in recvs]

    def body(i_ref, *refs):
        del i_ref
        for g_ref, s_ref, o_ref in zip(refs[:n], refs[n:2 * n], refs[2 * n:]):
            o_ref[...] = (g_ref[...].astype(F32) + s_ref[...].astype(F32)).astype(BF16)

    grid_spec = pltpu.PrefetchScalarGridSpec(
        num_scalar_prefetch=1, grid=(3,),
        in_specs=[pl.BlockSpec((r, D), lambda t, i_ref: (2 * i_ref[1 + t] + i_ref[0], 0)) for r in rows]
        + [pl.BlockSpec((r, D), lambda t, i_ref: (i_ref[1 + t], 0)) for r in rows],
        out_specs=[pl.BlockSpec((r, D), lambda t, i_ref: (i_ref[1 + t], 0)) for r in rows])
    return _call(body, name=name, grid_spec=grid_spec, out_shape=[_sds((4 * r, D), BF16) for r in rows],
                 compiler_params=_params("arbitrary"))(idx, *grads, *recvs)


def _adamw_math(w, g, m, v):
    m2 = B1 * m + (1.0 - B1) * g
    v2 = B2 * v + (1.0 - B2) * jnp.square(g)
    m_hat = m2 / (1.0 - B1 ** STEP)
    v_hat = v2 / (1.0 - B2 ** STEP)
    return -LR * (m_hat / (jnp.sqrt(v_hat) + EPS_ADAM) + WD * w), m2, v2


def _reduce_adamw(ws, grads, from_sibling, from_chips, idx, ms, vs, name):
    n = len(ws)
    nb = 2
    tiles = [w.shape[0] // nb for w in ws]
    for w, g, s, c in zip(ws, grads, from_sibling, from_chips):
        r = w.shape[0]
        assert g.shape == (N_DEV * r, D) and s.shape == (4 * r, D) and c.shape == (3 * r, D)

    def body(i_ref, *refs):
        del i_ref
        ins, outs = refs[:8 * n], refs[8 * n:]
        for a in range(n):
            w_ref, p_ref, s_ref, r0_ref, r1_ref, r2_ref, m_ref, v_ref = ins[8 * a:8 * a + 8]
            g_ref, d_ref, nm_ref, nv_ref = outs[4 * a:4 * a + 4]
            g = p_ref[...].astype(F32) + s_ref[...].astype(F32)
            g = ((g + r0_ref[...].astype(F32)) + r1_ref[...].astype(F32)) + r2_ref[...].astype(F32)
            g_ref[...] = g
            d_ref[...], nm_ref[...], nv_ref[...] = _adamw_math(w_ref[...], g, m_ref[...], v_ref[...])

    in_specs, out_specs, operands, out_shape = [], [], [], []
    for a, tr in enumerate(tiles):
        own = pl.BlockSpec((tr, D), lambda i, i_ref: (i, 0))
        in_specs += [own, pl.BlockSpec((tr, D), lambda i, i_ref: (i_ref[0] * nb + i, 0)),
                     pl.BlockSpec((tr, D), lambda i, i_ref: (i_ref[1] * nb + i, 0))]
        in_specs += [pl.BlockSpec((tr, D), lambda i, i_ref, j=j: (j * nb + i, 0)) for j in range(3)] + [own, own]
        operands += [ws[a], grads[a], from_sibling[a], from_chips[a], from_chips[a], from_chips[a], ms[a], vs[a]]
        out_specs += [own] * 4
        out_shape += [_sds(ws[a].shape, F32)] * 4
    grid_spec = pltpu.PrefetchScalarGridSpec(num_scalar_prefetch=1, grid=(nb,), in_specs=in_specs, out_specs=out_specs)
    outs = _call(body, name=name, grid_spec=grid_spec, out_shape=out_shape, compiler_params=_params("parallel"))(idx, *operands)
    return [tuple(outs[4 * a:4 * a + 4]) for a in range(n)]


SMALL_ROWS = 8


def _small_all_reduce(pack, name, after=()):
    def body(p_ref, *rest):
        tot_ref, loss_ref, gath, send_sems, recv_sems = rest[len(after):]
        x, y, c = _position()
        me_id = 4 * x + 2 * y + c
        gath[me_id] = p_ref[...]
        copies = []
        for k in range(1, N_DEV):
            peer = tuple(1 - v if (k >> b) & 1 else v for v, b in ((x, 2), (y, 1), (c, 0)))
            cp = pltpu.make_async_remote_copy(src_ref=p_ref, dst_ref=gath.at[me_id], send_sem=send_sems.at[k - 1],
                                              recv_sem=recv_sems.at[k - 1], device_id=peer, device_id_type=MESH)
            cp.start()
            copies.append(cp)
        for cp in copies:
            cp.wait_recv()
        for cp in copies:
            cp.wait_send()
        tot = gath[0]
        for d in range(1, N_DEV):
            tot = tot + gath[d]
        tot_ref[...] = tot
        loss_ref[...] = jnp.full((1, 128), (0.5 / D) * jnp.sum(tot[SMALL_ROWS - 1:SMALL_ROWS, :]), F32)

    vm = pl.BlockSpec(memory_space=pltpu.VMEM)
    return _call(
        body, name=name, in_specs=[vm] + [HBM_SPEC] * len(after), out_specs=[vm, vm],
        out_shape=[_sds((SMALL_ROWS, D), F32), _sds((1, 128), F32)],
        scratch_shapes=[pltpu.VMEM((N_DEV, SMALL_ROWS, D), F32), pltpu.SemaphoreType.DMA((N_DEV - 1,)),
                        pltpu.SemaphoreType.DMA((N_DEV - 1,))],
    )(pack, *after)


def _adamw_small(ws, gs, ms, vs, name):
    n = len(ws)

    def body(*refs):
        for a in range(n):
            w_ref, g_ref, m_ref, v_ref = (refs[k * n + a] for k in range(4))
            d_ref, nm_ref, nv_ref = (refs[(4 + k) * n + a] for k in range(3))
            d_ref[...], nm_ref[...], nv_ref[...] = _adamw_math(w_ref[...], g_ref[...], m_ref[...], v_ref[...])

    vm = pl.BlockSpec(memory_space=pltpu.VMEM)
    outs = _call(body, name=name, in_specs=[vm] * (4 * n), out_specs=[vm] * (3 * n),
                 out_shape=[_sds(w.shape, F32) for w in ws] * 3)(*ws, *gs, *ms, *vs)
    return [(outs[a], outs[n + a], outs[2 * n + a]) for a in range(n)]


def kernel(x, g_mix, w_in, conv_w, attn_sinks, w_conv_out, w_attn_out, w_o, g_ffn, w_gate_up, w_down, g_final, loss_target, m_g_mix, m_w_in, m_conv_w, m_attn_sinks, m_w_conv_out, m_w_attn_out, m_w_o, m_g_ffn, m_w_gate_up, m_w_down, m_g_final, v_g_mix, v_w_in, v_conv_w, v_attn_sinks, v_w_conv_out, v_w_attn_out, v_w_o, v_g_ffn, v_w_gate_up, v_w_down, v_g_final):
    cx, cy, cc = _position()
    chip = 2 * cx + cy
    partial_idx = jnp.stack([cc, 2 * (1 - cx) + cy, 2 * cx + (1 - cy), 2 * (1 - cx) + (1 - cy)]).astype(jnp.int32)
    own_idx = jnp.stack([2 * chip + cc, chip]).astype(jnp.int32)
    me = 4 * cx + 2 * cy + cc

    me_idx = jnp.reshape(me, (1,)).astype(jnp.int32)
    first = _place([jnp.transpose(w_in[0]), jnp.pad(conv_w[0], ((0, 5), (0, 0)))], me_idx, (BF16, F32), "place_in")
    (to_near,), first, token_in = _gather_phase(first, [], [_own_to_near], (), "gather_in_start")
    gather_tokens = (token_in,)

    class Gathered:
        def __init__(self):
            self.state = {}

        def begin(self, group, after):
            if group == "in":
                (near, relay), bufs, token = _gather_phase(
                    first, [(*to_near, 3, 3)], [_near_to_sibling, _relay_diagonal], after, "gather_in_relay")
                later = [_place([w], me_idx, (BF16,), "place_" + k, after=(token,))[0] for k, w in (
                    ("w_conv_out", w_conv_out[0]), ("w_attn_out", w_attn_out[0]), ("w_o", w_o[0]),
                    ("w_gate_up", jnp.transpose(w_gate_up[0])), ("w_down", w_down[0]))]
                (sems_mix, sems_ffn), later, token_later = _gather_start(later, [[0, 1, 2], [3, 4]], "gather_start_later")
                self.state.update({"in": (near, relay, bufs), "mix": (sems_mix, later[:3]), "ffn": (sems_ffn, later[3:])})
                return (token_later,)
            (send_sems, recv_sems), group_bufs = self.state[group]
            send2, recv2, group_bufs, token = _gather_forward(send_sems, recv_sems, group_bufs, after, "gather_forward_" + group)
            self.state[group] = ((send2, recv2), group_bufs)
            return (token,)

        def end(self, group, after):
            if group == "in":
                near, relay, bufs = self.state[group]
                (last,), bufs, token = _gather_phase(bufs, [(*relay, 1, 1)], [_diagonal_to_sibling], after, "gather_in_last")
                _, full, _ = _gather_phase(bufs, [(*near, 2, 2), (*last, 1, 1)], [], (token,), "gather_in_done")
                return full[0], jnp.transpose(full[1].reshape(N_DEV, 8, 128)[:, :3, :], (1, 0, 2)).reshape(3, D)
            (send2, recv2), group_bufs = self.state[group]
            return _gather_done(send2, recv2, group_bufs, after, "gather_done_" + group)

    in_flight, own_pieces = {}, {}

    transposed = ("w_in", "w_gate_up")

    def as2d(k, a):
        if k in transposed:
            return jnp.transpose(a[0])
        return a[None] if a.ndim == 1 else (a[0] if a.ndim == 3 else a)

    w_all = {"g_mix": g_mix, "w_in": w_in, "conv_w": conv_w, "attn_sinks": attn_sinks, "w_conv_out": w_conv_out,
             "w_attn_out": w_attn_out, "w_o": w_o, "g_ffn": g_ffn, "w_gate_up": w_gate_up, "w_down": w_down, "g_final": g_final}
    m_all = {"g_mix": m_g_mix, "w_in": m_w_in, "conv_w": m_conv_w, "attn_sinks": m_attn_sinks, "w_conv_out": m_w_conv_out,
             "w_attn_out": m_w_attn_out, "w_o": m_w_o, "g_ffn": m_g_ffn, "w_gate_up": m_w_gate_up, "w_down": m_w_down,
             "g_final": m_g_final}
    v_all = {"g_mix": v_g_mix, "w_in": v_w_in, "conv_w": v_conv_w, "attn_sinks": v_attn_sinks, "w_conv_out": v_w_conv_out,
             "w_attn_out": v_w_attn_out, "w_o": v_w_o, "g_ffn": v_g_ffn, "w_gate_up": v_w_gate_up, "w_down": v_w_down,
             "g_final": v_g_final}
    results = {}

    def record(k, *vals):
        results[k] = [(jnp.transpose(val) if k in transposed else val).reshape(w_all[k].shape) for val in vals]

    def update(group, names, grads, from_sibling, from_chips):
        outs = _reduce_adamw([as2d(k, w_all[k]) for k in names], grads, from_sibling, from_chips, own_idx,
                             [as2d(k, m_all[k]) for k in names], [as2d(k, v_all[k]) for k in names], "adamw_" + group)
        for k, vals in zip(names, outs):
            record(k, *vals)
        return tuple(vals[2] for vals in outs)

    def update_small(grads):
        keys = list(grads)
        outs = _adamw_small([as2d(k, w_all[k]) for k in keys], [grads[k] for k in keys], [as2d(k, m_all[k]) for k in keys],
                            [as2d(k, v_all[k]) for k in keys], "adamw_small")
        for k, (d, nm, nv) in zip(keys, outs):
            record(k, grads[k], d, nm, nv)
        return tuple(nm for _, nm, _ in outs)

    kernel_name = {"win_t": "w_in", "wgu_t": "w_gate_up", "wd": "w_down", "wco": "w_conv_out", "wao": "w_attn_out", "wo": "w_o"}

    def finish(group, after):
        keys, send_sems, recv_sems, parts, from_chips = in_flight[group]
        _, from_chips = _exchange_wait(send_sems, recv_sems, parts, from_chips, after, "rs_chips_wait_" + group)
        grads, from_sibling = own_pieces[group]
        return update(group, [kernel_name[k] for k in keys], grads, from_sibling, from_chips)

    class Reducer:
        def __init__(self):
            self.waiting = None

        def start(self, group, gdict):
            keys, glist = list(gdict), list(gdict.values())
            exchanges = [(glist, N_DEV, _to_sibling)]
            if self.waiting:
                exchanges.append((self.waiting[2], 4, _to_chips))
            started, token = _exchange_start(exchanges, "rs_sibling_start_" + group)
            in_flight[group] = (keys, *started[0])
            if self.waiting:
                in_flight[self.waiting[0]] = (self.waiting[1], *started[1])
            return (token,)

        def middle(self, group, after):
            keys, send_sems, recv_sems, glist, lands = in_flight[group]
            if group == "in":
                after = finish("ffn", after)
            glist, lands = _exchange_wait(send_sems, recv_sems, glist, lands, after, "rs_sibling_wait_" + group)
            parts = _chip_partial(glist, lands, partial_idx, "chip_partial_" + group)
            own_pieces[group] = (glist, lands)
            if group != "in":
                self.waiting = (group, keys, parts)
                return tuple(parts)
            self.waiting = None
            (started,), token = _exchange_start([(parts, 4, _to_chips)], "rs_chips_start_" + group)
            in_flight[group] = (keys, *started)
            return (token,)

    dx, _, small = _local_step(x[0], loss_target[0], g_mix, g_ffn, g_final[None], attn_sinks, Gathered(),
                               reducer=Reducer(), after=gather_tokens)
    after = finish("mix", (dx,))

    sinks_row = jnp.pad(small["sinks"], ((0, 0), (0, D - 128)))
    pack = jnp.concatenate([small["g_mix"], small["g_ffn"], small["g_final"], small["conv_w"], sinks_row, small["lossvec"]], axis=0)
    tot, loss_row = _small_all_reduce(pack, "small_all_reduce", after=after)
    loss = loss_row[0, 0]
    g_small = {
        "g_mix": tot[0:1], "g_ffn": tot[1:2], "g_final": tot[2:3],
        "conv_w": lax.dynamic_slice(tot, (3, me * 128), (3, 128)), "attn_sinks": tot[6:7, :N_HEADS],
    }
    finish("in", update_small(g_small))

    order = ["g_mix", "w_in", "conv_w", "attn_sinks", "w_conv_out", "w_attn_out", "w_o", "g_ffn", "w_gate_up", "w_down", "g_final"]
    return (loss, dx[None], *[results[k][i] for i in range(4) for k in order])
```

```python
import functools
import math

import jax
import jax.numpy as jnp
from jax import lax
from jax.experimental import pallas as pl
from jax.experimental.pallas import tpu as pltpu

F32 = jnp.float32
BF16 = jnp.bfloat16

D = 1024
HEAD_DIM = 64
N_HEADS = 16
N_KV = 4
GROUP = N_HEADS // N_KV
D_KV = N_KV * HEAD_DIM
BLOCK = 128
ROT_DIM = HEAD_DIM // 4
ROPE_THETA = 500000.0
ATTN_SCALE = 1.0 / math.sqrt(HEAD_DIM)
NEG_INF = -1e30
D_FF = 2816
N_IN = 6656
EPS = 1e-5
C_CB, C_CC, C_CX, C_Q, C_K, C_V, C_GC, C_GA = 0, 1024, 2048, 3072, 4096, 4352, 4608, 5632

LR, B1, B2, EPS_ADAM, WD, STEP = 0.001, 0.9, 0.999, 1e-08, 0.01, 10

N_DEV = 8
MESH = pl.DeviceIdType.MESH
VMEM_LIMIT = 56 * 1024 * 1024

NN = (((1,), (0,)), ((), ()))
NT = (((1,), (1,)), ((), ()))
TN = (((0,), (0,)), ((), ()))
HBM_SPEC = pl.BlockSpec(memory_space=pl.ANY)
ROW_SPLIT = 4


def _call(body, **kw):
    return pl.pallas_call(body, **kw)


def _params(*sem):
    return pltpu.CompilerParams(dimension_semantics=sem, vmem_limit_bytes=VMEM_LIMIT)


def _sds(shape, dtype):
    return jax.ShapeDtypeStruct(shape, dtype)


def _matmul(a, b, *, mode, tm, tn, tk, out_dtype, name, res=None, after=()):
    parts = list(a) if isinstance(a, (list, tuple)) else [a]
    rows_a = parts[0].shape[0]
    cols_a = sum(p.shape[1] for p in parts)
    if mode == "nn":
        (m, kk), (_, n), dims = (rows_a, cols_a), b.shape, NN
    elif mode == "nt":
        (m, kk), (n, _), dims = (rows_a, cols_a), b.shape, NT
    else:
        (kk, m), (_, n), dims = (rows_a, cols_a), b.shape, TN
    tm, tn, tk = min(tm, m), min(tn, n), min(tk, kk)
    assert m % tm == 0 and n % tn == 0 and kk % tk == 0, (name, m, n, kk, tm, tn, tk)
    nk = kk // tk
    split_axis, width = (2, tk) if mode == "nn" else (0, tm)
    assert len(parts) == 1 or mode in ("nn", "tn")
    assert len(parts) == 1 or all(p.shape[1] % width == 0 for p in parts), (name, width)
    counts = [p.shape[1] // width for p in parts]
    starts = [sum(counts[:p]) for p in range(len(parts))]

    def a_spec(p):
        def col(t):
            return jnp.clip(t - starts[p], 0, counts[p] - 1) if len(parts) > 1 else t

        if mode == "tn":
            return pl.BlockSpec((tk, tm), lambda i, j, k: (k, col(i)))
        return pl.BlockSpec((tm, tk), lambda i, j, k: (i, col(k)))

    if mode == "nt":
        b_spec = pl.BlockSpec((tn, tk), lambda i, j, k: (j, k))
    else:
        b_spec = pl.BlockSpec((tk, tn), lambda i, j, k: (k, j))
    o_spec = pl.BlockSpec((tm, tn), lambda i, j, k: (i, j))
    has_res = res is not None
    n_parts = len(parts)
    unit = 128 if mode == "tn" else 16
    split = ROW_SPLIT if tm % (ROW_SPLIT * unit) == 0 else 1

    def body(*refs):
        a_refs, b_ref = refs[:n_parts], refs[n_parts]
        r_ref = refs[n_parts + 1] if has_res else None
        o_ref = refs[n_parts + 1 + has_res + len(after)]
        k = pl.program_id(2)

        acc_ref = refs[-1] if nk > 1 else None

        def step(a_ref):
            def matmul(rows):
                a_blk = a_ref[:, rows] if mode == "tn" else a_ref[rows, :]
                return lax.dot_general(a_blk, b_ref[...], dims, preferred_element_type=F32)

            def finish(rows, part):
                if nk > 1:
                    acc_ref[rows, :] += part
                else:
                    o_ref[rows, :] = (part + r_ref[rows, :] if has_res else part).astype(o_ref.dtype)

            _row_pipeline(tm, matmul, finish, split)

        if nk > 1:
            @pl.when(k == 0)
            def _():
                acc_ref[...] = jnp.zeros_like(acc_ref)

        if n_parts == 1:
            step(a_refs[0])
        else:
            t = pl.program_id(split_axis)
            for p in range(n_parts):
                pl.when((t >= starts[p]) & (t < starts[p] + counts[p]))(functools.partial(step, a_refs[p]))

        if nk > 1:
            @pl.when(k == nk - 1)
            def _():
                o_ref[...] = (acc_ref[...] + r_ref[...] if has_res else acc_ref[...]).astype(o_ref.dtype)

    ins = parts + [b] + ([res] if has_res else []) + list(after)
    in_specs = [a_spec(p) for p in range(n_parts)] + [b_spec] + ([o_spec] if has_res else []) + [HBM_SPEC] * len(after)
    scratch = [] if nk == 1 else [pltpu.VMEM((tm, tn), F32)]
    return _call(
        body, name=name, grid=(m // tm, n // tn, nk), in_specs=in_specs, out_specs=o_spec,
        out_shape=_sds((m, n), out_dtype), scratch_shapes=scratch,
        compiler_params=_params("parallel", "parallel", "arbitrary"),
    )(*ins)


def _matmul_group(a_group, b_group, *, mode, tm, tn, out_dtype, name, after=()):
    a0, b0 = a_group[0], b_group[0]
    a_pair, b_pair, count = a_group, b_group, len(a_group)
    if mode == "nn":
        (m, kk), (_, n), dims = a0.shape, b0.shape, NN
    elif mode == "nt":
        (m, kk), (n, _), dims = a0.shape, b0.shape, NT
    else:
        (kk, m), (_, n), dims = a0.shape, b0.shape, TN
    assert all(a.shape == a0.shape for a in a_pair) and all(b.shape == b0.shape for b in b_pair)
    tm, tn = min(tm, m), min(tn, n)
    assert m % tm == 0 and n % tn == 0, (name, m, n, tm, tn)
    a_spec = pl.BlockSpec((kk, tm), lambda i, j: (0, i)) if mode == "tn" else pl.BlockSpec((tm, kk), lambda i, j: (i, 0))
    b_spec = pl.BlockSpec((tn, kk), lambda i, j: (j, 0)) if mode == "nt" else pl.BlockSpec((kk, tn), lambda i, j: (0, j))
    o_spec = pl.BlockSpec((tm, tn), lambda i, j: (i, j))
    unit = 128 if mode == "tn" else 16
    split = ROW_SPLIT if tm % (ROW_SPLIT * unit) == 0 else 1

    def body(*refs):
        a_refs, b_refs, o_refs = refs[:count], refs[count:2 * count], refs[2 * count + len(after):]

        def matmul(rows):
            return tuple(lax.dot_general(a_ref[:, rows] if mode == "tn" else a_ref[rows, :], b_ref[...], dims,
                                         preferred_element_type=F32) for a_ref, b_ref in zip(a_refs, b_refs))

        def finish(rows, parts):
            for o_ref, part in zip(o_refs, parts):
                o_ref[rows, :] = part.astype(out_dtype)

        _row_pipeline(tm, matmul, finish, split)

    return _call(
        body, name=name, grid=(m // tm, n // tn), in_specs=[a_spec] * count + [b_spec] * count + [HBM_SPEC] * len(after),
        out_specs=[o_spec] * count, out_shape=[_sds((m, n), out_dtype)] * count,
        compiler_params=_params("parallel", "parallel"),
    )(*a_pair, *b_pair, *after)


def _row_tile(s):
    return min(512, s)


def _rms_fwd(x, g, name, after=()):
    s = x.shape[0]
    tm = _row_tile(s)

    def body(x_ref, g_ref, *rest):
        h_ref = rest[-1]
        xv = x_ref[...]
        r = lax.rsqrt(jnp.mean(xv * xv, axis=-1, keepdims=True) + EPS)
        h_ref[...] = (xv * r * g_ref[...]).astype(BF16)

    row = pl.BlockSpec((tm, D), lambda i: (i, 0))
    return _call(
        body, name=name, grid=(s // tm,), in_specs=[row, pl.BlockSpec((1, D), lambda i: (0, 0))] + [HBM_SPEC] * len(after),
        out_specs=row, out_shape=_sds((s, D), BF16), compiler_params=_params("parallel"),
    )(x, g, *after)


def _rms_bwd(dh, x, g, dres, name, after=()):
    s = x.shape[0]
    tm = _row_tile(s)

    def body(dh_ref, x_ref, g_ref, dres_ref, *rest):
        dx_ref, dxb_ref, dg_ref = rest[len(after):]
        xv = x_ref[...]
        r = lax.rsqrt(jnp.mean(xv * xv, axis=-1, keepdims=True) + EPS)
        xh = xv * r
        dhv = dh_ref[...].astype(F32)
        dyg = dhv * g_ref[...]
        dx = dres_ref[...] + r * (dyg - xh * jnp.mean(dyg * xh, axis=-1, keepdims=True))
        dx_ref[...] = dx
        dxb_ref[...] = dx.astype(BF16)
        part = jnp.sum(dhv * xh, axis=0, keepdims=True)

        @pl.when(pl.program_id(0) == 0)
        def _():
            dg_ref[...] = part

        @pl.when(pl.program_id(0) > 0)
        def _():
            dg_ref[...] += part

    row = pl.BlockSpec((tm, D), lambda i: (i, 0))
    vec = pl.BlockSpec((1, D), lambda i: (0, 0))
    return _call(
        body, name=name, grid=(s // tm,), in_specs=[row, row, vec, row] + [HBM_SPEC] * len(after), out_specs=[row, row, vec],
        out_shape=[_sds((s, D), F32), _sds((s, D), BF16), _sds((1, D), F32)],
        compiler_params=_params("arbitrary"),
    )(dh, x, g, dres, *after)


def _loss_head(x2, g, tgt, name):
    s = x2.shape[0]
    tm = _row_tile(s)

    def body(x_ref, g_ref, t_ref, dx_ref, dxb_ref, dg_ref, l_ref):
        xv = x_ref[...]
        gv = g_ref[...]
        r = lax.rsqrt(jnp.mean(xv * xv, axis=-1, keepdims=True) + EPS)
        xh = xv * r
        err = xh * gv - t_ref[...]
        dy = err * (1.0 / D)
        dyg = dy * gv
        dx = r * (dyg - xh * jnp.mean(dyg * xh, axis=-1, keepdims=True))
        dx_ref[...] = dx
        dxb_ref[...] = dx.astype(BF16)
        dg_part = jnp.sum(dy * xh, axis=0, keepdims=True)
        l_part = jnp.sum(err * err, axis=0, keepdims=True)

        @pl.when(pl.program_id(0) == 0)
        def _():
            dg_ref[...] = dg_part
            l_ref[...] = l_part

        @pl.when(pl.program_id(0) > 0)
        def _():
            dg_ref[...] += dg_part
            l_ref[...] += l_part

    row = pl.BlockSpec((tm, D), lambda i: (i, 0))
    vec = pl.BlockSpec((1, D), lambda i: (0, 0))
    return _call(
        body, name=name, grid=(s // tm,), in_specs=[row, vec, row], out_specs=[row, row, vec, vec],
        out_shape=[_sds((s, D), F32), _sds((s, D), BF16), _sds((1, D), F32), _sds((1, D), F32)],
        compiler_params=_params("arbitrary"),
    )(x2, g, tgt)


CONV_TC = 256


def _shift_down(u, k, rows):
    return jnp.where(rows >= k, pltpu.roll(u, k, 0), 0.0)


def _shift_up(u, k, rows, s):
    return jnp.where(rows < s - k, pltpu.roll(u, s - k, 0), 0.0)


def _conv_specs(s):
    nb = D // CONV_TC

    def col(c0):
        return pl.BlockSpec((s, CONV_TC), lambda j, c0=c0: (0, c0 // CONV_TC + j))

    return nb, col


def _conv_fwd(proj, conv_w, name):
    s = proj.shape[0]
    nb, col = _conv_specs(s)

    def body(cb_ref, cc_ref, cx_ref, w_ref, y_ref):
        rows = lax.broadcasted_iota(jnp.int32, (s, CONV_TC), 0)
        u = cc_ref[...].astype(F32) * cx_ref[...].astype(F32)
        w = w_ref[...]
        c = w[0:1] * _shift_down(u, 2, rows) + w[1:2] * _shift_down(u, 1, rows) + w[2:3] * u
        y_ref[...] = (cb_ref[...].astype(F32) * c).astype(BF16)

    return _call(
        body, name=name, grid=(nb,),
        in_specs=[col(C_CB), col(C_CC), col(C_CX), pl.BlockSpec((3, CONV_TC), lambda j: (0, j))],
        out_specs=pl.BlockSpec((s, CONV_TC), lambda j: (0, j)), out_shape=_sds((s, D), BF16),
        compiler_params=_params("parallel"),
    )(proj, proj, proj, conv_w)


def _write_behind(t, nt, buf, sems, tiles, window, where):
    slot = t % 2

    def copies(sl, at):
        return [pltpu.make_async_copy(buf.at[sl, p], window(p, at), sems.at[sl, p]) for p in range(len(tiles))]

    @pl.when(t >= 2)
    def _():
        for cp in copies(slot, where):
            cp.wait()

    for p, tile in enumerate(tiles):
        buf[slot, p] = tile
    started = copies(slot, where)
    for cp in started:
        cp.start()

    @pl.when(t == nt - 1)
    def _():
        for cp in started:
            cp.wait()
        if nt > 1:
            for cp in copies(1 - slot, where):
                cp.wait()


def _conv_bwd(dy, proj, conv_w, dproj, name, after=()):
    s = proj.shape[0]
    nb, col = _conv_specs(s)

    def body(dy_ref, cb_ref, cc_ref, cx_ref, w_ref, *rest):
        dproj_ref, dw_ref, buf, sems = rest[1 + len(after):]
        j = pl.program_id(0)
        rows = lax.broadcasted_iota(jnp.int32, (s, CONV_TC), 0)
        cc = cc_ref[...].astype(F32)
        cx = cx_ref[...].astype(F32)
        u = cc * cx
        u1 = _shift_down(u, 1, rows)
        u2 = _shift_down(u, 2, rows)
        w = w_ref[...]
        c = w[0:1] * u2 + w[1:2] * u1 + w[2:3] * u
        dyv = dy_ref[...].astype(F32)
        dc = dyv * cb_ref[...].astype(F32)
        du = w[2:3] * dc + w[1:2] * _shift_up(dc, 1, rows, s) + w[0:1] * _shift_up(dc, 2, rows, s)

        def window(p, jj):
            start = pl.multiple_of((C_CB, C_CC, C_CX)[p] + jj * CONV_TC, CONV_TC)
            return dproj_ref.at[:, pl.ds(start, CONV_TC)]

        tiles = ((dyv * c).astype(BF16), (du * cx).astype(BF16), (du * cc).astype(BF16))
        _write_behind(j * 0, 1, buf, sems, tiles, window, j)
        dw_ref[...] = jnp.concatenate(
            [jnp.sum(dc * u2, axis=0, keepdims=True), jnp.sum(dc * u1, axis=0, keepdims=True),
             jnp.sum(dc * u, axis=0, keepdims=True)], axis=0)

    return _call(
        body, name=name, grid=(nb,),
        in_specs=[pl.BlockSpec((s, CONV_TC), lambda j: (0, j)), col(C_CB), col(C_CC), col(C_CX),
                  pl.BlockSpec((3, CONV_TC), lambda j: (0, j))] + [HBM_SPEC] * (1 + len(after)),
        out_specs=[pl.BlockSpec(memory_space=pl.ANY), pl.BlockSpec((3, CONV_TC), lambda j: (0, j))],
        out_shape=[_sds((s, N_IN), BF16), _sds((3, D), F32)],
        scratch_shapes=[pltpu.VMEM((1, 3, s, CONV_TC), BF16), pltpu.SemaphoreType.DMA((1, 3))],
        input_output_aliases={5: 0}, compiler_params=_params("arbitrary"),
    )(dy, proj, proj, proj, conv_w, dproj, *after)


def _rope_tables(s):
    half = ROT_DIM // 2
    inv_freq = ROPE_THETA ** (-jnp.arange(0, ROT_DIM, 2, dtype=F32) / ROT_DIM)
    inv64 = jnp.concatenate([inv_freq, inv_freq, jnp.zeros((HEAD_DIM - ROT_DIM,), F32)])
    ang = jnp.arange(s, dtype=F32)[:, None] * jnp.concatenate([inv64, inv64])[None, :]
    d = lax.broadcasted_iota(jnp.int32, (s, 128), 1) % HEAD_DIM
    cos, sin = jnp.cos(ang), jnp.sin(ang)
    c = jnp.where(d < ROT_DIM, cos, 1.0)
    a = jnp.where(d < half, -sin, 0.0)
    b = jnp.where((d >= half) & (d < ROT_DIM), sin, 0.0)
    return jnp.concatenate([c, a, b], axis=1)


def _rope(x, tab):
    c, a, b = tab[:, 0:128], tab[:, 128:256], tab[:, 256:384]
    outs = []
    for i in range(x.shape[1] // 128):
        xc = x[:, i * 128:(i + 1) * 128]
        outs.append(xc * c + pltpu.roll(xc, 120, 1) * a + pltpu.roll(xc, 8, 1) * b)
    return outs[0] if len(outs) == 1 else jnp.concatenate(outs, axis=1)


def _rope_t(dx, tab):
    c, a, b = tab[:, 0:128], tab[:, 128:256], tab[:, 256:384]
    outs = []
    for i in range(dx.shape[1] // 128):
        dc = dx[:, i * 128:(i + 1) * 128]
        outs.append(dc * c + pltpu.roll(dc * a, 8, 1) + pltpu.roll(dc * b, 120, 1))
    return outs[0] if len(outs) == 1 else jnp.concatenate(outs, axis=1)


def _attn_in_specs():
    prev = lambda n: jnp.maximum(n - 1, 0)
    return [
        pl.BlockSpec((BLOCK, D), lambda n: (n, C_Q // D)),
        pl.BlockSpec((BLOCK, D_KV), lambda n: (n, C_K // D_KV)),
        pl.BlockSpec((BLOCK, D_KV), lambda n: (prev(n), C_K // D_KV)),
        pl.BlockSpec((BLOCK, D_KV), lambda n: (n, C_V // D_KV)),
        pl.BlockSpec((BLOCK, D_KV), lambda n: (prev(n), C_V // D_KV)),
        pl.BlockSpec((BLOCK, 384), lambda n: (n, 0)),
        pl.BlockSpec((BLOCK, 384), lambda n: (prev(n), 0)),
        pl.BlockSpec(memory_space=pltpu.SMEM),
    ]


HALF = HEAD_DIM
N_CHUNK = D // 128


def _swa_bias(n):
    qi = lax.broadcasted_iota(jnp.int32, (BLOCK, 2 * BLOCK), 0)
    kj = lax.broadcasted_iota(jnp.int32, (BLOCK, 2 * BLOCK), 1)
    rel = qi + BLOCK - kj
    valid = (rel >= 0) & (rel < BLOCK) & ((kj >= BLOCK) | (n > 0))
    return jnp.where(valid, 0.0, NEG_INF)


def _halves(x):
    lo = lax.broadcasted_iota(jnp.int32, x.shape, 1) < HALF
    return jnp.where(lo, x, 0.0).astype(BF16), jnp.where(lo, 0.0, x).astype(BF16)


def _dup_heads(x):
    out = []
    for pair in range(N_KV // 2):
        xc = x[:, pair * 128:(pair + 1) * 128]
        xr = pltpu.roll(xc, HALF, 1)
        lo = lax.broadcasted_iota(jnp.int32, xc.shape, 1) < HALF
        out += [jnp.where(lo, xc, xr), jnp.where(lo, xr, xc)]
    return out


def _swa_load(q_ref, kc_ref, kp_ref, vc_ref, vp_ref, tc_ref, tp_ref):
    qf = _rope(q_ref[...].astype(F32), tc_ref[...]) * ATTN_SCALE
    q_halves = [_halves(qf[:, c * 128:(c + 1) * 128]) for c in range(N_CHUNK)]
    kf = jnp.concatenate([_rope(kp_ref[...].astype(F32), tp_ref[...]), _rope(kc_ref[...].astype(F32), tc_ref[...])], axis=0)
    vf = jnp.concatenate([vp_ref[...], vc_ref[...]], axis=0).astype(F32)
    return q_halves, _dup_heads(kf), _dup_heads(vf)


def _swa_probs(qh, kk, bias, sink):
    s = lax.dot_general(qh, kk, NT, preferred_element_type=F32) + bias
    m = jnp.maximum(jnp.max(jnp.maximum(s[:, :BLOCK], s[:, BLOCK:]), axis=1, keepdims=True), sink)
    return jnp.exp(s - m), m


def _swa_fwd(proj, tab, sinks, name, after=()):
    s = proj.shape[0]

    def body(q_ref, kc_ref, kp_ref, vc_ref, vp_ref, tc_ref, tp_ref, sink_ref, *rest):
        o_ref = rest[-1]
        n = pl.program_id(0)
        q_halves, kdup, vdup = _swa_load(q_ref, kc_ref, kp_ref, vc_ref, vp_ref, tc_ref, tp_ref)
        bias = _swa_bias(n)
        ones = jnp.ones((2 * BLOCK, 128), BF16)
        kk = [k.astype(BF16) for k in kdup]
        vv = [[jnp.concatenate([v_half, ones], axis=1) for v_half in _halves(v)] for v in vdup]
        heads = [(c, half) for c in range(N_CHUNK) for half in range(2)]
        scores = [lax.dot_general(q_halves[c][half], kk[c // (GROUP // 2)], NT, preferred_element_type=F32)
                  for c, half in heads]
        probs = []
        for (c, half), sc in zip(heads, scores):
            sc = sc + bias
            m = jnp.maximum(jnp.max(jnp.maximum(sc[:, :BLOCK], sc[:, BLOCK:]), axis=1, keepdims=True), sink_ref[0, 2 * c + half])
            probs.append((jnp.exp(sc - m).astype(BF16), jnp.exp(sink_ref[0, 2 * c + half] - m)))
        outs = [lax.dot_general(e, vv[c // (GROUP // 2)][half], NN, preferred_element_type=F32)
                for (c, half), (e, _) in zip(heads, probs)]
        for c in range(N_CHUNK):
            parts = [outs[2 * c + half][:, :128] * (1.0 / (outs[2 * c + half][:, 128:] + probs[2 * c + half][1]))
                     for half in range(2)]
            o_ref[:, c * 128:(c + 1) * 128] = (parts[0] + parts[1]).astype(BF16)

    return _call(
        body, name=name, grid=(s // BLOCK,), in_specs=_attn_in_specs() + [HBM_SPEC] * len(after),
        out_specs=pl.BlockSpec((BLOCK, D), lambda n: (n, 0)), out_shape=_sds((s, D), BF16),
        compiler_params=_params("parallel"),
    )(proj, proj, proj, proj, proj, tab, tab, sinks, *after)


def _swa_bwd(do, proj, tab, sinks, dproj, name, after=()):
    s = proj.shape[0]
    nblk = s // BLOCK
    kv_of = lambda c: c // (GROUP // 2)

    def body(do_ref, q_ref, kc_ref, kp_ref, vc_ref, vp_ref, tc_ref, tp_ref, sink_ref, *rest):
        dproj_ref, dk_ref, dv_ref, ds_ref, dqout, dkbuf, dvbuf, sems = rest[1 + len(after):]
        n = pl.program_id(0)

        @pl.when(n == 0)
        def _():
            dk_ref[...] = jnp.zeros_like(dk_ref)
            dv_ref[...] = jnp.zeros_like(dv_ref)
            ds_ref[...] = jnp.zeros_like(ds_ref)

        q_halves, kdup, vdup = _swa_load(q_ref, kc_ref, kp_ref, vc_ref, vp_ref, tc_ref, tp_ref)
        dof = do_ref[...].astype(F32)
        do_halves = [_halves(dof[:, c * 128:(c + 1) * 128]) for c in range(N_CHUNK)]
        bias = _swa_bias(n)
        ones = jnp.ones((2 * BLOCK, 128), BF16)
        kk = [k.astype(BF16) for k in kdup]
        vv = [v.astype(BF16) for v in vdup]
        k_halves = [_halves(k) for k in kdup]
        heads = [(c, half) for c in range(N_CHUNK) for half in range(2)]
        lane_row = lax.broadcasted_iota(jnp.int32, (1, 128), 1)
        lo_kv = lax.broadcasted_iota(jnp.int32, (2 * BLOCK, 128), 1) < HALF
        scores = [lax.dot_general(q_halves[c][half], kk[kv_of(c)], NT, preferred_element_type=F32) for c, half in heads]
        dps = [lax.dot_general(do_halves[c][half], vv[kv_of(c)], NT, preferred_element_type=F32) for c, half in heads]
        exps = []
        for (c, half), sc in zip(heads, scores):
            sink = sink_ref[0, 2 * c + half]
            sc = sc + bias
            m = jnp.maximum(jnp.max(jnp.maximum(sc[:, :BLOCK], sc[:, BLOCK:]), axis=1, keepdims=True), sink)
            exps.append((jnp.exp(sc - m), jnp.exp(sink - m)))
        sums = [lax.dot_general(e.astype(BF16), ones, NN, preferred_element_type=F32) for e, _ in exps]
        dsink_row = jnp.zeros((1, 128), F32)
        dsb, pb = [], []
        for h, ((e, es), row_sum, dp) in enumerate(zip(exps, sums, dps)):
            inv = 1.0 / (row_sum + es)
            p = e * jnp.concatenate([inv, inv], axis=1)
            t = p * dp
            delta = jnp.sum(t, axis=1, keepdims=True)
            dsb.append((t - p * delta).astype(BF16))
            pb.append(p.astype(BF16))
            dsink = -jnp.sum(es * inv * delta, axis=0, keepdims=True)
            dsink_row = dsink_row + jnp.where(lane_row == h, dsink, 0.0)
        dq_parts = [lax.dot_general(d, k_halves[kv_of(c)][half], NN, preferred_element_type=F32) for (c, half), d in zip(heads, dsb)]
        dk_parts = [lax.dot_general(d, q_halves[c][half], TN, preferred_element_type=F32) for (c, half), d in zip(heads, dsb)]
        dv_parts = [lax.dot_general(p, do_halves[c][half], TN, preferred_element_type=F32) for (c, half), p in zip(heads, pb)]
        dq = jnp.concatenate([(dq_parts[2 * c] + dq_parts[2 * c + 1]) * ATTN_SCALE for c in range(N_CHUNK)], axis=1)

        def kv_sum(parts, hk):
            acc = (parts[GROUP * hk] + parts[GROUP * hk + 1]) + (parts[GROUP * hk + 2] + parts[GROUP * hk + 3])
            return acc + pltpu.roll(acc, HALF, 1)

        for pair in range(N_KV // 2):
            dkbuf[:, pair * 128:(pair + 1) * 128] = jnp.where(lo_kv, kv_sum(dk_parts, 2 * pair), kv_sum(dk_parts, 2 * pair + 1))
            dvbuf[:, pair * 128:(pair + 1) * 128] = jnp.where(lo_kv, kv_sum(dv_parts, 2 * pair), kv_sum(dv_parts, 2 * pair + 1))
        prev0 = pl.multiple_of(jnp.maximum(n - 1, 0) * BLOCK, BLOCK)
        cur0 = pl.multiple_of(n * BLOCK, BLOCK)

        @pl.when(n > 0)
        def _():
            dk_ref[pl.ds(prev0, BLOCK), :] += dkbuf[0:BLOCK, :]
            dv_ref[pl.ds(prev0, BLOCK), :] += dvbuf[0:BLOCK, :]

        dk_ref[pl.ds(cur0, BLOCK), :] += dkbuf[BLOCK:2 * BLOCK, :]
        dv_ref[pl.ds(cur0, BLOCK), :] += dvbuf[BLOCK:2 * BLOCK, :]
        ds_ref[...] += dsink_row

        def window(p, at):
            return dproj_ref.at[pl.ds(pl.multiple_of(at * BLOCK, BLOCK), BLOCK), pl.ds(C_Q, D)]

        _write_behind(n, nblk, dqout, sems, (_rope_t(dq, tc_ref[...]).astype(BF16),), window, n)

    blk = lambda w: pl.BlockSpec((BLOCK, w), lambda n: (n, 0))
    whole = lambda w: pl.BlockSpec((s, w), lambda n: (0, 0))
    n_in = 1 + len(_attn_in_specs())
    return _call(
        body, name=name, grid=(nblk,), in_specs=[blk(D)] + _attn_in_specs() + [HBM_SPEC] * (1 + len(after)),
        out_specs=[HBM_SPEC, whole(D_KV), whole(D_KV), pl.BlockSpec((1, 128), lambda n: (0, 0))],
        out_shape=[_sds((s, N_IN), BF16), _sds((s, D_KV), F32), _sds((s, D_KV), F32), _sds((1, 128), F32)],
        scratch_shapes=[pltpu.VMEM((2, 1, BLOCK, D), BF16), pltpu.VMEM((2 * BLOCK, D_KV), F32),
                        pltpu.VMEM((2 * BLOCK, D_KV), F32), pltpu.SemaphoreType.DMA((2, 1))],
        input_output_aliases={n_in: 0}, compiler_params=_params("arbitrary"),
    )(do, proj, proj, proj, proj, proj, tab, tab, sinks, dproj, *after)


def _kv_bwd(dkr, dv, tab, dproj, name):
    s = dkr.shape[0]
    tm = _row_tile(s)

    def body(dk_ref, dv_ref, t_ref, dproj_in, o_ref):
        del dproj_in
        o_ref[:, 0:D_KV] = _rope_t(dk_ref[...], t_ref[...]).astype(BF16)
        o_ref[:, D_KV:2 * D_KV] = dv_ref[...].astype(BF16)

    row = lambda w: pl.BlockSpec((tm, w), lambda i: (i, 0))
    return _call(
        body, name=name, grid=(s // tm,),
        in_specs=[row(D_KV), row(D_KV), row(384), pl.BlockSpec(memory_space=pl.ANY)],
        out_specs=pl.BlockSpec((tm, 2 * D_KV), lambda i: (i, C_K // (2 * D_KV))),
        out_shape=_sds((s, N_IN), BF16), input_output_aliases={3: 0}, compiler_params=_params("parallel"),
    )(dkr, dv, tab, dproj)


EW_TC = 512


def _sigmoid(x):
    return 0.5 * jnp.tanh(0.5 * x) + 0.5


def _branches_merge_fwd(conv_y, attn, wco, wao, proj, name):
    s = proj.shape[0]
    tm = min(2048, s)

    def body(y_ref, a_ref, wc_ref, wa_ref, gc_ref, ga_ref, co_ref, ao_ref, m_ref):
        def matmuls(rows):
            return (lax.dot_general(y_ref[rows, :], wc_ref[...], NN, preferred_element_type=F32),
                    lax.dot_general(a_ref[rows, :], wa_ref[...], NN, preferred_element_type=F32))

        def finish(rows, parts):
            co, ao = parts
            co_ref[rows, :] = co.astype(BF16)
            ao_ref[rows, :] = ao.astype(BF16)
            m_ref[rows, :] = (_sigmoid(gc_ref[rows, :].astype(F32)) * co + _sigmoid(ga_ref[rows, :].astype(F32)) * ao).astype(BF16)

        _row_pipeline(tm, matmuls, finish)

    act = pl.BlockSpec((tm, D), lambda i, j: (i, 0))
    wgt = pl.BlockSpec((D, EW_TC), lambda i, j: (0, j))
    tile = pl.BlockSpec((tm, EW_TC), lambda i, j: (i, j))
    return _call(
        body, name=name, grid=(s // tm, D // EW_TC),
        in_specs=[act, act, wgt, wgt, pl.BlockSpec((tm, EW_TC), lambda i, j: (i, C_GC // EW_TC + j)),
                  pl.BlockSpec((tm, EW_TC), lambda i, j: (i, C_GA // EW_TC + j))],
        out_specs=[tile, tile, tile], out_shape=[_sds((s, D), BF16)] * 3, compiler_params=_params("parallel", "parallel"),
    )(conv_y, attn, wco, wao, proj, proj)


def _wo_merge_bwd(dx1b, wo, proj, conv_out, attn_out, name, after=()):
    s = proj.shape[0]
    tm = min(1024, s)
    nj = D // EW_TC

    def body(dx_ref, w_ref, gc_ref, ga_ref, co_ref, ao_ref, *rest):
        dproj_ref, dco_ref, dao_ref, buf, sems = rest[len(after):]
        i, j = pl.program_id(0), pl.program_id(1)
        gate_c, gate_a = [], []

        def matmul(rows):
            return lax.dot_general(dx_ref[rows, :], w_ref[...], NT, preferred_element_type=F32)

        def finish(rows, dm):
            sc = _sigmoid(gc_ref[rows, :].astype(F32))
            sa = _sigmoid(ga_ref[rows, :].astype(F32))
            dco_ref[rows, :] = (dm * sc).astype(BF16)
            dao_ref[rows, :] = (dm * sa).astype(BF16)
            gate_c.append((dm * co_ref[rows, :].astype(F32) * sc * (1.0 - sc)).astype(BF16))
            gate_a.append((dm * ao_ref[rows, :].astype(F32) * sa * (1.0 - sa)).astype(BF16))

        _row_pipeline(tm, matmul, finish)

        def window(p, at):
            start = pl.multiple_of((C_GC, C_GA)[p] + at[1] * EW_TC, EW_TC)
            return dproj_ref.at[pl.ds(pl.multiple_of(at[0] * tm, tm), tm), pl.ds(start, EW_TC)]

        tiles = (jnp.concatenate(gate_c, axis=0), jnp.concatenate(gate_a, axis=0))
        _write_behind(i * nj + j, (s // tm) * nj, buf, sems, tiles, window, (i, j))

    tile = pl.BlockSpec((tm, EW_TC), lambda i, j: (i, j))
    return _call(
        body, name=name, grid=(s // tm, nj),
        in_specs=[pl.BlockSpec((tm, D), lambda i, j: (i, 0)), pl.BlockSpec((EW_TC, D), lambda i, j: (j, 0)),
                  pl.BlockSpec((tm, EW_TC), lambda i, j: (i, C_GC // EW_TC + j)),
                  pl.BlockSpec((tm, EW_TC), lambda i, j: (i, C_GA // EW_TC + j)), tile, tile] + [HBM_SPEC] * len(after),
        out_specs=[HBM_SPEC, tile, tile],
        out_shape=[_sds((s, N_IN), BF16), _sds((s, D), BF16), _sds((s, D), BF16)],
        scratch_shapes=[pltpu.VMEM((2, 2, tm, EW_TC), BF16), pltpu.SemaphoreType.DMA((2, 2))],
        compiler_params=_params("arbitrary", "arbitrary"),
    )(dx1b, wo, proj, proj, conv_out, attn_out, *after)


FF_TC = 256
FF_TM = 2048


def _row_pipeline(tm, matmul, finish, split=ROW_SPLIT):
    step = tm // split
    pending = None
    for r in range(split):
        rows = pl.ds(r * step, step)
        result = matmul(rows)
        if pending is not None:
            finish(*pending)
        pending = (rows, result)
    finish(*pending)


def _gate_up_fwd(h2, wgu_t, name):
    s = h2.shape[0]
    tm = min(FF_TM, s)
    nb = D_FF // FF_TC

    def body(h_ref, wg_ref, wu_ref, a_ref, g_ref, u_ref):
        def matmuls(rows):
            h = h_ref[rows, :]
            return (lax.dot_general(h, wg_ref[...], NT, preferred_element_type=F32),
                    lax.dot_general(h, wu_ref[...], NT, preferred_element_type=F32))

        def finish(rows, gu):
            g, u = gu
            a_ref[rows, :] = (g * _sigmoid(g) * u).astype(BF16)
            g_ref[rows, :] = g.astype(BF16)
            u_ref[rows, :] = u.astype(BF16)

        _row_pipeline(tm, matmuls, finish)

    tile = pl.BlockSpec((tm, FF_TC), lambda j, i: (i, j))
    return _call(
        body, name=name, grid=(nb, s // tm),
        in_specs=[pl.BlockSpec((tm, D), lambda j, i: (i, 0)), pl.BlockSpec((FF_TC, D), lambda j, i: (j, 0)),
                  pl.BlockSpec((FF_TC, D), lambda j, i: (nb + j, 0))],
        out_specs=[tile, tile, tile], out_shape=[_sds((s, D_FF), BF16)] * 3,
        compiler_params=_params("parallel", "parallel"),
    )(h2, wgu_t, wgu_t)


def _down_bwd_x(dx2b, wd, gate, up, name):
    s = dx2b.shape[0]
    tm = min(FF_TM, s)
    nb = D_FF // FF_TC

    def body(dx_ref, w_ref, g_ref, u_ref, dg_ref, du_ref):
        def matmul(rows):
            return lax.dot_general(dx_ref[rows, :], w_ref[...], NT, preferred_element_type=F32)

        def finish(rows, da):
            g = g_ref[rows, :].astype(F32)
            sg = _sigmoid(g)
            dg_ref[rows, :] = (da * u_ref[rows, :].astype(F32) * (sg * (1.0 + g * (1.0 - sg)))).astype(BF16)
            du_ref[rows, :] = (da * (g * sg)).astype(BF16)

        _row_pipeline(tm, matmul, finish)

    tile = pl.BlockSpec((tm, FF_TC), lambda j, i: (i, j))
    return _call(
        body, name=name, grid=(nb, s // tm),
        in_specs=[pl.BlockSpec((tm, D), lambda j, i: (i, 0)), pl.BlockSpec((FF_TC, D), lambda j, i: (j, 0)), tile, tile],
        out_specs=[tile, tile], out_shape=[_sds((s, D_FF), BF16)] * 2,
        compiler_params=_params("parallel", "parallel"),
    )(dx2b, wd, gate, up)


class _Weights:
    def __init__(self, **groups):
        self.groups = groups

    def begin(self, group, after):
        return ()

    def end(self, group, after):
        return self.groups[group]


class _NoReduce:
    def start(self, group, grads):
        return ()

    def middle(self, group, after):
        return ()


def _local_step(x, tgt, g_mix, g_ffn, g_final, sinks, weights, reducer=None, after=()):
    reducer = reducer or _NoReduce()
    s = x.shape[0]
    tab = _rope_tables(s)
    big = dict(tm=2048, tn=512, tk=1024)
    h1 = _rms_fwd(x, g_mix, "rms1_fwd", after=after)
    win_t, conv_w = weights.end("in", weights.begin("in", (h1,)))
    proj = _matmul(h1, win_t, mode="nt", out_dtype=BF16, name="proj_fwd", tm=2048, tn=512, tk=1024)
    attn = _swa_fwd(proj, tab, sinks, "attn_fwd", after=weights.begin("mix", (proj,)))
    wco, wao, wo = weights.end("mix", (attn,))
    conv_y = _conv_fwd(proj, conv_w, "conv_fwd")
    conv_out, attn_out, merged = _branches_merge_fwd(conv_y, attn, wco, wao, proj, "branch_out_fwd")
    x1 = _matmul(merged, wo, mode="nn", out_dtype=F32, name="wo_fwd", res=x, **big)
    h2 = _rms_fwd(x1, g_ffn, "rms2_fwd", after=weights.begin("ffn", (x1,)))
    wgu_t, wd = weights.end("ffn", (h2,))
    act, gate, up = _gate_up_fwd(h2, wgu_t, "gate_up_fwd")
    x2 = _matmul(act, wd, mode="nn", out_dtype=F32, name="down_fwd", res=x1, tm=1024, tn=512, tk=D_FF)
    dx2, dx2b, dg_final, lossvec = _loss_head(x2, g_final, tgt, "loss_head")
    dgate, dup = _down_bwd_x(dx2b, wd, gate, up, "down_bwd_x")
    g_wd = _matmul(act, dx2b, mode="tn", out_dtype=BF16, name="down_bwd_w", tm=1408, tn=1024, tk=2048)
    dh2 = _matmul([dgate, dup], wgu_t, mode="nn", out_dtype=BF16, name="gate_up_bwd_x", tm=1024, tn=1024, tk=1408)
    g_wgu_t = _matmul([dgate, dup], h2, mode="tn", out_dtype=BF16, name="gate_up_bwd_w", tm=1408, tn=1024, tk=2048)
    after_ffn = reducer.start("ffn", dict(wgu_t=g_wgu_t, wd=g_wd))
    dx1, dx1b, dg_ffn = _rms_bwd(dh2, x1, g_ffn, dx2, "rms2_bwd")
    dproj, dco, dao = _wo_merge_bwd(dx1b, wo, proj, conv_out, attn_out, "wo_bwd_x", after=after_ffn)
    after_ffn = reducer.middle("ffn", (dco,))
    dconv_y, dattn = _matmul_group((dco, dao), (wco, wao), mode="nt", tm=2048, tn=512, out_dtype=BF16, name="branch_out_bwd_x",
                                   after=after_ffn)
    g_wco, g_wao, g_wo = _matmul_group((conv_y, attn, merged), (dco, dao, dx1b), mode="tn", tm=512, tn=1024, out_dtype=BF16,
                                       name="mix_bwd_w")
    after_mix = reducer.start("mix", dict(wco=g_wco, wao=g_wao, wo=g_wo))
    dproj, dconv_w = _conv_bwd(dconv_y, proj, conv_w, dproj, "conv_bwd", after=after_mix)
    after_mix = reducer.middle("mix", (dconv_w,))
    dproj, dkr, dv, dsinks = _swa_bwd(dattn, proj, tab, sinks, dproj, "attn_bwd", after=after_mix)
    dproj = _kv_bwd(dkr, dv, tab, dproj, "kv_bwd")
    g_win_t = _matmul(dproj, h1, mode="tn", out_dtype=BF16, name="proj_bwd_w", tm=512, tn=1024, tk=2048)
    after_in = reducer.middle("in", reducer.start("in", dict(win_t=g_win_t)))
    dh1 = _matmul(dproj, win_t, mode="nn", out_dtype=BF16, name="proj_bwd_x", tm=1024, tn=1024, tk=1664, after=after_in)
    dx, _, dg_mix = _rms_bwd(dh1, x, g_mix, dx1, "rms1_bwd")
    grads = dict(win_t=g_win_t, wgu_t=g_wgu_t, wd=g_wd, wco=g_wco, wao=g_wao, wo=g_wo)
    small = dict(g_mix=dg_mix, g_ffn=dg_ffn, g_final=dg_final, conv_w=dconv_w, sinks=dsinks, lossvec=lossvec)
    return dx, grads, small


def _position():
    return lax.axis_index("x"), lax.axis_index("y"), lax.axis_index("c")


def _other_chips(x, y):
    return [(1 - x, y), (x, 1 - y), (1 - x, 1 - y)]


SEM_SPEC = pl.BlockSpec(memory_space=pltpu.SEMAPHORE)
EFFECT = pltpu.SideEffectType.DATAFLOW_SIDE_EFFECTING
TOKEN = jax.ShapeDtypeStruct((8, 128), F32)
TOKEN_SPEC = pl.BlockSpec(memory_space=pltpu.VMEM)


def _hbm(a):
    return pltpu.with_memory_space_constraint(a, pltpu.HBM)


def _place(ws, me_idx, dtypes, name, after=()):
    n = len(ws)

    def body(i_ref, *refs):
        for w_ref, o_ref, dtype in zip(refs[:n], refs[n + len(after):], dtypes):
            o_ref[...] = w_ref[...].astype(dtype)

    grid_spec = pltpu.PrefetchScalarGridSpec(
        num_scalar_prefetch=1, grid=(1,),
        in_specs=[pl.BlockSpec(w.shape, lambda i, me: (0, 0)) for w in ws] + [HBM_SPEC] * len(after),
        out_specs=[pl.BlockSpec(w.shape, lambda i, me: (me[0], 0)) for w in ws])
    return _call(body, name=name, grid_spec=grid_spec,
                 out_shape=[_sds((N_DEV * w.shape[0], w.shape[1]), dtype) for w, dtype in zip(ws, dtypes)],
                 compiler_params=_params("arbitrary"))(me_idx, *ws, *after)


def _own_rows(ref, r, px, py, pc):
    return ref.at[pl.ds((4 * px + 2 * py + pc) * r, r), :]


def _gather_phase(bufs, waits, plans, after, name):
    n = len(bufs)
    rows = [b.shape[0] // N_DEV for b in bufs]
    nw, npl = len(waits), len(plans)

    def body(*refs):
        ins = refs[:n]
        wait_sems = refs[n:n + 2 * nw]
        out0 = n + 2 * nw + len(after)
        new_sems = refs[out0:out0 + 2 * npl]
        token = refs[-1]
        x, y, c = _position()
        for w, (_, _, sent, received) in enumerate(waits):
            for a in range(n):
                for count, wait in ((sent, "wait_send"), (received, "wait_recv")):
                    span = _whole(ins[a], count * rows[a])
                    getattr(pltpu.make_async_remote_copy(
                        src_ref=span, dst_ref=span, send_sem=wait_sems[2 * w].at[a], recv_sem=wait_sems[2 * w + 1].at[a],
                        device_id=(x, y, c), device_id_type=MESH), wait)()
        for k, plan in enumerate(plans):
            for a in range(n):
                for block, target in plan(x, y, c):
                    span = _own_rows(ins[a], rows[a], *block)
                    pltpu.make_async_remote_copy(src_ref=span, dst_ref=span, send_sem=new_sems[2 * k].at[a],
                                                 recv_sem=new_sems[2 * k + 1].at[a], device_id=target, device_id_type=MESH).start()
        token[...] = jnp.zeros_like(token)

    sem_ops = [s for send, recv, _, _ in waits for s in (send, recv)]
    outs = _call(
        body, name=name, in_specs=[HBM_SPEC] * n + [SEM_SPEC] * (2 * nw) + [HBM_SPEC] * len(after),
        out_specs=[SEM_SPEC] * (2 * npl) + [HBM_SPEC] * n + [TOKEN_SPEC],
        out_shape=[pltpu.SemaphoreType.DMA((n,))] * (2 * npl) + [pltpu.HBM(b.shape, b.dtype) for b in bufs] + [TOKEN],
        input_output_aliases={i: 2 * npl + i for i in range(n)},
        compiler_params=pltpu.CompilerParams(has_side_effects=EFFECT),
    )(*[_hbm(b) for b in bufs], *sem_ops, *after)
    pairs = [(outs[2 * k], outs[2 * k + 1]) for k in range(npl)]
    return pairs, list(outs[2 * npl:2 * npl + n]), outs[-1]


def _own_to_near(x, y, c):
    return [((x, y, c), (x, y, 1 - c)), ((x, y, c), (1 - x, y, c)), ((x, y, c), (x, 1 - y, c))]


def _near_to_sibling(x, y, c):
    return [((1 - x, y, c), (x, y, 1 - c)), ((x, 1 - y, c), (x, y, 1 - c))]


def _relay_diagonal(x, y, c):
    north = c
    source = (x * north + (1 - x) * (1 - north), (1 - y) * north + y * (1 - north), c)
    target = ((1 - x) * north + x * (1 - north), y * north + (1 - y) * (1 - north), c)
    return [(source, target)]


def _diagonal_to_sibling(x, y, c):
    return [((1 - x, 1 - y, c), (x, y, 1 - c))]


def _gather_start(bufs, groups, name, after=()):
    n = len(bufs)
    rows = [b.shape[0] // N_DEV for b in bufs]
    ng = len(groups)

    def body(*refs):
        ins = refs[:n]
        sems = refs[n + len(after):n + len(after) + 2 * ng]
        token = refs[-1]
        x, y, c = _position()
        targets = [(x, y, 1 - c)] + [(*chip, c) for chip in _other_chips(x, y)]
        for g, members in enumerate(groups):
            for slot, a in enumerate(members):
                own = _own_rows(ins[a], rows[a], x, y, c)
                for to in targets:
                    pltpu.make_async_remote_copy(src_ref=own, dst_ref=own, send_sem=sems[2 * g].at[slot],
                                                 recv_sem=sems[2 * g + 1].at[slot], device_id=to, device_id_type=MESH).start()
        token[...] = jnp.zeros_like(token)

    sem_shapes = []
    for members in groups:
        sem_shapes += [pltpu.SemaphoreType.DMA((len(members),))] * 2
    outs = _call(
        body, name=name, in_specs=[HBM_SPEC] * (n + len(after)),
        out_specs=[SEM_SPEC] * (2 * ng) + [HBM_SPEC] * n + [TOKEN_SPEC],
        out_shape=sem_shapes + [pltpu.HBM(b.shape, b.dtype) for b in bufs] + [TOKEN],
        input_output_aliases={i: 2 * ng + i for i in range(n)},
        compiler_params=pltpu.CompilerParams(has_side_effects=EFFECT),
    )(*[_hbm(b) for b in bufs], *after)
    sem_pairs = [(outs[2 * g], outs[2 * g + 1]) for g in range(ng)]
    return sem_pairs, list(outs[2 * ng:2 * ng + n]), outs[-1]


def _gather_forward(send_sems, recv_sems, bufs, after, name):
    n = len(bufs)
    rows = [b.shape[0] // N_DEV for b in bufs]

    def body(*refs):
        ins = refs[:n]
        send1, recv1 = refs[n], refs[n + 1]
        out0 = n + 2 + len(after)
        send2, recv2 = refs[out0], refs[out0 + 1]
        token = refs[-1]
        x, y, c = _position()
        for a in range(n):
            step1 = pltpu.make_async_remote_copy(
                src_ref=_whole(ins[a], 4 * rows[a]), dst_ref=_whole(ins[a], 4 * rows[a]), send_sem=send1.at[a],
                recv_sem=recv1.at[a], device_id=(x, y, c), device_id_type=MESH)
            step1.wait_send()
            step1.wait_recv()
        for a in range(n):
            for chip in _other_chips(x, y):
                blk = _own_rows(ins[a], rows[a], *chip, c)
                pltpu.make_async_remote_copy(src_ref=blk, dst_ref=blk, send_sem=send2.at[a], recv_sem=recv2.at[a],
                                             device_id=(x, y, 1 - c), device_id_type=MESH).start()
        token[...] = jnp.zeros_like(token)

    outs = _call(
        body, name=name, in_specs=[HBM_SPEC] * n + [SEM_SPEC, SEM_SPEC] + [HBM_SPEC] * len(after),
        out_specs=[SEM_SPEC, SEM_SPEC] + [HBM_SPEC] * n + [TOKEN_SPEC],
        out_shape=[pltpu.SemaphoreType.DMA((n,)), pltpu.SemaphoreType.DMA((n,))]
        + [pltpu.HBM(b.shape, b.dtype) for b in bufs] + [TOKEN],
        input_output_aliases={i: 2 + i for i in range(n)},
        compiler_params=pltpu.CompilerParams(has_side_effects=EFFECT),
    )(*bufs, send_sems, recv_sems, *after)
    return outs[0], outs[1], list(outs[2:2 + n]), outs[-1]


def _gather_done(send_sems, recv_sems, bufs, after, name):
    n = len(bufs)
    rows = [b.shape[0] // N_DEV for b in bufs]

    def body(*refs):
        ins = refs[:n]
        send2, recv2 = refs[n], refs[n + 1]
        x, y, c = _position()
        for a in range(n):
            step2 = pltpu.make_async_remote_copy(
                src_ref=_whole(ins[a], 3 * rows[a]), dst_ref=_whole(ins[a], 3 * rows[a]), send_sem=send2.at[a],
                recv_sem=recv2.at[a], device_id=(x, y, c), device_id_type=MESH)
            step2.wait_send()
            step2.wait_recv()

    outs = _call(
        body, name=name, in_specs=[HBM_SPEC] * n + [SEM_SPEC, SEM_SPEC] + [HBM_SPEC] * len(after),
        out_specs=[HBM_SPEC] * n, out_shape=[pltpu.HBM(b.shape, b.dtype) for b in bufs],
        input_output_aliases={i: i for i in range(n)},
        compiler_params=pltpu.CompilerParams(has_side_effects=EFFECT),
    )(*bufs, send_sems, recv_sems, *after)
    return list(outs)


def _whole(ref, nrows):
    return ref.at[pl.ds(0, nrows), :]


def _to_sibling(x, y, c):
    return [(2 * q + (1 - c), q, (x, y, 1 - c)) for q in range(4)]


def _to_chips(x, y, c):
    return [(2 * px + py, j, (px, py, c)) for j, (px, py) in enumerate(_other_chips(x, y))]


def _exchange_start(exchanges, name):
    members = [(e, a, src, src.shape[0] // slots, plan)
               for e, (srcs, slots, plan) in enumerate(exchanges) for a, src in enumerate(srcs)]
    n, n_sems = len(members), 2 * len(exchanges)
    lands = [lax.empty((len(plan(0, 0, 0)) * r, src.shape[1]), src.dtype) for _, _, src, r, plan in members]

    def body(*refs):
        ins, land_refs, sems = refs[:n], refs[n:2 * n], refs[2 * n:2 * n + n_sems]
        token = refs[-1]
        for i, (e, a, _, r, plan) in enumerate(members):
            for src_slot, dst_slot, target in plan(*_position()):
                pltpu.make_async_remote_copy(
                    src_ref=ins[i].at[pl.ds(src_slot * r, r), :], dst_ref=land_refs[i].at[pl.ds(dst_slot * r, r), :],
                    send_sem=sems[2 * e].at[a], recv_sem=sems[2 * e + 1].at[a], device_id=target, device_id_type=MESH).start()
        token[...] = jnp.zeros_like(token)

    sem_shapes = [pltpu.SemaphoreType.DMA((len(srcs),)) for srcs, _, _ in exchanges for _ in range(2)]
    outs = _call(
        body, name=name, in_specs=[HBM_SPEC] * (2 * n),
        out_specs=[SEM_SPEC] * n_sems + [HBM_SPEC] * (2 * n) + [TOKEN_SPEC],
        out_shape=sem_shapes + [pltpu.HBM(m[2].shape, m[2].dtype) for m in members]
        + [pltpu.HBM(l.shape, l.dtype) for l in lands] + [TOKEN],
        input_output_aliases={i: n_sems + i for i in range(2 * n)},
        compiler_params=pltpu.CompilerParams(has_side_effects=EFFECT),
    )(*[_hbm(m[2]) for m in members], *[_hbm(l) for l in lands])
    started, at = [], 0
    for e, (srcs, _, _) in enumerate(exchanges):
        k = len(srcs)
        started.append((outs[2 * e], outs[2 * e + 1], list(outs[n_sems + at:n_sems + at + k]),
                        list(outs[n_sems + n + at:n_sems + n + at + k])))
        at += k
    return started, outs[-1]


def _exchange_wait(send_sems, recv_sems, srcs, lands, after, name):
    n = len(srcs)

    def body(*refs):
        ins, land_refs = refs[:n], refs[n:2 * n]
        send_sems_ref, recv_sems_ref = refs[2 * n], refs[2 * n + 1]
        for a in range(n):
            span = _whole(land_refs[a], lands[a].shape[0])
            cp = pltpu.make_async_remote_copy(
                src_ref=span, dst_ref=span, send_sem=send_sems_ref.at[a],
                recv_sem=recv_sems_ref.at[a], device_id=_position(), device_id_type=MESH)
            cp.wait_send()
            cp.wait_recv()

    outs = _call(
        body, name=name, in_specs=[HBM_SPEC] * (2 * n) + [SEM_SPEC, SEM_SPEC] + [HBM_SPEC] * len(after),
        out_specs=[HBM_SPEC] * (2 * n),
        out_shape=[pltpu.HBM(a.shape, a.dtype) for a in srcs] + [pltpu.HBM(l.shape, l.dtype) for l in lands],
        input_output_aliases={i: i for i in range(2 * n)},
        compiler_params=pltpu.CompilerParams(has_side_effects=EFFECT),
    )(*srcs, *lands, send_sems, recv_sems, *after)
    return list(outs[:n]), list(outs[n:])


def _chip_partial(grads, recvs, idx, name):
    n = len(grads)
    rows = [recv.shape[0] // 4 for recv in recvs]

    def body(i_ref, *refs):
        del i_ref
        for g_ref, s_ref, o_ref in zip(refs[:n], refs[n:2 * n], refs[2 * n:]):
            o_ref[...] = (g_ref[...].astype(F32) + s_ref[...].astype(F32)).astype(BF16)

    grid_spec = pltpu.PrefetchScalarGridSpec(
        num_scalar_prefetch=1, grid=(3,),
        in_specs=[pl.BlockSpec((r, D), lambda t, i_ref: (2 * i_ref[1 + t] + i_ref[0], 0)) for r in rows]
        + [pl.BlockSpec((r, D), lambda t, i_ref: (i_ref[1 + t], 0)) for r in rows],
        out_specs=[pl.BlockSpec((r, D), lambda t, i_ref: (i_ref[1 + t], 0)) for r in rows])
    return _call(body, name=name, grid_spec=grid_spec, out_shape=[_sds((4 * r, D), BF16) for r in rows],
                 compiler_params=_params("arbitrary"))(idx, *grads, *recvs)


def _adamw_math(w, g, m, v):
    m2 = B1 * m + (1.0 - B1) * g
    v2 = B2 * v + (1.0 - B2) * jnp.square(g)
    m_hat = m2 / (1.0 - B1 ** STEP)
    v_hat = v2 / (1.0 - B2 ** STEP)
    return -LR * (m_hat / (jnp.sqrt(v_hat) + EPS_ADAM) + WD * w), m2, v2


def _reduce_adamw(ws, grads, from_sibling, from_chips, idx, ms, vs, name):
    n = len(ws)
    nb = 2
    tiles = [w.shape[0] // nb for w in ws]
    for w, g, s, c in zip(ws, grads, from_sibling, from_chips):
        r = w.shape[0]
        assert g.shape == (N_DEV * r, D) and s.shape == (4 * r, D) and c.shape == (3 * r, D)

    def body(i_ref, *refs):
        del i_ref
        ins, outs = refs[:8 * n], refs[8 * n:]
        for a in range(n):
            w_ref, p_ref, s_ref, r0_ref, r1_ref, r2_ref, m_ref, v_ref = ins[8 * a:8 * a + 8]
            g_ref, d_ref, nm_ref, nv_ref = outs[4 * a:4 * a + 4]
            g = p_ref[...].astype(F32) + s_ref[...].astype(F32)
            g = ((g + r0_ref[...].astype(F32)) + r1_ref[...].astype(F32)) + r2_ref[...].astype(F32)
            g_ref[...] = g
            d_ref[...], nm_ref[...], nv_ref[...] = _adamw_math(w_ref[...], g, m_ref[...], v_ref[...])

    in_specs, out_specs, operands, out_shape = [], [], [], []
    for a, tr in enumerate(tiles):
        own = pl.BlockSpec((tr, D), lambda i, i_ref: (i, 0))
        in_specs += [own, pl.BlockSpec((tr, D), lambda i, i_ref: (i_ref[0] * nb + i, 0)),
                     pl.BlockSpec((tr, D), lambda i, i_ref: (i_ref[1] * nb + i, 0))]
        in_specs += [pl.BlockSpec((tr, D), lambda i, i_ref, j=j: (j * nb + i, 0)) for j in range(3)] + [own, own]
        operands += [ws[a], grads[a], from_sibling[a], from_chips[a], from_chips[a], from_chips[a], ms[a], vs[a]]
        out_specs += [own] * 4
        out_shape += [_sds(ws[a].shape, F32)] * 4
    grid_spec = pltpu.PrefetchScalarGridSpec(num_scalar_prefetch=1, grid=(nb,), in_specs=in_specs, out_specs=out_specs)
    outs = _call(body, name=name, grid_spec=grid_spec, out_shape=out_shape, compiler_params=_params("parallel"))(idx, *operands)
    return [tuple(outs[4 * a:4 * a + 4]) for a in range(n)]


SMALL_ROWS = 8


def _small_all_reduce(pack, name, after=()):
    def body(p_ref, *rest):
        tot_ref, loss_ref, gath, send_sems, recv_sems = rest[len(after):]
        x, y, c = _position()
        me_id = 4 * x + 2 * y + c
        gath[me_id] = p_ref[...]
        copies = []
        for k in range(1, N_DEV):
            peer = tuple(1 - v if (k >> b) & 1 else v for v, b in ((x, 2), (y, 1), (c, 0)))
            cp = pltpu.make_async_remote_copy(src_ref=p_ref, dst_ref=gath.at[me_id], send_sem=send_sems.at[k - 1],
                                              recv_sem=recv_sems.at[k - 1], device_id=peer, device_id_type=MESH)
            cp.start()
            copies.append(cp)
        for cp in copies:
            cp.wait_recv()
        for cp in copies:
            cp.wait_send()
        tot = gath[0]
        for d in range(1, N_DEV):
            tot = tot + gath[d]
        tot_ref[...] = tot
        loss_ref[...] = jnp.full((1, 128), (0.5 / D) * jnp.sum(tot[SMALL_ROWS - 1:SMALL_ROWS, :]), F32)

    vm = pl.BlockSpec(memory_space=pltpu.VMEM)
    return _call(
        body, name=name, in_specs=[vm] + [HBM_SPEC] * len(after), out_specs=[vm, vm],
        out_shape=[_sds((SMALL_ROWS, D), F32), _sds((1, 128), F32)],
        scratch_shapes=[pltpu.VMEM((N_DEV, SMALL_ROWS, D), F32), pltpu.SemaphoreType.DMA((N_DEV - 1,)),
                        pltpu.SemaphoreType.DMA((N_DEV - 1,))],
    )(pack, *after)


def _adamw_small(ws, gs, ms, vs, name):
    n = len(ws)

    def body(*refs):
        for a in range(n):
            w_ref, g_ref, m_ref, v_ref = (refs[k * n + a] for k in range(4))
            d_ref, nm_ref, nv_ref = (refs[(4 + k) * n + a] for k in range(3))
            d_ref[...], nm_ref[...], nv_ref[...] = _adamw_math(w_ref[...], g_ref[...], m_ref[...], v_ref[...])

    vm = pl.BlockSpec(memory_space=pltpu.VMEM)
    outs = _call(body, name=name, in_specs=[vm] * (4 * n), out_specs=[vm] * (3 * n),
                 out_shape=[_sds(w.shape, F32) for w in ws] * 3)(*ws, *gs, *ms, *vs)
    return [(outs[a], outs[n + a], outs[2 * n + a]) for a in range(n)]


def kernel(x, g_mix, w_in, conv_w, attn_sinks, w_conv_out, w_attn_out, w_o, g_ffn, w_gate_up, w_down, g_final, loss_target, m_g_mix, m_w_in, m_conv_w, m_attn_sinks, m_w_conv_out, m_w_attn_out, m_w_o, m_g_ffn, m_w_gate_up, m_w_down, m_g_final, v_g_mix, v_w_in, v_conv_w, v_attn_sinks, v_w_conv_out, v_w_attn_out, v_w_o, v_g_ffn, v_w_gate_up, v_w_down, v_g_final):
    cx, cy, cc = _position()
    chip = 2 * cx + cy
    partial_idx = jnp.stack([cc, 2 * (1 - cx) + cy, 2 * cx + (1 - cy), 2 * (1 - cx) + (1 - cy)]).astype(jnp.int32)
    own_idx = jnp.stack([2 * chip + cc, chip]).astype(jnp.int32)
    me = 4 * cx + 2 * cy + cc

    me_idx = jnp.reshape(me, (1,)).astype(jnp.int32)
    first = _place([jnp.transpose(w_in[0]), jnp.pad(conv_w[0], ((0, 5), (0, 0)))], me_idx, (BF16, F32), "place_in")
    (to_near,), first, token_in = _gather_phase(first, [], [_own_to_near], (), "gather_in_start")
    gather_tokens = (token_in,)

    class Gathered:
        def __init__(self):
            self.state = {}

        def begin(self, group, after):
            if group == "in":
                (near, relay), bufs, token = _gather_phase(
                    first, [(*to_near, 3, 3)], [_near_to_sibling, _relay_diagonal], after, "gather_in_relay")
                later = [_place([w], me_idx, (BF16,), "place_" + k, after=(token,))[0] for k, w in (
                    ("w_conv_out", w_conv_out[0]), ("w_attn_out", w_attn_out[0]), ("w_o", w_o[0]),
                    ("w_gate_up", jnp.transpose(w_gate_up[0])), ("w_down", w_down[0]))]
                (sems_mix, sems_ffn), later, token_later = _gather_start(later, [[0, 1, 2], [3, 4]], "gather_start_later")
                self.state.update({"in": (near, relay, bufs), "mix": (sems_mix, later[:3]), "ffn": (sems_ffn, later[3:])})
                return (token_later,)
            (send_sems, recv_sems), group_bufs = self.state[group]
            send2, recv2, group_bufs, token = _gather_forward(send_sems, recv_sems, group_bufs, after, "gather_forward_" + group)
            self.state[group] = ((send2, recv2), group_bufs)
            return (token,)

        def end(self, group, after):
            if group == "in":
                near, relay, bufs = self.state[group]
                (last,), bufs, token = _gather_phase(bufs, [(*relay, 1, 1)], [_diagonal_to_sibling], after, "gather_in_last")
                _, full, _ = _gather_phase(bufs, [(*near, 2, 2), (*last, 1, 1)], [], (token,), "gather_in_done")
                return full[0], jnp.transpose(full[1].reshape(N_DEV, 8, 128)[:, :3, :], (1, 0, 2)).reshape(3, D)
            (send2, recv2), group_bufs = self.state[group]
            return _gather_done(send2, recv2, group_bufs, after, "gather_done_" + group)

    in_flight, own_pieces = {}, {}

    transposed = ("w_in", "w_gate_up")

    def as2d(k, a):
        if k in transposed:
            return jnp.transpose(a[0])
        return a[None] if a.ndim == 1 else (a[0] if a.ndim == 3 else a)

    w_all = {"g_mix": g_mix, "w_in": w_in, "conv_w": conv_w, "attn_sinks": attn_sinks, "w_conv_out": w_conv_out,
             "w_attn_out": w_attn_out, "w_o": w_o, "g_ffn": g_ffn, "w_gate_up": w_gate_up, "w_down": w_down, "g_final": g_final}
    m_all = {"g_mix": m_g_mix, "w_in": m_w_in, "conv_w": m_conv_w, "attn_sinks": m_attn_sinks, "w_conv_out": m_w_conv_out,
             "w_attn_out": m_w_attn_out, "w_o": m_w_o, "g_ffn": m_g_ffn, "w_gate_up": m_w_gate_up, "w_down": m_w_down,
             "g_final": m_g_final}
    v_all = {"g_mix": v_g_mix, "w_in": v_w_in, "conv_w": v_conv_w, "attn_sinks": v_attn_sinks, "w_conv_out": v_w_conv_out,
             "w_attn_out": v_w_attn_out, "w_o": v_w_o, "g_ffn": v_g_ffn, "w_gate_up": v_w_gate_up, "w_down": v_w_down,
             "g_final": v_g_final}
    results = {}

    def record(k, *vals):
        results[k] = [(jnp.transpose(val) if k in transposed else val).reshape(w_all[k].shape) for val in vals]

    def update(group, names, grads, from_sibling, from_chips):
        outs = _reduce_adamw([as2d(k, w_all[k]) for k in names], grads, from_sibling, from_chips, own_idx,
                             [as2d(k, m_all[k]) for k in names], [as2d(k, v_all[k]) for k in names], "adamw_" + group)
        for k, vals in zip(names, outs):
            record(k, *vals)
        return tuple(vals[2] for vals in outs)

    def update_small(grads):
        keys = list(grads)
        outs = _adamw_small([as2d(k, w_all[k]) for k in keys], [grads[k] for k in keys], [as2d(k, m_all[k]) for k in keys],
                            [as2d(k, v_all[k]) for k in keys], "adamw_small")
        for k, (d, nm, nv) in zip(keys, outs):
            record(k, grads[k], d, nm, nv)
        return tuple(nm for _, nm, _ in outs)

    kernel_name = {"win_t": "w_in", "wgu_t": "w_gate_up", "wd": "w_down", "wco": "w_conv_out", "wao": "w_attn_out", "wo": "w_o"}

    def finish(group, after):
        keys, send_sems, recv_sems, parts, from_chips = in_flight[group]
        _, from_chips = _exchange_wait(send_sems, recv_sems, parts, from_chips, after, "rs_chips_wait_" + group)
        grads, from_sibling = own_pieces[group]
        return update(group, [kernel_name[k] for k in keys], grads, from_sibling, from_chips)

    class Reducer:
        def __init__(self):
            self.waiting = None

        def start(self, group, gdict):
            keys, glist = list(gdict), list(gdict.values())
            exchanges = [(glist, N_DEV, _to_sibling)]
            if self.waiting:
                exchanges.append((self.waiting[2], 4, _to_chips))
            started, token = _exchange_start(exchanges, "rs_sibling_start_" + group)
            in_flight[group] = (keys, *started[0])
            if self.waiting:
                in_flight[self.waiting[0]] = (self.waiting[1], *started[1])
            return (token,)

        def middle(self, group, after):
            keys, send_sems, recv_sems, glist, lands = in_flight[group]
            if group == "in":
                after = finish("ffn", after)
            glist, lands = _exchange_wait(send_sems, recv_sems, glist, lands, after, "rs_sibling_wait_" + group)
            parts = _chip_partial(glist, lands, partial_idx, "chip_partial_" + group)
            own_pieces[group] = (glist, lands)
            if group != "in":
                self.waiting = (group, keys, parts)
                return tuple(parts)
            self.waiting = None
            (started,), token = _exchange_start([(parts, 4, _to_chips)], "rs_chips_start_" + group)
            in_flight[group] = (keys, *started)
            return (token,)

    dx, _, small = _local_step(x[0], loss_target[0], g_mix, g_ffn, g_final[None], attn_sinks, Gathered(),
                               reducer=Reducer(), after=gather_tokens)
    after = finish("mix", (dx,))

    sinks_row = jnp.pad(small["sinks"], ((0, 0), (0, D - 128)))
    pack = jnp.concatenate([small["g_mix"], small["g_ffn"], small["g_final"], small["conv_w"], sinks_row, small["lossvec"]], axis=0)
    tot, loss_row = _small_all_reduce(pack, "small_all_reduce", after=after)
    loss = loss_row[0, 0]
    g_small = {
        "g_mix": tot[0:1], "g_ffn": tot[1:2], "g_final": tot[2:3],
        "conv_w": lax.dynamic_slice(tot, (3, me * 128), (3, 128)), "attn_sinks": tot[6:7, :N_HEADS],
    }
    finish("in", update_small(g_small))

    order = ["g_mix", "w_in", "conv_w", "attn_sinks", "w_conv_out", "w_attn_out", "w_o", "g_ffn", "w_gate_up", "w_down", "g_final"]
    return (loss, dx[None], *[results[k][i] for i in range(4) for k in order])
```

```python
import functools
import math

import jax
import jax.numpy as jnp
from jax import lax
from jax.experimental import pallas as pl
from jax.experimental.pallas import tpu as pltpu

F32 = jnp.float32
BF16 = jnp.bfloat16

D = 1024
HEAD_DIM = 64
N_HEADS = 16
N_KV = 4
GROUP = N_HEADS // N_KV
D_KV = N_KV * HEAD_DIM
BLOCK = 128
ROT_DIM = HEAD_DIM // 4
ROPE_THETA = 500000.0
ATTN_SCALE = 1.0 / math.sqrt(HEAD_DIM)
NEG_INF = -1e30
D_FF = 2816
N_IN = 6656
EPS = 1e-5
C_CB, C_CC, C_CX, C_Q, C_K, C_V, C_GC, C_GA = 0, 1024, 2048, 3072, 4096, 4352, 4608, 5632

LR, B1, B2, EPS_ADAM, WD, STEP = 0.001, 0.9, 0.999, 1e-08, 0.01, 10

N_DEV = 8
MESH = pl.DeviceIdType.MESH
VMEM_LIMIT = 56 * 1024 * 1024

NN = (((1,), (0,)), ((), ()))
NT = (((1,), (1,)), ((), ()))
TN = (((0,), (0,)), ((), ()))
HBM_SPEC = pl.BlockSpec(memory_space=pl.ANY)
ROW_SPLIT = 4


def _call(body, **kw):
    return pl.pallas_call(body, **kw)


def _params(*sem):
    return pltpu.CompilerParams(dimension_semantics=sem, vmem_limit_bytes=VMEM_LIMIT)


def _sds(shape, dtype):
    return jax.ShapeDtypeStruct(shape, dtype)


def _matmul(a, b, *, mode, tm, tn, tk, out_dtype, name, res=None, after=()):
    parts = list(a) if isinstance(a, (list, tuple)) else [a]
    rows_a = parts[0].shape[0]
    cols_a = sum(p.shape[1] for p in parts)
    if mode == "nn":
        (m, kk), (_, n), dims = (rows_a, cols_a), b.shape, NN
    elif mode == "nt":
        (m, kk), (n, _), dims = (rows_a, cols_a), b.shape, NT
    else:
        (kk, m), (_, n), dims = (rows_a, cols_a), b.shape, TN
    tm, tn, tk = min(tm, m), min(tn, n), min(tk, kk)
    assert m % tm == 0 and n % tn == 0 and kk % tk == 0, (name, m, n, kk, tm, tn, tk)
    nk = kk // tk
    split_axis, width = (2, tk) if mode == "nn" else (0, tm)
    assert len(parts) == 1 or mode in ("nn", "tn")
    assert len(parts) == 1 or all(p.shape[1] % width == 0 for p in parts), (name, width)
    counts = [p.shape[1] // width for p in parts]
    starts = [sum(counts[:p]) for p in range(len(parts))]

    def a_spec(p):
        def col(t):
            return jnp.clip(t - starts[p], 0, counts[p] - 1) if len(parts) > 1 else t

        if mode == "tn":
            return pl.BlockSpec((tk, tm), lambda i, j, k: (k, col(i)))
        return pl.BlockSpec((tm, tk), lambda i, j, k: (i, col(k)))

    if mode == "nt":
        b_spec = pl.BlockSpec((tn, tk), lambda i, j, k: (j, k))
    else:
        b_spec = pl.BlockSpec((tk, tn), lambda i, j, k: (k, j))
    o_spec = pl.BlockSpec((tm, tn), lambda i, j, k: (i, j))
    has_res = res is not None
    n_parts = len(parts)
    unit = 128 if mode == "tn" else 16
    split = ROW_SPLIT if tm % (ROW_SPLIT * unit) == 0 else 1

    def body(*refs):
        a_refs, b_ref = refs[:n_parts], refs[n_parts]
        r_ref = refs[n_parts + 1] if has_res else None
        o_ref = refs[n_parts + 1 + has_res + len(after)]
        k = pl.program_id(2)

        acc_ref = refs[-1] if nk > 1 else None

        def step(a_ref):
            def matmul(rows):
                a_blk = a_ref[:, rows] if mode == "tn" else a_ref[rows, :]
                return lax.dot_general(a_blk, b_ref[...], dims, preferred_element_type=F32)

            def finish(rows, part):
                if nk > 1:
                    acc_ref[rows, :] += part
                else:
                    o_ref[rows, :] = (part + r_ref[rows, :] if has_res else part).astype(o_ref.dtype)

            _row_pipeline(tm, matmul, finish, split)

        if nk > 1:
            @pl.when(k == 0)
            def _():
                acc_ref[...] = jnp.zeros_like(acc_ref)

        if n_parts == 1:
            step(a_refs[0])
        else:
            t = pl.program_id(split_axis)
            for p in range(n_parts):
                pl.when((t >= starts[p]) & (t < starts[p] + counts[p]))(functools.partial(step, a_refs[p]))

        if nk > 1:
            @pl.when(k == nk - 1)
            def _():
                o_ref[...] = (acc_ref[...] + r_ref[...] if has_res else acc_ref[...]).astype(o_ref.dtype)

    ins = parts + [b] + ([res] if has_res else []) + list(after)
    in_specs = [a_spec(p) for p in range(n_parts)] + [b_spec] + ([o_spec] if has_res else []) + [HBM_SPEC] * len(after)
    scratch = [] if nk == 1 else [pltpu.VMEM((tm, tn), F32)]
    return _call(
        body, name=name, grid=(m // tm, n // tn, nk), in_specs=in_specs, out_specs=o_spec,
        out_shape=_sds((m, n), out_dtype), scratch_shapes=scratch,
        compiler_params=_params("parallel", "parallel", "arbitrary"),
    )(*ins)


def _matmul_group(a_group, b_group, *, mode, tm, tn, out_dtype, name, after=()):
    a0, b0 = a_group[0], b_group[0]
    a_pair, b_pair, count = a_group, b_group, len(a_group)
    if mode == "nn":
        (m, kk), (_, n), dims = a0.shape, b0.shape, NN
    elif mode == "nt":
        (m, kk), (n, _), dims = a0.shape, b0.shape, NT
    else:
        (kk, m), (_, n), dims = a0.shape, b0.shape, TN
    assert all(a.shape == a0.shape for a in a_pair) and all(b.shape == b0.shape for b in b_pair)
    tm, tn = min(tm, m), min(tn, n)
    assert m % tm == 0 and n % tn == 0, (name, m, n, tm, tn)
    a_spec = pl.BlockSpec((kk, tm), lambda i, j: (0, i)) if mode == "tn" else pl.BlockSpec((tm, kk), lambda i, j: (i, 0))
    b_spec = pl.BlockSpec((tn, kk), lambda i, j: (j, 0)) if mode == "nt" else pl.BlockSpec((kk, tn), lambda i, j: (0, j))
    o_spec = pl.BlockSpec((tm, tn), lambda i, j: (i, j))
    unit = 128 if mode == "tn" else 16
    split = ROW_SPLIT if tm % (ROW_SPLIT * unit) == 0 else 1

    def body(*refs):
        a_refs, b_refs, o_refs = refs[:count], refs[count:2 * count], refs[2 * count + len(after):]

        def matmul(rows):
            return tuple(lax.dot_general(a_ref[:, rows] if mode == "tn" else a_ref[rows, :], b_ref[...], dims,
                                         preferred_element_type=F32) for a_ref, b_ref in zip(a_refs, b_refs))

        def finish(rows, parts):
            for o_ref, part in zip(o_refs, parts):
                o_ref[rows, :] = part.astype(out_dtype)

        _row_pipeline(tm, matmul, finish, split)

    return _call(
        body, name=name, grid=(m // tm, n // tn), in_specs=[a_spec] * count + [b_spec] * count + [HBM_SPEC] * len(after),
        out_specs=[o_spec] * count, out_shape=[_sds((m, n), out_dtype)] * count,
        compiler_params=_params("parallel", "parallel"),
    )(*a_pair, *b_pair, *after)


def _row_tile(s):
    return min(512, s)


def _rms_fwd(x, g, name, after=()):
    s = x.shape[0]
    tm = _row_tile(s)

    def body(x_ref, g_ref, *rest):
        h_ref = rest[-1]
        xv = x_ref[...]
        r = lax.rsqrt(jnp.mean(xv * xv, axis=-1, keepdims=True) + EPS)
        h_ref[...] = (xv * r * g_ref[...]).astype(BF16)

    row = pl.BlockSpec((tm, D), lambda i: (i, 0))
    return _call(
        body, name=name, grid=(s // tm,), in_specs=[row, pl.BlockSpec((1, D), lambda i: (0, 0))] + [HBM_SPEC] * len(after),
        out_specs=row, out_shape=_sds((s, D), BF16), compiler_params=_params("parallel"),
    )(x, g, *after)


def _rms_bwd(dh, x, g, dres, name, after=()):
    s = x.shape[0]
    tm = _row_tile(s)

    def body(dh_ref, x_ref, g_ref, dres_ref, *rest):
        dx_ref, dxb_ref, dg_ref = rest[len(after):]
        xv = x_ref[...]
        r = lax.rsqrt(jnp.mean(xv * xv, axis=-1, keepdims=True) + EPS)
        xh = xv * r
        dhv = dh_ref[...].astype(F32)
        dyg = dhv * g_ref[...]
        dx = dres_ref[...] + r * (dyg - xh * jnp.mean(dyg * xh, axis=-1, keepdims=True))
        dx_ref[...] = dx
        dxb_ref[...] = dx.astype(BF16)
        part = jnp.sum(dhv * xh, axis=0, keepdims=True)

        @pl.when(pl.program_id(0) == 0)
        def _():
            dg_ref[...] = part

        @pl.when(pl.program_id(0) > 0)
        def _():
            dg_ref[...] += part

    row = pl.BlockSpec((tm, D), lambda i: (i, 0))
    vec = pl.BlockSpec((1, D), lambda i: (0, 0))
    return _call(
        body, name=name, grid=(s // tm,), in_specs=[row, row, vec, row] + [HBM_SPEC] * len(after), out_specs=[row, row, vec],
        out_shape=[_sds((s, D), F32), _sds((s, D), BF16), _sds((1, D), F32)],
        compiler_params=_params("arbitrary"),
    )(dh, x, g, dres, *after)


def _loss_head(x2, g, tgt, name):
    s = x2.shape[0]
    tm = _row_tile(s)

    def body(x_ref, g_ref, t_ref, dx_ref, dxb_ref, dg_ref, l_ref):
        xv = x_ref[...]
        gv = g_ref[...]
        r = lax.rsqrt(jnp.mean(xv * xv, axis=-1, keepdims=True) + EPS)
        xh = xv * r
        err = xh * gv - t_ref[...]
        dy = err * (1.0 / D)
        dyg = dy * gv
        dx = r * (dyg - xh * jnp.mean(dyg * xh, axis=-1, keepdims=True))
        dx_ref[...] = dx
        dxb_ref[...] = dx.astype(BF16)
        dg_part = jnp.sum(dy * xh, axis=0, keepdims=True)
        l_part = jnp.sum(err * err, axis=0, keepdims=True)

        @pl.when(pl.program_id(0) == 0)
        def _():
            dg_ref[...] = dg_part
            l_ref[...] = l_part

        @pl.when(pl.program_id(0) > 0)
        def _():
            dg_ref[...] += dg_part
            l_ref[...] += l_part

    row = pl.BlockSpec((tm, D), lambda i: (i, 0))
    vec = pl.BlockSpec((1, D), lambda i: (0, 0))
    return _call(
        body, name=name, grid=(s // tm,), in_specs=[row, vec, row], out_specs=[row, row, vec, vec],
        out_shape=[_sds((s, D), F32), _sds((s, D), BF16), _sds((1, D), F32), _sds((1, D), F32)],
        compiler_params=_params("arbitrary"),
    )(x2, g, tgt)


CONV_TC = 256


def _shift_down(u, k, rows):
    return jnp.where(rows >= k, pltpu.roll(u, k, 0), 0.0)


def _shift_up(u, k, rows, s):
    return jnp.where(rows < s - k, pltpu.roll(u, s - k, 0), 0.0)


def _conv_specs(s):
    nb = D // CONV_TC

    def col(c0):
        return pl.BlockSpec((s, CONV_TC), lambda j, c0=c0: (0, c0 // CONV_TC + j))

    return nb, col


def _conv_fwd(proj, conv_w, name):
    s = proj.shape[0]
    nb, col = _conv_specs(s)

    def body(cb_ref, cc_ref, cx_ref, w_ref, y_ref):
        rows = lax.broadcasted_iota(jnp.int32, (s, CONV_TC), 0)
        u = cc_ref[...].astype(F32) * cx_ref[...].astype(F32)
        w = w_ref[...]
        c = w[0:1] * _shift_down(u, 2, rows) + w[1:2] * _shift_down(u, 1, rows) + w[2:3] * u
        y_ref[...] = (cb_ref[...].astype(F32) * c).astype(BF16)

    return _call(
        body, name=name, grid=(nb,),
        in_specs=[col(C_CB), col(C_CC), col(C_CX), pl.BlockSpec((3, CONV_TC), lambda j: (0, j))],
        out_specs=pl.BlockSpec((s, CONV_TC), lambda j: (0, j)), out_shape=_sds((s, D), BF16),
        compiler_params=_params("parallel"),
    )(proj, proj, proj, conv_w)


def _write_behind(t, nt, buf, sems, tiles, window, where):
    slot = t % 2

    def copies(sl, at):
        return [pltpu.make_async_copy(buf.at[sl, p], window(p, at), sems.at[sl, p]) for p in range(len(tiles))]

    @pl.when(t >= 2)
    def _():
        for cp in copies(slot, where):
            cp.wait()

    for p, tile in enumerate(tiles):
        buf[slot, p] = tile
    started = copies(slot, where)
    for cp in started:
        cp.start()

    @pl.when(t == nt - 1)
    def _():
        for cp in started:
            cp.wait()
        if nt > 1:
            for cp in copies(1 - slot, where):
                cp.wait()


def _conv_bwd(dy, proj, conv_w, dproj, name, after=()):
    s = proj.shape[0]
    nb, col = _conv_specs(s)

    def body(dy_ref, cb_ref, cc_ref, cx_ref, w_ref, *rest):
        dproj_ref, dw_ref, buf, sems = rest[1 + len(after):]
        j = pl.program_id(0)
        rows = lax.broadcasted_iota(jnp.int32, (s, CONV_TC), 0)
        cc = cc_ref[...].astype(F32)
        cx = cx_ref[...].astype(F32)
        u = cc * cx
        u1 = _shift_down(u, 1, rows)
        u2 = _shift_down(u, 2, rows)
        w = w_ref[...]
        c = w[0:1] * u2 + w[1:2] * u1 + w[2:3] * u
        dyv = dy_ref[...].astype(F32)
        dc = dyv * cb_ref[...].astype(F32)
        du = w[2:3] * dc + w[1:2] * _shift_up(dc, 1, rows, s) + w[0:1] * _shift_up(dc, 2, rows, s)

        def window(p, jj):
            start = pl.multiple_of((C_CB, C_CC, C_CX)[p] + jj * CONV_TC, CONV_TC)
            return dproj_ref.at[:, pl.ds(start, CONV_TC)]

        tiles = ((dyv * c).astype(BF16), (du * cx).astype(BF16), (du * cc).astype(BF16))
        _write_behind(j * 0, 1, buf, sems, tiles, window, j)
        dw_ref[...] = jnp.concatenate(
            [jnp.sum(dc * u2, axis=0, keepdims=True), jnp.sum(dc * u1, axis=0, keepdims=True),
             jnp.sum(dc * u, axis=0, keepdims=True)], axis=0)

    return _call(
        body, name=name, grid=(nb,),
        in_specs=[pl.BlockSpec((s, CONV_TC), lambda j: (0, j)), col(C_CB), col(C_CC), col(C_CX),
                  pl.BlockSpec((3, CONV_TC), lambda j: (0, j))] + [HBM_SPEC] * (1 + len(after)),
        out_specs=[pl.BlockSpec(memory_space=pl.ANY), pl.BlockSpec((3, CONV_TC), lambda j: (0, j))],
        out_shape=[_sds((s, N_IN), BF16), _sds((3, D), F32)],
        scratch_shapes=[pltpu.VMEM((1, 3, s, CONV_TC), BF16), pltpu.SemaphoreType.DMA((1, 3))],
        input_output_aliases={5: 0}, compiler_params=_params("arbitrary"),
    )(dy, proj, proj, proj, conv_w, dproj, *after)


def _rope_tables(s):
    half = ROT_DIM // 2
    inv_freq = ROPE_THETA ** (-jnp.arange(0, ROT_DIM, 2, dtype=F32) / ROT_DIM)
    inv64 = jnp.concatenate([inv_freq, inv_freq, jnp.zeros((HEAD_DIM - ROT_DIM,), F32)])
    ang = jnp.arange(s, dtype=F32)[:, None] * jnp.concatenate([inv64, inv64])[None, :]
    d = lax.broadcasted_iota(jnp.int32, (s, 128), 1) % HEAD_DIM
    cos, sin = jnp.cos(ang), jnp.sin(ang)
    c = jnp.where(d < ROT_DIM, cos, 1.0)
    a = jnp.where(d < half, -sin, 0.0)
    b = jnp.where((d >= half) & (d < ROT_DIM), sin, 0.0)
    return jnp.concatenate([c, a, b], axis=1)


def _rope(x, tab):
    c, a, b = tab[:, 0:128], tab[:, 128:256], tab[:, 256:384]
    outs = []
    for i in range(x.shape[1] // 128):
        xc = x[:, i * 128:(i + 1) * 128]
        outs.append(xc * c + pltpu.roll(xc, 120, 1) * a + pltpu.roll(xc, 8, 1) * b)
    return outs[0] if len(outs) == 1 else jnp.concatenate(outs, axis=1)


def _rope_t(dx, tab):
    c, a, b = tab[:, 0:128], tab[:, 128:256], tab[:, 256:384]
    outs = []
    for i in range(dx.shape[1] // 128):
        dc = dx[:, i * 128:(i + 1) * 128]
        outs.append(dc * c + pltpu.roll(dc * a, 8, 1) + pltpu.roll(dc * b, 120, 1))
    return outs[0] if len(outs) == 1 else jnp.concatenate(outs, axis=1)


def _attn_in_specs():
    prev = lambda n: jnp.maximum(n - 1, 0)
    return [
        pl.BlockSpec((BLOCK, D), lambda n: (n, C_Q // D)),
        pl.BlockSpec((BLOCK, D_KV), lambda n: (n, C_K // D_KV)),
        pl.BlockSpec((BLOCK, D_KV), lambda n: (prev(n), C_K // D_KV)),
        pl.BlockSpec((BLOCK, D_KV), lambda n: (n, C_V // D_KV)),
        pl.BlockSpec((BLOCK, D_KV), lambda n: (prev(n), C_V // D_KV)),
        pl.BlockSpec((BLOCK, 384), lambda n: (n, 0)),
        pl.BlockSpec((BLOCK, 384), lambda n: (prev(n), 0)),
        pl.BlockSpec(memory_space=pltpu.SMEM),
    ]


HALF = HEAD_DIM
N_CHUNK = D // 128


def _swa_bias(n):
    qi = lax.broadcasted_iota(jnp.int32, (BLOCK, 2 * BLOCK), 0)
    kj = lax.broadcasted_iota(jnp.int32, (BLOCK, 2 * BLOCK), 1)
    rel = qi + BLOCK - kj
    valid = (rel >= 0) & (rel < BLOCK) & ((kj >= BLOCK) | (n > 0))
    return jnp.where(valid, 0.0, NEG_INF)


def _halves(x):
    lo = lax.broadcasted_iota(jnp.int32, x.shape, 1) < HALF
    return jnp.where(lo, x, 0.0).astype(BF16), jnp.where(lo, 0.0, x).astype(BF16)


def _dup_heads(x):
    out = []
    for pair in range(N_KV // 2):
        xc = x[:, pair * 128:(pair + 1) * 128]
        xr = pltpu.roll(xc, HALF, 1)
        lo = lax.broadcasted_iota(jnp.int32, xc.shape, 1) < HALF
        out += [jnp.where(lo, xc, xr), jnp.where(lo, xr, xc)]
    return out


def _swa_load(q_ref, kc_ref, kp_ref, vc_ref, vp_ref, tc_ref, tp_ref):
    qf = _rope(q_ref[...].astype(F32), tc_ref[...]) * ATTN_SCALE
    q_halves = [_halves(qf[:, c * 128:(c + 1) * 128]) for c in range(N_CHUNK)]
    kf = jnp.concatenate([_rope(kp_ref[...].astype(F32), tp_ref[...]), _rope(kc_ref[...].astype(F32), tc_ref[...])], axis=0)
    vf = jnp.concatenate([vp_ref[...], vc_ref[...]], axis=0).astype(F32)
    return q_halves, _dup_heads(kf), _dup_heads(vf)


def _swa_probs(qh, kk, bias, sink):
    s = lax.dot_general(qh, kk, NT, preferred_element_type=F32) + bias
    m = jnp.maximum(jnp.max(jnp.maximum(s[:, :BLOCK], s[:, BLOCK:]), axis=1, keepdims=True), sink)
    return jnp.exp(s - m), m


def _swa_fwd(proj, tab, sinks, name, after=()):
    s = proj.shape[0]

    def body(q_ref, kc_ref, kp_ref, vc_ref, vp_ref, tc_ref, tp_ref, sink_ref, *rest):
        o_ref = rest[-1]
        n = pl.program_id(0)
        q_halves, kdup, vdup = _swa_load(q_ref, kc_ref, kp_ref, vc_ref, vp_ref, tc_ref, tp_ref)
        bias = _swa_bias(n)
        ones = jnp.ones((2 * BLOCK, 128), BF16)
        kk = [k.astype(BF16) for k in kdup]
        vv = [[jnp.concatenate([v_half, ones], axis=1) for v_half in _halves(v)] for v in vdup]
        heads = [(c, half) for c in range(N_CHUNK) for half in range(2)]
        scores = [lax.dot_general(q_halves[c][half], kk[c // (GROUP // 2)], NT, preferred_element_type=F32)
                  for c, half in heads]
        probs = []
        for (c, half), sc in zip(heads, scores):
            sc = sc + bias
            m = jnp.maximum(jnp.max(jnp.maximum(sc[:, :BLOCK], sc[:, BLOCK:]), axis=1, keepdims=True), sink_ref[0, 2 * c + half])
            probs.append((jnp.exp(sc - m).astype(BF16), jnp.exp(sink_ref[0, 2 * c + half] - m)))
        outs = [lax.dot_general(e, vv[c // (GROUP // 2)][half], NN, preferred_element_type=F32)
                for (c, half), (e, _) in zip(heads, probs)]
        for c in range(N_CHUNK):
            parts = [outs[2 * c + half][:, :128] * (1.0 / (outs[2 * c + half][:, 128:] + probs[2 * c + half][1]))
                     for half in range(2)]
            o_ref[:, c * 128:(c + 1) * 128] = (parts[0] + parts[1]).astype(BF16)

    return _call(
        body, name=name, grid=(s // BLOCK,), in_specs=_attn_in_specs() + [HBM_SPEC] * len(after),
        out_specs=pl.BlockSpec((BLOCK, D), lambda n: (n, 0)), out_shape=_sds((s, D), BF16),
        compiler_params=_params("parallel"),
    )(proj, proj, proj, proj, proj, tab, tab, sinks, *after)


def _swa_bwd(do, proj, tab, sinks, dproj, name, after=()):
    s = proj.shape[0]
    nblk = s // BLOCK
    kv_of = lambda c: c // (GROUP // 2)

    def body(do_ref, q_ref, kc_ref, kp_ref, vc_ref, vp_ref, tc_ref, tp_ref, sink_ref, *rest):
        dproj_ref, dk_ref, dv_ref, ds_ref, dqout, dkbuf, dvbuf, sems = rest[1 + len(after):]
        n = pl.program_id(0)

        @pl.when(n == 0)
        def _():
            dk_ref[...] = jnp.zeros_like(dk_ref)
            dv_ref[...] = jnp.zeros_like(dv_ref)
            ds_ref[...] = jnp.zeros_like(ds_ref)

        q_halves, kdup, vdup = _swa_load(q_ref, kc_ref, kp_ref, vc_ref, vp_ref, tc_ref, tp_ref)
        dof = do_ref[...].astype(F32)
        do_halves = [_halves(dof[:, c * 128:(c + 1) * 128]) for c in range(N_CHUNK)]
        bias = _swa_bias(n)
        ones = jnp.ones((2 * BLOCK, 128), BF16)
        kk = [k.astype(BF16) for k in kdup]
        vv = [v.astype(BF16) for v in vdup]
        k_halves = [_halves(k) for k in kdup]
        heads = [(c, half) for c in range(N_CHUNK) for half in range(2)]
        lane_row = lax.broadcasted_iota(jnp.int32, (1, 128), 1)
        lo_kv = lax.broadcasted_iota(jnp.int32, (2 * BLOCK, 128), 1) < HALF
        scores = [lax.dot_general(q_halves[c][half], kk[kv_of(c)], NT, preferred_element_type=F32) for c, half in heads]
        dps = [lax.dot_general(do_halves[c][half], vv[kv_of(c)], NT, preferred_element_type=F32) for c, half in heads]
        exps = []
        for (c, half), sc in zip(heads, scores):
            sink = sink_ref[0, 2 * c + half]
            sc = sc + bias
            m = jnp.maximum(jnp.max(jnp.maximum(sc[:, :BLOCK], sc[:, BLOCK:]), axis=1, keepdims=True), sink)
            exps.append((jnp.exp(sc - m), jnp.exp(sink - m)))
        sums = [lax.dot_general(e.astype(BF16), ones, NN, preferred_element_type=F32) for e, _ in exps]
        dsink_row = jnp.zeros((1, 128), F32)
        dsb, pb = [], []
        for h, ((e, es), row_sum, dp) in enumerate(zip(exps, sums, dps)):
            inv = 1.0 / (row_sum + es)
            p = e * jnp.concatenate([inv, inv], axis=1)
            t = p * dp
            delta = jnp.sum(t, axis=1, keepdims=True)
            dsb.append((t - p * delta).astype(BF16))
            pb.append(p.astype(BF16))
            dsink = -jnp.sum(es * inv * delta, axis=0, keepdims=True)
            dsink_row = dsink_row + jnp.where(lane_row == h, dsink, 0.0)
        dq_parts = [lax.dot_general(d, k_halves[kv_of(c)][half], NN, preferred_element_type=F32) for (c, half), d in zip(heads, dsb)]
        dk_parts = [lax.dot_general(d, q_halves[c][half], TN, preferred_element_type=F32) for (c, half), d in zip(heads, dsb)]
        dv_parts = [lax.dot_general(p, do_halves[c][half], TN, preferred_element_type=F32) for (c, half), p in zip(heads, pb)]
        dq = jnp.concatenate([(dq_parts[2 * c] + dq_parts[2 * c + 1]) * ATTN_SCALE for c in range(N_CHUNK)], axis=1)

        def kv_sum(parts, hk):
            acc = (parts[GROUP * hk] + parts[GROUP * hk + 1]) + (parts[GROUP * hk + 2] + parts[GROUP * hk + 3])
            return acc + pltpu.roll(acc, HALF, 1)

        for pair in range(N_KV // 2):
            dkbuf[:, pair * 128:(pair + 1) * 128] = jnp.where(lo_kv, kv_sum(dk_parts, 2 * pair), kv_sum(dk_parts, 2 * pair + 1))
            dvbuf[:, pair * 128:(pair + 1) * 128] = jnp.where(lo_kv, kv_sum(dv_parts, 2 * pair), kv_sum(dv_parts, 2 * pair + 1))
        prev0 = pl.multiple_of(jnp.maximum(n - 1, 0) * BLOCK, BLOCK)
        cur0 = pl.multiple_of(n * BLOCK, BLOCK)

        @pl.when(n > 0)
        def _():
            dk_ref[pl.ds(prev0, BLOCK), :] += dkbuf[0:BLOCK, :]
            dv_ref[pl.ds(prev0, BLOCK), :] += dvbuf[0:BLOCK, :]

        dk_ref[pl.ds(cur0, BLOCK), :] += dkbuf[BLOCK:2 * BLOCK, :]
        dv_ref[pl.ds(cur0, BLOCK), :] += dvbuf[BLOCK:2 * BLOCK, :]
        ds_ref[...] += dsink_row

        def window(p, at):
            return dproj_ref.at[pl.ds(pl.multiple_of(at * BLOCK, BLOCK), BLOCK), pl.ds(C_Q, D)]

        _write_behind(n, nblk, dqout, sems, (_rope_t(dq, tc_ref[...]).astype(BF16),), window, n)

    blk = lambda w: pl.BlockSpec((BLOCK, w), lambda n: (n, 0))
    whole = lambda w: pl.BlockSpec((s, w), lambda n: (0, 0))
    n_in = 1 + len(_attn_in_specs())
    return _call(
        body, name=name, grid=(nblk,), in_specs=[blk(D)] + _attn_in_specs() + [HBM_SPEC] * (1 + len(after)),
        out_specs=[HBM_SPEC, whole(D_KV), whole(D_KV), pl.BlockSpec((1, 128), lambda n: (0, 0))],
        out_shape=[_sds((s, N_IN), BF16), _sds((s, D_KV), F32), _sds((s, D_KV), F32), _sds((1, 128), F32)],
        scratch_shapes=[pltpu.VMEM((2, 1, BLOCK, D), BF16), pltpu.VMEM((2 * BLOCK, D_KV), F32),
                        pltpu.VMEM((2 * BLOCK, D_KV), F32), pltpu.SemaphoreType.DMA((2, 1))],
        input_output_aliases={n_in: 0}, compiler_params=_params("arbitrary"),
    )(do, proj, proj, proj, proj, proj, tab, tab, sinks, dproj, *after)


def _kv_bwd(dkr, dv, tab, dproj, name):
    s = dkr.shape[0]
    tm = _row_tile(s)

    def body(dk_ref, dv_ref, t_ref, dproj_in, o_ref):
        del dproj_in
        o_ref[:, 0:D_KV] = _rope_t(dk_ref[...], t_ref[...]).astype(BF16)
        o_ref[:, D_KV:2 * D_KV] = dv_ref[...].astype(BF16)

    row = lambda w: pl.BlockSpec((tm, w), lambda i: (i, 0))
    return _call(
        body, name=name, grid=(s // tm,),
        in_specs=[row(D_KV), row(D_KV), row(384), pl.BlockSpec(memory_space=pl.ANY)],
        out_specs=pl.BlockSpec((tm, 2 * D_KV), lambda i: (i, C_K // (2 * D_KV))),
        out_shape=_sds((s, N_IN), BF16), input_output_aliases={3: 0}, compiler_params=_params("parallel"),
    )(dkr, dv, tab, dproj)


EW_TC = 512


def _sigmoid(x):
    return 0.5 * jnp.tanh(0.5 * x) + 0.5


def _branches_merge_fwd(conv_y, attn, wco, wao, proj, name):
    s = proj.shape[0]
    tm = min(2048, s)

    def body(y_ref, a_ref, wc_ref, wa_ref, gc_ref, ga_ref, co_ref, ao_ref, m_ref):
        def matmuls(rows):
            return (lax.dot_general(y_ref[rows, :], wc_ref[...], NN, preferred_element_type=F32),
                    lax.dot_general(a_ref[rows, :], wa_ref[...], NN, preferred_element_type=F32))

        def finish(rows, parts):
            co, ao = parts
            co_ref[rows, :] = co.astype(BF16)
            ao_ref[rows, :] = ao.astype(BF16)
            m_ref[rows, :] = (_sigmoid(gc_ref[rows, :].astype(F32)) * co + _sigmoid(ga_ref[rows, :].astype(F32)) * ao).astype(BF16)

        _row_pipeline(tm, matmuls, finish)

    act = pl.BlockSpec((tm, D), lambda i, j: (i, 0))
    wgt = pl.BlockSpec((D, EW_TC), lambda i, j: (0, j))
    tile = pl.BlockSpec((tm, EW_TC), lambda i, j: (i, j))
    return _call(
        body, name=name, grid=(s // tm, D // EW_TC),
        in_specs=[act, act, wgt, wgt, pl.BlockSpec((tm, EW_TC), lambda i, j: (i, C_GC // EW_TC + j)),
                  pl.BlockSpec((tm, EW_TC), lambda i, j: (i, C_GA // EW_TC + j))],
        out_specs=[tile, tile, tile], out_shape=[_sds((s, D), BF16)] * 3, compiler_params=_params("parallel", "parallel"),
    )(conv_y, attn, wco, wao, proj, proj)


def _wo_merge_bwd(dx1b, wo, proj, conv_out, attn_out, name, after=()):
    s = proj.shape[0]
    tm = min(1024, s)
    nj = D // EW_TC

    def body(dx_ref, w_ref, gc_ref, ga_ref, co_ref, ao_ref, *rest):
        dproj_ref, dco_ref, dao_ref, buf, sems = rest[len(after):]
        i, j = pl.program_id(0), pl.program_id(1)
        gate_c, gate_a = [], []

        def matmul(rows):
            return lax.dot_general(dx_ref[rows, :], w_ref[...], NT, preferred_element_type=F32)

        def finish(rows, dm):
            sc = _sigmoid(gc_ref[rows, :].astype(F32))
            sa = _sigmoid(ga_ref[rows, :].astype(F32))
            dco_ref[rows, :] = (dm * sc).astype(BF16)
            dao_ref[rows, :] = (dm * sa).astype(BF16)
            gate_c.append((dm * co_ref[rows, :].astype(F32) * sc * (1.0 - sc)).astype(BF16))
            gate_a.append((dm * ao_ref[rows, :].astype(F32) * sa * (1.0 - sa)).astype(BF16))

        _row_pipeline(tm, matmul, finish)

        def window(p, at):
            start = pl.multiple_of((C_GC, C_GA)[p] + at[1] * EW_TC, EW_TC)
            return dproj_ref.at[pl.ds(pl.multiple_of(at[0] * tm, tm), tm), pl.ds(start, EW_TC)]

        tiles = (jnp.concatenate(gate_c, axis=0), jnp.concatenate(gate_a, axis=0))
        _write_behind(i * nj + j, (s // tm) * nj, buf, sems, tiles, window, (i, j))

    tile = pl.BlockSpec((tm, EW_TC), lambda i, j: (i, j))
    return _call(
        body, name=name, grid=(s // tm, nj),
        in_specs=[pl.BlockSpec((tm, D), lambda i, j: (i, 0)), pl.BlockSpec((EW_TC, D), lambda i, j: (j, 0)),
                  pl.BlockSpec((tm, EW_TC), lambda i, j: (i, C_GC // EW_TC + j)),
                  pl.BlockSpec((tm, EW_TC), lambda i, j: (i, C_GA // EW_TC + j)), tile, tile] + [HBM_SPEC] * len(after),
        out_specs=[HBM_SPEC, tile, tile],
        out_shape=[_sds((s, N_IN), BF16), _sds((s, D), BF16), _sds((s, D), BF16)],
        scratch_shapes=[pltpu.VMEM((2, 2, tm, EW_TC), BF16), pltpu.SemaphoreType.DMA((2, 2))],
        compiler_params=_params("arbitrary", "arbitrary"),
    )(dx1b, wo, proj, proj, conv_out, attn_out, *after)


FF_TC = 256
FF_TM = 2048


def _row_pipeline(tm, matmul, finish, split=ROW_SPLIT):
    step = tm // split
    pending = None
    for r in range(split):
        rows = pl.ds(r * step, step)
        result = matmul(rows)
        if pending is not None:
            finish(*pending)
        pending = (rows, result)
    finish(*pending)


def _gate_up_fwd(h2, wgu_t, name):
    s = h2.shape[0]
    tm = min(FF_TM, s)
    nb = D_FF // FF_TC

    def body(h_ref, wg_ref, wu_ref, a_ref, g_ref, u_ref):
        def matmuls(rows):
            h = h_ref[rows, :]
            return (lax.dot_general(h, wg_ref[...], NT, preferred_element_type=F32),
                    lax.dot_general(h, wu_ref[...], NT, preferred_element_type=F32))

        def finish(rows, gu):
            g, u = gu
            a_ref[rows, :] = (g * _sigmoid(g) * u).astype(BF16)
            g_ref[rows, :] = g.astype(BF16)
            u_ref[rows, :] = u.astype(BF16)

        _row_pipeline(tm, matmuls, finish)

    tile = pl.BlockSpec((tm, FF_TC), lambda j, i: (i, j))
    return _call(
        body, name=name, grid=(nb, s // tm),
        in_specs=[pl.BlockSpec((tm, D), lambda j, i: (i, 0)), pl.BlockSpec((FF_TC, D), lambda j, i: (j, 0)),
                  pl.BlockSpec((FF_TC, D), lambda j, i: (nb + j, 0))],
        out_specs=[tile, tile, tile], out_shape=[_sds((s, D_FF), BF16)] * 3,
        compiler_params=_params("parallel", "parallel"),
    )(h2, wgu_t, wgu_t)


def _down_bwd_x(dx2b, wd, gate, up, name):
    s = dx2b.shape[0]
    tm = min(FF_TM, s)
    nb = D_FF // FF_TC

    def body(dx_ref, w_ref, g_ref, u_ref, dg_ref, du_ref):
        def matmul(rows):
            return lax.dot_general(dx_ref[rows, :], w_ref[...], NT, preferred_element_type=F32)

        def finish(rows, da):
            g = g_ref[rows, :].astype(F32)
            sg = _sigmoid(g)
            dg_ref[rows, :] = (da * u_ref[rows, :].astype(F32) * (sg * (1.0 + g * (1.0 - sg)))).astype(BF16)
            du_ref[rows, :] = (da * (g * sg)).astype(BF16)

        _row_pipeline(tm, matmul, finish)

    tile = pl.BlockSpec((tm, FF_TC), lambda j, i: (i, j))
    return _call(
        body, name=name, grid=(nb, s // tm),
        in_specs=[pl.BlockSpec((tm, D), lambda j, i: (i, 0)), pl.BlockSpec((FF_TC, D), lambda j, i: (j, 0)), tile, tile],
        out_specs=[tile, tile], out_shape=[_sds((s, D_FF), BF16)] * 2,
        compiler_params=_params("parallel", "parallel"),
    )(dx2b, wd, gate, up)


class _Weights:
    def __init__(self, **groups):
        self.groups = groups

    def begin(self, group, after):
        return ()

    def end(self, group, after):
        return self.groups[group]


class _NoReduce:
    def start(self, group, grads):
        return ()

    def middle(self, group, after):
        return ()


def _local_step(x, tgt, g_mix, g_ffn, g_final, sinks, weights, reducer=None, after=()):
    reducer = reducer or _NoReduce()
    s = x.shape[0]
    tab = _rope_tables(s)
    big = dict(tm=2048, tn=512, tk=1024)
    h1 = _rms_fwd(x, g_mix, "rms1_fwd", after=after)
    win_t, conv_w = weights.end("in", (*weights.begin("in", (h1,)), tab))
    proj = _matmul(h1, win_t, mode="nt", out_dtype=BF16, name="proj_fwd", tm=2048, tn=512, tk=1024)
    attn = _swa_fwd(proj, tab, sinks, "attn_fwd", after=weights.begin("mix", (proj,)))
    wco, wao, wo = weights.end("mix", (attn,))
    conv_y = _conv_fwd(proj, conv_w, "conv_fwd")
    conv_out, attn_out, merged = _branches_merge_fwd(conv_y, attn, wco, wao, proj, "branch_out_fwd")
    x1 = _matmul(merged, wo, mode="nn", out_dtype=F32, name="wo_fwd", res=x, **big)
    h2 = _rms_fwd(x1, g_ffn, "rms2_fwd", after=weights.begin("ffn", (x1,)))
    wgu_t, wd = weights.end("ffn", (h2,))
    act, gate, up = _gate_up_fwd(h2, wgu_t, "gate_up_fwd")
    x2 = _matmul(act, wd, mode="nn", out_dtype=F32, name="down_fwd", res=x1, tm=1024, tn=512, tk=D_FF)
    dx2, dx2b, dg_final, lossvec = _loss_head(x2, g_final, tgt, "loss_head")
    dgate, dup = _down_bwd_x(dx2b, wd, gate, up, "down_bwd_x")
    g_wd = _matmul(act, dx2b, mode="tn", out_dtype=BF16, name="down_bwd_w", tm=1408, tn=1024, tk=2048)
    dh2 = _matmul([dgate, dup], wgu_t, mode="nn", out_dtype=BF16, name="gate_up_bwd_x", tm=1024, tn=1024, tk=1408)
    g_wgu_t = _matmul([dgate, dup], h2, mode="tn", out_dtype=BF16, name="gate_up_bwd_w", tm=1408, tn=1024, tk=2048)
    after_ffn = reducer.start("ffn", dict(wgu_t=g_wgu_t, wd=g_wd))
    dx1, dx1b, dg_ffn = _rms_bwd(dh2, x1, g_ffn, dx2, "rms2_bwd")
    dproj, dco, dao = _wo_merge_bwd(dx1b, wo, proj, conv_out, attn_out, "wo_bwd_x", after=after_ffn)
    after_ffn = reducer.middle("ffn", (dco,))
    dconv_y, dattn = _matmul_group((dco, dao), (wco, wao), mode="nt", tm=2048, tn=512, out_dtype=BF16, name="branch_out_bwd_x",
                                   after=after_ffn)
    g_wco, g_wao, g_wo = _matmul_group((conv_y, attn, merged), (dco, dao, dx1b), mode="tn", tm=512, tn=1024, out_dtype=BF16,
                                       name="mix_bwd_w")
    after_mix = reducer.start("mix", dict(wco=g_wco, wao=g_wao, wo=g_wo))
    dproj, dconv_w = _conv_bwd(dconv_y, proj, conv_w, dproj, "conv_bwd", after=after_mix)
    after_mix = reducer.middle("mix", (dconv_w,))
    dproj, dkr, dv, dsinks = _swa_bwd(dattn, proj, tab, sinks, dproj, "attn_bwd", after=after_mix)
    dproj = _kv_bwd(dkr, dv, tab, dproj, "kv_bwd")
    g_win_t = _matmul(dproj, h1, mode="tn", out_dtype=BF16, name="proj_bwd_w", tm=512, tn=1024, tk=2048)
    after_in = reducer.middle("in", reducer.start("in", dict(win_t=g_win_t)))
    dh1 = _matmul(dproj, win_t, mode="nn", out_dtype=BF16, name="proj_bwd_x", tm=1024, tn=1024, tk=1664, after=after_in)
    dx, _, dg_mix = _rms_bwd(dh1, x, g_mix, dx1, "rms1_bwd")
    grads = dict(win_t=g_win_t, wgu_t=g_wgu_t, wd=g_wd, wco=g_wco, wao=g_wao, wo=g_wo)
    small = dict(g_mix=dg_mix, g_ffn=dg_ffn, g_final=dg_final, conv_w=dconv_w, sinks=dsinks, lossvec=lossvec)
    return dx, grads, small


def _position():
    return lax.axis_index("x"), lax.axis_index("y"), lax.axis_index("c")


def _other_chips(x, y):
    return [(1 - x, y), (x, 1 - y), (1 - x, 1 - y)]


SEM_SPEC = pl.BlockSpec(memory_space=pltpu.SEMAPHORE)
EFFECT = pltpu.SideEffectType.DATAFLOW_SIDE_EFFECTING
TOKEN = jax.ShapeDtypeStruct((8, 128), F32)
TOKEN_SPEC = pl.BlockSpec(memory_space=pltpu.VMEM)


def _hbm(a):
    return pltpu.with_memory_space_constraint(a, pltpu.HBM)


def _place(ws, me_idx, dtypes, name, after=()):
    n = len(ws)

    def body(i_ref, *refs):
        for w_ref, o_ref, dtype in zip(refs[:n], refs[n + len(after):], dtypes):
            o_ref[...] = w_ref[...].astype(dtype)

    grid_spec = pltpu.PrefetchScalarGridSpec(
        num_scalar_prefetch=1, grid=(1,),
        in_specs=[pl.BlockSpec(w.shape, lambda i, me: (0, 0)) for w in ws] + [HBM_SPEC] * len(after),
        out_specs=[pl.BlockSpec(w.shape, lambda i, me: (me[0], 0)) for w in ws])
    return _call(body, name=name, grid_spec=grid_spec,
                 out_shape=[_sds((N_DEV * w.shape[0], w.shape[1]), dtype) for w, dtype in zip(ws, dtypes)],
                 compiler_params=_params("arbitrary"))(me_idx, *ws, *after)


def _own_rows(ref, r, px, py, pc):
    return ref.at[pl.ds((4 * px + 2 * py + pc) * r, r), :]


def _gather_phase(bufs, waits, plans, after, name):
    n = len(bufs)
    rows = [b.shape[0] // N_DEV for b in bufs]
    nw, npl = len(waits), len(plans)

    def body(*refs):
        ins = refs[:n]
        wait_sems = refs[n:n + 2 * nw]
        out0 = n + 2 * nw + len(after)
        new_sems = refs[out0:out0 + 2 * npl]
        token = refs[-1]
        x, y, c = _position()
        for w, (_, _, sent, received) in enumerate(waits):
            for a in range(n):
                for count, wait in ((sent, "wait_send"), (received, "wait_recv")):
                    span = _whole(ins[a], count * rows[a])
                    getattr(pltpu.make_async_remote_copy(
                        src_ref=span, dst_ref=span, send_sem=wait_sems[2 * w].at[a], recv_sem=wait_sems[2 * w + 1].at[a],
                        device_id=(x, y, c), device_id_type=MESH), wait)()
        for k, plan in enumerate(plans):
            for a in range(n):
                for block, target in plan(x, y, c):
                    span = _own_rows(ins[a], rows[a], *block)
                    pltpu.make_async_remote_copy(src_ref=span, dst_ref=span, send_sem=new_sems[2 * k].at[a],
                                                 recv_sem=new_sems[2 * k + 1].at[a], device_id=target, device_id_type=MESH).start()
        token[...] = jnp.zeros_like(token)

    sem_ops = [s for send, recv, _, _ in waits for s in (send, recv)]
    outs = _call(
        body, name=name, in_specs=[HBM_SPEC] * n + [SEM_SPEC] * (2 * nw) + [HBM_SPEC] * len(after),
        out_specs=[SEM_SPEC] * (2 * npl) + [HBM_SPEC] * n + [TOKEN_SPEC],
        out_shape=[pltpu.SemaphoreType.DMA((n,))] * (2 * npl) + [pltpu.HBM(b.shape, b.dtype) for b in bufs] + [TOKEN],
        input_output_aliases={i: 2 * npl + i for i in range(n)},
        compiler_params=pltpu.CompilerParams(has_side_effects=EFFECT),
    )(*[_hbm(b) for b in bufs], *sem_ops, *after)
    pairs = [(outs[2 * k], outs[2 * k + 1]) for k in range(npl)]
    return pairs, list(outs[2 * npl:2 * npl + n]), outs[-1]


def _own_to_near(x, y, c):
    return [((x, y, c), (x, y, 1 - c)), ((x, y, c), (1 - x, y, c)), ((x, y, c), (x, 1 - y, c))]


def _near_to_sibling(x, y, c):
    return [((1 - x, y, c), (x, y, 1 - c)), ((x, 1 - y, c), (x, y, 1 - c))]


def _relay_diagonal(x, y, c):
    north = c
    source = (x * north + (1 - x) * (1 - north), (1 - y) * north + y * (1 - north), c)
    target = ((1 - x) * north + x * (1 - north), y * north + (1 - y) * (1 - north), c)
    return [(source, target)]


def _diagonal_to_sibling(x, y, c):
    return [((1 - x, 1 - y, c), (x, y, 1 - c))]


def _gather_start(bufs, groups, name, after=()):
    n = len(bufs)
    rows = [b.shape[0] // N_DEV for b in bufs]
    ng = len(groups)

    def body(*refs):
        ins = refs[:n]
        sems = refs[n + len(after):n + len(after) + 2 * ng]
        token = refs[-1]
        x, y, c = _position()
        targets = [(x, y, 1 - c)] + [(*chip, c) for chip in _other_chips(x, y)]
        for g, members in enumerate(groups):
            for slot, a in enumerate(members):
                own = _own_rows(ins[a], rows[a], x, y, c)
                for to in targets:
                    pltpu.make_async_remote_copy(src_ref=own, dst_ref=own, send_sem=sems[2 * g].at[slot],
                                                 recv_sem=sems[2 * g + 1].at[slot], device_id=to, device_id_type=MESH).start()
        token[...] = jnp.zeros_like(token)

    sem_shapes = []
    for members in groups:
        sem_shapes += [pltpu.SemaphoreType.DMA((len(members),))] * 2
    outs = _call(
        body, name=name, in_specs=[HBM_SPEC] * (n + len(after)),
        out_specs=[SEM_SPEC] * (2 * ng) + [HBM_SPEC] * n + [TOKEN_SPEC],
        out_shape=sem_shapes + [pltpu.HBM(b.shape, b.dtype) for b in bufs] + [TOKEN],
        input_output_aliases={i: 2 * ng + i for i in range(n)},
        compiler_params=pltpu.CompilerParams(has_side_effects=EFFECT),
    )(*[_hbm(b) for b in bufs], *after)
    sem_pairs = [(outs[2 * g], outs[2 * g + 1]) for g in range(ng)]
    return sem_pairs, list(outs[2 * ng:2 * ng + n]), outs[-1]


def _gather_forward(send_sems, recv_sems, bufs, after, name):
    n = len(bufs)
    rows = [b.shape[0] // N_DEV for b in bufs]

    def body(*refs):
        ins = refs[:n]
        send1, recv1 = refs[n], refs[n + 1]
        out0 = n + 2 + len(after)
        send2, recv2 = refs[out0], refs[out0 + 1]
        token = refs[-1]
        x, y, c = _position()
        for a in range(n):
            step1 = pltpu.make_async_remote_copy(
                src_ref=_whole(ins[a], 4 * rows[a]), dst_ref=_whole(ins[a], 4 * rows[a]), send_sem=send1.at[a],
                recv_sem=recv1.at[a], device_id=(x, y, c), device_id_type=MESH)
            step1.wait_send()
            step1.wait_recv()
        for a in range(n):
            for chip in _other_chips(x, y):
                blk = _own_rows(ins[a], rows[a], *chip, c)
                pltpu.make_async_remote_copy(src_ref=blk, dst_ref=blk, send_sem=send2.at[a], recv_sem=recv2.at[a],
                                             device_id=(x, y, 1 - c), device_id_type=MESH).start()
        token[...] = jnp.zeros_like(token)

    outs = _call(
        body, name=name, in_specs=[HBM_SPEC] * n + [SEM_SPEC, SEM_SPEC] + [HBM_SPEC] * len(after),
        out_specs=[SEM_SPEC, SEM_SPEC] + [HBM_SPEC] * n + [TOKEN_SPEC],
        out_shape=[pltpu.SemaphoreType.DMA((n,)), pltpu.SemaphoreType.DMA((n,))]
        + [pltpu.HBM(b.shape, b.dtype) for b in bufs] + [TOKEN],
        input_output_aliases={i: 2 + i for i in range(n)},
        compiler_params=pltpu.CompilerParams(has_side_effects=EFFECT),
    )(*bufs, send_sems, recv_sems, *after)
    return outs[0], outs[1], list(outs[2:2 + n]), outs[-1]


def _gather_done(send_sems, recv_sems, bufs, after, name):
    n = len(bufs)
    rows = [b.shape[0] // N_DEV for b in bufs]

    def body(*refs):
        ins = refs[:n]
        send2, recv2 = refs[n], refs[n + 1]
        x, y, c = _position()
        for a in range(n):
            step2 = pltpu.make_async_remote_copy(
                src_ref=_whole(ins[a], 3 * rows[a]), dst_ref=_whole(ins[a], 3 * rows[a]), send_sem=send2.at[a],
                recv_sem=recv2.at[a], device_id=(x, y, c), device_id_type=MESH)
            step2.wait_send()
            step2.wait_recv()

    outs = _call(
        body, name=name, in_specs=[HBM_SPEC] * n + [SEM_SPEC, SEM_SPEC] + [HBM_SPEC] * len(after),
        out_specs=[HBM_SPEC] * n, out_shape=[pltpu.HBM(b.shape, b.dtype) for b in bufs],
        input_output_aliases={i: i for i in range(n)},
        compiler_params=pltpu.CompilerParams(has_side_effects=EFFECT),
    )(*bufs, send_sems, recv_sems, *after)
    return list(outs)


def _whole(ref, nrows):
    return ref.at[pl.ds(0, nrows), :]


def _to_sibling(x, y, c):
    return [(2 * q + (1 - c), q, (x, y, 1 - c)) for q in range(4)]


def _to_chips(x, y, c):
    return [(2 * px + py, j, (px, py, c)) for j, (px, py) in enumerate(_other_chips(x, y))]


def _exchange_start(exchanges, name):
    members = [(e, a, src, src.shape[0] // slots, plan)
               for e, (srcs, slots, plan) in enumerate(exchanges) for a, src in enumerate(srcs)]
    n, n_sems = len(members), 2 * len(exchanges)
    lands = [lax.empty((len(plan(0, 0, 0)) * r, src.shape[1]), src.dtype) for _, _, src, r, plan in members]

    def body(*refs):
        ins, land_refs, sems = refs[:n], refs[n:2 * n], refs[2 * n:2 * n + n_sems]
        token = refs[-1]
        for i, (e, a, _, r, plan) in enumerate(members):
            for src_slot, dst_slot, target in plan(*_position()):
                pltpu.make_async_remote_copy(
                    src_ref=ins[i].at[pl.ds(src_slot * r, r), :], dst_ref=land_refs[i].at[pl.ds(dst_slot * r, r), :],
                    send_sem=sems[2 * e].at[a], recv_sem=sems[2 * e + 1].at[a], device_id=target, device_id_type=MESH).start()
        token[...] = jnp.zeros_like(token)

    sem_shapes = [pltpu.SemaphoreType.DMA((len(srcs),)) for srcs, _, _ in exchanges for _ in range(2)]
    outs = _call(
        body, name=name, in_specs=[HBM_SPEC] * (2 * n),
        out_specs=[SEM_SPEC] * n_sems + [HBM_SPEC] * (2 * n) + [TOKEN_SPEC],
        out_shape=sem_shapes + [pltpu.HBM(m[2].shape, m[2].dtype) for m in members]
        + [pltpu.HBM(l.shape, l.dtype) for l in lands] + [TOKEN],
        input_output_aliases={i: n_sems + i for i in range(2 * n)},
        compiler_params=pltpu.CompilerParams(has_side_effects=EFFECT),
    )(*[_hbm(m[2]) for m in members], *[_hbm(l) for l in lands])
    started, at = [], 0
    for e, (srcs, _, _) in enumerate(exchanges):
        k = len(srcs)
        started.append((outs[2 * e], outs[2 * e + 1], list(outs[n_sems + at:n_sems + at + k]),
                        list(outs[n_sems + n + at:n_sems + n + at + k])))
        at += k
    return started, outs[-1]


def _exchange_wait(send_sems, recv_sems, srcs, lands, after, name):
    n = len(srcs)

    def body(*refs):
        ins, land_refs = refs[:n], refs[n:2 * n]
        send_sems_ref, recv_sems_ref = refs[2 * n], refs[2 * n + 1]
        for a in range(n):
            span = _whole(land_refs[a], lands[a].shape[0])
            cp = pltpu.make_async_remote_copy(
                src_ref=span, dst_ref=span, send_sem=send_sems_ref.at[a],
                recv_sem=recv_sems_ref.at[a], device_id=_position(), device_id_type=MESH)
            cp.wait_send()
            cp.wait_recv()

    outs = _call(
        body, name=name, in_specs=[HBM_SPEC] * (2 * n) + [SEM_SPEC, SEM_SPEC] + [HBM_SPEC] * len(after),
        out_specs=[HBM_SPEC] * (2 * n),
        out_shape=[pltpu.HBM(a.shape, a.dtype) for a in srcs] + [pltpu.HBM(l.shape, l.dtype) for l in lands],
        input_output_aliases={i: i for i in range(2 * n)},
        compiler_params=pltpu.CompilerParams(has_side_effects=EFFECT),
    )(*srcs, *lands, send_sems, recv_sems, *after)
    return list(outs[:n]), list(outs[n:])


def _chip_partial(grads, recvs, idx, name):
    n = len(grads)
    rows = [recv.shape[0] // 4 for recv in recvs]

    def body(i_ref, *refs):
        del i_ref
        for g_ref, s_ref, o_ref in zip(refs[:n], refs[n:2 * n], refs[2 * n:]):
            o_ref[...] = (g_ref[...].astype(F32) + s_ref[...].astype(F32)).astype(BF16)

    grid_spec = pltpu.PrefetchScalarGridSpec(
        num_scalar_prefetch=1, grid=(3,),
        in_specs=[pl.BlockSpec((r, D), lambda t, i_ref: (2 * i_ref[1 + t] + i_ref[0], 0)) for r in rows]
        + [pl.BlockSpec((r, D), lambda t, i_ref: (i_ref[1 + t], 0)) for r in rows],
        out_specs=[pl.BlockSpec((r, D), lambda t, i_ref: (i_ref[1 + t], 0)) for r in rows])
    return _call(body, name=name, grid_spec=grid_spec, out_shape=[_sds((4 * r, D), BF16) for r in rows],
                 compiler_params=_params("arbitrary"))(idx, *grads, *recvs)


def _adamw_math(w, g, m, v):
    m2 = B1 * m + (1.0 - B1) * g
    v2 = B2 * v + (1.0 - B2) * jnp.square(g)
    m_hat = m2 / (1.0 - B1 ** STEP)
    v_hat = v2 / (1.0 - B2 ** STEP)
    return -LR * (m_hat / (jnp.sqrt(v_hat) + EPS_ADAM) + WD * w), m2, v2


def _reduce_adamw(ws, grads, from_sibling, from_chips, idx, ms, vs, name):
    n = len(ws)
    nb = 2
    tiles = [w.shape[0] // nb for w in ws]
    for w, g, s, c in zip(ws, grads, from_sibling, from_chips):
        r = w.shape[0]
        assert g.shape == (N_DEV * r, D) and s.shape == (4 * r, D) and c.shape == (3 * r, D)

    def body(i_ref, *refs):
        del i_ref
        ins, outs = refs[:8 * n], refs[8 * n:]
        for a in range(n):
            w_ref, p_ref, s_ref, r0_ref, r1_ref, r2_ref, m_ref, v_ref = ins[8 * a:8 * a + 8]
            g_ref, d_ref, nm_ref, nv_ref = outs[4 * a:4 * a + 4]
            g = p_ref[...].astype(F32) + s_ref[...].astype(F32)
            g = ((g + r0_ref[...].astype(F32)) + r1_ref[...].astype(F32)) + r2_ref[...].astype(F32)
            g_ref[...] = g
            d_ref[...], nm_ref[...], nv_ref[...] = _adamw_math(w_ref[...], g, m_ref[...], v_ref[...])

    in_specs, out_specs, operands, out_shape = [], [], [], []
    for a, tr in enumerate(tiles):
        own = pl.BlockSpec((tr, D), lambda i, i_ref: (i, 0))
        in_specs += [own, pl.BlockSpec((tr, D), lambda i, i_ref: (i_ref[0] * nb + i, 0)),
                     pl.BlockSpec((tr, D), lambda i, i_ref: (i_ref[1] * nb + i, 0))]
        in_specs += [pl.BlockSpec((tr, D), lambda i, i_ref, j=j: (j * nb + i, 0)) for j in range(3)] + [own, own]
        operands += [ws[a], grads[a], from_sibling[a], from_chips[a], from_chips[a], from_chips[a], ms[a], vs[a]]
        out_specs += [own] * 4
        out_shape += [_sds(ws[a].shape, F32)] * 4
    grid_spec = pltpu.PrefetchScalarGridSpec(num_scalar_prefetch=1, grid=(nb,), in_specs=in_specs, out_specs=out_specs)
    outs = _call(body, name=name, grid_spec=grid_spec, out_shape=out_shape, compiler_params=_params("parallel"))(idx, *operands)
    return [tuple(outs[4 * a:4 * a + 4]) for a in range(n)]


SMALL_ROWS = 8


def _small_all_reduce(pack, name, after=()):
    def body(p_ref, *rest):
        tot_ref, loss_ref, gath, send_sems, recv_sems = rest[len(after):]
        x, y, c = _position()
        me_id = 4 * x + 2 * y + c
        gath[me_id] = p_ref[...]
        copies = []
        for k in range(1, N_DEV):
            peer = tuple(1 - v if (k >> b) & 1 else v for v, b in ((x, 2), (y, 1), (c, 0)))
            cp = pltpu.make_async_remote_copy(src_ref=p_ref, dst_ref=gath.at[me_id], send_sem=send_sems.at[k - 1],
                                              recv_sem=recv_sems.at[k - 1], device_id=peer, device_id_type=MESH)
            cp.start()
            copies.append(cp)
        for cp in copies:
            cp.wait_recv()
        for cp in copies:
            cp.wait_send()
        tot = gath[0]
        for d in range(1, N_DEV):
            tot = tot + gath[d]
        tot_ref[...] = tot
        loss_ref[...] = jnp.full((1, 128), (0.5 / D) * jnp.sum(tot[SMALL_ROWS - 1:SMALL_ROWS, :]), F32)

    vm = pl.BlockSpec(memory_space=pltpu.VMEM)
    return _call(
        body, name=name, in_specs=[vm] + [HBM_SPEC] * len(after), out_specs=[vm, vm],
        out_shape=[_sds((SMALL_ROWS, D), F32), _sds((1, 128), F32)],
        scratch_shapes=[pltpu.VMEM((N_DEV, SMALL_ROWS, D), F32), pltpu.SemaphoreType.DMA((N_DEV - 1,)),
                        pltpu.SemaphoreType.DMA((N_DEV - 1,))],
    )(pack, *after)


def _adamw_small(ws, gs, ms, vs, name):
    n = len(ws)

    def body(*refs):
        for a in range(n):
            w_ref, g_ref, m_ref, v_ref = (refs[k * n + a] for k in range(4))
            d_ref, nm_ref, nv_ref = (refs[(4 + k) * n + a] for k in range(3))
            d_ref[...], nm_ref[...], nv_ref[...] = _adamw_math(w_ref[...], g_ref[...], m_ref[...], v_ref[...])

    vm = pl.BlockSpec(memory_space=pltpu.VMEM)
    outs = _call(body, name=name, in_specs=[vm] * (4 * n), out_specs=[vm] * (3 * n),
                 out_shape=[_sds(w.shape, F32) for w in ws] * 3)(*ws, *gs, *ms, *vs)
    return [(outs[a], outs[n + a], outs[2 * n + a]) for a in range(n)]


def kernel(x, g_mix, w_in, conv_w, attn_sinks, w_conv_out, w_attn_out, w_o, g_ffn, w_gate_up, w_down, g_final, loss_target, m_g_mix, m_w_in, m_conv_w, m_attn_sinks, m_w_conv_out, m_w_attn_out, m_w_o, m_g_ffn, m_w_gate_up, m_w_down, m_g_final, v_g_mix, v_w_in, v_conv_w, v_attn_sinks, v_w_conv_out, v_w_attn_out, v_w_o, v_g_ffn, v_w_gate_up, v_w_down, v_g_final):
    cx, cy, cc = _position()
    chip = 2 * cx + cy
    partial_idx = jnp.stack([cc, 2 * (1 - cx) + cy, 2 * cx + (1 - cy), 2 * (1 - cx) + (1 - cy)]).astype(jnp.int32)
    own_idx = jnp.stack([2 * chip + cc, chip]).astype(jnp.int32)
    me = 4 * cx + 2 * cy + cc

    me_idx = jnp.reshape(me, (1,)).astype(jnp.int32)
    first = _place([jnp.transpose(w_in[0]), jnp.pad(conv_w[0], ((0, 5), (0, 0)))], me_idx, (BF16, F32), "place_in")
    (to_near,), first, token_in = _gather_phase(first, [], [_own_to_near], (), "gather_in_start")
    gather_tokens = (token_in,)

    class Gathered:
        def __init__(self):
            self.state = {}

        def begin(self, group, after):
            if group == "in":
                (near, relay), bufs, token = _gather_phase(
                    first, [(*to_near, 3, 3)], [_near_to_sibling, _relay_diagonal], after, "gather_in_relay")
                later = [_place([w], me_idx, (BF16,), "place_" + k, after=(token,))[0] for k, w in (
                    ("w_conv_out", w_conv_out[0]), ("w_attn_out", w_attn_out[0]), ("w_o", w_o[0]),
                    ("w_gate_up", jnp.transpose(w_gate_up[0])), ("w_down", w_down[0]))]
                (sems_mix, sems_ffn), later, token_later = _gather_start(later, [[0, 1, 2], [3, 4]], "gather_start_later")
                self.state.update({"in": (near, relay, bufs), "mix": (sems_mix, later[:3]), "ffn": (sems_ffn, later[3:])})
                return (token_later,)
            (send_sems, recv_sems), group_bufs = self.state[group]
            send2, recv2, group_bufs, token = _gather_forward(send_sems, recv_sems, group_bufs, after, "gather_forward_" + group)
            self.state[group] = ((send2, recv2), group_bufs)
            return (token,)

        def end(self, group, after):
            if group == "in":
                near, relay, bufs = self.state[group]
                (last,), bufs, token = _gather_phase(bufs, [(*relay, 1, 1)], [_diagonal_to_sibling], after, "gather_in_last")
                _, full, _ = _gather_phase(bufs, [(*near, 2, 2), (*last, 1, 1)], [], (token,), "gather_in_done")
                return full[0], jnp.transpose(full[1].reshape(N_DEV, 8, 128)[:, :3, :], (1, 0, 2)).reshape(3, D)
            (send2, recv2), group_bufs = self.state[group]
            return _gather_done(send2, recv2, group_bufs, after, "gather_done_" + group)

    in_flight, own_pieces = {}, {}

    transposed = ("w_in", "w_gate_up")

    def as2d(k, a):
        if k in transposed:
            return jnp.transpose(a[0])
        return a[None] if a.ndim == 1 else (a[0] if a.ndim == 3 else a)

    w_all = {"g_mix": g_mix, "w_in": w_in, "conv_w": conv_w, "attn_sinks": attn_sinks, "w_conv_out": w_conv_out,
             "w_attn_out": w_attn_out, "w_o": w_o, "g_ffn": g_ffn, "w_gate_up": w_gate_up, "w_down": w_down, "g_final": g_final}
    m_all = {"g_mix": m_g_mix, "w_in": m_w_in, "conv_w": m_conv_w, "attn_sinks": m_attn_sinks, "w_conv_out": m_w_conv_out,
             "w_attn_out": m_w_attn_out, "w_o": m_w_o, "g_ffn": m_g_ffn, "w_gate_up": m_w_gate_up, "w_down": m_w_down,
             "g_final": m_g_final}
    v_all = {"g_mix": v_g_mix, "w_in": v_w_in, "conv_w": v_conv_w, "attn_sinks": v_attn_sinks, "w_conv_out": v_w_conv_out,
             "w_attn_out": v_w_attn_out, "w_o": v_w_o, "g_ffn": v_g_ffn, "w_gate_up": v_w_gate_up, "w_down": v_w_down,
             "g_final": v_g_final}
    results = {}

    def record(k, *vals):
        results[k] = [(jnp.transpose(val) if k in transposed else val).reshape(w_all[k].shape) for val in vals]

    def update(group, names, grads, from_sibling, from_chips):
        outs = _reduce_adamw([as2d(k, w_all[k]) for k in names], grads, from_sibling, from_chips, own_idx,
                             [as2d(k, m_all[k]) for k in names], [as2d(k, v_all[k]) for k in names], "adamw_" + group)
        for k, vals in zip(names, outs):
            record(k, *vals)
        return tuple(vals[2] for vals in outs)

    def update_small(grads):
        keys = list(grads)
        outs = _adamw_small([as2d(k, w_all[k]) for k in keys], [grads[k] for k in keys], [as2d(k, m_all[k]) for k in keys],
                            [as2d(k, v_all[k]) for k in keys], "adamw_small")
        for k, (d, nm, nv) in zip(keys, outs):
            record(k, grads[k], d, nm, nv)
        return tuple(nm for _, nm, _ in outs)

    kernel_name = {"win_t": "w_in", "wgu_t": "w_gate_up", "wd": "w_down", "wco": "w_conv_out", "wao": "w_attn_out", "wo": "w_o"}

    def finish(group, after):
        keys, send_sems, recv_sems, parts, from_chips = in_flight[group]
        _, from_chips = _exchange_wait(send_sems, recv_sems, parts, from_chips, after, "rs_chips_wait_" + group)
        grads, from_sibling = own_pieces[group]
        return update(group, [kernel_name[k] for k in keys], grads, from_sibling, from_chips)

    class Reducer:
        def __init__(self):
            self.waiting = None

        def start(self, group, gdict):
            keys, glist = list(gdict), list(gdict.values())
            exchanges = [(glist, N_DEV, _to_sibling)]
            if self.waiting:
                exchanges.append((self.waiting[2], 4, _to_chips))
            started, token = _exchange_start(exchanges, "rs_sibling_start_" + group)
            in_flight[group] = (keys, *started[0])
            if self.waiting:
                in_flight[self.waiting[0]] = (self.waiting[1], *started[1])
            return (token,)

        def middle(self, group, after):
            keys, send_sems, recv_sems, glist, lands = in_flight[group]
            if group == "in":
                after = finish("ffn", after)
            glist, lands = _exchange_wait(send_sems, recv_sems, glist, lands, after, "rs_sibling_wait_" + group)
            parts = _chip_partial(glist, lands, partial_idx, "chip_partial_" + group)
            own_pieces[group] = (glist, lands)
            if group != "in":
                self.waiting = (group, keys, parts)
                return tuple(parts)
            self.waiting = None
            (started,), token = _exchange_start([(parts, 4, _to_chips)], "rs_chips_start_" + group)
            in_flight[group] = (keys, *started)
            return (token,)

    dx, _, small = _local_step(x[0], loss_target[0], g_mix, g_ffn, g_final[None], attn_sinks, Gathered(),
                               reducer=Reducer(), after=gather_tokens)
    after = finish("mix", (dx,))

    sinks_row = jnp.pad(small["sinks"], ((0, 0), (0, D - 128)))
    pack = jnp.concatenate([small["g_mix"], small["g_ffn"], small["g_final"], small["conv_w"], sinks_row, small["lossvec"]], axis=0)
    tot, loss_row = _small_all_reduce(pack, "small_all_reduce", after=after)
    loss = loss_row[0, 0]
    g_small = {
        "g_mix": tot[0:1], "g_ffn": tot[1:2], "g_final": tot[2:3],
        "conv_w": lax.dynamic_slice(tot, (3, me * 128), (3, 128)), "attn_sinks": tot[6:7, :N_HEADS],
    }
    finish("in", update_small(g_small))

    order = ["g_mix", "w_in", "conv_w", "attn_sinks", "w_conv_out", "w_attn_out", "w_o", "g_ffn", "w_gate_up", "w_down", "g_final"]
    return (loss, dx[None], *[results[k][i] for i in range(4) for k in order])
```

```python
import functools
import math

import jax
import jax.numpy as jnp
from jax import lax
from jax.experimental import pallas as pl
from jax.experimental.pallas import tpu as pltpu

F32 = jnp.float32
BF16 = jnp.bfloat16

D = 1024
HEAD_DIM = 64
N_HEADS = 16
N_KV = 4
GROUP = N_HEADS // N_KV
D_KV = N_KV * HEAD_DIM
BLOCK = 128
ROT_DIM = HEAD_DIM // 4
ROPE_THETA = 500000.0
ATTN_SCALE = 1.0 / math.sqrt(HEAD_DIM)
NEG_INF = -1e30
D_FF = 2816
N_IN = 6656
EPS = 1e-5
C_CB, C_CC, C_CX, C_Q, C_K, C_V, C_GC, C_GA = 0, 1024, 2048, 3072, 4096, 4352, 4608, 5632

LR, B1, B2, EPS_ADAM, WD, STEP = 0.001, 0.9, 0.999, 1e-08, 0.01, 10

N_DEV = 8
MESH = pl.DeviceIdType.MESH
VMEM_LIMIT = 56 * 1024 * 1024

NN = (((1,), (0,)), ((), ()))
NT = (((1,), (1,)), ((), ()))
TN = (((0,), (0,)), ((), ()))
HBM_SPEC = pl.BlockSpec(memory_space=pl.ANY)
ROW_SPLIT = 4


def _call(body, **kw):
    return pl.pallas_call(body, **kw)


def _params(*sem):
    return pltpu.CompilerParams(dimension_semantics=sem, vmem_limit_bytes=VMEM_LIMIT)


def _sds(shape, dtype):
    return jax.ShapeDtypeStruct(shape, dtype)


def _matmul(a, b, *, mode, tm, tn, tk, out_dtype, name, res=None, after=()):
    parts = list(a) if isinstance(a, (list, tuple)) else [a]
    rows_a = parts[0].shape[0]
    cols_a = sum(p.shape[1] for p in parts)
    if mode == "nn":
        (m, kk), (_, n), dims = (rows_a, cols_a), b.shape, NN
    elif mode == "nt":
        (m, kk), (n, _), dims = (rows_a, cols_a), b.shape, NT
    else:
        (kk, m), (_, n), dims = (rows_a, cols_a), b.shape, TN
    tm, tn, tk = min(tm, m), min(tn, n), min(tk, kk)
    assert m % tm == 0 and n % tn == 0 and kk % tk == 0, (name, m, n, kk, tm, tn, tk)
    nk = kk // tk
    split_axis, width = (2, tk) if mode == "nn" else (0, tm)
    assert len(parts) == 1 or mode in ("nn", "tn")
    assert len(parts) == 1 or all(p.shape[1] % width == 0 for p in parts), (name, width)
    counts = [p.shape[1] // width for p in parts]
    starts = [sum(counts[:p]) for p in range(len(parts))]

    def a_spec(p):
        def col(t):
            return jnp.clip(t - starts[p], 0, counts[p] - 1) if len(parts) > 1 else t

        if mode == "tn":
            return pl.BlockSpec((tk, tm), lambda i, j, k: (k, col(i)))
        return pl.BlockSpec((tm, tk), lambda i, j, k: (i, col(k)))

    if mode == "nt":
        b_spec = pl.BlockSpec((tn, tk), lambda i, j, k: (j, k))
    else:
        b_spec = pl.BlockSpec((tk, tn), lambda i, j, k: (k, j))
    o_spec = pl.BlockSpec((tm, tn), lambda i, j, k: (i, j))
    has_res = res is not None
    n_parts = len(parts)
    unit = 128 if mode == "tn" else 16
    split = ROW_SPLIT if tm % (ROW_SPLIT * unit) == 0 else 1

    def body(*refs):
        a_refs, b_ref = refs[:n_parts], refs[n_parts]
        r_ref = refs[n_parts + 1] if has_res else None
        o_ref = refs[n_parts + 1 + has_res + len(after)]
        k = pl.program_id(2)

        acc_ref = refs[-1] if nk > 1 else None

        def step(a_ref):
            def matmul(rows):
                a_blk = a_ref[:, rows] if mode == "tn" else a_ref[rows, :]
                return lax.dot_general(a_blk, b_ref[...], dims, preferred_element_type=F32)

            def finish(rows, part):
                if nk > 1:
                    acc_ref[rows, :] += part
                else:
                    o_ref[rows, :] = (part + r_ref[rows, :] if has_res else part).astype(o_ref.dtype)

            _row_pipeline(tm, matmul, finish, split)

        if nk > 1:
            @pl.when(k == 0)
            def _():
                acc_ref[...] = jnp.zeros_like(acc_ref)

        if n_parts == 1:
            step(a_refs[0])
        else:
            t = pl.program_id(split_axis)
            for p in range(n_parts):
                pl.when((t >= starts[p]) & (t < starts[p] + counts[p]))(functools.partial(step, a_refs[p]))

        if nk > 1:
            @pl.when(k == nk - 1)
            def _():
                o_ref[...] = (acc_ref[...] + r_ref[...] if has_res else acc_ref[...]).astype(o_ref.dtype)

    ins = parts + [b] + ([res] if has_res else []) + list(after)
    in_specs = [a_spec(p) for p in range(n_parts)] + [b_spec] + ([o_spec] if has_res else []) + [HBM_SPEC] * len(after)
    scratch = [] if nk == 1 else [pltpu.VMEM((tm, tn), F32)]
    return _call(
        body, name=name, grid=(m // tm, n // tn, nk), in_specs=in_specs, out_specs=o_spec,
        out_shape=_sds((m, n), out_dtype), scratch_shapes=scratch,
        compiler_params=_params("parallel", "parallel", "arbitrary"),
    )(*ins)


def _matmul_group(a_group, b_group, *, mode, tm, tn, out_dtype, name, after=()):
    a0, b0 = a_group[0], b_group[0]
    a_pair, b_pair, count = a_group, b_group, len(a_group)
    if mode == "nn":
        (m, kk), (_, n), dims = a0.shape, b0.shape, NN
    elif mode == "nt":
        (m, kk), (n, _), dims = a0.shape, b0.shape, NT
    else:
        (kk, m), (_, n), dims = a0.shape, b0.shape, TN
    assert all(a.shape == a0.shape for a in a_pair) and all(b.shape == b0.shape for b in b_pair)
    tm, tn = min(tm, m), min(tn, n)
    assert m % tm == 0 and n % tn == 0, (name, m, n, tm, tn)
    a_spec = pl.BlockSpec((kk, tm), lambda i, j: (0, i)) if mode == "tn" else pl.BlockSpec((tm, kk), lambda i, j: (i, 0))
    b_spec = pl.BlockSpec((tn, kk), lambda i, j: (j, 0)) if mode == "nt" else pl.BlockSpec((kk, tn), lambda i, j: (0, j))
    o_spec = pl.BlockSpec((tm, tn), lambda i, j: (i, j))
    unit = 128 if mode == "tn" else 16
    split = ROW_SPLIT if tm % (ROW_SPLIT * unit) == 0 else 1

    def body(*refs):
        a_refs, b_refs, o_refs = refs[:count], refs[count:2 * count], refs[2 * count + len(after):]

        def matmul(rows):
            return tuple(lax.dot_general(a_ref[:, rows] if mode == "tn" else a_ref[rows, :], b_ref[...], dims,
                                         preferred_element_type=F32) for a_ref, b_ref in zip(a_refs, b_refs))

        def finish(rows, parts):
            for o_ref, part in zip(o_refs, parts):
                o_ref[rows, :] = part.astype(out_dtype)

        _row_pipeline(tm, matmul, finish, split)

    return _call(
        body, name=name, grid=(m // tm, n // tn), in_specs=[a_spec] * count + [b_spec] * count + [HBM_SPEC] * len(after),
        out_specs=[o_spec] * count, out_shape=[_sds((m, n), out_dtype)] * count,
        compiler_params=_params("parallel", "parallel"),
    )(*a_pair, *b_pair, *after)


def _row_tile(s):
    return min(512, s)


def _rms_fwd(x, g, name, after=()):
    s = x.shape[0]
    tm = _row_tile(s)

    def body(x_ref, g_ref, *rest):
        h_ref = rest[-1]
        xv = x_ref[...]
        r = lax.rsqrt(jnp.mean(xv * xv, axis=-1, keepdims=True) + EPS)
        h_ref[...] = (xv * r * g_ref[...]).astype(BF16)

    row = pl.BlockSpec((tm, D), lambda i: (i, 0))
    return _call(
        body, name=name, grid=(s // tm,), in_specs=[row, pl.BlockSpec((1, D), lambda i: (0, 0))] + [HBM_SPEC] * len(after),
        out_specs=row, out_shape=_sds((s, D), BF16), compiler_params=_params("parallel"),
    )(x, g, *after)


def _rms_bwd(dh, x, g, dres, name, after=()):
    s = x.shape[0]
    tm = _row_tile(s)

    def body(dh_ref, x_ref, g_ref, dres_ref, *rest):
        dx_ref, dxb_ref, dg_ref = rest[len(after):]
        xv = x_ref[...]
        r = lax.rsqrt(jnp.mean(xv * xv, axis=-1, keepdims=True) + EPS)
        xh = xv * r
        dhv = dh_ref[...].astype(F32)
        dyg = dhv * g_ref[...]
        dx = dres_ref[...] + r * (dyg - xh * jnp.mean(dyg * xh, axis=-1, keepdims=True))
        dx_ref[...] = dx
        dxb_ref[...] = dx.astype(BF16)
        part = jnp.sum(dhv * xh, axis=0, keepdims=True)

        @pl.when(pl.program_id(0) == 0)
        def _():
            dg_ref[...] = part

        @pl.when(pl.program_id(0) > 0)
        def _():
            dg_ref[...] += part

    row = pl.BlockSpec((tm, D), lambda i: (i, 0))
    vec = pl.BlockSpec((1, D), lambda i: (0, 0))
    return _call(
        body, name=name, grid=(s // tm,), in_specs=[row, row, vec, row] + [HBM_SPEC] * len(after), out_specs=[row, row, vec],
        out_shape=[_sds((s, D), F32), _sds((s, D), BF16), _sds((1, D), F32)],
        compiler_params=_params("arbitrary"),
    )(dh, x, g, dres, *after)


def _loss_head(x2, g, tgt, name):
    s = x2.shape[0]
    tm = _row_tile(s)

    def body(x_ref, g_ref, t_ref, dx_ref, dxb_ref, dg_ref, l_ref):
        xv = x_ref[...]
        gv = g_ref[...]
        r = lax.rsqrt(jnp.mean(xv * xv, axis=-1, keepdims=True) + EPS)
        xh = xv * r
        err = xh * gv - t_ref[...]
        dy = err * (1.0 / D)
        dyg = dy * gv
        dx = r * (dyg - xh * jnp.mean(dyg * xh, axis=-1, keepdims=True))
        dx_ref[...] = dx
        dxb_ref[...] = dx.astype(BF16)
        dg_part = jnp.sum(dy * xh, axis=0, keepdims=True)
        l_part = jnp.sum(err * err, axis=0, keepdims=True)

        @pl.when(pl.program_id(0) == 0)
        def _():
            dg_ref[...] = dg_part
            l_ref[...] = l_part

        @pl.when(pl.program_id(0) > 0)
        def _():
            dg_ref[...] += dg_part
            l_ref[...] += l_part

    row = pl.BlockSpec((tm, D), lambda i: (i, 0))
    vec = pl.BlockSpec((1, D), lambda i: (0, 0))
    return _call(
        body, name=name, grid=(s // tm,), in_specs=[row, vec, row], out_specs=[row, row, vec, vec],
        out_shape=[_sds((s, D), F32), _sds((s, D), BF16), _sds((1, D), F32), _sds((1, D), F32)],
        compiler_params=_params("arbitrary"),
    )(x2, g, tgt)


CONV_TC = 256


def _shift_down(u, k, rows):
    return jnp.where(rows >= k, pltpu.roll(u, k, 0), 0.0)


def _shift_up(u, k, rows, s):
    return jnp.where(rows < s - k, pltpu.roll(u, s - k, 0), 0.0)


def _conv_specs(s):
    nb = D // CONV_TC

    def col(c0):
        return pl.BlockSpec((s, CONV_TC), lambda j, c0=c0: (0, c0 // CONV_TC + j))

    return nb, col


def _conv_fwd(proj, conv_w, name):
    s = proj.shape[0]
    nb, col = _conv_specs(s)

    def body(cb_ref, cc_ref, cx_ref, w_ref, y_ref):
        rows = lax.broadcasted_iota(jnp.int32, (s, CONV_TC), 0)
        u = cc_ref[...].astype(F32) * cx_ref[...].astype(F32)
        w = w_ref[...]
        c = w[0:1] * _shift_down(u, 2, rows) + w[1:2] * _shift_down(u, 1, rows) + w[2:3] * u
        y_ref[...] = (cb_ref[...].astype(F32) * c).astype(BF16)

    return _call(
        body, name=name, grid=(nb,),
        in_specs=[col(C_CB), col(C_CC), col(C_CX), pl.BlockSpec((3, CONV_TC), lambda j: (0, j))],
        out_specs=pl.BlockSpec((s, CONV_TC), lambda j: (0, j)), out_shape=_sds((s, D), BF16),
        compiler_params=_params("parallel"),
    )(proj, proj, proj, conv_w)


def _write_behind(t, nt, buf, sems, tiles, window, where):
    slot = t % 2

    def copies(sl, at):
        return [pltpu.make_async_copy(buf.at[sl, p], window(p, at), sems.at[sl, p]) for p in range(len(tiles))]

    @pl.when(t >= 2)
    def _():
        for cp in copies(slot, where):
            cp.wait()

    for p, tile in enumerate(tiles):
        buf[slot, p] = tile
    started = copies(slot, where)
    for cp in started:
        cp.start()

    @pl.when(t == nt - 1)
    def _():
        for cp in started:
            cp.wait()
        if nt > 1:
            for cp in copies(1 - slot, where):
                cp.wait()


def _conv_bwd(dy, proj, conv_w, dproj, name, after=()):
    s = proj.shape[0]
    nb, col = _conv_specs(s)

    def body(dy_ref, cb_ref, cc_ref, cx_ref, w_ref, *rest):
        dproj_ref, dw_ref, buf, sems = rest[1 + len(after):]
        j = pl.program_id(0)
        rows = lax.broadcasted_iota(jnp.int32, (s, CONV_TC), 0)
        cc = cc_ref[...].astype(F32)
        cx = cx_ref[...].astype(F32)
        u = cc * cx
        u1 = _shift_down(u, 1, rows)
        u2 = _shift_down(u, 2, rows)
        w = w_ref[...]
        c = w[0:1] * u2 + w[1:2] * u1 + w[2:3] * u
        dyv = dy_ref[...].astype(F32)
        dc = dyv * cb_ref[...].astype(F32)
        du = w[2:3] * dc + w[1:2] * _shift_up(dc, 1, rows, s) + w[0:1] * _shift_up(dc, 2, rows, s)

        def window(p, jj):
            start = pl.multiple_of((C_CB, C_CC, C_CX)[p] + jj * CONV_TC, CONV_TC)
            return dproj_ref.at[:, pl.ds(start, CONV_TC)]

        tiles = ((dyv * c).astype(BF16), (du * cx).astype(BF16), (du * cc).astype(BF16))
        _write_behind(j * 0, 1, buf, sems, tiles, window, j)
        dw_ref[...] = jnp.concatenate(
            [jnp.sum(dc * u2, axis=0, keepdims=True), jnp.sum(dc * u1, axis=0, keepdims=True),
             jnp.sum(dc * u, axis=0, keepdims=True)], axis=0)

    return _call(
        body, name=name, grid=(nb,),
        in_specs=[pl.BlockSpec((s, CONV_TC), lambda j: (0, j)), col(C_CB), col(C_CC), col(C_CX),
                  pl.BlockSpec((3, CONV_TC), lambda j: (0, j))] + [HBM_SPEC] * (1 + len(after)),
        out_specs=[pl.BlockSpec(memory_space=pl.ANY), pl.BlockSpec((3, CONV_TC), lambda j: (0, j))],
        out_shape=[_sds((s, N_IN), BF16), _sds((3, D), F32)],
        scratch_shapes=[pltpu.VMEM((1, 3, s, CONV_TC), BF16), pltpu.SemaphoreType.DMA((1, 3))],
        input_output_aliases={5: 0}, compiler_params=_params("arbitrary"),
    )(dy, proj, proj, proj, conv_w, dproj, *after)


def _rope_tables(s):
    half = ROT_DIM // 2
    inv_freq = ROPE_THETA ** (-jnp.arange(0, ROT_DIM, 2, dtype=F32) / ROT_DIM)
    inv64 = jnp.concatenate([inv_freq, inv_freq, jnp.zeros((HEAD_DIM - ROT_DIM,), F32)])
    ang = jnp.arange(s, dtype=F32)[:, None] * jnp.concatenate([inv64, inv64])[None, :]
    d = lax.broadcasted_iota(jnp.int32, (s, 128), 1) % HEAD_DIM
    cos, sin = jnp.cos(ang), jnp.sin(ang)
    c = jnp.where(d < ROT_DIM, cos, 1.0)
    a = jnp.where(d < half, -sin, 0.0)
    b = jnp.where((d >= half) & (d < ROT_DIM), sin, 0.0)
    return jnp.concatenate([c, a, b], axis=1)


def _rope(x, tab):
    c, a, b = tab[:, 0:128], tab[:, 128:256], tab[:, 256:384]
    outs = []
    for i in range(x.shape[1] // 128):
        xc = x[:, i * 128:(i + 1) * 128]
        outs.append(xc * c + pltpu.roll(xc, 120, 1) * a + pltpu.roll(xc, 8, 1) * b)
    return outs[0] if len(outs) == 1 else jnp.concatenate(outs, axis=1)


def _rope_t(dx, tab):
    c, a, b = tab[:, 0:128], tab[:, 128:256], tab[:, 256:384]
    outs = []
    for i in range(dx.shape[1] // 128):
        dc = dx[:, i * 128:(i + 1) * 128]
        outs.append(dc * c + pltpu.roll(dc * a, 8, 1) + pltpu.roll(dc * b, 120, 1))
    return outs[0] if len(outs) == 1 else jnp.concatenate(outs, axis=1)


def _attn_in_specs():
    prev = lambda n: jnp.maximum(n - 1, 0)
    return [
        pl.BlockSpec((BLOCK, D), lambda n: (n, C_Q // D)),
        pl.BlockSpec((BLOCK, D_KV), lambda n: (n, C_K // D_KV)),
        pl.BlockSpec((BLOCK, D_KV), lambda n: (prev(n), C_K // D_KV)),
        pl.BlockSpec((BLOCK, D_KV), lambda n: (n, C_V // D_KV)),
        pl.BlockSpec((BLOCK, D_KV), lambda n: (prev(n), C_V // D_KV)),
        pl.BlockSpec((BLOCK, 384), lambda n: (n, 0)),
        pl.BlockSpec((BLOCK, 384), lambda n: (prev(n), 0)),
        pl.BlockSpec(memory_space=pltpu.SMEM),
    ]


HALF = HEAD_DIM
N_CHUNK = D // 128


def _swa_bias(n):
    qi = lax.broadcasted_iota(jnp.int32, (BLOCK, 2 * BLOCK), 0)
    kj = lax.broadcasted_iota(jnp.int32, (BLOCK, 2 * BLOCK), 1)
    rel = qi + BLOCK - kj
    valid = (rel >= 0) & (rel < BLOCK) & ((kj >= BLOCK) | (n > 0))
    return jnp.where(valid, 0.0, NEG_INF)


def _halves(x):
    lo = lax.broadcasted_iota(jnp.int32, x.shape, 1) < HALF
    return jnp.where(lo, x, 0.0).astype(BF16), jnp.where(lo, 0.0, x).astype(BF16)


def _dup_heads(x):
    out = []
    for pair in range(N_KV // 2):
        xc = x[:, pair * 128:(pair + 1) * 128]
        xr = pltpu.roll(xc, HALF, 1)
        lo = lax.broadcasted_iota(jnp.int32, xc.shape, 1) < HALF
        out += [jnp.where(lo, xc, xr), jnp.where(lo, xr, xc)]
    return out


def _swa_load(q_ref, kc_ref, kp_ref, vc_ref, vp_ref, tc_ref, tp_ref):
    qf = _rope(q_ref[...].astype(F32), tc_ref[...]) * ATTN_SCALE
    q_halves = [_halves(qf[:, c * 128:(c + 1) * 128]) for c in range(N_CHUNK)]
    kf = jnp.concatenate([_rope(kp_ref[...].astype(F32), tp_ref[...]), _rope(kc_ref[...].astype(F32), tc_ref[...])], axis=0)
    vf = jnp.concatenate([vp_ref[...], vc_ref[...]], axis=0).astype(F32)
    return q_halves, _dup_heads(kf), _dup_heads(vf)


def _swa_probs(qh, kk, bias, sink):
    s = lax.dot_general(qh, kk, NT, preferred_element_type=F32) + bias
    m = jnp.maximum(jnp.max(jnp.maximum(s[:, :BLOCK], s[:, BLOCK:]), axis=1, keepdims=True), sink)
    return jnp.exp(s - m), m


def _swa_fwd(proj, tab, sinks, name, after=()):
    s = proj.shape[0]

    def body(q_ref, kc_ref, kp_ref, vc_ref, vp_ref, tc_ref, tp_ref, sink_ref, *rest):
        o_ref = rest[-1]
        n = pl.program_id(0)
        q_halves, kdup, vdup = _swa_load(q_ref, kc_ref, kp_ref, vc_ref, vp_ref, tc_ref, tp_ref)
        bias = _swa_bias(n)
        ones = jnp.ones((2 * BLOCK, 128), BF16)
        kk = [k.astype(BF16) for k in kdup]
        vv = [[jnp.concatenate([v_half, ones], axis=1) for v_half in _halves(v)] for v in vdup]
        heads = [(c, half) for c in range(N_CHUNK) for half in range(2)]
        scores = [lax.dot_general(q_halves[c][half], kk[c // (GROUP // 2)], NT, preferred_element_type=F32)
                  for c, half in heads]
        probs = []
        for (c, half), sc in zip(heads, scores):
            sc = sc + bias
            m = jnp.maximum(jnp.max(jnp.maximum(sc[:, :BLOCK], sc[:, BLOCK:]), axis=1, keepdims=True), sink_ref[0, 2 * c + half])
            probs.append((jnp.exp(sc - m).astype(BF16), jnp.exp(sink_ref[0, 2 * c + half] - m)))
        outs = [lax.dot_general(e, vv[c // (GROUP // 2)][half], NN, preferred_element_type=F32)
                for (c, half), (e, _) in zip(heads, probs)]
        for c in range(N_CHUNK):
            parts = [outs[2 * c + half][:, :128] * (1.0 / (outs[2 * c + half][:, 128:] + probs[2 * c + half][1]))
                     for half in range(2)]
            o_ref[:, c * 128:(c + 1) * 128] = (parts[0] + parts[1]).astype(BF16)

    return _call(
        body, name=name, grid=(s // BLOCK,), in_specs=_attn_in_specs() + [HBM_SPEC] * len(after),
        out_specs=pl.BlockSpec((BLOCK, D), lambda n: (n, 0)), out_shape=_sds((s, D), BF16),
        compiler_params=_params("parallel"),
    )(proj, proj, proj, proj, proj, tab, tab, sinks, *after)


def _swa_bwd(do, proj, tab, sinks, dproj, name, after=()):
    s = proj.shape[0]
    nblk = s // BLOCK
    kv_of = lambda c: c // (GROUP // 2)

    def body(do_ref, q_ref, kc_ref, kp_ref, vc_ref, vp_ref, tc_ref, tp_ref, sink_ref, *rest):
        dproj_ref, dk_ref, dv_ref, ds_ref, dqout, dkbuf, dvbuf, sems = rest[1 + len(after):]
        n = pl.program_id(0)

        @pl.when(n == 0)
        def _():
            dk_ref[...] = jnp.zeros_like(dk_ref)
            dv_ref[...] = jnp.zeros_like(dv_ref)
            ds_ref[...] = jnp.zeros_like(ds_ref)

        q_halves, kdup, vdup = _swa_load(q_ref, kc_ref, kp_ref, vc_ref, vp_ref, tc_ref, tp_ref)
        dof = do_ref[...].astype(F32)
        do_halves = [_halves(dof[:, c * 128:(c + 1) * 128]) for c in range(N_CHUNK)]
        bias = _swa_bias(n)
        ones = jnp.ones((2 * BLOCK, 128), BF16)
        kk = [k.astype(BF16) for k in kdup]
        vv = [v.astype(BF16) for v in vdup]
        k_halves = [_halves(k) for k in kdup]
        heads = [(c, half) for c in range(N_CHUNK) for half in range(2)]
        lane_row = lax.broadcasted_iota(jnp.int32, (1, 128), 1)
        lo_kv = lax.broadcasted_iota(jnp.int32, (2 * BLOCK, 128), 1) < HALF
        scores = [lax.dot_general(q_halves[c][half], kk[kv_of(c)], NT, preferred_element_type=F32) for c, half in heads]
        dps = [lax.dot_general(do_halves[c][half], vv[kv_of(c)], NT, preferred_element_type=F32) for c, half in heads]
        exps = []
        for (c, half), sc in zip(heads, scores):
            sink = sink_ref[0, 2 * c + half]
            sc = sc + bias
            m = jnp.maximum(jnp.max(jnp.maximum(sc[:, :BLOCK], sc[:, BLOCK:]), axis=1, keepdims=True), sink)
            exps.append((jnp.exp(sc - m), jnp.exp(sink - m)))
        sums = [lax.dot_general(e.astype(BF16), ones, NN, preferred_element_type=F32) for e, _ in exps]
        dsink_row = jnp.zeros((1, 128), F32)
        dsb, pb = [], []
        for h, ((e, es), row_sum, dp) in enumerate(zip(exps, sums, dps)):
            inv = 1.0 / (row_sum + es)
            p = e * jnp.concatenate([inv, inv], axis=1)
            t = p * dp
            delta = jnp.sum(t, axis=1, keepdims=True)
            dsb.append((t - p * delta).astype(BF16))
            pb.append(p.astype(BF16))
            dsink = -jnp.sum(es * inv * delta, axis=0, keepdims=True)
            dsink_row = dsink_row + jnp.where(lane_row == h, dsink, 0.0)
        dq_parts = [lax.dot_general(d, k_halves[kv_of(c)][half], NN, preferred_element_type=F32) for (c, half), d in zip(heads, dsb)]
        dk_parts = [lax.dot_general(d, q_halves[c][half], TN, preferred_element_type=F32) for (c, half), d in zip(heads, dsb)]
        dv_parts = [lax.dot_general(p, do_halves[c][half], TN, preferred_element_type=F32) for (c, half), p in zip(heads, pb)]
        dq = jnp.concatenate([(dq_parts[2 * c] + dq_parts[2 * c + 1]) * ATTN_SCALE for c in range(N_CHUNK)], axis=1)

        def kv_sum(parts, hk):
            acc = (parts[GROUP * hk] + parts[GROUP * hk + 1]) + (parts[GROUP * hk + 2] + parts[GROUP * hk + 3])
            return acc + pltpu.roll(acc, HALF, 1)

        for pair in range(N_KV // 2):
            dkbuf[:, pair * 128:(pair + 1) * 128] = jnp.where(lo_kv, kv_sum(dk_parts, 2 * pair), kv_sum(dk_parts, 2 * pair + 1))
            dvbuf[:, pair * 128:(pair + 1) * 128] = jnp.where(lo_kv, kv_sum(dv_parts, 2 * pair), kv_sum(dv_parts, 2 * pair + 1))
        prev0 = pl.multiple_of(jnp.maximum(n - 1, 0) * BLOCK, BLOCK)
        cur0 = pl.multiple_of(n * BLOCK, BLOCK)

        @pl.when(n > 0)
        def _():
            dk_ref[pl.ds(prev0, BLOCK), :] += dkbuf[0:BLOCK, :]
            dv_ref[pl.ds(prev0, BLOCK), :] += dvbuf[0:BLOCK, :]

        dk_ref[pl.ds(cur0, BLOCK), :] += dkbuf[BLOCK:2 * BLOCK, :]
        dv_ref[pl.ds(cur0, BLOCK), :] += dvbuf[BLOCK:2 * BLOCK, :]
        ds_ref[...] += dsink_row

        def window(p, at):
            return dproj_ref.at[pl.ds(pl.multiple_of(at * BLOCK, BLOCK), BLOCK), pl.ds(C_Q, D)]

        _write_behind(n, nblk, dqout, sems, (_rope_t(dq, tc_ref[...]).astype(BF16),), window, n)

    blk = lambda w: pl.BlockSpec((BLOCK, w), lambda n: (n, 0))
    whole = lambda w: pl.BlockSpec((s, w), lambda n: (0, 0))
    n_in = 1 + len(_attn_in_specs())
    return _call(
        body, name=name, grid=(nblk,), in_specs=[blk(D)] + _attn_in_specs() + [HBM_SPEC] * (1 + len(after)),
        out_specs=[HBM_SPEC, whole(D_KV), whole(D_KV), pl.BlockSpec((1, 128), lambda n: (0, 0))],
        out_shape=[_sds((s, N_IN), BF16), _sds((s, D_KV), F32), _sds((s, D_KV), F32), _sds((1, 128), F32)],
        scratch_shapes=[pltpu.VMEM((2, 1, BLOCK, D), BF16), pltpu.VMEM((2 * BLOCK, D_KV), F32),
                        pltpu.VMEM((2 * BLOCK, D_KV), F32), pltpu.SemaphoreType.DMA((2, 1))],
        input_output_aliases={n_in: 0}, compiler_params=_params("arbitrary"),
    )(do, proj, proj, proj, proj, proj, tab, tab, sinks, dproj, *after)


def _kv_bwd(dkr, dv, tab, dproj, name):
    s = dkr.shape[0]
    tm = _row_tile(s)

    def body(dk_ref, dv_ref, t_ref, dproj_in, o_ref):
        del dproj_in
        o_ref[:, 0:D_KV] = _rope_t(dk_ref[...], t_ref[...]).astype(BF16)
        o_ref[:, D_KV:2 * D_KV] = dv_ref[...].astype(BF16)

    row = lambda w: pl.BlockSpec((tm, w), lambda i: (i, 0))
    return _call(
        body, name=name, grid=(s // tm,),
        in_specs=[row(D_KV), row(D_KV), row(384), pl.BlockSpec(memory_space=pl.ANY)],
        out_specs=pl.BlockSpec((tm, 2 * D_KV), lambda i: (i, C_K // (2 * D_KV))),
        out_shape=_sds((s, N_IN), BF16), input_output_aliases={3: 0}, compiler_params=_params("parallel"),
    )(dkr, dv, tab, dproj)


EW_TC = 512


def _sigmoid(x):
    return 0.5 * jnp.tanh(0.5 * x) + 0.5


def _branches_merge_fwd(conv_y, attn, wco, wao, proj, name):
    s = proj.shape[0]
    tm = min(2048, s)

    def body(y_ref, a_ref, wc_ref, wa_ref, gc_ref, ga_ref, co_ref, ao_ref, m_ref):
        def matmuls(rows):
            return (lax.dot_general(y_ref[rows, :], wc_ref[...], NN, preferred_element_type=F32),
                    lax.dot_general(a_ref[rows, :], wa_ref[...], NN, preferred_element_type=F32))

        def finish(rows, parts):
            co, ao = parts
            co_ref[rows, :] = co.astype(BF16)
            ao_ref[rows, :] = ao.astype(BF16)
            m_ref[rows, :] = (_sigmoid(gc_ref[rows, :].astype(F32)) * co + _sigmoid(ga_ref[rows, :].astype(F32)) * ao).astype(BF16)

        _row_pipeline(tm, matmuls, finish)

    act = pl.BlockSpec((tm, D), lambda i, j: (i, 0))
    wgt = pl.BlockSpec((D, EW_TC), lambda i, j: (0, j))
    tile = pl.BlockSpec((tm, EW_TC), lambda i, j: (i, j))
    return _call(
        body, name=name, grid=(s // tm, D // EW_TC),
        in_specs=[act, act, wgt, wgt, pl.BlockSpec((tm, EW_TC), lambda i, j: (i, C_GC // EW_TC + j)),
                  pl.BlockSpec((tm, EW_TC), lambda i, j: (i, C_GA // EW_TC + j))],
        out_specs=[tile, tile, tile], out_shape=[_sds((s, D), BF16)] * 3, compiler_params=_params("parallel", "parallel"),
    )(conv_y, attn, wco, wao, proj, proj)


def _wo_merge_bwd(dx1b, wo, proj, conv_out, attn_out, name, after=()):
    s = proj.shape[0]
    tm = min(1024, s)
    nj = D // EW_TC

    def body(dx_ref, w_ref, gc_ref, ga_ref, co_ref, ao_ref, *rest):
        dproj_ref, dco_ref, dao_ref, buf, sems = rest[len(after):]
        i, j = pl.program_id(0), pl.program_id(1)
        gate_c, gate_a = [], []

        def matmul(rows):
            return lax.dot_general(dx_ref[rows, :], w_ref[...], NT, preferred_element_type=F32)

        def finish(rows, dm):
            sc = _sigmoid(gc_ref[rows, :].astype(F32))
            sa = _sigmoid(ga_ref[rows, :].astype(F32))
            dco_ref[rows, :] = (dm * sc).astype(BF16)
            dao_ref[rows, :] = (dm * sa).astype(BF16)
            gate_c.append((dm * co_ref[rows, :].astype(F32) * sc * (1.0 - sc)).astype(BF16))
            gate_a.append((dm * ao_ref[rows, :].astype(F32) * sa * (1.0 - sa)).astype(BF16))

        _row_pipeline(tm, matmul, finish)

        def window(p, at):
            start = pl.multiple_of((C_GC, C_GA)[p] + at[1] * EW_TC, EW_TC)
            return dproj_ref.at[pl.ds(pl.multiple_of(at[0] * tm, tm), tm), pl.ds(start, EW_TC)]

        tiles = (jnp.concatenate(gate_c, axis=0), jnp.concatenate(gate_a, axis=0))
        _write_behind(i * nj + j, (s // tm) * nj, buf, sems, tiles, window, (i, j))

    tile = pl.BlockSpec((tm, EW_TC), lambda i, j: (i, j))
    return _call(
        body, name=name, grid=(s // tm, nj),
        in_specs=[pl.BlockSpec((tm, D), lambda i, j: (i, 0)), pl.BlockSpec((EW_TC, D), lambda i, j: (j, 0)),
                  pl.BlockSpec((tm, EW_TC), lambda i, j: (i, C_GC // EW_TC + j)),
                  pl.BlockSpec((tm, EW_TC), lambda i, j: (i, C_GA // EW_TC + j)), tile, tile] + [HBM_SPEC] * len(after),
        out_specs=[HBM_SPEC, tile, tile],
        out_shape=[_sds((s, N_IN), BF16), _sds((s, D), BF16), _sds((s, D), BF16)],
        scratch_shapes=[pltpu.VMEM((2, 2, tm, EW_TC), BF16), pltpu.SemaphoreType.DMA((2, 2))],
        compiler_params=_params("arbitrary", "arbitrary"),
    )(dx1b, wo, proj, proj, conv_out, attn_out, *after)


FF_TC = 256
FF_TM = 2048


def _row_pipeline(tm, matmul, finish, split=ROW_SPLIT):
    step = tm // split
    pending = None
    for r in range(split):
        rows = pl.ds(r * step, step)
        result = matmul(rows)
        if pending is not None:
            finish(*pending)
        pending = (rows, result)
    finish(*pending)


def _gate_up_fwd(h2, wgu_t, name):
    s = h2.shape[0]
    tm = min(FF_TM, s)
    nb = D_FF // FF_TC

    def body(h_ref, wg_ref, wu_ref, a_ref, g_ref, u_ref):
        def matmuls(rows):
            h = h_ref[rows, :]
            return (lax.dot_general(h, wg_ref[...], NT, preferred_element_type=F32),
                    lax.dot_general(h, wu_ref[...], NT, preferred_element_type=F32))

        def finish(rows, gu):
            g, u = gu
            a_ref[rows, :] = (g * _sigmoid(g) * u).astype(BF16)
            g_ref[rows, :] = g.astype(BF16)
            u_ref[rows, :] = u.astype(BF16)

        _row_pipeline(tm, matmuls, finish)

    tile = pl.BlockSpec((tm, FF_TC), lambda j, i: (i, j))
    return _call(
        body, name=name, grid=(nb, s // tm),
        in_specs=[pl.BlockSpec((tm, D), lambda j, i: (i, 0)), pl.BlockSpec((FF_TC, D), lambda j, i: (j, 0)),
                  pl.BlockSpec((FF_TC, D), lambda j, i: (nb + j, 0))],
        out_specs=[tile, tile, tile], out_shape=[_sds((s, D_FF), BF16)] * 3,
        compiler_params=_params("parallel", "parallel"),
    )(h2, wgu_t, wgu_t)


def _down_bwd_x(dx2b, wd, gate, up, name):
    s = dx2b.shape[0]
    tm = min(FF_TM, s)
    nb = D_FF // FF_TC

    def body(dx_ref, w_ref, g_ref, u_ref, dg_ref, du_ref):
        def matmul(rows):
            return lax.dot_general(dx_ref[rows, :], w_ref[...], NT, preferred_element_type=F32)

        def finish(rows, da):
            g = g_ref[rows, :].astype(F32)
            sg = _sigmoid(g)
            dg_ref[rows, :] = (da * u_ref[rows, :].astype(F32) * (sg * (1.0 + g * (1.0 - sg)))).astype(BF16)
            du_ref[rows, :] = (da * (g * sg)).astype(BF16)

        _row_pipeline(tm, matmul, finish)

    tile = pl.BlockSpec((tm, FF_TC), lambda j, i: (i, j))
    return _call(
        body, name=name, grid=(nb, s // tm),
        in_specs=[pl.BlockSpec((tm, D), lambda j, i: (i, 0)), pl.BlockSpec((FF_TC, D), lambda j, i: (j, 0)), tile, tile],
        out_specs=[tile, tile], out_shape=[_sds((s, D_FF), BF16)] * 2,
        compiler_params=_params("parallel", "parallel"),
    )(dx2b, wd, gate, up)


class _Weights:
    def __init__(self, **groups):
        self.groups = groups

    def begin(self, group, after):
        return ()

    def end(self, group, after):
        return self.groups[group]

    def early(self):
        return None


CHIP_COLS = N_IN // 4


def _proj_chunks(h, w, chips, name, prev=None, after=()):
    s, n = h.shape[0], chips.shape[0]
    tm = s // 2
    extra = ([prev] if prev is not None else []) + list(after)

    def body(c_ref, h_ref, w_ref, *rest):
        del c_ref
        o_ref = rest[len(extra)]

        def matmul(rows):
            return lax.dot_general(h_ref[rows, :], w_ref[...], NT, preferred_element_type=F32)

        def finish(rows, part):
            o_ref[rows, :] = part.astype(BF16)

        _row_pipeline(tm, matmul, finish)

    grid_spec = pltpu.PrefetchScalarGridSpec(
        num_scalar_prefetch=1, grid=(n, s // tm),
        in_specs=[pl.BlockSpec((tm, D), lambda t, i, c: (i, 0)), pl.BlockSpec((CHIP_COLS, D), lambda t, i, c: (c[t], 0))]
        + [HBM_SPEC] * len(extra),
        out_specs=pl.BlockSpec((tm, CHIP_COLS), lambda t, i, c: (i, c[t])))
    return _call(body, name=name, grid_spec=grid_spec, out_shape=_sds((s, N_IN), BF16),
                 input_output_aliases={3: 0} if prev is not None else {},
                 compiler_params=_params("arbitrary", "arbitrary"))(chips, h, w, *extra)


class _NoReduce:
    def start(self, group, grads):
        return ()

    def middle(self, group, after):
        return ()


def _local_step(x, tgt, g_mix, g_ffn, g_final, sinks, weights, reducer=None, after=()):
    reducer = reducer or _NoReduce()
    s = x.shape[0]
    tab = _rope_tables(s)
    big = dict(tm=2048, tn=512, tk=1024)
    h1 = _rms_fwd(x, g_mix, "rms1_fwd", after=after)
    after_in = weights.begin("in", (h1,))
    early = weights.early()
    if early is None:
        win_t, conv_w = weights.end("in", (*after_in, tab))
        proj = _matmul(h1, win_t, mode="nt", out_dtype=BF16, name="proj_fwd", tm=2048, tn=512, tk=1024)
    else:
        partial_win_t, own_chip, other_chips = early
        proj_own = _proj_chunks(h1, partial_win_t, own_chip, "proj_fwd_own", after=after_in)
        win_t, conv_w = weights.end("in", (*after_in, tab, proj_own))
        proj = _proj_chunks(h1, win_t, other_chips, "proj_fwd", prev=proj_own)
    attn = _swa_fwd(proj, tab, sinks, "attn_fwd", after=weights.begin("mix", (proj,)))
    wco, wao, wo = weights.end("mix", (attn,))
    conv_y = _conv_fwd(proj, conv_w, "conv_fwd")
    conv_out, attn_out, merged = _branches_merge_fwd(conv_y, attn, wco, wao, proj, "branch_out_fwd")
    x1 = _matmul(merged, wo, mode="nn", out_dtype=F32, name="wo_fwd", res=x, **big)
    h2 = _rms_fwd(x1, g_ffn, "rms2_fwd", after=weights.begin("ffn", (x1,)))
    wgu_t, wd = weights.end("ffn", (h2,))
    act, gate, up = _gate_up_fwd(h2, wgu_t, "gate_up_fwd")
    x2 = _matmul(act, wd, mode="nn", out_dtype=F32, name="down_fwd", res=x1, tm=1024, tn=512, tk=D_FF)
    dx2, dx2b, dg_final, lossvec = _loss_head(x2, g_final, tgt, "loss_head")
    dgate, dup = _down_bwd_x(dx2b, wd, gate, up, "down_bwd_x")
    g_wd = _matmul(act, dx2b, mode="tn", out_dtype=BF16, name="down_bwd_w", tm=1408, tn=1024, tk=2048)
    dh2 = _matmul([dgate, dup], wgu_t, mode="nn", out_dtype=BF16, name="gate_up_bwd_x", tm=1024, tn=1024, tk=1408)
    g_wgu_t = _matmul([dgate, dup], h2, mode="tn", out_dtype=BF16, name="gate_up_bwd_w", tm=1408, tn=1024, tk=2048)
    after_ffn = reducer.start("ffn", dict(wgu_t=g_wgu_t, wd=g_wd))
    dx1, dx1b, dg_ffn = _rms_bwd(dh2, x1, g_ffn, dx2, "rms2_bwd")
    dproj, dco, dao = _wo_merge_bwd(dx1b, wo, proj, conv_out, attn_out, "wo_bwd_x", after=after_ffn)
    after_ffn = reducer.middle("ffn", (dco,))
    dconv_y, dattn = _matmul_group((dco, dao), (wco, wao), mode="nt", tm=2048, tn=512, out_dtype=BF16, name="branch_out_bwd_x",
                                   after=after_ffn)
    g_wco, g_wao, g_wo = _matmul_group((conv_y, attn, merged), (dco, dao, dx1b), mode="tn", tm=512, tn=1024, out_dtype=BF16,
                                       name="mix_bwd_w")
    after_mix = reducer.start("mix", dict(wco=g_wco, wao=g_wao, wo=g_wo))
    dproj, dconv_w = _conv_bwd(dconv_y, proj, conv_w, dproj, "conv_bwd", after=after_mix)
    after_mix = reducer.middle("mix", (dconv_w,))
    dproj, dkr, dv, dsinks = _swa_bwd(dattn, proj, tab, sinks, dproj, "attn_bwd", after=after_mix)
    dproj = _kv_bwd(dkr, dv, tab, dproj, "kv_bwd")
    g_win_t = _matmul(dproj, h1, mode="tn", out_dtype=BF16, name="proj_bwd_w", tm=512, tn=1024, tk=2048)
    after_in = reducer.middle("in", reducer.start("in", dict(win_t=g_win_t)))
    dh1 = _matmul(dproj, win_t, mode="nn", out_dtype=BF16, name="proj_bwd_x", tm=1024, tn=1024, tk=1664, after=after_in)
    dx, _, dg_mix = _rms_bwd(dh1, x, g_mix, dx1, "rms1_bwd")
    grads = dict(win_t=g_win_t, wgu_t=g_wgu_t, wd=g_wd, wco=g_wco, wao=g_wao, wo=g_wo)
    small = dict(g_mix=dg_mix, g_ffn=dg_ffn, g_final=dg_final, conv_w=dconv_w, sinks=dsinks, lossvec=lossvec)
    return dx, grads, small


def _position():
    return lax.axis_index("x"), lax.axis_index("y"), lax.axis_index("c")


def _other_chips(x, y):
    return [(1 - x, y), (x, 1 - y), (1 - x, 1 - y)]


SEM_SPEC = pl.BlockSpec(memory_space=pltpu.SEMAPHORE)
EFFECT = pltpu.SideEffectType.DATAFLOW_SIDE_EFFECTING
TOKEN = jax.ShapeDtypeStruct((8, 128), F32)
TOKEN_SPEC = pl.BlockSpec(memory_space=pltpu.VMEM)


def _hbm(a):
    return pltpu.with_memory_space_constraint(a, pltpu.HBM)


def _place(ws, me_idx, dtypes, name, after=()):
    n = len(ws)

    def body(i_ref, *refs):
        for w_ref, o_ref, dtype in zip(refs[:n], refs[n + len(after):], dtypes):
            o_ref[...] = w_ref[...].astype(dtype)

    grid_spec = pltpu.PrefetchScalarGridSpec(
        num_scalar_prefetch=1, grid=(1,),
        in_specs=[pl.BlockSpec(w.shape, lambda i, me: (0, 0)) for w in ws] + [HBM_SPEC] * len(after),
        out_specs=[pl.BlockSpec(w.shape, lambda i, me: (me[0], 0)) for w in ws])
    return _call(body, name=name, grid_spec=grid_spec,
                 out_shape=[_sds((N_DEV * w.shape[0], w.shape[1]), dtype) for w, dtype in zip(ws, dtypes)],
                 compiler_params=_params("arbitrary"))(me_idx, *ws, *after)


def _own_rows(ref, r, px, py, pc):
    return ref.at[pl.ds((4 * px + 2 * py + pc) * r, r), :]


def _gather_phase(bufs, waits, plans, after, name):
    n = len(bufs)
    rows = [b.shape[0] // N_DEV for b in bufs]
    nw, npl = len(waits), len(plans)

    def body(*refs):
        ins = refs[:n]
        wait_sems = refs[n:n + 2 * nw]
        out0 = n + 2 * nw + len(after)
        new_sems = refs[out0:out0 + 2 * npl]
        token = refs[-1]
        x, y, c = _position()
        for w, (_, _, sent, received) in enumerate(waits):
            for a in range(n):
                for count, wait in ((sent, "wait_send"), (received, "wait_recv")):
                    span = _whole(ins[a], count * rows[a])
                    getattr(pltpu.make_async_remote_copy(
                        src_ref=span, dst_ref=span, send_sem=wait_sems[2 * w].at[a], recv_sem=wait_sems[2 * w + 1].at[a],
                        device_id=(x, y, c), device_id_type=MESH), wait)()
        for k, plan in enumerate(plans):
            for a in range(n):
                for block, target in plan(x, y, c):
                    span = _own_rows(ins[a], rows[a], *block)
                    pltpu.make_async_remote_copy(src_ref=span, dst_ref=span, send_sem=new_sems[2 * k].at[a],
                                                 recv_sem=new_sems[2 * k + 1].at[a], device_id=target, device_id_type=MESH).start()
        token[...] = jnp.zeros_like(token)

    sem_ops = [s for send, recv, _, _ in waits for s in (send, recv)]
    outs = _call(
        body, name=name, in_specs=[HBM_SPEC] * n + [SEM_SPEC] * (2 * nw) + [HBM_SPEC] * len(after),
        out_specs=[SEM_SPEC] * (2 * npl) + [HBM_SPEC] * n + [TOKEN_SPEC],
        out_shape=[pltpu.SemaphoreType.DMA((n,))] * (2 * npl) + [pltpu.HBM(b.shape, b.dtype) for b in bufs] + [TOKEN],
        input_output_aliases={i: 2 * npl + i for i in range(n)},
        compiler_params=pltpu.CompilerParams(has_side_effects=EFFECT),
    )(*[_hbm(b) for b in bufs], *sem_ops, *after)
    pairs = [(outs[2 * k], outs[2 * k + 1]) for k in range(npl)]
    return pairs, list(outs[2 * npl:2 * npl + n]), outs[-1]


def _own_to_near(x, y, c):
    return [((x, y, c), (x, y, 1 - c)), ((x, y, c), (1 - x, y, c)), ((x, y, c), (x, 1 - y, c))]


def _near_to_sibling(x, y, c):
    return [((1 - x, y, c), (x, y, 1 - c)), ((x, 1 - y, c), (x, y, 1 - c))]


def _relay_diagonal(x, y, c):
    north = c
    source = (x * north + (1 - x) * (1 - north), (1 - y) * north + y * (1 - north), c)
    target = ((1 - x) * north + x * (1 - north), y * north + (1 - y) * (1 - north), c)
    return [(source, target)]


def _diagonal_to_sibling(x, y, c):
    return [((1 - x, 1 - y, c), (x, y, 1 - c))]


def _gather_start(bufs, groups, name, after=()):
    n = len(bufs)
    rows = [b.shape[0] // N_DEV for b in bufs]
    ng = len(groups)

    def body(*refs):
        ins = refs[:n]
        sems = refs[n + len(after):n + len(after) + 2 * ng]
        token = refs[-1]
        x, y, c = _position()
        targets = [(x, y, 1 - c)] + [(*chip, c) for chip in _other_chips(x, y)]
        for g, members in enumerate(groups):
            for slot, a in enumerate(members):
                own = _own_rows(ins[a], rows[a], x, y, c)
                for to in targets:
                    pltpu.make_async_remote_copy(src_ref=own, dst_ref=own, send_sem=sems[2 * g].at[slot],
                                                 recv_sem=sems[2 * g + 1].at[slot], device_id=to, device_id_type=MESH).start()
        token[...] = jnp.zeros_like(token)

    sem_shapes = []
    for members in groups:
        sem_shapes += [pltpu.SemaphoreType.DMA((len(members),))] * 2
    outs = _call(
        body, name=name, in_specs=[HBM_SPEC] * (n + len(after)),
        out_specs=[SEM_SPEC] * (2 * ng) + [HBM_SPEC] * n + [TOKEN_SPEC],
        out_shape=sem_shapes + [pltpu.HBM(b.shape, b.dtype) for b in bufs] + [TOKEN],
        input_output_aliases={i: 2 * ng + i for i in range(n)},
        compiler_params=pltpu.CompilerParams(has_side_effects=EFFECT),
    )(*[_hbm(b) for b in bufs], *after)
    sem_pairs = [(outs[2 * g], outs[2 * g + 1]) for g in range(ng)]
    return sem_pairs, list(outs[2 * ng:2 * ng + n]), outs[-1]


def _gather_forward(send_sems, recv_sems, bufs, after, name):
    n = len(bufs)
    rows = [b.shape[0] // N_DEV for b in bufs]

    def body(*refs):
        ins = refs[:n]
        send1, recv1 = refs[n], refs[n + 1]
        out0 = n + 2 + len(after)
        send2, recv2 = refs[out0], refs[out0 + 1]
        token = refs[-1]
        x, y, c = _position()
        for a in range(n):
            step1 = pltpu.make_async_remote_copy(
                src_ref=_whole(ins[a], 4 * rows[a]), dst_ref=_whole(ins[a], 4 * rows[a]), send_sem=send1.at[a],
                recv_sem=recv1.at[a], device_id=(x, y, c), device_id_type=MESH)
            step1.wait_send()
            step1.wait_recv()
        for a in range(n):
            for chip in _other_chips(x, y):
                blk = _own_rows(ins[a], rows[a], *chip, c)
                pltpu.make_async_remote_copy(src_ref=blk, dst_ref=blk, send_sem=send2.at[a], recv_sem=recv2.at[a],
                                             device_id=(x, y, 1 - c), device_id_type=MESH).start()
        token[...] = jnp.zeros_like(token)

    outs = _call(
        body, name=name, in_specs=[HBM_SPEC] * n + [SEM_SPEC, SEM_SPEC] + [HBM_SPEC] * len(after),
        out_specs=[SEM_SPEC, SEM_SPEC] + [HBM_SPEC] * n + [TOKEN_SPEC],
        out_shape=[pltpu.SemaphoreType.DMA((n,)), pltpu.SemaphoreType.DMA((n,))]
        + [pltpu.HBM(b.shape, b.dtype) for b in bufs] + [TOKEN],
        input_output_aliases={i: 2 + i for i in range(n)},
        compiler_params=pltpu.CompilerParams(has_side_effects=EFFECT),
    )(*bufs, send_sems, recv_sems, *after)
    return outs[0], outs[1], list(outs[2:2 + n]), outs[-1]


def _gather_done(send_sems, recv_sems, bufs, after, name):
    n = len(bufs)
    rows = [b.shape[0] // N_DEV for b in bufs]

    def body(*refs):
        ins = refs[:n]
        send2, recv2 = refs[n], refs[n + 1]
        x, y, c = _position()
        for a in range(n):
            step2 = pltpu.make_async_remote_copy(
                src_ref=_whole(ins[a], 3 * rows[a]), dst_ref=_whole(ins[a], 3 * rows[a]), send_sem=send2.at[a],
                recv_sem=recv2.at[a], device_id=(x, y, c), device_id_type=MESH)
            step2.wait_send()
            step2.wait_recv()

    outs = _call(
        body, name=name, in_specs=[HBM_SPEC] * n + [SEM_SPEC, SEM_SPEC] + [HBM_SPEC] * len(after),
        out_specs=[HBM_SPEC] * n, out_shape=[pltpu.HBM(b.shape, b.dtype) for b in bufs],
        input_output_aliases={i: i for i in range(n)},
        compiler_params=pltpu.CompilerParams(has_side_effects=EFFECT),
    )(*bufs, send_sems, recv_sems, *after)
    return list(outs)


def _whole(ref, nrows):
    return ref.at[pl.ds(0, nrows), :]


def _to_sibling(x, y, c):
    return [(2 * q + (1 - c), q, (x, y, 1 - c)) for q in range(4)]


def _to_chips(x, y, c):
    return [(2 * px + py, j, (px, py, c)) for j, (px, py) in enumerate(_other_chips(x, y))]


def _exchange_start(exchanges, name):
    members = [(e, a, src, src.shape[0] // slots, plan)
               for e, (srcs, slots, plan) in enumerate(exchanges) for a, src in enumerate(srcs)]
    n, n_sems = len(members), 2 * len(exchanges)
    lands = [lax.empty((len(plan(0, 0, 0)) * r, src.shape[1]), src.dtype) for _, _, src, r, plan in members]

    def body(*refs):
        ins, land_refs, sems = refs[:n], refs[n:2 * n], refs[2 * n:2 * n + n_sems]
        token = refs[-1]
        for i, (e, a, _, r, plan) in enumerate(members):
            for src_slot, dst_slot, target in plan(*_position()):
                pltpu.make_async_remote_copy(
                    src_ref=ins[i].at[pl.ds(src_slot * r, r), :], dst_ref=land_refs[i].at[pl.ds(dst_slot * r, r), :],
                    send_sem=sems[2 * e].at[a], recv_sem=sems[2 * e + 1].at[a], device_id=target, device_id_type=MESH).start()
        token[...] = jnp.zeros_like(token)

    sem_shapes = [pltpu.SemaphoreType.DMA((len(srcs),)) for srcs, _, _ in exchanges for _ in range(2)]
    outs = _call(
        body, name=name, in_specs=[HBM_SPEC] * (2 * n),
        out_specs=[SEM_SPEC] * n_sems + [HBM_SPEC] * (2 * n) + [TOKEN_SPEC],
        out_shape=sem_shapes + [pltpu.HBM(m[2].shape, m[2].dtype) for m in members]
        + [pltpu.HBM(l.shape, l.dtype) for l in lands] + [TOKEN],
        input_output_aliases={i: n_sems + i for i in range(2 * n)},
        compiler_params=pltpu.CompilerParams(has_side_effects=EFFECT),
    )(*[_hbm(m[2]) for m in members], *[_hbm(l) for l in lands])
    started, at = [], 0
    for e, (srcs, _, _) in enumerate(exchanges):
        k = len(srcs)
        started.append((outs[2 * e], outs[2 * e + 1], list(outs[n_sems + at:n_sems + at + k]),
                        list(outs[n_sems + n + at:n_sems + n + at + k])))
        at += k
    return started, outs[-1]


def _exchange_wait(send_sems, recv_sems, srcs, lands, after, name):
    n = len(srcs)

    def body(*refs):
        ins, land_refs = refs[:n], refs[n:2 * n]
        send_sems_ref, recv_sems_ref = refs[2 * n], refs[2 * n + 1]
        for a in range(n):
            span = _whole(land_refs[a], lands[a].shape[0])
            cp = pltpu.make_async_remote_copy(
                src_ref=span, dst_ref=span, send_sem=send_sems_ref.at[a],
                recv_sem=recv_sems_ref.at[a], device_id=_position(), device_id_type=MESH)
            cp.wait_send()
            cp.wait_recv()

    outs = _call(
        body, name=name, in_specs=[HBM_SPEC] * (2 * n) + [SEM_SPEC, SEM_SPEC] + [HBM_SPEC] * len(after),
        out_specs=[HBM_SPEC] * (2 * n),
        out_shape=[pltpu.HBM(a.shape, a.dtype) for a in srcs] + [pltpu.HBM(l.shape, l.dtype) for l in lands],
        input_output_aliases={i: i for i in range(2 * n)},
        compiler_params=pltpu.CompilerParams(has_side_effects=EFFECT),
    )(*srcs, *lands, send_sems, recv_sems, *after)
    return list(outs[:n]), list(outs[n:])


def _chip_partial(grads, recvs, idx, name):
    n = len(grads)
    rows = [recv.shape[0] // 4 for recv in recvs]

    def body(i_ref, *refs):
        del i_ref
        for g_ref, s_ref, o_ref in zip(refs[:n], refs[n:2 * n], refs[2 * n:]):
            o_ref[...] = (g_ref[...].astype(F32) + s_ref[...].astype(F32)).astype(BF16)

    grid_spec = pltpu.PrefetchScalarGridSpec(
        num_scalar_prefetch=1, grid=(3,),
        in_specs=[pl.BlockSpec((r, D), lambda t, i_ref: (2 * i_ref[1 + t] + i_ref[0], 0)) for r in rows]
        + [pl.BlockSpec((r, D), lambda t, i_ref: (i_ref[1 + t], 0)) for r in rows],
        out_specs=[pl.BlockSpec((r, D), lambda t, i_ref: (i_ref[1 + t], 0)) for r in rows])
    return _call(body, name=name, grid_spec=grid_spec, out_shape=[_sds((4 * r, D), BF16) for r in rows],
                 compiler_params=_params("arbitrary"))(idx, *grads, *recvs)


def _adamw_math(w, g, m, v):
    m2 = B1 * m + (1.0 - B1) * g
    v2 = B2 * v + (1.0 - B2) * jnp.square(g)
    m_hat = m2 / (1.0 - B1 ** STEP)
    v_hat = v2 / (1.0 - B2 ** STEP)
    return -LR * (m_hat / (jnp.sqrt(v_hat) + EPS_ADAM) + WD * w), m2, v2


def _reduce_adamw(ws, grads, from_sibling, from_chips, idx, ms, vs, name):
    n = len(ws)
    nb = 2
    tiles = [w.shape[0] // nb for w in ws]
    for w, g, s, c in zip(ws, grads, from_sibling, from_chips):
        r = w.shape[0]
        assert g.shape == (N_DEV * r, D) and s.shape == (4 * r, D) and c.shape == (3 * r, D)

    def body(i_ref, *refs):
        del i_ref
        ins, outs = refs[:8 * n], refs[8 * n:]
        for a in range(n):
            w_ref, p_ref, s_ref, r0_ref, r1_ref, r2_ref, m_ref, v_ref = ins[8 * a:8 * a + 8]
            g_ref, d_ref, nm_ref, nv_ref = outs[4 * a:4 * a + 4]
            g = p_ref[...].astype(F32) + s_ref[...].astype(F32)
            g = ((g + r0_ref[...].astype(F32)) + r1_ref[...].astype(F32)) + r2_ref[...].astype(F32)
            g_ref[...] = g
            d_ref[...], nm_ref[...], nv_ref[...] = _adamw_math(w_ref[...], g, m_ref[...], v_ref[...])

    in_specs, out_specs, operands, out_shape = [], [], [], []
    for a, tr in enumerate(tiles):
        own = pl.BlockSpec((tr, D), lambda i, i_ref: (i, 0))
        in_specs += [own, pl.BlockSpec((tr, D), lambda i, i_ref: (i_ref[0] * nb + i, 0)),
                     pl.BlockSpec((tr, D), lambda i, i_ref: (i_ref[1] * nb + i, 0))]
        in_specs += [pl.BlockSpec((tr, D), lambda i, i_ref, j=j: (j * nb + i, 0)) for j in range(3)] + [own, own]
        operands += [ws[a], grads[a], from_sibling[a], from_chips[a], from_chips[a], from_chips[a], ms[a], vs[a]]
        out_specs += [own] * 4
        out_shape += [_sds(ws[a].shape, F32)] * 4
    grid_spec = pltpu.PrefetchScalarGridSpec(num_scalar_prefetch=1, grid=(nb,), in_specs=in_specs, out_specs=out_specs)
    outs = _call(body, name=name, grid_spec=grid_spec, out_shape=out_shape, compiler_params=_params("parallel"))(idx, *operands)
    return [tuple(outs[4 * a:4 * a + 4]) for a in range(n)]


SMALL_ROWS = 8


def _small_all_reduce(pack, name, after=()):
    def body(p_ref, *rest):
        tot_ref, loss_ref, gath, send_sems, recv_sems = rest[len(after):]
        x, y, c = _position()
        me_id = 4 * x + 2 * y + c
        gath[me_id] = p_ref[...]
        copies = []
        for k in range(1, N_DEV):
            peer = tuple(1 - v if (k >> b) & 1 else v for v, b in ((x, 2), (y, 1), (c, 0)))
            cp = pltpu.make_async_remote_copy(src_ref=p_ref, dst_ref=gath.at[me_id], send_sem=send_sems.at[k - 1],
                                              recv_sem=recv_sems.at[k - 1], device_id=peer, device_id_type=MESH)
            cp.start()
            copies.append(cp)
        for cp in copies:
            cp.wait_recv()
        for cp in copies:
            cp.wait_send()
        tot = gath[0]
        for d in range(1, N_DEV):
            tot = tot + gath[d]
        tot_ref[...] = tot
        loss_ref[...] = jnp.full((1, 128), (0.5 / D) * jnp.sum(tot[SMALL_ROWS - 1:SMALL_ROWS, :]), F32)

    vm = pl.BlockSpec(memory_space=pltpu.VMEM)
    return _call(
        body, name=name, in_specs=[vm] + [HBM_SPEC] * len(after), out_specs=[vm, vm],
        out_shape=[_sds((SMALL_ROWS, D), F32), _sds((1, 128), F32)],
        scratch_shapes=[pltpu.VMEM((N_DEV, SMALL_ROWS, D), F32), pltpu.SemaphoreType.DMA((N_DEV - 1,)),
                        pltpu.SemaphoreType.DMA((N_DEV - 1,))],
    )(pack, *after)


def _adamw_small(ws, gs, ms, vs, name):
    n = len(ws)

    def body(*refs):
        for a in range(n):
            w_ref, g_ref, m_ref, v_ref = (refs[k * n + a] for k in range(4))
            d_ref, nm_ref, nv_ref = (refs[(4 + k) * n + a] for k in range(3))
            d_ref[...], nm_ref[...], nv_ref[...] = _adamw_math(w_ref[...], g_ref[...], m_ref[...], v_ref[...])

    vm = pl.BlockSpec(memory_space=pltpu.VMEM)
    outs = _call(body, name=name, in_specs=[vm] * (4 * n), out_specs=[vm] * (3 * n),
                 out_shape=[_sds(w.shape, F32) for w in ws] * 3)(*ws, *gs, *ms, *vs)
    return [(outs[a], outs[n + a], outs[2 * n + a]) for a in range(n)]


def kernel(x, g_mix, w_in, conv_w, attn_sinks, w_conv_out, w_attn_out, w_o, g_ffn, w_gate_up, w_down, g_final, loss_target, m_g_mix, m_w_in, m_conv_w, m_attn_sinks, m_w_conv_out, m_w_attn_out, m_w_o, m_g_ffn, m_w_gate_up, m_w_down, m_g_final, v_g_mix, v_w_in, v_conv_w, v_attn_sinks, v_w_conv_out, v_w_attn_out, v_w_o, v_g_ffn, v_w_gate_up, v_w_down, v_g_final):
    cx, cy, cc = _position()
    chip = 2 * cx + cy
    partial_idx = jnp.stack([cc, 2 * (1 - cx) + cy, 2 * cx + (1 - cy), 2 * (1 - cx) + (1 - cy)]).astype(jnp.int32)
    own_idx = jnp.stack([2 * chip + cc, chip]).astype(jnp.int32)
    me = 4 * cx + 2 * cy + cc

    me_idx = jnp.reshape(me, (1,)).astype(jnp.int32)
    first = _place([jnp.transpose(w_in[0]), jnp.pad(conv_w[0], ((0, 5), (0, 0)))], me_idx, (BF16, F32), "place_in")
    (to_near,), first, token_in = _gather_phase(first, [], [_own_to_near], (), "gather_in_start")
    gather_tokens = (token_in,)

    class Gathered:
        def __init__(self):
            self.state = {}

        def begin(self, group, after):
            if group == "in":
                (near, relay), bufs, token = _gather_phase(
                    first, [(*to_near, 3, 3)], [_near_to_sibling, _relay_diagonal], after, "gather_in_relay")
                later = [_place([w], me_idx, (BF16,), "place_" + k, after=(token,))[0] for k, w in (
                    ("w_conv_out", w_conv_out[0]), ("w_attn_out", w_attn_out[0]), ("w_o", w_o[0]),
                    ("w_gate_up", jnp.transpose(w_gate_up[0])), ("w_down", w_down[0]))]
                (sems_mix, sems_ffn), later, token_later = _gather_start(later, [[0, 1, 2], [3, 4]], "gather_start_later")
                self.state.update({"in": (near, relay, bufs), "mix": (sems_mix, later[:3]), "ffn": (sems_ffn, later[3:])})
                return (token_later,)
            (send_sems, recv_sems), group_bufs = self.state[group]
            send2, recv2, group_bufs, token = _gather_forward(send_sems, recv_sems, group_bufs, after, "gather_forward_" + group)
            self.state[group] = ((send2, recv2), group_bufs)
            return (token,)

        def early(self):
            return self.state["in"][2][0], own_idx[1:2], partial_idx[1:4]

        def end(self, group, after):
            if group == "in":
                near, relay, bufs = self.state[group]
                (last,), bufs, token = _gather_phase(bufs, [(*relay, 1, 1)], [_diagonal_to_sibling], after, "gather_in_last")
                _, full, _ = _gather_phase(bufs, [(*near, 2, 2), (*last, 1, 1)], [], (token,), "gather_in_done")
                return full[0], jnp.transpose(full[1].reshape(N_DEV, 8, 128)[:, :3, :], (1, 0, 2)).reshape(3, D)
            (send2, recv2), group_bufs = self.state[group]
            return _gather_done(send2, recv2, group_bufs, after, "gather_done_" + group)

    in_flight, own_pieces = {}, {}

    transposed = ("w_in", "w_gate_up")

    def as2d(k, a):
        if k in transposed:
            return jnp.transpose(a[0])
        return a[None] if a.ndim == 1 else (a[0] if a.ndim == 3 else a)

    w_all = {"g_mix": g_mix, "w_in": w_in, "conv_w": conv_w, "attn_sinks": attn_sinks, "w_conv_out": w_conv_out,
             "w_attn_out": w_attn_out, "w_o": w_o, "g_ffn": g_ffn, "w_gate_up": w_gate_up, "w_down": w_down, "g_final": g_final}
    m_all = {"g_mix": m_g_mix, "w_in": m_w_in, "conv_w": m_conv_w, "attn_sinks": m_attn_sinks, "w_conv_out": m_w_conv_out,
             "w_attn_out": m_w_attn_out, "w_o": m_w_o, "g_ffn": m_g_ffn, "w_gate_up": m_w_gate_up, "w_down": m_w_down,
             "g_final": m_g_final}
    v_all = {"g_mix": v_g_mix, "w_in": v_w_in, "conv_w": v_conv_w, "attn_sinks": v_attn_sinks, "w_conv_out": v_w_conv_out,
             "w_attn_out": v_w_attn_out, "w_o": v_w_o, "g_ffn": v_g_ffn, "w_gate_up": v_w_gate_up, "w_down": v_w_down,
             "g_final": v_g_final}
    results = {}

    def record(k, *vals):
        results[k] = [(jnp.transpose(val) if k in transposed else val).reshape(w_all[k].shape) for val in vals]

    def update(group, names, grads, from_sibling, from_chips):
        outs = _reduce_adamw([as2d(k, w_all[k]) for k in names], grads, from_sibling, from_chips, own_idx,
                             [as2d(k, m_all[k]) for k in names], [as2d(k, v_all[k]) for k in names], "adamw_" + group)
        for k, vals in zip(names, outs):
            record(k, *vals)
        return tuple(vals[2] for vals in outs)

    def update_small(grads):
        keys = list(grads)
        outs = _adamw_small([as2d(k, w_all[k]) for k in keys], [grads[k] for k in keys], [as2d(k, m_all[k]) for k in keys],
                            [as2d(k, v_all[k]) for k in keys], "adamw_small")
        for k, (d, nm, nv) in zip(keys, outs):
            record(k, grads[k], d, nm, nv)
        return tuple(nm for _, nm, _ in outs)

    kernel_name = {"win_t": "w_in", "wgu_t": "w_gate_up", "wd": "w_down", "wco": "w_conv_out", "wao": "w_attn_out", "wo": "w_o"}

    def finish(group, after):
        keys, send_sems, recv_sems, parts, from_chips = in_flight[group]
        _, from_chips = _exchange_wait(send_sems, recv_sems, parts, from_chips, after, "rs_chips_wait_" + group)
        grads, from_sibling = own_pieces[group]
        return update(group, [kernel_name[k] for k in keys], grads, from_sibling, from_chips)

    class Reducer:
        def __init__(self):
            self.waiting = None

        def start(self, group, gdict):
            keys, glist = list(gdict), list(gdict.values())
            exchanges = [(glist, N_DEV, _to_sibling)]
            if self.waiting:
                exchanges.append((self.waiting[2], 4, _to_chips))
            started, token = _exchange_start(exchanges, "rs_sibling_start_" + group)
            in_flight[group] = (keys, *started[0])
            if self.waiting:
                in_flight[self.waiting[0]] = (self.waiting[1], *started[1])
            return (token,)

        def middle(self, group, after):
            keys, send_sems, recv_sems, glist, lands = in_flight[group]
            if group == "in":
                after = finish("ffn", after)
            glist, lands = _exchange_wait(send_sems, recv_sems, glist, lands, after, "rs_sibling_wait_" + group)
            parts = _chip_partial(glist, lands, partial_idx, "chip_partial_" + group)
            own_pieces[group] = (glist, lands)
            if group != "in":
                self.waiting = (group, keys, parts)
                return tuple(parts)
            self.waiting = None
            (started,), token = _exchange_start([(parts, 4, _to_chips)], "rs_chips_start_" + group)
            in_flight[group] = (keys, *started)
            return (token,)

    dx, _, small = _local_step(x[0], loss_target[0], g_mix, g_ffn, g_final[None], attn_sinks, Gathered(),
                               reducer=Reducer(), after=gather_tokens)
    after = finish("mix", (dx,))

    sinks_row = jnp.pad(small["sinks"], ((0, 0), (0, D - 128)))
    pack = jnp.concatenate([small["g_mix"], small["g_ffn"], small["g_final"], small["conv_w"], sinks_row, small["lossvec"]], axis=0)
    tot, loss_row = _small_all_reduce(pack, "small_all_reduce", after=after)
    loss = loss_row[0, 0]
    g_small = {
        "g_mix": tot[0:1], "g_ffn": tot[1:2], "g_final": tot[2:3],
        "conv_w": lax.dynamic_slice(tot, (3, me * 128), (3, 128)), "attn_sinks": tot[6:7, :N_HEADS],
    }
    finish("in", update_small(g_small))

    order = ["g_mix", "w_in", "conv_w", "attn_sinks", "w_conv_out", "w_attn_out", "w_o", "g_ffn", "w_gate_up", "w_down", "g_final"]
    return (loss, dx[None], *[results[k][i] for i in range(4) for k in order])
```
